```python
import math
import jax, jax.numpy as jnp
from jax import lax
import numpy as np

D_MODEL = 1024
BATCH = 8
SEQ = 2048
DEPTH = 2

N_MEM = 256
HEAD_DIM = 128
DN_HEADS = D_MODEL // HEAD_DIM
FOX_HEADS = D_MODEL // HEAD_DIM
MEM_HEADS = 4
MIX_WIDTH = D_MODEL
MEM_WIDTH = MEM_HEADS * HEAD_DIM
D_FF = 4 * D_MODEL
CONV_WIDTH = 4
CHUNK = 64
Q_BLOCK = 128
EPS = 1e-6
N_DN = (DEPTH + 1) // 2
N_FOX = DEPTH // 2
DN_IN = 4 * MIX_WIDTH + 2 * DN_HEADS + MEM_WIDTH
FOX_IN = 4 * MIX_WIDTH + FOX_HEADS + MEM_WIDTH
OUT_IN = MIX_WIDTH + MEM_WIDTH

kernel_name = "hybrid_deltanet_fox_memory_decoder"


def rms_norm(x, w):
    xf = x.astype(jnp.float32)
    y = xf * lax.rsqrt(jnp.mean(xf * xf, axis=-1, keepdims=True) + EPS)
    return (y * w.astype(jnp.float32)).astype(x.dtype)


def l2_norm(x):
    xf = x.astype(jnp.float32)
    return xf * lax.rsqrt(jnp.sum(xf * xf, axis=-1, keepdims=True) + EPS)


def split_heads(t, heads):
    return t.reshape(t.shape[:-1] + (heads, HEAD_DIM))


def causal_depthwise_conv(x, w):
    c = x.shape[-1]
    return lax.conv_general_dilated(
        x, w[:, None, :].astype(x.dtype), window_strides=(1,),
        padding=[(CONV_WIDTH - 1, 0)], dimension_numbers=("NWC", "WIO", "NWC"),
        feature_group_count=c)


def chunked_gated_delta_rule(q, k, v, g, beta):
    bsz, seq, heads, _ = q.shape
    n = seq // CHUNK

    def chunks(t):
        return t.reshape(bsz, n, CHUNK, heads, -1).transpose(1, 0, 3, 2, 4)

    qc, kc, vc = chunks(q), chunks(k), chunks(v)
    gc = chunks(g[..., None])[..., 0]
    bc = chunks(beta[..., None])[..., 0]
    gcum = jnp.cumsum(gc, axis=-1)
    causal = jnp.tril(jnp.ones((CHUNK, CHUNK), dtype=bool))
    strict = jnp.tril(jnp.ones((CHUNK, CHUNK), dtype=bool), -1)
    diff = gcum[..., :, None] - gcum[..., None, :]
    decay = jnp.where(causal, jnp.exp(jnp.where(causal, diff, 0.0)), 0.0)
    kb = kc * bc[..., None]
    a_mat = jnp.where(strict, jnp.einsum("nbhik,nbhjk->nbhij", kb, kc) * decay, 0.0)
    eye = jnp.eye(CHUNK, dtype=jnp.float32)
    t_mat = lax.linalg.triangular_solve(
        eye + a_mat, jnp.broadcast_to(eye, a_mat.shape), left_side=True, lower=True)
    u = jnp.einsum("nbhij,nbhjv->nbhiv", t_mat, vc * bc[..., None])
    w = jnp.einsum("nbhij,nbhjk->nbhik", t_mat, kb * jnp.exp(gcum)[..., None])
    qk = jnp.where(causal, jnp.einsum("nbhik,nbhjk->nbhij", qc, kc) * decay, 0.0)

    def step(state, inp):
        q_i, k_i, u_i, w_i, qk_i, g_i = inp
        v_new = u_i - jnp.einsum("bhck,bhkv->bhcv", w_i, state)
        out = (jnp.einsum("bhck,bhkv->bhcv", q_i * jnp.exp(g_i)[..., None], state)
               + jnp.einsum("bhij,bhjv->bhiv", qk_i, v_new))
        g_last = g_i[..., -1:]
        k_dec = k_i * jnp.exp(g_last - g_i)[..., None]
        state = state * jnp.exp(g_last)[..., None] + jnp.einsum("bhck,bhcv->bhkv", k_dec, v_new)
        return state, out

    state0 = jnp.zeros((bsz, heads, q.shape[-1], v.shape[-1]), jnp.float32)
    _, o = lax.scan(step, state0, (qc, kc, u, w, qk, gcum))
    return o.transpose(1, 0, 3, 2, 4).reshape(bsz, seq, heads, -1)


def gated_deltanet(h, w_in, conv_w, a_log, dt_bias, o_norm_w):
    bsz, seq, _ = h.shape
    proj = h @ w_in
    qkv = proj[..., : 3 * MIX_WIDTH]
    z = proj[..., 3 * MIX_WIDTH: 4 * MIX_WIDTH]
    a = proj[..., 4 * MIX_WIDTH: 4 * MIX_WIDTH + DN_HEADS]
    b = proj[..., 4 * MIX_WIDTH + DN_HEADS: 4 * MIX_WIDTH + 2 * DN_HEADS]
    q_mem = proj[..., 4 * MIX_WIDTH + 2 * DN_HEADS:]
    qkv = jax.nn.silu(causal_depthwise_conv(qkv, conv_w))
    q = l2_norm(split_heads(qkv[..., :MIX_WIDTH], DN_HEADS)) * (HEAD_DIM ** -0.5)
    k = l2_norm(split_heads(qkv[..., MIX_WIDTH: 2 * MIX_WIDTH], DN_HEADS))
    v = split_heads(qkv[..., 2 * MIX_WIDTH:], DN_HEADS).astype(jnp.float32)
    beta = jax.nn.sigmoid(b.astype(jnp.float32))
    g = -jnp.exp(a_log.astype(jnp.float32)) * jax.nn.softplus(
        a.astype(jnp.float32) + dt_bias.astype(jnp.float32))
    o = chunked_gated_delta_rule(q, k, v, g, beta)
    o = rms_norm(o, o_norm_w) * jax.nn.silu(split_heads(z, DN_HEADS).astype(jnp.float32))
    return o.reshape(bsz, seq, MIX_WIDTH).astype(h.dtype), q_mem


def forgetting_attention(h, w_in, f_bias, q_norm_w, k_norm_w):
    bsz, seq, _ = h.shape
    proj = h @ w_in
    q = split_heads(proj[..., :MIX_WIDTH], FOX_HEADS)
    k = split_heads(proj[..., MIX_WIDTH: 2 * MIX_WIDTH], FOX_HEADS)
    v = split_heads(proj[..., 2 * MIX_WIDTH: 3 * MIX_WIDTH], FOX_HEADS)
    gate = proj[..., 3 * MIX_WIDTH: 4 * MIX_WIDTH]
    f_logit = proj[..., 4 * MIX_WIDTH: 4 * MIX_WIDTH + FOX_HEADS]
    q_mem = proj[..., 4 * MIX_WIDTH + FOX_HEADS:]
    q = rms_norm(q, q_norm_w).astype(jnp.float32) * (HEAD_DIM ** -0.5)
    k = rms_norm(k, k_norm_w).astype(jnp.float32)
    log_f = jax.nn.log_sigmoid(f_logit.astype(jnp.float32) + f_bias.astype(jnp.float32))
    f_cum = jnp.cumsum(log_f, axis=1).transpose(0, 2, 1)
    nb = seq // Q_BLOCK
    q_blocks = q.reshape(bsz, nb, Q_BLOCK, FOX_HEADS, HEAD_DIM).transpose(1, 0, 2, 3, 4)
    f_blocks = f_cum.reshape(bsz, FOX_HEADS, nb, Q_BLOCK).transpose(2, 0, 1, 3)
    k_pos = jnp.arange(seq)

    def block(args):
        qb, fb, i = args
        s = jnp.einsum("bqhd,bkhd->bhqk", qb, k)
        bias = fb[..., :, None] - f_cum[:, :, None, :]
        q_pos = i * Q_BLOCK + jnp.arange(Q_BLOCK)
        mask = k_pos[None, :] <= q_pos[:, None]
        p = jax.nn.softmax(jnp.where(mask, s + bias, -jnp.inf), axis=-1)
        return jnp.einsum("bhqk,bkhd->bqhd", p.astype(v.dtype), v)

    o = lax.map(block, (q_blocks, f_blocks, jnp.arange(nb)))
    o = o.transpose(1, 0, 2, 3, 4).reshape(bsz, seq, MIX_WIDTH)
    o = o.astype(jnp.float32) * jax.nn.sigmoid(gate.astype(jnp.float32))
    return o.astype(h.dtype), q_mem


def memory_attention(q_mem, mem_k, mem_v, q_norm_w):
    bsz, seq, _ = q_mem.shape
    q = rms_norm(split_heads(q_mem, MEM_HEADS), q_norm_w).astype(jnp.float32) * (HEAD_DIM ** -0.5)
    p = jax.nn.softmax(jnp.einsum("bthd,bmhd->bhtm", q, mem_k), axis=-1)
    o = jnp.einsum("bhtm,bmhd->bthd", p.astype(mem_v.dtype), mem_v)
    return o.reshape(bsz, seq, MEM_WIDTH).astype(q_mem.dtype)


def _fwd_setup_inputs(seed: int = 0) -> dict:
    key = jax.random.key(seed)
    ks = jax.random.split(key, 20)
    f32 = jnp.float32

    def dense(k, shape, fan_in):
        return jax.random.normal(k, shape, f32) * fan_in ** -0.5

    def gain(k, shape):
        return 1.0 + 0.02 * jax.random.normal(k, shape, f32)

    dt = jnp.exp(jax.random.uniform(ks[9], (N_DN, DN_HEADS), f32, math.log(1e-3), math.log(1e-1)))
    return {
        "x": jax.random.normal(ks[0], (BATCH, SEQ, D_MODEL), f32),
        "mem": jax.random.normal(ks[1], (BATCH, N_MEM, D_MODEL), f32),
        "mem_norm_w": gain(ks[2], (D_MODEL,)),
        "w_mem_kv": dense(ks[3], (D_MODEL, 2 * MEM_WIDTH), D_MODEL),
        "mem_k_norm_w": gain(ks[4], (HEAD_DIM,)),
        "norm1_w": gain(ks[5], (DEPTH, D_MODEL)),
        "dn_w_in": dense(ks[6], (N_DN, D_MODEL, DN_IN), D_MODEL),
        "dn_conv_w": dense(ks[7], (N_DN, CONV_WIDTH, 3 * MIX_WIDTH), CONV_WIDTH),
        "dn_a_log": jnp.log(jax.random.uniform(ks[8], (N_DN, DN_HEADS), f32, 1.0, 16.0)),
        "dn_dt_bias": dt + jnp.log(-jnp.expm1(-dt)),
        "dn_o_norm_w": gain(ks[10], (N_DN, HEAD_DIM)),
        "fox_w_in": dense(ks[11], (N_FOX, D_MODEL, FOX_IN), D_MODEL),
        "fox_f_bias": jax.random.uniform(ks[12], (N_FOX, FOX_HEADS), f32, 1.0, 4.0),
        "fox_q_norm_w": gain(ks[13], (N_FOX, HEAD_DIM)),
        "fox_k_norm_w": gain(ks[14], (N_FOX, HEAD_DIM)),
        "memq_norm_w": gain(ks[15], (DEPTH, HEAD_DIM)),
        "w_out": dense(ks[16], (DEPTH, OUT_IN, D_MODEL), OUT_IN),
        "norm2_w": gain(ks[17], (DEPTH, D_MODEL)),
        "w_mlp1": dense(ks[18], (DEPTH, D_MODEL, D_FF), D_MODEL),
        "w_mlp2": dense(ks[19], (DEPTH, D_FF, D_MODEL), D_FF),
    }


def _fwd_reference(x, mem, mem_norm_w, w_mem_kv, mem_k_norm_w, norm1_w, dn_w_in, dn_conv_w,
              dn_a_log, dn_dt_bias, dn_o_norm_w, fox_w_in, fox_f_bias, fox_q_norm_w,
              fox_k_norm_w, memq_norm_w, w_out, norm2_w, w_mlp1, w_mlp2):
    mem_kv = rms_norm(mem, mem_norm_w) @ w_mem_kv
    mem_k = rms_norm(split_heads(mem_kv[..., :MEM_WIDTH], MEM_HEADS), mem_k_norm_w).astype(jnp.float32)
    mem_v = split_heads(mem_kv[..., MEM_WIDTH:], MEM_HEADS)
    for i in range(DEPTH):
        h = rms_norm(x, norm1_w[i])
        j = i // 2
        if i % 2 == 0:
            mix, q_mem = gated_deltanet(h, dn_w_in[j], dn_conv_w[j], dn_a_log[j],
                                        dn_dt_bias[j], dn_o_norm_w[j])
        else:
            mix, q_mem = forgetting_attention(h, fox_w_in[j], fox_f_bias[j],
                                              fox_q_norm_w[j], fox_k_norm_w[j])
        mem_out = memory_attention(q_mem, mem_k, mem_v, memq_norm_w[i])
        x = x + jnp.concatenate([mix, mem_out], axis=-1) @ w_out[i]
        h = rms_norm(x, norm2_w[i])
        x = x + jnp.square(jax.nn.relu(h @ w_mlp1[i])) @ w_mlp2[i]
    return x


import jax as _jax
import jax.numpy as _jnp

TWIN_FORMAT = 'train_step'
FWD_PARAMS = ['x', 'mem', 'mem_norm_w', 'w_mem_kv', 'mem_k_norm_w', 'norm1_w', 'dn_w_in', 'dn_conv_w', 'dn_a_log', 'dn_dt_bias', 'dn_o_norm_w', 'fox_w_in', 'fox_f_bias', 'fox_q_norm_w', 'fox_k_norm_w', 'memq_norm_w', 'w_out', 'norm2_w', 'w_mlp1', 'w_mlp2']
TWIN_WEIGHTS = ['mem_norm_w', 'w_mem_kv', 'mem_k_norm_w', 'norm1_w', 'dn_w_in', 'dn_conv_w', 'dn_a_log', 'dn_dt_bias', 'dn_o_norm_w', 'fox_w_in', 'fox_f_bias', 'fox_q_norm_w', 'fox_k_norm_w', 'memq_norm_w', 'w_out', 'norm2_w', 'w_mlp1', 'w_mlp2']
TWIN_DIFF_INPUT = 'x'
TWIN_INPUTS = ['x', 'mem', 'mem_norm_w', 'w_mem_kv', 'mem_k_norm_w', 'norm1_w', 'dn_w_in', 'dn_conv_w', 'dn_a_log', 'dn_dt_bias', 'dn_o_norm_w', 'fox_w_in', 'fox_f_bias', 'fox_q_norm_w', 'fox_k_norm_w', 'memq_norm_w', 'w_out', 'norm2_w', 'w_mlp1', 'w_mlp2', 'loss_target', 'm_mem_norm_w', 'm_w_mem_kv', 'm_mem_k_norm_w', 'm_norm1_w', 'm_dn_w_in', 'm_dn_conv_w', 'm_dn_a_log', 'm_dn_dt_bias', 'm_dn_o_norm_w', 'm_fox_w_in', 'm_fox_f_bias', 'm_fox_q_norm_w', 'm_fox_k_norm_w', 'm_memq_norm_w', 'm_w_out', 'm_norm2_w', 'm_w_mlp1', 'm_w_mlp2', 'v_mem_norm_w', 'v_w_mem_kv', 'v_mem_k_norm_w', 'v_norm1_w', 'v_dn_w_in', 'v_dn_conv_w', 'v_dn_a_log', 'v_dn_dt_bias', 'v_dn_o_norm_w', 'v_fox_w_in', 'v_fox_f_bias', 'v_fox_q_norm_w', 'v_fox_k_norm_w', 'v_memq_norm_w', 'v_w_out', 'v_norm2_w', 'v_w_mlp1', 'v_w_mlp2']
TWIN_OUTPUTS = ['loss', 'grad_x', 'grad_mem_norm_w', 'grad_w_mem_kv', 'grad_mem_k_norm_w', 'grad_norm1_w', 'grad_dn_w_in', 'grad_dn_conv_w', 'grad_dn_a_log', 'grad_dn_dt_bias', 'grad_dn_o_norm_w', 'grad_fox_w_in', 'grad_fox_f_bias', 'grad_fox_q_norm_w', 'grad_fox_k_norm_w', 'grad_memq_norm_w', 'grad_w_out', 'grad_norm2_w', 'grad_w_mlp1', 'grad_w_mlp2', 'delta_mem_norm_w', 'delta_w_mem_kv', 'delta_mem_k_norm_w', 'delta_norm1_w', 'delta_dn_w_in', 'delta_dn_conv_w', 'delta_dn_a_log', 'delta_dn_dt_bias', 'delta_dn_o_norm_w', 'delta_fox_w_in', 'delta_fox_f_bias', 'delta_fox_q_norm_w', 'delta_fox_k_norm_w', 'delta_memq_norm_w', 'delta_w_out', 'delta_norm2_w', 'delta_w_mlp1', 'delta_w_mlp2', 'new_m_mem_norm_w', 'new_m_w_mem_kv', 'new_m_mem_k_norm_w', 'new_m_norm1_w', 'new_m_dn_w_in', 'new_m_dn_conv_w', 'new_m_dn_a_log', 'new_m_dn_dt_bias', 'new_m_dn_o_norm_w', 'new_m_fox_w_in', 'new_m_fox_f_bias', 'new_m_fox_q_norm_w', 'new_m_fox_k_norm_w', 'new_m_memq_norm_w', 'new_m_w_out', 'new_m_norm2_w', 'new_m_w_mlp1', 'new_m_w_mlp2', 'new_v_mem_norm_w', 'new_v_w_mem_kv', 'new_v_mem_k_norm_w', 'new_v_norm1_w', 'new_v_dn_w_in', 'new_v_dn_conv_w', 'new_v_dn_a_log', 'new_v_dn_dt_bias', 'new_v_dn_o_norm_w', 'new_v_fox_w_in', 'new_v_fox_f_bias', 'new_v_fox_q_norm_w', 'new_v_fox_k_norm_w', 'new_v_memq_norm_w', 'new_v_w_out', 'new_v_norm2_w', 'new_v_w_mlp1', 'new_v_w_mlp2']
TWIN_LEAF_KINDS = {'loss': 'loss', 'grad_x': 'grad_x', 'grad_mem_norm_w': 'grad_w', 'grad_w_mem_kv': 'grad_w', 'grad_mem_k_norm_w': 'grad_w', 'grad_norm1_w': 'grad_w', 'grad_dn_w_in': 'grad_w', 'grad_dn_conv_w': 'grad_w', 'grad_dn_a_log': 'grad_w', 'grad_dn_dt_bias': 'grad_w', 'grad_dn_o_norm_w': 'grad_w', 'grad_fox_w_in': 'grad_w', 'grad_fox_f_bias': 'grad_w', 'grad_fox_q_norm_w': 'grad_w', 'grad_fox_k_norm_w': 'grad_w', 'grad_memq_norm_w': 'grad_w', 'grad_w_out': 'grad_w', 'grad_norm2_w': 'grad_w', 'grad_w_mlp1': 'grad_w', 'grad_w_mlp2': 'grad_w', 'delta_mem_norm_w': 'delta_w', 'delta_w_mem_kv': 'delta_w', 'delta_mem_k_norm_w': 'delta_w', 'delta_norm1_w': 'delta_w', 'delta_dn_w_in': 'delta_w', 'delta_dn_conv_w': 'delta_w', 'delta_dn_a_log': 'delta_w', 'delta_dn_dt_bias': 'delta_w', 'delta_dn_o_norm_w': 'delta_w', 'delta_fox_w_in': 'delta_w', 'delta_fox_f_bias': 'delta_w', 'delta_fox_q_norm_w': 'delta_w', 'delta_fox_k_norm_w': 'delta_w', 'delta_memq_norm_w': 'delta_w', 'delta_w_out': 'delta_w', 'delta_norm2_w': 'delta_w', 'delta_w_mlp1': 'delta_w', 'delta_w_mlp2': 'delta_w', 'new_m_mem_norm_w': 'new_m', 'new_m_w_mem_kv': 'new_m', 'new_m_mem_k_norm_w': 'new_m', 'new_m_norm1_w': 'new_m', 'new_m_dn_w_in': 'new_m', 'new_m_dn_conv_w': 'new_m', 'new_m_dn_a_log': 'new_m', 'new_m_dn_dt_bias': 'new_m', 'new_m_dn_o_norm_w': 'new_m', 'new_m_fox_w_in': 'new_m', 'new_m_fox_f_bias': 'new_m', 'new_m_fox_q_norm_w': 'new_m', 'new_m_fox_k_norm_w': 'new_m', 'new_m_memq_norm_w': 'new_m', 'new_m_w_out': 'new_m', 'new_m_norm2_w': 'new_m', 'new_m_w_mlp1': 'new_m', 'new_m_w_mlp2': 'new_m', 'new_v_mem_norm_w': 'new_v', 'new_v_w_mem_kv': 'new_v', 'new_v_mem_k_norm_w': 'new_v', 'new_v_norm1_w': 'new_v', 'new_v_dn_w_in': 'new_v', 'new_v_dn_conv_w': 'new_v', 'new_v_dn_a_log': 'new_v', 'new_v_dn_dt_bias': 'new_v', 'new_v_dn_o_norm_w': 'new_v', 'new_v_fox_w_in': 'new_v', 'new_v_fox_f_bias': 'new_v', 'new_v_fox_q_norm_w': 'new_v', 'new_v_fox_k_norm_w': 'new_v', 'new_v_memq_norm_w': 'new_v', 'new_v_w_out': 'new_v', 'new_v_norm2_w': 'new_v', 'new_v_w_mlp1': 'new_v', 'new_v_w_mlp2': 'new_v'}


def _forward(args):
    return _fwd_reference(*[args[k] for k in FWD_PARAMS])


def _output_shape():
    out = _jax.eval_shape(lambda: _forward(_fwd_setup_inputs(0)))
    return out.shape, out.dtype

N_MICROBATCH = 1
ADAM_LR = 0.001
ADAM_B1 = 0.9
ADAM_B2 = 0.999
ADAM_EPS = 1e-08
ADAM_WD = 0.01
ADAM_STEP = 10
PER_EXAMPLE_BATCH_AXIS = {'x': 0, 'mem': 0, 'loss_target': 0}
SHARED_INPUTS = []
_WEIGHT_DTYPES = {'mem_norm_w': _jnp.float32, 'w_mem_kv': _jnp.float32, 'mem_k_norm_w': _jnp.float32, 'norm1_w': _jnp.float32, 'dn_w_in': _jnp.float32, 'dn_conv_w': _jnp.float32, 'dn_a_log': _jnp.float32, 'dn_dt_bias': _jnp.float32, 'dn_o_norm_w': _jnp.float32, 'fox_w_in': _jnp.float32, 'fox_f_bias': _jnp.float32, 'fox_q_norm_w': _jnp.float32, 'fox_k_norm_w': _jnp.float32, 'memq_norm_w': _jnp.float32, 'w_out': _jnp.float32, 'norm2_w': _jnp.float32, 'w_mlp1': _jnp.float32, 'w_mlp2': _jnp.float32}
MOMENT_SCALE = {'mem_norm_w': 5.337895e-01, 'w_mem_kv': 5.263799e-01, 'mem_k_norm_w': 9.521328e-01, 'norm1_w': 3.274016e+00, 'dn_w_in': 3.410761e-01, 'dn_conv_w': 9.370427e-01, 'dn_a_log': 9.223378e+00, 'dn_dt_bias': 8.308548e+00, 'dn_o_norm_w': 3.006983e+01, 'fox_w_in': 1.024006e+00, 'fox_f_bias': 1.603028e+01, 'fox_q_norm_w': 1.135419e+00, 'fox_k_norm_w': 1.137243e+00, 'memq_norm_w': 5.529302e-01, 'w_out': 2.235956e+00, 'norm2_w': 4.802871e+01, 'w_mlp1': 2.095646e+00, 'w_mlp2': 7.857531e+00}


def _to_microbatches(a, axis):
    t = _jnp.moveaxis(a, axis, 0)
    t = t.reshape((N_MICROBATCH, t.shape[0] // N_MICROBATCH) + t.shape[1:])
    return _jnp.moveaxis(t, 1, axis + 1)


def setup_inputs(seed: int = 0) -> dict:
    inp = _fwd_setup_inputs(seed)
    key = _jax.random.fold_in(_jax.random.key(seed), 7919)
    shape, _ = _output_shape()
    out = dict(inp)
    out["loss_target"] = _jax.random.normal(_jax.random.fold_in(key, 0), shape, _jnp.float32)
    for i, name in enumerate(TWIN_WEIGHTS):
        w = inp[name].astype(_jnp.float32)
        if MOMENT_SCALE is None:
            s = _jnp.sqrt(_jnp.mean(_jnp.square(w)) + 1e-30)
        else:
            s = MOMENT_SCALE[name]
        km, kv = _jax.random.split(_jax.random.fold_in(key, i + 1))
        out[name] = w
        out["m_" + name] = s * _jax.random.normal(km, w.shape, _jnp.float32)
        out["v_" + name] = (s * s) * _jax.random.uniform(kv, w.shape, _jnp.float32, 0.5, 1.5)
    if N_MICROBATCH > 1:
        for name, axis in PER_EXAMPLE_BATCH_AXIS.items():
            out[name] = _to_microbatches(out[name], axis)
    return {'x': out['x'], 'mem': out['mem'], 'mem_norm_w': out['mem_norm_w'], 'w_mem_kv': out['w_mem_kv'], 'mem_k_norm_w': out['mem_k_norm_w'], 'norm1_w': out['norm1_w'], 'dn_w_in': out['dn_w_in'], 'dn_conv_w': out['dn_conv_w'], 'dn_a_log': out['dn_a_log'], 'dn_dt_bias': out['dn_dt_bias'], 'dn_o_norm_w': out['dn_o_norm_w'], 'fox_w_in': out['fox_w_in'], 'fox_f_bias': out['fox_f_bias'], 'fox_q_norm_w': out['fox_q_norm_w'], 'fox_k_norm_w': out['fox_k_norm_w'], 'memq_norm_w': out['memq_norm_w'], 'w_out': out['w_out'], 'norm2_w': out['norm2_w'], 'w_mlp1': out['w_mlp1'], 'w_mlp2': out['w_mlp2'], 'loss_target': out['loss_target'], 'm_mem_norm_w': out['m_mem_norm_w'], 'm_w_mem_kv': out['m_w_mem_kv'], 'm_mem_k_norm_w': out['m_mem_k_norm_w'], 'm_norm1_w': out['m_norm1_w'], 'm_dn_w_in': out['m_dn_w_in'], 'm_dn_conv_w': out['m_dn_conv_w'], 'm_dn_a_log': out['m_dn_a_log'], 'm_dn_dt_bias': out['m_dn_dt_bias'], 'm_dn_o_norm_w': out['m_dn_o_norm_w'], 'm_fox_w_in': out['m_fox_w_in'], 'm_fox_f_bias': out['m_fox_f_bias'], 'm_fox_q_norm_w': out['m_fox_q_norm_w'], 'm_fox_k_norm_w': out['m_fox_k_norm_w'], 'm_memq_norm_w': out['m_memq_norm_w'], 'm_w_out': out['m_w_out'], 'm_norm2_w': out['m_norm2_w'], 'm_w_mlp1': out['m_w_mlp1'], 'm_w_mlp2': out['m_w_mlp2'], 'v_mem_norm_w': out['v_mem_norm_w'], 'v_w_mem_kv': out['v_w_mem_kv'], 'v_mem_k_norm_w': out['v_mem_k_norm_w'], 'v_norm1_w': out['v_norm1_w'], 'v_dn_w_in': out['v_dn_w_in'], 'v_dn_conv_w': out['v_dn_conv_w'], 'v_dn_a_log': out['v_dn_a_log'], 'v_dn_dt_bias': out['v_dn_dt_bias'], 'v_dn_o_norm_w': out['v_dn_o_norm_w'], 'v_fox_w_in': out['v_fox_w_in'], 'v_fox_f_bias': out['v_fox_f_bias'], 'v_fox_q_norm_w': out['v_fox_q_norm_w'], 'v_fox_k_norm_w': out['v_fox_k_norm_w'], 'v_memq_norm_w': out['v_memq_norm_w'], 'v_w_out': out['v_w_out'], 'v_norm2_w': out['v_norm2_w'], 'v_w_mlp1': out['v_w_mlp1'], 'v_w_mlp2': out['v_w_mlp2']}


def _loss(weights, diff, rest, loss_target):
    with _jax.named_scope("forward"):
        args = {**rest, TWIN_DIFF_INPUT: diff, **{k: w.astype(_WEIGHT_DTYPES[k]) for k, w in weights.items()}}
        y = _forward(args)
    with _jax.named_scope("loss_head"):
        err = _jnp.square(y.astype(_jnp.float32) - loss_target)
        return 0.5 * _jnp.sum(_jnp.mean(err, axis=-1)) if err.ndim else 0.5 * err


def _adamw(w, g, m, v):
    m = ADAM_B1 * m + (1.0 - ADAM_B1) * g
    v = ADAM_B2 * v + (1.0 - ADAM_B2) * _jnp.square(g)
    m_hat = m / (1.0 - ADAM_B1 ** ADAM_STEP)
    v_hat = v / (1.0 - ADAM_B2 ** ADAM_STEP)
    delta = -ADAM_LR * (m_hat / (_jnp.sqrt(v_hat) + ADAM_EPS) + ADAM_WD * w)
    return delta, m, v


def reference(x, mem, mem_norm_w, w_mem_kv, mem_k_norm_w, norm1_w, dn_w_in, dn_conv_w, dn_a_log, dn_dt_bias, dn_o_norm_w, fox_w_in, fox_f_bias, fox_q_norm_w, fox_k_norm_w, memq_norm_w, w_out, norm2_w, w_mlp1, w_mlp2, loss_target, m_mem_norm_w, m_w_mem_kv, m_mem_k_norm_w, m_norm1_w, m_dn_w_in, m_dn_conv_w, m_dn_a_log, m_dn_dt_bias, m_dn_o_norm_w, m_fox_w_in, m_fox_f_bias, m_fox_q_norm_w, m_fox_k_norm_w, m_memq_norm_w, m_w_out, m_norm2_w, m_w_mlp1, m_w_mlp2, v_mem_norm_w, v_w_mem_kv, v_mem_k_norm_w, v_norm1_w, v_dn_w_in, v_dn_conv_w, v_dn_a_log, v_dn_dt_bias, v_dn_o_norm_w, v_fox_w_in, v_fox_f_bias, v_fox_q_norm_w, v_fox_k_norm_w, v_memq_norm_w, v_w_out, v_norm2_w, v_w_mlp1, v_w_mlp2):
    given = dict(x=x, mem=mem, mem_norm_w=mem_norm_w, w_mem_kv=w_mem_kv, mem_k_norm_w=mem_k_norm_w, norm1_w=norm1_w, dn_w_in=dn_w_in, dn_conv_w=dn_conv_w, dn_a_log=dn_a_log, dn_dt_bias=dn_dt_bias, dn_o_norm_w=dn_o_norm_w, fox_w_in=fox_w_in, fox_f_bias=fox_f_bias, fox_q_norm_w=fox_q_norm_w, fox_k_norm_w=fox_k_norm_w, memq_norm_w=memq_norm_w, w_out=w_out, norm2_w=norm2_w, w_mlp1=w_mlp1, w_mlp2=w_mlp2, loss_target=loss_target, m_mem_norm_w=m_mem_norm_w, m_w_mem_kv=m_w_mem_kv, m_mem_k_norm_w=m_mem_k_norm_w, m_norm1_w=m_norm1_w, m_dn_w_in=m_dn_w_in, m_dn_conv_w=m_dn_conv_w, m_dn_a_log=m_dn_a_log, m_dn_dt_bias=m_dn_dt_bias, m_dn_o_norm_w=m_dn_o_norm_w, m_fox_w_in=m_fox_w_in, m_fox_f_bias=m_fox_f_bias, m_fox_q_norm_w=m_fox_q_norm_w, m_fox_k_norm_w=m_fox_k_norm_w, m_memq_norm_w=m_memq_norm_w, m_w_out=m_w_out, m_norm2_w=m_norm2_w, m_w_mlp1=m_w_mlp1, m_w_mlp2=m_w_mlp2, v_mem_norm_w=v_mem_norm_w, v_w_mem_kv=v_w_mem_kv, v_mem_k_norm_w=v_mem_k_norm_w, v_norm1_w=v_norm1_w, v_dn_w_in=v_dn_w_in, v_dn_conv_w=v_dn_conv_w, v_dn_a_log=v_dn_a_log, v_dn_dt_bias=v_dn_dt_bias, v_dn_o_norm_w=v_dn_o_norm_w, v_fox_w_in=v_fox_w_in, v_fox_f_bias=v_fox_f_bias, v_fox_q_norm_w=v_fox_q_norm_w, v_fox_k_norm_w=v_fox_k_norm_w, v_memq_norm_w=v_memq_norm_w, v_w_out=v_w_out, v_norm2_w=v_norm2_w, v_w_mlp1=v_w_mlp1, v_w_mlp2=v_w_mlp2)
    weights = {n: given[n] for n in TWIN_WEIGHTS}
    shared = {n: given[n] for n in SHARED_INPUTS}
    per_example = {n: given[n] for n in ['x', 'mem']}
    grad_fn = _jax.value_and_grad(_loss, argnums=(0, 1))

    def one_microbatch(ex, loss_target):
        ex = dict(ex)
        diff = ex.pop(TWIN_DIFF_INPUT)
        return grad_fn(weights, diff, {**shared, **ex}, loss_target)

    if N_MICROBATCH == 1:
        loss, (grad_w, grad_x) = one_microbatch(per_example, given["loss_target"])
    else:
        def body(carry, xs):
            loss_sum, grad_sum = carry
            l_k, (gw_k, gx_k) = one_microbatch(xs[0], xs[1])
            with _jax.named_scope("update"):
                return (loss_sum + l_k, _jax.tree.map(_jnp.add, grad_sum, gw_k)), gx_k

        init = (_jnp.zeros((), _jnp.float32), _jax.tree.map(_jnp.zeros_like, weights))
        (loss, grad_w), grad_x = _jax.lax.scan(body, init, (per_example, given["loss_target"]))
    with _jax.named_scope("update"):
        delta_w, new_m, new_v = {}, {}, {}
        for n in TWIN_WEIGHTS:
            delta_w[n], new_m[n], new_v[n] = _adamw(weights[n], grad_w[n], given["m_" + n], given["v_" + n])
    return (loss, grad_x, *[grad_w[n] for n in TWIN_WEIGHTS], *[delta_w[n] for n in TWIN_WEIGHTS],
            *[new_m[n] for n in TWIN_WEIGHTS], *[new_v[n] for n in TWIN_WEIGHTS])
```

```python
import functools

import jax
import jax.numpy as jnp
from jax import lax
from jax.experimental import pallas as pl
from jax.experimental.pallas import tpu as pltpu

F32 = jnp.float32
MM = jnp.bfloat16
HI = lax.Precision.HIGHEST

D_MODEL = 1024
HEAD_DIM = 128
N_HEADS = 8
MEM_HEADS = 4
MEM_WIDTH = MEM_HEADS * HEAD_DIM
N_MEM = 256
D_FF = 4 * D_MODEL
CHUNK = 64
EPS = 1e-6
QSCALE = HEAD_DIM ** -0.5
PROJ_W = 4736
TAIL = 4608
TAIL_BLK = TAIL // HEAD_DIM
ROWS = 256
VMEM_LIMIT = 56 * 1024 * 1024

ADAM_LR = 0.001
ADAM_B1 = 0.9
ADAM_B2 = 0.999
ADAM_EPS = 1e-08
ADAM_WD = 0.01
ADAM_STEP = 10

N_DEV = 8
N_CHIP = 4
MESH = pl.DeviceIdType.MESH


def _cparams(sem=None):
    return pltpu.CompilerParams(dimension_semantics=sem, vmem_limit_bytes=VMEM_LIMIT)


def _dot(a, b, ca, cb, hi):
    dims = (((ca,), (cb,)), ((), ()))
    if hi:
        return lax.dot_general(a, b, dims, precision=HI, preferred_element_type=F32)
    return lax.dot_general(a.astype(MM), b.astype(MM), dims, preferred_element_type=F32)


@functools.partial(jax.custom_vjp, nondiff_argnums=(2, 3, 4))
def mmul(a, b, ca, cb, hi):
    return _dot(a, b, ca, cb, hi)


def _mmul_fwd(a, b, ca, cb, hi):
    return _dot(a, b, ca, cb, hi), (a, b)


def _mmul_bwd(ca, cb, hi, res, g):
    a, b = res
    if ca == 1:
        da = _dot(g, b, 1, 1, hi) if cb == 0 else _dot(g, b, 1, 0, hi)
    else:
        da = _dot(b, g, 1, 1, hi) if cb == 0 else _dot(b, g, 0, 1, hi)
    if cb == 0:
        db = _dot(a, g, 0, 0, hi) if ca == 1 else _dot(a, g, 1, 0, hi)
    else:
        db = _dot(g, a, 0, 0, hi) if ca == 1 else _dot(g, a, 0, 1, hi)
    return da.astype(a.dtype), db.astype(b.dtype)


mmul.defvjp(_mmul_fwd, _mmul_bwd)


def _iota2(n, m):
    return lax.broadcasted_iota(jnp.int32, (n, m), 0), lax.broadcasted_iota(jnp.int32, (n, m), 1)


def _same_block(r, c, shift):
    return lax.shift_right_logical(r, shift) == lax.shift_right_logical(c, shift)


def _tri_inv_impl(a):
    n = a.shape[0]
    r, c = _iota2(n, n)
    eye = (r == c).astype(F32)
    b16, b32 = _same_block(r, c, 4), _same_block(r, c, 5)
    a0 = jnp.where(b16, a, 0.0)
    p = eye - a0
    b = _dot(a0, a0, 1, 0, True)
    p = p + _dot(p, b, 1, 0, True)
    b = _dot(b, b, 1, 0, True)
    p = p + _dot(p, b, 1, 0, True)
    b = _dot(b, b, 1, 0, True)
    p = p + _dot(p, b, 1, 0, True)
    a1 = jnp.where(jnp.logical_and(b32, jnp.logical_not(b16)), a, 0.0)
    p = p - _dot(_dot(p, a1, 1, 0, True), p, 1, 0, True)
    a2 = jnp.where(b32, 0.0, a)
    p = p - _dot(_dot(p, a2, 1, 0, True), p, 1, 0, True)
    return p


@jax.custom_vjp
def tri_inv(a):
    return _tri_inv_impl(a)


def _tri_inv_fwd(a):
    p = _tri_inv_impl(a)
    return p, p


def _tri_inv_bwd(p, g):
    return (-_dot(_dot(p, g, 0, 0, True), p, 1, 1, True),)


tri_inv.defvjp(_tri_inv_fwd, _tri_inv_bwd)


def _sigmoid(x):
    return 1.0 / (1.0 + jnp.exp(-x))


def _softplus(x):
    return jnp.maximum(x, 0.0) + jnp.log(1.0 + jnp.exp(-jnp.abs(x)))


def _silu(x):
    return x * _sigmoid(x)


def _rms(x, w):
    return x * lax.rsqrt(jnp.mean(x * x, axis=-1, keepdims=True) + EPS) * w


def _bf_round(x):
    return x.astype(MM).astype(F32)


def _acc(ref, val, first):
    @pl.when(first)
    def _():
        ref[...] = val

    @pl.when(jnp.logical_not(first))
    def _():
        ref[...] += val


def _tile(n, pref):
    if n % pref == 0:
        return pref
    return n


def matmul(a, b, *, ta=False, tb=False, res=None, out_dtype=F32, name, tm=512, tn=512, tk=512):
    m, k = (a.shape[1], a.shape[0]) if ta else a.shape
    n = b.shape[0] if tb else b.shape[1]
    assert (b.shape[1] if tb else b.shape[0]) == k, (a.shape, b.shape, ta, tb)
    tm, tn, tk = _tile(m, tm), _tile(n, tn), _tile(k, tk)
    nk = k // tk
    ca, cb = (0 if ta else 1), (1 if tb else 0)

    def body(a_ref, b_ref, *rest):
        r_ref = rest[0] if res is not None else None
        o_ref, acc_ref = rest[-2:]
        kk = pl.program_id(2)
        part = _dot(a_ref[...], b_ref[...], ca, cb, False)

        @pl.when(kk == 0)
        def _():
            acc_ref[...] = part

        @pl.when(kk > 0)
        def _():
            acc_ref[...] += part

        @pl.when(kk == nk - 1)
        def _():
            total = acc_ref[...] if r_ref is None else acc_ref[...] + r_ref[...]
            o_ref[...] = total.astype(o_ref.dtype)

    a_spec = pl.BlockSpec((tk, tm), lambda i, j, l: (l, i)) if ta else pl.BlockSpec((tm, tk), lambda i, j, l: (i, l))
    b_spec = pl.BlockSpec((tn, tk), lambda i, j, l: (j, l)) if tb else pl.BlockSpec((tk, tn), lambda i, j, l: (l, j))
    o_spec = pl.BlockSpec((tm, tn), lambda i, j, l: (i, j))
    extra = () if res is None else (res,)
    return pl.pallas_call(
        body, name=name, grid=(m // tm, n // tn, nk),
        in_specs=[a_spec, b_spec] + [o_spec] * len(extra), out_specs=o_spec,
        out_shape=jax.ShapeDtypeStruct((m, n), out_dtype),
        scratch_shapes=[pltpu.VMEM((tm, tn), F32)],
        compiler_params=_cparams(("parallel", "parallel", "arbitrary")),
    )(a, b, *extra)


def rms_fwd(x, w, *, name):
    t, d = x.shape

    def body(x_ref, w_ref, o_ref):
        o_ref[...] = _rms(x_ref[...], w_ref[...]).astype(o_ref.dtype)

    return pl.pallas_call(
        body, name=name, grid=(t // ROWS,),
        in_specs=[pl.BlockSpec((ROWS, d), lambda i: (i, 0)), pl.BlockSpec((1, d), lambda i: (0, 0))],
        out_specs=pl.BlockSpec((ROWS, d), lambda i: (i, 0)),
        out_shape=jax.ShapeDtypeStruct((t, d), MM), compiler_params=_cparams(("parallel",)),
    )(x, w)


def rms_bwd(x, w, dh, dres, *, name):
    t, d = x.shape

    def body(x_ref, w_ref, dh_ref, dr_ref, dx_ref, dw_ref):
        _, vjp = jax.vjp(_rms, x_ref[...], w_ref[...])
        dx, dw = vjp(dh_ref[...].astype(F32))
        dx_ref[...] = dx + dr_ref[...]
        _acc(dw_ref, dw, pl.program_id(0) == 0)

    row = pl.BlockSpec((ROWS, d), lambda i: (i, 0))
    vec = pl.BlockSpec((1, d), lambda i: (0, 0))
    return pl.pallas_call(
        body, name=name, grid=(t // ROWS,), in_specs=[row, vec, row, row], out_specs=[row, vec],
        out_shape=[jax.ShapeDtypeStruct((t, d), F32), jax.ShapeDtypeStruct((1, d), F32)],
        compiler_params=_cparams(("arbitrary",)),
    )(x, w, dh, dres)


def _sqrelu(x):
    return jnp.square(jnp.maximum(x, 0.0))


def act_fwd(ff, *, name):
    t, f = ff.shape

    def body(x_ref, o_ref):
        o_ref[...] = _sqrelu(x_ref[...]).astype(o_ref.dtype)

    blk = pl.BlockSpec((ROWS, f), lambda i: (i, 0))
    return pl.pallas_call(body, name=name, grid=(t // ROWS,), in_specs=[blk], out_specs=blk,
                          out_shape=jax.ShapeDtypeStruct((t, f), MM), compiler_params=_cparams(("parallel",)))(ff)


def act_bwd(ff, dact, *, name):
    t, f = ff.shape

    def body(x_ref, g_ref, o_ref):
        o_ref[...] = (g_ref[...] * 2.0 * jnp.maximum(x_ref[...], 0.0)).astype(o_ref.dtype)

    blk = pl.BlockSpec((ROWS, f), lambda i: (i, 0))
    return pl.pallas_call(body, name=name, grid=(t // ROWS,), in_specs=[blk, blk], out_specs=blk,
                          out_shape=jax.ShapeDtypeStruct((t, f), MM), compiler_params=_cparams(("parallel",)))(ff, dact)


def loss_fwd(y, target, *, name):
    t, d = y.shape

    def body(y_ref, t_ref, dy_ref, l_ref):
        e = y_ref[...] - t_ref[...]
        dy_ref[...] = e * (1.0 / d)
        part = 0.5 * jnp.sum(jnp.sum(e * e, axis=-1, keepdims=True) * (1.0 / d), axis=0, keepdims=True)
        _acc(l_ref, jnp.broadcast_to(part, (1, HEAD_DIM)), pl.program_id(0) == 0)

    blk = pl.BlockSpec((ROWS, d), lambda i: (i, 0))
    return pl.pallas_call(
        body, name=name, grid=(t // ROWS,), in_specs=[blk, blk],
        out_specs=[blk, pl.BlockSpec((1, HEAD_DIM), lambda i: (0, 0))],
        out_shape=[jax.ShapeDtypeStruct((t, d), F32), jax.ShapeDtypeStruct((1, HEAD_DIM), F32)],
        compiler_params=_cparams(("arbitrary",)),
    )(y, target)


def _mem_kv(mem, wn, wkn, *ws):
    mn = _rms(mem, wn)
    outs = []
    for h in range(MEM_HEADS):
        outs.append(_rms(mmul(mn, ws[h], 1, 0, False), wkn))
    for h in range(MEM_HEADS):
        outs.append(mmul(mn, ws[MEM_HEADS + h], 1, 0, False))
    return tuple(outs)


def _w_cols(w_ref):
    return [w_ref[:, h * HEAD_DIM:(h + 1) * HEAD_DIM] for h in range(2 * MEM_HEADS)]


def mem_fwd(mem, wn, wkv, wkn):
    def body(mem_ref, wn_ref, w_ref, wkn_ref, k_ref, v_ref):
        outs = _mem_kv(mem_ref[...], wn_ref[...], wkn_ref[...], *_w_cols(w_ref))
        for h in range(MEM_HEADS):
            k_ref[:, h * HEAD_DIM:(h + 1) * HEAD_DIM] = outs[h]
            v_ref[:, h * HEAD_DIM:(h + 1) * HEAD_DIM] = outs[MEM_HEADS + h]

    shp = jax.ShapeDtypeStruct((mem.shape[0], MEM_WIDTH), F32)
    return pl.pallas_call(body, name="mem_fwd", out_shape=[shp, shp], compiler_params=_cparams())(mem, wn, wkv, wkn)


def mem_bwd(mem, wn, wkv, wkn, dk0, dv0, dk1, dv1):
    def body(mem_ref, wn_ref, w_ref, wkn_ref, dk0_ref, dv0_ref, dk1_ref, dv1_ref, dwn_ref, dw_ref, dwkn_ref):
        _, vjp = jax.vjp(lambda wn_, wkn_, *ws: _mem_kv(mem_ref[...], wn_, wkn_, *ws),
                         wn_ref[...], wkn_ref[...], *[w.astype(F32) for w in _w_cols(w_ref)])
        cols = lambda a, b: tuple(a[:, h * HEAD_DIM:(h + 1) * HEAD_DIM] + b[:, h * HEAD_DIM:(h + 1) * HEAD_DIM]
                                  for h in range(MEM_HEADS))
        cts = cols(dk0_ref, dk1_ref) + cols(dv0_ref, dv1_ref)
        grads = vjp(cts)
        dwn_ref[...] = grads[0]
        dwkn_ref[...] = grads[1]
        for h in range(2 * MEM_HEADS):
            dw_ref[:, h * HEAD_DIM:(h + 1) * HEAD_DIM] = grads[2 + h]

    return pl.pallas_call(
        body, name="mem_bwd",
        out_shape=[jax.ShapeDtypeStruct((1, D_MODEL), F32), jax.ShapeDtypeStruct((D_MODEL, 2 * MEM_WIDTH), F32),
                   jax.ShapeDtypeStruct((1, HEAD_DIM), F32)],
        compiler_params=_cparams(),
    )(mem, wn, wkv, wkn, dk0, dv0, dk1, dv1)


def _memattn(q, wq, mk, mv):
    qn = _rms(q, wq) * QSCALE
    s = mmul(qn, mk, 1, 1, False)
    s = s - jnp.max(s, axis=-1, keepdims=True)
    p = jnp.exp(s)
    p = p / jnp.sum(p, axis=-1, keepdims=True)
    return mmul(p, mv, 1, 0, False)


def _memattn_specs(t):
    qspec = pl.BlockSpec((ROWS, HEAD_DIM), lambda h, i: (i, TAIL_BLK - MEM_HEADS + h))
    wspec = pl.BlockSpec((1, HEAD_DIM), lambda h, i: (0, 0))
    mspec = pl.BlockSpec((N_MEM, HEAD_DIM), lambda h, i: (0, h))
    ospec = pl.BlockSpec((ROWS, HEAD_DIM), lambda h, i: (i, h))
    return qspec, wspec, mspec, ospec


def memattn_fwd(proj, wq, mk, mv, *, name):
    t = proj.shape[0]
    qspec, wspec, mspec, ospec = _memattn_specs(t)

    def body(q_ref, w_ref, k_ref, v_ref, o_ref):
        o_ref[...] = _memattn(q_ref[...], w_ref[...], k_ref[...], v_ref[...]).astype(o_ref.dtype)

    return pl.pallas_call(
        body, name=name, grid=(MEM_HEADS, t // ROWS), in_specs=[qspec, wspec, mspec, mspec], out_specs=ospec,
        out_shape=jax.ShapeDtypeStruct((t, MEM_WIDTH), MM), compiler_params=_cparams(("parallel", "parallel")),
    )(proj, wq, mk, mv)


def memattn_bwd(proj, wq, mk, mv, dcat, *, name):
    t = proj.shape[0]
    qspec, wspec, mspec, ospec = _memattn_specs(t)
    dospec = pl.BlockSpec((ROWS, HEAD_DIM), lambda h, i: (i, N_HEADS + h))

    def body(q_ref, w_ref, k_ref, v_ref, do_ref, dq_ref, dw_ref, dk_ref, dv_ref):
        _, vjp = jax.vjp(_memattn, q_ref[...], w_ref[...], k_ref[...], v_ref[...])
        dq, dw, dk, dv = vjp(do_ref[...].astype(F32))
        dq_ref[...] = dq.astype(dq_ref.dtype)
        _acc(dw_ref, dw, jnp.logical_and(pl.program_id(0) == 0, pl.program_id(1) == 0))
        _acc(dk_ref, dk, pl.program_id(1) == 0)
        _acc(dv_ref, dv, pl.program_id(1) == 0)

    mshape = jax.ShapeDtypeStruct((N_MEM, MEM_WIDTH), F32)
    return pl.pallas_call(
        body, name=name, grid=(MEM_HEADS, t // ROWS), in_specs=[qspec, wspec, mspec, mspec, dospec],
        out_specs=[ospec, wspec, mspec, mspec],
        out_shape=[jax.ShapeDtypeStruct((t, MEM_WIDTH), MM), jax.ShapeDtypeStruct((1, HEAD_DIM), F32), mshape, mshape],
        compiler_params=_cparams(("arbitrary", "arbitrary")),
    )(proj, wq, mk, mv, dcat)


def _shift_rows(x, s, up):
    n = x.shape[0]
    r = lax.broadcasted_iota(jnp.int32, x.shape, 0)
    if up:
        return jnp.where(r < n - s, pltpu.roll(x, n - s, 0), 0.0)
    return jnp.where(r >= s, pltpu.roll(x, s, 0), 0.0)


def _conv_fwd_vals(x, w):
    xb = _bf_round(x)
    wb = _bf_round(w)
    c = xb * wb[3:4, :]
    for j in range(3):
        c = c + _shift_rows(xb, 3 - j, False) * wb[j:j + 1, :]
    return xb, wb, c


def dn_prep_fwd(proj, conv_w):
    t = proj.shape[0]

    def body(x_ref, w_ref, o_ref):
        j = pl.program_id(0)
        _, _, c = _conv_fwd_vals(x_ref[...], w_ref[...])
        s = _silu(c)
        r = lax.rsqrt(jnp.sum(s * s, axis=-1, keepdims=True) + EPS)
        scale = jnp.where(j < N_HEADS, QSCALE, 1.0)
        o_ref[...] = jnp.where(j < 2 * N_HEADS, s * r * scale, s)

    return pl.pallas_call(
        body, name="dn_prep_fwd", grid=(3 * N_HEADS,),
        in_specs=[pl.BlockSpec((t, HEAD_DIM), lambda j: (0, j)), pl.BlockSpec((4, HEAD_DIM), lambda j: (0, j))],
        out_specs=pl.BlockSpec((None, t, HEAD_DIM), lambda j: (j // N_HEADS, 0, j % N_HEADS)),
        out_shape=jax.ShapeDtypeStruct((3, t, D_MODEL), F32), compiler_params=_cparams(("parallel",)),
    )(proj, conv_w)


def dn_prep_bwd(proj, conv_w, dqkv):
    t = proj.shape[0]

    def body(x_ref, w_ref, g_ref, dx_ref, dw_ref):
        j = pl.program_id(0)
        xb, wb, c = _conv_fwd_vals(x_ref[...], w_ref[...])
        sg = _sigmoid(c)
        s = c * sg
        g = g_ref[...]
        r = lax.rsqrt(jnp.sum(s * s, axis=-1, keepdims=True) + EPS)
        scale = jnp.where(j < N_HEADS, QSCALE, 1.0)
        gn = g * scale
        ds_norm = r * gn - s * (r * r * r) * jnp.sum(gn * s, axis=-1, keepdims=True)
        ds = jnp.where(j < 2 * N_HEADS, ds_norm, g)
        dc = ds * (sg + s * (1.0 - sg))
        dx = dc * wb[3:4, :]
        rows = [jnp.sum(dc * xb, axis=0, keepdims=True)]
        for jj in range(2, -1, -1):
            sh = 3 - jj
            dx = dx + _shift_rows(dc, sh, True) * wb[jj:jj + 1, :]
            rows.insert(0, jnp.sum(dc * _shift_rows(xb, sh, False), axis=0, keepdims=True))
        dx_ref[...] = dx.astype(dx_ref.dtype)
        dw_ref[...] = jnp.concatenate(rows + [jnp.zeros((4, HEAD_DIM), F32)], axis=0)

    col = pl.BlockSpec((t, HEAD_DIM), lambda j: (0, j))
    return pl.pallas_call(
        body, name="dn_prep_bwd", grid=(3 * N_HEADS,),
        in_specs=[col, pl.BlockSpec((4, HEAD_DIM), lambda j: (0, j)),
                  pl.BlockSpec((None, t, HEAD_DIM), lambda j: (j // N_HEADS, 0, j % N_HEADS))],
        out_specs=[col, pl.BlockSpec((8, HEAD_DIM), lambda j: (0, j))],
        out_shape=[jax.ShapeDtypeStruct((t, 3 * D_MODEL), MM), jax.ShapeDtypeStruct((8, 3 * D_MODEL), F32)],
        compiler_params=_cparams(("parallel",)),
    )(proj, conv_w, dqkv)


def _tri_ones(n, upper):
    r, c = _iota2(n, n)
    return (r <= c).astype(F32) if upper else (r >= c).astype(F32)


def dn_gates_fwd(proj, a_log, dt_bias):
    t = proj.shape[0]

    def body(x_ref, al_ref, dt_ref, o_ref):
        lane = lax.broadcasted_iota(jnp.int32, (CHUNK, HEAD_DIM), 1)
        tri = _tri_ones(CHUNK, False)

        def step(c, carry):
            rows = pl.ds(pl.multiple_of(c * CHUNK, CHUNK), CHUNK)
            x = x_ref[rows, :]
            g = jnp.where(lane < N_HEADS, -jnp.exp(al_ref[...]) * _softplus(x + dt_ref[...]), 0.0)
            gc = _dot(tri, g, 1, 0, True)
            o_ref[rows, :] = jnp.where(lane < N_HEADS, gc, jnp.where(lane < 2 * N_HEADS, _sigmoid(x), 0.0))
            return carry

        lax.fori_loop(0, t // CHUNK, step, 0)

    vec = pl.BlockSpec((1, HEAD_DIM), lambda i: (0, 0))
    return pl.pallas_call(
        body, name="dn_gates_fwd", grid=(1,),
        in_specs=[pl.BlockSpec((t, HEAD_DIM), lambda i: (0, TAIL_BLK)), vec, vec],
        out_specs=pl.BlockSpec((t, HEAD_DIM), lambda i: (0, 0)),
        out_shape=jax.ShapeDtypeStruct((t, HEAD_DIM), F32), compiler_params=_cparams(("arbitrary",)),
    )(proj, a_log, dt_bias)


def dn_gates_bwd(proj, a_log, dt_bias, dgates):
    t = proj.shape[0]

    def body(x_ref, al_ref, dt_ref, g_ref, dx_ref, dal_ref, ddt_ref):
        lane = lax.broadcasted_iota(jnp.int32, (CHUNK, HEAD_DIM), 1)
        tri = _tri_ones(CHUNK, True)
        dal_ref[...] = jnp.zeros_like(dal_ref)
        ddt_ref[...] = jnp.zeros_like(ddt_ref)

        def step(c, carry):
            rows = pl.ds(pl.multiple_of(c * CHUNK, CHUNK), CHUNK)
            x = x_ref[rows, :]
            dgc = jnp.where(lane < N_HEADS, g_ref[rows, :], 0.0)
            dg = _dot(tri, dgc, 1, 0, True)
            ea = -jnp.exp(al_ref[...])
            z = x + dt_ref[...]
            da = jnp.where(lane < N_HEADS, dg * ea * _sigmoid(z), 0.0)
            gval = jnp.where(lane < N_HEADS, ea * _softplus(z), 0.0)
            beta = _sigmoid(x)
            db = jnp.where(jnp.logical_and(lane >= N_HEADS, lane < 2 * N_HEADS), g_ref[rows, :] * beta * (1.0 - beta), 0.0)
            dx_ref[rows, :] = (da + db).astype(dx_ref.dtype)
            dal_ref[...] += jnp.sum(dg * gval, axis=0, keepdims=True)
            ddt_ref[...] += jnp.sum(da, axis=0, keepdims=True)
            return carry

        lax.fori_loop(0, t // CHUNK, step, 0)

    vec = pl.BlockSpec((1, HEAD_DIM), lambda i: (0, 0))
    full = pl.BlockSpec((t, HEAD_DIM), lambda i: (0, 0))
    return pl.pallas_call(
        body, name="dn_gates_bwd", grid=(1,),
        in_specs=[pl.BlockSpec((t, HEAD_DIM), lambda i: (0, TAIL_BLK)), vec, vec, full],
        out_specs=[full, vec, vec],
        out_shape=[jax.ShapeDtypeStruct((t, HEAD_DIM), MM), jax.ShapeDtypeStruct((1, HEAD_DIM), F32),
                   jax.ShapeDtypeStruct((1, HEAD_DIM), F32)],
        compiler_params=_cparams(("arbitrary",)),
    )(proj, a_log, dt_bias, dgates)


def _dn_chunk(q, k, v, gcol, grow, bcol, state):
    r, c = _iota2(CHUNK, CHUNK)
    causal, strict = r >= c, r > c
    decay = jnp.where(causal, jnp.exp(jnp.where(causal, gcol - grow, 0.0)), 0.0)
    kb = k * bcol
    a = jnp.where(strict, mmul(kb, k, 1, 1, False) * decay, 0.0)
    tm = tri_inv(a)
    u = mmul(tm, v * bcol, 1, 0, False)
    w = mmul(tm, kb * jnp.exp(gcol), 1, 0, False)
    qk = jnp.where(causal, mmul(q, k, 1, 1, False) * decay, 0.0)
    v_new = u - mmul(w, state, 1, 0, False)
    out = mmul(q * jnp.exp(gcol), state, 1, 0, False) + mmul(qk, v_new, 1, 0, False)
    rr = lax.broadcasted_iota(jnp.int32, (CHUNK, 1), 0)
    g_last = jnp.sum(jnp.where(rr == CHUNK - 1, gcol, 0.0), axis=0, keepdims=True)
    k_dec = k * jnp.exp(g_last - gcol)
    new_state = state * jnp.exp(g_last) + mmul(k_dec, v_new, 0, 0, False)
    return out, new_state


def _dn_specs(t):
    nc = t // CHUNK
    head = lambda which: pl.BlockSpec((None, t, HEAD_DIM), lambda h: (which, 0, h))
    flat = pl.BlockSpec((t, HEAD_DIM), lambda h: (0, h))
    col = pl.BlockSpec((1, nc, CHUNK, 1), lambda h: (h, 0, 0, 0))
    row = pl.BlockSpec((1, nc, 1, CHUNK), lambda h: (h, 0, 0, 0))
    st = pl.BlockSpec((1, nc, HEAD_DIM, HEAD_DIM), lambda h: (h, 0, 0, 0))
    return nc, head, flat, col, row, st


def dn_core_fwd(qkv, gcol, grow, bcol):
    t = qkv.shape[1]
    nc, head, flat, col, row, st = _dn_specs(t)

    def body(q_ref, k_ref, v_ref, gc_ref, gr_ref, bc_ref, o_ref, s_ref):
        def step(c, state):
            rows = pl.ds(pl.multiple_of(c * CHUNK, CHUNK), CHUNK)
            s_ref[0, c] = state
            out, new_state = _dn_chunk(q_ref[rows, :], k_ref[rows, :], v_ref[rows, :],
                                       gc_ref[0, c], gr_ref[0, c], bc_ref[0, c], state)
            o_ref[rows, :] = out
            return new_state

        lax.fori_loop(0, nc, step, jnp.zeros((HEAD_DIM, HEAD_DIM), F32))

    return pl.pallas_call(
        body, name="dn_core_fwd", grid=(N_HEADS,),
        in_specs=[head(0), head(1), head(2), col, row, col], out_specs=[flat, st],
        out_shape=[jax.ShapeDtypeStruct((t, D_MODEL), F32), jax.ShapeDtypeStruct((N_HEADS, nc, HEAD_DIM, HEAD_DIM), F32)],
        compiler_params=_cparams(("parallel",)),
    )(qkv, qkv, qkv, gcol, grow, bcol)


def dn_core_bwd(qkv, gcol, grow, bcol, states, do):
    t = qkv.shape[1]
    nc, head, flat, col, row, st = _dn_specs(t)

    def body(q_ref, k_ref, v_ref, gc_ref, gr_ref, bc_ref, s_ref, do_ref, dqkv_ref, dgc_ref, dgr_ref, dbc_ref):
        def step(i, dstate):
            c = nc - 1 - i
            rows = pl.ds(pl.multiple_of(c * CHUNK, CHUNK), CHUNK)
            _, vjp = jax.vjp(_dn_chunk, q_ref[rows, :], k_ref[rows, :], v_ref[rows, :],
                             gc_ref[0, c], gr_ref[0, c], bc_ref[0, c], s_ref[0, c])
            dq, dk, dv, dgc, dgr, dbc, dstate_in = vjp((do_ref[rows, :], dstate))
            dqkv_ref[0, rows, :] = dq
            dqkv_ref[1, rows, :] = dk
            dqkv_ref[2, rows, :] = dv
            dgc_ref[0, c] = dgc
            dgr_ref[0, c] = dgr
            dbc_ref[0, c] = dbc
            return dstate_in

        lax.fori_loop(0, nc, step, jnp.zeros((HEAD_DIM, HEAD_DIM), F32))

    return pl.pallas_call(
        body, name="dn_core_bwd", grid=(N_HEADS,),
        in_specs=[head(0), head(1), head(2), col, row, col, st, flat],
        out_specs=[pl.BlockSpec((3, t, HEAD_DIM), lambda h: (0, 0, h)), col, row, col],
        out_shape=[jax.ShapeDtypeStruct((3, t, D_MODEL), F32)] + [
            jax.ShapeDtypeStruct((N_HEADS, nc, CHUNK, 1), F32), jax.ShapeDtypeStruct((N_HEADS, nc, 1, CHUNK), F32),
            jax.ShapeDtypeStruct((N_HEADS, nc, CHUNK, 1), F32)],
        compiler_params=_cparams(("parallel",)),
    )(qkv, qkv, qkv, gcol, grow, bcol, states, do)


def gates_to_heads(gates):
    t = gates.shape[0]
    nc = t // CHUNK
    g = gates[:, :N_HEADS].T.reshape(N_HEADS, nc, CHUNK)
    b = gates[:, N_HEADS:2 * N_HEADS].T.reshape(N_HEADS, nc, CHUNK)
    return g[..., None], g[:, :, None, :], b[..., None]


def heads_to_gates(dgcol, dgrow, dbcol):
    nh, nc = dgcol.shape[:2]
    dg = (dgcol[..., 0] + dgrow[:, :, 0, :]).reshape(nh, nc * CHUNK).T
    db = dbcol[..., 0].reshape(nh, nc * CHUNK).T
    return jnp.concatenate([dg, db, jnp.zeros((nc * CHUNK, HEAD_DIM - 2 * nh), F32)], axis=1)


def _dn_out(o, z, w):
    return _rms(o, w) * _silu(z)


def _gate_specs():
    o_spec = pl.BlockSpec((ROWS, HEAD_DIM), lambda h, i: (i, h))
    z_spec = pl.BlockSpec((ROWS, HEAD_DIM), lambda h, i: (i, 3 * N_HEADS + h))
    w_spec = pl.BlockSpec((1, HEAD_DIM), lambda h, i: (0, 0))
    return o_spec, z_spec, w_spec


def dn_out_fwd(o, proj, w):
    t = o.shape[0]
    o_spec, z_spec, w_spec = _gate_specs()

    def body(o_ref, z_ref, w_ref, y_ref):
        y_ref[...] = _dn_out(o_ref[...], z_ref[...], w_ref[...]).astype(y_ref.dtype)

    return pl.pallas_call(
        body, name="dn_out_fwd", grid=(N_HEADS, t // ROWS), in_specs=[o_spec, z_spec, w_spec], out_specs=o_spec,
        out_shape=jax.ShapeDtypeStruct((t, D_MODEL), MM), compiler_params=_cparams(("parallel", "parallel")),
    )(o, proj, w)


def dn_out_bwd(o, proj, w, dcat):
    t = o.shape[0]
    o_spec, z_spec, w_spec = _gate_specs()

    def body(o_ref, z_ref, w_ref, g_ref, do_ref, dz_ref, dw_ref):
        _, vjp = jax.vjp(_dn_out, o_ref[...], z_ref[...], w_ref[...])
        do, dz, dw = vjp(g_ref[...].astype(F32))
        do_ref[...] = do
        dz_ref[...] = dz.astype(dz_ref.dtype)
        _acc(dw_ref, dw, jnp.logical_and(pl.program_id(0) == 0, pl.program_id(1) == 0))

    return pl.pallas_call(
        body, name="dn_out_bwd", grid=(N_HEADS, t // ROWS), in_specs=[o_spec, z_spec, w_spec, o_spec],
        out_specs=[o_spec, o_spec, w_spec],
        out_shape=[jax.ShapeDtypeStruct((t, D_MODEL), F32), jax.ShapeDtypeStruct((t, D_MODEL), MM),
                   jax.ShapeDtypeStruct((1, HEAD_DIM), F32)],
        compiler_params=_cparams(("arbitrary", "arbitrary")),
    )(o, proj, w, dcat)


def _fox_norm(x, w, scale):
    return _rms(x, w) * scale


def _fox_prep_specs():
    x_spec = pl.BlockSpec((ROWS, HEAD_DIM), lambda j, i: (i, j))
    w_spec = pl.BlockSpec((None, 1, HEAD_DIM), lambda j, i: (j // N_HEADS, 0, 0))
    y_spec = pl.BlockSpec((None, ROWS, HEAD_DIM), lambda j, i: (j // N_HEADS, i, j % N_HEADS))
    return x_spec, w_spec, y_spec


def fox_prep_fwd(proj, wqk):
    t = proj.shape[0]
    x_spec, w_spec, y_spec = _fox_prep_specs()

    def body(x_ref, w_ref, y_ref):
        scale = jnp.where(pl.program_id(0) < N_HEADS, QSCALE, 1.0)
        y_ref[...] = _fox_norm(x_ref[...], w_ref[...], scale).astype(y_ref.dtype)

    return pl.pallas_call(
        body, name="fox_prep_fwd", grid=(2 * N_HEADS, t // ROWS), in_specs=[x_spec, w_spec], out_specs=y_spec,
        out_shape=jax.ShapeDtypeStruct((2, t, D_MODEL), MM), compiler_params=_cparams(("parallel", "parallel")),
    )(proj, wqk)


def fox_prep_bwd(proj, wqk, dq, dk):
    t = proj.shape[0]
    x_spec, w_spec, _ = _fox_prep_specs()
    g_spec = pl.BlockSpec((ROWS, HEAD_DIM), lambda j, i: (i, j % N_HEADS))

    def body(x_ref, w_ref, dq_ref, dk_ref, dx_ref, dw_ref):
        j = pl.program_id(0)
        scale = jnp.where(j < N_HEADS, QSCALE, 1.0)
        g = jnp.where(j < N_HEADS, dq_ref[...], dk_ref[...])
        _, vjp = jax.vjp(lambda x, w: _fox_norm(x, w, scale), x_ref[...], w_ref[...])
        dx, dw = vjp(g)
        dx_ref[...] = dx.astype(dx_ref.dtype)
        _acc(dw_ref, dw, jnp.logical_and(j % N_HEADS == 0, pl.program_id(1) == 0))

    return pl.pallas_call(
        body, name="fox_prep_bwd", grid=(2 * N_HEADS, t // ROWS), in_specs=[x_spec, w_spec, g_spec, g_spec],
        out_specs=[x_spec, w_spec],
        out_shape=[jax.ShapeDtypeStruct((t, 2 * D_MODEL), MM), jax.ShapeDtypeStruct((2, 1, HEAD_DIM), F32)],
        compiler_params=_cparams(("arbitrary", "arbitrary")),
    )(proj, wqk, dq, dk)


def _row_pick(x, i):
    r = lax.broadcasted_iota(jnp.int32, x.shape, 0)
    return jnp.sum(jnp.where(r == i, x, 0.0), axis=0, keepdims=True)


def fox_gates_fwd(proj, f_bias):
    t = proj.shape[0]
    blk = HEAD_DIM

    def body(x_ref, b_ref, o_ref):
        lane = lax.broadcasted_iota(jnp.int32, (blk, HEAD_DIM), 1)
        tri = _tri_ones(blk, False)

        def step(c, carry):
            rows = pl.ds(pl.multiple_of(c * blk, blk), blk)
            lf = jnp.where(lane < N_HEADS, -_softplus(-(x_ref[rows, :] + b_ref[...])), 0.0)
            cum = _dot(tri, lf, 1, 0, True) + carry
            o_ref[rows, :] = cum
            return _row_pick(cum, blk - 1)

        lax.fori_loop(0, t // blk, step, jnp.zeros((1, HEAD_DIM), F32))

    vec = pl.BlockSpec((1, HEAD_DIM), lambda i: (0, 0))
    return pl.pallas_call(
        body, name="fox_gates_fwd", grid=(1,),
        in_specs=[pl.BlockSpec((t, HEAD_DIM), lambda i: (0, TAIL_BLK)), vec],
        out_specs=pl.BlockSpec((t, HEAD_DIM), lambda i: (0, 0)),
        out_shape=jax.ShapeDtypeStruct((t, HEAD_DIM), F32), compiler_params=_cparams(("arbitrary",)),
    )(proj, f_bias)


def fox_gates_bwd(proj, f_bias, dfcum):
    t = proj.shape[0]
    blk = HEAD_DIM
    nb = t // blk

    def body(x_ref, b_ref, g_ref, dx_ref, db_ref):
        lane = lax.broadcasted_iota(jnp.int32, (blk, HEAD_DIM), 1)
        tri = _tri_ones(blk, True)
        db_ref[...] = jnp.zeros_like(db_ref)

        def step(i, carry):
            c = nb - 1 - i
            rows = pl.ds(pl.multiple_of(c * blk, blk), blk)
            g = jnp.where(lane < N_HEADS, g_ref[rows, :], 0.0)
            dlf = _dot(tri, g, 1, 0, True) + carry
            dx = jnp.where(lane < N_HEADS, dlf * _sigmoid(-(x_ref[rows, :] + b_ref[...])), 0.0)
            dx_ref[rows, :] = dx.astype(dx_ref.dtype)
            db_ref[...] += jnp.sum(dx, axis=0, keepdims=True)
            return carry + jnp.sum(g, axis=0, keepdims=True)

        lax.fori_loop(0, nb, step, jnp.zeros((1, HEAD_DIM), F32))

    vec = pl.BlockSpec((1, HEAD_DIM), lambda i: (0, 0))
    full = pl.BlockSpec((t, HEAD_DIM), lambda i: (0, 0))
    return pl.pallas_call(
        body, name="fox_gates_bwd", grid=(1,),
        in_specs=[pl.BlockSpec((t, HEAD_DIM), lambda i: (0, TAIL_BLK)), vec, full], out_specs=[full, vec],
        out_shape=[jax.ShapeDtypeStruct((t, HEAD_DIM), MM), jax.ShapeDtypeStruct((1, HEAD_DIM), F32)],
        compiler_params=_cparams(("arbitrary",)),
    )(proj, f_bias, dfcum)


def fcum_to_heads(fcum):
    f = fcum[:, :N_HEADS].T
    return f[:, :, None], f[:, None, :]


def heads_to_fcum(dfcol, dfrow):
    d = (dfcol[:, :, 0] + dfrow[:, 0, :]).T
    return jnp.concatenate([d, jnp.zeros((d.shape[0], HEAD_DIM - N_HEADS), F32)], axis=1)


def _fox_tq(t):
    return min(t, 256)


def _fox_specs(t):
    tq = _fox_tq(t)
    q_spec = pl.BlockSpec((None, tq, HEAD_DIM), lambda h, i: (0, i, h))
    k_spec = pl.BlockSpec((None, t, HEAD_DIM), lambda h, i: (1, 0, h))
    v_spec = pl.BlockSpec((t, HEAD_DIM), lambda h, i: (0, 2 * N_HEADS + h))
    gate_spec = pl.BlockSpec((tq, HEAD_DIM), lambda h, i: (i, 3 * N_HEADS + h))
    col_spec = pl.BlockSpec((None, tq, 1), lambda h, i: (h, i, 0))
    row_spec = pl.BlockSpec((None, 1, t), lambda h, i: (h, 0, 0))
    blk_spec = pl.BlockSpec((tq, HEAD_DIM), lambda h, i: (i, h))
    head_spec = pl.BlockSpec((t, HEAD_DIM), lambda h, i: (0, h))
    return tq, q_spec, k_spec, v_spec, gate_spec, col_spec, row_spec, blk_spec, head_spec


def _fox_scores(q, k, fcol, frow, i, tq, t):
    s = _dot(q, k, 1, 1, False) + (fcol - frow)
    r = lax.broadcasted_iota(jnp.int32, (tq, t), 0) + i * tq
    c = lax.broadcasted_iota(jnp.int32, (tq, t), 1)
    return s, c <= r


def fox_attn_fwd(qk, proj, fcol, frow):
    t = proj.shape[0]
    tq, q_spec, k_spec, v_spec, gate_spec, col_spec, row_spec, blk_spec, _ = _fox_specs(t)

    def body(q_ref, k_ref, v_ref, gate_ref, fc_ref, fr_ref, mix_ref, o_ref, lse_ref):
        s, mask = _fox_scores(q_ref[...], k_ref[...], fc_ref[...], fr_ref[...], pl.program_id(1), tq, t)
        s = jnp.where(mask, s, -1e30)
        m = jnp.max(s, axis=-1, keepdims=True)
        p = jnp.where(mask, jnp.exp(s - m), 0.0)
        l = jnp.sum(p, axis=-1, keepdims=True)
        o = _dot(p, v_ref[...], 1, 0, False) / l
        o_ref[...] = o
        mix_ref[...] = (o * _sigmoid(gate_ref[...])).astype(mix_ref.dtype)
        lse_ref[...] = m + jnp.log(l)

    return pl.pallas_call(
        body, name="fox_attn_fwd", grid=(N_HEADS, t // tq),
        in_specs=[q_spec, k_spec, v_spec, gate_spec, col_spec, row_spec], out_specs=[blk_spec, blk_spec, col_spec],
        out_shape=[jax.ShapeDtypeStruct((t, D_MODEL), MM), jax.ShapeDtypeStruct((t, D_MODEL), F32),
                   jax.ShapeDtypeStruct((N_HEADS, t, 1), F32)],
        compiler_params=_cparams(("parallel", "parallel")),
    )(qk, qk, proj, proj, fcol, frow)


def fox_attn_bwd(qk, proj, fcol, frow, o, lse, dcat):
    t = proj.shape[0]
    tq, q_spec, k_spec, v_spec, gate_spec, col_spec, row_spec, blk_spec, head_spec = _fox_specs(t)

    def body(q_ref, k_ref, v_ref, gate_ref, fc_ref, fr_ref, o_ref, lse_ref, g_ref,
             dq_ref, dk_ref, dv_ref, dgate_ref, dfc_ref, dfr_ref):
        i = pl.program_id(1)
        sg = _sigmoid(gate_ref[...])
        g = g_ref[...].astype(F32)
        o_pre = o_ref[...]
        do = g * sg
        dgate_ref[...] = (g * o_pre * sg * (1.0 - sg)).astype(dgate_ref.dtype)
        s, mask = _fox_scores(q_ref[...], k_ref[...], fc_ref[...], fr_ref[...], i, tq, t)
        p = jnp.where(mask, jnp.exp(jnp.where(mask, s, 0.0) - lse_ref[...]), 0.0)
        dp = _dot(do, v_ref[...], 1, 1, False)
        delta = jnp.sum(do * o_pre, axis=-1, keepdims=True)
        ds = p * (dp - delta)
        dq_ref[...] = _dot(ds, k_ref[...], 1, 0, False)
        _acc(dk_ref, _dot(ds, q_ref[...], 0, 0, False), i == 0)
        _acc(dv_ref, _dot(p, do, 0, 0, False), i == 0)
        dfc_ref[...] = jnp.sum(ds, axis=-1, keepdims=True)
        _acc(dfr_ref, -jnp.sum(ds, axis=0, keepdims=True), i == 0)

    f32 = lambda *s: jax.ShapeDtypeStruct(s, F32)
    return pl.pallas_call(
        body, name="fox_attn_bwd", grid=(N_HEADS, t // tq),
        in_specs=[q_spec, k_spec, v_spec, gate_spec, col_spec, row_spec, blk_spec, col_spec, blk_spec],
        out_specs=[blk_spec, head_spec, head_spec, blk_spec, col_spec, row_spec],
        out_shape=[f32(t, D_MODEL), f32(t, D_MODEL), f32(t, D_MODEL), jax.ShapeDtypeStruct((t, D_MODEL), MM),
                   f32(N_HEADS, t, 1), f32(N_HEADS, 1, t)],
        compiler_params=_cparams(("parallel", "arbitrary")),
    )(qk, qk, proj, proj, fcol, frow, o, lse, dcat)


def adamw(w, g, m, v, *, name):
    r, c = w.shape
    rb = ROWS if r % ROWS == 0 else r

    def body(w_ref, g_ref, m_ref, v_ref, d_ref, nm_ref, nv_ref):
        g_ = g_ref[...]
        m_ = ADAM_B1 * m_ref[...] + (1.0 - ADAM_B1) * g_
        v_ = ADAM_B2 * v_ref[...] + (1.0 - ADAM_B2) * jnp.square(g_)
        m_hat = m_ / (1.0 - ADAM_B1 ** ADAM_STEP)
        v_hat = v_ / (1.0 - ADAM_B2 ** ADAM_STEP)
        d_ref[...] = -ADAM_LR * (m_hat / (jnp.sqrt(v_hat) + ADAM_EPS) + ADAM_WD * w_ref[...])
        nm_ref[...] = m_
        nv_ref[...] = v_

    blk = pl.BlockSpec((rb, c), lambda i: (i, 0))
    shp = jax.ShapeDtypeStruct((r, c), F32)
    return pl.pallas_call(body, name=name, grid=(r // rb,), in_specs=[blk] * 4, out_specs=[blk] * 3,
                          out_shape=[shp] * 3, compiler_params=_cparams(("parallel",)))(w, g, m, v)


def _place():
    x, y, c = lax.axis_index("x"), lax.axis_index("y"), lax.axis_index("c")
    return x, y, c, [(1 - x, y), (x, 1 - y), (1 - x, 1 - y)]


ANY = pl.BlockSpec(memory_space=pl.ANY)


def all_gather_shards(shard):
    r, w = shard.shape
    rh = r // 2

    def body(x_ref, o_ref, send_sems, recv_sems, fsend_sems, frecv_sems, local_sem):
        x, y, c, chips = _place()
        me_chip = 2 * x + y
        sibling = (x, y, 1 - c)
        mine_rows = pl.ds(pl.multiple_of(c * rh, 16), rh)
        other_rows = pl.ds(pl.multiple_of((1 - c) * rh, 16), rh)

        def slot(chip):
            return 2 * chip[0] + chip[1]

        local = pltpu.make_async_copy(x_ref, o_ref.at[me_chip], local_sem)
        local.start()
        sends = []
        for j, chip in enumerate(chips):
            cp = pltpu.make_async_remote_copy(
                src_ref=x_ref.at[mine_rows], dst_ref=o_ref.at[me_chip, mine_rows], send_sem=send_sems.at[j],
                recv_sem=recv_sems.at[j], device_id=(chip[0], chip[1], c), device_id_type=MESH)
            cp.start()
            sends.append(cp)
        for j, chip in enumerate(chips):
            landed = o_ref.at[slot(chip), mine_rows]
            pltpu.make_async_remote_copy(src_ref=landed, dst_ref=landed, send_sem=send_sems.at[j], recv_sem=recv_sems.at[j],
                                         device_id=(chip[0], chip[1], c), device_id_type=MESH).wait_recv()
            cp = pltpu.make_async_remote_copy(src_ref=landed, dst_ref=landed, send_sem=fsend_sems.at[j],
                                              recv_sem=frecv_sems.at[j], device_id=sibling, device_id_type=MESH)
            cp.start()
            sends.append(cp)
        for j, chip in enumerate(chips):
            passed = o_ref.at[slot(chip), other_rows]
            pltpu.make_async_remote_copy(src_ref=passed, dst_ref=passed, send_sem=fsend_sems.at[j], recv_sem=frecv_sems.at[j],
                                         device_id=sibling, device_id_type=MESH).wait_recv()
        for cp in sends:
            cp.wait_send()
        local.wait()

    return pl.pallas_call(
        body, name="all_gather_shards", in_specs=[ANY], out_specs=ANY,
        out_shape=jax.ShapeDtypeStruct((N_CHIP, r, w), shard.dtype),
        scratch_shapes=[pltpu.SemaphoreType.DMA((3,))] * 4 + [pltpu.SemaphoreType.DMA],
    )(shard)


def rs_pair_exchange(g):
    n, r, w = g.shape
    rh = r // 2

    def body(g_ref, o_ref, send_sem, recv_sem):
        x, y, c, _ = _place()
        other_rows = pl.ds(pl.multiple_of((1 - c) * rh, 8), rh)
        cp = pltpu.make_async_remote_copy(src_ref=g_ref.at[:, other_rows], dst_ref=o_ref, send_sem=send_sem,
                                          recv_sem=recv_sem, device_id=(x, y, 1 - c), device_id_type=MESH)
        cp.start()
        cp.wait()

    return pl.pallas_call(
        body, name="rs_pair_exchange", in_specs=[ANY], out_specs=ANY,
        out_shape=jax.ShapeDtypeStruct((n, rh, w), g.dtype),
        scratch_shapes=[pltpu.SemaphoreType.DMA, pltpu.SemaphoreType.DMA],
    )(g)


def rs_pair_add(place, g, got):
    n, r, w = g.shape
    rh = r // 2
    rb = ROWS
    assert rh % rb == 0, (rh, rb)
    nb = rh // rb

    def body(place_ref, g_ref, got_ref, o_ref):
        o_ref[...] = g_ref[...] + got_ref[...]

    return pl.pallas_call(
        body, name="rs_pair_add",
        grid_spec=pltpu.PrefetchScalarGridSpec(
            num_scalar_prefetch=1, grid=(n, nb),
            in_specs=[pl.BlockSpec((None, rb, w), lambda j, i, p: (j, p[0] * nb + i, 0)),
                      pl.BlockSpec((None, rb, w), lambda j, i, p: (j, i, 0))],
            out_specs=pl.BlockSpec((None, rb, w), lambda j, i, p: (j, i, 0))),
        out_shape=jax.ShapeDtypeStruct((n, rh, w), F32), compiler_params=_cparams(("parallel", "parallel")),
    )(place, g, got)


def rs_chip_exchange(p):
    n, rh, w = p.shape

    def body(p_ref, o_ref, send_sems, recv_sems):
        x, y, c, chips = _place()
        cps = []
        for j, chip in enumerate(chips):
            cp = pltpu.make_async_remote_copy(
                src_ref=p_ref.at[2 * chip[0] + chip[1]], dst_ref=o_ref.at[j], send_sem=send_sems.at[j],
                recv_sem=recv_sems.at[j], device_id=(chip[0], chip[1], c), device_id_type=MESH)
            cp.start()
            cps.append(cp)
        for cp in cps:
            cp.wait()

    return pl.pallas_call(
        body, name="rs_chip_exchange", in_specs=[ANY], out_specs=ANY,
        out_shape=jax.ShapeDtypeStruct((3, rh, w), p.dtype),
        scratch_shapes=[pltpu.SemaphoreType.DMA((3,)), pltpu.SemaphoreType.DMA((3,))],
    )(p)


def rs_chip_add(place, p, got):
    n, rh, w = p.shape
    rb = ROWS
    nb = rh // rb

    def body(place_ref, p_ref, a_ref, b_ref, c_ref, o_ref):
        o_ref[...] = ((p_ref[...] + a_ref[...]) + b_ref[...]) + c_ref[...]

    got_spec = lambda k: pl.BlockSpec((None, rb, w), lambda i, pr: (k, i, 0))
    return pl.pallas_call(
        body, name="rs_chip_add",
        grid_spec=pltpu.PrefetchScalarGridSpec(
            num_scalar_prefetch=1, grid=(nb,),
            in_specs=[pl.BlockSpec((None, rb, w), lambda i, pr: (pr[1], i, 0)), got_spec(0), got_spec(1), got_spec(2)],
            out_specs=pl.BlockSpec((rb, w), lambda i, pr: (i, 0))),
        out_shape=jax.ShapeDtypeStruct((rh, w), F32), compiler_params=_cparams(("parallel",)),
    )(place, p, got, got, got)


def rs_pair_gather(half):
    rh, w = half.shape

    def body(h_ref, o_ref, send_sem, recv_sem, local_sem):
        x, y, c, _ = _place()
        mine_rows = pl.ds(pl.multiple_of(c * rh, 8), rh)
        local = pltpu.make_async_copy(h_ref, o_ref.at[mine_rows], local_sem)
        local.start()
        cp = pltpu.make_async_remote_copy(src_ref=h_ref, dst_ref=o_ref.at[mine_rows], send_sem=send_sem,
                                          recv_sem=recv_sem, device_id=(x, y, 1 - c), device_id_type=MESH)
        cp.start()
        cp.wait_send()
        other_rows = pl.ds(pl.multiple_of((1 - c) * rh, 8), rh)
        pltpu.make_async_remote_copy(src_ref=h_ref, dst_ref=o_ref.at[other_rows], send_sem=send_sem, recv_sem=recv_sem,
                                     device_id=(x, y, 1 - c), device_id_type=MESH).wait_recv()
        local.wait()

    return pl.pallas_call(
        body, name="rs_pair_gather", in_specs=[ANY], out_specs=ANY,
        out_shape=jax.ShapeDtypeStruct((2 * rh, w), half.dtype),
        scratch_shapes=[pltpu.SemaphoreType.DMA, pltpu.SemaphoreType.DMA, pltpu.SemaphoreType.DMA],
    )(half)


def all_reduce_small(v):
    r, w = v.shape

    def body(v_ref, o_ref, buf, send_sems, recv_sems):
        x, y, c, _ = _place()
        me = 4 * x + 2 * y + c
        flip = lambda a, bit: 1 - a if bit else a
        cps = []
        for k in range(1, N_DEV):
            peer = (flip(x, k & 4), flip(y, k & 2), flip(c, k & 1))
            cp = pltpu.make_async_remote_copy(src_ref=v_ref, dst_ref=buf.at[me], send_sem=send_sems.at[k - 1],
                                              recv_sem=recv_sems.at[k - 1], device_id=peer, device_id_type=MESH)
            cp.start()
            cps.append((cp, 4 * peer[0] + 2 * peer[1] + peer[2]))
        buf[me] = v_ref[...]
        for k, (cp, peer_id) in enumerate(cps):
            pltpu.make_async_remote_copy(src_ref=v_ref, dst_ref=buf.at[peer_id], send_sem=send_sems.at[k],
                                         recv_sem=recv_sems.at[k], device_id=(x, y, c), device_id_type=MESH).wait_recv()
        for cp, _ in cps:
            cp.wait_send()
        acc = buf[0]
        for d in range(1, N_DEV):
            acc = acc + buf[d]
        o_ref[...] = acc

    vm = pl.BlockSpec(memory_space=pltpu.VMEM)
    return pl.pallas_call(
        body, name="all_reduce_small", in_specs=[vm], out_specs=vm, out_shape=jax.ShapeDtypeStruct((r, w), F32),
        scratch_shapes=[pltpu.VMEM((N_DEV, r, w), F32), pltpu.SemaphoreType.DMA((N_DEV - 1,)),
                        pltpu.SemaphoreType.DMA((N_DEV - 1,))],
    )(v)


def _vec8(v):
    return jnp.zeros((1, HEAD_DIM), F32).at[0, :N_HEADS].set(v.reshape(N_HEADS))


def _layer_fwd(i, x_in, wt, sm, mem_k, mem_v):
    tag = f"l{i}_"
    h = rms_fwd(x_in, sm["norm1_w"][i][None], name=tag + "rms1")
    w_in = wt["dn_w_in"] if i == 0 else wt["fox_w_in"]
    proj = matmul(h, w_in, name=tag + "proj", tm=256, tk=1024)
    sv = dict(x_in=x_in, h=h, proj=proj)
    if i == 0:
        qkv = dn_prep_fwd(proj, wt["conv_w"])
        gates = dn_gates_fwd(proj, _vec8(sm["dn_a_log"]), _vec8(sm["dn_dt_bias"]))
        gcol, grow, bcol = gates_to_heads(gates)
        o, states = dn_core_fwd(qkv, gcol, grow, bcol)
        mix = dn_out_fwd(o, proj, sm["dn_o_norm_w"])
        sv.update(qkv=qkv, gcol=gcol, grow=grow, bcol=bcol, states=states, o=o)
    else:
        wqk = jnp.stack([sm["fox_q_norm_w"], sm["fox_k_norm_w"]])
        qk = fox_prep_fwd(proj, wqk)
        fcum = fox_gates_fwd(proj, _vec8(sm["fox_f_bias"]))
        fcol, frow = fcum_to_heads(fcum)
        mix, o, lse = fox_attn_fwd(qk, proj, fcol, frow)
        sv.update(wqk=wqk, qk=qk, fcol=fcol, frow=frow, o=o, lse=lse)
    mem_out = memattn_fwd(proj, sm["memq_norm_w"][i][None], mem_k, mem_v, name=tag + "memattn_fwd")
    cat = jnp.concatenate([mix, mem_out], axis=1)
    x_mid = matmul(cat, wt["w_out"][i], res=x_in, name=tag + "out_proj")
    h2 = rms_fwd(x_mid, sm["norm2_w"][i][None], name=tag + "rms2")
    ff = matmul(h2, wt["w_mlp1"][i], name=tag + "mlp1", tk=1024)
    act = act_fwd(ff, name=tag + "act_fwd")
    x_out = matmul(act, wt["w_mlp2"][i], res=x_mid, name=tag + "mlp2", tk=1024)
    sv.update(cat=cat, x_mid=x_mid, h2=h2, ff=ff, act=act)
    return x_out, sv


def _layer_bwd(i, dx_out, sv, wt, sm, mem_k, mem_v):
    tag = f"l{i}_"
    big, small = {}, {}
    dact = matmul(dx_out, wt["w_mlp2"][i], tb=True, name=tag + "d_act", tk=1024)
    big["w_mlp2"] = matmul(sv["act"], dx_out, ta=True, name=tag + "d_w_mlp2")
    dff = act_bwd(sv["ff"], dact, name=tag + "act_bwd")
    dh2 = matmul(dff, wt["w_mlp1"][i], tb=True, name=tag + "d_h2", tk=1024)
    big["w_mlp1"] = matmul(sv["h2"], dff, ta=True, name=tag + "d_w_mlp1")
    dx_mid, small["norm2_w"] = rms_bwd(sv["x_mid"], sm["norm2_w"][i][None], dh2, dx_out, name=tag + "rms2_bwd")
    dcat = matmul(dx_mid, wt["w_out"][i], tb=True, name=tag + "d_cat", tk=1024)
    big["w_out"] = matmul(sv["cat"], dx_mid, ta=True, name=tag + "d_w_out")
    proj = sv["proj"]
    dqm, small["memq_norm_w"], dmk, dmv = memattn_bwd(proj, sm["memq_norm_w"][i][None], mem_k, mem_v, dcat,
                                                      name=tag + "memattn_bwd")
    t = proj.shape[0]
    pad = jnp.zeros((t, PROJ_W - TAIL - HEAD_DIM), MM)
    if i == 0:
        do, dz, small["dn_o_norm_w"] = dn_out_bwd(sv["o"], proj, sm["dn_o_norm_w"], dcat)
        dqkv, dgc, dgr, dbc = dn_core_bwd(sv["qkv"], sv["gcol"], sv["grow"], sv["bcol"], sv["states"], do)
        dtail, dal, ddt = dn_gates_bwd(proj, _vec8(sm["dn_a_log"]), _vec8(sm["dn_dt_bias"]), heads_to_gates(dgc, dgr, dbc))
        dmain, dconv = dn_prep_bwd(proj, wt["conv_w"], dqkv)
        small["dn_a_log"], small["dn_dt_bias"] = dal[:, :N_HEADS], ddt[:, :N_HEADS]
        big["conv_w"] = dconv[:4]
        dproj = jnp.concatenate([dmain, dz, dqm, dtail, pad], axis=1)
    else:
        dq, dk, dv, dgate, dfc, dfr = fox_attn_bwd(sv["qk"], proj, sv["fcol"], sv["frow"], sv["o"], sv["lse"], dcat)
        dtail, dfb = fox_gates_bwd(proj, _vec8(sm["fox_f_bias"]), heads_to_fcum(dfc, dfr))
        dqk, dwqk = fox_prep_bwd(proj, sv["wqk"], dq, dk)
        small["fox_f_bias"] = dfb[:, :N_HEADS]
        small["fox_q_norm_w"], small["fox_k_norm_w"] = dwqk[0], dwqk[1]
        dproj = jnp.concatenate([dqk, dv.astype(MM), dgate, dqm, dtail, pad], axis=1)
    w_in = wt["dn_w_in"] if i == 0 else wt["fox_w_in"]
    dh = matmul(dproj, w_in, tb=True, name=tag + "d_h")
    big["w_in"] = matmul(sv["h"], dproj, ta=True, name=tag + "d_w_in", tm=256)
    dx_in, small["norm1_w"] = rms_bwd(sv["x_in"], sm["norm1_w"][i][None], dh, dx_mid, name=tag + "rms1_bwd")
    return dx_in, big, small, (dmk, dmv)


def local_step(x, mem, target, wt, sm):
    mem_k, mem_v = mem_fwd(mem, sm["mem_norm_w"][None], wt["w_mem_kv"], sm["mem_k_norm_w"][None])
    x0, sv0 = _layer_fwd(0, x, wt, sm, mem_k, mem_v)
    x1, sv1 = _layer_fwd(1, x0, wt, sm, mem_k, mem_v)
    dy, loss = loss_fwd(x1, target, name="loss")
    dx1, big1, small1, dm1 = _layer_bwd(1, dy, sv1, wt, sm, mem_k, mem_v)
    dx0, big0, small0, dm0 = _layer_bwd(0, dx1, sv0, wt, sm, mem_k, mem_v)
    dwn, dwkv, dwkn = mem_bwd(mem, sm["mem_norm_w"][None], wt["w_mem_kv"], sm["mem_k_norm_w"][None], *dm0, *dm1)
    small = dict(mem_norm_w=dwn[0], mem_k_norm_w=dwkn[0],
                 norm1_w=jnp.concatenate([small0["norm1_w"], small1["norm1_w"]]),
                 norm2_w=jnp.concatenate([small0["norm2_w"], small1["norm2_w"]]),
                 memq_norm_w=jnp.concatenate([small0["memq_norm_w"], small1["memq_norm_w"]]),
                 dn_a_log=small0["dn_a_log"], dn_dt_bias=small0["dn_dt_bias"], dn_o_norm_w=small0["dn_o_norm_w"],
                 fox_f_bias=small1["fox_f_bias"], fox_q_norm_w=small1["fox_q_norm_w"], fox_k_norm_w=small1["fox_k_norm_w"])
    big = dict(w_mem_kv=dwkv, dn_w_in=big0["w_in"], fox_w_in=big1["w_in"], conv_w=big0["conv_w"],
               w_out=[big0["w_out"], big1["w_out"]], w_mlp1=[big0["w_mlp1"], big1["w_mlp1"]],
               w_mlp2=[big0["w_mlp2"], big1["w_mlp2"]])
    return loss, dx0, big, small


def w_in_to_kernel(w, n_scalars):
    pad = jnp.zeros((w.shape[0], PROJ_W - TAIL - n_scalars), w.dtype)
    return jnp.concatenate([w[:, :4096], w[:, 4096 + n_scalars:], w[:, 4096:4096 + n_scalars], pad], axis=1)


def w_in_from_kernel(w, n_scalars):
    return jnp.concatenate([w[:, :4096], w[:, TAIL:TAIL + n_scalars], w[:, 4096:TAIL]], axis=1)


PACK_W = 1024
PACK_PARTS = (("w_mem_kv", 256), ("dn_w_in", 1156), ("fox_w_in", 1154), ("w_out", 768), ("w_mlp1", 2048),
              ("w_mlp2", 2048), ("conv_w", 3))
PACK_ROWS = 7680


def _pack_offsets():
    offs, at = {}, 0
    for name, rows in PACK_PARTS:
        offs[name] = at
        at += -(-rows // 16) * 16
    assert at <= PACK_ROWS
    return offs


def pack_shard(parts, dtype):
    pieces, at = [], 0
    for name, rows in PACK_PARTS:
        padded = -(-rows // 16) * 16
        p = parts[name].astype(dtype).reshape(rows, PACK_W)
        pieces.append(jnp.pad(p, ((0, padded - rows), (0, 0))))
        at += padded
    pieces.append(jnp.zeros((PACK_ROWS - at, PACK_W), dtype))
    return jnp.concatenate(pieces, axis=0)


def unpack_shard(packed, shapes):
    offs = _pack_offsets()
    return {name: packed[offs[name]:offs[name] + rows].reshape(shapes[name]) for name, rows in PACK_PARTS}


SHARD_SHAPES = dict(w_mem_kv=(256, 1024), dn_w_in=(1024, 1156), fox_w_in=(1024, 1154), w_out=(2, 384, 1024),
                    w_mlp1=(2, 1024, 1024), w_mlp2=(2, 1024, 1024), conv_w=(4, 768))

SMALL = (("mem_norm_w", 1024), ("mem_k_norm_w", 128), ("norm1_w", 2048), ("dn_a_log", 8), ("dn_dt_bias", 8),
         ("dn_o_norm_w", 128), ("fox_f_bias", 8), ("fox_q_norm_w", 128), ("fox_k_norm_w", 128), ("memq_norm_w", 256),
         ("norm2_w", 2048))
SMALL_ROWS = 8
LOSS_AT = sum(n for _, n in SMALL)


def pack_small(parts, extra=None):
    flat = [parts[name].astype(F32).reshape(-1) for name, _ in SMALL]
    used = LOSS_AT
    if extra is not None:
        flat.append(extra.reshape(1))
        used += 1
    flat.append(jnp.zeros((SMALL_ROWS * PACK_W - used,), F32))
    return jnp.concatenate(flat).reshape(SMALL_ROWS, PACK_W)


def unpack_small(packed, shapes):
    flat, out, at = packed.reshape(-1), {}, 0
    for name, n in SMALL:
        out[name] = flat[at:at + n].reshape(shapes[name])
        at += n
    return out


def full_weights(gathered):
    per = [unpack_shard(gathered[j], SHARD_SHAPES) for j in range(N_CHIP)]
    cat = lambda name, axis: jnp.concatenate([p[name] for p in per], axis=axis)
    return dict(
        w_mem_kv=cat("w_mem_kv", 0),
        dn_w_in=w_in_to_kernel(cat("dn_w_in", 1), 2 * N_HEADS),
        fox_w_in=w_in_to_kernel(cat("fox_w_in", 1), N_HEADS),
        conv_w=cat("conv_w", 1).astype(F32),
        w_out=cat("w_out", 1), w_mlp1=cat("w_mlp1", 2), w_mlp2=cat("w_mlp2", 1))


def grad_shards(big):
    dn = w_in_from_kernel(big["dn_w_in"], 2 * N_HEADS)
    fox = w_in_from_kernel(big["fox_w_in"], N_HEADS)
    w_out, w_mlp1, w_mlp2 = jnp.stack(big["w_out"]), jnp.stack(big["w_mlp1"]), jnp.stack(big["w_mlp2"])
    slots = []
    for j in range(N_CHIP):
        slots.append(pack_shard(dict(
            w_mem_kv=big["w_mem_kv"][256 * j:256 * (j + 1)],
            dn_w_in=dn[:, 1156 * j:1156 * (j + 1)], fox_w_in=fox[:, 1154 * j:1154 * (j + 1)],
            w_out=w_out[:, 384 * j:384 * (j + 1)], w_mlp1=w_mlp1[:, :, 1024 * j:1024 * (j + 1)],
            w_mlp2=w_mlp2[:, 1024 * j:1024 * (j + 1)], conv_w=big["conv_w"][:, 768 * j:768 * (j + 1)]), F32))
    return jnp.stack(slots)


def reduce_scatter(grads, place):
    got = rs_pair_exchange(grads)
    pair = rs_pair_add(place, grads, got)
    got = rs_chip_exchange(pair)
    half = rs_chip_add(place, pair, got)
    return rs_pair_gather(half)


def _adam_all(w, g, m, v, name):
    shape = w.shape
    r2 = lambda a: a.reshape(-1, shape[-1])
    d, nm, nv = adamw(r2(w), r2(g), r2(m), r2(v), name=name)
    return d.reshape(shape), nm.reshape(shape), nv.reshape(shape)


BIG = ("w_mem_kv", "dn_w_in", "dn_conv_w", "fox_w_in", "w_out", "w_mlp1", "w_mlp2")
WEIGHTS = ("mem_norm_w", "w_mem_kv", "mem_k_norm_w", "norm1_w", "dn_w_in", "dn_conv_w", "dn_a_log", "dn_dt_bias",
           "dn_o_norm_w", "fox_w_in", "fox_f_bias", "fox_q_norm_w", "fox_k_norm_w", "memq_norm_w", "w_out", "norm2_w",
           "w_mlp1", "w_mlp2")


def kernel(x, mem, mem_norm_w, w_mem_kv, mem_k_norm_w, norm1_w, dn_w_in, dn_conv_w, dn_a_log, dn_dt_bias, dn_o_norm_w, fox_w_in, fox_f_bias, fox_q_norm_w, fox_k_norm_w, memq_norm_w, w_out, norm2_w, w_mlp1, w_mlp2, loss_target, m_mem_norm_w, m_w_mem_kv, m_mem_k_norm_w, m_norm1_w, m_dn_w_in, m_dn_conv_w, m_dn_a_log, m_dn_dt_bias, m_dn_o_norm_w, m_fox_w_in, m_fox_f_bias, m_fox_q_norm_w, m_fox_k_norm_w, m_memq_norm_w, m_w_out, m_norm2_w, m_w_mlp1, m_w_mlp2, v_mem_norm_w, v_w_mem_kv, v_mem_k_norm_w, v_norm1_w, v_dn_w_in, v_dn_conv_w, v_dn_a_log, v_dn_dt_bias, v_dn_o_norm_w, v_fox_w_in, v_fox_f_bias, v_fox_q_norm_w, v_fox_k_norm_w, v_memq_norm_w, v_w_out, v_norm2_w, v_w_mlp1, v_w_mlp2):
    args = dict(locals())
    w = {n: args[n] for n in WEIGHTS}
    m = {n: args["m_" + n] for n in WEIGHTS}
    v = {n: args["v_" + n] for n in WEIGHTS}
    place = jnp.stack([lax.axis_index("c"), 2 * lax.axis_index("x") + lax.axis_index("y")]).astype(jnp.int32)

    shard = pack_shard(dict(w_mem_kv=w_mem_kv, dn_w_in=dn_w_in[0], fox_w_in=fox_w_in[0], w_out=w_out, w_mlp1=w_mlp1,
                            w_mlp2=w_mlp2, conv_w=dn_conv_w[0]), MM)
    wt = full_weights(all_gather_shards(shard))
    sm = dict(mem_norm_w=mem_norm_w, mem_k_norm_w=mem_k_norm_w, norm1_w=norm1_w, norm2_w=norm2_w, memq_norm_w=memq_norm_w,
              dn_a_log=dn_a_log[0], dn_dt_bias=dn_dt_bias[0], dn_o_norm_w=dn_o_norm_w, fox_f_bias=fox_f_bias[0],
              fox_q_norm_w=fox_q_norm_w, fox_k_norm_w=fox_k_norm_w)
    loss_part, dx, big, small = local_step(x[0], mem[0], loss_target[0], wt, sm)

    big_sum = unpack_shard(reduce_scatter(grad_shards(big), place), SHARD_SHAPES)
    small_sum = all_reduce_small(pack_small(small, loss_part[0, :1]))
    loss = small_sum.reshape(-1)[LOSS_AT]
    grads = unpack_small(small_sum, {n: w[n].shape for n, _ in SMALL})
    grads.update(w_mem_kv=big_sum["w_mem_kv"], dn_w_in=big_sum["dn_w_in"][None], fox_w_in=big_sum["fox_w_in"][None],
                 dn_conv_w=big_sum["conv_w"][None], w_out=big_sum["w_out"], w_mlp1=big_sum["w_mlp1"], w_mlp2=big_sum["w_mlp2"])

    delta, new_m, new_v = {}, {}, {}
    for n in BIG:
        delta[n], new_m[n], new_v[n] = _adam_all(w[n], grads[n], m[n], v[n], "adamw_" + n)
    shapes = {n: w[n].shape for n, _ in SMALL}
    d_s, m_s, v_s = adamw(pack_small(w), small_sum, pack_small(m), pack_small(v), name="adamw_small")
    for out, packed in ((delta, d_s), (new_m, m_s), (new_v, v_s)):
        out.update(unpack_small(packed, shapes))
    return (loss, dx[None], *[grads[n] for n in WEIGHTS], *[delta[n] for n in WEIGHTS],
            *[new_m[n] for n in WEIGHTS], *[new_v[n] for n in WEIGHTS])
```

```python
import functools

import jax
import jax.numpy as jnp
from jax import lax
from jax.experimental import pallas as pl
from jax.experimental.pallas import tpu as pltpu

F32 = jnp.float32
MM = jnp.bfloat16
HI = lax.Precision.HIGHEST

D_MODEL = 1024
HEAD_DIM = 128
N_HEADS = 8
MEM_HEADS = 4
MEM_WIDTH = MEM_HEADS * HEAD_DIM
N_MEM = 256
D_FF = 4 * D_MODEL
CHUNK = 64
EPS = 1e-6
QSCALE = HEAD_DIM ** -0.5
PROJ_W = 4736
TAIL = 4608
TAIL_BLK = TAIL // HEAD_DIM
ROWS = 256
VMEM_LIMIT = 56 * 1024 * 1024

ADAM_LR = 0.001
ADAM_B1 = 0.9
ADAM_B2 = 0.999
ADAM_EPS = 1e-08
ADAM_WD = 0.01
ADAM_STEP = 10

N_DEV = 8
N_CHIP = 4
MESH = pl.DeviceIdType.MESH


def _cparams(sem=None):
    return pltpu.CompilerParams(dimension_semantics=sem, vmem_limit_bytes=VMEM_LIMIT)


def _dot(a, b, ca, cb, hi):
    dims = (((ca,), (cb,)), ((), ()))
    if hi:
        return lax.dot_general(a, b, dims, precision=HI, preferred_element_type=F32)
    return lax.dot_general(a.astype(MM), b.astype(MM), dims, preferred_element_type=F32)


@functools.partial(jax.custom_vjp, nondiff_argnums=(2, 3, 4))
def mmul(a, b, ca, cb, hi):
    return _dot(a, b, ca, cb, hi)


def _mmul_fwd(a, b, ca, cb, hi):
    return _dot(a, b, ca, cb, hi), (a, b)


def _mmul_bwd(ca, cb, hi, res, g):
    a, b = res
    if ca == 1:
        da = _dot(g, b, 1, 1, hi) if cb == 0 else _dot(g, b, 1, 0, hi)
    else:
        da = _dot(b, g, 1, 1, hi) if cb == 0 else _dot(b, g, 0, 1, hi)
    if cb == 0:
        db = _dot(a, g, 0, 0, hi) if ca == 1 else _dot(a, g, 1, 0, hi)
    else:
        db = _dot(g, a, 0, 0, hi) if ca == 1 else _dot(g, a, 0, 1, hi)
    return da.astype(a.dtype), db.astype(b.dtype)


mmul.defvjp(_mmul_fwd, _mmul_bwd)


def _iota2(n, m):
    return lax.broadcasted_iota(jnp.int32, (n, m), 0), lax.broadcasted_iota(jnp.int32, (n, m), 1)


def _same_block(r, c, shift):
    return lax.shift_right_logical(r, shift) == lax.shift_right_logical(c, shift)


def _tri_inv_impl(a):
    n = a.shape[0]
    r, c = _iota2(n, n)
    eye = (r == c).astype(F32)
    b16, b32 = _same_block(r, c, 4), _same_block(r, c, 5)
    a0 = jnp.where(b16, a, 0.0)
    p = eye - a0
    b = _dot(a0, a0, 1, 0, True)
    p = p + _dot(p, b, 1, 0, True)
    b = _dot(b, b, 1, 0, True)
    p = p + _dot(p, b, 1, 0, True)
    b = _dot(b, b, 1, 0, True)
    p = p + _dot(p, b, 1, 0, True)
    a1 = jnp.where(jnp.logical_and(b32, jnp.logical_not(b16)), a, 0.0)
    p = p - _dot(_dot(p, a1, 1, 0, True), p, 1, 0, True)
    a2 = jnp.where(b32, 0.0, a)
    p = p - _dot(_dot(p, a2, 1, 0, True), p, 1, 0, True)
    return p


@jax.custom_vjp
def tri_inv(a):
    return _tri_inv_impl(a)


def _tri_inv_fwd(a):
    p = _tri_inv_impl(a)
    return p, p


def _tri_inv_bwd(p, g):
    return (-_dot(_dot(p, g, 0, 0, True), p, 1, 1, True),)


tri_inv.defvjp(_tri_inv_fwd, _tri_inv_bwd)


def _sigmoid(x):
    return 1.0 / (1.0 + jnp.exp(-x))


def _softplus(x):
    return jnp.maximum(x, 0.0) + jnp.log(1.0 + jnp.exp(-jnp.abs(x)))


def _silu(x):
    return x * _sigmoid(x)


def _rms(x, w):
    return x * lax.rsqrt(jnp.mean(x * x, axis=-1, keepdims=True) + EPS) * w


def _bf_round(x):
    return x.astype(MM).astype(F32)


def _acc(ref, val, first):
    @pl.when(first)
    def _():
        ref[...] = val

    @pl.when(jnp.logical_not(first))
    def _():
        ref[...] += val


def _tile(n, pref):
    if n % pref == 0:
        return pref
    return n


def matmul(a, b, *, ta=False, tb=False, res=None, out_dtype=F32, name, tm=1024, tn=1024, tk=1024):
    m, k = (a.shape[1], a.shape[0]) if ta else a.shape
    n = b.shape[0] if tb else b.shape[1]
    assert (b.shape[1] if tb else b.shape[0]) == k, (a.shape, b.shape, ta, tb)
    tm, tn, tk = _tile(m, tm), _tile(n, tn), _tile(k, tk)
    nk = k // tk
    ca, cb = (0 if ta else 1), (1 if tb else 0)

    def body(a_ref, b_ref, *rest):
        r_ref = rest[0] if res is not None else None
        o_ref, acc_ref = rest[-2:]
        kk = pl.program_id(2)
        part = _dot(a_ref[...], b_ref[...], ca, cb, False)

        @pl.when(kk == 0)
        def _():
            acc_ref[...] = part

        @pl.when(kk > 0)
        def _():
            acc_ref[...] += part

        @pl.when(kk == nk - 1)
        def _():
            total = acc_ref[...] if r_ref is None else acc_ref[...] + r_ref[...]
            o_ref[...] = total.astype(o_ref.dtype)

    a_spec = pl.BlockSpec((tk, tm), lambda i, j, l: (l, i)) if ta else pl.BlockSpec((tm, tk), lambda i, j, l: (i, l))
    b_spec = pl.BlockSpec((tn, tk), lambda i, j, l: (j, l)) if tb else pl.BlockSpec((tk, tn), lambda i, j, l: (l, j))
    o_spec = pl.BlockSpec((tm, tn), lambda i, j, l: (i, j))
    extra = () if res is None else (res,)
    return pl.pallas_call(
        body, name=name, grid=(m // tm, n // tn, nk),
        in_specs=[a_spec, b_spec] + [o_spec] * len(extra), out_specs=o_spec,
        out_shape=jax.ShapeDtypeStruct((m, n), out_dtype),
        scratch_shapes=[pltpu.VMEM((tm, tn), F32)],
        compiler_params=_cparams(("parallel", "parallel", "arbitrary")),
    )(a, b, *extra)


def rms_fwd(x, w, *, name):
    t, d = x.shape

    def body(x_ref, w_ref, o_ref):
        o_ref[...] = _rms(x_ref[...], w_ref[...]).astype(o_ref.dtype)

    return pl.pallas_call(
        body, name=name, grid=(t // ROWS,),
        in_specs=[pl.BlockSpec((ROWS, d), lambda i: (i, 0)), pl.BlockSpec((1, d), lambda i: (0, 0))],
        out_specs=pl.BlockSpec((ROWS, d), lambda i: (i, 0)),
        out_shape=jax.ShapeDtypeStruct((t, d), MM), compiler_params=_cparams(("parallel",)),
    )(x, w)


def rms_bwd(x, w, dh, dres, *, name):
    t, d = x.shape

    def body(x_ref, w_ref, dh_ref, dr_ref, dx_ref, dw_ref):
        _, vjp = jax.vjp(_rms, x_ref[...], w_ref[...])
        dx, dw = vjp(dh_ref[...].astype(F32))
        dx_ref[...] = dx + dr_ref[...]
        _acc(dw_ref, dw, pl.program_id(0) == 0)

    row = pl.BlockSpec((ROWS, d), lambda i: (i, 0))
    vec = pl.BlockSpec((1, d), lambda i: (0, 0))
    return pl.pallas_call(
        body, name=name, grid=(t // ROWS,), in_specs=[row, vec, row, row], out_specs=[row, vec],
        out_shape=[jax.ShapeDtypeStruct((t, d), F32), jax.ShapeDtypeStruct((1, d), F32)],
        compiler_params=_cparams(("arbitrary",)),
    )(x, w, dh, dres)


def _sqrelu(x):
    return jnp.square(jnp.maximum(x, 0.0))


def act_fwd(ff, *, name):
    t, f = ff.shape

    def body(x_ref, o_ref):
        o_ref[...] = _sqrelu(x_ref[...]).astype(o_ref.dtype)

    blk = pl.BlockSpec((ROWS, f), lambda i: (i, 0))
    return pl.pallas_call(body, name=name, grid=(t // ROWS,), in_specs=[blk], out_specs=blk,
                          out_shape=jax.ShapeDtypeStruct((t, f), MM), compiler_params=_cparams(("parallel",)))(ff)


def act_bwd(ff, dact, *, name):
    t, f = ff.shape

    def body(x_ref, g_ref, o_ref):
        o_ref[...] = (g_ref[...] * 2.0 * jnp.maximum(x_ref[...], 0.0)).astype(o_ref.dtype)

    blk = pl.BlockSpec((ROWS, f), lambda i: (i, 0))
    return pl.pallas_call(body, name=name, grid=(t // ROWS,), in_specs=[blk, blk], out_specs=blk,
                          out_shape=jax.ShapeDtypeStruct((t, f), MM), compiler_params=_cparams(("parallel",)))(ff, dact)


def loss_fwd(y, target, *, name):
    t, d = y.shape

    def body(y_ref, t_ref, dy_ref, l_ref):
        e = y_ref[...] - t_ref[...]
        dy_ref[...] = e * (1.0 / d)
        part = 0.5 * jnp.sum(jnp.sum(e * e, axis=-1, keepdims=True) * (1.0 / d), axis=0, keepdims=True)
        _acc(l_ref, jnp.broadcast_to(part, (1, HEAD_DIM)), pl.program_id(0) == 0)

    blk = pl.BlockSpec((ROWS, d), lambda i: (i, 0))
    return pl.pallas_call(
        body, name=name, grid=(t // ROWS,), in_specs=[blk, blk],
        out_specs=[blk, pl.BlockSpec((1, HEAD_DIM), lambda i: (0, 0))],
        out_shape=[jax.ShapeDtypeStruct((t, d), F32), jax.ShapeDtypeStruct((1, HEAD_DIM), F32)],
        compiler_params=_cparams(("arbitrary",)),
    )(y, target)


def _mem_kv(mem, wn, wkn, *ws):
    mn = _rms(mem, wn)
    outs = []
    for h in range(MEM_HEADS):
        outs.append(_rms(mmul(mn, ws[h], 1, 0, False), wkn))
    for h in range(MEM_HEADS):
        outs.append(mmul(mn, ws[MEM_HEADS + h], 1, 0, False))
    return tuple(outs)


def _w_cols(w_ref):
    return [w_ref[:, h * HEAD_DIM:(h + 1) * HEAD_DIM] for h in range(2 * MEM_HEADS)]


def mem_fwd(mem, wn, wkv, wkn):
    def body(mem_ref, wn_ref, w_ref, wkn_ref, k_ref, v_ref):
        outs = _mem_kv(mem_ref[...], wn_ref[...], wkn_ref[...], *_w_cols(w_ref))
        for h in range(MEM_HEADS):
            k_ref[:, h * HEAD_DIM:(h + 1) * HEAD_DIM] = outs[h]
            v_ref[:, h * HEAD_DIM:(h + 1) * HEAD_DIM] = outs[MEM_HEADS + h]

    shp = jax.ShapeDtypeStruct((mem.shape[0], MEM_WIDTH), F32)
    return pl.pallas_call(body, name="mem_fwd", out_shape=[shp, shp], compiler_params=_cparams())(mem, wn, wkv, wkn)


def mem_bwd(mem, wn, wkv, wkn, dk0, dv0, dk1, dv1):
    def body(mem_ref, wn_ref, w_ref, wkn_ref, dk0_ref, dv0_ref, dk1_ref, dv1_ref, dwn_ref, dw_ref, dwkn_ref):
        _, vjp = jax.vjp(lambda wn_, wkn_, *ws: _mem_kv(mem_ref[...], wn_, wkn_, *ws),
                         wn_ref[...], wkn_ref[...], *[w.astype(F32) for w in _w_cols(w_ref)])
        cols = lambda a, b: tuple(a[:, h * HEAD_DIM:(h + 1) * HEAD_DIM] + b[:, h * HEAD_DIM:(h + 1) * HEAD_DIM]
                                  for h in range(MEM_HEADS))
        cts = cols(dk0_ref, dk1_ref) + cols(dv0_ref, dv1_ref)
        grads = vjp(cts)
        dwn_ref[...] = grads[0]
        dwkn_ref[...] = grads[1]
        for h in range(2 * MEM_HEADS):
            dw_ref[:, h * HEAD_DIM:(h + 1) * HEAD_DIM] = grads[2 + h]

    return pl.pallas_call(
        body, name="mem_bwd",
        out_shape=[jax.ShapeDtypeStruct((1, D_MODEL), F32), jax.ShapeDtypeStruct((D_MODEL, 2 * MEM_WIDTH), F32),
                   jax.ShapeDtypeStruct((1, HEAD_DIM), F32)],
        compiler_params=_cparams(),
    )(mem, wn, wkv, wkn, dk0, dv0, dk1, dv1)


def _memattn(q, wq, mk, mv):
    qn = _rms(q, wq) * QSCALE
    s = mmul(qn, mk, 1, 1, False)
    s = s - jnp.max(s, axis=-1, keepdims=True)
    p = jnp.exp(s)
    p = p / jnp.sum(p, axis=-1, keepdims=True)
    return mmul(p, mv, 1, 0, False)


def _lanes(j):
    return slice(j * HEAD_DIM, (j + 1) * HEAD_DIM)


def _memattn_specs(t):
    qspec = pl.BlockSpec((ROWS, MEM_WIDTH), lambda i: (i, (TAIL - MEM_WIDTH) // MEM_WIDTH))
    wspec = pl.BlockSpec((1, HEAD_DIM), lambda i: (0, 0))
    mspec = pl.BlockSpec((N_MEM, MEM_WIDTH), lambda i: (0, 0))
    ospec = pl.BlockSpec((ROWS, MEM_WIDTH), lambda i: (i, 0))
    return qspec, wspec, mspec, ospec


def memattn_fwd(proj, wq, mk, mv, *, name):
    t = proj.shape[0]
    qspec, wspec, mspec, ospec = _memattn_specs(t)

    def body(q_ref, w_ref, k_ref, v_ref, o_ref):
        for h in range(MEM_HEADS):
            o_ref[:, _lanes(h)] = _memattn(q_ref[:, _lanes(h)], w_ref[...], k_ref[:, _lanes(h)],
                                           v_ref[:, _lanes(h)]).astype(o_ref.dtype)

    return pl.pallas_call(
        body, name=name, grid=(t // ROWS,), in_specs=[qspec, wspec, mspec, mspec], out_specs=ospec,
        out_shape=jax.ShapeDtypeStruct((t, MEM_WIDTH), MM), compiler_params=_cparams(("parallel",)),
    )(proj, wq, mk, mv)


def memattn_bwd(proj, wq, mk, mv, dcat, *, name):
    t = proj.shape[0]
    qspec, wspec, mspec, ospec = _memattn_specs(t)
    dospec = pl.BlockSpec((ROWS, MEM_WIDTH), lambda i: (i, D_MODEL // MEM_WIDTH))

    def body(q_ref, w_ref, k_ref, v_ref, do_ref, dq_ref, dw_ref, dk_ref, dv_ref):
        first = pl.program_id(0) == 0
        dw_sum = jnp.zeros((1, HEAD_DIM), F32)
        for h in range(MEM_HEADS):
            _, vjp = jax.vjp(_memattn, q_ref[:, _lanes(h)], w_ref[...], k_ref[:, _lanes(h)], v_ref[:, _lanes(h)])
            dq, dw, dk, dv = vjp(do_ref[:, _lanes(h)].astype(F32))
            dq_ref[:, _lanes(h)] = dq.astype(dq_ref.dtype)
            dw_sum = dw_sum + dw
            _acc(dk_ref.at[:, _lanes(h)], dk, first)
            _acc(dv_ref.at[:, _lanes(h)], dv, first)
        _acc(dw_ref, dw_sum, first)

    mshape = jax.ShapeDtypeStruct((N_MEM, MEM_WIDTH), F32)
    return pl.pallas_call(
        body, name=name, grid=(t // ROWS,), in_specs=[qspec, wspec, mspec, mspec, dospec],
        out_specs=[ospec, wspec, mspec, mspec],
        out_shape=[jax.ShapeDtypeStruct((t, MEM_WIDTH), MM), jax.ShapeDtypeStruct((1, HEAD_DIM), F32), mshape, mshape],
        compiler_params=_cparams(("arbitrary",)),
    )(proj, wq, mk, mv, dcat)


def _shift_rows(x, s, up):
    n = x.shape[0]
    r = lax.broadcasted_iota(jnp.int32, x.shape, 0)
    if up:
        return jnp.where(r < n - s, pltpu.roll(x, n - s, 0), 0.0)
    return jnp.where(r >= s, pltpu.roll(x, s, 0), 0.0)


def _conv_fwd_vals(x, w):
    xb = _bf_round(x)
    wb = _bf_round(w)
    c = xb * wb[3:4, :]
    for j in range(3):
        c = c + _shift_rows(xb, 3 - j, False) * wb[j:j + 1, :]
    return xb, wb, c


def dn_prep_fwd(proj, conv_w):
    t = proj.shape[0]

    def body(x_ref, w_ref, o_ref):
        j = pl.program_id(0)
        _, _, c = _conv_fwd_vals(x_ref[...], w_ref[...])
        s = _silu(c)
        r = lax.rsqrt(jnp.sum(s * s, axis=-1, keepdims=True) + EPS)
        scale = jnp.where(j < N_HEADS, QSCALE, 1.0)
        o_ref[...] = jnp.where(j < 2 * N_HEADS, s * r * scale, s)

    return pl.pallas_call(
        body, name="dn_prep_fwd", grid=(3 * N_HEADS,),
        in_specs=[pl.BlockSpec((t, HEAD_DIM), lambda j: (0, j)), pl.BlockSpec((4, HEAD_DIM), lambda j: (0, j))],
        out_specs=pl.BlockSpec((None, t, HEAD_DIM), lambda j: (j // N_HEADS, 0, j % N_HEADS)),
        out_shape=jax.ShapeDtypeStruct((3, t, D_MODEL), F32), compiler_params=_cparams(("parallel",)),
    )(proj, conv_w)


def dn_prep_bwd(proj, conv_w, dqkv):
    t = proj.shape[0]

    def body(x_ref, w_ref, g_ref, dx_ref, dw_ref):
        j = pl.program_id(0)
        xb, wb, c = _conv_fwd_vals(x_ref[...], w_ref[...])
        sg = _sigmoid(c)
        s = c * sg
        g = g_ref[...]
        r = lax.rsqrt(jnp.sum(s * s, axis=-1, keepdims=True) + EPS)
        scale = jnp.where(j < N_HEADS, QSCALE, 1.0)
        gn = g * scale
        ds_norm = r * gn - s * (r * r * r) * jnp.sum(gn * s, axis=-1, keepdims=True)
        ds = jnp.where(j < 2 * N_HEADS, ds_norm, g)
        dc = ds * (sg + s * (1.0 - sg))
        dx = dc * wb[3:4, :]
        rows = [jnp.sum(dc * xb, axis=0, keepdims=True)]
        for jj in range(2, -1, -1):
            sh = 3 - jj
            dx = dx + _shift_rows(dc, sh, True) * wb[jj:jj + 1, :]
            rows.insert(0, jnp.sum(dc * _shift_rows(xb, sh, False), axis=0, keepdims=True))
        dx_ref[...] = dx.astype(dx_ref.dtype)
        dw_ref[...] = jnp.concatenate(rows + [jnp.zeros((4, HEAD_DIM), F32)], axis=0)

    col = pl.BlockSpec((t, HEAD_DIM), lambda j: (0, j))
    return pl.pallas_call(
        body, name="dn_prep_bwd", grid=(3 * N_HEADS,),
        in_specs=[col, pl.BlockSpec((4, HEAD_DIM), lambda j: (0, j)),
                  pl.BlockSpec((None, t, HEAD_DIM), lambda j: (j // N_HEADS, 0, j % N_HEADS))],
        out_specs=[col, pl.BlockSpec((8, HEAD_DIM), lambda j: (0, j))],
        out_shape=[jax.ShapeDtypeStruct((t, 3 * D_MODEL), MM), jax.ShapeDtypeStruct((8, 3 * D_MODEL), F32)],
        compiler_params=_cparams(("parallel",)),
    )(proj, conv_w, dqkv)


def _tri_ones(n, upper):
    r, c = _iota2(n, n)
    return (r <= c).astype(F32) if upper else (r >= c).astype(F32)


def dn_gates_fwd(proj, a_log, dt_bias):
    t = proj.shape[0]

    def body(x_ref, al_ref, dt_ref, o_ref):
        lane = lax.broadcasted_iota(jnp.int32, (CHUNK, HEAD_DIM), 1)
        tri = _tri_ones(CHUNK, False)

        def step(c, carry):
            rows = pl.ds(pl.multiple_of(c * CHUNK, CHUNK), CHUNK)
            x = x_ref[rows, :]
            g = jnp.where(lane < N_HEADS, -jnp.exp(al_ref[...]) * _softplus(x + dt_ref[...]), 0.0)
            gc = _dot(tri, g, 1, 0, True)
            o_ref[rows, :] = jnp.where(lane < N_HEADS, gc, jnp.where(lane < 2 * N_HEADS, _sigmoid(x), 0.0))
            return carry

        lax.fori_loop(0, t // CHUNK, step, 0)

    vec = pl.BlockSpec((1, HEAD_DIM), lambda i: (0, 0))
    return pl.pallas_call(
        body, name="dn_gates_fwd", grid=(1,),
        in_specs=[pl.BlockSpec((t, HEAD_DIM), lambda i: (0, TAIL_BLK)), vec, vec],
        out_specs=pl.BlockSpec((t, HEAD_DIM), lambda i: (0, 0)),
        out_shape=jax.ShapeDtypeStruct((t, HEAD_DIM), F32), compiler_params=_cparams(("arbitrary",)),
    )(proj, a_log, dt_bias)


def dn_gates_bwd(proj, a_log, dt_bias, dgates):
    t = proj.shape[0]

    def body(x_ref, al_ref, dt_ref, g_ref, dx_ref, dal_ref, ddt_ref):
        lane = lax.broadcasted_iota(jnp.int32, (CHUNK, HEAD_DIM), 1)
        tri = _tri_ones(CHUNK, True)
        dal_ref[...] = jnp.zeros_like(dal_ref)
        ddt_ref[...] = jnp.zeros_like(ddt_ref)

        def step(c, carry):
            rows = pl.ds(pl.multiple_of(c * CHUNK, CHUNK), CHUNK)
            x = x_ref[rows, :]
            dgc = jnp.where(lane < N_HEADS, g_ref[rows, :], 0.0)
            dg = _dot(tri, dgc, 1, 0, True)
            ea = -jnp.exp(al_ref[...])
            z = x + dt_ref[...]
            da = jnp.where(lane < N_HEADS, dg * ea * _sigmoid(z), 0.0)
            gval = jnp.where(lane < N_HEADS, ea * _softplus(z), 0.0)
            beta = _sigmoid(x)
            db = jnp.where(jnp.logical_and(lane >= N_HEADS, lane < 2 * N_HEADS), g_ref[rows, :] * beta * (1.0 - beta), 0.0)
            dx_ref[rows, :] = (da + db).astype(dx_ref.dtype)
            dal_ref[...] += jnp.sum(dg * gval, axis=0, keepdims=True)
            ddt_ref[...] += jnp.sum(da, axis=0, keepdims=True)
            return carry

        lax.fori_loop(0, t // CHUNK, step, 0)

    vec = pl.BlockSpec((1, HEAD_DIM), lambda i: (0, 0))
    full = pl.BlockSpec((t, HEAD_DIM), lambda i: (0, 0))
    return pl.pallas_call(
        body, name="dn_gates_bwd", grid=(1,),
        in_specs=[pl.BlockSpec((t, HEAD_DIM), lambda i: (0, TAIL_BLK)), vec, vec, full],
        out_specs=[full, vec, vec],
        out_shape=[jax.ShapeDtypeStruct((t, HEAD_DIM), MM), jax.ShapeDtypeStruct((1, HEAD_DIM), F32),
                   jax.ShapeDtypeStruct((1, HEAD_DIM), F32)],
        compiler_params=_cparams(("arbitrary",)),
    )(proj, a_log, dt_bias, dgates)


def _dn_chunk(q, k, v, gcol, grow, bcol, state):
    r, c = _iota2(CHUNK, CHUNK)
    causal, strict = r >= c, r > c
    decay = jnp.where(causal, jnp.exp(jnp.where(causal, gcol - grow, 0.0)), 0.0)
    kb = k * bcol
    a = jnp.where(strict, mmul(kb, k, 1, 1, False) * decay, 0.0)
    tm = tri_inv(a)
    u = mmul(tm, v * bcol, 1, 0, False)
    w = mmul(tm, kb * jnp.exp(gcol), 1, 0, False)
    qk = jnp.where(causal, mmul(q, k, 1, 1, False) * decay, 0.0)
    v_new = u - mmul(w, state, 1, 0, False)
    out = mmul(q * jnp.exp(gcol), state, 1, 0, False) + mmul(qk, v_new, 1, 0, False)
    rr = lax.broadcasted_iota(jnp.int32, (CHUNK, 1), 0)
    g_last = jnp.sum(jnp.where(rr == CHUNK - 1, gcol, 0.0), axis=0, keepdims=True)
    k_dec = k * jnp.exp(g_last - gcol)
    new_state = state * jnp.exp(g_last) + mmul(k_dec, v_new, 0, 0, False)
    return out, new_state


DN_HEADS_PER_STEP = 2


def _dn_specs(t):
    nc, hb = t // CHUNK, DN_HEADS_PER_STEP
    head = lambda which: pl.BlockSpec((None, t, hb * HEAD_DIM), lambda h: (which, 0, h))
    flat = pl.BlockSpec((t, hb * HEAD_DIM), lambda h: (0, h))
    col = pl.BlockSpec((hb, nc, CHUNK, 1), lambda h: (h, 0, 0, 0))
    row = pl.BlockSpec((hb, nc, 1, CHUNK), lambda h: (h, 0, 0, 0))
    st = pl.BlockSpec((hb, nc, HEAD_DIM, HEAD_DIM), lambda h: (h, 0, 0, 0))
    return nc, hb, head, flat, col, row, st


def dn_core_fwd(qkv, gcol, grow, bcol):
    t = qkv.shape[1]
    nc, hb, head, flat, col, row, st = _dn_specs(t)

    def body(q_ref, k_ref, v_ref, gc_ref, gr_ref, bc_ref, o_ref, s_ref):
        def step(c, states):
            rows = pl.ds(pl.multiple_of(c * CHUNK, CHUNK), CHUNK)
            new_states = []
            for j in range(hb):
                s_ref[j, c] = states[j]
                out, new_state = _dn_chunk(q_ref[rows, _lanes(j)], k_ref[rows, _lanes(j)], v_ref[rows, _lanes(j)],
                                           gc_ref[j, c], gr_ref[j, c], bc_ref[j, c], states[j])
                o_ref[rows, _lanes(j)] = out
                new_states.append(new_state)
            return tuple(new_states)

        lax.fori_loop(0, nc, step, tuple(jnp.zeros((HEAD_DIM, HEAD_DIM), F32) for _ in range(hb)))

    return pl.pallas_call(
        body, name="dn_core_fwd", grid=(N_HEADS // hb,),
        in_specs=[head(0), head(1), head(2), col, row, col], out_specs=[flat, st],
        out_shape=[jax.ShapeDtypeStruct((t, D_MODEL), F32), jax.ShapeDtypeStruct((N_HEADS, nc, HEAD_DIM, HEAD_DIM), F32)],
        compiler_params=_cparams(("parallel",)),
    )(qkv, qkv, qkv, gcol, grow, bcol)


def dn_core_bwd(qkv, gcol, grow, bcol, states, do):
    t = qkv.shape[1]
    nc, hb, head, flat, col, row, st = _dn_specs(t)

    def body(q_ref, k_ref, v_ref, gc_ref, gr_ref, bc_ref, s_ref, do_ref, dqkv_ref, dgc_ref, dgr_ref, dbc_ref):
        def step(i, dstates):
            c = nc - 1 - i
            rows = pl.ds(pl.multiple_of(c * CHUNK, CHUNK), CHUNK)
            dstates_in = []
            for j in range(hb):
                _, vjp = jax.vjp(_dn_chunk, q_ref[rows, _lanes(j)], k_ref[rows, _lanes(j)], v_ref[rows, _lanes(j)],
                                 gc_ref[j, c], gr_ref[j, c], bc_ref[j, c], s_ref[j, c])
                dq, dk, dv, dgc, dgr, dbc, dstate_in = vjp((do_ref[rows, _lanes(j)], dstates[j]))
                dqkv_ref[0, rows, _lanes(j)] = dq
                dqkv_ref[1, rows, _lanes(j)] = dk
                dqkv_ref[2, rows, _lanes(j)] = dv
                dgc_ref[j, c] = dgc
                dgr_ref[j, c] = dgr
                dbc_ref[j, c] = dbc
                dstates_in.append(dstate_in)
            return tuple(dstates_in)

        lax.fori_loop(0, nc, step, tuple(jnp.zeros((HEAD_DIM, HEAD_DIM), F32) for _ in range(hb)))

    return pl.pallas_call(
        body, name="dn_core_bwd", grid=(N_HEADS // hb,),
        in_specs=[head(0), head(1), head(2), col, row, col, st, flat],
        out_specs=[pl.BlockSpec((3, t, hb * HEAD_DIM), lambda h: (0, 0, h)), col, row, col],
        out_shape=[jax.ShapeDtypeStruct((3, t, D_MODEL), F32)] + [
            jax.ShapeDtypeStruct((N_HEADS, nc, CHUNK, 1), F32), jax.ShapeDtypeStruct((N_HEADS, nc, 1, CHUNK), F32),
            jax.ShapeDtypeStruct((N_HEADS, nc, CHUNK, 1), F32)],
        compiler_params=_cparams(("parallel",)),
    )(qkv, qkv, qkv, gcol, grow, bcol, states, do)


def gates_to_heads(gates):
    t = gates.shape[0]
    nc = t // CHUNK
    g = gates[:, :N_HEADS].T.reshape(N_HEADS, nc, CHUNK)
    b = gates[:, N_HEADS:2 * N_HEADS].T.reshape(N_HEADS, nc, CHUNK)
    return g[..., None], g[:, :, None, :], b[..., None]


def heads_to_gates(dgcol, dgrow, dbcol):
    nh, nc = dgcol.shape[:2]
    dg = (dgcol[..., 0] + dgrow[:, :, 0, :]).reshape(nh, nc * CHUNK).T
    db = dbcol[..., 0].reshape(nh, nc * CHUNK).T
    return jnp.concatenate([dg, db, jnp.zeros((nc * CHUNK, HEAD_DIM - 2 * nh), F32)], axis=1)


def _dn_out(o, z, w):
    return _rms(o, w) * _silu(z)


def _gate_specs():
    o_spec = pl.BlockSpec((ROWS, D_MODEL), lambda i: (i, 0))
    z_spec = pl.BlockSpec((ROWS, D_MODEL), lambda i: (i, 3))
    w_spec = pl.BlockSpec((1, HEAD_DIM), lambda i: (0, 0))
    return o_spec, z_spec, w_spec


def dn_out_fwd(o, proj, w):
    t = o.shape[0]
    o_spec, z_spec, w_spec = _gate_specs()

    def body(o_ref, z_ref, w_ref, y_ref):
        for h in range(N_HEADS):
            y_ref[:, _lanes(h)] = _dn_out(o_ref[:, _lanes(h)], z_ref[:, _lanes(h)], w_ref[...]).astype(y_ref.dtype)

    return pl.pallas_call(
        body, name="dn_out_fwd", grid=(t // ROWS,), in_specs=[o_spec, z_spec, w_spec], out_specs=o_spec,
        out_shape=jax.ShapeDtypeStruct((t, D_MODEL), MM), compiler_params=_cparams(("parallel",)),
    )(o, proj, w)


def dn_out_bwd(o, proj, w, dcat):
    t = o.shape[0]
    o_spec, z_spec, w_spec = _gate_specs()

    def body(o_ref, z_ref, w_ref, g_ref, do_ref, dz_ref, dw_ref):
        dw_sum = jnp.zeros((1, HEAD_DIM), F32)
        for h in range(N_HEADS):
            _, vjp = jax.vjp(_dn_out, o_ref[:, _lanes(h)], z_ref[:, _lanes(h)], w_ref[...])
            do, dz, dw = vjp(g_ref[:, _lanes(h)].astype(F32))
            do_ref[:, _lanes(h)] = do
            dz_ref[:, _lanes(h)] = dz.astype(dz_ref.dtype)
            dw_sum = dw_sum + dw
        _acc(dw_ref, dw_sum, pl.program_id(0) == 0)

    return pl.pallas_call(
        body, name="dn_out_bwd", grid=(t // ROWS,), in_specs=[o_spec, z_spec, w_spec, o_spec],
        out_specs=[o_spec, o_spec, w_spec],
        out_shape=[jax.ShapeDtypeStruct((t, D_MODEL), F32), jax.ShapeDtypeStruct((t, D_MODEL), MM),
                   jax.ShapeDtypeStruct((1, HEAD_DIM), F32)],
        compiler_params=_cparams(("arbitrary",)),
    )(o, proj, w, dcat)


def _fox_norm(x, w, scale):
    return _rms(x, w) * scale


def _fox_prep_specs():
    x_spec = pl.BlockSpec((ROWS, 2 * D_MODEL), lambda i: (i, 0))
    w_spec = pl.BlockSpec((2, 1, HEAD_DIM), lambda i: (0, 0, 0))
    y_spec = pl.BlockSpec((2, ROWS, D_MODEL), lambda i: (0, i, 0))
    return x_spec, w_spec, y_spec


def fox_prep_fwd(proj, wqk):
    t = proj.shape[0]
    x_spec, w_spec, y_spec = _fox_prep_specs()

    def body(x_ref, w_ref, y_ref):
        for j in range(2 * N_HEADS):
            which, scale = j // N_HEADS, (QSCALE if j < N_HEADS else 1.0)
            y_ref[which, :, _lanes(j % N_HEADS)] = _fox_norm(x_ref[:, _lanes(j)], w_ref[which], scale).astype(y_ref.dtype)

    return pl.pallas_call(
        body, name="fox_prep_fwd", grid=(t // ROWS,), in_specs=[x_spec, w_spec], out_specs=y_spec,
        out_shape=jax.ShapeDtypeStruct((2, t, D_MODEL), MM), compiler_params=_cparams(("parallel",)),
    )(proj, wqk)


def fox_prep_bwd(proj, wqk, dq, dk):
    t = proj.shape[0]
    x_spec, w_spec, _ = _fox_prep_specs()
    g_spec = pl.BlockSpec((ROWS, D_MODEL), lambda i: (i, 0))

    def body(x_ref, w_ref, dq_ref, dk_ref, dx_ref, dw_ref):
        dws = [jnp.zeros((1, HEAD_DIM), F32), jnp.zeros((1, HEAD_DIM), F32)]
        for j in range(2 * N_HEADS):
            which, scale = j // N_HEADS, (QSCALE if j < N_HEADS else 1.0)
            g_ref = dq_ref if which == 0 else dk_ref
            _, vjp = jax.vjp(lambda x, w: _fox_norm(x, w, scale), x_ref[:, _lanes(j)], w_ref[which])
            dx, dw = vjp(g_ref[:, _lanes(j % N_HEADS)])
            dx_ref[:, _lanes(j)] = dx.astype(dx_ref.dtype)
            dws[which] = dws[which] + dw
        first = pl.program_id(0) == 0
        _acc(dw_ref.at[0], dws[0], first)
        _acc(dw_ref.at[1], dws[1], first)

    return pl.pallas_call(
        body, name="fox_prep_bwd", grid=(t // ROWS,), in_specs=[x_spec, w_spec, g_spec, g_spec],
        out_specs=[x_spec, w_spec],
        out_shape=[jax.ShapeDtypeStruct((t, 2 * D_MODEL), MM), jax.ShapeDtypeStruct((2, 1, HEAD_DIM), F32)],
        compiler_params=_cparams(("arbitrary",)),
    )(proj, wqk, dq, dk)


def _row_pick(x, i):
    r = lax.broadcasted_iota(jnp.int32, x.shape, 0)
    return jnp.sum(jnp.where(r == i, x, 0.0), axis=0, keepdims=True)


def fox_gates_fwd(proj, f_bias):
    t = proj.shape[0]
    blk = HEAD_DIM

    def body(x_ref, b_ref, o_ref):
        lane = lax.broadcasted_iota(jnp.int32, (blk, HEAD_DIM), 1)
        tri = _tri_ones(blk, False)

        def step(c, carry):
            rows = pl.ds(pl.multiple_of(c * blk, blk), blk)
            lf = jnp.where(lane < N_HEADS, -_softplus(-(x_ref[rows, :] + b_ref[...])), 0.0)
            cum = _dot(tri, lf, 1, 0, True) + carry
            o_ref[rows, :] = cum
            return _row_pick(cum, blk - 1)

        lax.fori_loop(0, t // blk, step, jnp.zeros((1, HEAD_DIM), F32))

    vec = pl.BlockSpec((1, HEAD_DIM), lambda i: (0, 0))
    return pl.pallas_call(
        body, name="fox_gates_fwd", grid=(1,),
        in_specs=[pl.BlockSpec((t, HEAD_DIM), lambda i: (0, TAIL_BLK)), vec],
        out_specs=pl.BlockSpec((t, HEAD_DIM), lambda i: (0, 0)),
        out_shape=jax.ShapeDtypeStruct((t, HEAD_DIM), F32), compiler_params=_cparams(("arbitrary",)),
    )(proj, f_bias)


def fox_gates_bwd(proj, f_bias, dfcum):
    t = proj.shape[0]
    blk = HEAD_DIM
    nb = t // blk

    def body(x_ref, b_ref, g_ref, dx_ref, db_ref):
        lane = lax.broadcasted_iota(jnp.int32, (blk, HEAD_DIM), 1)
        tri = _tri_ones(blk, True)
        db_ref[...] = jnp.zeros_like(db_ref)

        def step(i, carry):
            c = nb - 1 - i
            rows = pl.ds(pl.multiple_of(c * blk, blk), blk)
            g = jnp.where(lane < N_HEADS, g_ref[rows, :], 0.0)
            dlf = _dot(tri, g, 1, 0, True) + carry
            dx = jnp.where(lane < N_HEADS, dlf * _sigmoid(-(x_ref[rows, :] + b_ref[...])), 0.0)
            dx_ref[rows, :] = dx.astype(dx_ref.dtype)
            db_ref[...] += jnp.sum(dx, axis=0, keepdims=True)
            return carry + jnp.sum(g, axis=0, keepdims=True)

        lax.fori_loop(0, nb, step, jnp.zeros((1, HEAD_DIM), F32))

    vec = pl.BlockSpec((1, HEAD_DIM), lambda i: (0, 0))
    full = pl.BlockSpec((t, HEAD_DIM), lambda i: (0, 0))
    return pl.pallas_call(
        body, name="fox_gates_bwd", grid=(1,),
        in_specs=[pl.BlockSpec((t, HEAD_DIM), lambda i: (0, TAIL_BLK)), vec, full], out_specs=[full, vec],
        out_shape=[jax.ShapeDtypeStruct((t, HEAD_DIM), MM), jax.ShapeDtypeStruct((1, HEAD_DIM), F32)],
        compiler_params=_cparams(("arbitrary",)),
    )(proj, f_bias, dfcum)


def fcum_to_heads(fcum):
    f = fcum[:, :N_HEADS].T
    return f[:, :, None], f[:, None, :]


def heads_to_fcum(dfcol, dfrow):
    d = (dfcol[:, :, 0] + dfrow[:, 0, :]).T
    return jnp.concatenate([d, jnp.zeros((d.shape[0], HEAD_DIM - N_HEADS), F32)], axis=1)


def _fox_tq(t):
    return min(t, 256)


def _fox_specs(t):
    tq = _fox_tq(t)
    q_spec = pl.BlockSpec((None, tq, HEAD_DIM), lambda h, i: (0, i, h))
    k_spec = pl.BlockSpec((None, t, HEAD_DIM), lambda h, i: (1, 0, h))
    v_spec = pl.BlockSpec((t, HEAD_DIM), lambda h, i: (0, 2 * N_HEADS + h))
    gate_spec = pl.BlockSpec((tq, HEAD_DIM), lambda h, i: (i, 3 * N_HEADS + h))
    col_spec = pl.BlockSpec((None, tq, 1), lambda h, i: (h, i, 0))
    row_spec = pl.BlockSpec((None, 1, t), lambda h, i: (h, 0, 0))
    blk_spec = pl.BlockSpec((tq, HEAD_DIM), lambda h, i: (i, h))
    head_spec = pl.BlockSpec((t, HEAD_DIM), lambda h, i: (0, h))
    return tq, q_spec, k_spec, v_spec, gate_spec, col_spec, row_spec, blk_spec, head_spec


def _fox_scores(q, k, fcol, frow, i, tq, t):
    s = _dot(q, k, 1, 1, False) + (fcol - frow)
    r = lax.broadcasted_iota(jnp.int32, (tq, t), 0) + i * tq
    c = lax.broadcasted_iota(jnp.int32, (tq, t), 1)
    return s, c <= r


def fox_attn_fwd(qk, proj, fcol, frow):
    t = proj.shape[0]
    tq, q_spec, k_spec, v_spec, gate_spec, col_spec, row_spec, blk_spec, _ = _fox_specs(t)

    def body(q_ref, k_ref, v_ref, gate_ref, fc_ref, fr_ref, mix_ref, o_ref, lse_ref):
        s, mask = _fox_scores(q_ref[...], k_ref[...], fc_ref[...], fr_ref[...], pl.program_id(1), tq, t)
        s = jnp.where(mask, s, -1e30)
        m = jnp.max(s, axis=-1, keepdims=True)
        p = jnp.where(mask, jnp.exp(s - m), 0.0)
        l = jnp.sum(p, axis=-1, keepdims=True)
        o = _dot(p, v_ref[...], 1, 0, False) / l
        o_ref[...] = o
        mix_ref[...] = (o * _sigmoid(gate_ref[...])).astype(mix_ref.dtype)
        lse_ref[...] = m + jnp.log(l)

    return pl.pallas_call(
        body, name="fox_attn_fwd", grid=(N_HEADS, t // tq),
        in_specs=[q_spec, k_spec, v_spec, gate_spec, col_spec, row_spec], out_specs=[blk_spec, blk_spec, col_spec],
        out_shape=[jax.ShapeDtypeStruct((t, D_MODEL), MM), jax.ShapeDtypeStruct((t, D_MODEL), F32),
                   jax.ShapeDtypeStruct((N_HEADS, t, 1), F32)],
        compiler_params=_cparams(("parallel", "parallel")),
    )(qk, qk, proj, proj, fcol, frow)


def fox_attn_bwd(qk, proj, fcol, frow, o, lse, dcat):
    t = proj.shape[0]
    tq, q_spec, k_spec, v_spec, gate_spec, col_spec, row_spec, blk_spec, head_spec = _fox_specs(t)

    def body(q_ref, k_ref, v_ref, gate_ref, fc_ref, fr_ref, o_ref, lse_ref, g_ref,
             dq_ref, dk_ref, dv_ref, dgate_ref, dfc_ref, dfr_ref):
        i = pl.program_id(1)
        sg = _sigmoid(gate_ref[...])
        g = g_ref[...].astype(F32)
        o_pre = o_ref[...]
        do = g * sg
        dgate_ref[...] = (g * o_pre * sg * (1.0 - sg)).astype(dgate_ref.dtype)
        s, mask = _fox_scores(q_ref[...], k_ref[...], fc_ref[...], fr_ref[...], i, tq, t)
        p = jnp.where(mask, jnp.exp(jnp.where(mask, s, 0.0) - lse_ref[...]), 0.0)
        dp = _dot(do, v_ref[...], 1, 1, False)
        delta = jnp.sum(do * o_pre, axis=-1, keepdims=True)
        ds = p * (dp - delta)
        dq_ref[...] = _dot(ds, k_ref[...], 1, 0, False)
        _acc(dk_ref, _dot(ds, q_ref[...], 0, 0, False), i == 0)
        _acc(dv_ref, _dot(p, do, 0, 0, False), i == 0)
        dfc_ref[...] = jnp.sum(ds, axis=-1, keepdims=True)
        _acc(dfr_ref, -jnp.sum(ds, axis=0, keepdims=True), i == 0)

    f32 = lambda *s: jax.ShapeDtypeStruct(s, F32)
    return pl.pallas_call(
        body, name="fox_attn_bwd", grid=(N_HEADS, t // tq),
        in_specs=[q_spec, k_spec, v_spec, gate_spec, col_spec, row_spec, blk_spec, col_spec, blk_spec],
        out_specs=[blk_spec, head_spec, head_spec, blk_spec, col_spec, row_spec],
        out_shape=[f32(t, D_MODEL), f32(t, D_MODEL), f32(t, D_MODEL), jax.ShapeDtypeStruct((t, D_MODEL), MM),
                   f32(N_HEADS, t, 1), f32(N_HEADS, 1, t)],
        compiler_params=_cparams(("parallel", "arbitrary")),
    )(qk, qk, proj, proj, fcol, frow, o, lse, dcat)


def adamw(w, g, m, v, *, name):
    r, c = w.shape
    rb = ROWS if r % ROWS == 0 else r

    def body(w_ref, g_ref, m_ref, v_ref, d_ref, nm_ref, nv_ref):
        g_ = g_ref[...]
        m_ = ADAM_B1 * m_ref[...] + (1.0 - ADAM_B1) * g_
        v_ = ADAM_B2 * v_ref[...] + (1.0 - ADAM_B2) * jnp.square(g_)
        m_hat = m_ / (1.0 - ADAM_B1 ** ADAM_STEP)
        v_hat = v_ / (1.0 - ADAM_B2 ** ADAM_STEP)
        d_ref[...] = -ADAM_LR * (m_hat / (jnp.sqrt(v_hat) + ADAM_EPS) + ADAM_WD * w_ref[...])
        nm_ref[...] = m_
        nv_ref[...] = v_

    blk = pl.BlockSpec((rb, c), lambda i: (i, 0))
    shp = jax.ShapeDtypeStruct((r, c), F32)
    return pl.pallas_call(body, name=name, grid=(r // rb,), in_specs=[blk] * 4, out_specs=[blk] * 3,
                          out_shape=[shp] * 3, compiler_params=_cparams(("parallel",)))(w, g, m, v)


def _place():
    x, y, c = lax.axis_index("x"), lax.axis_index("y"), lax.axis_index("c")
    return x, y, c, [(1 - x, y), (x, 1 - y), (1 - x, 1 - y)]


ANY = pl.BlockSpec(memory_space=pl.ANY)


def all_gather_shards(shard):
    r, w = shard.shape
    rh = r // 2

    def body(x_ref, o_ref, send_sems, recv_sems, fsend_sems, frecv_sems):
        x, y, c, chips = _place()
        me_chip = 2 * x + y
        sibling = (x, y, 1 - c)
        mine_rows = pl.ds(pl.multiple_of(c * rh, 16), rh)
        other_rows = pl.ds(pl.multiple_of((1 - c) * rh, 16), rh)

        def slot(chip):
            return 2 * chip[0] + chip[1]

        sends = []
        for j, chip in enumerate(chips):
            cp = pltpu.make_async_remote_copy(
                src_ref=x_ref.at[mine_rows], dst_ref=o_ref.at[me_chip, mine_rows], send_sem=send_sems.at[j],
                recv_sem=recv_sems.at[j], device_id=(chip[0], chip[1], c), device_id_type=MESH)
            cp.start()
            sends.append(cp)
        for j, chip in enumerate(chips):
            landed = o_ref.at[slot(chip), mine_rows]
            pltpu.make_async_remote_copy(src_ref=landed, dst_ref=landed, send_sem=send_sems.at[j], recv_sem=recv_sems.at[j],
                                         device_id=(chip[0], chip[1], c), device_id_type=MESH).wait_recv()
            cp = pltpu.make_async_remote_copy(src_ref=landed, dst_ref=landed, send_sem=fsend_sems.at[j],
                                              recv_sem=frecv_sems.at[j], device_id=sibling, device_id_type=MESH)
            cp.start()
            sends.append(cp)
        for j, chip in enumerate(chips):
            passed = o_ref.at[slot(chip), other_rows]
            pltpu.make_async_remote_copy(src_ref=passed, dst_ref=passed, send_sem=fsend_sems.at[j], recv_sem=frecv_sems.at[j],
                                         device_id=sibling, device_id_type=MESH).wait_recv()
        for cp in sends:
            cp.wait_send()

    return pl.pallas_call(
        body, name="all_gather_shards", in_specs=[ANY], out_specs=ANY,
        out_shape=jax.ShapeDtypeStruct((N_CHIP, r, w), shard.dtype),
        scratch_shapes=[pltpu.SemaphoreType.DMA((3,))] * 4,
    )(shard)


def rs_pair_exchange(g):
    n, r, w = g.shape
    rh = r // 2

    def body(g_ref, o_ref, send_sem, recv_sem):
        x, y, c, _ = _place()
        other_rows = pl.ds(pl.multiple_of((1 - c) * rh, 8), rh)
        cp = pltpu.make_async_remote_copy(src_ref=g_ref.at[:, other_rows], dst_ref=o_ref, send_sem=send_sem,
                                          recv_sem=recv_sem, device_id=(x, y, 1 - c), device_id_type=MESH)
        cp.start()
        cp.wait()

    return pl.pallas_call(
        body, name="rs_pair_exchange", in_specs=[ANY], out_specs=ANY,
        out_shape=jax.ShapeDtypeStruct((n, rh, w), g.dtype),
        scratch_shapes=[pltpu.SemaphoreType.DMA, pltpu.SemaphoreType.DMA],
    )(g)


def rs_pair_add(place, g, got):
    n, r, w = g.shape
    rh = r // 2
    rb = ROWS
    assert rh % rb == 0, (rh, rb)
    nb = rh // rb

    def body(place_ref, g_ref, got_ref, o_ref):
        o_ref[...] = (g_ref[...] + got_ref[...]).astype(o_ref.dtype)

    return pl.pallas_call(
        body, name="rs_pair_add",
        grid_spec=pltpu.PrefetchScalarGridSpec(
            num_scalar_prefetch=1, grid=(n, nb),
            in_specs=[pl.BlockSpec((None, rb, w), lambda j, i, p: (j, p[0] * nb + i, 0)),
                      pl.BlockSpec((None, rb, w), lambda j, i, p: (j, i, 0))],
            out_specs=pl.BlockSpec((None, rb, w), lambda j, i, p: (j, i, 0))),
        out_shape=jax.ShapeDtypeStruct((n, rh, w), MM), compiler_params=_cparams(("parallel", "parallel")),
    )(place, g, got)


def rs_chip_exchange(p):
    n, rh, w = p.shape

    def body(p_ref, o_ref, send_sems, recv_sems):
        x, y, c, chips = _place()
        cps = []
        for j, chip in enumerate(chips):
            cp = pltpu.make_async_remote_copy(
                src_ref=p_ref.at[2 * chip[0] + chip[1]], dst_ref=o_ref.at[j], send_sem=send_sems.at[j],
                recv_sem=recv_sems.at[j], device_id=(chip[0], chip[1], c), device_id_type=MESH)
            cp.start()
            cps.append(cp)
        for cp in cps:
            cp.wait()

    return pl.pallas_call(
        body, name="rs_chip_exchange", in_specs=[ANY], out_specs=ANY,
        out_shape=jax.ShapeDtypeStruct((3, rh, w), p.dtype),
        scratch_shapes=[pltpu.SemaphoreType.DMA((3,)), pltpu.SemaphoreType.DMA((3,))],
    )(p)


def rs_chip_add(place, g, got_pair, got_chips):
    n, r, w = g.shape
    rh = r // 2
    rb = ROWS
    nb = rh // rb

    def body(place_ref, g_ref, s_ref, a_ref, b_ref, c_ref, o_ref):
        own = g_ref[...] + s_ref[...]
        o_ref[...] = ((own + a_ref[...].astype(F32)) + b_ref[...].astype(F32)) + c_ref[...].astype(F32)

    got_spec = lambda k: pl.BlockSpec((None, rb, w), lambda i, pr: (k, i, 0))
    return pl.pallas_call(
        body, name="rs_chip_add",
        grid_spec=pltpu.PrefetchScalarGridSpec(
            num_scalar_prefetch=1, grid=(nb,),
            in_specs=[pl.BlockSpec((None, rb, w), lambda i, pr: (pr[1], pr[0] * nb + i, 0)),
                      pl.BlockSpec((None, rb, w), lambda i, pr: (pr[1], i, 0)), got_spec(0), got_spec(1), got_spec(2)],
            out_specs=pl.BlockSpec((rb, w), lambda i, pr: (pr[0] * nb + i, 0))),
        out_shape=jax.ShapeDtypeStruct((r, w), F32), compiler_params=_cparams(("parallel",)),
    )(place, g, got_pair, got_chips, got_chips, got_chips)


def rs_pair_gather(full):
    r, w = full.shape
    rh = r // 2

    def body(_, o_ref, send_sem, recv_sem):
        x, y, c, _ = _place()
        mine = o_ref.at[pl.ds(pl.multiple_of(c * rh, 8), rh)]
        theirs = o_ref.at[pl.ds(pl.multiple_of((1 - c) * rh, 8), rh)]
        cp = pltpu.make_async_remote_copy(src_ref=mine, dst_ref=mine, send_sem=send_sem, recv_sem=recv_sem,
                                          device_id=(x, y, 1 - c), device_id_type=MESH)
        cp.start()
        cp.wait_send()
        pltpu.make_async_remote_copy(src_ref=theirs, dst_ref=theirs, send_sem=send_sem, recv_sem=recv_sem,
                                     device_id=(x, y, 1 - c), device_id_type=MESH).wait_recv()

    return pl.pallas_call(
        body, name="rs_pair_gather", in_specs=[ANY], out_specs=ANY, input_output_aliases={0: 0},
        out_shape=jax.ShapeDtypeStruct((r, w), full.dtype),
        scratch_shapes=[pltpu.SemaphoreType.DMA, pltpu.SemaphoreType.DMA],
    )(full)


def all_reduce_small(v):
    r, w = v.shape

    def body(v_ref, o_ref, buf, send_sems, recv_sems):
        x, y, c, _ = _place()
        me = 4 * x + 2 * y + c
        flip = lambda a, bit: 1 - a if bit else a
        cps = []
        for k in range(1, N_DEV):
            peer = (flip(x, k & 4), flip(y, k & 2), flip(c, k & 1))
            cp = pltpu.make_async_remote_copy(src_ref=v_ref, dst_ref=buf.at[me], send_sem=send_sems.at[k - 1],
                                              recv_sem=recv_sems.at[k - 1], device_id=peer, device_id_type=MESH)
            cp.start()
            cps.append((cp, 4 * peer[0] + 2 * peer[1] + peer[2]))
        buf[me] = v_ref[...]
        for k, (cp, peer_id) in enumerate(cps):
            pltpu.make_async_remote_copy(src_ref=v_ref, dst_ref=buf.at[peer_id], send_sem=send_sems.at[k],
                                         recv_sem=recv_sems.at[k], device_id=(x, y, c), device_id_type=MESH).wait_recv()
        for cp, _ in cps:
            cp.wait_send()
        acc = buf[0]
        for d in range(1, N_DEV):
            acc = acc + buf[d]
        o_ref[...] = acc

    vm = pl.BlockSpec(memory_space=pltpu.VMEM)
    return pl.pallas_call(
        body, name="all_reduce_small", in_specs=[vm], out_specs=vm, out_shape=jax.ShapeDtypeStruct((r, w), F32),
        scratch_shapes=[pltpu.VMEM((N_DEV, r, w), F32), pltpu.SemaphoreType.DMA((N_DEV - 1,)),
                        pltpu.SemaphoreType.DMA((N_DEV - 1,))],
    )(v)


def _vec8(v):
    return jnp.zeros((1, HEAD_DIM), F32).at[0, :N_HEADS].set(v.reshape(N_HEADS))


def _layer_fwd(i, x_in, wt, sm, mem_k, mem_v):
    tag = f"l{i}_"
    h = rms_fwd(x_in, sm["norm1_w"][i][None], name=tag + "rms1")
    w_in = wt["dn_w_in"] if i == 0 else wt["fox_w_in"]
    proj = matmul(h, w_in, name=tag + "proj", tm=256, tk=1024)
    sv = dict(x_in=x_in, h=h, proj=proj)
    if i == 0:
        qkv = dn_prep_fwd(proj, wt["conv_w"])
        gates = dn_gates_fwd(proj, _vec8(sm["dn_a_log"]), _vec8(sm["dn_dt_bias"]))
        gcol, grow, bcol = gates_to_heads(gates)
        o, states = dn_core_fwd(qkv, gcol, grow, bcol)
        mix = dn_out_fwd(o, proj, sm["dn_o_norm_w"])
        sv.update(qkv=qkv, gcol=gcol, grow=grow, bcol=bcol, states=states, o=o)
    else:
        wqk = jnp.stack([sm["fox_q_norm_w"], sm["fox_k_norm_w"]])
        qk = fox_prep_fwd(proj, wqk)
        fcum = fox_gates_fwd(proj, _vec8(sm["fox_f_bias"]))
        fcol, frow = fcum_to_heads(fcum)
        mix, o, lse = fox_attn_fwd(qk, proj, fcol, frow)
        sv.update(wqk=wqk, qk=qk, fcol=fcol, frow=frow, o=o, lse=lse)
    mem_out = memattn_fwd(proj, sm["memq_norm_w"][i][None], mem_k, mem_v, name=tag + "memattn_fwd")
    cat = jnp.concatenate([mix, mem_out], axis=1)
    x_mid = matmul(cat, wt["w_out"][i], res=x_in, name=tag + "out_proj")
    h2 = rms_fwd(x_mid, sm["norm2_w"][i][None], name=tag + "rms2")
    ff = matmul(h2, wt["w_mlp1"][i], name=tag + "mlp1")
    act = act_fwd(ff, name=tag + "act_fwd")
    x_out = matmul(act, wt["w_mlp2"][i], res=x_mid, name=tag + "mlp2")
    sv.update(cat=cat, x_mid=x_mid, h2=h2, ff=ff, act=act)
    return x_out, sv


def _layer_bwd(i, dx_out, sv, wt, sm, mem_k, mem_v):
    tag = f"l{i}_"
    big, small = {}, {}
    dact = matmul(dx_out, wt["w_mlp2"][i], tb=True, name=tag + "d_act")
    big["w_mlp2"] = matmul(sv["act"], dx_out, ta=True, name=tag + "d_w_mlp2")
    dff = act_bwd(sv["ff"], dact, name=tag + "act_bwd")
    dh2 = matmul(dff, wt["w_mlp1"][i], tb=True, name=tag + "d_h2")
    big["w_mlp1"] = matmul(sv["h2"], dff, ta=True, name=tag + "d_w_mlp1", tm=512, tn=D_FF, tk=512)
    dx_mid, small["norm2_w"] = rms_bwd(sv["x_mid"], sm["norm2_w"][i][None], dh2, dx_out, name=tag + "rms2_bwd")
    dcat = matmul(dx_mid, wt["w_out"][i], tb=True, name=tag + "d_cat")
    big["w_out"] = matmul(sv["cat"], dx_mid, ta=True, name=tag + "d_w_out")
    proj = sv["proj"]
    dqm, small["memq_norm_w"], dmk, dmv = memattn_bwd(proj, sm["memq_norm_w"][i][None], mem_k, mem_v, dcat,
                                                      name=tag + "memattn_bwd")
    t = proj.shape[0]
    pad = jnp.zeros((t, PROJ_W - TAIL - HEAD_DIM), MM)
    if i == 0:
        do, dz, small["dn_o_norm_w"] = dn_out_bwd(sv["o"], proj, sm["dn_o_norm_w"], dcat)
        dqkv, dgc, dgr, dbc = dn_core_bwd(sv["qkv"], sv["gcol"], sv["grow"], sv["bcol"], sv["states"], do)
        dtail, dal, ddt = dn_gates_bwd(proj, _vec8(sm["dn_a_log"]), _vec8(sm["dn_dt_bias"]), heads_to_gates(dgc, dgr, dbc))
        dmain, dconv = dn_prep_bwd(proj, wt["conv_w"], dqkv)
        small["dn_a_log"], small["dn_dt_bias"] = dal[:, :N_HEADS], ddt[:, :N_HEADS]
        big["conv_w"] = dconv[:4]
        dproj = jnp.concatenate([dmain, dz, dqm, dtail, pad], axis=1)
    else:
        dq, dk, dv, dgate, dfc, dfr = fox_attn_bwd(sv["qk"], proj, sv["fcol"], sv["frow"], sv["o"], sv["lse"], dcat)
        dtail, dfb = fox_gates_bwd(proj, _vec8(sm["fox_f_bias"]), heads_to_fcum(dfc, dfr))
        dqk, dwqk = fox_prep_bwd(proj, sv["wqk"], dq, dk)
        small["fox_f_bias"] = dfb[:, :N_HEADS]
        small["fox_q_norm_w"], small["fox_k_norm_w"] = dwqk[0], dwqk[1]
        dproj = jnp.concatenate([dqk, dv.astype(MM), dgate, dqm, dtail, pad], axis=1)
    w_in = wt["dn_w_in"] if i == 0 else wt["fox_w_in"]
    dh = matmul(dproj, w_in, tb=True, name=tag + "d_h", tm=512)
    big["w_in"] = matmul(sv["h"], dproj, ta=True, name=tag + "d_w_in", tm=256)
    dx_in, small["norm1_w"] = rms_bwd(sv["x_in"], sm["norm1_w"][i][None], dh, dx_mid, name=tag + "rms1_bwd")
    return dx_in, big, small, (dmk, dmv)


def local_step(x, mem, target, wt, sm):
    mem_k, mem_v = mem_fwd(mem, sm["mem_norm_w"][None], wt["w_mem_kv"], sm["mem_k_norm_w"][None])
    x0, sv0 = _layer_fwd(0, x, wt, sm, mem_k, mem_v)
    x1, sv1 = _layer_fwd(1, x0, wt, sm, mem_k, mem_v)
    dy, loss = loss_fwd(x1, target, name="loss")
    dx1, big1, small1, dm1 = _layer_bwd(1, dy, sv1, wt, sm, mem_k, mem_v)
    dx0, big0, small0, dm0 = _layer_bwd(0, dx1, sv0, wt, sm, mem_k, mem_v)
    dwn, dwkv, dwkn = mem_bwd(mem, sm["mem_norm_w"][None], wt["w_mem_kv"], sm["mem_k_norm_w"][None], *dm0, *dm1)
    small = dict(mem_norm_w=dwn[0], mem_k_norm_w=dwkn[0],
                 norm1_w=jnp.concatenate([small0["norm1_w"], small1["norm1_w"]]),
                 norm2_w=jnp.concatenate([small0["norm2_w"], small1["norm2_w"]]),
                 memq_norm_w=jnp.concatenate([small0["memq_norm_w"], small1["memq_norm_w"]]),
                 dn_a_log=small0["dn_a_log"], dn_dt_bias=small0["dn_dt_bias"], dn_o_norm_w=small0["dn_o_norm_w"],
                 fox_f_bias=small1["fox_f_bias"], fox_q_norm_w=small1["fox_q_norm_w"], fox_k_norm_w=small1["fox_k_norm_w"])
    big = dict(w_mem_kv=dwkv, dn_w_in=big0["w_in"], fox_w_in=big1["w_in"], conv_w=big0["conv_w"],
               w_out=[big0["w_out"], big1["w_out"]], w_mlp1=[big0["w_mlp1"], big1["w_mlp1"]],
               w_mlp2=[big0["w_mlp2"], big1["w_mlp2"]])
    return loss, dx0, big, small


def w_in_to_kernel(w, n_scalars):
    pad = jnp.zeros((w.shape[0], PROJ_W - TAIL - n_scalars), w.dtype)
    return jnp.concatenate([w[:, :4096], w[:, 4096 + n_scalars:], w[:, 4096:4096 + n_scalars], pad], axis=1)


def w_in_from_kernel(w, n_scalars):
    return jnp.concatenate([w[:, :4096], w[:, TAIL:TAIL + n_scalars], w[:, 4096:TAIL]], axis=1)


PACK_W = 1024
PACK_PARTS = (("w_mem_kv", 256), ("dn_w_in", 1156), ("fox_w_in", 1154), ("w_out", 768), ("w_mlp1", 2048),
              ("w_mlp2", 2048), ("conv_w", 3))
PACK_ROWS = 7680


def _pack_offsets():
    offs, at = {}, 0
    for name, rows in PACK_PARTS:
        offs[name] = at
        at += -(-rows // 16) * 16
    assert at <= PACK_ROWS
    return offs


def pack_shard(parts, dtype):
    pieces, at = [], 0
    for name, rows in PACK_PARTS:
        padded = -(-rows // 16) * 16
        p = parts[name].astype(dtype).reshape(rows, PACK_W)
        pieces.append(jnp.pad(p, ((0, padded - rows), (0, 0))))
        at += padded
    pieces.append(jnp.zeros((PACK_ROWS - at, PACK_W), dtype))
    return jnp.concatenate(pieces, axis=0)


def unpack_shard(packed, shapes):
    offs = _pack_offsets()
    return {name: packed[offs[name]:offs[name] + rows].reshape(shapes[name]) for name, rows in PACK_PARTS}


SHARD_SHAPES = dict(w_mem_kv=(256, 1024), dn_w_in=(1024, 1156), fox_w_in=(1024, 1154), w_out=(2, 384, 1024),
                    w_mlp1=(2, 1024, 1024), w_mlp2=(2, 1024, 1024), conv_w=(4, 768))

SMALL = (("mem_norm_w", 1024), ("mem_k_norm_w", 128), ("norm1_w", 2048), ("dn_a_log", 8), ("dn_dt_bias", 8),
         ("dn_o_norm_w", 128), ("fox_f_bias", 8), ("fox_q_norm_w", 128), ("fox_k_norm_w", 128), ("memq_norm_w", 256),
         ("norm2_w", 2048))
SMALL_ROWS = 8
LOSS_AT = sum(n for _, n in SMALL)


def pack_small(parts, extra=None):
    flat = [parts[name].astype(F32).reshape(-1) for name, _ in SMALL]
    used = LOSS_AT
    if extra is not None:
        flat.append(extra.reshape(1))
        used += 1
    flat.append(jnp.zeros((SMALL_ROWS * PACK_W - used,), F32))
    return jnp.concatenate(flat).reshape(SMALL_ROWS, PACK_W)


def unpack_small(packed, shapes):
    flat, out, at = packed.reshape(-1), {}, 0
    for name, n in SMALL:
        out[name] = flat[at:at + n].reshape(shapes[name])
        at += n
    return out


def full_weights(gathered):
    per = [unpack_shard(gathered[j], SHARD_SHAPES) for j in range(N_CHIP)]
    cat = lambda name, axis: jnp.concatenate([p[name] for p in per], axis=axis)
    return dict(
        w_mem_kv=cat("w_mem_kv", 0),
        dn_w_in=w_in_to_kernel(cat("dn_w_in", 1), 2 * N_HEADS),
        fox_w_in=w_in_to_kernel(cat("fox_w_in", 1), N_HEADS),
        conv_w=cat("conv_w", 1).astype(F32),
        w_out=cat("w_out", 1), w_mlp1=cat("w_mlp1", 2), w_mlp2=cat("w_mlp2", 1))


def grad_shards(big):
    dn = w_in_from_kernel(big["dn_w_in"], 2 * N_HEADS)
    fox = w_in_from_kernel(big["fox_w_in"], N_HEADS)
    w_out, w_mlp1, w_mlp2 = jnp.stack(big["w_out"]), jnp.stack(big["w_mlp1"]), jnp.stack(big["w_mlp2"])
    slots = []
    for j in range(N_CHIP):
        slots.append(pack_shard(dict(
            w_mem_kv=big["w_mem_kv"][256 * j:256 * (j + 1)],
            dn_w_in=dn[:, 1156 * j:1156 * (j + 1)], fox_w_in=fox[:, 1154 * j:1154 * (j + 1)],
            w_out=w_out[:, 384 * j:384 * (j + 1)], w_mlp1=w_mlp1[:, :, 1024 * j:1024 * (j + 1)],
            w_mlp2=w_mlp2[:, 1024 * j:1024 * (j + 1)], conv_w=big["conv_w"][:, 768 * j:768 * (j + 1)]), F32))
    return jnp.stack(slots)


def reduce_scatter(grads, place):
    got_pair = rs_pair_exchange(grads)
    got_chips = rs_chip_exchange(rs_pair_add(place, grads, got_pair))
    return rs_pair_gather(rs_chip_add(place, grads, got_pair, got_chips))


def _adam_all(w, g, m, v, name):
    shape = w.shape
    r2 = lambda a: a.reshape(-1, shape[-1])
    d, nm, nv = adamw(r2(w), r2(g), r2(m), r2(v), name=name)
    return d.reshape(shape), nm.reshape(shape), nv.reshape(shape)


BIG = ("w_mem_kv", "dn_w_in", "dn_conv_w", "fox_w_in", "w_out", "w_mlp1", "w_mlp2")
WEIGHTS = ("mem_norm_w", "w_mem_kv", "mem_k_norm_w", "norm1_w", "dn_w_in", "dn_conv_w", "dn_a_log", "dn_dt_bias",
           "dn_o_norm_w", "fox_w_in", "fox_f_bias", "fox_q_norm_w", "fox_k_norm_w", "memq_norm_w", "w_out", "norm2_w",
           "w_mlp1", "w_mlp2")


def kernel(x, mem, mem_norm_w, w_mem_kv, mem_k_norm_w, norm1_w, dn_w_in, dn_conv_w, dn_a_log, dn_dt_bias, dn_o_norm_w, fox_w_in, fox_f_bias, fox_q_norm_w, fox_k_norm_w, memq_norm_w, w_out, norm2_w, w_mlp1, w_mlp2, loss_target, m_mem_norm_w, m_w_mem_kv, m_mem_k_norm_w, m_norm1_w, m_dn_w_in, m_dn_conv_w, m_dn_a_log, m_dn_dt_bias, m_dn_o_norm_w, m_fox_w_in, m_fox_f_bias, m_fox_q_norm_w, m_fox_k_norm_w, m_memq_norm_w, m_w_out, m_norm2_w, m_w_mlp1, m_w_mlp2, v_mem_norm_w, v_w_mem_kv, v_mem_k_norm_w, v_norm1_w, v_dn_w_in, v_dn_conv_w, v_dn_a_log, v_dn_dt_bias, v_dn_o_norm_w, v_fox_w_in, v_fox_f_bias, v_fox_q_norm_w, v_fox_k_norm_w, v_memq_norm_w, v_w_out, v_norm2_w, v_w_mlp1, v_w_mlp2):
    args = dict(locals())
    w = {n: args[n] for n in WEIGHTS}
    m = {n: args["m_" + n] for n in WEIGHTS}
    v = {n: args["v_" + n] for n in WEIGHTS}
    place = jnp.stack([lax.axis_index("c"), 2 * lax.axis_index("x") + lax.axis_index("y")]).astype(jnp.int32)

    shard = pack_shard(dict(w_mem_kv=w_mem_kv, dn_w_in=dn_w_in[0], fox_w_in=fox_w_in[0], w_out=w_out, w_mlp1=w_mlp1,
                            w_mlp2=w_mlp2, conv_w=dn_conv_w[0]), MM)
    gathered = lax.dynamic_update_slice(all_gather_shards(shard), shard[None], (place[1], 0, 0))
    wt = full_weights(gathered)
    sm = dict(mem_norm_w=mem_norm_w, mem_k_norm_w=mem_k_norm_w, norm1_w=norm1_w, norm2_w=norm2_w, memq_norm_w=memq_norm_w,
              dn_a_log=dn_a_log[0], dn_dt_bias=dn_dt_bias[0], dn_o_norm_w=dn_o_norm_w, fox_f_bias=fox_f_bias[0],
              fox_q_norm_w=fox_q_norm_w, fox_k_norm_w=fox_k_norm_w)
    loss_part, dx, big, small = local_step(x[0], mem[0], loss_target[0], wt, sm)

    big_sum = unpack_shard(reduce_scatter(grad_shards(big), place), SHARD_SHAPES)
    small_sum = all_reduce_small(pack_small(small, loss_part[0, :1]))
    loss = small_sum.reshape(-1)[LOSS_AT]
    grads = unpack_small(small_sum, {n: w[n].shape for n, _ in SMALL})
    grads.update(w_mem_kv=big_sum["w_mem_kv"], dn_w_in=big_sum["dn_w_in"][None], fox_w_in=big_sum["fox_w_in"][None],
                 dn_conv_w=big_sum["conv_w"][None], w_out=big_sum["w_out"], w_mlp1=big_sum["w_mlp1"], w_mlp2=big_sum["w_mlp2"])

    delta, new_m, new_v = {}, {}, {}
    for n in BIG:
        delta[n], new_m[n], new_v[n] = _adam_all(w[n], grads[n], m[n], v[n], "adamw_" + n)
    shapes = {n: w[n].shape for n, _ in SMALL}
    d_s, m_s, v_s = adamw(pack_small(w), small_sum, pack_small(m), pack_small(v), name="adamw_small")
    for out, packed in ((delta, d_s), (new_m, m_s), (new_v, v_s)):
        out.update(unpack_small(packed, shapes))
    return (loss, dx[None], *[grads[n] for n in WEIGHTS], *[delta[n] for n in WEIGHTS],
            *[new_m[n] for n in WEIGHTS], *[new_v[n] for n in WEIGHTS])
```

```python
import functools

import jax
import jax.numpy as jnp
from jax import lax
from jax.experimental import pallas as pl
from jax.experimental.pallas import tpu as pltpu

F32 = jnp.float32
MM = jnp.bfloat16
HI = lax.Precision.HIGHEST

D_MODEL = 1024
HEAD_DIM = 128
N_HEADS = 8
MEM_HEADS = 4
MEM_WIDTH = MEM_HEADS * HEAD_DIM
N_MEM = 256
D_FF = 4 * D_MODEL
CHUNK = 64
EPS = 1e-6
QSCALE = HEAD_DIM ** -0.5
PROJ_W = 4736
TAIL = 4608
TAIL_BLK = TAIL // HEAD_DIM
ROWS = 256
VMEM_LIMIT = 56 * 1024 * 1024

ADAM_LR = 0.001
ADAM_B1 = 0.9
ADAM_B2 = 0.999
ADAM_EPS = 1e-08
ADAM_WD = 0.01
ADAM_STEP = 10

N_DEV = 8
N_CHIP = 4
MESH = pl.DeviceIdType.MESH


def _cparams(sem=None):
    return pltpu.CompilerParams(dimension_semantics=sem, vmem_limit_bytes=VMEM_LIMIT)


def _dot(a, b, ca, cb, hi):
    dims = (((ca,), (cb,)), ((), ()))
    if hi:
        return lax.dot_general(a, b, dims, precision=HI, preferred_element_type=F32)
    return lax.dot_general(a.astype(MM), b.astype(MM), dims, preferred_element_type=F32)


@functools.partial(jax.custom_vjp, nondiff_argnums=(2, 3, 4))
def mmul(a, b, ca, cb, hi):
    return _dot(a, b, ca, cb, hi)


def _mmul_fwd(a, b, ca, cb, hi):
    return _dot(a, b, ca, cb, hi), (a, b)


def _mmul_bwd(ca, cb, hi, res, g):
    a, b = res
    if ca == 1:
        da = _dot(g, b, 1, 1, hi) if cb == 0 else _dot(g, b, 1, 0, hi)
    else:
        da = _dot(b, g, 1, 1, hi) if cb == 0 else _dot(b, g, 0, 1, hi)
    if cb == 0:
        db = _dot(a, g, 0, 0, hi) if ca == 1 else _dot(a, g, 1, 0, hi)
    else:
        db = _dot(g, a, 0, 0, hi) if ca == 1 else _dot(g, a, 0, 1, hi)
    return da.astype(a.dtype), db.astype(b.dtype)


mmul.defvjp(_mmul_fwd, _mmul_bwd)


def _iota2(n, m):
    return lax.broadcasted_iota(jnp.int32, (n, m), 0), lax.broadcasted_iota(jnp.int32, (n, m), 1)


def _same_block(r, c, shift):
    return lax.shift_right_logical(r, shift) == lax.shift_right_logical(c, shift)


def _split_bf16(x):
    hi = x.astype(jnp.bfloat16)
    return hi, (x - hi.astype(F32)).astype(jnp.bfloat16)


def _dot3(a, b, ca, cb):
    dims = (((ca,), (cb,)), ((), ()))
    (ah, al), (bh, bl) = _split_bf16(a), _split_bf16(b)
    d = lambda x, y: lax.dot_general(x, y, dims, preferred_element_type=F32)
    return d(ah, bh) + (d(ah, bl) + d(al, bh))


def _tri_inv_impl(a):
    n = a.shape[0]
    r, c = _iota2(n, n)
    eye = (r == c).astype(F32)
    b16, b32 = _same_block(r, c, 4), _same_block(r, c, 5)
    a0 = jnp.where(b16, a, 0.0)
    p = eye - a0
    b = _dot3(a0, a0, 1, 0)
    p = p + _dot3(p, b, 1, 0)
    b = _dot3(b, b, 1, 0)
    p = p + _dot3(p, b, 1, 0)
    b = _dot3(b, b, 1, 0)
    p = p + _dot3(p, b, 1, 0)
    a1 = jnp.where(jnp.logical_and(b32, jnp.logical_not(b16)), a, 0.0)
    p = p - _dot3(_dot3(p, a1, 1, 0), p, 1, 0)
    a2 = jnp.where(b32, 0.0, a)
    p = p - _dot3(_dot3(p, a2, 1, 0), p, 1, 0)
    return p


@jax.custom_vjp
def tri_inv(a):
    return _tri_inv_impl(a)


def _tri_inv_fwd(a):
    p = _tri_inv_impl(a)
    return p, p


def _tri_inv_bwd(p, g):
    return (-_dot3(_dot3(p, g, 0, 0), p, 1, 1),)


tri_inv.defvjp(_tri_inv_fwd, _tri_inv_bwd)


def _sigmoid(x):
    return 1.0 / (1.0 + jnp.exp(-x))


def _softplus(x):
    return jnp.maximum(x, 0.0) + jnp.log(1.0 + jnp.exp(-jnp.abs(x)))


def _silu(x):
    return x * _sigmoid(x)


def _rms(x, w):
    return x * lax.rsqrt(jnp.mean(x * x, axis=-1, keepdims=True) + EPS) * w


def _bf_round(x):
    return x.astype(MM).astype(F32)


def _acc(ref, val, first):
    @pl.when(first)
    def _():
        ref[...] = val

    @pl.when(jnp.logical_not(first))
    def _():
        ref[...] += val


def _tile(n, pref):
    if n % pref == 0:
        return pref
    return n


def matmul(a, b, *, ta=False, tb=False, res=None, out_dtype=F32, name, tm=1024, tn=1024, tk=1024):
    m, k = (a.shape[1], a.shape[0]) if ta else a.shape
    n = b.shape[0] if tb else b.shape[1]
    assert (b.shape[1] if tb else b.shape[0]) == k, (a.shape, b.shape, ta, tb)
    tm, tn, tk = _tile(m, tm), _tile(n, tn), _tile(k, tk)
    nk = k // tk
    ca, cb = (0 if ta else 1), (1 if tb else 0)

    def body(a_ref, b_ref, *rest):
        r_ref = rest[0] if res is not None else None
        o_ref, acc_ref = rest[-2:]
        kk = pl.program_id(2)
        part = _dot(a_ref[...], b_ref[...], ca, cb, False)

        @pl.when(kk == 0)
        def _():
            acc_ref[...] = part

        @pl.when(kk > 0)
        def _():
            acc_ref[...] += part

        @pl.when(kk == nk - 1)
        def _():
            total = acc_ref[...] if r_ref is None else acc_ref[...] + r_ref[...]
            o_ref[...] = total.astype(o_ref.dtype)

    a_spec = pl.BlockSpec((tk, tm), lambda i, j, l: (l, i)) if ta else pl.BlockSpec((tm, tk), lambda i, j, l: (i, l))
    b_spec = pl.BlockSpec((tn, tk), lambda i, j, l: (j, l)) if tb else pl.BlockSpec((tk, tn), lambda i, j, l: (l, j))
    o_spec = pl.BlockSpec((tm, tn), lambda i, j, l: (i, j))
    extra = () if res is None else (res,)
    return pl.pallas_call(
        body, name=name, grid=(m // tm, n // tn, nk),
        in_specs=[a_spec, b_spec] + [o_spec] * len(extra), out_specs=o_spec,
        out_shape=jax.ShapeDtypeStruct((m, n), out_dtype),
        scratch_shapes=[pltpu.VMEM((tm, tn), F32)],
        compiler_params=_cparams(("parallel", "parallel", "arbitrary")),
    )(a, b, *extra)


def rms_fwd(x, w, *, name):
    t, d = x.shape

    def body(x_ref, w_ref, o_ref):
        o_ref[...] = _rms(x_ref[...], w_ref[...]).astype(o_ref.dtype)

    return pl.pallas_call(
        body, name=name, grid=(t // ROWS,),
        in_specs=[pl.BlockSpec((ROWS, d), lambda i: (i, 0)), pl.BlockSpec((1, d), lambda i: (0, 0))],
        out_specs=pl.BlockSpec((ROWS, d), lambda i: (i, 0)),
        out_shape=jax.ShapeDtypeStruct((t, d), MM), compiler_params=_cparams(("parallel",)),
    )(x, w)


def rms_bwd(x, w, dh, dres, *, name):
    t, d = x.shape

    def body(x_ref, w_ref, dh_ref, dr_ref, dx_ref, dw_ref):
        _, vjp = jax.vjp(_rms, x_ref[...], w_ref[...])
        dx, dw = vjp(dh_ref[...].astype(F32))
        dx_ref[...] = dx + dr_ref[...]
        _acc(dw_ref, dw, pl.program_id(0) == 0)

    row = pl.BlockSpec((ROWS, d), lambda i: (i, 0))
    vec = pl.BlockSpec((1, d), lambda i: (0, 0))
    return pl.pallas_call(
        body, name=name, grid=(t // ROWS,), in_specs=[row, vec, row, row], out_specs=[row, vec],
        out_shape=[jax.ShapeDtypeStruct((t, d), F32), jax.ShapeDtypeStruct((1, d), F32)],
        compiler_params=_cparams(("arbitrary",)),
    )(x, w, dh, dres)


def _sqrelu(x):
    return jnp.square(jnp.maximum(x, 0.0))


def act_fwd(ff, *, name):
    t, f = ff.shape

    def body(x_ref, o_ref):
        o_ref[...] = _sqrelu(x_ref[...]).astype(o_ref.dtype)

    blk = pl.BlockSpec((ROWS, f), lambda i: (i, 0))
    return pl.pallas_call(body, name=name, grid=(t // ROWS,), in_specs=[blk], out_specs=blk,
                          out_shape=jax.ShapeDtypeStruct((t, f), MM), compiler_params=_cparams(("parallel",)))(ff)


def act_bwd(ff, dact, *, name):
    t, f = ff.shape

    def body(x_ref, g_ref, o_ref):
        o_ref[...] = (g_ref[...] * 2.0 * jnp.maximum(x_ref[...], 0.0)).astype(o_ref.dtype)

    blk = pl.BlockSpec((ROWS, f), lambda i: (i, 0))
    return pl.pallas_call(body, name=name, grid=(t // ROWS,), in_specs=[blk, blk], out_specs=blk,
                          out_shape=jax.ShapeDtypeStruct((t, f), MM), compiler_params=_cparams(("parallel",)))(ff, dact)


def loss_fwd(y, target, *, name):
    t, d = y.shape

    def body(y_ref, t_ref, dy_ref, l_ref):
        e = y_ref[...] - t_ref[...]
        dy_ref[...] = e * (1.0 / d)
        part = 0.5 * jnp.sum(jnp.sum(e * e, axis=-1, keepdims=True) * (1.0 / d), axis=0, keepdims=True)
        _acc(l_ref, jnp.broadcast_to(part, (1, HEAD_DIM)), pl.program_id(0) == 0)

    blk = pl.BlockSpec((ROWS, d), lambda i: (i, 0))
    return pl.pallas_call(
        body, name=name, grid=(t // ROWS,), in_specs=[blk, blk],
        out_specs=[blk, pl.BlockSpec((1, HEAD_DIM), lambda i: (0, 0))],
        out_shape=[jax.ShapeDtypeStruct((t, d), F32), jax.ShapeDtypeStruct((1, HEAD_DIM), F32)],
        compiler_params=_cparams(("arbitrary",)),
    )(y, target)


def _mem_kv(mem, wn, wkn, *ws):
    mn = _rms(mem, wn)
    outs = []
    for h in range(MEM_HEADS):
        outs.append(_rms(mmul(mn, ws[h], 1, 0, False), wkn))
    for h in range(MEM_HEADS):
        outs.append(mmul(mn, ws[MEM_HEADS + h], 1, 0, False))
    return tuple(outs)


def _w_cols(w_ref):
    return [w_ref[:, h * HEAD_DIM:(h + 1) * HEAD_DIM] for h in range(2 * MEM_HEADS)]


def mem_fwd(mem, wn, wkv, wkn):
    def body(mem_ref, wn_ref, w_ref, wkn_ref, k_ref, v_ref):
        outs = _mem_kv(mem_ref[...], wn_ref[...], wkn_ref[...], *_w_cols(w_ref))
        for h in range(MEM_HEADS):
            k_ref[:, h * HEAD_DIM:(h + 1) * HEAD_DIM] = outs[h]
            v_ref[:, h * HEAD_DIM:(h + 1) * HEAD_DIM] = outs[MEM_HEADS + h]

    shp = jax.ShapeDtypeStruct((mem.shape[0], MEM_WIDTH), F32)
    return pl.pallas_call(body, name="mem_fwd", out_shape=[shp, shp], compiler_params=_cparams())(mem, wn, wkv, wkn)


def mem_bwd(mem, wn, wkv, wkn, dk0, dv0, dk1, dv1):
    def body(mem_ref, wn_ref, w_ref, wkn_ref, dk0_ref, dv0_ref, dk1_ref, dv1_ref, dwn_ref, dw_ref, dwkn_ref):
        _, vjp = jax.vjp(lambda wn_, wkn_, *ws: _mem_kv(mem_ref[...], wn_, wkn_, *ws),
                         wn_ref[...], wkn_ref[...], *[w.astype(F32) for w in _w_cols(w_ref)])
        cols = lambda a, b: tuple(a[:, h * HEAD_DIM:(h + 1) * HEAD_DIM] + b[:, h * HEAD_DIM:(h + 1) * HEAD_DIM]
                                  for h in range(MEM_HEADS))
        cts = cols(dk0_ref, dk1_ref) + cols(dv0_ref, dv1_ref)
        grads = vjp(cts)
        dwn_ref[...] = grads[0]
        dwkn_ref[...] = grads[1]
        for h in range(2 * MEM_HEADS):
            dw_ref[:, h * HEAD_DIM:(h + 1) * HEAD_DIM] = grads[2 + h]

    return pl.pallas_call(
        body, name="mem_bwd",
        out_shape=[jax.ShapeDtypeStruct((1, D_MODEL), F32), jax.ShapeDtypeStruct((D_MODEL, 2 * MEM_WIDTH), F32),
                   jax.ShapeDtypeStruct((1, HEAD_DIM), F32)],
        compiler_params=_cparams(),
    )(mem, wn, wkv, wkn, dk0, dv0, dk1, dv1)


def _memattn(q, wq, mk, mv):
    qn = _rms(q, wq) * QSCALE
    s = mmul(qn, mk, 1, 1, False)
    s = s - jnp.max(s, axis=-1, keepdims=True)
    p = jnp.exp(s)
    p = p / jnp.sum(p, axis=-1, keepdims=True)
    return mmul(p, mv, 1, 0, False)


def _lanes(j):
    return slice(j * HEAD_DIM, (j + 1) * HEAD_DIM)


def _memattn_specs(t):
    qspec = pl.BlockSpec((ROWS, MEM_WIDTH), lambda i: (i, (TAIL - MEM_WIDTH) // MEM_WIDTH))
    wspec = pl.BlockSpec((1, HEAD_DIM), lambda i: (0, 0))
    mspec = pl.BlockSpec((N_MEM, MEM_WIDTH), lambda i: (0, 0))
    ospec = pl.BlockSpec((ROWS, MEM_WIDTH), lambda i: (i, 0))
    return qspec, wspec, mspec, ospec


def memattn_fwd(proj, wq, mk, mv, *, name):
    t = proj.shape[0]
    qspec, wspec, mspec, ospec = _memattn_specs(t)

    def body(q_ref, w_ref, k_ref, v_ref, o_ref):
        for h in range(MEM_HEADS):
            o_ref[:, _lanes(h)] = _memattn(q_ref[:, _lanes(h)], w_ref[...], k_ref[:, _lanes(h)],
                                           v_ref[:, _lanes(h)]).astype(o_ref.dtype)

    return pl.pallas_call(
        body, name=name, grid=(t // ROWS,), in_specs=[qspec, wspec, mspec, mspec], out_specs=ospec,
        out_shape=jax.ShapeDtypeStruct((t, MEM_WIDTH), MM), compiler_params=_cparams(("parallel",)),
    )(proj, wq, mk, mv)


def memattn_bwd(proj, wq, mk, mv, dcat, *, name):
    t = proj.shape[0]
    qspec, wspec, mspec, ospec = _memattn_specs(t)
    dospec = pl.BlockSpec((ROWS, MEM_WIDTH), lambda i: (i, D_MODEL // MEM_WIDTH))

    def body(q_ref, w_ref, k_ref, v_ref, do_ref, dq_ref, dw_ref, dk_ref, dv_ref):
        first = pl.program_id(0) == 0
        dw_sum = jnp.zeros((1, HEAD_DIM), F32)
        for h in range(MEM_HEADS):
            _, vjp = jax.vjp(_memattn, q_ref[:, _lanes(h)], w_ref[...], k_ref[:, _lanes(h)], v_ref[:, _lanes(h)])
            dq, dw, dk, dv = vjp(do_ref[:, _lanes(h)].astype(F32))
            dq_ref[:, _lanes(h)] = dq.astype(dq_ref.dtype)
            dw_sum = dw_sum + dw
            _acc(dk_ref.at[:, _lanes(h)], dk, first)
            _acc(dv_ref.at[:, _lanes(h)], dv, first)
        _acc(dw_ref, dw_sum, first)

    mshape = jax.ShapeDtypeStruct((N_MEM, MEM_WIDTH), F32)
    return pl.pallas_call(
        body, name=name, grid=(t // ROWS,), in_specs=[qspec, wspec, mspec, mspec, dospec],
        out_specs=[ospec, wspec, mspec, mspec],
        out_shape=[jax.ShapeDtypeStruct((t, MEM_WIDTH), MM), jax.ShapeDtypeStruct((1, HEAD_DIM), F32), mshape, mshape],
        compiler_params=_cparams(("arbitrary",)),
    )(proj, wq, mk, mv, dcat)


def _shift_rows(x, s, up):
    n = x.shape[0]
    r = lax.broadcasted_iota(jnp.int32, x.shape, 0)
    if up:
        return jnp.where(r < n - s, pltpu.roll(x, n - s, 0), 0.0)
    return jnp.where(r >= s, pltpu.roll(x, s, 0), 0.0)


def _conv_fwd_vals(x, w):
    xb = _bf_round(x)
    wb = _bf_round(w)
    c = xb * wb[3:4, :]
    for j in range(3):
        c = c + _shift_rows(xb, 3 - j, False) * wb[j:j + 1, :]
    return xb, wb, c


def dn_prep_fwd(proj, conv_w):
    t = proj.shape[0]

    def body(x_ref, w_ref, o_ref):
        j = pl.program_id(0)
        _, _, c = _conv_fwd_vals(x_ref[...], w_ref[...])
        s = _silu(c)
        r = lax.rsqrt(jnp.sum(s * s, axis=-1, keepdims=True) + EPS)
        scale = jnp.where(j < N_HEADS, QSCALE, 1.0)
        o_ref[...] = jnp.where(j < 2 * N_HEADS, s * r * scale, s)

    return pl.pallas_call(
        body, name="dn_prep_fwd", grid=(3 * N_HEADS,),
        in_specs=[pl.BlockSpec((t, HEAD_DIM), lambda j: (0, j)), pl.BlockSpec((4, HEAD_DIM), lambda j: (0, j))],
        out_specs=pl.BlockSpec((None, t, HEAD_DIM), lambda j: (j // N_HEADS, 0, j % N_HEADS)),
        out_shape=jax.ShapeDtypeStruct((3, t, D_MODEL), F32), compiler_params=_cparams(("parallel",)),
    )(proj, conv_w)


def dn_prep_bwd(proj, conv_w, dqkv):
    t = proj.shape[0]

    def body(x_ref, w_ref, g_ref, dx_ref, dw_ref):
        j = pl.program_id(0)
        xb, wb, c = _conv_fwd_vals(x_ref[...], w_ref[...])
        sg = _sigmoid(c)
        s = c * sg
        g = g_ref[...]
        r = lax.rsqrt(jnp.sum(s * s, axis=-1, keepdims=True) + EPS)
        scale = jnp.where(j < N_HEADS, QSCALE, 1.0)
        gn = g * scale
        ds_norm = r * gn - s * (r * r * r) * jnp.sum(gn * s, axis=-1, keepdims=True)
        ds = jnp.where(j < 2 * N_HEADS, ds_norm, g)
        dc = ds * (sg + s * (1.0 - sg))
        dx = dc * wb[3:4, :]
        rows = [jnp.sum(dc * xb, axis=0, keepdims=True)]
        for jj in range(2, -1, -1):
            sh = 3 - jj
            dx = dx + _shift_rows(dc, sh, True) * wb[jj:jj + 1, :]
            rows.insert(0, jnp.sum(dc * _shift_rows(xb, sh, False), axis=0, keepdims=True))
        dx_ref[...] = dx.astype(dx_ref.dtype)
        dw_ref[...] = jnp.concatenate(rows + [jnp.zeros((4, HEAD_DIM), F32)], axis=0)

    col = pl.BlockSpec((t, HEAD_DIM), lambda j: (0, j))
    return pl.pallas_call(
        body, name="dn_prep_bwd", grid=(3 * N_HEADS,),
        in_specs=[col, pl.BlockSpec((4, HEAD_DIM), lambda j: (0, j)),
                  pl.BlockSpec((None, t, HEAD_DIM), lambda j: (j // N_HEADS, 0, j % N_HEADS))],
        out_specs=[col, pl.BlockSpec((8, HEAD_DIM), lambda j: (0, j))],
        out_shape=[jax.ShapeDtypeStruct((t, 3 * D_MODEL), MM), jax.ShapeDtypeStruct((8, 3 * D_MODEL), F32)],
        compiler_params=_cparams(("parallel",)),
    )(proj, conv_w, dqkv)


def _tri_ones(n, upper):
    r, c = _iota2(n, n)
    return (r <= c).astype(F32) if upper else (r >= c).astype(F32)


def dn_gates_fwd(proj, a_log, dt_bias):
    t = proj.shape[0]

    def body(x_ref, al_ref, dt_ref, o_ref):
        lane = lax.broadcasted_iota(jnp.int32, (CHUNK, HEAD_DIM), 1)
        tri = _tri_ones(CHUNK, False)

        def step(c, carry):
            rows = pl.ds(pl.multiple_of(c * CHUNK, CHUNK), CHUNK)
            x = x_ref[rows, :]
            g = jnp.where(lane < N_HEADS, -jnp.exp(al_ref[...]) * _softplus(x + dt_ref[...]), 0.0)
            gc = _dot(tri, g, 1, 0, True)
            o_ref[rows, :] = jnp.where(lane < N_HEADS, gc, jnp.where(lane < 2 * N_HEADS, _sigmoid(x), 0.0))
            return carry

        lax.fori_loop(0, t // CHUNK, step, 0)

    vec = pl.BlockSpec((1, HEAD_DIM), lambda i: (0, 0))
    return pl.pallas_call(
        body, name="dn_gates_fwd", grid=(1,),
        in_specs=[pl.BlockSpec((t, HEAD_DIM), lambda i: (0, TAIL_BLK)), vec, vec],
        out_specs=pl.BlockSpec((t, HEAD_DIM), lambda i: (0, 0)),
        out_shape=jax.ShapeDtypeStruct((t, HEAD_DIM), F32), compiler_params=_cparams(("arbitrary",)),
    )(proj, a_log, dt_bias)


def dn_gates_bwd(proj, a_log, dt_bias, dgates):
    t = proj.shape[0]

    def body(x_ref, al_ref, dt_ref, g_ref, dx_ref, dal_ref, ddt_ref):
        lane = lax.broadcasted_iota(jnp.int32, (CHUNK, HEAD_DIM), 1)
        tri = _tri_ones(CHUNK, True)
        dal_ref[...] = jnp.zeros_like(dal_ref)
        ddt_ref[...] = jnp.zeros_like(ddt_ref)

        def step(c, carry):
            rows = pl.ds(pl.multiple_of(c * CHUNK, CHUNK), CHUNK)
            x = x_ref[rows, :]
            dgc = jnp.where(lane < N_HEADS, g_ref[rows, :], 0.0)
            dg = _dot(tri, dgc, 1, 0, True)
            ea = -jnp.exp(al_ref[...])
            z = x + dt_ref[...]
            da = jnp.where(lane < N_HEADS, dg * ea * _sigmoid(z), 0.0)
            gval = jnp.where(lane < N_HEADS, ea * _softplus(z), 0.0)
            beta = _sigmoid(x)
            db = jnp.where(jnp.logical_and(lane >= N_HEADS, lane < 2 * N_HEADS), g_ref[rows, :] * beta * (1.0 - beta), 0.0)
            dx_ref[rows, :] = (da + db).astype(dx_ref.dtype)
            dal_ref[...] += jnp.sum(dg * gval, axis=0, keepdims=True)
            ddt_ref[...] += jnp.sum(da, axis=0, keepdims=True)
            return carry

        lax.fori_loop(0, t // CHUNK, step, 0)

    vec = pl.BlockSpec((1, HEAD_DIM), lambda i: (0, 0))
    full = pl.BlockSpec((t, HEAD_DIM), lambda i: (0, 0))
    return pl.pallas_call(
        body, name="dn_gates_bwd", grid=(1,),
        in_specs=[pl.BlockSpec((t, HEAD_DIM), lambda i: (0, TAIL_BLK)), vec, vec, full],
        out_specs=[full, vec, vec],
        out_shape=[jax.ShapeDtypeStruct((t, HEAD_DIM), MM), jax.ShapeDtypeStruct((1, HEAD_DIM), F32),
                   jax.ShapeDtypeStruct((1, HEAD_DIM), F32)],
        compiler_params=_cparams(("arbitrary",)),
    )(proj, a_log, dt_bias, dgates)


def _dn_intra(q, k, v, gcol, grow, bcol):
    r, c = _iota2(CHUNK, CHUNK)
    causal, strict = r >= c, r > c
    decay = jnp.where(causal, jnp.exp(jnp.where(causal, gcol - grow, 0.0)), 0.0)
    kb = k * bcol
    a = jnp.where(strict, mmul(kb, k, 1, 1, False) * decay, 0.0)
    tm = tri_inv(a)
    u = mmul(tm, v * bcol, 1, 0, False)
    w = mmul(tm, kb * jnp.exp(gcol), 1, 0, False)
    qk = jnp.where(causal, mmul(q, k, 1, 1, False) * decay, 0.0)
    rr = lax.broadcasted_iota(jnp.int32, (CHUNK, 1), 0)
    g_last = jnp.sum(jnp.where(rr == CHUNK - 1, gcol, 0.0), axis=0, keepdims=True)
    return u, w, q * jnp.exp(gcol), k * jnp.exp(g_last - gcol), qk, jnp.exp(g_last)


def _dn_scan(u, w, qg, kd, qk, eg, state):
    v_new = u - mmul(w, state, 1, 0, False)
    out = mmul(qg, state, 1, 0, False) + mmul(qk, v_new, 1, 0, False)
    return out, state * eg + mmul(kd, v_new, 0, 0, False)


DN_HEADS_PER_STEP = 1
DN_GROUP = 8
DN_PARTS = ((CHUNK, HEAD_DIM),) * 4 + ((CHUNK, CHUNK), (1, 1))


def _dn_scratch(hb, nc):
    return [pltpu.VMEM((hb, nc) + shape, F32) for shape in DN_PARTS]


def _dn_group(nc):
    return min(DN_GROUP, nc)


def _dn_group_args(refs, j, g, grp):
    q_ref, k_ref, v_ref, gc_ref, gr_ref, bc_ref = refs
    rows = pl.ds(pl.multiple_of(g * (grp * CHUNK), grp * CHUNK), grp * CHUNK)
    cs = pl.ds(g * grp, grp)
    split = lambda ref: ref[rows, _lanes(j)].reshape(grp, CHUNK, HEAD_DIM)
    return split(q_ref), split(k_ref), split(v_ref), gc_ref[j, cs], gr_ref[j, cs], bc_ref[j, cs]


def _dn_intra_all(refs, parts, hb, nc):
    grp = _dn_group(nc)

    def group(g, carry):
        cs = pl.ds(g * grp, grp)
        for j in range(hb):
            for part, val in zip(parts, jax.vmap(_dn_intra)(*_dn_group_args(refs, j, g, grp))):
                part[j, cs] = val
        return carry

    lax.fori_loop(0, nc // grp, group, 0)


def _dn_specs(t):
    nc, hb = t // CHUNK, DN_HEADS_PER_STEP
    head = lambda which: pl.BlockSpec((None, t, hb * HEAD_DIM), lambda h: (which, 0, h))
    flat = pl.BlockSpec((t, hb * HEAD_DIM), lambda h: (0, h))
    col = pl.BlockSpec((hb, nc, CHUNK, 1), lambda h: (h, 0, 0, 0))
    row = pl.BlockSpec((hb, nc, 1, CHUNK), lambda h: (h, 0, 0, 0))
    st = pl.BlockSpec((hb, nc, HEAD_DIM, HEAD_DIM), lambda h: (h, 0, 0, 0))
    return nc, hb, head, flat, col, row, st


def dn_core_fwd(qkv, gcol, grow, bcol):
    t = qkv.shape[1]
    nc, hb, head, flat, col, row, st = _dn_specs(t)

    def body(q_ref, k_ref, v_ref, gc_ref, gr_ref, bc_ref, o_ref, s_ref, *parts):
        _dn_intra_all((q_ref, k_ref, v_ref, gc_ref, gr_ref, bc_ref), parts, hb, nc)

        def step(c, states):
            rows = pl.ds(pl.multiple_of(c * CHUNK, CHUNK), CHUNK)
            new_states = []
            for j in range(hb):
                s_ref[j, c] = states[j]
                out, new_state = _dn_scan(*[part[j, c] for part in parts], states[j])
                o_ref[rows, _lanes(j)] = out
                new_states.append(new_state)
            return tuple(new_states)

        lax.fori_loop(0, nc, step, tuple(jnp.zeros((HEAD_DIM, HEAD_DIM), F32) for _ in range(hb)))

    return pl.pallas_call(
        body, name="dn_core_fwd", grid=(N_HEADS // hb,),
        in_specs=[head(0), head(1), head(2), col, row, col], out_specs=[flat, st],
        out_shape=[jax.ShapeDtypeStruct((t, D_MODEL), F32), jax.ShapeDtypeStruct((N_HEADS, nc, HEAD_DIM, HEAD_DIM), F32)],
        scratch_shapes=_dn_scratch(hb, nc), compiler_params=_cparams(("parallel",)),
    )(qkv, qkv, qkv, gcol, grow, bcol)


def dn_core_bwd(qkv, gcol, grow, bcol, states, do):
    t = qkv.shape[1]
    nc, hb, head, flat, col, row, st = _dn_specs(t)

    def body(q_ref, k_ref, v_ref, gc_ref, gr_ref, bc_ref, s_ref, do_ref, dqkv_ref, dgc_ref, dgr_ref, dbc_ref, *scratch):
        parts, dparts = scratch[:len(DN_PARTS)], scratch[len(DN_PARTS):]
        refs = (q_ref, k_ref, v_ref, gc_ref, gr_ref, bc_ref)
        _dn_intra_all(refs, parts, hb, nc)

        def step(i, dstates):
            c = nc - 1 - i
            rows = pl.ds(pl.multiple_of(c * CHUNK, CHUNK), CHUNK)
            dstates_in = []
            for j in range(hb):
                _, vjp = jax.vjp(_dn_scan, *[part[j, c] for part in parts], s_ref[j, c])
                *dvals, dstate_in = vjp((do_ref[rows, _lanes(j)], dstates[j]))
                for dpart, dval in zip(dparts, dvals):
                    dpart[j, c] = dval
                dstates_in.append(dstate_in)
            return tuple(dstates_in)

        lax.fori_loop(0, nc, step, tuple(jnp.zeros((HEAD_DIM, HEAD_DIM), F32) for _ in range(hb)))

        grp = _dn_group(nc)

        def group(g, carry):
            rows = pl.ds(pl.multiple_of(g * (grp * CHUNK), grp * CHUNK), grp * CHUNK)
            cs = pl.ds(g * grp, grp)
            for j in range(hb):
                _, vjp = jax.vjp(jax.vmap(_dn_intra), *_dn_group_args(refs, j, g, grp))
                dq, dk, dv, dgc, dgr, dbc = vjp(tuple(dpart[j, cs] for dpart in dparts))
                for which, val in enumerate((dq, dk, dv)):
                    dqkv_ref[which, rows, _lanes(j)] = val.reshape(grp * CHUNK, HEAD_DIM)
                dgc_ref[j, cs] = dgc
                dgr_ref[j, cs] = dgr
                dbc_ref[j, cs] = dbc
            return carry

        lax.fori_loop(0, nc // grp, group, 0)

    return pl.pallas_call(
        body, name="dn_core_bwd", grid=(N_HEADS // hb,), scratch_shapes=_dn_scratch(hb, nc) * 2,
        in_specs=[head(0), head(1), head(2), col, row, col, st, flat],
        out_specs=[pl.BlockSpec((3, t, hb * HEAD_DIM), lambda h: (0, 0, h)), col, row, col],
        out_shape=[jax.ShapeDtypeStruct((3, t, D_MODEL), F32)] + [
            jax.ShapeDtypeStruct((N_HEADS, nc, CHUNK, 1), F32), jax.ShapeDtypeStruct((N_HEADS, nc, 1, CHUNK), F32),
            jax.ShapeDtypeStruct((N_HEADS, nc, CHUNK, 1), F32)],
        compiler_params=_cparams(("parallel",)),
    )(qkv, qkv, qkv, gcol, grow, bcol, states, do)


def gates_to_heads(gates):
    t = gates.shape[0]
    nc = t // CHUNK
    g = gates[:, :N_HEADS].T.reshape(N_HEADS, nc, CHUNK)
    b = gates[:, N_HEADS:2 * N_HEADS].T.reshape(N_HEADS, nc, CHUNK)
    return g[..., None], g[:, :, None, :], b[..., None]


def heads_to_gates(dgcol, dgrow, dbcol):
    nh, nc = dgcol.shape[:2]
    dg = (dgcol[..., 0] + dgrow[:, :, 0, :]).reshape(nh, nc * CHUNK).T
    db = dbcol[..., 0].reshape(nh, nc * CHUNK).T
    return jnp.concatenate([dg, db, jnp.zeros((nc * CHUNK, HEAD_DIM - 2 * nh), F32)], axis=1)


def _dn_out(o, z, w):
    return _rms(o, w) * _silu(z)


def _gate_specs():
    o_spec = pl.BlockSpec((ROWS, D_MODEL), lambda i: (i, 0))
    z_spec = pl.BlockSpec((ROWS, D_MODEL), lambda i: (i, 3))
    w_spec = pl.BlockSpec((1, HEAD_DIM), lambda i: (0, 0))
    return o_spec, z_spec, w_spec


def dn_out_fwd(o, proj, w):
    t = o.shape[0]
    o_spec, z_spec, w_spec = _gate_specs()

    def body(o_ref, z_ref, w_ref, y_ref):
        for h in range(N_HEADS):
            y_ref[:, _lanes(h)] = _dn_out(o_ref[:, _lanes(h)], z_ref[:, _lanes(h)], w_ref[...]).astype(y_ref.dtype)

    return pl.pallas_call(
        body, name="dn_out_fwd", grid=(t // ROWS,), in_specs=[o_spec, z_spec, w_spec], out_specs=o_spec,
        out_shape=jax.ShapeDtypeStruct((t, D_MODEL), MM), compiler_params=_cparams(("parallel",)),
    )(o, proj, w)


def dn_out_bwd(o, proj, w, dcat):
    t = o.shape[0]
    o_spec, z_spec, w_spec = _gate_specs()

    def body(o_ref, z_ref, w_ref, g_ref, do_ref, dz_ref, dw_ref):
        dw_sum = jnp.zeros((1, HEAD_DIM), F32)
        for h in range(N_HEADS):
            _, vjp = jax.vjp(_dn_out, o_ref[:, _lanes(h)], z_ref[:, _lanes(h)], w_ref[...])
            do, dz, dw = vjp(g_ref[:, _lanes(h)].astype(F32))
            do_ref[:, _lanes(h)] = do
            dz_ref[:, _lanes(h)] = dz.astype(dz_ref.dtype)
            dw_sum = dw_sum + dw
        _acc(dw_ref, dw_sum, pl.program_id(0) == 0)

    return pl.pallas_call(
        body, name="dn_out_bwd", grid=(t // ROWS,), in_specs=[o_spec, z_spec, w_spec, o_spec],
        out_specs=[o_spec, o_spec, w_spec],
        out_shape=[jax.ShapeDtypeStruct((t, D_MODEL), F32), jax.ShapeDtypeStruct((t, D_MODEL), MM),
                   jax.ShapeDtypeStruct((1, HEAD_DIM), F32)],
        compiler_params=_cparams(("arbitrary",)),
    )(o, proj, w, dcat)


def _fox_norm(x, w, scale):
    return _rms(x, w) * scale


def _fox_prep_specs():
    x_spec = pl.BlockSpec((ROWS, 2 * D_MODEL), lambda i: (i, 0))
    w_spec = pl.BlockSpec((2, 1, HEAD_DIM), lambda i: (0, 0, 0))
    y_spec = pl.BlockSpec((2, ROWS, D_MODEL), lambda i: (0, i, 0))
    return x_spec, w_spec, y_spec


def fox_prep_fwd(proj, wqk):
    t = proj.shape[0]
    x_spec, w_spec, y_spec = _fox_prep_specs()

    def body(x_ref, w_ref, y_ref):
        for j in range(2 * N_HEADS):
            which, scale = j // N_HEADS, (QSCALE if j < N_HEADS else 1.0)
            y_ref[which, :, _lanes(j % N_HEADS)] = _fox_norm(x_ref[:, _lanes(j)], w_ref[which], scale).astype(y_ref.dtype)

    return pl.pallas_call(
        body, name="fox_prep_fwd", grid=(t // ROWS,), in_specs=[x_spec, w_spec], out_specs=y_spec,
        out_shape=jax.ShapeDtypeStruct((2, t, D_MODEL), MM), compiler_params=_cparams(("parallel",)),
    )(proj, wqk)


def fox_prep_bwd(proj, wqk, dq, dk):
    t = proj.shape[0]
    x_spec, w_spec, _ = _fox_prep_specs()
    g_spec = pl.BlockSpec((ROWS, D_MODEL), lambda i: (i, 0))

    def body(x_ref, w_ref, dq_ref, dk_ref, dx_ref, dw_ref):
        dws = [jnp.zeros((1, HEAD_DIM), F32), jnp.zeros((1, HEAD_DIM), F32)]
        for j in range(2 * N_HEADS):
            which, scale = j // N_HEADS, (QSCALE if j < N_HEADS else 1.0)
            g_ref = dq_ref if which == 0 else dk_ref
            _, vjp = jax.vjp(lambda x, w: _fox_norm(x, w, scale), x_ref[:, _lanes(j)], w_ref[which])
            dx, dw = vjp(g_ref[:, _lanes(j % N_HEADS)])
            dx_ref[:, _lanes(j)] = dx.astype(dx_ref.dtype)
            dws[which] = dws[which] + dw
        first = pl.program_id(0) == 0
        _acc(dw_ref.at[0], dws[0], first)
        _acc(dw_ref.at[1], dws[1], first)

    return pl.pallas_call(
        body, name="fox_prep_bwd", grid=(t // ROWS,), in_specs=[x_spec, w_spec, g_spec, g_spec],
        out_specs=[x_spec, w_spec],
        out_shape=[jax.ShapeDtypeStruct((t, 2 * D_MODEL), MM), jax.ShapeDtypeStruct((2, 1, HEAD_DIM), F32)],
        compiler_params=_cparams(("arbitrary",)),
    )(proj, wqk, dq, dk)


def _row_pick(x, i):
    r = lax.broadcasted_iota(jnp.int32, x.shape, 0)
    return jnp.sum(jnp.where(r == i, x, 0.0), axis=0, keepdims=True)


def fox_gates_fwd(proj, f_bias):
    t = proj.shape[0]
    blk = HEAD_DIM

    def body(x_ref, b_ref, o_ref):
        lane = lax.broadcasted_iota(jnp.int32, (blk, HEAD_DIM), 1)
        tri = _tri_ones(blk, False)

        def step(c, carry):
            rows = pl.ds(pl.multiple_of(c * blk, blk), blk)
            lf = jnp.where(lane < N_HEADS, -_softplus(-(x_ref[rows, :] + b_ref[...])), 0.0)
            cum = _dot(tri, lf, 1, 0, True) + carry
            o_ref[rows, :] = cum
            return _row_pick(cum, blk - 1)

        lax.fori_loop(0, t // blk, step, jnp.zeros((1, HEAD_DIM), F32))

    vec = pl.BlockSpec((1, HEAD_DIM), lambda i: (0, 0))
    return pl.pallas_call(
        body, name="fox_gates_fwd", grid=(1,),
        in_specs=[pl.BlockSpec((t, HEAD_DIM), lambda i: (0, TAIL_BLK)), vec],
        out_specs=pl.BlockSpec((t, HEAD_DIM), lambda i: (0, 0)),
        out_shape=jax.ShapeDtypeStruct((t, HEAD_DIM), F32), compiler_params=_cparams(("arbitrary",)),
    )(proj, f_bias)


def fox_gates_bwd(proj, f_bias, dfcum):
    t = proj.shape[0]
    blk = HEAD_DIM
    nb = t // blk

    def body(x_ref, b_ref, g_ref, dx_ref, db_ref):
        lane = lax.broadcasted_iota(jnp.int32, (blk, HEAD_DIM), 1)
        tri = _tri_ones(blk, True)
        db_ref[...] = jnp.zeros_like(db_ref)

        def step(i, carry):
            c = nb - 1 - i
            rows = pl.ds(pl.multiple_of(c * blk, blk), blk)
            g = jnp.where(lane < N_HEADS, g_ref[rows, :], 0.0)
            dlf = _dot(tri, g, 1, 0, True) + carry
            dx = jnp.where(lane < N_HEADS, dlf * _sigmoid(-(x_ref[rows, :] + b_ref[...])), 0.0)
            dx_ref[rows, :] = dx.astype(dx_ref.dtype)
            db_ref[...] += jnp.sum(dx, axis=0, keepdims=True)
            return carry + jnp.sum(g, axis=0, keepdims=True)

        lax.fori_loop(0, nb, step, jnp.zeros((1, HEAD_DIM), F32))

    vec = pl.BlockSpec((1, HEAD_DIM), lambda i: (0, 0))
    full = pl.BlockSpec((t, HEAD_DIM), lambda i: (0, 0))
    return pl.pallas_call(
        body, name="fox_gates_bwd", grid=(1,),
        in_specs=[pl.BlockSpec((t, HEAD_DIM), lambda i: (0, TAIL_BLK)), vec, full], out_specs=[full, vec],
        out_shape=[jax.ShapeDtypeStruct((t, HEAD_DIM), MM), jax.ShapeDtypeStruct((1, HEAD_DIM), F32)],
        compiler_params=_cparams(("arbitrary",)),
    )(proj, f_bias, dfcum)


def fcum_to_heads(fcum):
    f = fcum[:, :N_HEADS].T
    return f[:, :, None], f[:, None, :]


def heads_to_fcum(dfcol, dfrow):
    d = (dfcol[:, :, 0] + dfrow[:, 0, :]).T
    return jnp.concatenate([d, jnp.zeros((d.shape[0], HEAD_DIM - N_HEADS), F32)], axis=1)


def _fox_tq(t):
    return min(t, 256)


def _fox_specs(t):
    tq = _fox_tq(t)
    q_spec = pl.BlockSpec((None, tq, HEAD_DIM), lambda h, i: (0, i, h))
    k_spec = pl.BlockSpec((None, t, HEAD_DIM), lambda h, i: (1, 0, h))
    v_spec = pl.BlockSpec((t, HEAD_DIM), lambda h, i: (0, 2 * N_HEADS + h))
    gate_spec = pl.BlockSpec((tq, HEAD_DIM), lambda h, i: (i, 3 * N_HEADS + h))
    col_spec = pl.BlockSpec((None, tq, 1), lambda h, i: (h, i, 0))
    row_spec = pl.BlockSpec((None, 1, t), lambda h, i: (h, 0, 0))
    blk_spec = pl.BlockSpec((tq, HEAD_DIM), lambda h, i: (i, h))
    head_spec = pl.BlockSpec((t, HEAD_DIM), lambda h, i: (0, h))
    return tq, q_spec, k_spec, v_spec, gate_spec, col_spec, row_spec, blk_spec, head_spec


def _fox_scores(q, k, fcol, frow, i, tq, t):
    s = _dot(q, k, 1, 1, False) + (fcol - frow)
    r = lax.broadcasted_iota(jnp.int32, (tq, t), 0) + i * tq
    c = lax.broadcasted_iota(jnp.int32, (tq, t), 1)
    return s, c <= r


def fox_attn_fwd(qk, proj, fcol, frow):
    t = proj.shape[0]
    tq, q_spec, k_spec, v_spec, gate_spec, col_spec, row_spec, blk_spec, _ = _fox_specs(t)

    def body(q_ref, k_ref, v_ref, gate_ref, fc_ref, fr_ref, mix_ref, o_ref, lse_ref):
        s, mask = _fox_scores(q_ref[...], k_ref[...], fc_ref[...], fr_ref[...], pl.program_id(1), tq, t)
        s = jnp.where(mask, s, -1e30)
        m = jnp.max(s, axis=-1, keepdims=True)
        p = jnp.where(mask, jnp.exp(s - m), 0.0)
        l = jnp.sum(p, axis=-1, keepdims=True)
        o = _dot(p, v_ref[...], 1, 0, False) / l
        o_ref[...] = o
        mix_ref[...] = (o * _sigmoid(gate_ref[...])).astype(mix_ref.dtype)
        lse_ref[...] = m + jnp.log(l)

    return pl.pallas_call(
        body, name="fox_attn_fwd", grid=(N_HEADS, t // tq),
        in_specs=[q_spec, k_spec, v_spec, gate_spec, col_spec, row_spec], out_specs=[blk_spec, blk_spec, col_spec],
        out_shape=[jax.ShapeDtypeStruct((t, D_MODEL), MM), jax.ShapeDtypeStruct((t, D_MODEL), F32),
                   jax.ShapeDtypeStruct((N_HEADS, t, 1), F32)],
        compiler_params=_cparams(("parallel", "parallel")),
    )(qk, qk, proj, proj, fcol, frow)


def fox_attn_bwd(qk, proj, fcol, frow, o, lse, dcat):
    t = proj.shape[0]
    tq, q_spec, k_spec, v_spec, gate_spec, col_spec, row_spec, blk_spec, head_spec = _fox_specs(t)

    def body(q_ref, k_ref, v_ref, gate_ref, fc_ref, fr_ref, o_ref, lse_ref, g_ref,
             dq_ref, dk_ref, dv_ref, dgate_ref, dfc_ref, dfr_ref):
        i = pl.program_id(1)
        sg = _sigmoid(gate_ref[...])
        g = g_ref[...].astype(F32)
        o_pre = o_ref[...]
        do = g * sg
        dgate_ref[...] = (g * o_pre * sg * (1.0 - sg)).astype(dgate_ref.dtype)
        s, mask = _fox_scores(q_ref[...], k_ref[...], fc_ref[...], fr_ref[...], i, tq, t)
        p = jnp.where(mask, jnp.exp(jnp.where(mask, s, 0.0) - lse_ref[...]), 0.0)
        dp = _dot(do, v_ref[...], 1, 1, False)
        delta = jnp.sum(do * o_pre, axis=-1, keepdims=True)
        ds = p * (dp - delta)
        dq_ref[...] = _dot(ds, k_ref[...], 1, 0, False)
        _acc(dk_ref, _dot(ds, q_ref[...], 0, 0, False), i == 0)
        _acc(dv_ref, _dot(p, do, 0, 0, False), i == 0)
        dfc_ref[...] = jnp.sum(ds, axis=-1, keepdims=True)
        _acc(dfr_ref, -jnp.sum(ds, axis=0, keepdims=True), i == 0)

    f32 = lambda *s: jax.ShapeDtypeStruct(s, F32)
    return pl.pallas_call(
        body, name="fox_attn_bwd", grid=(N_HEADS, t // tq),
        in_specs=[q_spec, k_spec, v_spec, gate_spec, col_spec, row_spec, blk_spec, col_spec, blk_spec],
        out_specs=[blk_spec, head_spec, head_spec, blk_spec, col_spec, row_spec],
        out_shape=[f32(t, D_MODEL), f32(t, D_MODEL), f32(t, D_MODEL), jax.ShapeDtypeStruct((t, D_MODEL), MM),
                   f32(N_HEADS, t, 1), f32(N_HEADS, 1, t)],
        compiler_params=_cparams(("parallel", "arbitrary")),
    )(qk, qk, proj, proj, fcol, frow, o, lse, dcat)


def adamw(w, g, m, v, *, name):
    r, c = w.shape
    rb = ROWS if r % ROWS == 0 else r

    def body(w_ref, g_ref, m_ref, v_ref, d_ref, nm_ref, nv_ref):
        g_ = g_ref[...]
        m_ = ADAM_B1 * m_ref[...] + (1.0 - ADAM_B1) * g_
        v_ = ADAM_B2 * v_ref[...] + (1.0 - ADAM_B2) * jnp.square(g_)
        m_hat = m_ / (1.0 - ADAM_B1 ** ADAM_STEP)
        v_hat = v_ / (1.0 - ADAM_B2 ** ADAM_STEP)
        d_ref[...] = -ADAM_LR * (m_hat / (jnp.sqrt(v_hat) + ADAM_EPS) + ADAM_WD * w_ref[...])
        nm_ref[...] = m_
        nv_ref[...] = v_

    blk = pl.BlockSpec((rb, c), lambda i: (i, 0))
    shp = jax.ShapeDtypeStruct((r, c), F32)
    return pl.pallas_call(body, name=name, grid=(r // rb,), in_specs=[blk] * 4, out_specs=[blk] * 3,
                          out_shape=[shp] * 3, compiler_params=_cparams(("parallel",)))(w, g, m, v)


def _place():
    x, y, c = lax.axis_index("x"), lax.axis_index("y"), lax.axis_index("c")
    return x, y, c, [(1 - x, y), (x, 1 - y), (1 - x, 1 - y)]


ANY = pl.BlockSpec(memory_space=pl.ANY)


def all_gather_shards(shard):
    r, w = shard.shape
    rh = r // 2

    def body(x_ref, o_ref, send_sems, recv_sems, fsend_sems, frecv_sems):
        x, y, c, chips = _place()
        me_chip = 2 * x + y
        sibling = (x, y, 1 - c)
        mine_rows = pl.ds(pl.multiple_of(c * rh, 16), rh)
        other_rows = pl.ds(pl.multiple_of((1 - c) * rh, 16), rh)

        def slot(chip):
            return 2 * chip[0] + chip[1]

        sends = []
        for j, chip in enumerate(chips):
            cp = pltpu.make_async_remote_copy(
                src_ref=x_ref.at[mine_rows], dst_ref=o_ref.at[me_chip, mine_rows], send_sem=send_sems.at[j],
                recv_sem=recv_sems.at[j], device_id=(chip[0], chip[1], c), device_id_type=MESH)
            cp.start()
            sends.append(cp)
        for j, chip in enumerate(chips):
            landed = o_ref.at[slot(chip), mine_rows]
            pltpu.make_async_remote_copy(src_ref=landed, dst_ref=landed, send_sem=send_sems.at[j], recv_sem=recv_sems.at[j],
                                         device_id=(chip[0], chip[1], c), device_id_type=MESH).wait_recv()
            cp = pltpu.make_async_remote_copy(src_ref=landed, dst_ref=landed, send_sem=fsend_sems.at[j],
                                              recv_sem=frecv_sems.at[j], device_id=sibling, device_id_type=MESH)
            cp.start()
            sends.append(cp)
        for j, chip in enumerate(chips):
            passed = o_ref.at[slot(chip), other_rows]
            pltpu.make_async_remote_copy(src_ref=passed, dst_ref=passed, send_sem=fsend_sems.at[j], recv_sem=frecv_sems.at[j],
                                         device_id=sibling, device_id_type=MESH).wait_recv()
        for cp in sends:
            cp.wait_send()

    return pl.pallas_call(
        body, name="all_gather_shards", in_specs=[ANY], out_specs=ANY,
        out_shape=jax.ShapeDtypeStruct((N_CHIP, r, w), shard.dtype),
        scratch_shapes=[pltpu.SemaphoreType.DMA((3,))] * 4,
    )(shard)


def rs_pair_exchange(g):
    n, r, w = g.shape
    rh = r // 2

    def body(g_ref, o_ref, send_sem, recv_sem):
        x, y, c, _ = _place()
        other_rows = pl.ds(pl.multiple_of((1 - c) * rh, 8), rh)
        cp = pltpu.make_async_remote_copy(src_ref=g_ref.at[:, other_rows], dst_ref=o_ref, send_sem=send_sem,
                                          recv_sem=recv_sem, device_id=(x, y, 1 - c), device_id_type=MESH)
        cp.start()
        cp.wait()

    return pl.pallas_call(
        body, name="rs_pair_exchange", in_specs=[ANY], out_specs=ANY,
        out_shape=jax.ShapeDtypeStruct((n, rh, w), g.dtype),
        scratch_shapes=[pltpu.SemaphoreType.DMA, pltpu.SemaphoreType.DMA],
    )(g)


def rs_pair_add(place, g, got):
    n, r, w = g.shape
    rh = r // 2
    rb = ROWS
    assert rh % rb == 0, (rh, rb)
    nb = rh // rb

    def body(place_ref, g_ref, got_ref, o_ref):
        o_ref[...] = (g_ref[...] + got_ref[...]).astype(o_ref.dtype)

    return pl.pallas_call(
        body, name="rs_pair_add",
        grid_spec=pltpu.PrefetchScalarGridSpec(
            num_scalar_prefetch=1, grid=(n, nb),
            in_specs=[pl.BlockSpec((None, rb, w), lambda j, i, p: (j, p[0] * nb + i, 0)),
                      pl.BlockSpec((None, rb, w), lambda j, i, p: (j, i, 0))],
            out_specs=pl.BlockSpec((None, rb, w), lambda j, i, p: (j, i, 0))),
        out_shape=jax.ShapeDtypeStruct((n, rh, w), MM), compiler_params=_cparams(("parallel", "parallel")),
    )(place, g, got)


def rs_chip_exchange(p):
    n, rh, w = p.shape

    def body(p_ref, o_ref, send_sems, recv_sems):
        x, y, c, chips = _place()
        cps = []
        for j, chip in enumerate(chips):
            cp = pltpu.make_async_remote_copy(
                src_ref=p_ref.at[2 * chip[0] + chip[1]], dst_ref=o_ref.at[j], send_sem=send_sems.at[j],
                recv_sem=recv_sems.at[j], device_id=(chip[0], chip[1], c), device_id_type=MESH)
            cp.start()
            cps.append(cp)
        for cp in cps:
            cp.wait()

    return pl.pallas_call(
        body, name="rs_chip_exchange", in_specs=[ANY], out_specs=ANY,
        out_shape=jax.ShapeDtypeStruct((3, rh, w), p.dtype),
        scratch_shapes=[pltpu.SemaphoreType.DMA((3,)), pltpu.SemaphoreType.DMA((3,))],
    )(p)


def rs_chip_add(place, g, got_pair, got_chips):
    n, r, w = g.shape
    rh = r // 2
    rb = ROWS
    nb = rh // rb

    def body(place_ref, g_ref, s_ref, a_ref, b_ref, c_ref, o_ref):
        own = g_ref[...] + s_ref[...]
        o_ref[...] = ((own + a_ref[...].astype(F32)) + b_ref[...].astype(F32)) + c_ref[...].astype(F32)

    got_spec = lambda k: pl.BlockSpec((None, rb, w), lambda i, pr: (k, i, 0))
    return pl.pallas_call(
        body, name="rs_chip_add",
        grid_spec=pltpu.PrefetchScalarGridSpec(
            num_scalar_prefetch=1, grid=(nb,),
            in_specs=[pl.BlockSpec((None, rb, w), lambda i, pr: (pr[1], pr[0] * nb + i, 0)),
                      pl.BlockSpec((None, rb, w), lambda i, pr: (pr[1], i, 0)), got_spec(0), got_spec(1), got_spec(2)],
            out_specs=pl.BlockSpec((rb, w), lambda i, pr: (pr[0] * nb + i, 0))),
        out_shape=jax.ShapeDtypeStruct((r, w), F32), compiler_params=_cparams(("parallel",)),
    )(place, g, got_pair, got_chips, got_chips, got_chips)


def rs_pair_gather(full):
    r, w = full.shape
    rh = r // 2

    def body(_, o_ref, send_sem, recv_sem):
        x, y, c, _ = _place()
        mine = o_ref.at[pl.ds(pl.multiple_of(c * rh, 8), rh)]
        theirs = o_ref.at[pl.ds(pl.multiple_of((1 - c) * rh, 8), rh)]
        cp = pltpu.make_async_remote_copy(src_ref=mine, dst_ref=mine, send_sem=send_sem, recv_sem=recv_sem,
                                          device_id=(x, y, 1 - c), device_id_type=MESH)
        cp.start()
        cp.wait_send()
        pltpu.make_async_remote_copy(src_ref=theirs, dst_ref=theirs, send_sem=send_sem, recv_sem=recv_sem,
                                     device_id=(x, y, 1 - c), device_id_type=MESH).wait_recv()

    return pl.pallas_call(
        body, name="rs_pair_gather", in_specs=[ANY], out_specs=ANY, input_output_aliases={0: 0},
        out_shape=jax.ShapeDtypeStruct((r, w), full.dtype),
        scratch_shapes=[pltpu.SemaphoreType.DMA, pltpu.SemaphoreType.DMA],
    )(full)


def all_reduce_small(v):
    r, w = v.shape

    def body(v_ref, o_ref, buf, send_sems, recv_sems):
        x, y, c, _ = _place()
        me = 4 * x + 2 * y + c
        flip = lambda a, bit: 1 - a if bit else a
        cps = []
        for k in range(1, N_DEV):
            peer = (flip(x, k & 4), flip(y, k & 2), flip(c, k & 1))
            cp = pltpu.make_async_remote_copy(src_ref=v_ref, dst_ref=buf.at[me], send_sem=send_sems.at[k - 1],
                                              recv_sem=recv_sems.at[k - 1], device_id=peer, device_id_type=MESH)
            cp.start()
            cps.append((cp, 4 * peer[0] + 2 * peer[1] + peer[2]))
        buf[me] = v_ref[...]
        for k, (cp, peer_id) in enumerate(cps):
            pltpu.make_async_remote_copy(src_ref=v_ref, dst_ref=buf.at[peer_id], send_sem=send_sems.at[k],
                                         recv_sem=recv_sems.at[k], device_id=(x, y, c), device_id_type=MESH).wait_recv()
        for cp, _ in cps:
            cp.wait_send()
        acc = buf[0]
        for d in range(1, N_DEV):
            acc = acc + buf[d]
        o_ref[...] = acc

    vm = pl.BlockSpec(memory_space=pltpu.VMEM)
    return pl.pallas_call(
        body, name="all_reduce_small", in_specs=[vm], out_specs=vm, out_shape=jax.ShapeDtypeStruct((r, w), F32),
        scratch_shapes=[pltpu.VMEM((N_DEV, r, w), F32), pltpu.SemaphoreType.DMA((N_DEV - 1,)),
                        pltpu.SemaphoreType.DMA((N_DEV - 1,))],
    )(v)


def _vec8(v):
    return jnp.zeros((1, HEAD_DIM), F32).at[0, :N_HEADS].set(v.reshape(N_HEADS))


def _layer_fwd(i, x_in, wt, sm, mem_k, mem_v):
    tag = f"l{i}_"
    h = rms_fwd(x_in, sm["norm1_w"][i][None], name=tag + "rms1")
    w_in = wt["dn_w_in"] if i == 0 else wt["fox_w_in"]
    proj = matmul(h, w_in, name=tag + "proj", tm=256, tk=1024)
    sv = dict(x_in=x_in, h=h, proj=proj)
    if i == 0:
        qkv = dn_prep_fwd(proj, wt["conv_w"])
        gates = dn_gates_fwd(proj, _vec8(sm["dn_a_log"]), _vec8(sm["dn_dt_bias"]))
        gcol, grow, bcol = gates_to_heads(gates)
        o, states = dn_core_fwd(qkv, gcol, grow, bcol)
        mix = dn_out_fwd(o, proj, sm["dn_o_norm_w"])
        sv.update(qkv=qkv, gcol=gcol, grow=grow, bcol=bcol, states=states, o=o)
    else:
        wqk = jnp.stack([sm["fox_q_norm_w"], sm["fox_k_norm_w"]])
        qk = fox_prep_fwd(proj, wqk)
        fcum = fox_gates_fwd(proj, _vec8(sm["fox_f_bias"]))
        fcol, frow = fcum_to_heads(fcum)
        mix, o, lse = fox_attn_fwd(qk, proj, fcol, frow)
        sv.update(wqk=wqk, qk=qk, fcol=fcol, frow=frow, o=o, lse=lse)
    mem_out = memattn_fwd(proj, sm["memq_norm_w"][i][None], mem_k, mem_v, name=tag + "memattn_fwd")
    cat = jnp.concatenate([mix, mem_out], axis=1)
    x_mid = matmul(cat, wt["w_out"][i], res=x_in, name=tag + "out_proj")
    h2 = rms_fwd(x_mid, sm["norm2_w"][i][None], name=tag + "rms2")
    ff = matmul(h2, wt["w_mlp1"][i], name=tag + "mlp1")
    act = act_fwd(ff, name=tag + "act_fwd")
    x_out = matmul(act, wt["w_mlp2"][i], res=x_mid, name=tag + "mlp2")
    sv.update(cat=cat, x_mid=x_mid, h2=h2, ff=ff, act=act)
    return x_out, sv


def _layer_bwd(i, dx_out, sv, wt, sm, mem_k, mem_v):
    tag = f"l{i}_"
    big, small = {}, {}
    dact = matmul(dx_out, wt["w_mlp2"][i], tb=True, name=tag + "d_act")
    big["w_mlp2"] = matmul(sv["act"], dx_out, ta=True, name=tag + "d_w_mlp2")
    dff = act_bwd(sv["ff"], dact, name=tag + "act_bwd")
    dh2 = matmul(dff, wt["w_mlp1"][i], tb=True, name=tag + "d_h2")
    big["w_mlp1"] = matmul(sv["h2"], dff, ta=True, name=tag + "d_w_mlp1", tm=512, tn=D_FF, tk=512)
    dx_mid, small["norm2_w"] = rms_bwd(sv["x_mid"], sm["norm2_w"][i][None], dh2, dx_out, name=tag + "rms2_bwd")
    dcat = matmul(dx_mid, wt["w_out"][i], tb=True, name=tag + "d_cat")
    big["w_out"] = matmul(sv["cat"], dx_mid, ta=True, name=tag + "d_w_out")
    proj = sv["proj"]
    dqm, small["memq_norm_w"], dmk, dmv = memattn_bwd(proj, sm["memq_norm_w"][i][None], mem_k, mem_v, dcat,
                                                      name=tag + "memattn_bwd")
    t = proj.shape[0]
    pad = jnp.zeros((t, PROJ_W - TAIL - HEAD_DIM), MM)
    if i == 0:
        do, dz, small["dn_o_norm_w"] = dn_out_bwd(sv["o"], proj, sm["dn_o_norm_w"], dcat)
        dqkv, dgc, dgr, dbc = dn_core_bwd(sv["qkv"], sv["gcol"], sv["grow"], sv["bcol"], sv["states"], do)
        dtail, dal, ddt = dn_gates_bwd(proj, _vec8(sm["dn_a_log"]), _vec8(sm["dn_dt_bias"]), heads_to_gates(dgc, dgr, dbc))
        dmain, dconv = dn_prep_bwd(proj, wt["conv_w"], dqkv)
        small["dn_a_log"], small["dn_dt_bias"] = dal[:, :N_HEADS], ddt[:, :N_HEADS]
        big["conv_w"] = dconv[:4]
        dproj = jnp.concatenate([dmain, dz, dqm, dtail, pad], axis=1)
    else:
        dq, dk, dv, dgate, dfc, dfr = fox_attn_bwd(sv["qk"], proj, sv["fcol"], sv["frow"], sv["o"], sv["lse"], dcat)
        dtail, dfb = fox_gates_bwd(proj, _vec8(sm["fox_f_bias"]), heads_to_fcum(dfc, dfr))
        dqk, dwqk = fox_prep_bwd(proj, sv["wqk"], dq, dk)
        small["fox_f_bias"] = dfb[:, :N_HEADS]
        small["fox_q_norm_w"], small["fox_k_norm_w"] = dwqk[0], dwqk[1]
        dproj = jnp.concatenate([dqk, dv.astype(MM), dgate, dqm, dtail, pad], axis=1)
    w_in = wt["dn_w_in"] if i == 0 else wt["fox_w_in"]
    dh = matmul(dproj, w_in, tb=True, name=tag + "d_h", tm=512)
    big["w_in"] = matmul(sv["h"], dproj, ta=True, name=tag + "d_w_in", tm=256)
    dx_in, small["norm1_w"] = rms_bwd(sv["x_in"], sm["norm1_w"][i][None], dh, dx_mid, name=tag + "rms1_bwd")
    return dx_in, big, small, (dmk, dmv)


def local_step(x, mem, target, wt, sm):
    mem_k, mem_v = mem_fwd(mem, sm["mem_norm_w"][None], wt["w_mem_kv"], sm["mem_k_norm_w"][None])
    x0, sv0 = _layer_fwd(0, x, wt, sm, mem_k, mem_v)
    x1, sv1 = _layer_fwd(1, x0, wt, sm, mem_k, mem_v)
    dy, loss = loss_fwd(x1, target, name="loss")
    dx1, big1, small1, dm1 = _layer_bwd(1, dy, sv1, wt, sm, mem_k, mem_v)
    dx0, big0, small0, dm0 = _layer_bwd(0, dx1, sv0, wt, sm, mem_k, mem_v)
    dwn, dwkv, dwkn = mem_bwd(mem, sm["mem_norm_w"][None], wt["w_mem_kv"], sm["mem_k_norm_w"][None], *dm0, *dm1)
    small = dict(mem_norm_w=dwn[0], mem_k_norm_w=dwkn[0],
                 norm1_w=jnp.concatenate([small0["norm1_w"], small1["norm1_w"]]),
                 norm2_w=jnp.concatenate([small0["norm2_w"], small1["norm2_w"]]),
                 memq_norm_w=jnp.concatenate([small0["memq_norm_w"], small1["memq_norm_w"]]),
                 dn_a_log=small0["dn_a_log"], dn_dt_bias=small0["dn_dt_bias"], dn_o_norm_w=small0["dn_o_norm_w"],
                 fox_f_bias=small1["fox_f_bias"], fox_q_norm_w=small1["fox_q_norm_w"], fox_k_norm_w=small1["fox_k_norm_w"])
    big = dict(w_mem_kv=dwkv, dn_w_in=big0["w_in"], fox_w_in=big1["w_in"], conv_w=big0["conv_w"],
               w_out=[big0["w_out"], big1["w_out"]], w_mlp1=[big0["w_mlp1"], big1["w_mlp1"]],
               w_mlp2=[big0["w_mlp2"], big1["w_mlp2"]])
    return loss, dx0, big, small


def w_in_to_kernel(w, n_scalars):
    pad = jnp.zeros((w.shape[0], PROJ_W - TAIL - n_scalars), w.dtype)
    return jnp.concatenate([w[:, :4096], w[:, 4096 + n_scalars:], w[:, 4096:4096 + n_scalars], pad], axis=1)


def w_in_from_kernel(w, n_scalars):
    return jnp.concatenate([w[:, :4096], w[:, TAIL:TAIL + n_scalars], w[:, 4096:TAIL]], axis=1)


PACK_W = 1024
PACK_PARTS = (("w_mem_kv", 256), ("dn_w_in", 1156), ("fox_w_in", 1154), ("w_out", 768), ("w_mlp1", 2048),
              ("w_mlp2", 2048), ("conv_w", 3))
PACK_ROWS = 7680


def _pack_offsets():
    offs, at = {}, 0
    for name, rows in PACK_PARTS:
        offs[name] = at
        at += -(-rows // 16) * 16
    assert at <= PACK_ROWS
    return offs


def pack_shard(parts, dtype):
    pieces, at = [], 0
    for name, rows in PACK_PARTS:
        padded = -(-rows // 16) * 16
        p = parts[name].astype(dtype).reshape(rows, PACK_W)
        pieces.append(jnp.pad(p, ((0, padded - rows), (0, 0))))
        at += padded
    pieces.append(jnp.zeros((PACK_ROWS - at, PACK_W), dtype))
    return jnp.concatenate(pieces, axis=0)


def unpack_shard(packed, shapes):
    offs = _pack_offsets()
    return {name: packed[offs[name]:offs[name] + rows].reshape(shapes[name]) for name, rows in PACK_PARTS}


SHARD_SHAPES = dict(w_mem_kv=(256, 1024), dn_w_in=(1024, 1156), fox_w_in=(1024, 1154), w_out=(2, 384, 1024),
                    w_mlp1=(2, 1024, 1024), w_mlp2=(2, 1024, 1024), conv_w=(4, 768))

SMALL = (("mem_norm_w", 1024), ("mem_k_norm_w", 128), ("norm1_w", 2048), ("dn_a_log", 8), ("dn_dt_bias", 8),
         ("dn_o_norm_w", 128), ("fox_f_bias", 8), ("fox_q_norm_w", 128), ("fox_k_norm_w", 128), ("memq_norm_w", 256),
         ("norm2_w", 2048))
SMALL_ROWS = 8
LOSS_AT = sum(n for _, n in SMALL)


def pack_small(parts, extra=None):
    flat = [parts[name].astype(F32).reshape(-1) for name, _ in SMALL]
    used = LOSS_AT
    if extra is not None:
        flat.append(extra.reshape(1))
        used += 1
    flat.append(jnp.zeros((SMALL_ROWS * PACK_W - used,), F32))
    return jnp.concatenate(flat).reshape(SMALL_ROWS, PACK_W)


def unpack_small(packed, shapes):
    flat, out, at = packed.reshape(-1), {}, 0
    for name, n in SMALL:
        out[name] = flat[at:at + n].reshape(shapes[name])
        at += n
    return out


def full_weights(gathered):
    per = [unpack_shard(gathered[j], SHARD_SHAPES) for j in range(N_CHIP)]
    cat = lambda name, axis: jnp.concatenate([p[name] for p in per], axis=axis)
    return dict(
        w_mem_kv=cat("w_mem_kv", 0),
        dn_w_in=w_in_to_kernel(cat("dn_w_in", 1), 2 * N_HEADS),
        fox_w_in=w_in_to_kernel(cat("fox_w_in", 1), N_HEADS),
        conv_w=cat("conv_w", 1).astype(F32),
        w_out=cat("w_out", 1), w_mlp1=cat("w_mlp1", 2), w_mlp2=cat("w_mlp2", 1))


def grad_shards(big):
    dn = w_in_from_kernel(big["dn_w_in"], 2 * N_HEADS)
    fox = w_in_from_kernel(big["fox_w_in"], N_HEADS)
    w_out, w_mlp1, w_mlp2 = jnp.stack(big["w_out"]), jnp.stack(big["w_mlp1"]), jnp.stack(big["w_mlp2"])
    slots = []
    for j in range(N_CHIP):
        slots.append(pack_shard(dict(
            w_mem_kv=big["w_mem_kv"][256 * j:256 * (j + 1)],
            dn_w_in=dn[:, 1156 * j:1156 * (j + 1)], fox_w_in=fox[:, 1154 * j:1154 * (j + 1)],
            w_out=w_out[:, 384 * j:384 * (j + 1)], w_mlp1=w_mlp1[:, :, 1024 * j:1024 * (j + 1)],
            w_mlp2=w_mlp2[:, 1024 * j:1024 * (j + 1)], conv_w=big["conv_w"][:, 768 * j:768 * (j + 1)]), F32))
    return jnp.stack(slots)


def reduce_scatter(grads, place):
    got_pair = rs_pair_exchange(grads)
    got_chips = rs_chip_exchange(rs_pair_add(place, grads, got_pair))
    return rs_pair_gather(rs_chip_add(place, grads, got_pair, got_chips))


def _adam_all(w, g, m, v, name):
    shape = w.shape
    r2 = lambda a: a.reshape(-1, shape[-1])
    d, nm, nv = adamw(r2(w), r2(g), r2(m), r2(v), name=name)
    return d.reshape(shape), nm.reshape(shape), nv.reshape(shape)


BIG = ("w_mem_kv", "dn_w_in", "dn_conv_w", "fox_w_in", "w_out", "w_mlp1", "w_mlp2")
WEIGHTS = ("mem_norm_w", "w_mem_kv", "mem_k_norm_w", "norm1_w", "dn_w_in", "dn_conv_w", "dn_a_log", "dn_dt_bias",
           "dn_o_norm_w", "fox_w_in", "fox_f_bias", "fox_q_norm_w", "fox_k_norm_w", "memq_norm_w", "w_out", "norm2_w",
           "w_mlp1", "w_mlp2")


def kernel(x, mem, mem_norm_w, w_mem_kv, mem_k_norm_w, norm1_w, dn_w_in, dn_conv_w, dn_a_log, dn_dt_bias, dn_o_norm_w, fox_w_in, fox_f_bias, fox_q_norm_w, fox_k_norm_w, memq_norm_w, w_out, norm2_w, w_mlp1, w_mlp2, loss_target, m_mem_norm_w, m_w_mem_kv, m_mem_k_norm_w, m_norm1_w, m_dn_w_in, m_dn_conv_w, m_dn_a_log, m_dn_dt_bias, m_dn_o_norm_w, m_fox_w_in, m_fox_f_bias, m_fox_q_norm_w, m_fox_k_norm_w, m_memq_norm_w, m_w_out, m_norm2_w, m_w_mlp1, m_w_mlp2, v_mem_norm_w, v_w_mem_kv, v_mem_k_norm_w, v_norm1_w, v_dn_w_in, v_dn_conv_w, v_dn_a_log, v_dn_dt_bias, v_dn_o_norm_w, v_fox_w_in, v_fox_f_bias, v_fox_q_norm_w, v_fox_k_norm_w, v_memq_norm_w, v_w_out, v_norm2_w, v_w_mlp1, v_w_mlp2):
    args = dict(locals())
    w = {n: args[n] for n in WEIGHTS}
    m = {n: args["m_" + n] for n in WEIGHTS}
    v = {n: args["v_" + n] for n in WEIGHTS}
    place = jnp.stack([lax.axis_index("c"), 2 * lax.axis_index("x") + lax.axis_index("y")]).astype(jnp.int32)

    shard = pack_shard(dict(w_mem_kv=w_mem_kv, dn_w_in=dn_w_in[0], fox_w_in=fox_w_in[0], w_out=w_out, w_mlp1=w_mlp1,
                            w_mlp2=w_mlp2, conv_w=dn_conv_w[0]), MM)
    gathered = lax.dynamic_update_slice(all_gather_shards(shard), shard[None], (place[1], 0, 0))
    wt = full_weights(gathered)
    sm = dict(mem_norm_w=mem_norm_w, mem_k_norm_w=mem_k_norm_w, norm1_w=norm1_w, norm2_w=norm2_w, memq_norm_w=memq_norm_w,
              dn_a_log=dn_a_log[0], dn_dt_bias=dn_dt_bias[0], dn_o_norm_w=dn_o_norm_w, fox_f_bias=fox_f_bias[0],
              fox_q_norm_w=fox_q_norm_w, fox_k_norm_w=fox_k_norm_w)
    loss_part, dx, big, small = local_step(x[0], mem[0], loss_target[0], wt, sm)

    big_sum = unpack_shard(reduce_scatter(grad_shards(big), place), SHARD_SHAPES)
    small_sum = all_reduce_small(pack_small(small, loss_part[0, :1]))
    loss = small_sum.reshape(-1)[LOSS_AT]
    grads = unpack_small(small_sum, {n: w[n].shape for n, _ in SMALL})
    grads.update(w_mem_kv=big_sum["w_mem_kv"], dn_w_in=big_sum["dn_w_in"][None], fox_w_in=big_sum["fox_w_in"][None],
                 dn_conv_w=big_sum["conv_w"][None], w_out=big_sum["w_out"], w_mlp1=big_sum["w_mlp1"], w_mlp2=big_sum["w_mlp2"])

    delta, new_m, new_v = {}, {}, {}
    for n in BIG:
        delta[n], new_m[n], new_v[n] = _adam_all(w[n], grads[n], m[n], v[n], "adamw_" + n)
    shapes = {n: w[n].shape for n, _ in SMALL}
    d_s, m_s, v_s = adamw(pack_small(w), small_sum, pack_small(m), pack_small(v), name="adamw_small")
    for out, packed in ((delta, d_s), (new_m, m_s), (new_v, v_s)):
        out.update(unpack_small(packed, shapes))
    return (loss, dx[None], *[grads[n] for n in WEIGHTS], *[delta[n] for n in WEIGHTS],
            *[new_m[n] for n in WEIGHTS], *[new_v[n] for n in WEIGHTS])
```

```python
import functools

import jax
import jax.numpy as jnp
from jax import lax
from jax.experimental import pallas as pl
from jax.experimental.pallas import tpu as pltpu

F32 = jnp.float32
MM = jnp.bfloat16
HI = lax.Precision.HIGHEST

D_MODEL = 1024
HEAD_DIM = 128
N_HEADS = 8
MEM_HEADS = 4
MEM_WIDTH = MEM_HEADS * HEAD_DIM
N_MEM = 256
D_FF = 4 * D_MODEL
CHUNK = 64
EPS = 1e-6
QSCALE = HEAD_DIM ** -0.5
PROJ_W = 4736
TAIL = 4608
TAIL_BLK = TAIL // HEAD_DIM
ROWS = 256
VMEM_LIMIT = 56 * 1024 * 1024

ADAM_LR = 0.001
ADAM_B1 = 0.9
ADAM_B2 = 0.999
ADAM_EPS = 1e-08
ADAM_WD = 0.01
ADAM_STEP = 10

N_DEV = 8
N_CHIP = 4
MESH = pl.DeviceIdType.MESH


def _cparams(sem=None):
    return pltpu.CompilerParams(dimension_semantics=sem, vmem_limit_bytes=VMEM_LIMIT)


def _dot(a, b, ca, cb, hi):
    dims = (((ca,), (cb,)), ((), ()))
    if hi:
        return lax.dot_general(a, b, dims, precision=HI, preferred_element_type=F32)
    return lax.dot_general(a.astype(MM), b.astype(MM), dims, preferred_element_type=F32)


@functools.partial(jax.custom_vjp, nondiff_argnums=(2, 3, 4))
def mmul(a, b, ca, cb, hi):
    return _dot(a, b, ca, cb, hi)


def _mmul_fwd(a, b, ca, cb, hi):
    return _dot(a, b, ca, cb, hi), (a, b)


def _mmul_bwd(ca, cb, hi, res, g):
    a, b = res
    if ca == 1:
        da = _dot(g, b, 1, 1, hi) if cb == 0 else _dot(g, b, 1, 0, hi)
    else:
        da = _dot(b, g, 1, 1, hi) if cb == 0 else _dot(b, g, 0, 1, hi)
    if cb == 0:
        db = _dot(a, g, 0, 0, hi) if ca == 1 else _dot(a, g, 1, 0, hi)
    else:
        db = _dot(g, a, 0, 0, hi) if ca == 1 else _dot(g, a, 0, 1, hi)
    return da.astype(a.dtype), db.astype(b.dtype)


mmul.defvjp(_mmul_fwd, _mmul_bwd)


def _iota2(n, m):
    return lax.broadcasted_iota(jnp.int32, (n, m), 0), lax.broadcasted_iota(jnp.int32, (n, m), 1)


def _same_block(r, c, shift):
    return lax.shift_right_logical(r, shift) == lax.shift_right_logical(c, shift)


def _split_bf16(x):
    hi = x.astype(jnp.bfloat16)
    return hi, (x - hi.astype(F32)).astype(jnp.bfloat16)


def _dot3(a, b, ca, cb):
    dims = (((ca,), (cb,)), ((), ()))
    (ah, al), (bh, bl) = _split_bf16(a), _split_bf16(b)
    d = lambda x, y: lax.dot_general(x, y, dims, preferred_element_type=F32)
    return d(ah, bh) + (d(ah, bl) + d(al, bh))


def _tri_inv_impl(a):
    n = a.shape[0]
    r, c = _iota2(n, n)
    eye = (r == c).astype(F32)
    b16, b32 = _same_block(r, c, 4), _same_block(r, c, 5)
    a0 = jnp.where(b16, a, 0.0)
    p = eye - a0
    b = _dot3(a0, a0, 1, 0)
    p = p + _dot3(p, b, 1, 0)
    b = _dot3(b, b, 1, 0)
    p = p + _dot3(p, b, 1, 0)
    b = _dot3(b, b, 1, 0)
    p = p + _dot3(p, b, 1, 0)
    a1 = jnp.where(jnp.logical_and(b32, jnp.logical_not(b16)), a, 0.0)
    p = p - _dot3(_dot3(p, a1, 1, 0), p, 1, 0)
    a2 = jnp.where(b32, 0.0, a)
    p = p - _dot3(_dot3(p, a2, 1, 0), p, 1, 0)
    return p


@jax.custom_vjp
def tri_inv(a):
    return _tri_inv_impl(a)


def _tri_inv_fwd(a):
    p = _tri_inv_impl(a)
    return p, p


def _tri_inv_bwd(p, g):
    return (-_dot3(_dot3(p, g, 0, 0), p, 1, 1),)


tri_inv.defvjp(_tri_inv_fwd, _tri_inv_bwd)


def _sigmoid(x):
    return 1.0 / (1.0 + jnp.exp(-x))


def _softplus(x):
    return jnp.maximum(x, 0.0) + jnp.log(1.0 + jnp.exp(-jnp.abs(x)))


def _silu(x):
    return x * _sigmoid(x)


def _rms(x, w):
    return x * lax.rsqrt(jnp.mean(x * x, axis=-1, keepdims=True) + EPS) * w


def _bf_round(x):
    return x.astype(MM).astype(F32)


def _acc(ref, val, first):
    @pl.when(first)
    def _():
        ref[...] = val

    @pl.when(jnp.logical_not(first))
    def _():
        ref[...] += val


def _tile(n, pref):
    if n % pref == 0:
        return pref
    return n


def matmul(a, b, *, ta=False, tb=False, res=None, out_dtype=F32, name, tm=1024, tn=1024, tk=1024):
    m, k = (a.shape[1], a.shape[0]) if ta else a.shape
    n = b.shape[0] if tb else b.shape[1]
    assert (b.shape[1] if tb else b.shape[0]) == k, (a.shape, b.shape, ta, tb)
    tm, tn, tk = _tile(m, tm), _tile(n, tn), _tile(k, tk)
    nk = k // tk
    ca, cb = (0 if ta else 1), (1 if tb else 0)

    def body(a_ref, b_ref, *rest):
        r_ref = rest[0] if res is not None else None
        o_ref, acc_ref = rest[-2:]
        kk = pl.program_id(2)
        part = _dot(a_ref[...], b_ref[...], ca, cb, False)

        @pl.when(kk == 0)
        def _():
            acc_ref[...] = part

        @pl.when(kk > 0)
        def _():
            acc_ref[...] += part

        @pl.when(kk == nk - 1)
        def _():
            total = acc_ref[...] if r_ref is None else acc_ref[...] + r_ref[...]
            o_ref[...] = total.astype(o_ref.dtype)

    a_spec = pl.BlockSpec((tk, tm), lambda i, j, l: (l, i)) if ta else pl.BlockSpec((tm, tk), lambda i, j, l: (i, l))
    b_spec = pl.BlockSpec((tn, tk), lambda i, j, l: (j, l)) if tb else pl.BlockSpec((tk, tn), lambda i, j, l: (l, j))
    o_spec = pl.BlockSpec((tm, tn), lambda i, j, l: (i, j))
    extra = () if res is None else (res,)
    return pl.pallas_call(
        body, name=name, grid=(m // tm, n // tn, nk),
        in_specs=[a_spec, b_spec] + [o_spec] * len(extra), out_specs=o_spec,
        out_shape=jax.ShapeDtypeStruct((m, n), out_dtype),
        scratch_shapes=[pltpu.VMEM((tm, tn), F32)],
        compiler_params=_cparams(("parallel", "parallel", "arbitrary")),
    )(a, b, *extra)


def rms_fwd(x, w, *, name):
    t, d = x.shape

    def body(x_ref, w_ref, o_ref):
        o_ref[...] = _rms(x_ref[...], w_ref[...]).astype(o_ref.dtype)

    return pl.pallas_call(
        body, name=name, grid=(t // ROWS,),
        in_specs=[pl.BlockSpec((ROWS, d), lambda i: (i, 0)), pl.BlockSpec((1, d), lambda i: (0, 0))],
        out_specs=pl.BlockSpec((ROWS, d), lambda i: (i, 0)),
        out_shape=jax.ShapeDtypeStruct((t, d), MM), compiler_params=_cparams(("parallel",)),
    )(x, w)


def rms_bwd(x, w, dh, dres, *, name):
    t, d = x.shape

    def body(x_ref, w_ref, dh_ref, dr_ref, dx_ref, dw_ref):
        _, vjp = jax.vjp(_rms, x_ref[...], w_ref[...])
        dx, dw = vjp(dh_ref[...].astype(F32))
        dx_ref[...] = dx + dr_ref[...]
        _acc(dw_ref, dw, pl.program_id(0) == 0)

    row = pl.BlockSpec((ROWS, d), lambda i: (i, 0))
    vec = pl.BlockSpec((1, d), lambda i: (0, 0))
    return pl.pallas_call(
        body, name=name, grid=(t // ROWS,), in_specs=[row, vec, row, row], out_specs=[row, vec],
        out_shape=[jax.ShapeDtypeStruct((t, d), F32), jax.ShapeDtypeStruct((1, d), F32)],
        compiler_params=_cparams(("arbitrary",)),
    )(x, w, dh, dres)


def _sqrelu(x):
    return jnp.square(jnp.maximum(x, 0.0))


def act_fwd(ff, *, name):
    t, f = ff.shape

    def body(x_ref, o_ref):
        o_ref[...] = _sqrelu(x_ref[...]).astype(o_ref.dtype)

    blk = pl.BlockSpec((ROWS, f), lambda i: (i, 0))
    return pl.pallas_call(body, name=name, grid=(t // ROWS,), in_specs=[blk], out_specs=blk,
                          out_shape=jax.ShapeDtypeStruct((t, f), MM), compiler_params=_cparams(("parallel",)))(ff)


def act_bwd(ff, dact, *, name):
    t, f = ff.shape

    def body(x_ref, g_ref, o_ref):
        o_ref[...] = (g_ref[...] * 2.0 * jnp.maximum(x_ref[...], 0.0)).astype(o_ref.dtype)

    blk = pl.BlockSpec((ROWS, f), lambda i: (i, 0))
    return pl.pallas_call(body, name=name, grid=(t // ROWS,), in_specs=[blk, blk], out_specs=blk,
                          out_shape=jax.ShapeDtypeStruct((t, f), MM), compiler_params=_cparams(("parallel",)))(ff, dact)


def loss_fwd(y, target, *, name):
    t, d = y.shape

    def body(y_ref, t_ref, dy_ref, l_ref):
        e = y_ref[...] - t_ref[...]
        dy_ref[...] = e * (1.0 / d)
        part = 0.5 * jnp.sum(jnp.sum(e * e, axis=-1, keepdims=True) * (1.0 / d), axis=0, keepdims=True)
        _acc(l_ref, jnp.broadcast_to(part, (1, HEAD_DIM)), pl.program_id(0) == 0)

    blk = pl.BlockSpec((ROWS, d), lambda i: (i, 0))
    return pl.pallas_call(
        body, name=name, grid=(t // ROWS,), in_specs=[blk, blk],
        out_specs=[blk, pl.BlockSpec((1, HEAD_DIM), lambda i: (0, 0))],
        out_shape=[jax.ShapeDtypeStruct((t, d), F32), jax.ShapeDtypeStruct((1, HEAD_DIM), F32)],
        compiler_params=_cparams(("arbitrary",)),
    )(y, target)


def _mem_kv(mem, wn, wkn, *ws):
    mn = _rms(mem, wn)
    outs = []
    for h in range(MEM_HEADS):
        outs.append(_rms(mmul(mn, ws[h], 1, 0, False), wkn))
    for h in range(MEM_HEADS):
        outs.append(mmul(mn, ws[MEM_HEADS + h], 1, 0, False))
    return tuple(outs)


def _w_cols(w_ref):
    return [w_ref[:, h * HEAD_DIM:(h + 1) * HEAD_DIM] for h in range(2 * MEM_HEADS)]


def mem_fwd(mem, wn, wkv, wkn):
    def body(mem_ref, wn_ref, w_ref, wkn_ref, k_ref, v_ref):
        outs = _mem_kv(mem_ref[...], wn_ref[...], wkn_ref[...], *_w_cols(w_ref))
        for h in range(MEM_HEADS):
            k_ref[:, h * HEAD_DIM:(h + 1) * HEAD_DIM] = outs[h]
            v_ref[:, h * HEAD_DIM:(h + 1) * HEAD_DIM] = outs[MEM_HEADS + h]

    shp = jax.ShapeDtypeStruct((mem.shape[0], MEM_WIDTH), F32)
    return pl.pallas_call(body, name="mem_fwd", out_shape=[shp, shp], compiler_params=_cparams())(mem, wn, wkv, wkn)


def mem_bwd(mem, wn, wkv, wkn, dk0, dv0, dk1, dv1):
    def body(mem_ref, wn_ref, w_ref, wkn_ref, dk0_ref, dv0_ref, dk1_ref, dv1_ref, dwn_ref, dw_ref, dwkn_ref):
        _, vjp = jax.vjp(lambda wn_, wkn_, *ws: _mem_kv(mem_ref[...], wn_, wkn_, *ws),
                         wn_ref[...], wkn_ref[...], *[w.astype(F32) for w in _w_cols(w_ref)])
        cols = lambda a, b: tuple(a[:, h * HEAD_DIM:(h + 1) * HEAD_DIM] + b[:, h * HEAD_DIM:(h + 1) * HEAD_DIM]
                                  for h in range(MEM_HEADS))
        cts = cols(dk0_ref, dk1_ref) + cols(dv0_ref, dv1_ref)
        grads = vjp(cts)
        dwn_ref[...] = grads[0]
        dwkn_ref[...] = grads[1]
        for h in range(2 * MEM_HEADS):
            dw_ref[:, h * HEAD_DIM:(h + 1) * HEAD_DIM] = grads[2 + h]

    return pl.pallas_call(
        body, name="mem_bwd",
        out_shape=[jax.ShapeDtypeStruct((1, D_MODEL), F32), jax.ShapeDtypeStruct((D_MODEL, 2 * MEM_WIDTH), F32),
                   jax.ShapeDtypeStruct((1, HEAD_DIM), F32)],
        compiler_params=_cparams(),
    )(mem, wn, wkv, wkn, dk0, dv0, dk1, dv1)


def _memattn(q, wq, mk, mv):
    qn = _rms(q, wq) * QSCALE
    s = mmul(qn, mk, 1, 1, False)
    s = s - jnp.max(s, axis=-1, keepdims=True)
    p = jnp.exp(s)
    p = p / jnp.sum(p, axis=-1, keepdims=True)
    return mmul(p, mv, 1, 0, False)


def _lanes(j):
    return slice(j * HEAD_DIM, (j + 1) * HEAD_DIM)


def _memattn_specs(t):
    qspec = pl.BlockSpec((ROWS, MEM_WIDTH), lambda i: (i, (TAIL - MEM_WIDTH) // MEM_WIDTH))
    wspec = pl.BlockSpec((1, HEAD_DIM), lambda i: (0, 0))
    mspec = pl.BlockSpec((N_MEM, MEM_WIDTH), lambda i: (0, 0))
    ospec = pl.BlockSpec((ROWS, MEM_WIDTH), lambda i: (i, 0))
    return qspec, wspec, mspec, ospec


def memattn_fwd(proj, wq, mk, mv, *, name):
    t = proj.shape[0]
    qspec, wspec, mspec, ospec = _memattn_specs(t)

    def body(q_ref, w_ref, k_ref, v_ref, o_ref):
        for h in range(MEM_HEADS):
            o_ref[:, _lanes(h)] = _memattn(q_ref[:, _lanes(h)], w_ref[...], k_ref[:, _lanes(h)],
                                           v_ref[:, _lanes(h)]).astype(o_ref.dtype)

    return pl.pallas_call(
        body, name=name, grid=(t // ROWS,), in_specs=[qspec, wspec, mspec, mspec], out_specs=ospec,
        out_shape=jax.ShapeDtypeStruct((t, MEM_WIDTH), MM), compiler_params=_cparams(("parallel",)),
    )(proj, wq, mk, mv)


def memattn_bwd(proj, wq, mk, mv, dcat, *, name):
    t = proj.shape[0]
    qspec, wspec, mspec, ospec = _memattn_specs(t)
    dospec = pl.BlockSpec((ROWS, MEM_WIDTH), lambda i: (i, D_MODEL // MEM_WIDTH))

    def body(q_ref, w_ref, k_ref, v_ref, do_ref, dq_ref, dw_ref, dk_ref, dv_ref):
        first = pl.program_id(0) == 0
        dw_sum = jnp.zeros((1, HEAD_DIM), F32)
        for h in range(MEM_HEADS):
            _, vjp = jax.vjp(_memattn, q_ref[:, _lanes(h)], w_ref[...], k_ref[:, _lanes(h)], v_ref[:, _lanes(h)])
            dq, dw, dk, dv = vjp(do_ref[:, _lanes(h)].astype(F32))
            dq_ref[:, _lanes(h)] = dq.astype(dq_ref.dtype)
            dw_sum = dw_sum + dw
            _acc(dk_ref.at[:, _lanes(h)], dk, first)
            _acc(dv_ref.at[:, _lanes(h)], dv, first)
        _acc(dw_ref, dw_sum, first)

    mshape = jax.ShapeDtypeStruct((N_MEM, MEM_WIDTH), F32)
    return pl.pallas_call(
        body, name=name, grid=(t // ROWS,), in_specs=[qspec, wspec, mspec, mspec, dospec],
        out_specs=[ospec, wspec, mspec, mspec],
        out_shape=[jax.ShapeDtypeStruct((t, MEM_WIDTH), MM), jax.ShapeDtypeStruct((1, HEAD_DIM), F32), mshape, mshape],
        compiler_params=_cparams(("arbitrary",)),
    )(proj, wq, mk, mv, dcat)


def _shift_rows(x, s, up):
    n = x.shape[0]
    r = lax.broadcasted_iota(jnp.int32, x.shape, 0)
    if up:
        return jnp.where(r < n - s, pltpu.roll(x, n - s, 0), 0.0)
    return jnp.where(r >= s, pltpu.roll(x, s, 0), 0.0)


def _conv_fwd_vals(x, w):
    xb = _bf_round(x)
    wb = _bf_round(w)
    c = xb * wb[3:4, :]
    for j in range(3):
        c = c + _shift_rows(xb, 3 - j, False) * wb[j:j + 1, :]
    return xb, wb, c


def dn_prep_fwd(proj, conv_w):
    t = proj.shape[0]

    def body(x_ref, w_ref, o_ref):
        j = pl.program_id(0)
        _, _, c = _conv_fwd_vals(x_ref[...], w_ref[...])
        s = _silu(c)
        r = lax.rsqrt(jnp.sum(s * s, axis=-1, keepdims=True) + EPS)
        scale = jnp.where(j < N_HEADS, QSCALE, 1.0)
        o_ref[...] = jnp.where(j < 2 * N_HEADS, s * r * scale, s)

    return pl.pallas_call(
        body, name="dn_prep_fwd", grid=(3 * N_HEADS,),
        in_specs=[pl.BlockSpec((t, HEAD_DIM), lambda j: (0, j)), pl.BlockSpec((4, HEAD_DIM), lambda j: (0, j))],
        out_specs=pl.BlockSpec((None, t, HEAD_DIM), lambda j: (j // N_HEADS, 0, j % N_HEADS)),
        out_shape=jax.ShapeDtypeStruct((3, t, D_MODEL), F32), compiler_params=_cparams(("parallel",)),
    )(proj, conv_w)


def dn_prep_bwd(proj, conv_w, dqkv):
    t = proj.shape[0]

    def body(x_ref, w_ref, g_ref, dx_ref, dw_ref):
        j = pl.program_id(0)
        xb, wb, c = _conv_fwd_vals(x_ref[...], w_ref[...])
        sg = _sigmoid(c)
        s = c * sg
        g = g_ref[...]
        r = lax.rsqrt(jnp.sum(s * s, axis=-1, keepdims=True) + EPS)
        scale = jnp.where(j < N_HEADS, QSCALE, 1.0)
        gn = g * scale
        ds_norm = r * gn - s * (r * r * r) * jnp.sum(gn * s, axis=-1, keepdims=True)
        ds = jnp.where(j < 2 * N_HEADS, ds_norm, g)
        dc = ds * (sg + s * (1.0 - sg))
        dx = dc * wb[3:4, :]
        rows = [jnp.sum(dc * xb, axis=0, keepdims=True)]
        for jj in range(2, -1, -1):
            sh = 3 - jj
            dx = dx + _shift_rows(dc, sh, True) * wb[jj:jj + 1, :]
            rows.insert(0, jnp.sum(dc * _shift_rows(xb, sh, False), axis=0, keepdims=True))
        dx_ref[...] = dx.astype(dx_ref.dtype)
        dw_ref[...] = jnp.concatenate(rows + [jnp.zeros((4, HEAD_DIM), F32)], axis=0)

    col = pl.BlockSpec((t, HEAD_DIM), lambda j: (0, j))
    return pl.pallas_call(
        body, name="dn_prep_bwd", grid=(3 * N_HEADS,),
        in_specs=[col, pl.BlockSpec((4, HEAD_DIM), lambda j: (0, j)),
                  pl.BlockSpec((None, t, HEAD_DIM), lambda j: (j // N_HEADS, 0, j % N_HEADS))],
        out_specs=[col, pl.BlockSpec((8, HEAD_DIM), lambda j: (0, j))],
        out_shape=[jax.ShapeDtypeStruct((t, 3 * D_MODEL), MM), jax.ShapeDtypeStruct((8, 3 * D_MODEL), F32)],
        compiler_params=_cparams(("parallel",)),
    )(proj, conv_w, dqkv)


def _tri_ones(n, upper):
    r, c = _iota2(n, n)
    return (r <= c).astype(F32) if upper else (r >= c).astype(F32)


def dn_gates_fwd(proj, a_log, dt_bias):
    t = proj.shape[0]

    def body(x_ref, al_ref, dt_ref, o_ref):
        lane = lax.broadcasted_iota(jnp.int32, (CHUNK, HEAD_DIM), 1)
        tri = _tri_ones(CHUNK, False)

        def step(c, carry):
            rows = pl.ds(pl.multiple_of(c * CHUNK, CHUNK), CHUNK)
            x = x_ref[rows, :]
            g = jnp.where(lane < N_HEADS, -jnp.exp(al_ref[...]) * _softplus(x + dt_ref[...]), 0.0)
            gc = _dot(tri, g, 1, 0, True)
            o_ref[rows, :] = jnp.where(lane < N_HEADS, gc, jnp.where(lane < 2 * N_HEADS, _sigmoid(x), 0.0))
            return carry

        lax.fori_loop(0, t // CHUNK, step, 0)

    vec = pl.BlockSpec((1, HEAD_DIM), lambda i: (0, 0))
    return pl.pallas_call(
        body, name="dn_gates_fwd", grid=(1,),
        in_specs=[pl.BlockSpec((t, HEAD_DIM), lambda i: (0, TAIL_BLK)), vec, vec],
        out_specs=pl.BlockSpec((t, HEAD_DIM), lambda i: (0, 0)),
        out_shape=jax.ShapeDtypeStruct((t, HEAD_DIM), F32), compiler_params=_cparams(("arbitrary",)),
    )(proj, a_log, dt_bias)


def dn_gates_bwd(proj, a_log, dt_bias, dgates):
    t = proj.shape[0]

    def body(x_ref, al_ref, dt_ref, g_ref, dx_ref, dal_ref, ddt_ref):
        lane = lax.broadcasted_iota(jnp.int32, (CHUNK, HEAD_DIM), 1)
        tri = _tri_ones(CHUNK, True)
        dal_ref[...] = jnp.zeros_like(dal_ref)
        ddt_ref[...] = jnp.zeros_like(ddt_ref)

        def step(c, carry):
            rows = pl.ds(pl.multiple_of(c * CHUNK, CHUNK), CHUNK)
            x = x_ref[rows, :]
            dgc = jnp.where(lane < N_HEADS, g_ref[rows, :], 0.0)
            dg = _dot(tri, dgc, 1, 0, True)
            ea = -jnp.exp(al_ref[...])
            z = x + dt_ref[...]
            da = jnp.where(lane < N_HEADS, dg * ea * _sigmoid(z), 0.0)
            gval = jnp.where(lane < N_HEADS, ea * _softplus(z), 0.0)
            beta = _sigmoid(x)
            db = jnp.where(jnp.logical_and(lane >= N_HEADS, lane < 2 * N_HEADS), g_ref[rows, :] * beta * (1.0 - beta), 0.0)
            dx_ref[rows, :] = (da + db).astype(dx_ref.dtype)
            dal_ref[...] += jnp.sum(dg * gval, axis=0, keepdims=True)
            ddt_ref[...] += jnp.sum(da, axis=0, keepdims=True)
            return carry

        lax.fori_loop(0, t // CHUNK, step, 0)

    vec = pl.BlockSpec((1, HEAD_DIM), lambda i: (0, 0))
    full = pl.BlockSpec((t, HEAD_DIM), lambda i: (0, 0))
    return pl.pallas_call(
        body, name="dn_gates_bwd", grid=(1,),
        in_specs=[pl.BlockSpec((t, HEAD_DIM), lambda i: (0, TAIL_BLK)), vec, vec, full],
        out_specs=[full, vec, vec],
        out_shape=[jax.ShapeDtypeStruct((t, HEAD_DIM), MM), jax.ShapeDtypeStruct((1, HEAD_DIM), F32),
                   jax.ShapeDtypeStruct((1, HEAD_DIM), F32)],
        compiler_params=_cparams(("arbitrary",)),
    )(proj, a_log, dt_bias, dgates)


def _dn_intra(q, k, v, gcol, grow, bcol):
    r, c = _iota2(CHUNK, CHUNK)
    causal, strict = r >= c, r > c
    decay = jnp.where(causal, jnp.exp(jnp.where(causal, gcol - grow, 0.0)), 0.0)
    kb = k * bcol
    a = jnp.where(strict, mmul(kb, k, 1, 1, False) * decay, 0.0)
    tm = tri_inv(a)
    u = mmul(tm, v * bcol, 1, 0, False)
    w = mmul(tm, kb * jnp.exp(gcol), 1, 0, False)
    qk = jnp.where(causal, mmul(q, k, 1, 1, False) * decay, 0.0)
    rr = lax.broadcasted_iota(jnp.int32, (CHUNK, 1), 0)
    g_last = jnp.sum(jnp.where(rr == CHUNK - 1, gcol, 0.0), axis=0, keepdims=True)
    return u, w, q * jnp.exp(gcol), k * jnp.exp(g_last - gcol), qk, jnp.exp(g_last)


def _dn_scan(u, w, qg, kd, qk, eg, state):
    v_new = u - mmul(w, state, 1, 0, False)
    out = mmul(qg, state, 1, 0, False) + mmul(qk, v_new, 1, 0, False)
    return out, state * eg + mmul(kd, v_new, 0, 0, False)


DN_HEADS_PER_STEP = 1
DN_GROUP = 8
DN_PARTS = ((CHUNK, HEAD_DIM),) * 4 + ((CHUNK, CHUNK), (1, 1))


def _dn_scratch(hb, nc):
    return [pltpu.VMEM((hb, nc) + shape, F32) for shape in DN_PARTS]


def _dn_group(nc):
    return min(DN_GROUP, nc)


def _dn_group_args(refs, j, g, grp):
    q_ref, k_ref, v_ref, gc_ref, gr_ref, bc_ref = refs
    rows = pl.ds(pl.multiple_of(g * (grp * CHUNK), grp * CHUNK), grp * CHUNK)
    cs = pl.ds(g * grp, grp)
    split = lambda ref: ref[rows, _lanes(j)].reshape(grp, CHUNK, HEAD_DIM)
    return split(q_ref), split(k_ref), split(v_ref), gc_ref[j, cs], gr_ref[j, cs], bc_ref[j, cs]


def _dn_intra_all(refs, parts, hb, nc):
    grp = _dn_group(nc)

    def group(g, carry):
        cs = pl.ds(g * grp, grp)
        for j in range(hb):
            for part, val in zip(parts, jax.vmap(_dn_intra)(*_dn_group_args(refs, j, g, grp))):
                part[j, cs] = val
        return carry

    lax.fori_loop(0, nc // grp, group, 0)


def _dn_specs(t):
    nc, hb = t // CHUNK, DN_HEADS_PER_STEP
    head = lambda which: pl.BlockSpec((None, t, hb * HEAD_DIM), lambda h: (which, 0, h))
    flat = pl.BlockSpec((t, hb * HEAD_DIM), lambda h: (0, h))
    col = pl.BlockSpec((hb, nc, CHUNK, 1), lambda h: (h, 0, 0, 0))
    row = pl.BlockSpec((hb, nc, 1, CHUNK), lambda h: (h, 0, 0, 0))
    st = pl.BlockSpec((hb, nc, HEAD_DIM, HEAD_DIM), lambda h: (h, 0, 0, 0))
    return nc, hb, head, flat, col, row, st


def dn_core_fwd(qkv, gcol, grow, bcol):
    t = qkv.shape[1]
    nc, hb, head, flat, col, row, st = _dn_specs(t)

    def body(q_ref, k_ref, v_ref, gc_ref, gr_ref, bc_ref, o_ref, s_ref, *parts):
        _dn_intra_all((q_ref, k_ref, v_ref, gc_ref, gr_ref, bc_ref), parts, hb, nc)

        def step(c, states):
            rows = pl.ds(pl.multiple_of(c * CHUNK, CHUNK), CHUNK)
            new_states = []
            for j in range(hb):
                s_ref[j, c] = states[j]
                out, new_state = _dn_scan(*[part[j, c] for part in parts], states[j])
                o_ref[rows, _lanes(j)] = out
                new_states.append(new_state)
            return tuple(new_states)

        lax.fori_loop(0, nc, step, tuple(jnp.zeros((HEAD_DIM, HEAD_DIM), F32) for _ in range(hb)))

    return pl.pallas_call(
        body, name="dn_core_fwd", grid=(N_HEADS // hb,),
        in_specs=[head(0), head(1), head(2), col, row, col], out_specs=[flat, st],
        out_shape=[jax.ShapeDtypeStruct((t, D_MODEL), F32), jax.ShapeDtypeStruct((N_HEADS, nc, HEAD_DIM, HEAD_DIM), F32)],
        scratch_shapes=_dn_scratch(hb, nc), compiler_params=_cparams(("parallel",)),
    )(qkv, qkv, qkv, gcol, grow, bcol)


def dn_core_bwd(qkv, gcol, grow, bcol, states, do):
    t = qkv.shape[1]
    nc, hb, head, flat, col, row, st = _dn_specs(t)

    def body(q_ref, k_ref, v_ref, gc_ref, gr_ref, bc_ref, s_ref, do_ref, dqkv_ref, dgc_ref, dgr_ref, dbc_ref, *scratch):
        parts, dparts = scratch[:len(DN_PARTS)], scratch[len(DN_PARTS):]
        refs = (q_ref, k_ref, v_ref, gc_ref, gr_ref, bc_ref)
        _dn_intra_all(refs, parts, hb, nc)

        def step(i, dstates):
            c = nc - 1 - i
            rows = pl.ds(pl.multiple_of(c * CHUNK, CHUNK), CHUNK)
            dstates_in = []
            for j in range(hb):
                _, vjp = jax.vjp(_dn_scan, *[part[j, c] for part in parts], s_ref[j, c])
                *dvals, dstate_in = vjp((do_ref[rows, _lanes(j)], dstates[j]))
                for dpart, dval in zip(dparts, dvals):
                    dpart[j, c] = dval
                dstates_in.append(dstate_in)
            return tuple(dstates_in)

        lax.fori_loop(0, nc, step, tuple(jnp.zeros((HEAD_DIM, HEAD_DIM), F32) for _ in range(hb)))

        grp = _dn_group(nc)

        def group(g, carry):
            rows = pl.ds(pl.multiple_of(g * (grp * CHUNK), grp * CHUNK), grp * CHUNK)
            cs = pl.ds(g * grp, grp)
            for j in range(hb):
                _, vjp = jax.vjp(jax.vmap(_dn_intra), *_dn_group_args(refs, j, g, grp))
                dq, dk, dv, dgc, dgr, dbc = vjp(tuple(dpart[j, cs] for dpart in dparts))
                for which, val in enumerate((dq, dk, dv)):
                    dqkv_ref[which, rows, _lanes(j)] = val.reshape(grp * CHUNK, HEAD_DIM)
                dgc_ref[j, cs] = dgc
                dgr_ref[j, cs] = dgr
                dbc_ref[j, cs] = dbc
            return carry

        lax.fori_loop(0, nc // grp, group, 0)

    return pl.pallas_call(
        body, name="dn_core_bwd", grid=(N_HEADS // hb,), scratch_shapes=_dn_scratch(hb, nc) * 2,
        in_specs=[head(0), head(1), head(2), col, row, col, st, flat],
        out_specs=[pl.BlockSpec((3, t, hb * HEAD_DIM), lambda h: (0, 0, h)), col, row, col],
        out_shape=[jax.ShapeDtypeStruct((3, t, D_MODEL), F32)] + [
            jax.ShapeDtypeStruct((N_HEADS, nc, CHUNK, 1), F32), jax.ShapeDtypeStruct((N_HEADS, nc, 1, CHUNK), F32),
            jax.ShapeDtypeStruct((N_HEADS, nc, CHUNK, 1), F32)],
        compiler_params=_cparams(("parallel",)),
    )(qkv, qkv, qkv, gcol, grow, bcol, states, do)


def gates_to_heads(gates):
    t = gates.shape[0]
    nc = t // CHUNK
    g = gates[:, :N_HEADS].T.reshape(N_HEADS, nc, CHUNK)
    b = gates[:, N_HEADS:2 * N_HEADS].T.reshape(N_HEADS, nc, CHUNK)
    return g[..., None], g[:, :, None, :], b[..., None]


def heads_to_gates(dgcol, dgrow, dbcol):
    nh, nc = dgcol.shape[:2]
    dg = (dgcol[..., 0] + dgrow[:, :, 0, :]).reshape(nh, nc * CHUNK).T
    db = dbcol[..., 0].reshape(nh, nc * CHUNK).T
    return jnp.concatenate([dg, db, jnp.zeros((nc * CHUNK, HEAD_DIM - 2 * nh), F32)], axis=1)


def _dn_out(o, z, w):
    return _rms(o, w) * _silu(z)


def _gate_specs():
    o_spec = pl.BlockSpec((ROWS, D_MODEL), lambda i: (i, 0))
    z_spec = pl.BlockSpec((ROWS, D_MODEL), lambda i: (i, 3))
    w_spec = pl.BlockSpec((1, HEAD_DIM), lambda i: (0, 0))
    return o_spec, z_spec, w_spec


def dn_out_fwd(o, proj, w):
    t = o.shape[0]
    o_spec, z_spec, w_spec = _gate_specs()

    def body(o_ref, z_ref, w_ref, y_ref):
        for h in range(N_HEADS):
            y_ref[:, _lanes(h)] = _dn_out(o_ref[:, _lanes(h)], z_ref[:, _lanes(h)], w_ref[...]).astype(y_ref.dtype)

    return pl.pallas_call(
        body, name="dn_out_fwd", grid=(t // ROWS,), in_specs=[o_spec, z_spec, w_spec], out_specs=o_spec,
        out_shape=jax.ShapeDtypeStruct((t, D_MODEL), MM), compiler_params=_cparams(("parallel",)),
    )(o, proj, w)


def dn_out_bwd(o, proj, w, dcat):
    t = o.shape[0]
    o_spec, z_spec, w_spec = _gate_specs()

    def body(o_ref, z_ref, w_ref, g_ref, do_ref, dz_ref, dw_ref):
        dw_sum = jnp.zeros((1, HEAD_DIM), F32)
        for h in range(N_HEADS):
            _, vjp = jax.vjp(_dn_out, o_ref[:, _lanes(h)], z_ref[:, _lanes(h)], w_ref[...])
            do, dz, dw = vjp(g_ref[:, _lanes(h)].astype(F32))
            do_ref[:, _lanes(h)] = do
            dz_ref[:, _lanes(h)] = dz.astype(dz_ref.dtype)
            dw_sum = dw_sum + dw
        _acc(dw_ref, dw_sum, pl.program_id(0) == 0)

    return pl.pallas_call(
        body, name="dn_out_bwd", grid=(t // ROWS,), in_specs=[o_spec, z_spec, w_spec, o_spec],
        out_specs=[o_spec, o_spec, w_spec],
        out_shape=[jax.ShapeDtypeStruct((t, D_MODEL), F32), jax.ShapeDtypeStruct((t, D_MODEL), MM),
                   jax.ShapeDtypeStruct((1, HEAD_DIM), F32)],
        compiler_params=_cparams(("arbitrary",)),
    )(o, proj, w, dcat)


def _fox_norm(x, w, scale):
    return _rms(x, w) * scale


def _fox_prep_specs():
    x_spec = pl.BlockSpec((ROWS, 2 * D_MODEL), lambda i: (i, 0))
    w_spec = pl.BlockSpec((2, 1, HEAD_DIM), lambda i: (0, 0, 0))
    y_spec = pl.BlockSpec((2, ROWS, D_MODEL), lambda i: (0, i, 0))
    return x_spec, w_spec, y_spec


def fox_prep_fwd(proj, wqk):
    t = proj.shape[0]
    x_spec, w_spec, y_spec = _fox_prep_specs()

    def body(x_ref, w_ref, y_ref):
        for j in range(2 * N_HEADS):
            which, scale = j // N_HEADS, (QSCALE if j < N_HEADS else 1.0)
            y_ref[which, :, _lanes(j % N_HEADS)] = _fox_norm(x_ref[:, _lanes(j)], w_ref[which], scale).astype(y_ref.dtype)

    return pl.pallas_call(
        body, name="fox_prep_fwd", grid=(t // ROWS,), in_specs=[x_spec, w_spec], out_specs=y_spec,
        out_shape=jax.ShapeDtypeStruct((2, t, D_MODEL), MM), compiler_params=_cparams(("parallel",)),
    )(proj, wqk)


def fox_prep_bwd(proj, wqk, dq, dk):
    t = proj.shape[0]
    x_spec, w_spec, _ = _fox_prep_specs()
    g_spec = pl.BlockSpec((ROWS, D_MODEL), lambda i: (i, 0))

    def body(x_ref, w_ref, dq_ref, dk_ref, dx_ref, dw_ref):
        dws = [jnp.zeros((1, HEAD_DIM), F32), jnp.zeros((1, HEAD_DIM), F32)]
        for j in range(2 * N_HEADS):
            which, scale = j // N_HEADS, (QSCALE if j < N_HEADS else 1.0)
            g_ref = dq_ref if which == 0 else dk_ref
            _, vjp = jax.vjp(lambda x, w: _fox_norm(x, w, scale), x_ref[:, _lanes(j)], w_ref[which])
            dx, dw = vjp(g_ref[:, _lanes(j % N_HEADS)])
            dx_ref[:, _lanes(j)] = dx.astype(dx_ref.dtype)
            dws[which] = dws[which] + dw
        first = pl.program_id(0) == 0
        _acc(dw_ref.at[0], dws[0], first)
        _acc(dw_ref.at[1], dws[1], first)

    return pl.pallas_call(
        body, name="fox_prep_bwd", grid=(t // ROWS,), in_specs=[x_spec, w_spec, g_spec, g_spec],
        out_specs=[x_spec, w_spec],
        out_shape=[jax.ShapeDtypeStruct((t, 2 * D_MODEL), MM), jax.ShapeDtypeStruct((2, 1, HEAD_DIM), F32)],
        compiler_params=_cparams(("arbitrary",)),
    )(proj, wqk, dq, dk)


def _row_pick(x, i):
    r = lax.broadcasted_iota(jnp.int32, x.shape, 0)
    return jnp.sum(jnp.where(r == i, x, 0.0), axis=0, keepdims=True)


def fox_gates_fwd(proj, f_bias):
    t = proj.shape[0]
    blk = HEAD_DIM

    def body(x_ref, b_ref, o_ref):
        lane = lax.broadcasted_iota(jnp.int32, (blk, HEAD_DIM), 1)
        tri = _tri_ones(blk, False)

        def step(c, carry):
            rows = pl.ds(pl.multiple_of(c * blk, blk), blk)
            lf = jnp.where(lane < N_HEADS, -_softplus(-(x_ref[rows, :] + b_ref[...])), 0.0)
            cum = _dot(tri, lf, 1, 0, True) + carry
            o_ref[rows, :] = cum
            return _row_pick(cum, blk - 1)

        lax.fori_loop(0, t // blk, step, jnp.zeros((1, HEAD_DIM), F32))

    vec = pl.BlockSpec((1, HEAD_DIM), lambda i: (0, 0))
    return pl.pallas_call(
        body, name="fox_gates_fwd", grid=(1,),
        in_specs=[pl.BlockSpec((t, HEAD_DIM), lambda i: (0, TAIL_BLK)), vec],
        out_specs=pl.BlockSpec((t, HEAD_DIM), lambda i: (0, 0)),
        out_shape=jax.ShapeDtypeStruct((t, HEAD_DIM), F32), compiler_params=_cparams(("arbitrary",)),
    )(proj, f_bias)


def fox_gates_bwd(proj, f_bias, dfcum):
    t = proj.shape[0]
    blk = HEAD_DIM
    nb = t // blk

    def body(x_ref, b_ref, g_ref, dx_ref, db_ref):
        lane = lax.broadcasted_iota(jnp.int32, (blk, HEAD_DIM), 1)
        tri = _tri_ones(blk, True)
        db_ref[...] = jnp.zeros_like(db_ref)

        def step(i, carry):
            c = nb - 1 - i
            rows = pl.ds(pl.multiple_of(c * blk, blk), blk)
            g = jnp.where(lane < N_HEADS, g_ref[rows, :], 0.0)
            dlf = _dot(tri, g, 1, 0, True) + carry
            dx = jnp.where(lane < N_HEADS, dlf * _sigmoid(-(x_ref[rows, :] + b_ref[...])), 0.0)
            dx_ref[rows, :] = dx.astype(dx_ref.dtype)
            db_ref[...] += jnp.sum(dx, axis=0, keepdims=True)
            return carry + jnp.sum(g, axis=0, keepdims=True)

        lax.fori_loop(0, nb, step, jnp.zeros((1, HEAD_DIM), F32))

    vec = pl.BlockSpec((1, HEAD_DIM), lambda i: (0, 0))
    full = pl.BlockSpec((t, HEAD_DIM), lambda i: (0, 0))
    return pl.pallas_call(
        body, name="fox_gates_bwd", grid=(1,),
        in_specs=[pl.BlockSpec((t, HEAD_DIM), lambda i: (0, TAIL_BLK)), vec, full], out_specs=[full, vec],
        out_shape=[jax.ShapeDtypeStruct((t, HEAD_DIM), MM), jax.ShapeDtypeStruct((1, HEAD_DIM), F32)],
        compiler_params=_cparams(("arbitrary",)),
    )(proj, f_bias, dfcum)


def fcum_to_heads(fcum):
    f = fcum[:, :N_HEADS].T
    return f[:, :, None], f[:, None, :]


def heads_to_fcum(dfcol, dfrow):
    d = (dfcol[:, :, 0] + dfrow[:, 0, :]).T
    return jnp.concatenate([d, jnp.zeros((d.shape[0], HEAD_DIM - N_HEADS), F32)], axis=1)


def _fox_tq(t):
    return min(t, 256)


def _fox_specs(t):
    tq = _fox_tq(t)
    q_spec = pl.BlockSpec((None, tq, HEAD_DIM), lambda h, i: (0, i, h))
    k_spec = pl.BlockSpec((None, t, HEAD_DIM), lambda h, i: (1, 0, h))
    v_spec = pl.BlockSpec((t, HEAD_DIM), lambda h, i: (0, 2 * N_HEADS + h))
    gate_spec = pl.BlockSpec((tq, HEAD_DIM), lambda h, i: (i, 3 * N_HEADS + h))
    col_spec = pl.BlockSpec((None, tq, 1), lambda h, i: (h, i, 0))
    row_spec = pl.BlockSpec((None, 1, t), lambda h, i: (h, 0, 0))
    blk_spec = pl.BlockSpec((tq, HEAD_DIM), lambda h, i: (i, h))
    head_spec = pl.BlockSpec((t, HEAD_DIM), lambda h, i: (0, h))
    return tq, q_spec, k_spec, v_spec, gate_spec, col_spec, row_spec, blk_spec, head_spec


def _fox_scores(q, k, fcol, frow, i, tq, t):
    s = _dot(q, k, 1, 1, False) + (fcol - frow)
    r = lax.broadcasted_iota(jnp.int32, (tq, t), 0) + i * tq
    c = lax.broadcasted_iota(jnp.int32, (tq, t), 1)
    return s, c <= r


def fox_attn_fwd(qk, proj, fcol, frow):
    t = proj.shape[0]
    tq, q_spec, k_spec, v_spec, gate_spec, col_spec, row_spec, blk_spec, _ = _fox_specs(t)

    def body(q_ref, k_ref, v_ref, gate_ref, fc_ref, fr_ref, mix_ref, o_ref, lse_ref):
        def block(i):
            w = (i + 1) * tq
            s, mask = _fox_scores(q_ref[...], k_ref[0:w, :], fc_ref[...], fr_ref[:, 0:w], i, tq, w)
            s = jnp.where(mask, s, -1e30)
            m = jnp.max(s, axis=-1, keepdims=True)
            p = jnp.where(mask, jnp.exp(s - m), 0.0)
            l = jnp.sum(p, axis=-1, keepdims=True)
            o = _dot(p, v_ref[0:w, :], 1, 0, False) / l
            o_ref[...] = o
            mix_ref[...] = (o * _sigmoid(gate_ref[...])).astype(mix_ref.dtype)
            lse_ref[...] = m + jnp.log(l)

        for i in range(t // tq):
            pl.when(pl.program_id(1) == i)(functools.partial(block, i))

    return pl.pallas_call(
        body, name="fox_attn_fwd", grid=(N_HEADS, t // tq),
        in_specs=[q_spec, k_spec, v_spec, gate_spec, col_spec, row_spec], out_specs=[blk_spec, blk_spec, col_spec],
        out_shape=[jax.ShapeDtypeStruct((t, D_MODEL), MM), jax.ShapeDtypeStruct((t, D_MODEL), F32),
                   jax.ShapeDtypeStruct((N_HEADS, t, 1), F32)],
        compiler_params=_cparams(("parallel", "parallel")),
    )(qk, qk, proj, proj, fcol, frow)


def fox_attn_bwd(qk, proj, fcol, frow, o, lse, dcat):
    t = proj.shape[0]
    tq, q_spec, k_spec, v_spec, gate_spec, col_spec, row_spec, blk_spec, head_spec = _fox_specs(t)

    def body(q_ref, k_ref, v_ref, gate_ref, fc_ref, fr_ref, o_ref, lse_ref, g_ref,
             dq_ref, dk_ref, dv_ref, dgate_ref, dfc_ref, dfr_ref):
        @pl.when(pl.program_id(1) == 0)
        def _():
            dk_ref[...] = jnp.zeros_like(dk_ref)
            dv_ref[...] = jnp.zeros_like(dv_ref)
            dfr_ref[...] = jnp.zeros_like(dfr_ref)

        def block(i):
            w = (i + 1) * tq
            sg = _sigmoid(gate_ref[...])
            g = g_ref[...].astype(F32)
            o_pre = o_ref[...]
            do = g * sg
            dgate_ref[...] = (g * o_pre * sg * (1.0 - sg)).astype(dgate_ref.dtype)
            s, mask = _fox_scores(q_ref[...], k_ref[0:w, :], fc_ref[...], fr_ref[:, 0:w], i, tq, w)
            p = jnp.where(mask, jnp.exp(jnp.where(mask, s, 0.0) - lse_ref[...]), 0.0)
            dp = _dot(do, v_ref[0:w, :], 1, 1, False)
            delta = jnp.sum(do * o_pre, axis=-1, keepdims=True)
            ds = p * (dp - delta)
            dq_ref[...] = _dot(ds, k_ref[0:w, :], 1, 0, False)
            dk_ref[0:w, :] += _dot(ds, q_ref[...], 0, 0, False)
            dv_ref[0:w, :] += _dot(p, do, 0, 0, False)
            dfc_ref[...] = jnp.sum(ds, axis=-1, keepdims=True)
            dfr_ref[:, 0:w] += -jnp.sum(ds, axis=0, keepdims=True)

        for i in range(t // tq):
            pl.when(pl.program_id(1) == i)(functools.partial(block, i))

    f32 = lambda *s: jax.ShapeDtypeStruct(s, F32)
    return pl.pallas_call(
        body, name="fox_attn_bwd", grid=(N_HEADS, t // tq),
        in_specs=[q_spec, k_spec, v_spec, gate_spec, col_spec, row_spec, blk_spec, col_spec, blk_spec],
        out_specs=[blk_spec, head_spec, head_spec, blk_spec, col_spec, row_spec],
        out_shape=[f32(t, D_MODEL), f32(t, D_MODEL), f32(t, D_MODEL), jax.ShapeDtypeStruct((t, D_MODEL), MM),
                   f32(N_HEADS, t, 1), f32(N_HEADS, 1, t)],
        compiler_params=_cparams(("parallel", "arbitrary")),
    )(qk, qk, proj, proj, fcol, frow, o, lse, dcat)


def adamw(w, g, m, v, *, name):
    r, c = w.shape
    rb = ROWS if r % ROWS == 0 else r

    def body(w_ref, g_ref, m_ref, v_ref, d_ref, nm_ref, nv_ref):
        g_ = g_ref[...]
        m_ = ADAM_B1 * m_ref[...] + (1.0 - ADAM_B1) * g_
        v_ = ADAM_B2 * v_ref[...] + (1.0 - ADAM_B2) * jnp.square(g_)
        m_hat = m_ / (1.0 - ADAM_B1 ** ADAM_STEP)
        v_hat = v_ / (1.0 - ADAM_B2 ** ADAM_STEP)
        d_ref[...] = -ADAM_LR * (m_hat / (jnp.sqrt(v_hat) + ADAM_EPS) + ADAM_WD * w_ref[...])
        nm_ref[...] = m_
        nv_ref[...] = v_

    blk = pl.BlockSpec((rb, c), lambda i: (i, 0))
    shp = jax.ShapeDtypeStruct((r, c), F32)
    return pl.pallas_call(body, name=name, grid=(r // rb,), in_specs=[blk] * 4, out_specs=[blk] * 3,
                          out_shape=[shp] * 3, compiler_params=_cparams(("parallel",)))(w, g, m, v)


def _place():
    x, y, c = lax.axis_index("x"), lax.axis_index("y"), lax.axis_index("c")
    return x, y, c, [(1 - x, y), (x, 1 - y), (1 - x, 1 - y)]


ANY = pl.BlockSpec(memory_space=pl.ANY)


def all_reduce_small(v):
    r, w = v.shape

    def body(v_ref, o_ref, buf, send_sems, recv_sems):
        x, y, c, _ = _place()
        me = 4 * x + 2 * y + c
        flip = lambda a, bit: 1 - a if bit else a
        cps = []
        for k in range(1, N_DEV):
            peer = (flip(x, k & 4), flip(y, k & 2), flip(c, k & 1))
            cp = pltpu.make_async_remote_copy(src_ref=v_ref, dst_ref=buf.at[me], send_sem=send_sems.at[k - 1],
                                              recv_sem=recv_sems.at[k - 1], device_id=peer, device_id_type=MESH)
            cp.start()
            cps.append((cp, 4 * peer[0] + 2 * peer[1] + peer[2]))
        buf[me] = v_ref[...]
        for k, (cp, peer_id) in enumerate(cps):
            pltpu.make_async_remote_copy(src_ref=v_ref, dst_ref=buf.at[peer_id], send_sem=send_sems.at[k],
                                         recv_sem=recv_sems.at[k], device_id=(x, y, c), device_id_type=MESH).wait_recv()
        for cp, _ in cps:
            cp.wait_send()
        acc = buf[0]
        for d in range(1, N_DEV):
            acc = acc + buf[d]
        o_ref[...] = acc

    vm = pl.BlockSpec(memory_space=pltpu.VMEM)
    return pl.pallas_call(
        body, name="all_reduce_small", in_specs=[vm], out_specs=vm, out_shape=jax.ShapeDtypeStruct((r, w), F32),
        scratch_shapes=[pltpu.VMEM((N_DEV, r, w), F32), pltpu.SemaphoreType.DMA((N_DEV - 1,)),
                        pltpu.SemaphoreType.DMA((N_DEV - 1,))],
    )(v)


def _vec8(v):
    return jnp.zeros((1, HEAD_DIM), F32).at[0, :N_HEADS].set(v.reshape(N_HEADS))


def _layer_fwd(i, x_in, wt, sm, mem_k, mem_v):
    tag = f"l{i}_"
    h = rms_fwd(x_in, sm["norm1_w"][i][None], name=tag + "rms1")
    w_in = wt["dn_w_in"] if i == 0 else wt["fox_w_in"]
    proj = matmul(h, w_in, name=tag + "proj", tm=256, tk=1024)
    sv = dict(x_in=x_in, h=h, proj=proj)
    if i == 0:
        qkv = dn_prep_fwd(proj, wt["conv_w"])
        gates = dn_gates_fwd(proj, _vec8(sm["dn_a_log"]), _vec8(sm["dn_dt_bias"]))
        gcol, grow, bcol = gates_to_heads(gates)
        o, states = dn_core_fwd(qkv, gcol, grow, bcol)
        mix = dn_out_fwd(o, proj, sm["dn_o_norm_w"])
        sv.update(qkv=qkv, gcol=gcol, grow=grow, bcol=bcol, states=states, o=o)
    else:
        wqk = jnp.stack([sm["fox_q_norm_w"], sm["fox_k_norm_w"]])
        qk = fox_prep_fwd(proj, wqk)
        fcum = fox_gates_fwd(proj, _vec8(sm["fox_f_bias"]))
        fcol, frow = fcum_to_heads(fcum)
        mix, o, lse = fox_attn_fwd(qk, proj, fcol, frow)
        sv.update(wqk=wqk, qk=qk, fcol=fcol, frow=frow, o=o, lse=lse)
    mem_out = memattn_fwd(proj, sm["memq_norm_w"][i][None], mem_k, mem_v, name=tag + "memattn_fwd")
    cat = jnp.concatenate([mix, mem_out], axis=1)
    x_mid = matmul(cat, wt["w_out"][i], res=x_in, name=tag + "out_proj")
    h2 = rms_fwd(x_mid, sm["norm2_w"][i][None], name=tag + "rms2")
    ff = matmul(h2, wt["w_mlp1"][i], name=tag + "mlp1")
    act = act_fwd(ff, name=tag + "act_fwd")
    x_out = matmul(act, wt["w_mlp2"][i], res=x_mid, name=tag + "mlp2")
    sv.update(cat=cat, x_mid=x_mid, h2=h2, ff=ff, act=act)
    return x_out, sv


def _layer_bwd(i, dx_out, sv, wt, sm, mem_k, mem_v):
    tag = f"l{i}_"
    big, small = {}, {}
    dact = matmul(dx_out, wt["w_mlp2"][i], tb=True, name=tag + "d_act")
    big["w_mlp2"] = matmul(sv["act"], dx_out, ta=True, name=tag + "d_w_mlp2")
    dff = act_bwd(sv["ff"], dact, name=tag + "act_bwd")
    dh2 = matmul(dff, wt["w_mlp1"][i], tb=True, name=tag + "d_h2")
    big["w_mlp1"] = matmul(sv["h2"], dff, ta=True, name=tag + "d_w_mlp1", tm=512, tn=D_FF, tk=512)
    dx_mid, small["norm2_w"] = rms_bwd(sv["x_mid"], sm["norm2_w"][i][None], dh2, dx_out, name=tag + "rms2_bwd")
    dcat = matmul(dx_mid, wt["w_out"][i], tb=True, name=tag + "d_cat")
    big["w_out"] = matmul(sv["cat"], dx_mid, ta=True, name=tag + "d_w_out")
    proj = sv["proj"]
    dqm, small["memq_norm_w"], dmk, dmv = memattn_bwd(proj, sm["memq_norm_w"][i][None], mem_k, mem_v, dcat,
                                                      name=tag + "memattn_bwd")
    t = proj.shape[0]
    pad = jnp.zeros((t, PROJ_W - TAIL - HEAD_DIM), MM)
    if i == 0:
        do, dz, small["dn_o_norm_w"] = dn_out_bwd(sv["o"], proj, sm["dn_o_norm_w"], dcat)
        dqkv, dgc, dgr, dbc = dn_core_bwd(sv["qkv"], sv["gcol"], sv["grow"], sv["bcol"], sv["states"], do)
        dtail, dal, ddt = dn_gates_bwd(proj, _vec8(sm["dn_a_log"]), _vec8(sm["dn_dt_bias"]), heads_to_gates(dgc, dgr, dbc))
        dmain, dconv = dn_prep_bwd(proj, wt["conv_w"], dqkv)
        small["dn_a_log"], small["dn_dt_bias"] = dal[:, :N_HEADS], ddt[:, :N_HEADS]
        big["conv_w"] = dconv[:4]
        dproj = jnp.concatenate([dmain, dz, dqm, dtail, pad], axis=1)
    else:
        dq, dk, dv, dgate, dfc, dfr = fox_attn_bwd(sv["qk"], proj, sv["fcol"], sv["frow"], sv["o"], sv["lse"], dcat)
        dtail, dfb = fox_gates_bwd(proj, _vec8(sm["fox_f_bias"]), heads_to_fcum(dfc, dfr))
        dqk, dwqk = fox_prep_bwd(proj, sv["wqk"], dq, dk)
        small["fox_f_bias"] = dfb[:, :N_HEADS]
        small["fox_q_norm_w"], small["fox_k_norm_w"] = dwqk[0], dwqk[1]
        dproj = jnp.concatenate([dqk, dv.astype(MM), dgate, dqm, dtail, pad], axis=1)
    w_in = wt["dn_w_in"] if i == 0 else wt["fox_w_in"]
    dh = matmul(dproj, w_in, tb=True, name=tag + "d_h", tm=512)
    big["w_in"] = matmul(sv["h"], dproj, ta=True, name=tag + "d_w_in", tm=256)
    dx_in, small["norm1_w"] = rms_bwd(sv["x_in"], sm["norm1_w"][i][None], dh, dx_mid, name=tag + "rms1_bwd")
    return dx_in, big, small, (dmk, dmv)


def local_step(x, mem, target, wt, sm):
    mem_k, mem_v = mem_fwd(mem, sm["mem_norm_w"][None], wt["w_mem_kv"], sm["mem_k_norm_w"][None])
    x0, sv0 = _layer_fwd(0, x, wt, sm, mem_k, mem_v)
    x1, sv1 = _layer_fwd(1, x0, wt, sm, mem_k, mem_v)
    dy, loss = loss_fwd(x1, target, name="loss")
    dx1, big1, small1, dm1 = _layer_bwd(1, dy, sv1, wt, sm, mem_k, mem_v)
    dx0, big0, small0, dm0 = _layer_bwd(0, dx1, sv0, wt, sm, mem_k, mem_v)
    dwn, dwkv, dwkn = mem_bwd(mem, sm["mem_norm_w"][None], wt["w_mem_kv"], sm["mem_k_norm_w"][None], *dm0, *dm1)
    small = dict(mem_norm_w=dwn[0], mem_k_norm_w=dwkn[0],
                 norm1_w=jnp.concatenate([small0["norm1_w"], small1["norm1_w"]]),
                 norm2_w=jnp.concatenate([small0["norm2_w"], small1["norm2_w"]]),
                 memq_norm_w=jnp.concatenate([small0["memq_norm_w"], small1["memq_norm_w"]]),
                 dn_a_log=small0["dn_a_log"], dn_dt_bias=small0["dn_dt_bias"], dn_o_norm_w=small0["dn_o_norm_w"],
                 fox_f_bias=small1["fox_f_bias"], fox_q_norm_w=small1["fox_q_norm_w"], fox_k_norm_w=small1["fox_k_norm_w"])
    big = dict(w_mem_kv=dwkv, dn_w_in=big0["w_in"], fox_w_in=big1["w_in"], conv_w=big0["conv_w"],
               w_out=[big0["w_out"], big1["w_out"]], w_mlp1=[big0["w_mlp1"], big1["w_mlp1"]],
               w_mlp2=[big0["w_mlp2"], big1["w_mlp2"]])
    return loss, dx0, big, small


def w_in_to_kernel(w, n_scalars):
    pad = jnp.zeros((w.shape[0], PROJ_W - TAIL - n_scalars), w.dtype)
    return jnp.concatenate([w[:, :4096], w[:, 4096 + n_scalars:], w[:, 4096:4096 + n_scalars], pad], axis=1)


def w_in_from_kernel(w, n_scalars):
    return jnp.concatenate([w[:, :4096], w[:, TAIL:TAIL + n_scalars], w[:, 4096:TAIL]], axis=1)


BIG_SPECS = dict(w_mem_kv=("rows", 1, 256, 1024), w_out=("rows", 2, 384, 1024), w_mlp2=("rows", 2, 1024, 1024),
                 w_mlp1=("cols", 2, 1024, 1024), dn_w_in=("rows", 1, 1024, 1156), fox_w_in=("rows", 1, 1024, 1154))
BIG_NAMES = tuple(BIG_SPECS)


def _full_shape(name, half=False):
    kind, a, b, c = BIG_SPECS[name]
    b = b // 2 if half else b
    return (a, N_CHIP, b, c) if kind == "rows" else (a, b, N_CHIP * c)


def _ds(start, size, align):
    return pl.ds(start if isinstance(start, int) else pl.multiple_of(start, align), size)


def _half_rows(name, h):
    b = BIG_SPECS[name][2]
    return _ds(h * (b // 2), b // 2, 16)


def _shard_idx(name, h):
    return (slice(None), _half_rows(name, h), slice(None))


def _full_idx(name, j=None, h=None):
    kind, _, _, c = BIG_SPECS[name]
    rows = slice(None) if h is None else _half_rows(name, h)
    if kind == "rows":
        return (slice(None), slice(None) if j is None else j, rows, slice(None))
    return (slice(None), rows, slice(None) if j is None else _ds(j * c, c, 128))


def _row_block(name):
    hs = BIG_SPECS[name][2] // 2
    return hs if hs <= ROWS else ROWS


def _remote(src, dst, send_sem, recv_sem, to):
    return pltpu.make_async_remote_copy(src_ref=src, dst_ref=dst, send_sem=send_sem, recv_sem=recv_sem, device_id=to,
                                        device_id_type=MESH)


def all_gather_big(shards):
    n = len(BIG_NAMES)

    def body(*refs):
        ins, outs = refs[:n], refs[n:2 * n]
        send_sems, recv_sems, fsend_sems, frecv_sems = refs[2 * n:]
        x, y, c, chips = _place()
        me_chip, sibling = 2 * x + y, (x, y, 1 - c)
        work = [(3 * a + k, a, name, chip) for a, name in enumerate(BIG_NAMES) for k, chip in enumerate(chips)]
        sends = []
        for s, a, name, chip in work:
            cp = _remote(ins[a].at[_shard_idx(name, c)], outs[a].at[_full_idx(name, me_chip, c)], send_sems.at[s],
                         recv_sems.at[s], (chip[0], chip[1], c))
            cp.start()
            sends.append(cp)
        for s, a, name, chip in work:
            landed = outs[a].at[_full_idx(name, 2 * chip[0] + chip[1], c)]
            _remote(landed, landed, send_sems.at[s], recv_sems.at[s], (chip[0], chip[1], c)).wait_recv()
            cp = _remote(landed, landed, fsend_sems.at[s], frecv_sems.at[s], sibling)
            cp.start()
            sends.append(cp)
        for s, a, name, chip in work:
            passed = outs[a].at[_full_idx(name, 2 * chip[0] + chip[1], 1 - c)]
            _remote(passed, passed, fsend_sems.at[s], frecv_sems.at[s], sibling).wait_recv()
        for cp in sends:
            cp.wait_send()

    outs = pl.pallas_call(
        body, name="all_gather_big", in_specs=[ANY] * n, out_specs=[ANY] * n,
        out_shape=[jax.ShapeDtypeStruct(_full_shape(name), MM) for name in BIG_NAMES],
        scratch_shapes=[pltpu.SemaphoreType.DMA((3 * n,))] * 4,
    )(*[shards[name] for name in BIG_NAMES])
    return dict(zip(BIG_NAMES, outs))


def with_own_slot(name, full, shard, chip):
    kind, _, _, c = BIG_SPECS[name]
    if kind == "rows":
        return lax.dynamic_update_slice(full, shard[:, None], (0, chip, 0, 0))
    return lax.dynamic_update_slice(full, shard, (0, 0, chip * c))


def rs_pair_exchange_big(grads):
    n = len(BIG_NAMES)

    def body(*refs):
        ins, outs, send_sems, recv_sems = refs[:n], refs[n:2 * n], refs[2 * n], refs[2 * n + 1]
        x, y, c, _ = _place()
        cps = []
        for a, name in enumerate(BIG_NAMES):
            cp = _remote(ins[a].at[_full_idx(name, None, 1 - c)], outs[a], send_sems.at[a], recv_sems.at[a], (x, y, 1 - c))
            cp.start()
            cps.append(cp)
        for cp in cps:
            cp.wait()

    outs = pl.pallas_call(
        body, name="rs_pair_exchange_big", in_specs=[ANY] * n, out_specs=[ANY] * n,
        out_shape=[jax.ShapeDtypeStruct(_full_shape(name, half=True), F32) for name in BIG_NAMES],
        scratch_shapes=[pltpu.SemaphoreType.DMA((n,)), pltpu.SemaphoreType.DMA((n,))],
    )(*[grads[name] for name in BIG_NAMES])
    return dict(zip(BIG_NAMES, outs))


def rs_pair_add_big(name, place, g, got):
    kind, a_, b_, c_ = BIG_SPECS[name]
    rb = _row_block(name)
    nb = (b_ // 2) // rb

    def body(place_ref, g_ref, got_ref, o_ref):
        o_ref[...] = (g_ref[...] + got_ref[...]).astype(o_ref.dtype)

    if kind == "rows":
        g_spec = pl.BlockSpec((None, None, rb, c_), lambda a, j, i, p: (a, j, p[0] * nb + i, 0))
        o_spec = pl.BlockSpec((None, None, rb, c_), lambda a, j, i, p: (a, j, i, 0))
    else:
        g_spec = pl.BlockSpec((None, rb, c_), lambda a, j, i, p: (a, p[0] * nb + i, j))
        o_spec = pl.BlockSpec((None, rb, c_), lambda a, j, i, p: (a, i, j))
    return pl.pallas_call(
        body, name="rs_pair_add_" + name,
        grid_spec=pltpu.PrefetchScalarGridSpec(num_scalar_prefetch=1, grid=(a_, N_CHIP, nb), in_specs=[g_spec, o_spec],
                                               out_specs=o_spec),
        out_shape=jax.ShapeDtypeStruct(_full_shape(name, half=True), MM),
        compiler_params=_cparams(("parallel", "parallel", "parallel")),
    )(place, g, got)


def rs_chip_exchange_big(pairs):
    n = len(BIG_NAMES)

    def body(*refs):
        ins, outs, send_sems, recv_sems = refs[:n], refs[n:2 * n], refs[2 * n], refs[2 * n + 1]
        x, y, c, chips = _place()
        cps = []
        for a, name in enumerate(BIG_NAMES):
            for k, chip in enumerate(chips):
                s = 3 * a + k
                cp = _remote(ins[a].at[_full_idx(name, 2 * chip[0] + chip[1])], outs[a].at[k], send_sems.at[s],
                             recv_sems.at[s], (chip[0], chip[1], c))
                cp.start()
                cps.append(cp)
        for cp in cps:
            cp.wait()

    def got_shape(name):
        _, a_, b_, c_ = BIG_SPECS[name]
        return jax.ShapeDtypeStruct((3, a_, b_ // 2, c_), MM)

    outs = pl.pallas_call(
        body, name="rs_chip_exchange_big", in_specs=[ANY] * n, out_specs=[ANY] * n,
        out_shape=[got_shape(name) for name in BIG_NAMES],
        scratch_shapes=[pltpu.SemaphoreType.DMA((3 * n,)), pltpu.SemaphoreType.DMA((3 * n,))],
    )(*[pairs[name] for name in BIG_NAMES])
    return dict(zip(BIG_NAMES, outs))


def rs_chip_add_big(name, place, g, got_pair, got_chips):
    kind, a_, b_, c_ = BIG_SPECS[name]
    rb = _row_block(name)
    nb = (b_ // 2) // rb

    def body(place_ref, g_ref, s_ref, r0_ref, r1_ref, r2_ref, o_ref):
        own = g_ref[...] + s_ref[...]
        o_ref[...] = ((own + r0_ref[...].astype(F32)) + r1_ref[...].astype(F32)) + r2_ref[...].astype(F32)

    if kind == "rows":
        g_spec = pl.BlockSpec((None, None, rb, c_), lambda a, i, p: (a, p[1], p[0] * nb + i, 0))
        s_spec = pl.BlockSpec((None, None, rb, c_), lambda a, i, p: (a, p[1], i, 0))
    else:
        g_spec = pl.BlockSpec((None, rb, c_), lambda a, i, p: (a, p[0] * nb + i, p[1]))
        s_spec = pl.BlockSpec((None, rb, c_), lambda a, i, p: (a, i, p[1]))
    r_spec = lambda k: pl.BlockSpec((None, None, rb, c_), lambda a, i, p: (k, a, i, 0))
    return pl.pallas_call(
        body, name="rs_chip_add_" + name,
        grid_spec=pltpu.PrefetchScalarGridSpec(
            num_scalar_prefetch=1, grid=(a_, nb), in_specs=[g_spec, s_spec, r_spec(0), r_spec(1), r_spec(2)],
            out_specs=pl.BlockSpec((None, rb, c_), lambda a, i, p: (a, p[0] * nb + i, 0))),
        out_shape=jax.ShapeDtypeStruct((a_, b_, c_), F32), compiler_params=_cparams(("parallel", "parallel")),
    )(place, g, got_pair, got_chips, got_chips, got_chips)


def rs_pair_gather_big(halves):
    n = len(BIG_NAMES)

    def body(*refs):
        outs, send_sems, recv_sems = refs[n:2 * n], refs[2 * n], refs[2 * n + 1]
        x, y, c, _ = _place()
        cps = []
        for a, name in enumerate(BIG_NAMES):
            mine = outs[a].at[_shard_idx(name, c)]
            cp = _remote(mine, mine, send_sems.at[a], recv_sems.at[a], (x, y, 1 - c))
            cp.start()
            cps.append(cp)
        for a, name in enumerate(BIG_NAMES):
            cps[a].wait_send()
            theirs = outs[a].at[_shard_idx(name, 1 - c)]
            _remote(theirs, theirs, send_sems.at[a], recv_sems.at[a], (x, y, 1 - c)).wait_recv()

    outs = pl.pallas_call(
        body, name="rs_pair_gather_big", in_specs=[ANY] * n, out_specs=[ANY] * n,
        input_output_aliases={a: a for a in range(n)},
        out_shape=[jax.ShapeDtypeStruct(BIG_SPECS[name][1:], F32) for name in BIG_NAMES],
        scratch_shapes=[pltpu.SemaphoreType.DMA((n,)), pltpu.SemaphoreType.DMA((n,))],
    )(*[halves[name] for name in BIG_NAMES])
    return dict(zip(BIG_NAMES, outs))


def reduce_scatter_big(grads, place):
    got_pair = rs_pair_exchange_big(grads)
    pairs = {name: rs_pair_add_big(name, place, grads[name], got_pair[name]) for name in BIG_NAMES}
    got_chips = rs_chip_exchange_big(pairs)
    return rs_pair_gather_big({name: rs_chip_add_big(name, place, grads[name], got_pair[name], got_chips[name])
                               for name in BIG_NAMES})


PACK_W = 1024
SMALL =(("mem_norm_w", 1024), ("mem_k_norm_w", 128), ("norm1_w", 2048), ("dn_a_log", 8), ("dn_dt_bias", 8),
         ("dn_o_norm_w", 128), ("fox_f_bias", 8), ("fox_q_norm_w", 128), ("fox_k_norm_w", 128), ("memq_norm_w", 256),
         ("norm2_w", 2048))
SMALL_ROWS = 8
CONV_ROWS = 4 * 3 * D_MODEL // PACK_W
LOSS_AT = sum(n for _, n in SMALL)


def pack_small(parts, extra=None):
    flat = [parts[name].astype(F32).reshape(-1) for name, _ in SMALL]
    used = LOSS_AT
    if extra is not None:
        flat.append(extra.reshape(1))
        used += 1
    flat.append(jnp.zeros((SMALL_ROWS * PACK_W - used,), F32))
    return jnp.concatenate(flat).reshape(SMALL_ROWS, PACK_W)


def unpack_small(packed, shapes):
    flat, out, at = packed.reshape(-1), {}, 0
    for name, n in SMALL:
        out[name] = flat[at:at + n].reshape(shapes[name])
        at += n
    return out


def _adam_all(w, g, m, v, name):
    shape = w.shape
    r2 = lambda a: a.reshape(-1, shape[-1])
    d, nm, nv = adamw(r2(w), r2(g), r2(m), r2(v), name=name)
    return d.reshape(shape), nm.reshape(shape), nv.reshape(shape)


BIG = ("w_mem_kv", "dn_w_in", "dn_conv_w", "fox_w_in", "w_out", "w_mlp1", "w_mlp2")
WEIGHTS = ("mem_norm_w", "w_mem_kv", "mem_k_norm_w", "norm1_w", "dn_w_in", "dn_conv_w", "dn_a_log", "dn_dt_bias",
           "dn_o_norm_w", "fox_w_in", "fox_f_bias", "fox_q_norm_w", "fox_k_norm_w", "memq_norm_w", "w_out", "norm2_w",
           "w_mlp1", "w_mlp2")


def kernel(x, mem, mem_norm_w, w_mem_kv, mem_k_norm_w, norm1_w, dn_w_in, dn_conv_w, dn_a_log, dn_dt_bias, dn_o_norm_w, fox_w_in, fox_f_bias, fox_q_norm_w, fox_k_norm_w, memq_norm_w, w_out, norm2_w, w_mlp1, w_mlp2, loss_target, m_mem_norm_w, m_w_mem_kv, m_mem_k_norm_w, m_norm1_w, m_dn_w_in, m_dn_conv_w, m_dn_a_log, m_dn_dt_bias, m_dn_o_norm_w, m_fox_w_in, m_fox_f_bias, m_fox_q_norm_w, m_fox_k_norm_w, m_memq_norm_w, m_w_out, m_norm2_w, m_w_mlp1, m_w_mlp2, v_mem_norm_w, v_w_mem_kv, v_mem_k_norm_w, v_norm1_w, v_dn_w_in, v_dn_conv_w, v_dn_a_log, v_dn_dt_bias, v_dn_o_norm_w, v_fox_w_in, v_fox_f_bias, v_fox_q_norm_w, v_fox_k_norm_w, v_memq_norm_w, v_w_out, v_norm2_w, v_w_mlp1, v_w_mlp2):
    args = dict(locals())
    w = {n: args[n] for n in WEIGHTS}
    m = {n: args["m_" + n] for n in WEIGHTS}
    v = {n: args["v_" + n] for n in WEIGHTS}
    core, chip = lax.axis_index("c"), 2 * lax.axis_index("x") + lax.axis_index("y")
    place = jnp.stack([core, chip]).astype(jnp.int32)

    shards = {name: w[name].reshape(BIG_SPECS[name][1:]).astype(MM) for name in BIG_NAMES}
    full = {name: with_own_slot(name, arr, shards[name], chip) for name, arr in all_gather_big(shards).items()}
    w_in_full = lambda name, n_scalars: w_in_to_kernel(
        full[name][0].transpose(1, 0, 2).reshape(D_MODEL, -1), n_scalars)
    conv_mine = jnp.where(core == 0, dn_conv_w[0], 0.0)
    conv_placed = lax.dynamic_update_slice(jnp.zeros((4, 3 * D_MODEL), F32), conv_mine, (0, 768 * chip))
    conv_full = all_reduce_small(jnp.pad(conv_placed.reshape(CONV_ROWS, PACK_W), ((0, 16 - CONV_ROWS), (0, 0))))
    wt = dict(w_mem_kv=full["w_mem_kv"].reshape(D_MODEL, 2 * MEM_WIDTH), dn_w_in=w_in_full("dn_w_in", 2 * N_HEADS),
              fox_w_in=w_in_full("fox_w_in", N_HEADS), conv_w=conv_full[:CONV_ROWS].reshape(4, 3 * D_MODEL),
              w_out=full["w_out"].reshape(2, 3 * MEM_WIDTH, D_MODEL), w_mlp1=full["w_mlp1"],
              w_mlp2=full["w_mlp2"].reshape(2, D_FF, D_MODEL))
    sm = dict(mem_norm_w=mem_norm_w, mem_k_norm_w=mem_k_norm_w, norm1_w=norm1_w, norm2_w=norm2_w, memq_norm_w=memq_norm_w,
              dn_a_log=dn_a_log[0], dn_dt_bias=dn_dt_bias[0], dn_o_norm_w=dn_o_norm_w, fox_f_bias=fox_f_bias[0],
              fox_q_norm_w=fox_q_norm_w, fox_k_norm_w=fox_k_norm_w)
    loss_part, dx, big, small = local_step(x[0], mem[0], loss_target[0], wt, sm)

    w_in_slots = lambda g, n_scalars: w_in_from_kernel(g, n_scalars).reshape(D_MODEL, N_CHIP, -1).transpose(1, 0, 2)[None]
    full_grads = dict(w_mem_kv=big["w_mem_kv"].reshape(_full_shape("w_mem_kv")),
                      w_out=jnp.stack(big["w_out"]).reshape(_full_shape("w_out")),
                      w_mlp2=jnp.stack(big["w_mlp2"]).reshape(_full_shape("w_mlp2")), w_mlp1=jnp.stack(big["w_mlp1"]),
                      dn_w_in=w_in_slots(big["dn_w_in"], 2 * N_HEADS), fox_w_in=w_in_slots(big["fox_w_in"], N_HEADS))
    big_sum = reduce_scatter_big(full_grads, place)
    small_pack = jnp.concatenate([pack_small(small, loss_part[0, :1]), big["conv_w"].reshape(CONV_ROWS, PACK_W),
                                  jnp.zeros((24 - SMALL_ROWS - CONV_ROWS, PACK_W), F32)])
    small_all = all_reduce_small(small_pack)
    small_sum = small_all[:SMALL_ROWS]
    conv_sum = lax.dynamic_slice(small_all[SMALL_ROWS:SMALL_ROWS + CONV_ROWS].reshape(4, 3 * D_MODEL), (0, 768 * chip), (4, 768))
    loss = small_sum.reshape(-1)[LOSS_AT]
    grads = unpack_small(small_sum, {n: w[n].shape for n, _ in SMALL})
    grads.update({name: big_sum[name].reshape(w[name].shape) for name in BIG_NAMES}, dn_conv_w=conv_sum[None])

    delta, new_m, new_v = {}, {}, {}
    for n in BIG:
        delta[n], new_m[n], new_v[n] = _adam_all(w[n], grads[n], m[n], v[n], "adamw_" + n)
    shapes = {n: w[n].shape for n, _ in SMALL}
    d_s, m_s, v_s = adamw(pack_small(w), small_sum, pack_small(m), pack_small(v), name="adamw_small")
    for out, packed in ((delta, d_s), (new_m, m_s), (new_v, v_s)):
        out.update(unpack_small(packed, shapes))
    return (loss, dx[None], *[grads[n] for n in WEIGHTS], *[delta[n] for n in WEIGHTS],
            *[new_m[n] for n in WEIGHTS], *[new_v[n] for n in WEIGHTS])
```

```python
import functools

import jax
import jax.numpy as jnp
from jax import lax
from jax.experimental import pallas as pl
from jax.experimental.pallas import tpu as pltpu

F32 = jnp.float32
MM = jnp.bfloat16
HI = lax.Precision.HIGHEST

D_MODEL = 1024
HEAD_DIM = 128
N_HEADS = 8
MEM_HEADS = 4
MEM_WIDTH = MEM_HEADS * HEAD_DIM
N_MEM = 256
D_FF = 4 * D_MODEL
CHUNK = 64
EPS = 1e-6
QSCALE = HEAD_DIM ** -0.5
PROJ_W = 4736
TAIL = 4608
TAIL_BLK = TAIL // HEAD_DIM
ROWS = 256
VMEM_LIMIT = 56 * 1024 * 1024

ADAM_LR = 0.001
ADAM_B1 = 0.9
ADAM_B2 = 0.999
ADAM_EPS = 1e-08
ADAM_WD = 0.01
ADAM_STEP = 10

N_DEV = 8
N_CHIP = 4
MESH = pl.DeviceIdType.MESH


def _cparams(sem=None):
    return pltpu.CompilerParams(dimension_semantics=sem, vmem_limit_bytes=VMEM_LIMIT)


def _dot(a, b, ca, cb, hi):
    dims = (((ca,), (cb,)), ((), ()))
    if hi:
        return lax.dot_general(a, b, dims, precision=HI, preferred_element_type=F32)
    return lax.dot_general(a.astype(MM), b.astype(MM), dims, preferred_element_type=F32)


@functools.partial(jax.custom_vjp, nondiff_argnums=(2, 3, 4))
def mmul(a, b, ca, cb, hi):
    return _dot(a, b, ca, cb, hi)


def _mmul_fwd(a, b, ca, cb, hi):
    return _dot(a, b, ca, cb, hi), (a, b)


def _mmul_bwd(ca, cb, hi, res, g):
    a, b = res
    if ca == 1:
        da = _dot(g, b, 1, 1, hi) if cb == 0 else _dot(g, b, 1, 0, hi)
    else:
        da = _dot(b, g, 1, 1, hi) if cb == 0 else _dot(b, g, 0, 1, hi)
    if cb == 0:
        db = _dot(a, g, 0, 0, hi) if ca == 1 else _dot(a, g, 1, 0, hi)
    else:
        db = _dot(g, a, 0, 0, hi) if ca == 1 else _dot(g, a, 0, 1, hi)
    return da.astype(a.dtype), db.astype(b.dtype)


mmul.defvjp(_mmul_fwd, _mmul_bwd)


def _iota2(n, m):
    return lax.broadcasted_iota(jnp.int32, (n, m), 0), lax.broadcasted_iota(jnp.int32, (n, m), 1)


def _same_block(r, c, shift):
    return lax.shift_right_logical(r, shift) == lax.shift_right_logical(c, shift)


def _split_bf16(x):
    hi = x.astype(jnp.bfloat16)
    return hi, (x - hi.astype(F32)).astype(jnp.bfloat16)


def _dot3(a, b, ca, cb):
    dims = (((ca,), (cb,)), ((), ()))
    (ah, al), (bh, bl) = _split_bf16(a), _split_bf16(b)
    d = lambda x, y: lax.dot_general(x, y, dims, preferred_element_type=F32)
    return d(ah, bh) + (d(ah, bl) + d(al, bh))


def _tri_inv_impl(a):
    n = a.shape[0]
    r, c = _iota2(n, n)
    eye = (r == c).astype(F32)
    b16, b32 = _same_block(r, c, 4), _same_block(r, c, 5)
    a0 = jnp.where(b16, a, 0.0)
    p = eye - a0
    b = _dot3(a0, a0, 1, 0)
    p = p + _dot3(p, b, 1, 0)
    b = _dot3(b, b, 1, 0)
    p = p + _dot3(p, b, 1, 0)
    b = _dot3(b, b, 1, 0)
    p = p + _dot3(p, b, 1, 0)
    a1 = jnp.where(jnp.logical_and(b32, jnp.logical_not(b16)), a, 0.0)
    p = p - _dot3(_dot3(p, a1, 1, 0), p, 1, 0)
    a2 = jnp.where(b32, 0.0, a)
    p = p - _dot3(_dot3(p, a2, 1, 0), p, 1, 0)
    return p


@jax.custom_vjp
def tri_inv(a):
    return _tri_inv_impl(a)


def _tri_inv_fwd(a):
    p = _tri_inv_impl(a)
    return p, p


def _tri_inv_bwd(p, g):
    return (-_dot3(_dot3(p, g, 0, 0), p, 1, 1),)


tri_inv.defvjp(_tri_inv_fwd, _tri_inv_bwd)


def _sigmoid(x):
    return 1.0 / (1.0 + jnp.exp(-x))


def _softplus(x):
    return jnp.maximum(x, 0.0) + jnp.log(1.0 + jnp.exp(-jnp.abs(x)))


def _silu(x):
    return x * _sigmoid(x)


def _rms(x, w):
    return x * lax.rsqrt(jnp.mean(x * x, axis=-1, keepdims=True) + EPS) * w


def _bf_round(x):
    return x.astype(MM).astype(F32)


def _acc(ref, val, first):
    @pl.when(first)
    def _():
        ref[...] = val

    @pl.when(jnp.logical_not(first))
    def _():
        ref[...] += val


def _tile(n, pref):
    if n % pref == 0:
        return pref
    return n


def matmul(a, b, *, ta=False, tb=False, res=None, out_dtype=F32, name, tm=1024, tn=1024, tk=1024):
    m, k = (a.shape[1], a.shape[0]) if ta else a.shape
    n = b.shape[0] if tb else b.shape[1]
    assert (b.shape[1] if tb else b.shape[0]) == k, (a.shape, b.shape, ta, tb)
    tm, tn, tk = _tile(m, tm), _tile(n, tn), _tile(k, tk)
    nk = k // tk
    ca, cb = (0 if ta else 1), (1 if tb else 0)

    def body(a_ref, b_ref, *rest):
        r_ref = rest[0] if res is not None else None
        o_ref, acc_ref = rest[-2:]
        kk = pl.program_id(2)
        part = _dot(a_ref[...], b_ref[...], ca, cb, False)

        @pl.when(kk == 0)
        def _():
            acc_ref[...] = part

        @pl.when(kk > 0)
        def _():
            acc_ref[...] += part

        @pl.when(kk == nk - 1)
        def _():
            total = acc_ref[...] if r_ref is None else acc_ref[...] + r_ref[...]
            o_ref[...] = total.astype(o_ref.dtype)

    a_spec = pl.BlockSpec((tk, tm), lambda i, j, l: (l, i)) if ta else pl.BlockSpec((tm, tk), lambda i, j, l: (i, l))
    b_spec = pl.BlockSpec((tn, tk), lambda i, j, l: (j, l)) if tb else pl.BlockSpec((tk, tn), lambda i, j, l: (l, j))
    o_spec = pl.BlockSpec((tm, tn), lambda i, j, l: (i, j))
    extra = () if res is None else (res,)
    return pl.pallas_call(
        body, name=name, grid=(m // tm, n // tn, nk),
        in_specs=[a_spec, b_spec] + [o_spec] * len(extra), out_specs=o_spec,
        out_shape=jax.ShapeDtypeStruct((m, n), out_dtype),
        scratch_shapes=[pltpu.VMEM((tm, tn), F32)],
        compiler_params=_cparams(("parallel", "parallel", "arbitrary")),
    )(a, b, *extra)


def rms_fwd(x, w, *, name):
    t, d = x.shape

    def body(x_ref, w_ref, o_ref):
        o_ref[...] = _rms(x_ref[...], w_ref[...]).astype(o_ref.dtype)

    return pl.pallas_call(
        body, name=name, grid=(t // ROWS,),
        in_specs=[pl.BlockSpec((ROWS, d), lambda i: (i, 0)), pl.BlockSpec((1, d), lambda i: (0, 0))],
        out_specs=pl.BlockSpec((ROWS, d), lambda i: (i, 0)),
        out_shape=jax.ShapeDtypeStruct((t, d), MM), compiler_params=_cparams(("parallel",)),
    )(x, w)


def rms_bwd(x, w, dh, dres, *, name):
    t, d = x.shape

    def body(x_ref, w_ref, dh_ref, dr_ref, dx_ref, dw_ref):
        _, vjp = jax.vjp(_rms, x_ref[...], w_ref[...])
        dx, dw = vjp(dh_ref[...].astype(F32))
        dx_ref[...] = dx + dr_ref[...]
        _acc(dw_ref, dw, pl.program_id(0) == 0)

    row = pl.BlockSpec((ROWS, d), lambda i: (i, 0))
    vec = pl.BlockSpec((1, d), lambda i: (0, 0))
    return pl.pallas_call(
        body, name=name, grid=(t // ROWS,), in_specs=[row, vec, row, row], out_specs=[row, vec],
        out_shape=[jax.ShapeDtypeStruct((t, d), F32), jax.ShapeDtypeStruct((1, d), F32)],
        compiler_params=_cparams(("arbitrary",)),
    )(x, w, dh, dres)


def _sqrelu(x):
    return jnp.square(jnp.maximum(x, 0.0))


def act_fwd(ff, *, name):
    t, f = ff.shape

    def body(x_ref, o_ref):
        o_ref[...] = _sqrelu(x_ref[...]).astype(o_ref.dtype)

    blk = pl.BlockSpec((ROWS, f), lambda i: (i, 0))
    return pl.pallas_call(body, name=name, grid=(t // ROWS,), in_specs=[blk], out_specs=blk,
                          out_shape=jax.ShapeDtypeStruct((t, f), MM), compiler_params=_cparams(("parallel",)))(ff)


def act_bwd(ff, dact, *, name):
    t, f = ff.shape

    def body(x_ref, g_ref, o_ref):
        o_ref[...] = (g_ref[...] * 2.0 * jnp.maximum(x_ref[...], 0.0)).astype(o_ref.dtype)

    blk = pl.BlockSpec((ROWS, f), lambda i: (i, 0))
    return pl.pallas_call(body, name=name, grid=(t // ROWS,), in_specs=[blk, blk], out_specs=blk,
                          out_shape=jax.ShapeDtypeStruct((t, f), MM), compiler_params=_cparams(("parallel",)))(ff, dact)


def loss_fwd(y, target, *, name):
    t, d = y.shape

    def body(y_ref, t_ref, dy_ref, l_ref):
        e = y_ref[...] - t_ref[...]
        dy_ref[...] = e * (1.0 / d)
        part = 0.5 * jnp.sum(jnp.sum(e * e, axis=-1, keepdims=True) * (1.0 / d), axis=0, keepdims=True)
        _acc(l_ref, jnp.broadcast_to(part, (1, HEAD_DIM)), pl.program_id(0) == 0)

    blk = pl.BlockSpec((ROWS, d), lambda i: (i, 0))
    return pl.pallas_call(
        body, name=name, grid=(t // ROWS,), in_specs=[blk, blk],
        out_specs=[blk, pl.BlockSpec((1, HEAD_DIM), lambda i: (0, 0))],
        out_shape=[jax.ShapeDtypeStruct((t, d), F32), jax.ShapeDtypeStruct((1, HEAD_DIM), F32)],
        compiler_params=_cparams(("arbitrary",)),
    )(y, target)


def _mem_kv(mem, wn, wkn, *ws):
    mn = _rms(mem, wn)
    outs = []
    for h in range(MEM_HEADS):
        outs.append(_rms(mmul(mn, ws[h], 1, 0, False), wkn))
    for h in range(MEM_HEADS):
        outs.append(mmul(mn, ws[MEM_HEADS + h], 1, 0, False))
    return tuple(outs)


def _w_cols(w_ref):
    return [w_ref[:, h * HEAD_DIM:(h + 1) * HEAD_DIM] for h in range(2 * MEM_HEADS)]


def mem_fwd(mem, wn, wkv, wkn):
    def body(mem_ref, wn_ref, w_ref, wkn_ref, k_ref, v_ref):
        outs = _mem_kv(mem_ref[...], wn_ref[...], wkn_ref[...], *_w_cols(w_ref))
        for h in range(MEM_HEADS):
            k_ref[:, h * HEAD_DIM:(h + 1) * HEAD_DIM] = outs[h]
            v_ref[:, h * HEAD_DIM:(h + 1) * HEAD_DIM] = outs[MEM_HEADS + h]

    shp = jax.ShapeDtypeStruct((mem.shape[0], MEM_WIDTH), F32)
    return pl.pallas_call(body, name="mem_fwd", out_shape=[shp, shp], compiler_params=_cparams())(mem, wn, wkv, wkn)


def mem_bwd(mem, wn, wkv, wkn, dk0, dv0, dk1, dv1):
    def body(mem_ref, wn_ref, w_ref, wkn_ref, dk0_ref, dv0_ref, dk1_ref, dv1_ref, dwn_ref, dw_ref, dwkn_ref):
        _, vjp = jax.vjp(lambda wn_, wkn_, *ws: _mem_kv(mem_ref[...], wn_, wkn_, *ws),
                         wn_ref[...], wkn_ref[...], *[w.astype(F32) for w in _w_cols(w_ref)])
        cols = lambda a, b: tuple(a[:, h * HEAD_DIM:(h + 1) * HEAD_DIM] + b[:, h * HEAD_DIM:(h + 1) * HEAD_DIM]
                                  for h in range(MEM_HEADS))
        cts = cols(dk0_ref, dk1_ref) + cols(dv0_ref, dv1_ref)
        grads = vjp(cts)
        dwn_ref[...] = grads[0]
        dwkn_ref[...] = grads[1]
        for h in range(2 * MEM_HEADS):
            dw_ref[:, h * HEAD_DIM:(h + 1) * HEAD_DIM] = grads[2 + h]

    return pl.pallas_call(
        body, name="mem_bwd",
        out_shape=[jax.ShapeDtypeStruct((1, D_MODEL), F32), jax.ShapeDtypeStruct((D_MODEL, 2 * MEM_WIDTH), F32),
                   jax.ShapeDtypeStruct((1, HEAD_DIM), F32)],
        compiler_params=_cparams(),
    )(mem, wn, wkv, wkn, dk0, dv0, dk1, dv1)


def _memattn(q, wq, mk, mv):
    qn = _rms(q, wq) * QSCALE
    s = mmul(qn, mk, 1, 1, False)
    s = s - jnp.max(s, axis=-1, keepdims=True)
    p = jnp.exp(s)
    p = p / jnp.sum(p, axis=-1, keepdims=True)
    return mmul(p, mv, 1, 0, False)


def _lanes(j):
    return slice(j * HEAD_DIM, (j + 1) * HEAD_DIM)


def _memattn_specs(t):
    qspec = pl.BlockSpec((ROWS, MEM_WIDTH), lambda i: (i, (TAIL - MEM_WIDTH) // MEM_WIDTH))
    wspec = pl.BlockSpec((1, HEAD_DIM), lambda i: (0, 0))
    mspec = pl.BlockSpec((N_MEM, MEM_WIDTH), lambda i: (0, 0))
    ospec = pl.BlockSpec((ROWS, MEM_WIDTH), lambda i: (i, 0))
    return qspec, wspec, mspec, ospec


def memattn_fwd(proj, wq, mk, mv, *, name):
    t = proj.shape[0]
    qspec, wspec, mspec, ospec = _memattn_specs(t)

    def body(q_ref, w_ref, k_ref, v_ref, o_ref):
        for h in range(MEM_HEADS):
            o_ref[:, _lanes(h)] = _memattn(q_ref[:, _lanes(h)], w_ref[...], k_ref[:, _lanes(h)],
                                           v_ref[:, _lanes(h)]).astype(o_ref.dtype)

    return pl.pallas_call(
        body, name=name, grid=(t // ROWS,), in_specs=[qspec, wspec, mspec, mspec], out_specs=ospec,
        out_shape=jax.ShapeDtypeStruct((t, MEM_WIDTH), MM), compiler_params=_cparams(("parallel",)),
    )(proj, wq, mk, mv)


def memattn_bwd(proj, wq, mk, mv, dcat, *, name):
    t = proj.shape[0]
    qspec, wspec, mspec, ospec = _memattn_specs(t)
    dospec = pl.BlockSpec((ROWS, MEM_WIDTH), lambda i: (i, D_MODEL // MEM_WIDTH))

    def body(q_ref, w_ref, k_ref, v_ref, do_ref, dq_ref, dw_ref, dk_ref, dv_ref):
        first = pl.program_id(0) == 0
        dw_sum = jnp.zeros((1, HEAD_DIM), F32)
        for h in range(MEM_HEADS):
            _, vjp = jax.vjp(_memattn, q_ref[:, _lanes(h)], w_ref[...], k_ref[:, _lanes(h)], v_ref[:, _lanes(h)])
            dq, dw, dk, dv = vjp(do_ref[:, _lanes(h)].astype(F32))
            dq_ref[:, _lanes(h)] = dq.astype(dq_ref.dtype)
            dw_sum = dw_sum + dw
            _acc(dk_ref.at[:, _lanes(h)], dk, first)
            _acc(dv_ref.at[:, _lanes(h)], dv, first)
        _acc(dw_ref, dw_sum, first)

    mshape = jax.ShapeDtypeStruct((N_MEM, MEM_WIDTH), F32)
    return pl.pallas_call(
        body, name=name, grid=(t // ROWS,), in_specs=[qspec, wspec, mspec, mspec, dospec],
        out_specs=[ospec, wspec, mspec, mspec],
        out_shape=[jax.ShapeDtypeStruct((t, MEM_WIDTH), MM), jax.ShapeDtypeStruct((1, HEAD_DIM), F32), mshape, mshape],
        compiler_params=_cparams(("arbitrary",)),
    )(proj, wq, mk, mv, dcat)


def _shift_rows(x, s, up):
    n = x.shape[0]
    r = lax.broadcasted_iota(jnp.int32, x.shape, 0)
    if up:
        return jnp.where(r < n - s, pltpu.roll(x, n - s, 0), 0.0)
    return jnp.where(r >= s, pltpu.roll(x, s, 0), 0.0)


def _conv_fwd_vals(x, w):
    xb = _bf_round(x)
    wb = _bf_round(w)
    c = xb * wb[3:4, :]
    for j in range(3):
        c = c + _shift_rows(xb, 3 - j, False) * wb[j:j + 1, :]
    return xb, wb, c


def dn_prep_fwd(proj, conv_w):
    t = proj.shape[0]

    def body(x_ref, w_ref, o_ref):
        j = pl.program_id(0)
        _, _, c = _conv_fwd_vals(x_ref[...], w_ref[...])
        s = _silu(c)
        r = lax.rsqrt(jnp.sum(s * s, axis=-1, keepdims=True) + EPS)
        scale = jnp.where(j < N_HEADS, QSCALE, 1.0)
        o_ref[...] = jnp.where(j < 2 * N_HEADS, s * r * scale, s)

    return pl.pallas_call(
        body, name="dn_prep_fwd", grid=(3 * N_HEADS,),
        in_specs=[pl.BlockSpec((t, HEAD_DIM), lambda j: (0, j)), pl.BlockSpec((4, HEAD_DIM), lambda j: (0, j))],
        out_specs=pl.BlockSpec((None, t, HEAD_DIM), lambda j: (j // N_HEADS, 0, j % N_HEADS)),
        out_shape=jax.ShapeDtypeStruct((3, t, D_MODEL), F32), compiler_params=_cparams(("parallel",)),
    )(proj, conv_w)


def dn_prep_bwd(proj, conv_w, dqkv):
    t = proj.shape[0]

    def body(x_ref, w_ref, g_ref, dx_ref, dw_ref):
        j = pl.program_id(0)
        xb, wb, c = _conv_fwd_vals(x_ref[...], w_ref[...])
        sg = _sigmoid(c)
        s = c * sg
        g = g_ref[...]
        r = lax.rsqrt(jnp.sum(s * s, axis=-1, keepdims=True) + EPS)
        scale = jnp.where(j < N_HEADS, QSCALE, 1.0)
        gn = g * scale
        ds_norm = r * gn - s * (r * r * r) * jnp.sum(gn * s, axis=-1, keepdims=True)
        ds = jnp.where(j < 2 * N_HEADS, ds_norm, g)
        dc = ds * (sg + s * (1.0 - sg))
        dx = dc * wb[3:4, :]
        rows = [jnp.sum(dc * xb, axis=0, keepdims=True)]
        for jj in range(2, -1, -1):
            sh = 3 - jj
            dx = dx + _shift_rows(dc, sh, True) * wb[jj:jj + 1, :]
            rows.insert(0, jnp.sum(dc * _shift_rows(xb, sh, False), axis=0, keepdims=True))
        dx_ref[...] = dx.astype(dx_ref.dtype)
        dw_ref[...] = jnp.concatenate(rows + [jnp.zeros((4, HEAD_DIM), F32)], axis=0)

    col = pl.BlockSpec((t, HEAD_DIM), lambda j: (0, j))
    return pl.pallas_call(
        body, name="dn_prep_bwd", grid=(3 * N_HEADS,),
        in_specs=[col, pl.BlockSpec((4, HEAD_DIM), lambda j: (0, j)),
                  pl.BlockSpec((None, t, HEAD_DIM), lambda j: (j // N_HEADS, 0, j % N_HEADS))],
        out_specs=[col, pl.BlockSpec((8, HEAD_DIM), lambda j: (0, j))],
        out_shape=[jax.ShapeDtypeStruct((t, 3 * D_MODEL), MM), jax.ShapeDtypeStruct((8, 3 * D_MODEL), F32)],
        compiler_params=_cparams(("parallel",)),
    )(proj, conv_w, dqkv)


def _tri_ones(n, upper):
    r, c = _iota2(n, n)
    return (r <= c).astype(F32) if upper else (r >= c).astype(F32)


def dn_gates_fwd(proj, a_log, dt_bias):
    t = proj.shape[0]

    def body(x_ref, al_ref, dt_ref, o_ref):
        lane = lax.broadcasted_iota(jnp.int32, (CHUNK, HEAD_DIM), 1)
        tri = _tri_ones(CHUNK, False)

        def step(c, carry):
            rows = pl.ds(pl.multiple_of(c * CHUNK, CHUNK), CHUNK)
            x = x_ref[rows, :]
            g = jnp.where(lane < N_HEADS, -jnp.exp(al_ref[...]) * _softplus(x + dt_ref[...]), 0.0)
            gc = _dot(tri, g, 1, 0, True)
            o_ref[rows, :] = jnp.where(lane < N_HEADS, gc, jnp.where(lane < 2 * N_HEADS, _sigmoid(x), 0.0))
            return carry

        lax.fori_loop(0, t // CHUNK, step, 0)

    vec = pl.BlockSpec((1, HEAD_DIM), lambda i: (0, 0))
    return pl.pallas_call(
        body, name="dn_gates_fwd", grid=(1,),
        in_specs=[pl.BlockSpec((t, HEAD_DIM), lambda i: (0, TAIL_BLK)), vec, vec],
        out_specs=pl.BlockSpec((t, HEAD_DIM), lambda i: (0, 0)),
        out_shape=jax.ShapeDtypeStruct((t, HEAD_DIM), F32), compiler_params=_cparams(("arbitrary",)),
    )(proj, a_log, dt_bias)


def dn_gates_bwd(proj, a_log, dt_bias, dgates):
    t = proj.shape[0]

    def body(x_ref, al_ref, dt_ref, g_ref, dx_ref, dal_ref, ddt_ref):
        lane = lax.broadcasted_iota(jnp.int32, (CHUNK, HEAD_DIM), 1)
        tri = _tri_ones(CHUNK, True)
        dal_ref[...] = jnp.zeros_like(dal_ref)
        ddt_ref[...] = jnp.zeros_like(ddt_ref)

        def step(c, carry):
            rows = pl.ds(pl.multiple_of(c * CHUNK, CHUNK), CHUNK)
            x = x_ref[rows, :]
            dgc = jnp.where(lane < N_HEADS, g_ref[rows, :], 0.0)
            dg = _dot(tri, dgc, 1, 0, True)
            ea = -jnp.exp(al_ref[...])
            z = x + dt_ref[...]
            da = jnp.where(lane < N_HEADS, dg * ea * _sigmoid(z), 0.0)
            gval = jnp.where(lane < N_HEADS, ea * _softplus(z), 0.0)
            beta = _sigmoid(x)
            db = jnp.where(jnp.logical_and(lane >= N_HEADS, lane < 2 * N_HEADS), g_ref[rows, :] * beta * (1.0 - beta), 0.0)
            dx_ref[rows, :] = (da + db).astype(dx_ref.dtype)
            dal_ref[...] += jnp.sum(dg * gval, axis=0, keepdims=True)
            ddt_ref[...] += jnp.sum(da, axis=0, keepdims=True)
            return carry

        lax.fori_loop(0, t // CHUNK, step, 0)

    vec = pl.BlockSpec((1, HEAD_DIM), lambda i: (0, 0))
    full = pl.BlockSpec((t, HEAD_DIM), lambda i: (0, 0))
    return pl.pallas_call(
        body, name="dn_gates_bwd", grid=(1,),
        in_specs=[pl.BlockSpec((t, HEAD_DIM), lambda i: (0, TAIL_BLK)), vec, vec, full],
        out_specs=[full, vec, vec],
        out_shape=[jax.ShapeDtypeStruct((t, HEAD_DIM), MM), jax.ShapeDtypeStruct((1, HEAD_DIM), F32),
                   jax.ShapeDtypeStruct((1, HEAD_DIM), F32)],
        compiler_params=_cparams(("arbitrary",)),
    )(proj, a_log, dt_bias, dgates)


def _dn_intra(q, k, v, gcol, grow, bcol):
    r, c = _iota2(CHUNK, CHUNK)
    causal, strict = r >= c, r > c
    decay = jnp.where(causal, jnp.exp(jnp.where(causal, gcol - grow, 0.0)), 0.0)
    kb = k * bcol
    a = jnp.where(strict, mmul(kb, k, 1, 1, False) * decay, 0.0)
    tm = tri_inv(a)
    u = mmul(tm, v * bcol, 1, 0, False)
    w = mmul(tm, kb * jnp.exp(gcol), 1, 0, False)
    qk = jnp.where(causal, mmul(q, k, 1, 1, False) * decay, 0.0)
    rr = lax.broadcasted_iota(jnp.int32, (CHUNK, 1), 0)
    g_last = jnp.sum(jnp.where(rr == CHUNK - 1, gcol, 0.0), axis=0, keepdims=True)
    return u, w, q * jnp.exp(gcol), k * jnp.exp(g_last - gcol), qk, jnp.exp(g_last)


def _dn_scan(u, w, qg, kd, qk, eg, state):
    v_new = u - mmul(w, state, 1, 0, False)
    out = mmul(qg, state, 1, 0, False) + mmul(qk, v_new, 1, 0, False)
    return out, state * eg + mmul(kd, v_new, 0, 0, False)


DN_HEADS_PER_STEP = 1
DN_GROUP = 8
DN_PARTS = ((CHUNK, HEAD_DIM),) * 4 + ((CHUNK, CHUNK), (1, 1))


def _dn_scratch(hb, nc):
    return [pltpu.VMEM((hb, nc) + shape, F32) for shape in DN_PARTS]


def _dn_group(nc):
    return min(DN_GROUP, nc)


def _dn_group_args(refs, j, g, grp):
    q_ref, k_ref, v_ref, gc_ref, gr_ref, bc_ref = refs
    rows = pl.ds(pl.multiple_of(g * (grp * CHUNK), grp * CHUNK), grp * CHUNK)
    cs = pl.ds(g * grp, grp)
    split = lambda ref: ref[rows, _lanes(j)].reshape(grp, CHUNK, HEAD_DIM)
    return split(q_ref), split(k_ref), split(v_ref), gc_ref[j, cs], gr_ref[j, cs], bc_ref[j, cs]


def _dn_intra_all(refs, parts, hb, nc):
    grp = _dn_group(nc)

    def group(g, carry):
        cs = pl.ds(g * grp, grp)
        for j in range(hb):
            for part, val in zip(parts, jax.vmap(_dn_intra)(*_dn_group_args(refs, j, g, grp))):
                part[j, cs] = val
        return carry

    lax.fori_loop(0, nc // grp, group, 0)


def _dn_specs(t):
    nc, hb = t // CHUNK, DN_HEADS_PER_STEP
    head = lambda which: pl.BlockSpec((None, t, hb * HEAD_DIM), lambda h: (which, 0, h))
    flat = pl.BlockSpec((t, hb * HEAD_DIM), lambda h: (0, h))
    col = pl.BlockSpec((hb, nc, CHUNK, 1), lambda h: (h, 0, 0, 0))
    row = pl.BlockSpec((hb, nc, 1, CHUNK), lambda h: (h, 0, 0, 0))
    st = pl.BlockSpec((hb, nc, HEAD_DIM, HEAD_DIM), lambda h: (h, 0, 0, 0))
    return nc, hb, head, flat, col, row, st


def dn_core_fwd(qkv, gcol, grow, bcol):
    t = qkv.shape[1]
    nc, hb, head, flat, col, row, st = _dn_specs(t)

    def body(q_ref, k_ref, v_ref, gc_ref, gr_ref, bc_ref, o_ref, s_ref, *parts):
        _dn_intra_all((q_ref, k_ref, v_ref, gc_ref, gr_ref, bc_ref), parts, hb, nc)

        def step(c, states):
            rows = pl.ds(pl.multiple_of(c * CHUNK, CHUNK), CHUNK)
            new_states = []
            for j in range(hb):
                s_ref[j, c] = states[j]
                out, new_state = _dn_scan(*[part[j, c] for part in parts], states[j])
                o_ref[rows, _lanes(j)] = out
                new_states.append(new_state)
            return tuple(new_states)

        lax.fori_loop(0, nc, step, tuple(jnp.zeros((HEAD_DIM, HEAD_DIM), F32) for _ in range(hb)))

    return pl.pallas_call(
        body, name="dn_core_fwd", grid=(N_HEADS // hb,),
        in_specs=[head(0), head(1), head(2), col, row, col], out_specs=[flat, st],
        out_shape=[jax.ShapeDtypeStruct((t, D_MODEL), F32), jax.ShapeDtypeStruct((N_HEADS, nc, HEAD_DIM, HEAD_DIM), F32)],
        scratch_shapes=_dn_scratch(hb, nc), compiler_params=_cparams(("parallel",)),
    )(qkv, qkv, qkv, gcol, grow, bcol)


def dn_core_bwd(qkv, gcol, grow, bcol, states, do):
    t = qkv.shape[1]
    nc, hb, head, flat, col, row, st = _dn_specs(t)

    def body(q_ref, k_ref, v_ref, gc_ref, gr_ref, bc_ref, s_ref, do_ref, dqkv_ref, dgc_ref, dgr_ref, dbc_ref, *scratch):
        parts, dparts = scratch[:len(DN_PARTS)], scratch[len(DN_PARTS):]
        refs = (q_ref, k_ref, v_ref, gc_ref, gr_ref, bc_ref)
        _dn_intra_all(refs, parts, hb, nc)

        def step(i, dstates):
            c = nc - 1 - i
            rows = pl.ds(pl.multiple_of(c * CHUNK, CHUNK), CHUNK)
            dstates_in = []
            for j in range(hb):
                _, vjp = jax.vjp(_dn_scan, *[part[j, c] for part in parts], s_ref[j, c])
                *dvals, dstate_in = vjp((do_ref[rows, _lanes(j)], dstates[j]))
                for dpart, dval in zip(dparts, dvals):
                    dpart[j, c] = dval
                dstates_in.append(dstate_in)
            return tuple(dstates_in)

        lax.fori_loop(0, nc, step, tuple(jnp.zeros((HEAD_DIM, HEAD_DIM), F32) for _ in range(hb)))

        grp = _dn_group(nc)

        def group(g, carry):
            rows = pl.ds(pl.multiple_of(g * (grp * CHUNK), grp * CHUNK), grp * CHUNK)
            cs = pl.ds(g * grp, grp)
            for j in range(hb):
                _, vjp = jax.vjp(jax.vmap(_dn_intra), *_dn_group_args(refs, j, g, grp))
                dq, dk, dv, dgc, dgr, dbc = vjp(tuple(dpart[j, cs] for dpart in dparts))
                for which, val in enumerate((dq, dk, dv)):
                    dqkv_ref[which, rows, _lanes(j)] = val.reshape(grp * CHUNK, HEAD_DIM)
                dgc_ref[j, cs] = dgc
                dgr_ref[j, cs] = dgr
                dbc_ref[j, cs] = dbc
            return carry

        lax.fori_loop(0, nc // grp, group, 0)

    return pl.pallas_call(
        body, name="dn_core_bwd", grid=(N_HEADS // hb,), scratch_shapes=_dn_scratch(hb, nc) * 2,
        in_specs=[head(0), head(1), head(2), col, row, col, st, flat],
        out_specs=[pl.BlockSpec((3, t, hb * HEAD_DIM), lambda h: (0, 0, h)), col, row, col],
        out_shape=[jax.ShapeDtypeStruct((3, t, D_MODEL), F32)] + [
            jax.ShapeDtypeStruct((N_HEADS, nc, CHUNK, 1), F32), jax.ShapeDtypeStruct((N_HEADS, nc, 1, CHUNK), F32),
            jax.ShapeDtypeStruct((N_HEADS, nc, CHUNK, 1), F32)],
        compiler_params=_cparams(("parallel",)),
    )(qkv, qkv, qkv, gcol, grow, bcol, states, do)


def gates_to_heads(gates):
    t = gates.shape[0]
    nc = t // CHUNK
    g = gates[:, :N_HEADS].T.reshape(N_HEADS, nc, CHUNK)
    b = gates[:, N_HEADS:2 * N_HEADS].T.reshape(N_HEADS, nc, CHUNK)
    return g[..., None], g[:, :, None, :], b[..., None]


def heads_to_gates(dgcol, dgrow, dbcol):
    nh, nc = dgcol.shape[:2]
    dg = (dgcol[..., 0] + dgrow[:, :, 0, :]).reshape(nh, nc * CHUNK).T
    db = dbcol[..., 0].reshape(nh, nc * CHUNK).T
    return jnp.concatenate([dg, db, jnp.zeros((nc * CHUNK, HEAD_DIM - 2 * nh), F32)], axis=1)


def _dn_out(o, z, w):
    return _rms(o, w) * _silu(z)


def _gate_specs():
    o_spec = pl.BlockSpec((ROWS, D_MODEL), lambda i: (i, 0))
    z_spec = pl.BlockSpec((ROWS, D_MODEL), lambda i: (i, 3))
    w_spec = pl.BlockSpec((1, HEAD_DIM), lambda i: (0, 0))
    return o_spec, z_spec, w_spec


def dn_out_fwd(o, proj, w):
    t = o.shape[0]
    o_spec, z_spec, w_spec = _gate_specs()

    def body(o_ref, z_ref, w_ref, y_ref):
        for h in range(N_HEADS):
            y_ref[:, _lanes(h)] = _dn_out(o_ref[:, _lanes(h)], z_ref[:, _lanes(h)], w_ref[...]).astype(y_ref.dtype)

    return pl.pallas_call(
        body, name="dn_out_fwd", grid=(t // ROWS,), in_specs=[o_spec, z_spec, w_spec], out_specs=o_spec,
        out_shape=jax.ShapeDtypeStruct((t, D_MODEL), MM), compiler_params=_cparams(("parallel",)),
    )(o, proj, w)


def dn_out_bwd(o, proj, w, dcat):
    t = o.shape[0]
    o_spec, z_spec, w_spec = _gate_specs()

    def body(o_ref, z_ref, w_ref, g_ref, do_ref, dz_ref, dw_ref):
        dw_sum = jnp.zeros((1, HEAD_DIM), F32)
        for h in range(N_HEADS):
            _, vjp = jax.vjp(_dn_out, o_ref[:, _lanes(h)], z_ref[:, _lanes(h)], w_ref[...])
            do, dz, dw = vjp(g_ref[:, _lanes(h)].astype(F32))
            do_ref[:, _lanes(h)] = do
            dz_ref[:, _lanes(h)] = dz.astype(dz_ref.dtype)
            dw_sum = dw_sum + dw
        _acc(dw_ref, dw_sum, pl.program_id(0) == 0)

    return pl.pallas_call(
        body, name="dn_out_bwd", grid=(t // ROWS,), in_specs=[o_spec, z_spec, w_spec, o_spec],
        out_specs=[o_spec, o_spec, w_spec],
        out_shape=[jax.ShapeDtypeStruct((t, D_MODEL), F32), jax.ShapeDtypeStruct((t, D_MODEL), MM),
                   jax.ShapeDtypeStruct((1, HEAD_DIM), F32)],
        compiler_params=_cparams(("arbitrary",)),
    )(o, proj, w, dcat)


def _fox_norm(x, w, scale):
    return _rms(x, w) * scale


def _fox_prep_specs():
    x_spec = pl.BlockSpec((ROWS, 2 * D_MODEL), lambda i: (i, 0))
    w_spec = pl.BlockSpec((2, 1, HEAD_DIM), lambda i: (0, 0, 0))
    y_spec = pl.BlockSpec((2, ROWS, D_MODEL), lambda i: (0, i, 0))
    return x_spec, w_spec, y_spec


def fox_prep_fwd(proj, wqk):
    t = proj.shape[0]
    x_spec, w_spec, y_spec = _fox_prep_specs()

    def body(x_ref, w_ref, y_ref):
        for j in range(2 * N_HEADS):
            which, scale = j // N_HEADS, (QSCALE if j < N_HEADS else 1.0)
            y_ref[which, :, _lanes(j % N_HEADS)] = _fox_norm(x_ref[:, _lanes(j)], w_ref[which], scale).astype(y_ref.dtype)

    return pl.pallas_call(
        body, name="fox_prep_fwd", grid=(t // ROWS,), in_specs=[x_spec, w_spec], out_specs=y_spec,
        out_shape=jax.ShapeDtypeStruct((2, t, D_MODEL), MM), compiler_params=_cparams(("parallel",)),
    )(proj, wqk)


def fox_prep_bwd(proj, wqk, dq, dk):
    t = proj.shape[0]
    x_spec, w_spec, _ = _fox_prep_specs()
    g_spec = pl.BlockSpec((ROWS, D_MODEL), lambda i: (i, 0))

    def body(x_ref, w_ref, dq_ref, dk_ref, dx_ref, dw_ref):
        dws = [jnp.zeros((1, HEAD_DIM), F32), jnp.zeros((1, HEAD_DIM), F32)]
        for j in range(2 * N_HEADS):
            which, scale = j // N_HEADS, (QSCALE if j < N_HEADS else 1.0)
            g_ref = dq_ref if which == 0 else dk_ref
            _, vjp = jax.vjp(lambda x, w: _fox_norm(x, w, scale), x_ref[:, _lanes(j)], w_ref[which])
            dx, dw = vjp(g_ref[:, _lanes(j % N_HEADS)])
            dx_ref[:, _lanes(j)] = dx.astype(dx_ref.dtype)
            dws[which] = dws[which] + dw
        first = pl.program_id(0) == 0
        _acc(dw_ref.at[0], dws[0], first)
        _acc(dw_ref.at[1], dws[1], first)

    return pl.pallas_call(
        body, name="fox_prep_bwd", grid=(t // ROWS,), in_specs=[x_spec, w_spec, g_spec, g_spec],
        out_specs=[x_spec, w_spec],
        out_shape=[jax.ShapeDtypeStruct((t, 2 * D_MODEL), MM), jax.ShapeDtypeStruct((2, 1, HEAD_DIM), F32)],
        compiler_params=_cparams(("arbitrary",)),
    )(proj, wqk, dq, dk)


def _row_pick(x, i):
    r = lax.broadcasted_iota(jnp.int32, x.shape, 0)
    return jnp.sum(jnp.where(r == i, x, 0.0), axis=0, keepdims=True)


def fox_gates_fwd(proj, f_bias):
    t = proj.shape[0]
    blk = HEAD_DIM

    def body(x_ref, b_ref, o_ref):
        lane = lax.broadcasted_iota(jnp.int32, (blk, HEAD_DIM), 1)
        tri = _tri_ones(blk, False)

        def step(c, carry):
            rows = pl.ds(pl.multiple_of(c * blk, blk), blk)
            lf = jnp.where(lane < N_HEADS, -_softplus(-(x_ref[rows, :] + b_ref[...])), 0.0)
            cum = _dot(tri, lf, 1, 0, True) + carry
            o_ref[rows, :] = cum
            return _row_pick(cum, blk - 1)

        lax.fori_loop(0, t // blk, step, jnp.zeros((1, HEAD_DIM), F32))

    vec = pl.BlockSpec((1, HEAD_DIM), lambda i: (0, 0))
    return pl.pallas_call(
        body, name="fox_gates_fwd", grid=(1,),
        in_specs=[pl.BlockSpec((t, HEAD_DIM), lambda i: (0, TAIL_BLK)), vec],
        out_specs=pl.BlockSpec((t, HEAD_DIM), lambda i: (0, 0)),
        out_shape=jax.ShapeDtypeStruct((t, HEAD_DIM), F32), compiler_params=_cparams(("arbitrary",)),
    )(proj, f_bias)


def fox_gates_bwd(proj, f_bias, dfcum):
    t = proj.shape[0]
    blk = HEAD_DIM
    nb = t // blk

    def body(x_ref, b_ref, g_ref, dx_ref, db_ref):
        lane = lax.broadcasted_iota(jnp.int32, (blk, HEAD_DIM), 1)
        tri = _tri_ones(blk, True)
        db_ref[...] = jnp.zeros_like(db_ref)

        def step(i, carry):
            c = nb - 1 - i
            rows = pl.ds(pl.multiple_of(c * blk, blk), blk)
            g = jnp.where(lane < N_HEADS, g_ref[rows, :], 0.0)
            dlf = _dot(tri, g, 1, 0, True) + carry
            dx = jnp.where(lane < N_HEADS, dlf * _sigmoid(-(x_ref[rows, :] + b_ref[...])), 0.0)
            dx_ref[rows, :] = dx.astype(dx_ref.dtype)
            db_ref[...] += jnp.sum(dx, axis=0, keepdims=True)
            return carry + jnp.sum(g, axis=0, keepdims=True)

        lax.fori_loop(0, nb, step, jnp.zeros((1, HEAD_DIM), F32))

    vec = pl.BlockSpec((1, HEAD_DIM), lambda i: (0, 0))
    full = pl.BlockSpec((t, HEAD_DIM), lambda i: (0, 0))
    return pl.pallas_call(
        body, name="fox_gates_bwd", grid=(1,),
        in_specs=[pl.BlockSpec((t, HEAD_DIM), lambda i: (0, TAIL_BLK)), vec, full], out_specs=[full, vec],
        out_shape=[jax.ShapeDtypeStruct((t, HEAD_DIM), MM), jax.ShapeDtypeStruct((1, HEAD_DIM), F32)],
        compiler_params=_cparams(("arbitrary",)),
    )(proj, f_bias, dfcum)


def fcum_to_heads(fcum):
    f = fcum[:, :N_HEADS].T
    return f[:, :, None], f[:, None, :]


def heads_to_fcum(dfcol, dfrow):
    d = (dfcol[:, :, 0] + dfrow[:, 0, :]).T
    return jnp.concatenate([d, jnp.zeros((d.shape[0], HEAD_DIM - N_HEADS), F32)], axis=1)


def _fox_tq(t):
    return min(t, 256)


def _fox_specs(t):
    tq = _fox_tq(t)
    q_spec = pl.BlockSpec((None, tq, HEAD_DIM), lambda h, i: (0, i, h))
    k_spec = pl.BlockSpec((None, t, HEAD_DIM), lambda h, i: (1, 0, h))
    v_spec = pl.BlockSpec((t, HEAD_DIM), lambda h, i: (0, 2 * N_HEADS + h))
    gate_spec = pl.BlockSpec((tq, HEAD_DIM), lambda h, i: (i, 3 * N_HEADS + h))
    col_spec = pl.BlockSpec((None, tq, 1), lambda h, i: (h, i, 0))
    row_spec = pl.BlockSpec((None, 1, t), lambda h, i: (h, 0, 0))
    blk_spec = pl.BlockSpec((tq, HEAD_DIM), lambda h, i: (i, h))
    head_spec = pl.BlockSpec((t, HEAD_DIM), lambda h, i: (0, h))
    return tq, q_spec, k_spec, v_spec, gate_spec, col_spec, row_spec, blk_spec, head_spec


def _fox_scores(q, k, fcol, frow, i, tq, t):
    s = _dot(q, k, 1, 1, False) + (fcol - frow)
    r = lax.broadcasted_iota(jnp.int32, (tq, t), 0) + i * tq
    c = lax.broadcasted_iota(jnp.int32, (tq, t), 1)
    return s, c <= r


def fox_attn_fwd(qk, proj, fcol, frow):
    t = proj.shape[0]
    tq, q_spec, k_spec, v_spec, gate_spec, col_spec, row_spec, blk_spec, _ = _fox_specs(t)

    def body(q_ref, k_ref, v_ref, gate_ref, fc_ref, fr_ref, mix_ref, o_ref, lse_ref):
        def block(i):
            w = (i + 1) * tq
            s, mask = _fox_scores(q_ref[...], k_ref[0:w, :], fc_ref[...], fr_ref[:, 0:w], i, tq, w)
            s = jnp.where(mask, s, -1e30)
            m = jnp.max(s, axis=-1, keepdims=True)
            p = jnp.where(mask, jnp.exp(s - m), 0.0)
            l = jnp.sum(p, axis=-1, keepdims=True)
            o = _dot(p, v_ref[0:w, :], 1, 0, False) / l
            o_ref[...] = o
            mix_ref[...] = (o * _sigmoid(gate_ref[...])).astype(mix_ref.dtype)
            lse_ref[...] = m + jnp.log(l)

        for i in range(t // tq):
            pl.when(pl.program_id(1) == i)(functools.partial(block, i))

    return pl.pallas_call(
        body, name="fox_attn_fwd", grid=(N_HEADS, t // tq),
        in_specs=[q_spec, k_spec, v_spec, gate_spec, col_spec, row_spec], out_specs=[blk_spec, blk_spec, col_spec],
        out_shape=[jax.ShapeDtypeStruct((t, D_MODEL), MM), jax.ShapeDtypeStruct((t, D_MODEL), F32),
                   jax.ShapeDtypeStruct((N_HEADS, t, 1), F32)],
        compiler_params=_cparams(("parallel", "parallel")),
    )(qk, qk, proj, proj, fcol, frow)


def fox_attn_bwd(qk, proj, fcol, frow, o, lse, dcat):
    t = proj.shape[0]
    tq, q_spec, k_spec, v_spec, gate_spec, col_spec, row_spec, blk_spec, head_spec = _fox_specs(t)

    def body(q_ref, k_ref, v_ref, gate_ref, fc_ref, fr_ref, o_ref, lse_ref, g_ref,
             dq_ref, dk_ref, dv_ref, dgate_ref, dfc_ref, dfr_ref):
        @pl.when(pl.program_id(1) == 0)
        def _():
            dk_ref[...] = jnp.zeros_like(dk_ref)
            dv_ref[...] = jnp.zeros_like(dv_ref)
            dfr_ref[...] = jnp.zeros_like(dfr_ref)

        def block(i):
            w = (i + 1) * tq
            sg = _sigmoid(gate_ref[...])
            g = g_ref[...].astype(F32)
            o_pre = o_ref[...]
            do = g * sg
            dgate_ref[...] = (g * o_pre * sg * (1.0 - sg)).astype(dgate_ref.dtype)
            s, mask = _fox_scores(q_ref[...], k_ref[0:w, :], fc_ref[...], fr_ref[:, 0:w], i, tq, w)
            p = jnp.where(mask, jnp.exp(jnp.where(mask, s, 0.0) - lse_ref[...]), 0.0)
            dp = _dot(do, v_ref[0:w, :], 1, 1, False)
            delta = jnp.sum(do * o_pre, axis=-1, keepdims=True)
            ds = p * (dp - delta)
            dq_ref[...] = _dot(ds, k_ref[0:w, :], 1, 0, False)
            dk_ref[0:w, :] += _dot(ds, q_ref[...], 0, 0, False)
            dv_ref[0:w, :] += _dot(p, do, 0, 0, False)
            dfc_ref[...] = jnp.sum(ds, axis=-1, keepdims=True)
            dfr_ref[:, 0:w] += -jnp.sum(ds, axis=0, keepdims=True)

        for i in range(t // tq):
            pl.when(pl.program_id(1) == i)(functools.partial(block, i))

    f32 = lambda *s: jax.ShapeDtypeStruct(s, F32)
    return pl.pallas_call(
        body, name="fox_attn_bwd", grid=(N_HEADS, t // tq),
        in_specs=[q_spec, k_spec, v_spec, gate_spec, col_spec, row_spec, blk_spec, col_spec, blk_spec],
        out_specs=[blk_spec, head_spec, head_spec, blk_spec, col_spec, row_spec],
        out_shape=[f32(t, D_MODEL), f32(t, D_MODEL), f32(t, D_MODEL), jax.ShapeDtypeStruct((t, D_MODEL), MM),
                   f32(N_HEADS, t, 1), f32(N_HEADS, 1, t)],
        compiler_params=_cparams(("parallel", "arbitrary")),
    )(qk, qk, proj, proj, fcol, frow, o, lse, dcat)


def adamw(w, g, m, v, *, name):
    r, c = w.shape
    rb = ROWS if r % ROWS == 0 else r

    def body(w_ref, g_ref, m_ref, v_ref, d_ref, nm_ref, nv_ref):
        g_ = g_ref[...]
        m_ = ADAM_B1 * m_ref[...] + (1.0 - ADAM_B1) * g_
        v_ = ADAM_B2 * v_ref[...] + (1.0 - ADAM_B2) * jnp.square(g_)
        m_hat = m_ / (1.0 - ADAM_B1 ** ADAM_STEP)
        v_hat = v_ / (1.0 - ADAM_B2 ** ADAM_STEP)
        d_ref[...] = -ADAM_LR * (m_hat / (jnp.sqrt(v_hat) + ADAM_EPS) + ADAM_WD * w_ref[...])
        nm_ref[...] = m_
        nv_ref[...] = v_

    blk = pl.BlockSpec((rb, c), lambda i: (i, 0))
    shp = jax.ShapeDtypeStruct((r, c), F32)
    return pl.pallas_call(body, name=name, grid=(r // rb,), in_specs=[blk] * 4, out_specs=[blk] * 3,
                          out_shape=[shp] * 3, compiler_params=_cparams(("parallel",)))(w, g, m, v)


def _place():
    x, y, c = lax.axis_index("x"), lax.axis_index("y"), lax.axis_index("c")
    return x, y, c, [(1 - x, y), (x, 1 - y), (1 - x, 1 - y)]


ANY = pl.BlockSpec(memory_space=pl.ANY)


def all_reduce_small(v):
    r, w = v.shape

    def body(v_ref, o_ref, buf, send_sems, recv_sems):
        x, y, c, _ = _place()
        me = 4 * x + 2 * y + c
        flip = lambda a, bit: 1 - a if bit else a
        cps = []
        for k in range(1, N_DEV):
            peer = (flip(x, k & 4), flip(y, k & 2), flip(c, k & 1))
            cp = pltpu.make_async_remote_copy(src_ref=v_ref, dst_ref=buf.at[me], send_sem=send_sems.at[k - 1],
                                              recv_sem=recv_sems.at[k - 1], device_id=peer, device_id_type=MESH)
            cp.start()
            cps.append((cp, 4 * peer[0] + 2 * peer[1] + peer[2]))
        buf[me] = v_ref[...]
        for k, (cp, peer_id) in enumerate(cps):
            pltpu.make_async_remote_copy(src_ref=v_ref, dst_ref=buf.at[peer_id], send_sem=send_sems.at[k],
                                         recv_sem=recv_sems.at[k], device_id=(x, y, c), device_id_type=MESH).wait_recv()
        for cp, _ in cps:
            cp.wait_send()
        acc = buf[0]
        for d in range(1, N_DEV):
            acc = acc + buf[d]
        o_ref[...] = acc

    vm = pl.BlockSpec(memory_space=pltpu.VMEM)
    return pl.pallas_call(
        body, name="all_reduce_small", in_specs=[vm], out_specs=vm, out_shape=jax.ShapeDtypeStruct((r, w), F32),
        scratch_shapes=[pltpu.VMEM((N_DEV, r, w), F32), pltpu.SemaphoreType.DMA((N_DEV - 1,)),
                        pltpu.SemaphoreType.DMA((N_DEV - 1,))],
    )(v)


def _vec8(v):
    return jnp.zeros((1, HEAD_DIM), F32).at[0, :N_HEADS].set(v.reshape(N_HEADS))


def _layer_fwd(i, x_in, wt, sm, mem_k, mem_v, late=None):
    tag = f"l{i}_"
    h = rms_fwd(x_in, sm["norm1_w"][i][None], name=tag + "rms1")
    w_in = wt["dn_w_in"] if i == 0 else wt["fox_w_in"]
    proj = matmul(h, w_in, name=tag + "proj", tm=256, tk=1024)
    sv = dict(x_in=x_in, h=h, proj=proj)
    if i == 0:
        qkv = dn_prep_fwd(proj, wt["conv_w"])
        gates = dn_gates_fwd(proj, _vec8(sm["dn_a_log"]), _vec8(sm["dn_dt_bias"]))
        gcol, grow, bcol = gates_to_heads(gates)
        o, states = dn_core_fwd(qkv, gcol, grow, bcol)
        mix = dn_out_fwd(o, proj, sm["dn_o_norm_w"])
        sv.update(qkv=qkv, gcol=gcol, grow=grow, bcol=bcol, states=states, o=o)
    else:
        wqk = jnp.stack([sm["fox_q_norm_w"], sm["fox_k_norm_w"]])
        qk = fox_prep_fwd(proj, wqk)
        fcum = fox_gates_fwd(proj, _vec8(sm["fox_f_bias"]))
        fcol, frow = fcum_to_heads(fcum)
        mix, o, lse = fox_attn_fwd(qk, proj, fcol, frow)
        sv.update(wqk=wqk, qk=qk, fcol=fcol, frow=frow, o=o, lse=lse)
    mem_out = memattn_fwd(proj, sm["memq_norm_w"][i][None], mem_k, mem_v, name=tag + "memattn_fwd")
    cat = jnp.concatenate([mix, mem_out], axis=1)
    if late is not None:
        wt.update(late(cat))
    x_mid = matmul(cat, wt["w_out"][i], res=x_in, name=tag + "out_proj")
    h2 = rms_fwd(x_mid, sm["norm2_w"][i][None], name=tag + "rms2")
    ff = matmul(h2, wt["w_mlp1"][i], name=tag + "mlp1")
    act = act_fwd(ff, name=tag + "act_fwd")
    x_out = matmul(act, wt["w_mlp2"][i], res=x_mid, name=tag + "mlp2")
    sv.update(cat=cat, x_mid=x_mid, h2=h2, ff=ff, act=act)
    return x_out, sv


def _layer_bwd(i, dx_out, sv, wt, sm, mem_k, mem_v):
    tag = f"l{i}_"
    big, small = {}, {}
    dact = matmul(dx_out, wt["w_mlp2"][i], tb=True, name=tag + "d_act")
    big["w_mlp2"] = matmul(sv["act"], dx_out, ta=True, name=tag + "d_w_mlp2")
    dff = act_bwd(sv["ff"], dact, name=tag + "act_bwd")
    dh2 = matmul(dff, wt["w_mlp1"][i], tb=True, name=tag + "d_h2")
    big["w_mlp1"] = matmul(sv["h2"], dff, ta=True, name=tag + "d_w_mlp1", tm=512, tn=D_FF, tk=512)
    dx_mid, small["norm2_w"] = rms_bwd(sv["x_mid"], sm["norm2_w"][i][None], dh2, dx_out, name=tag + "rms2_bwd")
    dcat = matmul(dx_mid, wt["w_out"][i], tb=True, name=tag + "d_cat")
    big["w_out"] = matmul(sv["cat"], dx_mid, ta=True, name=tag + "d_w_out")
    proj = sv["proj"]
    dqm, small["memq_norm_w"], dmk, dmv = memattn_bwd(proj, sm["memq_norm_w"][i][None], mem_k, mem_v, dcat,
                                                      name=tag + "memattn_bwd")
    t = proj.shape[0]
    pad = jnp.zeros((t, PROJ_W - TAIL - HEAD_DIM), MM)
    if i == 0:
        do, dz, small["dn_o_norm_w"] = dn_out_bwd(sv["o"], proj, sm["dn_o_norm_w"], dcat)
        dqkv, dgc, dgr, dbc = dn_core_bwd(sv["qkv"], sv["gcol"], sv["grow"], sv["bcol"], sv["states"], do)
        dtail, dal, ddt = dn_gates_bwd(proj, _vec8(sm["dn_a_log"]), _vec8(sm["dn_dt_bias"]), heads_to_gates(dgc, dgr, dbc))
        dmain, dconv = dn_prep_bwd(proj, wt["conv_w"], dqkv)
        small["dn_a_log"], small["dn_dt_bias"] = dal[:, :N_HEADS], ddt[:, :N_HEADS]
        big["conv_w"] = dconv[:4]
        dproj = jnp.concatenate([dmain, dz, dqm, dtail, pad], axis=1)
    else:
        dq, dk, dv, dgate, dfc, dfr = fox_attn_bwd(sv["qk"], proj, sv["fcol"], sv["frow"], sv["o"], sv["lse"], dcat)
        dtail, dfb = fox_gates_bwd(proj, _vec8(sm["fox_f_bias"]), heads_to_fcum(dfc, dfr))
        dqk, dwqk = fox_prep_bwd(proj, sv["wqk"], dq, dk)
        small["fox_f_bias"] = dfb[:, :N_HEADS]
        small["fox_q_norm_w"], small["fox_k_norm_w"] = dwqk[0], dwqk[1]
        dproj = jnp.concatenate([dqk, dv.astype(MM), dgate, dqm, dtail, pad], axis=1)
    w_in = wt["dn_w_in"] if i == 0 else wt["fox_w_in"]
    dh = matmul(dproj, w_in, tb=True, name=tag + "d_h", tm=512)
    big["w_in"] = matmul(sv["h"], dproj, ta=True, name=tag + "d_w_in", tm=256)
    dx_in, small["norm1_w"] = rms_bwd(sv["x_in"], sm["norm1_w"][i][None], dh, dx_mid, name=tag + "rms1_bwd")
    return dx_in, big, small, (dmk, dmv)


def local_step(x, mem, target, wt, sm, late=None):
    wt = dict(wt)
    mem_k, mem_v = mem_fwd(mem, sm["mem_norm_w"][None], wt["w_mem_kv"], sm["mem_k_norm_w"][None])
    x0, sv0 = _layer_fwd(0, x, wt, sm, mem_k, mem_v, late)
    x1, sv1 = _layer_fwd(1, x0, wt, sm, mem_k, mem_v)
    dy, loss = loss_fwd(x1, target, name="loss")
    dx1, big1, small1, dm1 = _layer_bwd(1, dy, sv1, wt, sm, mem_k, mem_v)
    dx0, big0, small0, dm0 = _layer_bwd(0, dx1, sv0, wt, sm, mem_k, mem_v)
    dwn, dwkv, dwkn = mem_bwd(mem, sm["mem_norm_w"][None], wt["w_mem_kv"], sm["mem_k_norm_w"][None], *dm0, *dm1)
    small = dict(mem_norm_w=dwn[0], mem_k_norm_w=dwkn[0],
                 norm1_w=jnp.concatenate([small0["norm1_w"], small1["norm1_w"]]),
                 norm2_w=jnp.concatenate([small0["norm2_w"], small1["norm2_w"]]),
                 memq_norm_w=jnp.concatenate([small0["memq_norm_w"], small1["memq_norm_w"]]),
                 dn_a_log=small0["dn_a_log"], dn_dt_bias=small0["dn_dt_bias"], dn_o_norm_w=small0["dn_o_norm_w"],
                 fox_f_bias=small1["fox_f_bias"], fox_q_norm_w=small1["fox_q_norm_w"], fox_k_norm_w=small1["fox_k_norm_w"])
    big = dict(w_mem_kv=dwkv, dn_w_in=big0["w_in"], fox_w_in=big1["w_in"], conv_w=big0["conv_w"],
               w_out=[big0["w_out"], big1["w_out"]], w_mlp1=[big0["w_mlp1"], big1["w_mlp1"]],
               w_mlp2=[big0["w_mlp2"], big1["w_mlp2"]])
    return loss, dx0, big, small


def w_in_to_kernel(w, n_scalars):
    pad = jnp.zeros((w.shape[0], PROJ_W - TAIL - n_scalars), w.dtype)
    return jnp.concatenate([w[:, :4096], w[:, 4096 + n_scalars:], w[:, 4096:4096 + n_scalars], pad], axis=1)


def w_in_from_kernel(w, n_scalars):
    return jnp.concatenate([w[:, :4096], w[:, TAIL:TAIL + n_scalars], w[:, 4096:TAIL]], axis=1)


BIG_SPECS = dict(w_mem_kv=("rows", 1, 256, 1024), w_out=("rows", 2, 384, 1024), w_mlp2=("rows", 2, 1024, 1024),
                 w_mlp1=("cols", 2, 1024, 1024), dn_w_in=("rows", 1, 1024, 1156), fox_w_in=("rows", 1, 1024, 1154))
BIG_NAMES = tuple(BIG_SPECS)
EARLY_NAMES = ("w_mem_kv", "dn_w_in")
LATE_NAMES = ("w_out", "w_mlp2", "w_mlp1", "fox_w_in")


def _full_shape(name, half=False):
    kind, a, b, c = BIG_SPECS[name]
    b = b // 2 if half else b
    return (a, N_CHIP, b, c) if kind == "rows" else (a, b, N_CHIP * c)


def _ds(start, size, align):
    return pl.ds(start if isinstance(start, int) else pl.multiple_of(start, align), size)


def _half_rows(name, h):
    b = BIG_SPECS[name][2]
    return _ds(h * (b // 2), b // 2, 16)


def _shard_idx(name, h):
    return (slice(None), _half_rows(name, h), slice(None))


def _full_idx(name, j=None, h=None):
    kind, _, _, c = BIG_SPECS[name]
    rows = slice(None) if h is None else _half_rows(name, h)
    if kind == "rows":
        return (slice(None), slice(None) if j is None else j, rows, slice(None))
    return (slice(None), rows, slice(None) if j is None else _ds(j * c, c, 128))


def _row_block(name):
    hs = BIG_SPECS[name][2] // 2
    return hs if hs <= ROWS else ROWS


def _remote(src, dst, send_sem, recv_sem, to):
    return pltpu.make_async_remote_copy(src_ref=src, dst_ref=dst, send_sem=send_sem, recv_sem=recv_sem, device_id=to,
                                        device_id_type=MESH)


HBM = pl.BlockSpec(memory_space=pltpu.HBM)
SEM = pl.BlockSpec(memory_space=pltpu.SEMAPHORE)
EFFECT = pltpu.CompilerParams(has_side_effects=pltpu.SideEffectType.DATAFLOW_SIDE_EFFECTING)


def _in_hbm(a):
    return pltpu.with_memory_space_constraint(a, pltpu.HBM)


def _chip_copies(names, ins, lands, send_sems, recv_sems):
    x, y, c, chips = _place()
    return [_remote(ins[a].at[_shard_idx(name, c)], lands[a].at[_full_idx(name, 2 * x + y, c)], send_sems.at[3 * a + k],
                    recv_sems.at[3 * a + k], (chip[0], chip[1], c))
            for a, name in enumerate(names) for k, chip in enumerate(chips)]


def all_gather_start(shards, names):
    n = len(names)

    def body(*refs):
        ins, lands, send_sems, recv_sems, token = refs[:n], refs[n:2 * n], refs[2 * n], refs[2 * n + 1], refs[-1]
        for cp in _chip_copies(names, ins, lands, send_sems, recv_sems):
            cp.start()
        token[...] = jnp.zeros_like(token)

    ins = [_in_hbm(shards[name]) for name in names]
    lands = [_in_hbm(lax.empty(_full_shape(name), MM)) for name in names]
    sems = (pltpu.SemaphoreType.DMA((3 * n,)), pltpu.SemaphoreType.DMA((3 * n,)))
    outs = pl.pallas_call(
        body, name="all_gather_start", in_specs=[HBM] * (2 * n),
        out_specs=(SEM, SEM) + (HBM,) * (2 * n) + (pl.BlockSpec(memory_space=pltpu.VMEM),),
        out_shape=sems + tuple(pltpu.HBM(a.shape, a.dtype) for a in ins + lands) + (jax.ShapeDtypeStruct((8, HEAD_DIM), F32),),
        input_output_aliases={a: 2 + a for a in range(2 * n)}, compiler_params=EFFECT,
    )(*ins, *lands)
    return outs[:-1], outs[-1]


def all_gather_wait(state, names, after):
    n = len(names)

    def body(*refs):
        send_sems, recv_sems, ins, lands = refs[0], refs[1], refs[2:2 + n], refs[2 + n:2 + 2 * n]
        for cp in _chip_copies(names, ins, lands, send_sems, recv_sems):
            cp.wait_send()
            cp.wait_recv()

    outs = pl.pallas_call(
        body, name="all_gather_wait", in_specs=[SEM, SEM] + [HBM] * (2 * n) + [ANY], out_specs=(HBM,) * (2 * n),
        out_shape=tuple(pltpu.HBM(a.shape, a.dtype) for a in state[2:]),
        input_output_aliases={2 + a: a for a in range(2 * n)}, compiler_params=EFFECT,
    )(*state, after)
    return dict(zip(names, outs[:n])), dict(zip(names, outs[n:]))


def all_gather_pass_on(lands, names):
    n = len(names)

    def body(*refs):
        outs, send_sems, recv_sems = refs[n:2 * n], refs[2 * n], refs[2 * n + 1]
        x, y, c, chips = _place()
        work = [(3 * a + k, a, name, 2 * chip[0] + chip[1]) for a, name in enumerate(names) for k, chip in enumerate(chips)]
        cps = []
        for s, a, name, slot in work:
            landed = outs[a].at[_full_idx(name, slot, c)]
            cps.append(_remote(landed, landed, send_sems.at[s], recv_sems.at[s], (x, y, 1 - c)))
            cps[-1].start()
        for s, a, name, slot in work:
            passed = outs[a].at[_full_idx(name, slot, 1 - c)]
            _remote(passed, passed, send_sems.at[s], recv_sems.at[s], (x, y, 1 - c)).wait_recv()
        for cp in cps:
            cp.wait_send()

    outs = pl.pallas_call(
        body, name="all_gather_pass_on", in_specs=[ANY] * n, out_specs=[ANY] * n,
        input_output_aliases={a: a for a in range(n)},
        out_shape=[jax.ShapeDtypeStruct(_full_shape(name), MM) for name in names],
        scratch_shapes=[pltpu.SemaphoreType.DMA((3 * n,)), pltpu.SemaphoreType.DMA((3 * n,))],
    )(*[lands[name] for name in names])
    return dict(zip(names, outs))


def all_gather_big(shards, names):
    n = len(names)
    BIG_NAMES = names

    def body(*refs):
        ins, outs = refs[:n], refs[n:2 * n]
        send_sems, recv_sems, fsend_sems, frecv_sems = refs[2 * n:]
        x, y, c, chips = _place()
        me_chip, sibling = 2 * x + y, (x, y, 1 - c)
        work = [(3 * a + k, a, name, chip) for a, name in enumerate(BIG_NAMES) for k, chip in enumerate(chips)]
        sends = []
        for s, a, name, chip in work:
            cp = _remote(ins[a].at[_shard_idx(name, c)], outs[a].at[_full_idx(name, me_chip, c)], send_sems.at[s],
                         recv_sems.at[s], (chip[0], chip[1], c))
            cp.start()
            sends.append(cp)
        for s, a, name, chip in work:
            landed = outs[a].at[_full_idx(name, 2 * chip[0] + chip[1], c)]
            _remote(landed, landed, send_sems.at[s], recv_sems.at[s], (chip[0], chip[1], c)).wait_recv()
            cp = _remote(landed, landed, fsend_sems.at[s], frecv_sems.at[s], sibling)
            cp.start()
            sends.append(cp)
        for s, a, name, chip in work:
            passed = outs[a].at[_full_idx(name, 2 * chip[0] + chip[1], 1 - c)]
            _remote(passed, passed, fsend_sems.at[s], frecv_sems.at[s], sibling).wait_recv()
        for cp in sends:
            cp.wait_send()

    outs = pl.pallas_call(
        body, name="all_gather_big", in_specs=[ANY] * n, out_specs=[ANY] * n,
        out_shape=[jax.ShapeDtypeStruct(_full_shape(name), MM) for name in BIG_NAMES],
        scratch_shapes=[pltpu.SemaphoreType.DMA((3 * n,))] * 4,
    )(*[shards[name] for name in BIG_NAMES])
    return dict(zip(BIG_NAMES, outs))


def with_own_slot(name, full, shard, chip):
    kind, _, _, c = BIG_SPECS[name]
    if kind == "rows":
        return lax.dynamic_update_slice(full, shard[:, None], (0, chip, 0, 0))
    return lax.dynamic_update_slice(full, shard, (0, 0, chip * c))


def rs_pair_exchange_big(grads):
    n = len(BIG_NAMES)

    def body(*refs):
        ins, outs, send_sems, recv_sems = refs[:n], refs[n:2 * n], refs[2 * n], refs[2 * n + 1]
        x, y, c, _ = _place()
        cps = []
        for a, name in enumerate(BIG_NAMES):
            cp = _remote(ins[a].at[_full_idx(name, None, 1 - c)], outs[a], send_sems.at[a], recv_sems.at[a], (x, y, 1 - c))
            cp.start()
            cps.append(cp)
        for cp in cps:
            cp.wait()

    outs = pl.pallas_call(
        body, name="rs_pair_exchange_big", in_specs=[ANY] * n, out_specs=[ANY] * n,
        out_shape=[jax.ShapeDtypeStruct(_full_shape(name, half=True), F32) for name in BIG_NAMES],
        scratch_shapes=[pltpu.SemaphoreType.DMA((n,)), pltpu.SemaphoreType.DMA((n,))],
    )(*[grads[name] for name in BIG_NAMES])
    return dict(zip(BIG_NAMES, outs))


def rs_pair_add_big(name, place, g, got):
    kind, a_, b_, c_ = BIG_SPECS[name]
    rb = _row_block(name)
    nb = (b_ // 2) // rb

    def body(place_ref, g_ref, got_ref, o_ref):
        o_ref[...] = (g_ref[...] + got_ref[...]).astype(o_ref.dtype)

    if kind == "rows":
        g_spec = pl.BlockSpec((None, None, rb, c_), lambda a, j, i, p: (a, j, p[0] * nb + i, 0))
        o_spec = pl.BlockSpec((None, None, rb, c_), lambda a, j, i, p: (a, j, i, 0))
    else:
        g_spec = pl.BlockSpec((None, rb, c_), lambda a, j, i, p: (a, p[0] * nb + i, j))
        o_spec = pl.BlockSpec((None, rb, c_), lambda a, j, i, p: (a, i, j))
    return pl.pallas_call(
        body, name="rs_pair_add_" + name,
        grid_spec=pltpu.PrefetchScalarGridSpec(num_scalar_prefetch=1, grid=(a_, N_CHIP, nb), in_specs=[g_spec, o_spec],
                                               out_specs=o_spec),
        out_shape=jax.ShapeDtypeStruct(_full_shape(name, half=True), MM),
        compiler_params=_cparams(("parallel", "parallel", "parallel")),
    )(place, g, got)


def rs_chip_exchange_big(pairs):
    n = len(BIG_NAMES)

    def body(*refs):
        ins, outs, send_sems, recv_sems = refs[:n], refs[n:2 * n], refs[2 * n], refs[2 * n + 1]
        x, y, c, chips = _place()
        cps = []
        for a, name in enumerate(BIG_NAMES):
            for k, chip in enumerate(chips):
                s = 3 * a + k
                cp = _remote(ins[a].at[_full_idx(name, 2 * chip[0] + chip[1])], outs[a].at[k], send_sems.at[s],
                             recv_sems.at[s], (chip[0], chip[1], c))
                cp.start()
                cps.append(cp)
        for cp in cps:
            cp.wait()

    def got_shape(name):
        _, a_, b_, c_ = BIG_SPECS[name]
        return jax.ShapeDtypeStruct((3, a_, b_ // 2, c_), MM)

    outs = pl.pallas_call(
        body, name="rs_chip_exchange_big", in_specs=[ANY] * n, out_specs=[ANY] * n,
        out_shape=[got_shape(name) for name in BIG_NAMES],
        scratch_shapes=[pltpu.SemaphoreType.DMA((3 * n,)), pltpu.SemaphoreType.DMA((3 * n,))],
    )(*[pairs[name] for name in BIG_NAMES])
    return dict(zip(BIG_NAMES, outs))


def rs_chip_add_big(name, place, g, got_pair, got_chips):
    kind, a_, b_, c_ = BIG_SPECS[name]
    rb = _row_block(name)
    nb = (b_ // 2) // rb

    def body(place_ref, g_ref, s_ref, r0_ref, r1_ref, r2_ref, o_ref):
        own = g_ref[...] + s_ref[...]
        o_ref[...] = ((own + r0_ref[...].astype(F32)) + r1_ref[...].astype(F32)) + r2_ref[...].astype(F32)

    if kind == "rows":
        g_spec = pl.BlockSpec((None, None, rb, c_), lambda a, i, p: (a, p[1], p[0] * nb + i, 0))
        s_spec = pl.BlockSpec((None, None, rb, c_), lambda a, i, p: (a, p[1], i, 0))
    else:
        g_spec = pl.BlockSpec((None, rb, c_), lambda a, i, p: (a, p[0] * nb + i, p[1]))
        s_spec = pl.BlockSpec((None, rb, c_), lambda a, i, p: (a, i, p[1]))
    r_spec = lambda k: pl.BlockSpec((None, None, rb, c_), lambda a, i, p: (k, a, i, 0))
    return pl.pallas_call(
        body, name="rs_chip_add_" + name,
        grid_spec=pltpu.PrefetchScalarGridSpec(
            num_scalar_prefetch=1, grid=(a_, nb), in_specs=[g_spec, s_spec, r_spec(0), r_spec(1), r_spec(2)],
            out_specs=pl.BlockSpec((None, rb, c_), lambda a, i, p: (a, p[0] * nb + i, 0))),
        out_shape=jax.ShapeDtypeStruct((a_, b_, c_), F32), compiler_params=_cparams(("parallel", "parallel")),
    )(place, g, got_pair, got_chips, got_chips, got_chips)


def rs_pair_gather_big(halves):
    n = len(BIG_NAMES)

    def body(*refs):
        outs, send_sems, recv_sems = refs[n:2 * n], refs[2 * n], refs[2 * n + 1]
        x, y, c, _ = _place()
        cps = []
        for a, name in enumerate(BIG_NAMES):
            mine = outs[a].at[_shard_idx(name, c)]
            cp = _remote(mine, mine, send_sems.at[a], recv_sems.at[a], (x, y, 1 - c))
            cp.start()
            cps.append(cp)
        for a, name in enumerate(BIG_NAMES):
            cps[a].wait_send()
            theirs = outs[a].at[_shard_idx(name, 1 - c)]
            _remote(theirs, theirs, send_sems.at[a], recv_sems.at[a], (x, y, 1 - c)).wait_recv()

    outs = pl.pallas_call(
        body, name="rs_pair_gather_big", in_specs=[ANY] * n, out_specs=[ANY] * n,
        input_output_aliases={a: a for a in range(n)},
        out_shape=[jax.ShapeDtypeStruct(BIG_SPECS[name][1:], F32) for name in BIG_NAMES],
        scratch_shapes=[pltpu.SemaphoreType.DMA((n,)), pltpu.SemaphoreType.DMA((n,))],
    )(*[halves[name] for name in BIG_NAMES])
    return dict(zip(BIG_NAMES, outs))


def reduce_scatter_big(grads, place):
    got_pair = rs_pair_exchange_big(grads)
    pairs = {name: rs_pair_add_big(name, place, grads[name], got_pair[name]) for name in BIG_NAMES}
    got_chips = rs_chip_exchange_big(pairs)
    return rs_pair_gather_big({name: rs_chip_add_big(name, place, grads[name], got_pair[name], got_chips[name])
                               for name in BIG_NAMES})


PACK_W = 1024
SMALL =(("mem_norm_w", 1024), ("mem_k_norm_w", 128), ("norm1_w", 2048), ("dn_a_log", 8), ("dn_dt_bias", 8),
         ("dn_o_norm_w", 128), ("fox_f_bias", 8), ("fox_q_norm_w", 128), ("fox_k_norm_w", 128), ("memq_norm_w", 256),
         ("norm2_w", 2048))
SMALL_ROWS = 8
CONV_ROWS = 4 * 3 * D_MODEL // PACK_W
LOSS_AT = sum(n for _, n in SMALL)


def pack_small(parts, extra=None):
    flat = [parts[name].astype(F32).reshape(-1) for name, _ in SMALL]
    used = LOSS_AT
    if extra is not None:
        flat.append(extra.reshape(1))
        used += 1
    flat.append(jnp.zeros((SMALL_ROWS * PACK_W - used,), F32))
    return jnp.concatenate(flat).reshape(SMALL_ROWS, PACK_W)


def unpack_small(packed, shapes):
    flat, out, at = packed.reshape(-1), {}, 0
    for name, n in SMALL:
        out[name] = flat[at:at + n].reshape(shapes[name])
        at += n
    return out


def _adam_all(w, g, m, v, name):
    shape = w.shape
    r2 = lambda a: a.reshape(-1, shape[-1])
    d, nm, nv = adamw(r2(w), r2(g), r2(m), r2(v), name=name)
    return d.reshape(shape), nm.reshape(shape), nv.reshape(shape)


BIG = ("w_mem_kv", "dn_w_in", "dn_conv_w", "fox_w_in", "w_out", "w_mlp1", "w_mlp2")
WEIGHTS = ("mem_norm_w", "w_mem_kv", "mem_k_norm_w", "norm1_w", "dn_w_in", "dn_conv_w", "dn_a_log", "dn_dt_bias",
           "dn_o_norm_w", "fox_w_in", "fox_f_bias", "fox_q_norm_w", "fox_k_norm_w", "memq_norm_w", "w_out", "norm2_w",
           "w_mlp1", "w_mlp2")


def kernel(x, mem, mem_norm_w, w_mem_kv, mem_k_norm_w, norm1_w, dn_w_in, dn_conv_w, dn_a_log, dn_dt_bias, dn_o_norm_w, fox_w_in, fox_f_bias, fox_q_norm_w, fox_k_norm_w, memq_norm_w, w_out, norm2_w, w_mlp1, w_mlp2, loss_target, m_mem_norm_w, m_w_mem_kv, m_mem_k_norm_w, m_norm1_w, m_dn_w_in, m_dn_conv_w, m_dn_a_log, m_dn_dt_bias, m_dn_o_norm_w, m_fox_w_in, m_fox_f_bias, m_fox_q_norm_w, m_fox_k_norm_w, m_memq_norm_w, m_w_out, m_norm2_w, m_w_mlp1, m_w_mlp2, v_mem_norm_w, v_w_mem_kv, v_mem_k_norm_w, v_norm1_w, v_dn_w_in, v_dn_conv_w, v_dn_a_log, v_dn_dt_bias, v_dn_o_norm_w, v_fox_w_in, v_fox_f_bias, v_fox_q_norm_w, v_fox_k_norm_w, v_memq_norm_w, v_w_out, v_norm2_w, v_w_mlp1, v_w_mlp2):
    args = dict(locals())
    w = {n: args[n] for n in WEIGHTS}
    m = {n: args["m_" + n] for n in WEIGHTS}
    v = {n: args["v_" + n] for n in WEIGHTS}
    core, chip = lax.axis_index("c"), 2 * lax.axis_index("x") + lax.axis_index("y")
    place = jnp.stack([core, chip]).astype(jnp.int32)

    shards = {name: w[name].reshape(BIG_SPECS[name][1:]).astype(MM) for name in BIG_NAMES}
    late_state, token = all_gather_start(shards, LATE_NAMES)
    tie = token[0, 0]
    w_in_full = lambda arr, n_scalars: w_in_to_kernel(arr[0].transpose(1, 0, 2).reshape(D_MODEL, -1), n_scalars)
    early_shards = {name: shards[name] + tie.astype(MM) for name in EARLY_NAMES}
    early = {name: with_own_slot(name, arr, early_shards[name], chip)
             for name, arr in all_gather_big(early_shards, EARLY_NAMES).items()}
    conv_mine = jnp.where(core == 0, dn_conv_w[0], 0.0) + tie
    conv_placed = lax.dynamic_update_slice(jnp.zeros((4, 3 * D_MODEL), F32), conv_mine, (0, 768 * chip))
    conv_full = all_reduce_small(jnp.pad(conv_placed.reshape(CONV_ROWS, PACK_W), ((0, 16 - CONV_ROWS), (0, 0))))
    wt = dict(w_mem_kv=early["w_mem_kv"].reshape(D_MODEL, 2 * MEM_WIDTH), dn_w_in=w_in_full(early["dn_w_in"], 2 * N_HEADS),
              conv_w=conv_full[:CONV_ROWS].reshape(4, 3 * D_MODEL))

    def late(after):
        late_shards, lands = all_gather_wait(late_state, LATE_NAMES, after)
        full = {name: with_own_slot(name, arr, late_shards[name], chip)
                for name, arr in all_gather_pass_on(lands, LATE_NAMES).items()}
        return dict(fox_w_in=w_in_full(full["fox_w_in"], N_HEADS), w_out=full["w_out"].reshape(2, 3 * MEM_WIDTH, D_MODEL),
                    w_mlp1=full["w_mlp1"], w_mlp2=full["w_mlp2"].reshape(2, D_FF, D_MODEL))

    sm = dict(mem_norm_w=mem_norm_w, mem_k_norm_w=mem_k_norm_w, norm1_w=norm1_w, norm2_w=norm2_w, memq_norm_w=memq_norm_w,
              dn_a_log=dn_a_log[0], dn_dt_bias=dn_dt_bias[0], dn_o_norm_w=dn_o_norm_w, fox_f_bias=fox_f_bias[0],
              fox_q_norm_w=fox_q_norm_w, fox_k_norm_w=fox_k_norm_w)
    loss_part, dx, big, small = local_step(x[0], mem[0], loss_target[0], wt, sm, late)

    w_in_slots = lambda g, n_scalars: w_in_from_kernel(g, n_scalars).reshape(D_MODEL, N_CHIP, -1).transpose(1, 0, 2)[None]
    full_grads = dict(w_mem_kv=big["w_mem_kv"].reshape(_full_shape("w_mem_kv")),
                      w_out=jnp.stack(big["w_out"]).reshape(_full_shape("w_out")),
                      w_mlp2=jnp.stack(big["w_mlp2"]).reshape(_full_shape("w_mlp2")), w_mlp1=jnp.stack(big["w_mlp1"]),
                      dn_w_in=w_in_slots(big["dn_w_in"], 2 * N_HEADS), fox_w_in=w_in_slots(big["fox_w_in"], N_HEADS))
    big_sum = reduce_scatter_big(full_grads, place)
    small_pack = jnp.concatenate([pack_small(small, loss_part[0, :1]), big["conv_w"].reshape(CONV_ROWS, PACK_W),
                                  jnp.zeros((24 - SMALL_ROWS - CONV_ROWS, PACK_W), F32)])
    small_all = all_reduce_small(small_pack)
    small_sum = small_all[:SMALL_ROWS]
    conv_sum = lax.dynamic_slice(small_all[SMALL_ROWS:SMALL_ROWS + CONV_ROWS].reshape(4, 3 * D_MODEL), (0, 768 * chip), (4, 768))
    loss = small_sum.reshape(-1)[LOSS_AT]
    grads = unpack_small(small_sum, {n: w[n].shape for n, _ in SMALL})
    grads.update({name: big_sum[name].reshape(w[name].shape) for name in BIG_NAMES}, dn_conv_w=conv_sum[None])

    delta, new_m, new_v = {}, {}, {}
    for n in BIG:
        delta[n], new_m[n], new_v[n] = _adam_all(w[n], grads[n], m[n], v[n], "adamw_" + n)
    shapes = {n: w[n].shape for n, _ in SMALL}
    d_s, m_s, v_s = adamw(pack_small(w), small_sum, pack_small(m), pack_small(v), name="adamw_small")
    for out, packed in ((delta, d_s), (new_m, m_s), (new_v, v_s)):
        out.update(unpack_small(packed, shapes))
    return (loss, dx[None], *[grads[n] for n in WEIGHTS], *[delta[n] for n in WEIGHTS],
            *[new_m[n] for n in WEIGHTS], *[new_v[n] for n in WEIGHTS])
```

```python
import functools

import jax
import jax.numpy as jnp
from jax import lax
from jax.experimental import pallas as pl
from jax.experimental.pallas import tpu as pltpu

F32 = jnp.float32
MM = jnp.bfloat16
HI = lax.Precision.HIGHEST

D_MODEL = 1024
HEAD_DIM = 128
N_HEADS = 8
MEM_HEADS = 4
MEM_WIDTH = MEM_HEADS * HEAD_DIM
N_MEM = 256
D_FF = 4 * D_MODEL
CHUNK = 64
EPS = 1e-6
QSCALE = HEAD_DIM ** -0.5
PROJ_W = 4736
TAIL = 4608
TAIL_BLK = TAIL // HEAD_DIM
ROWS = 256
VMEM_LIMIT = 56 * 1024 * 1024

ADAM_LR = 0.001
ADAM_B1 = 0.9
ADAM_B2 = 0.999
ADAM_EPS = 1e-08
ADAM_WD = 0.01
ADAM_STEP = 10

N_DEV = 8
N_CHIP = 4
MESH = pl.DeviceIdType.MESH


def _cparams(sem=None):
    return pltpu.CompilerParams(dimension_semantics=sem, vmem_limit_bytes=VMEM_LIMIT)


def _dot(a, b, ca, cb, hi):
    dims = (((ca,), (cb,)), ((), ()))
    if hi:
        return lax.dot_general(a, b, dims, precision=HI, preferred_element_type=F32)
    return lax.dot_general(a.astype(MM), b.astype(MM), dims, preferred_element_type=F32)


@functools.partial(jax.custom_vjp, nondiff_argnums=(2, 3, 4))
def mmul(a, b, ca, cb, hi):
    return _dot(a, b, ca, cb, hi)


def _mmul_fwd(a, b, ca, cb, hi):
    return _dot(a, b, ca, cb, hi), (a, b)


def _mmul_bwd(ca, cb, hi, res, g):
    a, b = res
    if ca == 1:
        da = _dot(g, b, 1, 1, hi) if cb == 0 else _dot(g, b, 1, 0, hi)
    else:
        da = _dot(b, g, 1, 1, hi) if cb == 0 else _dot(b, g, 0, 1, hi)
    if cb == 0:
        db = _dot(a, g, 0, 0, hi) if ca == 1 else _dot(a, g, 1, 0, hi)
    else:
        db = _dot(g, a, 0, 0, hi) if ca == 1 else _dot(g, a, 0, 1, hi)
    return da.astype(a.dtype), db.astype(b.dtype)


mmul.defvjp(_mmul_fwd, _mmul_bwd)


def _iota2(n, m):
    return lax.broadcasted_iota(jnp.int32, (n, m), 0), lax.broadcasted_iota(jnp.int32, (n, m), 1)


def _same_block(r, c, shift):
    return lax.shift_right_logical(r, shift) == lax.shift_right_logical(c, shift)


def _split_bf16(x):
    hi = x.astype(jnp.bfloat16)
    return hi, (x - hi.astype(F32)).astype(jnp.bfloat16)


def _dot3(a, b, ca, cb):
    dims = (((ca,), (cb,)), ((), ()))
    (ah, al), (bh, bl) = _split_bf16(a), _split_bf16(b)
    d = lambda x, y: lax.dot_general(x, y, dims, preferred_element_type=F32)
    return d(ah, bh) + (d(ah, bl) + d(al, bh))


def _tri_inv_impl(a):
    n = a.shape[0]
    r, c = _iota2(n, n)
    eye = (r == c).astype(F32)
    b16, b32 = _same_block(r, c, 4), _same_block(r, c, 5)
    a0 = jnp.where(b16, a, 0.0)
    p = eye - a0
    b = _dot3(a0, a0, 1, 0)
    p = p + _dot3(p, b, 1, 0)
    b = _dot3(b, b, 1, 0)
    p = p + _dot3(p, b, 1, 0)
    b = _dot3(b, b, 1, 0)
    p = p + _dot3(p, b, 1, 0)
    a1 = jnp.where(jnp.logical_and(b32, jnp.logical_not(b16)), a, 0.0)
    p = p - _dot3(_dot3(p, a1, 1, 0), p, 1, 0)
    a2 = jnp.where(b32, 0.0, a)
    p = p - _dot3(_dot3(p, a2, 1, 0), p, 1, 0)
    return p


@jax.custom_vjp
def tri_inv(a):
    return _tri_inv_impl(a)


def _tri_inv_fwd(a):
    p = _tri_inv_impl(a)
    return p, p


def _tri_inv_bwd(p, g):
    return (-_dot3(_dot3(p, g, 0, 0), p, 1, 1),)


tri_inv.defvjp(_tri_inv_fwd, _tri_inv_bwd)


def _sigmoid(x):
    return 1.0 / (1.0 + jnp.exp(-x))


def _softplus(x):
    return jnp.maximum(x, 0.0) + jnp.log(1.0 + jnp.exp(-jnp.abs(x)))


def _silu(x):
    return x * _sigmoid(x)


def _rms(x, w):
    return x * lax.rsqrt(jnp.mean(x * x, axis=-1, keepdims=True) + EPS) * w


def _bf_round(x):
    return x.astype(MM).astype(F32)


def _acc(ref, val, first):
    @pl.when(first)
    def _():
        ref[...] = val

    @pl.when(jnp.logical_not(first))
    def _():
        ref[...] += val


def _tile(n, pref):
    if n % pref == 0:
        return pref
    return n


def matmul(a, b, *, ta=False, tb=False, b_slots=False, res=None, out_dtype=F32, name, tm=1024, tn=1024, tk=1024):
    m, k = (a.shape[1], a.shape[0]) if ta else a.shape
    if b_slots:
        n = b.shape[1] if tb else N_CHIP * b.shape[2]
        assert (N_CHIP * b.shape[2] if tb else b.shape[1]) == k, (a.shape, b.shape, ta, tb)
        tn, tk = (tn, b.shape[2]) if tb else (b.shape[2], tk)
    else:
        n = b.shape[0] if tb else b.shape[1]
        assert (b.shape[1] if tb else b.shape[0]) == k, (a.shape, b.shape, ta, tb)
    tm, tn, tk = _tile(m, tm), _tile(n, tn), _tile(k, tk)
    nk = k // tk
    ca, cb = (0 if ta else 1), (1 if tb else 0)

    def body(a_ref, b_ref, *rest):
        r_ref = rest[0] if res is not None else None
        o_ref, acc_ref = rest[-2:]
        kk = pl.program_id(2)
        part = _dot(a_ref[...], b_ref[...], ca, cb, False)

        @pl.when(kk == 0)
        def _():
            acc_ref[...] = part

        @pl.when(kk > 0)
        def _():
            acc_ref[...] += part

        @pl.when(kk == nk - 1)
        def _():
            total = acc_ref[...] if r_ref is None else acc_ref[...] + r_ref[...]
            o_ref[...] = total.astype(o_ref.dtype)

    a_spec = pl.BlockSpec((tk, tm), lambda i, j, l: (l, i)) if ta else pl.BlockSpec((tm, tk), lambda i, j, l: (i, l))
    if b_slots:
        b_spec = (pl.BlockSpec((None, tn, tk), lambda i, j, l: (l, j, 0)) if tb else
                  pl.BlockSpec((None, tk, tn), lambda i, j, l: (j, l, 0)))
    else:
        b_spec = pl.BlockSpec((tn, tk), lambda i, j, l: (j, l)) if tb else pl.BlockSpec((tk, tn), lambda i, j, l: (l, j))
    o_spec = pl.BlockSpec((tm, tn), lambda i, j, l: (i, j))
    extra = () if res is None else (res,)
    return pl.pallas_call(
        body, name=name, grid=(m // tm, n // tn, nk),
        in_specs=[a_spec, b_spec] + [o_spec] * len(extra), out_specs=o_spec,
        out_shape=jax.ShapeDtypeStruct((m, n), out_dtype),
        scratch_shapes=[pltpu.VMEM((tm, tn), F32)],
        compiler_params=_cparams(("parallel", "parallel", "arbitrary")),
    )(a, b, *extra)


def rms_fwd(x, w, *, name):
    t, d = x.shape

    def body(x_ref, w_ref, o_ref):
        o_ref[...] = _rms(x_ref[...], w_ref[...]).astype(o_ref.dtype)

    return pl.pallas_call(
        body, name=name, grid=(t // ROWS,),
        in_specs=[pl.BlockSpec((ROWS, d), lambda i: (i, 0)), pl.BlockSpec((1, d), lambda i: (0, 0))],
        out_specs=pl.BlockSpec((ROWS, d), lambda i: (i, 0)),
        out_shape=jax.ShapeDtypeStruct((t, d), MM), compiler_params=_cparams(("parallel",)),
    )(x, w)


def rms_bwd(x, w, dh, dres, *, name):
    t, d = x.shape

    def body(x_ref, w_ref, dh_ref, dr_ref, dx_ref, dw_ref):
        _, vjp = jax.vjp(_rms, x_ref[...], w_ref[...])
        dx, dw = vjp(dh_ref[...].astype(F32))
        dx_ref[...] = dx + dr_ref[...]
        _acc(dw_ref, dw, pl.program_id(0) == 0)

    row = pl.BlockSpec((ROWS, d), lambda i: (i, 0))
    vec = pl.BlockSpec((1, d), lambda i: (0, 0))
    return pl.pallas_call(
        body, name=name, grid=(t // ROWS,), in_specs=[row, vec, row, row], out_specs=[row, vec],
        out_shape=[jax.ShapeDtypeStruct((t, d), F32), jax.ShapeDtypeStruct((1, d), F32)],
        compiler_params=_cparams(("arbitrary",)),
    )(x, w, dh, dres)


def _sqrelu(x):
    return jnp.square(jnp.maximum(x, 0.0))


def act_fwd(ff, *, name):
    t, f = ff.shape

    def body(x_ref, o_ref):
        o_ref[...] = _sqrelu(x_ref[...]).astype(o_ref.dtype)

    blk = pl.BlockSpec((ROWS, f), lambda i: (i, 0))
    return pl.pallas_call(body, name=name, grid=(t // ROWS,), in_specs=[blk], out_specs=blk,
                          out_shape=jax.ShapeDtypeStruct((t, f), MM), compiler_params=_cparams(("parallel",)))(ff)


def act_bwd(ff, dact, *, name):
    t, f = ff.shape

    def body(x_ref, g_ref, o_ref):
        o_ref[...] = (g_ref[...] * 2.0 * jnp.maximum(x_ref[...], 0.0)).astype(o_ref.dtype)

    blk = pl.BlockSpec((ROWS, f), lambda i: (i, 0))
    return pl.pallas_call(body, name=name, grid=(t // ROWS,), in_specs=[blk, blk], out_specs=blk,
                          out_shape=jax.ShapeDtypeStruct((t, f), MM), compiler_params=_cparams(("parallel",)))(ff, dact)


def loss_fwd(y, target, *, name):
    t, d = y.shape

    def body(y_ref, t_ref, dy_ref, l_ref):
        e = y_ref[...] - t_ref[...]
        dy_ref[...] = e * (1.0 / d)
        part = 0.5 * jnp.sum(jnp.sum(e * e, axis=-1, keepdims=True) * (1.0 / d), axis=0, keepdims=True)
        _acc(l_ref, jnp.broadcast_to(part, (1, HEAD_DIM)), pl.program_id(0) == 0)

    blk = pl.BlockSpec((ROWS, d), lambda i: (i, 0))
    return pl.pallas_call(
        body, name=name, grid=(t // ROWS,), in_specs=[blk, blk],
        out_specs=[blk, pl.BlockSpec((1, HEAD_DIM), lambda i: (0, 0))],
        out_shape=[jax.ShapeDtypeStruct((t, d), F32), jax.ShapeDtypeStruct((1, HEAD_DIM), F32)],
        compiler_params=_cparams(("arbitrary",)),
    )(y, target)


def _mem_kv(mem, wn, wkn, *ws):
    mn = _rms(mem, wn)
    outs = []
    for h in range(MEM_HEADS):
        outs.append(_rms(mmul(mn, ws[h], 1, 0, False), wkn))
    for h in range(MEM_HEADS):
        outs.append(mmul(mn, ws[MEM_HEADS + h], 1, 0, False))
    return tuple(outs)


def _w_cols(w_ref):
    return [w_ref[:, h * HEAD_DIM:(h + 1) * HEAD_DIM] for h in range(2 * MEM_HEADS)]


def mem_fwd(mem, wn, wkv, wkn):
    def body(mem_ref, wn_ref, w_ref, wkn_ref, k_ref, v_ref):
        outs = _mem_kv(mem_ref[...], wn_ref[...], wkn_ref[...], *_w_cols(w_ref))
        for h in range(MEM_HEADS):
            k_ref[:, h * HEAD_DIM:(h + 1) * HEAD_DIM] = outs[h]
            v_ref[:, h * HEAD_DIM:(h + 1) * HEAD_DIM] = outs[MEM_HEADS + h]

    shp = jax.ShapeDtypeStruct((mem.shape[0], MEM_WIDTH), F32)
    return pl.pallas_call(body, name="mem_fwd", out_shape=[shp, shp], compiler_params=_cparams())(mem, wn, wkv, wkn)


def mem_bwd(mem, wn, wkv, wkn, dk0, dv0, dk1, dv1):
    def body(mem_ref, wn_ref, w_ref, wkn_ref, dk0_ref, dv0_ref, dk1_ref, dv1_ref, dwn_ref, dw_ref, dwkn_ref):
        _, vjp = jax.vjp(lambda wn_, wkn_, *ws: _mem_kv(mem_ref[...], wn_, wkn_, *ws),
                         wn_ref[...], wkn_ref[...], *[w.astype(F32) for w in _w_cols(w_ref)])
        cols = lambda a, b: tuple(a[:, h * HEAD_DIM:(h + 1) * HEAD_DIM] + b[:, h * HEAD_DIM:(h + 1) * HEAD_DIM]
                                  for h in range(MEM_HEADS))
        cts = cols(dk0_ref, dk1_ref) + cols(dv0_ref, dv1_ref)
        grads = vjp(cts)
        dwn_ref[...] = grads[0]
        dwkn_ref[...] = grads[1]
        for h in range(2 * MEM_HEADS):
            dw_ref[:, h * HEAD_DIM:(h + 1) * HEAD_DIM] = grads[2 + h]

    return pl.pallas_call(
        body, name="mem_bwd",
        out_shape=[jax.ShapeDtypeStruct((1, D_MODEL), F32), jax.ShapeDtypeStruct((D_MODEL, 2 * MEM_WIDTH), F32),
                   jax.ShapeDtypeStruct((1, HEAD_DIM), F32)],
        compiler_params=_cparams(),
    )(mem, wn, wkv, wkn, dk0, dv0, dk1, dv1)


def _memattn(q, wq, mk, mv):
    qn = _rms(q, wq) * QSCALE
    s = mmul(qn, mk, 1, 1, False)
    s = s - jnp.max(s, axis=-1, keepdims=True)
    p = jnp.exp(s)
    p = p / jnp.sum(p, axis=-1, keepdims=True)
    return mmul(p, mv, 1, 0, False)


def _lanes(j):
    return slice(j * HEAD_DIM, (j + 1) * HEAD_DIM)


def _memattn_specs(t):
    qspec = pl.BlockSpec((ROWS, MEM_WIDTH), lambda i: (i, (TAIL - MEM_WIDTH) // MEM_WIDTH))
    wspec = pl.BlockSpec((1, HEAD_DIM), lambda i: (0, 0))
    mspec = pl.BlockSpec((N_MEM, MEM_WIDTH), lambda i: (0, 0))
    ospec = pl.BlockSpec((ROWS, MEM_WIDTH), lambda i: (i, 0))
    return qspec, wspec, mspec, ospec


def memattn_fwd(proj, wq, mk, mv, *, name):
    t = proj.shape[0]
    qspec, wspec, mspec, ospec = _memattn_specs(t)

    def body(q_ref, w_ref, k_ref, v_ref, o_ref):
        for h in range(MEM_HEADS):
            o_ref[:, _lanes(h)] = _memattn(q_ref[:, _lanes(h)], w_ref[...], k_ref[:, _lanes(h)],
                                           v_ref[:, _lanes(h)]).astype(o_ref.dtype)

    return pl.pallas_call(
        body, name=name, grid=(t // ROWS,), in_specs=[qspec, wspec, mspec, mspec], out_specs=ospec,
        out_shape=jax.ShapeDtypeStruct((t, MEM_WIDTH), MM), compiler_params=_cparams(("parallel",)),
    )(proj, wq, mk, mv)


def memattn_bwd(proj, wq, mk, mv, dcat, *, name):
    t = proj.shape[0]
    qspec, wspec, mspec, ospec = _memattn_specs(t)
    dospec = pl.BlockSpec((ROWS, MEM_WIDTH), lambda i: (i, D_MODEL // MEM_WIDTH))

    def body(q_ref, w_ref, k_ref, v_ref, do_ref, dq_ref, dw_ref, dk_ref, dv_ref):
        first = pl.program_id(0) == 0
        dw_sum = jnp.zeros((1, HEAD_DIM), F32)
        for h in range(MEM_HEADS):
            _, vjp = jax.vjp(_memattn, q_ref[:, _lanes(h)], w_ref[...], k_ref[:, _lanes(h)], v_ref[:, _lanes(h)])
            dq, dw, dk, dv = vjp(do_ref[:, _lanes(h)].astype(F32))
            dq_ref[:, _lanes(h)] = dq.astype(dq_ref.dtype)
            dw_sum = dw_sum + dw
            _acc(dk_ref.at[:, _lanes(h)], dk, first)
            _acc(dv_ref.at[:, _lanes(h)], dv, first)
        _acc(dw_ref, dw_sum, first)

    mshape = jax.ShapeDtypeStruct((N_MEM, MEM_WIDTH), F32)
    return pl.pallas_call(
        body, name=name, grid=(t // ROWS,), in_specs=[qspec, wspec, mspec, mspec, dospec],
        out_specs=[ospec, wspec, mspec, mspec],
        out_shape=[jax.ShapeDtypeStruct((t, MEM_WIDTH), MM), jax.ShapeDtypeStruct((1, HEAD_DIM), F32), mshape, mshape],
        compiler_params=_cparams(("arbitrary",)),
    )(proj, wq, mk, mv, dcat)


def _shift_rows(x, s, up):
    n = x.shape[0]
    r = lax.broadcasted_iota(jnp.int32, x.shape, 0)
    if up:
        return jnp.where(r < n - s, pltpu.roll(x, n - s, 0), 0.0)
    return jnp.where(r >= s, pltpu.roll(x, s, 0), 0.0)


def _conv_fwd_vals(x, w):
    xb = _bf_round(x)
    wb = _bf_round(w)
    c = xb * wb[3:4, :]
    for j in range(3):
        c = c + _shift_rows(xb, 3 - j, False) * wb[j:j + 1, :]
    return xb, wb, c


def dn_prep_fwd(proj, conv_w):
    t = proj.shape[0]

    def body(x_ref, w_ref, o_ref):
        j = pl.program_id(0)
        _, _, c = _conv_fwd_vals(x_ref[...], w_ref[...])
        s = _silu(c)
        r = lax.rsqrt(jnp.sum(s * s, axis=-1, keepdims=True) + EPS)
        scale = jnp.where(j < N_HEADS, QSCALE, 1.0)
        o_ref[...] = jnp.where(j < 2 * N_HEADS, s * r * scale, s)

    return pl.pallas_call(
        body, name="dn_prep_fwd", grid=(3 * N_HEADS,),
        in_specs=[pl.BlockSpec((t, HEAD_DIM), lambda j: (0, j)), pl.BlockSpec((4, HEAD_DIM), lambda j: (0, j))],
        out_specs=pl.BlockSpec((None, t, HEAD_DIM), lambda j: (j // N_HEADS, 0, j % N_HEADS)),
        out_shape=jax.ShapeDtypeStruct((3, t, D_MODEL), F32), compiler_params=_cparams(("parallel",)),
    )(proj, conv_w)


def dn_prep_bwd(proj, conv_w, dqkv):
    t = proj.shape[0]

    def body(x_ref, w_ref, g_ref, dx_ref, dw_ref):
        j = pl.program_id(0)
        xb, wb, c = _conv_fwd_vals(x_ref[...], w_ref[...])
        sg = _sigmoid(c)
        s = c * sg
        g = g_ref[...]
        r = lax.rsqrt(jnp.sum(s * s, axis=-1, keepdims=True) + EPS)
        scale = jnp.where(j < N_HEADS, QSCALE, 1.0)
        gn = g * scale
        ds_norm = r * gn - s * (r * r * r) * jnp.sum(gn * s, axis=-1, keepdims=True)
        ds = jnp.where(j < 2 * N_HEADS, ds_norm, g)
        dc = ds * (sg + s * (1.0 - sg))
        dx = dc * wb[3:4, :]
        rows = [jnp.sum(dc * xb, axis=0, keepdims=True)]
        for jj in range(2, -1, -1):
            sh = 3 - jj
            dx = dx + _shift_rows(dc, sh, True) * wb[jj:jj + 1, :]
            rows.insert(0, jnp.sum(dc * _shift_rows(xb, sh, False), axis=0, keepdims=True))
        dx_ref[...] = dx.astype(dx_ref.dtype)
        dw_ref[...] = jnp.concatenate(rows + [jnp.zeros((4, HEAD_DIM), F32)], axis=0)

    col = pl.BlockSpec((t, HEAD_DIM), lambda j: (0, j))
    return pl.pallas_call(
        body, name="dn_prep_bwd", grid=(3 * N_HEADS,),
        in_specs=[col, pl.BlockSpec((4, HEAD_DIM), lambda j: (0, j)),
                  pl.BlockSpec((None, t, HEAD_DIM), lambda j: (j // N_HEADS, 0, j % N_HEADS))],
        out_specs=[col, pl.BlockSpec((8, HEAD_DIM), lambda j: (0, j))],
        out_shape=[jax.ShapeDtypeStruct((t, 3 * D_MODEL), MM), jax.ShapeDtypeStruct((8, 3 * D_MODEL), F32)],
        compiler_params=_cparams(("parallel",)),
    )(proj, conv_w, dqkv)


def _tri_ones(n, upper):
    r, c = _iota2(n, n)
    return (r <= c).astype(F32) if upper else (r >= c).astype(F32)


def dn_gates_fwd(proj, a_log, dt_bias):
    t = proj.shape[0]

    def body(x_ref, al_ref, dt_ref, o_ref):
        lane = lax.broadcasted_iota(jnp.int32, (CHUNK, HEAD_DIM), 1)
        tri = _tri_ones(CHUNK, False)

        def step(c, carry):
            rows = pl.ds(pl.multiple_of(c * CHUNK, CHUNK), CHUNK)
            x = x_ref[rows, :]
            g = jnp.where(lane < N_HEADS, -jnp.exp(al_ref[...]) * _softplus(x + dt_ref[...]), 0.0)
            gc = _dot(tri, g, 1, 0, True)
            o_ref[rows, :] = jnp.where(lane < N_HEADS, gc, jnp.where(lane < 2 * N_HEADS, _sigmoid(x), 0.0))
            return carry

        lax.fori_loop(0, t // CHUNK, step, 0)

    vec = pl.BlockSpec((1, HEAD_DIM), lambda i: (0, 0))
    return pl.pallas_call(
        body, name="dn_gates_fwd", grid=(1,),
        in_specs=[pl.BlockSpec((t, HEAD_DIM), lambda i: (0, TAIL_BLK)), vec, vec],
        out_specs=pl.BlockSpec((t, HEAD_DIM), lambda i: (0, 0)),
        out_shape=jax.ShapeDtypeStruct((t, HEAD_DIM), F32), compiler_params=_cparams(("arbitrary",)),
    )(proj, a_log, dt_bias)


def dn_gates_bwd(proj, a_log, dt_bias, dgates):
    t = proj.shape[0]

    def body(x_ref, al_ref, dt_ref, g_ref, dx_ref, dal_ref, ddt_ref):
        lane = lax.broadcasted_iota(jnp.int32, (CHUNK, HEAD_DIM), 1)
        tri = _tri_ones(CHUNK, True)
        dal_ref[...] = jnp.zeros_like(dal_ref)
        ddt_ref[...] = jnp.zeros_like(ddt_ref)

        def step(c, carry):
            rows = pl.ds(pl.multiple_of(c * CHUNK, CHUNK), CHUNK)
            x = x_ref[rows, :]
            dgc = jnp.where(lane < N_HEADS, g_ref[rows, :], 0.0)
            dg = _dot(tri, dgc, 1, 0, True)
            ea = -jnp.exp(al_ref[...])
            z = x + dt_ref[...]
            da = jnp.where(lane < N_HEADS, dg * ea * _sigmoid(z), 0.0)
            gval = jnp.where(lane < N_HEADS, ea * _softplus(z), 0.0)
            beta = _sigmoid(x)
            db = jnp.where(jnp.logical_and(lane >= N_HEADS, lane < 2 * N_HEADS), g_ref[rows, :] * beta * (1.0 - beta), 0.0)
            dx_ref[rows, :] = (da + db).astype(dx_ref.dtype)
            dal_ref[...] += jnp.sum(dg * gval, axis=0, keepdims=True)
            ddt_ref[...] += jnp.sum(da, axis=0, keepdims=True)
            return carry

        lax.fori_loop(0, t // CHUNK, step, 0)

    vec = pl.BlockSpec((1, HEAD_DIM), lambda i: (0, 0))
    full = pl.BlockSpec((t, HEAD_DIM), lambda i: (0, 0))
    return pl.pallas_call(
        body, name="dn_gates_bwd", grid=(1,),
        in_specs=[pl.BlockSpec((t, HEAD_DIM), lambda i: (0, TAIL_BLK)), vec, vec, full],
        out_specs=[full, vec, vec],
        out_shape=[jax.ShapeDtypeStruct((t, HEAD_DIM), MM), jax.ShapeDtypeStruct((1, HEAD_DIM), F32),
                   jax.ShapeDtypeStruct((1, HEAD_DIM), F32)],
        compiler_params=_cparams(("arbitrary",)),
    )(proj, a_log, dt_bias, dgates)


def _dn_intra(q, k, v, gcol, grow, bcol):
    r, c = _iota2(CHUNK, CHUNK)
    causal, strict = r >= c, r > c
    decay = jnp.where(causal, jnp.exp(jnp.where(causal, gcol - grow, 0.0)), 0.0)
    kb = k * bcol
    a = jnp.where(strict, mmul(kb, k, 1, 1, False) * decay, 0.0)
    tm = tri_inv(a)
    u = mmul(tm, v * bcol, 1, 0, False)
    w = mmul(tm, kb * jnp.exp(gcol), 1, 0, False)
    qk = jnp.where(causal, mmul(q, k, 1, 1, False) * decay, 0.0)
    rr = lax.broadcasted_iota(jnp.int32, (CHUNK, 1), 0)
    g_last = jnp.sum(jnp.where(rr == CHUNK - 1, gcol, 0.0), axis=0, keepdims=True)
    return u, w, q * jnp.exp(gcol), k * jnp.exp(g_last - gcol), qk, jnp.exp(g_last)


def _dn_scan(u, w, qg, kd, qk, eg, state):
    v_new = u - mmul(w, state, 1, 0, False)
    out = mmul(qg, state, 1, 0, False) + mmul(qk, v_new, 1, 0, False)
    return out, state * eg + mmul(kd, v_new, 0, 0, False)


DN_HEADS_PER_STEP = 1
DN_GROUP = 8
DN_PARTS = ((CHUNK, HEAD_DIM),) * 4 + ((CHUNK, CHUNK), (1, 1))


def _dn_scratch(hb, nc):
    return [pltpu.VMEM((hb, nc) + shape, F32) for shape in DN_PARTS]


def _dn_group(nc):
    return min(DN_GROUP, nc)


def _dn_group_args(refs, j, g, grp):
    q_ref, k_ref, v_ref, gc_ref, gr_ref, bc_ref = refs
    rows = pl.ds(pl.multiple_of(g * (grp * CHUNK), grp * CHUNK), grp * CHUNK)
    cs = pl.ds(g * grp, grp)
    split = lambda ref: ref[rows, _lanes(j)].reshape(grp, CHUNK, HEAD_DIM)
    return split(q_ref), split(k_ref), split(v_ref), gc_ref[j, cs], gr_ref[j, cs], bc_ref[j, cs]


def _dn_intra_all(refs, parts, hb, nc):
    grp = _dn_group(nc)

    def group(g, carry):
        cs = pl.ds(g * grp, grp)
        for j in range(hb):
            for part, val in zip(parts, jax.vmap(_dn_intra)(*_dn_group_args(refs, j, g, grp))):
                part[j, cs] = val
        return carry

    lax.fori_loop(0, nc // grp, group, 0)


def _dn_specs(t):
    nc, hb = t // CHUNK, DN_HEADS_PER_STEP
    head = lambda which: pl.BlockSpec((None, t, hb * HEAD_DIM), lambda h: (which, 0, h))
    flat = pl.BlockSpec((t, hb * HEAD_DIM), lambda h: (0, h))
    col = pl.BlockSpec((hb, nc, CHUNK, 1), lambda h: (h, 0, 0, 0))
    row = pl.BlockSpec((hb, nc, 1, CHUNK), lambda h: (h, 0, 0, 0))
    st = pl.BlockSpec((hb, nc, HEAD_DIM, HEAD_DIM), lambda h: (h, 0, 0, 0))
    return nc, hb, head, flat, col, row, st


def dn_core_fwd(qkv, gcol, grow, bcol):
    t = qkv.shape[1]
    nc, hb, head, flat, col, row, st = _dn_specs(t)

    def body(q_ref, k_ref, v_ref, gc_ref, gr_ref, bc_ref, o_ref, s_ref, *parts):
        _dn_intra_all((q_ref, k_ref, v_ref, gc_ref, gr_ref, bc_ref), parts, hb, nc)

        def step(c, states):
            rows = pl.ds(pl.multiple_of(c * CHUNK, CHUNK), CHUNK)
            new_states = []
            for j in range(hb):
                s_ref[j, c] = states[j]
                out, new_state = _dn_scan(*[part[j, c] for part in parts], states[j])
                o_ref[rows, _lanes(j)] = out
                new_states.append(new_state)
            return tuple(new_states)

        lax.fori_loop(0, nc, step, tuple(jnp.zeros((HEAD_DIM, HEAD_DIM), F32) for _ in range(hb)))

    return pl.pallas_call(
        body, name="dn_core_fwd", grid=(N_HEADS // hb,),
        in_specs=[head(0), head(1), head(2), col, row, col], out_specs=[flat, st],
        out_shape=[jax.ShapeDtypeStruct((t, D_MODEL), F32), jax.ShapeDtypeStruct((N_HEADS, nc, HEAD_DIM, HEAD_DIM), F32)],
        scratch_shapes=_dn_scratch(hb, nc), compiler_params=_cparams(("parallel",)),
    )(qkv, qkv, qkv, gcol, grow, bcol)


def dn_core_bwd(qkv, gcol, grow, bcol, states, do):
    t = qkv.shape[1]
    nc, hb, head, flat, col, row, st = _dn_specs(t)

    def body(q_ref, k_ref, v_ref, gc_ref, gr_ref, bc_ref, s_ref, do_ref, dqkv_ref, dgc_ref, dgr_ref, dbc_ref, *scratch):
        parts, dparts = scratch[:len(DN_PARTS)], scratch[len(DN_PARTS):]
        refs = (q_ref, k_ref, v_ref, gc_ref, gr_ref, bc_ref)
        _dn_intra_all(refs, parts, hb, nc)

        def step(i, dstates):
            c = nc - 1 - i
            rows = pl.ds(pl.multiple_of(c * CHUNK, CHUNK), CHUNK)
            dstates_in = []
            for j in range(hb):
                _, vjp = jax.vjp(_dn_scan, *[part[j, c] for part in parts], s_ref[j, c])
                *dvals, dstate_in = vjp((do_ref[rows, _lanes(j)], dstates[j]))
                for dpart, dval in zip(dparts, dvals):
                    dpart[j, c] = dval
                dstates_in.append(dstate_in)
            return tuple(dstates_in)

        lax.fori_loop(0, nc, step, tuple(jnp.zeros((HEAD_DIM, HEAD_DIM), F32) for _ in range(hb)))

        grp = _dn_group(nc)

        def group(g, carry):
            rows = pl.ds(pl.multiple_of(g * (grp * CHUNK), grp * CHUNK), grp * CHUNK)
            cs = pl.ds(g * grp, grp)
            for j in range(hb):
                _, vjp = jax.vjp(jax.vmap(_dn_intra), *_dn_group_args(refs, j, g, grp))
                dq, dk, dv, dgc, dgr, dbc = vjp(tuple(dpart[j, cs] for dpart in dparts))
                for which, val in enumerate((dq, dk, dv)):
                    dqkv_ref[which, rows, _lanes(j)] = val.reshape(grp * CHUNK, HEAD_DIM)
                dgc_ref[j, cs] = dgc
                dgr_ref[j, cs] = dgr
                dbc_ref[j, cs] = dbc
            return carry

        lax.fori_loop(0, nc // grp, group, 0)

    return pl.pallas_call(
        body, name="dn_core_bwd", grid=(N_HEADS // hb,), scratch_shapes=_dn_scratch(hb, nc) * 2,
        in_specs=[head(0), head(1), head(2), col, row, col, st, flat],
        out_specs=[pl.BlockSpec((3, t, hb * HEAD_DIM), lambda h: (0, 0, h)), col, row, col],
        out_shape=[jax.ShapeDtypeStruct((3, t, D_MODEL), F32)] + [
            jax.ShapeDtypeStruct((N_HEADS, nc, CHUNK, 1), F32), jax.ShapeDtypeStruct((N_HEADS, nc, 1, CHUNK), F32),
            jax.ShapeDtypeStruct((N_HEADS, nc, CHUNK, 1), F32)],
        compiler_params=_cparams(("parallel",)),
    )(qkv, qkv, qkv, gcol, grow, bcol, states, do)


def gates_to_heads(gates):
    t = gates.shape[0]
    nc = t // CHUNK
    g = gates[:, :N_HEADS].T.reshape(N_HEADS, nc, CHUNK)
    b = gates[:, N_HEADS:2 * N_HEADS].T.reshape(N_HEADS, nc, CHUNK)
    return g[..., None], g[:, :, None, :], b[..., None]


def heads_to_gates(dgcol, dgrow, dbcol):
    nh, nc = dgcol.shape[:2]
    dg = (dgcol[..., 0] + dgrow[:, :, 0, :]).reshape(nh, nc * CHUNK).T
    db = dbcol[..., 0].reshape(nh, nc * CHUNK).T
    return jnp.concatenate([dg, db, jnp.zeros((nc * CHUNK, HEAD_DIM - 2 * nh), F32)], axis=1)


def _dn_out(o, z, w):
    return _rms(o, w) * _silu(z)


def _gate_specs():
    o_spec = pl.BlockSpec((ROWS, D_MODEL), lambda i: (i, 0))
    z_spec = pl.BlockSpec((ROWS, D_MODEL), lambda i: (i, 3))
    w_spec = pl.BlockSpec((1, HEAD_DIM), lambda i: (0, 0))
    return o_spec, z_spec, w_spec


def dn_out_fwd(o, proj, w):
    t = o.shape[0]
    o_spec, z_spec, w_spec = _gate_specs()

    def body(o_ref, z_ref, w_ref, y_ref):
        for h in range(N_HEADS):
            y_ref[:, _lanes(h)] = _dn_out(o_ref[:, _lanes(h)], z_ref[:, _lanes(h)], w_ref[...]).astype(y_ref.dtype)

    return pl.pallas_call(
        body, name="dn_out_fwd", grid=(t // ROWS,), in_specs=[o_spec, z_spec, w_spec], out_specs=o_spec,
        out_shape=jax.ShapeDtypeStruct((t, D_MODEL), MM), compiler_params=_cparams(("parallel",)),
    )(o, proj, w)


def dn_out_bwd(o, proj, w, dcat):
    t = o.shape[0]
    o_spec, z_spec, w_spec = _gate_specs()

    def body(o_ref, z_ref, w_ref, g_ref, do_ref, dz_ref, dw_ref):
        dw_sum = jnp.zeros((1, HEAD_DIM), F32)
        for h in range(N_HEADS):
            _, vjp = jax.vjp(_dn_out, o_ref[:, _lanes(h)], z_ref[:, _lanes(h)], w_ref[...])
            do, dz, dw = vjp(g_ref[:, _lanes(h)].astype(F32))
            do_ref[:, _lanes(h)] = do
            dz_ref[:, _lanes(h)] = dz.astype(dz_ref.dtype)
            dw_sum = dw_sum + dw
        _acc(dw_ref, dw_sum, pl.program_id(0) == 0)

    return pl.pallas_call(
        body, name="dn_out_bwd", grid=(t // ROWS,), in_specs=[o_spec, z_spec, w_spec, o_spec],
        out_specs=[o_spec, o_spec, w_spec],
        out_shape=[jax.ShapeDtypeStruct((t, D_MODEL), F32), jax.ShapeDtypeStruct((t, D_MODEL), MM),
                   jax.ShapeDtypeStruct((1, HEAD_DIM), F32)],
        compiler_params=_cparams(("arbitrary",)),
    )(o, proj, w, dcat)


def _fox_norm(x, w, scale):
    return _rms(x, w) * scale


def _fox_prep_specs():
    x_spec = pl.BlockSpec((ROWS, 2 * D_MODEL), lambda i: (i, 0))
    w_spec = pl.BlockSpec((2, 1, HEAD_DIM), lambda i: (0, 0, 0))
    y_spec = pl.BlockSpec((2, ROWS, D_MODEL), lambda i: (0, i, 0))
    return x_spec, w_spec, y_spec


def fox_prep_fwd(proj, wqk):
    t = proj.shape[0]
    x_spec, w_spec, y_spec = _fox_prep_specs()

    def body(x_ref, w_ref, y_ref):
        for j in range(2 * N_HEADS):
            which, scale = j // N_HEADS, (QSCALE if j < N_HEADS else 1.0)
            y_ref[which, :, _lanes(j % N_HEADS)] = _fox_norm(x_ref[:, _lanes(j)], w_ref[which], scale).astype(y_ref.dtype)

    return pl.pallas_call(
        body, name="fox_prep_fwd", grid=(t // ROWS,), in_specs=[x_spec, w_spec], out_specs=y_spec,
        out_shape=jax.ShapeDtypeStruct((2, t, D_MODEL), MM), compiler_params=_cparams(("parallel",)),
    )(proj, wqk)


def fox_prep_bwd(proj, wqk, dq, dk):
    t = proj.shape[0]
    x_spec, w_spec, _ = _fox_prep_specs()
    g_spec = pl.BlockSpec((ROWS, D_MODEL), lambda i: (i, 0))

    def body(x_ref, w_ref, dq_ref, dk_ref, dx_ref, dw_ref):
        dws = [jnp.zeros((1, HEAD_DIM), F32), jnp.zeros((1, HEAD_DIM), F32)]
        for j in range(2 * N_HEADS):
            which, scale = j // N_HEADS, (QSCALE if j < N_HEADS else 1.0)
            g_ref = dq_ref if which == 0 else dk_ref
            _, vjp = jax.vjp(lambda x, w: _fox_norm(x, w, scale), x_ref[:, _lanes(j)], w_ref[which])
            dx, dw = vjp(g_ref[:, _lanes(j % N_HEADS)])
            dx_ref[:, _lanes(j)] = dx.astype(dx_ref.dtype)
            dws[which] = dws[which] + dw
        first = pl.program_id(0) == 0
        _acc(dw_ref.at[0], dws[0], first)
        _acc(dw_ref.at[1], dws[1], first)

    return pl.pallas_call(
        body, name="fox_prep_bwd", grid=(t // ROWS,), in_specs=[x_spec, w_spec, g_spec, g_spec],
        out_specs=[x_spec, w_spec],
        out_shape=[jax.ShapeDtypeStruct((t, 2 * D_MODEL), MM), jax.ShapeDtypeStruct((2, 1, HEAD_DIM), F32)],
        compiler_params=_cparams(("arbitrary",)),
    )(proj, wqk, dq, dk)


def _row_pick(x, i):
    r = lax.broadcasted_iota(jnp.int32, x.shape, 0)
    return jnp.sum(jnp.where(r == i, x, 0.0), axis=0, keepdims=True)


def fox_gates_fwd(proj, f_bias):
    t = proj.shape[0]
    blk = HEAD_DIM

    def body(x_ref, b_ref, o_ref):
        lane = lax.broadcasted_iota(jnp.int32, (blk, HEAD_DIM), 1)
        tri = _tri_ones(blk, False)

        def step(c, carry):
            rows = pl.ds(pl.multiple_of(c * blk, blk), blk)
            lf = jnp.where(lane < N_HEADS, -_softplus(-(x_ref[rows, :] + b_ref[...])), 0.0)
            cum = _dot(tri, lf, 1, 0, True) + carry
            o_ref[rows, :] = cum
            return _row_pick(cum, blk - 1)

        lax.fori_loop(0, t // blk, step, jnp.zeros((1, HEAD_DIM), F32))

    vec = pl.BlockSpec((1, HEAD_DIM), lambda i: (0, 0))
    return pl.pallas_call(
        body, name="fox_gates_fwd", grid=(1,),
        in_specs=[pl.BlockSpec((t, HEAD_DIM), lambda i: (0, TAIL_BLK)), vec],
        out_specs=pl.BlockSpec((t, HEAD_DIM), lambda i: (0, 0)),
        out_shape=jax.ShapeDtypeStruct((t, HEAD_DIM), F32), compiler_params=_cparams(("arbitrary",)),
    )(proj, f_bias)


def fox_gates_bwd(proj, f_bias, dfcum):
    t = proj.shape[0]
    blk = HEAD_DIM
    nb = t // blk

    def body(x_ref, b_ref, g_ref, dx_ref, db_ref):
        lane = lax.broadcasted_iota(jnp.int32, (blk, HEAD_DIM), 1)
        tri = _tri_ones(blk, True)
        db_ref[...] = jnp.zeros_like(db_ref)

        def step(i, carry):
            c = nb - 1 - i
            rows = pl.ds(pl.multiple_of(c * blk, blk), blk)
            g = jnp.where(lane < N_HEADS, g_ref[rows, :], 0.0)
            dlf = _dot(tri, g, 1, 0, True) + carry
            dx = jnp.where(lane < N_HEADS, dlf * _sigmoid(-(x_ref[rows, :] + b_ref[...])), 0.0)
            dx_ref[rows, :] = dx.astype(dx_ref.dtype)
            db_ref[...] += jnp.sum(dx, axis=0, keepdims=True)
            return carry + jnp.sum(g, axis=0, keepdims=True)

        lax.fori_loop(0, nb, step, jnp.zeros((1, HEAD_DIM), F32))

    vec = pl.BlockSpec((1, HEAD_DIM), lambda i: (0, 0))
    full = pl.BlockSpec((t, HEAD_DIM), lambda i: (0, 0))
    return pl.pallas_call(
        body, name="fox_gates_bwd", grid=(1,),
        in_specs=[pl.BlockSpec((t, HEAD_DIM), lambda i: (0, TAIL_BLK)), vec, full], out_specs=[full, vec],
        out_shape=[jax.ShapeDtypeStruct((t, HEAD_DIM), MM), jax.ShapeDtypeStruct((1, HEAD_DIM), F32)],
        compiler_params=_cparams(("arbitrary",)),
    )(proj, f_bias, dfcum)


def fcum_to_heads(fcum):
    f = fcum[:, :N_HEADS].T
    return f[:, :, None], f[:, None, :]


def heads_to_fcum(dfcol, dfrow):
    d = (dfcol[:, :, 0] + dfrow[:, 0, :]).T
    return jnp.concatenate([d, jnp.zeros((d.shape[0], HEAD_DIM - N_HEADS), F32)], axis=1)


def _fox_tq(t):
    return min(t, 256)


def _fox_specs(t):
    tq = _fox_tq(t)
    q_spec = pl.BlockSpec((None, tq, HEAD_DIM), lambda h, i: (0, i, h))
    k_spec = pl.BlockSpec((None, t, HEAD_DIM), lambda h, i: (1, 0, h))
    v_spec = pl.BlockSpec((t, HEAD_DIM), lambda h, i: (0, 2 * N_HEADS + h))
    gate_spec = pl.BlockSpec((tq, HEAD_DIM), lambda h, i: (i, 3 * N_HEADS + h))
    col_spec = pl.BlockSpec((None, tq, 1), lambda h, i: (h, i, 0))
    row_spec = pl.BlockSpec((None, 1, t), lambda h, i: (h, 0, 0))
    blk_spec = pl.BlockSpec((tq, HEAD_DIM), lambda h, i: (i, h))
    head_spec = pl.BlockSpec((t, HEAD_DIM), lambda h, i: (0, h))
    return tq, q_spec, k_spec, v_spec, gate_spec, col_spec, row_spec, blk_spec, head_spec


def _fox_scores(q, k, fcol, frow, i, tq, t):
    s = _dot(q, k, 1, 1, False) + (fcol - frow)
    r = lax.broadcasted_iota(jnp.int32, (tq, t), 0) + i * tq
    c = lax.broadcasted_iota(jnp.int32, (tq, t), 1)
    return s, c <= r


def fox_attn_fwd(qk, proj, fcol, frow):
    t = proj.shape[0]
    tq, q_spec, k_spec, v_spec, gate_spec, col_spec, row_spec, blk_spec, _ = _fox_specs(t)

    def body(q_ref, k_ref, v_ref, gate_ref, fc_ref, fr_ref, mix_ref, o_ref, lse_ref):
        def block(i):
            w = (i + 1) * tq
            s, mask = _fox_scores(q_ref[...], k_ref[0:w, :], fc_ref[...], fr_ref[:, 0:w], i, tq, w)
            s = jnp.where(mask, s, -1e30)
            m = jnp.max(s, axis=-1, keepdims=True)
            p = jnp.where(mask, jnp.exp(s - m), 0.0)
            l = jnp.sum(p, axis=-1, keepdims=True)
            o = _dot(p, v_ref[0:w, :], 1, 0, False) / l
            o_ref[...] = o
            mix_ref[...] = (o * _sigmoid(gate_ref[...])).astype(mix_ref.dtype)
            lse_ref[...] = m + jnp.log(l)

        for i in range(t // tq):
            pl.when(pl.program_id(1) == i)(functools.partial(block, i))

    return pl.pallas_call(
        body, name="fox_attn_fwd", grid=(N_HEADS, t // tq),
        in_specs=[q_spec, k_spec, v_spec, gate_spec, col_spec, row_spec], out_specs=[blk_spec, blk_spec, col_spec],
        out_shape=[jax.ShapeDtypeStruct((t, D_MODEL), MM), jax.ShapeDtypeStruct((t, D_MODEL), F32),
                   jax.ShapeDtypeStruct((N_HEADS, t, 1), F32)],
        compiler_params=_cparams(("parallel", "parallel")),
    )(qk, qk, proj, proj, fcol, frow)


def fox_attn_bwd(qk, proj, fcol, frow, o, lse, dcat):
    t = proj.shape[0]
    tq, q_spec, k_spec, v_spec, gate_spec, col_spec, row_spec, blk_spec, head_spec = _fox_specs(t)

    def body(q_ref, k_ref, v_ref, gate_ref, fc_ref, fr_ref, o_ref, lse_ref, g_ref,
             dq_ref, dk_ref, dv_ref, dgate_ref, dfc_ref, dfr_ref):
        @pl.when(pl.program_id(1) == 0)
        def _():
            dk_ref[...] = jnp.zeros_like(dk_ref)
            dv_ref[...] = jnp.zeros_like(dv_ref)
            dfr_ref[...] = jnp.zeros_like(dfr_ref)

        def block(i):
            w = (i + 1) * tq
            sg = _sigmoid(gate_ref[...])
            g = g_ref[...].astype(F32)
            o_pre = o_ref[...]
            do = g * sg
            dgate_ref[...] = (g * o_pre * sg * (1.0 - sg)).astype(dgate_ref.dtype)
            s, mask = _fox_scores(q_ref[...], k_ref[0:w, :], fc_ref[...], fr_ref[:, 0:w], i, tq, w)
            p = jnp.where(mask, jnp.exp(jnp.where(mask, s, 0.0) - lse_ref[...]), 0.0)
            dp = _dot(do, v_ref[0:w, :], 1, 1, False)
            delta = jnp.sum(do * o_pre, axis=-1, keepdims=True)
            ds = p * (dp - delta)
            dq_ref[...] = _dot(ds, k_ref[0:w, :], 1, 0, False)
            dk_ref[0:w, :] += _dot(ds, q_ref[...], 0, 0, False)
            dv_ref[0:w, :] += _dot(p, do, 0, 0, False)
            dfc_ref[...] = jnp.sum(ds, axis=-1, keepdims=True)
            dfr_ref[:, 0:w] += -jnp.sum(ds, axis=0, keepdims=True)

        for i in range(t // tq):
            pl.when(pl.program_id(1) == i)(functools.partial(block, i))

    f32 = lambda *s: jax.ShapeDtypeStruct(s, F32)
    return pl.pallas_call(
        body, name="fox_attn_bwd", grid=(N_HEADS, t // tq),
        in_specs=[q_spec, k_spec, v_spec, gate_spec, col_spec, row_spec, blk_spec, col_spec, blk_spec],
        out_specs=[blk_spec, head_spec, head_spec, blk_spec, col_spec, row_spec],
        out_shape=[f32(t, D_MODEL), f32(t, D_MODEL), f32(t, D_MODEL), jax.ShapeDtypeStruct((t, D_MODEL), MM),
                   f32(N_HEADS, t, 1), f32(N_HEADS, 1, t)],
        compiler_params=_cparams(("parallel", "arbitrary")),
    )(qk, qk, proj, proj, fcol, frow, o, lse, dcat)


def adamw(w, g, m, v, *, name):
    r, c = w.shape
    rb = ROWS if r % ROWS == 0 else r

    def body(w_ref, g_ref, m_ref, v_ref, d_ref, nm_ref, nv_ref):
        g_ = g_ref[...]
        m_ = ADAM_B1 * m_ref[...] + (1.0 - ADAM_B1) * g_
        v_ = ADAM_B2 * v_ref[...] + (1.0 - ADAM_B2) * jnp.square(g_)
        m_hat = m_ / (1.0 - ADAM_B1 ** ADAM_STEP)
        v_hat = v_ / (1.0 - ADAM_B2 ** ADAM_STEP)
        d_ref[...] = -ADAM_LR * (m_hat / (jnp.sqrt(v_hat) + ADAM_EPS) + ADAM_WD * w_ref[...])
        nm_ref[...] = m_
        nv_ref[...] = v_

    blk = pl.BlockSpec((rb, c), lambda i: (i, 0))
    shp = jax.ShapeDtypeStruct((r, c), F32)
    return pl.pallas_call(body, name=name, grid=(r // rb,), in_specs=[blk] * 4, out_specs=[blk] * 3,
                          out_shape=[shp] * 3, compiler_params=_cparams(("parallel",)))(w, g, m, v)


def _place():
    x, y, c = lax.axis_index("x"), lax.axis_index("y"), lax.axis_index("c")
    return x, y, c, [(1 - x, y), (x, 1 - y), (1 - x, 1 - y)]


ANY = pl.BlockSpec(memory_space=pl.ANY)


def all_reduce_small(v):
    r, w = v.shape

    def body(v_ref, o_ref, buf, send_sems, recv_sems):
        x, y, c, _ = _place()
        me = 4 * x + 2 * y + c
        flip = lambda a, bit: 1 - a if bit else a
        cps = []
        for k in range(1, N_DEV):
            peer = (flip(x, k & 4), flip(y, k & 2), flip(c, k & 1))
            cp = pltpu.make_async_remote_copy(src_ref=v_ref, dst_ref=buf.at[me], send_sem=send_sems.at[k - 1],
                                              recv_sem=recv_sems.at[k - 1], device_id=peer, device_id_type=MESH)
            cp.start()
            cps.append((cp, 4 * peer[0] + 2 * peer[1] + peer[2]))
        buf[me] = v_ref[...]
        for k, (cp, peer_id) in enumerate(cps):
            pltpu.make_async_remote_copy(src_ref=v_ref, dst_ref=buf.at[peer_id], send_sem=send_sems.at[k],
                                         recv_sem=recv_sems.at[k], device_id=(x, y, c), device_id_type=MESH).wait_recv()
        for cp, _ in cps:
            cp.wait_send()
        acc = buf[0]
        for d in range(1, N_DEV):
            acc = acc + buf[d]
        o_ref[...] = acc

    vm = pl.BlockSpec(memory_space=pltpu.VMEM)
    return pl.pallas_call(
        body, name="all_reduce_small", in_specs=[vm], out_specs=vm, out_shape=jax.ShapeDtypeStruct((r, w), F32),
        scratch_shapes=[pltpu.VMEM((N_DEV, r, w), F32), pltpu.SemaphoreType.DMA((N_DEV - 1,)),
                        pltpu.SemaphoreType.DMA((N_DEV - 1,))],
    )(v)


def _vec8(v):
    return jnp.zeros((1, HEAD_DIM), F32).at[0, :N_HEADS].set(v.reshape(N_HEADS))


def _layer_fwd(i, x_in, wt, sm, mem_k, mem_v, late=None):
    tag = f"l{i}_"
    h = rms_fwd(x_in, sm["norm1_w"][i][None], name=tag + "rms1")
    w_in = wt["dn_w_in"] if i == 0 else wt["fox_w_in"]
    proj = matmul(h, w_in, name=tag + "proj", tm=256, tk=1024)
    sv = dict(x_in=x_in, h=h, proj=proj)
    if i == 0:
        qkv = dn_prep_fwd(proj, wt["conv_w"])
        gates = dn_gates_fwd(proj, _vec8(sm["dn_a_log"]), _vec8(sm["dn_dt_bias"]))
        gcol, grow, bcol = gates_to_heads(gates)
        o, states = dn_core_fwd(qkv, gcol, grow, bcol)
        mix = dn_out_fwd(o, proj, sm["dn_o_norm_w"])
        sv.update(qkv=qkv, gcol=gcol, grow=grow, bcol=bcol, states=states, o=o)
    else:
        wqk = jnp.stack([sm["fox_q_norm_w"], sm["fox_k_norm_w"]])
        qk = fox_prep_fwd(proj, wqk)
        fcum = fox_gates_fwd(proj, _vec8(sm["fox_f_bias"]))
        fcol, frow = fcum_to_heads(fcum)
        mix, o, lse = fox_attn_fwd(qk, proj, fcol, frow)
        sv.update(wqk=wqk, qk=qk, fcol=fcol, frow=frow, o=o, lse=lse)
    mem_out = memattn_fwd(proj, sm["memq_norm_w"][i][None], mem_k, mem_v, name=tag + "memattn_fwd")
    cat = jnp.concatenate([mix, mem_out], axis=1)
    if late is not None:
        wt.update(late(cat))
    x_mid = matmul(cat, wt["w_out"][i], res=x_in, name=tag + "out_proj")
    h2 = rms_fwd(x_mid, sm["norm2_w"][i][None], name=tag + "rms2")
    ff = matmul(h2, wt["w_mlp1"][i], b_slots=True, name=tag + "mlp1")
    act = act_fwd(ff, name=tag + "act_fwd")
    x_out = matmul(act, wt["w_mlp2"][i], res=x_mid, name=tag + "mlp2")
    sv.update(cat=cat, x_mid=x_mid, h2=h2, ff=ff, act=act)
    return x_out, sv


def _layer_bwd(i, dx_out, sv, wt, sm, mem_k, mem_v):
    tag = f"l{i}_"
    big, small = {}, {}
    dact = matmul(dx_out, wt["w_mlp2"][i], tb=True, name=tag + "d_act")
    big["w_mlp2"] = matmul(sv["act"], dx_out, ta=True, name=tag + "d_w_mlp2")
    dff = act_bwd(sv["ff"], dact, name=tag + "act_bwd")
    dh2 = matmul(dff, wt["w_mlp1"][i], tb=True, b_slots=True, name=tag + "d_h2")
    big["w_mlp1"] = matmul(sv["h2"], dff, ta=True, name=tag + "d_w_mlp1", tm=512, tn=D_FF, tk=512)
    dx_mid, small["norm2_w"] = rms_bwd(sv["x_mid"], sm["norm2_w"][i][None], dh2, dx_out, name=tag + "rms2_bwd")
    dcat = matmul(dx_mid, wt["w_out"][i], tb=True, name=tag + "d_cat")
    big["w_out"] = matmul(sv["cat"], dx_mid, ta=True, name=tag + "d_w_out")
    proj = sv["proj"]
    dqm, small["memq_norm_w"], dmk, dmv = memattn_bwd(proj, sm["memq_norm_w"][i][None], mem_k, mem_v, dcat,
                                                      name=tag + "memattn_bwd")
    t = proj.shape[0]
    pad = jnp.zeros((t, PROJ_W - TAIL - HEAD_DIM), MM)
    if i == 0:
        do, dz, small["dn_o_norm_w"] = dn_out_bwd(sv["o"], proj, sm["dn_o_norm_w"], dcat)
        dqkv, dgc, dgr, dbc = dn_core_bwd(sv["qkv"], sv["gcol"], sv["grow"], sv["bcol"], sv["states"], do)
        dtail, dal, ddt = dn_gates_bwd(proj, _vec8(sm["dn_a_log"]), _vec8(sm["dn_dt_bias"]), heads_to_gates(dgc, dgr, dbc))
        dmain, dconv = dn_prep_bwd(proj, wt["conv_w"], dqkv)
        small["dn_a_log"], small["dn_dt_bias"] = dal[:, :N_HEADS], ddt[:, :N_HEADS]
        big["conv_w"] = dconv[:4]
        dproj = jnp.concatenate([dmain, dz, dqm, dtail, pad], axis=1)
    else:
        dq, dk, dv, dgate, dfc, dfr = fox_attn_bwd(sv["qk"], proj, sv["fcol"], sv["frow"], sv["o"], sv["lse"], dcat)
        dtail, dfb = fox_gates_bwd(proj, _vec8(sm["fox_f_bias"]), heads_to_fcum(dfc, dfr))
        dqk, dwqk = fox_prep_bwd(proj, sv["wqk"], dq, dk)
        small["fox_f_bias"] = dfb[:, :N_HEADS]
        small["fox_q_norm_w"], small["fox_k_norm_w"] = dwqk[0], dwqk[1]
        dproj = jnp.concatenate([dqk, dv.astype(MM), dgate, dqm, dtail, pad], axis=1)
    w_in = wt["dn_w_in"] if i == 0 else wt["fox_w_in"]
    dh = matmul(dproj, w_in, tb=True, name=tag + "d_h", tm=512)
    big["w_in"] = matmul(sv["h"], dproj, ta=True, name=tag + "d_w_in", tm=256)
    dx_in, small["norm1_w"] = rms_bwd(sv["x_in"], sm["norm1_w"][i][None], dh, dx_mid, name=tag + "rms1_bwd")
    return dx_in, big, small, (dmk, dmv)


def local_step(x, mem, target, wt, sm, late=None):
    wt = dict(wt)
    mem_k, mem_v = mem_fwd(mem, sm["mem_norm_w"][None], wt["w_mem_kv"], sm["mem_k_norm_w"][None])
    x0, sv0 = _layer_fwd(0, x, wt, sm, mem_k, mem_v, late)
    x1, sv1 = _layer_fwd(1, x0, wt, sm, mem_k, mem_v)
    dy, loss = loss_fwd(x1, target, name="loss")
    dx1, big1, small1, dm1 = _layer_bwd(1, dy, sv1, wt, sm, mem_k, mem_v)
    dx0, big0, small0, dm0 = _layer_bwd(0, dx1, sv0, wt, sm, mem_k, mem_v)
    dwn, dwkv, dwkn = mem_bwd(mem, sm["mem_norm_w"][None], wt["w_mem_kv"], sm["mem_k_norm_w"][None], *dm0, *dm1)
    small = dict(mem_norm_w=dwn[0], mem_k_norm_w=dwkn[0],
                 norm1_w=jnp.concatenate([small0["norm1_w"], small1["norm1_w"]]),
                 norm2_w=jnp.concatenate([small0["norm2_w"], small1["norm2_w"]]),
                 memq_norm_w=jnp.concatenate([small0["memq_norm_w"], small1["memq_norm_w"]]),
                 dn_a_log=small0["dn_a_log"], dn_dt_bias=small0["dn_dt_bias"], dn_o_norm_w=small0["dn_o_norm_w"],
                 fox_f_bias=small1["fox_f_bias"], fox_q_norm_w=small1["fox_q_norm_w"], fox_k_norm_w=small1["fox_k_norm_w"])
    big = dict(w_mem_kv=dwkv, dn_w_in=big0["w_in"], fox_w_in=big1["w_in"], conv_w=big0["conv_w"],
               w_out=[big0["w_out"], big1["w_out"]], w_mlp1=[big0["w_mlp1"], big1["w_mlp1"]],
               w_mlp2=[big0["w_mlp2"], big1["w_mlp2"]])
    return loss, dx0, big, small


def w_in_to_kernel(w, n_scalars):
    pad = jnp.zeros((w.shape[0], PROJ_W - TAIL - n_scalars), w.dtype)
    return jnp.concatenate([w[:, :4096], w[:, 4096 + n_scalars:], w[:, 4096:4096 + n_scalars], pad], axis=1)


def w_in_from_kernel(w, n_scalars):
    return jnp.concatenate([w[:, :4096], w[:, TAIL:TAIL + n_scalars], w[:, 4096:TAIL]], axis=1)


BIG_SPECS = dict(w_mem_kv=("rows", 1, 256, 1024), w_out=("rows", 2, 384, 1024), w_mlp2=("rows", 2, 1024, 1024),
                 w_mlp1=("cols", 2, 1024, 1024), dn_w_in=("rows", 1, 1024, 1156), fox_w_in=("rows", 1, 1024, 1154))
BIG_NAMES = tuple(BIG_SPECS)
EARLY_NAMES = ("w_mem_kv", "dn_w_in")
LATE_NAMES = ("w_out", "w_mlp2", "w_mlp1", "fox_w_in")


def _full_shape(name, half=False):
    kind, a, b, c = BIG_SPECS[name]
    b = b // 2 if half else b
    return (a, N_CHIP, b, c) if kind == "rows" else (a, b, N_CHIP * c)


def _ds(start, size, align):
    return pl.ds(start if isinstance(start, int) else pl.multiple_of(start, align), size)


def _half_rows(name, h):
    b = BIG_SPECS[name][2]
    return _ds(h * (b // 2), b // 2, 16)


def _shard_idx(name, h):
    return (slice(None), _half_rows(name, h), slice(None))


def _full_idx(name, j=None, h=None):
    kind, _, _, c = BIG_SPECS[name]
    rows = slice(None) if h is None else _half_rows(name, h)
    if kind == "rows":
        return (slice(None), slice(None) if j is None else j, rows, slice(None))
    return (slice(None), rows, slice(None) if j is None else _ds(j * c, c, 128))


def _slots_shape(name):
    _, a, b, c = BIG_SPECS[name]
    return (a, N_CHIP, b, c)


def _slots_idx(name, j, h):
    return (slice(None), j, _half_rows(name, h), slice(None))


def _row_block(name):
    hs = BIG_SPECS[name][2] // 2
    return hs if hs <= ROWS else ROWS


def _remote(src, dst, send_sem, recv_sem, to):
    return pltpu.make_async_remote_copy(src_ref=src, dst_ref=dst, send_sem=send_sem, recv_sem=recv_sem, device_id=to,
                                        device_id_type=MESH)


HBM = pl.BlockSpec(memory_space=pltpu.HBM)
SEM = pl.BlockSpec(memory_space=pltpu.SEMAPHORE)
EFFECT = pltpu.CompilerParams(has_side_effects=pltpu.SideEffectType.DATAFLOW_SIDE_EFFECTING)


def _in_hbm(a):
    return pltpu.with_memory_space_constraint(a, pltpu.HBM)


def _chip_copies(names, ins, lands, send_sems, recv_sems):
    x, y, c, chips = _place()
    return [_remote(ins[a].at[_shard_idx(name, c)], lands[a].at[_slots_idx(name, 2 * x + y, c)], send_sems.at[3 * a + k],
                    recv_sems.at[3 * a + k], (chip[0], chip[1], c))
            for a, name in enumerate(names) for k, chip in enumerate(chips)]


def all_gather_start(shards, names):
    n = len(names)

    def body(*refs):
        ins, lands, send_sems, recv_sems, token = refs[:n], refs[n:2 * n], refs[2 * n], refs[2 * n + 1], refs[-1]
        for cp in _chip_copies(names, ins, lands, send_sems, recv_sems):
            cp.start()
        token[...] = jnp.zeros_like(token)

    ins = [_in_hbm(shards[name]) for name in names]
    lands = [_in_hbm(lax.empty(_slots_shape(name), MM)) for name in names]
    sems = (pltpu.SemaphoreType.DMA((3 * n,)), pltpu.SemaphoreType.DMA((3 * n,)))
    outs = pl.pallas_call(
        body, name="all_gather_start", in_specs=[HBM] * (2 * n),
        out_specs=(SEM, SEM) + (HBM,) * (2 * n) + (pl.BlockSpec(memory_space=pltpu.VMEM),),
        out_shape=sems + tuple(pltpu.HBM(a.shape, a.dtype) for a in ins + lands) + (jax.ShapeDtypeStruct((8, HEAD_DIM), F32),),
        input_output_aliases={a: 2 + a for a in range(2 * n)}, compiler_params=EFFECT,
    )(*ins, *lands)
    return outs[:-1], outs[-1]


def all_gather_wait(state, names, after):
    n = len(names)

    def body(*refs):
        send_sems, recv_sems, ins, lands = refs[0], refs[1], refs[2:2 + n], refs[2 + n:2 + 2 * n]
        for cp in _chip_copies(names, ins, lands, send_sems, recv_sems):
            cp.wait_send()
            cp.wait_recv()

    outs = pl.pallas_call(
        body, name="all_gather_wait", in_specs=[SEM, SEM] + [HBM] * (2 * n) + [ANY], out_specs=(HBM,) * (2 * n),
        out_shape=tuple(pltpu.HBM(a.shape, a.dtype) for a in state[2:]),
        input_output_aliases={2 + a: a for a in range(2 * n)}, compiler_params=EFFECT,
    )(*state, after)
    return dict(zip(names, outs[:n])), dict(zip(names, outs[n:]))


def all_gather_pass_on(lands, names):
    n = len(names)

    def body(*refs):
        outs, send_sems, recv_sems = refs[n:2 * n], refs[2 * n], refs[2 * n + 1]
        x, y, c, chips = _place()
        work = [(3 * a + k, a, name, 2 * chip[0] + chip[1]) for a, name in enumerate(names) for k, chip in enumerate(chips)]
        cps = []
        for s, a, name, slot in work:
            landed = outs[a].at[_slots_idx(name, slot, c)]
            cps.append(_remote(landed, landed, send_sems.at[s], recv_sems.at[s], (x, y, 1 - c)))
            cps[-1].start()
        for s, a, name, slot in work:
            passed = outs[a].at[_slots_idx(name, slot, 1 - c)]
            _remote(passed, passed, send_sems.at[s], recv_sems.at[s], (x, y, 1 - c)).wait_recv()
        for cp in cps:
            cp.wait_send()

    outs = pl.pallas_call(
        body, name="all_gather_pass_on", in_specs=[ANY] * n, out_specs=[ANY] * n,
        input_output_aliases={a: a for a in range(n)},
        out_shape=[jax.ShapeDtypeStruct(_slots_shape(name), MM) for name in names],
        scratch_shapes=[pltpu.SemaphoreType.DMA((3 * n,)), pltpu.SemaphoreType.DMA((3 * n,))],
    )(*[lands[name] for name in names])
    return dict(zip(names, outs))


def all_gather_big(shards, names):
    n = len(names)
    BIG_NAMES = names

    def body(*refs):
        ins, outs = refs[:n], refs[n:2 * n]
        send_sems, recv_sems, fsend_sems, frecv_sems = refs[2 * n:]
        x, y, c, chips = _place()
        me_chip, sibling = 2 * x + y, (x, y, 1 - c)
        work = [(3 * a + k, a, name, chip) for a, name in enumerate(BIG_NAMES) for k, chip in enumerate(chips)]
        sends = []
        for s, a, name, chip in work:
            cp = _remote(ins[a].at[_shard_idx(name, c)], outs[a].at[_slots_idx(name, me_chip, c)], send_sems.at[s],
                         recv_sems.at[s], (chip[0], chip[1], c))
            cp.start()
            sends.append(cp)
        for s, a, name, chip in work:
            landed = outs[a].at[_slots_idx(name, 2 * chip[0] + chip[1], c)]
            _remote(landed, landed, send_sems.at[s], recv_sems.at[s], (chip[0], chip[1], c)).wait_recv()
            cp = _remote(landed, landed, fsend_sems.at[s], frecv_sems.at[s], sibling)
            cp.start()
            sends.append(cp)
        for s, a, name, chip in work:
            passed = outs[a].at[_slots_idx(name, 2 * chip[0] + chip[1], 1 - c)]
            _remote(passed, passed, fsend_sems.at[s], frecv_sems.at[s], sibling).wait_recv()
        for cp in sends:
            cp.wait_send()

    outs = pl.pallas_call(
        body, name="all_gather_big", in_specs=[ANY] * n, out_specs=[ANY] * n,
        out_shape=[jax.ShapeDtypeStruct(_slots_shape(name), MM) for name in BIG_NAMES],
        scratch_shapes=[pltpu.SemaphoreType.DMA((3 * n,))] * 4,
    )(*[shards[name] for name in BIG_NAMES])
    return dict(zip(BIG_NAMES, outs))


def with_own_slot(name, full, shard, chip):
    return lax.dynamic_update_slice(full, shard[:, None], (0, chip, 0, 0))


def rs_pair_exchange_big(grads):
    n = len(BIG_NAMES)

    def body(*refs):
        ins, outs, send_sems, recv_sems = refs[:n], refs[n:2 * n], refs[2 * n], refs[2 * n + 1]
        x, y, c, _ = _place()
        cps = []
        for a, name in enumerate(BIG_NAMES):
            cp = _remote(ins[a].at[_full_idx(name, None, 1 - c)], outs[a], send_sems.at[a], recv_sems.at[a], (x, y, 1 - c))
            cp.start()
            cps.append(cp)
        for cp in cps:
            cp.wait()

    outs = pl.pallas_call(
        body, name="rs_pair_exchange_big", in_specs=[ANY] * n, out_specs=[ANY] * n,
        out_shape=[jax.ShapeDtypeStruct(_full_shape(name, half=True), F32) for name in BIG_NAMES],
        scratch_shapes=[pltpu.SemaphoreType.DMA((n,)), pltpu.SemaphoreType.DMA((n,))],
    )(*[grads[name] for name in BIG_NAMES])
    return dict(zip(BIG_NAMES, outs))


def rs_pair_add_big(name, place, g, got):
    kind, a_, b_, c_ = BIG_SPECS[name]
    rb = _row_block(name)
    nb = (b_ // 2) // rb

    def body(place_ref, g_ref, got_ref, o_ref):
        o_ref[...] = (g_ref[...] + got_ref[...]).astype(o_ref.dtype)

    if kind == "rows":
        g_spec = pl.BlockSpec((None, None, rb, c_), lambda a, j, i, p: (a, j, p[0] * nb + i, 0))
        o_spec = pl.BlockSpec((None, None, rb, c_), lambda a, j, i, p: (a, j, i, 0))
    else:
        g_spec = pl.BlockSpec((None, rb, c_), lambda a, j, i, p: (a, p[0] * nb + i, j))
        o_spec = pl.BlockSpec((None, rb, c_), lambda a, j, i, p: (a, i, j))
    return pl.pallas_call(
        body, name="rs_pair_add_" + name,
        grid_spec=pltpu.PrefetchScalarGridSpec(num_scalar_prefetch=1, grid=(a_, N_CHIP, nb), in_specs=[g_spec, o_spec],
                                               out_specs=o_spec),
        out_shape=jax.ShapeDtypeStruct(_full_shape(name, half=True), MM),
        compiler_params=_cparams(("parallel", "parallel", "parallel")),
    )(place, g, got)


def rs_chip_exchange_big(pairs):
    n = len(BIG_NAMES)

    def body(*refs):
        ins, outs, send_sems, recv_sems = refs[:n], refs[n:2 * n], refs[2 * n], refs[2 * n + 1]
        x, y, c, chips = _place()
        cps = []
        for a, name in enumerate(BIG_NAMES):
            for k, chip in enumerate(chips):
                s = 3 * a + k
                cp = _remote(ins[a].at[_full_idx(name, 2 * chip[0] + chip[1])], outs[a].at[k], send_sems.at[s],
                             recv_sems.at[s], (chip[0], chip[1], c))
                cp.start()
                cps.append(cp)
        for cp in cps:
            cp.wait()

    def got_shape(name):
        _, a_, b_, c_ = BIG_SPECS[name]
        return jax.ShapeDtypeStruct((3, a_, b_ // 2, c_), MM)

    outs = pl.pallas_call(
        body, name="rs_chip_exchange_big", in_specs=[ANY] * n, out_specs=[ANY] * n,
        out_shape=[got_shape(name) for name in BIG_NAMES],
        scratch_shapes=[pltpu.SemaphoreType.DMA((3 * n,)), pltpu.SemaphoreType.DMA((3 * n,))],
    )(*[pairs[name] for name in BIG_NAMES])
    return dict(zip(BIG_NAMES, outs))


def rs_chip_add_big(name, place, g, got_pair, got_chips):
    kind, a_, b_, c_ = BIG_SPECS[name]
    rb = _row_block(name)
    nb = (b_ // 2) // rb

    def body(place_ref, g_ref, s_ref, r0_ref, r1_ref, r2_ref, o_ref):
        own = g_ref[...] + s_ref[...]
        o_ref[...] = ((own + r0_ref[...].astype(F32)) + r1_ref[...].astype(F32)) + r2_ref[...].astype(F32)

    if kind == "rows":
        g_spec = pl.BlockSpec((None, None, rb, c_), lambda a, i, p: (a, p[1], p[0] * nb + i, 0))
        s_spec = pl.BlockSpec((None, None, rb, c_), lambda a, i, p: (a, p[1], i, 0))
    else:
        g_spec = pl.BlockSpec((None, rb, c_), lambda a, i, p: (a, p[0] * nb + i, p[1]))
        s_spec = pl.BlockSpec((None, rb, c_), lambda a, i, p: (a, i, p[1]))
    r_spec = lambda k: pl.BlockSpec((None, None, rb, c_), lambda a, i, p: (k, a, i, 0))
    return pl.pallas_call(
        body, name="rs_chip_add_" + name,
        grid_spec=pltpu.PrefetchScalarGridSpec(
            num_scalar_prefetch=1, grid=(a_, nb), in_specs=[g_spec, s_spec, r_spec(0), r_spec(1), r_spec(2)],
            out_specs=pl.BlockSpec((None, rb, c_), lambda a, i, p: (a, p[0] * nb + i, 0))),
        out_shape=jax.ShapeDtypeStruct((a_, b_, c_), F32), compiler_params=_cparams(("parallel", "parallel")),
    )(place, g, got_pair, got_chips, got_chips, got_chips)


def rs_pair_gather_big(halves):
    n = len(BIG_NAMES)

    def body(*refs):
        outs, send_sems, recv_sems = refs[n:2 * n], refs[2 * n], refs[2 * n + 1]
        x, y, c, _ = _place()
        cps = []
        for a, name in enumerate(BIG_NAMES):
            mine = outs[a].at[_shard_idx(name, c)]
            cp = _remote(mine, mine, send_sems.at[a], recv_sems.at[a], (x, y, 1 - c))
            cp.start()
            cps.append(cp)
        for a, name in enumerate(BIG_NAMES):
            cps[a].wait_send()
            theirs = outs[a].at[_shard_idx(name, 1 - c)]
            _remote(theirs, theirs, send_sems.at[a], recv_sems.at[a], (x, y, 1 - c)).wait_recv()

    outs = pl.pallas_call(
        body, name="rs_pair_gather_big", in_specs=[ANY] * n, out_specs=[ANY] * n,
        input_output_aliases={a: a for a in range(n)},
        out_shape=[jax.ShapeDtypeStruct(BIG_SPECS[name][1:], F32) for name in BIG_NAMES],
        scratch_shapes=[pltpu.SemaphoreType.DMA((n,)), pltpu.SemaphoreType.DMA((n,))],
    )(*[halves[name] for name in BIG_NAMES])
    return dict(zip(BIG_NAMES, outs))


def reduce_scatter_big(grads, place):
    got_pair = rs_pair_exchange_big(grads)
    pairs = {name: rs_pair_add_big(name, place, grads[name], got_pair[name]) for name in BIG_NAMES}
    got_chips = rs_chip_exchange_big(pairs)
    return rs_pair_gather_big({name: rs_chip_add_big(name, place, grads[name], got_pair[name], got_chips[name])
                               for name in BIG_NAMES})


PACK_W = 1024
SMALL =(("mem_norm_w", 1024), ("mem_k_norm_w", 128), ("norm1_w", 2048), ("dn_a_log", 8), ("dn_dt_bias", 8),
         ("dn_o_norm_w", 128), ("fox_f_bias", 8), ("fox_q_norm_w", 128), ("fox_k_norm_w", 128), ("memq_norm_w", 256),
         ("norm2_w", 2048))
SMALL_ROWS = 8
CONV_ROWS = 4 * 3 * D_MODEL // PACK_W
LOSS_AT = sum(n for _, n in SMALL)


def pack_small(parts, extra=None):
    flat = [parts[name].astype(F32).reshape(-1) for name, _ in SMALL]
    used = LOSS_AT
    if extra is not None:
        flat.append(extra.reshape(1))
        used += 1
    flat.append(jnp.zeros((SMALL_ROWS * PACK_W - used,), F32))
    return jnp.concatenate(flat).reshape(SMALL_ROWS, PACK_W)


def unpack_small(packed, shapes):
    flat, out, at = packed.reshape(-1), {}, 0
    for name, n in SMALL:
        out[name] = flat[at:at + n].reshape(shapes[name])
        at += n
    return out


def _adam_all(w, g, m, v, name):
    shape = w.shape
    r2 = lambda a: a.reshape(-1, shape[-1])
    d, nm, nv = adamw(r2(w), r2(g), r2(m), r2(v), name=name)
    return d.reshape(shape), nm.reshape(shape), nv.reshape(shape)


BIG = ("w_mem_kv", "dn_w_in", "dn_conv_w", "fox_w_in", "w_out", "w_mlp1", "w_mlp2")
WEIGHTS = ("mem_norm_w", "w_mem_kv", "mem_k_norm_w", "norm1_w", "dn_w_in", "dn_conv_w", "dn_a_log", "dn_dt_bias",
           "dn_o_norm_w", "fox_w_in", "fox_f_bias", "fox_q_norm_w", "fox_k_norm_w", "memq_norm_w", "w_out", "norm2_w",
           "w_mlp1", "w_mlp2")


def kernel(x, mem, mem_norm_w, w_mem_kv, mem_k_norm_w, norm1_w, dn_w_in, dn_conv_w, dn_a_log, dn_dt_bias, dn_o_norm_w, fox_w_in, fox_f_bias, fox_q_norm_w, fox_k_norm_w, memq_norm_w, w_out, norm2_w, w_mlp1, w_mlp2, loss_target, m_mem_norm_w, m_w_mem_kv, m_mem_k_norm_w, m_norm1_w, m_dn_w_in, m_dn_conv_w, m_dn_a_log, m_dn_dt_bias, m_dn_o_norm_w, m_fox_w_in, m_fox_f_bias, m_fox_q_norm_w, m_fox_k_norm_w, m_memq_norm_w, m_w_out, m_norm2_w, m_w_mlp1, m_w_mlp2, v_mem_norm_w, v_w_mem_kv, v_mem_k_norm_w, v_norm1_w, v_dn_w_in, v_dn_conv_w, v_dn_a_log, v_dn_dt_bias, v_dn_o_norm_w, v_fox_w_in, v_fox_f_bias, v_fox_q_norm_w, v_fox_k_norm_w, v_memq_norm_w, v_w_out, v_norm2_w, v_w_mlp1, v_w_mlp2):
    args = dict(locals())
    w = {n: args[n] for n in WEIGHTS}
    m = {n: args["m_" + n] for n in WEIGHTS}
    v = {n: args["v_" + n] for n in WEIGHTS}
    core, chip = lax.axis_index("c"), 2 * lax.axis_index("x") + lax.axis_index("y")
    place = jnp.stack([core, chip]).astype(jnp.int32)

    shards = {name: w[name].reshape(BIG_SPECS[name][1:]).astype(MM) for name in BIG_NAMES}
    w_in_full = lambda arr, n_scalars: w_in_to_kernel(arr[0].transpose(1, 0, 2).reshape(D_MODEL, -1), n_scalars)
    early = {name: with_own_slot(name, arr, shards[name], chip)
             for name, arr in all_gather_big(shards, EARLY_NAMES).items()}
    late_shards, early = lax.optimization_barrier(({name: shards[name] for name in LATE_NAMES}, early))
    late_state, token = all_gather_start(late_shards, LATE_NAMES)
    tie = token[0, 0]
    conv_mine = jnp.where(core == 0, dn_conv_w[0], 0.0) + tie
    conv_placed = lax.dynamic_update_slice(jnp.zeros((4, 3 * D_MODEL), F32), conv_mine, (0, 768 * chip))
    conv_full = all_reduce_small(jnp.pad(conv_placed.reshape(CONV_ROWS, PACK_W), ((0, 16 - CONV_ROWS), (0, 0))))
    wt = dict(w_mem_kv=early["w_mem_kv"].reshape(D_MODEL, 2 * MEM_WIDTH) + tie.astype(MM),
              dn_w_in=w_in_full(early["dn_w_in"], 2 * N_HEADS), conv_w=conv_full[:CONV_ROWS].reshape(4, 3 * D_MODEL))

    def late(after):
        late_shards, lands = all_gather_wait(late_state, LATE_NAMES, after)
        full = {name: with_own_slot(name, arr, late_shards[name], chip)
                for name, arr in all_gather_pass_on(lands, LATE_NAMES).items()}
        return dict(fox_w_in=w_in_full(full["fox_w_in"], N_HEADS), w_out=full["w_out"].reshape(2, 3 * MEM_WIDTH, D_MODEL),
                    w_mlp1=full["w_mlp1"], w_mlp2=full["w_mlp2"].reshape(2, D_FF, D_MODEL))

    sm = dict(mem_norm_w=mem_norm_w, mem_k_norm_w=mem_k_norm_w, norm1_w=norm1_w, norm2_w=norm2_w, memq_norm_w=memq_norm_w,
              dn_a_log=dn_a_log[0], dn_dt_bias=dn_dt_bias[0], dn_o_norm_w=dn_o_norm_w, fox_f_bias=fox_f_bias[0],
              fox_q_norm_w=fox_q_norm_w, fox_k_norm_w=fox_k_norm_w)
    loss_part, dx, big, small = local_step(x[0], mem[0], loss_target[0], wt, sm, late)

    w_in_slots = lambda g, n_scalars: w_in_from_kernel(g, n_scalars).reshape(D_MODEL, N_CHIP, -1).transpose(1, 0, 2)[None]
    full_grads = dict(w_mem_kv=big["w_mem_kv"].reshape(_full_shape("w_mem_kv")),
                      w_out=jnp.stack(big["w_out"]).reshape(_full_shape("w_out")),
                      w_mlp2=jnp.stack(big["w_mlp2"]).reshape(_full_shape("w_mlp2")), w_mlp1=jnp.stack(big["w_mlp1"]),
                      dn_w_in=w_in_slots(big["dn_w_in"], 2 * N_HEADS), fox_w_in=w_in_slots(big["fox_w_in"], N_HEADS))
    big_sum = reduce_scatter_big(full_grads, place)
    small_pack = jnp.concatenate([pack_small(small, loss_part[0, :1]), big["conv_w"].reshape(CONV_ROWS, PACK_W),
                                  jnp.zeros((24 - SMALL_ROWS - CONV_ROWS, PACK_W), F32)])
    small_all = all_reduce_small(small_pack)
    small_sum = small_all[:SMALL_ROWS]
    conv_sum = lax.dynamic_slice(small_all[SMALL_ROWS:SMALL_ROWS + CONV_ROWS].reshape(4, 3 * D_MODEL), (0, 768 * chip), (4, 768))
    loss = small_sum.reshape(-1)[LOSS_AT]
    grads = unpack_small(small_sum, {n: w[n].shape for n, _ in SMALL})
    grads.update({name: big_sum[name].reshape(w[name].shape) for name in BIG_NAMES}, dn_conv_w=conv_sum[None])

    delta, new_m, new_v = {}, {}, {}
    for n in BIG:
        delta[n], new_m[n], new_v[n] = _adam_all(w[n], grads[n], m[n], v[n], "adamw_" + n)
    shapes = {n: w[n].shape for n, _ in SMALL}
    d_s, m_s, v_s = adamw(pack_small(w), small_sum, pack_small(m), pack_small(v), name="adamw_small")
    for out, packed in ((delta, d_s), (new_m, m_s), (new_v, v_s)):
        out.update(unpack_small(packed, shapes))
    return (loss, dx[None], *[grads[n] for n in WEIGHTS], *[delta[n] for n in WEIGHTS],
            *[new_m[n] for n in WEIGHTS], *[new_v[n] for n in WEIGHTS])
```

```python
import functools

import jax
import jax.numpy as jnp
from jax import lax
from jax.experimental import pallas as pl
from jax.experimental.pallas import tpu as pltpu

F32 = jnp.float32
MM = jnp.bfloat16
HI = lax.Precision.HIGHEST

D_MODEL = 1024
HEAD_DIM = 128
N_HEADS = 8
MEM_HEADS = 4
MEM_WIDTH = MEM_HEADS * HEAD_DIM
N_MEM = 256
D_FF = 4 * D_MODEL
CHUNK = 64
EPS = 1e-6
QSCALE = HEAD_DIM ** -0.5
PROJ_W = 4736
TAIL = 4608
TAIL_BLK = TAIL // HEAD_DIM
ROWS = 256
VMEM_LIMIT = 56 * 1024 * 1024

ADAM_LR = 0.001
ADAM_B1 = 0.9
ADAM_B2 = 0.999
ADAM_EPS = 1e-08
ADAM_WD = 0.01
ADAM_STEP = 10

N_DEV = 8
N_CHIP = 4
MESH = pl.DeviceIdType.MESH


def _cparams(sem=None):
    return pltpu.CompilerParams(dimension_semantics=sem, vmem_limit_bytes=VMEM_LIMIT)


def _dot(a, b, ca, cb, hi):
    dims = (((ca,), (cb,)), ((), ()))
    if hi:
        return lax.dot_general(a, b, dims, precision=HI, preferred_element_type=F32)
    return lax.dot_general(a.astype(MM), b.astype(MM), dims, preferred_element_type=F32)


@functools.partial(jax.custom_vjp, nondiff_argnums=(2, 3, 4))
def mmul(a, b, ca, cb, hi):
    return _dot(a, b, ca, cb, hi)


def _mmul_fwd(a, b, ca, cb, hi):
    return _dot(a, b, ca, cb, hi), (a, b)


def _mmul_bwd(ca, cb, hi, res, g):
    a, b = res
    if ca == 1:
        da = _dot(g, b, 1, 1, hi) if cb == 0 else _dot(g, b, 1, 0, hi)
    else:
        da = _dot(b, g, 1, 1, hi) if cb == 0 else _dot(b, g, 0, 1, hi)
    if cb == 0:
        db = _dot(a, g, 0, 0, hi) if ca == 1 else _dot(a, g, 1, 0, hi)
    else:
        db = _dot(g, a, 0, 0, hi) if ca == 1 else _dot(g, a, 0, 1, hi)
    return da.astype(a.dtype), db.astype(b.dtype)


mmul.defvjp(_mmul_fwd, _mmul_bwd)


def _iota2(n, m):
    return lax.broadcasted_iota(jnp.int32, (n, m), 0), lax.broadcasted_iota(jnp.int32, (n, m), 1)


def _same_block(r, c, shift):
    return lax.shift_right_logical(r, shift) == lax.shift_right_logical(c, shift)


def _split_bf16(x):
    hi = x.astype(jnp.bfloat16)
    return hi, (x - hi.astype(F32)).astype(jnp.bfloat16)


def _dot3(a, b, ca, cb):
    dims = (((ca,), (cb,)), ((), ()))
    (ah, al), (bh, bl) = _split_bf16(a), _split_bf16(b)
    d = lambda x, y: lax.dot_general(x, y, dims, preferred_element_type=F32)
    return d(ah, bh) + (d(ah, bl) + d(al, bh))


def _tri_inv_impl(a):
    n = a.shape[0]
    r, c = _iota2(n, n)
    eye = (r == c).astype(F32)
    b16, b32 = _same_block(r, c, 4), _same_block(r, c, 5)
    a0 = jnp.where(b16, a, 0.0)
    p = eye - a0
    b = _dot3(a0, a0, 1, 0)
    p = p + _dot3(p, b, 1, 0)
    b = _dot3(b, b, 1, 0)
    p = p + _dot3(p, b, 1, 0)
    b = _dot3(b, b, 1, 0)
    p = p + _dot3(p, b, 1, 0)
    a1 = jnp.where(jnp.logical_and(b32, jnp.logical_not(b16)), a, 0.0)
    p = p - _dot3(_dot3(p, a1, 1, 0), p, 1, 0)
    a2 = jnp.where(b32, 0.0, a)
    p = p - _dot3(_dot3(p, a2, 1, 0), p, 1, 0)
    return p


@jax.custom_vjp
def tri_inv(a):
    return _tri_inv_impl(a)


def _tri_inv_fwd(a):
    p = _tri_inv_impl(a)
    return p, p


def _tri_inv_bwd(p, g):
    return (-_dot3(_dot3(p, g, 0, 0), p, 1, 1),)


tri_inv.defvjp(_tri_inv_fwd, _tri_inv_bwd)


def _sigmoid(x):
    return 1.0 / (1.0 + jnp.exp(-x))


def _softplus(x):
    return jnp.maximum(x, 0.0) + jnp.log(1.0 + jnp.exp(-jnp.abs(x)))


def _silu(x):
    return x * _sigmoid(x)


def _rms(x, w):
    return x * lax.rsqrt(jnp.mean(x * x, axis=-1, keepdims=True) + EPS) * w


def _bf_round(x):
    return x.astype(MM).astype(F32)


def _acc(ref, val, first):
    @pl.when(first)
    def _():
        ref[...] = val

    @pl.when(jnp.logical_not(first))
    def _():
        ref[...] += val


def _tile(n, pref):
    if n % pref == 0:
        return pref
    return n


def matmul(a, b, *, ta=False, tb=False, b_slots=False, res=None, also_sqrelu=False, times_dsqrelu=None, out_dtype=F32,
           name, tm=1024, tn=1024, tk=1024):
    m, k = (a.shape[1], a.shape[0]) if ta else a.shape
    if b_slots:
        n = b.shape[1] if tb else N_CHIP * b.shape[2]
        assert (N_CHIP * b.shape[2] if tb else b.shape[1]) == k, (a.shape, b.shape, ta, tb)
        tn, tk = (tn, b.shape[2]) if tb else (b.shape[2], tk)
    else:
        n = b.shape[0] if tb else b.shape[1]
        assert (b.shape[1] if tb else b.shape[0]) == k, (a.shape, b.shape, ta, tb)
    tm, tn, tk = _tile(m, tm), _tile(n, tn), _tile(k, tk)
    nk = k // tk
    ca, cb = (0 if ta else 1), (1 if tb else 0)

    extra = tuple(e for e in (res, times_dsqrelu) if e is not None)
    assert len(extra) <= 1

    def body(a_ref, b_ref, *rest):
        e_ref = rest[0] if extra else None
        o_ref, acc_ref = rest[len(extra)], rest[-1]
        kk = pl.program_id(2)
        part = _dot(a_ref[...], b_ref[...], ca, cb, False)

        @pl.when(kk == 0)
        def _():
            acc_ref[...] = part

        @pl.when(kk > 0)
        def _():
            acc_ref[...] += part

        @pl.when(kk == nk - 1)
        def _():
            total = acc_ref[...]
            if res is not None:
                total = total + e_ref[...]
            if times_dsqrelu is not None:
                total = total * (2.0 * jnp.maximum(e_ref[...], 0.0))
            o_ref[...] = total.astype(o_ref.dtype)
            if also_sqrelu:
                rest[len(extra) + 1][...] = _sqrelu(total).astype(MM)

    a_spec = pl.BlockSpec((tk, tm), lambda i, j, l: (l, i)) if ta else pl.BlockSpec((tm, tk), lambda i, j, l: (i, l))
    if b_slots:
        b_spec = (pl.BlockSpec((None, tn, tk), lambda i, j, l: (l, j, 0)) if tb else
                  pl.BlockSpec((None, tk, tn), lambda i, j, l: (j, l, 0)))
    else:
        b_spec = pl.BlockSpec((tn, tk), lambda i, j, l: (j, l)) if tb else pl.BlockSpec((tk, tn), lambda i, j, l: (l, j))
    o_spec = pl.BlockSpec((tm, tn), lambda i, j, l: (i, j))
    out_shape = [jax.ShapeDtypeStruct((m, n), out_dtype)] + [jax.ShapeDtypeStruct((m, n), MM)] * also_sqrelu
    outs = pl.pallas_call(
        body, name=name, grid=(m // tm, n // tn, nk),
        in_specs=[a_spec, b_spec] + [o_spec] * len(extra), out_specs=[o_spec] * len(out_shape), out_shape=out_shape,
        scratch_shapes=[pltpu.VMEM((tm, tn), F32)],
        compiler_params=_cparams(("parallel", "parallel", "arbitrary")),
    )(a, b, *extra)
    return outs if also_sqrelu else outs[0]


def rms_fwd(x, w, *, name):
    t, d = x.shape

    def body(x_ref, w_ref, o_ref):
        o_ref[...] = _rms(x_ref[...], w_ref[...]).astype(o_ref.dtype)

    return pl.pallas_call(
        body, name=name, grid=(t // ROWS,),
        in_specs=[pl.BlockSpec((ROWS, d), lambda i: (i, 0)), pl.BlockSpec((1, d), lambda i: (0, 0))],
        out_specs=pl.BlockSpec((ROWS, d), lambda i: (i, 0)),
        out_shape=jax.ShapeDtypeStruct((t, d), MM), compiler_params=_cparams(("parallel",)),
    )(x, w)


def rms_bwd(x, w, dh, dres, *, name):
    t, d = x.shape

    def body(x_ref, w_ref, dh_ref, dr_ref, dx_ref, dw_ref):
        _, vjp = jax.vjp(_rms, x_ref[...], w_ref[...])
        dx, dw = vjp(dh_ref[...].astype(F32))
        dx_ref[...] = dx + dr_ref[...]
        _acc(dw_ref, dw, pl.program_id(0) == 0)

    row = pl.BlockSpec((ROWS, d), lambda i: (i, 0))
    vec = pl.BlockSpec((1, d), lambda i: (0, 0))
    return pl.pallas_call(
        body, name=name, grid=(t // ROWS,), in_specs=[row, vec, row, row], out_specs=[row, vec],
        out_shape=[jax.ShapeDtypeStruct((t, d), F32), jax.ShapeDtypeStruct((1, d), F32)],
        compiler_params=_cparams(("arbitrary",)),
    )(x, w, dh, dres)


def _sqrelu(x):
    return jnp.square(jnp.maximum(x, 0.0))


def loss_fwd(y, target, *, name):
    t, d = y.shape

    def body(y_ref, t_ref, dy_ref, l_ref):
        e = y_ref[...] - t_ref[...]
        dy_ref[...] = e * (1.0 / d)
        part = 0.5 * jnp.sum(jnp.sum(e * e, axis=-1, keepdims=True) * (1.0 / d), axis=0, keepdims=True)
        _acc(l_ref, jnp.broadcast_to(part, (1, HEAD_DIM)), pl.program_id(0) == 0)

    blk = pl.BlockSpec((ROWS, d), lambda i: (i, 0))
    return pl.pallas_call(
        body, name=name, grid=(t // ROWS,), in_specs=[blk, blk],
        out_specs=[blk, pl.BlockSpec((1, HEAD_DIM), lambda i: (0, 0))],
        out_shape=[jax.ShapeDtypeStruct((t, d), F32), jax.ShapeDtypeStruct((1, HEAD_DIM), F32)],
        compiler_params=_cparams(("arbitrary",)),
    )(y, target)


def _mem_kv(mem, wn, wkn, *ws):
    mn = _rms(mem, wn)
    outs = []
    for h in range(MEM_HEADS):
        outs.append(_rms(mmul(mn, ws[h], 1, 0, False), wkn))
    for h in range(MEM_HEADS):
        outs.append(mmul(mn, ws[MEM_HEADS + h], 1, 0, False))
    return tuple(outs)


def _w_cols(w_ref):
    return [w_ref[:, h * HEAD_DIM:(h + 1) * HEAD_DIM] for h in range(2 * MEM_HEADS)]


def mem_fwd(mem, wn, wkv, wkn):
    def body(mem_ref, wn_ref, w_ref, wkn_ref, k_ref, v_ref):
        outs = _mem_kv(mem_ref[...], wn_ref[...], wkn_ref[...], *_w_cols(w_ref))
        for h in range(MEM_HEADS):
            k_ref[:, h * HEAD_DIM:(h + 1) * HEAD_DIM] = outs[h]
            v_ref[:, h * HEAD_DIM:(h + 1) * HEAD_DIM] = outs[MEM_HEADS + h]

    shp = jax.ShapeDtypeStruct((mem.shape[0], MEM_WIDTH), F32)
    return pl.pallas_call(body, name="mem_fwd", out_shape=[shp, shp], compiler_params=_cparams())(mem, wn, wkv, wkn)


def mem_bwd(mem, wn, wkv, wkn, dk0, dv0, dk1, dv1):
    def body(mem_ref, wn_ref, w_ref, wkn_ref, dk0_ref, dv0_ref, dk1_ref, dv1_ref, dwn_ref, dw_ref, dwkn_ref):
        _, vjp = jax.vjp(lambda wn_, wkn_, *ws: _mem_kv(mem_ref[...], wn_, wkn_, *ws),
                         wn_ref[...], wkn_ref[...], *[w.astype(F32) for w in _w_cols(w_ref)])
        cols = lambda a, b: tuple(a[:, h * HEAD_DIM:(h + 1) * HEAD_DIM] + b[:, h * HEAD_DIM:(h + 1) * HEAD_DIM]
                                  for h in range(MEM_HEADS))
        cts = cols(dk0_ref, dk1_ref) + cols(dv0_ref, dv1_ref)
        grads = vjp(cts)
        dwn_ref[...] = grads[0]
        dwkn_ref[...] = grads[1]
        for h in range(2 * MEM_HEADS):
            dw_ref[:, h * HEAD_DIM:(h + 1) * HEAD_DIM] = grads[2 + h]

    return pl.pallas_call(
        body, name="mem_bwd",
        out_shape=[jax.ShapeDtypeStruct((1, D_MODEL), F32), jax.ShapeDtypeStruct((D_MODEL, 2 * MEM_WIDTH), F32),
                   jax.ShapeDtypeStruct((1, HEAD_DIM), F32)],
        compiler_params=_cparams(),
    )(mem, wn, wkv, wkn, dk0, dv0, dk1, dv1)


def _memattn(q, wq, mk, mv):
    qn = _rms(q, wq) * QSCALE
    s = mmul(qn, mk, 1, 1, False)
    s = s - jnp.max(s, axis=-1, keepdims=True)
    p = jnp.exp(s)
    p = p / jnp.sum(p, axis=-1, keepdims=True)
    return mmul(p, mv, 1, 0, False)


def _lanes(j):
    return slice(j * HEAD_DIM, (j + 1) * HEAD_DIM)


def _memattn_specs(t):
    qspec = pl.BlockSpec((ROWS, MEM_WIDTH), lambda i: (i, (TAIL - MEM_WIDTH) // MEM_WIDTH))
    wspec = pl.BlockSpec((1, HEAD_DIM), lambda i: (0, 0))
    mspec = pl.BlockSpec((N_MEM, MEM_WIDTH), lambda i: (0, 0))
    ospec = pl.BlockSpec((ROWS, MEM_WIDTH), lambda i: (i, 0))
    return qspec, wspec, mspec, ospec


def memattn_fwd(proj, wq, mk, mv, *, name):
    t = proj.shape[0]
    qspec, wspec, mspec, ospec = _memattn_specs(t)

    def body(q_ref, w_ref, k_ref, v_ref, o_ref):
        for h in range(MEM_HEADS):
            o_ref[:, _lanes(h)] = _memattn(q_ref[:, _lanes(h)], w_ref[...], k_ref[:, _lanes(h)],
                                           v_ref[:, _lanes(h)]).astype(o_ref.dtype)

    return pl.pallas_call(
        body, name=name, grid=(t // ROWS,), in_specs=[qspec, wspec, mspec, mspec], out_specs=ospec,
        out_shape=jax.ShapeDtypeStruct((t, MEM_WIDTH), MM), compiler_params=_cparams(("parallel",)),
    )(proj, wq, mk, mv)


def memattn_bwd(proj, wq, mk, mv, dcat, *, name):
    t = proj.shape[0]
    qspec, wspec, mspec, ospec = _memattn_specs(t)
    dospec = pl.BlockSpec((ROWS, MEM_WIDTH), lambda i: (i, D_MODEL // MEM_WIDTH))

    def body(q_ref, w_ref, k_ref, v_ref, do_ref, dq_ref, dw_ref, dk_ref, dv_ref):
        first = pl.program_id(0) == 0
        dw_sum = jnp.zeros((1, HEAD_DIM), F32)
        for h in range(MEM_HEADS):
            _, vjp = jax.vjp(_memattn, q_ref[:, _lanes(h)], w_ref[...], k_ref[:, _lanes(h)], v_ref[:, _lanes(h)])
            dq, dw, dk, dv = vjp(do_ref[:, _lanes(h)].astype(F32))
            dq_ref[:, _lanes(h)] = dq.astype(dq_ref.dtype)
            dw_sum = dw_sum + dw
            _acc(dk_ref.at[:, _lanes(h)], dk, first)
            _acc(dv_ref.at[:, _lanes(h)], dv, first)
        _acc(dw_ref, dw_sum, first)

    mshape = jax.ShapeDtypeStruct((N_MEM, MEM_WIDTH), F32)
    return pl.pallas_call(
        body, name=name, grid=(t // ROWS,), in_specs=[qspec, wspec, mspec, mspec, dospec],
        out_specs=[ospec, wspec, mspec, mspec],
        out_shape=[jax.ShapeDtypeStruct((t, MEM_WIDTH), MM), jax.ShapeDtypeStruct((1, HEAD_DIM), F32), mshape, mshape],
        compiler_params=_cparams(("arbitrary",)),
    )(proj, wq, mk, mv, dcat)


def _shift_rows(x, s, up):
    n = x.shape[0]
    r = lax.broadcasted_iota(jnp.int32, x.shape, 0)
    if up:
        return jnp.where(r < n - s, pltpu.roll(x, n - s, 0), 0.0)
    return jnp.where(r >= s, pltpu.roll(x, s, 0), 0.0)


def _conv_fwd_vals(x, w):
    xb = _bf_round(x)
    wb = _bf_round(w)
    c = xb * wb[3:4, :]
    for j in range(3):
        c = c + _shift_rows(xb, 3 - j, False) * wb[j:j + 1, :]
    return xb, wb, c


def dn_prep_fwd(proj, conv_w):
    t = proj.shape[0]

    def body(x_ref, w_ref, o_ref):
        j = pl.program_id(0)
        _, _, c = _conv_fwd_vals(x_ref[...], w_ref[...])
        s = _silu(c)
        r = lax.rsqrt(jnp.sum(s * s, axis=-1, keepdims=True) + EPS)
        scale = jnp.where(j < N_HEADS, QSCALE, 1.0)
        o_ref[...] = jnp.where(j < 2 * N_HEADS, s * r * scale, s)

    return pl.pallas_call(
        body, name="dn_prep_fwd", grid=(3 * N_HEADS,),
        in_specs=[pl.BlockSpec((t, HEAD_DIM), lambda j: (0, j)), pl.BlockSpec((4, HEAD_DIM), lambda j: (0, j))],
        out_specs=pl.BlockSpec((None, t, HEAD_DIM), lambda j: (j // N_HEADS, 0, j % N_HEADS)),
        out_shape=jax.ShapeDtypeStruct((3, t, D_MODEL), F32), compiler_params=_cparams(("parallel",)),
    )(proj, conv_w)


def dn_prep_bwd(proj, conv_w, dqkv):
    t = proj.shape[0]

    def body(x_ref, w_ref, g_ref, dx_ref, dw_ref):
        j = pl.program_id(0)
        xb, wb, c = _conv_fwd_vals(x_ref[...], w_ref[...])
        sg = _sigmoid(c)
        s = c * sg
        g = g_ref[...]
        r = lax.rsqrt(jnp.sum(s * s, axis=-1, keepdims=True) + EPS)
        scale = jnp.where(j < N_HEADS, QSCALE, 1.0)
        gn = g * scale
        ds_norm = r * gn - s * (r * r * r) * jnp.sum(gn * s, axis=-1, keepdims=True)
        ds = jnp.where(j < 2 * N_HEADS, ds_norm, g)
        dc = ds * (sg + s * (1.0 - sg))
        dx = dc * wb[3:4, :]
        rows = [jnp.sum(dc * xb, axis=0, keepdims=True)]
        for jj in range(2, -1, -1):
            sh = 3 - jj
            dx = dx + _shift_rows(dc, sh, True) * wb[jj:jj + 1, :]
            rows.insert(0, jnp.sum(dc * _shift_rows(xb, sh, False), axis=0, keepdims=True))
        dx_ref[...] = dx.astype(dx_ref.dtype)
        dw_ref[...] = jnp.concatenate(rows + [jnp.zeros((4, HEAD_DIM), F32)], axis=0)

    col = pl.BlockSpec((t, HEAD_DIM), lambda j: (0, j))
    return pl.pallas_call(
        body, name="dn_prep_bwd", grid=(3 * N_HEADS,),
        in_specs=[col, pl.BlockSpec((4, HEAD_DIM), lambda j: (0, j)),
                  pl.BlockSpec((None, t, HEAD_DIM), lambda j: (j // N_HEADS, 0, j % N_HEADS))],
        out_specs=[col, pl.BlockSpec((8, HEAD_DIM), lambda j: (0, j))],
        out_shape=[jax.ShapeDtypeStruct((t, 3 * D_MODEL), MM), jax.ShapeDtypeStruct((8, 3 * D_MODEL), F32)],
        compiler_params=_cparams(("parallel",)),
    )(proj, conv_w, dqkv)


def _tri_ones(n, upper):
    r, c = _iota2(n, n)
    return (r <= c).astype(F32) if upper else (r >= c).astype(F32)


def dn_gates_fwd(proj, a_log, dt_bias):
    t = proj.shape[0]

    def body(x_ref, al_ref, dt_ref, o_ref):
        lane = lax.broadcasted_iota(jnp.int32, (CHUNK, HEAD_DIM), 1)
        tri = _tri_ones(CHUNK, False)

        def step(c, carry):
            rows = pl.ds(pl.multiple_of(c * CHUNK, CHUNK), CHUNK)
            x = x_ref[rows, :]
            g = jnp.where(lane < N_HEADS, -jnp.exp(al_ref[...]) * _softplus(x + dt_ref[...]), 0.0)
            gc = _dot(tri, g, 1, 0, True)
            o_ref[rows, :] = jnp.where(lane < N_HEADS, gc, jnp.where(lane < 2 * N_HEADS, _sigmoid(x), 0.0))
            return carry

        lax.fori_loop(0, t // CHUNK, step, 0)

    vec = pl.BlockSpec((1, HEAD_DIM), lambda i: (0, 0))
    return pl.pallas_call(
        body, name="dn_gates_fwd", grid=(1,),
        in_specs=[pl.BlockSpec((t, HEAD_DIM), lambda i: (0, TAIL_BLK)), vec, vec],
        out_specs=pl.BlockSpec((t, HEAD_DIM), lambda i: (0, 0)),
        out_shape=jax.ShapeDtypeStruct((t, HEAD_DIM), F32), compiler_params=_cparams(("arbitrary",)),
    )(proj, a_log, dt_bias)


def dn_gates_bwd(proj, a_log, dt_bias, dgates):
    t = proj.shape[0]

    def body(x_ref, al_ref, dt_ref, g_ref, dx_ref, dal_ref, ddt_ref):
        lane = lax.broadcasted_iota(jnp.int32, (CHUNK, HEAD_DIM), 1)
        tri = _tri_ones(CHUNK, True)
        dal_ref[...] = jnp.zeros_like(dal_ref)
        ddt_ref[...] = jnp.zeros_like(ddt_ref)

        def step(c, carry):
            rows = pl.ds(pl.multiple_of(c * CHUNK, CHUNK), CHUNK)
            x = x_ref[rows, :]
            dgc = jnp.where(lane < N_HEADS, g_ref[rows, :], 0.0)
            dg = _dot(tri, dgc, 1, 0, True)
            ea = -jnp.exp(al_ref[...])
            z = x + dt_ref[...]
            da = jnp.where(lane < N_HEADS, dg * ea * _sigmoid(z), 0.0)
            gval = jnp.where(lane < N_HEADS, ea * _softplus(z), 0.0)
            beta = _sigmoid(x)
            db = jnp.where(jnp.logical_and(lane >= N_HEADS, lane < 2 * N_HEADS), g_ref[rows, :] * beta * (1.0 - beta), 0.0)
            dx_ref[rows, :] = (da + db).astype(dx_ref.dtype)
            dal_ref[...] += jnp.sum(dg * gval, axis=0, keepdims=True)
            ddt_ref[...] += jnp.sum(da, axis=0, keepdims=True)
            return carry

        lax.fori_loop(0, t // CHUNK, step, 0)

    vec = pl.BlockSpec((1, HEAD_DIM), lambda i: (0, 0))
    full = pl.BlockSpec((t, HEAD_DIM), lambda i: (0, 0))
    return pl.pallas_call(
        body, name="dn_gates_bwd", grid=(1,),
        in_specs=[pl.BlockSpec((t, HEAD_DIM), lambda i: (0, TAIL_BLK)), vec, vec, full],
        out_specs=[full, vec, vec],
        out_shape=[jax.ShapeDtypeStruct((t, HEAD_DIM), MM), jax.ShapeDtypeStruct((1, HEAD_DIM), F32),
                   jax.ShapeDtypeStruct((1, HEAD_DIM), F32)],
        compiler_params=_cparams(("arbitrary",)),
    )(proj, a_log, dt_bias, dgates)


def _dn_intra(q, k, v, gcol, grow, bcol):
    r, c = _iota2(CHUNK, CHUNK)
    causal, strict = r >= c, r > c
    decay = jnp.where(causal, jnp.exp(jnp.where(causal, gcol - grow, 0.0)), 0.0)
    kb = k * bcol
    a = jnp.where(strict, mmul(kb, k, 1, 1, False) * decay, 0.0)
    tm = tri_inv(a)
    u = mmul(tm, v * bcol, 1, 0, False)
    w = mmul(tm, kb * jnp.exp(gcol), 1, 0, False)
    qk = jnp.where(causal, mmul(q, k, 1, 1, False) * decay, 0.0)
    rr = lax.broadcasted_iota(jnp.int32, (CHUNK, 1), 0)
    g_last = jnp.sum(jnp.where(rr == CHUNK - 1, gcol, 0.0), axis=0, keepdims=True)
    return u, w, q * jnp.exp(gcol), k * jnp.exp(g_last - gcol), qk, jnp.exp(g_last)


def _dn_scan(u, w, qg, kd, qk, eg, state):
    v_new = u - mmul(w, state, 1, 0, False)
    out = mmul(qg, state, 1, 0, False) + mmul(qk, v_new, 1, 0, False)
    return out, state * eg + mmul(kd, v_new, 0, 0, False)


DN_HEADS_PER_STEP = 1
DN_GROUP = 8
DN_PARTS = ((CHUNK, HEAD_DIM),) * 4 + ((CHUNK, CHUNK), (1, 1))


def _dn_scratch(hb, nc):
    return [pltpu.VMEM((hb, nc) + shape, F32) for shape in DN_PARTS]


def _dn_group(nc):
    return min(DN_GROUP, nc)


def _dn_group_args(refs, j, g, grp):
    q_ref, k_ref, v_ref, gc_ref, gr_ref, bc_ref = refs
    rows = pl.ds(pl.multiple_of(g * (grp * CHUNK), grp * CHUNK), grp * CHUNK)
    cs = pl.ds(g * grp, grp)
    split = lambda ref: ref[rows, _lanes(j)].reshape(grp, CHUNK, HEAD_DIM)
    return split(q_ref), split(k_ref), split(v_ref), gc_ref[j, cs], gr_ref[j, cs], bc_ref[j, cs]


def _dn_intra_all(refs, parts, hb, nc):
    grp = _dn_group(nc)

    def group(g, carry):
        cs = pl.ds(g * grp, grp)
        for j in range(hb):
            for part, val in zip(parts, jax.vmap(_dn_intra)(*_dn_group_args(refs, j, g, grp))):
                part[j, cs] = val
        return carry

    lax.fori_loop(0, nc // grp, group, 0)


def _dn_specs(t):
    nc, hb = t // CHUNK, DN_HEADS_PER_STEP
    head = lambda which: pl.BlockSpec((None, t, hb * HEAD_DIM), lambda h: (which, 0, h))
    flat = pl.BlockSpec((t, hb * HEAD_DIM), lambda h: (0, h))
    col = pl.BlockSpec((hb, nc, CHUNK, 1), lambda h: (h, 0, 0, 0))
    row = pl.BlockSpec((hb, nc, 1, CHUNK), lambda h: (h, 0, 0, 0))
    st = pl.BlockSpec((hb, nc, HEAD_DIM, HEAD_DIM), lambda h: (h, 0, 0, 0))
    return nc, hb, head, flat, col, row, st


def dn_core_fwd(qkv, gcol, grow, bcol):
    t = qkv.shape[1]
    nc, hb, head, flat, col, row, st = _dn_specs(t)

    def body(q_ref, k_ref, v_ref, gc_ref, gr_ref, bc_ref, o_ref, s_ref, *parts):
        _dn_intra_all((q_ref, k_ref, v_ref, gc_ref, gr_ref, bc_ref), parts, hb, nc)

        def step(c, states):
            rows = pl.ds(pl.multiple_of(c * CHUNK, CHUNK), CHUNK)
            new_states = []
            for j in range(hb):
                s_ref[j, c] = states[j]
                out, new_state = _dn_scan(*[part[j, c] for part in parts], states[j])
                o_ref[rows, _lanes(j)] = out
                new_states.append(new_state)
            return tuple(new_states)

        lax.fori_loop(0, nc, step, tuple(jnp.zeros((HEAD_DIM, HEAD_DIM), F32) for _ in range(hb)))

    return pl.pallas_call(
        body, name="dn_core_fwd", grid=(N_HEADS // hb,),
        in_specs=[head(0), head(1), head(2), col, row, col], out_specs=[flat, st],
        out_shape=[jax.ShapeDtypeStruct((t, D_MODEL), F32), jax.ShapeDtypeStruct((N_HEADS, nc, HEAD_DIM, HEAD_DIM), F32)],
        scratch_shapes=_dn_scratch(hb, nc), compiler_params=_cparams(("parallel",)),
    )(qkv, qkv, qkv, gcol, grow, bcol)


def dn_core_bwd(qkv, gcol, grow, bcol, states, do):
    t = qkv.shape[1]
    nc, hb, head, flat, col, row, st = _dn_specs(t)

    def body(q_ref, k_ref, v_ref, gc_ref, gr_ref, bc_ref, s_ref, do_ref, dqkv_ref, dgc_ref, dgr_ref, dbc_ref, *scratch):
        parts, dparts = scratch[:len(DN_PARTS)], scratch[len(DN_PARTS):]
        refs = (q_ref, k_ref, v_ref, gc_ref, gr_ref, bc_ref)
        _dn_intra_all(refs, parts, hb, nc)

        def step(i, dstates):
            c = nc - 1 - i
            rows = pl.ds(pl.multiple_of(c * CHUNK, CHUNK), CHUNK)
            dstates_in = []
            for j in range(hb):
                _, vjp = jax.vjp(_dn_scan, *[part[j, c] for part in parts], s_ref[j, c])
                *dvals, dstate_in = vjp((do_ref[rows, _lanes(j)], dstates[j]))
                for dpart, dval in zip(dparts, dvals):
                    dpart[j, c] = dval
                dstates_in.append(dstate_in)
            return tuple(dstates_in)

        lax.fori_loop(0, nc, step, tuple(jnp.zeros((HEAD_DIM, HEAD_DIM), F32) for _ in range(hb)))

        grp = _dn_group(nc)

        def group(g, carry):
            rows = pl.ds(pl.multiple_of(g * (grp * CHUNK), grp * CHUNK), grp * CHUNK)
            cs = pl.ds(g * grp, grp)
            for j in range(hb):
                _, vjp = jax.vjp(jax.vmap(_dn_intra), *_dn_group_args(refs, j, g, grp))
                dq, dk, dv, dgc, dgr, dbc = vjp(tuple(dpart[j, cs] for dpart in dparts))
                for which, val in enumerate((dq, dk, dv)):
                    dqkv_ref[which, rows, _lanes(j)] = val.reshape(grp * CHUNK, HEAD_DIM)
                dgc_ref[j, cs] = dgc
                dgr_ref[j, cs] = dgr
                dbc_ref[j, cs] = dbc
            return carry

        lax.fori_loop(0, nc // grp, group, 0)

    return pl.pallas_call(
        body, name="dn_core_bwd", grid=(N_HEADS // hb,), scratch_shapes=_dn_scratch(hb, nc) * 2,
        in_specs=[head(0), head(1), head(2), col, row, col, st, flat],
        out_specs=[pl.BlockSpec((3, t, hb * HEAD_DIM), lambda h: (0, 0, h)), col, row, col],
        out_shape=[jax.ShapeDtypeStruct((3, t, D_MODEL), F32)] + [
            jax.ShapeDtypeStruct((N_HEADS, nc, CHUNK, 1), F32), jax.ShapeDtypeStruct((N_HEADS, nc, 1, CHUNK), F32),
            jax.ShapeDtypeStruct((N_HEADS, nc, CHUNK, 1), F32)],
        compiler_params=_cparams(("parallel",)),
    )(qkv, qkv, qkv, gcol, grow, bcol, states, do)


def gates_to_heads(gates):
    t = gates.shape[0]
    nc = t // CHUNK
    g = gates[:, :N_HEADS].T.reshape(N_HEADS, nc, CHUNK)
    b = gates[:, N_HEADS:2 * N_HEADS].T.reshape(N_HEADS, nc, CHUNK)
    return g[..., None], g[:, :, None, :], b[..., None]


def heads_to_gates(dgcol, dgrow, dbcol):
    nh, nc = dgcol.shape[:2]
    dg = (dgcol[..., 0] + dgrow[:, :, 0, :]).reshape(nh, nc * CHUNK).T
    db = dbcol[..., 0].reshape(nh, nc * CHUNK).T
    return jnp.concatenate([dg, db, jnp.zeros((nc * CHUNK, HEAD_DIM - 2 * nh), F32)], axis=1)


def _dn_out(o, z, w):
    return _rms(o, w) * _silu(z)


def _gate_specs():
    o_spec = pl.BlockSpec((ROWS, D_MODEL), lambda i: (i, 0))
    z_spec = pl.BlockSpec((ROWS, D_MODEL), lambda i: (i, 3))
    w_spec = pl.BlockSpec((1, HEAD_DIM), lambda i: (0, 0))
    return o_spec, z_spec, w_spec


def dn_out_fwd(o, proj, w):
    t = o.shape[0]
    o_spec, z_spec, w_spec = _gate_specs()

    def body(o_ref, z_ref, w_ref, y_ref):
        for h in range(N_HEADS):
            y_ref[:, _lanes(h)] = _dn_out(o_ref[:, _lanes(h)], z_ref[:, _lanes(h)], w_ref[...]).astype(y_ref.dtype)

    return pl.pallas_call(
        body, name="dn_out_fwd", grid=(t // ROWS,), in_specs=[o_spec, z_spec, w_spec], out_specs=o_spec,
        out_shape=jax.ShapeDtypeStruct((t, D_MODEL), MM), compiler_params=_cparams(("parallel",)),
    )(o, proj, w)


def dn_out_bwd(o, proj, w, dcat):
    t = o.shape[0]
    o_spec, z_spec, w_spec = _gate_specs()

    def body(o_ref, z_ref, w_ref, g_ref, do_ref, dz_ref, dw_ref):
        dw_sum = jnp.zeros((1, HEAD_DIM), F32)
        for h in range(N_HEADS):
            _, vjp = jax.vjp(_dn_out, o_ref[:, _lanes(h)], z_ref[:, _lanes(h)], w_ref[...])
            do, dz, dw = vjp(g_ref[:, _lanes(h)].astype(F32))
            do_ref[:, _lanes(h)] = do
            dz_ref[:, _lanes(h)] = dz.astype(dz_ref.dtype)
            dw_sum = dw_sum + dw
        _acc(dw_ref, dw_sum, pl.program_id(0) == 0)

    return pl.pallas_call(
        body, name="dn_out_bwd", grid=(t // ROWS,), in_specs=[o_spec, z_spec, w_spec, o_spec],
        out_specs=[o_spec, o_spec, w_spec],
        out_shape=[jax.ShapeDtypeStruct((t, D_MODEL), F32), jax.ShapeDtypeStruct((t, D_MODEL), MM),
                   jax.ShapeDtypeStruct((1, HEAD_DIM), F32)],
        compiler_params=_cparams(("arbitrary",)),
    )(o, proj, w, dcat)


def _fox_norm(x, w, scale):
    return _rms(x, w) * scale


def _fox_prep_specs():
    x_spec = pl.BlockSpec((ROWS, 2 * D_MODEL), lambda i: (i, 0))
    w_spec = pl.BlockSpec((2, 1, HEAD_DIM), lambda i: (0, 0, 0))
    y_spec = pl.BlockSpec((2, ROWS, D_MODEL), lambda i: (0, i, 0))
    return x_spec, w_spec, y_spec


def fox_prep_fwd(proj, wqk):
    t = proj.shape[0]
    x_spec, w_spec, y_spec = _fox_prep_specs()

    def body(x_ref, w_ref, y_ref):
        for j in range(2 * N_HEADS):
            which, scale = j // N_HEADS, (QSCALE if j < N_HEADS else 1.0)
            y_ref[which, :, _lanes(j % N_HEADS)] = _fox_norm(x_ref[:, _lanes(j)], w_ref[which], scale).astype(y_ref.dtype)

    return pl.pallas_call(
        body, name="fox_prep_fwd", grid=(t // ROWS,), in_specs=[x_spec, w_spec], out_specs=y_spec,
        out_shape=jax.ShapeDtypeStruct((2, t, D_MODEL), MM), compiler_params=_cparams(("parallel",)),
    )(proj, wqk)


def fox_prep_bwd(proj, wqk, dq, dk):
    t = proj.shape[0]
    x_spec, w_spec, _ = _fox_prep_specs()
    g_spec = pl.BlockSpec((ROWS, D_MODEL), lambda i: (i, 0))

    def body(x_ref, w_ref, dq_ref, dk_ref, dx_ref, dw_ref):
        dws = [jnp.zeros((1, HEAD_DIM), F32), jnp.zeros((1, HEAD_DIM), F32)]
        for j in range(2 * N_HEADS):
            which, scale = j // N_HEADS, (QSCALE if j < N_HEADS else 1.0)
            g_ref = dq_ref if which == 0 else dk_ref
            _, vjp = jax.vjp(lambda x, w: _fox_norm(x, w, scale), x_ref[:, _lanes(j)], w_ref[which])
            dx, dw = vjp(g_ref[:, _lanes(j % N_HEADS)])
            dx_ref[:, _lanes(j)] = dx.astype(dx_ref.dtype)
            dws[which] = dws[which] + dw
        first = pl.program_id(0) == 0
        _acc(dw_ref.at[0], dws[0], first)
        _acc(dw_ref.at[1], dws[1], first)

    return pl.pallas_call(
        body, name="fox_prep_bwd", grid=(t // ROWS,), in_specs=[x_spec, w_spec, g_spec, g_spec],
        out_specs=[x_spec, w_spec],
        out_shape=[jax.ShapeDtypeStruct((t, 2 * D_MODEL), MM), jax.ShapeDtypeStruct((2, 1, HEAD_DIM), F32)],
        compiler_params=_cparams(("arbitrary",)),
    )(proj, wqk, dq, dk)


def _row_pick(x, i):
    r = lax.broadcasted_iota(jnp.int32, x.shape, 0)
    return jnp.sum(jnp.where(r == i, x, 0.0), axis=0, keepdims=True)


def fox_gates_fwd(proj, f_bias):
    t = proj.shape[0]
    blk = HEAD_DIM

    def body(x_ref, b_ref, o_ref):
        lane = lax.broadcasted_iota(jnp.int32, (blk, HEAD_DIM), 1)
        tri = _tri_ones(blk, False)

        def step(c, carry):
            rows = pl.ds(pl.multiple_of(c * blk, blk), blk)
            lf = jnp.where(lane < N_HEADS, -_softplus(-(x_ref[rows, :] + b_ref[...])), 0.0)
            cum = _dot(tri, lf, 1, 0, True) + carry
            o_ref[rows, :] = cum
            return _row_pick(cum, blk - 1)

        lax.fori_loop(0, t // blk, step, jnp.zeros((1, HEAD_DIM), F32))

    vec = pl.BlockSpec((1, HEAD_DIM), lambda i: (0, 0))
    return pl.pallas_call(
        body, name="fox_gates_fwd", grid=(1,),
        in_specs=[pl.BlockSpec((t, HEAD_DIM), lambda i: (0, TAIL_BLK)), vec],
        out_specs=pl.BlockSpec((t, HEAD_DIM), lambda i: (0, 0)),
        out_shape=jax.ShapeDtypeStruct((t, HEAD_DIM), F32), compiler_params=_cparams(("arbitrary",)),
    )(proj, f_bias)


def fox_gates_bwd(proj, f_bias, dfcum):
    t = proj.shape[0]
    blk = HEAD_DIM
    nb = t // blk

    def body(x_ref, b_ref, g_ref, dx_ref, db_ref):
        lane = lax.broadcasted_iota(jnp.int32, (blk, HEAD_DIM), 1)
        tri = _tri_ones(blk, True)
        db_ref[...] = jnp.zeros_like(db_ref)

        def step(i, carry):
            c = nb - 1 - i
            rows = pl.ds(pl.multiple_of(c * blk, blk), blk)
            g = jnp.where(lane < N_HEADS, g_ref[rows, :], 0.0)
            dlf = _dot(tri, g, 1, 0, True) + carry
            dx = jnp.where(lane < N_HEADS, dlf * _sigmoid(-(x_ref[rows, :] + b_ref[...])), 0.0)
            dx_ref[rows, :] = dx.astype(dx_ref.dtype)
            db_ref[...] += jnp.sum(dx, axis=0, keepdims=True)
            return carry + jnp.sum(g, axis=0, keepdims=True)

        lax.fori_loop(0, nb, step, jnp.zeros((1, HEAD_DIM), F32))

    vec = pl.BlockSpec((1, HEAD_DIM), lambda i: (0, 0))
    full = pl.BlockSpec((t, HEAD_DIM), lambda i: (0, 0))
    return pl.pallas_call(
        body, name="fox_gates_bwd", grid=(1,),
        in_specs=[pl.BlockSpec((t, HEAD_DIM), lambda i: (0, TAIL_BLK)), vec, full], out_specs=[full, vec],
        out_shape=[jax.ShapeDtypeStruct((t, HEAD_DIM), MM), jax.ShapeDtypeStruct((1, HEAD_DIM), F32)],
        compiler_params=_cparams(("arbitrary",)),
    )(proj, f_bias, dfcum)


def fcum_to_heads(fcum):
    f = fcum[:, :N_HEADS].T
    return f[:, :, None], f[:, None, :]


def heads_to_fcum(dfcol, dfrow):
    d = (dfcol[:, :, 0] + dfrow[:, 0, :]).T
    return jnp.concatenate([d, jnp.zeros((d.shape[0], HEAD_DIM - N_HEADS), F32)], axis=1)


def _fox_tq(t):
    return min(t, 256)


def _fox_specs(t):
    tq = _fox_tq(t)
    q_spec = pl.BlockSpec((None, tq, HEAD_DIM), lambda h, i: (0, i, h))
    k_spec = pl.BlockSpec((None, t, HEAD_DIM), lambda h, i: (1, 0, h))
    v_spec = pl.BlockSpec((t, HEAD_DIM), lambda h, i: (0, 2 * N_HEADS + h))
    gate_spec = pl.BlockSpec((tq, HEAD_DIM), lambda h, i: (i, 3 * N_HEADS + h))
    col_spec = pl.BlockSpec((None, tq, 1), lambda h, i: (h, i, 0))
    row_spec = pl.BlockSpec((None, 1, t), lambda h, i: (h, 0, 0))
    blk_spec = pl.BlockSpec((tq, HEAD_DIM), lambda h, i: (i, h))
    head_spec = pl.BlockSpec((t, HEAD_DIM), lambda h, i: (0, h))
    return tq, q_spec, k_spec, v_spec, gate_spec, col_spec, row_spec, blk_spec, head_spec


def _fox_scores(q, k, fcol, frow, i, tq, t):
    s = _dot(q, k, 1, 1, False) + (fcol - frow)
    r = lax.broadcasted_iota(jnp.int32, (tq, t), 0) + i * tq
    c = lax.broadcasted_iota(jnp.int32, (tq, t), 1)
    return s, c <= r


def fox_attn_fwd(qk, proj, fcol, frow):
    t = proj.shape[0]
    tq, q_spec, k_spec, v_spec, gate_spec, col_spec, row_spec, blk_spec, _ = _fox_specs(t)

    def body(q_ref, k_ref, v_ref, gate_ref, fc_ref, fr_ref, mix_ref, o_ref, lse_ref):
        def block(i):
            w = (i + 1) * tq
            s, mask = _fox_scores(q_ref[...], k_ref[0:w, :], fc_ref[...], fr_ref[:, 0:w], i, tq, w)
            s = jnp.where(mask, s, -1e30)
            m = jnp.max(s, axis=-1, keepdims=True)
            p = jnp.where(mask, jnp.exp(s - m), 0.0)
            l = jnp.sum(p, axis=-1, keepdims=True)
            o = _dot(p, v_ref[0:w, :], 1, 0, False) / l
            o_ref[...] = o
            mix_ref[...] = (o * _sigmoid(gate_ref[...])).astype(mix_ref.dtype)
            lse_ref[...] = m + jnp.log(l)

        for i in range(t // tq):
            pl.when(pl.program_id(1) == i)(functools.partial(block, i))

    return pl.pallas_call(
        body, name="fox_attn_fwd", grid=(N_HEADS, t // tq),
        in_specs=[q_spec, k_spec, v_spec, gate_spec, col_spec, row_spec], out_specs=[blk_spec, blk_spec, col_spec],
        out_shape=[jax.ShapeDtypeStruct((t, D_MODEL), MM), jax.ShapeDtypeStruct((t, D_MODEL), F32),
                   jax.ShapeDtypeStruct((N_HEADS, t, 1), F32)],
        compiler_params=_cparams(("parallel", "parallel")),
    )(qk, qk, proj, proj, fcol, frow)


def fox_attn_bwd(qk, proj, fcol, frow, o, lse, dcat):
    t = proj.shape[0]
    tq, q_spec, k_spec, v_spec, gate_spec, col_spec, row_spec, blk_spec, head_spec = _fox_specs(t)

    def body(q_ref, k_ref, v_ref, gate_ref, fc_ref, fr_ref, o_ref, lse_ref, g_ref,
             dq_ref, dk_ref, dv_ref, dgate_ref, dfc_ref, dfr_ref):
        @pl.when(pl.program_id(1) == 0)
        def _():
            dk_ref[...] = jnp.zeros_like(dk_ref)
            dv_ref[...] = jnp.zeros_like(dv_ref)
            dfr_ref[...] = jnp.zeros_like(dfr_ref)

        def block(i):
            w = (i + 1) * tq
            sg = _sigmoid(gate_ref[...])
            g = g_ref[...].astype(F32)
            o_pre = o_ref[...]
            do = g * sg
            dgate_ref[...] = (g * o_pre * sg * (1.0 - sg)).astype(dgate_ref.dtype)
            s, mask = _fox_scores(q_ref[...], k_ref[0:w, :], fc_ref[...], fr_ref[:, 0:w], i, tq, w)
            p = jnp.where(mask, jnp.exp(jnp.where(mask, s, 0.0) - lse_ref[...]), 0.0)
            dp = _dot(do, v_ref[0:w, :], 1, 1, False)
            delta = jnp.sum(do * o_pre, axis=-1, keepdims=True)
            ds = p * (dp - delta)
            dq_ref[...] = _dot(ds, k_ref[0:w, :], 1, 0, False)
            dk_ref[0:w, :] += _dot(ds, q_ref[...], 0, 0, False)
            dv_ref[0:w, :] += _dot(p, do, 0, 0, False)
            dfc_ref[...] = jnp.sum(ds, axis=-1, keepdims=True)
            dfr_ref[:, 0:w] += -jnp.sum(ds, axis=0, keepdims=True)

        for i in range(t // tq):
            pl.when(pl.program_id(1) == i)(functools.partial(block, i))

    f32 = lambda *s: jax.ShapeDtypeStruct(s, F32)
    return pl.pallas_call(
        body, name="fox_attn_bwd", grid=(N_HEADS, t // tq),
        in_specs=[q_spec, k_spec, v_spec, gate_spec, col_spec, row_spec, blk_spec, col_spec, blk_spec],
        out_specs=[blk_spec, head_spec, head_spec, blk_spec, col_spec, row_spec],
        out_shape=[f32(t, D_MODEL), f32(t, D_MODEL), f32(t, D_MODEL), jax.ShapeDtypeStruct((t, D_MODEL), MM),
                   f32(N_HEADS, t, 1), f32(N_HEADS, 1, t)],
        compiler_params=_cparams(("parallel", "arbitrary")),
    )(qk, qk, proj, proj, fcol, frow, o, lse, dcat)


def adamw(w, g, m, v, *, name):
    r, c = w.shape
    rb = ROWS if r % ROWS == 0 else r

    def body(w_ref, g_ref, m_ref, v_ref, d_ref, nm_ref, nv_ref):
        g_ = g_ref[...]
        m_ = ADAM_B1 * m_ref[...] + (1.0 - ADAM_B1) * g_
        v_ = ADAM_B2 * v_ref[...] + (1.0 - ADAM_B2) * jnp.square(g_)
        m_hat = m_ / (1.0 - ADAM_B1 ** ADAM_STEP)
        v_hat = v_ / (1.0 - ADAM_B2 ** ADAM_STEP)
        d_ref[...] = -ADAM_LR * (m_hat / (jnp.sqrt(v_hat) + ADAM_EPS) + ADAM_WD * w_ref[...])
        nm_ref[...] = m_
        nv_ref[...] = v_

    blk = pl.BlockSpec((rb, c), lambda i: (i, 0))
    shp = jax.ShapeDtypeStruct((r, c), F32)
    return pl.pallas_call(body, name=name, grid=(r // rb,), in_specs=[blk] * 4, out_specs=[blk] * 3,
                          out_shape=[shp] * 3, compiler_params=_cparams(("parallel",)))(w, g, m, v)


def _place():
    x, y, c = lax.axis_index("x"), lax.axis_index("y"), lax.axis_index("c")
    return x, y, c, [(1 - x, y), (x, 1 - y), (1 - x, 1 - y)]


ANY = pl.BlockSpec(memory_space=pl.ANY)


def all_reduce_small(v):
    r, w = v.shape

    def body(v_ref, o_ref, buf, send_sems, recv_sems):
        x, y, c, _ = _place()
        me = 4 * x + 2 * y + c
        flip = lambda a, bit: 1 - a if bit else a
        cps = []
        for k in range(1, N_DEV):
            peer = (flip(x, k & 4), flip(y, k & 2), flip(c, k & 1))
            cp = pltpu.make_async_remote_copy(src_ref=v_ref, dst_ref=buf.at[me], send_sem=send_sems.at[k - 1],
                                              recv_sem=recv_sems.at[k - 1], device_id=peer, device_id_type=MESH)
            cp.start()
            cps.append((cp, 4 * peer[0] + 2 * peer[1] + peer[2]))
        buf[me] = v_ref[...]
        for k, (cp, peer_id) in enumerate(cps):
            pltpu.make_async_remote_copy(src_ref=v_ref, dst_ref=buf.at[peer_id], send_sem=send_sems.at[k],
                                         recv_sem=recv_sems.at[k], device_id=(x, y, c), device_id_type=MESH).wait_recv()
        for cp, _ in cps:
            cp.wait_send()
        acc = buf[0]
        for d in range(1, N_DEV):
            acc = acc + buf[d]
        o_ref[...] = acc

    vm = pl.BlockSpec(memory_space=pltpu.VMEM)
    return pl.pallas_call(
        body, name="all_reduce_small", in_specs=[vm], out_specs=vm, out_shape=jax.ShapeDtypeStruct((r, w), F32),
        scratch_shapes=[pltpu.VMEM((N_DEV, r, w), F32), pltpu.SemaphoreType.DMA((N_DEV - 1,)),
                        pltpu.SemaphoreType.DMA((N_DEV - 1,))],
    )(v)


def _vec8(v):
    return jnp.zeros((1, HEAD_DIM), F32).at[0, :N_HEADS].set(v.reshape(N_HEADS))


def _layer_fwd(i, x_in, wt, sm, mem_k, mem_v, late=None):
    tag = f"l{i}_"
    h = rms_fwd(x_in, sm["norm1_w"][i][None], name=tag + "rms1")
    w_in = wt["dn_w_in"] if i == 0 else wt["fox_w_in"]
    proj = matmul(h, w_in, name=tag + "proj", tm=256, tk=1024)
    sv = dict(x_in=x_in, h=h, proj=proj)
    if i == 0:
        qkv = dn_prep_fwd(proj, wt["conv_w"])
        gates = dn_gates_fwd(proj, _vec8(sm["dn_a_log"]), _vec8(sm["dn_dt_bias"]))
        gcol, grow, bcol = gates_to_heads(gates)
        o, states = dn_core_fwd(qkv, gcol, grow, bcol)
        mix = dn_out_fwd(o, proj, sm["dn_o_norm_w"])
        sv.update(qkv=qkv, gcol=gcol, grow=grow, bcol=bcol, states=states, o=o)
    else:
        wqk = jnp.stack([sm["fox_q_norm_w"], sm["fox_k_norm_w"]])
        qk = fox_prep_fwd(proj, wqk)
        fcum = fox_gates_fwd(proj, _vec8(sm["fox_f_bias"]))
        fcol, frow = fcum_to_heads(fcum)
        mix, o, lse = fox_attn_fwd(qk, proj, fcol, frow)
        sv.update(wqk=wqk, qk=qk, fcol=fcol, frow=frow, o=o, lse=lse)
    mem_out = memattn_fwd(proj, sm["memq_norm_w"][i][None], mem_k, mem_v, name=tag + "memattn_fwd")
    cat = jnp.concatenate([mix, mem_out], axis=1)
    if late is not None:
        wt.update(late(cat))
    x_mid = matmul(cat, wt["w_out"][i], res=x_in, name=tag + "out_proj")
    h2 = rms_fwd(x_mid, sm["norm2_w"][i][None], name=tag + "rms2")
    ff, act = matmul(h2, wt["w_mlp1"][i], b_slots=True, also_sqrelu=True, name=tag + "mlp1")
    x_out = matmul(act, wt["w_mlp2"][i], res=x_mid, name=tag + "mlp2")
    sv.update(cat=cat, x_mid=x_mid, h2=h2, ff=ff, act=act)
    return x_out, sv


def _layer_bwd(i, dx_out, sv, wt, sm, mem_k, mem_v):
    tag = f"l{i}_"
    big, small = {}, {}
    dff = matmul(dx_out, wt["w_mlp2"][i], tb=True, times_dsqrelu=sv["ff"], out_dtype=MM, name=tag + "d_ff")
    big["w_mlp2"] = matmul(sv["act"], dx_out, ta=True, name=tag + "d_w_mlp2")
    dh2 = matmul(dff, wt["w_mlp1"][i], tb=True, b_slots=True, name=tag + "d_h2")
    big["w_mlp1"] = matmul(sv["h2"], dff, ta=True, name=tag + "d_w_mlp1", tm=512, tn=D_FF, tk=512)
    dx_mid, small["norm2_w"] = rms_bwd(sv["x_mid"], sm["norm2_w"][i][None], dh2, dx_out, name=tag + "rms2_bwd")
    dcat = matmul(dx_mid, wt["w_out"][i], tb=True, name=tag + "d_cat")
    big["w_out"] = matmul(sv["cat"], dx_mid, ta=True, name=tag + "d_w_out")
    proj = sv["proj"]
    dqm, small["memq_norm_w"], dmk, dmv = memattn_bwd(proj, sm["memq_norm_w"][i][None], mem_k, mem_v, dcat,
                                                      name=tag + "memattn_bwd")
    t = proj.shape[0]
    pad = jnp.zeros((t, PROJ_W - TAIL - HEAD_DIM), MM)
    if i == 0:
        do, dz, small["dn_o_norm_w"] = dn_out_bwd(sv["o"], proj, sm["dn_o_norm_w"], dcat)
        dqkv, dgc, dgr, dbc = dn_core_bwd(sv["qkv"], sv["gcol"], sv["grow"], sv["bcol"], sv["states"], do)
        dtail, dal, ddt = dn_gates_bwd(proj, _vec8(sm["dn_a_log"]), _vec8(sm["dn_dt_bias"]), heads_to_gates(dgc, dgr, dbc))
        dmain, dconv = dn_prep_bwd(proj, wt["conv_w"], dqkv)
        small["dn_a_log"], small["dn_dt_bias"] = dal[:, :N_HEADS], ddt[:, :N_HEADS]
        big["conv_w"] = dconv[:4]
        dproj = jnp.concatenate([dmain, dz, dqm, dtail, pad], axis=1)
    else:
        dq, dk, dv, dgate, dfc, dfr = fox_attn_bwd(sv["qk"], proj, sv["fcol"], sv["frow"], sv["o"], sv["lse"], dcat)
        dtail, dfb = fox_gates_bwd(proj, _vec8(sm["fox_f_bias"]), heads_to_fcum(dfc, dfr))
        dqk, dwqk = fox_prep_bwd(proj, sv["wqk"], dq, dk)
        small["fox_f_bias"] = dfb[:, :N_HEADS]
        small["fox_q_norm_w"], small["fox_k_norm_w"] = dwqk[0], dwqk[1]
        dproj = jnp.concatenate([dqk, dv.astype(MM), dgate, dqm, dtail, pad], axis=1)
    w_in = wt["dn_w_in"] if i == 0 else wt["fox_w_in"]
    dh = matmul(dproj, w_in, tb=True, name=tag + "d_h", tm=512)
    big["w_in"] = matmul(sv["h"], dproj, ta=True, name=tag + "d_w_in", tm=256)
    dx_in, small["norm1_w"] = rms_bwd(sv["x_in"], sm["norm1_w"][i][None], dh, dx_mid, name=tag + "rms1_bwd")
    return dx_in, big, small, (dmk, dmv)


def local_step(x, mem, target, wt, sm, late=None):
    wt = dict(wt)
    mem_k, mem_v = mem_fwd(mem, sm["mem_norm_w"][None], wt["w_mem_kv"], sm["mem_k_norm_w"][None])
    x0, sv0 = _layer_fwd(0, x, wt, sm, mem_k, mem_v, late)
    x1, sv1 = _layer_fwd(1, x0, wt, sm, mem_k, mem_v)
    dy, loss = loss_fwd(x1, target, name="loss")
    dx1, big1, small1, dm1 = _layer_bwd(1, dy, sv1, wt, sm, mem_k, mem_v)
    dx0, big0, small0, dm0 = _layer_bwd(0, dx1, sv0, wt, sm, mem_k, mem_v)
    dwn, dwkv, dwkn = mem_bwd(mem, sm["mem_norm_w"][None], wt["w_mem_kv"], sm["mem_k_norm_w"][None], *dm0, *dm1)
    small = dict(mem_norm_w=dwn[0], mem_k_norm_w=dwkn[0],
                 norm1_w=jnp.concatenate([small0["norm1_w"], small1["norm1_w"]]),
                 norm2_w=jnp.concatenate([small0["norm2_w"], small1["norm2_w"]]),
                 memq_norm_w=jnp.concatenate([small0["memq_norm_w"], small1["memq_norm_w"]]),
                 dn_a_log=small0["dn_a_log"], dn_dt_bias=small0["dn_dt_bias"], dn_o_norm_w=small0["dn_o_norm_w"],
                 fox_f_bias=small1["fox_f_bias"], fox_q_norm_w=small1["fox_q_norm_w"], fox_k_norm_w=small1["fox_k_norm_w"])
    big = dict(w_mem_kv=dwkv, dn_w_in=big0["w_in"], fox_w_in=big1["w_in"], conv_w=big0["conv_w"],
               w_out=[big0["w_out"], big1["w_out"]], w_mlp1=[big0["w_mlp1"], big1["w_mlp1"]],
               w_mlp2=[big0["w_mlp2"], big1["w_mlp2"]])
    return loss, dx0, big, small


def w_in_to_kernel(w, n_scalars):
    pad = jnp.zeros((w.shape[0], PROJ_W - TAIL - n_scalars), w.dtype)
    return jnp.concatenate([w[:, :4096], w[:, 4096 + n_scalars:], w[:, 4096:4096 + n_scalars], pad], axis=1)


def w_in_from_kernel(w, n_scalars):
    return jnp.concatenate([w[:, :4096], w[:, TAIL:TAIL + n_scalars], w[:, 4096:TAIL]], axis=1)


BIG_SPECS = dict(w_mem_kv=("rows", 1, 256, 1024), w_out=("rows", 2, 384, 1024), w_mlp2=("rows", 2, 1024, 1024),
                 w_mlp1=("cols", 2, 1024, 1024), dn_w_in=("rows", 1, 1024, 1156), fox_w_in=("rows", 1, 1024, 1154))
BIG_NAMES = tuple(BIG_SPECS)
EARLY_NAMES = ("w_mem_kv", "dn_w_in")
LATE_NAMES = ("w_out", "w_mlp2", "w_mlp1", "fox_w_in")


def _full_shape(name, half=False):
    kind, a, b, c = BIG_SPECS[name]
    b = b // 2 if half else b
    return (a, N_CHIP, b, c) if kind == "rows" else (a, b, N_CHIP * c)


def _ds(start, size, align):
    return pl.ds(start if isinstance(start, int) else pl.multiple_of(start, align), size)


def _half_rows(name, h):
    b = BIG_SPECS[name][2]
    return _ds(h * (b // 2), b // 2, 16)


def _shard_idx(name, h):
    return (slice(None), _half_rows(name, h), slice(None))


def _full_idx(name, j=None, h=None):
    kind, _, _, c = BIG_SPECS[name]
    rows = slice(None) if h is None else _half_rows(name, h)
    if kind == "rows":
        return (slice(None), slice(None) if j is None else j, rows, slice(None))
    return (slice(None), rows, slice(None) if j is None else _ds(j * c, c, 128))


def _slots_shape(name):
    _, a, b, c = BIG_SPECS[name]
    return (a, N_CHIP, b, c)


def _slots_idx(name, j, h):
    return (slice(None), j, _half_rows(name, h), slice(None))


def _row_block(name):
    hs = BIG_SPECS[name][2] // 2
    return hs if hs <= ROWS else ROWS


def _remote(src, dst, send_sem, recv_sem, to):
    return pltpu.make_async_remote_copy(src_ref=src, dst_ref=dst, send_sem=send_sem, recv_sem=recv_sem, device_id=to,
                                        device_id_type=MESH)


HBM = pl.BlockSpec(memory_space=pltpu.HBM)
SEM = pl.BlockSpec(memory_space=pltpu.SEMAPHORE)
EFFECT = pltpu.CompilerParams(has_side_effects=pltpu.SideEffectType.DATAFLOW_SIDE_EFFECTING)


def _in_hbm(a):
    return pltpu.with_memory_space_constraint(a, pltpu.HBM)


def _chip_copies(names, ins, lands, send_sems, recv_sems):
    x, y, c, chips = _place()
    return [_remote(ins[a].at[_shard_idx(name, c)], lands[a].at[_slots_idx(name, 2 * x + y, c)], send_sems.at[3 * a + k],
                    recv_sems.at[3 * a + k], (chip[0], chip[1], c))
            for a, name in enumerate(names) for k, chip in enumerate(chips)]


def all_gather_start(shards, names):
    n = len(names)

    def body(*refs):
        ins, lands, send_sems, recv_sems, token = refs[:n], refs[n:2 * n], refs[2 * n], refs[2 * n + 1], refs[-1]
        for cp in _chip_copies(names, ins, lands, send_sems, recv_sems):
            cp.start()
        token[...] = jnp.zeros_like(token)

    ins = [_in_hbm(shards[name]) for name in names]
    lands = [_in_hbm(lax.empty(_slots_shape(name), MM)) for name in names]
    sems = (pltpu.SemaphoreType.DMA((3 * n,)), pltpu.SemaphoreType.DMA((3 * n,)))
    outs = pl.pallas_call(
        body, name="all_gather_start", in_specs=[HBM] * (2 * n),
        out_specs=(SEM, SEM) + (HBM,) * (2 * n) + (pl.BlockSpec(memory_space=pltpu.VMEM),),
        out_shape=sems + tuple(pltpu.HBM(a.shape, a.dtype) for a in ins + lands) + (jax.ShapeDtypeStruct((8, HEAD_DIM), F32),),
        input_output_aliases={a: 2 + a for a in range(2 * n)}, compiler_params=EFFECT,
    )(*ins, *lands)
    return outs[:-1], outs[-1]


def all_gather_wait(state, names, after):
    n = len(names)

    def body(*refs):
        send_sems, recv_sems, ins, lands = refs[0], refs[1], refs[2:2 + n], refs[2 + n:2 + 2 * n]
        for cp in _chip_copies(names, ins, lands, send_sems, recv_sems):
            cp.wait_send()
            cp.wait_recv()

    outs = pl.pallas_call(
        body, name="all_gather_wait", in_specs=[SEM, SEM] + [HBM] * (2 * n) + [ANY], out_specs=(HBM,) * (2 * n),
        out_shape=tuple(pltpu.HBM(a.shape, a.dtype) for a in state[2:]),
        input_output_aliases={2 + a: a for a in range(2 * n)}, compiler_params=EFFECT,
    )(*state, after)
    return dict(zip(names, outs[:n])), dict(zip(names, outs[n:]))


def all_gather_pass_on(lands, names):
    n = len(names)

    def body(*refs):
        outs, send_sems, recv_sems = refs[n:2 * n], refs[2 * n], refs[2 * n + 1]
        x, y, c, chips = _place()
        work = [(3 * a + k, a, name, 2 * chip[0] + chip[1]) for a, name in enumerate(names) for k, chip in enumerate(chips)]
        cps = []
        for s, a, name, slot in work:
            landed = outs[a].at[_slots_idx(name, slot, c)]
            cps.append(_remote(landed, landed, send_sems.at[s], recv_sems.at[s], (x, y, 1 - c)))
            cps[-1].start()
        for s, a, name, slot in work:
            passed = outs[a].at[_slots_idx(name, slot, 1 - c)]
            _remote(passed, passed, send_sems.at[s], recv_sems.at[s], (x, y, 1 - c)).wait_recv()
        for cp in cps:
            cp.wait_send()

    outs = pl.pallas_call(
        body, name="all_gather_pass_on", in_specs=[ANY] * n, out_specs=[ANY] * n,
        input_output_aliases={a: a for a in range(n)},
        out_shape=[jax.ShapeDtypeStruct(_slots_shape(name), MM) for name in names],
        scratch_shapes=[pltpu.SemaphoreType.DMA((3 * n,)), pltpu.SemaphoreType.DMA((3 * n,))],
    )(*[lands[name] for name in names])
    return dict(zip(names, outs))


def all_gather_big(shards, names):
    n = len(names)
    BIG_NAMES = names

    def body(*refs):
        ins, outs = refs[:n], refs[n:2 * n]
        send_sems, recv_sems, fsend_sems, frecv_sems = refs[2 * n:]
        x, y, c, chips = _place()
        me_chip, sibling = 2 * x + y, (x, y, 1 - c)
        work = [(3 * a + k, a, name, chip) for a, name in enumerate(BIG_NAMES) for k, chip in enumerate(chips)]
        sends = []
        for s, a, name, chip in work:
            cp = _remote(ins[a].at[_shard_idx(name, c)], outs[a].at[_slots_idx(name, me_chip, c)], send_sems.at[s],
                         recv_sems.at[s], (chip[0], chip[1], c))
            cp.start()
            sends.append(cp)
        for s, a, name, chip in work:
            landed = outs[a].at[_slots_idx(name, 2 * chip[0] + chip[1], c)]
            _remote(landed, landed, send_sems.at[s], recv_sems.at[s], (chip[0], chip[1], c)).wait_recv()
            cp = _remote(landed, landed, fsend_sems.at[s], frecv_sems.at[s], sibling)
            cp.start()
            sends.append(cp)
        for s, a, name, chip in work:
            passed = outs[a].at[_slots_idx(name, 2 * chip[0] + chip[1], 1 - c)]
            _remote(passed, passed, fsend_sems.at[s], frecv_sems.at[s], sibling).wait_recv()
        for cp in sends:
            cp.wait_send()

    outs = pl.pallas_call(
        body, name="all_gather_big", in_specs=[ANY] * n, out_specs=[ANY] * n,
        out_shape=[jax.ShapeDtypeStruct(_slots_shape(name), MM) for name in BIG_NAMES],
        scratch_shapes=[pltpu.SemaphoreType.DMA((3 * n,))] * 4,
    )(*[shards[name] for name in BIG_NAMES])
    return dict(zip(BIG_NAMES, outs))


def with_own_slot(name, full, shard, chip):
    return lax.dynamic_update_slice(full, shard[:, None], (0, chip, 0, 0))


def rs_pair_exchange_big(grads):
    n = len(BIG_NAMES)

    def body(*refs):
        ins, outs, send_sems, recv_sems = refs[:n], refs[n:2 * n], refs[2 * n], refs[2 * n + 1]
        x, y, c, _ = _place()
        cps = []
        for a, name in enumerate(BIG_NAMES):
            cp = _remote(ins[a].at[_full_idx(name, None, 1 - c)], outs[a], send_sems.at[a], recv_sems.at[a], (x, y, 1 - c))
            cp.start()
            cps.append(cp)
        for cp in cps:
            cp.wait()

    outs = pl.pallas_call(
        body, name="rs_pair_exchange_big", in_specs=[ANY] * n, out_specs=[ANY] * n,
        out_shape=[jax.ShapeDtypeStruct(_full_shape(name, half=True), F32) for name in BIG_NAMES],
        scratch_shapes=[pltpu.SemaphoreType.DMA((n,)), pltpu.SemaphoreType.DMA((n,))],
    )(*[grads[name] for name in BIG_NAMES])
    return dict(zip(BIG_NAMES, outs))


def rs_pair_add_big(name, place, g, got):
    kind, a_, b_, c_ = BIG_SPECS[name]
    rb = _row_block(name)
    nb = (b_ // 2) // rb

    def body(place_ref, g_ref, got_ref, o_ref):
        o_ref[...] = (g_ref[...] + got_ref[...]).astype(o_ref.dtype)

    if kind == "rows":
        g_spec = pl.BlockSpec((None, None, rb, c_), lambda a, j, i, p: (a, j, p[0] * nb + i, 0))
        o_spec = pl.BlockSpec((None, None, rb, c_), lambda a, j, i, p: (a, j, i, 0))
    else:
        g_spec = pl.BlockSpec((None, rb, c_), lambda a, j, i, p: (a, p[0] * nb + i, j))
        o_spec = pl.BlockSpec((None, rb, c_), lambda a, j, i, p: (a, i, j))
    return pl.pallas_call(
        body, name="rs_pair_add_" + name,
        grid_spec=pltpu.PrefetchScalarGridSpec(num_scalar_prefetch=1, grid=(a_, N_CHIP, nb), in_specs=[g_spec, o_spec],
                                               out_specs=o_spec),
        out_shape=jax.ShapeDtypeStruct(_full_shape(name, half=True), MM),
        compiler_params=_cparams(("parallel", "parallel", "parallel")),
    )(place, g, got)


def rs_chip_exchange_big(pairs):
    n = len(BIG_NAMES)

    def body(*refs):
        ins, outs, send_sems, recv_sems = refs[:n], refs[n:2 * n], refs[2 * n], refs[2 * n + 1]
        x, y, c, chips = _place()
        cps = []
        for a, name in enumerate(BIG_NAMES):
            for k, chip in enumerate(chips):
                s = 3 * a + k
                cp = _remote(ins[a].at[_full_idx(name, 2 * chip[0] + chip[1])], outs[a].at[k], send_sems.at[s],
                             recv_sems.at[s], (chip[0], chip[1], c))
                cp.start()
                cps.append(cp)
        for cp in cps:
            cp.wait()

    def got_shape(name):
        _, a_, b_, c_ = BIG_SPECS[name]
        return jax.ShapeDtypeStruct((3, a_, b_ // 2, c_), MM)

    outs = pl.pallas_call(
        body, name="rs_chip_exchange_big", in_specs=[ANY] * n, out_specs=[ANY] * n,
        out_shape=[got_shape(name) for name in BIG_NAMES],
        scratch_shapes=[pltpu.SemaphoreType.DMA((3 * n,)), pltpu.SemaphoreType.DMA((3 * n,))],
    )(*[pairs[name] for name in BIG_NAMES])
    return dict(zip(BIG_NAMES, outs))


def rs_chip_add_big(name, place, g, got_pair, got_chips):
    kind, a_, b_, c_ = BIG_SPECS[name]
    rb = _row_block(name)
    nb = (b_ // 2) // rb

    def body(place_ref, g_ref, s_ref, r0_ref, r1_ref, r2_ref, o_ref):
        own = g_ref[...] + s_ref[...]
        o_ref[...] = ((own + r0_ref[...].astype(F32)) + r1_ref[...].astype(F32)) + r2_ref[...].astype(F32)

    if kind == "rows":
        g_spec = pl.BlockSpec((None, None, rb, c_), lambda a, i, p: (a, p[1], p[0] * nb + i, 0))
        s_spec = pl.BlockSpec((None, None, rb, c_), lambda a, i, p: (a, p[1], i, 0))
    else:
        g_spec = pl.BlockSpec((None, rb, c_), lambda a, i, p: (a, p[0] * nb + i, p[1]))
        s_spec = pl.BlockSpec((None, rb, c_), lambda a, i, p: (a, i, p[1]))
    r_spec = lambda k: pl.BlockSpec((None, None, rb, c_), lambda a, i, p: (k, a, i, 0))
    return pl.pallas_call(
        body, name="rs_chip_add_" + name,
        grid_spec=pltpu.PrefetchScalarGridSpec(
            num_scalar_prefetch=1, grid=(a_, nb), in_specs=[g_spec, s_spec, r_spec(0), r_spec(1), r_spec(2)],
            out_specs=pl.BlockSpec((None, rb, c_), lambda a, i, p: (a, p[0] * nb + i, 0))),
        out_shape=jax.ShapeDtypeStruct((a_, b_, c_), F32), compiler_params=_cparams(("parallel", "parallel")),
    )(place, g, got_pair, got_chips, got_chips, got_chips)


def rs_pair_gather_big(halves):
    n = len(BIG_NAMES)

    def body(*refs):
        outs, send_sems, recv_sems = refs[n:2 * n], refs[2 * n], refs[2 * n + 1]
        x, y, c, _ = _place()
        cps = []
        for a, name in enumerate(BIG_NAMES):
            mine = outs[a].at[_shard_idx(name, c)]
            cp = _remote(mine, mine, send_sems.at[a], recv_sems.at[a], (x, y, 1 - c))
            cp.start()
            cps.append(cp)
        for a, name in enumerate(BIG_NAMES):
            cps[a].wait_send()
            theirs = outs[a].at[_shard_idx(name, 1 - c)]
            _remote(theirs, theirs, send_sems.at[a], recv_sems.at[a], (x, y, 1 - c)).wait_recv()

    outs = pl.pallas_call(
        body, name="rs_pair_gather_big", in_specs=[ANY] * n, out_specs=[ANY] * n,
        input_output_aliases={a: a for a in range(n)},
        out_shape=[jax.ShapeDtypeStruct(BIG_SPECS[name][1:], F32) for name in BIG_NAMES],
        scratch_shapes=[pltpu.SemaphoreType.DMA((n,)), pltpu.SemaphoreType.DMA((n,))],
    )(*[halves[name] for name in BIG_NAMES])
    return dict(zip(BIG_NAMES, outs))


def reduce_scatter_big(grads, place):
    got_pair = rs_pair_exchange_big(grads)
    pairs = {name: rs_pair_add_big(name, place, grads[name], got_pair[name]) for name in BIG_NAMES}
    got_chips = rs_chip_exchange_big(pairs)
    return rs_pair_gather_big({name: rs_chip_add_big(name, place, grads[name], got_pair[name], got_chips[name])
                               for name in BIG_NAMES})


PACK_W = 1024
SMALL =(("mem_norm_w", 1024), ("mem_k_norm_w", 128), ("norm1_w", 2048), ("dn_a_log", 8), ("dn_dt_bias", 8),
         ("dn_o_norm_w", 128), ("fox_f_bias", 8), ("fox_q_norm_w", 128), ("fox_k_norm_w", 128), ("memq_norm_w", 256),
         ("norm2_w", 2048))
SMALL_ROWS = 8
CONV_ROWS = 4 * 3 * D_MODEL // PACK_W
LOSS_AT = sum(n for _, n in SMALL)


def pack_small(parts, extra=None):
    flat = [parts[name].astype(F32).reshape(-1) for name, _ in SMALL]
    used = LOSS_AT
    if extra is not None:
        flat.append(extra.reshape(1))
        used += 1
    flat.append(jnp.zeros((SMALL_ROWS * PACK_W - used,), F32))
    return jnp.concatenate(flat).reshape(SMALL_ROWS, PACK_W)


def unpack_small(packed, shapes):
    flat, out, at = packed.reshape(-1), {}, 0
    for name, n in SMALL:
        out[name] = flat[at:at + n].reshape(shapes[name])
        at += n
    return out


def _adam_all(w, g, m, v, name):
    shape = w.shape
    r2 = lambda a: a.reshape(-1, shape[-1])
    d, nm, nv = adamw(r2(w), r2(g), r2(m), r2(v), name=name)
    return d.reshape(shape), nm.reshape(shape), nv.reshape(shape)


BIG = ("w_mem_kv", "dn_w_in", "dn_conv_w", "fox_w_in", "w_out", "w_mlp1", "w_mlp2")
WEIGHTS = ("mem_norm_w", "w_mem_kv", "mem_k_norm_w", "norm1_w", "dn_w_in", "dn_conv_w", "dn_a_log", "dn_dt_bias",
           "dn_o_norm_w", "fox_w_in", "fox_f_bias", "fox_q_norm_w", "fox_k_norm_w", "memq_norm_w", "w_out", "norm2_w",
           "w_mlp1", "w_mlp2")


def kernel(x, mem, mem_norm_w, w_mem_kv, mem_k_norm_w, norm1_w, dn_w_in, dn_conv_w, dn_a_log, dn_dt_bias, dn_o_norm_w, fox_w_in, fox_f_bias, fox_q_norm_w, fox_k_norm_w, memq_norm_w, w_out, norm2_w, w_mlp1, w_mlp2, loss_target, m_mem_norm_w, m_w_mem_kv, m_mem_k_norm_w, m_norm1_w, m_dn_w_in, m_dn_conv_w, m_dn_a_log, m_dn_dt_bias, m_dn_o_norm_w, m_fox_w_in, m_fox_f_bias, m_fox_q_norm_w, m_fox_k_norm_w, m_memq_norm_w, m_w_out, m_norm2_w, m_w_mlp1, m_w_mlp2, v_mem_norm_w, v_w_mem_kv, v_mem_k_norm_w, v_norm1_w, v_dn_w_in, v_dn_conv_w, v_dn_a_log, v_dn_dt_bias, v_dn_o_norm_w, v_fox_w_in, v_fox_f_bias, v_fox_q_norm_w, v_fox_k_norm_w, v_memq_norm_w, v_w_out, v_norm2_w, v_w_mlp1, v_w_mlp2):
    args = dict(locals())
    w = {n: args[n] for n in WEIGHTS}
    m = {n: args["m_" + n] for n in WEIGHTS}
    v = {n: args["v_" + n] for n in WEIGHTS}
    core, chip = lax.axis_index("c"), 2 * lax.axis_index("x") + lax.axis_index("y")
    place = jnp.stack([core, chip]).astype(jnp.int32)

    shards = {name: w[name].reshape(BIG_SPECS[name][1:]).astype(MM) for name in BIG_NAMES}
    w_in_full = lambda arr, n_scalars: w_in_to_kernel(arr[0].transpose(1, 0, 2).reshape(D_MODEL, -1), n_scalars)
    early = {name: with_own_slot(name, arr, shards[name], chip)
             for name, arr in all_gather_big(shards, EARLY_NAMES).items()}
    conv_mine = jnp.where(core == 0, dn_conv_w[0], 0.0)
    conv_placed = lax.dynamic_update_slice(jnp.zeros((4, 3 * D_MODEL), F32), conv_mine, (0, 768 * chip))
    conv_full = all_reduce_small(jnp.pad(conv_placed.reshape(CONV_ROWS, PACK_W), ((0, 16 - CONV_ROWS), (0, 0))))
    late_shards, early, conv_full = lax.optimization_barrier(
        ({name: shards[name] for name in LATE_NAMES}, early, conv_full))
    late_state, token = all_gather_start(late_shards, LATE_NAMES)
    tie = token[0, 0]
    wt = dict(w_mem_kv=early["w_mem_kv"].reshape(D_MODEL, 2 * MEM_WIDTH) + tie.astype(MM),
              dn_w_in=w_in_full(early["dn_w_in"], 2 * N_HEADS), conv_w=conv_full[:CONV_ROWS].reshape(4, 3 * D_MODEL))

    def late(after):
        late_shards, lands = all_gather_wait(late_state, LATE_NAMES, after)
        full = {name: with_own_slot(name, arr, late_shards[name], chip)
                for name, arr in all_gather_pass_on(lands, LATE_NAMES).items()}
        return dict(fox_w_in=w_in_full(full["fox_w_in"], N_HEADS), w_out=full["w_out"].reshape(2, 3 * MEM_WIDTH, D_MODEL),
                    w_mlp1=full["w_mlp1"], w_mlp2=full["w_mlp2"].reshape(2, D_FF, D_MODEL))

    sm = dict(mem_norm_w=mem_norm_w, mem_k_norm_w=mem_k_norm_w, norm1_w=norm1_w, norm2_w=norm2_w, memq_norm_w=memq_norm_w,
              dn_a_log=dn_a_log[0], dn_dt_bias=dn_dt_bias[0], dn_o_norm_w=dn_o_norm_w, fox_f_bias=fox_f_bias[0],
              fox_q_norm_w=fox_q_norm_w, fox_k_norm_w=fox_k_norm_w)
    loss_part, dx, big, small = local_step(x[0], mem[0], loss_target[0], wt, sm, late)

    w_in_slots = lambda g, n_scalars: w_in_from_kernel(g, n_scalars).reshape(D_MODEL, N_CHIP, -1).transpose(1, 0, 2)[None]
    full_grads = dict(w_mem_kv=big["w_mem_kv"].reshape(_full_shape("w_mem_kv")),
                      w_out=jnp.stack(big["w_out"]).reshape(_full_shape("w_out")),
                      w_mlp2=jnp.stack(big["w_mlp2"]).reshape(_full_shape("w_mlp2")), w_mlp1=jnp.stack(big["w_mlp1"]),
                      dn_w_in=w_in_slots(big["dn_w_in"], 2 * N_HEADS), fox_w_in=w_in_slots(big["fox_w_in"], N_HEADS))
    big_sum = reduce_scatter_big(full_grads, place)
    small_pack = jnp.concatenate([pack_small(small, loss_part[0, :1]), big["conv_w"].reshape(CONV_ROWS, PACK_W),
                                  jnp.zeros((24 - SMALL_ROWS - CONV_ROWS, PACK_W), F32)])
    small_all = all_reduce_small(small_pack)
    small_sum = small_all[:SMALL_ROWS]
    conv_sum = lax.dynamic_slice(small_all[SMALL_ROWS:SMALL_ROWS + CONV_ROWS].reshape(4, 3 * D_MODEL), (0, 768 * chip), (4, 768))
    loss = small_sum.reshape(-1)[LOSS_AT]
    grads = unpack_small(small_sum, {n: w[n].shape for n, _ in SMALL})
    grads.update({name: big_sum[name].reshape(w[name].shape) for name in BIG_NAMES}, dn_conv_w=conv_sum[None])

    delta, new_m, new_v = {}, {}, {}
    for n in BIG:
        delta[n], new_m[n], new_v[n] = _adam_all(w[n], grads[n], m[n], v[n], "adamw_" + n)
    shapes = {n: w[n].shape for n, _ in SMALL}
    d_s, m_s, v_s = adamw(pack_small(w), small_sum, pack_small(m), pack_small(v), name="adamw_small")
    for out, packed in ((delta, d_s), (new_m, m_s), (new_v, v_s)):
        out.update(unpack_small(packed, shapes))
    return (loss, dx[None], *[grads[n] for n in WEIGHTS], *[delta[n] for n in WEIGHTS],
            *[new_m[n] for n in WEIGHTS], *[new_v[n] for n in WEIGHTS])
```

```python
import functools

import jax
import jax.numpy as jnp
from jax import lax
from jax.experimental import pallas as pl
from jax.experimental.pallas import tpu as pltpu

F32 = jnp.float32
MM = jnp.bfloat16
HI = lax.Precision.HIGHEST

D_MODEL = 1024
HEAD_DIM = 128
N_HEADS = 8
MEM_HEADS = 4
MEM_WIDTH = MEM_HEADS * HEAD_DIM
N_MEM = 256
D_FF = 4 * D_MODEL
CHUNK = 64
EPS = 1e-6
QSCALE = HEAD_DIM ** -0.5
PROJ_W = 4736
TAIL = 4608
TAIL_BLK = TAIL // HEAD_DIM
ROWS = 256
VMEM_LIMIT = 56 * 1024 * 1024

ADAM_LR = 0.001
ADAM_B1 = 0.9
ADAM_B2 = 0.999
ADAM_EPS = 1e-08
ADAM_WD = 0.01
ADAM_STEP = 10

N_DEV = 8
N_CHIP = 4
MESH = pl.DeviceIdType.MESH


def _cparams(sem=None):
    return pltpu.CompilerParams(dimension_semantics=sem, vmem_limit_bytes=VMEM_LIMIT)


def _dot(a, b, ca, cb, hi):
    dims = (((ca,), (cb,)), ((), ()))
    if hi:
        return lax.dot_general(a, b, dims, precision=HI, preferred_element_type=F32)
    return lax.dot_general(a.astype(MM), b.astype(MM), dims, preferred_element_type=F32)


@functools.partial(jax.custom_vjp, nondiff_argnums=(2, 3, 4))
def mmul(a, b, ca, cb, hi):
    return _dot(a, b, ca, cb, hi)


def _mmul_fwd(a, b, ca, cb, hi):
    return _dot(a, b, ca, cb, hi), (a, b)


def _mmul_bwd(ca, cb, hi, res, g):
    a, b = res
    if ca == 1:
        da = _dot(g, b, 1, 1, hi) if cb == 0 else _dot(g, b, 1, 0, hi)
    else:
        da = _dot(b, g, 1, 1, hi) if cb == 0 else _dot(b, g, 0, 1, hi)
    if cb == 0:
        db = _dot(a, g, 0, 0, hi) if ca == 1 else _dot(a, g, 1, 0, hi)
    else:
        db = _dot(g, a, 0, 0, hi) if ca == 1 else _dot(g, a, 0, 1, hi)
    return da.astype(a.dtype), db.astype(b.dtype)


mmul.defvjp(_mmul_fwd, _mmul_bwd)


def _iota2(n, m):
    return lax.broadcasted_iota(jnp.int32, (n, m), 0), lax.broadcasted_iota(jnp.int32, (n, m), 1)


def _same_block(r, c, shift):
    return lax.shift_right_logical(r, shift) == lax.shift_right_logical(c, shift)


def _split_bf16(x):
    hi = x.astype(jnp.bfloat16)
    return hi, (x - hi.astype(F32)).astype(jnp.bfloat16)


def _dot3(a, b, ca, cb):
    dims = (((ca,), (cb,)), ((), ()))
    (ah, al), (bh, bl) = _split_bf16(a), _split_bf16(b)
    d = lambda x, y: lax.dot_general(x, y, dims, preferred_element_type=F32)
    return d(ah, bh) + (d(ah, bl) + d(al, bh))


def _tri_inv_impl(a):
    n = a.shape[0]
    r, c = _iota2(n, n)
    eye = (r == c).astype(F32)
    b16, b32 = _same_block(r, c, 4), _same_block(r, c, 5)
    a0 = jnp.where(b16, a, 0.0)
    p = eye - a0
    b = _dot3(a0, a0, 1, 0)
    p = p + _dot3(p, b, 1, 0)
    b = _dot3(b, b, 1, 0)
    p = p + _dot3(p, b, 1, 0)
    b = _dot3(b, b, 1, 0)
    p = p + _dot3(p, b, 1, 0)
    a1 = jnp.where(jnp.logical_and(b32, jnp.logical_not(b16)), a, 0.0)
    p = p - _dot3(_dot3(p, a1, 1, 0), p, 1, 0)
    a2 = jnp.where(b32, 0.0, a)
    p = p - _dot3(_dot3(p, a2, 1, 0), p, 1, 0)
    return p


@jax.custom_vjp
def tri_inv(a):
    return _tri_inv_impl(a)


def _tri_inv_fwd(a):
    p = _tri_inv_impl(a)
    return p, p


def _tri_inv_bwd(p, g):
    return (-_dot3(_dot3(p, g, 0, 0), p, 1, 1),)


tri_inv.defvjp(_tri_inv_fwd, _tri_inv_bwd)


def _sigmoid(x):
    return 1.0 / (1.0 + jnp.exp(-x))


def _softplus(x):
    return jnp.maximum(x, 0.0) + jnp.log(1.0 + jnp.exp(-jnp.abs(x)))


def _silu(x):
    return x * _sigmoid(x)


def _rms(x, w):
    return x * lax.rsqrt(jnp.mean(x * x, axis=-1, keepdims=True) + EPS) * w


def _bf_round(x):
    return x.astype(MM).astype(F32)


def _acc(ref, val, first):
    @pl.when(first)
    def _():
        ref[...] = val

    @pl.when(jnp.logical_not(first))
    def _():
        ref[...] += val


def _tile(n, pref):
    if n % pref == 0:
        return pref
    return n


def matmul(a, b, *, ta=False, tb=False, b_slots=False, res=None, also_sqrelu=False, times_dsqrelu=None, out_dtype=F32,
           name, tm=1024, tn=1024, tk=1024):
    m, k = (a.shape[1], a.shape[0]) if ta else a.shape
    if b_slots:
        n = b.shape[1] if tb else N_CHIP * b.shape[2]
        assert (N_CHIP * b.shape[2] if tb else b.shape[1]) == k, (a.shape, b.shape, ta, tb)
        tn, tk = (tn, b.shape[2]) if tb else (b.shape[2], tk)
    else:
        n = b.shape[0] if tb else b.shape[1]
        assert (b.shape[1] if tb else b.shape[0]) == k, (a.shape, b.shape, ta, tb)
    tm, tn, tk = _tile(m, tm), _tile(n, tn), _tile(k, tk)
    nk = k // tk
    ca, cb = (0 if ta else 1), (1 if tb else 0)

    extra = tuple(e for e in (res, times_dsqrelu) if e is not None)
    assert len(extra) <= 1

    def body(a_ref, b_ref, *rest):
        e_ref = rest[0] if extra else None
        o_ref, acc_ref = rest[len(extra)], rest[-1]
        kk = pl.program_id(2)
        part = _dot(a_ref[...], b_ref[...], ca, cb, False)

        @pl.when(kk == 0)
        def _():
            acc_ref[...] = part

        @pl.when(kk > 0)
        def _():
            acc_ref[...] += part

        @pl.when(kk == nk - 1)
        def _():
            total = acc_ref[...]
            if res is not None:
                total = total + e_ref[...]
            if times_dsqrelu is not None:
                total = total * (2.0 * jnp.maximum(e_ref[...], 0.0))
            o_ref[...] = total.astype(o_ref.dtype)
            if also_sqrelu:
                rest[len(extra) + 1][...] = _sqrelu(total).astype(MM)

    a_spec = pl.BlockSpec((tk, tm), lambda i, j, l: (l, i)) if ta else pl.BlockSpec((tm, tk), lambda i, j, l: (i, l))
    if b_slots:
        b_spec = (pl.BlockSpec((None, tn, tk), lambda i, j, l: (l, j, 0)) if tb else
                  pl.BlockSpec((None, tk, tn), lambda i, j, l: (j, l, 0)))
    else:
        b_spec = pl.BlockSpec((tn, tk), lambda i, j, l: (j, l)) if tb else pl.BlockSpec((tk, tn), lambda i, j, l: (l, j))
    o_spec = pl.BlockSpec((tm, tn), lambda i, j, l: (i, j))
    out_shape = [jax.ShapeDtypeStruct((m, n), out_dtype)] + [jax.ShapeDtypeStruct((m, n), MM)] * also_sqrelu
    outs = pl.pallas_call(
        body, name=name, grid=(m // tm, n // tn, nk),
        in_specs=[a_spec, b_spec] + [o_spec] * len(extra), out_specs=[o_spec] * len(out_shape), out_shape=out_shape,
        scratch_shapes=[pltpu.VMEM((tm, tn), F32)],
        compiler_params=_cparams(("parallel", "parallel", "arbitrary")),
    )(a, b, *extra)
    return outs if also_sqrelu else outs[0]


def rms_fwd(x, w, *, name):
    t, d = x.shape

    def body(x_ref, w_ref, o_ref):
        o_ref[...] = _rms(x_ref[...], w_ref[...]).astype(o_ref.dtype)

    return pl.pallas_call(
        body, name=name, grid=(t // ROWS,),
        in_specs=[pl.BlockSpec((ROWS, d), lambda i: (i, 0)), pl.BlockSpec((1, d), lambda i: (0, 0))],
        out_specs=pl.BlockSpec((ROWS, d), lambda i: (i, 0)),
        out_shape=jax.ShapeDtypeStruct((t, d), MM), compiler_params=_cparams(("parallel",)),
    )(x, w)


def rms_bwd(x, w, dh, dres, *, name):
    t, d = x.shape

    def body(x_ref, w_ref, dh_ref, dr_ref, dx_ref, dw_ref):
        _, vjp = jax.vjp(_rms, x_ref[...], w_ref[...])
        dx, dw = vjp(dh_ref[...].astype(F32))
        dx_ref[...] = dx + dr_ref[...]
        _acc(dw_ref, dw, pl.program_id(0) == 0)

    row = pl.BlockSpec((ROWS, d), lambda i: (i, 0))
    vec = pl.BlockSpec((1, d), lambda i: (0, 0))
    return pl.pallas_call(
        body, name=name, grid=(t // ROWS,), in_specs=[row, vec, row, row], out_specs=[row, vec],
        out_shape=[jax.ShapeDtypeStruct((t, d), F32), jax.ShapeDtypeStruct((1, d), F32)],
        compiler_params=_cparams(("arbitrary",)),
    )(x, w, dh, dres)


def _sqrelu(x):
    return jnp.square(jnp.maximum(x, 0.0))


def loss_fwd(y, target, *, name):
    t, d = y.shape

    def body(y_ref, t_ref, dy_ref, l_ref):
        e = y_ref[...] - t_ref[...]
        dy_ref[...] = e * (1.0 / d)
        part = 0.5 * jnp.sum(jnp.sum(e * e, axis=-1, keepdims=True) * (1.0 / d), axis=0, keepdims=True)
        _acc(l_ref, jnp.broadcast_to(part, (1, HEAD_DIM)), pl.program_id(0) == 0)

    blk = pl.BlockSpec((ROWS, d), lambda i: (i, 0))
    return pl.pallas_call(
        body, name=name, grid=(t // ROWS,), in_specs=[blk, blk],
        out_specs=[blk, pl.BlockSpec((1, HEAD_DIM), lambda i: (0, 0))],
        out_shape=[jax.ShapeDtypeStruct((t, d), F32), jax.ShapeDtypeStruct((1, HEAD_DIM), F32)],
        compiler_params=_cparams(("arbitrary",)),
    )(y, target)


def _mem_kv(mem, wn, wkn, *ws):
    mn = _rms(mem, wn)
    outs = []
    for h in range(MEM_HEADS):
        outs.append(_rms(mmul(mn, ws[h], 1, 0, False), wkn))
    for h in range(MEM_HEADS):
        outs.append(mmul(mn, ws[MEM_HEADS + h], 1, 0, False))
    return tuple(outs)


def _w_cols(w_ref):
    return [w_ref[:, h * HEAD_DIM:(h + 1) * HEAD_DIM] for h in range(2 * MEM_HEADS)]


def mem_fwd(mem, wn, wkv, wkn):
    def body(mem_ref, wn_ref, w_ref, wkn_ref, k_ref, v_ref):
        outs = _mem_kv(mem_ref[...], wn_ref[...], wkn_ref[...], *_w_cols(w_ref))
        for h in range(MEM_HEADS):
            k_ref[:, h * HEAD_DIM:(h + 1) * HEAD_DIM] = outs[h]
            v_ref[:, h * HEAD_DIM:(h + 1) * HEAD_DIM] = outs[MEM_HEADS + h]

    shp = jax.ShapeDtypeStruct((mem.shape[0], MEM_WIDTH), F32)
    return pl.pallas_call(body, name="mem_fwd", out_shape=[shp, shp], compiler_params=_cparams())(mem, wn, wkv, wkn)


def mem_bwd(mem, wn, wkv, wkn, dk0, dv0, dk1, dv1):
    def body(mem_ref, wn_ref, w_ref, wkn_ref, dk0_ref, dv0_ref, dk1_ref, dv1_ref, dwn_ref, dw_ref, dwkn_ref):
        _, vjp = jax.vjp(lambda wn_, wkn_, *ws: _mem_kv(mem_ref[...], wn_, wkn_, *ws),
                         wn_ref[...], wkn_ref[...], *[w.astype(F32) for w in _w_cols(w_ref)])
        cols = lambda a, b: tuple(a[:, h * HEAD_DIM:(h + 1) * HEAD_DIM] + b[:, h * HEAD_DIM:(h + 1) * HEAD_DIM]
                                  for h in range(MEM_HEADS))
        cts = cols(dk0_ref, dk1_ref) + cols(dv0_ref, dv1_ref)
        grads = vjp(cts)
        dwn_ref[...] = grads[0]
        dwkn_ref[...] = grads[1]
        for h in range(2 * MEM_HEADS):
            dw_ref[:, h * HEAD_DIM:(h + 1) * HEAD_DIM] = grads[2 + h]

    return pl.pallas_call(
        body, name="mem_bwd",
        out_shape=[jax.ShapeDtypeStruct((1, D_MODEL), F32), jax.ShapeDtypeStruct((D_MODEL, 2 * MEM_WIDTH), F32),
                   jax.ShapeDtypeStruct((1, HEAD_DIM), F32)],
        compiler_params=_cparams(),
    )(mem, wn, wkv, wkn, dk0, dv0, dk1, dv1)


def _memattn(q, wq, mk, mv):
    qn = _rms(q, wq) * QSCALE
    s = mmul(qn, mk, 1, 1, False)
    s = s - jnp.max(s, axis=-1, keepdims=True)
    p = jnp.exp(s)
    p = p / jnp.sum(p, axis=-1, keepdims=True)
    return mmul(p, mv, 1, 0, False)


def _lanes(j):
    return slice(j * HEAD_DIM, (j + 1) * HEAD_DIM)


def _memattn_specs(t):
    qspec = pl.BlockSpec((ROWS, MEM_WIDTH), lambda i: (i, (TAIL - MEM_WIDTH) // MEM_WIDTH))
    wspec = pl.BlockSpec((1, HEAD_DIM), lambda i: (0, 0))
    mspec = pl.BlockSpec((N_MEM, MEM_WIDTH), lambda i: (0, 0))
    ospec = pl.BlockSpec((ROWS, MEM_WIDTH), lambda i: (i, 0))
    return qspec, wspec, mspec, ospec


def memattn_fwd(proj, wq, mk, mv, *, name):
    t = proj.shape[0]
    qspec, wspec, mspec, ospec = _memattn_specs(t)

    def body(q_ref, w_ref, k_ref, v_ref, o_ref):
        for h in range(MEM_HEADS):
            o_ref[:, _lanes(h)] = _memattn(q_ref[:, _lanes(h)], w_ref[...], k_ref[:, _lanes(h)],
                                           v_ref[:, _lanes(h)]).astype(o_ref.dtype)

    return pl.pallas_call(
        body, name=name, grid=(t // ROWS,), in_specs=[qspec, wspec, mspec, mspec], out_specs=ospec,
        out_shape=jax.ShapeDtypeStruct((t, MEM_WIDTH), MM), compiler_params=_cparams(("parallel",)),
    )(proj, wq, mk, mv)


def memattn_bwd(proj, wq, mk, mv, dcat, *, name):
    t = proj.shape[0]
    qspec, wspec, mspec, ospec = _memattn_specs(t)
    dospec = pl.BlockSpec((ROWS, MEM_WIDTH), lambda i: (i, D_MODEL // MEM_WIDTH))

    def body(q_ref, w_ref, k_ref, v_ref, do_ref, dq_ref, dw_ref, dk_ref, dv_ref):
        first = pl.program_id(0) == 0
        dw_sum = jnp.zeros((1, HEAD_DIM), F32)
        for h in range(MEM_HEADS):
            _, vjp = jax.vjp(_memattn, q_ref[:, _lanes(h)], w_ref[...], k_ref[:, _lanes(h)], v_ref[:, _lanes(h)])
            dq, dw, dk, dv = vjp(do_ref[:, _lanes(h)].astype(F32))
            dq_ref[:, _lanes(h)] = dq.astype(dq_ref.dtype)
            dw_sum = dw_sum + dw
            _acc(dk_ref.at[:, _lanes(h)], dk, first)
            _acc(dv_ref.at[:, _lanes(h)], dv, first)
        _acc(dw_ref, dw_sum, first)

    mshape = jax.ShapeDtypeStruct((N_MEM, MEM_WIDTH), F32)
    return pl.pallas_call(
        body, name=name, grid=(t // ROWS,), in_specs=[qspec, wspec, mspec, mspec, dospec],
        out_specs=[ospec, wspec, mspec, mspec],
        out_shape=[jax.ShapeDtypeStruct((t, MEM_WIDTH), MM), jax.ShapeDtypeStruct((1, HEAD_DIM), F32), mshape, mshape],
        compiler_params=_cparams(("arbitrary",)),
    )(proj, wq, mk, mv, dcat)


def _shift_rows(x, s, up):
    n = x.shape[0]
    r = lax.broadcasted_iota(jnp.int32, x.shape, 0)
    if up:
        return jnp.where(r < n - s, pltpu.roll(x, n - s, 0), 0.0)
    return jnp.where(r >= s, pltpu.roll(x, s, 0), 0.0)


def _conv_fwd_vals(x, w):
    xb = _bf_round(x)
    wb = _bf_round(w)
    c = xb * wb[3:4, :]
    for j in range(3):
        c = c + _shift_rows(xb, 3 - j, False) * wb[j:j + 1, :]
    return xb, wb, c


def dn_prep_fwd(proj, conv_w):
    t = proj.shape[0]

    def body(x_ref, w_ref, o_ref):
        j = pl.program_id(0)
        _, _, c = _conv_fwd_vals(x_ref[...], w_ref[...])
        s = _silu(c)
        r = lax.rsqrt(jnp.sum(s * s, axis=-1, keepdims=True) + EPS)
        scale = jnp.where(j < N_HEADS, QSCALE, 1.0)
        o_ref[...] = jnp.where(j < 2 * N_HEADS, s * r * scale, s)

    return pl.pallas_call(
        body, name="dn_prep_fwd", grid=(3 * N_HEADS,),
        in_specs=[pl.BlockSpec((t, HEAD_DIM), lambda j: (0, j)), pl.BlockSpec((4, HEAD_DIM), lambda j: (0, j))],
        out_specs=pl.BlockSpec((None, t, HEAD_DIM), lambda j: (j // N_HEADS, 0, j % N_HEADS)),
        out_shape=jax.ShapeDtypeStruct((3, t, D_MODEL), F32), compiler_params=_cparams(("parallel",)),
    )(proj, conv_w)


def dn_prep_bwd(proj, conv_w, dqkv):
    t = proj.shape[0]

    def body(x_ref, w_ref, g_ref, dx_ref, dw_ref):
        j = pl.program_id(0)
        xb, wb, c = _conv_fwd_vals(x_ref[...], w_ref[...])
        sg = _sigmoid(c)
        s = c * sg
        g = g_ref[...]
        r = lax.rsqrt(jnp.sum(s * s, axis=-1, keepdims=True) + EPS)
        scale = jnp.where(j < N_HEADS, QSCALE, 1.0)
        gn = g * scale
        ds_norm = r * gn - s * (r * r * r) * jnp.sum(gn * s, axis=-1, keepdims=True)
        ds = jnp.where(j < 2 * N_HEADS, ds_norm, g)
        dc = ds * (sg + s * (1.0 - sg))
        dx = dc * wb[3:4, :]
        rows = [jnp.sum(dc * xb, axis=0, keepdims=True)]
        for jj in range(2, -1, -1):
            sh = 3 - jj
            dx = dx + _shift_rows(dc, sh, True) * wb[jj:jj + 1, :]
            rows.insert(0, jnp.sum(dc * _shift_rows(xb, sh, False), axis=0, keepdims=True))
        dx_ref[...] = dx.astype(dx_ref.dtype)
        dw_ref[...] = jnp.concatenate(rows + [jnp.zeros((4, HEAD_DIM), F32)], axis=0)

    col = pl.BlockSpec((t, HEAD_DIM), lambda j: (0, j))
    return pl.pallas_call(
        body, name="dn_prep_bwd", grid=(3 * N_HEADS,),
        in_specs=[col, pl.BlockSpec((4, HEAD_DIM), lambda j: (0, j)),
                  pl.BlockSpec((None, t, HEAD_DIM), lambda j: (j // N_HEADS, 0, j % N_HEADS))],
        out_specs=[col, pl.BlockSpec((8, HEAD_DIM), lambda j: (0, j))],
        out_shape=[jax.ShapeDtypeStruct((t, 3 * D_MODEL), MM), jax.ShapeDtypeStruct((8, 3 * D_MODEL), F32)],
        compiler_params=_cparams(("parallel",)),
    )(proj, conv_w, dqkv)


def _tri_ones(n, upper):
    r, c = _iota2(n, n)
    return (r <= c).astype(F32) if upper else (r >= c).astype(F32)


def dn_gates_fwd(proj, a_log, dt_bias):
    t = proj.shape[0]

    def body(x_ref, al_ref, dt_ref, o_ref):
        lane = lax.broadcasted_iota(jnp.int32, (CHUNK, HEAD_DIM), 1)
        tri = _tri_ones(CHUNK, False)

        def step(c, carry):
            rows = pl.ds(pl.multiple_of(c * CHUNK, CHUNK), CHUNK)
            x = x_ref[rows, :]
            g = jnp.where(lane < N_HEADS, -jnp.exp(al_ref[...]) * _softplus(x + dt_ref[...]), 0.0)
            gc = _dot(tri, g, 1, 0, True)
            o_ref[rows, :] = jnp.where(lane < N_HEADS, gc, jnp.where(lane < 2 * N_HEADS, _sigmoid(x), 0.0))
            return carry

        lax.fori_loop(0, t // CHUNK, step, 0)

    vec = pl.BlockSpec((1, HEAD_DIM), lambda i: (0, 0))
    return pl.pallas_call(
        body, name="dn_gates_fwd", grid=(1,),
        in_specs=[pl.BlockSpec((t, HEAD_DIM), lambda i: (0, TAIL_BLK)), vec, vec],
        out_specs=pl.BlockSpec((t, HEAD_DIM), lambda i: (0, 0)),
        out_shape=jax.ShapeDtypeStruct((t, HEAD_DIM), F32), compiler_params=_cparams(("arbitrary",)),
    )(proj, a_log, dt_bias)


def dn_gates_bwd(proj, a_log, dt_bias, dgates):
    t = proj.shape[0]

    def body(x_ref, al_ref, dt_ref, g_ref, dx_ref, dal_ref, ddt_ref):
        lane = lax.broadcasted_iota(jnp.int32, (CHUNK, HEAD_DIM), 1)
        tri = _tri_ones(CHUNK, True)
        dal_ref[...] = jnp.zeros_like(dal_ref)
        ddt_ref[...] = jnp.zeros_like(ddt_ref)

        def step(c, carry):
            rows = pl.ds(pl.multiple_of(c * CHUNK, CHUNK), CHUNK)
            x = x_ref[rows, :]
            dgc = jnp.where(lane < N_HEADS, g_ref[rows, :], 0.0)
            dg = _dot(tri, dgc, 1, 0, True)
            ea = -jnp.exp(al_ref[...])
            z = x + dt_ref[...]
            da = jnp.where(lane < N_HEADS, dg * ea * _sigmoid(z), 0.0)
            gval = jnp.where(lane < N_HEADS, ea * _softplus(z), 0.0)
            beta = _sigmoid(x)
            db = jnp.where(jnp.logical_and(lane >= N_HEADS, lane < 2 * N_HEADS), g_ref[rows, :] * beta * (1.0 - beta), 0.0)
            dx_ref[rows, :] = (da + db).astype(dx_ref.dtype)
            dal_ref[...] += jnp.sum(dg * gval, axis=0, keepdims=True)
            ddt_ref[...] += jnp.sum(da, axis=0, keepdims=True)
            return carry

        lax.fori_loop(0, t // CHUNK, step, 0)

    vec = pl.BlockSpec((1, HEAD_DIM), lambda i: (0, 0))
    full = pl.BlockSpec((t, HEAD_DIM), lambda i: (0, 0))
    return pl.pallas_call(
        body, name="dn_gates_bwd", grid=(1,),
        in_specs=[pl.BlockSpec((t, HEAD_DIM), lambda i: (0, TAIL_BLK)), vec, vec, full],
        out_specs=[full, vec, vec],
        out_shape=[jax.ShapeDtypeStruct((t, HEAD_DIM), MM), jax.ShapeDtypeStruct((1, HEAD_DIM), F32),
                   jax.ShapeDtypeStruct((1, HEAD_DIM), F32)],
        compiler_params=_cparams(("arbitrary",)),
    )(proj, a_log, dt_bias, dgates)


def _dn_intra(q, k, v, gcol, grow, bcol):
    r, c = _iota2(CHUNK, CHUNK)
    causal, strict = r >= c, r > c
    decay = jnp.where(causal, jnp.exp(jnp.where(causal, gcol - grow, 0.0)), 0.0)
    kb = k * bcol
    a = jnp.where(strict, mmul(kb, k, 1, 1, False) * decay, 0.0)
    tm = tri_inv(a)
    u = mmul(tm, v * bcol, 1, 0, False)
    w = mmul(tm, kb * jnp.exp(gcol), 1, 0, False)
    qk = jnp.where(causal, mmul(q, k, 1, 1, False) * decay, 0.0)
    rr = lax.broadcasted_iota(jnp.int32, (CHUNK, 1), 0)
    g_last = jnp.sum(jnp.where(rr == CHUNK - 1, gcol, 0.0), axis=0, keepdims=True)
    return u, w, q * jnp.exp(gcol), k * jnp.exp(g_last - gcol), qk, jnp.exp(g_last)


def _dn_scan(u, w, qg, kd, qk, eg, state):
    v_new = u - mmul(w, state, 1, 0, False)
    out = mmul(qg, state, 1, 0, False) + mmul(qk, v_new, 1, 0, False)
    return out, state * eg + mmul(kd, v_new, 0, 0, False)


DN_HEADS_PER_STEP = 1
DN_GROUP = 8
DN_PARTS = ((CHUNK, HEAD_DIM),) * 4 + ((CHUNK, CHUNK), (1, 1))


def _dn_scratch(hb, nc):
    return [pltpu.VMEM((hb, nc) + shape, F32) for shape in DN_PARTS]


def _dn_group(nc):
    return min(DN_GROUP, nc)


def _dn_group_args(refs, j, g, grp):
    q_ref, k_ref, v_ref, gc_ref, gr_ref, bc_ref = refs
    rows = pl.ds(pl.multiple_of(g * (grp * CHUNK), grp * CHUNK), grp * CHUNK)
    cs = pl.ds(g * grp, grp)
    split = lambda ref: ref[rows, _lanes(j)].reshape(grp, CHUNK, HEAD_DIM)
    return split(q_ref), split(k_ref), split(v_ref), gc_ref[j, cs], gr_ref[j, cs], bc_ref[j, cs]


def _dn_intra_all(refs, parts, hb, nc):
    grp = _dn_group(nc)

    def group(g, carry):
        cs = pl.ds(g * grp, grp)
        for j in range(hb):
            for part, val in zip(parts, jax.vmap(_dn_intra)(*_dn_group_args(refs, j, g, grp))):
                part[j, cs] = val
        return carry

    lax.fori_loop(0, nc // grp, group, 0)


def _dn_specs(t):
    nc, hb = t // CHUNK, DN_HEADS_PER_STEP
    head = lambda which: pl.BlockSpec((None, t, hb * HEAD_DIM), lambda h: (which, 0, h))
    flat = pl.BlockSpec((t, hb * HEAD_DIM), lambda h: (0, h))
    col = pl.BlockSpec((hb, nc, CHUNK, 1), lambda h: (h, 0, 0, 0))
    row = pl.BlockSpec((hb, nc, 1, CHUNK), lambda h: (h, 0, 0, 0))
    st = pl.BlockSpec((hb, nc, HEAD_DIM, HEAD_DIM), lambda h: (h, 0, 0, 0))
    return nc, hb, head, flat, col, row, st


def dn_core_fwd(qkv, gcol, grow, bcol):
    t = qkv.shape[1]
    nc, hb, head, flat, col, row, st = _dn_specs(t)

    def body(q_ref, k_ref, v_ref, gc_ref, gr_ref, bc_ref, o_ref, s_ref, *parts):
        _dn_intra_all((q_ref, k_ref, v_ref, gc_ref, gr_ref, bc_ref), parts, hb, nc)

        def step(c, states):
            rows = pl.ds(pl.multiple_of(c * CHUNK, CHUNK), CHUNK)
            new_states = []
            for j in range(hb):
                s_ref[j, c] = states[j]
                out, new_state = _dn_scan(*[part[j, c] for part in parts], states[j])
                o_ref[rows, _lanes(j)] = out
                new_states.append(new_state)
            return tuple(new_states)

        lax.fori_loop(0, nc, step, tuple(jnp.zeros((HEAD_DIM, HEAD_DIM), F32) for _ in range(hb)))

    return pl.pallas_call(
        body, name="dn_core_fwd", grid=(N_HEADS // hb,),
        in_specs=[head(0), head(1), head(2), col, row, col], out_specs=[flat, st],
        out_shape=[jax.ShapeDtypeStruct((t, D_MODEL), F32), jax.ShapeDtypeStruct((N_HEADS, nc, HEAD_DIM, HEAD_DIM), F32)],
        scratch_shapes=_dn_scratch(hb, nc), compiler_params=_cparams(("parallel",)),
    )(qkv, qkv, qkv, gcol, grow, bcol)


def dn_core_bwd(qkv, gcol, grow, bcol, states, do):
    t = qkv.shape[1]
    nc, hb, head, flat, col, row, st = _dn_specs(t)

    def body(q_ref, k_ref, v_ref, gc_ref, gr_ref, bc_ref, s_ref, do_ref, dqkv_ref, dgc_ref, dgr_ref, dbc_ref, *scratch):
        parts, dparts = scratch[:len(DN_PARTS)], scratch[len(DN_PARTS):]
        refs = (q_ref, k_ref, v_ref, gc_ref, gr_ref, bc_ref)
        _dn_intra_all(refs, parts, hb, nc)

        def step(i, dstates):
            c = nc - 1 - i
            rows = pl.ds(pl.multiple_of(c * CHUNK, CHUNK), CHUNK)
            dstates_in = []
            for j in range(hb):
                _, vjp = jax.vjp(_dn_scan, *[part[j, c] for part in parts], s_ref[j, c])
                *dvals, dstate_in = vjp((do_ref[rows, _lanes(j)], dstates[j]))
                for dpart, dval in zip(dparts, dvals):
                    dpart[j, c] = dval
                dstates_in.append(dstate_in)
            return tuple(dstates_in)

        lax.fori_loop(0, nc, step, tuple(jnp.zeros((HEAD_DIM, HEAD_DIM), F32) for _ in range(hb)))

        grp = _dn_group(nc)

        def group(g, carry):
            rows = pl.ds(pl.multiple_of(g * (grp * CHUNK), grp * CHUNK), grp * CHUNK)
            cs = pl.ds(g * grp, grp)
            for j in range(hb):
                _, vjp = jax.vjp(jax.vmap(_dn_intra), *_dn_group_args(refs, j, g, grp))
                dq, dk, dv, dgc, dgr, dbc = vjp(tuple(dpart[j, cs] for dpart in dparts))
                for which, val in enumerate((dq, dk, dv)):
                    dqkv_ref[which, rows, _lanes(j)] = val.reshape(grp * CHUNK, HEAD_DIM)
                dgc_ref[j, cs] = dgc
                dgr_ref[j, cs] = dgr
                dbc_ref[j, cs] = dbc
            return carry

        lax.fori_loop(0, nc // grp, group, 0)

    return pl.pallas_call(
        body, name="dn_core_bwd", grid=(N_HEADS // hb,), scratch_shapes=_dn_scratch(hb, nc) * 2,
        in_specs=[head(0), head(1), head(2), col, row, col, st, flat],
        out_specs=[pl.BlockSpec((3, t, hb * HEAD_DIM), lambda h: (0, 0, h)), col, row, col],
        out_shape=[jax.ShapeDtypeStruct((3, t, D_MODEL), F32)] + [
            jax.ShapeDtypeStruct((N_HEADS, nc, CHUNK, 1), F32), jax.ShapeDtypeStruct((N_HEADS, nc, 1, CHUNK), F32),
            jax.ShapeDtypeStruct((N_HEADS, nc, CHUNK, 1), F32)],
        compiler_params=_cparams(("parallel",)),
    )(qkv, qkv, qkv, gcol, grow, bcol, states, do)


def gates_to_heads(gates):
    t = gates.shape[0]
    nc = t // CHUNK
    g = gates[:, :N_HEADS].T.reshape(N_HEADS, nc, CHUNK)
    b = gates[:, N_HEADS:2 * N_HEADS].T.reshape(N_HEADS, nc, CHUNK)
    return g[..., None], g[:, :, None, :], b[..., None]


def heads_to_gates(dgcol, dgrow, dbcol):
    nh, nc = dgcol.shape[:2]
    dg = (dgcol[..., 0] + dgrow[:, :, 0, :]).reshape(nh, nc * CHUNK).T
    db = dbcol[..., 0].reshape(nh, nc * CHUNK).T
    return jnp.concatenate([dg, db, jnp.zeros((nc * CHUNK, HEAD_DIM - 2 * nh), F32)], axis=1)


def _dn_out(o, z, w):
    return _rms(o, w) * _silu(z)


def _gate_specs():
    o_spec = pl.BlockSpec((ROWS, D_MODEL), lambda i: (i, 0))
    z_spec = pl.BlockSpec((ROWS, D_MODEL), lambda i: (i, 3))
    w_spec = pl.BlockSpec((1, HEAD_DIM), lambda i: (0, 0))
    return o_spec, z_spec, w_spec


def dn_out_fwd(o, proj, w):
    t = o.shape[0]
    o_spec, z_spec, w_spec = _gate_specs()

    def body(o_ref, z_ref, w_ref, y_ref):
        for h in range(N_HEADS):
            y_ref[:, _lanes(h)] = _dn_out(o_ref[:, _lanes(h)], z_ref[:, _lanes(h)], w_ref[...]).astype(y_ref.dtype)

    return pl.pallas_call(
        body, name="dn_out_fwd", grid=(t // ROWS,), in_specs=[o_spec, z_spec, w_spec], out_specs=o_spec,
        out_shape=jax.ShapeDtypeStruct((t, D_MODEL), MM), compiler_params=_cparams(("parallel",)),
    )(o, proj, w)


def dn_out_bwd(o, proj, w, dcat):
    t = o.shape[0]
    o_spec, z_spec, w_spec = _gate_specs()

    def body(o_ref, z_ref, w_ref, g_ref, do_ref, dz_ref, dw_ref):
        dw_sum = jnp.zeros((1, HEAD_DIM), F32)
        for h in range(N_HEADS):
            _, vjp = jax.vjp(_dn_out, o_ref[:, _lanes(h)], z_ref[:, _lanes(h)], w_ref[...])
            do, dz, dw = vjp(g_ref[:, _lanes(h)].astype(F32))
            do_ref[:, _lanes(h)] = do
            dz_ref[:, _lanes(h)] = dz.astype(dz_ref.dtype)
            dw_sum = dw_sum + dw
        _acc(dw_ref, dw_sum, pl.program_id(0) == 0)

    return pl.pallas_call(
        body, name="dn_out_bwd", grid=(t // ROWS,), in_specs=[o_spec, z_spec, w_spec, o_spec],
        out_specs=[o_spec, o_spec, w_spec],
        out_shape=[jax.ShapeDtypeStruct((t, D_MODEL), F32), jax.ShapeDtypeStruct((t, D_MODEL), MM),
                   jax.ShapeDtypeStruct((1, HEAD_DIM), F32)],
        compiler_params=_cparams(("arbitrary",)),
    )(o, proj, w, dcat)


def _fox_norm(x, w, scale):
    return _rms(x, w) * scale


def _fox_prep_specs():
    x_spec = pl.BlockSpec((ROWS, 2 * D_MODEL), lambda i: (i, 0))
    w_spec = pl.BlockSpec((2, 1, HEAD_DIM), lambda i: (0, 0, 0))
    y_spec = pl.BlockSpec((2, ROWS, D_MODEL), lambda i: (0, i, 0))
    return x_spec, w_spec, y_spec


def fox_prep_fwd(proj, wqk):
    t = proj.shape[0]
    x_spec, w_spec, y_spec = _fox_prep_specs()

    def body(x_ref, w_ref, y_ref):
        for j in range(2 * N_HEADS):
            which, scale = j // N_HEADS, (QSCALE if j < N_HEADS else 1.0)
            y_ref[which, :, _lanes(j % N_HEADS)] = _fox_norm(x_ref[:, _lanes(j)], w_ref[which], scale).astype(y_ref.dtype)

    return pl.pallas_call(
        body, name="fox_prep_fwd", grid=(t // ROWS,), in_specs=[x_spec, w_spec], out_specs=y_spec,
        out_shape=jax.ShapeDtypeStruct((2, t, D_MODEL), MM), compiler_params=_cparams(("parallel",)),
    )(proj, wqk)


def fox_prep_bwd(proj, wqk, dq, dk):
    t = proj.shape[0]
    x_spec, w_spec, _ = _fox_prep_specs()
    g_spec = pl.BlockSpec((ROWS, D_MODEL), lambda i: (i, 0))

    def body(x_ref, w_ref, dq_ref, dk_ref, dx_ref, dw_ref):
        dws = [jnp.zeros((1, HEAD_DIM), F32), jnp.zeros((1, HEAD_DIM), F32)]
        for j in range(2 * N_HEADS):
            which, scale = j // N_HEADS, (QSCALE if j < N_HEADS else 1.0)
            g_ref = dq_ref if which == 0 else dk_ref
            _, vjp = jax.vjp(lambda x, w: _fox_norm(x, w, scale), x_ref[:, _lanes(j)], w_ref[which])
            dx, dw = vjp(g_ref[:, _lanes(j % N_HEADS)])
            dx_ref[:, _lanes(j)] = dx.astype(dx_ref.dtype)
            dws[which] = dws[which] + dw
        first = pl.program_id(0) == 0
        _acc(dw_ref.at[0], dws[0], first)
        _acc(dw_ref.at[1], dws[1], first)

    return pl.pallas_call(
        body, name="fox_prep_bwd", grid=(t // ROWS,), in_specs=[x_spec, w_spec, g_spec, g_spec],
        out_specs=[x_spec, w_spec],
        out_shape=[jax.ShapeDtypeStruct((t, 2 * D_MODEL), MM), jax.ShapeDtypeStruct((2, 1, HEAD_DIM), F32)],
        compiler_params=_cparams(("arbitrary",)),
    )(proj, wqk, dq, dk)


def _row_pick(x, i):
    r = lax.broadcasted_iota(jnp.int32, x.shape, 0)
    return jnp.sum(jnp.where(r == i, x, 0.0), axis=0, keepdims=True)


def fox_gates_fwd(proj, f_bias):
    t = proj.shape[0]
    blk = HEAD_DIM

    def body(x_ref, b_ref, o_ref):
        lane = lax.broadcasted_iota(jnp.int32, (blk, HEAD_DIM), 1)
        tri = _tri_ones(blk, False)

        def step(c, carry):
            rows = pl.ds(pl.multiple_of(c * blk, blk), blk)
            lf = jnp.where(lane < N_HEADS, -_softplus(-(x_ref[rows, :] + b_ref[...])), 0.0)
            cum = _dot(tri, lf, 1, 0, True) + carry
            o_ref[rows, :] = cum
            return _row_pick(cum, blk - 1)

        lax.fori_loop(0, t // blk, step, jnp.zeros((1, HEAD_DIM), F32))

    vec = pl.BlockSpec((1, HEAD_DIM), lambda i: (0, 0))
    return pl.pallas_call(
        body, name="fox_gates_fwd", grid=(1,),
        in_specs=[pl.BlockSpec((t, HEAD_DIM), lambda i: (0, TAIL_BLK)), vec],
        out_specs=pl.BlockSpec((t, HEAD_DIM), lambda i: (0, 0)),
        out_shape=jax.ShapeDtypeStruct((t, HEAD_DIM), F32), compiler_params=_cparams(("arbitrary",)),
    )(proj, f_bias)


def fox_gates_bwd(proj, f_bias, dfcum):
    t = proj.shape[0]
    blk = HEAD_DIM
    nb = t // blk

    def body(x_ref, b_ref, g_ref, dx_ref, db_ref):
        lane = lax.broadcasted_iota(jnp.int32, (blk, HEAD_DIM), 1)
        tri = _tri_ones(blk, True)
        db_ref[...] = jnp.zeros_like(db_ref)

        def step(i, carry):
            c = nb - 1 - i
            rows = pl.ds(pl.multiple_of(c * blk, blk), blk)
            g = jnp.where(lane < N_HEADS, g_ref[rows, :], 0.0)
            dlf = _dot(tri, g, 1, 0, True) + carry
            dx = jnp.where(lane < N_HEADS, dlf * _sigmoid(-(x_ref[rows, :] + b_ref[...])), 0.0)
            dx_ref[rows, :] = dx.astype(dx_ref.dtype)
            db_ref[...] += jnp.sum(dx, axis=0, keepdims=True)
            return carry + jnp.sum(g, axis=0, keepdims=True)

        lax.fori_loop(0, nb, step, jnp.zeros((1, HEAD_DIM), F32))

    vec = pl.BlockSpec((1, HEAD_DIM), lambda i: (0, 0))
    full = pl.BlockSpec((t, HEAD_DIM), lambda i: (0, 0))
    return pl.pallas_call(
        body, name="fox_gates_bwd", grid=(1,),
        in_specs=[pl.BlockSpec((t, HEAD_DIM), lambda i: (0, TAIL_BLK)), vec, full], out_specs=[full, vec],
        out_shape=[jax.ShapeDtypeStruct((t, HEAD_DIM), MM), jax.ShapeDtypeStruct((1, HEAD_DIM), F32)],
        compiler_params=_cparams(("arbitrary",)),
    )(proj, f_bias, dfcum)


def fcum_to_heads(fcum):
    f = fcum[:, :N_HEADS].T
    return f[:, :, None], f[:, None, :]


def heads_to_fcum(dfcol, dfrow):
    d = (dfcol[:, :, 0] + dfrow[:, 0, :]).T
    return jnp.concatenate([d, jnp.zeros((d.shape[0], HEAD_DIM - N_HEADS), F32)], axis=1)


def _fox_tq(t):
    return min(t, 256)


def _fox_specs(t):
    tq = _fox_tq(t)
    q_spec = pl.BlockSpec((None, tq, HEAD_DIM), lambda h, i: (0, i, h))
    k_spec = pl.BlockSpec((None, t, HEAD_DIM), lambda h, i: (1, 0, h))
    v_spec = pl.BlockSpec((t, HEAD_DIM), lambda h, i: (0, 2 * N_HEADS + h))
    gate_spec = pl.BlockSpec((tq, HEAD_DIM), lambda h, i: (i, 3 * N_HEADS + h))
    col_spec = pl.BlockSpec((None, tq, 1), lambda h, i: (h, i, 0))
    row_spec = pl.BlockSpec((None, 1, t), lambda h, i: (h, 0, 0))
    blk_spec = pl.BlockSpec((tq, HEAD_DIM), lambda h, i: (i, h))
    head_spec = pl.BlockSpec((t, HEAD_DIM), lambda h, i: (0, h))
    return tq, q_spec, k_spec, v_spec, gate_spec, col_spec, row_spec, blk_spec, head_spec


def _fox_scores(q, k, fcol, frow, i, tq, t):
    s = _dot(q, k, 1, 1, False) + (fcol - frow)
    r = lax.broadcasted_iota(jnp.int32, (tq, t), 0) + i * tq
    c = lax.broadcasted_iota(jnp.int32, (tq, t), 1)
    return s, c <= r


def fox_attn_fwd(qk, proj, fcol, frow):
    t = proj.shape[0]
    tq, q_spec, k_spec, v_spec, gate_spec, col_spec, row_spec, blk_spec, _ = _fox_specs(t)

    def body(q_ref, k_ref, v_ref, gate_ref, fc_ref, fr_ref, mix_ref, o_ref, lse_ref):
        def block(i):
            w = (i + 1) * tq
            s, mask = _fox_scores(q_ref[...], k_ref[0:w, :], fc_ref[...], fr_ref[:, 0:w], i, tq, w)
            s = jnp.where(mask, s, -1e30)
            m = jnp.max(s, axis=-1, keepdims=True)
            p = jnp.where(mask, jnp.exp(s - m), 0.0)
            l = jnp.sum(p, axis=-1, keepdims=True)
            o = _dot(p, v_ref[0:w, :], 1, 0, False) / l
            o_ref[...] = o
            mix_ref[...] = (o * _sigmoid(gate_ref[...])).astype(mix_ref.dtype)
            lse_ref[...] = m + jnp.log(l)

        for i in range(t // tq):
            pl.when(pl.program_id(1) == i)(functools.partial(block, i))

    return pl.pallas_call(
        body, name="fox_attn_fwd", grid=(N_HEADS, t // tq),
        in_specs=[q_spec, k_spec, v_spec, gate_spec, col_spec, row_spec], out_specs=[blk_spec, blk_spec, col_spec],
        out_shape=[jax.ShapeDtypeStruct((t, D_MODEL), MM), jax.ShapeDtypeStruct((t, D_MODEL), F32),
                   jax.ShapeDtypeStruct((N_HEADS, t, 1), F32)],
        compiler_params=_cparams(("parallel", "parallel")),
    )(qk, qk, proj, proj, fcol, frow)


def fox_attn_bwd(qk, proj, fcol, frow, o, lse, dcat):
    t = proj.shape[0]
    tq, q_spec, k_spec, v_spec, gate_spec, col_spec, row_spec, blk_spec, head_spec = _fox_specs(t)

    def body(q_ref, k_ref, v_ref, gate_ref, fc_ref, fr_ref, o_ref, lse_ref, g_ref,
             dq_ref, dk_ref, dv_ref, dgate_ref, dfc_ref, dfr_ref):
        @pl.when(pl.program_id(1) == 0)
        def _():
            dk_ref[...] = jnp.zeros_like(dk_ref)
            dv_ref[...] = jnp.zeros_like(dv_ref)
            dfr_ref[...] = jnp.zeros_like(dfr_ref)

        def block(i):
            w = (i + 1) * tq
            sg = _sigmoid(gate_ref[...])
            g = g_ref[...].astype(F32)
            o_pre = o_ref[...]
            do = g * sg
            dgate_ref[...] = (g * o_pre * sg * (1.0 - sg)).astype(dgate_ref.dtype)
            s, mask = _fox_scores(q_ref[...], k_ref[0:w, :], fc_ref[...], fr_ref[:, 0:w], i, tq, w)
            p = jnp.where(mask, jnp.exp(jnp.where(mask, s, 0.0) - lse_ref[...]), 0.0)
            dp = _dot(do, v_ref[0:w, :], 1, 1, False)
            delta = jnp.sum(do * o_pre, axis=-1, keepdims=True)
            ds = p * (dp - delta)
            dq_ref[...] = _dot(ds, k_ref[0:w, :], 1, 0, False)
            dk_ref[0:w, :] += _dot(ds, q_ref[...], 0, 0, False)
            dv_ref[0:w, :] += _dot(p, do, 0, 0, False)
            dfc_ref[...] = jnp.sum(ds, axis=-1, keepdims=True)
            dfr_ref[:, 0:w] += -jnp.sum(ds, axis=0, keepdims=True)

        for i in range(t // tq):
            pl.when(pl.program_id(1) == i)(functools.partial(block, i))

    f32 = lambda *s: jax.ShapeDtypeStruct(s, F32)
    return pl.pallas_call(
        body, name="fox_attn_bwd", grid=(N_HEADS, t // tq),
        in_specs=[q_spec, k_spec, v_spec, gate_spec, col_spec, row_spec, blk_spec, col_spec, blk_spec],
        out_specs=[blk_spec, head_spec, head_spec, blk_spec, col_spec, row_spec],
        out_shape=[f32(t, D_MODEL), f32(t, D_MODEL), f32(t, D_MODEL), jax.ShapeDtypeStruct((t, D_MODEL), MM),
                   f32(N_HEADS, t, 1), f32(N_HEADS, 1, t)],
        compiler_params=_cparams(("parallel", "arbitrary")),
    )(qk, qk, proj, proj, fcol, frow, o, lse, dcat)


def adamw(w, g, m, v, *, name):
    r, c = w.shape
    rb = ROWS if r % ROWS == 0 else r

    def body(w_ref, g_ref, m_ref, v_ref, d_ref, nm_ref, nv_ref):
        g_ = g_ref[...]
        m_ = ADAM_B1 * m_ref[...] + (1.0 - ADAM_B1) * g_
        v_ = ADAM_B2 * v_ref[...] + (1.0 - ADAM_B2) * jnp.square(g_)
        m_hat = m_ / (1.0 - ADAM_B1 ** ADAM_STEP)
        v_hat = v_ / (1.0 - ADAM_B2 ** ADAM_STEP)
        d_ref[...] = -ADAM_LR * (m_hat / (jnp.sqrt(v_hat) + ADAM_EPS) + ADAM_WD * w_ref[...])
        nm_ref[...] = m_
        nv_ref[...] = v_

    blk = pl.BlockSpec((rb, c), lambda i: (i, 0))
    shp = jax.ShapeDtypeStruct((r, c), F32)
    return pl.pallas_call(body, name=name, grid=(r // rb,), in_specs=[blk] * 4, out_specs=[blk] * 3,
                          out_shape=[shp] * 3, compiler_params=_cparams(("parallel",)))(w, g, m, v)


def _place():
    x, y, c = lax.axis_index("x"), lax.axis_index("y"), lax.axis_index("c")
    return x, y, c, [(1 - x, y), (x, 1 - y), (1 - x, 1 - y)]


ANY = pl.BlockSpec(memory_space=pl.ANY)


def all_reduce_small(v):
    r, w = v.shape

    def body(v_ref, o_ref, buf, send_sems, recv_sems):
        x, y, c, _ = _place()
        me = 4 * x + 2 * y + c
        flip = lambda a, bit: 1 - a if bit else a
        cps = []
        for k in range(1, N_DEV):
            peer = (flip(x, k & 4), flip(y, k & 2), flip(c, k & 1))
            cp = pltpu.make_async_remote_copy(src_ref=v_ref, dst_ref=buf.at[me], send_sem=send_sems.at[k - 1],
                                              recv_sem=recv_sems.at[k - 1], device_id=peer, device_id_type=MESH)
            cp.start()
            cps.append((cp, 4 * peer[0] + 2 * peer[1] + peer[2]))
        buf[me] = v_ref[...]
        for k, (cp, peer_id) in enumerate(cps):
            pltpu.make_async_remote_copy(src_ref=v_ref, dst_ref=buf.at[peer_id], send_sem=send_sems.at[k],
                                         recv_sem=recv_sems.at[k], device_id=(x, y, c), device_id_type=MESH).wait_recv()
        for cp, _ in cps:
            cp.wait_send()
        acc = buf[0]
        for d in range(1, N_DEV):
            acc = acc + buf[d]
        o_ref[...] = acc

    vm = pl.BlockSpec(memory_space=pltpu.VMEM)
    return pl.pallas_call(
        body, name="all_reduce_small", in_specs=[vm], out_specs=vm, out_shape=jax.ShapeDtypeStruct((r, w), F32),
        scratch_shapes=[pltpu.VMEM((N_DEV, r, w), F32), pltpu.SemaphoreType.DMA((N_DEV - 1,)),
                        pltpu.SemaphoreType.DMA((N_DEV - 1,))],
    )(v)


def _vec8(v):
    return jnp.zeros((1, HEAD_DIM), F32).at[0, :N_HEADS].set(v.reshape(N_HEADS))


def _layer_fwd(i, x_in, wt, sm, mem_k, mem_v, late=None):
    tag = f"l{i}_"
    h = rms_fwd(x_in, sm["norm1_w"][i][None], name=tag + "rms1")
    w_in = wt["dn_w_in"] if i == 0 else wt["fox_w_in"]
    proj = matmul(h, w_in, name=tag + "proj", tm=256, tk=1024)
    sv = dict(x_in=x_in, h=h, proj=proj)
    if i == 0:
        qkv = dn_prep_fwd(proj, wt["conv_w"])
        gates = dn_gates_fwd(proj, _vec8(sm["dn_a_log"]), _vec8(sm["dn_dt_bias"]))
        gcol, grow, bcol = gates_to_heads(gates)
        o, states = dn_core_fwd(qkv, gcol, grow, bcol)
        mix = dn_out_fwd(o, proj, sm["dn_o_norm_w"])
        sv.update(qkv=qkv, gcol=gcol, grow=grow, bcol=bcol, states=states, o=o)
    else:
        wqk = jnp.stack([sm["fox_q_norm_w"], sm["fox_k_norm_w"]])
        qk = fox_prep_fwd(proj, wqk)
        fcum = fox_gates_fwd(proj, _vec8(sm["fox_f_bias"]))
        fcol, frow = fcum_to_heads(fcum)
        mix, o, lse = fox_attn_fwd(qk, proj, fcol, frow)
        sv.update(wqk=wqk, qk=qk, fcol=fcol, frow=frow, o=o, lse=lse)
    mem_out = memattn_fwd(proj, sm["memq_norm_w"][i][None], mem_k, mem_v, name=tag + "memattn_fwd")
    cat = jnp.concatenate([mix, mem_out], axis=1)
    if late is not None:
        wt.update(late(cat))
    x_mid = matmul(cat, wt["w_out"][i], res=x_in, name=tag + "out_proj")
    h2 = rms_fwd(x_mid, sm["norm2_w"][i][None], name=tag + "rms2")
    ff, act = matmul(h2, wt["w_mlp1"][i], b_slots=True, also_sqrelu=True, name=tag + "mlp1")
    x_out = matmul(act, wt["w_mlp2"][i], res=x_mid, name=tag + "mlp2")
    sv.update(cat=cat, x_mid=x_mid, h2=h2, ff=ff, act=act)
    return x_out, sv


def _layer_bwd(i, dx_out, sv, wt, sm, mem_k, mem_v, on_mlp=None):
    tag = f"l{i}_"
    big, small = {}, {}
    dff = matmul(dx_out, wt["w_mlp2"][i], tb=True, times_dsqrelu=sv["ff"], out_dtype=MM, name=tag + "d_ff")
    big["w_mlp2"] = matmul(sv["act"], dx_out, ta=True, name=tag + "d_w_mlp2")
    dh2 = matmul(dff, wt["w_mlp1"][i], tb=True, b_slots=True, name=tag + "d_h2")
    big["w_mlp1"] = matmul(sv["h2"], dff, ta=True, name=tag + "d_w_mlp1", tm=512, tn=D_FF, tk=512)
    norm2_w = sm["norm2_w"][i][None]
    if on_mlp is not None:
        norm2_w = norm2_w + on_mlp(big["w_mlp2"], big["w_mlp1"])
    dx_mid, small["norm2_w"] = rms_bwd(sv["x_mid"], norm2_w, dh2, dx_out, name=tag + "rms2_bwd")
    dcat = matmul(dx_mid, wt["w_out"][i], tb=True, name=tag + "d_cat")
    big["w_out"] = matmul(sv["cat"], dx_mid, ta=True, name=tag + "d_w_out")
    proj = sv["proj"]
    dqm, small["memq_norm_w"], dmk, dmv = memattn_bwd(proj, sm["memq_norm_w"][i][None], mem_k, mem_v, dcat,
                                                      name=tag + "memattn_bwd")
    t = proj.shape[0]
    pad = jnp.zeros((t, PROJ_W - TAIL - HEAD_DIM), MM)
    if i == 0:
        do, dz, small["dn_o_norm_w"] = dn_out_bwd(sv["o"], proj, sm["dn_o_norm_w"], dcat)
        dqkv, dgc, dgr, dbc = dn_core_bwd(sv["qkv"], sv["gcol"], sv["grow"], sv["bcol"], sv["states"], do)
        dtail, dal, ddt = dn_gates_bwd(proj, _vec8(sm["dn_a_log"]), _vec8(sm["dn_dt_bias"]), heads_to_gates(dgc, dgr, dbc))
        dmain, dconv = dn_prep_bwd(proj, wt["conv_w"], dqkv)
        small["dn_a_log"], small["dn_dt_bias"] = dal[:, :N_HEADS], ddt[:, :N_HEADS]
        big["conv_w"] = dconv[:4]
        dproj = jnp.concatenate([dmain, dz, dqm, dtail, pad], axis=1)
    else:
        dq, dk, dv, dgate, dfc, dfr = fox_attn_bwd(sv["qk"], proj, sv["fcol"], sv["frow"], sv["o"], sv["lse"], dcat)
        dtail, dfb = fox_gates_bwd(proj, _vec8(sm["fox_f_bias"]), heads_to_fcum(dfc, dfr))
        dqk, dwqk = fox_prep_bwd(proj, sv["wqk"], dq, dk)
        small["fox_f_bias"] = dfb[:, :N_HEADS]
        small["fox_q_norm_w"], small["fox_k_norm_w"] = dwqk[0], dwqk[1]
        dproj = jnp.concatenate([dqk, dv.astype(MM), dgate, dqm, dtail, pad], axis=1)
    w_in = wt["dn_w_in"] if i == 0 else wt["fox_w_in"]
    dh = matmul(dproj, w_in, tb=True, name=tag + "d_h", tm=512)
    big["w_in"] = matmul(sv["h"], dproj, ta=True, name=tag + "d_w_in", tm=256)
    dx_in, small["norm1_w"] = rms_bwd(sv["x_in"], sm["norm1_w"][i][None], dh, dx_mid, name=tag + "rms1_bwd")
    return dx_in, big, small, (dmk, dmv)


def local_step(x, mem, target, wt, sm, late=None, on_layer1=None, on_mlp0=None):
    wt = dict(wt)
    mem_k, mem_v = mem_fwd(mem, sm["mem_norm_w"][None], wt["w_mem_kv"], sm["mem_k_norm_w"][None])
    x0, sv0 = _layer_fwd(0, x, wt, sm, mem_k, mem_v, late)
    x1, sv1 = _layer_fwd(1, x0, wt, sm, mem_k, mem_v)
    dy, loss = loss_fwd(x1, target, name="loss")
    dx1, big1, small1, dm1 = _layer_bwd(1, dy, sv1, wt, sm, mem_k, mem_v)
    if on_layer1 is not None:
        dx1 = dx1 + on_layer1(big1)
    dx0, big0, small0, dm0 = _layer_bwd(0, dx1, sv0, wt, sm, mem_k, mem_v, on_mlp0)
    dwn, dwkv, dwkn = mem_bwd(mem, sm["mem_norm_w"][None], wt["w_mem_kv"], sm["mem_k_norm_w"][None], *dm0, *dm1)
    small = dict(mem_norm_w=dwn[0], mem_k_norm_w=dwkn[0],
                 norm1_w=jnp.concatenate([small0["norm1_w"], small1["norm1_w"]]),
                 norm2_w=jnp.concatenate([small0["norm2_w"], small1["norm2_w"]]),
                 memq_norm_w=jnp.concatenate([small0["memq_norm_w"], small1["memq_norm_w"]]),
                 dn_a_log=small0["dn_a_log"], dn_dt_bias=small0["dn_dt_bias"], dn_o_norm_w=small0["dn_o_norm_w"],
                 fox_f_bias=small1["fox_f_bias"], fox_q_norm_w=small1["fox_q_norm_w"], fox_k_norm_w=small1["fox_k_norm_w"])
    big = dict(w_mem_kv=dwkv, dn_w_in=big0["w_in"], fox_w_in=big1["w_in"], conv_w=big0["conv_w"],
               w_out=[big0["w_out"], big1["w_out"]], w_mlp1=[big0["w_mlp1"], big1["w_mlp1"]],
               w_mlp2=[big0["w_mlp2"], big1["w_mlp2"]])
    return loss, dx0, big, small


def w_in_to_kernel(w, n_scalars):
    pad = jnp.zeros((w.shape[0], PROJ_W - TAIL - n_scalars), w.dtype)
    return jnp.concatenate([w[:, :4096], w[:, 4096 + n_scalars:], w[:, 4096:4096 + n_scalars], pad], axis=1)


def w_in_from_kernel(w, n_scalars):
    return jnp.concatenate([w[:, :4096], w[:, TAIL:TAIL + n_scalars], w[:, 4096:TAIL]], axis=1)


BIG_SPECS = dict(w_mem_kv=("rows", 1, 256, 1024), w_out=("rows", 2, 384, 1024), w_mlp2=("rows", 2, 1024, 1024),
                 w_mlp1=("cols", 2, 1024, 1024), dn_w_in=("rows", 1, 1024, 1156), fox_w_in=("rows", 1, 1024, 1154))
BIG_NAMES = tuple(BIG_SPECS)
EARLY_NAMES = ("w_mem_kv", "dn_w_in")
LATE_NAMES = ("w_out", "w_mlp2", "w_mlp1", "fox_w_in")
BIG_SPECS.update({f"{name}_{i}": (BIG_SPECS[name][0], 1) + BIG_SPECS[name][2:]
                  for name in ("w_out", "w_mlp2", "w_mlp1") for i in range(2)})
RS_LAYER1 = ("fox_w_in", "w_out_1", "w_mlp2_1", "w_mlp1_1")
RS_MLP0 = ("w_mlp2_0", "w_mlp1_0")
RS_LAST = ("w_out_0", "dn_w_in", "w_mem_kv")


def _full_shape(name, half=False):
    kind, a, b, c = BIG_SPECS[name]
    b = b // 2 if half else b
    return (a, N_CHIP, b, c) if kind == "rows" else (a, b, N_CHIP * c)


def _ds(start, size, align):
    return pl.ds(start if isinstance(start, int) else pl.multiple_of(start, align), size)


def _half_rows(name, h):
    b = BIG_SPECS[name][2]
    return _ds(h * (b // 2), b // 2, 16)


def _shard_idx(name, h):
    return (slice(None), _half_rows(name, h), slice(None))


def _full_idx(name, j=None, h=None):
    kind, _, _, c = BIG_SPECS[name]
    rows = slice(None) if h is None else _half_rows(name, h)
    if kind == "rows":
        return (slice(None), slice(None) if j is None else j, rows, slice(None))
    return (slice(None), rows, slice(None) if j is None else _ds(j * c, c, 128))


def _slots_shape(name):
    _, a, b, c = BIG_SPECS[name]
    return (a, N_CHIP, b, c)


def _slots_idx(name, j, h):
    return (slice(None), j, _half_rows(name, h), slice(None))


def _row_block(name):
    hs = BIG_SPECS[name][2] // 2
    return hs if hs <= ROWS else ROWS


def _remote(src, dst, send_sem, recv_sem, to):
    return pltpu.make_async_remote_copy(src_ref=src, dst_ref=dst, send_sem=send_sem, recv_sem=recv_sem, device_id=to,
                                        device_id_type=MESH)


HBM = pl.BlockSpec(memory_space=pltpu.HBM)
SEM = pl.BlockSpec(memory_space=pltpu.SEMAPHORE)
EFFECT = pltpu.CompilerParams(has_side_effects=pltpu.SideEffectType.DATAFLOW_SIDE_EFFECTING)


def _in_hbm(a):
    return pltpu.with_memory_space_constraint(a, pltpu.HBM)


def _chip_copies(names, ins, lands, send_sems, recv_sems):
    x, y, c, chips = _place()
    return [_remote(ins[a].at[_shard_idx(name, c)], lands[a].at[_slots_idx(name, 2 * x + y, c)], send_sems.at[3 * a + k],
                    recv_sems.at[3 * a + k], (chip[0], chip[1], c))
            for a, name in enumerate(names) for k, chip in enumerate(chips)]


def _copies_start(call_name, copies, sources, land_shapes):
    n = len(sources)

    def body(*refs):
        ins, lands, send_sems, recv_sems, token = refs[:n], refs[n:2 * n], refs[2 * n], refs[2 * n + 1], refs[-1]
        for cp in copies(ins, lands, send_sems, recv_sems):
            cp.start()
        token[...] = jnp.zeros_like(token)

    ins = [_in_hbm(a) for a in sources]
    lands = [_in_hbm(lax.empty(shape, MM)) for shape in land_shapes]
    sems = (pltpu.SemaphoreType.DMA((3 * n,)), pltpu.SemaphoreType.DMA((3 * n,)))
    outs = pl.pallas_call(
        body, name=call_name, in_specs=[HBM] * (2 * n),
        out_specs=(SEM, SEM) + (HBM,) * (2 * n) + (pl.BlockSpec(memory_space=pltpu.VMEM),),
        out_shape=sems + tuple(pltpu.HBM(a.shape, a.dtype) for a in ins + lands) + (jax.ShapeDtypeStruct((8, HEAD_DIM), F32),),
        input_output_aliases={a: 2 + a for a in range(2 * n)}, compiler_params=EFFECT,
    )(*ins, *lands)
    return outs[:-1], outs[-1]


def _copies_wait(call_name, copies, state, after):
    n = (len(state) - 2) // 2

    def body(*refs):
        send_sems, recv_sems, ins, lands = refs[0], refs[1], refs[2:2 + n], refs[2 + n:2 + 2 * n]
        for cp in copies(ins, lands, send_sems, recv_sems):
            cp.wait_send()
            cp.wait_recv()

    outs = pl.pallas_call(
        body, name=call_name, in_specs=[SEM, SEM] + [HBM] * (2 * n) + [ANY], out_specs=(HBM,) * (2 * n),
        out_shape=tuple(pltpu.HBM(a.shape, a.dtype) for a in state[2:]),
        input_output_aliases={2 + a: a for a in range(2 * n)}, compiler_params=EFFECT,
    )(*state, after)
    return outs[:n], outs[n:]


def all_gather_start(shards, names):
    return _copies_start("all_gather_start", functools.partial(_chip_copies, names), [shards[name] for name in names],
                         [_slots_shape(name) for name in names])


def all_gather_wait(state, names, after):
    ins, lands = _copies_wait("all_gather_wait", functools.partial(_chip_copies, names), state, after)
    return dict(zip(names, ins)), dict(zip(names, lands))


def _chip_sends(names, ins, lands, send_sems, recv_sems):
    x, y, c, chips = _place()
    return [_remote(ins[a].at[_full_idx(name, 2 * chip[0] + chip[1])], lands[a].at[k], send_sems.at[3 * a + k],
                    recv_sems.at[3 * a + k], (chip[0], chip[1], c))
            for a, name in enumerate(names) for k, chip in enumerate(chips)]


def _got_shape(name):
    _, a_, b_, c_ = BIG_SPECS[name]
    return (3, a_, b_ // 2, c_)


def rs_chip_start(pairs, names, tag):
    return _copies_start("rs_chip_start_" + tag, functools.partial(_chip_sends, names), [pairs[name] for name in names],
                         [_got_shape(name) for name in names])


def rs_chip_wait(state, names, tag, after):
    _, lands = _copies_wait("rs_chip_wait_" + tag, functools.partial(_chip_sends, names), state, after)
    return dict(zip(names, lands))


def all_gather_pass_on(lands, names):
    n = len(names)

    def body(*refs):
        outs, send_sems, recv_sems = refs[n:2 * n], refs[2 * n], refs[2 * n + 1]
        x, y, c, chips = _place()
        work = [(3 * a + k, a, name, 2 * chip[0] + chip[1]) for a, name in enumerate(names) for k, chip in enumerate(chips)]
        cps = []
        for s, a, name, slot in work:
            landed = outs[a].at[_slots_idx(name, slot, c)]
            cps.append(_remote(landed, landed, send_sems.at[s], recv_sems.at[s], (x, y, 1 - c)))
            cps[-1].start()
        for s, a, name, slot in work:
            passed = outs[a].at[_slots_idx(name, slot, 1 - c)]
            _remote(passed, passed, send_sems.at[s], recv_sems.at[s], (x, y, 1 - c)).wait_recv()
        for cp in cps:
            cp.wait_send()

    outs = pl.pallas_call(
        body, name="all_gather_pass_on", in_specs=[ANY] * n, out_specs=[ANY] * n,
        input_output_aliases={a: a for a in range(n)},
        out_shape=[jax.ShapeDtypeStruct(_slots_shape(name), MM) for name in names],
        scratch_shapes=[pltpu.SemaphoreType.DMA((3 * n,)), pltpu.SemaphoreType.DMA((3 * n,))],
    )(*[lands[name] for name in names])
    return dict(zip(names, outs))


def all_gather_big(shards, names):
    n = len(names)
    BIG_NAMES = names

    def body(*refs):
        ins, outs = refs[:n], refs[n:2 * n]
        send_sems, recv_sems, fsend_sems, frecv_sems = refs[2 * n:]
        x, y, c, chips = _place()
        me_chip, sibling = 2 * x + y, (x, y, 1 - c)
        work = [(3 * a + k, a, name, chip) for a, name in enumerate(BIG_NAMES) for k, chip in enumerate(chips)]
        sends = []
        for s, a, name, chip in work:
            cp = _remote(ins[a].at[_shard_idx(name, c)], outs[a].at[_slots_idx(name, me_chip, c)], send_sems.at[s],
                         recv_sems.at[s], (chip[0], chip[1], c))
            cp.start()
            sends.append(cp)
        for s, a, name, chip in work:
            landed = outs[a].at[_slots_idx(name, 2 * chip[0] + chip[1], c)]
            _remote(landed, landed, send_sems.at[s], recv_sems.at[s], (chip[0], chip[1], c)).wait_recv()
            cp = _remote(landed, landed, fsend_sems.at[s], frecv_sems.at[s], sibling)
            cp.start()
            sends.append(cp)
        for s, a, name, chip in work:
            passed = outs[a].at[_slots_idx(name, 2 * chip[0] + chip[1], 1 - c)]
            _remote(passed, passed, fsend_sems.at[s], frecv_sems.at[s], sibling).wait_recv()
        for cp in sends:
            cp.wait_send()

    outs = pl.pallas_call(
        body, name="all_gather_big", in_specs=[ANY] * n, out_specs=[ANY] * n,
        out_shape=[jax.ShapeDtypeStruct(_slots_shape(name), MM) for name in BIG_NAMES],
        scratch_shapes=[pltpu.SemaphoreType.DMA((3 * n,))] * 4,
    )(*[shards[name] for name in BIG_NAMES])
    return dict(zip(BIG_NAMES, outs))


def with_own_slot(name, full, shard, chip):
    return lax.dynamic_update_slice(full, shard[:, None], (0, chip, 0, 0))


def rs_pair_exchange_big(grads, names, tag):
    n = len(names)

    def body(*refs):
        ins, outs, send_sems, recv_sems = refs[:n], refs[n:2 * n], refs[2 * n], refs[2 * n + 1]
        x, y, c, _ = _place()
        cps = []
        for a, name in enumerate(names):
            cp = _remote(ins[a].at[_full_idx(name, None, 1 - c)], outs[a], send_sems.at[a], recv_sems.at[a], (x, y, 1 - c))
            cp.start()
            cps.append(cp)
        for cp in cps:
            cp.wait()

    outs = pl.pallas_call(
        body, name="rs_pair_exchange_" + tag, in_specs=[ANY] * n, out_specs=[ANY] * n,
        out_shape=[jax.ShapeDtypeStruct(_full_shape(name, half=True), F32) for name in names],
        scratch_shapes=[pltpu.SemaphoreType.DMA((n,)), pltpu.SemaphoreType.DMA((n,))],
    )(*[grads[name] for name in names])
    return dict(zip(names, outs))


def rs_pair_add_big(name, place, g, got):
    kind, a_, b_, c_ = BIG_SPECS[name]
    rb = _row_block(name)
    nb = (b_ // 2) // rb

    def body(place_ref, g_ref, got_ref, o_ref):
        o_ref[...] = (g_ref[...] + got_ref[...]).astype(o_ref.dtype)

    if kind == "rows":
        g_spec = pl.BlockSpec((None, None, rb, c_), lambda a, j, i, p: (a, j, p[0] * nb + i, 0))
        o_spec = pl.BlockSpec((None, None, rb, c_), lambda a, j, i, p: (a, j, i, 0))
    else:
        g_spec = pl.BlockSpec((None, rb, c_), lambda a, j, i, p: (a, p[0] * nb + i, j))
        o_spec = pl.BlockSpec((None, rb, c_), lambda a, j, i, p: (a, i, j))
    return pl.pallas_call(
        body, name="rs_pair_add_" + name,
        grid_spec=pltpu.PrefetchScalarGridSpec(num_scalar_prefetch=1, grid=(a_, N_CHIP, nb), in_specs=[g_spec, o_spec],
                                               out_specs=o_spec),
        out_shape=jax.ShapeDtypeStruct(_full_shape(name, half=True), MM),
        compiler_params=_cparams(("parallel", "parallel", "parallel")),
    )(place, g, got)


def rs_chip_exchange_big(pairs, names, tag):
    n = len(names)

    def body(*refs):
        ins, outs, send_sems, recv_sems = refs[:n], refs[n:2 * n], refs[2 * n], refs[2 * n + 1]
        cps = _chip_sends(names, ins, outs, send_sems, recv_sems)
        for cp in cps:
            cp.start()
        for cp in cps:
            cp.wait()

    outs = pl.pallas_call(
        body, name="rs_chip_exchange_" + tag, in_specs=[ANY] * n, out_specs=[ANY] * n,
        out_shape=[jax.ShapeDtypeStruct(_got_shape(name), MM) for name in names],
        scratch_shapes=[pltpu.SemaphoreType.DMA((3 * n,)), pltpu.SemaphoreType.DMA((3 * n,))],
    )(*[pairs[name] for name in names])
    return dict(zip(names, outs))


def rs_chip_add_big(name, place, g, got_pair, got_chips):
    kind, a_, b_, c_ = BIG_SPECS[name]
    rb = _row_block(name)
    nb = (b_ // 2) // rb

    def body(place_ref, g_ref, s_ref, r0_ref, r1_ref, r2_ref, o_ref):
        own = g_ref[...] + s_ref[...]
        o_ref[...] = ((own + r0_ref[...].astype(F32)) + r1_ref[...].astype(F32)) + r2_ref[...].astype(F32)

    if kind == "rows":
        g_spec = pl.BlockSpec((None, None, rb, c_), lambda a, i, p: (a, p[1], p[0] * nb + i, 0))
        s_spec = pl.BlockSpec((None, None, rb, c_), lambda a, i, p: (a, p[1], i, 0))
    else:
        g_spec = pl.BlockSpec((None, rb, c_), lambda a, i, p: (a, p[0] * nb + i, p[1]))
        s_spec = pl.BlockSpec((None, rb, c_), lambda a, i, p: (a, i, p[1]))
    r_spec = lambda k: pl.BlockSpec((None, None, rb, c_), lambda a, i, p: (k, a, i, 0))
    return pl.pallas_call(
        body, name="rs_chip_add_" + name,
        grid_spec=pltpu.PrefetchScalarGridSpec(
            num_scalar_prefetch=1, grid=(a_, nb), in_specs=[g_spec, s_spec, r_spec(0), r_spec(1), r_spec(2)],
            out_specs=pl.BlockSpec((None, rb, c_), lambda a, i, p: (a, p[0] * nb + i, 0))),
        out_shape=jax.ShapeDtypeStruct((a_, b_, c_), F32), compiler_params=_cparams(("parallel", "parallel")),
    )(place, g, got_pair, got_chips, got_chips, got_chips)


def rs_pair_gather_big(halves):
    names = tuple(halves)
    n = len(names)

    def body(*refs):
        outs, send_sems, recv_sems = refs[n:2 * n], refs[2 * n], refs[2 * n + 1]
        x, y, c, _ = _place()
        cps = []
        for a, name in enumerate(names):
            mine = outs[a].at[_shard_idx(name, c)]
            cp = _remote(mine, mine, send_sems.at[a], recv_sems.at[a], (x, y, 1 - c))
            cp.start()
            cps.append(cp)
        for a, name in enumerate(names):
            cps[a].wait_send()
            theirs = outs[a].at[_shard_idx(name, 1 - c)]
            _remote(theirs, theirs, send_sems.at[a], recv_sems.at[a], (x, y, 1 - c)).wait_recv()

    outs = pl.pallas_call(
        body, name="rs_pair_gather_big", in_specs=[ANY] * n, out_specs=[ANY] * n,
        input_output_aliases={a: a for a in range(n)},
        out_shape=[jax.ShapeDtypeStruct(BIG_SPECS[name][1:], F32) for name in names],
        scratch_shapes=[pltpu.SemaphoreType.DMA((n,)), pltpu.SemaphoreType.DMA((n,))],
    )(*[halves[name] for name in names])
    return dict(zip(names, outs))


def rs_begin(grads, names, tag, place):
    got_pair = rs_pair_exchange_big(grads, names, tag)
    pairs = {name: rs_pair_add_big(name, place, grads[name], got_pair[name]) for name in names}
    state, token = rs_chip_start(pairs, names, tag)
    return (grads, got_pair, state), token


def rs_end(begun, names, tag, place, after):
    grads, got_pair, state = begun
    got_chips = rs_chip_wait(state, names, tag, after)
    return {name: rs_chip_add_big(name, place, grads[name], got_pair[name], got_chips[name]) for name in names}


def rs_whole(grads, names, tag, place):
    got_pair = rs_pair_exchange_big(grads, names, tag)
    pairs = {name: rs_pair_add_big(name, place, grads[name], got_pair[name]) for name in names}
    got_chips = rs_chip_exchange_big(pairs, names, tag)
    return {name: rs_chip_add_big(name, place, grads[name], got_pair[name], got_chips[name]) for name in names}


PACK_W = 1024
SMALL =(("mem_norm_w", 1024), ("mem_k_norm_w", 128), ("norm1_w", 2048), ("dn_a_log", 8), ("dn_dt_bias", 8),
         ("dn_o_norm_w", 128), ("fox_f_bias", 8), ("fox_q_norm_w", 128), ("fox_k_norm_w", 128), ("memq_norm_w", 256),
         ("norm2_w", 2048))
SMALL_ROWS = 8
CONV_ROWS = 4 * 3 * D_MODEL // PACK_W
LOSS_AT = sum(n for _, n in SMALL)


def pack_small(parts, extra=None):
    flat = [parts[name].astype(F32).reshape(-1) for name, _ in SMALL]
    used = LOSS_AT
    if extra is not None:
        flat.append(extra.reshape(1))
        used += 1
    flat.append(jnp.zeros((SMALL_ROWS * PACK_W - used,), F32))
    return jnp.concatenate(flat).reshape(SMALL_ROWS, PACK_W)


def unpack_small(packed, shapes):
    flat, out, at = packed.reshape(-1), {}, 0
    for name, n in SMALL:
        out[name] = flat[at:at + n].reshape(shapes[name])
        at += n
    return out


def _adam_all(w, g, m, v, name):
    shape = w.shape
    r2 = lambda a: a.reshape(-1, shape[-1])
    d, nm, nv = adamw(r2(w), r2(g), r2(m), r2(v), name=name)
    return d.reshape(shape), nm.reshape(shape), nv.reshape(shape)


BIG = ("w_mem_kv", "dn_w_in", "dn_conv_w", "fox_w_in", "w_out", "w_mlp1", "w_mlp2")
WEIGHTS = ("mem_norm_w", "w_mem_kv", "mem_k_norm_w", "norm1_w", "dn_w_in", "dn_conv_w", "dn_a_log", "dn_dt_bias",
           "dn_o_norm_w", "fox_w_in", "fox_f_bias", "fox_q_norm_w", "fox_k_norm_w", "memq_norm_w", "w_out", "norm2_w",
           "w_mlp1", "w_mlp2")


def kernel(x, mem, mem_norm_w, w_mem_kv, mem_k_norm_w, norm1_w, dn_w_in, dn_conv_w, dn_a_log, dn_dt_bias, dn_o_norm_w, fox_w_in, fox_f_bias, fox_q_norm_w, fox_k_norm_w, memq_norm_w, w_out, norm2_w, w_mlp1, w_mlp2, loss_target, m_mem_norm_w, m_w_mem_kv, m_mem_k_norm_w, m_norm1_w, m_dn_w_in, m_dn_conv_w, m_dn_a_log, m_dn_dt_bias, m_dn_o_norm_w, m_fox_w_in, m_fox_f_bias, m_fox_q_norm_w, m_fox_k_norm_w, m_memq_norm_w, m_w_out, m_norm2_w, m_w_mlp1, m_w_mlp2, v_mem_norm_w, v_w_mem_kv, v_mem_k_norm_w, v_norm1_w, v_dn_w_in, v_dn_conv_w, v_dn_a_log, v_dn_dt_bias, v_dn_o_norm_w, v_fox_w_in, v_fox_f_bias, v_fox_q_norm_w, v_fox_k_norm_w, v_memq_norm_w, v_w_out, v_norm2_w, v_w_mlp1, v_w_mlp2):
    args = dict(locals())
    w = {n: args[n] for n in WEIGHTS}
    m = {n: args["m_" + n] for n in WEIGHTS}
    v = {n: args["v_" + n] for n in WEIGHTS}
    core, chip = lax.axis_index("c"), 2 * lax.axis_index("x") + lax.axis_index("y")
    place = jnp.stack([core, chip]).astype(jnp.int32)

    shards = {name: w[name].reshape(BIG_SPECS[name][1:]).astype(MM) for name in BIG_NAMES}
    w_in_full = lambda arr, n_scalars: w_in_to_kernel(arr[0].transpose(1, 0, 2).reshape(D_MODEL, -1), n_scalars)
    early = {name: with_own_slot(name, arr, shards[name], chip)
             for name, arr in all_gather_big(shards, EARLY_NAMES).items()}
    conv_mine = jnp.where(core == 0, dn_conv_w[0], 0.0)
    conv_placed = lax.dynamic_update_slice(jnp.zeros((4, 3 * D_MODEL), F32), conv_mine, (0, 768 * chip))
    conv_full = all_reduce_small(jnp.pad(conv_placed.reshape(CONV_ROWS, PACK_W), ((0, 16 - CONV_ROWS), (0, 0))))
    late_shards, early, conv_full = lax.optimization_barrier(
        ({name: shards[name] for name in LATE_NAMES}, early, conv_full))
    late_state, token = all_gather_start(late_shards, LATE_NAMES)
    tie = token[0, 0]
    wt = dict(w_mem_kv=early["w_mem_kv"].reshape(D_MODEL, 2 * MEM_WIDTH) + tie.astype(MM),
              dn_w_in=w_in_full(early["dn_w_in"], 2 * N_HEADS), conv_w=conv_full[:CONV_ROWS].reshape(4, 3 * D_MODEL))

    def late(after):
        late_shards, lands = all_gather_wait(late_state, LATE_NAMES, after)
        full = {name: with_own_slot(name, arr, late_shards[name], chip)
                for name, arr in all_gather_pass_on(lands, LATE_NAMES).items()}
        return dict(fox_w_in=w_in_full(full["fox_w_in"], N_HEADS), w_out=full["w_out"].reshape(2, 3 * MEM_WIDTH, D_MODEL),
                    w_mlp1=full["w_mlp1"], w_mlp2=full["w_mlp2"].reshape(2, D_FF, D_MODEL))

    sm = dict(mem_norm_w=mem_norm_w, mem_k_norm_w=mem_k_norm_w, norm1_w=norm1_w, norm2_w=norm2_w, memq_norm_w=memq_norm_w,
              dn_a_log=dn_a_log[0], dn_dt_bias=dn_dt_bias[0], dn_o_norm_w=dn_o_norm_w, fox_f_bias=fox_f_bias[0],
              fox_q_norm_w=fox_q_norm_w, fox_k_norm_w=fox_k_norm_w)
    w_in_slots = lambda g, n_scalars: w_in_from_kernel(g, n_scalars).reshape(D_MODEL, N_CHIP, -1).transpose(1, 0, 2)[None]
    rows_view = lambda g, name: g.reshape(_full_shape(name))
    begun = {}

    def on_layer1(big1):
        grads1 = dict(fox_w_in=w_in_slots(big1["w_in"], N_HEADS), w_out_1=rows_view(big1["w_out"], "w_out_1"),
                      w_mlp2_1=rows_view(big1["w_mlp2"], "w_mlp2_1"), w_mlp1_1=big1["w_mlp1"][None])
        begun["layer1"], token = rs_begin(grads1, RS_LAYER1, "layer1", place)
        return token[0, 0]

    def on_mlp0(d_w_mlp2, d_w_mlp1):
        grads0 = dict(w_mlp2_0=rows_view(d_w_mlp2, "w_mlp2_0"), w_mlp1_0=d_w_mlp1[None])
        begun["mlp0"], token = rs_begin(grads0, RS_MLP0, "mlp0", place)
        return token[0, 0]

    loss_part, dx, big, small = local_step(x[0], mem[0], loss_target[0], wt, sm, late, on_layer1, on_mlp0)
    last = dict(w_out_0=rows_view(big["w_out"][0], "w_out_0"), dn_w_in=w_in_slots(big["dn_w_in"], 2 * N_HEADS),
                w_mem_kv=rows_view(big["w_mem_kv"], "w_mem_kv"))
    halves = rs_whole(last, RS_LAST, "last", place)
    halves.update(rs_end(begun["layer1"], RS_LAYER1, "layer1", place, dx))
    halves.update(rs_end(begun["mlp0"], RS_MLP0, "mlp0", place, dx))
    summed = rs_pair_gather_big(halves)
    big_sum = {name: summed[name] for name in ("w_mem_kv", "dn_w_in", "fox_w_in")}
    big_sum.update({name: jnp.concatenate([summed[name + "_0"], summed[name + "_1"]]) for name in ("w_out", "w_mlp2", "w_mlp1")})
    small_pack = jnp.concatenate([pack_small(small, loss_part[0, :1]), big["conv_w"].reshape(CONV_ROWS, PACK_W),
                                  jnp.zeros((24 - SMALL_ROWS - CONV_ROWS, PACK_W), F32)])
    small_all = all_reduce_small(small_pack)
    small_sum = small_all[:SMALL_ROWS]
    conv_sum = lax.dynamic_slice(small_all[SMALL_ROWS:SMALL_ROWS + CONV_ROWS].reshape(4, 3 * D_MODEL), (0, 768 * chip), (4, 768))
    loss = small_sum.reshape(-1)[LOSS_AT]
    grads = unpack_small(small_sum, {n: w[n].shape for n, _ in SMALL})
    grads.update({name: big_sum[name].reshape(w[name].shape) for name in BIG_NAMES}, dn_conv_w=conv_sum[None])

    delta, new_m, new_v = {}, {}, {}
    for n in BIG:
        delta[n], new_m[n], new_v[n] = _adam_all(w[n], grads[n], m[n], v[n], "adamw_" + n)
    shapes = {n: w[n].shape for n, _ in SMALL}
    d_s, m_s, v_s = adamw(pack_small(w), small_sum, pack_small(m), pack_small(v), name="adamw_small")
    for out, packed in ((delta, d_s), (new_m, m_s), (new_v, v_s)):
        out.update(unpack_small(packed, shapes))
    return (loss, dx[None], *[grads[n] for n in WEIGHTS], *[delta[n] for n in WEIGHTS],
            *[new_m[n] for n in WEIGHTS], *[new_v[n] for n in WEIGHTS])
```

```python
import functools

import jax
import jax.numpy as jnp
from jax import lax
from jax.experimental import pallas as pl
from jax.experimental.pallas import tpu as pltpu

F32 = jnp.float32
MM = jnp.bfloat16
HI = lax.Precision.HIGHEST

D_MODEL = 1024
HEAD_DIM = 128
N_HEADS = 8
MEM_HEADS = 4
MEM_WIDTH = MEM_HEADS * HEAD_DIM
N_MEM = 256
D_FF = 4 * D_MODEL
CHUNK = 64
EPS = 1e-6
QSCALE = HEAD_DIM ** -0.5
PROJ_W = 4736
TAIL = 4608
TAIL_BLK = TAIL // HEAD_DIM
ROWS = 256
VMEM_LIMIT = 56 * 1024 * 1024

ADAM_LR = 0.001
ADAM_B1 = 0.9
ADAM_B2 = 0.999
ADAM_EPS = 1e-08
ADAM_WD = 0.01
ADAM_STEP = 10

N_DEV = 8
N_CHIP = 4
MESH = pl.DeviceIdType.MESH


def _cparams(sem=None):
    return pltpu.CompilerParams(dimension_semantics=sem, vmem_limit_bytes=VMEM_LIMIT)


def _dot(a, b, ca, cb, hi):
    dims = (((ca,), (cb,)), ((), ()))
    if hi:
        return lax.dot_general(a, b, dims, precision=HI, preferred_element_type=F32)
    return lax.dot_general(a.astype(MM), b.astype(MM), dims, preferred_element_type=F32)


@functools.partial(jax.custom_vjp, nondiff_argnums=(2, 3, 4))
def mmul(a, b, ca, cb, hi):
    return _dot(a, b, ca, cb, hi)


def _mmul_fwd(a, b, ca, cb, hi):
    return _dot(a, b, ca, cb, hi), (a, b)


def _mmul_bwd(ca, cb, hi, res, g):
    a, b = res
    if ca == 1:
        da = _dot(g, b, 1, 1, hi) if cb == 0 else _dot(g, b, 1, 0, hi)
    else:
        da = _dot(b, g, 1, 1, hi) if cb == 0 else _dot(b, g, 0, 1, hi)
    if cb == 0:
        db = _dot(a, g, 0, 0, hi) if ca == 1 else _dot(a, g, 1, 0, hi)
    else:
        db = _dot(g, a, 0, 0, hi) if ca == 1 else _dot(g, a, 0, 1, hi)
    return da.astype(a.dtype), db.astype(b.dtype)


mmul.defvjp(_mmul_fwd, _mmul_bwd)


def _iota2(n, m):
    return lax.broadcasted_iota(jnp.int32, (n, m), 0), lax.broadcasted_iota(jnp.int32, (n, m), 1)


def _same_block(r, c, shift):
    return lax.shift_right_logical(r, shift) == lax.shift_right_logical(c, shift)


def _split_bf16(x):
    hi = x.astype(jnp.bfloat16)
    return hi, (x - hi.astype(F32)).astype(jnp.bfloat16)


def _dot3(a, b, ca, cb):
    dims = (((ca,), (cb,)), ((), ()))
    (ah, al), (bh, bl) = _split_bf16(a), _split_bf16(b)
    d = lambda x, y: lax.dot_general(x, y, dims, preferred_element_type=F32)
    return d(ah, bh) + (d(ah, bl) + d(al, bh))


def _tri_inv_impl(a):
    n = a.shape[0]
    r, c = _iota2(n, n)
    eye = (r == c).astype(F32)
    b16, b32 = _same_block(r, c, 4), _same_block(r, c, 5)
    a0 = jnp.where(b16, a, 0.0)
    p = eye - a0
    b = _dot3(a0, a0, 1, 0)
    p = p + _dot3(p, b, 1, 0)
    b = _dot3(b, b, 1, 0)
    p = p + _dot3(p, b, 1, 0)
    b = _dot3(b, b, 1, 0)
    p = p + _dot3(p, b, 1, 0)
    a1 = jnp.where(jnp.logical_and(b32, jnp.logical_not(b16)), a, 0.0)
    p = p - _dot3(_dot3(p, a1, 1, 0), p, 1, 0)
    a2 = jnp.where(b32, 0.0, a)
    p = p - _dot3(_dot3(p, a2, 1, 0), p, 1, 0)
    return p


@jax.custom_vjp
def tri_inv(a):
    return _tri_inv_impl(a)


def _tri_inv_fwd(a):
    p = _tri_inv_impl(a)
    return p, p


def _tri_inv_bwd(p, g):
    return (-_dot3(_dot3(p, g, 0, 0), p, 1, 1),)


tri_inv.defvjp(_tri_inv_fwd, _tri_inv_bwd)


def _sigmoid(x):
    return 1.0 / (1.0 + jnp.exp(-x))


def _softplus(x):
    return jnp.maximum(x, 0.0) + jnp.log(1.0 + jnp.exp(-jnp.abs(x)))


def _silu(x):
    return x * _sigmoid(x)


def _rms(x, w):
    return x * lax.rsqrt(jnp.mean(x * x, axis=-1, keepdims=True) + EPS) * w


def _bf_round(x):
    return x.astype(MM).astype(F32)


def _acc(ref, val, first):
    @pl.when(first)
    def _():
        ref[...] = val

    @pl.when(jnp.logical_not(first))
    def _():
        ref[...] += val


def _tile(n, pref):
    if n % pref == 0:
        return pref
    return n


def matmul(a, b, *, ta=False, tb=False, b_slots=False, res=None, also_sqrelu=False, times_dsqrelu=None, out_dtype=F32,
           name, tm=1024, tn=1024, tk=1024):
    m, k = (a.shape[1], a.shape[0]) if ta else a.shape
    if b_slots:
        n = b.shape[1] if tb else N_CHIP * b.shape[2]
        assert (N_CHIP * b.shape[2] if tb else b.shape[1]) == k, (a.shape, b.shape, ta, tb)
        tn, tk = (tn, b.shape[2]) if tb else (b.shape[2], tk)
    else:
        n = b.shape[0] if tb else b.shape[1]
        assert (b.shape[1] if tb else b.shape[0]) == k, (a.shape, b.shape, ta, tb)
    tm, tn, tk = _tile(m, tm), _tile(n, tn), _tile(k, tk)
    nk = k // tk
    ca, cb = (0 if ta else 1), (1 if tb else 0)

    extra = tuple(e for e in (res, times_dsqrelu) if e is not None)
    assert len(extra) <= 1

    def body(a_ref, b_ref, *rest):
        e_ref = rest[0] if extra else None
        o_ref = rest[len(extra)]

        def finish(total):
            if res is not None:
                total = total + e_ref[...]
            if times_dsqrelu is not None:
                total = total * (2.0 * jnp.maximum(e_ref[...], 0.0))
            o_ref[...] = total.astype(o_ref.dtype)
            if also_sqrelu:
                rest[len(extra) + 1][...] = _sqrelu(total).astype(MM)

        if nk == 1:
            finish(_dot(a_ref[...], b_ref[...], ca, cb, False))
            return
        acc_ref, kk = rest[-1], pl.program_id(2)

        @pl.when(kk == 0)
        def _():
            acc_ref[...] = jnp.zeros_like(acc_ref)

        acc_ref[...] += _dot(a_ref[...], b_ref[...], ca, cb, False)

        @pl.when(kk == nk - 1)
        def _():
            finish(acc_ref[...])

    a_spec = pl.BlockSpec((tk, tm), lambda i, j, l: (l, i)) if ta else pl.BlockSpec((tm, tk), lambda i, j, l: (i, l))
    if b_slots:
        b_spec = (pl.BlockSpec((None, tn, tk), lambda i, j, l: (l, j, 0)) if tb else
                  pl.BlockSpec((None, tk, tn), lambda i, j, l: (j, l, 0)))
    else:
        b_spec = pl.BlockSpec((tn, tk), lambda i, j, l: (j, l)) if tb else pl.BlockSpec((tk, tn), lambda i, j, l: (l, j))
    o_spec = pl.BlockSpec((tm, tn), lambda i, j, l: (i, j))
    out_shape = [jax.ShapeDtypeStruct((m, n), out_dtype)] + [jax.ShapeDtypeStruct((m, n), MM)] * also_sqrelu
    outs = pl.pallas_call(
        body, name=name, grid=(m // tm, n // tn, nk),
        in_specs=[a_spec, b_spec] + [o_spec] * len(extra), out_specs=[o_spec] * len(out_shape), out_shape=out_shape,
        scratch_shapes=[pltpu.VMEM((tm, tn), F32)] * (nk > 1),
        compiler_params=_cparams(("parallel", "parallel", "arbitrary")),
    )(a, b, *extra)
    return outs if also_sqrelu else outs[0]


def rms_fwd(x, w, *, name):
    t, d = x.shape

    def body(x_ref, w_ref, o_ref):
        o_ref[...] = _rms(x_ref[...], w_ref[...]).astype(o_ref.dtype)

    return pl.pallas_call(
        body, name=name, grid=(t // ROWS,),
        in_specs=[pl.BlockSpec((ROWS, d), lambda i: (i, 0)), pl.BlockSpec((1, d), lambda i: (0, 0))],
        out_specs=pl.BlockSpec((ROWS, d), lambda i: (i, 0)),
        out_shape=jax.ShapeDtypeStruct((t, d), MM), compiler_params=_cparams(("parallel",)),
    )(x, w)


def rms_bwd(x, w, dh, dres, *, name):
    t, d = x.shape

    def body(x_ref, w_ref, dh_ref, dr_ref, dx_ref, dw_ref):
        _, vjp = jax.vjp(_rms, x_ref[...], w_ref[...])
        dx, dw = vjp(dh_ref[...].astype(F32))
        dx_ref[...] = dx + dr_ref[...]
        _acc(dw_ref, dw, pl.program_id(0) == 0)

    row = pl.BlockSpec((ROWS, d), lambda i: (i, 0))
    vec = pl.BlockSpec((1, d), lambda i: (0, 0))
    return pl.pallas_call(
        body, name=name, grid=(t // ROWS,), in_specs=[row, vec, row, row], out_specs=[row, vec],
        out_shape=[jax.ShapeDtypeStruct((t, d), F32), jax.ShapeDtypeStruct((1, d), F32)],
        compiler_params=_cparams(("arbitrary",)),
    )(x, w, dh, dres)


def _sqrelu(x):
    return jnp.square(jnp.maximum(x, 0.0))


def loss_fwd(y, target, *, name):
    t, d = y.shape

    def body(y_ref, t_ref, dy_ref, l_ref):
        e = y_ref[...] - t_ref[...]
        dy_ref[...] = e * (1.0 / d)
        part = 0.5 * jnp.sum(jnp.sum(e * e, axis=-1, keepdims=True) * (1.0 / d), axis=0, keepdims=True)
        _acc(l_ref, jnp.broadcast_to(part, (1, HEAD_DIM)), pl.program_id(0) == 0)

    blk = pl.BlockSpec((ROWS, d), lambda i: (i, 0))
    return pl.pallas_call(
        body, name=name, grid=(t // ROWS,), in_specs=[blk, blk],
        out_specs=[blk, pl.BlockSpec((1, HEAD_DIM), lambda i: (0, 0))],
        out_shape=[jax.ShapeDtypeStruct((t, d), F32), jax.ShapeDtypeStruct((1, HEAD_DIM), F32)],
        compiler_params=_cparams(("arbitrary",)),
    )(y, target)


def _mem_kv(mem, wn, wkn, *ws):
    mn = _rms(mem, wn)
    outs = []
    for h in range(MEM_HEADS):
        outs.append(_rms(mmul(mn, ws[h], 1, 0, False), wkn))
    for h in range(MEM_HEADS):
        outs.append(mmul(mn, ws[MEM_HEADS + h], 1, 0, False))
    return tuple(outs)


def _w_cols(w_ref):
    return [w_ref[:, h * HEAD_DIM:(h + 1) * HEAD_DIM] for h in range(2 * MEM_HEADS)]


def mem_fwd(mem, wn, wkv, wkn):
    def body(mem_ref, wn_ref, w_ref, wkn_ref, k_ref, v_ref):
        outs = _mem_kv(mem_ref[...], wn_ref[...], wkn_ref[...], *_w_cols(w_ref))
        for h in range(MEM_HEADS):
            k_ref[:, h * HEAD_DIM:(h + 1) * HEAD_DIM] = outs[h]
            v_ref[:, h * HEAD_DIM:(h + 1) * HEAD_DIM] = outs[MEM_HEADS + h]

    shp = jax.ShapeDtypeStruct((mem.shape[0], MEM_WIDTH), F32)
    return pl.pallas_call(body, name="mem_fwd", out_shape=[shp, shp], compiler_params=_cparams())(mem, wn, wkv, wkn)


def mem_bwd(mem, wn, wkv, wkn, dk0, dv0, dk1, dv1):
    def body(mem_ref, wn_ref, w_ref, wkn_ref, dk0_ref, dv0_ref, dk1_ref, dv1_ref, dwn_ref, dw_ref, dwkn_ref):
        _, vjp = jax.vjp(lambda wn_, wkn_, *ws: _mem_kv(mem_ref[...], wn_, wkn_, *ws),
                         wn_ref[...], wkn_ref[...], *[w.astype(F32) for w in _w_cols(w_ref)])
        cols = lambda a, b: tuple(a[:, h * HEAD_DIM:(h + 1) * HEAD_DIM] + b[:, h * HEAD_DIM:(h + 1) * HEAD_DIM]
                                  for h in range(MEM_HEADS))
        cts = cols(dk0_ref, dk1_ref) + cols(dv0_ref, dv1_ref)
        grads = vjp(cts)
        dwn_ref[...] = grads[0]
        dwkn_ref[...] = grads[1]
        for h in range(2 * MEM_HEADS):
            dw_ref[:, h * HEAD_DIM:(h + 1) * HEAD_DIM] = grads[2 + h]

    return pl.pallas_call(
        body, name="mem_bwd",
        out_shape=[jax.ShapeDtypeStruct((1, D_MODEL), F32), jax.ShapeDtypeStruct((D_MODEL, 2 * MEM_WIDTH), F32),
                   jax.ShapeDtypeStruct((1, HEAD_DIM), F32)],
        compiler_params=_cparams(),
    )(mem, wn, wkv, wkn, dk0, dv0, dk1, dv1)


def _memattn(q, wq, mk, mv):
    qn = _rms(q, wq) * QSCALE
    s = mmul(qn, mk, 1, 1, False)
    s = s - jnp.max(s, axis=-1, keepdims=True)
    p = jnp.exp(s)
    p = p / jnp.sum(p, axis=-1, keepdims=True)
    return mmul(p, mv, 1, 0, False)


def _lanes(j):
    return slice(j * HEAD_DIM, (j + 1) * HEAD_DIM)


def _memattn_specs(t):
    qspec = pl.BlockSpec((ROWS, MEM_WIDTH), lambda i: (i, (TAIL - MEM_WIDTH) // MEM_WIDTH))
    wspec = pl.BlockSpec((1, HEAD_DIM), lambda i: (0, 0))
    mspec = pl.BlockSpec((N_MEM, MEM_WIDTH), lambda i: (0, 0))
    ospec = pl.BlockSpec((ROWS, MEM_WIDTH), lambda i: (i, 0))
    return qspec, wspec, mspec, ospec


def memattn_fwd(proj, wq, mk, mv, *, name):
    t = proj.shape[0]
    qspec, wspec, mspec, ospec = _memattn_specs(t)

    def body(q_ref, w_ref, k_ref, v_ref, o_ref):
        for h in range(MEM_HEADS):
            o_ref[:, _lanes(h)] = _memattn(q_ref[:, _lanes(h)], w_ref[...], k_ref[:, _lanes(h)],
                                           v_ref[:, _lanes(h)]).astype(o_ref.dtype)

    return pl.pallas_call(
        body, name=name, grid=(t // ROWS,), in_specs=[qspec, wspec, mspec, mspec], out_specs=ospec,
        out_shape=jax.ShapeDtypeStruct((t, MEM_WIDTH), MM), compiler_params=_cparams(("parallel",)),
    )(proj, wq, mk, mv)


def memattn_bwd(proj, wq, mk, mv, dcat, *, name):
    t = proj.shape[0]
    qspec, wspec, mspec, ospec = _memattn_specs(t)
    dospec = pl.BlockSpec((ROWS, MEM_WIDTH), lambda i: (i, D_MODEL // MEM_WIDTH))

    def body(q_ref, w_ref, k_ref, v_ref, do_ref, dq_ref, dw_ref, dk_ref, dv_ref):
        first = pl.program_id(0) == 0
        dw_sum = jnp.zeros((1, HEAD_DIM), F32)
        for h in range(MEM_HEADS):
            _, vjp = jax.vjp(_memattn, q_ref[:, _lanes(h)], w_ref[...], k_ref[:, _lanes(h)], v_ref[:, _lanes(h)])
            dq, dw, dk, dv = vjp(do_ref[:, _lanes(h)].astype(F32))
            dq_ref[:, _lanes(h)] = dq.astype(dq_ref.dtype)
            dw_sum = dw_sum + dw
            _acc(dk_ref.at[:, _lanes(h)], dk, first)
            _acc(dv_ref.at[:, _lanes(h)], dv, first)
        _acc(dw_ref, dw_sum, first)

    mshape = jax.ShapeDtypeStruct((N_MEM, MEM_WIDTH), F32)
    return pl.pallas_call(
        body, name=name, grid=(t // ROWS,), in_specs=[qspec, wspec, mspec, mspec, dospec],
        out_specs=[ospec, wspec, mspec, mspec],
        out_shape=[jax.ShapeDtypeStruct((t, MEM_WIDTH), MM), jax.ShapeDtypeStruct((1, HEAD_DIM), F32), mshape, mshape],
        compiler_params=_cparams(("arbitrary",)),
    )(proj, wq, mk, mv, dcat)


def _shift_rows(x, s, up):
    n = x.shape[0]
    r = lax.broadcasted_iota(jnp.int32, x.shape, 0)
    if up:
        return jnp.where(r < n - s, pltpu.roll(x, n - s, 0), 0.0)
    return jnp.where(r >= s, pltpu.roll(x, s, 0), 0.0)


def _conv_fwd_vals(x, w):
    xb = _bf_round(x)
    wb = _bf_round(w)
    c = xb * wb[3:4, :]
    for j in range(3):
        c = c + _shift_rows(xb, 3 - j, False) * wb[j:j + 1, :]
    return xb, wb, c


def dn_prep_fwd(proj, conv_w):
    t = proj.shape[0]

    def body(x_ref, w_ref, o_ref):
        j = pl.program_id(0)
        _, _, c = _conv_fwd_vals(x_ref[...], w_ref[...])
        s = _silu(c)
        r = lax.rsqrt(jnp.sum(s * s, axis=-1, keepdims=True) + EPS)
        scale = jnp.where(j < N_HEADS, QSCALE, 1.0)
        o_ref[...] = jnp.where(j < 2 * N_HEADS, s * r * scale, s)

    return pl.pallas_call(
        body, name="dn_prep_fwd", grid=(3 * N_HEADS,),
        in_specs=[pl.BlockSpec((t, HEAD_DIM), lambda j: (0, j)), pl.BlockSpec((4, HEAD_DIM), lambda j: (0, j))],
        out_specs=pl.BlockSpec((None, t, HEAD_DIM), lambda j: (j // N_HEADS, 0, j % N_HEADS)),
        out_shape=jax.ShapeDtypeStruct((3, t, D_MODEL), F32), compiler_params=_cparams(("parallel",)),
    )(proj, conv_w)


def dn_prep_bwd(proj, conv_w, dqkv):
    t = proj.shape[0]

    def body(x_ref, w_ref, g_ref, dx_ref, dw_ref):
        j = pl.program_id(0)
        xb, wb, c = _conv_fwd_vals(x_ref[...], w_ref[...])
        sg = _sigmoid(c)
        s = c * sg
        g = g_ref[...]
        r = lax.rsqrt(jnp.sum(s * s, axis=-1, keepdims=True) + EPS)
        scale = jnp.where(j < N_HEADS, QSCALE, 1.0)
        gn = g * scale
        ds_norm = r * gn - s * (r * r * r) * jnp.sum(gn * s, axis=-1, keepdims=True)
        ds = jnp.where(j < 2 * N_HEADS, ds_norm, g)
        dc = ds * (sg + s * (1.0 - sg))
        dx = dc * wb[3:4, :]
        rows = [jnp.sum(dc * xb, axis=0, keepdims=True)]
        for jj in range(2, -1, -1):
            sh = 3 - jj
            dx = dx + _shift_rows(dc, sh, True) * wb[jj:jj + 1, :]
            rows.insert(0, jnp.sum(dc * _shift_rows(xb, sh, False), axis=0, keepdims=True))
        dx_ref[...] = dx.astype(dx_ref.dtype)
        dw_ref[...] = jnp.concatenate(rows + [jnp.zeros((4, HEAD_DIM), F32)], axis=0)

    col = pl.BlockSpec((t, HEAD_DIM), lambda j: (0, j))
    return pl.pallas_call(
        body, name="dn_prep_bwd", grid=(3 * N_HEADS,),
        in_specs=[col, pl.BlockSpec((4, HEAD_DIM), lambda j: (0, j)),
                  pl.BlockSpec((None, t, HEAD_DIM), lambda j: (j // N_HEADS, 0, j % N_HEADS))],
        out_specs=[col, pl.BlockSpec((8, HEAD_DIM), lambda j: (0, j))],
        out_shape=[jax.ShapeDtypeStruct((t, 3 * D_MODEL), MM), jax.ShapeDtypeStruct((8, 3 * D_MODEL), F32)],
        compiler_params=_cparams(("parallel",)),
    )(proj, conv_w, dqkv)


def _tri_ones(n, upper):
    r, c = _iota2(n, n)
    return (r <= c).astype(F32) if upper else (r >= c).astype(F32)


def dn_gates_fwd(proj, a_log, dt_bias):
    t = proj.shape[0]

    def body(x_ref, al_ref, dt_ref, o_ref):
        lane = lax.broadcasted_iota(jnp.int32, (CHUNK, HEAD_DIM), 1)
        tri = _tri_ones(CHUNK, False)

        def step(c, carry):
            rows = pl.ds(pl.multiple_of(c * CHUNK, CHUNK), CHUNK)
            x = x_ref[rows, :]
            g = jnp.where(lane < N_HEADS, -jnp.exp(al_ref[...]) * _softplus(x + dt_ref[...]), 0.0)
            gc = _dot(tri, g, 1, 0, True)
            o_ref[rows, :] = jnp.where(lane < N_HEADS, gc, jnp.where(lane < 2 * N_HEADS, _sigmoid(x), 0.0))
            return carry

        lax.fori_loop(0, t // CHUNK, step, 0)

    vec = pl.BlockSpec((1, HEAD_DIM), lambda i: (0, 0))
    return pl.pallas_call(
        body, name="dn_gates_fwd", grid=(1,),
        in_specs=[pl.BlockSpec((t, HEAD_DIM), lambda i: (0, TAIL_BLK)), vec, vec],
        out_specs=pl.BlockSpec((t, HEAD_DIM), lambda i: (0, 0)),
        out_shape=jax.ShapeDtypeStruct((t, HEAD_DIM), F32), compiler_params=_cparams(("arbitrary",)),
    )(proj, a_log, dt_bias)


def dn_gates_bwd(proj, a_log, dt_bias, dgates):
    t = proj.shape[0]

    def body(x_ref, al_ref, dt_ref, g_ref, dx_ref, dal_ref, ddt_ref):
        lane = lax.broadcasted_iota(jnp.int32, (CHUNK, HEAD_DIM), 1)
        tri = _tri_ones(CHUNK, True)
        dal_ref[...] = jnp.zeros_like(dal_ref)
        ddt_ref[...] = jnp.zeros_like(ddt_ref)

        def step(c, carry):
            rows = pl.ds(pl.multiple_of(c * CHUNK, CHUNK), CHUNK)
            x = x_ref[rows, :]
            dgc = jnp.where(lane < N_HEADS, g_ref[rows, :], 0.0)
            dg = _dot(tri, dgc, 1, 0, True)
            ea = -jnp.exp(al_ref[...])
            z = x + dt_ref[...]
            da = jnp.where(lane < N_HEADS, dg * ea * _sigmoid(z), 0.0)
            gval = jnp.where(lane < N_HEADS, ea * _softplus(z), 0.0)
            beta = _sigmoid(x)
            db = jnp.where(jnp.logical_and(lane >= N_HEADS, lane < 2 * N_HEADS), g_ref[rows, :] * beta * (1.0 - beta), 0.0)
            dx_ref[rows, :] = (da + db).astype(dx_ref.dtype)
            dal_ref[...] += jnp.sum(dg * gval, axis=0, keepdims=True)
            ddt_ref[...] += jnp.sum(da, axis=0, keepdims=True)
            return carry

        lax.fori_loop(0, t // CHUNK, step, 0)

    vec = pl.BlockSpec((1, HEAD_DIM), lambda i: (0, 0))
    full = pl.BlockSpec((t, HEAD_DIM), lambda i: (0, 0))
    return pl.pallas_call(
        body, name="dn_gates_bwd", grid=(1,),
        in_specs=[pl.BlockSpec((t, HEAD_DIM), lambda i: (0, TAIL_BLK)), vec, vec, full],
        out_specs=[full, vec, vec],
        out_shape=[jax.ShapeDtypeStruct((t, HEAD_DIM), MM), jax.ShapeDtypeStruct((1, HEAD_DIM), F32),
                   jax.ShapeDtypeStruct((1, HEAD_DIM), F32)],
        compiler_params=_cparams(("arbitrary",)),
    )(proj, a_log, dt_bias, dgates)


def _dn_intra(q, k, v, gcol, grow, bcol):
    r, c = _iota2(CHUNK, CHUNK)
    causal, strict = r >= c, r > c
    decay = jnp.where(causal, jnp.exp(jnp.where(causal, gcol - grow, 0.0)), 0.0)
    kb = k * bcol
    a = jnp.where(strict, mmul(kb, k, 1, 1, False) * decay, 0.0)
    tm = tri_inv(a)
    u = mmul(tm, v * bcol, 1, 0, False)
    w = mmul(tm, kb * jnp.exp(gcol), 1, 0, False)
    qk = jnp.where(causal, mmul(q, k, 1, 1, False) * decay, 0.0)
    rr = lax.broadcasted_iota(jnp.int32, (CHUNK, 1), 0)
    g_last = jnp.sum(jnp.where(rr == CHUNK - 1, gcol, 0.0), axis=0, keepdims=True)
    return u, w, q * jnp.exp(gcol), k * jnp.exp(g_last - gcol), qk, jnp.exp(g_last)


def _dn_scan(u, w, qg, kd, qk, eg, state):
    v_new = u - mmul(w, state, 1, 0, False)
    out = mmul(qg, state, 1, 0, False) + mmul(qk, v_new, 1, 0, False)
    return out, state * eg + mmul(kd, v_new, 0, 0, False)


DN_HEADS_PER_STEP = 1
DN_GROUP = 8
DN_PARTS = ((CHUNK, HEAD_DIM),) * 4 + ((CHUNK, CHUNK), (1, 1))


def _dn_scratch(hb, nc):
    return [pltpu.VMEM((hb, nc) + shape, F32) for shape in DN_PARTS]


def _dn_part_specs(hb, nc):
    return [pl.BlockSpec((hb, nc) + shape, lambda h: (h, 0, 0, 0)) for shape in DN_PARTS]


def _dn_group(nc):
    return min(DN_GROUP, nc)


def _dn_group_args(refs, j, g, grp):
    q_ref, k_ref, v_ref, gc_ref, gr_ref, bc_ref = refs
    rows = pl.ds(pl.multiple_of(g * (grp * CHUNK), grp * CHUNK), grp * CHUNK)
    cs = pl.ds(g * grp, grp)
    split = lambda ref: ref[rows, _lanes(j)].reshape(grp, CHUNK, HEAD_DIM)
    return split(q_ref), split(k_ref), split(v_ref), gc_ref[j, cs], gr_ref[j, cs], bc_ref[j, cs]


def _dn_intra_all(refs, parts, hb, nc):
    grp = _dn_group(nc)

    def group(g, carry):
        cs = pl.ds(g * grp, grp)
        for j in range(hb):
            for part, val in zip(parts, jax.vmap(_dn_intra)(*_dn_group_args(refs, j, g, grp))):
                part[j, cs] = val
        return carry

    lax.fori_loop(0, nc // grp, group, 0)


def _dn_specs(t):
    nc, hb = t // CHUNK, DN_HEADS_PER_STEP
    head = lambda which: pl.BlockSpec((None, t, hb * HEAD_DIM), lambda h: (which, 0, h))
    flat = pl.BlockSpec((t, hb * HEAD_DIM), lambda h: (0, h))
    col = pl.BlockSpec((hb, nc, CHUNK, 1), lambda h: (h, 0, 0, 0))
    row = pl.BlockSpec((hb, nc, 1, CHUNK), lambda h: (h, 0, 0, 0))
    st = pl.BlockSpec((hb, nc, HEAD_DIM, HEAD_DIM), lambda h: (h, 0, 0, 0))
    return nc, hb, head, flat, col, row, st


def dn_core_fwd(qkv, gcol, grow, bcol):
    t = qkv.shape[1]
    nc, hb, head, flat, col, row, st = _dn_specs(t)

    def body(q_ref, k_ref, v_ref, gc_ref, gr_ref, bc_ref, o_ref, s_ref, *parts):
        _dn_intra_all((q_ref, k_ref, v_ref, gc_ref, gr_ref, bc_ref), parts, hb, nc)

        def step(c, states):
            rows = pl.ds(pl.multiple_of(c * CHUNK, CHUNK), CHUNK)
            new_states = []
            for j in range(hb):
                s_ref[j, c] = states[j]
                out, new_state = _dn_scan(*[part[j, c] for part in parts], states[j])
                o_ref[rows, _lanes(j)] = out
                new_states.append(new_state)
            return tuple(new_states)

        lax.fori_loop(0, nc, step, tuple(jnp.zeros((HEAD_DIM, HEAD_DIM), F32) for _ in range(hb)))

    outs = pl.pallas_call(
        body, name="dn_core_fwd", grid=(N_HEADS // hb,),
        in_specs=[head(0), head(1), head(2), col, row, col], out_specs=[flat, st] + _dn_part_specs(hb, nc),
        out_shape=[jax.ShapeDtypeStruct((t, D_MODEL), F32), jax.ShapeDtypeStruct((N_HEADS, nc, HEAD_DIM, HEAD_DIM), F32)]
        + [jax.ShapeDtypeStruct((N_HEADS, nc) + shape, F32) for shape in DN_PARTS],
        compiler_params=_cparams(("parallel",)),
    )(qkv, qkv, qkv, gcol, grow, bcol)
    return outs[0], outs[1], tuple(outs[2:])


def dn_core_bwd(qkv, gcol, grow, bcol, states, parts, do):
    t = qkv.shape[1]
    nc, hb, head, flat, col, row, st = _dn_specs(t)
    n_parts = len(DN_PARTS)

    def body(q_ref, k_ref, v_ref, gc_ref, gr_ref, bc_ref, s_ref, do_ref, *rest):
        parts, (dqkv_ref, dgc_ref, dgr_ref, dbc_ref), dparts = rest[:n_parts], rest[n_parts:n_parts + 4], rest[n_parts + 4:]
        refs = (q_ref, k_ref, v_ref, gc_ref, gr_ref, bc_ref)

        def step(i, dstates):
            c = nc - 1 - i
            rows = pl.ds(pl.multiple_of(c * CHUNK, CHUNK), CHUNK)
            dstates_in = []
            for j in range(hb):
                _, vjp = jax.vjp(_dn_scan, *[part[j, c] for part in parts], s_ref[j, c])
                *dvals, dstate_in = vjp((do_ref[rows, _lanes(j)], dstates[j]))
                for dpart, dval in zip(dparts, dvals):
                    dpart[j, c] = dval
                dstates_in.append(dstate_in)
            return tuple(dstates_in)

        lax.fori_loop(0, nc, step, tuple(jnp.zeros((HEAD_DIM, HEAD_DIM), F32) for _ in range(hb)))

        grp = _dn_group(nc)

        def group(g, carry):
            rows = pl.ds(pl.multiple_of(g * (grp * CHUNK), grp * CHUNK), grp * CHUNK)
            cs = pl.ds(g * grp, grp)
            for j in range(hb):
                _, vjp = jax.vjp(jax.vmap(_dn_intra), *_dn_group_args(refs, j, g, grp))
                dq, dk, dv, dgc, dgr, dbc = vjp(tuple(dpart[j, cs] for dpart in dparts))
                for which, val in enumerate((dq, dk, dv)):
                    dqkv_ref[which, rows, _lanes(j)] = val.reshape(grp * CHUNK, HEAD_DIM)
                dgc_ref[j, cs] = dgc
                dgr_ref[j, cs] = dgr
                dbc_ref[j, cs] = dbc
            return carry

        lax.fori_loop(0, nc // grp, group, 0)

    return pl.pallas_call(
        body, name="dn_core_bwd", grid=(N_HEADS // hb,), scratch_shapes=_dn_scratch(hb, nc),
        in_specs=[head(0), head(1), head(2), col, row, col, st, flat] + _dn_part_specs(hb, nc),
        out_specs=[pl.BlockSpec((3, t, hb * HEAD_DIM), lambda h: (0, 0, h)), col, row, col],
        out_shape=[jax.ShapeDtypeStruct((3, t, D_MODEL), F32)] + [
            jax.ShapeDtypeStruct((N_HEADS, nc, CHUNK, 1), F32), jax.ShapeDtypeStruct((N_HEADS, nc, 1, CHUNK), F32),
            jax.ShapeDtypeStruct((N_HEADS, nc, CHUNK, 1), F32)],
        compiler_params=_cparams(("parallel",)),
    )(qkv, qkv, qkv, gcol, grow, bcol, states, do, *parts)


def gates_to_heads(gates):
    t = gates.shape[0]
    nc = t // CHUNK
    g = gates[:, :N_HEADS].T.reshape(N_HEADS, nc, CHUNK)
    b = gates[:, N_HEADS:2 * N_HEADS].T.reshape(N_HEADS, nc, CHUNK)
    return g[..., None], g[:, :, None, :], b[..., None]


def heads_to_gates(dgcol, dgrow, dbcol):
    nh, nc = dgcol.shape[:2]
    dg = (dgcol[..., 0] + dgrow[:, :, 0, :]).reshape(nh, nc * CHUNK).T
    db = dbcol[..., 0].reshape(nh, nc * CHUNK).T
    return jnp.concatenate([dg, db, jnp.zeros((nc * CHUNK, HEAD_DIM - 2 * nh), F32)], axis=1)


def _dn_out(o, z, w):
    return _rms(o, w) * _silu(z)


def _gate_specs():
    o_spec = pl.BlockSpec((ROWS, D_MODEL), lambda i: (i, 0))
    z_spec = pl.BlockSpec((ROWS, D_MODEL), lambda i: (i, 3))
    w_spec = pl.BlockSpec((1, HEAD_DIM), lambda i: (0, 0))
    return o_spec, z_spec, w_spec


def dn_out_fwd(o, proj, w):
    t = o.shape[0]
    o_spec, z_spec, w_spec = _gate_specs()

    def body(o_ref, z_ref, w_ref, y_ref):
        for h in range(N_HEADS):
            y_ref[:, _lanes(h)] = _dn_out(o_ref[:, _lanes(h)], z_ref[:, _lanes(h)], w_ref[...]).astype(y_ref.dtype)

    return pl.pallas_call(
        body, name="dn_out_fwd", grid=(t // ROWS,), in_specs=[o_spec, z_spec, w_spec], out_specs=o_spec,
        out_shape=jax.ShapeDtypeStruct((t, D_MODEL), MM), compiler_params=_cparams(("parallel",)),
    )(o, proj, w)


def dn_out_bwd(o, proj, w, dcat):
    t = o.shape[0]
    o_spec, z_spec, w_spec = _gate_specs()

    def body(o_ref, z_ref, w_ref, g_ref, do_ref, dz_ref, dw_ref):
        dw_sum = jnp.zeros((1, HEAD_DIM), F32)
        for h in range(N_HEADS):
            _, vjp = jax.vjp(_dn_out, o_ref[:, _lanes(h)], z_ref[:, _lanes(h)], w_ref[...])
            do, dz, dw = vjp(g_ref[:, _lanes(h)].astype(F32))
            do_ref[:, _lanes(h)] = do
            dz_ref[:, _lanes(h)] = dz.astype(dz_ref.dtype)
            dw_sum = dw_sum + dw
        _acc(dw_ref, dw_sum, pl.program_id(0) == 0)

    return pl.pallas_call(
        body, name="dn_out_bwd", grid=(t // ROWS,), in_specs=[o_spec, z_spec, w_spec, o_spec],
        out_specs=[o_spec, o_spec, w_spec],
        out_shape=[jax.ShapeDtypeStruct((t, D_MODEL), F32), jax.ShapeDtypeStruct((t, D_MODEL), MM),
                   jax.ShapeDtypeStruct((1, HEAD_DIM), F32)],
        compiler_params=_cparams(("arbitrary",)),
    )(o, proj, w, dcat)


def _fox_norm(x, w, scale):
    return _rms(x, w) * scale


def _fox_prep_specs():
    x_spec = pl.BlockSpec((ROWS, 2 * D_MODEL), lambda i: (i, 0))
    w_spec = pl.BlockSpec((2, 1, HEAD_DIM), lambda i: (0, 0, 0))
    y_spec = pl.BlockSpec((2, ROWS, D_MODEL), lambda i: (0, i, 0))
    return x_spec, w_spec, y_spec


def fox_prep_fwd(proj, wqk):
    t = proj.shape[0]
    x_spec, w_spec, y_spec = _fox_prep_specs()

    def body(x_ref, w_ref, y_ref):
        for j in range(2 * N_HEADS):
            which, scale = j // N_HEADS, (QSCALE if j < N_HEADS else 1.0)
            y_ref[which, :, _lanes(j % N_HEADS)] = _fox_norm(x_ref[:, _lanes(j)], w_ref[which], scale).astype(y_ref.dtype)

    return pl.pallas_call(
        body, name="fox_prep_fwd", grid=(t // ROWS,), in_specs=[x_spec, w_spec], out_specs=y_spec,
        out_shape=jax.ShapeDtypeStruct((2, t, D_MODEL), MM), compiler_params=_cparams(("parallel",)),
    )(proj, wqk)


def fox_prep_bwd(proj, wqk, dq, dk):
    t = proj.shape[0]
    x_spec, w_spec, _ = _fox_prep_specs()
    g_spec = pl.BlockSpec((ROWS, D_MODEL), lambda i: (i, 0))

    def body(x_ref, w_ref, dq_ref, dk_ref, dx_ref, dw_ref):
        dws = [jnp.zeros((1, HEAD_DIM), F32), jnp.zeros((1, HEAD_DIM), F32)]
        for j in range(2 * N_HEADS):
            which, scale = j // N_HEADS, (QSCALE if j < N_HEADS else 1.0)
            g_ref = dq_ref if which == 0 else dk_ref
            _, vjp = jax.vjp(lambda x, w: _fox_norm(x, w, scale), x_ref[:, _lanes(j)], w_ref[which])
            dx, dw = vjp(g_ref[:, _lanes(j % N_HEADS)])
            dx_ref[:, _lanes(j)] = dx.astype(dx_ref.dtype)
            dws[which] = dws[which] + dw
        first = pl.program_id(0) == 0
        _acc(dw_ref.at[0], dws[0], first)
        _acc(dw_ref.at[1], dws[1], first)

    return pl.pallas_call(
        body, name="fox_prep_bwd", grid=(t // ROWS,), in_specs=[x_spec, w_spec, g_spec, g_spec],
        out_specs=[x_spec, w_spec],
        out_shape=[jax.ShapeDtypeStruct((t, 2 * D_MODEL), MM), jax.ShapeDtypeStruct((2, 1, HEAD_DIM), F32)],
        compiler_params=_cparams(("arbitrary",)),
    )(proj, wqk, dq, dk)


def _row_pick(x, i):
    r = lax.broadcasted_iota(jnp.int32, x.shape, 0)
    return jnp.sum(jnp.where(r == i, x, 0.0), axis=0, keepdims=True)


def fox_gates_fwd(proj, f_bias):
    t = proj.shape[0]
    blk = HEAD_DIM

    def body(x_ref, b_ref, o_ref):
        lane = lax.broadcasted_iota(jnp.int32, (blk, HEAD_DIM), 1)
        tri = _tri_ones(blk, False)

        def step(c, carry):
            rows = pl.ds(pl.multiple_of(c * blk, blk), blk)
            lf = jnp.where(lane < N_HEADS, -_softplus(-(x_ref[rows, :] + b_ref[...])), 0.0)
            cum = _dot(tri, lf, 1, 0, True) + carry
            o_ref[rows, :] = cum
            return _row_pick(cum, blk - 1)

        lax.fori_loop(0, t // blk, step, jnp.zeros((1, HEAD_DIM), F32))

    vec = pl.BlockSpec((1, HEAD_DIM), lambda i: (0, 0))
    return pl.pallas_call(
        body, name="fox_gates_fwd", grid=(1,),
        in_specs=[pl.BlockSpec((t, HEAD_DIM), lambda i: (0, TAIL_BLK)), vec],
        out_specs=pl.BlockSpec((t, HEAD_DIM), lambda i: (0, 0)),
        out_shape=jax.ShapeDtypeStruct((t, HEAD_DIM), F32), compiler_params=_cparams(("arbitrary",)),
    )(proj, f_bias)


def fox_gates_bwd(proj, f_bias, dfcum):
    t = proj.shape[0]
    blk = HEAD_DIM
    nb = t // blk

    def body(x_ref, b_ref, g_ref, dx_ref, db_ref):
        lane = lax.broadcasted_iota(jnp.int32, (blk, HEAD_DIM), 1)
        tri = _tri_ones(blk, True)
        db_ref[...] = jnp.zeros_like(db_ref)

        def step(i, carry):
            c = nb - 1 - i
            rows = pl.ds(pl.multiple_of(c * blk, blk), blk)
            g = jnp.where(lane < N_HEADS, g_ref[rows, :], 0.0)
            dlf = _dot(tri, g, 1, 0, True) + carry
            dx = jnp.where(lane < N_HEADS, dlf * _sigmoid(-(x_ref[rows, :] + b_ref[...])), 0.0)
            dx_ref[rows, :] = dx.astype(dx_ref.dtype)
            db_ref[...] += jnp.sum(dx, axis=0, keepdims=True)
            return carry + jnp.sum(g, axis=0, keepdims=True)

        lax.fori_loop(0, nb, step, jnp.zeros((1, HEAD_DIM), F32))

    vec = pl.BlockSpec((1, HEAD_DIM), lambda i: (0, 0))
    full = pl.BlockSpec((t, HEAD_DIM), lambda i: (0, 0))
    return pl.pallas_call(
        body, name="fox_gates_bwd", grid=(1,),
        in_specs=[pl.BlockSpec((t, HEAD_DIM), lambda i: (0, TAIL_BLK)), vec, full], out_specs=[full, vec],
        out_shape=[jax.ShapeDtypeStruct((t, HEAD_DIM), MM), jax.ShapeDtypeStruct((1, HEAD_DIM), F32)],
        compiler_params=_cparams(("arbitrary",)),
    )(proj, f_bias, dfcum)


def fcum_to_heads(fcum):
    f = fcum[:, :N_HEADS].T
    return f[:, :, None], f[:, None, :]


def heads_to_fcum(dfcol, dfrow):
    d = (dfcol[:, :, 0] + dfrow[:, 0, :]).T
    return jnp.concatenate([d, jnp.zeros((d.shape[0], HEAD_DIM - N_HEADS), F32)], axis=1)


def _fox_tq(t):
    return min(t, 256)


def _fox_specs(t):
    tq = _fox_tq(t)
    q_spec = pl.BlockSpec((None, tq, HEAD_DIM), lambda h, i: (0, i, h))
    k_spec = pl.BlockSpec((None, t, HEAD_DIM), lambda h, i: (1, 0, h))
    v_spec = pl.BlockSpec((t, HEAD_DIM), lambda h, i: (0, 2 * N_HEADS + h))
    gate_spec = pl.BlockSpec((tq, HEAD_DIM), lambda h, i: (i, 3 * N_HEADS + h))
    col_spec = pl.BlockSpec((None, tq, 1), lambda h, i: (h, i, 0))
    row_spec = pl.BlockSpec((None, 1, t), lambda h, i: (h, 0, 0))
    blk_spec = pl.BlockSpec((tq, HEAD_DIM), lambda h, i: (i, h))
    head_spec = pl.BlockSpec((t, HEAD_DIM), lambda h, i: (0, h))
    return tq, q_spec, k_spec, v_spec, gate_spec, col_spec, row_spec, blk_spec, head_spec


def _fox_scores(q, k, fcol, frow, i, tq, t):
    s = _dot(q, k, 1, 1, False) + (fcol - frow)
    r = lax.broadcasted_iota(jnp.int32, (tq, t), 0) + i * tq
    c = lax.broadcasted_iota(jnp.int32, (tq, t), 1)
    return s, c <= r


def fox_attn_fwd(qk, proj, fcol, frow):
    t = proj.shape[0]
    tq, q_spec, k_spec, v_spec, gate_spec, col_spec, row_spec, blk_spec, _ = _fox_specs(t)

    def body(q_ref, k_ref, v_ref, gate_ref, fc_ref, fr_ref, mix_ref, o_ref, lse_ref):
        def block(i):
            w = (i + 1) * tq
            s, mask = _fox_scores(q_ref[...], k_ref[0:w, :], fc_ref[...], fr_ref[:, 0:w], i, tq, w)
            s = jnp.where(mask, s, -1e30)
            m = jnp.max(s, axis=-1, keepdims=True)
            p = jnp.where(mask, jnp.exp(s - m), 0.0)
            l = jnp.sum(p, axis=-1, keepdims=True)
            o = _dot(p, v_ref[0:w, :], 1, 0, False) / l
            o_ref[...] = o
            mix_ref[...] = (o * _sigmoid(gate_ref[...])).astype(mix_ref.dtype)
            lse_ref[...] = m + jnp.log(l)

        for i in range(t // tq):
            pl.when(pl.program_id(1) == i)(functools.partial(block, i))

    return pl.pallas_call(
        body, name="fox_attn_fwd", grid=(N_HEADS, t // tq),
        in_specs=[q_spec, k_spec, v_spec, gate_spec, col_spec, row_spec], out_specs=[blk_spec, blk_spec, col_spec],
        out_shape=[jax.ShapeDtypeStruct((t, D_MODEL), MM), jax.ShapeDtypeStruct((t, D_MODEL), F32),
                   jax.ShapeDtypeStruct((N_HEADS, t, 1), F32)],
        compiler_params=_cparams(("parallel", "parallel")),
    )(qk, qk, proj, proj, fcol, frow)


def fox_attn_bwd(qk, proj, fcol, frow, o, lse, dcat):
    t = proj.shape[0]
    tq, q_spec, k_spec, v_spec, gate_spec, col_spec, row_spec, blk_spec, head_spec = _fox_specs(t)

    def body(q_ref, k_ref, v_ref, gate_ref, fc_ref, fr_ref, o_ref, lse_ref, g_ref,
             dq_ref, dk_ref, dv_ref, dgate_ref, dfc_ref, dfr_ref):
        @pl.when(pl.program_id(1) == 0)
        def _():
            dk_ref[...] = jnp.zeros_like(dk_ref)
            dv_ref[...] = jnp.zeros_like(dv_ref)
            dfr_ref[...] = jnp.zeros_like(dfr_ref)

        def block(i):
            w = (i + 1) * tq
            sg = _sigmoid(gate_ref[...])
            g = g_ref[...].astype(F32)
            o_pre = o_ref[...]
            do = g * sg
            dgate_ref[...] = (g * o_pre * sg * (1.0 - sg)).astype(dgate_ref.dtype)
            s, mask = _fox_scores(q_ref[...], k_ref[0:w, :], fc_ref[...], fr_ref[:, 0:w], i, tq, w)
            p = jnp.where(mask, jnp.exp(jnp.where(mask, s, 0.0) - lse_ref[...]), 0.0)
            dp = _dot(do, v_ref[0:w, :], 1, 1, False)
            delta = jnp.sum(do * o_pre, axis=-1, keepdims=True)
            ds = p * (dp - delta)
            dq_ref[...] = _dot(ds, k_ref[0:w, :], 1, 0, False)
            dk_ref[0:w, :] += _dot(ds, q_ref[...], 0, 0, False)
            dv_ref[0:w, :] += _dot(p, do, 0, 0, False)
            dfc_ref[...] = jnp.sum(ds, axis=-1, keepdims=True)
            dfr_ref[:, 0:w] += -jnp.sum(ds, axis=0, keepdims=True)

        for i in range(t // tq):
            pl.when(pl.program_id(1) == i)(functools.partial(block, i))

    f32 = lambda *s: jax.ShapeDtypeStruct(s, F32)
    return pl.pallas_call(
        body, name="fox_attn_bwd", grid=(N_HEADS, t // tq),
        in_specs=[q_spec, k_spec, v_spec, gate_spec, col_spec, row_spec, blk_spec, col_spec, blk_spec],
        out_specs=[blk_spec, head_spec, head_spec, blk_spec, col_spec, row_spec],
        out_shape=[f32(t, D_MODEL), f32(t, D_MODEL), f32(t, D_MODEL), jax.ShapeDtypeStruct((t, D_MODEL), MM),
                   f32(N_HEADS, t, 1), f32(N_HEADS, 1, t)],
        compiler_params=_cparams(("parallel", "arbitrary")),
    )(qk, qk, proj, proj, fcol, frow, o, lse, dcat)


def adamw(w, g, m, v, *, name):
    r, c = w.shape
    rb = ROWS if r % ROWS == 0 else r

    def body(w_ref, g_ref, m_ref, v_ref, d_ref, nm_ref, nv_ref):
        g_ = g_ref[...]
        m_ = ADAM_B1 * m_ref[...] + (1.0 - ADAM_B1) * g_
        v_ = ADAM_B2 * v_ref[...] + (1.0 - ADAM_B2) * jnp.square(g_)
        m_hat = m_ / (1.0 - ADAM_B1 ** ADAM_STEP)
        v_hat = v_ / (1.0 - ADAM_B2 ** ADAM_STEP)
        d_ref[...] = -ADAM_LR * (m_hat / (jnp.sqrt(v_hat) + ADAM_EPS) + ADAM_WD * w_ref[...])
        nm_ref[...] = m_
        nv_ref[...] = v_

    blk = pl.BlockSpec((rb, c), lambda i: (i, 0))
    shp = jax.ShapeDtypeStruct((r, c), F32)
    return pl.pallas_call(body, name=name, grid=(r // rb,), in_specs=[blk] * 4, out_specs=[blk] * 3,
                          out_shape=[shp] * 3, compiler_params=_cparams(("parallel",)))(w, g, m, v)


def _place():
    x, y, c = lax.axis_index("x"), lax.axis_index("y"), lax.axis_index("c")
    return x, y, c, [(1 - x, y), (x, 1 - y), (1 - x, 1 - y)]


ANY = pl.BlockSpec(memory_space=pl.ANY)


def all_reduce_small(v):
    r, w = v.shape

    def body(v_ref, o_ref, buf, send_sems, recv_sems):
        x, y, c, _ = _place()
        me = 4 * x + 2 * y + c
        flip = lambda a, bit: 1 - a if bit else a
        cps = []
        for k in range(1, N_DEV):
            peer = (flip(x, k & 4), flip(y, k & 2), flip(c, k & 1))
            cp = pltpu.make_async_remote_copy(src_ref=v_ref, dst_ref=buf.at[me], send_sem=send_sems.at[k - 1],
                                              recv_sem=recv_sems.at[k - 1], device_id=peer, device_id_type=MESH)
            cp.start()
            cps.append((cp, 4 * peer[0] + 2 * peer[1] + peer[2]))
        buf[me] = v_ref[...]
        for k, (cp, peer_id) in enumerate(cps):
            pltpu.make_async_remote_copy(src_ref=v_ref, dst_ref=buf.at[peer_id], send_sem=send_sems.at[k],
                                         recv_sem=recv_sems.at[k], device_id=(x, y, c), device_id_type=MESH).wait_recv()
        for cp, _ in cps:
            cp.wait_send()
        acc = buf[0]
        for d in range(1, N_DEV):
            acc = acc + buf[d]
        o_ref[...] = acc

    vm = pl.BlockSpec(memory_space=pltpu.VMEM)
    return pl.pallas_call(
        body, name="all_reduce_small", in_specs=[vm], out_specs=vm, out_shape=jax.ShapeDtypeStruct((r, w), F32),
        scratch_shapes=[pltpu.VMEM((N_DEV, r, w), F32), pltpu.SemaphoreType.DMA((N_DEV - 1,)),
                        pltpu.SemaphoreType.DMA((N_DEV - 1,))],
    )(v)


def _vec8(v):
    return jnp.zeros((1, HEAD_DIM), F32).at[0, :N_HEADS].set(v.reshape(N_HEADS))


def _layer_fwd(i, x_in, wt, sm, mem_k, mem_v, late=None):
    tag = f"l{i}_"
    h = rms_fwd(x_in, sm["norm1_w"][i][None], name=tag + "rms1")
    w_in = wt["dn_w_in"] if i == 0 else wt["fox_w_in"]
    proj = matmul(h, w_in, name=tag + "proj", tm=256, tk=1024)
    sv = dict(x_in=x_in, h=h, proj=proj)
    if i == 0:
        qkv = dn_prep_fwd(proj, wt["conv_w"])
        gates = dn_gates_fwd(proj, _vec8(sm["dn_a_log"]), _vec8(sm["dn_dt_bias"]))
        gcol, grow, bcol = gates_to_heads(gates)
        o, states, parts = dn_core_fwd(qkv, gcol, grow, bcol)
        mix = dn_out_fwd(o, proj, sm["dn_o_norm_w"])
        sv.update(qkv=qkv, gcol=gcol, grow=grow, bcol=bcol, states=states, parts=parts, o=o)
    else:
        wqk = jnp.stack([sm["fox_q_norm_w"], sm["fox_k_norm_w"]])
        qk = fox_prep_fwd(proj, wqk)
        fcum = fox_gates_fwd(proj, _vec8(sm["fox_f_bias"]))
        fcol, frow = fcum_to_heads(fcum)
        mix, o, lse = fox_attn_fwd(qk, proj, fcol, frow)
        sv.update(wqk=wqk, qk=qk, fcol=fcol, frow=frow, o=o, lse=lse)
    mem_out = memattn_fwd(proj, sm["memq_norm_w"][i][None], mem_k, mem_v, name=tag + "memattn_fwd")
    cat = jnp.concatenate([mix, mem_out], axis=1)
    if late is not None:
        wt.update(late(cat))
    x_mid = matmul(cat, wt["w_out"][i], res=x_in, name=tag + "out_proj")
    h2 = rms_fwd(x_mid, sm["norm2_w"][i][None], name=tag + "rms2")
    ff, act = matmul(h2, wt["w_mlp1"][i], b_slots=True, also_sqrelu=True, name=tag + "mlp1")
    x_out = matmul(act, wt["w_mlp2"][i], res=x_mid, name=tag + "mlp2")
    sv.update(cat=cat, x_mid=x_mid, h2=h2, ff=ff, act=act)
    return x_out, sv


def _layer_bwd(i, dx_out, sv, wt, sm, mem_k, mem_v, on_mlp=None):
    tag = f"l{i}_"
    big, small = {}, {}
    dff = matmul(dx_out, wt["w_mlp2"][i], tb=True, times_dsqrelu=sv["ff"], out_dtype=MM, name=tag + "d_ff")
    big["w_mlp2"] = matmul(sv["act"], dx_out, ta=True, name=tag + "d_w_mlp2", tk=2048)
    dh2 = matmul(dff, wt["w_mlp1"][i], tb=True, b_slots=True, name=tag + "d_h2")
    big["w_mlp1"] = matmul(sv["h2"], dff, ta=True, name=tag + "d_w_mlp1", tm=512, tn=D_FF, tk=512)
    norm2_w = sm["norm2_w"][i][None]
    if on_mlp is not None:
        norm2_w = norm2_w + on_mlp(big["w_mlp2"], big["w_mlp1"])
    dx_mid, small["norm2_w"] = rms_bwd(sv["x_mid"], norm2_w, dh2, dx_out, name=tag + "rms2_bwd")
    dcat = matmul(dx_mid, wt["w_out"][i], tb=True, name=tag + "d_cat")
    big["w_out"] = matmul(sv["cat"], dx_mid, ta=True, name=tag + "d_w_out", tk=2048)
    proj = sv["proj"]
    dqm, small["memq_norm_w"], dmk, dmv = memattn_bwd(proj, sm["memq_norm_w"][i][None], mem_k, mem_v, dcat,
                                                      name=tag + "memattn_bwd")
    t = proj.shape[0]
    pad = jnp.zeros((t, PROJ_W - TAIL - HEAD_DIM), MM)
    if i == 0:
        do, dz, small["dn_o_norm_w"] = dn_out_bwd(sv["o"], proj, sm["dn_o_norm_w"], dcat)
        dqkv, dgc, dgr, dbc = dn_core_bwd(sv["qkv"], sv["gcol"], sv["grow"], sv["bcol"], sv["states"], sv["parts"], do)
        dtail, dal, ddt = dn_gates_bwd(proj, _vec8(sm["dn_a_log"]), _vec8(sm["dn_dt_bias"]), heads_to_gates(dgc, dgr, dbc))
        dmain, dconv = dn_prep_bwd(proj, wt["conv_w"], dqkv)
        small["dn_a_log"], small["dn_dt_bias"] = dal[:, :N_HEADS], ddt[:, :N_HEADS]
        big["conv_w"] = dconv[:4]
        dproj = jnp.concatenate([dmain, dz, dqm, dtail, pad], axis=1)
    else:
        dq, dk, dv, dgate, dfc, dfr = fox_attn_bwd(sv["qk"], proj, sv["fcol"], sv["frow"], sv["o"], sv["lse"], dcat)
        dtail, dfb = fox_gates_bwd(proj, _vec8(sm["fox_f_bias"]), heads_to_fcum(dfc, dfr))
        dqk, dwqk = fox_prep_bwd(proj, sv["wqk"], dq, dk)
        small["fox_f_bias"] = dfb[:, :N_HEADS]
        small["fox_q_norm_w"], small["fox_k_norm_w"] = dwqk[0], dwqk[1]
        dproj = jnp.concatenate([dqk, dv.astype(MM), dgate, dqm, dtail, pad], axis=1)
    w_in = wt["dn_w_in"] if i == 0 else wt["fox_w_in"]
    dh = matmul(dproj, w_in, tb=True, name=tag + "d_h", tm=512)
    big["w_in"] = matmul(sv["h"], dproj, ta=True, name=tag + "d_w_in", tm=256)
    dx_in, small["norm1_w"] = rms_bwd(sv["x_in"], sm["norm1_w"][i][None], dh, dx_mid, name=tag + "rms1_bwd")
    return dx_in, big, small, (dmk, dmv)


def local_step(x, mem, target, wt, sm, late=None, on_layer1=None, on_mlp0=None):
    wt = dict(wt)
    mem_k, mem_v = mem_fwd(mem, sm["mem_norm_w"][None], wt["w_mem_kv"], sm["mem_k_norm_w"][None])
    x0, sv0 = _layer_fwd(0, x, wt, sm, mem_k, mem_v, late)
    x1, sv1 = _layer_fwd(1, x0, wt, sm, mem_k, mem_v)
    dy, loss = loss_fwd(x1, target, name="loss")
    dx1, big1, small1, dm1 = _layer_bwd(1, dy, sv1, wt, sm, mem_k, mem_v)
    if on_layer1 is not None:
        dx1 = dx1 + on_layer1(big1)
    dx0, big0, small0, dm0 = _layer_bwd(0, dx1, sv0, wt, sm, mem_k, mem_v, on_mlp0)
    dwn, dwkv, dwkn = mem_bwd(mem, sm["mem_norm_w"][None], wt["w_mem_kv"], sm["mem_k_norm_w"][None], *dm0, *dm1)
    small = dict(mem_norm_w=dwn[0], mem_k_norm_w=dwkn[0],
                 norm1_w=jnp.concatenate([small0["norm1_w"], small1["norm1_w"]]),
                 norm2_w=jnp.concatenate([small0["norm2_w"], small1["norm2_w"]]),
                 memq_norm_w=jnp.concatenate([small0["memq_norm_w"], small1["memq_norm_w"]]),
                 dn_a_log=small0["dn_a_log"], dn_dt_bias=small0["dn_dt_bias"], dn_o_norm_w=small0["dn_o_norm_w"],
                 fox_f_bias=small1["fox_f_bias"], fox_q_norm_w=small1["fox_q_norm_w"], fox_k_norm_w=small1["fox_k_norm_w"])
    big = dict(w_mem_kv=dwkv, dn_w_in=big0["w_in"], fox_w_in=big1["w_in"], conv_w=big0["conv_w"],
               w_out=[big0["w_out"], big1["w_out"]], w_mlp1=[big0["w_mlp1"], big1["w_mlp1"]],
               w_mlp2=[big0["w_mlp2"], big1["w_mlp2"]])
    return loss, dx0, big, small


def w_in_to_kernel(w, n_scalars):
    pad = jnp.zeros((w.shape[0], PROJ_W - TAIL - n_scalars), w.dtype)
    return jnp.concatenate([w[:, :4096], w[:, 4096 + n_scalars:], w[:, 4096:4096 + n_scalars], pad], axis=1)


def w_in_from_kernel(w, n_scalars):
    return jnp.concatenate([w[:, :4096], w[:, TAIL:TAIL + n_scalars], w[:, 4096:TAIL]], axis=1)


BIG_SPECS = dict(w_mem_kv=("rows", 1, 256, 1024), w_out=("rows", 2, 384, 1024), w_mlp2=("rows", 2, 1024, 1024),
                 w_mlp1=("cols", 2, 1024, 1024), dn_w_in=("rows", 1, 1024, 1156), fox_w_in=("rows", 1, 1024, 1154))
BIG_NAMES = tuple(BIG_SPECS)
EARLY_NAMES = ("w_mem_kv", "dn_w_in")
LATE_NAMES = ("w_out", "w_mlp2", "w_mlp1", "fox_w_in")
BIG_SPECS.update({f"{name}_{i}": (BIG_SPECS[name][0], 1) + BIG_SPECS[name][2:]
                  for name in ("w_out", "w_mlp2", "w_mlp1") for i in range(2)})
RS_LAYER1 = ("fox_w_in", "w_out_1", "w_mlp2_1", "w_mlp1_1")
RS_MLP0 = ("w_mlp2_0", "w_mlp1_0")
RS_LAST = ("w_out_0", "dn_w_in", "w_mem_kv")


def _full_shape(name, half=False):
    kind, a, b, c = BIG_SPECS[name]
    b = b // 2 if half else b
    return (a, N_CHIP, b, c) if kind == "rows" else (a, b, N_CHIP * c)


def _ds(start, size, align):
    return pl.ds(start if isinstance(start, int) else pl.multiple_of(start, align), size)


def _half_rows(name, h):
    b = BIG_SPECS[name][2]
    return _ds(h * (b // 2), b // 2, 16)


def _shard_idx(name, h):
    return (slice(None), _half_rows(name, h), slice(None))


def _full_idx(name, j=None, h=None):
    kind, _, _, c = BIG_SPECS[name]
    rows = slice(None) if h is None else _half_rows(name, h)
    if kind == "rows":
        return (slice(None), slice(None) if j is None else j, rows, slice(None))
    return (slice(None), rows, slice(None) if j is None else _ds(j * c, c, 128))


def _slots_shape(name):
    _, a, b, c = BIG_SPECS[name]
    return (a, N_CHIP, b, c)


def _slots_idx(name, j, h):
    return (slice(None), j, _half_rows(name, h), slice(None))


def _row_block(name):
    hs = BIG_SPECS[name][2] // 2
    return hs if hs <= ROWS else ROWS


def _remote(src, dst, send_sem, recv_sem, to):
    return pltpu.make_async_remote_copy(src_ref=src, dst_ref=dst, send_sem=send_sem, recv_sem=recv_sem, device_id=to,
                                        device_id_type=MESH)


HBM = pl.BlockSpec(memory_space=pltpu.HBM)
SEM = pl.BlockSpec(memory_space=pltpu.SEMAPHORE)
EFFECT = pltpu.CompilerParams(has_side_effects=pltpu.SideEffectType.DATAFLOW_SIDE_EFFECTING)


def _in_hbm(a):
    return pltpu.with_memory_space_constraint(a, pltpu.HBM)


def _chip_copies(names, ins, lands, send_sems, recv_sems):
    x, y, c, chips = _place()
    return [_remote(ins[a].at[_shard_idx(name, c)], lands[a].at[_slots_idx(name, 2 * x + y, c)], send_sems.at[3 * a + k],
                    recv_sems.at[3 * a + k], (chip[0], chip[1], c))
            for a, name in enumerate(names) for k, chip in enumerate(chips)]


def _copies_start(call_name, copies, sources, land_shapes):
    n = len(sources)

    def body(*refs):
        ins, lands, send_sems, recv_sems, token = refs[:n], refs[n:2 * n], refs[2 * n], refs[2 * n + 1], refs[-1]
        for cp in copies(ins, lands, send_sems, recv_sems):
            cp.start()
        token[...] = jnp.zeros_like(token)

    ins = [_in_hbm(a) for a in sources]
    lands = [_in_hbm(lax.empty(shape, MM)) for shape in land_shapes]
    sems = (pltpu.SemaphoreType.DMA((3 * n,)), pltpu.SemaphoreType.DMA((3 * n,)))
    outs = pl.pallas_call(
        body, name=call_name, in_specs=[HBM] * (2 * n),
        out_specs=(SEM, SEM) + (HBM,) * (2 * n) + (pl.BlockSpec(memory_space=pltpu.VMEM),),
        out_shape=sems + tuple(pltpu.HBM(a.shape, a.dtype) for a in ins + lands) + (jax.ShapeDtypeStruct((8, HEAD_DIM), F32),),
        input_output_aliases={a: 2 + a for a in range(2 * n)}, compiler_params=EFFECT,
    )(*ins, *lands)
    return outs[:-1], outs[-1]


def _copies_wait(call_name, copies, state, after):
    n = (len(state) - 2) // 2

    def body(*refs):
        send_sems, recv_sems, ins, lands = refs[0], refs[1], refs[2:2 + n], refs[2 + n:2 + 2 * n]
        for cp in copies(ins, lands, send_sems, recv_sems):
            cp.wait_send()
            cp.wait_recv()

    outs = pl.pallas_call(
        body, name=call_name, in_specs=[SEM, SEM] + [HBM] * (2 * n) + [ANY], out_specs=(HBM,) * (2 * n),
        out_shape=tuple(pltpu.HBM(a.shape, a.dtype) for a in state[2:]),
        input_output_aliases={2 + a: a for a in range(2 * n)}, compiler_params=EFFECT,
    )(*state, after)
    return outs[:n], outs[n:]


def all_gather_start(shards, names):
    return _copies_start("all_gather_start", functools.partial(_chip_copies, names), [shards[name] for name in names],
                         [_slots_shape(name) for name in names])


def all_gather_wait(state, names, after):
    ins, lands = _copies_wait("all_gather_wait", functools.partial(_chip_copies, names), state, after)
    return dict(zip(names, ins)), dict(zip(names, lands))


def _chip_sends(names, ins, lands, send_sems, recv_sems):
    x, y, c, chips = _place()
    return [_remote(ins[a].at[_full_idx(name, 2 * chip[0] + chip[1])], lands[a].at[k], send_sems.at[3 * a + k],
                    recv_sems.at[3 * a + k], (chip[0], chip[1], c))
            for a, name in enumerate(names) for k, chip in enumerate(chips)]


def _got_shape(name):
    _, a_, b_, c_ = BIG_SPECS[name]
    return (3, a_, b_ // 2, c_)


def rs_chip_start(pairs, names, tag):
    return _copies_start("rs_chip_start_" + tag, functools.partial(_chip_sends, names), [pairs[name] for name in names],
                         [_got_shape(name) for name in names])


def rs_chip_wait(state, names, tag, after):
    _, lands = _copies_wait("rs_chip_wait_" + tag, functools.partial(_chip_sends, names), state, after)
    return dict(zip(names, lands))


def all_gather_pass_on(lands, names):
    n = len(names)

    def body(*refs):
        outs, send_sems, recv_sems = refs[n:2 * n], refs[2 * n], refs[2 * n + 1]
        x, y, c, chips = _place()
        work = [(3 * a + k, a, name, 2 * chip[0] + chip[1]) for a, name in enumerate(names) for k, chip in enumerate(chips)]
        cps = []
        for s, a, name, slot in work:
            landed = outs[a].at[_slots_idx(name, slot, c)]
            cps.append(_remote(landed, landed, send_sems.at[s], recv_sems.at[s], (x, y, 1 - c)))
            cps[-1].start()
        for s, a, name, slot in work:
            passed = outs[a].at[_slots_idx(name, slot, 1 - c)]
            _remote(passed, passed, send_sems.at[s], recv_sems.at[s], (x, y, 1 - c)).wait_recv()
        for cp in cps:
            cp.wait_send()

    outs = pl.pallas_call(
        body, name="all_gather_pass_on", in_specs=[ANY] * n, out_specs=[ANY] * n,
        input_output_aliases={a: a for a in range(n)},
        out_shape=[jax.ShapeDtypeStruct(_slots_shape(name), MM) for name in names],
        scratch_shapes=[pltpu.SemaphoreType.DMA((3 * n,)), pltpu.SemaphoreType.DMA((3 * n,))],
    )(*[lands[name] for name in names])
    return dict(zip(names, outs))


def all_gather_big(shards, names):
    n = len(names)
    BIG_NAMES = names

    def body(*refs):
        ins, outs = refs[:n], refs[n:2 * n]
        send_sems, recv_sems, fsend_sems, frecv_sems = refs[2 * n:]
        x, y, c, chips = _place()
        me_chip, sibling = 2 * x + y, (x, y, 1 - c)
        work = [(3 * a + k, a, name, chip) for a, name in enumerate(BIG_NAMES) for k, chip in enumerate(chips)]
        sends = []
        for s, a, name, chip in work:
            cp = _remote(ins[a].at[_shard_idx(name, c)], outs[a].at[_slots_idx(name, me_chip, c)], send_sems.at[s],
                         recv_sems.at[s], (chip[0], chip[1], c))
            cp.start()
            sends.append(cp)
        for s, a, name, chip in work:
            landed = outs[a].at[_slots_idx(name, 2 * chip[0] + chip[1], c)]
            _remote(landed, landed, send_sems.at[s], recv_sems.at[s], (chip[0], chip[1], c)).wait_recv()
            cp = _remote(landed, landed, fsend_sems.at[s], frecv_sems.at[s], sibling)
            cp.start()
            sends.append(cp)
        for s, a, name, chip in work:
            passed = outs[a].at[_slots_idx(name, 2 * chip[0] + chip[1], 1 - c)]
            _remote(passed, passed, fsend_sems.at[s], frecv_sems.at[s], sibling).wait_recv()
        for cp in sends:
            cp.wait_send()

    outs = pl.pallas_call(
        body, name="all_gather_big", in_specs=[ANY] * n, out_specs=[ANY] * n,
        out_shape=[jax.ShapeDtypeStruct(_slots_shape(name), MM) for name in BIG_NAMES],
        scratch_shapes=[pltpu.SemaphoreType.DMA((3 * n,))] * 4,
    )(*[shards[name] for name in BIG_NAMES])
    return dict(zip(BIG_NAMES, outs))


def with_own_slot(name, full, shard, chip):
    return lax.dynamic_update_slice(full, shard[:, None], (0, chip, 0, 0))


def rs_pair_exchange_big(grads, names, tag):
    n = len(names)

    def body(*refs):
        ins, outs, send_sems, recv_sems = refs[:n], refs[n:2 * n], refs[2 * n], refs[2 * n + 1]
        x, y, c, _ = _place()
        cps = []
        for a, name in enumerate(names):
            cp = _remote(ins[a].at[_full_idx(name, None, 1 - c)], outs[a], send_sems.at[a], recv_sems.at[a], (x, y, 1 - c))
            cp.start()
            cps.append(cp)
        for cp in cps:
            cp.wait()

    outs = pl.pallas_call(
        body, name="rs_pair_exchange_" + tag, in_specs=[ANY] * n, out_specs=[ANY] * n,
        out_shape=[jax.ShapeDtypeStruct(_full_shape(name, half=True), F32) for name in names],
        scratch_shapes=[pltpu.SemaphoreType.DMA((n,)), pltpu.SemaphoreType.DMA((n,))],
    )(*[grads[name] for name in names])
    return dict(zip(names, outs))


def rs_pair_add_big(name, place, g, got):
    kind, a_, b_, c_ = BIG_SPECS[name]
    rb = _row_block(name)
    nb = (b_ // 2) // rb

    def body(place_ref, g_ref, got_ref, o_ref):
        o_ref[...] = (g_ref[...] + got_ref[...]).astype(o_ref.dtype)

    if kind == "rows":
        g_spec = pl.BlockSpec((None, None, rb, c_), lambda a, j, i, p: (a, j, p[0] * nb + i, 0))
        o_spec = pl.BlockSpec((None, None, rb, c_), lambda a, j, i, p: (a, j, i, 0))
    else:
        g_spec = pl.BlockSpec((None, rb, c_), lambda a, j, i, p: (a, p[0] * nb + i, j))
        o_spec = pl.BlockSpec((None, rb, c_), lambda a, j, i, p: (a, i, j))
    return pl.pallas_call(
        body, name="rs_pair_add_" + name,
        grid_spec=pltpu.PrefetchScalarGridSpec(num_scalar_prefetch=1, grid=(a_, N_CHIP, nb), in_specs=[g_spec, o_spec],
                                               out_specs=o_spec),
        out_shape=jax.ShapeDtypeStruct(_full_shape(name, half=True), MM),
        compiler_params=_cparams(("parallel", "parallel", "parallel")),
    )(place, g, got)


def rs_chip_exchange_big(pairs, names, tag):
    n = len(names)

    def body(*refs):
        ins, outs, send_sems, recv_sems = refs[:n], refs[n:2 * n], refs[2 * n], refs[2 * n + 1]
        cps = _chip_sends(names, ins, outs, send_sems, recv_sems)
        for cp in cps:
            cp.start()
        for cp in cps:
            cp.wait()

    outs = pl.pallas_call(
        body, name="rs_chip_exchange_" + tag, in_specs=[ANY] * n, out_specs=[ANY] * n,
        out_shape=[jax.ShapeDtypeStruct(_got_shape(name), MM) for name in names],
        scratch_shapes=[pltpu.SemaphoreType.DMA((3 * n,)), pltpu.SemaphoreType.DMA((3 * n,))],
    )(*[pairs[name] for name in names])
    return dict(zip(names, outs))


def rs_chip_add_big(name, place, g, got_pair, got_chips):
    kind, a_, b_, c_ = BIG_SPECS[name]
    rb = _row_block(name)
    nb = (b_ // 2) // rb

    def body(place_ref, g_ref, s_ref, r0_ref, r1_ref, r2_ref, o_ref):
        own = g_ref[...] + s_ref[...]
        o_ref[...] = ((own + r0_ref[...].astype(F32)) + r1_ref[...].astype(F32)) + r2_ref[...].astype(F32)

    if kind == "rows":
        g_spec = pl.BlockSpec((None, None, rb, c_), lambda a, i, p: (a, p[1], p[0] * nb + i, 0))
        s_spec = pl.BlockSpec((None, None, rb, c_), lambda a, i, p: (a, p[1], i, 0))
    else:
        g_spec = pl.BlockSpec((None, rb, c_), lambda a, i, p: (a, p[0] * nb + i, p[1]))
        s_spec = pl.BlockSpec((None, rb, c_), lambda a, i, p: (a, i, p[1]))
    r_spec = lambda k: pl.BlockSpec((None, None, rb, c_), lambda a, i, p: (k, a, i, 0))
    return pl.pallas_call(
        body, name="rs_chip_add_" + name,
        grid_spec=pltpu.PrefetchScalarGridSpec(
            num_scalar_prefetch=1, grid=(a_, nb), in_specs=[g_spec, s_spec, r_spec(0), r_spec(1), r_spec(2)],
            out_specs=pl.BlockSpec((None, rb, c_), lambda a, i, p: (a, p[0] * nb + i, 0))),
        out_shape=jax.ShapeDtypeStruct((a_, b_, c_), F32), compiler_params=_cparams(("parallel", "parallel")),
    )(place, g, got_pair, got_chips, got_chips, got_chips)


def rs_pair_gather_big(halves):
    names = tuple(halves)
    n = len(names)

    def body(*refs):
        outs, send_sems, recv_sems = refs[n:2 * n], refs[2 * n], refs[2 * n + 1]
        x, y, c, _ = _place()
        cps = []
        for a, name in enumerate(names):
            mine = outs[a].at[_shard_idx(name, c)]
            cp = _remote(mine, mine, send_sems.at[a], recv_sems.at[a], (x, y, 1 - c))
            cp.start()
            cps.append(cp)
        for a, name in enumerate(names):
            cps[a].wait_send()
            theirs = outs[a].at[_shard_idx(name, 1 - c)]
            _remote(theirs, theirs, send_sems.at[a], recv_sems.at[a], (x, y, 1 - c)).wait_recv()

    outs = pl.pallas_call(
        body, name="rs_pair_gather_big", in_specs=[ANY] * n, out_specs=[ANY] * n,
        input_output_aliases={a: a for a in range(n)},
        out_shape=[jax.ShapeDtypeStruct(BIG_SPECS[name][1:], F32) for name in names],
        scratch_shapes=[pltpu.SemaphoreType.DMA((n,)), pltpu.SemaphoreType.DMA((n,))],
    )(*[halves[name] for name in names])
    return dict(zip(names, outs))


def rs_begin(grads, names, tag, place):
    got_pair = rs_pair_exchange_big(grads, names, tag)
    pairs = {name: rs_pair_add_big(name, place, grads[name], got_pair[name]) for name in names}
    state, token = rs_chip_start(pairs, names, tag)
    return (grads, got_pair, state), token


def rs_end(begun, names, tag, place, after):
    grads, got_pair, state = begun
    got_chips = rs_chip_wait(state, names, tag, after)
    return {name: rs_chip_add_big(name, place, grads[name], got_pair[name], got_chips[name]) for name in names}


def rs_whole(grads, names, tag, place):
    got_pair = rs_pair_exchange_big(grads, names, tag)
    pairs = {name: rs_pair_add_big(name, place, grads[name], got_pair[name]) for name in names}
    got_chips = rs_chip_exchange_big(pairs, names, tag)
    return {name: rs_chip_add_big(name, place, grads[name], got_pair[name], got_chips[name]) for name in names}


PACK_W = 1024
SMALL =(("mem_norm_w", 1024), ("mem_k_norm_w", 128), ("norm1_w", 2048), ("dn_a_log", 8), ("dn_dt_bias", 8),
         ("dn_o_norm_w", 128), ("fox_f_bias", 8), ("fox_q_norm_w", 128), ("fox_k_norm_w", 128), ("memq_norm_w", 256),
         ("norm2_w", 2048))
SMALL_ROWS = 8
CONV_ROWS = 4 * 3 * D_MODEL // PACK_W
LOSS_AT = sum(n for _, n in SMALL)


def pack_small(parts, extra=None):
    flat = [parts[name].astype(F32).reshape(-1) for name, _ in SMALL]
    used = LOSS_AT
    if extra is not None:
        flat.append(extra.reshape(1))
        used += 1
    flat.append(jnp.zeros((SMALL_ROWS * PACK_W - used,), F32))
    return jnp.concatenate(flat).reshape(SMALL_ROWS, PACK_W)


def unpack_small(packed, shapes):
    flat, out, at = packed.reshape(-1), {}, 0
    for name, n in SMALL:
        out[name] = flat[at:at + n].reshape(shapes[name])
        at += n
    return out


def _adam_all(w, g, m, v, name):
    shape = w.shape
    r2 = lambda a: a.reshape(-1, shape[-1])
    d, nm, nv = adamw(r2(w), r2(g), r2(m), r2(v), name=name)
    return d.reshape(shape), nm.reshape(shape), nv.reshape(shape)


BIG = ("w_mem_kv", "dn_w_in", "dn_conv_w", "fox_w_in", "w_out", "w_mlp1", "w_mlp2")
WEIGHTS = ("mem_norm_w", "w_mem_kv", "mem_k_norm_w", "norm1_w", "dn_w_in", "dn_conv_w", "dn_a_log", "dn_dt_bias",
           "dn_o_norm_w", "fox_w_in", "fox_f_bias", "fox_q_norm_w", "fox_k_norm_w", "memq_norm_w", "w_out", "norm2_w",
           "w_mlp1", "w_mlp2")


def kernel(x, mem, mem_norm_w, w_mem_kv, mem_k_norm_w, norm1_w, dn_w_in, dn_conv_w, dn_a_log, dn_dt_bias, dn_o_norm_w, fox_w_in, fox_f_bias, fox_q_norm_w, fox_k_norm_w, memq_norm_w, w_out, norm2_w, w_mlp1, w_mlp2, loss_target, m_mem_norm_w, m_w_mem_kv, m_mem_k_norm_w, m_norm1_w, m_dn_w_in, m_dn_conv_w, m_dn_a_log, m_dn_dt_bias, m_dn_o_norm_w, m_fox_w_in, m_fox_f_bias, m_fox_q_norm_w, m_fox_k_norm_w, m_memq_norm_w, m_w_out, m_norm2_w, m_w_mlp1, m_w_mlp2, v_mem_norm_w, v_w_mem_kv, v_mem_k_norm_w, v_norm1_w, v_dn_w_in, v_dn_conv_w, v_dn_a_log, v_dn_dt_bias, v_dn_o_norm_w, v_fox_w_in, v_fox_f_bias, v_fox_q_norm_w, v_fox_k_norm_w, v_memq_norm_w, v_w_out, v_norm2_w, v_w_mlp1, v_w_mlp2):
    args = dict(locals())
    w = {n: args[n] for n in WEIGHTS}
    m = {n: args["m_" + n] for n in WEIGHTS}
    v = {n: args["v_" + n] for n in WEIGHTS}
    core, chip = lax.axis_index("c"), 2 * lax.axis_index("x") + lax.axis_index("y")
    place = jnp.stack([core, chip]).astype(jnp.int32)

    shards = {name: w[name].reshape(BIG_SPECS[name][1:]).astype(MM) for name in BIG_NAMES}
    w_in_full = lambda arr, n_scalars: w_in_to_kernel(arr[0].transpose(1, 0, 2).reshape(D_MODEL, -1), n_scalars)
    early = {name: with_own_slot(name, arr, shards[name], chip)
             for name, arr in all_gather_big(shards, EARLY_NAMES).items()}
    conv_mine = jnp.where(core == 0, dn_conv_w[0], 0.0)
    conv_placed = lax.dynamic_update_slice(jnp.zeros((4, 3 * D_MODEL), F32), conv_mine, (0, 768 * chip))
    conv_full = all_reduce_small(jnp.pad(conv_placed.reshape(CONV_ROWS, PACK_W), ((0, 16 - CONV_ROWS), (0, 0))))
    late_shards, early, conv_full = lax.optimization_barrier(
        ({name: shards[name] for name in LATE_NAMES}, early, conv_full))
    late_state, token = all_gather_start(late_shards, LATE_NAMES)
    tie = token[0, 0]
    wt = dict(w_mem_kv=early["w_mem_kv"].reshape(D_MODEL, 2 * MEM_WIDTH) + tie.astype(MM),
              dn_w_in=w_in_full(early["dn_w_in"], 2 * N_HEADS), conv_w=conv_full[:CONV_ROWS].reshape(4, 3 * D_MODEL))

    def late(after):
        late_shards, lands = all_gather_wait(late_state, LATE_NAMES, after)
        full = {name: with_own_slot(name, arr, late_shards[name], chip)
                for name, arr in all_gather_pass_on(lands, LATE_NAMES).items()}
        return dict(fox_w_in=w_in_full(full["fox_w_in"], N_HEADS), w_out=full["w_out"].reshape(2, 3 * MEM_WIDTH, D_MODEL),
                    w_mlp1=full["w_mlp1"], w_mlp2=full["w_mlp2"].reshape(2, D_FF, D_MODEL))

    sm = dict(mem_norm_w=mem_norm_w, mem_k_norm_w=mem_k_norm_w, norm1_w=norm1_w, norm2_w=norm2_w, memq_norm_w=memq_norm_w,
              dn_a_log=dn_a_log[0], dn_dt_bias=dn_dt_bias[0], dn_o_norm_w=dn_o_norm_w, fox_f_bias=fox_f_bias[0],
              fox_q_norm_w=fox_q_norm_w, fox_k_norm_w=fox_k_norm_w)
    w_in_slots = lambda g, n_scalars: w_in_from_kernel(g, n_scalars).reshape(D_MODEL, N_CHIP, -1).transpose(1, 0, 2)[None]
    rows_view = lambda g, name: g.reshape(_full_shape(name))
    begun = {}

    def on_layer1(big1):
        grads1 = dict(fox_w_in=w_in_slots(big1["w_in"], N_HEADS), w_out_1=rows_view(big1["w_out"], "w_out_1"),
                      w_mlp2_1=rows_view(big1["w_mlp2"], "w_mlp2_1"), w_mlp1_1=big1["w_mlp1"][None])
        begun["layer1"], token = rs_begin(grads1, RS_LAYER1, "layer1", place)
        return token[0, 0]

    def on_mlp0(d_w_mlp2, d_w_mlp1):
        grads0 = dict(w_mlp2_0=rows_view(d_w_mlp2, "w_mlp2_0"), w_mlp1_0=d_w_mlp1[None])
        begun["mlp0"], token = rs_begin(grads0, RS_MLP0, "mlp0", place)
        return token[0, 0]

    loss_part, dx, big, small = local_step(x[0], mem[0], loss_target[0], wt, sm, late, on_layer1, on_mlp0)
    last = dict(w_out_0=rows_view(big["w_out"][0], "w_out_0"), dn_w_in=w_in_slots(big["dn_w_in"], 2 * N_HEADS),
                w_mem_kv=rows_view(big["w_mem_kv"], "w_mem_kv"))
    halves = rs_whole(last, RS_LAST, "last", place)
    halves.update(rs_end(begun["layer1"], RS_LAYER1, "layer1", place, dx))
    halves.update(rs_end(begun["mlp0"], RS_MLP0, "mlp0", place, dx))
    summed = rs_pair_gather_big(halves)
    big_sum = {name: summed[name] for name in ("w_mem_kv", "dn_w_in", "fox_w_in")}
    big_sum.update({name: jnp.concatenate([summed[name + "_0"], summed[name + "_1"]]) for name in ("w_out", "w_mlp2", "w_mlp1")})
    small_pack = jnp.concatenate([pack_small(small, loss_part[0, :1]), big["conv_w"].reshape(CONV_ROWS, PACK_W),
                                  jnp.zeros((24 - SMALL_ROWS - CONV_ROWS, PACK_W), F32)])
    small_all = all_reduce_small(small_pack)
    small_sum = small_all[:SMALL_ROWS]
    conv_sum = lax.dynamic_slice(small_all[SMALL_ROWS:SMALL_ROWS + CONV_ROWS].reshape(4, 3 * D_MODEL), (0, 768 * chip), (4, 768))
    loss = small_sum.reshape(-1)[LOSS_AT]
    grads = unpack_small(small_sum, {n: w[n].shape for n, _ in SMALL})
    grads.update({name: big_sum[name].reshape(w[name].shape) for name in BIG_NAMES}, dn_conv_w=conv_sum[None])

    delta, new_m, new_v = {}, {}, {}
    for n in BIG:
        delta[n], new_m[n], new_v[n] = _adam_all(w[n], grads[n], m[n], v[n], "adamw_" + n)
    shapes = {n: w[n].shape for n, _ in SMALL}
    d_s, m_s, v_s = adamw(pack_small(w), small_sum, pack_small(m), pack_small(v), name="adamw_small")
    for out, packed in ((delta, d_s), (new_m, m_s), (new_v, v_s)):
        out.update(unpack_small(packed, shapes))
    return (loss, dx[None], *[grads[n] for n in WEIGHTS], *[delta[n] for n in WEIGHTS],
            *[new_m[n] for n in WEIGHTS], *[new_v[n] for n in WEIGHTS])
```

```python
import functools

import jax
import jax.numpy as jnp
from jax import lax
from jax.experimental import pallas as pl
from jax.experimental.pallas import tpu as pltpu

F32 = jnp.float32
MM = jnp.bfloat16
HI = lax.Precision.HIGHEST

D_MODEL = 1024
HEAD_DIM = 128
N_HEADS = 8
MEM_HEADS = 4
MEM_WIDTH = MEM_HEADS * HEAD_DIM
N_MEM = 256
D_FF = 4 * D_MODEL
CHUNK = 64
EPS = 1e-6
QSCALE = HEAD_DIM ** -0.5
PROJ_W = 4736
TAIL = 4608
TAIL_BLK = TAIL // HEAD_DIM
ROWS = 256
VMEM_LIMIT = 56 * 1024 * 1024

ADAM_LR = 0.001
ADAM_B1 = 0.9
ADAM_B2 = 0.999
ADAM_EPS = 1e-08
ADAM_WD = 0.01
ADAM_STEP = 10

N_DEV = 8
N_CHIP = 4
MESH = pl.DeviceIdType.MESH


def _cparams(sem=None):
    return pltpu.CompilerParams(dimension_semantics=sem, vmem_limit_bytes=VMEM_LIMIT)


def _dot(a, b, ca, cb, hi):
    dims = (((ca,), (cb,)), ((), ()))
    if hi:
        return lax.dot_general(a, b, dims, precision=HI, preferred_element_type=F32)
    return lax.dot_general(a.astype(MM), b.astype(MM), dims, preferred_element_type=F32)


@functools.partial(jax.custom_vjp, nondiff_argnums=(2, 3, 4))
def mmul(a, b, ca, cb, hi):
    return _dot(a, b, ca, cb, hi)


def _mmul_fwd(a, b, ca, cb, hi):
    return _dot(a, b, ca, cb, hi), (a, b)


def _mmul_bwd(ca, cb, hi, res, g):
    a, b = res
    if ca == 1:
        da = _dot(g, b, 1, 1, hi) if cb == 0 else _dot(g, b, 1, 0, hi)
    else:
        da = _dot(b, g, 1, 1, hi) if cb == 0 else _dot(b, g, 0, 1, hi)
    if cb == 0:
        db = _dot(a, g, 0, 0, hi) if ca == 1 else _dot(a, g, 1, 0, hi)
    else:
        db = _dot(g, a, 0, 0, hi) if ca == 1 else _dot(g, a, 0, 1, hi)
    return da.astype(a.dtype), db.astype(b.dtype)


mmul.defvjp(_mmul_fwd, _mmul_bwd)


def _iota2(n, m):
    return lax.broadcasted_iota(jnp.int32, (n, m), 0), lax.broadcasted_iota(jnp.int32, (n, m), 1)


def _same_block(r, c, shift):
    return lax.shift_right_logical(r, shift) == lax.shift_right_logical(c, shift)


def _split_bf16(x):
    hi = x.astype(jnp.bfloat16)
    return hi, (x - hi.astype(F32)).astype(jnp.bfloat16)


def _dot3(a, b, ca, cb):
    dims = (((ca,), (cb,)), ((), ()))
    (ah, al), (bh, bl) = _split_bf16(a), _split_bf16(b)
    d = lambda x, y: lax.dot_general(x, y, dims, preferred_element_type=F32)
    return d(ah, bh) + (d(ah, bl) + d(al, bh))


def _tri_inv_impl(a):
    n = a.shape[0]
    r, c = _iota2(n, n)
    eye = (r == c).astype(F32)
    b16, b32 = _same_block(r, c, 4), _same_block(r, c, 5)
    a0 = jnp.where(b16, a, 0.0)
    p = eye - a0
    b = _dot3(a0, a0, 1, 0)
    p = p + _dot3(p, b, 1, 0)
    b = _dot3(b, b, 1, 0)
    p = p + _dot3(p, b, 1, 0)
    b = _dot3(b, b, 1, 0)
    p = p + _dot3(p, b, 1, 0)
    a1 = jnp.where(jnp.logical_and(b32, jnp.logical_not(b16)), a, 0.0)
    p = p - _dot3(_dot3(p, a1, 1, 0), p, 1, 0)
    a2 = jnp.where(b32, 0.0, a)
    p = p - _dot3(_dot3(p, a2, 1, 0), p, 1, 0)
    return p


@jax.custom_vjp
def tri_inv(a):
    return _tri_inv_impl(a)


def _tri_inv_fwd(a):
    p = _tri_inv_impl(a)
    return p, p


def _tri_inv_bwd(p, g):
    return (-_dot3(_dot3(p, g, 0, 0), p, 1, 1),)


tri_inv.defvjp(_tri_inv_fwd, _tri_inv_bwd)


def _sigmoid(x):
    return 1.0 / (1.0 + jnp.exp(-x))


def _softplus(x):
    return jnp.maximum(x, 0.0) + jnp.log(1.0 + jnp.exp(-jnp.abs(x)))


def _silu(x):
    return x * _sigmoid(x)


def _rms(x, w):
    return x * lax.rsqrt(jnp.mean(x * x, axis=-1, keepdims=True) + EPS) * w


def _bf_round(x):
    return x.astype(MM).astype(F32)


def _acc(ref, val, first):
    @pl.when(first)
    def _():
        ref[...] = val

    @pl.when(jnp.logical_not(first))
    def _():
        ref[...] += val


def _tile(n, pref):
    if n % pref == 0:
        return pref
    return n


def matmul(a, b, *, ta=False, tb=False, b_slots=False, res=None, also_sqrelu=False, times_dsqrelu=None, out_dtype=F32,
           name, tm=1024, tn=1024, tk=1024):
    m, k = (a.shape[1], a.shape[0]) if ta else a.shape
    if b_slots:
        n = b.shape[1] if tb else N_CHIP * b.shape[2]
        assert (N_CHIP * b.shape[2] if tb else b.shape[1]) == k, (a.shape, b.shape, ta, tb)
        tn, tk = (tn, b.shape[2]) if tb else (b.shape[2], tk)
    else:
        n = b.shape[0] if tb else b.shape[1]
        assert (b.shape[1] if tb else b.shape[0]) == k, (a.shape, b.shape, ta, tb)
    tm, tn, tk = _tile(m, tm), _tile(n, tn), _tile(k, tk)
    nk = k // tk
    ca, cb = (0 if ta else 1), (1 if tb else 0)

    extra = tuple(e for e in (res, times_dsqrelu) if e is not None)
    assert len(extra) <= 1

    def body(a_ref, b_ref, *rest):
        e_ref = rest[0] if extra else None
        o_ref = rest[len(extra)]

        def finish(total):
            if res is not None:
                total = total + e_ref[...]
            if times_dsqrelu is not None:
                total = total * (2.0 * jnp.maximum(e_ref[...], 0.0))
            o_ref[...] = total.astype(o_ref.dtype)
            if also_sqrelu:
                rest[len(extra) + 1][...] = _sqrelu(total).astype(MM)

        if nk == 1:
            finish(_dot(a_ref[...], b_ref[...], ca, cb, False))
            return
        acc_ref, kk = rest[-1], pl.program_id(2)

        @pl.when(kk == 0)
        def _():
            acc_ref[...] = jnp.zeros_like(acc_ref)

        acc_ref[...] += _dot(a_ref[...], b_ref[...], ca, cb, False)

        @pl.when(kk == nk - 1)
        def _():
            finish(acc_ref[...])

    a_spec = pl.BlockSpec((tk, tm), lambda i, j, l: (l, i)) if ta else pl.BlockSpec((tm, tk), lambda i, j, l: (i, l))
    if b_slots:
        b_spec = (pl.BlockSpec((None, tn, tk), lambda i, j, l: (l, j, 0)) if tb else
                  pl.BlockSpec((None, tk, tn), lambda i, j, l: (j, l, 0)))
    else:
        b_spec = pl.BlockSpec((tn, tk), lambda i, j, l: (j, l)) if tb else pl.BlockSpec((tk, tn), lambda i, j, l: (l, j))
    o_spec = pl.BlockSpec((tm, tn), lambda i, j, l: (i, j))
    out_shape = [jax.ShapeDtypeStruct((m, n), out_dtype)] + [jax.ShapeDtypeStruct((m, n), MM)] * also_sqrelu
    outs = pl.pallas_call(
        body, name=name, grid=(m // tm, n // tn, nk),
        in_specs=[a_spec, b_spec] + [o_spec] * len(extra), out_specs=[o_spec] * len(out_shape), out_shape=out_shape,
        scratch_shapes=[pltpu.VMEM((tm, tn), F32)] * (nk > 1),
        compiler_params=_cparams(("parallel", "parallel", "arbitrary")),
    )(a, b, *extra)
    return outs if also_sqrelu else outs[0]


def rms_fwd(x, w, *, name):
    t, d = x.shape

    def body(x_ref, w_ref, o_ref):
        o_ref[...] = _rms(x_ref[...], w_ref[...]).astype(o_ref.dtype)

    return pl.pallas_call(
        body, name=name, grid=(t // ROWS,),
        in_specs=[pl.BlockSpec((ROWS, d), lambda i: (i, 0)), pl.BlockSpec((1, d), lambda i: (0, 0))],
        out_specs=pl.BlockSpec((ROWS, d), lambda i: (i, 0)),
        out_shape=jax.ShapeDtypeStruct((t, d), MM), compiler_params=_cparams(("parallel",)),
    )(x, w)


def rms_bwd(x, w, dh, dres, *, name):
    t, d = x.shape

    def body(x_ref, w_ref, dh_ref, dr_ref, dx_ref, dw_ref):
        _, vjp = jax.vjp(_rms, x_ref[...], w_ref[...])
        dx, dw = vjp(dh_ref[...].astype(F32))
        dx_ref[...] = dx + dr_ref[...]
        _acc(dw_ref, dw, pl.program_id(0) == 0)

    row = pl.BlockSpec((ROWS, d), lambda i: (i, 0))
    vec = pl.BlockSpec((1, d), lambda i: (0, 0))
    return pl.pallas_call(
        body, name=name, grid=(t // ROWS,), in_specs=[row, vec, row, row], out_specs=[row, vec],
        out_shape=[jax.ShapeDtypeStruct((t, d), F32), jax.ShapeDtypeStruct((1, d), F32)],
        compiler_params=_cparams(("arbitrary",)),
    )(x, w, dh, dres)


def _sqrelu(x):
    return jnp.square(jnp.maximum(x, 0.0))


def loss_fwd(y, target, *, name):
    t, d = y.shape

    def body(y_ref, t_ref, dy_ref, l_ref):
        e = y_ref[...] - t_ref[...]
        dy_ref[...] = e * (1.0 / d)
        part = 0.5 * jnp.sum(jnp.sum(e * e, axis=-1, keepdims=True) * (1.0 / d), axis=0, keepdims=True)
        _acc(l_ref, jnp.broadcast_to(part, (1, HEAD_DIM)), pl.program_id(0) == 0)

    blk = pl.BlockSpec((ROWS, d), lambda i: (i, 0))
    return pl.pallas_call(
        body, name=name, grid=(t // ROWS,), in_specs=[blk, blk],
        out_specs=[blk, pl.BlockSpec((1, HEAD_DIM), lambda i: (0, 0))],
        out_shape=[jax.ShapeDtypeStruct((t, d), F32), jax.ShapeDtypeStruct((1, HEAD_DIM), F32)],
        compiler_params=_cparams(("arbitrary",)),
    )(y, target)


def _mem_kv(mem, wn, wkn, *ws):
    mn = _rms(mem, wn)
    outs = []
    for h in range(MEM_HEADS):
        outs.append(_rms(mmul(mn, ws[h], 1, 0, False), wkn))
    for h in range(MEM_HEADS):
        outs.append(mmul(mn, ws[MEM_HEADS + h], 1, 0, False))
    return tuple(outs)


def _w_cols(w_ref):
    return [w_ref[:, h * HEAD_DIM:(h + 1) * HEAD_DIM] for h in range(2 * MEM_HEADS)]


def mem_fwd(mem, wn, wkv, wkn):
    def body(mem_ref, wn_ref, w_ref, wkn_ref, k_ref, v_ref):
        outs = _mem_kv(mem_ref[...], wn_ref[...], wkn_ref[...], *_w_cols(w_ref))
        for h in range(MEM_HEADS):
            k_ref[:, h * HEAD_DIM:(h + 1) * HEAD_DIM] = outs[h]
            v_ref[:, h * HEAD_DIM:(h + 1) * HEAD_DIM] = outs[MEM_HEADS + h]

    shp = jax.ShapeDtypeStruct((mem.shape[0], MEM_WIDTH), F32)
    return pl.pallas_call(body, name="mem_fwd", out_shape=[shp, shp], compiler_params=_cparams())(mem, wn, wkv, wkn)


def mem_bwd(mem, wn, wkv, wkn, dk0, dv0, dk1, dv1):
    def body(mem_ref, wn_ref, w_ref, wkn_ref, dk0_ref, dv0_ref, dk1_ref, dv1_ref, dwn_ref, dw_ref, dwkn_ref):
        _, vjp = jax.vjp(lambda wn_, wkn_, *ws: _mem_kv(mem_ref[...], wn_, wkn_, *ws),
                         wn_ref[...], wkn_ref[...], *[w.astype(F32) for w in _w_cols(w_ref)])
        cols = lambda a, b: tuple(a[:, h * HEAD_DIM:(h + 1) * HEAD_DIM] + b[:, h * HEAD_DIM:(h + 1) * HEAD_DIM]
                                  for h in range(MEM_HEADS))
        cts = cols(dk0_ref, dk1_ref) + cols(dv0_ref, dv1_ref)
        grads = vjp(cts)
        dwn_ref[...] = grads[0]
        dwkn_ref[...] = grads[1]
        for h in range(2 * MEM_HEADS):
            dw_ref[:, h * HEAD_DIM:(h + 1) * HEAD_DIM] = grads[2 + h]

    return pl.pallas_call(
        body, name="mem_bwd",
        out_shape=[jax.ShapeDtypeStruct((1, D_MODEL), F32), jax.ShapeDtypeStruct((D_MODEL, 2 * MEM_WIDTH), F32),
                   jax.ShapeDtypeStruct((1, HEAD_DIM), F32)],
        compiler_params=_cparams(),
    )(mem, wn, wkv, wkn, dk0, dv0, dk1, dv1)


def _memattn(q, wq, mk, mv):
    qn = _rms(q, wq) * QSCALE
    s = mmul(qn, mk, 1, 1, False)
    s = s - jnp.max(s, axis=-1, keepdims=True)
    p = jnp.exp(s)
    p = p / jnp.sum(p, axis=-1, keepdims=True)
    return mmul(p, mv, 1, 0, False)


def _lanes(j):
    return slice(j * HEAD_DIM, (j + 1) * HEAD_DIM)


def _memattn_specs(t):
    qspec = pl.BlockSpec((ROWS, MEM_WIDTH), lambda i: (i, (TAIL - MEM_WIDTH) // MEM_WIDTH))
    wspec = pl.BlockSpec((1, HEAD_DIM), lambda i: (0, 0))
    mspec = pl.BlockSpec((N_MEM, MEM_WIDTH), lambda i: (0, 0))
    ospec = pl.BlockSpec((ROWS, MEM_WIDTH), lambda i: (i, 0))
    return qspec, wspec, mspec, ospec


def memattn_fwd(proj, wq, mk, mv, *, name):
    t = proj.shape[0]
    qspec, wspec, mspec, ospec = _memattn_specs(t)

    def body(q_ref, w_ref, k_ref, v_ref, o_ref):
        for h in range(MEM_HEADS):
            o_ref[:, _lanes(h)] = _memattn(q_ref[:, _lanes(h)], w_ref[...], k_ref[:, _lanes(h)],
                                           v_ref[:, _lanes(h)]).astype(o_ref.dtype)

    return pl.pallas_call(
        body, name=name, grid=(t // ROWS,), in_specs=[qspec, wspec, mspec, mspec], out_specs=ospec,
        out_shape=jax.ShapeDtypeStruct((t, MEM_WIDTH), MM), compiler_params=_cparams(("parallel",)),
    )(proj, wq, mk, mv)


def memattn_bwd(proj, wq, mk, mv, dcat, *, name):
    t = proj.shape[0]
    qspec, wspec, mspec, ospec = _memattn_specs(t)
    dospec = pl.BlockSpec((ROWS, MEM_WIDTH), lambda i: (i, D_MODEL // MEM_WIDTH))

    def body(q_ref, w_ref, k_ref, v_ref, do_ref, dq_ref, dw_ref, dk_ref, dv_ref):
        first = pl.program_id(0) == 0
        dw_sum = jnp.zeros((1, HEAD_DIM), F32)
        for h in range(MEM_HEADS):
            _, vjp = jax.vjp(_memattn, q_ref[:, _lanes(h)], w_ref[...], k_ref[:, _lanes(h)], v_ref[:, _lanes(h)])
            dq, dw, dk, dv = vjp(do_ref[:, _lanes(h)].astype(F32))
            dq_ref[:, _lanes(h)] = dq.astype(dq_ref.dtype)
            dw_sum = dw_sum + dw
            _acc(dk_ref.at[:, _lanes(h)], dk, first)
            _acc(dv_ref.at[:, _lanes(h)], dv, first)
        _acc(dw_ref, dw_sum, first)

    mshape = jax.ShapeDtypeStruct((N_MEM, MEM_WIDTH), F32)
    return pl.pallas_call(
        body, name=name, grid=(t // ROWS,), in_specs=[qspec, wspec, mspec, mspec, dospec],
        out_specs=[ospec, wspec, mspec, mspec],
        out_shape=[jax.ShapeDtypeStruct((t, MEM_WIDTH), MM), jax.ShapeDtypeStruct((1, HEAD_DIM), F32), mshape, mshape],
        compiler_params=_cparams(("arbitrary",)),
    )(proj, wq, mk, mv, dcat)


def _shift_rows(x, s, up):
    n = x.shape[0]
    r = lax.broadcasted_iota(jnp.int32, x.shape, 0)
    if up:
        return jnp.where(r < n - s, pltpu.roll(x, n - s, 0), 0.0)
    return jnp.where(r >= s, pltpu.roll(x, s, 0), 0.0)


def _conv_fwd_vals(x, w):
    xb = _bf_round(x)
    wb = _bf_round(w)
    c = xb * wb[3:4, :]
    for j in range(3):
        c = c + _shift_rows(xb, 3 - j, False) * wb[j:j + 1, :]
    return xb, wb, c


def dn_prep_fwd(proj, conv_w):
    t = proj.shape[0]

    def body(x_ref, w_ref, o_ref):
        j = pl.program_id(0)
        _, _, c = _conv_fwd_vals(x_ref[...], w_ref[...])
        s = _silu(c)
        r = lax.rsqrt(jnp.sum(s * s, axis=-1, keepdims=True) + EPS)
        scale = jnp.where(j < N_HEADS, QSCALE, 1.0)
        o_ref[...] = jnp.where(j < 2 * N_HEADS, s * r * scale, s)

    return pl.pallas_call(
        body, name="dn_prep_fwd", grid=(3 * N_HEADS,),
        in_specs=[pl.BlockSpec((t, HEAD_DIM), lambda j: (0, j)), pl.BlockSpec((4, HEAD_DIM), lambda j: (0, j))],
        out_specs=pl.BlockSpec((None, t, HEAD_DIM), lambda j: (j // N_HEADS, 0, j % N_HEADS)),
        out_shape=jax.ShapeDtypeStruct((3, t, D_MODEL), F32), compiler_params=_cparams(("parallel",)),
    )(proj, conv_w)


def dn_prep_bwd(proj, conv_w, dqkv):
    t = proj.shape[0]

    def body(x_ref, w_ref, g_ref, dx_ref, dw_ref):
        j = pl.program_id(0)
        xb, wb, c = _conv_fwd_vals(x_ref[...], w_ref[...])
        sg = _sigmoid(c)
        s = c * sg
        g = g_ref[...]
        r = lax.rsqrt(jnp.sum(s * s, axis=-1, keepdims=True) + EPS)
        scale = jnp.where(j < N_HEADS, QSCALE, 1.0)
        gn = g * scale
        ds_norm = r * gn - s * (r * r * r) * jnp.sum(gn * s, axis=-1, keepdims=True)
        ds = jnp.where(j < 2 * N_HEADS, ds_norm, g)
        dc = ds * (sg + s * (1.0 - sg))
        dx = dc * wb[3:4, :]
        rows = [jnp.sum(dc * xb, axis=0, keepdims=True)]
        for jj in range(2, -1, -1):
            sh = 3 - jj
            dx = dx + _shift_rows(dc, sh, True) * wb[jj:jj + 1, :]
            rows.insert(0, jnp.sum(dc * _shift_rows(xb, sh, False), axis=0, keepdims=True))
        dx_ref[...] = dx.astype(dx_ref.dtype)
        dw_ref[...] = jnp.concatenate(rows + [jnp.zeros((4, HEAD_DIM), F32)], axis=0)

    col = pl.BlockSpec((t, HEAD_DIM), lambda j: (0, j))
    return pl.pallas_call(
        body, name="dn_prep_bwd", grid=(3 * N_HEADS,),
        in_specs=[col, pl.BlockSpec((4, HEAD_DIM), lambda j: (0, j)),
                  pl.BlockSpec((None, t, HEAD_DIM), lambda j: (j // N_HEADS, 0, j % N_HEADS))],
        out_specs=[col, pl.BlockSpec((8, HEAD_DIM), lambda j: (0, j))],
        out_shape=[jax.ShapeDtypeStruct((t, 3 * D_MODEL), MM), jax.ShapeDtypeStruct((8, 3 * D_MODEL), F32)],
        compiler_params=_cparams(("parallel",)),
    )(proj, conv_w, dqkv)


def _tri_ones(n, upper):
    r, c = _iota2(n, n)
    return (r <= c).astype(F32) if upper else (r >= c).astype(F32)


def dn_gates_fwd(proj, a_log, dt_bias):
    t = proj.shape[0]

    def body(x_ref, al_ref, dt_ref, o_ref):
        lane = lax.broadcasted_iota(jnp.int32, (CHUNK, HEAD_DIM), 1)
        tri = _tri_ones(CHUNK, False)

        def step(c, carry):
            rows = pl.ds(pl.multiple_of(c * CHUNK, CHUNK), CHUNK)
            x = x_ref[rows, :]
            g = jnp.where(lane < N_HEADS, -jnp.exp(al_ref[...]) * _softplus(x + dt_ref[...]), 0.0)
            gc = _dot(tri, g, 1, 0, True)
            o_ref[rows, :] = jnp.where(lane < N_HEADS, gc, jnp.where(lane < 2 * N_HEADS, _sigmoid(x), 0.0))
            return carry

        lax.fori_loop(0, t // CHUNK, step, 0)

    vec = pl.BlockSpec((1, HEAD_DIM), lambda i: (0, 0))
    return pl.pallas_call(
        body, name="dn_gates_fwd", grid=(1,),
        in_specs=[pl.BlockSpec((t, HEAD_DIM), lambda i: (0, TAIL_BLK)), vec, vec],
        out_specs=pl.BlockSpec((t, HEAD_DIM), lambda i: (0, 0)),
        out_shape=jax.ShapeDtypeStruct((t, HEAD_DIM), F32), compiler_params=_cparams(("arbitrary",)),
    )(proj, a_log, dt_bias)


def dn_gates_bwd(proj, a_log, dt_bias, dgates):
    t = proj.shape[0]

    def body(x_ref, al_ref, dt_ref, g_ref, dx_ref, dal_ref, ddt_ref):
        lane = lax.broadcasted_iota(jnp.int32, (CHUNK, HEAD_DIM), 1)
        tri = _tri_ones(CHUNK, True)
        dal_ref[...] = jnp.zeros_like(dal_ref)
        ddt_ref[...] = jnp.zeros_like(ddt_ref)

        def step(c, carry):
            rows = pl.ds(pl.multiple_of(c * CHUNK, CHUNK), CHUNK)
            x = x_ref[rows, :]
            dgc = jnp.where(lane < N_HEADS, g_ref[rows, :], 0.0)
            dg = _dot(tri, dgc, 1, 0, True)
            ea = -jnp.exp(al_ref[...])
            z = x + dt_ref[...]
            da = jnp.where(lane < N_HEADS, dg * ea * _sigmoid(z), 0.0)
            gval = jnp.where(lane < N_HEADS, ea * _softplus(z), 0.0)
            beta = _sigmoid(x)
            db = jnp.where(jnp.logical_and(lane >= N_HEADS, lane < 2 * N_HEADS), g_ref[rows, :] * beta * (1.0 - beta), 0.0)
            dx_ref[rows, :] = (da + db).astype(dx_ref.dtype)
            dal_ref[...] += jnp.sum(dg * gval, axis=0, keepdims=True)
            ddt_ref[...] += jnp.sum(da, axis=0, keepdims=True)
            return carry

        lax.fori_loop(0, t // CHUNK, step, 0)

    vec = pl.BlockSpec((1, HEAD_DIM), lambda i: (0, 0))
    full = pl.BlockSpec((t, HEAD_DIM), lambda i: (0, 0))
    return pl.pallas_call(
        body, name="dn_gates_bwd", grid=(1,),
        in_specs=[pl.BlockSpec((t, HEAD_DIM), lambda i: (0, TAIL_BLK)), vec, vec, full],
        out_specs=[full, vec, vec],
        out_shape=[jax.ShapeDtypeStruct((t, HEAD_DIM), MM), jax.ShapeDtypeStruct((1, HEAD_DIM), F32),
                   jax.ShapeDtypeStruct((1, HEAD_DIM), F32)],
        compiler_params=_cparams(("arbitrary",)),
    )(proj, a_log, dt_bias, dgates)


def _dn_intra(q, k, v, gcol, grow, bcol):
    r, c = _iota2(CHUNK, CHUNK)
    causal, strict = r >= c, r > c
    decay = jnp.where(causal, jnp.exp(jnp.where(causal, gcol - grow, 0.0)), 0.0)
    kb = k * bcol
    a = jnp.where(strict, mmul(kb, k, 1, 1, False) * decay, 0.0)
    tm = tri_inv(a)
    u = mmul(tm, v * bcol, 1, 0, False)
    w = mmul(tm, kb * jnp.exp(gcol), 1, 0, False)
    qk = jnp.where(causal, mmul(q, k, 1, 1, False) * decay, 0.0)
    rr = lax.broadcasted_iota(jnp.int32, (CHUNK, 1), 0)
    g_last = jnp.sum(jnp.where(rr == CHUNK - 1, gcol, 0.0), axis=0, keepdims=True)
    return u, w, q * jnp.exp(gcol), k * jnp.exp(g_last - gcol), qk, jnp.exp(g_last)


def _dn_scan(u, w, qg, kd, qk, eg, state):
    v_new = u - mmul(w, state, 1, 0, False)
    out = mmul(qg, state, 1, 0, False) + mmul(qk, v_new, 1, 0, False)
    return out, state * eg + mmul(kd, v_new, 0, 0, False)


DN_HEADS_PER_STEP = 1
DN_GROUP = 8
DN_PARTS = ((CHUNK, HEAD_DIM),) * 4 + ((CHUNK, CHUNK), (1, 1))


def _dn_scratch(hb, nc):
    return [pltpu.VMEM((hb, nc) + shape, F32) for shape in DN_PARTS]


def _dn_part_specs(hb, nc):
    return [pl.BlockSpec((hb, nc) + shape, lambda h: (h, 0, 0, 0)) for shape in DN_PARTS]


def _dn_group(nc):
    return min(DN_GROUP, nc)


def _dn_group_args(refs, j, g, grp):
    q_ref, k_ref, v_ref, gc_ref, gr_ref, bc_ref = refs
    rows = pl.ds(pl.multiple_of(g * (grp * CHUNK), grp * CHUNK), grp * CHUNK)
    cs = pl.ds(g * grp, grp)
    split = lambda ref: ref[rows, _lanes(j)].reshape(grp, CHUNK, HEAD_DIM)
    return split(q_ref), split(k_ref), split(v_ref), gc_ref[j, cs], gr_ref[j, cs], bc_ref[j, cs]


def _dn_intra_all(refs, parts, hb, nc):
    grp = _dn_group(nc)

    def group(g, carry):
        cs = pl.ds(g * grp, grp)
        for j in range(hb):
            for part, val in zip(parts, jax.vmap(_dn_intra)(*_dn_group_args(refs, j, g, grp))):
                part[j, cs] = val
        return carry

    lax.fori_loop(0, nc // grp, group, 0)


def _dn_specs(t):
    nc, hb = t // CHUNK, DN_HEADS_PER_STEP
    head = lambda which: pl.BlockSpec((None, t, hb * HEAD_DIM), lambda h: (which, 0, h))
    flat = pl.BlockSpec((t, hb * HEAD_DIM), lambda h: (0, h))
    col = pl.BlockSpec((hb, nc, CHUNK, 1), lambda h: (h, 0, 0, 0))
    row = pl.BlockSpec((hb, nc, 1, CHUNK), lambda h: (h, 0, 0, 0))
    st = pl.BlockSpec((hb, nc, HEAD_DIM, HEAD_DIM), lambda h: (h, 0, 0, 0))
    return nc, hb, head, flat, col, row, st


def dn_core_fwd(qkv, gcol, grow, bcol):
    t = qkv.shape[1]
    nc, hb, head, flat, col, row, st = _dn_specs(t)

    def body(q_ref, k_ref, v_ref, gc_ref, gr_ref, bc_ref, o_ref, s_ref, *parts):
        _dn_intra_all((q_ref, k_ref, v_ref, gc_ref, gr_ref, bc_ref), parts, hb, nc)

        def step(c, states):
            rows = pl.ds(pl.multiple_of(c * CHUNK, CHUNK), CHUNK)
            new_states = []
            for j in range(hb):
                s_ref[j, c] = states[j]
                out, new_state = _dn_scan(*[part[j, c] for part in parts], states[j])
                o_ref[rows, _lanes(j)] = out
                new_states.append(new_state)
            return tuple(new_states)

        lax.fori_loop(0, nc, step, tuple(jnp.zeros((HEAD_DIM, HEAD_DIM), F32) for _ in range(hb)))

    outs = pl.pallas_call(
        body, name="dn_core_fwd", grid=(N_HEADS // hb,),
        in_specs=[head(0), head(1), head(2), col, row, col], out_specs=[flat, st] + _dn_part_specs(hb, nc),
        out_shape=[jax.ShapeDtypeStruct((t, D_MODEL), F32), jax.ShapeDtypeStruct((N_HEADS, nc, HEAD_DIM, HEAD_DIM), F32)]
        + [jax.ShapeDtypeStruct((N_HEADS, nc) + shape, F32) for shape in DN_PARTS],
        compiler_params=_cparams(("parallel",)),
    )(qkv, qkv, qkv, gcol, grow, bcol)
    return outs[0], outs[1], tuple(outs[2:])


def dn_core_bwd(qkv, gcol, grow, bcol, states, parts, do):
    t = qkv.shape[1]
    nc, hb, head, flat, col, row, st = _dn_specs(t)
    n_parts = len(DN_PARTS)

    def body(q_ref, k_ref, v_ref, gc_ref, gr_ref, bc_ref, s_ref, do_ref, *rest):
        parts, (dqkv_ref, dgc_ref, dgr_ref, dbc_ref), dparts = rest[:n_parts], rest[n_parts:n_parts + 4], rest[n_parts + 4:]
        refs = (q_ref, k_ref, v_ref, gc_ref, gr_ref, bc_ref)

        def step(i, dstates):
            c = nc - 1 - i
            rows = pl.ds(pl.multiple_of(c * CHUNK, CHUNK), CHUNK)
            dstates_in = []
            for j in range(hb):
                _, vjp = jax.vjp(_dn_scan, *[part[j, c] for part in parts], s_ref[j, c])
                *dvals, dstate_in = vjp((do_ref[rows, _lanes(j)], dstates[j]))
                for dpart, dval in zip(dparts, dvals):
                    dpart[j, c] = dval
                dstates_in.append(dstate_in)
            return tuple(dstates_in)

        lax.fori_loop(0, nc, step, tuple(jnp.zeros((HEAD_DIM, HEAD_DIM), F32) for _ in range(hb)))

        grp = _dn_group(nc)

        def group(g, carry):
            rows = pl.ds(pl.multiple_of(g * (grp * CHUNK), grp * CHUNK), grp * CHUNK)
            cs = pl.ds(g * grp, grp)
            for j in range(hb):
                _, vjp = jax.vjp(jax.vmap(_dn_intra), *_dn_group_args(refs, j, g, grp))
                dq, dk, dv, dgc, dgr, dbc = vjp(tuple(dpart[j, cs] for dpart in dparts))
                for which, val in enumerate((dq, dk, dv)):
                    dqkv_ref[which, rows, _lanes(j)] = val.reshape(grp * CHUNK, HEAD_DIM)
                dgc_ref[j, cs] = dgc
                dgr_ref[j, cs] = dgr
                dbc_ref[j, cs] = dbc
            return carry

        lax.fori_loop(0, nc // grp, group, 0)

    return pl.pallas_call(
        body, name="dn_core_bwd", grid=(N_HEADS // hb,), scratch_shapes=_dn_scratch(hb, nc),
        in_specs=[head(0), head(1), head(2), col, row, col, st, flat] + _dn_part_specs(hb, nc),
        out_specs=[pl.BlockSpec((3, t, hb * HEAD_DIM), lambda h: (0, 0, h)), col, row, col],
        out_shape=[jax.ShapeDtypeStruct((3, t, D_MODEL), F32)] + [
            jax.ShapeDtypeStruct((N_HEADS, nc, CHUNK, 1), F32), jax.ShapeDtypeStruct((N_HEADS, nc, 1, CHUNK), F32),
            jax.ShapeDtypeStruct((N_HEADS, nc, CHUNK, 1), F32)],
        compiler_params=_cparams(("parallel",)),
    )(qkv, qkv, qkv, gcol, grow, bcol, states, do, *parts)


def gates_to_heads(gates):
    t = gates.shape[0]
    nc = t // CHUNK
    g = gates[:, :N_HEADS].T.reshape(N_HEADS, nc, CHUNK)
    b = gates[:, N_HEADS:2 * N_HEADS].T.reshape(N_HEADS, nc, CHUNK)
    return g[..., None], g[:, :, None, :], b[..., None]


def heads_to_gates(dgcol, dgrow, dbcol):
    nh, nc = dgcol.shape[:2]
    dg = (dgcol[..., 0] + dgrow[:, :, 0, :]).reshape(nh, nc * CHUNK).T
    db = dbcol[..., 0].reshape(nh, nc * CHUNK).T
    return jnp.concatenate([dg, db, jnp.zeros((nc * CHUNK, HEAD_DIM - 2 * nh), F32)], axis=1)


def _dn_out(o, z, w):
    return _rms(o, w) * _silu(z)


def _gate_specs():
    o_spec = pl.BlockSpec((ROWS, D_MODEL), lambda i: (i, 0))
    z_spec = pl.BlockSpec((ROWS, D_MODEL), lambda i: (i, 3))
    w_spec = pl.BlockSpec((1, HEAD_DIM), lambda i: (0, 0))
    return o_spec, z_spec, w_spec


def dn_out_fwd(o, proj, w):
    t = o.shape[0]
    o_spec, z_spec, w_spec = _gate_specs()

    def body(o_ref, z_ref, w_ref, y_ref):
        for h in range(N_HEADS):
            y_ref[:, _lanes(h)] = _dn_out(o_ref[:, _lanes(h)], z_ref[:, _lanes(h)], w_ref[...]).astype(y_ref.dtype)

    return pl.pallas_call(
        body, name="dn_out_fwd", grid=(t // ROWS,), in_specs=[o_spec, z_spec, w_spec], out_specs=o_spec,
        out_shape=jax.ShapeDtypeStruct((t, D_MODEL), MM), compiler_params=_cparams(("parallel",)),
    )(o, proj, w)


def dn_out_bwd(o, proj, w, dcat):
    t = o.shape[0]
    o_spec, z_spec, w_spec = _gate_specs()

    def body(o_ref, z_ref, w_ref, g_ref, do_ref, dz_ref, dw_ref):
        dw_sum = jnp.zeros((1, HEAD_DIM), F32)
        for h in range(N_HEADS):
            _, vjp = jax.vjp(_dn_out, o_ref[:, _lanes(h)], z_ref[:, _lanes(h)], w_ref[...])
            do, dz, dw = vjp(g_ref[:, _lanes(h)].astype(F32))
            do_ref[:, _lanes(h)] = do
            dz_ref[:, _lanes(h)] = dz.astype(dz_ref.dtype)
            dw_sum = dw_sum + dw
        _acc(dw_ref, dw_sum, pl.program_id(0) == 0)

    return pl.pallas_call(
        body, name="dn_out_bwd", grid=(t // ROWS,), in_specs=[o_spec, z_spec, w_spec, o_spec],
        out_specs=[o_spec, o_spec, w_spec],
        out_shape=[jax.ShapeDtypeStruct((t, D_MODEL), F32), jax.ShapeDtypeStruct((t, D_MODEL), MM),
                   jax.ShapeDtypeStruct((1, HEAD_DIM), F32)],
        compiler_params=_cparams(("arbitrary",)),
    )(o, proj, w, dcat)


def _fox_norm(x, w, scale):
    return _rms(x, w) * scale


def _fox_prep_specs():
    x_spec = pl.BlockSpec((ROWS, 2 * D_MODEL), lambda i: (i, 0))
    w_spec = pl.BlockSpec((2, 1, HEAD_DIM), lambda i: (0, 0, 0))
    y_spec = pl.BlockSpec((2, ROWS, D_MODEL), lambda i: (0, i, 0))
    return x_spec, w_spec, y_spec


def fox_prep_fwd(proj, wqk):
    t = proj.shape[0]
    x_spec, w_spec, y_spec = _fox_prep_specs()

    def body(x_ref, w_ref, y_ref):
        for j in range(2 * N_HEADS):
            which, scale = j // N_HEADS, (QSCALE if j < N_HEADS else 1.0)
            y_ref[which, :, _lanes(j % N_HEADS)] = _fox_norm(x_ref[:, _lanes(j)], w_ref[which], scale).astype(y_ref.dtype)

    return pl.pallas_call(
        body, name="fox_prep_fwd", grid=(t // ROWS,), in_specs=[x_spec, w_spec], out_specs=y_spec,
        out_shape=jax.ShapeDtypeStruct((2, t, D_MODEL), MM), compiler_params=_cparams(("parallel",)),
    )(proj, wqk)


def fox_prep_bwd(proj, wqk, dq, dk):
    t = proj.shape[0]
    x_spec, w_spec, _ = _fox_prep_specs()
    g_spec = pl.BlockSpec((ROWS, D_MODEL), lambda i: (i, 0))

    def body(x_ref, w_ref, dq_ref, dk_ref, dx_ref, dw_ref):
        dws = [jnp.zeros((1, HEAD_DIM), F32), jnp.zeros((1, HEAD_DIM), F32)]
        for j in range(2 * N_HEADS):
            which, scale = j // N_HEADS, (QSCALE if j < N_HEADS else 1.0)
            g_ref = dq_ref if which == 0 else dk_ref
            _, vjp = jax.vjp(lambda x, w: _fox_norm(x, w, scale), x_ref[:, _lanes(j)], w_ref[which])
            dx, dw = vjp(g_ref[:, _lanes(j % N_HEADS)])
            dx_ref[:, _lanes(j)] = dx.astype(dx_ref.dtype)
            dws[which] = dws[which] + dw
        first = pl.program_id(0) == 0
        _acc(dw_ref.at[0], dws[0], first)
        _acc(dw_ref.at[1], dws[1], first)

    return pl.pallas_call(
        body, name="fox_prep_bwd", grid=(t // ROWS,), in_specs=[x_spec, w_spec, g_spec, g_spec],
        out_specs=[x_spec, w_spec],
        out_shape=[jax.ShapeDtypeStruct((t, 2 * D_MODEL), MM), jax.ShapeDtypeStruct((2, 1, HEAD_DIM), F32)],
        compiler_params=_cparams(("arbitrary",)),
    )(proj, wqk, dq, dk)


def _row_pick(x, i):
    r = lax.broadcasted_iota(jnp.int32, x.shape, 0)
    return jnp.sum(jnp.where(r == i, x, 0.0), axis=0, keepdims=True)


def fox_gates_fwd(proj, f_bias):
    t = proj.shape[0]
    blk = HEAD_DIM

    def body(x_ref, b_ref, o_ref):
        lane = lax.broadcasted_iota(jnp.int32, (blk, HEAD_DIM), 1)
        tri = _tri_ones(blk, False)

        def step(c, carry):
            rows = pl.ds(pl.multiple_of(c * blk, blk), blk)
            lf = jnp.where(lane < N_HEADS, -_softplus(-(x_ref[rows, :] + b_ref[...])), 0.0)
            cum = _dot(tri, lf, 1, 0, True) + carry
            o_ref[rows, :] = cum
            return _row_pick(cum, blk - 1)

        lax.fori_loop(0, t // blk, step, jnp.zeros((1, HEAD_DIM), F32))

    vec = pl.BlockSpec((1, HEAD_DIM), lambda i: (0, 0))
    return pl.pallas_call(
        body, name="fox_gates_fwd", grid=(1,),
        in_specs=[pl.BlockSpec((t, HEAD_DIM), lambda i: (0, TAIL_BLK)), vec],
        out_specs=pl.BlockSpec((t, HEAD_DIM), lambda i: (0, 0)),
        out_shape=jax.ShapeDtypeStruct((t, HEAD_DIM), F32), compiler_params=_cparams(("arbitrary",)),
    )(proj, f_bias)


def fox_gates_bwd(proj, f_bias, dfcum):
    t = proj.shape[0]
    blk = HEAD_DIM
    nb = t // blk

    def body(x_ref, b_ref, g_ref, dx_ref, db_ref):
        lane = lax.broadcasted_iota(jnp.int32, (blk, HEAD_DIM), 1)
        tri = _tri_ones(blk, True)
        db_ref[...] = jnp.zeros_like(db_ref)

        def step(i, carry):
            c = nb - 1 - i
            rows = pl.ds(pl.multiple_of(c * blk, blk), blk)
            g = jnp.where(lane < N_HEADS, g_ref[rows, :], 0.0)
            dlf = _dot(tri, g, 1, 0, True) + carry
            dx = jnp.where(lane < N_HEADS, dlf * _sigmoid(-(x_ref[rows, :] + b_ref[...])), 0.0)
            dx_ref[rows, :] = dx.astype(dx_ref.dtype)
            db_ref[...] += jnp.sum(dx, axis=0, keepdims=True)
            return carry + jnp.sum(g, axis=0, keepdims=True)

        lax.fori_loop(0, nb, step, jnp.zeros((1, HEAD_DIM), F32))

    vec = pl.BlockSpec((1, HEAD_DIM), lambda i: (0, 0))
    full = pl.BlockSpec((t, HEAD_DIM), lambda i: (0, 0))
    return pl.pallas_call(
        body, name="fox_gates_bwd", grid=(1,),
        in_specs=[pl.BlockSpec((t, HEAD_DIM), lambda i: (0, TAIL_BLK)), vec, full], out_specs=[full, vec],
        out_shape=[jax.ShapeDtypeStruct((t, HEAD_DIM), MM), jax.ShapeDtypeStruct((1, HEAD_DIM), F32)],
        compiler_params=_cparams(("arbitrary",)),
    )(proj, f_bias, dfcum)


def fcum_to_heads(fcum):
    f = fcum[:, :N_HEADS].T
    return f[:, :, None], f[:, None, :]


def heads_to_fcum(dfcol, dfrow):
    d = (dfcol[:, :, 0] + dfrow[:, 0, :]).T
    return jnp.concatenate([d, jnp.zeros((d.shape[0], HEAD_DIM - N_HEADS), F32)], axis=1)


def _fox_tq(t):
    return min(t, 256)


def _fox_specs(t):
    tq = _fox_tq(t)
    q_spec = pl.BlockSpec((None, tq, HEAD_DIM), lambda h, i: (0, i, h))
    k_spec = pl.BlockSpec((None, t, HEAD_DIM), lambda h, i: (1, 0, h))
    v_spec = pl.BlockSpec((t, HEAD_DIM), lambda h, i: (0, 2 * N_HEADS + h))
    gate_spec = pl.BlockSpec((tq, HEAD_DIM), lambda h, i: (i, 3 * N_HEADS + h))
    col_spec = pl.BlockSpec((None, tq, 1), lambda h, i: (h, i, 0))
    row_spec = pl.BlockSpec((None, 1, t), lambda h, i: (h, 0, 0))
    blk_spec = pl.BlockSpec((tq, HEAD_DIM), lambda h, i: (i, h))
    head_spec = pl.BlockSpec((t, HEAD_DIM), lambda h, i: (0, h))
    return tq, q_spec, k_spec, v_spec, gate_spec, col_spec, row_spec, blk_spec, head_spec


def _fox_segments(i, tq):
    return ([(0, i * tq, False)] if i else []) + [(i * tq, (i + 1) * tq, True)]


def _fox_scores(q_ref, k_ref, fc_ref, fr_ref, lo, hi, causal):
    s = _dot(q_ref[...], k_ref[lo:hi, :], 1, 1, False) + (fc_ref[...] - fr_ref[:, lo:hi])
    if not causal:
        return s, None
    r, c = _iota2(hi - lo, hi - lo)
    return s, c <= r


def fox_attn_fwd(qk, proj, fcol, frow):
    t = proj.shape[0]
    tq, q_spec, k_spec, v_spec, gate_spec, col_spec, row_spec, blk_spec, _ = _fox_specs(t)

    def body(q_ref, k_ref, v_ref, gate_ref, fc_ref, fr_ref, mix_ref, o_ref, lse_ref):
        def block(i):
            segs = _fox_segments(i, tq)
            scores = [_fox_scores(q_ref, k_ref, fc_ref, fr_ref, *seg) for seg in segs]
            scores = [(s if mask is None else jnp.where(mask, s, -1e30), mask) for s, mask in scores]
            m = functools.reduce(jnp.maximum, [jnp.max(s, axis=-1, keepdims=True) for s, _ in scores])
            l, o = 0.0, 0.0
            for (lo, hi, _), (s, mask) in zip(segs, scores):
                p = jnp.exp(s - m)
                p = p if mask is None else jnp.where(mask, p, 0.0)
                l = l + jnp.sum(p, axis=-1, keepdims=True)
                o = o + _dot(p, v_ref[lo:hi, :], 1, 0, False)
            o = o / l
            o_ref[...] = o
            mix_ref[...] = (o * _sigmoid(gate_ref[...])).astype(mix_ref.dtype)
            lse_ref[...] = m + jnp.log(l)

        for i in range(t // tq):
            pl.when(pl.program_id(1) == i)(functools.partial(block, i))

    return pl.pallas_call(
        body, name="fox_attn_fwd", grid=(N_HEADS, t // tq),
        in_specs=[q_spec, k_spec, v_spec, gate_spec, col_spec, row_spec], out_specs=[blk_spec, blk_spec, col_spec],
        out_shape=[jax.ShapeDtypeStruct((t, D_MODEL), MM), jax.ShapeDtypeStruct((t, D_MODEL), F32),
                   jax.ShapeDtypeStruct((N_HEADS, t, 1), F32)],
        compiler_params=_cparams(("parallel", "parallel")),
    )(qk, qk, proj, proj, fcol, frow)


def fox_attn_bwd(qk, proj, fcol, frow, o, lse, dcat):
    t = proj.shape[0]
    tq, q_spec, k_spec, v_spec, gate_spec, col_spec, row_spec, blk_spec, head_spec = _fox_specs(t)

    def body(q_ref, k_ref, v_ref, gate_ref, fc_ref, fr_ref, o_ref, lse_ref, g_ref,
             dq_ref, dk_ref, dv_ref, dgate_ref, dfc_ref, dfr_ref):
        @pl.when(pl.program_id(1) == 0)
        def _():
            dk_ref[...] = jnp.zeros_like(dk_ref)
            dv_ref[...] = jnp.zeros_like(dv_ref)
            dfr_ref[...] = jnp.zeros_like(dfr_ref)

        def block(i):
            sg = _sigmoid(gate_ref[...])
            g = g_ref[...].astype(F32)
            o_pre = o_ref[...]
            do = g * sg
            dgate_ref[...] = (g * o_pre * sg * (1.0 - sg)).astype(dgate_ref.dtype)
            delta = jnp.sum(do * o_pre, axis=-1, keepdims=True)
            dq, dfc = 0.0, 0.0
            for lo, hi, causal in _fox_segments(i, tq):
                s, mask = _fox_scores(q_ref, k_ref, fc_ref, fr_ref, lo, hi, causal)
                if causal:
                    p = jnp.where(mask, jnp.exp(jnp.where(mask, s, 0.0) - lse_ref[...]), 0.0)
                else:
                    p = jnp.exp(s - lse_ref[...])
                ds = p * (_dot(do, v_ref[lo:hi, :], 1, 1, False) - delta)
                dq = dq + _dot(ds, k_ref[lo:hi, :], 1, 0, False)
                dk_ref[lo:hi, :] += _dot(ds, q_ref[...], 0, 0, False)
                dv_ref[lo:hi, :] += _dot(p, do, 0, 0, False)
                dfc = dfc + jnp.sum(ds, axis=-1, keepdims=True)
                dfr_ref[:, lo:hi] += -jnp.sum(ds, axis=0, keepdims=True)
            dq_ref[...] = dq
            dfc_ref[...] = dfc

        for i in range(t // tq):
            pl.when(pl.program_id(1) == i)(functools.partial(block, i))

    f32 = lambda *s: jax.ShapeDtypeStruct(s, F32)
    return pl.pallas_call(
        body, name="fox_attn_bwd", grid=(N_HEADS, t // tq),
        in_specs=[q_spec, k_spec, v_spec, gate_spec, col_spec, row_spec, blk_spec, col_spec, blk_spec],
        out_specs=[blk_spec, head_spec, head_spec, blk_spec, col_spec, row_spec],
        out_shape=[f32(t, D_MODEL), f32(t, D_MODEL), f32(t, D_MODEL), jax.ShapeDtypeStruct((t, D_MODEL), MM),
                   f32(N_HEADS, t, 1), f32(N_HEADS, 1, t)],
        compiler_params=_cparams(("parallel", "arbitrary")),
    )(qk, qk, proj, proj, fcol, frow, o, lse, dcat)


def adamw(w, g, m, v, *, name):
    r, c = w.shape
    rb = ROWS if r % ROWS == 0 else r

    def body(w_ref, g_ref, m_ref, v_ref, d_ref, nm_ref, nv_ref):
        g_ = g_ref[...]
        m_ = ADAM_B1 * m_ref[...] + (1.0 - ADAM_B1) * g_
        v_ = ADAM_B2 * v_ref[...] + (1.0 - ADAM_B2) * jnp.square(g_)
        m_hat = m_ / (1.0 - ADAM_B1 ** ADAM_STEP)
        v_hat = v_ / (1.0 - ADAM_B2 ** ADAM_STEP)
        d_ref[...] = -ADAM_LR * (m_hat / (jnp.sqrt(v_hat) + ADAM_EPS) + ADAM_WD * w_ref[...])
        nm_ref[...] = m_
        nv_ref[...] = v_

    blk = pl.BlockSpec((rb, c), lambda i: (i, 0))
    shp = jax.ShapeDtypeStruct((r, c), F32)
    return pl.pallas_call(body, name=name, grid=(r // rb,), in_specs=[blk] * 4, out_specs=[blk] * 3,
                          out_shape=[shp] * 3, compiler_params=_cparams(("parallel",)))(w, g, m, v)


def _place():
    x, y, c = lax.axis_index("x"), lax.axis_index("y"), lax.axis_index("c")
    return x, y, c, [(1 - x, y), (x, 1 - y), (1 - x, 1 - y)]


ANY = pl.BlockSpec(memory_space=pl.ANY)


def all_reduce_small(v):
    r, w = v.shape

    def body(v_ref, o_ref, buf, send_sems, recv_sems):
        x, y, c, _ = _place()
        me = 4 * x + 2 * y + c
        flip = lambda a, bit: 1 - a if bit else a
        cps = []
        for k in range(1, N_DEV):
            peer = (flip(x, k & 4), flip(y, k & 2), flip(c, k & 1))
            cp = pltpu.make_async_remote_copy(src_ref=v_ref, dst_ref=buf.at[me], send_sem=send_sems.at[k - 1],
                                              recv_sem=recv_sems.at[k - 1], device_id=peer, device_id_type=MESH)
            cp.start()
            cps.append((cp, 4 * peer[0] + 2 * peer[1] + peer[2]))
        buf[me] = v_ref[...]
        for k, (cp, peer_id) in enumerate(cps):
            pltpu.make_async_remote_copy(src_ref=v_ref, dst_ref=buf.at[peer_id], send_sem=send_sems.at[k],
                                         recv_sem=recv_sems.at[k], device_id=(x, y, c), device_id_type=MESH).wait_recv()
        for cp, _ in cps:
            cp.wait_send()
        acc = buf[0]
        for d in range(1, N_DEV):
            acc = acc + buf[d]
        o_ref[...] = acc

    vm = pl.BlockSpec(memory_space=pltpu.VMEM)
    return pl.pallas_call(
        body, name="all_reduce_small", in_specs=[vm], out_specs=vm, out_shape=jax.ShapeDtypeStruct((r, w), F32),
        scratch_shapes=[pltpu.VMEM((N_DEV, r, w), F32), pltpu.SemaphoreType.DMA((N_DEV - 1,)),
                        pltpu.SemaphoreType.DMA((N_DEV - 1,))],
    )(v)


def _vec8(v):
    return jnp.zeros((1, HEAD_DIM), F32).at[0, :N_HEADS].set(v.reshape(N_HEADS))


def _layer_fwd(i, x_in, wt, sm, mem_k, mem_v, late=None):
    tag = f"l{i}_"
    h = rms_fwd(x_in, sm["norm1_w"][i][None], name=tag + "rms1")
    w_in = wt["dn_w_in"] if i == 0 else wt["fox_w_in"]
    proj = matmul(h, w_in, name=tag + "proj", tm=256, tk=1024)
    sv = dict(x_in=x_in, h=h, proj=proj)
    if i == 0:
        qkv = dn_prep_fwd(proj, wt["conv_w"])
        gates = dn_gates_fwd(proj, _vec8(sm["dn_a_log"]), _vec8(sm["dn_dt_bias"]))
        gcol, grow, bcol = gates_to_heads(gates)
        o, states, parts = dn_core_fwd(qkv, gcol, grow, bcol)
        mix = dn_out_fwd(o, proj, sm["dn_o_norm_w"])
        sv.update(qkv=qkv, gcol=gcol, grow=grow, bcol=bcol, states=states, parts=parts, o=o)
    else:
        wqk = jnp.stack([sm["fox_q_norm_w"], sm["fox_k_norm_w"]])
        qk = fox_prep_fwd(proj, wqk)
        fcum = fox_gates_fwd(proj, _vec8(sm["fox_f_bias"]))
        fcol, frow = fcum_to_heads(fcum)
        mix, o, lse = fox_attn_fwd(qk, proj, fcol, frow)
        sv.update(wqk=wqk, qk=qk, fcol=fcol, frow=frow, o=o, lse=lse)
    mem_out = memattn_fwd(proj, sm["memq_norm_w"][i][None], mem_k, mem_v, name=tag + "memattn_fwd")
    cat = jnp.concatenate([mix, mem_out], axis=1)
    if late is not None:
        wt.update(late(cat))
    x_mid = matmul(cat, wt["w_out"][i], res=x_in, name=tag + "out_proj")
    h2 = rms_fwd(x_mid, sm["norm2_w"][i][None], name=tag + "rms2")
    ff, act = matmul(h2, wt["w_mlp1"][i], b_slots=True, also_sqrelu=True, out_dtype=MM, name=tag + "mlp1")
    x_out = matmul(act, wt["w_mlp2"][i], res=x_mid, name=tag + "mlp2")
    sv.update(cat=cat, x_mid=x_mid, h2=h2, ff=ff, act=act)
    return x_out, sv


def _layer_bwd(i, dx_out, sv, wt, sm, mem_k, mem_v, on_mlp=None):
    tag = f"l{i}_"
    big, small = {}, {}
    dff = matmul(dx_out, wt["w_mlp2"][i], tb=True, times_dsqrelu=sv["ff"], out_dtype=MM, name=tag + "d_ff")
    big["w_mlp2"] = matmul(sv["act"], dx_out, ta=True, name=tag + "d_w_mlp2", tk=2048)
    dh2 = matmul(dff, wt["w_mlp1"][i], tb=True, b_slots=True, name=tag + "d_h2")
    big["w_mlp1"] = matmul(sv["h2"], dff, ta=True, name=tag + "d_w_mlp1", tm=512, tn=D_FF, tk=512)
    dx_mid, small["norm2_w"] = rms_bwd(sv["x_mid"], sm["norm2_w"][i][None], dh2, dx_out, name=tag + "rms2_bwd")
    dcat = matmul(dx_mid, wt["w_out"][i], tb=True, name=tag + "d_cat")
    big["w_out"] = matmul(sv["cat"], dx_mid, ta=True, name=tag + "d_w_out", tk=2048)
    proj = sv["proj"]
    memq_norm_w = sm["memq_norm_w"][i][None]
    if on_mlp is not None:
        memq_norm_w = memq_norm_w + on_mlp(big["w_mlp2"], big["w_mlp1"], big["w_out"])
    dqm, small["memq_norm_w"], dmk, dmv = memattn_bwd(proj, memq_norm_w, mem_k, mem_v, dcat, name=tag + "memattn_bwd")
    t = proj.shape[0]
    pad = jnp.zeros((t, PROJ_W - TAIL - HEAD_DIM), MM)
    if i == 0:
        do, dz, small["dn_o_norm_w"] = dn_out_bwd(sv["o"], proj, sm["dn_o_norm_w"], dcat)
        dqkv, dgc, dgr, dbc = dn_core_bwd(sv["qkv"], sv["gcol"], sv["grow"], sv["bcol"], sv["states"], sv["parts"], do)
        dtail, dal, ddt = dn_gates_bwd(proj, _vec8(sm["dn_a_log"]), _vec8(sm["dn_dt_bias"]), heads_to_gates(dgc, dgr, dbc))
        dmain, dconv = dn_prep_bwd(proj, wt["conv_w"], dqkv)
        small["dn_a_log"], small["dn_dt_bias"] = dal[:, :N_HEADS], ddt[:, :N_HEADS]
        big["conv_w"] = dconv[:4]
        dproj = jnp.concatenate([dmain, dz, dqm, dtail, pad], axis=1)
    else:
        dq, dk, dv, dgate, dfc, dfr = fox_attn_bwd(sv["qk"], proj, sv["fcol"], sv["frow"], sv["o"], sv["lse"], dcat)
        dtail, dfb = fox_gates_bwd(proj, _vec8(sm["fox_f_bias"]), heads_to_fcum(dfc, dfr))
        dqk, dwqk = fox_prep_bwd(proj, sv["wqk"], dq, dk)
        small["fox_f_bias"] = dfb[:, :N_HEADS]
        small["fox_q_norm_w"], small["fox_k_norm_w"] = dwqk[0], dwqk[1]
        dproj = jnp.concatenate([dqk, dv.astype(MM), dgate, dqm, dtail, pad], axis=1)
    w_in = wt["dn_w_in"] if i == 0 else wt["fox_w_in"]
    dh = matmul(dproj, w_in, tb=True, name=tag + "d_h", tm=512)
    big["w_in"] = matmul(sv["h"], dproj, ta=True, name=tag + "d_w_in", tm=256)
    dx_in, small["norm1_w"] = rms_bwd(sv["x_in"], sm["norm1_w"][i][None], dh, dx_mid, name=tag + "rms1_bwd")
    return dx_in, big, small, (dmk, dmv)


def local_step(x, mem, target, wt, sm, late=None, on_layer1=None, on_mlp0=None):
    wt = dict(wt)
    mem_k, mem_v = mem_fwd(mem, sm["mem_norm_w"][None], wt["w_mem_kv"], sm["mem_k_norm_w"][None])
    x0, sv0 = _layer_fwd(0, x, wt, sm, mem_k, mem_v, late)
    x1, sv1 = _layer_fwd(1, x0, wt, sm, mem_k, mem_v)
    dy, loss = loss_fwd(x1, target, name="loss")
    dx1, big1, small1, dm1 = _layer_bwd(1, dy, sv1, wt, sm, mem_k, mem_v)
    if on_layer1 is not None:
        dx1 = dx1 + on_layer1(big1)
    dx0, big0, small0, dm0 = _layer_bwd(0, dx1, sv0, wt, sm, mem_k, mem_v, on_mlp0)
    dwn, dwkv, dwkn = mem_bwd(mem, sm["mem_norm_w"][None], wt["w_mem_kv"], sm["mem_k_norm_w"][None], *dm0, *dm1)
    small = dict(mem_norm_w=dwn[0], mem_k_norm_w=dwkn[0],
                 norm1_w=jnp.concatenate([small0["norm1_w"], small1["norm1_w"]]),
                 norm2_w=jnp.concatenate([small0["norm2_w"], small1["norm2_w"]]),
                 memq_norm_w=jnp.concatenate([small0["memq_norm_w"], small1["memq_norm_w"]]),
                 dn_a_log=small0["dn_a_log"], dn_dt_bias=small0["dn_dt_bias"], dn_o_norm_w=small0["dn_o_norm_w"],
                 fox_f_bias=small1["fox_f_bias"], fox_q_norm_w=small1["fox_q_norm_w"], fox_k_norm_w=small1["fox_k_norm_w"])
    big = dict(w_mem_kv=dwkv, dn_w_in=big0["w_in"], fox_w_in=big1["w_in"], conv_w=big0["conv_w"],
               w_out=[big0["w_out"], big1["w_out"]], w_mlp1=[big0["w_mlp1"], big1["w_mlp1"]],
               w_mlp2=[big0["w_mlp2"], big1["w_mlp2"]])
    return loss, dx0, big, small


def w_in_to_kernel(w, n_scalars):
    pad = jnp.zeros((w.shape[0], PROJ_W - TAIL - n_scalars), w.dtype)
    return jnp.concatenate([w[:, :4096], w[:, 4096 + n_scalars:], w[:, 4096:4096 + n_scalars], pad], axis=1)


def w_in_from_kernel(w, n_scalars):
    return jnp.concatenate([w[:, :4096], w[:, TAIL:TAIL + n_scalars], w[:, 4096:TAIL]], axis=1)


BIG_SPECS = dict(w_mem_kv=("rows", 1, 256, 1024), w_out=("rows", 2, 384, 1024), w_mlp2=("rows", 2, 1024, 1024),
                 w_mlp1=("cols", 2, 1024, 1024), dn_w_in=("rows", 1, 1024, 1156), fox_w_in=("rows", 1, 1024, 1154))
BIG_NAMES = tuple(BIG_SPECS)
EARLY_NAMES = ("w_mem_kv", "dn_w_in")
LATE_NAMES = ("w_out", "w_mlp2", "w_mlp1", "fox_w_in")
BIG_SPECS.update({f"{name}_{i}": (BIG_SPECS[name][0], 1) + BIG_SPECS[name][2:]
                  for name in ("w_out", "w_mlp2", "w_mlp1") for i in range(2)})
RS_LAYER1 = ("fox_w_in", "w_out_1", "w_mlp2_1", "w_mlp1_1")
RS_MLP0 = ("w_mlp2_0", "w_mlp1_0", "w_out_0")
RS_LAST = ("dn_w_in", "w_mem_kv")


def _full_shape(name, half=False):
    kind, a, b, c = BIG_SPECS[name]
    b = b // 2 if half else b
    return (a, N_CHIP, b, c) if kind == "rows" else (a, b, N_CHIP * c)


def _ds(start, size, align):
    return pl.ds(start if isinstance(start, int) else pl.multiple_of(start, align), size)


def _half_rows(name, h):
    b = BIG_SPECS[name][2]
    return _ds(h * (b // 2), b // 2, 16)


def _shard_idx(name, h):
    return (slice(None), _half_rows(name, h), slice(None))


def _full_idx(name, j=None, h=None):
    kind, _, _, c = BIG_SPECS[name]
    rows = slice(None) if h is None else _half_rows(name, h)
    if kind == "rows":
        return (slice(None), slice(None) if j is None else j, rows, slice(None))
    return (slice(None), rows, slice(None) if j is None else _ds(j * c, c, 128))


def _slots_shape(name):
    _, a, b, c = BIG_SPECS[name]
    return (a, N_CHIP, b, c)


def _slots_idx(name, j, h):
    return (slice(None), j, _half_rows(name, h), slice(None))


def _row_block(name):
    hs = BIG_SPECS[name][2] // 2
    return hs if hs <= ROWS else ROWS


def _remote(src, dst, send_sem, recv_sem, to):
    return pltpu.make_async_remote_copy(src_ref=src, dst_ref=dst, send_sem=send_sem, recv_sem=recv_sem, device_id=to,
                                        device_id_type=MESH)


HBM = pl.BlockSpec(memory_space=pltpu.HBM)
SEM = pl.BlockSpec(memory_space=pltpu.SEMAPHORE)
EFFECT = pltpu.CompilerParams(has_side_effects=pltpu.SideEffectType.DATAFLOW_SIDE_EFFECTING)


def _in_hbm(a):
    return pltpu.with_memory_space_constraint(a, pltpu.HBM)


def _chip_copies(names, ins, lands, send_sems, recv_sems):
    x, y, c, chips = _place()
    return [_remote(ins[a].at[_shard_idx(name, c)], lands[a].at[_slots_idx(name, 2 * x + y, c)], send_sems.at[3 * a + k],
                    recv_sems.at[3 * a + k], (chip[0], chip[1], c))
            for a, name in enumerate(names) for k, chip in enumerate(chips)]


def _copies_start(call_name, copies, sources, land_shapes):
    n = len(sources)

    def body(*refs):
        ins, lands, send_sems, recv_sems, token = refs[:n], refs[n:2 * n], refs[2 * n], refs[2 * n + 1], refs[-1]
        for cp in copies(ins, lands, send_sems, recv_sems):
            cp.start()
        token[...] = jnp.zeros_like(token)

    ins = [_in_hbm(a) for a in sources]
    lands = [_in_hbm(lax.empty(shape, MM)) for shape in land_shapes]
    sems = (pltpu.SemaphoreType.DMA((3 * n,)), pltpu.SemaphoreType.DMA((3 * n,)))
    outs = pl.pallas_call(
        body, name=call_name, in_specs=[HBM] * (2 * n),
        out_specs=(SEM, SEM) + (HBM,) * (2 * n) + (pl.BlockSpec(memory_space=pltpu.VMEM),),
        out_shape=sems + tuple(pltpu.HBM(a.shape, a.dtype) for a in ins + lands) + (jax.ShapeDtypeStruct((8, HEAD_DIM), F32),),
        input_output_aliases={a: 2 + a for a in range(2 * n)}, compiler_params=EFFECT,
    )(*ins, *lands)
    return outs[:-1], outs[-1]


def _copies_wait(call_name, copies, state, after):
    n = (len(state) - 2) // 2

    def body(*refs):
        send_sems, recv_sems, ins, lands = refs[0], refs[1], refs[2:2 + n], refs[2 + n:2 + 2 * n]
        for cp in copies(ins, lands, send_sems, recv_sems):
            cp.wait_send()
            cp.wait_recv()

    outs = pl.pallas_call(
        body, name=call_name, in_specs=[SEM, SEM] + [HBM] * (2 * n) + [ANY], out_specs=(HBM,) * (2 * n),
        out_shape=tuple(pltpu.HBM(a.shape, a.dtype) for a in state[2:]),
        input_output_aliases={2 + a: a for a in range(2 * n)}, compiler_params=EFFECT,
    )(*state, after)
    return outs[:n], outs[n:]


def all_gather_start(shards, names):
    return _copies_start("all_gather_start", functools.partial(_chip_copies, names), [shards[name] for name in names],
                         [_slots_shape(name) for name in names])


def all_gather_wait(state, names, after):
    ins, lands = _copies_wait("all_gather_wait", functools.partial(_chip_copies, names), state, after)
    return dict(zip(names, ins)), dict(zip(names, lands))


def _chip_sends(names, ins, lands, send_sems, recv_sems):
    x, y, c, chips = _place()
    return [_remote(ins[a].at[_full_idx(name, 2 * chip[0] + chip[1])], lands[a].at[k], send_sems.at[3 * a + k],
                    recv_sems.at[3 * a + k], (chip[0], chip[1], c))
            for a, name in enumerate(names) for k, chip in enumerate(chips)]


def _got_shape(name):
    _, a_, b_, c_ = BIG_SPECS[name]
    return (3, a_, b_ // 2, c_)


def rs_chip_start(pairs, names, tag):
    return _copies_start("rs_chip_start_" + tag, functools.partial(_chip_sends, names), [pairs[name] for name in names],
                         [_got_shape(name) for name in names])


def rs_chip_wait(state, names, tag, after):
    _, lands = _copies_wait("rs_chip_wait_" + tag, functools.partial(_chip_sends, names), state, after)
    return dict(zip(names, lands))


def all_gather_pass_on(lands, names):
    n = len(names)

    def body(*refs):
        outs, send_sems, recv_sems = refs[n:2 * n], refs[2 * n], refs[2 * n + 1]
        x, y, c, chips = _place()
        work = [(3 * a + k, a, name, 2 * chip[0] + chip[1]) for a, name in enumerate(names) for k, chip in enumerate(chips)]
        cps = []
        for s, a, name, slot in work:
            landed = outs[a].at[_slots_idx(name, slot, c)]
            cps.append(_remote(landed, landed, send_sems.at[s], recv_sems.at[s], (x, y, 1 - c)))
            cps[-1].start()
        for s, a, name, slot in work:
            passed = outs[a].at[_slots_idx(name, slot, 1 - c)]
            _remote(passed, passed, send_sems.at[s], recv_sems.at[s], (x, y, 1 - c)).wait_recv()
        for cp in cps:
            cp.wait_send()

    outs = pl.pallas_call(
        body, name="all_gather_pass_on", in_specs=[ANY] * n, out_specs=[ANY] * n,
        input_output_aliases={a: a for a in range(n)},
        out_shape=[jax.ShapeDtypeStruct(_slots_shape(name), MM) for name in names],
        scratch_shapes=[pltpu.SemaphoreType.DMA((3 * n,)), pltpu.SemaphoreType.DMA((3 * n,))],
    )(*[lands[name] for name in names])
    return dict(zip(names, outs))


def all_gather_big(shards, names):
    n = len(names)
    BIG_NAMES = names

    def body(*refs):
        ins, outs = refs[:n], refs[n:2 * n]
        send_sems, recv_sems, fsend_sems, frecv_sems = refs[2 * n:]
        x, y, c, chips = _place()
        me_chip, sibling = 2 * x + y, (x, y, 1 - c)
        work = [(3 * a + k, a, name, chip) for a, name in enumerate(BIG_NAMES) for k, chip in enumerate(chips)]
        sends = []
        for s, a, name, chip in work:
            cp = _remote(ins[a].at[_shard_idx(name, c)], outs[a].at[_slots_idx(name, me_chip, c)], send_sems.at[s],
                         recv_sems.at[s], (chip[0], chip[1], c))
            cp.start()
            sends.append(cp)
        for s, a, name, chip in work:
            landed = outs[a].at[_slots_idx(name, 2 * chip[0] + chip[1], c)]
            _remote(landed, landed, send_sems.at[s], recv_sems.at[s], (chip[0], chip[1], c)).wait_recv()
            cp = _remote(landed, landed, fsend_sems.at[s], frecv_sems.at[s], sibling)
            cp.start()
            sends.append(cp)
        for s, a, name, chip in work:
            passed = outs[a].at[_slots_idx(name, 2 * chip[0] + chip[1], 1 - c)]
            _remote(passed, passed, fsend_sems.at[s], frecv_sems.at[s], sibling).wait_recv()
        for cp in sends:
            cp.wait_send()

    outs = pl.pallas_call(
        body, name="all_gather_big", in_specs=[ANY] * n, out_specs=[ANY] * n,
        out_shape=[jax.ShapeDtypeStruct(_slots_shape(name), MM) for name in BIG_NAMES],
        scratch_shapes=[pltpu.SemaphoreType.DMA((3 * n,))] * 4,
    )(*[shards[name] for name in BIG_NAMES])
    return dict(zip(BIG_NAMES, outs))


def with_own_slot(name, full, shard, chip):
    return lax.dynamic_update_slice(full, shard[:, None], (0, chip, 0, 0))


def rs_pair_exchange_big(grads, names, tag):
    n = len(names)

    def body(*refs):
        ins, outs, send_sems, recv_sems = refs[:n], refs[n:2 * n], refs[2 * n], refs[2 * n + 1]
        x, y, c, _ = _place()
        cps = []
        for a, name in enumerate(names):
            cp = _remote(ins[a].at[_full_idx(name, None, 1 - c)], outs[a], send_sems.at[a], recv_sems.at[a], (x, y, 1 - c))
            cp.start()
            cps.append(cp)
        for cp in cps:
            cp.wait()

    outs = pl.pallas_call(
        body, name="rs_pair_exchange_" + tag, in_specs=[ANY] * n, out_specs=[ANY] * n,
        out_shape=[jax.ShapeDtypeStruct(_full_shape(name, half=True), F32) for name in names],
        scratch_shapes=[pltpu.SemaphoreType.DMA((n,)), pltpu.SemaphoreType.DMA((n,))],
    )(*[grads[name] for name in names])
    return dict(zip(names, outs))


def rs_pair_add_big(name, place, g, got):
    kind, a_, b_, c_ = BIG_SPECS[name]
    rb = _row_block(name)
    nb = (b_ // 2) // rb

    def body(place_ref, g_ref, got_ref, o_ref):
        o_ref[...] = (g_ref[...] + got_ref[...]).astype(o_ref.dtype)

    if kind == "rows":
        g_spec = pl.BlockSpec((None, None, rb, c_), lambda a, j, i, p: (a, j, p[0] * nb + i, 0))
        o_spec = pl.BlockSpec((None, None, rb, c_), lambda a, j, i, p: (a, j, i, 0))
    else:
        g_spec = pl.BlockSpec((None, rb, c_), lambda a, j, i, p: (a, p[0] * nb + i, j))
        o_spec = pl.BlockSpec((None, rb, c_), lambda a, j, i, p: (a, i, j))
    return pl.pallas_call(
        body, name="rs_pair_add_" + name,
        grid_spec=pltpu.PrefetchScalarGridSpec(num_scalar_prefetch=1, grid=(a_, N_CHIP, nb), in_specs=[g_spec, o_spec],
                                               out_specs=o_spec),
        out_shape=jax.ShapeDtypeStruct(_full_shape(name, half=True), MM),
        compiler_params=_cparams(("parallel", "parallel", "parallel")),
    )(place, g, got)


def rs_chip_exchange_big(pairs, names, tag):
    n = len(names)

    def body(*refs):
        ins, outs, send_sems, recv_sems = refs[:n], refs[n:2 * n], refs[2 * n], refs[2 * n + 1]
        cps = _chip_sends(names, ins, outs, send_sems, recv_sems)
        for cp in cps:
            cp.start()
        for cp in cps:
            cp.wait()

    outs = pl.pallas_call(
        body, name="rs_chip_exchange_" + tag, in_specs=[ANY] * n, out_specs=[ANY] * n,
        out_shape=[jax.ShapeDtypeStruct(_got_shape(name), MM) for name in names],
        scratch_shapes=[pltpu.SemaphoreType.DMA((3 * n,)), pltpu.SemaphoreType.DMA((3 * n,))],
    )(*[pairs[name] for name in names])
    return dict(zip(names, outs))


def rs_chip_add_big(name, place, g, got_pair, got_chips):
    kind, a_, b_, c_ = BIG_SPECS[name]
    rb = _row_block(name)
    nb = (b_ // 2) // rb

    def body(place_ref, g_ref, s_ref, r0_ref, r1_ref, r2_ref, o_ref):
        own = g_ref[...] + s_ref[...]
        o_ref[...] = ((own + r0_ref[...].astype(F32)) + r1_ref[...].astype(F32)) + r2_ref[...].astype(F32)

    if kind == "rows":
        g_spec = pl.BlockSpec((None, None, rb, c_), lambda a, i, p: (a, p[1], p[0] * nb + i, 0))
        s_spec = pl.BlockSpec((None, None, rb, c_), lambda a, i, p: (a, p[1], i, 0))
    else:
        g_spec = pl.BlockSpec((None, rb, c_), lambda a, i, p: (a, p[0] * nb + i, p[1]))
        s_spec = pl.BlockSpec((None, rb, c_), lambda a, i, p: (a, i, p[1]))
    r_spec = lambda k: pl.BlockSpec((None, None, rb, c_), lambda a, i, p: (k, a, i, 0))
    return pl.pallas_call(
        body, name="rs_chip_add_" + name,
        grid_spec=pltpu.PrefetchScalarGridSpec(
            num_scalar_prefetch=1, grid=(a_, nb), in_specs=[g_spec, s_spec, r_spec(0), r_spec(1), r_spec(2)],
            out_specs=pl.BlockSpec((None, rb, c_), lambda a, i, p: (a, p[0] * nb + i, 0))),
        out_shape=jax.ShapeDtypeStruct((a_, b_, c_), F32), compiler_params=_cparams(("parallel", "parallel")),
    )(place, g, got_pair, got_chips, got_chips, got_chips)


def rs_pair_gather_big(halves):
    names = tuple(halves)
    n = len(names)

    def body(*refs):
        outs, send_sems, recv_sems = refs[n:2 * n], refs[2 * n], refs[2 * n + 1]
        x, y, c, _ = _place()
        cps = []
        for a, name in enumerate(names):
            mine = outs[a].at[_shard_idx(name, c)]
            cp = _remote(mine, mine, send_sems.at[a], recv_sems.at[a], (x, y, 1 - c))
            cp.start()
            cps.append(cp)
        for a, name in enumerate(names):
            cps[a].wait_send()
            theirs = outs[a].at[_shard_idx(name, 1 - c)]
            _remote(theirs, theirs, send_sems.at[a], recv_sems.at[a], (x, y, 1 - c)).wait_recv()

    outs = pl.pallas_call(
        body, name="rs_pair_gather_big", in_specs=[ANY] * n, out_specs=[ANY] * n,
        input_output_aliases={a: a for a in range(n)},
        out_shape=[jax.ShapeDtypeStruct(BIG_SPECS[name][1:], F32) for name in names],
        scratch_shapes=[pltpu.SemaphoreType.DMA((n,)), pltpu.SemaphoreType.DMA((n,))],
    )(*[halves[name] for name in names])
    return dict(zip(names, outs))


def rs_begin(grads, names, tag, place):
    got_pair = rs_pair_exchange_big(grads, names, tag)
    pairs = {name: rs_pair_add_big(name, place, grads[name], got_pair[name]) for name in names}
    state, token = rs_chip_start(pairs, names, tag)
    return (grads, got_pair, state), token


def rs_end(begun, names, tag, place, after):
    grads, got_pair, state = begun
    got_chips = rs_chip_wait(state, names, tag, after)
    return {name: rs_chip_add_big(name, place, grads[name], got_pair[name], got_chips[name]) for name in names}


def rs_whole(grads, names, tag, place):
    got_pair = rs_pair_exchange_big(grads, names, tag)
    pairs = {name: rs_pair_add_big(name, place, grads[name], got_pair[name]) for name in names}
    got_chips = rs_chip_exchange_big(pairs, names, tag)
    return {name: rs_chip_add_big(name, place, grads[name], got_pair[name], got_chips[name]) for name in names}


PACK_W = 1024
SMALL =(("mem_norm_w", 1024), ("mem_k_norm_w", 128), ("norm1_w", 2048), ("dn_a_log", 8), ("dn_dt_bias", 8),
         ("dn_o_norm_w", 128), ("fox_f_bias", 8), ("fox_q_norm_w", 128), ("fox_k_norm_w", 128), ("memq_norm_w", 256),
         ("norm2_w", 2048))
SMALL_ROWS = 8
CONV_ROWS = 4 * 3 * D_MODEL // PACK_W
LOSS_AT = sum(n for _, n in SMALL)


def pack_small(parts, extra=None):
    flat = [parts[name].astype(F32).reshape(-1) for name, _ in SMALL]
    used = LOSS_AT
    if extra is not None:
        flat.append(extra.reshape(1))
        used += 1
    flat.append(jnp.zeros((SMALL_ROWS * PACK_W - used,), F32))
    return jnp.concatenate(flat).reshape(SMALL_ROWS, PACK_W)


def unpack_small(packed, shapes):
    flat, out, at = packed.reshape(-1), {}, 0
    for name, n in SMALL:
        out[name] = flat[at:at + n].reshape(shapes[name])
        at += n
    return out


def _adam_all(w, g, m, v, name):
    shape = w.shape
    r2 = lambda a: a.reshape(-1, shape[-1])
    d, nm, nv = adamw(r2(w), r2(g), r2(m), r2(v), name=name)
    return d.reshape(shape), nm.reshape(shape), nv.reshape(shape)


BIG = ("w_mem_kv", "dn_w_in", "dn_conv_w", "fox_w_in", "w_out", "w_mlp1", "w_mlp2")
WEIGHTS = ("mem_norm_w", "w_mem_kv", "mem_k_norm_w", "norm1_w", "dn_w_in", "dn_conv_w", "dn_a_log", "dn_dt_bias",
           "dn_o_norm_w", "fox_w_in", "fox_f_bias", "fox_q_norm_w", "fox_k_norm_w", "memq_norm_w", "w_out", "norm2_w",
           "w_mlp1", "w_mlp2")


def kernel(x, mem, mem_norm_w, w_mem_kv, mem_k_norm_w, norm1_w, dn_w_in, dn_conv_w, dn_a_log, dn_dt_bias, dn_o_norm_w, fox_w_in, fox_f_bias, fox_q_norm_w, fox_k_norm_w, memq_norm_w, w_out, norm2_w, w_mlp1, w_mlp2, loss_target, m_mem_norm_w, m_w_mem_kv, m_mem_k_norm_w, m_norm1_w, m_dn_w_in, m_dn_conv_w, m_dn_a_log, m_dn_dt_bias, m_dn_o_norm_w, m_fox_w_in, m_fox_f_bias, m_fox_q_norm_w, m_fox_k_norm_w, m_memq_norm_w, m_w_out, m_norm2_w, m_w_mlp1, m_w_mlp2, v_mem_norm_w, v_w_mem_kv, v_mem_k_norm_w, v_norm1_w, v_dn_w_in, v_dn_conv_w, v_dn_a_log, v_dn_dt_bias, v_dn_o_norm_w, v_fox_w_in, v_fox_f_bias, v_fox_q_norm_w, v_fox_k_norm_w, v_memq_norm_w, v_w_out, v_norm2_w, v_w_mlp1, v_w_mlp2):
    args = dict(locals())
    w = {n: args[n] for n in WEIGHTS}
    m = {n: args["m_" + n] for n in WEIGHTS}
    v = {n: args["v_" + n] for n in WEIGHTS}
    core, chip = lax.axis_index("c"), 2 * lax.axis_index("x") + lax.axis_index("y")
    place = jnp.stack([core, chip]).astype(jnp.int32)

    shards = {name: w[name].reshape(BIG_SPECS[name][1:]).astype(MM) for name in BIG_NAMES}
    w_in_full = lambda arr, n_scalars: w_in_to_kernel(arr[0].transpose(1, 0, 2).reshape(D_MODEL, -1), n_scalars)
    early = {name: with_own_slot(name, arr, shards[name], chip)
             for name, arr in all_gather_big(shards, EARLY_NAMES).items()}
    conv_mine = jnp.where(core == 0, dn_conv_w[0], 0.0)
    conv_placed = lax.dynamic_update_slice(jnp.zeros((4, 3 * D_MODEL), F32), conv_mine, (0, 768 * chip))
    conv_full = all_reduce_small(jnp.pad(conv_placed.reshape(CONV_ROWS, PACK_W), ((0, 16 - CONV_ROWS), (0, 0))))
    late_shards, early, conv_full = lax.optimization_barrier(
        ({name: shards[name] for name in LATE_NAMES}, early, conv_full))
    late_state, token = all_gather_start(late_shards, LATE_NAMES)
    tie = token[0, 0]
    wt = dict(w_mem_kv=early["w_mem_kv"].reshape(D_MODEL, 2 * MEM_WIDTH) + tie.astype(MM),
              dn_w_in=w_in_full(early["dn_w_in"], 2 * N_HEADS), conv_w=conv_full[:CONV_ROWS].reshape(4, 3 * D_MODEL))

    def late(after):
        late_shards, lands = all_gather_wait(late_state, LATE_NAMES, after)
        full = {name: with_own_slot(name, arr, late_shards[name], chip)
                for name, arr in all_gather_pass_on(lands, LATE_NAMES).items()}
        return dict(fox_w_in=w_in_full(full["fox_w_in"], N_HEADS), w_out=full["w_out"].reshape(2, 3 * MEM_WIDTH, D_MODEL),
                    w_mlp1=full["w_mlp1"], w_mlp2=full["w_mlp2"].reshape(2, D_FF, D_MODEL))

    sm = dict(mem_norm_w=mem_norm_w, mem_k_norm_w=mem_k_norm_w, norm1_w=norm1_w, norm2_w=norm2_w, memq_norm_w=memq_norm_w,
              dn_a_log=dn_a_log[0], dn_dt_bias=dn_dt_bias[0], dn_o_norm_w=dn_o_norm_w, fox_f_bias=fox_f_bias[0],
              fox_q_norm_w=fox_q_norm_w, fox_k_norm_w=fox_k_norm_w)
    w_in_slots = lambda g, n_scalars: w_in_from_kernel(g, n_scalars).reshape(D_MODEL, N_CHIP, -1).transpose(1, 0, 2)[None]
    rows_view = lambda g, name: g.reshape(_full_shape(name))
    begun = {}

    def on_layer1(big1):
        grads1 = dict(fox_w_in=w_in_slots(big1["w_in"], N_HEADS), w_out_1=rows_view(big1["w_out"], "w_out_1"),
                      w_mlp2_1=rows_view(big1["w_mlp2"], "w_mlp2_1"), w_mlp1_1=big1["w_mlp1"][None])
        begun["layer1"], token = rs_begin(grads1, RS_LAYER1, "layer1", place)
        return token[0, 0]

    def on_mlp0(d_w_mlp2, d_w_mlp1, d_w_out):
        grads0 = dict(w_mlp2_0=rows_view(d_w_mlp2, "w_mlp2_0"), w_mlp1_0=d_w_mlp1[None],
                      w_out_0=rows_view(d_w_out, "w_out_0"))
        begun["mlp0"], token = rs_begin(grads0, RS_MLP0, "mlp0", place)
        return token[0, 0]

    loss_part, dx, big, small = local_step(x[0], mem[0], loss_target[0], wt, sm, late, on_layer1, on_mlp0)
    last = dict(dn_w_in=w_in_slots(big["dn_w_in"], 2 * N_HEADS), w_mem_kv=rows_view(big["w_mem_kv"], "w_mem_kv"))
    halves = rs_whole(last, RS_LAST, "last", place)
    halves.update(rs_end(begun["layer1"], RS_LAYER1, "layer1", place, dx))
    halves.update(rs_end(begun["mlp0"], RS_MLP0, "mlp0", place, dx))
    summed = rs_pair_gather_big(halves)
    big_sum = {name: summed[name] for name in ("w_mem_kv", "dn_w_in", "fox_w_in")}
    big_sum.update({name: jnp.concatenate([summed[name + "_0"], summed[name + "_1"]]) for name in ("w_out", "w_mlp2", "w_mlp1")})
    small_pack = jnp.concatenate([pack_small(small, loss_part[0, :1]), big["conv_w"].reshape(CONV_ROWS, PACK_W),
                                  jnp.zeros((24 - SMALL_ROWS - CONV_ROWS, PACK_W), F32)])
    small_all = all_reduce_small(small_pack)
    small_sum = small_all[:SMALL_ROWS]
    conv_sum = lax.dynamic_slice(small_all[SMALL_ROWS:SMALL_ROWS + CONV_ROWS].reshape(4, 3 * D_MODEL), (0, 768 * chip), (4, 768))
    loss = small_sum.reshape(-1)[LOSS_AT]
    grads = unpack_small(small_sum, {n: w[n].shape for n, _ in SMALL})
    grads.update({name: big_sum[name].reshape(w[name].shape) for name in BIG_NAMES}, dn_conv_w=conv_sum[None])

    delta, new_m, new_v = {}, {}, {}
    for n in BIG:
        delta[n], new_m[n], new_v[n] = _adam_all(w[n], grads[n], m[n], v[n], "adamw_" + n)
    shapes = {n: w[n].shape for n, _ in SMALL}
    d_s, m_s, v_s = adamw(pack_small(w), small_sum, pack_small(m), pack_small(v), name="adamw_small")
    for out, packed in ((delta, d_s), (new_m, m_s), (new_v, v_s)):
        out.update(unpack_small(packed, shapes))
    return (loss, dx[None], *[grads[n] for n in WEIGHTS], *[delta[n] for n in WEIGHTS],
            *[new_m[n] for n in WEIGHTS], *[new_v[n] for n in WEIGHTS])
```

```python
import functools

import jax
import jax.numpy as jnp
from jax import lax
from jax.experimental import pallas as pl
from jax.experimental.pallas import tpu as pltpu

F32 = jnp.float32
MM = jnp.bfloat16
HI = lax.Precision.HIGHEST

D_MODEL = 1024
HEAD_DIM = 128
N_HEADS = 8
MEM_HEADS = 4
MEM_WIDTH = MEM_HEADS * HEAD_DIM
N_MEM = 256
D_FF = 4 * D_MODEL
CHUNK = 64
EPS = 1e-6
QSCALE = HEAD_DIM ** -0.5
PROJ_W = 4736
TAIL = 4608
TAIL_BLK = TAIL // HEAD_DIM
ROWS = 256
VMEM_LIMIT = 56 * 1024 * 1024

ADAM_LR = 0.001
ADAM_B1 = 0.9
ADAM_B2 = 0.999
ADAM_EPS = 1e-08
ADAM_WD = 0.01
ADAM_STEP = 10

N_DEV = 8
N_CHIP = 4
MESH = pl.DeviceIdType.MESH


def _cparams(sem=None):
    return pltpu.CompilerParams(dimension_semantics=sem, vmem_limit_bytes=VMEM_LIMIT)


def _dot(a, b, ca, cb, hi):
    dims = (((ca,), (cb,)), ((), ()))
    if hi:
        return lax.dot_general(a, b, dims, precision=HI, preferred_element_type=F32)
    return lax.dot_general(a.astype(MM), b.astype(MM), dims, preferred_element_type=F32)


@functools.partial(jax.custom_vjp, nondiff_argnums=(2, 3, 4))
def mmul(a, b, ca, cb, hi):
    return _dot(a, b, ca, cb, hi)


def _mmul_fwd(a, b, ca, cb, hi):
    return _dot(a, b, ca, cb, hi), (a, b)


def _mmul_bwd(ca, cb, hi, res, g):
    a, b = res
    if ca == 1:
        da = _dot(g, b, 1, 1, hi) if cb == 0 else _dot(g, b, 1, 0, hi)
    else:
        da = _dot(b, g, 1, 1, hi) if cb == 0 else _dot(b, g, 0, 1, hi)
    if cb == 0:
        db = _dot(a, g, 0, 0, hi) if ca == 1 else _dot(a, g, 1, 0, hi)
    else:
        db = _dot(g, a, 0, 0, hi) if ca == 1 else _dot(g, a, 0, 1, hi)
    return da.astype(a.dtype), db.astype(b.dtype)


mmul.defvjp(_mmul_fwd, _mmul_bwd)


def _iota2(n, m):
    return lax.broadcasted_iota(jnp.int32, (n, m), 0), lax.broadcasted_iota(jnp.int32, (n, m), 1)


def _same_block(r, c, shift):
    return lax.shift_right_logical(r, shift) == lax.shift_right_logical(c, shift)


def _split_bf16(x):
    hi = x.astype(jnp.bfloat16)
    return hi, (x - hi.astype(F32)).astype(jnp.bfloat16)


def _dot3(a, b, ca, cb):
    dims = (((ca,), (cb,)), ((), ()))
    (ah, al), (bh, bl) = _split_bf16(a), _split_bf16(b)
    d = lambda x, y: lax.dot_general(x, y, dims, preferred_element_type=F32)
    return d(ah, bh) + (d(ah, bl) + d(al, bh))


def _tri_inv_impl(a):
    n = a.shape[0]
    r, c = _iota2(n, n)
    eye = (r == c).astype(F32)
    b16, b32 = _same_block(r, c, 4), _same_block(r, c, 5)
    a0 = jnp.where(b16, a, 0.0)
    p = eye - a0
    b = _dot3(a0, a0, 1, 0)
    p = p + _dot3(p, b, 1, 0)
    b = _dot3(b, b, 1, 0)
    p = p + _dot3(p, b, 1, 0)
    b = _dot3(b, b, 1, 0)
    p = p + _dot3(p, b, 1, 0)
    a1 = jnp.where(jnp.logical_and(b32, jnp.logical_not(b16)), a, 0.0)
    p = p - _dot3(_dot3(p, a1, 1, 0), p, 1, 0)
    a2 = jnp.where(b32, 0.0, a)
    p = p - _dot3(_dot3(p, a2, 1, 0), p, 1, 0)
    return p


@jax.custom_vjp
def tri_inv(a):
    return _tri_inv_impl(a)


def _tri_inv_fwd(a):
    p = _tri_inv_impl(a)
    return p, p


def _tri_inv_bwd(p, g):
    return (-_dot3(_dot3(p, g, 0, 0), p, 1, 1),)


tri_inv.defvjp(_tri_inv_fwd, _tri_inv_bwd)


def _sigmoid(x):
    return 1.0 / (1.0 + jnp.exp(-x))


def _softplus(x):
    return jnp.maximum(x, 0.0) + jnp.log(1.0 + jnp.exp(-jnp.abs(x)))


def _silu(x):
    return x * _sigmoid(x)


def _rms(x, w):
    return x * lax.rsqrt(jnp.mean(x * x, axis=-1, keepdims=True) + EPS) * w


def _bf_round(x):
    return x.astype(MM).astype(F32)


def _acc(ref, val, first):
    @pl.when(first)
    def _():
        ref[...] = val

    @pl.when(jnp.logical_not(first))
    def _():
        ref[...] += val


def _tile(n, pref):
    if n % pref == 0:
        return pref
    return n


def matmul(a, b, *, ta=False, tb=False, b_slots=False, res=None, also_sqrelu=False, times_dsqrelu=None, out_dtype=F32,
           name, tm=1024, tn=1024, tk=1024):
    m, k = (a.shape[1], a.shape[0]) if ta else a.shape
    if b_slots:
        n = b.shape[1] if tb else N_CHIP * b.shape[2]
        assert (N_CHIP * b.shape[2] if tb else b.shape[1]) == k, (a.shape, b.shape, ta, tb)
        tn, tk = (tn, b.shape[2]) if tb else (b.shape[2], tk)
    else:
        n = b.shape[0] if tb else b.shape[1]
        assert (b.shape[1] if tb else b.shape[0]) == k, (a.shape, b.shape, ta, tb)
    tm, tn, tk = _tile(m, tm), _tile(n, tn), _tile(k, tk)
    nk = k // tk
    ca, cb = (0 if ta else 1), (1 if tb else 0)

    extra = tuple(e for e in (res, times_dsqrelu) if e is not None)
    assert len(extra) <= 1

    def body(a_ref, b_ref, *rest):
        e_ref = rest[0] if extra else None
        o_ref = rest[len(extra)]

        def finish(total):
            if res is not None:
                total = total + e_ref[...]
            if times_dsqrelu is not None:
                total = total * (2.0 * jnp.maximum(e_ref[...], 0.0))
            o_ref[...] = total.astype(o_ref.dtype)
            if also_sqrelu:
                rest[len(extra) + 1][...] = _sqrelu(total).astype(MM)

        if nk == 1:
            finish(_dot(a_ref[...], b_ref[...], ca, cb, False))
            return
        acc_ref, kk = rest[-1], pl.program_id(2)

        @pl.when(kk == 0)
        def _():
            acc_ref[...] = jnp.zeros_like(acc_ref)

        acc_ref[...] += _dot(a_ref[...], b_ref[...], ca, cb, False)

        @pl.when(kk == nk - 1)
        def _():
            finish(acc_ref[...])

    a_spec = pl.BlockSpec((tk, tm), lambda i, j, l: (l, i)) if ta else pl.BlockSpec((tm, tk), lambda i, j, l: (i, l))
    if b_slots:
        b_spec = (pl.BlockSpec((None, tn, tk), lambda i, j, l: (l, j, 0)) if tb else
                  pl.BlockSpec((None, tk, tn), lambda i, j, l: (j, l, 0)))
    else:
        b_spec = pl.BlockSpec((tn, tk), lambda i, j, l: (j, l)) if tb else pl.BlockSpec((tk, tn), lambda i, j, l: (l, j))
    o_spec = pl.BlockSpec((tm, tn), lambda i, j, l: (i, j))
    out_shape = [jax.ShapeDtypeStruct((m, n), out_dtype)] + [jax.ShapeDtypeStruct((m, n), MM)] * also_sqrelu
    outs = pl.pallas_call(
        body, name=name, grid=(m // tm, n // tn, nk),
        in_specs=[a_spec, b_spec] + [o_spec] * len(extra), out_specs=[o_spec] * len(out_shape), out_shape=out_shape,
        scratch_shapes=[pltpu.VMEM((tm, tn), F32)] * (nk > 1),
        compiler_params=_cparams(("parallel", "parallel", "arbitrary")),
    )(a, b, *extra)
    return outs if also_sqrelu else outs[0]


def rms_fwd(x, w, *, name):
    t, d = x.shape

    def body(x_ref, w_ref, o_ref):
        o_ref[...] = _rms(x_ref[...], w_ref[...]).astype(o_ref.dtype)

    return pl.pallas_call(
        body, name=name, grid=(t // ROWS,),
        in_specs=[pl.BlockSpec((ROWS, d), lambda i: (i, 0)), pl.BlockSpec((1, d), lambda i: (0, 0))],
        out_specs=pl.BlockSpec((ROWS, d), lambda i: (i, 0)),
        out_shape=jax.ShapeDtypeStruct((t, d), MM), compiler_params=_cparams(("parallel",)),
    )(x, w)


def rms_bwd(x, w, dh, dres, *, name):
    t, d = x.shape

    def body(x_ref, w_ref, dh_ref, dr_ref, dx_ref, dw_ref):
        _, vjp = jax.vjp(_rms, x_ref[...], w_ref[...])
        dx, dw = vjp(dh_ref[...].astype(F32))
        dx_ref[...] = dx + dr_ref[...]
        _acc(dw_ref, dw, pl.program_id(0) == 0)

    row = pl.BlockSpec((ROWS, d), lambda i: (i, 0))
    vec = pl.BlockSpec((1, d), lambda i: (0, 0))
    return pl.pallas_call(
        body, name=name, grid=(t // ROWS,), in_specs=[row, vec, row, row], out_specs=[row, vec],
        out_shape=[jax.ShapeDtypeStruct((t, d), F32), jax.ShapeDtypeStruct((1, d), F32)],
        compiler_params=_cparams(("arbitrary",)),
    )(x, w, dh, dres)


def _sqrelu(x):
    return jnp.square(jnp.maximum(x, 0.0))


def loss_fwd(y, target, *, name):
    t, d = y.shape

    def body(y_ref, t_ref, dy_ref, l_ref):
        e = y_ref[...] - t_ref[...]
        dy_ref[...] = e * (1.0 / d)
        part = 0.5 * jnp.sum(jnp.sum(e * e, axis=-1, keepdims=True) * (1.0 / d), axis=0, keepdims=True)
        _acc(l_ref, jnp.broadcast_to(part, (1, HEAD_DIM)), pl.program_id(0) == 0)

    blk = pl.BlockSpec((ROWS, d), lambda i: (i, 0))
    return pl.pallas_call(
        body, name=name, grid=(t // ROWS,), in_specs=[blk, blk],
        out_specs=[blk, pl.BlockSpec((1, HEAD_DIM), lambda i: (0, 0))],
        out_shape=[jax.ShapeDtypeStruct((t, d), F32), jax.ShapeDtypeStruct((1, HEAD_DIM), F32)],
        compiler_params=_cparams(("arbitrary",)),
    )(y, target)


def _mem_kv(mem, wn, wkn, *ws):
    mn = _rms(mem, wn)
    outs = []
    for h in range(MEM_HEADS):
        outs.append(_rms(mmul(mn, ws[h], 1, 0, False), wkn))
    for h in range(MEM_HEADS):
        outs.append(mmul(mn, ws[MEM_HEADS + h], 1, 0, False))
    return tuple(outs)


def _w_cols(w_ref):
    return [w_ref[:, h * HEAD_DIM:(h + 1) * HEAD_DIM] for h in range(2 * MEM_HEADS)]


def mem_fwd(mem, wn, wkv, wkn):
    def body(mem_ref, wn_ref, w_ref, wkn_ref, k_ref, v_ref):
        outs = _mem_kv(mem_ref[...], wn_ref[...], wkn_ref[...], *_w_cols(w_ref))
        for h in range(MEM_HEADS):
            k_ref[:, h * HEAD_DIM:(h + 1) * HEAD_DIM] = outs[h]
            v_ref[:, h * HEAD_DIM:(h + 1) * HEAD_DIM] = outs[MEM_HEADS + h]

    shp = jax.ShapeDtypeStruct((mem.shape[0], MEM_WIDTH), F32)
    return pl.pallas_call(body, name="mem_fwd", out_shape=[shp, shp], compiler_params=_cparams())(mem, wn, wkv, wkn)


def mem_bwd(mem, wn, wkv, wkn, dk0, dv0, dk1, dv1):
    def body(mem_ref, wn_ref, w_ref, wkn_ref, dk0_ref, dv0_ref, dk1_ref, dv1_ref, dwn_ref, dw_ref, dwkn_ref):
        _, vjp = jax.vjp(lambda wn_, wkn_, *ws: _mem_kv(mem_ref[...], wn_, wkn_, *ws),
                         wn_ref[...], wkn_ref[...], *[w.astype(F32) for w in _w_cols(w_ref)])
        cols = lambda a, b: tuple(a[:, h * HEAD_DIM:(h + 1) * HEAD_DIM] + b[:, h * HEAD_DIM:(h + 1) * HEAD_DIM]
                                  for h in range(MEM_HEADS))
        cts = cols(dk0_ref, dk1_ref) + cols(dv0_ref, dv1_ref)
        grads = vjp(cts)
        dwn_ref[...] = grads[0]
        dwkn_ref[...] = grads[1]
        for h in range(2 * MEM_HEADS):
            dw_ref[:, h * HEAD_DIM:(h + 1) * HEAD_DIM] = grads[2 + h]

    return pl.pallas_call(
        body, name="mem_bwd",
        out_shape=[jax.ShapeDtypeStruct((1, D_MODEL), F32), jax.ShapeDtypeStruct((D_MODEL, 2 * MEM_WIDTH), F32),
                   jax.ShapeDtypeStruct((1, HEAD_DIM), F32)],
        compiler_params=_cparams(),
    )(mem, wn, wkv, wkn, dk0, dv0, dk1, dv1)


def _memattn(q, wq, mk, mv):
    qn = _rms(q, wq) * QSCALE
    s = mmul(qn, mk, 1, 1, False)
    s = s - jnp.max(s, axis=-1, keepdims=True)
    p = jnp.exp(s)
    p = p / jnp.sum(p, axis=-1, keepdims=True)
    return mmul(p, mv, 1, 0, False)


def _lanes(j):
    return slice(j * HEAD_DIM, (j + 1) * HEAD_DIM)


def _memattn_specs(t):
    qspec = pl.BlockSpec((ROWS, MEM_WIDTH), lambda i: (i, (TAIL - MEM_WIDTH) // MEM_WIDTH))
    wspec = pl.BlockSpec((1, HEAD_DIM), lambda i: (0, 0))
    mspec = pl.BlockSpec((N_MEM, MEM_WIDTH), lambda i: (0, 0))
    ospec = pl.BlockSpec((ROWS, MEM_WIDTH), lambda i: (i, 0))
    return qspec, wspec, mspec, ospec


def memattn_fwd(proj, wq, mk, mv, *, name):
    t = proj.shape[0]
    qspec, wspec, mspec, ospec = _memattn_specs(t)

    def body(q_ref, w_ref, k_ref, v_ref, o_ref):
        for h in range(MEM_HEADS):
            o_ref[:, _lanes(h)] = _memattn(q_ref[:, _lanes(h)], w_ref[...], k_ref[:, _lanes(h)],
                                           v_ref[:, _lanes(h)]).astype(o_ref.dtype)

    return pl.pallas_call(
        body, name=name, grid=(t // ROWS,), in_specs=[qspec, wspec, mspec, mspec], out_specs=ospec,
        out_shape=jax.ShapeDtypeStruct((t, MEM_WIDTH), MM), compiler_params=_cparams(("parallel",)),
    )(proj, wq, mk, mv)


def memattn_bwd(proj, wq, mk, mv, dcat, *, name):
    t = proj.shape[0]
    qspec, wspec, mspec, ospec = _memattn_specs(t)
    dospec = pl.BlockSpec((ROWS, MEM_WIDTH), lambda i: (i, D_MODEL // MEM_WIDTH))

    def body(q_ref, w_ref, k_ref, v_ref, do_ref, dq_ref, dw_ref, dk_ref, dv_ref):
        first = pl.program_id(0) == 0
        dw_sum = jnp.zeros((1, HEAD_DIM), F32)
        for h in range(MEM_HEADS):
            _, vjp = jax.vjp(_memattn, q_ref[:, _lanes(h)], w_ref[...], k_ref[:, _lanes(h)], v_ref[:, _lanes(h)])
            dq, dw, dk, dv = vjp(do_ref[:, _lanes(h)].astype(F32))
            dq_ref[:, _lanes(h)] = dq.astype(dq_ref.dtype)
            dw_sum = dw_sum + dw
            _acc(dk_ref.at[:, _lanes(h)], dk, first)
            _acc(dv_ref.at[:, _lanes(h)], dv, first)
        _acc(dw_ref, dw_sum, first)

    mshape = jax.ShapeDtypeStruct((N_MEM, MEM_WIDTH), F32)
    return pl.pallas_call(
        body, name=name, grid=(t // ROWS,), in_specs=[qspec, wspec, mspec, mspec, dospec],
        out_specs=[ospec, wspec, mspec, mspec],
        out_shape=[jax.ShapeDtypeStruct((t, MEM_WIDTH), MM), jax.ShapeDtypeStruct((1, HEAD_DIM), F32), mshape, mshape],
        compiler_params=_cparams(("arbitrary",)),
    )(proj, wq, mk, mv, dcat)


def _shift_rows(x, s, up):
    n = x.shape[0]
    r = lax.broadcasted_iota(jnp.int32, x.shape, 0)
    if up:
        return jnp.where(r < n - s, pltpu.roll(x, n - s, 0), 0.0)
    return jnp.where(r >= s, pltpu.roll(x, s, 0), 0.0)


def _conv_fwd_vals(x, w):
    xb = _bf_round(x)
    wb = _bf_round(w)
    c = xb * wb[3:4, :]
    for j in range(3):
        c = c + _shift_rows(xb, 3 - j, False) * wb[j:j + 1, :]
    return xb, wb, c


def dn_prep_fwd(proj, conv_w):
    t = proj.shape[0]

    def body(x_ref, w_ref, o_ref):
        j = pl.program_id(0)
        _, _, c = _conv_fwd_vals(x_ref[...], w_ref[...])
        s = _silu(c)
        r = lax.rsqrt(jnp.sum(s * s, axis=-1, keepdims=True) + EPS)
        scale = jnp.where(j < N_HEADS, QSCALE, 1.0)
        o_ref[...] = jnp.where(j < 2 * N_HEADS, s * r * scale, s)

    return pl.pallas_call(
        body, name="dn_prep_fwd", grid=(3 * N_HEADS,),
        in_specs=[pl.BlockSpec((t, HEAD_DIM), lambda j: (0, j)), pl.BlockSpec((4, HEAD_DIM), lambda j: (0, j))],
        out_specs=pl.BlockSpec((None, t, HEAD_DIM), lambda j: (j // N_HEADS, 0, j % N_HEADS)),
        out_shape=jax.ShapeDtypeStruct((3, t, D_MODEL), F32), compiler_params=_cparams(("parallel",)),
    )(proj, conv_w)


def dn_prep_bwd(proj, conv_w, dqkv):
    t = proj.shape[0]

    def body(x_ref, w_ref, g_ref, dx_ref, dw_ref):
        j = pl.program_id(0)
        xb, wb, c = _conv_fwd_vals(x_ref[...], w_ref[...])
        sg = _sigmoid(c)
        s = c * sg
        g = g_ref[...]
        r = lax.rsqrt(jnp.sum(s * s, axis=-1, keepdims=True) + EPS)
        scale = jnp.where(j < N_HEADS, QSCALE, 1.0)
        gn = g * scale
        ds_norm = r * gn - s * (r * r * r) * jnp.sum(gn * s, axis=-1, keepdims=True)
        ds = jnp.where(j < 2 * N_HEADS, ds_norm, g)
        dc = ds * (sg + s * (1.0 - sg))
        dx = dc * wb[3:4, :]
        rows = [jnp.sum(dc * xb, axis=0, keepdims=True)]
        for jj in range(2, -1, -1):
            sh = 3 - jj
            dx = dx + _shift_rows(dc, sh, True) * wb[jj:jj + 1, :]
            rows.insert(0, jnp.sum(dc * _shift_rows(xb, sh, False), axis=0, keepdims=True))
        dx_ref[...] = dx.astype(dx_ref.dtype)
        dw_ref[...] = jnp.concatenate(rows + [jnp.zeros((4, HEAD_DIM), F32)], axis=0)

    col = pl.BlockSpec((t, HEAD_DIM), lambda j: (0, j))
    return pl.pallas_call(
        body, name="dn_prep_bwd", grid=(3 * N_HEADS,),
        in_specs=[col, pl.BlockSpec((4, HEAD_DIM), lambda j: (0, j)),
                  pl.BlockSpec((None, t, HEAD_DIM), lambda j: (j // N_HEADS, 0, j % N_HEADS))],
        out_specs=[col, pl.BlockSpec((8, HEAD_DIM), lambda j: (0, j))],
        out_shape=[jax.ShapeDtypeStruct((t, 3 * D_MODEL), MM), jax.ShapeDtypeStruct((8, 3 * D_MODEL), F32)],
        compiler_params=_cparams(("parallel",)),
    )(proj, conv_w, dqkv)


def _tri_ones(n, upper):
    r, c = _iota2(n, n)
    return (r <= c).astype(F32) if upper else (r >= c).astype(F32)


def dn_gates_fwd(proj, a_log, dt_bias):
    t = proj.shape[0]

    def body(x_ref, al_ref, dt_ref, o_ref):
        lane = lax.broadcasted_iota(jnp.int32, (CHUNK, HEAD_DIM), 1)
        tri = _tri_ones(CHUNK, False)

        def step(c, carry):
            rows = pl.ds(pl.multiple_of(c * CHUNK, CHUNK), CHUNK)
            x = x_ref[rows, :]
            g = jnp.where(lane < N_HEADS, -jnp.exp(al_ref[...]) * _softplus(x + dt_ref[...]), 0.0)
            gc = _dot(tri, g, 1, 0, True)
            o_ref[rows, :] = jnp.where(lane < N_HEADS, gc, jnp.where(lane < 2 * N_HEADS, _sigmoid(x), 0.0))
            return carry

        lax.fori_loop(0, t // CHUNK, step, 0)

    vec = pl.BlockSpec((1, HEAD_DIM), lambda i: (0, 0))
    return pl.pallas_call(
        body, name="dn_gates_fwd", grid=(1,),
        in_specs=[pl.BlockSpec((t, HEAD_DIM), lambda i: (0, TAIL_BLK)), vec, vec],
        out_specs=pl.BlockSpec((t, HEAD_DIM), lambda i: (0, 0)),
        out_shape=jax.ShapeDtypeStruct((t, HEAD_DIM), F32), compiler_params=_cparams(("arbitrary",)),
    )(proj, a_log, dt_bias)


def dn_gates_bwd(proj, a_log, dt_bias, dgates):
    t = proj.shape[0]

    def body(x_ref, al_ref, dt_ref, g_ref, dx_ref, dal_ref, ddt_ref):
        lane = lax.broadcasted_iota(jnp.int32, (CHUNK, HEAD_DIM), 1)
        tri = _tri_ones(CHUNK, True)
        dal_ref[...] = jnp.zeros_like(dal_ref)
        ddt_ref[...] = jnp.zeros_like(ddt_ref)

        def step(c, carry):
            rows = pl.ds(pl.multiple_of(c * CHUNK, CHUNK), CHUNK)
            x = x_ref[rows, :]
            dgc = jnp.where(lane < N_HEADS, g_ref[rows, :], 0.0)
            dg = _dot(tri, dgc, 1, 0, True)
            ea = -jnp.exp(al_ref[...])
            z = x + dt_ref[...]
            da = jnp.where(lane < N_HEADS, dg * ea * _sigmoid(z), 0.0)
            gval = jnp.where(lane < N_HEADS, ea * _softplus(z), 0.0)
            beta = _sigmoid(x)
            db = jnp.where(jnp.logical_and(lane >= N_HEADS, lane < 2 * N_HEADS), g_ref[rows, :] * beta * (1.0 - beta), 0.0)
            dx_ref[rows, :] = (da + db).astype(dx_ref.dtype)
            dal_ref[...] += jnp.sum(dg * gval, axis=0, keepdims=True)
            ddt_ref[...] += jnp.sum(da, axis=0, keepdims=True)
            return carry

        lax.fori_loop(0, t // CHUNK, step, 0)

    vec = pl.BlockSpec((1, HEAD_DIM), lambda i: (0, 0))
    full = pl.BlockSpec((t, HEAD_DIM), lambda i: (0, 0))
    return pl.pallas_call(
        body, name="dn_gates_bwd", grid=(1,),
        in_specs=[pl.BlockSpec((t, HEAD_DIM), lambda i: (0, TAIL_BLK)), vec, vec, full],
        out_specs=[full, vec, vec],
        out_shape=[jax.ShapeDtypeStruct((t, HEAD_DIM), MM), jax.ShapeDtypeStruct((1, HEAD_DIM), F32),
                   jax.ShapeDtypeStruct((1, HEAD_DIM), F32)],
        compiler_params=_cparams(("arbitrary",)),
    )(proj, a_log, dt_bias, dgates)


def _dn_intra(q, k, v, gcol, grow, bcol):
    r, c = _iota2(CHUNK, CHUNK)
    causal, strict = r >= c, r > c
    decay = jnp.where(causal, jnp.exp(jnp.where(causal, gcol - grow, 0.0)), 0.0)
    kb = k * bcol
    a = jnp.where(strict, mmul(kb, k, 1, 1, False) * decay, 0.0)
    tm = tri_inv(a)
    u = mmul(tm, v * bcol, 1, 0, False)
    w = mmul(tm, kb * jnp.exp(gcol), 1, 0, False)
    qk = jnp.where(causal, mmul(q, k, 1, 1, False) * decay, 0.0)
    rr = lax.broadcasted_iota(jnp.int32, (CHUNK, 1), 0)
    g_last = jnp.sum(jnp.where(rr == CHUNK - 1, gcol, 0.0), axis=0, keepdims=True)
    return u, w, q * jnp.exp(gcol), k * jnp.exp(g_last - gcol), qk, jnp.exp(g_last)


def _dn_scan(u, w, qg, kd, qk, eg, state):
    v_new = u - mmul(w, state, 1, 0, False)
    out = mmul(qg, state, 1, 0, False) + mmul(qk, v_new, 1, 0, False)
    return out, state * eg + mmul(kd, v_new, 0, 0, False)


DN_HEADS_PER_STEP = 1
DN_GROUP = 8
DN_PARTS = ((CHUNK, HEAD_DIM),) * 4 + ((CHUNK, CHUNK), (1, 1))


def _dn_scratch(hb, nc):
    return [pltpu.VMEM((hb, nc) + shape, F32) for shape in DN_PARTS]


def _dn_part_specs(hb, nc):
    return [pl.BlockSpec((hb, nc) + shape, lambda h: (h, 0, 0, 0)) for shape in DN_PARTS]


def _dn_group(nc):
    return min(DN_GROUP, nc)


def _dn_group_args(refs, j, g, grp):
    q_ref, k_ref, v_ref, gc_ref, gr_ref, bc_ref = refs
    rows = pl.ds(pl.multiple_of(g * (grp * CHUNK), grp * CHUNK), grp * CHUNK)
    cs = pl.ds(g * grp, grp)
    split = lambda ref: ref[rows, _lanes(j)].reshape(grp, CHUNK, HEAD_DIM)
    return split(q_ref), split(k_ref), split(v_ref), gc_ref[j, cs], gr_ref[j, cs], bc_ref[j, cs]


def _dn_intra_all(refs, parts, hb, nc):
    grp = _dn_group(nc)

    def group(g, carry):
        cs = pl.ds(g * grp, grp)
        for j in range(hb):
            for part, val in zip(parts, jax.vmap(_dn_intra)(*_dn_group_args(refs, j, g, grp))):
                part[j, cs] = val
        return carry

    lax.fori_loop(0, nc // grp, group, 0)


def _dn_specs(t):
    nc, hb = t // CHUNK, DN_HEADS_PER_STEP
    head = lambda which: pl.BlockSpec((None, t, hb * HEAD_DIM), lambda h: (which, 0, h))
    flat = pl.BlockSpec((t, hb * HEAD_DIM), lambda h: (0, h))
    col = pl.BlockSpec((hb, nc, CHUNK, 1), lambda h: (h, 0, 0, 0))
    row = pl.BlockSpec((hb, nc, 1, CHUNK), lambda h: (h, 0, 0, 0))
    st = pl.BlockSpec((hb, nc, HEAD_DIM, HEAD_DIM), lambda h: (h, 0, 0, 0))
    return nc, hb, head, flat, col, row, st


def dn_core_fwd(qkv, gcol, grow, bcol):
    t = qkv.shape[1]
    nc, hb, head, flat, col, row, st = _dn_specs(t)

    def body(q_ref, k_ref, v_ref, gc_ref, gr_ref, bc_ref, o_ref, s_ref, *parts):
        _dn_intra_all((q_ref, k_ref, v_ref, gc_ref, gr_ref, bc_ref), parts, hb, nc)

        def step(c, states):
            rows = pl.ds(pl.multiple_of(c * CHUNK, CHUNK), CHUNK)
            new_states = []
            for j in range(hb):
                s_ref[j, c] = states[j]
                out, new_state = _dn_scan(*[part[j, c] for part in parts], states[j])
                o_ref[rows, _lanes(j)] = out
                new_states.append(new_state)
            return tuple(new_states)

        lax.fori_loop(0, nc, step, tuple(jnp.zeros((HEAD_DIM, HEAD_DIM), F32) for _ in range(hb)))

    outs = pl.pallas_call(
        body, name="dn_core_fwd", grid=(N_HEADS // hb,),
        in_specs=[head(0), head(1), head(2), col, row, col], out_specs=[flat, st] + _dn_part_specs(hb, nc),
        out_shape=[jax.ShapeDtypeStruct((t, D_MODEL), F32), jax.ShapeDtypeStruct((N_HEADS, nc, HEAD_DIM, HEAD_DIM), F32)]
        + [jax.ShapeDtypeStruct((N_HEADS, nc) + shape, F32) for shape in DN_PARTS],
        compiler_params=_cparams(("parallel",)),
    )(qkv, qkv, qkv, gcol, grow, bcol)
    return outs[0], outs[1], tuple(outs[2:])


def dn_core_bwd(qkv, gcol, grow, bcol, states, parts, do):
    t = qkv.shape[1]
    nc, hb, head, flat, col, row, st = _dn_specs(t)
    n_parts = len(DN_PARTS)

    def body(q_ref, k_ref, v_ref, gc_ref, gr_ref, bc_ref, s_ref, do_ref, *rest):
        parts, (dqkv_ref, dgc_ref, dgr_ref, dbc_ref), dparts = rest[:n_parts], rest[n_parts:n_parts + 4], rest[n_parts + 4:]
        refs = (q_ref, k_ref, v_ref, gc_ref, gr_ref, bc_ref)

        def step(i, dstates):
            c = nc - 1 - i
            rows = pl.ds(pl.multiple_of(c * CHUNK, CHUNK), CHUNK)
            dstates_in = []
            for j in range(hb):
                _, vjp = jax.vjp(_dn_scan, *[part[j, c] for part in parts], s_ref[j, c])
                *dvals, dstate_in = vjp((do_ref[rows, _lanes(j)], dstates[j]))
                for dpart, dval in zip(dparts, dvals):
                    dpart[j, c] = dval
                dstates_in.append(dstate_in)
            return tuple(dstates_in)

        lax.fori_loop(0, nc, step, tuple(jnp.zeros((HEAD_DIM, HEAD_DIM), F32) for _ in range(hb)))

        grp = _dn_group(nc)

        def group(g, carry):
            rows = pl.ds(pl.multiple_of(g * (grp * CHUNK), grp * CHUNK), grp * CHUNK)
            cs = pl.ds(g * grp, grp)
            for j in range(hb):
                _, vjp = jax.vjp(jax.vmap(_dn_intra), *_dn_group_args(refs, j, g, grp))
                dq, dk, dv, dgc, dgr, dbc = vjp(tuple(dpart[j, cs] for dpart in dparts))
                for which, val in enumerate((dq, dk, dv)):
                    dqkv_ref[which, rows, _lanes(j)] = val.reshape(grp * CHUNK, HEAD_DIM)
                dgc_ref[j, cs] = dgc
                dgr_ref[j, cs] = dgr
                dbc_ref[j, cs] = dbc
            return carry

        lax.fori_loop(0, nc // grp, group, 0)

    return pl.pallas_call(
        body, name="dn_core_bwd", grid=(N_HEADS // hb,), scratch_shapes=_dn_scratch(hb, nc),
        in_specs=[head(0), head(1), head(2), col, row, col, st, flat] + _dn_part_specs(hb, nc),
        out_specs=[pl.BlockSpec((3, t, hb * HEAD_DIM), lambda h: (0, 0, h)), col, row, col],
        out_shape=[jax.ShapeDtypeStruct((3, t, D_MODEL), F32)] + [
            jax.ShapeDtypeStruct((N_HEADS, nc, CHUNK, 1), F32), jax.ShapeDtypeStruct((N_HEADS, nc, 1, CHUNK), F32),
            jax.ShapeDtypeStruct((N_HEADS, nc, CHUNK, 1), F32)],
        compiler_params=_cparams(("parallel",)),
    )(qkv, qkv, qkv, gcol, grow, bcol, states, do, *parts)


def gates_to_heads(gates):
    t = gates.shape[0]
    nc = t // CHUNK
    g = gates[:, :N_HEADS].T.reshape(N_HEADS, nc, CHUNK)
    b = gates[:, N_HEADS:2 * N_HEADS].T.reshape(N_HEADS, nc, CHUNK)
    return g[..., None], g[:, :, None, :], b[..., None]


def heads_to_gates(dgcol, dgrow, dbcol):
    nh, nc = dgcol.shape[:2]
    dg = (dgcol[..., 0] + dgrow[:, :, 0, :]).reshape(nh, nc * CHUNK).T
    db = dbcol[..., 0].reshape(nh, nc * CHUNK).T
    return jnp.concatenate([dg, db, jnp.zeros((nc * CHUNK, HEAD_DIM - 2 * nh), F32)], axis=1)


def _dn_out(o, z, w):
    return _rms(o, w) * _silu(z)


def _gate_specs():
    o_spec = pl.BlockSpec((ROWS, D_MODEL), lambda i: (i, 0))
    z_spec = pl.BlockSpec((ROWS, D_MODEL), lambda i: (i, 3))
    w_spec = pl.BlockSpec((1, HEAD_DIM), lambda i: (0, 0))
    return o_spec, z_spec, w_spec


def dn_out_fwd(o, proj, w):
    t = o.shape[0]
    o_spec, z_spec, w_spec = _gate_specs()

    def body(o_ref, z_ref, w_ref, y_ref):
        for h in range(N_HEADS):
            y_ref[:, _lanes(h)] = _dn_out(o_ref[:, _lanes(h)], z_ref[:, _lanes(h)], w_ref[...]).astype(y_ref.dtype)

    return pl.pallas_call(
        body, name="dn_out_fwd", grid=(t // ROWS,), in_specs=[o_spec, z_spec, w_spec], out_specs=o_spec,
        out_shape=jax.ShapeDtypeStruct((t, D_MODEL), MM), compiler_params=_cparams(("parallel",)),
    )(o, proj, w)


def dn_out_bwd(o, proj, w, dcat):
    t = o.shape[0]
    o_spec, z_spec, w_spec = _gate_specs()

    def body(o_ref, z_ref, w_ref, g_ref, do_ref, dz_ref, dw_ref):
        dw_sum = jnp.zeros((1, HEAD_DIM), F32)
        for h in range(N_HEADS):
            _, vjp = jax.vjp(_dn_out, o_ref[:, _lanes(h)], z_ref[:, _lanes(h)], w_ref[...])
            do, dz, dw = vjp(g_ref[:, _lanes(h)].astype(F32))
            do_ref[:, _lanes(h)] = do
            dz_ref[:, _lanes(h)] = dz.astype(dz_ref.dtype)
            dw_sum = dw_sum + dw
        _acc(dw_ref, dw_sum, pl.program_id(0) == 0)

    return pl.pallas_call(
        body, name="dn_out_bwd", grid=(t // ROWS,), in_specs=[o_spec, z_spec, w_spec, o_spec],
        out_specs=[o_spec, o_spec, w_spec],
        out_shape=[jax.ShapeDtypeStruct((t, D_MODEL), F32), jax.ShapeDtypeStruct((t, D_MODEL), MM),
                   jax.ShapeDtypeStruct((1, HEAD_DIM), F32)],
        compiler_params=_cparams(("arbitrary",)),
    )(o, proj, w, dcat)


def _fox_norm(x, w, scale):
    return _rms(x, w) * scale


def _fox_prep_specs():
    x_spec = pl.BlockSpec((ROWS, 2 * D_MODEL), lambda i: (i, 0))
    w_spec = pl.BlockSpec((2, 1, HEAD_DIM), lambda i: (0, 0, 0))
    y_spec = pl.BlockSpec((2, ROWS, D_MODEL), lambda i: (0, i, 0))
    return x_spec, w_spec, y_spec


def fox_prep_fwd(proj, wqk):
    t = proj.shape[0]
    x_spec, w_spec, y_spec = _fox_prep_specs()

    def body(x_ref, w_ref, y_ref):
        for j in range(2 * N_HEADS):
            which, scale = j // N_HEADS, (QSCALE if j < N_HEADS else 1.0)
            y_ref[which, :, _lanes(j % N_HEADS)] = _fox_norm(x_ref[:, _lanes(j)], w_ref[which], scale).astype(y_ref.dtype)

    return pl.pallas_call(
        body, name="fox_prep_fwd", grid=(t // ROWS,), in_specs=[x_spec, w_spec], out_specs=y_spec,
        out_shape=jax.ShapeDtypeStruct((2, t, D_MODEL), MM), compiler_params=_cparams(("parallel",)),
    )(proj, wqk)


def fox_prep_bwd(proj, wqk, dq, dk):
    t = proj.shape[0]
    x_spec, w_spec, _ = _fox_prep_specs()
    g_spec = pl.BlockSpec((ROWS, D_MODEL), lambda i: (i, 0))

    def body(x_ref, w_ref, dq_ref, dk_ref, dx_ref, dw_ref):
        dws = [jnp.zeros((1, HEAD_DIM), F32), jnp.zeros((1, HEAD_DIM), F32)]
        for j in range(2 * N_HEADS):
            which, scale = j // N_HEADS, (QSCALE if j < N_HEADS else 1.0)
            g_ref = dq_ref if which == 0 else dk_ref
            _, vjp = jax.vjp(lambda x, w: _fox_norm(x, w, scale), x_ref[:, _lanes(j)], w_ref[which])
            dx, dw = vjp(g_ref[:, _lanes(j % N_HEADS)])
            dx_ref[:, _lanes(j)] = dx.astype(dx_ref.dtype)
            dws[which] = dws[which] + dw
        first = pl.program_id(0) == 0
        _acc(dw_ref.at[0], dws[0], first)
        _acc(dw_ref.at[1], dws[1], first)

    return pl.pallas_call(
        body, name="fox_prep_bwd", grid=(t // ROWS,), in_specs=[x_spec, w_spec, g_spec, g_spec],
        out_specs=[x_spec, w_spec],
        out_shape=[jax.ShapeDtypeStruct((t, 2 * D_MODEL), MM), jax.ShapeDtypeStruct((2, 1, HEAD_DIM), F32)],
        compiler_params=_cparams(("arbitrary",)),
    )(proj, wqk, dq, dk)


def _row_pick(x, i):
    r = lax.broadcasted_iota(jnp.int32, x.shape, 0)
    return jnp.sum(jnp.where(r == i, x, 0.0), axis=0, keepdims=True)


def fox_gates_fwd(proj, f_bias):
    t = proj.shape[0]
    blk = HEAD_DIM

    def body(x_ref, b_ref, o_ref):
        lane = lax.broadcasted_iota(jnp.int32, (blk, HEAD_DIM), 1)
        tri = _tri_ones(blk, False)

        def step(c, carry):
            rows = pl.ds(pl.multiple_of(c * blk, blk), blk)
            lf = jnp.where(lane < N_HEADS, -_softplus(-(x_ref[rows, :] + b_ref[...])), 0.0)
            cum = _dot(tri, lf, 1, 0, True) + carry
            o_ref[rows, :] = cum
            return _row_pick(cum, blk - 1)

        lax.fori_loop(0, t // blk, step, jnp.zeros((1, HEAD_DIM), F32))

    vec = pl.BlockSpec((1, HEAD_DIM), lambda i: (0, 0))
    return pl.pallas_call(
        body, name="fox_gates_fwd", grid=(1,),
        in_specs=[pl.BlockSpec((t, HEAD_DIM), lambda i: (0, TAIL_BLK)), vec],
        out_specs=pl.BlockSpec((t, HEAD_DIM), lambda i: (0, 0)),
        out_shape=jax.ShapeDtypeStruct((t, HEAD_DIM), F32), compiler_params=_cparams(("arbitrary",)),
    )(proj, f_bias)


def fox_gates_bwd(proj, f_bias, dfcum):
    t = proj.shape[0]
    blk = HEAD_DIM
    nb = t // blk

    def body(x_ref, b_ref, g_ref, dx_ref, db_ref):
        lane = lax.broadcasted_iota(jnp.int32, (blk, HEAD_DIM), 1)
        tri = _tri_ones(blk, True)
        db_ref[...] = jnp.zeros_like(db_ref)

        def step(i, carry):
            c = nb - 1 - i
            rows = pl.ds(pl.multiple_of(c * blk, blk), blk)
            g = jnp.where(lane < N_HEADS, g_ref[rows, :], 0.0)
            dlf = _dot(tri, g, 1, 0, True) + carry
            dx = jnp.where(lane < N_HEADS, dlf * _sigmoid(-(x_ref[rows, :] + b_ref[...])), 0.0)
            dx_ref[rows, :] = dx.astype(dx_ref.dtype)
            db_ref[...] += jnp.sum(dx, axis=0, keepdims=True)
            return carry + jnp.sum(g, axis=0, keepdims=True)

        lax.fori_loop(0, nb, step, jnp.zeros((1, HEAD_DIM), F32))

    vec = pl.BlockSpec((1, HEAD_DIM), lambda i: (0, 0))
    full = pl.BlockSpec((t, HEAD_DIM), lambda i: (0, 0))
    return pl.pallas_call(
        body, name="fox_gates_bwd", grid=(1,),
        in_specs=[pl.BlockSpec((t, HEAD_DIM), lambda i: (0, TAIL_BLK)), vec, full], out_specs=[full, vec],
        out_shape=[jax.ShapeDtypeStruct((t, HEAD_DIM), MM), jax.ShapeDtypeStruct((1, HEAD_DIM), F32)],
        compiler_params=_cparams(("arbitrary",)),
    )(proj, f_bias, dfcum)


def fcum_to_heads(fcum):
    f = fcum[:, :N_HEADS].T
    return f[:, :, None], f[:, None, :]


def heads_to_fcum(dfcol, dfrow):
    d = (dfcol[:, :, 0] + dfrow[:, 0, :]).T
    return jnp.concatenate([d, jnp.zeros((d.shape[0], HEAD_DIM - N_HEADS), F32)], axis=1)


def _fox_tq(t):
    return min(t, 256)


def _fox_specs(t):
    tq = _fox_tq(t)
    q_spec = pl.BlockSpec((None, tq, HEAD_DIM), lambda h, i: (0, i, h))
    k_spec = pl.BlockSpec((None, t, HEAD_DIM), lambda h, i: (1, 0, h))
    v_spec = pl.BlockSpec((t, HEAD_DIM), lambda h, i: (0, 2 * N_HEADS + h))
    gate_spec = pl.BlockSpec((tq, HEAD_DIM), lambda h, i: (i, 3 * N_HEADS + h))
    col_spec = pl.BlockSpec((None, tq, 1), lambda h, i: (h, i, 0))
    row_spec = pl.BlockSpec((None, 1, t), lambda h, i: (h, 0, 0))
    blk_spec = pl.BlockSpec((tq, HEAD_DIM), lambda h, i: (i, h))
    head_spec = pl.BlockSpec((t, HEAD_DIM), lambda h, i: (0, h))
    return tq, q_spec, k_spec, v_spec, gate_spec, col_spec, row_spec, blk_spec, head_spec


def _fox_segments(i, tq):
    return ([(0, i * tq, False)] if i else []) + [(i * tq, (i + 1) * tq, True)]


def _fox_scores(q_ref, k_ref, fc_ref, fr_ref, lo, hi, causal):
    s = _dot(q_ref[...], k_ref[lo:hi, :], 1, 1, False) + (fc_ref[...] - fr_ref[:, lo:hi])
    if not causal:
        return s, None
    r, c = _iota2(hi - lo, hi - lo)
    return s, c <= r


def fox_attn_fwd(qk, proj, fcol, frow):
    t = proj.shape[0]
    tq, q_spec, k_spec, v_spec, gate_spec, col_spec, row_spec, blk_spec, _ = _fox_specs(t)

    def body(q_ref, k_ref, v_ref, gate_ref, fc_ref, fr_ref, mix_ref, o_ref, lse_ref):
        def block(i):
            segs = _fox_segments(i, tq)
            scores = [_fox_scores(q_ref, k_ref, fc_ref, fr_ref, *seg) for seg in segs]
            scores = [(s if mask is None else jnp.where(mask, s, -1e30), mask) for s, mask in scores]
            m = functools.reduce(jnp.maximum, [jnp.max(s, axis=-1, keepdims=True) for s, _ in scores])
            l, o = 0.0, 0.0
            for (lo, hi, _), (s, mask) in zip(segs, scores):
                p = jnp.exp(s - m)
                p = p if mask is None else jnp.where(mask, p, 0.0)
                l = l + jnp.sum(p, axis=-1, keepdims=True)
                o = o + _dot(p, v_ref[lo:hi, :], 1, 0, False)
            o = o / l
            o_ref[...] = o
            mix_ref[...] = (o * _sigmoid(gate_ref[...])).astype(mix_ref.dtype)
            lse_ref[...] = m + jnp.log(l)

        for i in range(t // tq):
            pl.when(pl.program_id(1) == i)(functools.partial(block, i))

    return pl.pallas_call(
        body, name="fox_attn_fwd", grid=(N_HEADS, t // tq),
        in_specs=[q_spec, k_spec, v_spec, gate_spec, col_spec, row_spec], out_specs=[blk_spec, blk_spec, col_spec],
        out_shape=[jax.ShapeDtypeStruct((t, D_MODEL), MM), jax.ShapeDtypeStruct((t, D_MODEL), F32),
                   jax.ShapeDtypeStruct((N_HEADS, t, 1), F32)],
        compiler_params=_cparams(("parallel", "parallel")),
    )(qk, qk, proj, proj, fcol, frow)


def fox_attn_bwd(qk, proj, fcol, frow, o, lse, dcat):
    t = proj.shape[0]
    tq, q_spec, k_spec, v_spec, gate_spec, col_spec, row_spec, blk_spec, head_spec = _fox_specs(t)

    def body(q_ref, k_ref, v_ref, gate_ref, fc_ref, fr_ref, o_ref, lse_ref, g_ref,
             dq_ref, dk_ref, dv_ref, dgate_ref, dfc_ref, dfr_ref):
        @pl.when(pl.program_id(1) == 0)
        def _():
            dk_ref[...] = jnp.zeros_like(dk_ref)
            dv_ref[...] = jnp.zeros_like(dv_ref)
            dfr_ref[...] = jnp.zeros_like(dfr_ref)

        def block(i):
            sg = _sigmoid(gate_ref[...])
            g = g_ref[...].astype(F32)
            o_pre = o_ref[...]
            do = g * sg
            dgate_ref[...] = (g * o_pre * sg * (1.0 - sg)).astype(dgate_ref.dtype)
            delta = jnp.sum(do * o_pre, axis=-1, keepdims=True)
            dq, dfc = 0.0, 0.0
            for lo, hi, causal in _fox_segments(i, tq):
                s, mask = _fox_scores(q_ref, k_ref, fc_ref, fr_ref, lo, hi, causal)
                if causal:
                    p = jnp.where(mask, jnp.exp(jnp.where(mask, s, 0.0) - lse_ref[...]), 0.0)
                else:
                    p = jnp.exp(s - lse_ref[...])
                ds = p * (_dot(do, v_ref[lo:hi, :], 1, 1, False) - delta)
                dq = dq + _dot(ds, k_ref[lo:hi, :], 1, 0, False)
                dk_ref[lo:hi, :] += _dot(ds, q_ref[...], 0, 0, False)
                dv_ref[lo:hi, :] += _dot(p, do, 0, 0, False)
                dfc = dfc + jnp.sum(ds, axis=-1, keepdims=True)
                dfr_ref[:, lo:hi] += -jnp.sum(ds, axis=0, keepdims=True)
            dq_ref[...] = dq
            dfc_ref[...] = dfc

        for i in range(t // tq):
            pl.when(pl.program_id(1) == i)(functools.partial(block, i))

    f32 = lambda *s: jax.ShapeDtypeStruct(s, F32)
    return pl.pallas_call(
        body, name="fox_attn_bwd", grid=(N_HEADS, t // tq),
        in_specs=[q_spec, k_spec, v_spec, gate_spec, col_spec, row_spec, blk_spec, col_spec, blk_spec],
        out_specs=[blk_spec, head_spec, head_spec, blk_spec, col_spec, row_spec],
        out_shape=[f32(t, D_MODEL), f32(t, D_MODEL), f32(t, D_MODEL), jax.ShapeDtypeStruct((t, D_MODEL), MM),
                   f32(N_HEADS, t, 1), f32(N_HEADS, 1, t)],
        compiler_params=_cparams(("parallel", "arbitrary")),
    )(qk, qk, proj, proj, fcol, frow, o, lse, dcat)


def adamw(w, g, m, v, *, name):
    r, c = w.shape
    rb = ROWS if r % ROWS == 0 else r

    def body(w_ref, g_ref, m_ref, v_ref, d_ref, nm_ref, nv_ref):
        g_ = g_ref[...]
        m_ = ADAM_B1 * m_ref[...] + (1.0 - ADAM_B1) * g_
        v_ = ADAM_B2 * v_ref[...] + (1.0 - ADAM_B2) * jnp.square(g_)
        m_hat = m_ / (1.0 - ADAM_B1 ** ADAM_STEP)
        v_hat = v_ / (1.0 - ADAM_B2 ** ADAM_STEP)
        d_ref[...] = -ADAM_LR * (m_hat / (jnp.sqrt(v_hat) + ADAM_EPS) + ADAM_WD * w_ref[...])
        nm_ref[...] = m_
        nv_ref[...] = v_

    blk = pl.BlockSpec((rb, c), lambda i: (i, 0))
    shp = jax.ShapeDtypeStruct((r, c), F32)
    return pl.pallas_call(body, name=name, grid=(r // rb,), in_specs=[blk] * 4, out_specs=[blk] * 3,
                          out_shape=[shp] * 3, compiler_params=_cparams(("parallel",)))(w, g, m, v)


def _place():
    x, y, c = lax.axis_index("x"), lax.axis_index("y"), lax.axis_index("c")
    return x, y, c, [(1 - x, y), (x, 1 - y), (1 - x, 1 - y)]


ANY = pl.BlockSpec(memory_space=pl.ANY)


def all_reduce_small(v):
    r, w = v.shape

    def body(v_ref, o_ref, buf, send_sems, recv_sems):
        x, y, c, _ = _place()
        me = 4 * x + 2 * y + c
        flip = lambda a, bit: 1 - a if bit else a
        cps = []
        for k in range(1, N_DEV):
            peer = (flip(x, k & 4), flip(y, k & 2), flip(c, k & 1))
            cp = pltpu.make_async_remote_copy(src_ref=v_ref, dst_ref=buf.at[me], send_sem=send_sems.at[k - 1],
                                              recv_sem=recv_sems.at[k - 1], device_id=peer, device_id_type=MESH)
            cp.start()
            cps.append((cp, 4 * peer[0] + 2 * peer[1] + peer[2]))
        buf[me] = v_ref[...]
        for k, (cp, peer_id) in enumerate(cps):
            pltpu.make_async_remote_copy(src_ref=v_ref, dst_ref=buf.at[peer_id], send_sem=send_sems.at[k],
                                         recv_sem=recv_sems.at[k], device_id=(x, y, c), device_id_type=MESH).wait_recv()
        for cp, _ in cps:
            cp.wait_send()
        acc = buf[0]
        for d in range(1, N_DEV):
            acc = acc + buf[d]
        o_ref[...] = acc

    vm = pl.BlockSpec(memory_space=pltpu.VMEM)
    return pl.pallas_call(
        body, name="all_reduce_small", in_specs=[vm], out_specs=vm, out_shape=jax.ShapeDtypeStruct((r, w), F32),
        scratch_shapes=[pltpu.VMEM((N_DEV, r, w), F32), pltpu.SemaphoreType.DMA((N_DEV - 1,)),
                        pltpu.SemaphoreType.DMA((N_DEV - 1,))],
    )(v)


def _vec8(v):
    return jnp.zeros((1, HEAD_DIM), F32).at[0, :N_HEADS].set(v.reshape(N_HEADS))


def _layer_fwd(i, x_in, wt, sm, mem_k, mem_v, late=None):
    tag = f"l{i}_"
    h = rms_fwd(x_in, sm["norm1_w"][i][None], name=tag + "rms1")
    w_in = wt["dn_w_in"] if i == 0 else wt["fox_w_in"]
    proj = matmul(h, w_in, name=tag + "proj", tm=256, tk=1024)
    sv = dict(x_in=x_in, h=h, proj=proj)
    if i == 0:
        qkv = dn_prep_fwd(proj, wt["conv_w"])
        gates = dn_gates_fwd(proj, _vec8(sm["dn_a_log"]), _vec8(sm["dn_dt_bias"]))
        gcol, grow, bcol = gates_to_heads(gates)
        o, states, parts = dn_core_fwd(qkv, gcol, grow, bcol)
        mix = dn_out_fwd(o, proj, sm["dn_o_norm_w"])
        sv.update(qkv=qkv, gcol=gcol, grow=grow, bcol=bcol, states=states, parts=parts, o=o)
    else:
        wqk = jnp.stack([sm["fox_q_norm_w"], sm["fox_k_norm_w"]])
        qk = fox_prep_fwd(proj, wqk)
        fcum = fox_gates_fwd(proj, _vec8(sm["fox_f_bias"]))
        fcol, frow = fcum_to_heads(fcum)
        mix, o, lse = fox_attn_fwd(qk, proj, fcol, frow)
        sv.update(wqk=wqk, qk=qk, fcol=fcol, frow=frow, o=o, lse=lse)
    mem_out = memattn_fwd(proj, sm["memq_norm_w"][i][None], mem_k, mem_v, name=tag + "memattn_fwd")
    cat = jnp.concatenate([mix, mem_out], axis=1)
    if late is not None:
        wt.update(late(cat))
    x_mid = matmul(cat, wt["w_out"][i], res=x_in, name=tag + "out_proj")
    h2 = rms_fwd(x_mid, sm["norm2_w"][i][None], name=tag + "rms2")
    ff, act = matmul(h2, wt["w_mlp1"][i], b_slots=True, also_sqrelu=True, out_dtype=MM, name=tag + "mlp1")
    x_out = matmul(act, wt["w_mlp2"][i], res=x_mid, name=tag + "mlp2")
    sv.update(cat=cat, x_mid=x_mid, h2=h2, ff=ff, act=act)
    return x_out, sv


def _layer_bwd(i, dx_out, sv, wt, sm, mem_k, mem_v, on_mlp=None, on_core=None):
    tag = f"l{i}_"
    big, small = {}, {}
    dff = matmul(dx_out, wt["w_mlp2"][i], tb=True, times_dsqrelu=sv["ff"], out_dtype=MM, name=tag + "d_ff")
    big["w_mlp2"] = matmul(sv["act"], dx_out, ta=True, name=tag + "d_w_mlp2", tk=2048)
    dh2 = matmul(dff, wt["w_mlp1"][i], tb=True, b_slots=True, name=tag + "d_h2")
    big["w_mlp1"] = matmul(sv["h2"], dff, ta=True, name=tag + "d_w_mlp1", tm=512, tn=D_FF, tk=512)
    dx_mid, small["norm2_w"] = rms_bwd(sv["x_mid"], sm["norm2_w"][i][None], dh2, dx_out, name=tag + "rms2_bwd")
    dcat = matmul(dx_mid, wt["w_out"][i], tb=True, name=tag + "d_cat")
    big["w_out"] = matmul(sv["cat"], dx_mid, ta=True, name=tag + "d_w_out", tk=2048)
    proj = sv["proj"]
    memq_norm_w = sm["memq_norm_w"][i][None]
    if on_mlp is not None:
        memq_norm_w = memq_norm_w + on_mlp(big["w_mlp2"], big["w_mlp1"], big["w_out"])
    dqm, small["memq_norm_w"], dmk, dmv = memattn_bwd(proj, memq_norm_w, mem_k, mem_v, dcat, name=tag + "memattn_bwd")
    t = proj.shape[0]
    pad = jnp.zeros((t, PROJ_W - TAIL - HEAD_DIM), MM)
    if i == 0:
        do, dz, small["dn_o_norm_w"] = dn_out_bwd(sv["o"], proj, sm["dn_o_norm_w"], dcat)
        dqkv, dgc, dgr, dbc = dn_core_bwd(sv["qkv"], sv["gcol"], sv["grow"], sv["bcol"], sv["states"], sv["parts"], do)
        a_log = _vec8(sm["dn_a_log"])
        if on_core is not None:
            a_log = a_log + on_core(dqkv)
        dtail, dal, ddt = dn_gates_bwd(proj, a_log, _vec8(sm["dn_dt_bias"]), heads_to_gates(dgc, dgr, dbc))
        dmain, dconv = dn_prep_bwd(proj, wt["conv_w"], dqkv)
        small["dn_a_log"], small["dn_dt_bias"] = dal[:, :N_HEADS], ddt[:, :N_HEADS]
        big["conv_w"] = dconv[:4]
        dproj = jnp.concatenate([dmain, dz, dqm, dtail, pad], axis=1)
    else:
        dq, dk, dv, dgate, dfc, dfr = fox_attn_bwd(sv["qk"], proj, sv["fcol"], sv["frow"], sv["o"], sv["lse"], dcat)
        dtail, dfb = fox_gates_bwd(proj, _vec8(sm["fox_f_bias"]), heads_to_fcum(dfc, dfr))
        dqk, dwqk = fox_prep_bwd(proj, sv["wqk"], dq, dk)
        small["fox_f_bias"] = dfb[:, :N_HEADS]
        small["fox_q_norm_w"], small["fox_k_norm_w"] = dwqk[0], dwqk[1]
        dproj = jnp.concatenate([dqk, dv.astype(MM), dgate, dqm, dtail, pad], axis=1)
    w_in = wt["dn_w_in"] if i == 0 else wt["fox_w_in"]
    dh = matmul(dproj, w_in, tb=True, name=tag + "d_h", tm=512)
    big["w_in"] = matmul(sv["h"], dproj, ta=True, name=tag + "d_w_in", tm=256)
    dx_in, small["norm1_w"] = rms_bwd(sv["x_in"], sm["norm1_w"][i][None], dh, dx_mid, name=tag + "rms1_bwd")
    return dx_in, big, small, (dmk, dmv)


def local_step(x, mem, target, wt, sm, late=None, on_layer1=None, on_mlp0=None, on_core0=None):
    wt = dict(wt)
    mem_k, mem_v = mem_fwd(mem, sm["mem_norm_w"][None], wt["w_mem_kv"], sm["mem_k_norm_w"][None])
    x0, sv0 = _layer_fwd(0, x, wt, sm, mem_k, mem_v, late)
    x1, sv1 = _layer_fwd(1, x0, wt, sm, mem_k, mem_v)
    dy, loss = loss_fwd(x1, target, name="loss")
    dx1, big1, small1, dm1 = _layer_bwd(1, dy, sv1, wt, sm, mem_k, mem_v)
    if on_layer1 is not None:
        dx1 = dx1 + on_layer1(big1)
    dx0, big0, small0, dm0 = _layer_bwd(0, dx1, sv0, wt, sm, mem_k, mem_v, on_mlp0, on_core0)
    dwn, dwkv, dwkn = mem_bwd(mem, sm["mem_norm_w"][None], wt["w_mem_kv"], sm["mem_k_norm_w"][None], *dm0, *dm1)
    small = dict(mem_norm_w=dwn[0], mem_k_norm_w=dwkn[0],
                 norm1_w=jnp.concatenate([small0["norm1_w"], small1["norm1_w"]]),
                 norm2_w=jnp.concatenate([small0["norm2_w"], small1["norm2_w"]]),
                 memq_norm_w=jnp.concatenate([small0["memq_norm_w"], small1["memq_norm_w"]]),
                 dn_a_log=small0["dn_a_log"], dn_dt_bias=small0["dn_dt_bias"], dn_o_norm_w=small0["dn_o_norm_w"],
                 fox_f_bias=small1["fox_f_bias"], fox_q_norm_w=small1["fox_q_norm_w"], fox_k_norm_w=small1["fox_k_norm_w"])
    big = dict(w_mem_kv=dwkv, dn_w_in=big0["w_in"], fox_w_in=big1["w_in"], conv_w=big0["conv_w"],
               w_out=[big0["w_out"], big1["w_out"]], w_mlp1=[big0["w_mlp1"], big1["w_mlp1"]],
               w_mlp2=[big0["w_mlp2"], big1["w_mlp2"]])
    return loss, dx0, big, small


def w_in_to_kernel(w, n_scalars):
    pad = jnp.zeros((w.shape[0], PROJ_W - TAIL - n_scalars), w.dtype)
    return jnp.concatenate([w[:, :4096], w[:, 4096 + n_scalars:], w[:, 4096:4096 + n_scalars], pad], axis=1)


def w_in_from_kernel(w, n_scalars):
    return jnp.concatenate([w[:, :4096], w[:, TAIL:TAIL + n_scalars], w[:, 4096:TAIL]], axis=1)


BIG_SPECS = dict(w_mem_kv=("rows", 1, 256, 1024), w_out=("rows", 2, 384, 1024), w_mlp2=("rows", 2, 1024, 1024),
                 w_mlp1=("cols", 2, 1024, 1024), dn_w_in=("rows", 1, 1024, 1156), fox_w_in=("rows", 1, 1024, 1154))
BIG_NAMES = tuple(BIG_SPECS)
EARLY_NAMES = ("w_mem_kv", "dn_w_in")
LATE_NAMES = ("w_out", "w_mlp2", "w_mlp1", "fox_w_in")
BIG_SPECS.update({f"{name}_{i}": (BIG_SPECS[name][0], 1) + BIG_SPECS[name][2:]
                  for name in ("w_out", "w_mlp2", "w_mlp1") for i in range(2)})
RS_LAYER1 = ("fox_w_in", "w_out_1", "w_mlp2_1", "w_mlp1_1")
RS_MLP0 = ("w_mlp2_0", "w_mlp1_0", "w_out_0")
RS_LAST = ("dn_w_in", "w_mem_kv")


def _full_shape(name, half=False):
    kind, a, b, c = BIG_SPECS[name]
    b = b // 2 if half else b
    return (a, N_CHIP, b, c) if kind == "rows" else (a, b, N_CHIP * c)


def _ds(start, size, align):
    return pl.ds(start if isinstance(start, int) else pl.multiple_of(start, align), size)


def _half_rows(name, h):
    b = BIG_SPECS[name][2]
    return _ds(h * (b // 2), b // 2, 16)


def _shard_idx(name, h):
    return (slice(None), _half_rows(name, h), slice(None))


def _full_idx(name, j=None, h=None):
    kind, _, _, c = BIG_SPECS[name]
    rows = slice(None) if h is None else _half_rows(name, h)
    if kind == "rows":
        return (slice(None), slice(None) if j is None else j, rows, slice(None))
    return (slice(None), rows, slice(None) if j is None else _ds(j * c, c, 128))


def _slots_shape(name):
    _, a, b, c = BIG_SPECS[name]
    return (a, N_CHIP, b, c)


def _slots_idx(name, j, h):
    return (slice(None), j, _half_rows(name, h), slice(None))


def _row_block(name):
    hs = BIG_SPECS[name][2] // 2
    return hs if hs <= ROWS else ROWS


def _remote(src, dst, send_sem, recv_sem, to):
    return pltpu.make_async_remote_copy(src_ref=src, dst_ref=dst, send_sem=send_sem, recv_sem=recv_sem, device_id=to,
                                        device_id_type=MESH)


HBM = pl.BlockSpec(memory_space=pltpu.HBM)
SEM = pl.BlockSpec(memory_space=pltpu.SEMAPHORE)
EFFECT = pltpu.CompilerParams(has_side_effects=pltpu.SideEffectType.DATAFLOW_SIDE_EFFECTING)


def _in_hbm(a):
    return pltpu.with_memory_space_constraint(a, pltpu.HBM)


def _chip_copies(names, ins, lands, send_sems, recv_sems):
    x, y, c, chips = _place()
    return [_remote(ins[a].at[_shard_idx(name, c)], lands[a].at[_slots_idx(name, 2 * x + y, c)], send_sems.at[3 * a + k],
                    recv_sems.at[3 * a + k], (chip[0], chip[1], c))
            for a, name in enumerate(names) for k, chip in enumerate(chips)]


def _copies_start(call_name, copies, sources, land_shapes, per_source=3, land_dtype=MM):
    n = len(sources)

    def body(*refs):
        ins, lands, send_sems, recv_sems, token = refs[:n], refs[n:2 * n], refs[2 * n], refs[2 * n + 1], refs[-1]
        for cp in copies(ins, lands, send_sems, recv_sems):
            cp.start()
        token[...] = jnp.zeros_like(token)

    ins = [_in_hbm(a) for a in sources]
    lands = [_in_hbm(lax.empty(shape, land_dtype)) for shape in land_shapes]
    sems = (pltpu.SemaphoreType.DMA((per_source * n,)), pltpu.SemaphoreType.DMA((per_source * n,)))
    outs = pl.pallas_call(
        body, name=call_name, in_specs=[HBM] * (2 * n),
        out_specs=(SEM, SEM) + (HBM,) * (2 * n) + (pl.BlockSpec(memory_space=pltpu.VMEM),),
        out_shape=sems + tuple(pltpu.HBM(a.shape, a.dtype) for a in ins + lands) + (jax.ShapeDtypeStruct((8, HEAD_DIM), F32),),
        input_output_aliases={a: 2 + a for a in range(2 * n)}, compiler_params=EFFECT,
    )(*ins, *lands)
    return outs[:-1], outs[-1]


def _copies_wait(call_name, copies, state, after):
    n = (len(state) - 2) // 2

    def body(*refs):
        send_sems, recv_sems, ins, lands = refs[0], refs[1], refs[2:2 + n], refs[2 + n:2 + 2 * n]
        for cp in copies(ins, lands, send_sems, recv_sems):
            cp.wait_send()
            cp.wait_recv()

    outs = pl.pallas_call(
        body, name=call_name, in_specs=[SEM, SEM] + [HBM] * (2 * n) + [ANY], out_specs=(HBM,) * (2 * n),
        out_shape=tuple(pltpu.HBM(a.shape, a.dtype) for a in state[2:]),
        input_output_aliases={2 + a: a for a in range(2 * n)}, compiler_params=EFFECT,
    )(*state, after)
    return outs[:n], outs[n:]


def all_gather_start(shards, names):
    return _copies_start("all_gather_start", functools.partial(_chip_copies, names), [shards[name] for name in names],
                         [_slots_shape(name) for name in names])


def all_gather_wait(state, names, after):
    ins, lands = _copies_wait("all_gather_wait", functools.partial(_chip_copies, names), state, after)
    return dict(zip(names, ins)), dict(zip(names, lands))


def _chip_sends(names, ins, lands, send_sems, recv_sems):
    x, y, c, chips = _place()
    return [_remote(ins[a].at[_full_idx(name, 2 * chip[0] + chip[1])], lands[a].at[k], send_sems.at[3 * a + k],
                    recv_sems.at[3 * a + k], (chip[0], chip[1], c))
            for a, name in enumerate(names) for k, chip in enumerate(chips)]


def _got_shape(name):
    _, a_, b_, c_ = BIG_SPECS[name]
    return (3, a_, b_ // 2, c_)


def rs_chip_start(pairs, names, tag):
    return _copies_start("rs_chip_start_" + tag, functools.partial(_chip_sends, names), [pairs[name] for name in names],
                         [_got_shape(name) for name in names])


def rs_chip_wait(state, names, tag, after):
    _, lands = _copies_wait("rs_chip_wait_" + tag, functools.partial(_chip_sends, names), state, after)
    return dict(zip(names, lands))


def all_gather_pass_on(lands, names):
    n = len(names)

    def body(*refs):
        outs, send_sems, recv_sems = refs[n:2 * n], refs[2 * n], refs[2 * n + 1]
        x, y, c, chips = _place()
        work = [(3 * a + k, a, name, 2 * chip[0] + chip[1]) for a, name in enumerate(names) for k, chip in enumerate(chips)]
        cps = []
        for s, a, name, slot in work:
            landed = outs[a].at[_slots_idx(name, slot, c)]
            cps.append(_remote(landed, landed, send_sems.at[s], recv_sems.at[s], (x, y, 1 - c)))
            cps[-1].start()
        for s, a, name, slot in work:
            passed = outs[a].at[_slots_idx(name, slot, 1 - c)]
            _remote(passed, passed, send_sems.at[s], recv_sems.at[s], (x, y, 1 - c)).wait_recv()
        for cp in cps:
            cp.wait_send()

    outs = pl.pallas_call(
        body, name="all_gather_pass_on", in_specs=[ANY] * n, out_specs=[ANY] * n,
        input_output_aliases={a: a for a in range(n)},
        out_shape=[jax.ShapeDtypeStruct(_slots_shape(name), MM) for name in names],
        scratch_shapes=[pltpu.SemaphoreType.DMA((3 * n,)), pltpu.SemaphoreType.DMA((3 * n,))],
    )(*[lands[name] for name in names])
    return dict(zip(names, outs))


def all_gather_big(shards, names):
    n = len(names)
    BIG_NAMES = names

    def body(*refs):
        ins, outs = refs[:n], refs[n:2 * n]
        send_sems, recv_sems, fsend_sems, frecv_sems = refs[2 * n:]
        x, y, c, chips = _place()
        me_chip, sibling = 2 * x + y, (x, y, 1 - c)
        work = [(3 * a + k, a, name, chip) for a, name in enumerate(BIG_NAMES) for k, chip in enumerate(chips)]
        sends = []
        for s, a, name, chip in work:
            cp = _remote(ins[a].at[_shard_idx(name, c)], outs[a].at[_slots_idx(name, me_chip, c)], send_sems.at[s],
                         recv_sems.at[s], (chip[0], chip[1], c))
            cp.start()
            sends.append(cp)
        for s, a, name, chip in work:
            landed = outs[a].at[_slots_idx(name, 2 * chip[0] + chip[1], c)]
            _remote(landed, landed, send_sems.at[s], recv_sems.at[s], (chip[0], chip[1], c)).wait_recv()
            cp = _remote(landed, landed, fsend_sems.at[s], frecv_sems.at[s], sibling)
            cp.start()
            sends.append(cp)
        for s, a, name, chip in work:
            passed = outs[a].at[_slots_idx(name, 2 * chip[0] + chip[1], 1 - c)]
            _remote(passed, passed, fsend_sems.at[s], frecv_sems.at[s], sibling).wait_recv()
        for cp in sends:
            cp.wait_send()

    outs = pl.pallas_call(
        body, name="all_gather_big", in_specs=[ANY] * n, out_specs=[ANY] * n,
        out_shape=[jax.ShapeDtypeStruct(_slots_shape(name), MM) for name in BIG_NAMES],
        scratch_shapes=[pltpu.SemaphoreType.DMA((3 * n,))] * 4,
    )(*[shards[name] for name in BIG_NAMES])
    return dict(zip(BIG_NAMES, outs))


def with_own_slot(name, full, shard, chip):
    return lax.dynamic_update_slice(full, shard[:, None], (0, chip, 0, 0))


def rs_pair_exchange_big(grads, names, tag):
    n = len(names)

    def body(*refs):
        ins, outs, send_sems, recv_sems = refs[:n], refs[n:2 * n], refs[2 * n], refs[2 * n + 1]
        x, y, c, _ = _place()
        cps = []
        for a, name in enumerate(names):
            cp = _remote(ins[a].at[_full_idx(name, None, 1 - c)], outs[a], send_sems.at[a], recv_sems.at[a], (x, y, 1 - c))
            cp.start()
            cps.append(cp)
        for cp in cps:
            cp.wait()

    outs = pl.pallas_call(
        body, name="rs_pair_exchange_" + tag, in_specs=[ANY] * n, out_specs=[ANY] * n,
        out_shape=[jax.ShapeDtypeStruct(_full_shape(name, half=True), F32) for name in names],
        scratch_shapes=[pltpu.SemaphoreType.DMA((n,)), pltpu.SemaphoreType.DMA((n,))],
    )(*[grads[name] for name in names])
    return dict(zip(names, outs))


def rs_pair_add_big(name, place, g, got):
    kind, a_, b_, c_ = BIG_SPECS[name]
    rb = _row_block(name)
    nb = (b_ // 2) // rb

    def body(place_ref, g_ref, got_ref, o_ref):
        o_ref[...] = (g_ref[...] + got_ref[...]).astype(o_ref.dtype)

    if kind == "rows":
        g_spec = pl.BlockSpec((None, None, rb, c_), lambda a, j, i, p: (a, j, p[0] * nb + i, 0))
        o_spec = pl.BlockSpec((None, None, rb, c_), lambda a, j, i, p: (a, j, i, 0))
    else:
        g_spec = pl.BlockSpec((None, rb, c_), lambda a, j, i, p: (a, p[0] * nb + i, j))
        o_spec = pl.BlockSpec((None, rb, c_), lambda a, j, i, p: (a, i, j))
    return pl.pallas_call(
        body, name="rs_pair_add_" + name,
        grid_spec=pltpu.PrefetchScalarGridSpec(num_scalar_prefetch=1, grid=(a_, N_CHIP, nb), in_specs=[g_spec, o_spec],
                                               out_specs=o_spec),
        out_shape=jax.ShapeDtypeStruct(_full_shape(name, half=True), MM),
        compiler_params=_cparams(("parallel", "parallel", "parallel")),
    )(place, g, got)


def rs_chip_exchange_big(pairs, names, tag):
    n = len(names)

    def body(*refs):
        ins, outs, send_sems, recv_sems = refs[:n], refs[n:2 * n], refs[2 * n], refs[2 * n + 1]
        cps = _chip_sends(names, ins, outs, send_sems, recv_sems)
        for cp in cps:
            cp.start()
        for cp in cps:
            cp.wait()

    outs = pl.pallas_call(
        body, name="rs_chip_exchange_" + tag, in_specs=[ANY] * n, out_specs=[ANY] * n,
        out_shape=[jax.ShapeDtypeStruct(_got_shape(name), MM) for name in names],
        scratch_shapes=[pltpu.SemaphoreType.DMA((3 * n,)), pltpu.SemaphoreType.DMA((3 * n,))],
    )(*[pairs[name] for name in names])
    return dict(zip(names, outs))


def rs_chip_add_big(name, place, g, got_pair, got_chips):
    kind, a_, b_, c_ = BIG_SPECS[name]
    rb = _row_block(name)
    nb = (b_ // 2) // rb

    def body(place_ref, g_ref, s_ref, r0_ref, r1_ref, r2_ref, o_ref):
        own = g_ref[...] + s_ref[...]
        o_ref[...] = ((own + r0_ref[...].astype(F32)) + r1_ref[...].astype(F32)) + r2_ref[...].astype(F32)

    if kind == "rows":
        g_spec = pl.BlockSpec((None, None, rb, c_), lambda a, i, p: (a, p[1], p[0] * nb + i, 0))
        s_spec = pl.BlockSpec((None, None, rb, c_), lambda a, i, p: (a, p[1], i, 0))
    else:
        g_spec = pl.BlockSpec((None, rb, c_), lambda a, i, p: (a, p[0] * nb + i, p[1]))
        s_spec = pl.BlockSpec((None, rb, c_), lambda a, i, p: (a, i, p[1]))
    r_spec = lambda k: pl.BlockSpec((None, None, rb, c_), lambda a, i, p: (k, a, i, 0))
    return pl.pallas_call(
        body, name="rs_chip_add_" + name,
        grid_spec=pltpu.PrefetchScalarGridSpec(
            num_scalar_prefetch=1, grid=(a_, nb), in_specs=[g_spec, s_spec, r_spec(0), r_spec(1), r_spec(2)],
            out_specs=pl.BlockSpec((None, rb, c_), lambda a, i, p: (a, p[0] * nb + i, 0))),
        out_shape=jax.ShapeDtypeStruct((a_, b_, c_), F32), compiler_params=_cparams(("parallel", "parallel")),
    )(place, g, got_pair, got_chips, got_chips, got_chips)


def rs_pair_gather_big(halves):
    names = tuple(halves)
    n = len(names)

    def body(*refs):
        outs, send_sems, recv_sems = refs[n:2 * n], refs[2 * n], refs[2 * n + 1]
        x, y, c, _ = _place()
        cps = []
        for a, name in enumerate(names):
            mine = outs[a].at[_shard_idx(name, c)]
            cp = _remote(mine, mine, send_sems.at[a], recv_sems.at[a], (x, y, 1 - c))
            cp.start()
            cps.append(cp)
        for a, name in enumerate(names):
            cps[a].wait_send()
            theirs = outs[a].at[_shard_idx(name, 1 - c)]
            _remote(theirs, theirs, send_sems.at[a], recv_sems.at[a], (x, y, 1 - c)).wait_recv()

    outs = pl.pallas_call(
        body, name="rs_pair_gather_big", in_specs=[ANY] * n, out_specs=[ANY] * n,
        input_output_aliases={a: a for a in range(n)},
        out_shape=[jax.ShapeDtypeStruct(BIG_SPECS[name][1:], F32) for name in names],
        scratch_shapes=[pltpu.SemaphoreType.DMA((n,)), pltpu.SemaphoreType.DMA((n,))],
    )(*[halves[name] for name in names])
    return dict(zip(names, outs))


def _pair_sends(names, ins, lands, send_sems, recv_sems):
    x, y, c, _ = _place()
    return [_remote(ins[a].at[_full_idx(name, None, 1 - c)], lands[a], send_sems.at[a], recv_sems.at[a], (x, y, 1 - c))
            for a, name in enumerate(names)]


def rs_pair_start(grads, names, tag):
    return _copies_start("rs_pair_start_" + tag, functools.partial(_pair_sends, names), [grads[name] for name in names],
                         [_full_shape(name, half=True) for name in names], per_source=1, land_dtype=F32)


def rs_middle(pair_state, names, tag, place, after):
    ins, lands = _copies_wait("rs_pair_wait_" + tag, functools.partial(_pair_sends, names), pair_state, after)
    grads, got_pair = dict(zip(names, ins)), dict(zip(names, lands))
    pairs = {name: rs_pair_add_big(name, place, grads[name], got_pair[name]) for name in names}
    state, token = rs_chip_start(pairs, names, tag)
    return (grads, got_pair, state), token


def rs_end(begun, names, tag, place, after):
    grads, got_pair, state = begun
    got_chips = rs_chip_wait(state, names, tag, after)
    return {name: rs_chip_add_big(name, place, grads[name], got_pair[name], got_chips[name]) for name in names}


def rs_whole(grads, names, tag, place):
    got_pair = rs_pair_exchange_big(grads, names, tag)
    pairs = {name: rs_pair_add_big(name, place, grads[name], got_pair[name]) for name in names}
    got_chips = rs_chip_exchange_big(pairs, names, tag)
    return {name: rs_chip_add_big(name, place, grads[name], got_pair[name], got_chips[name]) for name in names}


PACK_W = 1024
SMALL =(("mem_norm_w", 1024), ("mem_k_norm_w", 128), ("norm1_w", 2048), ("dn_a_log", 8), ("dn_dt_bias", 8),
         ("dn_o_norm_w", 128), ("fox_f_bias", 8), ("fox_q_norm_w", 128), ("fox_k_norm_w", 128), ("memq_norm_w", 256),
         ("norm2_w", 2048))
SMALL_ROWS = 8
CONV_ROWS = 4 * 3 * D_MODEL // PACK_W
LOSS_AT = sum(n for _, n in SMALL)


def pack_small(parts, extra=None):
    flat = [parts[name].astype(F32).reshape(-1) for name, _ in SMALL]
    used = LOSS_AT
    if extra is not None:
        flat.append(extra.reshape(1))
        used += 1
    flat.append(jnp.zeros((SMALL_ROWS * PACK_W - used,), F32))
    return jnp.concatenate(flat).reshape(SMALL_ROWS, PACK_W)


def unpack_small(packed, shapes):
    flat, out, at = packed.reshape(-1), {}, 0
    for name, n in SMALL:
        out[name] = flat[at:at + n].reshape(shapes[name])
        at += n
    return out


def _adam_all(w, g, m, v, name):
    shape = w.shape
    r2 = lambda a: a.reshape(-1, shape[-1])
    d, nm, nv = adamw(r2(w), r2(g), r2(m), r2(v), name=name)
    return d.reshape(shape), nm.reshape(shape), nv.reshape(shape)


BIG = ("w_mem_kv", "dn_w_in", "dn_conv_w", "fox_w_in", "w_out", "w_mlp1", "w_mlp2")
WEIGHTS = ("mem_norm_w", "w_mem_kv", "mem_k_norm_w", "norm1_w", "dn_w_in", "dn_conv_w", "dn_a_log", "dn_dt_bias",
           "dn_o_norm_w", "fox_w_in", "fox_f_bias", "fox_q_norm_w", "fox_k_norm_w", "memq_norm_w", "w_out", "norm2_w",
           "w_mlp1", "w_mlp2")


def kernel(x, mem, mem_norm_w, w_mem_kv, mem_k_norm_w, norm1_w, dn_w_in, dn_conv_w, dn_a_log, dn_dt_bias, dn_o_norm_w, fox_w_in, fox_f_bias, fox_q_norm_w, fox_k_norm_w, memq_norm_w, w_out, norm2_w, w_mlp1, w_mlp2, loss_target, m_mem_norm_w, m_w_mem_kv, m_mem_k_norm_w, m_norm1_w, m_dn_w_in, m_dn_conv_w, m_dn_a_log, m_dn_dt_bias, m_dn_o_norm_w, m_fox_w_in, m_fox_f_bias, m_fox_q_norm_w, m_fox_k_norm_w, m_memq_norm_w, m_w_out, m_norm2_w, m_w_mlp1, m_w_mlp2, v_mem_norm_w, v_w_mem_kv, v_mem_k_norm_w, v_norm1_w, v_dn_w_in, v_dn_conv_w, v_dn_a_log, v_dn_dt_bias, v_dn_o_norm_w, v_fox_w_in, v_fox_f_bias, v_fox_q_norm_w, v_fox_k_norm_w, v_memq_norm_w, v_w_out, v_norm2_w, v_w_mlp1, v_w_mlp2):
    args = dict(locals())
    w = {n: args[n] for n in WEIGHTS}
    m = {n: args["m_" + n] for n in WEIGHTS}
    v = {n: args["v_" + n] for n in WEIGHTS}
    core, chip = lax.axis_index("c"), 2 * lax.axis_index("x") + lax.axis_index("y")
    place = jnp.stack([core, chip]).astype(jnp.int32)

    shards = {name: w[name].reshape(BIG_SPECS[name][1:]).astype(MM) for name in BIG_NAMES}
    w_in_full = lambda arr, n_scalars: w_in_to_kernel(arr[0].transpose(1, 0, 2).reshape(D_MODEL, -1), n_scalars)
    early = {name: with_own_slot(name, arr, shards[name], chip)
             for name, arr in all_gather_big(shards, EARLY_NAMES).items()}
    conv_mine = jnp.where(core == 0, dn_conv_w[0], 0.0)
    conv_placed = lax.dynamic_update_slice(jnp.zeros((4, 3 * D_MODEL), F32), conv_mine, (0, 768 * chip))
    conv_full = all_reduce_small(jnp.pad(conv_placed.reshape(CONV_ROWS, PACK_W), ((0, 16 - CONV_ROWS), (0, 0))))
    late_shards, early, conv_full = lax.optimization_barrier(
        ({name: shards[name] for name in LATE_NAMES}, early, conv_full))
    late_state, token = all_gather_start(late_shards, LATE_NAMES)
    tie = token[0, 0]
    wt = dict(w_mem_kv=early["w_mem_kv"].reshape(D_MODEL, 2 * MEM_WIDTH) + tie.astype(MM),
              dn_w_in=w_in_full(early["dn_w_in"], 2 * N_HEADS), conv_w=conv_full[:CONV_ROWS].reshape(4, 3 * D_MODEL))

    def late(after):
        late_shards, lands = all_gather_wait(late_state, LATE_NAMES, after)
        full = {name: with_own_slot(name, arr, late_shards[name], chip)
                for name, arr in all_gather_pass_on(lands, LATE_NAMES).items()}
        return dict(fox_w_in=w_in_full(full["fox_w_in"], N_HEADS), w_out=full["w_out"].reshape(2, 3 * MEM_WIDTH, D_MODEL),
                    w_mlp1=full["w_mlp1"], w_mlp2=full["w_mlp2"].reshape(2, D_FF, D_MODEL))

    sm = dict(mem_norm_w=mem_norm_w, mem_k_norm_w=mem_k_norm_w, norm1_w=norm1_w, norm2_w=norm2_w, memq_norm_w=memq_norm_w,
              dn_a_log=dn_a_log[0], dn_dt_bias=dn_dt_bias[0], dn_o_norm_w=dn_o_norm_w, fox_f_bias=fox_f_bias[0],
              fox_q_norm_w=fox_q_norm_w, fox_k_norm_w=fox_k_norm_w)
    w_in_slots = lambda g, n_scalars: w_in_from_kernel(g, n_scalars).reshape(D_MODEL, N_CHIP, -1).transpose(1, 0, 2)[None]
    rows_view = lambda g, name: g.reshape(_full_shape(name))
    pair_started, begun = {}, {}

    def on_layer1(big1):
        grads1 = dict(fox_w_in=w_in_slots(big1["w_in"], N_HEADS), w_out_1=rows_view(big1["w_out"], "w_out_1"),
                      w_mlp2_1=rows_view(big1["w_mlp2"], "w_mlp2_1"), w_mlp1_1=big1["w_mlp1"][None])
        pair_started["layer1"], token = rs_pair_start(grads1, RS_LAYER1, "layer1")
        return token[0, 0]

    def on_mlp0(d_w_mlp2, d_w_mlp1, d_w_out):
        begun["layer1"], token1 = rs_middle(pair_started["layer1"], RS_LAYER1, "layer1", place, d_w_out)
        grads0 = dict(w_mlp2_0=rows_view(d_w_mlp2, "w_mlp2_0"), w_mlp1_0=d_w_mlp1[None],
                      w_out_0=rows_view(d_w_out, "w_out_0"))
        grads0, _ = lax.optimization_barrier((grads0, token1))
        pair_started["mlp0"], token0 = rs_pair_start(grads0, RS_MLP0, "mlp0")
        return token1[0, 0] + token0[0, 0]

    def on_core0(dqkv):
        begun["mlp0"], token = rs_middle(pair_started["mlp0"], RS_MLP0, "mlp0", place, dqkv)
        return token[0, 0]

    loss_part, dx, big, small = local_step(x[0], mem[0], loss_target[0], wt, sm, late, on_layer1, on_mlp0, on_core0)
    last = dict(dn_w_in=w_in_slots(big["dn_w_in"], 2 * N_HEADS), w_mem_kv=rows_view(big["w_mem_kv"], "w_mem_kv"))
    halves = rs_whole(last, RS_LAST, "last", place)
    halves.update(rs_end(begun["layer1"], RS_LAYER1, "layer1", place, dx))
    halves.update(rs_end(begun["mlp0"], RS_MLP0, "mlp0", place, dx))
    summed = rs_pair_gather_big(halves)
    big_sum = {name: summed[name] for name in ("w_mem_kv", "dn_w_in", "fox_w_in")}
    big_sum.update({name: jnp.concatenate([summed[name + "_0"], summed[name + "_1"]]) for name in ("w_out", "w_mlp2", "w_mlp1")})
    small_pack = jnp.concatenate([pack_small(small, loss_part[0, :1]), big["conv_w"].reshape(CONV_ROWS, PACK_W),
                                  jnp.zeros((24 - SMALL_ROWS - CONV_ROWS, PACK_W), F32)])
    small_all = all_reduce_small(small_pack)
    small_sum = small_all[:SMALL_ROWS]
    conv_sum = lax.dynamic_slice(small_all[SMALL_ROWS:SMALL_ROWS + CONV_ROWS].reshape(4, 3 * D_MODEL), (0, 768 * chip), (4, 768))
    loss = small_sum.reshape(-1)[LOSS_AT]
    grads = unpack_small(small_sum, {n: w[n].shape for n, _ in SMALL})
    grads.update({name: big_sum[name].reshape(w[name].shape) for name in BIG_NAMES}, dn_conv_w=conv_sum[None])

    delta, new_m, new_v = {}, {}, {}
    for n in BIG:
        delta[n], new_m[n], new_v[n] = _adam_all(w[n], grads[n], m[n], v[n], "adamw_" + n)
    shapes = {n: w[n].shape for n, _ in SMALL}
    d_s, m_s, v_s = adamw(pack_small(w), small_sum, pack_small(m), pack_small(v), name="adamw_small")
    for out, packed in ((delta, d_s), (new_m, m_s), (new_v, v_s)):
        out.update(unpack_small(packed, shapes))
    return (loss, dx[None], *[grads[n] for n in WEIGHTS], *[delta[n] for n in WEIGHTS],
            *[new_m[n] for n in WEIGHTS], *[new_v[n] for n in WEIGHTS])
```

```python
import functools

import jax
import jax.numpy as jnp
from jax import lax
from jax.experimental import pallas as pl
from jax.experimental.pallas import tpu as pltpu

F32 = jnp.float32
MM = jnp.bfloat16
HI = lax.Precision.HIGHEST

D_MODEL = 1024
HEAD_DIM = 128
N_HEADS = 8
MEM_HEADS = 4
MEM_WIDTH = MEM_HEADS * HEAD_DIM
N_MEM = 256
D_FF = 4 * D_MODEL
CHUNK = 64
EPS = 1e-6
QSCALE = HEAD_DIM ** -0.5
PROJ_W = 4736
TAIL = 4608
TAIL_BLK = TAIL // HEAD_DIM
ROWS = 256
VMEM_LIMIT = 56 * 1024 * 1024

ADAM_LR = 0.001
ADAM_B1 = 0.9
ADAM_B2 = 0.999
ADAM_EPS = 1e-08
ADAM_WD = 0.01
ADAM_STEP = 10

N_DEV = 8
N_CHIP = 4
MESH = pl.DeviceIdType.MESH


def _cparams(sem=None):
    return pltpu.CompilerParams(dimension_semantics=sem, vmem_limit_bytes=VMEM_LIMIT)


def _dot(a, b, ca, cb, hi):
    dims = (((ca,), (cb,)), ((), ()))
    if hi:
        return lax.dot_general(a, b, dims, precision=HI, preferred_element_type=F32)
    return lax.dot_general(a.astype(MM), b.astype(MM), dims, preferred_element_type=F32)


@functools.partial(jax.custom_vjp, nondiff_argnums=(2, 3, 4))
def mmul(a, b, ca, cb, hi):
    return _dot(a, b, ca, cb, hi)


def _mmul_fwd(a, b, ca, cb, hi):
    return _dot(a, b, ca, cb, hi), (a, b)


def _mmul_bwd(ca, cb, hi, res, g):
    a, b = res
    if ca == 1:
        da = _dot(g, b, 1, 1, hi) if cb == 0 else _dot(g, b, 1, 0, hi)
    else:
        da = _dot(b, g, 1, 1, hi) if cb == 0 else _dot(b, g, 0, 1, hi)
    if cb == 0:
        db = _dot(a, g, 0, 0, hi) if ca == 1 else _dot(a, g, 1, 0, hi)
    else:
        db = _dot(g, a, 0, 0, hi) if ca == 1 else _dot(g, a, 0, 1, hi)
    return da.astype(a.dtype), db.astype(b.dtype)


mmul.defvjp(_mmul_fwd, _mmul_bwd)


def _iota2(n, m):
    return lax.broadcasted_iota(jnp.int32, (n, m), 0), lax.broadcasted_iota(jnp.int32, (n, m), 1)


def _same_block(r, c, shift):
    return lax.shift_right_logical(r, shift) == lax.shift_right_logical(c, shift)


def _split_bf16(x):
    hi = x.astype(jnp.bfloat16)
    return hi, (x - hi.astype(F32)).astype(jnp.bfloat16)


def _dot3(a, b, ca, cb):
    dims = (((ca,), (cb,)), ((), ()))
    (ah, al), (bh, bl) = _split_bf16(a), _split_bf16(b)
    d = lambda x, y: lax.dot_general(x, y, dims, preferred_element_type=F32)
    return d(ah, bh) + (d(ah, bl) + d(al, bh))


def _tri_inv_impl(a):
    n = a.shape[0]
    r, c = _iota2(n, n)
    eye = (r == c).astype(F32)
    b16, b32 = _same_block(r, c, 4), _same_block(r, c, 5)
    a0 = jnp.where(b16, a, 0.0)
    p = eye - a0
    b = _dot3(a0, a0, 1, 0)
    p = p + _dot3(p, b, 1, 0)
    b = _dot3(b, b, 1, 0)
    p = p + _dot3(p, b, 1, 0)
    b = _dot3(b, b, 1, 0)
    p = p + _dot3(p, b, 1, 0)
    a1 = jnp.where(jnp.logical_and(b32, jnp.logical_not(b16)), a, 0.0)
    p = p - _dot3(_dot3(p, a1, 1, 0), p, 1, 0)
    a2 = jnp.where(b32, 0.0, a)
    p = p - _dot3(_dot3(p, a2, 1, 0), p, 1, 0)
    return p


@jax.custom_vjp
def tri_inv(a):
    return _tri_inv_impl(a)


def _tri_inv_fwd(a):
    p = _tri_inv_impl(a)
    return p, p


def _tri_inv_bwd(p, g):
    return (-_dot3(_dot3(p, g, 0, 0), p, 1, 1),)


tri_inv.defvjp(_tri_inv_fwd, _tri_inv_bwd)


def _sigmoid(x):
    return 1.0 / (1.0 + jnp.exp(-x))


def _softplus(x):
    return jnp.maximum(x, 0.0) + jnp.log(1.0 + jnp.exp(-jnp.abs(x)))


def _silu(x):
    return x * _sigmoid(x)


def _rms(x, w):
    return x * lax.rsqrt(jnp.mean(x * x, axis=-1, keepdims=True) + EPS) * w


def _bf_round(x):
    return x.astype(MM).astype(F32)


def _acc(ref, val, first):
    @pl.when(first)
    def _():
        ref[...] = val

    @pl.when(jnp.logical_not(first))
    def _():
        ref[...] += val


def _tile(n, pref):
    if n % pref == 0:
        return pref
    return n


def matmul(a, b, *, ta=False, tb=False, b_slots=False, res=None, also_sqrelu=False, times_dsqrelu=None, out_dtype=F32,
           name, tm=1024, tn=1024, tk=1024):
    m, k = (a.shape[1], a.shape[0]) if ta else a.shape
    if b_slots:
        n = b.shape[1] if tb else N_CHIP * b.shape[2]
        assert (N_CHIP * b.shape[2] if tb else b.shape[1]) == k, (a.shape, b.shape, ta, tb)
        tn, tk = (tn, b.shape[2]) if tb else (b.shape[2], tk)
    else:
        n = b.shape[0] if tb else b.shape[1]
        assert (b.shape[1] if tb else b.shape[0]) == k, (a.shape, b.shape, ta, tb)
    tm, tn, tk = _tile(m, tm), _tile(n, tn), _tile(k, tk)
    nk = k // tk
    ca, cb = (0 if ta else 1), (1 if tb else 0)

    extra = tuple(e for e in (res, times_dsqrelu) if e is not None)
    assert len(extra) <= 1

    def body(a_ref, b_ref, *rest):
        e_ref = rest[0] if extra else None
        o_ref = rest[len(extra)]

        def finish(total):
            if res is not None:
                total = total + e_ref[...]
            if times_dsqrelu is not None:
                total = total * (2.0 * jnp.maximum(e_ref[...], 0.0))
            o_ref[...] = total.astype(o_ref.dtype)
            if also_sqrelu:
                rest[len(extra) + 1][...] = _sqrelu(total).astype(MM)

        if nk == 1:
            finish(_dot(a_ref[...], b_ref[...], ca, cb, False))
            return
        acc_ref, kk = rest[-1], pl.program_id(2)

        @pl.when(kk == 0)
        def _():
            acc_ref[...] = jnp.zeros_like(acc_ref)

        acc_ref[...] += _dot(a_ref[...], b_ref[...], ca, cb, False)

        @pl.when(kk == nk - 1)
        def _():
            finish(acc_ref[...])

    a_spec = pl.BlockSpec((tk, tm), lambda i, j, l: (l, i)) if ta else pl.BlockSpec((tm, tk), lambda i, j, l: (i, l))
    if b_slots:
        b_spec = (pl.BlockSpec((None, tn, tk), lambda i, j, l: (l, j, 0)) if tb else
                  pl.BlockSpec((None, tk, tn), lambda i, j, l: (j, l, 0)))
    else:
        b_spec = pl.BlockSpec((tn, tk), lambda i, j, l: (j, l)) if tb else pl.BlockSpec((tk, tn), lambda i, j, l: (l, j))
    o_spec = pl.BlockSpec((tm, tn), lambda i, j, l: (i, j))
    out_shape = [jax.ShapeDtypeStruct((m, n), out_dtype)] + [jax.ShapeDtypeStruct((m, n), MM)] * also_sqrelu
    outs = pl.pallas_call(
        body, name=name, grid=(m // tm, n // tn, nk),
        in_specs=[a_spec, b_spec] + [o_spec] * len(extra), out_specs=[o_spec] * len(out_shape), out_shape=out_shape,
        scratch_shapes=[pltpu.VMEM((tm, tn), F32)] * (nk > 1),
        compiler_params=_cparams(("parallel", "parallel", "arbitrary")),
    )(a, b, *extra)
    return outs if also_sqrelu else outs[0]


def rms_fwd(x, w, *, name):
    t, d = x.shape

    def body(x_ref, w_ref, o_ref):
        o_ref[...] = _rms(x_ref[...], w_ref[...]).astype(o_ref.dtype)

    return pl.pallas_call(
        body, name=name, grid=(t // ROWS,),
        in_specs=[pl.BlockSpec((ROWS, d), lambda i: (i, 0)), pl.BlockSpec((1, d), lambda i: (0, 0))],
        out_specs=pl.BlockSpec((ROWS, d), lambda i: (i, 0)),
        out_shape=jax.ShapeDtypeStruct((t, d), MM), compiler_params=_cparams(("parallel",)),
    )(x, w)


def rms_bwd(x, w, dh, dres, *, name):
    t, d = x.shape

    def body(x_ref, w_ref, dh_ref, dr_ref, dx_ref, dw_ref):
        _, vjp = jax.vjp(_rms, x_ref[...], w_ref[...])
        dx, dw = vjp(dh_ref[...].astype(F32))
        dx_ref[...] = dx + dr_ref[...]
        _acc(dw_ref, dw, pl.program_id(0) == 0)

    row = pl.BlockSpec((ROWS, d), lambda i: (i, 0))
    vec = pl.BlockSpec((1, d), lambda i: (0, 0))
    return pl.pallas_call(
        body, name=name, grid=(t // ROWS,), in_specs=[row, vec, row, row], out_specs=[row, vec],
        out_shape=[jax.ShapeDtypeStruct((t, d), F32), jax.ShapeDtypeStruct((1, d), F32)],
        compiler_params=_cparams(("arbitrary",)),
    )(x, w, dh, dres)


def _sqrelu(x):
    return jnp.square(jnp.maximum(x, 0.0))


def loss_fwd(y, target, *, name):
    t, d = y.shape

    def body(y_ref, t_ref, dy_ref, l_ref):
        e = y_ref[...] - t_ref[...]
        dy_ref[...] = e * (1.0 / d)
        part = 0.5 * jnp.sum(jnp.sum(e * e, axis=-1, keepdims=True) * (1.0 / d), axis=0, keepdims=True)
        _acc(l_ref, jnp.broadcast_to(part, (1, HEAD_DIM)), pl.program_id(0) == 0)

    blk = pl.BlockSpec((ROWS, d), lambda i: (i, 0))
    return pl.pallas_call(
        body, name=name, grid=(t // ROWS,), in_specs=[blk, blk],
        out_specs=[blk, pl.BlockSpec((1, HEAD_DIM), lambda i: (0, 0))],
        out_shape=[jax.ShapeDtypeStruct((t, d), F32), jax.ShapeDtypeStruct((1, HEAD_DIM), F32)],
        compiler_params=_cparams(("arbitrary",)),
    )(y, target)


def _mem_kv(mem, wn, wkn, *ws):
    mn = _rms(mem, wn)
    outs = []
    for h in range(MEM_HEADS):
        outs.append(_rms(mmul(mn, ws[h], 1, 0, False), wkn))
    for h in range(MEM_HEADS):
        outs.append(mmul(mn, ws[MEM_HEADS + h], 1, 0, False))
    return tuple(outs)


def _w_cols(w_ref):
    return [w_ref[:, h * HEAD_DIM:(h + 1) * HEAD_DIM] for h in range(2 * MEM_HEADS)]


def mem_fwd(mem, wn, wkv, wkn):
    def body(mem_ref, wn_ref, w_ref, wkn_ref, k_ref, v_ref):
        outs = _mem_kv(mem_ref[...], wn_ref[...], wkn_ref[...], *_w_cols(w_ref))
        for h in range(MEM_HEADS):
            k_ref[:, h * HEAD_DIM:(h + 1) * HEAD_DIM] = outs[h]
            v_ref[:, h * HEAD_DIM:(h + 1) * HEAD_DIM] = outs[MEM_HEADS + h]

    shp = jax.ShapeDtypeStruct((mem.shape[0], MEM_WIDTH), F32)
    return pl.pallas_call(body, name="mem_fwd", out_shape=[shp, shp], compiler_params=_cparams())(mem, wn, wkv, wkn)


def mem_bwd(mem, wn, wkv, wkn, dk0, dv0, dk1, dv1):
    def body(mem_ref, wn_ref, w_ref, wkn_ref, dk0_ref, dv0_ref, dk1_ref, dv1_ref, dwn_ref, dw_ref, dwkn_ref):
        _, vjp = jax.vjp(lambda wn_, wkn_, *ws: _mem_kv(mem_ref[...], wn_, wkn_, *ws),
                         wn_ref[...], wkn_ref[...], *[w.astype(F32) for w in _w_cols(w_ref)])
        cols = lambda a, b: tuple(a[:, h * HEAD_DIM:(h + 1) * HEAD_DIM] + b[:, h * HEAD_DIM:(h + 1) * HEAD_DIM]
                                  for h in range(MEM_HEADS))
        cts = cols(dk0_ref, dk1_ref) + cols(dv0_ref, dv1_ref)
        grads = vjp(cts)
        dwn_ref[...] = grads[0]
        dwkn_ref[...] = grads[1]
        for h in range(2 * MEM_HEADS):
            dw_ref[:, h * HEAD_DIM:(h + 1) * HEAD_DIM] = grads[2 + h]

    return pl.pallas_call(
        body, name="mem_bwd",
        out_shape=[jax.ShapeDtypeStruct((1, D_MODEL), F32), jax.ShapeDtypeStruct((D_MODEL, 2 * MEM_WIDTH), F32),
                   jax.ShapeDtypeStruct((1, HEAD_DIM), F32)],
        compiler_params=_cparams(),
    )(mem, wn, wkv, wkn, dk0, dv0, dk1, dv1)


def _memattn(q, wq, mk, mv):
    qn = _rms(q, wq) * QSCALE
    s = mmul(qn, mk, 1, 1, False)
    s = s - jnp.max(s, axis=-1, keepdims=True)
    p = jnp.exp(s)
    p = p / jnp.sum(p, axis=-1, keepdims=True)
    return mmul(p, mv, 1, 0, False)


def _lanes(j):
    return slice(j * HEAD_DIM, (j + 1) * HEAD_DIM)


def _memattn_specs(t):
    qspec = pl.BlockSpec((ROWS, MEM_WIDTH), lambda i: (i, (TAIL - MEM_WIDTH) // MEM_WIDTH))
    wspec = pl.BlockSpec((1, HEAD_DIM), lambda i: (0, 0))
    mspec = pl.BlockSpec((N_MEM, MEM_WIDTH), lambda i: (0, 0))
    ospec = pl.BlockSpec((ROWS, MEM_WIDTH), lambda i: (i, 0))
    return qspec, wspec, mspec, ospec


def memattn_fwd(proj, wq, mk, mv, *, name):
    t = proj.shape[0]
    qspec, wspec, mspec, ospec = _memattn_specs(t)

    def body(q_ref, w_ref, k_ref, v_ref, o_ref):
        for h in range(MEM_HEADS):
            o_ref[:, _lanes(h)] = _memattn(q_ref[:, _lanes(h)], w_ref[...], k_ref[:, _lanes(h)],
                                           v_ref[:, _lanes(h)]).astype(o_ref.dtype)

    return pl.pallas_call(
        body, name=name, grid=(t // ROWS,), in_specs=[qspec, wspec, mspec, mspec], out_specs=ospec,
        out_shape=jax.ShapeDtypeStruct((t, MEM_WIDTH), MM), compiler_params=_cparams(("parallel",)),
    )(proj, wq, mk, mv)


def memattn_bwd(proj, wq, mk, mv, dcat, *, name):
    t = proj.shape[0]
    qspec, wspec, mspec, ospec = _memattn_specs(t)
    dospec = pl.BlockSpec((ROWS, MEM_WIDTH), lambda i: (i, D_MODEL // MEM_WIDTH))

    def body(q_ref, w_ref, k_ref, v_ref, do_ref, dq_ref, dw_ref, dk_ref, dv_ref):
        first = pl.program_id(0) == 0
        dw_sum = jnp.zeros((1, HEAD_DIM), F32)
        for h in range(MEM_HEADS):
            _, vjp = jax.vjp(_memattn, q_ref[:, _lanes(h)], w_ref[...], k_ref[:, _lanes(h)], v_ref[:, _lanes(h)])
            dq, dw, dk, dv = vjp(do_ref[:, _lanes(h)].astype(F32))
            dq_ref[:, _lanes(h)] = dq.astype(dq_ref.dtype)
            dw_sum = dw_sum + dw
            _acc(dk_ref.at[:, _lanes(h)], dk, first)
            _acc(dv_ref.at[:, _lanes(h)], dv, first)
        _acc(dw_ref, dw_sum, first)

    mshape = jax.ShapeDtypeStruct((N_MEM, MEM_WIDTH), F32)
    return pl.pallas_call(
        body, name=name, grid=(t // ROWS,), in_specs=[qspec, wspec, mspec, mspec, dospec],
        out_specs=[ospec, wspec, mspec, mspec],
        out_shape=[jax.ShapeDtypeStruct((t, MEM_WIDTH), MM), jax.ShapeDtypeStruct((1, HEAD_DIM), F32), mshape, mshape],
        compiler_params=_cparams(("arbitrary",)),
    )(proj, wq, mk, mv, dcat)


def _shift_rows(x, s, up):
    n = x.shape[0]
    r = lax.broadcasted_iota(jnp.int32, x.shape, 0)
    if up:
        return jnp.where(r < n - s, pltpu.roll(x, n - s, 0), 0.0)
    return jnp.where(r >= s, pltpu.roll(x, s, 0), 0.0)


def _conv_fwd_vals(x, w):
    xb = _bf_round(x)
    wb = _bf_round(w)
    c = xb * wb[3:4, :]
    for j in range(3):
        c = c + _shift_rows(xb, 3 - j, False) * wb[j:j + 1, :]
    return xb, wb, c


def dn_prep_fwd(proj, conv_w):
    t = proj.shape[0]

    def body(x_ref, w_ref, o_ref):
        j = pl.program_id(0)
        _, _, c = _conv_fwd_vals(x_ref[...], w_ref[...])
        s = _silu(c)
        r = lax.rsqrt(jnp.sum(s * s, axis=-1, keepdims=True) + EPS)
        scale = jnp.where(j < N_HEADS, QSCALE, 1.0)
        o_ref[...] = jnp.where(j < 2 * N_HEADS, s * r * scale, s)

    return pl.pallas_call(
        body, name="dn_prep_fwd", grid=(3 * N_HEADS,),
        in_specs=[pl.BlockSpec((t, HEAD_DIM), lambda j: (0, j)), pl.BlockSpec((4, HEAD_DIM), lambda j: (0, j))],
        out_specs=pl.BlockSpec((None, t, HEAD_DIM), lambda j: (j // N_HEADS, 0, j % N_HEADS)),
        out_shape=jax.ShapeDtypeStruct((3, t, D_MODEL), F32), compiler_params=_cparams(("parallel",)),
    )(proj, conv_w)


def dn_prep_bwd(proj, conv_w, dqkv):
    t = proj.shape[0]

    def body(x_ref, w_ref, g_ref, dx_ref, dw_ref):
        j = pl.program_id(0)
        xb, wb, c = _conv_fwd_vals(x_ref[...], w_ref[...])
        sg = _sigmoid(c)
        s = c * sg
        g = g_ref[...]
        r = lax.rsqrt(jnp.sum(s * s, axis=-1, keepdims=True) + EPS)
        scale = jnp.where(j < N_HEADS, QSCALE, 1.0)
        gn = g * scale
        ds_norm = r * gn - s * (r * r * r) * jnp.sum(gn * s, axis=-1, keepdims=True)
        ds = jnp.where(j < 2 * N_HEADS, ds_norm, g)
        dc = ds * (sg + s * (1.0 - sg))
        dx = dc * wb[3:4, :]
        rows = [jnp.sum(dc * xb, axis=0, keepdims=True)]
        for jj in range(2, -1, -1):
            sh = 3 - jj
            dx = dx + _shift_rows(dc, sh, True) * wb[jj:jj + 1, :]
            rows.insert(0, jnp.sum(dc * _shift_rows(xb, sh, False), axis=0, keepdims=True))
        dx_ref[...] = dx.astype(dx_ref.dtype)
        dw_ref[...] = jnp.concatenate(rows + [jnp.zeros((4, HEAD_DIM), F32)], axis=0)

    col = pl.BlockSpec((t, HEAD_DIM), lambda j: (0, j))
    return pl.pallas_call(
        body, name="dn_prep_bwd", grid=(3 * N_HEADS,),
        in_specs=[col, pl.BlockSpec((4, HEAD_DIM), lambda j: (0, j)),
                  pl.BlockSpec((None, t, HEAD_DIM), lambda j: (j // N_HEADS, 0, j % N_HEADS))],
        out_specs=[col, pl.BlockSpec((8, HEAD_DIM), lambda j: (0, j))],
        out_shape=[jax.ShapeDtypeStruct((t, 3 * D_MODEL), MM), jax.ShapeDtypeStruct((8, 3 * D_MODEL), F32)],
        compiler_params=_cparams(("parallel",)),
    )(proj, conv_w, dqkv)


def _tri_ones(n, upper):
    r, c = _iota2(n, n)
    return (r <= c).astype(F32) if upper else (r >= c).astype(F32)


def dn_gates_fwd(proj, a_log, dt_bias):
    t = proj.shape[0]

    def body(x_ref, al_ref, dt_ref, o_ref):
        lane = lax.broadcasted_iota(jnp.int32, (CHUNK, HEAD_DIM), 1)
        tri = _tri_ones(CHUNK, False)

        def step(c, carry):
            rows = pl.ds(pl.multiple_of(c * CHUNK, CHUNK), CHUNK)
            x = x_ref[rows, :]
            g = jnp.where(lane < N_HEADS, -jnp.exp(al_ref[...]) * _softplus(x + dt_ref[...]), 0.0)
            gc = _dot(tri, g, 1, 0, True)
            o_ref[rows, :] = jnp.where(lane < N_HEADS, gc, jnp.where(lane < 2 * N_HEADS, _sigmoid(x), 0.0))
            return carry

        lax.fori_loop(0, t // CHUNK, step, 0)

    vec = pl.BlockSpec((1, HEAD_DIM), lambda i: (0, 0))
    return pl.pallas_call(
        body, name="dn_gates_fwd", grid=(1,),
        in_specs=[pl.BlockSpec((t, HEAD_DIM), lambda i: (0, TAIL_BLK)), vec, vec],
        out_specs=pl.BlockSpec((t, HEAD_DIM), lambda i: (0, 0)),
        out_shape=jax.ShapeDtypeStruct((t, HEAD_DIM), F32), compiler_params=_cparams(("arbitrary",)),
    )(proj, a_log, dt_bias)


def dn_gates_bwd(proj, a_log, dt_bias, dgates):
    t = proj.shape[0]

    def body(x_ref, al_ref, dt_ref, g_ref, dx_ref, dal_ref, ddt_ref):
        lane = lax.broadcasted_iota(jnp.int32, (CHUNK, HEAD_DIM), 1)
        tri = _tri_ones(CHUNK, True)
        dal_ref[...] = jnp.zeros_like(dal_ref)
        ddt_ref[...] = jnp.zeros_like(ddt_ref)

        def step(c, carry):
            rows = pl.ds(pl.multiple_of(c * CHUNK, CHUNK), CHUNK)
            x = x_ref[rows, :]
            dgc = jnp.where(lane < N_HEADS, g_ref[rows, :], 0.0)
            dg = _dot(tri, dgc, 1, 0, True)
            ea = -jnp.exp(al_ref[...])
            z = x + dt_ref[...]
            da = jnp.where(lane < N_HEADS, dg * ea * _sigmoid(z), 0.0)
            gval = jnp.where(lane < N_HEADS, ea * _softplus(z), 0.0)
            beta = _sigmoid(x)
            db = jnp.where(jnp.logical_and(lane >= N_HEADS, lane < 2 * N_HEADS), g_ref[rows, :] * beta * (1.0 - beta), 0.0)
            dx_ref[rows, :] = (da + db).astype(dx_ref.dtype)
            dal_ref[...] += jnp.sum(dg * gval, axis=0, keepdims=True)
            ddt_ref[...] += jnp.sum(da, axis=0, keepdims=True)
            return carry

        lax.fori_loop(0, t // CHUNK, step, 0)

    vec = pl.BlockSpec((1, HEAD_DIM), lambda i: (0, 0))
    full = pl.BlockSpec((t, HEAD_DIM), lambda i: (0, 0))
    return pl.pallas_call(
        body, name="dn_gates_bwd", grid=(1,),
        in_specs=[pl.BlockSpec((t, HEAD_DIM), lambda i: (0, TAIL_BLK)), vec, vec, full],
        out_specs=[full, vec, vec],
        out_shape=[jax.ShapeDtypeStruct((t, HEAD_DIM), MM), jax.ShapeDtypeStruct((1, HEAD_DIM), F32),
                   jax.ShapeDtypeStruct((1, HEAD_DIM), F32)],
        compiler_params=_cparams(("arbitrary",)),
    )(proj, a_log, dt_bias, dgates)


def _dn_intra(q, k, v, gcol, grow, bcol):
    r, c = _iota2(CHUNK, CHUNK)
    causal, strict = r >= c, r > c
    decay = jnp.where(causal, jnp.exp(jnp.where(causal, gcol - grow, 0.0)), 0.0)
    kb = k * bcol
    a = jnp.where(strict, mmul(kb, k, 1, 1, False) * decay, 0.0)
    tm = tri_inv(a)
    u = mmul(tm, v * bcol, 1, 0, False)
    w = mmul(tm, kb * jnp.exp(gcol), 1, 0, False)
    qk = jnp.where(causal, mmul(q, k, 1, 1, False) * decay, 0.0)
    rr = lax.broadcasted_iota(jnp.int32, (CHUNK, 1), 0)
    g_last = jnp.sum(jnp.where(rr == CHUNK - 1, gcol, 0.0), axis=0, keepdims=True)
    return u, w, q * jnp.exp(gcol), k * jnp.exp(g_last - gcol), qk, jnp.exp(g_last)


def _dn_scan(u, w, qg, kd, qk, eg, state):
    v_new = u - mmul(w, state, 1, 0, False)
    out = mmul(qg, state, 1, 0, False) + mmul(qk, v_new, 1, 0, False)
    return out, state * eg + mmul(kd, v_new, 0, 0, False)


DN_HEADS_PER_STEP = 1
DN_GROUP = 8
DN_PARTS = ((CHUNK, HEAD_DIM),) * 4 + ((CHUNK, CHUNK), (1, 1))


def _dn_scratch(hb, nc):
    return [pltpu.VMEM((hb, nc) + shape, F32) for shape in DN_PARTS]


def _dn_part_specs(hb, nc):
    return [pl.BlockSpec((hb, nc) + shape, lambda h: (h, 0, 0, 0)) for shape in DN_PARTS]


def _dn_group(nc):
    return min(DN_GROUP, nc)


def _dn_group_args(refs, j, g, grp):
    q_ref, k_ref, v_ref, gc_ref, gr_ref, bc_ref = refs
    rows = pl.ds(pl.multiple_of(g * (grp * CHUNK), grp * CHUNK), grp * CHUNK)
    cs = pl.ds(g * grp, grp)
    split = lambda ref: ref[rows, _lanes(j)].reshape(grp, CHUNK, HEAD_DIM)
    return split(q_ref), split(k_ref), split(v_ref), gc_ref[j, cs], gr_ref[j, cs], bc_ref[j, cs]


def _dn_intra_all(refs, parts, hb, nc):
    grp = _dn_group(nc)

    def group(g, carry):
        cs = pl.ds(g * grp, grp)
        for j in range(hb):
            for part, val in zip(parts, jax.vmap(_dn_intra)(*_dn_group_args(refs, j, g, grp))):
                part[j, cs] = val
        return carry

    lax.fori_loop(0, nc // grp, group, 0)


def _dn_specs(t):
    nc, hb = t // CHUNK, DN_HEADS_PER_STEP
    head = lambda which: pl.BlockSpec((None, t, hb * HEAD_DIM), lambda h: (which, 0, h))
    flat = pl.BlockSpec((t, hb * HEAD_DIM), lambda h: (0, h))
    col = pl.BlockSpec((hb, nc, CHUNK, 1), lambda h: (h, 0, 0, 0))
    row = pl.BlockSpec((hb, nc, 1, CHUNK), lambda h: (h, 0, 0, 0))
    st = pl.BlockSpec((hb, nc, HEAD_DIM, HEAD_DIM), lambda h: (h, 0, 0, 0))
    return nc, hb, head, flat, col, row, st


def dn_core_fwd(qkv, gcol, grow, bcol):
    t = qkv.shape[1]
    nc, hb, head, flat, col, row, st = _dn_specs(t)

    def body(q_ref, k_ref, v_ref, gc_ref, gr_ref, bc_ref, o_ref, s_ref, *parts):
        _dn_intra_all((q_ref, k_ref, v_ref, gc_ref, gr_ref, bc_ref), parts, hb, nc)

        def step(c, states):
            rows = pl.ds(pl.multiple_of(c * CHUNK, CHUNK), CHUNK)
            new_states = []
            for j in range(hb):
                s_ref[j, c] = states[j]
                out, new_state = _dn_scan(*[part[j, c] for part in parts], states[j])
                o_ref[rows, _lanes(j)] = out
                new_states.append(new_state)
            return tuple(new_states)

        lax.fori_loop(0, nc, step, tuple(jnp.zeros((HEAD_DIM, HEAD_DIM), F32) for _ in range(hb)))

    outs = pl.pallas_call(
        body, name="dn_core_fwd", grid=(N_HEADS // hb,),
        in_specs=[head(0), head(1), head(2), col, row, col], out_specs=[flat, st] + _dn_part_specs(hb, nc),
        out_shape=[jax.ShapeDtypeStruct((t, D_MODEL), F32), jax.ShapeDtypeStruct((N_HEADS, nc, HEAD_DIM, HEAD_DIM), F32)]
        + [jax.ShapeDtypeStruct((N_HEADS, nc) + shape, F32) for shape in DN_PARTS],
        compiler_params=_cparams(("parallel",)),
    )(qkv, qkv, qkv, gcol, grow, bcol)
    return outs[0], outs[1], tuple(outs[2:])


def dn_core_bwd(qkv, gcol, grow, bcol, states, parts, do):
    t = qkv.shape[1]
    nc, hb, head, flat, col, row, st = _dn_specs(t)
    n_parts = len(DN_PARTS)

    def body(q_ref, k_ref, v_ref, gc_ref, gr_ref, bc_ref, s_ref, do_ref, *rest):
        parts, (dqkv_ref, dgc_ref, dgr_ref, dbc_ref), dparts = rest[:n_parts], rest[n_parts:n_parts + 4], rest[n_parts + 4:]
        refs = (q_ref, k_ref, v_ref, gc_ref, gr_ref, bc_ref)

        def step(i, dstates):
            c = nc - 1 - i
            rows = pl.ds(pl.multiple_of(c * CHUNK, CHUNK), CHUNK)
            dstates_in = []
            for j in range(hb):
                _, vjp = jax.vjp(_dn_scan, *[part[j, c] for part in parts], s_ref[j, c])
                *dvals, dstate_in = vjp((do_ref[rows, _lanes(j)], dstates[j]))
                for dpart, dval in zip(dparts, dvals):
                    dpart[j, c] = dval
                dstates_in.append(dstate_in)
            return tuple(dstates_in)

        lax.fori_loop(0, nc, step, tuple(jnp.zeros((HEAD_DIM, HEAD_DIM), F32) for _ in range(hb)))

        grp = _dn_group(nc)

        def group(g, carry):
            rows = pl.ds(pl.multiple_of(g * (grp * CHUNK), grp * CHUNK), grp * CHUNK)
            cs = pl.ds(g * grp, grp)
            for j in range(hb):
                _, vjp = jax.vjp(jax.vmap(_dn_intra), *_dn_group_args(refs, j, g, grp))
                dq, dk, dv, dgc, dgr, dbc = vjp(tuple(dpart[j, cs] for dpart in dparts))
                for which, val in enumerate((dq, dk, dv)):
                    dqkv_ref[which, rows, _lanes(j)] = val.reshape(grp * CHUNK, HEAD_DIM)
                dgc_ref[j, cs] = dgc
                dgr_ref[j, cs] = dgr
                dbc_ref[j, cs] = dbc
            return carry

        lax.fori_loop(0, nc // grp, group, 0)

    return pl.pallas_call(
        body, name="dn_core_bwd", grid=(N_HEADS // hb,), scratch_shapes=_dn_scratch(hb, nc),
        in_specs=[head(0), head(1), head(2), col, row, col, st, flat] + _dn_part_specs(hb, nc),
        out_specs=[pl.BlockSpec((3, t, hb * HEAD_DIM), lambda h: (0, 0, h)), col, row, col],
        out_shape=[jax.ShapeDtypeStruct((3, t, D_MODEL), F32)] + [
            jax.ShapeDtypeStruct((N_HEADS, nc, CHUNK, 1), F32), jax.ShapeDtypeStruct((N_HEADS, nc, 1, CHUNK), F32),
            jax.ShapeDtypeStruct((N_HEADS, nc, CHUNK, 1), F32)],
        compiler_params=_cparams(("parallel",)),
    )(qkv, qkv, qkv, gcol, grow, bcol, states, do, *parts)


def gates_to_heads(gates):
    t = gates.shape[0]
    nc = t // CHUNK
    g = gates[:, :N_HEADS].T.reshape(N_HEADS, nc, CHUNK)
    b = gates[:, N_HEADS:2 * N_HEADS].T.reshape(N_HEADS, nc, CHUNK)
    return g[..., None], g[:, :, None, :], b[..., None]


def heads_to_gates(dgcol, dgrow, dbcol):
    nh, nc = dgcol.shape[:2]
    dg = (dgcol[..., 0] + dgrow[:, :, 0, :]).reshape(nh, nc * CHUNK).T
    db = dbcol[..., 0].reshape(nh, nc * CHUNK).T
    return jnp.concatenate([dg, db, jnp.zeros((nc * CHUNK, HEAD_DIM - 2 * nh), F32)], axis=1)


def _dn_out(o, z, w):
    return _rms(o, w) * _silu(z)


def _gate_specs():
    o_spec = pl.BlockSpec((ROWS, D_MODEL), lambda i: (i, 0))
    z_spec = pl.BlockSpec((ROWS, D_MODEL), lambda i: (i, 3))
    w_spec = pl.BlockSpec((1, HEAD_DIM), lambda i: (0, 0))
    return o_spec, z_spec, w_spec


def dn_out_fwd(o, proj, w):
    t = o.shape[0]
    o_spec, z_spec, w_spec = _gate_specs()

    def body(o_ref, z_ref, w_ref, y_ref):
        for h in range(N_HEADS):
            y_ref[:, _lanes(h)] = _dn_out(o_ref[:, _lanes(h)], z_ref[:, _lanes(h)], w_ref[...]).astype(y_ref.dtype)

    return pl.pallas_call(
        body, name="dn_out_fwd", grid=(t // ROWS,), in_specs=[o_spec, z_spec, w_spec], out_specs=o_spec,
        out_shape=jax.ShapeDtypeStruct((t, D_MODEL), MM), compiler_params=_cparams(("parallel",)),
    )(o, proj, w)


def dn_out_bwd(o, proj, w, dcat):
    t = o.shape[0]
    o_spec, z_spec, w_spec = _gate_specs()

    def body(o_ref, z_ref, w_ref, g_ref, do_ref, dz_ref, dw_ref):
        dw_sum = jnp.zeros((1, HEAD_DIM), F32)
        for h in range(N_HEADS):
            _, vjp = jax.vjp(_dn_out, o_ref[:, _lanes(h)], z_ref[:, _lanes(h)], w_ref[...])
            do, dz, dw = vjp(g_ref[:, _lanes(h)].astype(F32))
            do_ref[:, _lanes(h)] = do
            dz_ref[:, _lanes(h)] = dz.astype(dz_ref.dtype)
            dw_sum = dw_sum + dw
        _acc(dw_ref, dw_sum, pl.program_id(0) == 0)

    return pl.pallas_call(
        body, name="dn_out_bwd", grid=(t // ROWS,), in_specs=[o_spec, z_spec, w_spec, o_spec],
        out_specs=[o_spec, o_spec, w_spec],
        out_shape=[jax.ShapeDtypeStruct((t, D_MODEL), F32), jax.ShapeDtypeStruct((t, D_MODEL), MM),
                   jax.ShapeDtypeStruct((1, HEAD_DIM), F32)],
        compiler_params=_cparams(("arbitrary",)),
    )(o, proj, w, dcat)


def _fox_norm(x, w, scale):
    return _rms(x, w) * scale


def _fox_prep_specs():
    x_spec = pl.BlockSpec((ROWS, 2 * D_MODEL), lambda i: (i, 0))
    w_spec = pl.BlockSpec((2, 1, HEAD_DIM), lambda i: (0, 0, 0))
    y_spec = pl.BlockSpec((2, ROWS, D_MODEL), lambda i: (0, i, 0))
    return x_spec, w_spec, y_spec


def fox_prep_fwd(proj, wqk):
    t = proj.shape[0]
    x_spec, w_spec, y_spec = _fox_prep_specs()

    def body(x_ref, w_ref, y_ref):
        for j in range(2 * N_HEADS):
            which, scale = j // N_HEADS, (QSCALE if j < N_HEADS else 1.0)
            y_ref[which, :, _lanes(j % N_HEADS)] = _fox_norm(x_ref[:, _lanes(j)], w_ref[which], scale).astype(y_ref.dtype)

    return pl.pallas_call(
        body, name="fox_prep_fwd", grid=(t // ROWS,), in_specs=[x_spec, w_spec], out_specs=y_spec,
        out_shape=jax.ShapeDtypeStruct((2, t, D_MODEL), MM), compiler_params=_cparams(("parallel",)),
    )(proj, wqk)


def fox_prep_bwd(proj, wqk, dq, dk):
    t = proj.shape[0]
    x_spec, w_spec, _ = _fox_prep_specs()
    g_spec = pl.BlockSpec((ROWS, D_MODEL), lambda i: (i, 0))

    def body(x_ref, w_ref, dq_ref, dk_ref, dx_ref, dw_ref):
        dws = [jnp.zeros((1, HEAD_DIM), F32), jnp.zeros((1, HEAD_DIM), F32)]
        for j in range(2 * N_HEADS):
            which, scale = j // N_HEADS, (QSCALE if j < N_HEADS else 1.0)
            g_ref = dq_ref if which == 0 else dk_ref
            _, vjp = jax.vjp(lambda x, w: _fox_norm(x, w, scale), x_ref[:, _lanes(j)], w_ref[which])
            dx, dw = vjp(g_ref[:, _lanes(j % N_HEADS)])
            dx_ref[:, _lanes(j)] = dx.astype(dx_ref.dtype)
            dws[which] = dws[which] + dw
        first = pl.program_id(0) == 0
        _acc(dw_ref.at[0], dws[0], first)
        _acc(dw_ref.at[1], dws[1], first)

    return pl.pallas_call(
        body, name="fox_prep_bwd", grid=(t // ROWS,), in_specs=[x_spec, w_spec, g_spec, g_spec],
        out_specs=[x_spec, w_spec],
        out_shape=[jax.ShapeDtypeStruct((t, 2 * D_MODEL), MM), jax.ShapeDtypeStruct((2, 1, HEAD_DIM), F32)],
        compiler_params=_cparams(("arbitrary",)),
    )(proj, wqk, dq, dk)


def _row_pick(x, i):
    r = lax.broadcasted_iota(jnp.int32, x.shape, 0)
    return jnp.sum(jnp.where(r == i, x, 0.0), axis=0, keepdims=True)


def fox_gates_fwd(proj, f_bias):
    t = proj.shape[0]
    blk = HEAD_DIM

    def body(x_ref, b_ref, o_ref):
        lane = lax.broadcasted_iota(jnp.int32, (blk, HEAD_DIM), 1)
        tri = _tri_ones(blk, False)

        def step(c, carry):
            rows = pl.ds(pl.multiple_of(c * blk, blk), blk)
            lf = jnp.where(lane < N_HEADS, -_softplus(-(x_ref[rows, :] + b_ref[...])), 0.0)
            cum = _dot(tri, lf, 1, 0, True) + carry
            o_ref[rows, :] = cum
            return _row_pick(cum, blk - 1)

        lax.fori_loop(0, t // blk, step, jnp.zeros((1, HEAD_DIM), F32))

    vec = pl.BlockSpec((1, HEAD_DIM), lambda i: (0, 0))
    return pl.pallas_call(
        body, name="fox_gates_fwd", grid=(1,),
        in_specs=[pl.BlockSpec((t, HEAD_DIM), lambda i: (0, TAIL_BLK)), vec],
        out_specs=pl.BlockSpec((t, HEAD_DIM), lambda i: (0, 0)),
        out_shape=jax.ShapeDtypeStruct((t, HEAD_DIM), F32), compiler_params=_cparams(("arbitrary",)),
    )(proj, f_bias)


def fox_gates_bwd(proj, f_bias, dfcum):
    t = proj.shape[0]
    blk = HEAD_DIM
    nb = t // blk

    def body(x_ref, b_ref, g_ref, dx_ref, db_ref):
        lane = lax.broadcasted_iota(jnp.int32, (blk, HEAD_DIM), 1)
        tri = _tri_ones(blk, True)
        db_ref[...] = jnp.zeros_like(db_ref)

        def step(i, carry):
            c = nb - 1 - i
            rows = pl.ds(pl.multiple_of(c * blk, blk), blk)
            g = jnp.where(lane < N_HEADS, g_ref[rows, :], 0.0)
            dlf = _dot(tri, g, 1, 0, True) + carry
            dx = jnp.where(lane < N_HEADS, dlf * _sigmoid(-(x_ref[rows, :] + b_ref[...])), 0.0)
            dx_ref[rows, :] = dx.astype(dx_ref.dtype)
            db_ref[...] += jnp.sum(dx, axis=0, keepdims=True)
            return carry + jnp.sum(g, axis=0, keepdims=True)

        lax.fori_loop(0, nb, step, jnp.zeros((1, HEAD_DIM), F32))

    vec = pl.BlockSpec((1, HEAD_DIM), lambda i: (0, 0))
    full = pl.BlockSpec((t, HEAD_DIM), lambda i: (0, 0))
    return pl.pallas_call(
        body, name="fox_gates_bwd", grid=(1,),
        in_specs=[pl.BlockSpec((t, HEAD_DIM), lambda i: (0, TAIL_BLK)), vec, full], out_specs=[full, vec],
        out_shape=[jax.ShapeDtypeStruct((t, HEAD_DIM), MM), jax.ShapeDtypeStruct((1, HEAD_DIM), F32)],
        compiler_params=_cparams(("arbitrary",)),
    )(proj, f_bias, dfcum)


def fcum_to_heads(fcum):
    f = fcum[:, :N_HEADS].T
    return f[:, :, None], f[:, None, :]


def heads_to_fcum(dfcol, dfrow):
    d = (dfcol[:, :, 0] + dfrow[:, 0, :]).T
    return jnp.concatenate([d, jnp.zeros((d.shape[0], HEAD_DIM - N_HEADS), F32)], axis=1)


def _fox_tq(t):
    return min(t, 256)


def _fox_specs(t):
    tq = _fox_tq(t)
    q_spec = pl.BlockSpec((None, tq, HEAD_DIM), lambda h, i: (0, i, h))
    k_spec = pl.BlockSpec((None, t, HEAD_DIM), lambda h, i: (1, 0, h))
    v_spec = pl.BlockSpec((t, HEAD_DIM), lambda h, i: (0, 2 * N_HEADS + h))
    gate_spec = pl.BlockSpec((tq, HEAD_DIM), lambda h, i: (i, 3 * N_HEADS + h))
    col_spec = pl.BlockSpec((None, tq, 1), lambda h, i: (h, i, 0))
    row_spec = pl.BlockSpec((None, 1, t), lambda h, i: (h, 0, 0))
    blk_spec = pl.BlockSpec((tq, HEAD_DIM), lambda h, i: (i, h))
    head_spec = pl.BlockSpec((t, HEAD_DIM), lambda h, i: (0, h))
    return tq, q_spec, k_spec, v_spec, gate_spec, col_spec, row_spec, blk_spec, head_spec


def _fox_segments(i, tq):
    return ([(0, i * tq, False)] if i else []) + [(i * tq, (i + 1) * tq, True)]


def _fox_scores(q_ref, k_ref, fc_ref, fr_ref, lo, hi, causal):
    s = _dot(q_ref[...], k_ref[lo:hi, :], 1, 1, False) + (fc_ref[...] - fr_ref[:, lo:hi])
    if not causal:
        return s, None
    r, c = _iota2(hi - lo, hi - lo)
    return s, c <= r


def fox_attn_fwd(qk, proj, fcol, frow):
    t = proj.shape[0]
    tq, q_spec, k_spec, v_spec, gate_spec, col_spec, row_spec, blk_spec, _ = _fox_specs(t)

    def body(q_ref, k_ref, v_ref, gate_ref, fc_ref, fr_ref, mix_ref, o_ref, lse_ref):
        def block(i):
            segs = _fox_segments(i, tq)
            scores = [_fox_scores(q_ref, k_ref, fc_ref, fr_ref, *seg) for seg in segs]
            scores = [(s if mask is None else jnp.where(mask, s, -1e30), mask) for s, mask in scores]
            m = functools.reduce(jnp.maximum, [jnp.max(s, axis=-1, keepdims=True) for s, _ in scores])
            l, o = 0.0, 0.0
            for (lo, hi, _), (s, mask) in zip(segs, scores):
                p = jnp.exp(s - m)
                p = p if mask is None else jnp.where(mask, p, 0.0)
                l = l + jnp.sum(p, axis=-1, keepdims=True)
                o = o + _dot(p, v_ref[lo:hi, :], 1, 0, False)
            o = o / l
            o_ref[...] = o
            mix_ref[...] = (o * _sigmoid(gate_ref[...])).astype(mix_ref.dtype)
            lse_ref[...] = m + jnp.log(l)

        for i in range(t // tq):
            pl.when(pl.program_id(1) == i)(functools.partial(block, i))

    return pl.pallas_call(
        body, name="fox_attn_fwd", grid=(N_HEADS, t // tq),
        in_specs=[q_spec, k_spec, v_spec, gate_spec, col_spec, row_spec], out_specs=[blk_spec, blk_spec, col_spec],
        out_shape=[jax.ShapeDtypeStruct((t, D_MODEL), MM), jax.ShapeDtypeStruct((t, D_MODEL), F32),
                   jax.ShapeDtypeStruct((N_HEADS, t, 1), F32)],
        compiler_params=_cparams(("parallel", "parallel")),
    )(qk, qk, proj, proj, fcol, frow)


def fox_attn_bwd(qk, proj, fcol, frow, o, lse, dcat):
    t = proj.shape[0]
    tq, q_spec, k_spec, v_spec, gate_spec, col_spec, row_spec, blk_spec, head_spec = _fox_specs(t)

    def body(q_ref, k_ref, v_ref, gate_ref, fc_ref, fr_ref, o_ref, lse_ref, g_ref,
             dq_ref, dk_ref, dv_ref, dgate_ref, dfc_ref, dfr_ref):
        @pl.when(pl.program_id(1) == 0)
        def _():
            dk_ref[...] = jnp.zeros_like(dk_ref)
            dv_ref[...] = jnp.zeros_like(dv_ref)
            dfr_ref[...] = jnp.zeros_like(dfr_ref)

        def block(i):
            sg = _sigmoid(gate_ref[...])
            g = g_ref[...].astype(F32)
            o_pre = o_ref[...]
            do = g * sg
            dgate_ref[...] = (g * o_pre * sg * (1.0 - sg)).astype(dgate_ref.dtype)
            delta = jnp.sum(do * o_pre, axis=-1, keepdims=True)
            dq, dfc = 0.0, 0.0
            for lo, hi, causal in _fox_segments(i, tq):
                s, mask = _fox_scores(q_ref, k_ref, fc_ref, fr_ref, lo, hi, causal)
                if causal:
                    p = jnp.where(mask, jnp.exp(jnp.where(mask, s, 0.0) - lse_ref[...]), 0.0)
                else:
                    p = jnp.exp(s - lse_ref[...])
                ds = p * (_dot(do, v_ref[lo:hi, :], 1, 1, False) - delta)
                dq = dq + _dot(ds, k_ref[lo:hi, :], 1, 0, False)
                dk_ref[lo:hi, :] += _dot(ds, q_ref[...], 0, 0, False)
                dv_ref[lo:hi, :] += _dot(p, do, 0, 0, False)
                dfc = dfc + jnp.sum(ds, axis=-1, keepdims=True)
                dfr_ref[:, lo:hi] += -jnp.sum(ds, axis=0, keepdims=True)
            dq_ref[...] = dq
            dfc_ref[...] = dfc

        for i in range(t // tq):
            pl.when(pl.program_id(1) == i)(functools.partial(block, i))

    f32 = lambda *s: jax.ShapeDtypeStruct(s, F32)
    return pl.pallas_call(
        body, name="fox_attn_bwd", grid=(N_HEADS, t // tq),
        in_specs=[q_spec, k_spec, v_spec, gate_spec, col_spec, row_spec, blk_spec, col_spec, blk_spec],
        out_specs=[blk_spec, head_spec, head_spec, blk_spec, col_spec, row_spec],
        out_shape=[f32(t, D_MODEL), f32(t, D_MODEL), f32(t, D_MODEL), jax.ShapeDtypeStruct((t, D_MODEL), MM),
                   f32(N_HEADS, t, 1), f32(N_HEADS, 1, t)],
        compiler_params=_cparams(("parallel", "arbitrary")),
    )(qk, qk, proj, proj, fcol, frow, o, lse, dcat)


def adamw(w, g, m, v, *, name):
    r, c = w.shape
    rb = ROWS if r % ROWS == 0 else r

    def body(w_ref, g_ref, m_ref, v_ref, d_ref, nm_ref, nv_ref):
        g_ = g_ref[...]
        m_ = ADAM_B1 * m_ref[...] + (1.0 - ADAM_B1) * g_
        v_ = ADAM_B2 * v_ref[...] + (1.0 - ADAM_B2) * jnp.square(g_)
        m_hat = m_ / (1.0 - ADAM_B1 ** ADAM_STEP)
        v_hat = v_ / (1.0 - ADAM_B2 ** ADAM_STEP)
        d_ref[...] = -ADAM_LR * (m_hat / (jnp.sqrt(v_hat) + ADAM_EPS) + ADAM_WD * w_ref[...])
        nm_ref[...] = m_
        nv_ref[...] = v_

    blk = pl.BlockSpec((rb, c), lambda i: (i, 0))
    shp = jax.ShapeDtypeStruct((r, c), F32)
    return pl.pallas_call(body, name=name, grid=(r // rb,), in_specs=[blk] * 4, out_specs=[blk] * 3,
                          out_shape=[shp] * 3, compiler_params=_cparams(("parallel",)))(w, g, m, v)


def _place():
    x, y, c = lax.axis_index("x"), lax.axis_index("y"), lax.axis_index("c")
    return x, y, c, [(1 - x, y), (x, 1 - y), (1 - x, 1 - y)]


ANY = pl.BlockSpec(memory_space=pl.ANY)


def all_reduce_small(v):
    r, w = v.shape

    def body(v_ref, o_ref, buf, send_sems, recv_sems):
        x, y, c, _ = _place()
        me = 4 * x + 2 * y + c
        flip = lambda a, bit: 1 - a if bit else a
        cps = []
        for k in range(1, N_DEV):
            peer = (flip(x, k & 4), flip(y, k & 2), flip(c, k & 1))
            cp = pltpu.make_async_remote_copy(src_ref=v_ref, dst_ref=buf.at[me], send_sem=send_sems.at[k - 1],
                                              recv_sem=recv_sems.at[k - 1], device_id=peer, device_id_type=MESH)
            cp.start()
            cps.append((cp, 4 * peer[0] + 2 * peer[1] + peer[2]))
        buf[me] = v_ref[...]
        for k, (cp, peer_id) in enumerate(cps):
            pltpu.make_async_remote_copy(src_ref=v_ref, dst_ref=buf.at[peer_id], send_sem=send_sems.at[k],
                                         recv_sem=recv_sems.at[k], device_id=(x, y, c), device_id_type=MESH).wait_recv()
        for cp, _ in cps:
            cp.wait_send()
        acc = buf[0]
        for d in range(1, N_DEV):
            acc = acc + buf[d]
        o_ref[...] = acc

    vm = pl.BlockSpec(memory_space=pltpu.VMEM)
    return pl.pallas_call(
        body, name="all_reduce_small", in_specs=[vm], out_specs=vm, out_shape=jax.ShapeDtypeStruct((r, w), F32),
        scratch_shapes=[pltpu.VMEM((N_DEV, r, w), F32), pltpu.SemaphoreType.DMA((N_DEV - 1,)),
                        pltpu.SemaphoreType.DMA((N_DEV - 1,))],
    )(v)


def _vec8(v):
    return jnp.zeros((1, HEAD_DIM), F32).at[0, :N_HEADS].set(v.reshape(N_HEADS))


def _layer_fwd(i, x_in, wt, sm, mem_k, mem_v, late=None):
    tag = f"l{i}_"
    h = rms_fwd(x_in, sm["norm1_w"][i][None], name=tag + "rms1")
    w_in = wt["dn_w_in"] if i == 0 else wt["fox_w_in"]
    proj = matmul(h, w_in, name=tag + "proj", tm=256, tk=1024)
    sv = dict(x_in=x_in, h=h, proj=proj)
    if i == 0:
        qkv = dn_prep_fwd(proj, wt["conv_w"])
        gates = dn_gates_fwd(proj, _vec8(sm["dn_a_log"]), _vec8(sm["dn_dt_bias"]))
        gcol, grow, bcol = gates_to_heads(gates)
        o, states, parts = dn_core_fwd(qkv, gcol, grow, bcol)
        mix = dn_out_fwd(o, proj, sm["dn_o_norm_w"])
        sv.update(qkv=qkv, gcol=gcol, grow=grow, bcol=bcol, states=states, parts=parts, o=o)
    else:
        wqk = jnp.stack([sm["fox_q_norm_w"], sm["fox_k_norm_w"]])
        qk = fox_prep_fwd(proj, wqk)
        fcum = fox_gates_fwd(proj, _vec8(sm["fox_f_bias"]))
        fcol, frow = fcum_to_heads(fcum)
        mix, o, lse = fox_attn_fwd(qk, proj, fcol, frow)
        sv.update(wqk=wqk, qk=qk, fcol=fcol, frow=frow, o=o, lse=lse)
    mem_out = memattn_fwd(proj, sm["memq_norm_w"][i][None], mem_k, mem_v, name=tag + "memattn_fwd")
    cat = jnp.concatenate([mix, mem_out], axis=1)
    if late is not None:
        wt.update(late(cat))
    x_mid = matmul(cat, wt["w_out"][i], res=x_in, name=tag + "out_proj")
    h2 = rms_fwd(x_mid, sm["norm2_w"][i][None], name=tag + "rms2")
    ff, act = matmul(h2, wt["w_mlp1"][i], b_slots=True, also_sqrelu=True, out_dtype=MM, name=tag + "mlp1")
    x_out = matmul(act, wt["w_mlp2"][i], res=x_mid, name=tag + "mlp2")
    sv.update(cat=cat, x_mid=x_mid, h2=h2, ff=ff, act=act)
    return x_out, sv


def _layer_bwd(i, dx_out, sv, wt, sm, mem_k, mem_v, on_mlp=None, on_core=None):
    tag = f"l{i}_"
    big, small = {}, {}
    dff = matmul(dx_out, wt["w_mlp2"][i], tb=True, times_dsqrelu=sv["ff"], out_dtype=MM, name=tag + "d_ff")
    big["w_mlp2"] = matmul(sv["act"], dx_out, ta=True, name=tag + "d_w_mlp2", tk=2048)
    dh2 = matmul(dff, wt["w_mlp1"][i], tb=True, b_slots=True, name=tag + "d_h2")
    big["w_mlp1"] = matmul(sv["h2"], dff, ta=True, name=tag + "d_w_mlp1", tm=512, tn=D_FF, tk=512)
    dx_mid, small["norm2_w"] = rms_bwd(sv["x_mid"], sm["norm2_w"][i][None], dh2, dx_out, name=tag + "rms2_bwd")
    dcat = matmul(dx_mid, wt["w_out"][i], tb=True, name=tag + "d_cat")
    big["w_out"] = matmul(sv["cat"], dx_mid, ta=True, name=tag + "d_w_out", tk=2048)
    proj = sv["proj"]
    memq_norm_w = sm["memq_norm_w"][i][None]
    if on_mlp is not None:
        memq_norm_w = memq_norm_w + on_mlp(big["w_mlp2"], big["w_mlp1"], big["w_out"])
    dqm, small["memq_norm_w"], dmk, dmv = memattn_bwd(proj, memq_norm_w, mem_k, mem_v, dcat, name=tag + "memattn_bwd")
    t = proj.shape[0]
    pad = jnp.zeros((t, PROJ_W - TAIL - HEAD_DIM), MM)
    if i == 0:
        do, dz, small["dn_o_norm_w"] = dn_out_bwd(sv["o"], proj, sm["dn_o_norm_w"], dcat)
        bcol = sv["bcol"] if on_core is None else sv["bcol"] + on_core(do)
        dqkv, dgc, dgr, dbc = dn_core_bwd(sv["qkv"], sv["gcol"], sv["grow"], bcol, sv["states"], sv["parts"], do)
        dtail, dal, ddt = dn_gates_bwd(proj, _vec8(sm["dn_a_log"]), _vec8(sm["dn_dt_bias"]), heads_to_gates(dgc, dgr, dbc))
        dmain, dconv = dn_prep_bwd(proj, wt["conv_w"], dqkv)
        small["dn_a_log"], small["dn_dt_bias"] = dal[:, :N_HEADS], ddt[:, :N_HEADS]
        big["conv_w"] = dconv[:4]
        dproj = jnp.concatenate([dmain, dz, dqm, dtail, pad], axis=1)
    else:
        dq, dk, dv, dgate, dfc, dfr = fox_attn_bwd(sv["qk"], proj, sv["fcol"], sv["frow"], sv["o"], sv["lse"], dcat)
        dtail, dfb = fox_gates_bwd(proj, _vec8(sm["fox_f_bias"]), heads_to_fcum(dfc, dfr))
        dqk, dwqk = fox_prep_bwd(proj, sv["wqk"], dq, dk)
        small["fox_f_bias"] = dfb[:, :N_HEADS]
        small["fox_q_norm_w"], small["fox_k_norm_w"] = dwqk[0], dwqk[1]
        dproj = jnp.concatenate([dqk, dv.astype(MM), dgate, dqm, dtail, pad], axis=1)
    w_in = wt["dn_w_in"] if i == 0 else wt["fox_w_in"]
    dh = matmul(dproj, w_in, tb=True, name=tag + "d_h", tm=512)
    big["w_in"] = matmul(sv["h"], dproj, ta=True, name=tag + "d_w_in", tm=256)
    dx_in, small["norm1_w"] = rms_bwd(sv["x_in"], sm["norm1_w"][i][None], dh, dx_mid, name=tag + "rms1_bwd")
    return dx_in, big, small, (dmk, dmv)


def local_step(x, mem, target, wt, sm, late=None, on_layer1=None, on_mlp0=None, on_core0=None):
    wt = dict(wt)
    mem_k, mem_v = mem_fwd(mem, sm["mem_norm_w"][None], wt["w_mem_kv"], sm["mem_k_norm_w"][None])
    x0, sv0 = _layer_fwd(0, x, wt, sm, mem_k, mem_v, late)
    x1, sv1 = _layer_fwd(1, x0, wt, sm, mem_k, mem_v)
    dy, loss = loss_fwd(x1, target, name="loss")
    dx1, big1, small1, dm1 = _layer_bwd(1, dy, sv1, wt, sm, mem_k, mem_v)
    if on_layer1 is not None:
        dx1 = dx1 + on_layer1(big1)
    dx0, big0, small0, dm0 = _layer_bwd(0, dx1, sv0, wt, sm, mem_k, mem_v, on_mlp0, on_core0)
    dwn, dwkv, dwkn = mem_bwd(mem, sm["mem_norm_w"][None], wt["w_mem_kv"], sm["mem_k_norm_w"][None], *dm0, *dm1)
    small = dict(mem_norm_w=dwn[0], mem_k_norm_w=dwkn[0],
                 norm1_w=jnp.concatenate([small0["norm1_w"], small1["norm1_w"]]),
                 norm2_w=jnp.concatenate([small0["norm2_w"], small1["norm2_w"]]),
                 memq_norm_w=jnp.concatenate([small0["memq_norm_w"], small1["memq_norm_w"]]),
                 dn_a_log=small0["dn_a_log"], dn_dt_bias=small0["dn_dt_bias"], dn_o_norm_w=small0["dn_o_norm_w"],
                 fox_f_bias=small1["fox_f_bias"], fox_q_norm_w=small1["fox_q_norm_w"], fox_k_norm_w=small1["fox_k_norm_w"])
    big = dict(w_mem_kv=dwkv, dn_w_in=big0["w_in"], fox_w_in=big1["w_in"], conv_w=big0["conv_w"],
               w_out=[big0["w_out"], big1["w_out"]], w_mlp1=[big0["w_mlp1"], big1["w_mlp1"]],
               w_mlp2=[big0["w_mlp2"], big1["w_mlp2"]])
    return loss, dx0, big, small


def w_in_to_kernel(w, n_scalars):
    pad = jnp.zeros((w.shape[0], PROJ_W - TAIL - n_scalars), w.dtype)
    return jnp.concatenate([w[:, :4096], w[:, 4096 + n_scalars:], w[:, 4096:4096 + n_scalars], pad], axis=1)


def w_in_from_kernel(w, n_scalars):
    return jnp.concatenate([w[:, :4096], w[:, TAIL:TAIL + n_scalars], w[:, 4096:TAIL]], axis=1)


BIG_SPECS = dict(w_mem_kv=("rows", 1, 256, 1024), w_out=("rows", 2, 384, 1024), w_mlp2=("rows", 2, 1024, 1024),
                 w_mlp1=("cols", 2, 1024, 1024), dn_w_in=("rows", 1, 1024, 1156), fox_w_in=("rows", 1, 1024, 1154))
BIG_NAMES = tuple(BIG_SPECS)
EARLY_NAMES = ("w_mem_kv", "dn_w_in")
LATE_NAMES = ("w_out", "w_mlp2", "w_mlp1", "fox_w_in")
BIG_SPECS.update({f"{name}_{i}": (BIG_SPECS[name][0], 1) + BIG_SPECS[name][2:]
                  for name in ("w_out", "w_mlp2", "w_mlp1") for i in range(2)})
RS_LAYER1 = ("fox_w_in", "w_out_1", "w_mlp2_1", "w_mlp1_1")
RS_MLP0 = ("w_mlp2_0", "w_mlp1_0", "w_out_0")
RS_LAST = ("dn_w_in", "w_mem_kv")


def _full_shape(name, half=False):
    kind, a, b, c = BIG_SPECS[name]
    b = b // 2 if half else b
    return (a, N_CHIP, b, c) if kind == "rows" else (a, b, N_CHIP * c)


def _ds(start, size, align):
    return pl.ds(start if isinstance(start, int) else pl.multiple_of(start, align), size)


def _half_rows(name, h):
    b = BIG_SPECS[name][2]
    return _ds(h * (b // 2), b // 2, 16)


def _shard_idx(name, h):
    return (slice(None), _half_rows(name, h), slice(None))


def _full_idx(name, j=None, h=None):
    kind, _, _, c = BIG_SPECS[name]
    rows = slice(None) if h is None else _half_rows(name, h)
    if kind == "rows":
        return (slice(None), slice(None) if j is None else j, rows, slice(None))
    return (slice(None), rows, slice(None) if j is None else _ds(j * c, c, 128))


def _slots_shape(name):
    _, a, b, c = BIG_SPECS[name]
    return (a, N_CHIP, b, c)


def _slots_idx(name, j, h):
    return (slice(None), j, _half_rows(name, h), slice(None))


def _row_block(name):
    hs = BIG_SPECS[name][2] // 2
    return hs if hs <= ROWS else ROWS


def _remote(src, dst, send_sem, recv_sem, to):
    return pltpu.make_async_remote_copy(src_ref=src, dst_ref=dst, send_sem=send_sem, recv_sem=recv_sem, device_id=to,
                                        device_id_type=MESH)


HBM = pl.BlockSpec(memory_space=pltpu.HBM)
SEM = pl.BlockSpec(memory_space=pltpu.SEMAPHORE)
EFFECT = pltpu.CompilerParams(has_side_effects=pltpu.SideEffectType.DATAFLOW_SIDE_EFFECTING)


def _in_hbm(a):
    return pltpu.with_memory_space_constraint(a, pltpu.HBM)


def _chip_copies(names, ins, lands, send_sems, recv_sems):
    x, y, c, chips = _place()
    return [_remote(ins[a].at[_shard_idx(name, c)], lands[a].at[_slots_idx(name, 2 * x + y, c)], send_sems.at[3 * a + k],
                    recv_sems.at[3 * a + k], (chip[0], chip[1], c))
            for a, name in enumerate(names) for k, chip in enumerate(chips)]


def _copies_start(call_name, copies, sources, land_shapes, per_source=3, land_dtype=MM):
    n = len(sources)

    def body(*refs):
        ins, lands, send_sems, recv_sems, token = refs[:n], refs[n:2 * n], refs[2 * n], refs[2 * n + 1], refs[-1]
        for cp in copies(ins, lands, send_sems, recv_sems):
            cp.start()
        token[...] = jnp.zeros_like(token)

    ins = [_in_hbm(a) for a in sources]
    lands = [_in_hbm(lax.empty(shape, land_dtype)) for shape in land_shapes]
    sems = (pltpu.SemaphoreType.DMA((per_source * n,)), pltpu.SemaphoreType.DMA((per_source * n,)))
    outs = pl.pallas_call(
        body, name=call_name, in_specs=[HBM] * (2 * n),
        out_specs=(SEM, SEM) + (HBM,) * (2 * n) + (pl.BlockSpec(memory_space=pltpu.VMEM),),
        out_shape=sems + tuple(pltpu.HBM(a.shape, a.dtype) for a in ins + lands) + (jax.ShapeDtypeStruct((8, HEAD_DIM), F32),),
        input_output_aliases={a: 2 + a for a in range(2 * n)}, compiler_params=EFFECT,
    )(*ins, *lands)
    return outs[:-1], outs[-1]


def _copies_wait(call_name, copies, state, after):
    n = (len(state) - 2) // 2

    def body(*refs):
        send_sems, recv_sems, ins, lands = refs[0], refs[1], refs[2:2 + n], refs[2 + n:2 + 2 * n]
        for cp in copies(ins, lands, send_sems, recv_sems):
            cp.wait_send()
            cp.wait_recv()

    outs = pl.pallas_call(
        body, name=call_name, in_specs=[SEM, SEM] + [HBM] * (2 * n) + [ANY], out_specs=(HBM,) * (2 * n),
        out_shape=tuple(pltpu.HBM(a.shape, a.dtype) for a in state[2:]),
        input_output_aliases={2 + a: a for a in range(2 * n)}, compiler_params=EFFECT,
    )(*state, after)
    return outs[:n], outs[n:]


def all_gather_start(shards, names):
    return _copies_start("all_gather_start", functools.partial(_chip_copies, names), [shards[name] for name in names],
                         [_slots_shape(name) for name in names])


def all_gather_wait(state, names, after):
    ins, lands = _copies_wait("all_gather_wait", functools.partial(_chip_copies, names), state, after)
    return dict(zip(names, ins)), dict(zip(names, lands))


def _chip_sends(names, ins, lands, send_sems, recv_sems):
    x, y, c, chips = _place()
    return [_remote(ins[a].at[_full_idx(name, 2 * chip[0] + chip[1])], lands[a].at[k], send_sems.at[3 * a + k],
                    recv_sems.at[3 * a + k], (chip[0], chip[1], c))
            for a, name in enumerate(names) for k, chip in enumerate(chips)]


def _got_shape(name):
    _, a_, b_, c_ = BIG_SPECS[name]
    return (3, a_, b_ // 2, c_)


def rs_chip_start(pairs, names, tag):
    return _copies_start("rs_chip_start_" + tag, functools.partial(_chip_sends, names), [pairs[name] for name in names],
                         [_got_shape(name) for name in names])


def rs_chip_wait(state, names, tag, after):
    _, lands = _copies_wait("rs_chip_wait_" + tag, functools.partial(_chip_sends, names), state, after)
    return dict(zip(names, lands))


def all_gather_pass_on(lands, names):
    n = len(names)

    def body(*refs):
        outs, send_sems, recv_sems = refs[n:2 * n], refs[2 * n], refs[2 * n + 1]
        x, y, c, chips = _place()
        work = [(3 * a + k, a, name, 2 * chip[0] + chip[1]) for a, name in enumerate(names) for k, chip in enumerate(chips)]
        cps = []
        for s, a, name, slot in work:
            landed = outs[a].at[_slots_idx(name, slot, c)]
            cps.append(_remote(landed, landed, send_sems.at[s], recv_sems.at[s], (x, y, 1 - c)))
            cps[-1].start()
        for s, a, name, slot in work:
            passed = outs[a].at[_slots_idx(name, slot, 1 - c)]
            _remote(passed, passed, send_sems.at[s], recv_sems.at[s], (x, y, 1 - c)).wait_recv()
        for cp in cps:
            cp.wait_send()

    outs = pl.pallas_call(
        body, name="all_gather_pass_on", in_specs=[ANY] * n, out_specs=[ANY] * n,
        input_output_aliases={a: a for a in range(n)},
        out_shape=[jax.ShapeDtypeStruct(_slots_shape(name), MM) for name in names],
        scratch_shapes=[pltpu.SemaphoreType.DMA((3 * n,)), pltpu.SemaphoreType.DMA((3 * n,))],
    )(*[lands[name] for name in names])
    return dict(zip(names, outs))


def all_gather_big(shards, names):
    n = len(names)
    BIG_NAMES = names

    def body(*refs):
        ins, outs = refs[:n], refs[n:2 * n]
        send_sems, recv_sems, fsend_sems, frecv_sems = refs[2 * n:]
        x, y, c, chips = _place()
        me_chip, sibling = 2 * x + y, (x, y, 1 - c)
        work = [(3 * a + k, a, name, chip) for a, name in enumerate(BIG_NAMES) for k, chip in enumerate(chips)]
        sends = []
        for s, a, name, chip in work:
            cp = _remote(ins[a].at[_shard_idx(name, c)], outs[a].at[_slots_idx(name, me_chip, c)], send_sems.at[s],
                         recv_sems.at[s], (chip[0], chip[1], c))
            cp.start()
            sends.append(cp)
        for s, a, name, chip in work:
            landed = outs[a].at[_slots_idx(name, 2 * chip[0] + chip[1], c)]
            _remote(landed, landed, send_sems.at[s], recv_sems.at[s], (chip[0], chip[1], c)).wait_recv()
            cp = _remote(landed, landed, fsend_sems.at[s], frecv_sems.at[s], sibling)
            cp.start()
            sends.append(cp)
        for s, a, name, chip in work:
            passed = outs[a].at[_slots_idx(name, 2 * chip[0] + chip[1], 1 - c)]
            _remote(passed, passed, fsend_sems.at[s], frecv_sems.at[s], sibling).wait_recv()
        for cp in sends:
            cp.wait_send()

    outs = pl.pallas_call(
        body, name="all_gather_big", in_specs=[ANY] * n, out_specs=[ANY] * n,
        out_shape=[jax.ShapeDtypeStruct(_slots_shape(name), MM) for name in BIG_NAMES],
        scratch_shapes=[pltpu.SemaphoreType.DMA((3 * n,))] * 4,
    )(*[shards[name] for name in BIG_NAMES])
    return dict(zip(BIG_NAMES, outs))


def with_own_slot(name, full, shard, chip):
    return lax.dynamic_update_slice(full, shard[:, None], (0, chip, 0, 0))


def rs_pair_exchange_big(grads, names, tag):
    n = len(names)

    def body(*refs):
        ins, outs, send_sems, recv_sems = refs[:n], refs[n:2 * n], refs[2 * n], refs[2 * n + 1]
        x, y, c, _ = _place()
        cps = []
        for a, name in enumerate(names):
            cp = _remote(ins[a].at[_full_idx(name, None, 1 - c)], outs[a], send_sems.at[a], recv_sems.at[a], (x, y, 1 - c))
            cp.start()
            cps.append(cp)
        for cp in cps:
            cp.wait()

    outs = pl.pallas_call(
        body, name="rs_pair_exchange_" + tag, in_specs=[ANY] * n, out_specs=[ANY] * n,
        out_shape=[jax.ShapeDtypeStruct(_full_shape(name, half=True), F32) for name in names],
        scratch_shapes=[pltpu.SemaphoreType.DMA((n,)), pltpu.SemaphoreType.DMA((n,))],
    )(*[grads[name] for name in names])
    return dict(zip(names, outs))


def rs_pair_add_big(name, place, g, got):
    kind, a_, b_, c_ = BIG_SPECS[name]
    rb = _row_block(name)
    nb = (b_ // 2) // rb

    def body(place_ref, g_ref, got_ref, o_ref):
        o_ref[...] = (g_ref[...] + got_ref[...]).astype(o_ref.dtype)

    if kind == "rows":
        g_spec = pl.BlockSpec((None, None, rb, c_), lambda a, j, i, p: (a, j, p[0] * nb + i, 0))
        o_spec = pl.BlockSpec((None, None, rb, c_), lambda a, j, i, p: (a, j, i, 0))
    else:
        g_spec = pl.BlockSpec((None, rb, c_), lambda a, j, i, p: (a, p[0] * nb + i, j))
        o_spec = pl.BlockSpec((None, rb, c_), lambda a, j, i, p: (a, i, j))
    return pl.pallas_call(
        body, name="rs_pair_add_" + name,
        grid_spec=pltpu.PrefetchScalarGridSpec(num_scalar_prefetch=1, grid=(a_, N_CHIP, nb), in_specs=[g_spec, o_spec],
                                               out_specs=o_spec),
        out_shape=jax.ShapeDtypeStruct(_full_shape(name, half=True), MM),
        compiler_params=_cparams(("parallel", "parallel", "parallel")),
    )(place, g, got)


def rs_chip_exchange_big(pairs, names, tag):
    n = len(names)

    def body(*refs):
        ins, outs, send_sems, recv_sems = refs[:n], refs[n:2 * n], refs[2 * n], refs[2 * n + 1]
        cps = _chip_sends(names, ins, outs, send_sems, recv_sems)
        for cp in cps:
            cp.start()
        for cp in cps:
            cp.wait()

    outs = pl.pallas_call(
        body, name="rs_chip_exchange_" + tag, in_specs=[ANY] * n, out_specs=[ANY] * n,
        out_shape=[jax.ShapeDtypeStruct(_got_shape(name), MM) for name in names],
        scratch_shapes=[pltpu.SemaphoreType.DMA((3 * n,)), pltpu.SemaphoreType.DMA((3 * n,))],
    )(*[pairs[name] for name in names])
    return dict(zip(names, outs))


def rs_chip_add_big(name, place, g, got_pair, got_chips):
    kind, a_, b_, c_ = BIG_SPECS[name]
    rb = _row_block(name)
    nb = (b_ // 2) // rb

    def body(place_ref, g_ref, s_ref, r0_ref, r1_ref, r2_ref, o_ref):
        own = g_ref[...] + s_ref[...]
        o_ref[...] = ((own + r0_ref[...].astype(F32)) + r1_ref[...].astype(F32)) + r2_ref[...].astype(F32)

    if kind == "rows":
        g_spec = pl.BlockSpec((None, None, rb, c_), lambda a, i, p: (a, p[1], p[0] * nb + i, 0))
        s_spec = pl.BlockSpec((None, None, rb, c_), lambda a, i, p: (a, p[1], i, 0))
    else:
        g_spec = pl.BlockSpec((None, rb, c_), lambda a, i, p: (a, p[0] * nb + i, p[1]))
        s_spec = pl.BlockSpec((None, rb, c_), lambda a, i, p: (a, i, p[1]))
    r_spec = lambda k: pl.BlockSpec((None, None, rb, c_), lambda a, i, p: (k, a, i, 0))
    return pl.pallas_call(
        body, name="rs_chip_add_" + name,
        grid_spec=pltpu.PrefetchScalarGridSpec(
            num_scalar_prefetch=1, grid=(a_, nb), in_specs=[g_spec, s_spec, r_spec(0), r_spec(1), r_spec(2)],
            out_specs=pl.BlockSpec((None, rb, c_), lambda a, i, p: (a, p[0] * nb + i, 0))),
        out_shape=jax.ShapeDtypeStruct((a_, b_, c_), F32), compiler_params=_cparams(("parallel", "parallel")),
    )(place, g, got_pair, got_chips, got_chips, got_chips)


def rs_pair_gather_big(halves):
    names = tuple(halves)
    n = len(names)

    def body(*refs):
        outs, send_sems, recv_sems = refs[n:2 * n], refs[2 * n], refs[2 * n + 1]
        x, y, c, _ = _place()
        cps = []
        for a, name in enumerate(names):
            mine = outs[a].at[_shard_idx(name, c)]
            cp = _remote(mine, mine, send_sems.at[a], recv_sems.at[a], (x, y, 1 - c))
            cp.start()
            cps.append(cp)
        for a, name in enumerate(names):
            cps[a].wait_send()
            theirs = outs[a].at[_shard_idx(name, 1 - c)]
            _remote(theirs, theirs, send_sems.at[a], recv_sems.at[a], (x, y, 1 - c)).wait_recv()

    outs = pl.pallas_call(
        body, name="rs_pair_gather_big", in_specs=[ANY] * n, out_specs=[ANY] * n,
        input_output_aliases={a: a for a in range(n)},
        out_shape=[jax.ShapeDtypeStruct(BIG_SPECS[name][1:], F32) for name in names],
        scratch_shapes=[pltpu.SemaphoreType.DMA((n,)), pltpu.SemaphoreType.DMA((n,))],
    )(*[halves[name] for name in names])
    return dict(zip(names, outs))


def _pair_sends(names, ins, lands, send_sems, recv_sems):
    x, y, c, _ = _place()
    return [_remote(ins[a].at[_full_idx(name, None, 1 - c)], lands[a], send_sems.at[a], recv_sems.at[a], (x, y, 1 - c))
            for a, name in enumerate(names)]


def rs_pair_start(grads, names, tag):
    return _copies_start("rs_pair_start_" + tag, functools.partial(_pair_sends, names), [grads[name] for name in names],
                         [_full_shape(name, half=True) for name in names], per_source=1, land_dtype=F32)


def rs_middle(pair_state, names, tag, place, after):
    ins, lands = _copies_wait("rs_pair_wait_" + tag, functools.partial(_pair_sends, names), pair_state, after)
    grads, got_pair = dict(zip(names, ins)), dict(zip(names, lands))
    pairs = {name: rs_pair_add_big(name, place, grads[name], got_pair[name]) for name in names}
    state, token = rs_chip_start(pairs, names, tag)
    return (grads, got_pair, state), token


def rs_end(begun, names, tag, place, after):
    grads, got_pair, state = begun
    got_chips = rs_chip_wait(state, names, tag, after)
    return {name: rs_chip_add_big(name, place, grads[name], got_pair[name], got_chips[name]) for name in names}


def rs_whole(grads, names, tag, place):
    got_pair = rs_pair_exchange_big(grads, names, tag)
    pairs = {name: rs_pair_add_big(name, place, grads[name], got_pair[name]) for name in names}
    got_chips = rs_chip_exchange_big(pairs, names, tag)
    return {name: rs_chip_add_big(name, place, grads[name], got_pair[name], got_chips[name]) for name in names}


PACK_W = 1024
SMALL =(("mem_norm_w", 1024), ("mem_k_norm_w", 128), ("norm1_w", 2048), ("dn_a_log", 8), ("dn_dt_bias", 8),
         ("dn_o_norm_w", 128), ("fox_f_bias", 8), ("fox_q_norm_w", 128), ("fox_k_norm_w", 128), ("memq_norm_w", 256),
         ("norm2_w", 2048))
SMALL_ROWS = 8
CONV_ROWS = 4 * 3 * D_MODEL // PACK_W
LOSS_AT = sum(n for _, n in SMALL)


def pack_small(parts, extra=None):
    flat = [parts[name].astype(F32).reshape(-1) for name, _ in SMALL]
    used = LOSS_AT
    if extra is not None:
        flat.append(extra.reshape(1))
        used += 1
    flat.append(jnp.zeros((SMALL_ROWS * PACK_W - used,), F32))
    return jnp.concatenate(flat).reshape(SMALL_ROWS, PACK_W)


def unpack_small(packed, shapes):
    flat, out, at = packed.reshape(-1), {}, 0
    for name, n in SMALL:
        out[name] = flat[at:at + n].reshape(shapes[name])
        at += n
    return out


def _adam_all(w, g, m, v, name):
    shape = w.shape
    r2 = lambda a: a.reshape(-1, shape[-1])
    d, nm, nv = adamw(r2(w), r2(g), r2(m), r2(v), name=name)
    return d.reshape(shape), nm.reshape(shape), nv.reshape(shape)


BIG = ("w_mem_kv", "dn_w_in", "dn_conv_w", "fox_w_in", "w_out", "w_mlp1", "w_mlp2")
WEIGHTS = ("mem_norm_w", "w_mem_kv", "mem_k_norm_w", "norm1_w", "dn_w_in", "dn_conv_w", "dn_a_log", "dn_dt_bias",
           "dn_o_norm_w", "fox_w_in", "fox_f_bias", "fox_q_norm_w", "fox_k_norm_w", "memq_norm_w", "w_out", "norm2_w",
           "w_mlp1", "w_mlp2")


def kernel(x, mem, mem_norm_w, w_mem_kv, mem_k_norm_w, norm1_w, dn_w_in, dn_conv_w, dn_a_log, dn_dt_bias, dn_o_norm_w, fox_w_in, fox_f_bias, fox_q_norm_w, fox_k_norm_w, memq_norm_w, w_out, norm2_w, w_mlp1, w_mlp2, loss_target, m_mem_norm_w, m_w_mem_kv, m_mem_k_norm_w, m_norm1_w, m_dn_w_in, m_dn_conv_w, m_dn_a_log, m_dn_dt_bias, m_dn_o_norm_w, m_fox_w_in, m_fox_f_bias, m_fox_q_norm_w, m_fox_k_norm_w, m_memq_norm_w, m_w_out, m_norm2_w, m_w_mlp1, m_w_mlp2, v_mem_norm_w, v_w_mem_kv, v_mem_k_norm_w, v_norm1_w, v_dn_w_in, v_dn_conv_w, v_dn_a_log, v_dn_dt_bias, v_dn_o_norm_w, v_fox_w_in, v_fox_f_bias, v_fox_q_norm_w, v_fox_k_norm_w, v_memq_norm_w, v_w_out, v_norm2_w, v_w_mlp1, v_w_mlp2):
    args = dict(locals())
    w = {n: args[n] for n in WEIGHTS}
    m = {n: args["m_" + n] for n in WEIGHTS}
    v = {n: args["v_" + n] for n in WEIGHTS}
    core, chip = lax.axis_index("c"), 2 * lax.axis_index("x") + lax.axis_index("y")
    place = jnp.stack([core, chip]).astype(jnp.int32)

    shards = {name: w[name].reshape(BIG_SPECS[name][1:]).astype(MM) for name in BIG_NAMES}
    w_in_full = lambda arr, n_scalars: w_in_to_kernel(arr[0].transpose(1, 0, 2).reshape(D_MODEL, -1), n_scalars)
    early = {name: with_own_slot(name, arr, shards[name], chip)
             for name, arr in all_gather_big(shards, EARLY_NAMES).items()}
    conv_mine = jnp.where(core == 0, dn_conv_w[0], 0.0)
    conv_placed = lax.dynamic_update_slice(jnp.zeros((4, 3 * D_MODEL), F32), conv_mine, (0, 768 * chip))
    conv_full = all_reduce_small(jnp.pad(conv_placed.reshape(CONV_ROWS, PACK_W), ((0, 16 - CONV_ROWS), (0, 0))))
    late_shards, early, conv_full = lax.optimization_barrier(
        ({name: shards[name] for name in LATE_NAMES}, early, conv_full))
    late_state, token = all_gather_start(late_shards, LATE_NAMES)
    tie = token[0, 0]
    wt = dict(w_mem_kv=early["w_mem_kv"].reshape(D_MODEL, 2 * MEM_WIDTH) + tie.astype(MM),
              dn_w_in=w_in_full(early["dn_w_in"], 2 * N_HEADS), conv_w=conv_full[:CONV_ROWS].reshape(4, 3 * D_MODEL))

    def late(after):
        late_shards, lands = all_gather_wait(late_state, LATE_NAMES, after)
        full = {name: with_own_slot(name, arr, late_shards[name], chip)
                for name, arr in all_gather_pass_on(lands, LATE_NAMES).items()}
        return dict(fox_w_in=w_in_full(full["fox_w_in"], N_HEADS), w_out=full["w_out"].reshape(2, 3 * MEM_WIDTH, D_MODEL),
                    w_mlp1=full["w_mlp1"], w_mlp2=full["w_mlp2"].reshape(2, D_FF, D_MODEL))

    sm = dict(mem_norm_w=mem_norm_w, mem_k_norm_w=mem_k_norm_w, norm1_w=norm1_w, norm2_w=norm2_w, memq_norm_w=memq_norm_w,
              dn_a_log=dn_a_log[0], dn_dt_bias=dn_dt_bias[0], dn_o_norm_w=dn_o_norm_w, fox_f_bias=fox_f_bias[0],
              fox_q_norm_w=fox_q_norm_w, fox_k_norm_w=fox_k_norm_w)
    w_in_slots = lambda g, n_scalars: w_in_from_kernel(g, n_scalars).reshape(D_MODEL, N_CHIP, -1).transpose(1, 0, 2)[None]
    rows_view = lambda g, name: g.reshape(_full_shape(name))
    pair_started, begun = {}, {}

    def on_layer1(big1):
        grads1 = dict(fox_w_in=w_in_slots(big1["w_in"], N_HEADS), w_out_1=rows_view(big1["w_out"], "w_out_1"),
                      w_mlp2_1=rows_view(big1["w_mlp2"], "w_mlp2_1"), w_mlp1_1=big1["w_mlp1"][None])
        pair_started["layer1"], token = rs_pair_start(grads1, RS_LAYER1, "layer1")
        return token[0, 0]

    def on_mlp0(d_w_mlp2, d_w_mlp1, d_w_out):
        begun["layer1"], token1 = rs_middle(pair_started["layer1"], RS_LAYER1, "layer1", place, d_w_out)
        grads0 = dict(w_mlp2_0=rows_view(d_w_mlp2, "w_mlp2_0"), w_mlp1_0=d_w_mlp1[None],
                      w_out_0=rows_view(d_w_out, "w_out_0"))
        grads0, _ = lax.optimization_barrier((grads0, token1))
        pair_started["mlp0"], token0 = rs_pair_start(grads0, RS_MLP0, "mlp0")
        return token1[0, 0] + token0[0, 0]

    def on_core0(d_o):
        begun["mlp0"], token = rs_middle(pair_started["mlp0"], RS_MLP0, "mlp0", place, d_o)
        return token[0, 0]

    loss_part, dx, big, small = local_step(x[0], mem[0], loss_target[0], wt, sm, late, on_layer1, on_mlp0, on_core0)
    last = dict(dn_w_in=w_in_slots(big["dn_w_in"], 2 * N_HEADS), w_mem_kv=rows_view(big["w_mem_kv"], "w_mem_kv"))
    halves = rs_whole(last, RS_LAST, "last", place)
    halves.update(rs_end(begun["layer1"], RS_LAYER1, "layer1", place, dx))
    halves.update(rs_end(begun["mlp0"], RS_MLP0, "mlp0", place, dx))
    summed = rs_pair_gather_big(halves)
    big_sum = {name: summed[name] for name in ("w_mem_kv", "dn_w_in", "fox_w_in")}
    big_sum.update({name: jnp.concatenate([summed[name + "_0"], summed[name + "_1"]]) for name in ("w_out", "w_mlp2", "w_mlp1")})
    small_pack = jnp.concatenate([pack_small(small, loss_part[0, :1]), big["conv_w"].reshape(CONV_ROWS, PACK_W),
                                  jnp.zeros((24 - SMALL_ROWS - CONV_ROWS, PACK_W), F32)])
    small_all = all_reduce_small(small_pack)
    small_sum = small_all[:SMALL_ROWS]
    conv_sum = lax.dynamic_slice(small_all[SMALL_ROWS:SMALL_ROWS + CONV_ROWS].reshape(4, 3 * D_MODEL), (0, 768 * chip), (4, 768))
    loss = small_sum.reshape(-1)[LOSS_AT]
    grads = unpack_small(small_sum, {n: w[n].shape for n, _ in SMALL})
    grads.update({name: big_sum[name].reshape(w[name].shape) for name in BIG_NAMES}, dn_conv_w=conv_sum[None])

    delta, new_m, new_v = {}, {}, {}
    for n in BIG:
        delta[n], new_m[n], new_v[n] = _adam_all(w[n], grads[n], m[n], v[n], "adamw_" + n)
    shapes = {n: w[n].shape for n, _ in SMALL}
    d_s, m_s, v_s = adamw(pack_small(w), small_sum, pack_small(m), pack_small(v), name="adamw_small")
    for out, packed in ((delta, d_s), (new_m, m_s), (new_v, v_s)):
        out.update(unpack_small(packed, shapes))
    return (loss, dx[None], *[grads[n] for n in WEIGHTS], *[delta[n] for n in WEIGHTS],
            *[new_m[n] for n in WEIGHTS], *[new_v[n] for n in WEIGHTS])
```

```python
import functools

import jax
import jax.numpy as jnp
from jax import lax
from jax.experimental import pallas as pl
from jax.experimental.pallas import tpu as pltpu

F32 = jnp.float32
MM = jnp.bfloat16
HI = lax.Precision.HIGHEST

D_MODEL = 1024
HEAD_DIM = 128
N_HEADS = 8
MEM_HEADS = 4
MEM_WIDTH = MEM_HEADS * HEAD_DIM
N_MEM = 256
D_FF = 4 * D_MODEL
CHUNK = 64
EPS = 1e-6
QSCALE = HEAD_DIM ** -0.5
PROJ_W = 4736
TAIL = 4608
TAIL_BLK = TAIL // HEAD_DIM
ROWS = 256
VMEM_LIMIT = 56 * 1024 * 1024

ADAM_LR = 0.001
ADAM_B1 = 0.9
ADAM_B2 = 0.999
ADAM_EPS = 1e-08
ADAM_WD = 0.01
ADAM_STEP = 10

N_DEV = 8
N_CHIP = 4
MESH = pl.DeviceIdType.MESH


def _cparams(sem=None):
    return pltpu.CompilerParams(dimension_semantics=sem, vmem_limit_bytes=VMEM_LIMIT)


def _dot(a, b, ca, cb, hi):
    dims = (((ca,), (cb,)), ((), ()))
    if hi:
        return lax.dot_general(a, b, dims, precision=HI, preferred_element_type=F32)
    return lax.dot_general(a.astype(MM), b.astype(MM), dims, preferred_element_type=F32)


@functools.partial(jax.custom_vjp, nondiff_argnums=(2, 3, 4))
def mmul(a, b, ca, cb, hi):
    return _dot(a, b, ca, cb, hi)


def _mmul_fwd(a, b, ca, cb, hi):
    return _dot(a, b, ca, cb, hi), (a, b)


def _mmul_bwd(ca, cb, hi, res, g):
    a, b = res
    if ca == 1:
        da = _dot(g, b, 1, 1, hi) if cb == 0 else _dot(g, b, 1, 0, hi)
    else:
        da = _dot(b, g, 1, 1, hi) if cb == 0 else _dot(b, g, 0, 1, hi)
    if cb == 0:
        db = _dot(a, g, 0, 0, hi) if ca == 1 else _dot(a, g, 1, 0, hi)
    else:
        db = _dot(g, a, 0, 0, hi) if ca == 1 else _dot(g, a, 0, 1, hi)
    return da.astype(a.dtype), db.astype(b.dtype)


mmul.defvjp(_mmul_fwd, _mmul_bwd)


def _iota2(n, m):
    return lax.broadcasted_iota(jnp.int32, (n, m), 0), lax.broadcasted_iota(jnp.int32, (n, m), 1)


def _same_block(r, c, shift):
    return lax.shift_right_logical(r, shift) == lax.shift_right_logical(c, shift)


def _split_bf16(x):
    hi = x.astype(jnp.bfloat16)
    return hi, (x - hi.astype(F32)).astype(jnp.bfloat16)


def _dot3(a, b, ca, cb):
    dims = (((ca,), (cb,)), ((), ()))
    (ah, al), (bh, bl) = _split_bf16(a), _split_bf16(b)
    d = lambda x, y: lax.dot_general(x, y, dims, preferred_element_type=F32)
    return d(ah, bh) + (d(ah, bl) + d(al, bh))


def _tri_inv_impl(a):
    n = a.shape[0]
    r, c = _iota2(n, n)
    eye = (r == c).astype(F32)
    b16, b32 = _same_block(r, c, 4), _same_block(r, c, 5)
    a0 = jnp.where(b16, a, 0.0)
    p = eye - a0
    b = _dot3(a0, a0, 1, 0)
    p = p + _dot3(p, b, 1, 0)
    b = _dot3(b, b, 1, 0)
    p = p + _dot3(p, b, 1, 0)
    b = _dot3(b, b, 1, 0)
    p = p + _dot3(p, b, 1, 0)
    a1 = jnp.where(jnp.logical_and(b32, jnp.logical_not(b16)), a, 0.0)
    p = p - _dot3(_dot3(p, a1, 1, 0), p, 1, 0)
    a2 = jnp.where(b32, 0.0, a)
    p = p - _dot3(_dot3(p, a2, 1, 0), p, 1, 0)
    return p


@jax.custom_vjp
def tri_inv(a):
    return _tri_inv_impl(a)


def _tri_inv_fwd(a):
    p = _tri_inv_impl(a)
    return p, p


def _tri_inv_bwd(p, g):
    return (-_dot3(_dot3(p, g, 0, 0), p, 1, 1),)


tri_inv.defvjp(_tri_inv_fwd, _tri_inv_bwd)


def _sigmoid(x):
    return 1.0 / (1.0 + jnp.exp(-x))


def _softplus(x):
    return jnp.maximum(x, 0.0) + jnp.log(1.0 + jnp.exp(-jnp.abs(x)))


def _silu(x):
    return x * _sigmoid(x)


def _rms(x, w):
    return x * lax.rsqrt(jnp.mean(x * x, axis=-1, keepdims=True) + EPS) * w


def _bf_round(x):
    return x.astype(MM).astype(F32)


def _acc(ref, val, first):
    @pl.when(first)
    def _():
        ref[...] = val

    @pl.when(jnp.logical_not(first))
    def _():
        ref[...] += val


def _tile(n, pref):
    if n % pref == 0:
        return pref
    return n


def matmul(a, b, *, ta=False, tb=False, b_slots=False, res=None, also_sqrelu=False, times_dsqrelu=None, out_dtype=F32,
           name, tm=1024, tn=1024, tk=1024):
    m, k = (a.shape[1], a.shape[0]) if ta else a.shape
    if b_slots:
        n = b.shape[1] if tb else N_CHIP * b.shape[2]
        assert (N_CHIP * b.shape[2] if tb else b.shape[1]) == k, (a.shape, b.shape, ta, tb)
        tn, tk = (tn, b.shape[2]) if tb else (b.shape[2], tk)
    else:
        n = b.shape[0] if tb else b.shape[1]
        assert (b.shape[1] if tb else b.shape[0]) == k, (a.shape, b.shape, ta, tb)
    tm, tn, tk = _tile(m, tm), _tile(n, tn), _tile(k, tk)
    nk = k // tk
    ca, cb = (0 if ta else 1), (1 if tb else 0)

    extra = tuple(e for e in (res, times_dsqrelu) if e is not None)
    assert len(extra) <= 1

    def body(a_ref, b_ref, *rest):
        e_ref = rest[0] if extra else None
        o_ref = rest[len(extra)]

        def finish(total):
            if res is not None:
                total = total + e_ref[...]
            if times_dsqrelu is not None:
                total = total * (2.0 * jnp.maximum(e_ref[...], 0.0))
            o_ref[...] = total.astype(o_ref.dtype)
            if also_sqrelu:
                rest[len(extra) + 1][...] = _sqrelu(total).astype(MM)

        if nk == 1:
            finish(_dot(a_ref[...], b_ref[...], ca, cb, False))
            return
        acc_ref, kk = rest[-1], pl.program_id(2)

        @pl.when(kk == 0)
        def _():
            acc_ref[...] = jnp.zeros_like(acc_ref)

        acc_ref[...] += _dot(a_ref[...], b_ref[...], ca, cb, False)

        @pl.when(kk == nk - 1)
        def _():
            finish(acc_ref[...])

    a_spec = pl.BlockSpec((tk, tm), lambda i, j, l: (l, i)) if ta else pl.BlockSpec((tm, tk), lambda i, j, l: (i, l))
    if b_slots:
        b_spec = (pl.BlockSpec((None, tn, tk), lambda i, j, l: (l, j, 0)) if tb else
                  pl.BlockSpec((None, tk, tn), lambda i, j, l: (j, l, 0)))
    else:
        b_spec = pl.BlockSpec((tn, tk), lambda i, j, l: (j, l)) if tb else pl.BlockSpec((tk, tn), lambda i, j, l: (l, j))
    o_spec = pl.BlockSpec((tm, tn), lambda i, j, l: (i, j))
    out_shape = [jax.ShapeDtypeStruct((m, n), out_dtype)] + [jax.ShapeDtypeStruct((m, n), MM)] * also_sqrelu
    outs = pl.pallas_call(
        body, name=name, grid=(m // tm, n // tn, nk),
        in_specs=[a_spec, b_spec] + [o_spec] * len(extra), out_specs=[o_spec] * len(out_shape), out_shape=out_shape,
        scratch_shapes=[pltpu.VMEM((tm, tn), F32)] * (nk > 1),
        compiler_params=_cparams(("parallel", "parallel", "arbitrary")),
    )(a, b, *extra)
    return outs if also_sqrelu else outs[0]


def rms_fwd(x, w, *, name):
    t, d = x.shape

    def body(x_ref, w_ref, o_ref):
        o_ref[...] = _rms(x_ref[...], w_ref[...]).astype(o_ref.dtype)

    return pl.pallas_call(
        body, name=name, grid=(t // ROWS,),
        in_specs=[pl.BlockSpec((ROWS, d), lambda i: (i, 0)), pl.BlockSpec((1, d), lambda i: (0, 0))],
        out_specs=pl.BlockSpec((ROWS, d), lambda i: (i, 0)),
        out_shape=jax.ShapeDtypeStruct((t, d), MM), compiler_params=_cparams(("parallel",)),
    )(x, w)


def rms_bwd(x, w, dh, dres, *, name):
    t, d = x.shape

    def body(x_ref, w_ref, dh_ref, dr_ref, dx_ref, dw_ref):
        _, vjp = jax.vjp(_rms, x_ref[...], w_ref[...])
        dx, dw = vjp(dh_ref[...].astype(F32))
        dx_ref[...] = dx + dr_ref[...]
        _acc(dw_ref, dw, pl.program_id(0) == 0)

    row = pl.BlockSpec((ROWS, d), lambda i: (i, 0))
    vec = pl.BlockSpec((1, d), lambda i: (0, 0))
    return pl.pallas_call(
        body, name=name, grid=(t // ROWS,), in_specs=[row, vec, row, row], out_specs=[row, vec],
        out_shape=[jax.ShapeDtypeStruct((t, d), F32), jax.ShapeDtypeStruct((1, d), F32)],
        compiler_params=_cparams(("arbitrary",)),
    )(x, w, dh, dres)


def _sqrelu(x):
    return jnp.square(jnp.maximum(x, 0.0))


def loss_fwd(y, target, *, name):
    t, d = y.shape

    def body(y_ref, t_ref, dy_ref, l_ref):
        e = y_ref[...] - t_ref[...]
        dy_ref[...] = e * (1.0 / d)
        part = 0.5 * jnp.sum(jnp.sum(e * e, axis=-1, keepdims=True) * (1.0 / d), axis=0, keepdims=True)
        _acc(l_ref, jnp.broadcast_to(part, (1, HEAD_DIM)), pl.program_id(0) == 0)

    blk = pl.BlockSpec((ROWS, d), lambda i: (i, 0))
    return pl.pallas_call(
        body, name=name, grid=(t // ROWS,), in_specs=[blk, blk],
        out_specs=[blk, pl.BlockSpec((1, HEAD_DIM), lambda i: (0, 0))],
        out_shape=[jax.ShapeDtypeStruct((t, d), F32), jax.ShapeDtypeStruct((1, HEAD_DIM), F32)],
        compiler_params=_cparams(("arbitrary",)),
    )(y, target)


def _mem_kv(mem, wn, wkn, *ws):
    mn = _rms(mem, wn)
    outs = []
    for h in range(MEM_HEADS):
        outs.append(_rms(mmul(mn, ws[h], 1, 0, False), wkn))
    for h in range(MEM_HEADS):
        outs.append(mmul(mn, ws[MEM_HEADS + h], 1, 0, False))
    return tuple(outs)


def _w_cols(w_ref):
    return [w_ref[:, h * HEAD_DIM:(h + 1) * HEAD_DIM] for h in range(2 * MEM_HEADS)]


def mem_fwd(mem, wn, wkv, wkn):
    def body(mem_ref, wn_ref, w_ref, wkn_ref, k_ref, v_ref):
        outs = _mem_kv(mem_ref[...], wn_ref[...], wkn_ref[...], *_w_cols(w_ref))
        for h in range(MEM_HEADS):
            k_ref[:, h * HEAD_DIM:(h + 1) * HEAD_DIM] = outs[h]
            v_ref[:, h * HEAD_DIM:(h + 1) * HEAD_DIM] = outs[MEM_HEADS + h]

    shp = jax.ShapeDtypeStruct((mem.shape[0], MEM_WIDTH), F32)
    return pl.pallas_call(body, name="mem_fwd", out_shape=[shp, shp], compiler_params=_cparams())(mem, wn, wkv, wkn)


def mem_bwd(mem, wn, wkv, wkn, dk0, dv0, dk1, dv1):
    def body(mem_ref, wn_ref, w_ref, wkn_ref, dk0_ref, dv0_ref, dk1_ref, dv1_ref, dwn_ref, dw_ref, dwkn_ref):
        _, vjp = jax.vjp(lambda wn_, wkn_, *ws: _mem_kv(mem_ref[...], wn_, wkn_, *ws),
                         wn_ref[...], wkn_ref[...], *[w.astype(F32) for w in _w_cols(w_ref)])
        cols = lambda a, b: tuple(a[:, h * HEAD_DIM:(h + 1) * HEAD_DIM] + b[:, h * HEAD_DIM:(h + 1) * HEAD_DIM]
                                  for h in range(MEM_HEADS))
        cts = cols(dk0_ref, dk1_ref) + cols(dv0_ref, dv1_ref)
        grads = vjp(cts)
        dwn_ref[...] = grads[0]
        dwkn_ref[...] = grads[1]
        for h in range(2 * MEM_HEADS):
            dw_ref[:, h * HEAD_DIM:(h + 1) * HEAD_DIM] = grads[2 + h]

    return pl.pallas_call(
        body, name="mem_bwd",
        out_shape=[jax.ShapeDtypeStruct((1, D_MODEL), F32), jax.ShapeDtypeStruct((D_MODEL, 2 * MEM_WIDTH), F32),
                   jax.ShapeDtypeStruct((1, HEAD_DIM), F32)],
        compiler_params=_cparams(),
    )(mem, wn, wkv, wkn, dk0, dv0, dk1, dv1)


def _memattn(q, wq, mk, mv):
    qn = _rms(q, wq) * QSCALE
    s = mmul(qn, mk, 1, 1, False)
    s = s - jnp.max(s, axis=-1, keepdims=True)
    p = jnp.exp(s)
    p = p / jnp.sum(p, axis=-1, keepdims=True)
    return mmul(p, mv, 1, 0, False)


def _lanes(j):
    return slice(j * HEAD_DIM, (j + 1) * HEAD_DIM)


def _memattn_specs(t):
    qspec = pl.BlockSpec((ROWS, MEM_WIDTH), lambda i: (i, (TAIL - MEM_WIDTH) // MEM_WIDTH))
    wspec = pl.BlockSpec((1, HEAD_DIM), lambda i: (0, 0))
    mspec = pl.BlockSpec((N_MEM, MEM_WIDTH), lambda i: (0, 0))
    ospec = pl.BlockSpec((ROWS, MEM_WIDTH), lambda i: (i, 0))
    return qspec, wspec, mspec, ospec


def memattn_fwd(proj, wq, mk, mv, *, name):
    t = proj.shape[0]
    qspec, wspec, mspec, ospec = _memattn_specs(t)

    def body(q_ref, w_ref, k_ref, v_ref, o_ref):
        for h in range(MEM_HEADS):
            o_ref[:, _lanes(h)] = _memattn(q_ref[:, _lanes(h)], w_ref[...], k_ref[:, _lanes(h)],
                                           v_ref[:, _lanes(h)]).astype(o_ref.dtype)

    return pl.pallas_call(
        body, name=name, grid=(t // ROWS,), in_specs=[qspec, wspec, mspec, mspec], out_specs=ospec,
        out_shape=jax.ShapeDtypeStruct((t, MEM_WIDTH), MM), compiler_params=_cparams(("parallel",)),
    )(proj, wq, mk, mv)


def memattn_bwd(proj, wq, mk, mv, dcat, *, name):
    t = proj.shape[0]
    qspec, wspec, mspec, ospec = _memattn_specs(t)
    dospec = pl.BlockSpec((ROWS, MEM_WIDTH), lambda i: (i, D_MODEL // MEM_WIDTH))

    def body(q_ref, w_ref, k_ref, v_ref, do_ref, dq_ref, dw_ref, dk_ref, dv_ref):
        first = pl.program_id(0) == 0
        dw_sum = jnp.zeros((1, HEAD_DIM), F32)
        for h in range(MEM_HEADS):
            _, vjp = jax.vjp(_memattn, q_ref[:, _lanes(h)], w_ref[...], k_ref[:, _lanes(h)], v_ref[:, _lanes(h)])
            dq, dw, dk, dv = vjp(do_ref[:, _lanes(h)].astype(F32))
            dq_ref[:, _lanes(h)] = dq.astype(dq_ref.dtype)
            dw_sum = dw_sum + dw
            _acc(dk_ref.at[:, _lanes(h)], dk, first)
            _acc(dv_ref.at[:, _lanes(h)], dv, first)
        _acc(dw_ref, dw_sum, first)

    mshape = jax.ShapeDtypeStruct((N_MEM, MEM_WIDTH), F32)
    return pl.pallas_call(
        body, name=name, grid=(t // ROWS,), in_specs=[qspec, wspec, mspec, mspec, dospec],
        out_specs=[ospec, wspec, mspec, mspec],
        out_shape=[jax.ShapeDtypeStruct((t, MEM_WIDTH), MM), jax.ShapeDtypeStruct((1, HEAD_DIM), F32), mshape, mshape],
        compiler_params=_cparams(("arbitrary",)),
    )(proj, wq, mk, mv, dcat)


def _shift_rows(x, s, up):
    n = x.shape[0]
    r = lax.broadcasted_iota(jnp.int32, x.shape, 0)
    if up:
        return jnp.where(r < n - s, pltpu.roll(x, n - s, 0), 0.0)
    return jnp.where(r >= s, pltpu.roll(x, s, 0), 0.0)


def _conv_fwd_vals(x, w):
    xb = _bf_round(x)
    wb = _bf_round(w)
    c = xb * wb[3:4, :]
    for j in range(3):
        c = c + _shift_rows(xb, 3 - j, False) * wb[j:j + 1, :]
    return xb, wb, c


def dn_prep_fwd(proj, conv_w):
    t = proj.shape[0]

    def body(x_ref, w_ref, o_ref):
        j = pl.program_id(0)
        _, _, c = _conv_fwd_vals(x_ref[...], w_ref[...])
        s = _silu(c)
        r = lax.rsqrt(jnp.sum(s * s, axis=-1, keepdims=True) + EPS)
        scale = jnp.where(j < N_HEADS, QSCALE, 1.0)
        o_ref[...] = jnp.where(j < 2 * N_HEADS, s * r * scale, s)

    return pl.pallas_call(
        body, name="dn_prep_fwd", grid=(3 * N_HEADS,),
        in_specs=[pl.BlockSpec((t, HEAD_DIM), lambda j: (0, j)), pl.BlockSpec((4, HEAD_DIM), lambda j: (0, j))],
        out_specs=pl.BlockSpec((None, t, HEAD_DIM), lambda j: (j // N_HEADS, 0, j % N_HEADS)),
        out_shape=jax.ShapeDtypeStruct((3, t, D_MODEL), F32), compiler_params=_cparams(("parallel",)),
    )(proj, conv_w)


def dn_prep_bwd(proj, conv_w, dqkv):
    t = proj.shape[0]

    def body(x_ref, w_ref, g_ref, dx_ref, dw_ref):
        j = pl.program_id(0)
        xb, wb, c = _conv_fwd_vals(x_ref[...], w_ref[...])
        sg = _sigmoid(c)
        s = c * sg
        g = g_ref[...]
        r = lax.rsqrt(jnp.sum(s * s, axis=-1, keepdims=True) + EPS)
        scale = jnp.where(j < N_HEADS, QSCALE, 1.0)
        gn = g * scale
        ds_norm = r * gn - s * (r * r * r) * jnp.sum(gn * s, axis=-1, keepdims=True)
        ds = jnp.where(j < 2 * N_HEADS, ds_norm, g)
        dc = ds * (sg + s * (1.0 - sg))
        dx = dc * wb[3:4, :]
        rows = [jnp.sum(dc * xb, axis=0, keepdims=True)]
        for jj in range(2, -1, -1):
            sh = 3 - jj
            dx = dx + _shift_rows(dc, sh, True) * wb[jj:jj + 1, :]
            rows.insert(0, jnp.sum(dc * _shift_rows(xb, sh, False), axis=0, keepdims=True))
        dx_ref[...] = dx.astype(dx_ref.dtype)
        dw_ref[...] = jnp.concatenate(rows + [jnp.zeros((4, HEAD_DIM), F32)], axis=0)

    col = pl.BlockSpec((t, HEAD_DIM), lambda j: (0, j))
    return pl.pallas_call(
        body, name="dn_prep_bwd", grid=(3 * N_HEADS,),
        in_specs=[col, pl.BlockSpec((4, HEAD_DIM), lambda j: (0, j)),
                  pl.BlockSpec((None, t, HEAD_DIM), lambda j: (j // N_HEADS, 0, j % N_HEADS))],
        out_specs=[col, pl.BlockSpec((8, HEAD_DIM), lambda j: (0, j))],
        out_shape=[jax.ShapeDtypeStruct((t, 3 * D_MODEL), MM), jax.ShapeDtypeStruct((8, 3 * D_MODEL), F32)],
        compiler_params=_cparams(("parallel",)),
    )(proj, conv_w, dqkv)


def _tri_ones(n, upper):
    r, c = _iota2(n, n)
    return (r <= c).astype(F32) if upper else (r >= c).astype(F32)


def dn_gates_fwd(proj, a_log, dt_bias):
    t = proj.shape[0]

    def body(x_ref, al_ref, dt_ref, o_ref):
        lane = lax.broadcasted_iota(jnp.int32, (CHUNK, HEAD_DIM), 1)
        tri = _tri_ones(CHUNK, False)

        def step(c, carry):
            rows = pl.ds(pl.multiple_of(c * CHUNK, CHUNK), CHUNK)
            x = x_ref[rows, :]
            g = jnp.where(lane < N_HEADS, -jnp.exp(al_ref[...]) * _softplus(x + dt_ref[...]), 0.0)
            gc = _dot(tri, g, 1, 0, True)
            o_ref[rows, :] = jnp.where(lane < N_HEADS, gc, jnp.where(lane < 2 * N_HEADS, _sigmoid(x), 0.0))
            return carry

        lax.fori_loop(0, t // CHUNK, step, 0)

    vec = pl.BlockSpec((1, HEAD_DIM), lambda i: (0, 0))
    return pl.pallas_call(
        body, name="dn_gates_fwd", grid=(1,),
        in_specs=[pl.BlockSpec((t, HEAD_DIM), lambda i: (0, TAIL_BLK)), vec, vec],
        out_specs=pl.BlockSpec((t, HEAD_DIM), lambda i: (0, 0)),
        out_shape=jax.ShapeDtypeStruct((t, HEAD_DIM), F32), compiler_params=_cparams(("arbitrary",)),
    )(proj, a_log, dt_bias)


def dn_gates_bwd(proj, a_log, dt_bias, dgates):
    t = proj.shape[0]

    def body(x_ref, al_ref, dt_ref, g_ref, dx_ref, dal_ref, ddt_ref):
        lane = lax.broadcasted_iota(jnp.int32, (CHUNK, HEAD_DIM), 1)
        tri = _tri_ones(CHUNK, True)
        dal_ref[...] = jnp.zeros_like(dal_ref)
        ddt_ref[...] = jnp.zeros_like(ddt_ref)

        def step(c, carry):
            rows = pl.ds(pl.multiple_of(c * CHUNK, CHUNK), CHUNK)
            x = x_ref[rows, :]
            dgc = jnp.where(lane < N_HEADS, g_ref[rows, :], 0.0)
            dg = _dot(tri, dgc, 1, 0, True)
            ea = -jnp.exp(al_ref[...])
            z = x + dt_ref[...]
            da = jnp.where(lane < N_HEADS, dg * ea * _sigmoid(z), 0.0)
            gval = jnp.where(lane < N_HEADS, ea * _softplus(z), 0.0)
            beta = _sigmoid(x)
            db = jnp.where(jnp.logical_and(lane >= N_HEADS, lane < 2 * N_HEADS), g_ref[rows, :] * beta * (1.0 - beta), 0.0)
            dx_ref[rows, :] = (da + db).astype(dx_ref.dtype)
            dal_ref[...] += jnp.sum(dg * gval, axis=0, keepdims=True)
            ddt_ref[...] += jnp.sum(da, axis=0, keepdims=True)
            return carry

        lax.fori_loop(0, t // CHUNK, step, 0)

    vec = pl.BlockSpec((1, HEAD_DIM), lambda i: (0, 0))
    full = pl.BlockSpec((t, HEAD_DIM), lambda i: (0, 0))
    return pl.pallas_call(
        body, name="dn_gates_bwd", grid=(1,),
        in_specs=[pl.BlockSpec((t, HEAD_DIM), lambda i: (0, TAIL_BLK)), vec, vec, full],
        out_specs=[full, vec, vec],
        out_shape=[jax.ShapeDtypeStruct((t, HEAD_DIM), MM), jax.ShapeDtypeStruct((1, HEAD_DIM), F32),
                   jax.ShapeDtypeStruct((1, HEAD_DIM), F32)],
        compiler_params=_cparams(("arbitrary",)),
    )(proj, a_log, dt_bias, dgates)


def _dn_intra(q, k, v, gcol, grow, bcol):
    r, c = _iota2(CHUNK, CHUNK)
    causal, strict = r >= c, r > c
    decay = jnp.where(causal, jnp.exp(jnp.where(causal, gcol - grow, 0.0)), 0.0)
    kb = k * bcol
    a = jnp.where(strict, mmul(kb, k, 1, 1, False) * decay, 0.0)
    tm = tri_inv(a)
    u = mmul(tm, v * bcol, 1, 0, False)
    w = mmul(tm, kb * jnp.exp(gcol), 1, 0, False)
    qk = jnp.where(causal, mmul(q, k, 1, 1, False) * decay, 0.0)
    rr = lax.broadcasted_iota(jnp.int32, (CHUNK, 1), 0)
    g_last = jnp.sum(jnp.where(rr == CHUNK - 1, gcol, 0.0), axis=0, keepdims=True)
    return u, w, q * jnp.exp(gcol), k * jnp.exp(g_last - gcol), qk, jnp.exp(g_last)


def _dn_scan(u, w, qg, kd, qk, eg, state):
    v_new = u - mmul(w, state, 1, 0, False)
    out = mmul(qg, state, 1, 0, False) + mmul(qk, v_new, 1, 0, False)
    return out, state * eg + mmul(kd, v_new, 0, 0, False)


DN_HEADS_PER_STEP = 1
DN_GROUP = 8
DN_PARTS = ((CHUNK, HEAD_DIM),) * 4 + ((CHUNK, CHUNK), (1, 1))


def _dn_scratch(hb, nc):
    return [pltpu.VMEM((hb, nc) + shape, F32) for shape in DN_PARTS]


def _dn_part_specs(hb, nc):
    return [pl.BlockSpec((hb, nc) + shape, lambda h: (h, 0, 0, 0)) for shape in DN_PARTS]


def _dn_group(nc):
    return min(DN_GROUP, nc)


def _dn_group_args(refs, j, g, grp):
    q_ref, k_ref, v_ref, gc_ref, gr_ref, bc_ref = refs
    rows = pl.ds(pl.multiple_of(g * (grp * CHUNK), grp * CHUNK), grp * CHUNK)
    cs = pl.ds(g * grp, grp)
    split = lambda ref: ref[rows, _lanes(j)].reshape(grp, CHUNK, HEAD_DIM)
    return split(q_ref), split(k_ref), split(v_ref), gc_ref[j, cs], gr_ref[j, cs], bc_ref[j, cs]


def _dn_intra_all(refs, parts, hb, nc):
    grp = _dn_group(nc)

    def group(g, carry):
        cs = pl.ds(g * grp, grp)
        for j in range(hb):
            for part, val in zip(parts, jax.vmap(_dn_intra)(*_dn_group_args(refs, j, g, grp))):
                part[j, cs] = val
        return carry

    lax.fori_loop(0, nc // grp, group, 0)


def _dn_specs(t):
    nc, hb = t // CHUNK, DN_HEADS_PER_STEP
    head = lambda which: pl.BlockSpec((None, t, hb * HEAD_DIM), lambda h: (which, 0, h))
    flat = pl.BlockSpec((t, hb * HEAD_DIM), lambda h: (0, h))
    col = pl.BlockSpec((hb, nc, CHUNK, 1), lambda h: (h, 0, 0, 0))
    row = pl.BlockSpec((hb, nc, 1, CHUNK), lambda h: (h, 0, 0, 0))
    st = pl.BlockSpec((hb, nc, HEAD_DIM, HEAD_DIM), lambda h: (h, 0, 0, 0))
    return nc, hb, head, flat, col, row, st


def dn_core_fwd(qkv, gcol, grow, bcol):
    t = qkv.shape[1]
    nc, hb, head, flat, col, row, st = _dn_specs(t)

    def body(q_ref, k_ref, v_ref, gc_ref, gr_ref, bc_ref, o_ref, s_ref, *parts):
        _dn_intra_all((q_ref, k_ref, v_ref, gc_ref, gr_ref, bc_ref), parts, hb, nc)

        def step(c, states):
            rows = pl.ds(pl.multiple_of(c * CHUNK, CHUNK), CHUNK)
            new_states = []
            for j in range(hb):
                s_ref[j, c] = states[j]
                out, new_state = _dn_scan(*[part[j, c] for part in parts], states[j])
                o_ref[rows, _lanes(j)] = out
                new_states.append(new_state)
            return tuple(new_states)

        lax.fori_loop(0, nc, step, tuple(jnp.zeros((HEAD_DIM, HEAD_DIM), F32) for _ in range(hb)))

    outs = pl.pallas_call(
        body, name="dn_core_fwd", grid=(N_HEADS // hb,),
        in_specs=[head(0), head(1), head(2), col, row, col], out_specs=[flat, st] + _dn_part_specs(hb, nc),
        out_shape=[jax.ShapeDtypeStruct((t, D_MODEL), F32), jax.ShapeDtypeStruct((N_HEADS, nc, HEAD_DIM, HEAD_DIM), F32)]
        + [jax.ShapeDtypeStruct((N_HEADS, nc) + shape, F32) for shape in DN_PARTS],
        compiler_params=_cparams(("parallel",)),
    )(qkv, qkv, qkv, gcol, grow, bcol)
    return outs[0], outs[1], tuple(outs[2:])


def dn_core_bwd(qkv, gcol, grow, bcol, states, parts, do):
    t = qkv.shape[1]
    nc, hb, head, flat, col, row, st = _dn_specs(t)
    n_parts = len(DN_PARTS)

    def body(q_ref, k_ref, v_ref, gc_ref, gr_ref, bc_ref, s_ref, do_ref, *rest):
        parts, (dqkv_ref, dgc_ref, dgr_ref, dbc_ref), dparts = rest[:n_parts], rest[n_parts:n_parts + 4], rest[n_parts + 4:]
        refs = (q_ref, k_ref, v_ref, gc_ref, gr_ref, bc_ref)

        def step(i, dstates):
            c = nc - 1 - i
            rows = pl.ds(pl.multiple_of(c * CHUNK, CHUNK), CHUNK)
            dstates_in = []
            for j in range(hb):
                _, vjp = jax.vjp(_dn_scan, *[part[j, c] for part in parts], s_ref[j, c])
                *dvals, dstate_in = vjp((do_ref[rows, _lanes(j)], dstates[j]))
                for dpart, dval in zip(dparts, dvals):
                    dpart[j, c] = dval
                dstates_in.append(dstate_in)
            return tuple(dstates_in)

        lax.fori_loop(0, nc, step, tuple(jnp.zeros((HEAD_DIM, HEAD_DIM), F32) for _ in range(hb)))

        grp = _dn_group(nc)

        def group(g, carry):
            rows = pl.ds(pl.multiple_of(g * (grp * CHUNK), grp * CHUNK), grp * CHUNK)
            cs = pl.ds(g * grp, grp)
            for j in range(hb):
                _, vjp = jax.vjp(jax.vmap(_dn_intra), *_dn_group_args(refs, j, g, grp))
                dq, dk, dv, dgc, dgr, dbc = vjp(tuple(dpart[j, cs] for dpart in dparts))
                for which, val in enumerate((dq, dk, dv)):
                    dqkv_ref[which, rows, _lanes(j)] = val.reshape(grp * CHUNK, HEAD_DIM)
                dgc_ref[j, cs] = dgc
                dgr_ref[j, cs] = dgr
                dbc_ref[j, cs] = dbc
            return carry

        lax.fori_loop(0, nc // grp, group, 0)

    return pl.pallas_call(
        body, name="dn_core_bwd", grid=(N_HEADS // hb,), scratch_shapes=_dn_scratch(hb, nc),
        in_specs=[head(0), head(1), head(2), col, row, col, st, flat] + _dn_part_specs(hb, nc),
        out_specs=[pl.BlockSpec((3, t, hb * HEAD_DIM), lambda h: (0, 0, h)), col, row, col],
        out_shape=[jax.ShapeDtypeStruct((3, t, D_MODEL), F32)] + [
            jax.ShapeDtypeStruct((N_HEADS, nc, CHUNK, 1), F32), jax.ShapeDtypeStruct((N_HEADS, nc, 1, CHUNK), F32),
            jax.ShapeDtypeStruct((N_HEADS, nc, CHUNK, 1), F32)],
        compiler_params=_cparams(("parallel",)),
    )(qkv, qkv, qkv, gcol, grow, bcol, states, do, *parts)


def gates_to_heads(gates):
    t = gates.shape[0]
    nc = t // CHUNK
    g = gates[:, :N_HEADS].T.reshape(N_HEADS, nc, CHUNK)
    b = gates[:, N_HEADS:2 * N_HEADS].T.reshape(N_HEADS, nc, CHUNK)
    return g[..., None], g[:, :, None, :], b[..., None]


def heads_to_gates(dgcol, dgrow, dbcol):
    nh, nc = dgcol.shape[:2]
    dg = (dgcol[..., 0] + dgrow[:, :, 0, :]).reshape(nh, nc * CHUNK).T
    db = dbcol[..., 0].reshape(nh, nc * CHUNK).T
    return jnp.concatenate([dg, db, jnp.zeros((nc * CHUNK, HEAD_DIM - 2 * nh), F32)], axis=1)


def _dn_out(o, z, w):
    return _rms(o, w) * _silu(z)


def _gate_specs():
    o_spec = pl.BlockSpec((ROWS, D_MODEL), lambda i: (i, 0))
    z_spec = pl.BlockSpec((ROWS, D_MODEL), lambda i: (i, 3))
    w_spec = pl.BlockSpec((1, HEAD_DIM), lambda i: (0, 0))
    return o_spec, z_spec, w_spec


def dn_out_fwd(o, proj, w):
    t = o.shape[0]
    o_spec, z_spec, w_spec = _gate_specs()

    def body(o_ref, z_ref, w_ref, y_ref):
        for h in range(N_HEADS):
            y_ref[:, _lanes(h)] = _dn_out(o_ref[:, _lanes(h)], z_ref[:, _lanes(h)], w_ref[...]).astype(y_ref.dtype)

    return pl.pallas_call(
        body, name="dn_out_fwd", grid=(t // ROWS,), in_specs=[o_spec, z_spec, w_spec], out_specs=o_spec,
        out_shape=jax.ShapeDtypeStruct((t, D_MODEL), MM), compiler_params=_cparams(("parallel",)),
    )(o, proj, w)


def dn_out_bwd(o, proj, w, dcat):
    t = o.shape[0]
    o_spec, z_spec, w_spec = _gate_specs()

    def body(o_ref, z_ref, w_ref, g_ref, do_ref, dz_ref, dw_ref):
        dw_sum = jnp.zeros((1, HEAD_DIM), F32)
        for h in range(N_HEADS):
            _, vjp = jax.vjp(_dn_out, o_ref[:, _lanes(h)], z_ref[:, _lanes(h)], w_ref[...])
            do, dz, dw = vjp(g_ref[:, _lanes(h)].astype(F32))
            do_ref[:, _lanes(h)] = do
            dz_ref[:, _lanes(h)] = dz.astype(dz_ref.dtype)
            dw_sum = dw_sum + dw
        _acc(dw_ref, dw_sum, pl.program_id(0) == 0)

    return pl.pallas_call(
        body, name="dn_out_bwd", grid=(t // ROWS,), in_specs=[o_spec, z_spec, w_spec, o_spec],
        out_specs=[o_spec, o_spec, w_spec],
        out_shape=[jax.ShapeDtypeStruct((t, D_MODEL), F32), jax.ShapeDtypeStruct((t, D_MODEL), MM),
                   jax.ShapeDtypeStruct((1, HEAD_DIM), F32)],
        compiler_params=_cparams(("arbitrary",)),
    )(o, proj, w, dcat)


def _fox_norm(x, w, scale):
    return _rms(x, w) * scale


def _fox_prep_specs():
    x_spec = pl.BlockSpec((ROWS, 2 * D_MODEL), lambda i: (i, 0))
    w_spec = pl.BlockSpec((2, 1, HEAD_DIM), lambda i: (0, 0, 0))
    y_spec = pl.BlockSpec((2, ROWS, D_MODEL), lambda i: (0, i, 0))
    return x_spec, w_spec, y_spec


def fox_prep_fwd(proj, wqk):
    t = proj.shape[0]
    x_spec, w_spec, y_spec = _fox_prep_specs()

    def body(x_ref, w_ref, y_ref):
        for j in range(2 * N_HEADS):
            which, scale = j // N_HEADS, (QSCALE if j < N_HEADS else 1.0)
            y_ref[which, :, _lanes(j % N_HEADS)] = _fox_norm(x_ref[:, _lanes(j)], w_ref[which], scale).astype(y_ref.dtype)

    return pl.pallas_call(
        body, name="fox_prep_fwd", grid=(t // ROWS,), in_specs=[x_spec, w_spec], out_specs=y_spec,
        out_shape=jax.ShapeDtypeStruct((2, t, D_MODEL), MM), compiler_params=_cparams(("parallel",)),
    )(proj, wqk)


def fox_prep_bwd(proj, wqk, dq, dk):
    t = proj.shape[0]
    x_spec, w_spec, _ = _fox_prep_specs()
    g_spec = pl.BlockSpec((ROWS, D_MODEL), lambda i: (i, 0))

    def body(x_ref, w_ref, dq_ref, dk_ref, dx_ref, dw_ref):
        dws = [jnp.zeros((1, HEAD_DIM), F32), jnp.zeros((1, HEAD_DIM), F32)]
        for j in range(2 * N_HEADS):
            which, scale = j // N_HEADS, (QSCALE if j < N_HEADS else 1.0)
            g_ref = dq_ref if which == 0 else dk_ref
            _, vjp = jax.vjp(lambda x, w: _fox_norm(x, w, scale), x_ref[:, _lanes(j)], w_ref[which])
            dx, dw = vjp(g_ref[:, _lanes(j % N_HEADS)])
            dx_ref[:, _lanes(j)] = dx.astype(dx_ref.dtype)
            dws[which] = dws[which] + dw
        first = pl.program_id(0) == 0
        _acc(dw_ref.at[0], dws[0], first)
        _acc(dw_ref.at[1], dws[1], first)

    return pl.pallas_call(
        body, name="fox_prep_bwd", grid=(t // ROWS,), in_specs=[x_spec, w_spec, g_spec, g_spec],
        out_specs=[x_spec, w_spec],
        out_shape=[jax.ShapeDtypeStruct((t, 2 * D_MODEL), MM), jax.ShapeDtypeStruct((2, 1, HEAD_DIM), F32)],
        compiler_params=_cparams(("arbitrary",)),
    )(proj, wqk, dq, dk)


def _row_pick(x, i):
    r = lax.broadcasted_iota(jnp.int32, x.shape, 0)
    return jnp.sum(jnp.where(r == i, x, 0.0), axis=0, keepdims=True)


def fox_gates_fwd(proj, f_bias):
    t = proj.shape[0]
    blk = HEAD_DIM

    def body(x_ref, b_ref, o_ref):
        lane = lax.broadcasted_iota(jnp.int32, (blk, HEAD_DIM), 1)
        tri = _tri_ones(blk, False)

        def step(c, carry):
            rows = pl.ds(pl.multiple_of(c * blk, blk), blk)
            lf = jnp.where(lane < N_HEADS, -_softplus(-(x_ref[rows, :] + b_ref[...])), 0.0)
            cum = _dot(tri, lf, 1, 0, True) + carry
            o_ref[rows, :] = cum
            return _row_pick(cum, blk - 1)

        lax.fori_loop(0, t // blk, step, jnp.zeros((1, HEAD_DIM), F32))

    vec = pl.BlockSpec((1, HEAD_DIM), lambda i: (0, 0))
    return pl.pallas_call(
        body, name="fox_gates_fwd", grid=(1,),
        in_specs=[pl.BlockSpec((t, HEAD_DIM), lambda i: (0, TAIL_BLK)), vec],
        out_specs=pl.BlockSpec((t, HEAD_DIM), lambda i: (0, 0)),
        out_shape=jax.ShapeDtypeStruct((t, HEAD_DIM), F32), compiler_params=_cparams(("arbitrary",)),
    )(proj, f_bias)


def fox_gates_bwd(proj, f_bias, dfcum):
    t = proj.shape[0]
    blk = HEAD_DIM
    nb = t // blk

    def body(x_ref, b_ref, g_ref, dx_ref, db_ref):
        lane = lax.broadcasted_iota(jnp.int32, (blk, HEAD_DIM), 1)
        tri = _tri_ones(blk, True)
        db_ref[...] = jnp.zeros_like(db_ref)

        def step(i, carry):
            c = nb - 1 - i
            rows = pl.ds(pl.multiple_of(c * blk, blk), blk)
            g = jnp.where(lane < N_HEADS, g_ref[rows, :], 0.0)
            dlf = _dot(tri, g, 1, 0, True) + carry
            dx = jnp.where(lane < N_HEADS, dlf * _sigmoid(-(x_ref[rows, :] + b_ref[...])), 0.0)
            dx_ref[rows, :] = dx.astype(dx_ref.dtype)
            db_ref[...] += jnp.sum(dx, axis=0, keepdims=True)
            return carry + jnp.sum(g, axis=0, keepdims=True)

        lax.fori_loop(0, nb, step, jnp.zeros((1, HEAD_DIM), F32))

    vec = pl.BlockSpec((1, HEAD_DIM), lambda i: (0, 0))
    full = pl.BlockSpec((t, HEAD_DIM), lambda i: (0, 0))
    return pl.pallas_call(
        body, name="fox_gates_bwd", grid=(1,),
        in_specs=[pl.BlockSpec((t, HEAD_DIM), lambda i: (0, TAIL_BLK)), vec, full], out_specs=[full, vec],
        out_shape=[jax.ShapeDtypeStruct((t, HEAD_DIM), MM), jax.ShapeDtypeStruct((1, HEAD_DIM), F32)],
        compiler_params=_cparams(("arbitrary",)),
    )(proj, f_bias, dfcum)


def fcum_to_heads(fcum):
    f = fcum[:, :N_HEADS].T
    return f[:, :, None], f[:, None, :]


def heads_to_fcum(dfcol, dfrow):
    d = (dfcol[:, :, 0] + dfrow[:, 0, :]).T
    return jnp.concatenate([d, jnp.zeros((d.shape[0], HEAD_DIM - N_HEADS), F32)], axis=1)


def _fox_tq(t):
    return min(t, 256)


def _fox_specs(t):
    tq = _fox_tq(t)
    q_spec = pl.BlockSpec((None, tq, HEAD_DIM), lambda h, i: (0, i, h))
    k_spec = pl.BlockSpec((None, t, HEAD_DIM), lambda h, i: (1, 0, h))
    v_spec = pl.BlockSpec((t, HEAD_DIM), lambda h, i: (0, 2 * N_HEADS + h))
    gate_spec = pl.BlockSpec((tq, HEAD_DIM), lambda h, i: (i, 3 * N_HEADS + h))
    col_spec = pl.BlockSpec((None, tq, 1), lambda h, i: (h, i, 0))
    row_spec = pl.BlockSpec((None, 1, t), lambda h, i: (h, 0, 0))
    blk_spec = pl.BlockSpec((tq, HEAD_DIM), lambda h, i: (i, h))
    head_spec = pl.BlockSpec((t, HEAD_DIM), lambda h, i: (0, h))
    return tq, q_spec, k_spec, v_spec, gate_spec, col_spec, row_spec, blk_spec, head_spec


def _fox_segments(i, tq):
    return ([(0, i * tq, False)] if i else []) + [(i * tq, (i + 1) * tq, True)]


def _fox_scores(q_ref, k_ref, fc_ref, fr_ref, lo, hi, causal):
    s = _dot(q_ref[...], k_ref[lo:hi, :], 1, 1, False) + (fc_ref[...] - fr_ref[:, lo:hi])
    if not causal:
        return s, None
    r, c = _iota2(hi - lo, hi - lo)
    return s, c <= r


def fox_attn_fwd(qk, proj, fcol, frow):
    t = proj.shape[0]
    tq, q_spec, k_spec, v_spec, gate_spec, col_spec, row_spec, blk_spec, _ = _fox_specs(t)

    def body(q_ref, k_ref, v_ref, gate_ref, fc_ref, fr_ref, mix_ref, o_ref, lse_ref):
        def block(i):
            segs = _fox_segments(i, tq)
            scores = [_fox_scores(q_ref, k_ref, fc_ref, fr_ref, *seg) for seg in segs]
            scores = [(s if mask is None else jnp.where(mask, s, -1e30), mask) for s, mask in scores]
            m = functools.reduce(jnp.maximum, [jnp.max(s, axis=-1, keepdims=True) for s, _ in scores])
            l, o = 0.0, 0.0
            for (lo, hi, _), (s, mask) in zip(segs, scores):
                p = jnp.exp(s - m)
                p = p if mask is None else jnp.where(mask, p, 0.0)
                l = l + jnp.sum(p, axis=-1, keepdims=True)
                o = o + _dot(p, v_ref[lo:hi, :], 1, 0, False)
            o = o / l
            o_ref[...] = o
            mix_ref[...] = (o * _sigmoid(gate_ref[...])).astype(mix_ref.dtype)
            lse_ref[...] = m + jnp.log(l)

        for i in range(t // tq):
            pl.when(pl.program_id(1) == i)(functools.partial(block, i))

    return pl.pallas_call(
        body, name="fox_attn_fwd", grid=(N_HEADS, t // tq),
        in_specs=[q_spec, k_spec, v_spec, gate_spec, col_spec, row_spec], out_specs=[blk_spec, blk_spec, col_spec],
        out_shape=[jax.ShapeDtypeStruct((t, D_MODEL), MM), jax.ShapeDtypeStruct((t, D_MODEL), F32),
                   jax.ShapeDtypeStruct((N_HEADS, t, 1), F32)],
        compiler_params=_cparams(("parallel", "parallel")),
    )(qk, qk, proj, proj, fcol, frow)


def fox_attn_bwd(qk, proj, fcol, frow, o, lse, dcat):
    t = proj.shape[0]
    tq, q_spec, k_spec, v_spec, gate_spec, col_spec, row_spec, blk_spec, head_spec = _fox_specs(t)

    def body(q_ref, k_ref, v_ref, gate_ref, fc_ref, fr_ref, o_ref, lse_ref, g_ref,
             dq_ref, dk_ref, dv_ref, dgate_ref, dfc_ref, dfr_ref):
        @pl.when(pl.program_id(1) == 0)
        def _():
            dk_ref[...] = jnp.zeros_like(dk_ref)
            dv_ref[...] = jnp.zeros_like(dv_ref)
            dfr_ref[...] = jnp.zeros_like(dfr_ref)

        def block(i):
            sg = _sigmoid(gate_ref[...])
            g = g_ref[...].astype(F32)
            o_pre = o_ref[...]
            do = g * sg
            dgate_ref[...] = (g * o_pre * sg * (1.0 - sg)).astype(dgate_ref.dtype)
            delta = jnp.sum(do * o_pre, axis=-1, keepdims=True)
            dq, dfc = 0.0, 0.0
            for lo, hi, causal in _fox_segments(i, tq):
                s, mask = _fox_scores(q_ref, k_ref, fc_ref, fr_ref, lo, hi, causal)
                if causal:
                    p = jnp.where(mask, jnp.exp(jnp.where(mask, s, 0.0) - lse_ref[...]), 0.0)
                else:
                    p = jnp.exp(s - lse_ref[...])
                ds = p * (_dot(do, v_ref[lo:hi, :], 1, 1, False) - delta)
                dq = dq + _dot(ds, k_ref[lo:hi, :], 1, 0, False)
                dk_ref[lo:hi, :] += _dot(ds, q_ref[...], 0, 0, False)
                dv_ref[lo:hi, :] += _dot(p, do, 0, 0, False)
                dfc = dfc + jnp.sum(ds, axis=-1, keepdims=True)
                dfr_ref[:, lo:hi] += -jnp.sum(ds, axis=0, keepdims=True)
            dq_ref[...] = dq
            dfc_ref[...] = dfc

        for i in range(t // tq):
            pl.when(pl.program_id(1) == i)(functools.partial(block, i))

    f32 = lambda *s: jax.ShapeDtypeStruct(s, F32)
    return pl.pallas_call(
        body, name="fox_attn_bwd", grid=(N_HEADS, t // tq),
        in_specs=[q_spec, k_spec, v_spec, gate_spec, col_spec, row_spec, blk_spec, col_spec, blk_spec],
        out_specs=[blk_spec, head_spec, head_spec, blk_spec, col_spec, row_spec],
        out_shape=[f32(t, D_MODEL), f32(t, D_MODEL), f32(t, D_MODEL), jax.ShapeDtypeStruct((t, D_MODEL), MM),
                   f32(N_HEADS, t, 1), f32(N_HEADS, 1, t)],
        compiler_params=_cparams(("parallel", "arbitrary")),
    )(qk, qk, proj, proj, fcol, frow, o, lse, dcat)


def adamw(w, g, m, v, *, name):
    r, c = w.shape
    rb = ROWS if r % ROWS == 0 else r

    def body(w_ref, g_ref, m_ref, v_ref, d_ref, nm_ref, nv_ref):
        g_ = g_ref[...]
        m_ = ADAM_B1 * m_ref[...] + (1.0 - ADAM_B1) * g_
        v_ = ADAM_B2 * v_ref[...] + (1.0 - ADAM_B2) * jnp.square(g_)
        m_hat = m_ / (1.0 - ADAM_B1 ** ADAM_STEP)
        v_hat = v_ / (1.0 - ADAM_B2 ** ADAM_STEP)
        d_ref[...] = -ADAM_LR * (m_hat / (jnp.sqrt(v_hat) + ADAM_EPS) + ADAM_WD * w_ref[...])
        nm_ref[...] = m_
        nv_ref[...] = v_

    blk = pl.BlockSpec((rb, c), lambda i: (i, 0))
    shp = jax.ShapeDtypeStruct((r, c), F32)
    return pl.pallas_call(body, name=name, grid=(r // rb,), in_specs=[blk] * 4, out_specs=[blk] * 3,
                          out_shape=[shp] * 3, compiler_params=_cparams(("parallel",)))(w, g, m, v)


def _place():
    x, y, c = lax.axis_index("x"), lax.axis_index("y"), lax.axis_index("c")
    return x, y, c, [(1 - x, y), (x, 1 - y), (1 - x, 1 - y)]


ANY = pl.BlockSpec(memory_space=pl.ANY)


def all_reduce_small(v):
    r, w = v.shape

    def body(v_ref, o_ref, buf, send_sems, recv_sems):
        x, y, c, _ = _place()
        me = 4 * x + 2 * y + c
        flip = lambda a, bit: 1 - a if bit else a
        cps = []
        for k in range(1, N_DEV):
            peer = (flip(x, k & 4), flip(y, k & 2), flip(c, k & 1))
            cp = pltpu.make_async_remote_copy(src_ref=v_ref, dst_ref=buf.at[me], send_sem=send_sems.at[k - 1],
                                              recv_sem=recv_sems.at[k - 1], device_id=peer, device_id_type=MESH)
            cp.start()
            cps.append((cp, 4 * peer[0] + 2 * peer[1] + peer[2]))
        buf[me] = v_ref[...]
        for k, (cp, peer_id) in enumerate(cps):
            pltpu.make_async_remote_copy(src_ref=v_ref, dst_ref=buf.at[peer_id], send_sem=send_sems.at[k],
                                         recv_sem=recv_sems.at[k], device_id=(x, y, c), device_id_type=MESH).wait_recv()
        for cp, _ in cps:
            cp.wait_send()
        acc = buf[0]
        for d in range(1, N_DEV):
            acc = acc + buf[d]
        o_ref[...] = acc

    vm = pl.BlockSpec(memory_space=pltpu.VMEM)
    return pl.pallas_call(
        body, name="all_reduce_small", in_specs=[vm], out_specs=vm, out_shape=jax.ShapeDtypeStruct((r, w), F32),
        scratch_shapes=[pltpu.VMEM((N_DEV, r, w), F32), pltpu.SemaphoreType.DMA((N_DEV - 1,)),
                        pltpu.SemaphoreType.DMA((N_DEV - 1,))],
    )(v)


def _vec8(v):
    return jnp.zeros((1, HEAD_DIM), F32).at[0, :N_HEADS].set(v.reshape(N_HEADS))


def _layer_fwd(i, x_in, wt, sm, mem_k, mem_v, late=None):
    tag = f"l{i}_"
    h = rms_fwd(x_in, sm["norm1_w"][i][None], name=tag + "rms1")
    w_in = wt["dn_w_in"] if i == 0 else wt["fox_w_in"]
    proj = matmul(h, w_in, name=tag + "proj", tm=256, tk=1024)
    sv = dict(x_in=x_in, h=h, proj=proj)
    if i == 0:
        qkv = dn_prep_fwd(proj, wt["conv_w"])
        gates = dn_gates_fwd(proj, _vec8(sm["dn_a_log"]), _vec8(sm["dn_dt_bias"]))
        gcol, grow, bcol = gates_to_heads(gates)
        o, states, parts = dn_core_fwd(qkv, gcol, grow, bcol)
        mix = dn_out_fwd(o, proj, sm["dn_o_norm_w"])
        sv.update(qkv=qkv, gcol=gcol, grow=grow, bcol=bcol, states=states, parts=parts, o=o)
    else:
        wqk = jnp.stack([sm["fox_q_norm_w"], sm["fox_k_norm_w"]])
        qk = fox_prep_fwd(proj, wqk)
        fcum = fox_gates_fwd(proj, _vec8(sm["fox_f_bias"]))
        fcol, frow = fcum_to_heads(fcum)
        mix, o, lse = fox_attn_fwd(qk, proj, fcol, frow)
        sv.update(wqk=wqk, qk=qk, fcol=fcol, frow=frow, o=o, lse=lse)
    mem_out = memattn_fwd(proj, sm["memq_norm_w"][i][None], mem_k, mem_v, name=tag + "memattn_fwd")
    cat = jnp.concatenate([mix, mem_out], axis=1)
    if late is not None:
        wt.update(late(cat))
    x_mid = matmul(cat, wt["w_out"][i], res=x_in, name=tag + "out_proj")
    h2 = rms_fwd(x_mid, sm["norm2_w"][i][None], name=tag + "rms2")
    ff, act = matmul(h2, wt["w_mlp1"][i], b_slots=True, also_sqrelu=True, out_dtype=MM, name=tag + "mlp1")
    x_out = matmul(act, wt["w_mlp2"][i], res=x_mid, name=tag + "mlp2")
    sv.update(cat=cat, x_mid=x_mid, h2=h2, ff=ff, act=act)
    return x_out, sv


def _layer_bwd(i, dx_out, sv, wt, sm, mem_k, mem_v, on_mlp=None, on_core=None):
    tag = f"l{i}_"
    big, small = {}, {}
    dff = matmul(dx_out, wt["w_mlp2"][i], tb=True, times_dsqrelu=sv["ff"], out_dtype=MM, name=tag + "d_ff")
    big["w_mlp2"] = matmul(sv["act"], dx_out, ta=True, name=tag + "d_w_mlp2", tk=2048)
    dh2 = matmul(dff, wt["w_mlp1"][i], tb=True, b_slots=True, name=tag + "d_h2")
    big["w_mlp1"] = matmul(sv["h2"], dff, ta=True, name=tag + "d_w_mlp1", tm=512, tn=D_FF, tk=512)
    dx_mid, small["norm2_w"] = rms_bwd(sv["x_mid"], sm["norm2_w"][i][None], dh2, dx_out, name=tag + "rms2_bwd")
    dcat = matmul(dx_mid, wt["w_out"][i], tb=True, name=tag + "d_cat")
    big["w_out"] = matmul(sv["cat"], dx_mid, ta=True, name=tag + "d_w_out", tk=2048)
    proj = sv["proj"]
    memq_norm_w = sm["memq_norm_w"][i][None]
    if on_mlp is not None:
        memq_norm_w = memq_norm_w + on_mlp(big["w_mlp2"], big["w_mlp1"], big["w_out"])
    dqm, small["memq_norm_w"], dmk, dmv = memattn_bwd(proj, memq_norm_w, mem_k, mem_v, dcat, name=tag + "memattn_bwd")
    t = proj.shape[0]
    pad = jnp.zeros((t, PROJ_W - TAIL - HEAD_DIM), MM)
    if i == 0:
        do, dz, small["dn_o_norm_w"] = dn_out_bwd(sv["o"], proj, sm["dn_o_norm_w"], dcat)
        bcol = sv["bcol"] if on_core is None else sv["bcol"] + on_core(do)
        dqkv, dgc, dgr, dbc = dn_core_bwd(sv["qkv"], sv["gcol"], sv["grow"], bcol, sv["states"], sv["parts"], do)
        dtail, dal, ddt = dn_gates_bwd(proj, _vec8(sm["dn_a_log"]), _vec8(sm["dn_dt_bias"]), heads_to_gates(dgc, dgr, dbc))
        dmain, dconv = dn_prep_bwd(proj, wt["conv_w"], dqkv)
        small["dn_a_log"], small["dn_dt_bias"] = dal[:, :N_HEADS], ddt[:, :N_HEADS]
        big["conv_w"] = dconv[:4]
        dproj = jnp.concatenate([dmain, dz, dqm, dtail, pad], axis=1)
    else:
        dq, dk, dv, dgate, dfc, dfr = fox_attn_bwd(sv["qk"], proj, sv["fcol"], sv["frow"], sv["o"], sv["lse"], dcat)
        dtail, dfb = fox_gates_bwd(proj, _vec8(sm["fox_f_bias"]), heads_to_fcum(dfc, dfr))
        dqk, dwqk = fox_prep_bwd(proj, sv["wqk"], dq, dk)
        small["fox_f_bias"] = dfb[:, :N_HEADS]
        small["fox_q_norm_w"], small["fox_k_norm_w"] = dwqk[0], dwqk[1]
        dproj = jnp.concatenate([dqk, dv.astype(MM), dgate, dqm, dtail, pad], axis=1)
    w_in = wt["dn_w_in"] if i == 0 else wt["fox_w_in"]
    dh = matmul(dproj, w_in, tb=True, name=tag + "d_h", tm=512)
    big["w_in"] = matmul(sv["h"], dproj, ta=True, name=tag + "d_w_in", tm=256)
    dx_in, small["norm1_w"] = rms_bwd(sv["x_in"], sm["norm1_w"][i][None], dh, dx_mid, name=tag + "rms1_bwd")
    return dx_in, big, small, (dmk, dmv)


def local_step(x, mem, target, wt, sm, late=None, on_layer1=None, on_mlp0=None, on_core0=None):
    wt = dict(wt)
    mem_k, mem_v = mem_fwd(mem, sm["mem_norm_w"][None], wt["w_mem_kv"], sm["mem_k_norm_w"][None])
    x0, sv0 = _layer_fwd(0, x, wt, sm, mem_k, mem_v, late)
    x1, sv1 = _layer_fwd(1, x0, wt, sm, mem_k, mem_v)
    dy, loss = loss_fwd(x1, target, name="loss")
    dx1, big1, small1, dm1 = _layer_bwd(1, dy, sv1, wt, sm, mem_k, mem_v)
    if on_layer1 is not None:
        dx1 = dx1 + on_layer1(big1)
    dx0, big0, small0, dm0 = _layer_bwd(0, dx1, sv0, wt, sm, mem_k, mem_v, on_mlp0, on_core0)
    dwn, dwkv, dwkn = mem_bwd(mem, sm["mem_norm_w"][None], wt["w_mem_kv"], sm["mem_k_norm_w"][None], *dm0, *dm1)
    small = dict(mem_norm_w=dwn[0], mem_k_norm_w=dwkn[0],
                 norm1_w=jnp.concatenate([small0["norm1_w"], small1["norm1_w"]]),
                 norm2_w=jnp.concatenate([small0["norm2_w"], small1["norm2_w"]]),
                 memq_norm_w=jnp.concatenate([small0["memq_norm_w"], small1["memq_norm_w"]]),
                 dn_a_log=small0["dn_a_log"], dn_dt_bias=small0["dn_dt_bias"], dn_o_norm_w=small0["dn_o_norm_w"],
                 fox_f_bias=small1["fox_f_bias"], fox_q_norm_w=small1["fox_q_norm_w"], fox_k_norm_w=small1["fox_k_norm_w"])
    big = dict(w_mem_kv=dwkv, dn_w_in=big0["w_in"], fox_w_in=big1["w_in"], conv_w=big0["conv_w"],
               w_out=[big0["w_out"], big1["w_out"]], w_mlp1=[big0["w_mlp1"], big1["w_mlp1"]],
               w_mlp2=[big0["w_mlp2"], big1["w_mlp2"]])
    return loss, dx0, big, small


def w_in_to_kernel(w, n_scalars):
    pad = jnp.zeros((w.shape[0], PROJ_W - TAIL - n_scalars), w.dtype)
    return jnp.concatenate([w[:, :4096], w[:, 4096 + n_scalars:], w[:, 4096:4096 + n_scalars], pad], axis=1)


def w_in_from_kernel(w, n_scalars):
    return jnp.concatenate([w[:, :4096], w[:, TAIL:TAIL + n_scalars], w[:, 4096:TAIL]], axis=1)


BIG_SPECS = dict(w_mem_kv=("rows", 1, 256, 1024), w_out=("rows", 2, 384, 1024), w_mlp2=("rows", 2, 1024, 1024),
                 w_mlp1=("cols", 2, 1024, 1024), dn_w_in=("rows", 1, 1024, 1156), fox_w_in=("rows", 1, 1024, 1154))
BIG_NAMES = tuple(BIG_SPECS)
EARLY_NAMES = ("w_mem_kv", "dn_w_in")
LATE_NAMES = ("w_out", "w_mlp2", "w_mlp1", "fox_w_in")
BIG_SPECS.update({f"{name}_{i}": (BIG_SPECS[name][0], 1) + BIG_SPECS[name][2:]
                  for name in ("w_out", "w_mlp2", "w_mlp1") for i in range(2)})
RS_LAYER1 = ("fox_w_in", "w_out_1", "w_mlp2_1", "w_mlp1_1")
RS_MLP0 = ("w_mlp2_0", "w_mlp1_0", "w_out_0")
RS_LAST = ("dn_w_in", "w_mem_kv")


def _full_shape(name, half=False):
    kind, a, b, c = BIG_SPECS[name]
    b = b // 2 if half else b
    return (a, N_CHIP, b, c) if kind == "rows" else (a, b, N_CHIP * c)


def _ds(start, size, align):
    return pl.ds(start if isinstance(start, int) else pl.multiple_of(start, align), size)


def _half_rows(name, h):
    b = BIG_SPECS[name][2]
    return _ds(h * (b // 2), b // 2, 16)


def _shard_idx(name, h):
    return (slice(None), _half_rows(name, h), slice(None))


def _full_idx(name, j=None, h=None):
    kind, _, _, c = BIG_SPECS[name]
    rows = slice(None) if h is None else _half_rows(name, h)
    if kind == "rows":
        return (slice(None), slice(None) if j is None else j, rows, slice(None))
    return (slice(None), rows, slice(None) if j is None else _ds(j * c, c, 128))


def _slots_shape(name):
    _, a, b, c = BIG_SPECS[name]
    return (a, N_CHIP, b, c)


def _slots_idx(name, j, h):
    return (slice(None), j, _half_rows(name, h), slice(None))


def _row_block(name):
    hs = BIG_SPECS[name][2] // 2
    return hs if hs <= ROWS else ROWS


def _remote(src, dst, send_sem, recv_sem, to):
    return pltpu.make_async_remote_copy(src_ref=src, dst_ref=dst, send_sem=send_sem, recv_sem=recv_sem, device_id=to,
                                        device_id_type=MESH)


HBM = pl.BlockSpec(memory_space=pltpu.HBM)
SEM = pl.BlockSpec(memory_space=pltpu.SEMAPHORE)
EFFECT = pltpu.CompilerParams(has_side_effects=pltpu.SideEffectType.DATAFLOW_SIDE_EFFECTING)


def _in_hbm(a):
    return pltpu.with_memory_space_constraint(a, pltpu.HBM)


def _chip_copies(names, ins, lands, send_sems, recv_sems):
    x, y, c, chips = _place()
    return [_remote(ins[a].at[_shard_idx(name, c)], lands[a].at[_slots_idx(name, 2 * x + y, c)], send_sems.at[3 * a + k],
                    recv_sems.at[3 * a + k], (chip[0], chip[1], c))
            for a, name in enumerate(names) for k, chip in enumerate(chips)]


def _copies_start(call_name, copies, sources, land_shapes, per_source=3, land_dtype=MM):
    n = len(sources)

    def body(*refs):
        ins, lands, send_sems, recv_sems, token = refs[:n], refs[n:2 * n], refs[2 * n], refs[2 * n + 1], refs[-1]
        for cp in copies(ins, lands, send_sems, recv_sems):
            cp.start()
        token[...] = jnp.zeros_like(token)

    ins = [_in_hbm(a) for a in sources]
    lands = [_in_hbm(lax.empty(shape, land_dtype)) for shape in land_shapes]
    sems = (pltpu.SemaphoreType.DMA((per_source * n,)), pltpu.SemaphoreType.DMA((per_source * n,)))
    outs = pl.pallas_call(
        body, name=call_name, in_specs=[HBM] * (2 * n),
        out_specs=(SEM, SEM) + (HBM,) * (2 * n) + (pl.BlockSpec(memory_space=pltpu.VMEM),),
        out_shape=sems + tuple(pltpu.HBM(a.shape, a.dtype) for a in ins + lands) + (jax.ShapeDtypeStruct((8, HEAD_DIM), F32),),
        input_output_aliases={a: 2 + a for a in range(2 * n)}, compiler_params=EFFECT,
    )(*ins, *lands)
    return outs[:-1], outs[-1]


def _copies_wait(call_name, copies, state, after):
    n = (len(state) - 2) // 2

    def body(*refs):
        send_sems, recv_sems, ins, lands = refs[0], refs[1], refs[2:2 + n], refs[2 + n:2 + 2 * n]
        for cp in copies(ins, lands, send_sems, recv_sems):
            cp.wait_send()
            cp.wait_recv()

    outs = pl.pallas_call(
        body, name=call_name, in_specs=[SEM, SEM] + [HBM] * (2 * n) + [ANY], out_specs=(HBM,) * (2 * n),
        out_shape=tuple(pltpu.HBM(a.shape, a.dtype) for a in state[2:]),
        input_output_aliases={2 + a: a for a in range(2 * n)}, compiler_params=EFFECT,
    )(*state, after)
    return outs[:n], outs[n:]


def all_gather_start(shards, names):
    return _copies_start("all_gather_start", functools.partial(_chip_copies, names), [shards[name] for name in names],
                         [_slots_shape(name) for name in names])


def all_gather_wait(state, names, after):
    ins, lands = _copies_wait("all_gather_wait", functools.partial(_chip_copies, names), state, after)
    return dict(zip(names, ins)), dict(zip(names, lands))


def _chip_sends(names, ins, lands, send_sems, recv_sems):
    x, y, c, chips = _place()
    return [_remote(ins[a].at[_full_idx(name, 2 * chip[0] + chip[1])], lands[a].at[k], send_sems.at[3 * a + k],
                    recv_sems.at[3 * a + k], (chip[0], chip[1], c))
            for a, name in enumerate(names) for k, chip in enumerate(chips)]


def _got_shape(name):
    _, a_, b_, c_ = BIG_SPECS[name]
    return (3, a_, b_ // 2, c_)


def rs_chip_start(pairs, names, tag):
    return _copies_start("rs_chip_start_" + tag, functools.partial(_chip_sends, names), [pairs[name] for name in names],
                         [_got_shape(name) for name in names])


def rs_chip_wait(state, names, tag, after):
    _, lands = _copies_wait("rs_chip_wait_" + tag, functools.partial(_chip_sends, names), state, after)
    return dict(zip(names, lands))


def all_gather_pass_on(lands, names):
    n = len(names)

    def body(*refs):
        outs, send_sems, recv_sems = refs[n:2 * n], refs[2 * n], refs[2 * n + 1]
        x, y, c, chips = _place()
        work = [(3 * a + k, a, name, 2 * chip[0] + chip[1]) for a, name in enumerate(names) for k, chip in enumerate(chips)]
        cps = []
        for s, a, name, slot in work:
            landed = outs[a].at[_slots_idx(name, slot, c)]
            cps.append(_remote(landed, landed, send_sems.at[s], recv_sems.at[s], (x, y, 1 - c)))
            cps[-1].start()
        for s, a, name, slot in work:
            passed = outs[a].at[_slots_idx(name, slot, 1 - c)]
            _remote(passed, passed, send_sems.at[s], recv_sems.at[s], (x, y, 1 - c)).wait_recv()
        for cp in cps:
            cp.wait_send()

    outs = pl.pallas_call(
        body, name="all_gather_pass_on", in_specs=[ANY] * n, out_specs=[ANY] * n,
        input_output_aliases={a: a for a in range(n)},
        out_shape=[jax.ShapeDtypeStruct(_slots_shape(name), MM) for name in names],
        scratch_shapes=[pltpu.SemaphoreType.DMA((3 * n,)), pltpu.SemaphoreType.DMA((3 * n,))],
    )(*[lands[name] for name in names])
    return dict(zip(names, outs))


def all_gather_big(shards, names):
    n = len(names)
    BIG_NAMES = names

    def body(*refs):
        ins, outs = refs[:n], refs[n:2 * n]
        send_sems, recv_sems, fsend_sems, frecv_sems = refs[2 * n:]
        x, y, c, chips = _place()
        me_chip, sibling = 2 * x + y, (x, y, 1 - c)
        work = [(3 * a + k, a, name, chip) for a, name in enumerate(BIG_NAMES) for k, chip in enumerate(chips)]
        sends = []
        for s, a, name, chip in work:
            cp = _remote(ins[a].at[_shard_idx(name, c)], outs[a].at[_slots_idx(name, me_chip, c)], send_sems.at[s],
                         recv_sems.at[s], (chip[0], chip[1], c))
            cp.start()
            sends.append(cp)
        for s, a, name, chip in work:
            landed = outs[a].at[_slots_idx(name, 2 * chip[0] + chip[1], c)]
            _remote(landed, landed, send_sems.at[s], recv_sems.at[s], (chip[0], chip[1], c)).wait_recv()
            cp = _remote(landed, landed, fsend_sems.at[s], frecv_sems.at[s], sibling)
            cp.start()
            sends.append(cp)
        for s, a, name, chip in work:
            passed = outs[a].at[_slots_idx(name, 2 * chip[0] + chip[1], 1 - c)]
            _remote(passed, passed, fsend_sems.at[s], frecv_sems.at[s], sibling).wait_recv()
        for cp in sends:
            cp.wait_send()

    outs = pl.pallas_call(
        body, name="all_gather_big", in_specs=[ANY] * n, out_specs=[ANY] * n,
        out_shape=[jax.ShapeDtypeStruct(_slots_shape(name), MM) for name in BIG_NAMES],
        scratch_shapes=[pltpu.SemaphoreType.DMA((3 * n,))] * 4,
    )(*[shards[name] for name in BIG_NAMES])
    return dict(zip(BIG_NAMES, outs))


def with_own_slot(name, full, shard, chip):
    return lax.dynamic_update_slice(full, shard[:, None], (0, chip, 0, 0))


def rs_pair_exchange_big(grads, names, tag):
    n = len(names)

    def body(*refs):
        ins, outs, send_sems, recv_sems = refs[:n], refs[n:2 * n], refs[2 * n], refs[2 * n + 1]
        x, y, c, _ = _place()
        cps = []
        for a, name in enumerate(names):
            cp = _remote(ins[a].at[_full_idx(name, None, 1 - c)], outs[a], send_sems.at[a], recv_sems.at[a], (x, y, 1 - c))
            cp.start()
            cps.append(cp)
        for cp in cps:
            cp.wait()

    outs = pl.pallas_call(
        body, name="rs_pair_exchange_" + tag, in_specs=[ANY] * n, out_specs=[ANY] * n,
        out_shape=[jax.ShapeDtypeStruct(_full_shape(name, half=True), F32) for name in names],
        scratch_shapes=[pltpu.SemaphoreType.DMA((n,)), pltpu.SemaphoreType.DMA((n,))],
    )(*[grads[name] for name in names])
    return dict(zip(names, outs))


def rs_pair_add_big(name, place, g, got):
    kind, a_, b_, c_ = BIG_SPECS[name]
    rb = _row_block(name)
    nb = (b_ // 2) // rb

    def body(place_ref, g_ref, got_ref, o_ref):
        o_ref[...] = (g_ref[...] + got_ref[...]).astype(o_ref.dtype)

    if kind == "rows":
        g_spec = pl.BlockSpec((None, None, rb, c_), lambda a, j, i, p: (a, j, p[0] * nb + i, 0))
        o_spec = pl.BlockSpec((None, None, rb, c_), lambda a, j, i, p: (a, j, i, 0))
    else:
        g_spec = pl.BlockSpec((None, rb, c_), lambda a, j, i, p: (a, p[0] * nb + i, j))
        o_spec = pl.BlockSpec((None, rb, c_), lambda a, j, i, p: (a, i, j))
    return pl.pallas_call(
        body, name="rs_pair_add_" + name,
        grid_spec=pltpu.PrefetchScalarGridSpec(num_scalar_prefetch=1, grid=(a_, N_CHIP, nb), in_specs=[g_spec, o_spec],
                                               out_specs=o_spec),
        out_shape=jax.ShapeDtypeStruct(_full_shape(name, half=True), MM),
        compiler_params=_cparams(("parallel", "parallel", "parallel")),
    )(place, g, got)


def rs_chip_exchange_big(pairs, names, tag):
    n = len(names)

    def body(*refs):
        ins, outs, send_sems, recv_sems = refs[:n], refs[n:2 * n], refs[2 * n], refs[2 * n + 1]
        cps = _chip_sends(names, ins, outs, send_sems, recv_sems)
        for cp in cps:
            cp.start()
        for cp in cps:
            cp.wait()

    outs = pl.pallas_call(
        body, name="rs_chip_exchange_" + tag, in_specs=[ANY] * n, out_specs=[ANY] * n,
        out_shape=[jax.ShapeDtypeStruct(_got_shape(name), MM) for name in names],
        scratch_shapes=[pltpu.SemaphoreType.DMA((3 * n,)), pltpu.SemaphoreType.DMA((3 * n,))],
    )(*[pairs[name] for name in names])
    return dict(zip(names, outs))


def rs_chip_add_big(name, place, g, got_pair, got_chips):
    kind, a_, b_, c_ = BIG_SPECS[name]
    rb = _row_block(name)
    nb = (b_ // 2) // rb

    def body(place_ref, g_ref, s_ref, r0_ref, r1_ref, r2_ref, o_ref):
        own = g_ref[...] + s_ref[...]
        o_ref[...] = ((own + r0_ref[...].astype(F32)) + r1_ref[...].astype(F32)) + r2_ref[...].astype(F32)

    if kind == "rows":
        g_spec = pl.BlockSpec((None, None, rb, c_), lambda a, i, p: (a, p[1], p[0] * nb + i, 0))
        s_spec = pl.BlockSpec((None, None, rb, c_), lambda a, i, p: (a, p[1], i, 0))
    else:
        g_spec = pl.BlockSpec((None, rb, c_), lambda a, i, p: (a, p[0] * nb + i, p[1]))
        s_spec = pl.BlockSpec((None, rb, c_), lambda a, i, p: (a, i, p[1]))
    r_spec = lambda k: pl.BlockSpec((None, None, rb, c_), lambda a, i, p: (k, a, i, 0))
    return pl.pallas_call(
        body, name="rs_chip_add_" + name,
        grid_spec=pltpu.PrefetchScalarGridSpec(
            num_scalar_prefetch=1, grid=(a_, nb), in_specs=[g_spec, s_spec, r_spec(0), r_spec(1), r_spec(2)],
            out_specs=pl.BlockSpec((None, rb, c_), lambda a, i, p: (a, p[0] * nb + i, 0))),
        out_shape=jax.ShapeDtypeStruct((a_, b_, c_), F32), compiler_params=_cparams(("parallel", "parallel")),
    )(place, g, got_pair, got_chips, got_chips, got_chips)


def rs_pair_gather_big(halves, tag):
    names = tuple(halves)
    n = len(names)

    def body(*refs):
        outs, send_sems, recv_sems = refs[n:2 * n], refs[2 * n], refs[2 * n + 1]
        x, y, c, _ = _place()
        cps = []
        for a, name in enumerate(names):
            mine = outs[a].at[_shard_idx(name, c)]
            cp = _remote(mine, mine, send_sems.at[a], recv_sems.at[a], (x, y, 1 - c))
            cp.start()
            cps.append(cp)
        for a, name in enumerate(names):
            cps[a].wait_send()
            theirs = outs[a].at[_shard_idx(name, 1 - c)]
            _remote(theirs, theirs, send_sems.at[a], recv_sems.at[a], (x, y, 1 - c)).wait_recv()

    outs = pl.pallas_call(
        body, name="rs_pair_gather_" + tag, in_specs=[ANY] * n, out_specs=[ANY] * n,
        input_output_aliases={a: a for a in range(n)},
        out_shape=[jax.ShapeDtypeStruct(BIG_SPECS[name][1:], F32) for name in names],
        scratch_shapes=[pltpu.SemaphoreType.DMA((n,)), pltpu.SemaphoreType.DMA((n,))],
    )(*[halves[name] for name in names])
    return dict(zip(names, outs))


def _pair_sends(names, ins, lands, send_sems, recv_sems):
    x, y, c, _ = _place()
    return [_remote(ins[a].at[_full_idx(name, None, 1 - c)], lands[a], send_sems.at[a], recv_sems.at[a], (x, y, 1 - c))
            for a, name in enumerate(names)]


def rs_pair_start(grads, names, tag):
    return _copies_start("rs_pair_start_" + tag, functools.partial(_pair_sends, names), [grads[name] for name in names],
                         [_full_shape(name, half=True) for name in names], per_source=1, land_dtype=F32)


def rs_middle(pair_state, names, tag, place, after):
    ins, lands = _copies_wait("rs_pair_wait_" + tag, functools.partial(_pair_sends, names), pair_state, after)
    grads, got_pair = dict(zip(names, ins)), dict(zip(names, lands))
    pairs = {name: rs_pair_add_big(name, place, grads[name], got_pair[name]) for name in names}
    state, token = rs_chip_start(pairs, names, tag)
    return (grads, got_pair, state), token


def rs_end(begun, names, tag, place, after):
    grads, got_pair, state = begun
    got_chips = rs_chip_wait(state, names, tag, after)
    return {name: rs_chip_add_big(name, place, grads[name], got_pair[name], got_chips[name]) for name in names}


def rs_whole(grads, names, tag, place):
    got_pair = rs_pair_exchange_big(grads, names, tag)
    pairs = {name: rs_pair_add_big(name, place, grads[name], got_pair[name]) for name in names}
    got_chips = rs_chip_exchange_big(pairs, names, tag)
    return rs_pair_gather_big({name: rs_chip_add_big(name, place, grads[name], got_pair[name], got_chips[name])
                               for name in names}, tag)


PACK_W = 1024
SMALL =(("mem_norm_w", 1024), ("mem_k_norm_w", 128), ("norm1_w", 2048), ("dn_a_log", 8), ("dn_dt_bias", 8),
         ("dn_o_norm_w", 128), ("fox_f_bias", 8), ("fox_q_norm_w", 128), ("fox_k_norm_w", 128), ("memq_norm_w", 256),
         ("norm2_w", 2048))
SMALL_ROWS = 8
CONV_ROWS = 4 * 3 * D_MODEL // PACK_W
LOSS_AT = sum(n for _, n in SMALL)


def pack_small(parts, extra=None):
    flat = [parts[name].astype(F32).reshape(-1) for name, _ in SMALL]
    used = LOSS_AT
    if extra is not None:
        flat.append(extra.reshape(1))
        used += 1
    flat.append(jnp.zeros((SMALL_ROWS * PACK_W - used,), F32))
    return jnp.concatenate(flat).reshape(SMALL_ROWS, PACK_W)


def unpack_small(packed, shapes):
    flat, out, at = packed.reshape(-1), {}, 0
    for name, n in SMALL:
        out[name] = flat[at:at + n].reshape(shapes[name])
        at += n
    return out


def _adam_all(w, g, m, v, name):
    shape = w.shape
    r2 = lambda a: a.reshape(-1, shape[-1])
    d, nm, nv = adamw(r2(w), r2(g), r2(m), r2(v), name=name)
    return d.reshape(shape), nm.reshape(shape), nv.reshape(shape)


BIG = ("w_mem_kv", "dn_w_in", "dn_conv_w", "fox_w_in", "w_out", "w_mlp1", "w_mlp2")
WEIGHTS = ("mem_norm_w", "w_mem_kv", "mem_k_norm_w", "norm1_w", "dn_w_in", "dn_conv_w", "dn_a_log", "dn_dt_bias",
           "dn_o_norm_w", "fox_w_in", "fox_f_bias", "fox_q_norm_w", "fox_k_norm_w", "memq_norm_w", "w_out", "norm2_w",
           "w_mlp1", "w_mlp2")


def kernel(x, mem, mem_norm_w, w_mem_kv, mem_k_norm_w, norm1_w, dn_w_in, dn_conv_w, dn_a_log, dn_dt_bias, dn_o_norm_w, fox_w_in, fox_f_bias, fox_q_norm_w, fox_k_norm_w, memq_norm_w, w_out, norm2_w, w_mlp1, w_mlp2, loss_target, m_mem_norm_w, m_w_mem_kv, m_mem_k_norm_w, m_norm1_w, m_dn_w_in, m_dn_conv_w, m_dn_a_log, m_dn_dt_bias, m_dn_o_norm_w, m_fox_w_in, m_fox_f_bias, m_fox_q_norm_w, m_fox_k_norm_w, m_memq_norm_w, m_w_out, m_norm2_w, m_w_mlp1, m_w_mlp2, v_mem_norm_w, v_w_mem_kv, v_mem_k_norm_w, v_norm1_w, v_dn_w_in, v_dn_conv_w, v_dn_a_log, v_dn_dt_bias, v_dn_o_norm_w, v_fox_w_in, v_fox_f_bias, v_fox_q_norm_w, v_fox_k_norm_w, v_memq_norm_w, v_w_out, v_norm2_w, v_w_mlp1, v_w_mlp2):
    args = dict(locals())
    w = {n: args[n] for n in WEIGHTS}
    m = {n: args["m_" + n] for n in WEIGHTS}
    v = {n: args["v_" + n] for n in WEIGHTS}
    core, chip = lax.axis_index("c"), 2 * lax.axis_index("x") + lax.axis_index("y")
    place = jnp.stack([core, chip]).astype(jnp.int32)

    shards = {name: w[name].reshape(BIG_SPECS[name][1:]).astype(MM) for name in BIG_NAMES}
    w_in_full = lambda arr, n_scalars: w_in_to_kernel(arr[0].transpose(1, 0, 2).reshape(D_MODEL, -1), n_scalars)
    early = {name: with_own_slot(name, arr, shards[name], chip)
             for name, arr in all_gather_big(shards, EARLY_NAMES).items()}
    conv_mine = jnp.where(core == 0, dn_conv_w[0], 0.0)
    conv_placed = lax.dynamic_update_slice(jnp.zeros((4, 3 * D_MODEL), F32), conv_mine, (0, 768 * chip))
    conv_full = all_reduce_small(jnp.pad(conv_placed.reshape(CONV_ROWS, PACK_W), ((0, 16 - CONV_ROWS), (0, 0))))
    late_shards, early, conv_full = lax.optimization_barrier(
        ({name: shards[name] for name in LATE_NAMES}, early, conv_full))
    late_state, token = all_gather_start(late_shards, LATE_NAMES)
    tie = token[0, 0]
    wt = dict(w_mem_kv=early["w_mem_kv"].reshape(D_MODEL, 2 * MEM_WIDTH) + tie.astype(MM),
              dn_w_in=w_in_full(early["dn_w_in"], 2 * N_HEADS), conv_w=conv_full[:CONV_ROWS].reshape(4, 3 * D_MODEL))

    def late(after):
        late_shards, lands = all_gather_wait(late_state, LATE_NAMES, after)
        full = {name: with_own_slot(name, arr, late_shards[name], chip)
                for name, arr in all_gather_pass_on(lands, LATE_NAMES).items()}
        return dict(fox_w_in=w_in_full(full["fox_w_in"], N_HEADS), w_out=full["w_out"].reshape(2, 3 * MEM_WIDTH, D_MODEL),
                    w_mlp1=full["w_mlp1"], w_mlp2=full["w_mlp2"].reshape(2, D_FF, D_MODEL))

    sm = dict(mem_norm_w=mem_norm_w, mem_k_norm_w=mem_k_norm_w, norm1_w=norm1_w, norm2_w=norm2_w, memq_norm_w=memq_norm_w,
              dn_a_log=dn_a_log[0], dn_dt_bias=dn_dt_bias[0], dn_o_norm_w=dn_o_norm_w, fox_f_bias=fox_f_bias[0],
              fox_q_norm_w=fox_q_norm_w, fox_k_norm_w=fox_k_norm_w)
    w_in_slots = lambda g, n_scalars: w_in_from_kernel(g, n_scalars).reshape(D_MODEL, N_CHIP, -1).transpose(1, 0, 2)[None]
    rows_view = lambda g, name: g.reshape(_full_shape(name))
    pair_started, begun = {}, {}

    def on_layer1(big1):
        grads1 = dict(fox_w_in=w_in_slots(big1["w_in"], N_HEADS), w_out_1=rows_view(big1["w_out"], "w_out_1"),
                      w_mlp2_1=rows_view(big1["w_mlp2"], "w_mlp2_1"), w_mlp1_1=big1["w_mlp1"][None])
        pair_started["layer1"], token = rs_pair_start(grads1, RS_LAYER1, "layer1")
        return token[0, 0]

    def on_mlp0(d_w_mlp2, d_w_mlp1, d_w_out):
        begun["layer1"], token1 = rs_middle(pair_started["layer1"], RS_LAYER1, "layer1", place, d_w_out)
        grads0 = dict(w_mlp2_0=rows_view(d_w_mlp2, "w_mlp2_0"), w_mlp1_0=d_w_mlp1[None],
                      w_out_0=rows_view(d_w_out, "w_out_0"))
        grads0, _ = lax.optimization_barrier((grads0, token1))
        pair_started["mlp0"], token0 = rs_pair_start(grads0, RS_MLP0, "mlp0")
        return token1[0, 0] + token0[0, 0]

    def on_core0(d_o):
        begun["mlp0"], token = rs_middle(pair_started["mlp0"], RS_MLP0, "mlp0", place, d_o)
        return token[0, 0]

    loss_part, dx, big, small = local_step(x[0], mem[0], loss_target[0], wt, sm, late, on_layer1, on_mlp0, on_core0)
    small_pack = jnp.concatenate([pack_small(small, loss_part[0, :1]), big["conv_w"].reshape(CONV_ROWS, PACK_W),
                                  jnp.zeros((24 - SMALL_ROWS - CONV_ROWS, PACK_W), F32)])
    small_all = all_reduce_small(small_pack)
    small_sum = small_all[:SMALL_ROWS]
    conv_sum = lax.dynamic_slice(small_all[SMALL_ROWS:SMALL_ROWS + CONV_ROWS].reshape(4, 3 * D_MODEL), (0, 768 * chip), (4, 768))
    loss = small_sum.reshape(-1)[LOSS_AT]
    halves = rs_end(begun["layer1"], RS_LAYER1, "layer1", place, small_all)
    halves.update(rs_end(begun["mlp0"], RS_MLP0, "mlp0", place, small_all))
    summed = rs_pair_gather_big(halves, "early")

    last = dict(dn_w_in=w_in_slots(big["dn_w_in"], 2 * N_HEADS), w_mem_kv=rows_view(big["w_mem_kv"], "w_mem_kv"))
    last, summed = lax.optimization_barrier((last, summed))
    got_pair = rs_pair_exchange_big(last, RS_LAST, "last")
    pairs = {name: rs_pair_add_big(name, place, last[name], got_pair[name]) for name in RS_LAST}
    last_state, token = rs_chip_start(pairs, RS_LAST, "last")
    summed, token = lax.optimization_barrier((summed, token))

    big_sum = {"fox_w_in": summed["fox_w_in"]}
    big_sum.update({name: jnp.concatenate([summed[name + "_0"], summed[name + "_1"]]) for name in ("w_out", "w_mlp2", "w_mlp1")})
    grads = unpack_small(small_sum, {n: w[n].shape for n, _ in SMALL})
    grads.update({name: big_sum[name].reshape(w[name].shape) for name in big_sum}, dn_conv_w=conv_sum[None])
    delta, new_m, new_v = {}, {}, {}
    for n in ("fox_w_in", "w_out", "w_mlp1", "w_mlp2", "dn_conv_w"):
        delta[n], new_m[n], new_v[n] = _adam_all(w[n], grads[n], m[n], v[n], "adamw_" + n)

    got_chips = rs_chip_wait(last_state, RS_LAST, "last", delta["w_mlp2"])
    summed_last = rs_pair_gather_big({name: rs_chip_add_big(name, place, last[name], got_pair[name], got_chips[name])
                                      for name in RS_LAST}, "last")
    for n in RS_LAST:
        grads[n] = summed_last[n].reshape(w[n].shape)
        delta[n], new_m[n], new_v[n] = _adam_all(w[n], grads[n], m[n], v[n], "adamw_" + n)
    shapes = {n: w[n].shape for n, _ in SMALL}
    d_s, m_s, v_s = adamw(pack_small(w), small_sum, pack_small(m), pack_small(v), name="adamw_small")
    for out, packed in ((delta, d_s), (new_m, m_s), (new_v, v_s)):
        out.update(unpack_small(packed, shapes))
    return (loss, dx[None], *[grads[n] for n in WEIGHTS], *[delta[n] for n in WEIGHTS],
            *[new_m[n] for n in WEIGHTS], *[new_v[n] for n in WEIGHTS])
```

```python
import functools

import jax
import jax.numpy as jnp
from jax import lax
from jax.experimental import pallas as pl
from jax.experimental.pallas import tpu as pltpu

F32 = jnp.float32
MM = jnp.bfloat16
HI = lax.Precision.HIGHEST

D_MODEL = 1024
HEAD_DIM = 128
N_HEADS = 8
MEM_HEADS = 4
MEM_WIDTH = MEM_HEADS * HEAD_DIM
N_MEM = 256
D_FF = 4 * D_MODEL
CHUNK = 64
EPS = 1e-6
QSCALE = HEAD_DIM ** -0.5
PROJ_W = 4736
TAIL = 4608
TAIL_BLK = TAIL // HEAD_DIM
ROWS = 256
VMEM_LIMIT = 56 * 1024 * 1024

ADAM_LR = 0.001
ADAM_B1 = 0.9
ADAM_B2 = 0.999
ADAM_EPS = 1e-08
ADAM_WD = 0.01
ADAM_STEP = 10

N_DEV = 8
N_CHIP = 4
MESH = pl.DeviceIdType.MESH


def _cparams(sem=None):
    return pltpu.CompilerParams(dimension_semantics=sem, vmem_limit_bytes=VMEM_LIMIT)


def _dot(a, b, ca, cb, hi):
    dims = (((ca,), (cb,)), ((), ()))
    if hi:
        return lax.dot_general(a, b, dims, precision=HI, preferred_element_type=F32)
    return lax.dot_general(a.astype(MM), b.astype(MM), dims, preferred_element_type=F32)


@functools.partial(jax.custom_vjp, nondiff_argnums=(2, 3, 4))
def mmul(a, b, ca, cb, hi):
    return _dot(a, b, ca, cb, hi)


def _mmul_fwd(a, b, ca, cb, hi):
    return _dot(a, b, ca, cb, hi), (a, b)


def _mmul_bwd(ca, cb, hi, res, g):
    a, b = res
    if ca == 1:
        da = _dot(g, b, 1, 1, hi) if cb == 0 else _dot(g, b, 1, 0, hi)
    else:
        da = _dot(b, g, 1, 1, hi) if cb == 0 else _dot(b, g, 0, 1, hi)
    if cb == 0:
        db = _dot(a, g, 0, 0, hi) if ca == 1 else _dot(a, g, 1, 0, hi)
    else:
        db = _dot(g, a, 0, 0, hi) if ca == 1 else _dot(g, a, 0, 1, hi)
    return da.astype(a.dtype), db.astype(b.dtype)


mmul.defvjp(_mmul_fwd, _mmul_bwd)


def _iota2(n, m):
    return lax.broadcasted_iota(jnp.int32, (n, m), 0), lax.broadcasted_iota(jnp.int32, (n, m), 1)


def _same_block(r, c, shift):
    return lax.shift_right_logical(r, shift) == lax.shift_right_logical(c, shift)


def _split_bf16(x):
    hi = x.astype(jnp.bfloat16)
    return hi, (x - hi.astype(F32)).astype(jnp.bfloat16)


def _dot3(a, b, ca, cb):
    dims = (((ca,), (cb,)), ((), ()))
    (ah, al), (bh, bl) = _split_bf16(a), _split_bf16(b)
    d = lambda x, y: lax.dot_general(x, y, dims, preferred_element_type=F32)
    return d(ah, bh) + (d(ah, bl) + d(al, bh))


def _tri_inv_impl(a):
    n = a.shape[0]
    r, c = _iota2(n, n)
    eye = (r == c).astype(F32)
    b16, b32 = _same_block(r, c, 4), _same_block(r, c, 5)
    a0 = jnp.where(b16, a, 0.0)
    p = eye - a0
    b = _dot3(a0, a0, 1, 0)
    p = p + _dot3(p, b, 1, 0)
    b = _dot3(b, b, 1, 0)
    p = p + _dot3(p, b, 1, 0)
    b = _dot3(b, b, 1, 0)
    p = p + _dot3(p, b, 1, 0)
    a1 = jnp.where(jnp.logical_and(b32, jnp.logical_not(b16)), a, 0.0)
    p = p - _dot3(_dot3(p, a1, 1, 0), p, 1, 0)
    a2 = jnp.where(b32, 0.0, a)
    p = p - _dot3(_dot3(p, a2, 1, 0), p, 1, 0)
    return p


@jax.custom_vjp
def tri_inv(a):
    return _tri_inv_impl(a)


def _tri_inv_fwd(a):
    p = _tri_inv_impl(a)
    return p, p


def _tri_inv_bwd(p, g):
    return (-_dot3(_dot3(p, g, 0, 0), p, 1, 1),)


tri_inv.defvjp(_tri_inv_fwd, _tri_inv_bwd)


def _sigmoid(x):
    return 1.0 / (1.0 + jnp.exp(-x))


def _softplus(x):
    return jnp.maximum(x, 0.0) + jnp.log(1.0 + jnp.exp(-jnp.abs(x)))


def _silu(x):
    return x * _sigmoid(x)


def _rms(x, w):
    return x * lax.rsqrt(jnp.mean(x * x, axis=-1, keepdims=True) + EPS) * w


def _bf_round(x):
    return x.astype(MM).astype(F32)


def _acc(ref, val, first):
    @pl.when(first)
    def _():
        ref[...] = val

    @pl.when(jnp.logical_not(first))
    def _():
        ref[...] += val


def _tile(n, pref):
    if n % pref == 0:
        return pref
    return n


def matmul(a, b, *, ta=False, tb=False, b_slots=False, b_layer=None, res=None, also_sqrelu=False, times_dsqrelu=None,
           out_dtype=F32, name, tm=1024, tn=1024, tk=1024):
    m, k = (a.shape[1], a.shape[0]) if ta else a.shape
    b_shape = b.shape if b_layer is None else b.shape[1:]
    if b_slots:
        n = b_shape[1] if tb else N_CHIP * b_shape[2]
        assert (N_CHIP * b_shape[2] if tb else b_shape[1]) == k, (a.shape, b.shape, ta, tb)
        tn, tk = (tn, b_shape[2]) if tb else (b_shape[2], tk)
    else:
        n = b_shape[0] if tb else b_shape[1]
        assert (b_shape[1] if tb else b_shape[0]) == k, (a.shape, b.shape, ta, tb)
    tm, tn, tk = _tile(m, tm), _tile(n, tn), _tile(k, tk)
    nk = k // tk
    ca, cb = (0 if ta else 1), (1 if tb else 0)

    extra = tuple(e for e in (res, times_dsqrelu) if e is not None)
    assert len(extra) <= 1

    def body(a_ref, b_ref, *rest):
        e_ref = rest[0] if extra else None
        o_ref = rest[len(extra)]

        def finish(total):
            if res is not None:
                total = total + e_ref[...]
            if times_dsqrelu is not None:
                total = total * (2.0 * jnp.maximum(e_ref[...], 0.0))
            o_ref[...] = total.astype(o_ref.dtype)
            if also_sqrelu:
                rest[len(extra) + 1][...] = _sqrelu(total).astype(MM)

        if nk == 1:
            finish(_dot(a_ref[...], b_ref[...], ca, cb, False))
            return
        acc_ref, kk = rest[-1], pl.program_id(2)

        @pl.when(kk == 0)
        def _():
            acc_ref[...] = jnp.zeros_like(acc_ref)

        acc_ref[...] += _dot(a_ref[...], b_ref[...], ca, cb, False)

        @pl.when(kk == nk - 1)
        def _():
            finish(acc_ref[...])

    a_spec = pl.BlockSpec((tk, tm), lambda i, j, l: (l, i)) if ta else pl.BlockSpec((tm, tk), lambda i, j, l: (i, l))
    lead = () if b_layer is None else (b_layer,)
    if b_slots:
        b_block, b_index = ((None, tn, tk), lambda i, j, l: (l, j, 0)) if tb else ((None, tk, tn), lambda i, j, l: (j, l, 0))
    else:
        b_block, b_index = ((tn, tk), lambda i, j, l: (j, l)) if tb else ((tk, tn), lambda i, j, l: (l, j))
    b_spec = pl.BlockSpec((None,) * len(lead) + b_block, lambda i, j, l: lead + b_index(i, j, l))
    o_spec = pl.BlockSpec((tm, tn), lambda i, j, l: (i, j))
    out_shape = [jax.ShapeDtypeStruct((m, n), out_dtype)] + [jax.ShapeDtypeStruct((m, n), MM)] * also_sqrelu
    outs = pl.pallas_call(
        body, name=name, grid=(m // tm, n // tn, nk),
        in_specs=[a_spec, b_spec] + [o_spec] * len(extra), out_specs=[o_spec] * len(out_shape), out_shape=out_shape,
        scratch_shapes=[pltpu.VMEM((tm, tn), F32)] * (nk > 1),
        compiler_params=_cparams(("parallel", "parallel", "arbitrary")),
    )(a, b, *extra)
    return outs if also_sqrelu else outs[0]


def rms_fwd(x, w, *, name):
    t, d = x.shape

    def body(x_ref, w_ref, o_ref):
        o_ref[...] = _rms(x_ref[...], w_ref[...]).astype(o_ref.dtype)

    return pl.pallas_call(
        body, name=name, grid=(t // ROWS,),
        in_specs=[pl.BlockSpec((ROWS, d), lambda i: (i, 0)), pl.BlockSpec((1, d), lambda i: (0, 0))],
        out_specs=pl.BlockSpec((ROWS, d), lambda i: (i, 0)),
        out_shape=jax.ShapeDtypeStruct((t, d), MM), compiler_params=_cparams(("parallel",)),
    )(x, w)


def rms_bwd(x, w, dh, dres, *, name):
    t, d = x.shape

    def body(x_ref, w_ref, dh_ref, dr_ref, dx_ref, dw_ref):
        _, vjp = jax.vjp(_rms, x_ref[...], w_ref[...])
        dx, dw = vjp(dh_ref[...].astype(F32))
        dx_ref[...] = dx + dr_ref[...]
        _acc(dw_ref, dw, pl.program_id(0) == 0)

    row = pl.BlockSpec((ROWS, d), lambda i: (i, 0))
    vec = pl.BlockSpec((1, d), lambda i: (0, 0))
    return pl.pallas_call(
        body, name=name, grid=(t // ROWS,), in_specs=[row, vec, row, row], out_specs=[row, vec],
        out_shape=[jax.ShapeDtypeStruct((t, d), F32), jax.ShapeDtypeStruct((1, d), F32)],
        compiler_params=_cparams(("arbitrary",)),
    )(x, w, dh, dres)


def _sqrelu(x):
    return jnp.square(jnp.maximum(x, 0.0))


def loss_fwd(y, target, *, name):
    t, d = y.shape

    def body(y_ref, t_ref, dy_ref, l_ref):
        e = y_ref[...] - t_ref[...]
        dy_ref[...] = e * (1.0 / d)
        part = 0.5 * jnp.sum(jnp.sum(e * e, axis=-1, keepdims=True) * (1.0 / d), axis=0, keepdims=True)
        _acc(l_ref, jnp.broadcast_to(part, (1, HEAD_DIM)), pl.program_id(0) == 0)

    blk = pl.BlockSpec((ROWS, d), lambda i: (i, 0))
    return pl.pallas_call(
        body, name=name, grid=(t // ROWS,), in_specs=[blk, blk],
        out_specs=[blk, pl.BlockSpec((1, HEAD_DIM), lambda i: (0, 0))],
        out_shape=[jax.ShapeDtypeStruct((t, d), F32), jax.ShapeDtypeStruct((1, HEAD_DIM), F32)],
        compiler_params=_cparams(("arbitrary",)),
    )(y, target)


def _mem_kv(mem, wn, wkn, *ws):
    mn = _rms(mem, wn)
    outs = []
    for h in range(MEM_HEADS):
        outs.append(_rms(mmul(mn, ws[h], 1, 0, False), wkn))
    for h in range(MEM_HEADS):
        outs.append(mmul(mn, ws[MEM_HEADS + h], 1, 0, False))
    return tuple(outs)


def _w_cols(w_ref):
    return [w_ref[:, h * HEAD_DIM:(h + 1) * HEAD_DIM] for h in range(2 * MEM_HEADS)]


def mem_fwd(mem, wn, wkv, wkn):
    def body(mem_ref, wn_ref, w_ref, wkn_ref, k_ref, v_ref):
        outs = _mem_kv(mem_ref[...], wn_ref[...], wkn_ref[...], *_w_cols(w_ref))
        for h in range(MEM_HEADS):
            k_ref[:, h * HEAD_DIM:(h + 1) * HEAD_DIM] = outs[h]
            v_ref[:, h * HEAD_DIM:(h + 1) * HEAD_DIM] = outs[MEM_HEADS + h]

    shp = jax.ShapeDtypeStruct((mem.shape[0], MEM_WIDTH), F32)
    return pl.pallas_call(body, name="mem_fwd", out_shape=[shp, shp], compiler_params=_cparams())(mem, wn, wkv, wkn)


def mem_bwd(mem, wn, wkv, wkn, dk0, dv0, dk1, dv1):
    def body(mem_ref, wn_ref, w_ref, wkn_ref, dk0_ref, dv0_ref, dk1_ref, dv1_ref, dwn_ref, dw_ref, dwkn_ref):
        _, vjp = jax.vjp(lambda wn_, wkn_, *ws: _mem_kv(mem_ref[...], wn_, wkn_, *ws),
                         wn_ref[...], wkn_ref[...], *[w.astype(F32) for w in _w_cols(w_ref)])
        cols = lambda a, b: tuple(a[:, h * HEAD_DIM:(h + 1) * HEAD_DIM] + b[:, h * HEAD_DIM:(h + 1) * HEAD_DIM]
                                  for h in range(MEM_HEADS))
        cts = cols(dk0_ref, dk1_ref) + cols(dv0_ref, dv1_ref)
        grads = vjp(cts)
        dwn_ref[...] = grads[0]
        dwkn_ref[...] = grads[1]
        for h in range(2 * MEM_HEADS):
            dw_ref[:, h * HEAD_DIM:(h + 1) * HEAD_DIM] = grads[2 + h]

    return pl.pallas_call(
        body, name="mem_bwd",
        out_shape=[jax.ShapeDtypeStruct((1, D_MODEL), F32), jax.ShapeDtypeStruct((D_MODEL, 2 * MEM_WIDTH), F32),
                   jax.ShapeDtypeStruct((1, HEAD_DIM), F32)],
        compiler_params=_cparams(),
    )(mem, wn, wkv, wkn, dk0, dv0, dk1, dv1)


def _memattn(q, wq, mk, mv):
    qn = _rms(q, wq) * QSCALE
    s = mmul(qn, mk, 1, 1, False)
    s = s - jnp.max(s, axis=-1, keepdims=True)
    p = jnp.exp(s)
    p = p / jnp.sum(p, axis=-1, keepdims=True)
    return mmul(p, mv, 1, 0, False)


def _lanes(j):
    return slice(j * HEAD_DIM, (j + 1) * HEAD_DIM)


def _memattn_specs(t):
    qspec = pl.BlockSpec((ROWS, MEM_WIDTH), lambda i: (i, (TAIL - MEM_WIDTH) // MEM_WIDTH))
    wspec = pl.BlockSpec((1, HEAD_DIM), lambda i: (0, 0))
    mspec = pl.BlockSpec((N_MEM, MEM_WIDTH), lambda i: (0, 0))
    ospec = pl.BlockSpec((ROWS, MEM_WIDTH), lambda i: (i, 0))
    return qspec, wspec, mspec, ospec


def memattn_fwd(proj, wq, mk, mv, *, name):
    t = proj.shape[0]
    qspec, wspec, mspec, ospec = _memattn_specs(t)

    def body(q_ref, w_ref, k_ref, v_ref, o_ref):
        for h in range(MEM_HEADS):
            o_ref[:, _lanes(h)] = _memattn(q_ref[:, _lanes(h)], w_ref[...], k_ref[:, _lanes(h)],
                                           v_ref[:, _lanes(h)]).astype(o_ref.dtype)

    return pl.pallas_call(
        body, name=name, grid=(t // ROWS,), in_specs=[qspec, wspec, mspec, mspec], out_specs=ospec,
        out_shape=jax.ShapeDtypeStruct((t, MEM_WIDTH), MM), compiler_params=_cparams(("parallel",)),
    )(proj, wq, mk, mv)


def memattn_bwd(proj, wq, mk, mv, dcat, *, name):
    t = proj.shape[0]
    qspec, wspec, mspec, ospec = _memattn_specs(t)
    dospec = pl.BlockSpec((ROWS, MEM_WIDTH), lambda i: (i, D_MODEL // MEM_WIDTH))

    def body(q_ref, w_ref, k_ref, v_ref, do_ref, dq_ref, dw_ref, dk_ref, dv_ref):
        first = pl.program_id(0) == 0
        dw_sum = jnp.zeros((1, HEAD_DIM), F32)
        for h in range(MEM_HEADS):
            _, vjp = jax.vjp(_memattn, q_ref[:, _lanes(h)], w_ref[...], k_ref[:, _lanes(h)], v_ref[:, _lanes(h)])
            dq, dw, dk, dv = vjp(do_ref[:, _lanes(h)].astype(F32))
            dq_ref[:, _lanes(h)] = dq.astype(dq_ref.dtype)
            dw_sum = dw_sum + dw
            _acc(dk_ref.at[:, _lanes(h)], dk, first)
            _acc(dv_ref.at[:, _lanes(h)], dv, first)
        _acc(dw_ref, dw_sum, first)

    mshape = jax.ShapeDtypeStruct((N_MEM, MEM_WIDTH), F32)
    return pl.pallas_call(
        body, name=name, grid=(t // ROWS,), in_specs=[qspec, wspec, mspec, mspec, dospec],
        out_specs=[ospec, wspec, mspec, mspec],
        out_shape=[jax.ShapeDtypeStruct((t, MEM_WIDTH), MM), jax.ShapeDtypeStruct((1, HEAD_DIM), F32), mshape, mshape],
        compiler_params=_cparams(("arbitrary",)),
    )(proj, wq, mk, mv, dcat)


def _shift_rows(x, s, up):
    n = x.shape[0]
    r = lax.broadcasted_iota(jnp.int32, x.shape, 0)
    if up:
        return jnp.where(r < n - s, pltpu.roll(x, n - s, 0), 0.0)
    return jnp.where(r >= s, pltpu.roll(x, s, 0), 0.0)


def _conv_fwd_vals(x, w):
    xb = _bf_round(x)
    wb = _bf_round(w)
    c = xb * wb[3:4, :]
    for j in range(3):
        c = c + _shift_rows(xb, 3 - j, False) * wb[j:j + 1, :]
    return xb, wb, c


def dn_prep_fwd(proj, conv_w):
    t = proj.shape[0]

    def body(x_ref, w_ref, o_ref):
        j = pl.program_id(0)
        _, _, c = _conv_fwd_vals(x_ref[...], w_ref[...])
        s = _silu(c)
        r = lax.rsqrt(jnp.sum(s * s, axis=-1, keepdims=True) + EPS)
        scale = jnp.where(j < N_HEADS, QSCALE, 1.0)
        o_ref[...] = jnp.where(j < 2 * N_HEADS, s * r * scale, s)

    return pl.pallas_call(
        body, name="dn_prep_fwd", grid=(3 * N_HEADS,),
        in_specs=[pl.BlockSpec((t, HEAD_DIM), lambda j: (0, j)), pl.BlockSpec((4, HEAD_DIM), lambda j: (0, j))],
        out_specs=pl.BlockSpec((None, t, HEAD_DIM), lambda j: (j // N_HEADS, 0, j % N_HEADS)),
        out_shape=jax.ShapeDtypeStruct((3, t, D_MODEL), F32), compiler_params=_cparams(("parallel",)),
    )(proj, conv_w)


def dn_prep_bwd(proj, conv_w, dqkv):
    t = proj.shape[0]

    def body(x_ref, w_ref, g_ref, dx_ref, dw_ref):
        j = pl.program_id(0)
        xb, wb, c = _conv_fwd_vals(x_ref[...], w_ref[...])
        sg = _sigmoid(c)
        s = c * sg
        g = g_ref[...]
        r = lax.rsqrt(jnp.sum(s * s, axis=-1, keepdims=True) + EPS)
        scale = jnp.where(j < N_HEADS, QSCALE, 1.0)
        gn = g * scale
        ds_norm = r * gn - s * (r * r * r) * jnp.sum(gn * s, axis=-1, keepdims=True)
        ds = jnp.where(j < 2 * N_HEADS, ds_norm, g)
        dc = ds * (sg + s * (1.0 - sg))
        dx = dc * wb[3:4, :]
        rows = [jnp.sum(dc * xb, axis=0, keepdims=True)]
        for jj in range(2, -1, -1):
            sh = 3 - jj
            dx = dx + _shift_rows(dc, sh, True) * wb[jj:jj + 1, :]
            rows.insert(0, jnp.sum(dc * _shift_rows(xb, sh, False), axis=0, keepdims=True))
        dx_ref[...] = dx.astype(dx_ref.dtype)
        dw_ref[...] = jnp.concatenate(rows + [jnp.zeros((4, HEAD_DIM), F32)], axis=0)

    col = pl.BlockSpec((t, HEAD_DIM), lambda j: (0, j))
    return pl.pallas_call(
        body, name="dn_prep_bwd", grid=(3 * N_HEADS,),
        in_specs=[col, pl.BlockSpec((4, HEAD_DIM), lambda j: (0, j)),
                  pl.BlockSpec((None, t, HEAD_DIM), lambda j: (j // N_HEADS, 0, j % N_HEADS))],
        out_specs=[col, pl.BlockSpec((8, HEAD_DIM), lambda j: (0, j))],
        out_shape=[jax.ShapeDtypeStruct((t, 3 * D_MODEL), MM), jax.ShapeDtypeStruct((8, 3 * D_MODEL), F32)],
        compiler_params=_cparams(("parallel",)),
    )(proj, conv_w, dqkv)


def _tri_ones(n, upper):
    r, c = _iota2(n, n)
    return (r <= c).astype(F32) if upper else (r >= c).astype(F32)


def dn_gates_fwd(proj, a_log, dt_bias):
    t = proj.shape[0]

    def body(x_ref, al_ref, dt_ref, o_ref):
        lane = lax.broadcasted_iota(jnp.int32, (CHUNK, HEAD_DIM), 1)
        tri = _tri_ones(CHUNK, False)

        def step(c, carry):
            rows = pl.ds(pl.multiple_of(c * CHUNK, CHUNK), CHUNK)
            x = x_ref[rows, :]
            g = jnp.where(lane < N_HEADS, -jnp.exp(al_ref[...]) * _softplus(x + dt_ref[...]), 0.0)
            gc = _dot(tri, g, 1, 0, True)
            o_ref[rows, :] = jnp.where(lane < N_HEADS, gc, jnp.where(lane < 2 * N_HEADS, _sigmoid(x), 0.0))
            return carry

        lax.fori_loop(0, t // CHUNK, step, 0)

    vec = pl.BlockSpec((1, HEAD_DIM), lambda i: (0, 0))
    return pl.pallas_call(
        body, name="dn_gates_fwd", grid=(1,),
        in_specs=[pl.BlockSpec((t, HEAD_DIM), lambda i: (0, TAIL_BLK)), vec, vec],
        out_specs=pl.BlockSpec((t, HEAD_DIM), lambda i: (0, 0)),
        out_shape=jax.ShapeDtypeStruct((t, HEAD_DIM), F32), compiler_params=_cparams(("arbitrary",)),
    )(proj, a_log, dt_bias)


def dn_gates_bwd(proj, a_log, dt_bias, dgates):
    t = proj.shape[0]

    def body(x_ref, al_ref, dt_ref, g_ref, dx_ref, dal_ref, ddt_ref):
        lane = lax.broadcasted_iota(jnp.int32, (CHUNK, HEAD_DIM), 1)
        tri = _tri_ones(CHUNK, True)
        dal_ref[...] = jnp.zeros_like(dal_ref)
        ddt_ref[...] = jnp.zeros_like(ddt_ref)

        def step(c, carry):
            rows = pl.ds(pl.multiple_of(c * CHUNK, CHUNK), CHUNK)
            x = x_ref[rows, :]
            dgc = jnp.where(lane < N_HEADS, g_ref[rows, :], 0.0)
            dg = _dot(tri, dgc, 1, 0, True)
            ea = -jnp.exp(al_ref[...])
            z = x + dt_ref[...]
            da = jnp.where(lane < N_HEADS, dg * ea * _sigmoid(z), 0.0)
            gval = jnp.where(lane < N_HEADS, ea * _softplus(z), 0.0)
            beta = _sigmoid(x)
            db = jnp.where(jnp.logical_and(lane >= N_HEADS, lane < 2 * N_HEADS), g_ref[rows, :] * beta * (1.0 - beta), 0.0)
            dx_ref[rows, :] = (da + db).astype(dx_ref.dtype)
            dal_ref[...] += jnp.sum(dg * gval, axis=0, keepdims=True)
            ddt_ref[...] += jnp.sum(da, axis=0, keepdims=True)
            return carry

        lax.fori_loop(0, t // CHUNK, step, 0)

    vec = pl.BlockSpec((1, HEAD_DIM), lambda i: (0, 0))
    full = pl.BlockSpec((t, HEAD_DIM), lambda i: (0, 0))
    return pl.pallas_call(
        body, name="dn_gates_bwd", grid=(1,),
        in_specs=[pl.BlockSpec((t, HEAD_DIM), lambda i: (0, TAIL_BLK)), vec, vec, full],
        out_specs=[full, vec, vec],
        out_shape=[jax.ShapeDtypeStruct((t, HEAD_DIM), MM), jax.ShapeDtypeStruct((1, HEAD_DIM), F32),
                   jax.ShapeDtypeStruct((1, HEAD_DIM), F32)],
        compiler_params=_cparams(("arbitrary",)),
    )(proj, a_log, dt_bias, dgates)


def _dn_intra(q, k, v, gcol, grow, bcol):
    r, c = _iota2(CHUNK, CHUNK)
    causal, strict = r >= c, r > c
    decay = jnp.where(causal, jnp.exp(jnp.where(causal, gcol - grow, 0.0)), 0.0)
    kb = k * bcol
    a = jnp.where(strict, mmul(kb, k, 1, 1, False) * decay, 0.0)
    tm = tri_inv(a)
    u = mmul(tm, v * bcol, 1, 0, False)
    w = mmul(tm, kb * jnp.exp(gcol), 1, 0, False)
    qk = jnp.where(causal, mmul(q, k, 1, 1, False) * decay, 0.0)
    rr = lax.broadcasted_iota(jnp.int32, (CHUNK, 1), 0)
    g_last = jnp.sum(jnp.where(rr == CHUNK - 1, gcol, 0.0), axis=0, keepdims=True)
    return u, w, q * jnp.exp(gcol), k * jnp.exp(g_last - gcol), qk, jnp.exp(g_last)


def _dn_scan(u, w, qg, kd, qk, eg, state):
    v_new = u - mmul(w, state, 1, 0, False)
    out = mmul(qg, state, 1, 0, False) + mmul(qk, v_new, 1, 0, False)
    return out, state * eg + mmul(kd, v_new, 0, 0, False)


DN_HEADS_PER_STEP = 1
DN_GROUP = 8
DN_PARTS = ((CHUNK, HEAD_DIM),) * 4 + ((CHUNK, CHUNK), (1, 1))


def _dn_scratch(hb, nc):
    return [pltpu.VMEM((hb, nc) + shape, F32) for shape in DN_PARTS]


def _dn_part_specs(hb, nc):
    return [pl.BlockSpec((hb, nc) + shape, lambda h: (h, 0, 0, 0)) for shape in DN_PARTS]


def _dn_group(nc):
    return min(DN_GROUP, nc)


def _dn_group_args(refs, j, g, grp):
    q_ref, k_ref, v_ref, gc_ref, gr_ref, bc_ref = refs
    rows = pl.ds(pl.multiple_of(g * (grp * CHUNK), grp * CHUNK), grp * CHUNK)
    cs = pl.ds(g * grp, grp)
    split = lambda ref: ref[rows, _lanes(j)].reshape(grp, CHUNK, HEAD_DIM)
    return split(q_ref), split(k_ref), split(v_ref), gc_ref[j, cs], gr_ref[j, cs], bc_ref[j, cs]


def _dn_intra_all(refs, parts, hb, nc):
    grp = _dn_group(nc)

    def group(g, carry):
        cs = pl.ds(g * grp, grp)
        for j in range(hb):
            for part, val in zip(parts, jax.vmap(_dn_intra)(*_dn_group_args(refs, j, g, grp))):
                part[j, cs] = val
        return carry

    lax.fori_loop(0, nc // grp, group, 0)


def _dn_specs(t):
    nc, hb = t // CHUNK, DN_HEADS_PER_STEP
    head = lambda which: pl.BlockSpec((None, t, hb * HEAD_DIM), lambda h: (which, 0, h))
    flat = pl.BlockSpec((t, hb * HEAD_DIM), lambda h: (0, h))
    col = pl.BlockSpec((hb, nc, CHUNK, 1), lambda h: (h, 0, 0, 0))
    row = pl.BlockSpec((hb, nc, 1, CHUNK), lambda h: (h, 0, 0, 0))
    st = pl.BlockSpec((hb, nc, HEAD_DIM, HEAD_DIM), lambda h: (h, 0, 0, 0))
    return nc, hb, head, flat, col, row, st


def dn_core_fwd(qkv, gcol, grow, bcol):
    t = qkv.shape[1]
    nc, hb, head, flat, col, row, st = _dn_specs(t)

    def body(q_ref, k_ref, v_ref, gc_ref, gr_ref, bc_ref, o_ref, s_ref, *parts):
        _dn_intra_all((q_ref, k_ref, v_ref, gc_ref, gr_ref, bc_ref), parts, hb, nc)

        def step(c, states):
            rows = pl.ds(pl.multiple_of(c * CHUNK, CHUNK), CHUNK)
            new_states = []
            for j in range(hb):
                s_ref[j, c] = states[j]
                out, new_state = _dn_scan(*[part[j, c] for part in parts], states[j])
                o_ref[rows, _lanes(j)] = out
                new_states.append(new_state)
            return tuple(new_states)

        lax.fori_loop(0, nc, step, tuple(jnp.zeros((HEAD_DIM, HEAD_DIM), F32) for _ in range(hb)))

    outs = pl.pallas_call(
        body, name="dn_core_fwd", grid=(N_HEADS // hb,),
        in_specs=[head(0), head(1), head(2), col, row, col], out_specs=[flat, st] + _dn_part_specs(hb, nc),
        out_shape=[jax.ShapeDtypeStruct((t, D_MODEL), F32), jax.ShapeDtypeStruct((N_HEADS, nc, HEAD_DIM, HEAD_DIM), F32)]
        + [jax.ShapeDtypeStruct((N_HEADS, nc) + shape, F32) for shape in DN_PARTS],
        compiler_params=_cparams(("parallel",)),
    )(qkv, qkv, qkv, gcol, grow, bcol)
    return outs[0], outs[1], tuple(outs[2:])


def dn_core_bwd(qkv, gcol, grow, bcol, states, parts, do):
    t = qkv.shape[1]
    nc, hb, head, flat, col, row, st = _dn_specs(t)
    n_parts = len(DN_PARTS)

    def body(q_ref, k_ref, v_ref, gc_ref, gr_ref, bc_ref, s_ref, do_ref, *rest):
        parts, (dqkv_ref, dgc_ref, dgr_ref, dbc_ref), dparts = rest[:n_parts], rest[n_parts:n_parts + 4], rest[n_parts + 4:]
        refs = (q_ref, k_ref, v_ref, gc_ref, gr_ref, bc_ref)

        def step(i, dstates):
            c = nc - 1 - i
            rows = pl.ds(pl.multiple_of(c * CHUNK, CHUNK), CHUNK)
            dstates_in = []
            for j in range(hb):
                _, vjp = jax.vjp(_dn_scan, *[part[j, c] for part in parts], s_ref[j, c])
                *dvals, dstate_in = vjp((do_ref[rows, _lanes(j)], dstates[j]))
                for dpart, dval in zip(dparts, dvals):
                    dpart[j, c] = dval
                dstates_in.append(dstate_in)
            return tuple(dstates_in)

        lax.fori_loop(0, nc, step, tuple(jnp.zeros((HEAD_DIM, HEAD_DIM), F32) for _ in range(hb)))

        grp = _dn_group(nc)

        def group(g, carry):
            rows = pl.ds(pl.multiple_of(g * (grp * CHUNK), grp * CHUNK), grp * CHUNK)
            cs = pl.ds(g * grp, grp)
            for j in range(hb):
                _, vjp = jax.vjp(jax.vmap(_dn_intra), *_dn_group_args(refs, j, g, grp))
                dq, dk, dv, dgc, dgr, dbc = vjp(tuple(dpart[j, cs] for dpart in dparts))
                for which, val in enumerate((dq, dk, dv)):
                    dqkv_ref[which, rows, _lanes(j)] = val.reshape(grp * CHUNK, HEAD_DIM)
                dgc_ref[j, cs] = dgc
                dgr_ref[j, cs] = dgr
                dbc_ref[j, cs] = dbc
            return carry

        lax.fori_loop(0, nc // grp, group, 0)

    return pl.pallas_call(
        body, name="dn_core_bwd", grid=(N_HEADS // hb,), scratch_shapes=_dn_scratch(hb, nc),
        in_specs=[head(0), head(1), head(2), col, row, col, st, flat] + _dn_part_specs(hb, nc),
        out_specs=[pl.BlockSpec((3, t, hb * HEAD_DIM), lambda h: (0, 0, h)), col, row, col],
        out_shape=[jax.ShapeDtypeStruct((3, t, D_MODEL), F32)] + [
            jax.ShapeDtypeStruct((N_HEADS, nc, CHUNK, 1), F32), jax.ShapeDtypeStruct((N_HEADS, nc, 1, CHUNK), F32),
            jax.ShapeDtypeStruct((N_HEADS, nc, CHUNK, 1), F32)],
        compiler_params=_cparams(("parallel",)),
    )(qkv, qkv, qkv, gcol, grow, bcol, states, do, *parts)


def gates_to_heads(gates):
    t = gates.shape[0]
    nc = t // CHUNK
    g = gates[:, :N_HEADS].T.reshape(N_HEADS, nc, CHUNK)
    b = gates[:, N_HEADS:2 * N_HEADS].T.reshape(N_HEADS, nc, CHUNK)
    return g[..., None], g[:, :, None, :], b[..., None]


def heads_to_gates(dgcol, dgrow, dbcol):
    nh, nc = dgcol.shape[:2]
    dg = (dgcol[..., 0] + dgrow[:, :, 0, :]).reshape(nh, nc * CHUNK).T
    db = dbcol[..., 0].reshape(nh, nc * CHUNK).T
    return jnp.concatenate([dg, db, jnp.zeros((nc * CHUNK, HEAD_DIM - 2 * nh), F32)], axis=1)


def _dn_out(o, z, w):
    return _rms(o, w) * _silu(z)


def _gate_specs():
    o_spec = pl.BlockSpec((ROWS, D_MODEL), lambda i: (i, 0))
    z_spec = pl.BlockSpec((ROWS, D_MODEL), lambda i: (i, 3))
    w_spec = pl.BlockSpec((1, HEAD_DIM), lambda i: (0, 0))
    return o_spec, z_spec, w_spec


def dn_out_fwd(o, proj, w):
    t = o.shape[0]
    o_spec, z_spec, w_spec = _gate_specs()

    def body(o_ref, z_ref, w_ref, y_ref):
        for h in range(N_HEADS):
            y_ref[:, _lanes(h)] = _dn_out(o_ref[:, _lanes(h)], z_ref[:, _lanes(h)], w_ref[...]).astype(y_ref.dtype)

    return pl.pallas_call(
        body, name="dn_out_fwd", grid=(t // ROWS,), in_specs=[o_spec, z_spec, w_spec], out_specs=o_spec,
        out_shape=jax.ShapeDtypeStruct((t, D_MODEL), MM), compiler_params=_cparams(("parallel",)),
    )(o, proj, w)


def dn_out_bwd(o, proj, w, dcat):
    t = o.shape[0]
    o_spec, z_spec, w_spec = _gate_specs()

    def body(o_ref, z_ref, w_ref, g_ref, do_ref, dz_ref, dw_ref):
        dw_sum = jnp.zeros((1, HEAD_DIM), F32)
        for h in range(N_HEADS):
            _, vjp = jax.vjp(_dn_out, o_ref[:, _lanes(h)], z_ref[:, _lanes(h)], w_ref[...])
            do, dz, dw = vjp(g_ref[:, _lanes(h)].astype(F32))
            do_ref[:, _lanes(h)] = do
            dz_ref[:, _lanes(h)] = dz.astype(dz_ref.dtype)
            dw_sum = dw_sum + dw
        _acc(dw_ref, dw_sum, pl.program_id(0) == 0)

    return pl.pallas_call(
        body, name="dn_out_bwd", grid=(t // ROWS,), in_specs=[o_spec, z_spec, w_spec, o_spec],
        out_specs=[o_spec, o_spec, w_spec],
        out_shape=[jax.ShapeDtypeStruct((t, D_MODEL), F32), jax.ShapeDtypeStruct((t, D_MODEL), MM),
                   jax.ShapeDtypeStruct((1, HEAD_DIM), F32)],
        compiler_params=_cparams(("arbitrary",)),
    )(o, proj, w, dcat)


def _fox_norm(x, w, scale):
    return _rms(x, w) * scale


def _fox_prep_specs():
    x_spec = pl.BlockSpec((ROWS, 2 * D_MODEL), lambda i: (i, 0))
    w_spec = pl.BlockSpec((2, 1, HEAD_DIM), lambda i: (0, 0, 0))
    y_spec = pl.BlockSpec((2, ROWS, D_MODEL), lambda i: (0, i, 0))
    return x_spec, w_spec, y_spec


def fox_prep_fwd(proj, wqk):
    t = proj.shape[0]
    x_spec, w_spec, y_spec = _fox_prep_specs()

    def body(x_ref, w_ref, y_ref):
        for j in range(2 * N_HEADS):
            which, scale = j // N_HEADS, (QSCALE if j < N_HEADS else 1.0)
            y_ref[which, :, _lanes(j % N_HEADS)] = _fox_norm(x_ref[:, _lanes(j)], w_ref[which], scale).astype(y_ref.dtype)

    return pl.pallas_call(
        body, name="fox_prep_fwd", grid=(t // ROWS,), in_specs=[x_spec, w_spec], out_specs=y_spec,
        out_shape=jax.ShapeDtypeStruct((2, t, D_MODEL), MM), compiler_params=_cparams(("parallel",)),
    )(proj, wqk)


def fox_prep_bwd(proj, wqk, dq, dk):
    t = proj.shape[0]
    x_spec, w_spec, _ = _fox_prep_specs()
    g_spec = pl.BlockSpec((ROWS, D_MODEL), lambda i: (i, 0))

    def body(x_ref, w_ref, dq_ref, dk_ref, dx_ref, dw_ref):
        dws = [jnp.zeros((1, HEAD_DIM), F32), jnp.zeros((1, HEAD_DIM), F32)]
        for j in range(2 * N_HEADS):
            which, scale = j // N_HEADS, (QSCALE if j < N_HEADS else 1.0)
            g_ref = dq_ref if which == 0 else dk_ref
            _, vjp = jax.vjp(lambda x, w: _fox_norm(x, w, scale), x_ref[:, _lanes(j)], w_ref[which])
            dx, dw = vjp(g_ref[:, _lanes(j % N_HEADS)])
            dx_ref[:, _lanes(j)] = dx.astype(dx_ref.dtype)
            dws[which] = dws[which] + dw
        first = pl.program_id(0) == 0
        _acc(dw_ref.at[0], dws[0], first)
        _acc(dw_ref.at[1], dws[1], first)

    return pl.pallas_call(
        body, name="fox_prep_bwd", grid=(t // ROWS,), in_specs=[x_spec, w_spec, g_spec, g_spec],
        out_specs=[x_spec, w_spec],
        out_shape=[jax.ShapeDtypeStruct((t, 2 * D_MODEL), MM), jax.ShapeDtypeStruct((2, 1, HEAD_DIM), F32)],
        compiler_params=_cparams(("arbitrary",)),
    )(proj, wqk, dq, dk)


def _row_pick(x, i):
    r = lax.broadcasted_iota(jnp.int32, x.shape, 0)
    return jnp.sum(jnp.where(r == i, x, 0.0), axis=0, keepdims=True)


def fox_gates_fwd(proj, f_bias):
    t = proj.shape[0]
    blk = HEAD_DIM

    def body(x_ref, b_ref, o_ref):
        lane = lax.broadcasted_iota(jnp.int32, (blk, HEAD_DIM), 1)
        tri = _tri_ones(blk, False)

        def step(c, carry):
            rows = pl.ds(pl.multiple_of(c * blk, blk), blk)
            lf = jnp.where(lane < N_HEADS, -_softplus(-(x_ref[rows, :] + b_ref[...])), 0.0)
            cum = _dot(tri, lf, 1, 0, True) + carry
            o_ref[rows, :] = cum
            return _row_pick(cum, blk - 1)

        lax.fori_loop(0, t // blk, step, jnp.zeros((1, HEAD_DIM), F32))

    vec = pl.BlockSpec((1, HEAD_DIM), lambda i: (0, 0))
    return pl.pallas_call(
        body, name="fox_gates_fwd", grid=(1,),
        in_specs=[pl.BlockSpec((t, HEAD_DIM), lambda i: (0, TAIL_BLK)), vec],
        out_specs=pl.BlockSpec((t, HEAD_DIM), lambda i: (0, 0)),
        out_shape=jax.ShapeDtypeStruct((t, HEAD_DIM), F32), compiler_params=_cparams(("arbitrary",)),
    )(proj, f_bias)


def fox_gates_bwd(proj, f_bias, dfcum):
    t = proj.shape[0]
    blk = HEAD_DIM
    nb = t // blk

    def body(x_ref, b_ref, g_ref, dx_ref, db_ref):
        lane = lax.broadcasted_iota(jnp.int32, (blk, HEAD_DIM), 1)
        tri = _tri_ones(blk, True)
        db_ref[...] = jnp.zeros_like(db_ref)

        def step(i, carry):
            c = nb - 1 - i
            rows = pl.ds(pl.multiple_of(c * blk, blk), blk)
            g = jnp.where(lane < N_HEADS, g_ref[rows, :], 0.0)
            dlf = _dot(tri, g, 1, 0, True) + carry
            dx = jnp.where(lane < N_HEADS, dlf * _sigmoid(-(x_ref[rows, :] + b_ref[...])), 0.0)
            dx_ref[rows, :] = dx.astype(dx_ref.dtype)
            db_ref[...] += jnp.sum(dx, axis=0, keepdims=True)
            return carry + jnp.sum(g, axis=0, keepdims=True)

        lax.fori_loop(0, nb, step, jnp.zeros((1, HEAD_DIM), F32))

    vec = pl.BlockSpec((1, HEAD_DIM), lambda i: (0, 0))
    full = pl.BlockSpec((t, HEAD_DIM), lambda i: (0, 0))
    return pl.pallas_call(
        body, name="fox_gates_bwd", grid=(1,),
        in_specs=[pl.BlockSpec((t, HEAD_DIM), lambda i: (0, TAIL_BLK)), vec, full], out_specs=[full, vec],
        out_shape=[jax.ShapeDtypeStruct((t, HEAD_DIM), MM), jax.ShapeDtypeStruct((1, HEAD_DIM), F32)],
        compiler_params=_cparams(("arbitrary",)),
    )(proj, f_bias, dfcum)


def fcum_to_heads(fcum):
    f = fcum[:, :N_HEADS].T
    return f[:, :, None], f[:, None, :]


def heads_to_fcum(dfcol, dfrow):
    d = (dfcol[:, :, 0] + dfrow[:, 0, :]).T
    return jnp.concatenate([d, jnp.zeros((d.shape[0], HEAD_DIM - N_HEADS), F32)], axis=1)


def _fox_tq(t):
    return min(t, 256)


def _fox_specs(t):
    tq = _fox_tq(t)
    q_spec = pl.BlockSpec((None, tq, HEAD_DIM), lambda h, i: (0, i, h))
    k_spec = pl.BlockSpec((None, t, HEAD_DIM), lambda h, i: (1, 0, h))
    v_spec = pl.BlockSpec((t, HEAD_DIM), lambda h, i: (0, 2 * N_HEADS + h))
    gate_spec = pl.BlockSpec((tq, HEAD_DIM), lambda h, i: (i, 3 * N_HEADS + h))
    col_spec = pl.BlockSpec((None, tq, 1), lambda h, i: (h, i, 0))
    row_spec = pl.BlockSpec((None, 1, t), lambda h, i: (h, 0, 0))
    blk_spec = pl.BlockSpec((tq, HEAD_DIM), lambda h, i: (i, h))
    head_spec = pl.BlockSpec((t, HEAD_DIM), lambda h, i: (0, h))
    return tq, q_spec, k_spec, v_spec, gate_spec, col_spec, row_spec, blk_spec, head_spec


def _fox_segments(i, tq):
    return ([(0, i * tq, False)] if i else []) + [(i * tq, (i + 1) * tq, True)]


def _fox_scores(q_ref, k_ref, fc_ref, fr_ref, lo, hi, causal):
    s = _dot(q_ref[...], k_ref[lo:hi, :], 1, 1, False) + (fc_ref[...] - fr_ref[:, lo:hi])
    if not causal:
        return s, None
    r, c = _iota2(hi - lo, hi - lo)
    return s, c <= r


def fox_attn_fwd(qk, proj, fcol, frow):
    t = proj.shape[0]
    tq, q_spec, k_spec, v_spec, gate_spec, col_spec, row_spec, blk_spec, _ = _fox_specs(t)

    def body(q_ref, k_ref, v_ref, gate_ref, fc_ref, fr_ref, mix_ref, o_ref, lse_ref):
        def block(i):
            segs = _fox_segments(i, tq)
            scores = [_fox_scores(q_ref, k_ref, fc_ref, fr_ref, *seg) for seg in segs]
            scores = [(s if mask is None else jnp.where(mask, s, -1e30), mask) for s, mask in scores]
            m = functools.reduce(jnp.maximum, [jnp.max(s, axis=-1, keepdims=True) for s, _ in scores])
            l, o = 0.0, 0.0
            for (lo, hi, _), (s, mask) in zip(segs, scores):
                p = jnp.exp(s - m)
                p = p if mask is None else jnp.where(mask, p, 0.0)
                l = l + jnp.sum(p, axis=-1, keepdims=True)
                o = o + _dot(p, v_ref[lo:hi, :], 1, 0, False)
            o = o / l
            o_ref[...] = o
            mix_ref[...] = (o * _sigmoid(gate_ref[...])).astype(mix_ref.dtype)
            lse_ref[...] = m + jnp.log(l)

        for i in range(t // tq):
            pl.when(pl.program_id(1) == i)(functools.partial(block, i))

    return pl.pallas_call(
        body, name="fox_attn_fwd", grid=(N_HEADS, t // tq),
        in_specs=[q_spec, k_spec, v_spec, gate_spec, col_spec, row_spec], out_specs=[blk_spec, blk_spec, col_spec],
        out_shape=[jax.ShapeDtypeStruct((t, D_MODEL), MM), jax.ShapeDtypeStruct((t, D_MODEL), F32),
                   jax.ShapeDtypeStruct((N_HEADS, t, 1), F32)],
        compiler_params=_cparams(("parallel", "parallel")),
    )(qk, qk, proj, proj, fcol, frow)


def fox_attn_bwd(qk, proj, fcol, frow, o, lse, dcat):
    t = proj.shape[0]
    tq, q_spec, k_spec, v_spec, gate_spec, col_spec, row_spec, blk_spec, head_spec = _fox_specs(t)

    def body(q_ref, k_ref, v_ref, gate_ref, fc_ref, fr_ref, o_ref, lse_ref, g_ref,
             dq_ref, dk_ref, dv_ref, dgate_ref, dfc_ref, dfr_ref):
        @pl.when(pl.program_id(1) == 0)
        def _():
            dk_ref[...] = jnp.zeros_like(dk_ref)
            dv_ref[...] = jnp.zeros_like(dv_ref)
            dfr_ref[...] = jnp.zeros_like(dfr_ref)

        def block(i):
            sg = _sigmoid(gate_ref[...])
            g = g_ref[...].astype(F32)
            o_pre = o_ref[...]
            do = g * sg
            dgate_ref[...] = (g * o_pre * sg * (1.0 - sg)).astype(dgate_ref.dtype)
            delta = jnp.sum(do * o_pre, axis=-1, keepdims=True)
            dq, dfc = 0.0, 0.0
            for lo, hi, causal in _fox_segments(i, tq):
                s, mask = _fox_scores(q_ref, k_ref, fc_ref, fr_ref, lo, hi, causal)
                if causal:
                    p = jnp.where(mask, jnp.exp(jnp.where(mask, s, 0.0) - lse_ref[...]), 0.0)
                else:
                    p = jnp.exp(s - lse_ref[...])
                ds = p * (_dot(do, v_ref[lo:hi, :], 1, 1, False) - delta)
                dq = dq + _dot(ds, k_ref[lo:hi, :], 1, 0, False)
                dk_ref[lo:hi, :] += _dot(ds, q_ref[...], 0, 0, False)
                dv_ref[lo:hi, :] += _dot(p, do, 0, 0, False)
                dfc = dfc + jnp.sum(ds, axis=-1, keepdims=True)
                dfr_ref[:, lo:hi] += -jnp.sum(ds, axis=0, keepdims=True)
            dq_ref[...] = dq
            dfc_ref[...] = dfc

        for i in range(t // tq):
            pl.when(pl.program_id(1) == i)(functools.partial(block, i))

    f32 = lambda *s: jax.ShapeDtypeStruct(s, F32)
    return pl.pallas_call(
        body, name="fox_attn_bwd", grid=(N_HEADS, t // tq),
        in_specs=[q_spec, k_spec, v_spec, gate_spec, col_spec, row_spec, blk_spec, col_spec, blk_spec],
        out_specs=[blk_spec, head_spec, head_spec, blk_spec, col_spec, row_spec],
        out_shape=[f32(t, D_MODEL), f32(t, D_MODEL), f32(t, D_MODEL), jax.ShapeDtypeStruct((t, D_MODEL), MM),
                   f32(N_HEADS, t, 1), f32(N_HEADS, 1, t)],
        compiler_params=_cparams(("parallel", "arbitrary")),
    )(qk, qk, proj, proj, fcol, frow, o, lse, dcat)


def adamw(w, g, m, v, *, name):
    r, c = w.shape
    rb = ROWS if r % ROWS == 0 else r

    def body(w_ref, g_ref, m_ref, v_ref, d_ref, nm_ref, nv_ref):
        g_ = g_ref[...]
        m_ = ADAM_B1 * m_ref[...] + (1.0 - ADAM_B1) * g_
        v_ = ADAM_B2 * v_ref[...] + (1.0 - ADAM_B2) * jnp.square(g_)
        m_hat = m_ / (1.0 - ADAM_B1 ** ADAM_STEP)
        v_hat = v_ / (1.0 - ADAM_B2 ** ADAM_STEP)
        d_ref[...] = -ADAM_LR * (m_hat / (jnp.sqrt(v_hat) + ADAM_EPS) + ADAM_WD * w_ref[...])
        nm_ref[...] = m_
        nv_ref[...] = v_

    blk = pl.BlockSpec((rb, c), lambda i: (i, 0))
    shp = jax.ShapeDtypeStruct((r, c), F32)
    return pl.pallas_call(body, name=name, grid=(r // rb,), in_specs=[blk] * 4, out_specs=[blk] * 3,
                          out_shape=[shp] * 3, compiler_params=_cparams(("parallel",)))(w, g, m, v)


def _place():
    x, y, c = lax.axis_index("x"), lax.axis_index("y"), lax.axis_index("c")
    return x, y, c, [(1 - x, y), (x, 1 - y), (1 - x, 1 - y)]


ANY = pl.BlockSpec(memory_space=pl.ANY)


def all_reduce_small(v):
    r, w = v.shape

    def body(v_ref, o_ref, buf, send_sems, recv_sems):
        x, y, c, _ = _place()
        me = 4 * x + 2 * y + c
        flip = lambda a, bit: 1 - a if bit else a
        cps = []
        for k in range(1, N_DEV):
            peer = (flip(x, k & 4), flip(y, k & 2), flip(c, k & 1))
            cp = pltpu.make_async_remote_copy(src_ref=v_ref, dst_ref=buf.at[me], send_sem=send_sems.at[k - 1],
                                              recv_sem=recv_sems.at[k - 1], device_id=peer, device_id_type=MESH)
            cp.start()
            cps.append((cp, 4 * peer[0] + 2 * peer[1] + peer[2]))
        buf[me] = v_ref[...]
        for k, (cp, peer_id) in enumerate(cps):
            pltpu.make_async_remote_copy(src_ref=v_ref, dst_ref=buf.at[peer_id], send_sem=send_sems.at[k],
                                         recv_sem=recv_sems.at[k], device_id=(x, y, c), device_id_type=MESH).wait_recv()
        for cp, _ in cps:
            cp.wait_send()
        acc = buf[0]
        for d in range(1, N_DEV):
            acc = acc + buf[d]
        o_ref[...] = acc

    vm = pl.BlockSpec(memory_space=pltpu.VMEM)
    return pl.pallas_call(
        body, name="all_reduce_small", in_specs=[vm], out_specs=vm, out_shape=jax.ShapeDtypeStruct((r, w), F32),
        scratch_shapes=[pltpu.VMEM((N_DEV, r, w), F32), pltpu.SemaphoreType.DMA((N_DEV - 1,)),
                        pltpu.SemaphoreType.DMA((N_DEV - 1,))],
    )(v)


def _vec8(v):
    return jnp.zeros((1, HEAD_DIM), F32).at[0, :N_HEADS].set(v.reshape(N_HEADS))


def _layer_fwd(i, x_in, wt, sm, mem_k, mem_v, late=None):
    tag = f"l{i}_"
    h = rms_fwd(x_in, sm["norm1_w"][i][None], name=tag + "rms1")
    w_in = wt["dn_w_in"] if i == 0 else wt["fox_w_in"]
    proj = matmul(h, w_in, name=tag + "proj", tm=256, tk=1024)
    sv = dict(x_in=x_in, h=h, proj=proj)
    if i == 0:
        qkv = dn_prep_fwd(proj, wt["conv_w"])
        gates = dn_gates_fwd(proj, _vec8(sm["dn_a_log"]), _vec8(sm["dn_dt_bias"]))
        gcol, grow, bcol = gates_to_heads(gates)
        o, states, parts = dn_core_fwd(qkv, gcol, grow, bcol)
        mix = dn_out_fwd(o, proj, sm["dn_o_norm_w"])
        sv.update(qkv=qkv, gcol=gcol, grow=grow, bcol=bcol, states=states, parts=parts, o=o)
    else:
        wqk = jnp.stack([sm["fox_q_norm_w"], sm["fox_k_norm_w"]])
        qk = fox_prep_fwd(proj, wqk)
        fcum = fox_gates_fwd(proj, _vec8(sm["fox_f_bias"]))
        fcol, frow = fcum_to_heads(fcum)
        mix, o, lse = fox_attn_fwd(qk, proj, fcol, frow)
        sv.update(wqk=wqk, qk=qk, fcol=fcol, frow=frow, o=o, lse=lse)
    mem_out = memattn_fwd(proj, sm["memq_norm_w"][i][None], mem_k, mem_v, name=tag + "memattn_fwd")
    cat = jnp.concatenate([mix, mem_out], axis=1)
    if late is not None:
        wt.update(late(cat))
    x_mid = matmul(cat, wt["w_out"], b_layer=i, res=x_in, name=tag + "out_proj")
    h2 = rms_fwd(x_mid, sm["norm2_w"][i][None], name=tag + "rms2")
    ff, act = matmul(h2, wt["w_mlp1"], b_layer=i, b_slots=True, also_sqrelu=True, out_dtype=MM, name=tag + "mlp1")
    x_out = matmul(act, wt["w_mlp2"], b_layer=i, res=x_mid, name=tag + "mlp2")
    sv.update(cat=cat, x_mid=x_mid, h2=h2, ff=ff, act=act)
    return x_out, sv


def _layer_bwd(i, dx_out, sv, wt, sm, mem_k, mem_v, on_mlp=None, on_core=None):
    tag = f"l{i}_"
    big, small = {}, {}
    dff = matmul(dx_out, wt["w_mlp2"], b_layer=i, tb=True, times_dsqrelu=sv["ff"], out_dtype=MM, name=tag + "d_ff")
    big["w_mlp2"] = matmul(sv["act"], dx_out, ta=True, name=tag + "d_w_mlp2", tk=2048)
    dh2 = matmul(dff, wt["w_mlp1"], b_layer=i, tb=True, b_slots=True, name=tag + "d_h2")
    big["w_mlp1"] = matmul(sv["h2"], dff, ta=True, name=tag + "d_w_mlp1", tm=512, tn=D_FF, tk=512)
    dx_mid, small["norm2_w"] = rms_bwd(sv["x_mid"], sm["norm2_w"][i][None], dh2, dx_out, name=tag + "rms2_bwd")
    dcat = matmul(dx_mid, wt["w_out"], b_layer=i, tb=True, name=tag + "d_cat")
    big["w_out"] = matmul(sv["cat"], dx_mid, ta=True, name=tag + "d_w_out", tk=2048)
    proj = sv["proj"]
    memq_norm_w = sm["memq_norm_w"][i][None]
    if on_mlp is not None:
        memq_norm_w = memq_norm_w + on_mlp(big["w_mlp2"], big["w_mlp1"], big["w_out"])
    dqm, small["memq_norm_w"], dmk, dmv = memattn_bwd(proj, memq_norm_w, mem_k, mem_v, dcat, name=tag + "memattn_bwd")
    t = proj.shape[0]
    pad = jnp.zeros((t, PROJ_W - TAIL - HEAD_DIM), MM)
    if i == 0:
        do, dz, small["dn_o_norm_w"] = dn_out_bwd(sv["o"], proj, sm["dn_o_norm_w"], dcat)
        bcol = sv["bcol"] if on_core is None else sv["bcol"] + on_core(do)
        dqkv, dgc, dgr, dbc = dn_core_bwd(sv["qkv"], sv["gcol"], sv["grow"], bcol, sv["states"], sv["parts"], do)
        dtail, dal, ddt = dn_gates_bwd(proj, _vec8(sm["dn_a_log"]), _vec8(sm["dn_dt_bias"]), heads_to_gates(dgc, dgr, dbc))
        dmain, dconv = dn_prep_bwd(proj, wt["conv_w"], dqkv)
        small["dn_a_log"], small["dn_dt_bias"] = dal[:, :N_HEADS], ddt[:, :N_HEADS]
        big["conv_w"] = dconv[:4]
        dproj = jnp.concatenate([dmain, dz, dqm, dtail, pad], axis=1)
    else:
        dq, dk, dv, dgate, dfc, dfr = fox_attn_bwd(sv["qk"], proj, sv["fcol"], sv["frow"], sv["o"], sv["lse"], dcat)
        dtail, dfb = fox_gates_bwd(proj, _vec8(sm["fox_f_bias"]), heads_to_fcum(dfc, dfr))
        dqk, dwqk = fox_prep_bwd(proj, sv["wqk"], dq, dk)
        small["fox_f_bias"] = dfb[:, :N_HEADS]
        small["fox_q_norm_w"], small["fox_k_norm_w"] = dwqk[0], dwqk[1]
        dproj = jnp.concatenate([dqk, dv.astype(MM), dgate, dqm, dtail, pad], axis=1)
    w_in = wt["dn_w_in"] if i == 0 else wt["fox_w_in"]
    dh = matmul(dproj, w_in, tb=True, name=tag + "d_h", tm=512)
    big["w_in"] = matmul(sv["h"], dproj, ta=True, name=tag + "d_w_in", tm=256)
    dx_in, small["norm1_w"] = rms_bwd(sv["x_in"], sm["norm1_w"][i][None], dh, dx_mid, name=tag + "rms1_bwd")
    return dx_in, big, small, (dmk, dmv)


def local_step(x, mem, target, wt, sm, late=None, on_layer1=None, on_mlp0=None, on_core0=None):
    wt = dict(wt)
    mem_k, mem_v = mem_fwd(mem, sm["mem_norm_w"][None], wt["w_mem_kv"], sm["mem_k_norm_w"][None])
    x0, sv0 = _layer_fwd(0, x, wt, sm, mem_k, mem_v, late)
    x1, sv1 = _layer_fwd(1, x0, wt, sm, mem_k, mem_v)
    dy, loss = loss_fwd(x1, target, name="loss")
    dx1, big1, small1, dm1 = _layer_bwd(1, dy, sv1, wt, sm, mem_k, mem_v)
    if on_layer1 is not None:
        dx1 = dx1 + on_layer1(big1)
    dx0, big0, small0, dm0 = _layer_bwd(0, dx1, sv0, wt, sm, mem_k, mem_v, on_mlp0, on_core0)
    dwn, dwkv, dwkn = mem_bwd(mem, sm["mem_norm_w"][None], wt["w_mem_kv"], sm["mem_k_norm_w"][None], *dm0, *dm1)
    small = dict(mem_norm_w=dwn[0], mem_k_norm_w=dwkn[0],
                 norm1_w=jnp.concatenate([small0["norm1_w"], small1["norm1_w"]]),
                 norm2_w=jnp.concatenate([small0["norm2_w"], small1["norm2_w"]]),
                 memq_norm_w=jnp.concatenate([small0["memq_norm_w"], small1["memq_norm_w"]]),
                 dn_a_log=small0["dn_a_log"], dn_dt_bias=small0["dn_dt_bias"], dn_o_norm_w=small0["dn_o_norm_w"],
                 fox_f_bias=small1["fox_f_bias"], fox_q_norm_w=small1["fox_q_norm_w"], fox_k_norm_w=small1["fox_k_norm_w"])
    big = dict(w_mem_kv=dwkv, dn_w_in=big0["w_in"], fox_w_in=big1["w_in"], conv_w=big0["conv_w"],
               w_out=[big0["w_out"], big1["w_out"]], w_mlp1=[big0["w_mlp1"], big1["w_mlp1"]],
               w_mlp2=[big0["w_mlp2"], big1["w_mlp2"]])
    return loss, dx0, big, small


def w_in_to_kernel(w, n_scalars):
    pad = jnp.zeros((w.shape[0], PROJ_W - TAIL - n_scalars), w.dtype)
    return jnp.concatenate([w[:, :4096], w[:, 4096 + n_scalars:], w[:, 4096:4096 + n_scalars], pad], axis=1)


def w_in_from_kernel(w, n_scalars):
    return jnp.concatenate([w[:, :4096], w[:, TAIL:TAIL + n_scalars], w[:, 4096:TAIL]], axis=1)


BIG_SPECS = dict(w_mem_kv=("rows", 1, 256, 1024), w_out=("rows", 2, 384, 1024), w_mlp2=("rows", 2, 1024, 1024),
                 w_mlp1=("cols", 2, 1024, 1024), dn_w_in=("rows", 1, 1024, 1156), fox_w_in=("rows", 1, 1024, 1154))
BIG_NAMES = tuple(BIG_SPECS)
EARLY_NAMES = ("w_mem_kv", "dn_w_in")
LATE_NAMES = ("w_out", "w_mlp2", "w_mlp1", "fox_w_in")
BIG_SPECS.update({f"{name}_{i}": (BIG_SPECS[name][0], 1) + BIG_SPECS[name][2:]
                  for name in ("w_out", "w_mlp2", "w_mlp1") for i in range(2)})
RS_LAYER1 = ("fox_w_in", "w_out_1", "w_mlp2_1", "w_mlp1_1")
RS_MLP0 = ("w_mlp2_0", "w_mlp1_0", "w_out_0")
RS_LAST = ("dn_w_in", "w_mem_kv")


def _full_shape(name, half=False):
    kind, a, b, c = BIG_SPECS[name]
    b = b // 2 if half else b
    return (a, N_CHIP, b, c) if kind == "rows" else (a, b, N_CHIP * c)


def _ds(start, size, align):
    return pl.ds(start if isinstance(start, int) else pl.multiple_of(start, align), size)


def _half_rows(name, h):
    b = BIG_SPECS[name][2]
    return _ds(h * (b // 2), b // 2, 16)


def _shard_idx(name, h):
    return (slice(None), _half_rows(name, h), slice(None))


def _full_idx(name, j=None, h=None):
    kind, _, _, c = BIG_SPECS[name]
    rows = slice(None) if h is None else _half_rows(name, h)
    if kind == "rows":
        return (slice(None), slice(None) if j is None else j, rows, slice(None))
    return (slice(None), rows, slice(None) if j is None else _ds(j * c, c, 128))


def _slots_shape(name):
    _, a, b, c = BIG_SPECS[name]
    return (a, N_CHIP, b, c)


def _slots_idx(name, j, h):
    return (slice(None), j, _half_rows(name, h), slice(None))


def _row_block(name):
    hs = BIG_SPECS[name][2] // 2
    return hs if hs <= ROWS else ROWS


def _remote(src, dst, send_sem, recv_sem, to):
    return pltpu.make_async_remote_copy(src_ref=src, dst_ref=dst, send_sem=send_sem, recv_sem=recv_sem, device_id=to,
                                        device_id_type=MESH)


HBM = pl.BlockSpec(memory_space=pltpu.HBM)
SEM = pl.BlockSpec(memory_space=pltpu.SEMAPHORE)
EFFECT = pltpu.CompilerParams(has_side_effects=pltpu.SideEffectType.DATAFLOW_SIDE_EFFECTING)


def _in_hbm(a):
    return pltpu.with_memory_space_constraint(a, pltpu.HBM)


def _chip_copies(names, ins, lands, send_sems, recv_sems):
    x, y, c, chips = _place()
    return [_remote(ins[a].at[_shard_idx(name, c)], lands[a].at[_slots_idx(name, 2 * x + y, c)], send_sems.at[3 * a + k],
                    recv_sems.at[3 * a + k], (chip[0], chip[1], c))
            for a, name in enumerate(names) for k, chip in enumerate(chips)]


def _copies_start(call_name, copies, sources, land_shapes, per_source=3, land_dtype=MM):
    n = len(sources)

    def body(*refs):
        ins, lands, send_sems, recv_sems, token = refs[:n], refs[n:2 * n], refs[2 * n], refs[2 * n + 1], refs[-1]
        for cp in copies(ins, lands, send_sems, recv_sems):
            cp.start()
        token[...] = jnp.zeros_like(token)

    ins = [_in_hbm(a) for a in sources]
    lands = [_in_hbm(lax.empty(shape, land_dtype)) for shape in land_shapes]
    sems = (pltpu.SemaphoreType.DMA((per_source * n,)), pltpu.SemaphoreType.DMA((per_source * n,)))
    outs = pl.pallas_call(
        body, name=call_name, in_specs=[HBM] * (2 * n),
        out_specs=(SEM, SEM) + (HBM,) * (2 * n) + (pl.BlockSpec(memory_space=pltpu.VMEM),),
        out_shape=sems + tuple(pltpu.HBM(a.shape, a.dtype) for a in ins + lands) + (jax.ShapeDtypeStruct((8, HEAD_DIM), F32),),
        input_output_aliases={a: 2 + a for a in range(2 * n)}, compiler_params=EFFECT,
    )(*ins, *lands)
    return outs[:-1], outs[-1]


def _copies_wait(call_name, copies, state, after):
    n = (len(state) - 2) // 2

    def body(*refs):
        send_sems, recv_sems, ins, lands = refs[0], refs[1], refs[2:2 + n], refs[2 + n:2 + 2 * n]
        for cp in copies(ins, lands, send_sems, recv_sems):
            cp.wait_send()
            cp.wait_recv()

    outs = pl.pallas_call(
        body, name=call_name, in_specs=[SEM, SEM] + [HBM] * (2 * n) + [ANY], out_specs=(HBM,) * (2 * n),
        out_shape=tuple(pltpu.HBM(a.shape, a.dtype) for a in state[2:]),
        input_output_aliases={2 + a: a for a in range(2 * n)}, compiler_params=EFFECT,
    )(*state, after)
    return outs[:n], outs[n:]


def all_gather_start(shards, names):
    return _copies_start("all_gather_start", functools.partial(_chip_copies, names), [shards[name] for name in names],
                         [_slots_shape(name) for name in names])


def all_gather_wait(state, names, after):
    ins, lands = _copies_wait("all_gather_wait", functools.partial(_chip_copies, names), state, after)
    return dict(zip(names, ins)), dict(zip(names, lands))


def _chip_sends(names, ins, lands, send_sems, recv_sems):
    x, y, c, chips = _place()
    return [_remote(ins[a].at[_full_idx(name, 2 * chip[0] + chip[1])], lands[a].at[k], send_sems.at[3 * a + k],
                    recv_sems.at[3 * a + k], (chip[0], chip[1], c))
            for a, name in enumerate(names) for k, chip in enumerate(chips)]


def _got_shape(name):
    _, a_, b_, c_ = BIG_SPECS[name]
    return (3, a_, b_ // 2, c_)


def rs_chip_start(pairs, names, tag):
    return _copies_start("rs_chip_start_" + tag, functools.partial(_chip_sends, names), [pairs[name] for name in names],
                         [_got_shape(name) for name in names])


def rs_chip_wait(state, names, tag, after):
    _, lands = _copies_wait("rs_chip_wait_" + tag, functools.partial(_chip_sends, names), state, after)
    return dict(zip(names, lands))


def all_gather_pass_on(lands, names):
    n = len(names)

    def body(*refs):
        outs, send_sems, recv_sems = refs[n:2 * n], refs[2 * n], refs[2 * n + 1]
        x, y, c, chips = _place()
        work = [(3 * a + k, a, name, 2 * chip[0] + chip[1]) for a, name in enumerate(names) for k, chip in enumerate(chips)]
        cps = []
        for s, a, name, slot in work:
            landed = outs[a].at[_slots_idx(name, slot, c)]
            cps.append(_remote(landed, landed, send_sems.at[s], recv_sems.at[s], (x, y, 1 - c)))
            cps[-1].start()
        for s, a, name, slot in work:
            passed = outs[a].at[_slots_idx(name, slot, 1 - c)]
            _remote(passed, passed, send_sems.at[s], recv_sems.at[s], (x, y, 1 - c)).wait_recv()
        for cp in cps:
            cp.wait_send()

    outs = pl.pallas_call(
        body, name="all_gather_pass_on", in_specs=[ANY] * n, out_specs=[ANY] * n,
        input_output_aliases={a: a for a in range(n)},
        out_shape=[jax.ShapeDtypeStruct(_slots_shape(name), MM) for name in names],
        scratch_shapes=[pltpu.SemaphoreType.DMA((3 * n,)), pltpu.SemaphoreType.DMA((3 * n,))],
    )(*[lands[name] for name in names])
    return dict(zip(names, outs))


def all_gather_big(shards, names):
    n = len(names)

    def body(*refs):
        ins, outs = refs[:n], refs[n:2 * n]
        send_sems, recv_sems, fsend_sems, frecv_sems = refs[2 * n:]
        x, y, c, chips = _place()
        me_chip, sibling = 2 * x + y, (x, y, 1 - c)
        work = [(3 * a + k, a, name, chip) for a, name in enumerate(names) for k, chip in enumerate(chips)]
        sends = []
        for s, a, name, chip in work:
            cp = _remote(ins[a].at[_shard_idx(name, c)], outs[a].at[_slots_idx(name, me_chip, c)], send_sems.at[s],
                         recv_sems.at[s], (chip[0], chip[1], c))
            cp.start()
            sends.append(cp)
        for s, a, name, chip in work:
            landed = outs[a].at[_slots_idx(name, 2 * chip[0] + chip[1], c)]
            _remote(landed, landed, send_sems.at[s], recv_sems.at[s], (chip[0], chip[1], c)).wait_recv()
            cp = _remote(landed, landed, fsend_sems.at[s], frecv_sems.at[s], sibling)
            cp.start()
            sends.append(cp)
        for s, a, name, chip in work:
            passed = outs[a].at[_slots_idx(name, 2 * chip[0] + chip[1], 1 - c)]
            _remote(passed, passed, fsend_sems.at[s], frecv_sems.at[s], sibling).wait_recv()
        for cp in sends:
            cp.wait_send()

    outs = pl.pallas_call(
        body, name="all_gather_big", in_specs=[ANY] * n, out_specs=[ANY] * n,
        out_shape=[jax.ShapeDtypeStruct(_slots_shape(name), MM) for name in names],
        scratch_shapes=[pltpu.SemaphoreType.DMA((3 * n,))] * 4,
    )(*[shards[name] for name in names])
    return dict(zip(names, outs))


def with_own_slot(name, full, shard, chip):
    return lax.dynamic_update_slice(full, shard[:, None], (0, chip, 0, 0))


def rs_pair_exchange_big(grads, names, tag):
    n = len(names)

    def body(*refs):
        ins, outs, send_sems, recv_sems = refs[:n], refs[n:2 * n], refs[2 * n], refs[2 * n + 1]
        x, y, c, _ = _place()
        cps = []
        for a, name in enumerate(names):
            cp = _remote(ins[a].at[_full_idx(name, None, 1 - c)], outs[a], send_sems.at[a], recv_sems.at[a], (x, y, 1 - c))
            cp.start()
            cps.append(cp)
        for cp in cps:
            cp.wait()

    outs = pl.pallas_call(
        body, name="rs_pair_exchange_" + tag, in_specs=[ANY] * n, out_specs=[ANY] * n,
        out_shape=[jax.ShapeDtypeStruct(_full_shape(name, half=True), F32) for name in names],
        scratch_shapes=[pltpu.SemaphoreType.DMA((n,)), pltpu.SemaphoreType.DMA((n,))],
    )(*[grads[name] for name in names])
    return dict(zip(names, outs))


def rs_pair_add_big(name, place, g, got):
    kind, a_, b_, c_ = BIG_SPECS[name]
    rb = _row_block(name)
    nb = (b_ // 2) // rb

    def body(place_ref, g_ref, got_ref, o_ref):
        o_ref[...] = (g_ref[...] + got_ref[...]).astype(o_ref.dtype)

    if kind == "rows":
        g_spec = pl.BlockSpec((None, None, rb, c_), lambda a, j, i, p: (a, j, p[0] * nb + i, 0))
        o_spec = pl.BlockSpec((None, None, rb, c_), lambda a, j, i, p: (a, j, i, 0))
    else:
        g_spec = pl.BlockSpec((None, rb, c_), lambda a, j, i, p: (a, p[0] * nb + i, j))
        o_spec = pl.BlockSpec((None, rb, c_), lambda a, j, i, p: (a, i, j))
    return pl.pallas_call(
        body, name="rs_pair_add_" + name,
        grid_spec=pltpu.PrefetchScalarGridSpec(num_scalar_prefetch=1, grid=(a_, N_CHIP, nb), in_specs=[g_spec, o_spec],
                                               out_specs=o_spec),
        out_shape=jax.ShapeDtypeStruct(_full_shape(name, half=True), MM),
        compiler_params=_cparams(("parallel", "parallel", "parallel")),
    )(place, g, got)


def rs_chip_add_big(name, place, g, got_pair, got_chips):
    kind, a_, b_, c_ = BIG_SPECS[name]
    rb = _row_block(name)
    nb = (b_ // 2) // rb

    def body(place_ref, g_ref, s_ref, r0_ref, r1_ref, r2_ref, o_ref):
        own = g_ref[...] + s_ref[...]
        o_ref[...] = ((own + r0_ref[...].astype(F32)) + r1_ref[...].astype(F32)) + r2_ref[...].astype(F32)

    if kind == "rows":
        g_spec = pl.BlockSpec((None, None, rb, c_), lambda a, i, p: (a, p[1], p[0] * nb + i, 0))
        s_spec = pl.BlockSpec((None, None, rb, c_), lambda a, i, p: (a, p[1], i, 0))
    else:
        g_spec = pl.BlockSpec((None, rb, c_), lambda a, i, p: (a, p[0] * nb + i, p[1]))
        s_spec = pl.BlockSpec((None, rb, c_), lambda a, i, p: (a, i, p[1]))
    r_spec = lambda k: pl.BlockSpec((None, None, rb, c_), lambda a, i, p: (k, a, i, 0))
    return pl.pallas_call(
        body, name="rs_chip_add_" + name,
        grid_spec=pltpu.PrefetchScalarGridSpec(
            num_scalar_prefetch=1, grid=(a_, nb), in_specs=[g_spec, s_spec, r_spec(0), r_spec(1), r_spec(2)],
            out_specs=pl.BlockSpec((None, rb, c_), lambda a, i, p: (a, p[0] * nb + i, 0))),
        out_shape=jax.ShapeDtypeStruct((a_, b_, c_), F32), compiler_params=_cparams(("parallel", "parallel")),
    )(place, g, got_pair, got_chips, got_chips, got_chips)


def rs_pair_gather_big(halves, tag):
    names = tuple(halves)
    n = len(names)

    def body(*refs):
        outs, send_sems, recv_sems = refs[n:2 * n], refs[2 * n], refs[2 * n + 1]
        x, y, c, _ = _place()
        cps = []
        for a, name in enumerate(names):
            mine = outs[a].at[_shard_idx(name, c)]
            cp = _remote(mine, mine, send_sems.at[a], recv_sems.at[a], (x, y, 1 - c))
            cp.start()
            cps.append(cp)
        for a, name in enumerate(names):
            cps[a].wait_send()
            theirs = outs[a].at[_shard_idx(name, 1 - c)]
            _remote(theirs, theirs, send_sems.at[a], recv_sems.at[a], (x, y, 1 - c)).wait_recv()

    outs = pl.pallas_call(
        body, name="rs_pair_gather_" + tag, in_specs=[ANY] * n, out_specs=[ANY] * n,
        input_output_aliases={a: a for a in range(n)},
        out_shape=[jax.ShapeDtypeStruct(BIG_SPECS[name][1:], F32) for name in names],
        scratch_shapes=[pltpu.SemaphoreType.DMA((n,)), pltpu.SemaphoreType.DMA((n,))],
    )(*[halves[name] for name in names])
    return dict(zip(names, outs))


def _pair_sends(names, ins, lands, send_sems, recv_sems):
    x, y, c, _ = _place()
    return [_remote(ins[a].at[_full_idx(name, None, 1 - c)], lands[a], send_sems.at[a], recv_sems.at[a], (x, y, 1 - c))
            for a, name in enumerate(names)]


def rs_pair_start(grads, names, tag):
    return _copies_start("rs_pair_start_" + tag, functools.partial(_pair_sends, names), [grads[name] for name in names],
                         [_full_shape(name, half=True) for name in names], per_source=1, land_dtype=F32)


def rs_middle(pair_state, names, tag, place, after):
    ins, lands = _copies_wait("rs_pair_wait_" + tag, functools.partial(_pair_sends, names), pair_state, after)
    grads, got_pair = dict(zip(names, ins)), dict(zip(names, lands))
    pairs = {name: rs_pair_add_big(name, place, grads[name], got_pair[name]) for name in names}
    state, token = rs_chip_start(pairs, names, tag)
    return (grads, got_pair, state), token


def rs_end(begun, names, tag, place, after):
    grads, got_pair, state = begun
    got_chips = rs_chip_wait(state, names, tag, after)
    return {name: rs_chip_add_big(name, place, grads[name], got_pair[name], got_chips[name]) for name in names}


PACK_W = 1024
SMALL =(("mem_norm_w", 1024), ("mem_k_norm_w", 128), ("norm1_w", 2048), ("dn_a_log", 8), ("dn_dt_bias", 8),
         ("dn_o_norm_w", 128), ("fox_f_bias", 8), ("fox_q_norm_w", 128), ("fox_k_norm_w", 128), ("memq_norm_w", 256),
         ("norm2_w", 2048))
SMALL_ROWS = 8
CONV_ROWS = 4 * 3 * D_MODEL // PACK_W
LOSS_AT = sum(n for _, n in SMALL)


def pack_small(parts, extra=None):
    flat = [parts[name].astype(F32).reshape(-1) for name, _ in SMALL]
    used = LOSS_AT
    if extra is not None:
        flat.append(extra.reshape(1))
        used += 1
    flat.append(jnp.zeros((SMALL_ROWS * PACK_W - used,), F32))
    return jnp.concatenate(flat).reshape(SMALL_ROWS, PACK_W)


def unpack_small(packed, shapes):
    flat, out, at = packed.reshape(-1), {}, 0
    for name, n in SMALL:
        out[name] = flat[at:at + n].reshape(shapes[name])
        at += n
    return out


def _adam_all(w, g, m, v, name):
    shape = w.shape
    r2 = lambda a: a.reshape(-1, shape[-1])
    d, nm, nv = adamw(r2(w), r2(g), r2(m), r2(v), name=name)
    return d.reshape(shape), nm.reshape(shape), nv.reshape(shape)


WEIGHTS = ("mem_norm_w", "w_mem_kv", "mem_k_norm_w", "norm1_w", "dn_w_in", "dn_conv_w", "dn_a_log", "dn_dt_bias",
           "dn_o_norm_w", "fox_w_in", "fox_f_bias", "fox_q_norm_w", "fox_k_norm_w", "memq_norm_w", "w_out", "norm2_w",
           "w_mlp1", "w_mlp2")


def kernel(x, mem, mem_norm_w, w_mem_kv, mem_k_norm_w, norm1_w, dn_w_in, dn_conv_w, dn_a_log, dn_dt_bias, dn_o_norm_w, fox_w_in, fox_f_bias, fox_q_norm_w, fox_k_norm_w, memq_norm_w, w_out, norm2_w, w_mlp1, w_mlp2, loss_target, m_mem_norm_w, m_w_mem_kv, m_mem_k_norm_w, m_norm1_w, m_dn_w_in, m_dn_conv_w, m_dn_a_log, m_dn_dt_bias, m_dn_o_norm_w, m_fox_w_in, m_fox_f_bias, m_fox_q_norm_w, m_fox_k_norm_w, m_memq_norm_w, m_w_out, m_norm2_w, m_w_mlp1, m_w_mlp2, v_mem_norm_w, v_w_mem_kv, v_mem_k_norm_w, v_norm1_w, v_dn_w_in, v_dn_conv_w, v_dn_a_log, v_dn_dt_bias, v_dn_o_norm_w, v_fox_w_in, v_fox_f_bias, v_fox_q_norm_w, v_fox_k_norm_w, v_memq_norm_w, v_w_out, v_norm2_w, v_w_mlp1, v_w_mlp2):
    args = dict(locals())
    w = {n: args[n] for n in WEIGHTS}
    m = {n: args["m_" + n] for n in WEIGHTS}
    v = {n: args["v_" + n] for n in WEIGHTS}
    core, chip = lax.axis_index("c"), 2 * lax.axis_index("x") + lax.axis_index("y")
    place = jnp.stack([core, chip]).astype(jnp.int32)

    shards = {name: w[name].reshape(BIG_SPECS[name][1:]).astype(MM) for name in BIG_NAMES}
    w_in_full = lambda arr, n_scalars: w_in_to_kernel(arr[0].transpose(1, 0, 2).reshape(D_MODEL, -1), n_scalars)
    early = {name: with_own_slot(name, arr, shards[name], chip)
             for name, arr in all_gather_big(shards, EARLY_NAMES).items()}
    conv_mine = jnp.where(core == 0, dn_conv_w[0], 0.0)
    conv_placed = lax.dynamic_update_slice(jnp.zeros((4, 3 * D_MODEL), F32), conv_mine, (0, 768 * chip))
    conv_full = all_reduce_small(jnp.pad(conv_placed.reshape(CONV_ROWS, PACK_W), ((0, 16 - CONV_ROWS), (0, 0))))
    late_shards, early, conv_full = lax.optimization_barrier(
        ({name: shards[name] for name in LATE_NAMES}, early, conv_full))
    late_state, token = all_gather_start(late_shards, LATE_NAMES)
    tie = token[0, 0]
    wt = dict(w_mem_kv=early["w_mem_kv"].reshape(D_MODEL, 2 * MEM_WIDTH) + tie.astype(MM),
              dn_w_in=w_in_full(early["dn_w_in"], 2 * N_HEADS), conv_w=conv_full[:CONV_ROWS].reshape(4, 3 * D_MODEL))

    def late(after):
        late_shards, lands = all_gather_wait(late_state, LATE_NAMES, after)
        full = {name: with_own_slot(name, arr, late_shards[name], chip)
                for name, arr in all_gather_pass_on(lands, LATE_NAMES).items()}
        return dict(fox_w_in=w_in_full(full["fox_w_in"], N_HEADS), w_out=full["w_out"].reshape(2, 3 * MEM_WIDTH, D_MODEL),
                    w_mlp1=full["w_mlp1"], w_mlp2=full["w_mlp2"].reshape(2, D_FF, D_MODEL))

    sm = dict(mem_norm_w=mem_norm_w, mem_k_norm_w=mem_k_norm_w, norm1_w=norm1_w, norm2_w=norm2_w, memq_norm_w=memq_norm_w,
              dn_a_log=dn_a_log[0], dn_dt_bias=dn_dt_bias[0], dn_o_norm_w=dn_o_norm_w, fox_f_bias=fox_f_bias[0],
              fox_q_norm_w=fox_q_norm_w, fox_k_norm_w=fox_k_norm_w)
    w_in_slots = lambda g, n_scalars: w_in_from_kernel(g, n_scalars).reshape(D_MODEL, N_CHIP, -1).transpose(1, 0, 2)[None]
    rows_view = lambda g, name: g.reshape(_full_shape(name))
    pair_started, begun = {}, {}

    def on_layer1(big1):
        grads1 = dict(fox_w_in=w_in_slots(big1["w_in"], N_HEADS), w_out_1=rows_view(big1["w_out"], "w_out_1"),
                      w_mlp2_1=rows_view(big1["w_mlp2"], "w_mlp2_1"), w_mlp1_1=big1["w_mlp1"][None])
        pair_started["layer1"], token = rs_pair_start(grads1, RS_LAYER1, "layer1")
        return token[0, 0]

    def on_mlp0(d_w_mlp2, d_w_mlp1, d_w_out):
        begun["layer1"], token1 = rs_middle(pair_started["layer1"], RS_LAYER1, "layer1", place, d_w_out)
        grads0 = dict(w_mlp2_0=rows_view(d_w_mlp2, "w_mlp2_0"), w_mlp1_0=d_w_mlp1[None],
                      w_out_0=rows_view(d_w_out, "w_out_0"))
        grads0, _ = lax.optimization_barrier((grads0, token1))
        pair_started["mlp0"], token0 = rs_pair_start(grads0, RS_MLP0, "mlp0")
        return token1[0, 0] + token0[0, 0]

    def on_core0(d_o):
        begun["mlp0"], token = rs_middle(pair_started["mlp0"], RS_MLP0, "mlp0", place, d_o)
        return token[0, 0]

    loss_part, dx, big, small = local_step(x[0], mem[0], loss_target[0], wt, sm, late, on_layer1, on_mlp0, on_core0)
    small_pack = jnp.concatenate([pack_small(small, loss_part[0, :1]), big["conv_w"].reshape(CONV_ROWS, PACK_W),
                                  jnp.zeros((24 - SMALL_ROWS - CONV_ROWS, PACK_W), F32)])
    small_all = all_reduce_small(small_pack)
    small_sum = small_all[:SMALL_ROWS]
    conv_sum = lax.dynamic_slice(small_all[SMALL_ROWS:SMALL_ROWS + CONV_ROWS].reshape(4, 3 * D_MODEL), (0, 768 * chip), (4, 768))
    loss = small_sum.reshape(-1)[LOSS_AT]
    halves = rs_end(begun["layer1"], RS_LAYER1, "layer1", place, small_all)
    halves.update(rs_end(begun["mlp0"], RS_MLP0, "mlp0", place, small_all))
    summed = rs_pair_gather_big(halves, "early")

    last = dict(dn_w_in=w_in_slots(big["dn_w_in"], 2 * N_HEADS), w_mem_kv=rows_view(big["w_mem_kv"], "w_mem_kv"))
    last, summed = lax.optimization_barrier((last, summed))
    got_pair = rs_pair_exchange_big(last, RS_LAST, "last")
    pairs = {name: rs_pair_add_big(name, place, last[name], got_pair[name]) for name in RS_LAST}
    last_state, token = rs_chip_start(pairs, RS_LAST, "last")
    summed, token = lax.optimization_barrier((summed, token))

    big_sum = {"fox_w_in": summed["fox_w_in"]}
    big_sum.update({name: jnp.concatenate([summed[name + "_0"], summed[name + "_1"]]) for name in ("w_out", "w_mlp2", "w_mlp1")})
    grads = unpack_small(small_sum, {n: w[n].shape for n, _ in SMALL})
    grads.update({name: big_sum[name].reshape(w[name].shape) for name in big_sum}, dn_conv_w=conv_sum[None])
    delta, new_m, new_v = {}, {}, {}
    for n in ("fox_w_in", "w_out", "w_mlp1", "w_mlp2", "dn_conv_w"):
        delta[n], new_m[n], new_v[n] = _adam_all(w[n], grads[n], m[n], v[n], "adamw_" + n)

    got_chips = rs_chip_wait(last_state, RS_LAST, "last", delta["w_mlp2"])
    summed_last = rs_pair_gather_big({name: rs_chip_add_big(name, place, last[name], got_pair[name], got_chips[name])
                                      for name in RS_LAST}, "last")
    for n in RS_LAST:
        grads[n] = summed_last[n].reshape(w[n].shape)
        delta[n], new_m[n], new_v[n] = _adam_all(w[n], grads[n], m[n], v[n], "adamw_" + n)
    shapes = {n: w[n].shape for n, _ in SMALL}
    d_s, m_s, v_s = adamw(pack_small(w), small_sum, pack_small(m), pack_small(v), name="adamw_small")
    for out, packed in ((delta, d_s), (new_m, m_s), (new_v, v_s)):
        out.update(unpack_small(packed, shapes))
    return (loss, dx[None], *[grads[n] for n in WEIGHTS], *[delta[n] for n in WEIGHTS],
            *[new_m[n] for n in WEIGHTS], *[new_v[n] for n in WEIGHTS])
```

```python
import functools

import jax
import jax.numpy as jnp
from jax import lax
from jax.experimental import pallas as pl
from jax.experimental.pallas import tpu as pltpu

F32 = jnp.float32
MM = jnp.bfloat16
HI = lax.Precision.HIGHEST

D_MODEL = 1024
HEAD_DIM = 128
N_HEADS = 8
MEM_HEADS = 4
MEM_WIDTH = MEM_HEADS * HEAD_DIM
N_MEM = 256
D_FF = 4 * D_MODEL
CHUNK = 64
EPS = 1e-6
QSCALE = HEAD_DIM ** -0.5
PROJ_W = 4736
TAIL = 4608
TAIL_BLK = TAIL // HEAD_DIM
ROWS = 256
VMEM_LIMIT = 56 * 1024 * 1024

ADAM_LR = 0.001
ADAM_B1 = 0.9
ADAM_B2 = 0.999
ADAM_EPS = 1e-08
ADAM_WD = 0.01
ADAM_STEP = 10

N_DEV = 8
N_CHIP = 4
MESH = pl.DeviceIdType.MESH


def _cparams(sem=None):
    return pltpu.CompilerParams(dimension_semantics=sem, vmem_limit_bytes=VMEM_LIMIT)


def _dot(a, b, ca, cb, hi):
    dims = (((ca,), (cb,)), ((), ()))
    if hi:
        return lax.dot_general(a, b, dims, precision=HI, preferred_element_type=F32)
    return lax.dot_general(a.astype(MM), b.astype(MM), dims, preferred_element_type=F32)


@functools.partial(jax.custom_vjp, nondiff_argnums=(2, 3, 4))
def mmul(a, b, ca, cb, hi):
    return _dot(a, b, ca, cb, hi)


def _mmul_fwd(a, b, ca, cb, hi):
    return _dot(a, b, ca, cb, hi), (a, b)


def _mmul_bwd(ca, cb, hi, res, g):
    a, b = res
    if ca == 1:
        da = _dot(g, b, 1, 1, hi) if cb == 0 else _dot(g, b, 1, 0, hi)
    else:
        da = _dot(b, g, 1, 1, hi) if cb == 0 else _dot(b, g, 0, 1, hi)
    if cb == 0:
        db = _dot(a, g, 0, 0, hi) if ca == 1 else _dot(a, g, 1, 0, hi)
    else:
        db = _dot(g, a, 0, 0, hi) if ca == 1 else _dot(g, a, 0, 1, hi)
    return da.astype(a.dtype), db.astype(b.dtype)


mmul.defvjp(_mmul_fwd, _mmul_bwd)


def _iota2(n, m):
    return lax.broadcasted_iota(jnp.int32, (n, m), 0), lax.broadcasted_iota(jnp.int32, (n, m), 1)


def _same_block(r, c, shift):
    return lax.shift_right_logical(r, shift) == lax.shift_right_logical(c, shift)


def _split_bf16(x):
    hi = x.astype(jnp.bfloat16)
    return hi, (x - hi.astype(F32)).astype(jnp.bfloat16)


def _dot3(a, b, ca, cb):
    dims = (((ca,), (cb,)), ((), ()))
    (ah, al), (bh, bl) = _split_bf16(a), _split_bf16(b)
    d = lambda x, y: lax.dot_general(x, y, dims, preferred_element_type=F32)
    return d(ah, bh) + (d(ah, bl) + d(al, bh))


def _tri_inv_impl(a):
    n = a.shape[0]
    r, c = _iota2(n, n)
    eye = (r == c).astype(F32)
    b16, b32 = _same_block(r, c, 4), _same_block(r, c, 5)
    a0 = jnp.where(b16, a, 0.0)
    p = eye - a0
    b = _dot3(a0, a0, 1, 0)
    p = p + _dot3(p, b, 1, 0)
    b = _dot3(b, b, 1, 0)
    p = p + _dot3(p, b, 1, 0)
    b = _dot3(b, b, 1, 0)
    p = p + _dot3(p, b, 1, 0)
    a1 = jnp.where(jnp.logical_and(b32, jnp.logical_not(b16)), a, 0.0)
    p = p - _dot3(_dot3(p, a1, 1, 0), p, 1, 0)
    a2 = jnp.where(b32, 0.0, a)
    p = p - _dot3(_dot3(p, a2, 1, 0), p, 1, 0)
    return p


@jax.custom_vjp
def tri_inv(a):
    return _tri_inv_impl(a)


def _tri_inv_fwd(a):
    p = _tri_inv_impl(a)
    return p, p


def _tri_inv_bwd(p, g):
    return (-_dot3(_dot3(p, g, 0, 0), p, 1, 1),)


tri_inv.defvjp(_tri_inv_fwd, _tri_inv_bwd)


def _sigmoid(x):
    return 1.0 / (1.0 + jnp.exp(-x))


def _softplus(x):
    return jnp.maximum(x, 0.0) + jnp.log(1.0 + jnp.exp(-jnp.abs(x)))


def _silu(x):
    return x * _sigmoid(x)


def _rms(x, w):
    return x * lax.rsqrt(jnp.mean(x * x, axis=-1, keepdims=True) + EPS) * w


def _bf_round(x):
    return x.astype(MM).astype(F32)


def _acc(ref, val, first):
    @pl.when(first)
    def _():
        ref[...] = val

    @pl.when(jnp.logical_not(first))
    def _():
        ref[...] += val


def _tile(n, pref):
    if n % pref == 0:
        return pref
    return n


def matmul(a, b, *, ta=False, tb=False, b_slots=False, b_layer=None, res=None, also_sqrelu=False, times_dsqrelu=None,
           out_dtype=F32, name, tm=1024, tn=1024, tk=1024):
    m, k = (a.shape[1], a.shape[0]) if ta else a.shape
    b_shape = b.shape if b_layer is None else b.shape[1:]
    if b_slots:
        n = b_shape[1] if tb else N_CHIP * b_shape[2]
        assert (N_CHIP * b_shape[2] if tb else b_shape[1]) == k, (a.shape, b.shape, ta, tb)
        tn, tk = (tn, b_shape[2]) if tb else (b_shape[2], tk)
    else:
        n = b_shape[0] if tb else b_shape[1]
        assert (b_shape[1] if tb else b_shape[0]) == k, (a.shape, b.shape, ta, tb)
    tm, tn, tk = _tile(m, tm), _tile(n, tn), _tile(k, tk)
    nk = k // tk
    ca, cb = (0 if ta else 1), (1 if tb else 0)

    extra = tuple(e for e in (res, times_dsqrelu) if e is not None)
    assert len(extra) <= 1

    def body(a_ref, b_ref, *rest):
        e_ref = rest[0] if extra else None
        o_ref = rest[len(extra)]

        def finish(total):
            if res is not None:
                total = total + e_ref[...]
            if times_dsqrelu is not None:
                total = total * (2.0 * jnp.maximum(e_ref[...], 0.0))
            o_ref[...] = total.astype(o_ref.dtype)
            if also_sqrelu:
                rest[len(extra) + 1][...] = _sqrelu(total).astype(MM)

        if nk == 1:
            finish(_dot(a_ref[...], b_ref[...], ca, cb, False))
            return
        acc_ref, kk = rest[-1], pl.program_id(2)

        @pl.when(kk == 0)
        def _():
            acc_ref[...] = jnp.zeros_like(acc_ref)

        acc_ref[...] += _dot(a_ref[...], b_ref[...], ca, cb, False)

        @pl.when(kk == nk - 1)
        def _():
            finish(acc_ref[...])

    a_spec = pl.BlockSpec((tk, tm), lambda i, j, l: (l, i)) if ta else pl.BlockSpec((tm, tk), lambda i, j, l: (i, l))
    lead = () if b_layer is None else (b_layer,)
    if b_slots:
        b_block, b_index = ((None, tn, tk), lambda i, j, l: (l, j, 0)) if tb else ((None, tk, tn), lambda i, j, l: (j, l, 0))
    else:
        b_block, b_index = ((tn, tk), lambda i, j, l: (j, l)) if tb else ((tk, tn), lambda i, j, l: (l, j))
    b_spec = pl.BlockSpec((None,) * len(lead) + b_block, lambda i, j, l: lead + b_index(i, j, l))
    o_spec = pl.BlockSpec((tm, tn), lambda i, j, l: (i, j))
    out_shape = [jax.ShapeDtypeStruct((m, n), out_dtype)] + [jax.ShapeDtypeStruct((m, n), MM)] * also_sqrelu
    outs = pl.pallas_call(
        body, name=name, grid=(m // tm, n // tn, nk),
        in_specs=[a_spec, b_spec] + [o_spec] * len(extra), out_specs=[o_spec] * len(out_shape), out_shape=out_shape,
        scratch_shapes=[pltpu.VMEM((tm, tn), F32)] * (nk > 1),
        compiler_params=_cparams(("parallel", "parallel", "arbitrary")),
    )(a, b, *extra)
    return outs if also_sqrelu else outs[0]


def rms_fwd(x, w, *, name):
    t, d = x.shape

    def body(x_ref, w_ref, o_ref):
        o_ref[...] = _rms(x_ref[...], w_ref[...]).astype(o_ref.dtype)

    return pl.pallas_call(
        body, name=name, grid=(t // ROWS,),
        in_specs=[pl.BlockSpec((ROWS, d), lambda i: (i, 0)), pl.BlockSpec((1, d), lambda i: (0, 0))],
        out_specs=pl.BlockSpec((ROWS, d), lambda i: (i, 0)),
        out_shape=jax.ShapeDtypeStruct((t, d), MM), compiler_params=_cparams(("parallel",)),
    )(x, w)


def rms_bwd(x, w, dh, dres, *, name):
    t, d = x.shape

    def body(x_ref, w_ref, dh_ref, dr_ref, dx_ref, dw_ref):
        _, vjp = jax.vjp(_rms, x_ref[...], w_ref[...])
        dx, dw = vjp(dh_ref[...].astype(F32))
        dx_ref[...] = dx + dr_ref[...]
        _acc(dw_ref, dw, pl.program_id(0) == 0)

    row = pl.BlockSpec((ROWS, d), lambda i: (i, 0))
    vec = pl.BlockSpec((1, d), lambda i: (0, 0))
    return pl.pallas_call(
        body, name=name, grid=(t // ROWS,), in_specs=[row, vec, row, row], out_specs=[row, vec],
        out_shape=[jax.ShapeDtypeStruct((t, d), F32), jax.ShapeDtypeStruct((1, d), F32)],
        compiler_params=_cparams(("arbitrary",)),
    )(x, w, dh, dres)


def _sqrelu(x):
    return jnp.square(jnp.maximum(x, 0.0))


def loss_fwd(y, target, *, name):
    t, d = y.shape

    def body(y_ref, t_ref, dy_ref, l_ref):
        e = y_ref[...] - t_ref[...]
        dy_ref[...] = e * (1.0 / d)
        part = 0.5 * jnp.sum(jnp.sum(e * e, axis=-1, keepdims=True) * (1.0 / d), axis=0, keepdims=True)
        _acc(l_ref, jnp.broadcast_to(part, (1, HEAD_DIM)), pl.program_id(0) == 0)

    blk = pl.BlockSpec((ROWS, d), lambda i: (i, 0))
    return pl.pallas_call(
        body, name=name, grid=(t // ROWS,), in_specs=[blk, blk],
        out_specs=[blk, pl.BlockSpec((1, HEAD_DIM), lambda i: (0, 0))],
        out_shape=[jax.ShapeDtypeStruct((t, d), F32), jax.ShapeDtypeStruct((1, HEAD_DIM), F32)],
        compiler_params=_cparams(("arbitrary",)),
    )(y, target)


def _mem_kv(mem, wn, wkn, *ws):
    mn = _rms(mem, wn)
    outs = []
    for h in range(MEM_HEADS):
        outs.append(_rms(mmul(mn, ws[h], 1, 0, False), wkn))
    for h in range(MEM_HEADS):
        outs.append(mmul(mn, ws[MEM_HEADS + h], 1, 0, False))
    return tuple(outs)


def _w_cols(w_ref):
    return [w_ref[:, h * HEAD_DIM:(h + 1) * HEAD_DIM] for h in range(2 * MEM_HEADS)]


def mem_fwd(mem, wn, wkv, wkn):
    def body(mem_ref, wn_ref, w_ref, wkn_ref, k_ref, v_ref):
        outs = _mem_kv(mem_ref[...], wn_ref[...], wkn_ref[...], *_w_cols(w_ref))
        for h in range(MEM_HEADS):
            k_ref[:, h * HEAD_DIM:(h + 1) * HEAD_DIM] = outs[h]
            v_ref[:, h * HEAD_DIM:(h + 1) * HEAD_DIM] = outs[MEM_HEADS + h]

    shp = jax.ShapeDtypeStruct((mem.shape[0], MEM_WIDTH), F32)
    return pl.pallas_call(body, name="mem_fwd", out_shape=[shp, shp], compiler_params=_cparams())(mem, wn, wkv, wkn)


def mem_bwd(mem, wn, wkv, wkn, dk0, dv0, dk1, dv1):
    def body(mem_ref, wn_ref, w_ref, wkn_ref, dk0_ref, dv0_ref, dk1_ref, dv1_ref, dwn_ref, dw_ref, dwkn_ref):
        _, vjp = jax.vjp(lambda wn_, wkn_, *ws: _mem_kv(mem_ref[...], wn_, wkn_, *ws),
                         wn_ref[...], wkn_ref[...], *[w.astype(F32) for w in _w_cols(w_ref)])
        cols = lambda a, b: tuple(a[:, h * HEAD_DIM:(h + 1) * HEAD_DIM] + b[:, h * HEAD_DIM:(h + 1) * HEAD_DIM]
                                  for h in range(MEM_HEADS))
        cts = cols(dk0_ref, dk1_ref) + cols(dv0_ref, dv1_ref)
        grads = vjp(cts)
        dwn_ref[...] = grads[0]
        dwkn_ref[...] = grads[1]
        for h in range(2 * MEM_HEADS):
            dw_ref[:, h * HEAD_DIM:(h + 1) * HEAD_DIM] = grads[2 + h]

    return pl.pallas_call(
        body, name="mem_bwd",
        out_shape=[jax.ShapeDtypeStruct((1, D_MODEL), F32), jax.ShapeDtypeStruct((D_MODEL, 2 * MEM_WIDTH), F32),
                   jax.ShapeDtypeStruct((1, HEAD_DIM), F32)],
        compiler_params=_cparams(),
    )(mem, wn, wkv, wkn, dk0, dv0, dk1, dv1)


def _memattn(q, wq, mk, mv):
    qn = _rms(q, wq) * QSCALE
    s = mmul(qn, mk, 1, 1, False)
    s = s - jnp.max(s, axis=-1, keepdims=True)
    p = jnp.exp(s)
    p = p / jnp.sum(p, axis=-1, keepdims=True)
    return mmul(p, mv, 1, 0, False)


def _lanes(j):
    return slice(j * HEAD_DIM, (j + 1) * HEAD_DIM)


def _memattn_specs(t):
    qspec = pl.BlockSpec((ROWS, MEM_WIDTH), lambda i: (i, (TAIL - MEM_WIDTH) // MEM_WIDTH))
    wspec = pl.BlockSpec((1, HEAD_DIM), lambda i: (0, 0))
    mspec = pl.BlockSpec((N_MEM, MEM_WIDTH), lambda i: (0, 0))
    ospec = pl.BlockSpec((ROWS, MEM_WIDTH), lambda i: (i, 0))
    return qspec, wspec, mspec, ospec


def memattn_fwd(proj, wq, mk, mv, *, name):
    t = proj.shape[0]
    qspec, wspec, mspec, ospec = _memattn_specs(t)

    def body(q_ref, w_ref, k_ref, v_ref, o_ref):
        for h in range(MEM_HEADS):
            o_ref[:, _lanes(h)] = _memattn(q_ref[:, _lanes(h)], w_ref[...], k_ref[:, _lanes(h)],
                                           v_ref[:, _lanes(h)]).astype(o_ref.dtype)

    return pl.pallas_call(
        body, name=name, grid=(t // ROWS,), in_specs=[qspec, wspec, mspec, mspec], out_specs=ospec,
        out_shape=jax.ShapeDtypeStruct((t, MEM_WIDTH), MM), compiler_params=_cparams(("parallel",)),
    )(proj, wq, mk, mv)


def memattn_bwd(proj, wq, mk, mv, dcat, *, name):
    t = proj.shape[0]
    qspec, wspec, mspec, ospec = _memattn_specs(t)
    dospec = pl.BlockSpec((ROWS, MEM_WIDTH), lambda i: (i, D_MODEL // MEM_WIDTH))

    def body(q_ref, w_ref, k_ref, v_ref, do_ref, dq_ref, dw_ref, dk_ref, dv_ref):
        first = pl.program_id(0) == 0
        dw_sum = jnp.zeros((1, HEAD_DIM), F32)
        for h in range(MEM_HEADS):
            _, vjp = jax.vjp(_memattn, q_ref[:, _lanes(h)], w_ref[...], k_ref[:, _lanes(h)], v_ref[:, _lanes(h)])
            dq, dw, dk, dv = vjp(do_ref[:, _lanes(h)].astype(F32))
            dq_ref[:, _lanes(h)] = dq.astype(dq_ref.dtype)
            dw_sum = dw_sum + dw
            _acc(dk_ref.at[:, _lanes(h)], dk, first)
            _acc(dv_ref.at[:, _lanes(h)], dv, first)
        _acc(dw_ref, dw_sum, first)

    mshape = jax.ShapeDtypeStruct((N_MEM, MEM_WIDTH), F32)
    return pl.pallas_call(
        body, name=name, grid=(t // ROWS,), in_specs=[qspec, wspec, mspec, mspec, dospec],
        out_specs=[ospec, wspec, mspec, mspec],
        out_shape=[jax.ShapeDtypeStruct((t, MEM_WIDTH), MM), jax.ShapeDtypeStruct((1, HEAD_DIM), F32), mshape, mshape],
        compiler_params=_cparams(("arbitrary",)),
    )(proj, wq, mk, mv, dcat)


def _shift_rows(x, s, up):
    n = x.shape[0]
    r = lax.broadcasted_iota(jnp.int32, x.shape, 0)
    if up:
        return jnp.where(r < n - s, pltpu.roll(x, n - s, 0), 0.0)
    return jnp.where(r >= s, pltpu.roll(x, s, 0), 0.0)


def _conv_fwd_vals(x, w):
    xb = _bf_round(x)
    wb = _bf_round(w)
    c = xb * wb[3:4, :]
    for j in range(3):
        c = c + _shift_rows(xb, 3 - j, False) * wb[j:j + 1, :]
    return xb, wb, c


def dn_prep_fwd(proj, conv_w):
    t = proj.shape[0]

    def body(x_ref, w_ref, o_ref):
        j = pl.program_id(0)
        _, _, c = _conv_fwd_vals(x_ref[...], w_ref[...])
        s = _silu(c)
        r = lax.rsqrt(jnp.sum(s * s, axis=-1, keepdims=True) + EPS)
        scale = jnp.where(j < N_HEADS, QSCALE, 1.0)
        o_ref[...] = jnp.where(j < 2 * N_HEADS, s * r * scale, s)

    return pl.pallas_call(
        body, name="dn_prep_fwd", grid=(3 * N_HEADS,),
        in_specs=[pl.BlockSpec((t, HEAD_DIM), lambda j: (0, j)), pl.BlockSpec((4, HEAD_DIM), lambda j: (0, j))],
        out_specs=pl.BlockSpec((None, t, HEAD_DIM), lambda j: (j // N_HEADS, 0, j % N_HEADS)),
        out_shape=jax.ShapeDtypeStruct((3, t, D_MODEL), F32), compiler_params=_cparams(("parallel",)),
    )(proj, conv_w)


def dn_prep_bwd(proj, conv_w, dqkv):
    t = proj.shape[0]

    def body(x_ref, w_ref, g_ref, dx_ref, dw_ref):
        j = pl.program_id(0)
        xb, wb, c = _conv_fwd_vals(x_ref[...], w_ref[...])
        sg = _sigmoid(c)
        s = c * sg
        g = g_ref[...]
        r = lax.rsqrt(jnp.sum(s * s, axis=-1, keepdims=True) + EPS)
        scale = jnp.where(j < N_HEADS, QSCALE, 1.0)
        gn = g * scale
        ds_norm = r * gn - s * (r * r * r) * jnp.sum(gn * s, axis=-1, keepdims=True)
        ds = jnp.where(j < 2 * N_HEADS, ds_norm, g)
        dc = ds * (sg + s * (1.0 - sg))
        dx = dc * wb[3:4, :]
        rows = [jnp.sum(dc * xb, axis=0, keepdims=True)]
        for jj in range(2, -1, -1):
            sh = 3 - jj
            dx = dx + _shift_rows(dc, sh, True) * wb[jj:jj + 1, :]
            rows.insert(0, jnp.sum(dc * _shift_rows(xb, sh, False), axis=0, keepdims=True))
        dx_ref[...] = dx.astype(dx_ref.dtype)
        dw_ref[...] = jnp.concatenate(rows + [jnp.zeros((4, HEAD_DIM), F32)], axis=0)

    col = pl.BlockSpec((t, HEAD_DIM), lambda j: (0, j))
    return pl.pallas_call(
        body, name="dn_prep_bwd", grid=(3 * N_HEADS,),
        in_specs=[col, pl.BlockSpec((4, HEAD_DIM), lambda j: (0, j)),
                  pl.BlockSpec((None, t, HEAD_DIM), lambda j: (j // N_HEADS, 0, j % N_HEADS))],
        out_specs=[col, pl.BlockSpec((8, HEAD_DIM), lambda j: (0, j))],
        out_shape=[jax.ShapeDtypeStruct((t, 3 * D_MODEL), MM), jax.ShapeDtypeStruct((8, 3 * D_MODEL), F32)],
        compiler_params=_cparams(("parallel",)),
    )(proj, conv_w, dqkv)


def _tri_ones(n, upper):
    r, c = _iota2(n, n)
    return (r <= c).astype(F32) if upper else (r >= c).astype(F32)


def dn_gates_fwd(proj, a_log, dt_bias):
    t = proj.shape[0]

    def body(x_ref, al_ref, dt_ref, o_ref):
        lane = lax.broadcasted_iota(jnp.int32, (CHUNK, HEAD_DIM), 1)
        tri = _tri_ones(CHUNK, False)

        def step(c, carry):
            rows = pl.ds(pl.multiple_of(c * CHUNK, CHUNK), CHUNK)
            x = x_ref[rows, :]
            g = jnp.where(lane < N_HEADS, -jnp.exp(al_ref[...]) * _softplus(x + dt_ref[...]), 0.0)
            gc = _dot(tri, g, 1, 0, True)
            o_ref[rows, :] = jnp.where(lane < N_HEADS, gc, jnp.where(lane < 2 * N_HEADS, _sigmoid(x), 0.0))
            return carry

        lax.fori_loop(0, t // CHUNK, step, 0)

    vec = pl.BlockSpec((1, HEAD_DIM), lambda i: (0, 0))
    return pl.pallas_call(
        body, name="dn_gates_fwd", grid=(1,),
        in_specs=[pl.BlockSpec((t, HEAD_DIM), lambda i: (0, TAIL_BLK)), vec, vec],
        out_specs=pl.BlockSpec((t, HEAD_DIM), lambda i: (0, 0)),
        out_shape=jax.ShapeDtypeStruct((t, HEAD_DIM), F32), compiler_params=_cparams(("arbitrary",)),
    )(proj, a_log, dt_bias)


def dn_gates_bwd(proj, a_log, dt_bias, dgates):
    t = proj.shape[0]

    def body(x_ref, al_ref, dt_ref, g_ref, dx_ref, dal_ref, ddt_ref):
        lane = lax.broadcasted_iota(jnp.int32, (CHUNK, HEAD_DIM), 1)
        tri = _tri_ones(CHUNK, True)
        dal_ref[...] = jnp.zeros_like(dal_ref)
        ddt_ref[...] = jnp.zeros_like(ddt_ref)

        def step(c, carry):
            rows = pl.ds(pl.multiple_of(c * CHUNK, CHUNK), CHUNK)
            x = x_ref[rows, :]
            dgc = jnp.where(lane < N_HEADS, g_ref[rows, :], 0.0)
            dg = _dot(tri, dgc, 1, 0, True)
            ea = -jnp.exp(al_ref[...])
            z = x + dt_ref[...]
            da = jnp.where(lane < N_HEADS, dg * ea * _sigmoid(z), 0.0)
            gval = jnp.where(lane < N_HEADS, ea * _softplus(z), 0.0)
            beta = _sigmoid(x)
            db = jnp.where(jnp.logical_and(lane >= N_HEADS, lane < 2 * N_HEADS), g_ref[rows, :] * beta * (1.0 - beta), 0.0)
            dx_ref[rows, :] = (da + db).astype(dx_ref.dtype)
            dal_ref[...] += jnp.sum(dg * gval, axis=0, keepdims=True)
            ddt_ref[...] += jnp.sum(da, axis=0, keepdims=True)
            return carry

        lax.fori_loop(0, t // CHUNK, step, 0)

    vec = pl.BlockSpec((1, HEAD_DIM), lambda i: (0, 0))
    full = pl.BlockSpec((t, HEAD_DIM), lambda i: (0, 0))
    return pl.pallas_call(
        body, name="dn_gates_bwd", grid=(1,),
        in_specs=[pl.BlockSpec((t, HEAD_DIM), lambda i: (0, TAIL_BLK)), vec, vec, full],
        out_specs=[full, vec, vec],
        out_shape=[jax.ShapeDtypeStruct((t, HEAD_DIM), MM), jax.ShapeDtypeStruct((1, HEAD_DIM), F32),
                   jax.ShapeDtypeStruct((1, HEAD_DIM), F32)],
        compiler_params=_cparams(("arbitrary",)),
    )(proj, a_log, dt_bias, dgates)


def _dn_intra(q, k, v, gcol, grow, bcol):
    r, c = _iota2(CHUNK, CHUNK)
    causal, strict = r >= c, r > c
    decay = jnp.where(causal, jnp.exp(jnp.where(causal, gcol - grow, 0.0)), 0.0)
    kb = k * bcol
    a = jnp.where(strict, mmul(kb, k, 1, 1, False) * decay, 0.0)
    tm = tri_inv(a)
    u = mmul(tm, v * bcol, 1, 0, False)
    w = mmul(tm, kb * jnp.exp(gcol), 1, 0, False)
    qk = jnp.where(causal, mmul(q, k, 1, 1, False) * decay, 0.0)
    rr = lax.broadcasted_iota(jnp.int32, (CHUNK, 1), 0)
    g_last = jnp.sum(jnp.where(rr == CHUNK - 1, gcol, 0.0), axis=0, keepdims=True)
    return u, w, q * jnp.exp(gcol), k * jnp.exp(g_last - gcol), qk, jnp.exp(g_last)


def _dn_scan(u, w, qg, kd, qk, eg, state):
    v_new = u - mmul(w, state, 1, 0, False)
    out = mmul(qg, state, 1, 0, False) + mmul(qk, v_new, 1, 0, False)
    return out, state * eg + mmul(kd, v_new, 0, 0, False)


DN_HEADS_PER_STEP = 1
DN_GROUP = 8
DN_PARTS = ((CHUNK, HEAD_DIM),) * 4 + ((CHUNK, CHUNK), (1, 1))


def _dn_scratch(hb, nc):
    return [pltpu.VMEM((hb, nc) + shape, F32) for shape in DN_PARTS]


def _dn_part_specs(hb, nc):
    return [pl.BlockSpec((hb, nc) + shape, lambda h: (h, 0, 0, 0)) for shape in DN_PARTS]


def _dn_group(nc):
    return min(DN_GROUP, nc)


def _dn_group_args(refs, j, g, grp):
    q_ref, k_ref, v_ref, gc_ref, gr_ref, bc_ref = refs
    rows = pl.ds(pl.multiple_of(g * (grp * CHUNK), grp * CHUNK), grp * CHUNK)
    cs = pl.ds(g * grp, grp)
    split = lambda ref: ref[rows, _lanes(j)].reshape(grp, CHUNK, HEAD_DIM)
    return split(q_ref), split(k_ref), split(v_ref), gc_ref[j, cs], gr_ref[j, cs], bc_ref[j, cs]


def _dn_intra_all(refs, parts, hb, nc):
    grp = _dn_group(nc)

    def group(g, carry):
        cs = pl.ds(g * grp, grp)
        for j in range(hb):
            for part, val in zip(parts, jax.vmap(_dn_intra)(*_dn_group_args(refs, j, g, grp))):
                part[j, cs] = val
        return carry

    lax.fori_loop(0, nc // grp, group, 0)


def _dn_specs(t):
    nc, hb = t // CHUNK, DN_HEADS_PER_STEP
    head = lambda which: pl.BlockSpec((None, t, hb * HEAD_DIM), lambda h: (which, 0, h))
    flat = pl.BlockSpec((t, hb * HEAD_DIM), lambda h: (0, h))
    col = pl.BlockSpec((hb, nc, CHUNK, 1), lambda h: (h, 0, 0, 0))
    row = pl.BlockSpec((hb, nc, 1, CHUNK), lambda h: (h, 0, 0, 0))
    st = pl.BlockSpec((hb, nc, HEAD_DIM, HEAD_DIM), lambda h: (h, 0, 0, 0))
    return nc, hb, head, flat, col, row, st


def dn_core_fwd(qkv, gcol, grow, bcol):
    t = qkv.shape[1]
    nc, hb, head, flat, col, row, st = _dn_specs(t)

    def body(q_ref, k_ref, v_ref, gc_ref, gr_ref, bc_ref, o_ref, s_ref, *parts):
        _dn_intra_all((q_ref, k_ref, v_ref, gc_ref, gr_ref, bc_ref), parts, hb, nc)

        def step(c, states):
            rows = pl.ds(pl.multiple_of(c * CHUNK, CHUNK), CHUNK)
            new_states = []
            for j in range(hb):
                s_ref[j, c] = states[j]
                out, new_state = _dn_scan(*[part[j, c] for part in parts], states[j])
                o_ref[rows, _lanes(j)] = out
                new_states.append(new_state)
            return tuple(new_states)

        lax.fori_loop(0, nc, step, tuple(jnp.zeros((HEAD_DIM, HEAD_DIM), F32) for _ in range(hb)))

    outs = pl.pallas_call(
        body, name="dn_core_fwd", grid=(N_HEADS // hb,),
        in_specs=[head(0), head(1), head(2), col, row, col], out_specs=[flat, st] + _dn_part_specs(hb, nc),
        out_shape=[jax.ShapeDtypeStruct((t, D_MODEL), F32), jax.ShapeDtypeStruct((N_HEADS, nc, HEAD_DIM, HEAD_DIM), F32)]
        + [jax.ShapeDtypeStruct((N_HEADS, nc) + shape, F32) for shape in DN_PARTS],
        compiler_params=_cparams(("parallel",)),
    )(qkv, qkv, qkv, gcol, grow, bcol)
    return outs[0], outs[1], tuple(outs[2:])


def dn_core_bwd(qkv, gcol, grow, bcol, states, parts, do):
    t = qkv.shape[1]
    nc, hb, head, flat, col, row, st = _dn_specs(t)
    n_parts = len(DN_PARTS)

    def body(q_ref, k_ref, v_ref, gc_ref, gr_ref, bc_ref, s_ref, do_ref, *rest):
        parts, (dqkv_ref, dgc_ref, dgr_ref, dbc_ref), dparts = rest[:n_parts], rest[n_parts:n_parts + 4], rest[n_parts + 4:]
        refs = (q_ref, k_ref, v_ref, gc_ref, gr_ref, bc_ref)

        def step(i, dstates):
            c = nc - 1 - i
            rows = pl.ds(pl.multiple_of(c * CHUNK, CHUNK), CHUNK)
            dstates_in = []
            for j in range(hb):
                _, vjp = jax.vjp(_dn_scan, *[part[j, c] for part in parts], s_ref[j, c])
                *dvals, dstate_in = vjp((do_ref[rows, _lanes(j)], dstates[j]))
                for dpart, dval in zip(dparts, dvals):
                    dpart[j, c] = dval
                dstates_in.append(dstate_in)
            return tuple(dstates_in)

        lax.fori_loop(0, nc, step, tuple(jnp.zeros((HEAD_DIM, HEAD_DIM), F32) for _ in range(hb)))

        grp = _dn_group(nc)

        def group(g, carry):
            rows = pl.ds(pl.multiple_of(g * (grp * CHUNK), grp * CHUNK), grp * CHUNK)
            cs = pl.ds(g * grp, grp)
            for j in range(hb):
                _, vjp = jax.vjp(jax.vmap(_dn_intra), *_dn_group_args(refs, j, g, grp))
                dq, dk, dv, dgc, dgr, dbc = vjp(tuple(dpart[j, cs] for dpart in dparts))
                for which, val in enumerate((dq, dk, dv)):
                    dqkv_ref[which, rows, _lanes(j)] = val.reshape(grp * CHUNK, HEAD_DIM)
                dgc_ref[j, cs] = dgc
                dgr_ref[j, cs] = dgr
                dbc_ref[j, cs] = dbc
            return carry

        lax.fori_loop(0, nc // grp, group, 0)

    return pl.pallas_call(
        body, name="dn_core_bwd", grid=(N_HEADS // hb,), scratch_shapes=_dn_scratch(hb, nc),
        in_specs=[head(0), head(1), head(2), col, row, col, st, flat] + _dn_part_specs(hb, nc),
        out_specs=[pl.BlockSpec((3, t, hb * HEAD_DIM), lambda h: (0, 0, h)), col, row, col],
        out_shape=[jax.ShapeDtypeStruct((3, t, D_MODEL), F32)] + [
            jax.ShapeDtypeStruct((N_HEADS, nc, CHUNK, 1), F32), jax.ShapeDtypeStruct((N_HEADS, nc, 1, CHUNK), F32),
            jax.ShapeDtypeStruct((N_HEADS, nc, CHUNK, 1), F32)],
        compiler_params=_cparams(("parallel",)),
    )(qkv, qkv, qkv, gcol, grow, bcol, states, do, *parts)


def gates_to_heads(gates):
    t = gates.shape[0]
    nc = t // CHUNK
    g = gates[:, :N_HEADS].T.reshape(N_HEADS, nc, CHUNK)
    b = gates[:, N_HEADS:2 * N_HEADS].T.reshape(N_HEADS, nc, CHUNK)
    return g[..., None], g[:, :, None, :], b[..., None]


def heads_to_gates(dgcol, dgrow, dbcol):
    nh, nc = dgcol.shape[:2]
    dg = (dgcol[..., 0] + dgrow[:, :, 0, :]).reshape(nh, nc * CHUNK).T
    db = dbcol[..., 0].reshape(nh, nc * CHUNK).T
    return jnp.concatenate([dg, db, jnp.zeros((nc * CHUNK, HEAD_DIM - 2 * nh), F32)], axis=1)


def _dn_out(o, z, w):
    return _rms(o, w) * _silu(z)


def _gate_specs():
    o_spec = pl.BlockSpec((ROWS, D_MODEL), lambda i: (i, 0))
    z_spec = pl.BlockSpec((ROWS, D_MODEL), lambda i: (i, 3))
    w_spec = pl.BlockSpec((1, HEAD_DIM), lambda i: (0, 0))
    return o_spec, z_spec, w_spec


def dn_out_fwd(o, proj, w):
    t = o.shape[0]
    o_spec, z_spec, w_spec = _gate_specs()

    def body(o_ref, z_ref, w_ref, y_ref):
        for h in range(N_HEADS):
            y_ref[:, _lanes(h)] = _dn_out(o_ref[:, _lanes(h)], z_ref[:, _lanes(h)], w_ref[...]).astype(y_ref.dtype)

    return pl.pallas_call(
        body, name="dn_out_fwd", grid=(t // ROWS,), in_specs=[o_spec, z_spec, w_spec], out_specs=o_spec,
        out_shape=jax.ShapeDtypeStruct((t, D_MODEL), MM), compiler_params=_cparams(("parallel",)),
    )(o, proj, w)


def dn_out_bwd(o, proj, w, dcat):
    t = o.shape[0]
    o_spec, z_spec, w_spec = _gate_specs()

    def body(o_ref, z_ref, w_ref, g_ref, do_ref, dz_ref, dw_ref):
        dw_sum = jnp.zeros((1, HEAD_DIM), F32)
        for h in range(N_HEADS):
            _, vjp = jax.vjp(_dn_out, o_ref[:, _lanes(h)], z_ref[:, _lanes(h)], w_ref[...])
            do, dz, dw = vjp(g_ref[:, _lanes(h)].astype(F32))
            do_ref[:, _lanes(h)] = do
            dz_ref[:, _lanes(h)] = dz.astype(dz_ref.dtype)
            dw_sum = dw_sum + dw
        _acc(dw_ref, dw_sum, pl.program_id(0) == 0)

    return pl.pallas_call(
        body, name="dn_out_bwd", grid=(t // ROWS,), in_specs=[o_spec, z_spec, w_spec, o_spec],
        out_specs=[o_spec, o_spec, w_spec],
        out_shape=[jax.ShapeDtypeStruct((t, D_MODEL), F32), jax.ShapeDtypeStruct((t, D_MODEL), MM),
                   jax.ShapeDtypeStruct((1, HEAD_DIM), F32)],
        compiler_params=_cparams(("arbitrary",)),
    )(o, proj, w, dcat)


def _fox_norm(x, w, scale):
    return _rms(x, w) * scale


def _fox_prep_specs():
    x_spec = pl.BlockSpec((ROWS, 2 * D_MODEL), lambda i: (i, 0))
    w_spec = pl.BlockSpec((2, 1, HEAD_DIM), lambda i: (0, 0, 0))
    y_spec = pl.BlockSpec((2, ROWS, D_MODEL), lambda i: (0, i, 0))
    return x_spec, w_spec, y_spec


def fox_prep_fwd(proj, wqk):
    t = proj.shape[0]
    x_spec, w_spec, y_spec = _fox_prep_specs()

    def body(x_ref, w_ref, y_ref):
        for j in range(2 * N_HEADS):
            which, scale = j // N_HEADS, (QSCALE if j < N_HEADS else 1.0)
            y_ref[which, :, _lanes(j % N_HEADS)] = _fox_norm(x_ref[:, _lanes(j)], w_ref[which], scale).astype(y_ref.dtype)

    return pl.pallas_call(
        body, name="fox_prep_fwd", grid=(t // ROWS,), in_specs=[x_spec, w_spec], out_specs=y_spec,
        out_shape=jax.ShapeDtypeStruct((2, t, D_MODEL), MM), compiler_params=_cparams(("parallel",)),
    )(proj, wqk)


def fox_prep_bwd(proj, wqk, dq, dk):
    t = proj.shape[0]
    x_spec, w_spec, _ = _fox_prep_specs()
    g_spec = pl.BlockSpec((ROWS, D_MODEL), lambda i: (i, 0))

    def body(x_ref, w_ref, dq_ref, dk_ref, dx_ref, dw_ref):
        dws = [jnp.zeros((1, HEAD_DIM), F32), jnp.zeros((1, HEAD_DIM), F32)]
        for j in range(2 * N_HEADS):
            which, scale = j // N_HEADS, (QSCALE if j < N_HEADS else 1.0)
            g_ref = dq_ref if which == 0 else dk_ref
            _, vjp = jax.vjp(lambda x, w: _fox_norm(x, w, scale), x_ref[:, _lanes(j)], w_ref[which])
            dx, dw = vjp(g_ref[:, _lanes(j % N_HEADS)])
            dx_ref[:, _lanes(j)] = dx.astype(dx_ref.dtype)
            dws[which] = dws[which] + dw
        first = pl.program_id(0) == 0
        _acc(dw_ref.at[0], dws[0], first)
        _acc(dw_ref.at[1], dws[1], first)

    return pl.pallas_call(
        body, name="fox_prep_bwd", grid=(t // ROWS,), in_specs=[x_spec, w_spec, g_spec, g_spec],
        out_specs=[x_spec, w_spec],
        out_shape=[jax.ShapeDtypeStruct((t, 2 * D_MODEL), MM), jax.ShapeDtypeStruct((2, 1, HEAD_DIM), F32)],
        compiler_params=_cparams(("arbitrary",)),
    )(proj, wqk, dq, dk)


def _row_pick(x, i):
    r = lax.broadcasted_iota(jnp.int32, x.shape, 0)
    return jnp.sum(jnp.where(r == i, x, 0.0), axis=0, keepdims=True)


def fox_gates_fwd(proj, f_bias):
    t = proj.shape[0]
    blk = HEAD_DIM

    def body(x_ref, b_ref, o_ref):
        lane = lax.broadcasted_iota(jnp.int32, (blk, HEAD_DIM), 1)
        tri = _tri_ones(blk, False)

        def step(c, carry):
            rows = pl.ds(pl.multiple_of(c * blk, blk), blk)
            lf = jnp.where(lane < N_HEADS, -_softplus(-(x_ref[rows, :] + b_ref[...])), 0.0)
            cum = _dot(tri, lf, 1, 0, True) + carry
            o_ref[rows, :] = cum
            return _row_pick(cum, blk - 1)

        lax.fori_loop(0, t // blk, step, jnp.zeros((1, HEAD_DIM), F32))

    vec = pl.BlockSpec((1, HEAD_DIM), lambda i: (0, 0))
    return pl.pallas_call(
        body, name="fox_gates_fwd", grid=(1,),
        in_specs=[pl.BlockSpec((t, HEAD_DIM), lambda i: (0, TAIL_BLK)), vec],
        out_specs=pl.BlockSpec((t, HEAD_DIM), lambda i: (0, 0)),
        out_shape=jax.ShapeDtypeStruct((t, HEAD_DIM), F32), compiler_params=_cparams(("arbitrary",)),
    )(proj, f_bias)


def fox_gates_bwd(proj, f_bias, dfcum):
    t = proj.shape[0]
    blk = HEAD_DIM
    nb = t // blk

    def body(x_ref, b_ref, g_ref, dx_ref, db_ref):
        lane = lax.broadcasted_iota(jnp.int32, (blk, HEAD_DIM), 1)
        tri = _tri_ones(blk, True)
        db_ref[...] = jnp.zeros_like(db_ref)

        def step(i, carry):
            c = nb - 1 - i
            rows = pl.ds(pl.multiple_of(c * blk, blk), blk)
            g = jnp.where(lane < N_HEADS, g_ref[rows, :], 0.0)
            dlf = _dot(tri, g, 1, 0, True) + carry
            dx = jnp.where(lane < N_HEADS, dlf * _sigmoid(-(x_ref[rows, :] + b_ref[...])), 0.0)
            dx_ref[rows, :] = dx.astype(dx_ref.dtype)
            db_ref[...] += jnp.sum(dx, axis=0, keepdims=True)
            return carry + jnp.sum(g, axis=0, keepdims=True)

        lax.fori_loop(0, nb, step, jnp.zeros((1, HEAD_DIM), F32))

    vec = pl.BlockSpec((1, HEAD_DIM), lambda i: (0, 0))
    full = pl.BlockSpec((t, HEAD_DIM), lambda i: (0, 0))
    return pl.pallas_call(
        body, name="fox_gates_bwd", grid=(1,),
        in_specs=[pl.BlockSpec((t, HEAD_DIM), lambda i: (0, TAIL_BLK)), vec, full], out_specs=[full, vec],
        out_shape=[jax.ShapeDtypeStruct((t, HEAD_DIM), MM), jax.ShapeDtypeStruct((1, HEAD_DIM), F32)],
        compiler_params=_cparams(("arbitrary",)),
    )(proj, f_bias, dfcum)


def fcum_to_heads(fcum):
    f = fcum[:, :N_HEADS].T
    return f[:, :, None], f[:, None, :]


def heads_to_fcum(dfcol, dfrow):
    d = (dfcol[:, :, 0] + dfrow[:, 0, :]).T
    return jnp.concatenate([d, jnp.zeros((d.shape[0], HEAD_DIM - N_HEADS), F32)], axis=1)


def _fox_tq(t):
    return min(t, 256)


def _fox_specs(t):
    tq = _fox_tq(t)
    q_spec = pl.BlockSpec((None, tq, HEAD_DIM), lambda h, i: (0, i, h))
    k_spec = pl.BlockSpec((None, t, HEAD_DIM), lambda h, i: (1, 0, h))
    v_spec = pl.BlockSpec((t, HEAD_DIM), lambda h, i: (0, 2 * N_HEADS + h))
    gate_spec = pl.BlockSpec((tq, HEAD_DIM), lambda h, i: (i, 3 * N_HEADS + h))
    col_spec = pl.BlockSpec((None, tq, 1), lambda h, i: (h, i, 0))
    row_spec = pl.BlockSpec((None, 1, t), lambda h, i: (h, 0, 0))
    blk_spec = pl.BlockSpec((tq, HEAD_DIM), lambda h, i: (i, h))
    head_spec = pl.BlockSpec((t, HEAD_DIM), lambda h, i: (0, h))
    return tq, q_spec, k_spec, v_spec, gate_spec, col_spec, row_spec, blk_spec, head_spec


def _fox_segments(i, tq):
    return ([(0, i * tq, False)] if i else []) + [(i * tq, (i + 1) * tq, True)]


def _fox_scores(q_ref, k_ref, fc_ref, fr_ref, lo, hi, causal):
    s = _dot(q_ref[...], k_ref[lo:hi, :], 1, 1, False) + (fc_ref[...] - fr_ref[:, lo:hi])
    if not causal:
        return s, None
    r, c = _iota2(hi - lo, hi - lo)
    return s, c <= r


def fox_attn_fwd(qk, proj, fcol, frow):
    t = proj.shape[0]
    tq, q_spec, k_spec, v_spec, gate_spec, col_spec, row_spec, blk_spec, _ = _fox_specs(t)

    def body(q_ref, k_ref, v_ref, gate_ref, fc_ref, fr_ref, mix_ref, o_ref, lse_ref):
        def block(i):
            segs = _fox_segments(i, tq)
            scores = [_fox_scores(q_ref, k_ref, fc_ref, fr_ref, *seg) for seg in segs]
            scores = [(s if mask is None else jnp.where(mask, s, -1e30), mask) for s, mask in scores]
            m = functools.reduce(jnp.maximum, [jnp.max(s, axis=-1, keepdims=True) for s, _ in scores])
            l, o = 0.0, 0.0
            for (lo, hi, _), (s, mask) in zip(segs, scores):
                p = jnp.exp(s - m)
                p = p if mask is None else jnp.where(mask, p, 0.0)
                l = l + jnp.sum(p, axis=-1, keepdims=True)
                o = o + _dot(p, v_ref[lo:hi, :], 1, 0, False)
            o = o / l
            o_ref[...] = o
            mix_ref[...] = (o * _sigmoid(gate_ref[...])).astype(mix_ref.dtype)
            lse_ref[...] = m + jnp.log(l)

        for i in range(t // tq):
            pl.when(pl.program_id(1) == i)(functools.partial(block, i))

    return pl.pallas_call(
        body, name="fox_attn_fwd", grid=(N_HEADS, t // tq),
        in_specs=[q_spec, k_spec, v_spec, gate_spec, col_spec, row_spec], out_specs=[blk_spec, blk_spec, col_spec],
        out_shape=[jax.ShapeDtypeStruct((t, D_MODEL), MM), jax.ShapeDtypeStruct((t, D_MODEL), F32),
                   jax.ShapeDtypeStruct((N_HEADS, t, 1), F32)],
        compiler_params=_cparams(("parallel", "parallel")),
    )(qk, qk, proj, proj, fcol, frow)


def fox_attn_bwd(qk, proj, fcol, frow, o, lse, dcat):
    t = proj.shape[0]
    tq, q_spec, k_spec, v_spec, gate_spec, col_spec, row_spec, blk_spec, head_spec = _fox_specs(t)

    def body(q_ref, k_ref, v_ref, gate_ref, fc_ref, fr_ref, o_ref, lse_ref, g_ref,
             dq_ref, dk_ref, dv_ref, dgate_ref, dfc_ref, dfr_ref):
        @pl.when(pl.program_id(1) == 0)
        def _():
            dk_ref[...] = jnp.zeros_like(dk_ref)
            dv_ref[...] = jnp.zeros_like(dv_ref)
            dfr_ref[...] = jnp.zeros_like(dfr_ref)

        def block(i):
            sg = _sigmoid(gate_ref[...])
            g = g_ref[...].astype(F32)
            o_pre = o_ref[...]
            do = g * sg
            dgate_ref[...] = (g * o_pre * sg * (1.0 - sg)).astype(dgate_ref.dtype)
            delta = jnp.sum(do * o_pre, axis=-1, keepdims=True)
            dq, dfc = 0.0, 0.0
            for lo, hi, causal in _fox_segments(i, tq):
                s, mask = _fox_scores(q_ref, k_ref, fc_ref, fr_ref, lo, hi, causal)
                if causal:
                    p = jnp.where(mask, jnp.exp(jnp.where(mask, s, 0.0) - lse_ref[...]), 0.0)
                else:
                    p = jnp.exp(s - lse_ref[...])
                ds = p * (_dot(do, v_ref[lo:hi, :], 1, 1, False) - delta)
                dq = dq + _dot(ds, k_ref[lo:hi, :], 1, 0, False)
                dk_ref[lo:hi, :] += _dot(ds, q_ref[...], 0, 0, False)
                dv_ref[lo:hi, :] += _dot(p, do, 0, 0, False)
                dfc = dfc + jnp.sum(ds, axis=-1, keepdims=True)
                dfr_ref[:, lo:hi] += -jnp.sum(ds, axis=0, keepdims=True)
            dq_ref[...] = dq
            dfc_ref[...] = dfc

        for i in range(t // tq):
            pl.when(pl.program_id(1) == i)(functools.partial(block, i))

    f32 = lambda *s: jax.ShapeDtypeStruct(s, F32)
    return pl.pallas_call(
        body, name="fox_attn_bwd", grid=(N_HEADS, t // tq),
        in_specs=[q_spec, k_spec, v_spec, gate_spec, col_spec, row_spec, blk_spec, col_spec, blk_spec],
        out_specs=[blk_spec, head_spec, head_spec, blk_spec, col_spec, row_spec],
        out_shape=[f32(t, D_MODEL), f32(t, D_MODEL), f32(t, D_MODEL), jax.ShapeDtypeStruct((t, D_MODEL), MM),
                   f32(N_HEADS, t, 1), f32(N_HEADS, 1, t)],
        compiler_params=_cparams(("parallel", "arbitrary")),
    )(qk, qk, proj, proj, fcol, frow, o, lse, dcat)


def adamw(w, g, m, v, *, name):
    r, c = w.shape
    rb = ROWS if r % ROWS == 0 else r

    def body(w_ref, g_ref, m_ref, v_ref, d_ref, nm_ref, nv_ref):
        g_ = g_ref[...]
        m_ = ADAM_B1 * m_ref[...] + (1.0 - ADAM_B1) * g_
        v_ = ADAM_B2 * v_ref[...] + (1.0 - ADAM_B2) * jnp.square(g_)
        m_hat = m_ / (1.0 - ADAM_B1 ** ADAM_STEP)
        v_hat = v_ / (1.0 - ADAM_B2 ** ADAM_STEP)
        d_ref[...] = -ADAM_LR * (m_hat / (jnp.sqrt(v_hat) + ADAM_EPS) + ADAM_WD * w_ref[...])
        nm_ref[...] = m_
        nv_ref[...] = v_

    blk = pl.BlockSpec((rb, c), lambda i: (i, 0))
    shp = jax.ShapeDtypeStruct((r, c), F32)
    return pl.pallas_call(body, name=name, grid=(r // rb,), in_specs=[blk] * 4, out_specs=[blk] * 3,
                          out_shape=[shp] * 3, compiler_params=_cparams(("parallel",)))(w, g, m, v)


def _place():
    x, y, c = lax.axis_index("x"), lax.axis_index("y"), lax.axis_index("c")
    return x, y, c, [(1 - x, y), (x, 1 - y), (1 - x, 1 - y)]


ANY = pl.BlockSpec(memory_space=pl.ANY)


def all_reduce_small(v):
    r, w = v.shape

    def body(v_ref, o_ref, buf, send_sems, recv_sems):
        x, y, c, _ = _place()
        me = 4 * x + 2 * y + c
        flip = lambda a, bit: 1 - a if bit else a
        cps = []
        for k in range(1, N_DEV):
            peer = (flip(x, k & 4), flip(y, k & 2), flip(c, k & 1))
            cp = pltpu.make_async_remote_copy(src_ref=v_ref, dst_ref=buf.at[me], send_sem=send_sems.at[k - 1],
                                              recv_sem=recv_sems.at[k - 1], device_id=peer, device_id_type=MESH)
            cp.start()
            cps.append((cp, 4 * peer[0] + 2 * peer[1] + peer[2]))
        buf[me] = v_ref[...]
        for k, (cp, peer_id) in enumerate(cps):
            pltpu.make_async_remote_copy(src_ref=v_ref, dst_ref=buf.at[peer_id], send_sem=send_sems.at[k],
                                         recv_sem=recv_sems.at[k], device_id=(x, y, c), device_id_type=MESH).wait_recv()
        for cp, _ in cps:
            cp.wait_send()
        acc = buf[0]
        for d in range(1, N_DEV):
            acc = acc + buf[d]
        o_ref[...] = acc

    vm = pl.BlockSpec(memory_space=pltpu.VMEM)
    return pl.pallas_call(
        body, name="all_reduce_small", in_specs=[vm], out_specs=vm, out_shape=jax.ShapeDtypeStruct((r, w), F32),
        scratch_shapes=[pltpu.VMEM((N_DEV, r, w), F32), pltpu.SemaphoreType.DMA((N_DEV - 1,)),
                        pltpu.SemaphoreType.DMA((N_DEV - 1,))],
    )(v)


def _after(x, token):
    return lax.optimization_barrier((x, token))[0]


def _vec8(v):
    return jnp.zeros((1, HEAD_DIM), F32).at[0, :N_HEADS].set(v.reshape(N_HEADS))


def _layer_fwd(i, x_in, wt, sm, mem_k, mem_v, late=None):
    tag = f"l{i}_"
    h = rms_fwd(x_in, sm["norm1_w"][i][None], name=tag + "rms1")
    w_in = wt["dn_w_in"] if i == 0 else wt["fox_w_in"]
    proj = matmul(h, w_in, name=tag + "proj", tm=256, tk=1024)
    sv = dict(x_in=x_in, h=h, proj=proj)
    if i == 0:
        qkv = dn_prep_fwd(proj, wt["conv_w"])
        gates = dn_gates_fwd(proj, _vec8(sm["dn_a_log"]), _vec8(sm["dn_dt_bias"]))
        gcol, grow, bcol = gates_to_heads(gates)
        o, states, parts = dn_core_fwd(qkv, gcol, grow, bcol)
        mix = dn_out_fwd(o, proj, sm["dn_o_norm_w"])
        sv.update(qkv=qkv, gcol=gcol, grow=grow, bcol=bcol, states=states, parts=parts, o=o)
    else:
        wqk = jnp.stack([sm["fox_q_norm_w"], sm["fox_k_norm_w"]])
        qk = fox_prep_fwd(proj, wqk)
        fcum = fox_gates_fwd(proj, _vec8(sm["fox_f_bias"]))
        fcol, frow = fcum_to_heads(fcum)
        mix, o, lse = fox_attn_fwd(qk, proj, fcol, frow)
        sv.update(wqk=wqk, qk=qk, fcol=fcol, frow=frow, o=o, lse=lse)
    mem_out = memattn_fwd(proj, sm["memq_norm_w"][i][None], mem_k, mem_v, name=tag + "memattn_fwd")
    cat = jnp.concatenate([mix, mem_out], axis=1)
    if late is not None:
        wt.update(late(cat))
    x_mid = matmul(cat, wt["w_out"], b_layer=i, res=x_in, name=tag + "out_proj")
    h2 = rms_fwd(x_mid, sm["norm2_w"][i][None], name=tag + "rms2")
    ff, act = matmul(h2, wt["w_mlp1"], b_layer=i, b_slots=True, also_sqrelu=True, out_dtype=MM, name=tag + "mlp1")
    x_out = matmul(act, wt["w_mlp2"], b_layer=i, res=x_mid, name=tag + "mlp2")
    sv.update(cat=cat, x_mid=x_mid, h2=h2, ff=ff, act=act)
    return x_out, sv


def _layer_bwd(i, dx_out, sv, wt, sm, mem_k, mem_v, on_mlp=None, on_core=None):
    tag = f"l{i}_"
    big, small = {}, {}
    dff = matmul(dx_out, wt["w_mlp2"], b_layer=i, tb=True, times_dsqrelu=sv["ff"], out_dtype=MM, name=tag + "d_ff")
    big["w_mlp2"] = matmul(sv["act"], dx_out, ta=True, name=tag + "d_w_mlp2", tk=2048)
    dh2 = matmul(dff, wt["w_mlp1"], b_layer=i, tb=True, b_slots=True, name=tag + "d_h2")
    big["w_mlp1"] = matmul(sv["h2"], dff, ta=True, name=tag + "d_w_mlp1", tm=512, tn=D_FF, tk=512)
    dx_mid, small["norm2_w"] = rms_bwd(sv["x_mid"], sm["norm2_w"][i][None], dh2, dx_out, name=tag + "rms2_bwd")
    dcat = matmul(dx_mid, wt["w_out"], b_layer=i, tb=True, name=tag + "d_cat")
    big["w_out"] = matmul(sv["cat"], dx_mid, ta=True, name=tag + "d_w_out", tk=2048)
    proj = sv["proj"]
    memq_norm_w = sm["memq_norm_w"][i][None]
    if on_mlp is not None:
        memq_norm_w = _after(memq_norm_w, on_mlp(big["w_mlp2"], big["w_mlp1"], big["w_out"]))
    dqm, small["memq_norm_w"], dmk, dmv = memattn_bwd(proj, memq_norm_w, mem_k, mem_v, dcat, name=tag + "memattn_bwd")
    t = proj.shape[0]
    pad = jnp.zeros((t, PROJ_W - TAIL - HEAD_DIM), MM)
    if i == 0:
        do, dz, small["dn_o_norm_w"] = dn_out_bwd(sv["o"], proj, sm["dn_o_norm_w"], dcat)
        bcol = sv["bcol"] if on_core is None else _after(sv["bcol"], on_core(do))
        dqkv, dgc, dgr, dbc = dn_core_bwd(sv["qkv"], sv["gcol"], sv["grow"], bcol, sv["states"], sv["parts"], do)
        dtail, dal, ddt = dn_gates_bwd(proj, _vec8(sm["dn_a_log"]), _vec8(sm["dn_dt_bias"]), heads_to_gates(dgc, dgr, dbc))
        dmain, dconv = dn_prep_bwd(proj, wt["conv_w"], dqkv)
        small["dn_a_log"], small["dn_dt_bias"] = dal[:, :N_HEADS], ddt[:, :N_HEADS]
        big["conv_w"] = dconv[:4]
        dproj = jnp.concatenate([dmain, dz, dqm, dtail, pad], axis=1)
    else:
        dq, dk, dv, dgate, dfc, dfr = fox_attn_bwd(sv["qk"], proj, sv["fcol"], sv["frow"], sv["o"], sv["lse"], dcat)
        dtail, dfb = fox_gates_bwd(proj, _vec8(sm["fox_f_bias"]), heads_to_fcum(dfc, dfr))
        dqk, dwqk = fox_prep_bwd(proj, sv["wqk"], dq, dk)
        small["fox_f_bias"] = dfb[:, :N_HEADS]
        small["fox_q_norm_w"], small["fox_k_norm_w"] = dwqk[0], dwqk[1]
        dproj = jnp.concatenate([dqk, dv.astype(MM), dgate, dqm, dtail, pad], axis=1)
    w_in = wt["dn_w_in"] if i == 0 else wt["fox_w_in"]
    dh = matmul(dproj, w_in, tb=True, name=tag + "d_h", tm=512)
    big["w_in"] = matmul(sv["h"], dproj, ta=True, name=tag + "d_w_in", tm=256)
    dx_in, small["norm1_w"] = rms_bwd(sv["x_in"], sm["norm1_w"][i][None], dh, dx_mid, name=tag + "rms1_bwd")
    return dx_in, big, small, (dmk, dmv)


def local_step(x, mem, target, wt, sm, late=None, on_layer1=None, on_mlp0=None, on_core0=None):
    wt = dict(wt)
    mem_k, mem_v = mem_fwd(mem, sm["mem_norm_w"][None], wt["w_mem_kv"], sm["mem_k_norm_w"][None])
    x0, sv0 = _layer_fwd(0, x, wt, sm, mem_k, mem_v, late)
    x1, sv1 = _layer_fwd(1, x0, wt, sm, mem_k, mem_v)
    dy, loss = loss_fwd(x1, target, name="loss")
    dx1, big1, small1, dm1 = _layer_bwd(1, dy, sv1, wt, sm, mem_k, mem_v)
    if on_layer1 is not None:
        dx1 = _after(dx1, on_layer1(big1))
    dx0, big0, small0, dm0 = _layer_bwd(0, dx1, sv0, wt, sm, mem_k, mem_v, on_mlp0, on_core0)
    dwn, dwkv, dwkn = mem_bwd(mem, sm["mem_norm_w"][None], wt["w_mem_kv"], sm["mem_k_norm_w"][None], *dm0, *dm1)
    small = dict(mem_norm_w=dwn[0], mem_k_norm_w=dwkn[0],
                 norm1_w=jnp.concatenate([small0["norm1_w"], small1["norm1_w"]]),
                 norm2_w=jnp.concatenate([small0["norm2_w"], small1["norm2_w"]]),
                 memq_norm_w=jnp.concatenate([small0["memq_norm_w"], small1["memq_norm_w"]]),
                 dn_a_log=small0["dn_a_log"], dn_dt_bias=small0["dn_dt_bias"], dn_o_norm_w=small0["dn_o_norm_w"],
                 fox_f_bias=small1["fox_f_bias"], fox_q_norm_w=small1["fox_q_norm_w"], fox_k_norm_w=small1["fox_k_norm_w"])
    big = dict(w_mem_kv=dwkv, dn_w_in=big0["w_in"], fox_w_in=big1["w_in"], conv_w=big0["conv_w"],
               w_out=[big0["w_out"], big1["w_out"]], w_mlp1=[big0["w_mlp1"], big1["w_mlp1"]],
               w_mlp2=[big0["w_mlp2"], big1["w_mlp2"]])
    return loss, dx0, big, small


def w_in_slots_to_kernel(slots, n_scalars):
    c = slots.shape[2]
    cut = 4096 - 3 * c
    pad = jnp.zeros((slots.shape[1], PROJ_W - TAIL - n_scalars), slots.dtype)
    return jnp.concatenate([slots[0], slots[1], slots[2], slots[3, :, :cut], slots[3, :, cut + n_scalars:],
                            slots[3, :, cut:cut + n_scalars], pad], axis=1)


def w_in_kernel_to_slots(w, n_scalars):
    c = (4096 + n_scalars + MEM_WIDTH) // N_CHIP
    last = jnp.concatenate([w[:, 3 * c:4096], w[:, TAIL:TAIL + n_scalars], w[:, 4096:TAIL]], axis=1)
    return jnp.stack([w[:, :c], w[:, c:2 * c], w[:, 2 * c:3 * c], last])


BIG_SPECS = dict(w_mem_kv=("rows", 1, 256, 1024), w_out=("rows", 2, 384, 1024), w_mlp2=("rows", 2, 1024, 1024),
                 w_mlp1=("cols", 2, 1024, 1024), dn_w_in=("rows", 1, 1024, 1156), fox_w_in=("rows", 1, 1024, 1154))
BIG_NAMES = tuple(BIG_SPECS)
EARLY_NAMES = ("w_mem_kv", "dn_w_in")
LATE_NAMES = ("w_out", "w_mlp2", "w_mlp1", "fox_w_in")
BIG_SPECS.update({f"{name}_{i}": (BIG_SPECS[name][0], 1) + BIG_SPECS[name][2:]
                  for name in ("w_out", "w_mlp2", "w_mlp1") for i in range(2)})
RS_LAYER1 = ("fox_w_in", "w_out_1", "w_mlp2_1", "w_mlp1_1")
RS_MLP0 = ("w_mlp2_0", "w_mlp1_0", "w_out_0")
RS_LAST = ("dn_w_in", "w_mem_kv")


def _full_shape(name, half=False):
    kind, a, b, c = BIG_SPECS[name]
    b = b // 2 if half else b
    return (a, N_CHIP, b, c) if kind == "rows" else (a, b, N_CHIP * c)


def _ds(start, size, align):
    return pl.ds(start if isinstance(start, int) else pl.multiple_of(start, align), size)


def _half_rows(name, h):
    b = BIG_SPECS[name][2]
    return _ds(h * (b // 2), b // 2, 16)


def _shard_idx(name, h):
    return (slice(None), _half_rows(name, h), slice(None))


def _full_idx(name, j=None, h=None):
    kind, _, _, c = BIG_SPECS[name]
    rows = slice(None) if h is None else _half_rows(name, h)
    if kind == "rows":
        return (slice(None), slice(None) if j is None else j, rows, slice(None))
    return (slice(None), rows, slice(None) if j is None else _ds(j * c, c, 128))


def _slots_shape(name):
    _, a, b, c = BIG_SPECS[name]
    return (a, N_CHIP, b, c)


def _slots_idx(name, j, h):
    return (slice(None), j, _half_rows(name, h), slice(None))


def _row_block(name):
    hs = BIG_SPECS[name][2] // 2
    return hs if hs <= ROWS else ROWS


def _remote(src, dst, send_sem, recv_sem, to):
    return pltpu.make_async_remote_copy(src_ref=src, dst_ref=dst, send_sem=send_sem, recv_sem=recv_sem, device_id=to,
                                        device_id_type=MESH)


HBM = pl.BlockSpec(memory_space=pltpu.HBM)
SEM = pl.BlockSpec(memory_space=pltpu.SEMAPHORE)
EFFECT = pltpu.CompilerParams(has_side_effects=pltpu.SideEffectType.DATAFLOW_SIDE_EFFECTING)


def _in_hbm(a):
    return pltpu.with_memory_space_constraint(a, pltpu.HBM)


def _chip_copies(names, ins, lands, send_sems, recv_sems):
    x, y, c, chips = _place()
    return [_remote(ins[a].at[_shard_idx(name, c)], lands[a].at[_slots_idx(name, 2 * x + y, c)], send_sems.at[3 * a + k],
                    recv_sems.at[3 * a + k], (chip[0], chip[1], c))
            for a, name in enumerate(names) for k, chip in enumerate(chips)]


def _copies_start(call_name, copies, sources, land_shapes, per_source=3, land_dtype=MM):
    n = len(sources)

    def body(*refs):
        ins, lands, send_sems, recv_sems, token = refs[:n], refs[n:2 * n], refs[2 * n], refs[2 * n + 1], refs[-1]
        for cp in copies(ins, lands, send_sems, recv_sems):
            cp.start()
        token[...] = jnp.zeros_like(token)

    ins = [_in_hbm(a) for a in sources]
    lands = [_in_hbm(lax.empty(shape, land_dtype)) for shape in land_shapes]
    sems = (pltpu.SemaphoreType.DMA((per_source * n,)), pltpu.SemaphoreType.DMA((per_source * n,)))
    outs = pl.pallas_call(
        body, name=call_name, in_specs=[HBM] * (2 * n),
        out_specs=(SEM, SEM) + (HBM,) * (2 * n) + (pl.BlockSpec(memory_space=pltpu.VMEM),),
        out_shape=sems + tuple(pltpu.HBM(a.shape, a.dtype) for a in ins + lands) + (jax.ShapeDtypeStruct((8, HEAD_DIM), F32),),
        input_output_aliases={a: 2 + a for a in range(2 * n)}, compiler_params=EFFECT,
    )(*ins, *lands)
    return outs[:-1], outs[-1]


def _copies_wait(call_name, copies, state, after):
    n = (len(state) - 2) // 2

    def body(*refs):
        send_sems, recv_sems, ins, lands = refs[0], refs[1], refs[2:2 + n], refs[2 + n:2 + 2 * n]
        for cp in copies(ins, lands, send_sems, recv_sems):
            cp.wait_send()
            cp.wait_recv()

    outs = pl.pallas_call(
        body, name=call_name, in_specs=[SEM, SEM] + [HBM] * (2 * n) + [ANY], out_specs=(HBM,) * (2 * n),
        out_shape=tuple(pltpu.HBM(a.shape, a.dtype) for a in state[2:]),
        input_output_aliases={2 + a: a for a in range(2 * n)}, compiler_params=EFFECT,
    )(*state, after)
    return outs[:n], outs[n:]


def all_gather_start(shards, names):
    return _copies_start("all_gather_start", functools.partial(_chip_copies, names), [shards[name] for name in names],
                         [_slots_shape(name) for name in names])


def all_gather_wait(state, names, after):
    ins, lands = _copies_wait("all_gather_wait", functools.partial(_chip_copies, names), state, after)
    return dict(zip(names, ins)), dict(zip(names, lands))


def _chip_sends(names, ins, lands, send_sems, recv_sems):
    x, y, c, chips = _place()
    return [_remote(ins[a].at[_full_idx(name, 2 * chip[0] + chip[1])], lands[a].at[k], send_sems.at[3 * a + k],
                    recv_sems.at[3 * a + k], (chip[0], chip[1], c))
            for a, name in enumerate(names) for k, chip in enumerate(chips)]


def _got_shape(name):
    _, a_, b_, c_ = BIG_SPECS[name]
    return (3, a_, b_ // 2, c_)


def rs_chip_start(pairs, names, tag):
    return _copies_start("rs_chip_start_" + tag, functools.partial(_chip_sends, names), [pairs[name] for name in names],
                         [_got_shape(name) for name in names])


def rs_chip_wait(state, names, tag, after):
    _, lands = _copies_wait("rs_chip_wait_" + tag, functools.partial(_chip_sends, names), state, after)
    return dict(zip(names, lands))


def all_gather_pass_on(lands, names):
    n = len(names)

    def body(*refs):
        outs, send_sems, recv_sems = refs[n:2 * n], refs[2 * n], refs[2 * n + 1]
        x, y, c, chips = _place()
        work = [(3 * a + k, a, name, 2 * chip[0] + chip[1]) for a, name in enumerate(names) for k, chip in enumerate(chips)]
        cps = []
        for s, a, name, slot in work:
            landed = outs[a].at[_slots_idx(name, slot, c)]
            cps.append(_remote(landed, landed, send_sems.at[s], recv_sems.at[s], (x, y, 1 - c)))
            cps[-1].start()
        for s, a, name, slot in work:
            passed = outs[a].at[_slots_idx(name, slot, 1 - c)]
            _remote(passed, passed, send_sems.at[s], recv_sems.at[s], (x, y, 1 - c)).wait_recv()
        for cp in cps:
            cp.wait_send()

    outs = pl.pallas_call(
        body, name="all_gather_pass_on", in_specs=[ANY] * n, out_specs=[ANY] * n,
        input_output_aliases={a: a for a in range(n)},
        out_shape=[jax.ShapeDtypeStruct(_slots_shape(name), MM) for name in names],
        scratch_shapes=[pltpu.SemaphoreType.DMA((3 * n,)), pltpu.SemaphoreType.DMA((3 * n,))],
    )(*[lands[name] for name in names])
    return dict(zip(names, outs))


def all_gather_big(shards, names):
    n = len(names)

    def body(*refs):
        ins, outs = refs[:n], refs[n:2 * n]
        send_sems, recv_sems, fsend_sems, frecv_sems = refs[2 * n:]
        x, y, c, chips = _place()
        me_chip, sibling = 2 * x + y, (x, y, 1 - c)
        work = [(3 * a + k, a, name, chip) for a, name in enumerate(names) for k, chip in enumerate(chips)]
        sends = []
        for s, a, name, chip in work:
            cp = _remote(ins[a].at[_shard_idx(name, c)], outs[a].at[_slots_idx(name, me_chip, c)], send_sems.at[s],
                         recv_sems.at[s], (chip[0], chip[1], c))
            cp.start()
            sends.append(cp)
        for s, a, name, chip in work:
            landed = outs[a].at[_slots_idx(name, 2 * chip[0] + chip[1], c)]
            _remote(landed, landed, send_sems.at[s], recv_sems.at[s], (chip[0], chip[1], c)).wait_recv()
            cp = _remote(landed, landed, fsend_sems.at[s], frecv_sems.at[s], sibling)
            cp.start()
            sends.append(cp)
        for s, a, name, chip in work:
            passed = outs[a].at[_slots_idx(name, 2 * chip[0] + chip[1], 1 - c)]
            _remote(passed, passed, fsend_sems.at[s], frecv_sems.at[s], sibling).wait_recv()
        for cp in sends:
            cp.wait_send()

    outs = pl.pallas_call(
        body, name="all_gather_big", in_specs=[ANY] * n, out_specs=[ANY] * n,
        out_shape=[jax.ShapeDtypeStruct(_slots_shape(name), MM) for name in names],
        scratch_shapes=[pltpu.SemaphoreType.DMA((3 * n,))] * 4,
    )(*[shards[name] for name in names])
    return dict(zip(names, outs))


def with_own_slot(name, full, shard, chip):
    return lax.dynamic_update_slice(full, shard[:, None], (0, chip, 0, 0))


def rs_pair_exchange_big(grads, names, tag):
    n = len(names)

    def body(*refs):
        ins, outs, send_sems, recv_sems = refs[:n], refs[n:2 * n], refs[2 * n], refs[2 * n + 1]
        x, y, c, _ = _place()
        cps = []
        for a, name in enumerate(names):
            cp = _remote(ins[a].at[_full_idx(name, None, 1 - c)], outs[a], send_sems.at[a], recv_sems.at[a], (x, y, 1 - c))
            cp.start()
            cps.append(cp)
        for cp in cps:
            cp.wait()

    outs = pl.pallas_call(
        body, name="rs_pair_exchange_" + tag, in_specs=[ANY] * n, out_specs=[ANY] * n,
        out_shape=[jax.ShapeDtypeStruct(_full_shape(name, half=True), F32) for name in names],
        scratch_shapes=[pltpu.SemaphoreType.DMA((n,)), pltpu.SemaphoreType.DMA((n,))],
    )(*[grads[name] for name in names])
    return dict(zip(names, outs))


def rs_pair_add_big(name, place, g, got):
    kind, a_, b_, c_ = BIG_SPECS[name]
    rb = _row_block(name)
    nb = (b_ // 2) // rb

    def body(place_ref, g_ref, got_ref, o_ref):
        o_ref[...] = (g_ref[...] + got_ref[...]).astype(o_ref.dtype)

    if kind == "rows":
        g_spec = pl.BlockSpec((None, None, rb, c_), lambda a, j, i, p: (a, j, p[0] * nb + i, 0))
        o_spec = pl.BlockSpec((None, None, rb, c_), lambda a, j, i, p: (a, j, i, 0))
    else:
        g_spec = pl.BlockSpec((None, rb, c_), lambda a, j, i, p: (a, p[0] * nb + i, j))
        o_spec = pl.BlockSpec((None, rb, c_), lambda a, j, i, p: (a, i, j))
    return pl.pallas_call(
        body, name="rs_pair_add_" + name,
        grid_spec=pltpu.PrefetchScalarGridSpec(num_scalar_prefetch=1, grid=(a_, N_CHIP, nb), in_specs=[g_spec, o_spec],
                                               out_specs=o_spec),
        out_shape=jax.ShapeDtypeStruct(_full_shape(name, half=True), MM),
        compiler_params=_cparams(("parallel", "parallel", "parallel")),
    )(place, g, got)


def rs_chip_add_big(name, place, g, got_pair, got_chips):
    kind, a_, b_, c_ = BIG_SPECS[name]
    rb = _row_block(name)
    nb = (b_ // 2) // rb

    def body(place_ref, g_ref, s_ref, r0_ref, r1_ref, r2_ref, o_ref):
        own = g_ref[...] + s_ref[...]
        o_ref[...] = ((own + r0_ref[...].astype(F32)) + r1_ref[...].astype(F32)) + r2_ref[...].astype(F32)

    if kind == "rows":
        g_spec = pl.BlockSpec((None, None, rb, c_), lambda a, i, p: (a, p[1], p[0] * nb + i, 0))
        s_spec = pl.BlockSpec((None, None, rb, c_), lambda a, i, p: (a, p[1], i, 0))
    else:
        g_spec = pl.BlockSpec((None, rb, c_), lambda a, i, p: (a, p[0] * nb + i, p[1]))
        s_spec = pl.BlockSpec((None, rb, c_), lambda a, i, p: (a, i, p[1]))
    r_spec = lambda k: pl.BlockSpec((None, None, rb, c_), lambda a, i, p: (k, a, i, 0))
    return pl.pallas_call(
        body, name="rs_chip_add_" + name,
        grid_spec=pltpu.PrefetchScalarGridSpec(
            num_scalar_prefetch=1, grid=(a_, nb), in_specs=[g_spec, s_spec, r_spec(0), r_spec(1), r_spec(2)],
            out_specs=pl.BlockSpec((None, rb, c_), lambda a, i, p: (a, p[0] * nb + i, 0))),
        out_shape=jax.ShapeDtypeStruct((a_, b_, c_), F32), compiler_params=_cparams(("parallel", "parallel")),
    )(place, g, got_pair, got_chips, got_chips, got_chips)


def rs_pair_gather_big(halves, tag):
    names = tuple(halves)
    n = len(names)

    def body(*refs):
        outs, send_sems, recv_sems = refs[n:2 * n], refs[2 * n], refs[2 * n + 1]
        x, y, c, _ = _place()
        cps = []
        for a, name in enumerate(names):
            mine = outs[a].at[_shard_idx(name, c)]
            cp = _remote(mine, mine, send_sems.at[a], recv_sems.at[a], (x, y, 1 - c))
            cp.start()
            cps.append(cp)
        for a, name in enumerate(names):
            cps[a].wait_send()
            theirs = outs[a].at[_shard_idx(name, 1 - c)]
            _remote(theirs, theirs, send_sems.at[a], recv_sems.at[a], (x, y, 1 - c)).wait_recv()

    outs = pl.pallas_call(
        body, name="rs_pair_gather_" + tag, in_specs=[ANY] * n, out_specs=[ANY] * n,
        input_output_aliases={a: a for a in range(n)},
        out_shape=[jax.ShapeDtypeStruct(BIG_SPECS[name][1:], F32) for name in names],
        scratch_shapes=[pltpu.SemaphoreType.DMA((n,)), pltpu.SemaphoreType.DMA((n,))],
    )(*[halves[name] for name in names])
    return dict(zip(names, outs))


def _pair_sends(names, ins, lands, send_sems, recv_sems):
    x, y, c, _ = _place()
    return [_remote(ins[a].at[_full_idx(name, None, 1 - c)], lands[a], send_sems.at[a], recv_sems.at[a], (x, y, 1 - c))
            for a, name in enumerate(names)]


def rs_pair_start(grads, names, tag):
    return _copies_start("rs_pair_start_" + tag, functools.partial(_pair_sends, names), [grads[name] for name in names],
                         [_full_shape(name, half=True) for name in names], per_source=1, land_dtype=F32)


def rs_middle(pair_state, names, tag, place, after):
    ins, lands = _copies_wait("rs_pair_wait_" + tag, functools.partial(_pair_sends, names), pair_state, after)
    grads, got_pair = dict(zip(names, ins)), dict(zip(names, lands))
    pairs = {name: rs_pair_add_big(name, place, grads[name], got_pair[name]) for name in names}
    state, token = rs_chip_start(pairs, names, tag)
    return (grads, got_pair, state), token


def rs_end(begun, names, tag, place, after):
    grads, got_pair, state = begun
    got_chips = rs_chip_wait(state, names, tag, after)
    return {name: rs_chip_add_big(name, place, grads[name], got_pair[name], got_chips[name]) for name in names}


PACK_W = 1024
SMALL =(("mem_norm_w", 1024), ("mem_k_norm_w", 128), ("norm1_w", 2048), ("dn_a_log", 8), ("dn_dt_bias", 8),
         ("dn_o_norm_w", 128), ("fox_f_bias", 8), ("fox_q_norm_w", 128), ("fox_k_norm_w", 128), ("memq_norm_w", 256),
         ("norm2_w", 2048))
SMALL_ROWS = 8
CONV_ROWS = 4 * 3 * D_MODEL // PACK_W
LOSS_AT = sum(n for _, n in SMALL)


def pack_small(parts, extra=None):
    flat = [parts[name].astype(F32).reshape(-1) for name, _ in SMALL]
    used = LOSS_AT
    if extra is not None:
        flat.append(extra.reshape(1))
        used += 1
    flat.append(jnp.zeros((SMALL_ROWS * PACK_W - used,), F32))
    return jnp.concatenate(flat).reshape(SMALL_ROWS, PACK_W)


def unpack_small(packed, shapes):
    flat, out, at = packed.reshape(-1), {}, 0
    for name, n in SMALL:
        out[name] = flat[at:at + n].reshape(shapes[name])
        at += n
    return out


def _adam_all(w, g, m, v, name):
    shape = w.shape
    r2 = lambda a: a.reshape(-1, shape[-1])
    d, nm, nv = adamw(r2(w), r2(g), r2(m), r2(v), name=name)
    return d.reshape(shape), nm.reshape(shape), nv.reshape(shape)


WEIGHTS = ("mem_norm_w", "w_mem_kv", "mem_k_norm_w", "norm1_w", "dn_w_in", "dn_conv_w", "dn_a_log", "dn_dt_bias",
           "dn_o_norm_w", "fox_w_in", "fox_f_bias", "fox_q_norm_w", "fox_k_norm_w", "memq_norm_w", "w_out", "norm2_w",
           "w_mlp1", "w_mlp2")


def kernel(x, mem, mem_norm_w, w_mem_kv, mem_k_norm_w, norm1_w, dn_w_in, dn_conv_w, dn_a_log, dn_dt_bias, dn_o_norm_w, fox_w_in, fox_f_bias, fox_q_norm_w, fox_k_norm_w, memq_norm_w, w_out, norm2_w, w_mlp1, w_mlp2, loss_target, m_mem_norm_w, m_w_mem_kv, m_mem_k_norm_w, m_norm1_w, m_dn_w_in, m_dn_conv_w, m_dn_a_log, m_dn_dt_bias, m_dn_o_norm_w, m_fox_w_in, m_fox_f_bias, m_fox_q_norm_w, m_fox_k_norm_w, m_memq_norm_w, m_w_out, m_norm2_w, m_w_mlp1, m_w_mlp2, v_mem_norm_w, v_w_mem_kv, v_mem_k_norm_w, v_norm1_w, v_dn_w_in, v_dn_conv_w, v_dn_a_log, v_dn_dt_bias, v_dn_o_norm_w, v_fox_w_in, v_fox_f_bias, v_fox_q_norm_w, v_fox_k_norm_w, v_memq_norm_w, v_w_out, v_norm2_w, v_w_mlp1, v_w_mlp2):
    args = dict(locals())
    w = {n: args[n] for n in WEIGHTS}
    m = {n: args["m_" + n] for n in WEIGHTS}
    v = {n: args["v_" + n] for n in WEIGHTS}
    core, chip = lax.axis_index("c"), 2 * lax.axis_index("x") + lax.axis_index("y")
    place = jnp.stack([core, chip]).astype(jnp.int32)

    shards = {name: w[name].reshape(BIG_SPECS[name][1:]).astype(MM) for name in BIG_NAMES}
    w_in_full = lambda arr, n_scalars: w_in_slots_to_kernel(arr[0], n_scalars)
    early = {name: with_own_slot(name, arr, shards[name], chip)
             for name, arr in all_gather_big(shards, EARLY_NAMES).items()}
    conv_mine = jnp.where(core == 0, dn_conv_w[0], 0.0)
    conv_placed = lax.dynamic_update_slice(jnp.zeros((4, 3 * D_MODEL), F32), conv_mine, (0, 768 * chip))
    conv_full = all_reduce_small(jnp.pad(conv_placed.reshape(CONV_ROWS, PACK_W), ((0, 16 - CONV_ROWS), (0, 0))))
    late_shards, early, conv_full = lax.optimization_barrier(
        ({name: shards[name] for name in LATE_NAMES}, early, conv_full))
    late_state, token = all_gather_start(late_shards, LATE_NAMES)
    wt = dict(w_mem_kv=_after(early["w_mem_kv"].reshape(D_MODEL, 2 * MEM_WIDTH), token),
              dn_w_in=w_in_full(early["dn_w_in"], 2 * N_HEADS), conv_w=conv_full[:CONV_ROWS].reshape(4, 3 * D_MODEL))

    def late(after):
        late_shards, lands = all_gather_wait(late_state, LATE_NAMES, after)
        full = {name: with_own_slot(name, arr, late_shards[name], chip)
                for name, arr in all_gather_pass_on(lands, LATE_NAMES).items()}
        return dict(fox_w_in=w_in_full(full["fox_w_in"], N_HEADS), w_out=full["w_out"].reshape(2, 3 * MEM_WIDTH, D_MODEL),
                    w_mlp1=full["w_mlp1"], w_mlp2=full["w_mlp2"].reshape(2, D_FF, D_MODEL))

    sm = dict(mem_norm_w=mem_norm_w, mem_k_norm_w=mem_k_norm_w, norm1_w=norm1_w, norm2_w=norm2_w, memq_norm_w=memq_norm_w,
              dn_a_log=dn_a_log[0], dn_dt_bias=dn_dt_bias[0], dn_o_norm_w=dn_o_norm_w, fox_f_bias=fox_f_bias[0],
              fox_q_norm_w=fox_q_norm_w, fox_k_norm_w=fox_k_norm_w)
    w_in_slots = lambda g, n_scalars: w_in_kernel_to_slots(g, n_scalars)[None]
    rows_view = lambda g, name: g.reshape(_full_shape(name))
    pair_started, begun = {}, {}

    def on_layer1(big1):
        grads1 = dict(fox_w_in=w_in_slots(big1["w_in"], N_HEADS), w_out_1=rows_view(big1["w_out"], "w_out_1"),
                      w_mlp2_1=rows_view(big1["w_mlp2"], "w_mlp2_1"), w_mlp1_1=big1["w_mlp1"][None])
        pair_started["layer1"], token = rs_pair_start(grads1, RS_LAYER1, "layer1")
        return token

    def on_mlp0(d_w_mlp2, d_w_mlp1, d_w_out):
        begun["layer1"], token1 = rs_middle(pair_started["layer1"], RS_LAYER1, "layer1", place, d_w_out)
        grads0 = dict(w_mlp2_0=rows_view(d_w_mlp2, "w_mlp2_0"), w_mlp1_0=d_w_mlp1[None],
                      w_out_0=rows_view(d_w_out, "w_out_0"))
        pair_started["mlp0"], token0 = rs_pair_start(_after(grads0, token1), RS_MLP0, "mlp0")
        return token1, token0

    def on_core0(d_o):
        begun["mlp0"], token = rs_middle(pair_started["mlp0"], RS_MLP0, "mlp0", place, d_o)
        return token

    loss_part, dx, big, small = local_step(x[0], mem[0], loss_target[0], wt, sm, late, on_layer1, on_mlp0, on_core0)
    small_pack = jnp.concatenate([pack_small(small, loss_part[0, :1]), big["conv_w"].reshape(CONV_ROWS, PACK_W),
                                  jnp.zeros((24 - SMALL_ROWS - CONV_ROWS, PACK_W), F32)])
    small_all = all_reduce_small(small_pack)
    small_sum = small_all[:SMALL_ROWS]
    conv_sum = lax.dynamic_slice(small_all[SMALL_ROWS:SMALL_ROWS + CONV_ROWS].reshape(4, 3 * D_MODEL), (0, 768 * chip), (4, 768))
    loss = small_sum.reshape(-1)[LOSS_AT]
    halves = rs_end(begun["layer1"], RS_LAYER1, "layer1", place, small_all)
    halves.update(rs_end(begun["mlp0"], RS_MLP0, "mlp0", place, small_all))
    summed = rs_pair_gather_big(halves, "early")

    last = dict(dn_w_in=w_in_slots(big["dn_w_in"], 2 * N_HEADS), w_mem_kv=rows_view(big["w_mem_kv"], "w_mem_kv"))
    last, summed = lax.optimization_barrier((last, summed))
    got_pair = rs_pair_exchange_big(last, RS_LAST, "last")
    pairs = {name: rs_pair_add_big(name, place, last[name], got_pair[name]) for name in RS_LAST}
    last_state, token = rs_chip_start(pairs, RS_LAST, "last")
    summed, token = lax.optimization_barrier((summed, token))

    big_sum = {"fox_w_in": summed["fox_w_in"]}
    big_sum.update({name: jnp.concatenate([summed[name + "_0"], summed[name + "_1"]]) for name in ("w_out", "w_mlp2", "w_mlp1")})
    grads = unpack_small(small_sum, {n: w[n].shape for n, _ in SMALL})
    grads.update({name: big_sum[name].reshape(w[name].shape) for name in big_sum}, dn_conv_w=conv_sum[None])
    delta, new_m, new_v = {}, {}, {}
    for n in ("fox_w_in", "w_out", "w_mlp1", "w_mlp2", "dn_conv_w"):
        delta[n], new_m[n], new_v[n] = _adam_all(w[n], grads[n], m[n], v[n], "adamw_" + n)

    got_chips = rs_chip_wait(last_state, RS_LAST, "last", delta["w_mlp2"])
    summed_last = rs_pair_gather_big({name: rs_chip_add_big(name, place, last[name], got_pair[name], got_chips[name])
                                      for name in RS_LAST}, "last")
    for n in RS_LAST:
        grads[n] = summed_last[n].reshape(w[n].shape)
        delta[n], new_m[n], new_v[n] = _adam_all(w[n], grads[n], m[n], v[n], "adamw_" + n)
    shapes = {n: w[n].shape for n, _ in SMALL}
    d_s, m_s, v_s = adamw(pack_small(w), small_sum, pack_small(m), pack_small(v), name="adamw_small")
    for out, packed in ((delta, d_s), (new_m, m_s), (new_v, v_s)):
        out.update(unpack_small(packed, shapes))
    return (loss, dx[None], *[grads[n] for n in WEIGHTS], *[delta[n] for n in WEIGHTS],
            *[new_m[n] for n in WEIGHTS], *[new_v[n] for n in WEIGHTS])
```

```python
import functools

import jax
import jax.numpy as jnp
from jax import lax
from jax.experimental import pallas as pl
from jax.experimental.pallas import tpu as pltpu

F32 = jnp.float32
MM = jnp.bfloat16
HI = lax.Precision.HIGHEST

D_MODEL = 1024
HEAD_DIM = 128
N_HEADS = 8
MEM_HEADS = 4
MEM_WIDTH = MEM_HEADS * HEAD_DIM
N_MEM = 256
D_FF = 4 * D_MODEL
CHUNK = 64
EPS = 1e-6
QSCALE = HEAD_DIM ** -0.5
PROJ_W = 4736
TAIL = 4608
TAIL_BLK = TAIL // HEAD_DIM
ROWS = 256
VMEM_LIMIT = 56 * 1024 * 1024

ADAM_LR = 0.001
ADAM_B1 = 0.9
ADAM_B2 = 0.999
ADAM_EPS = 1e-08
ADAM_WD = 0.01
ADAM_STEP = 10

N_DEV = 8
N_CHIP = 4
MESH = pl.DeviceIdType.MESH


def _cparams(sem=None):
    return pltpu.CompilerParams(dimension_semantics=sem, vmem_limit_bytes=VMEM_LIMIT)


def _dot(a, b, ca, cb, hi):
    dims = (((ca,), (cb,)), ((), ()))
    if hi:
        return lax.dot_general(a, b, dims, precision=HI, preferred_element_type=F32)
    return lax.dot_general(a.astype(MM), b.astype(MM), dims, preferred_element_type=F32)


@functools.partial(jax.custom_vjp, nondiff_argnums=(2, 3, 4))
def mmul(a, b, ca, cb, hi):
    return _dot(a, b, ca, cb, hi)


def _mmul_fwd(a, b, ca, cb, hi):
    return _dot(a, b, ca, cb, hi), (a, b)


def _mmul_bwd(ca, cb, hi, res, g):
    a, b = res
    if ca == 1:
        da = _dot(g, b, 1, 1, hi) if cb == 0 else _dot(g, b, 1, 0, hi)
    else:
        da = _dot(b, g, 1, 1, hi) if cb == 0 else _dot(b, g, 0, 1, hi)
    if cb == 0:
        db = _dot(a, g, 0, 0, hi) if ca == 1 else _dot(a, g, 1, 0, hi)
    else:
        db = _dot(g, a, 0, 0, hi) if ca == 1 else _dot(g, a, 0, 1, hi)
    return da.astype(a.dtype), db.astype(b.dtype)


mmul.defvjp(_mmul_fwd, _mmul_bwd)


def _iota2(n, m):
    return lax.broadcasted_iota(jnp.int32, (n, m), 0), lax.broadcasted_iota(jnp.int32, (n, m), 1)


def _same_block(r, c, shift):
    return lax.shift_right_logical(r, shift) == lax.shift_right_logical(c, shift)


def _split_bf16(x):
    hi = x.astype(jnp.bfloat16)
    return hi, (x - hi.astype(F32)).astype(jnp.bfloat16)


def _dot3(a, b, ca, cb):
    dims = (((ca,), (cb,)), ((), ()))
    (ah, al), (bh, bl) = _split_bf16(a), _split_bf16(b)
    d = lambda x, y: lax.dot_general(x, y, dims, preferred_element_type=F32)
    return d(ah, bh) + (d(ah, bl) + d(al, bh))


def _tri_inv_impl(a):
    n = a.shape[0]
    r, c = _iota2(n, n)
    eye = (r == c).astype(F32)
    b16, b32 = _same_block(r, c, 4), _same_block(r, c, 5)
    a0 = jnp.where(b16, a, 0.0)
    p = eye - a0
    b = _dot3(a0, a0, 1, 0)
    p = p + _dot3(p, b, 1, 0)
    b = _dot3(b, b, 1, 0)
    p = p + _dot3(p, b, 1, 0)
    b = _dot3(b, b, 1, 0)
    p = p + _dot3(p, b, 1, 0)
    a1 = jnp.where(jnp.logical_and(b32, jnp.logical_not(b16)), a, 0.0)
    p = p - _dot3(_dot3(p, a1, 1, 0), p, 1, 0)
    a2 = jnp.where(b32, 0.0, a)
    p = p - _dot3(_dot3(p, a2, 1, 0), p, 1, 0)
    return p


@jax.custom_vjp
def tri_inv(a):
    return _tri_inv_impl(a)


def _tri_inv_fwd(a):
    p = _tri_inv_impl(a)
    return p, p


def _tri_inv_bwd(p, g):
    return (-_dot3(_dot3(p, g, 0, 0), p, 1, 1),)


tri_inv.defvjp(_tri_inv_fwd, _tri_inv_bwd)


def _sigmoid(x):
    return 1.0 / (1.0 + jnp.exp(-x))


def _softplus(x):
    return jnp.maximum(x, 0.0) + jnp.log(1.0 + jnp.exp(-jnp.abs(x)))


def _silu(x):
    return x * _sigmoid(x)


def _rms(x, w):
    return x * lax.rsqrt(jnp.mean(x * x, axis=-1, keepdims=True) + EPS) * w


def _bf_round(x):
    return x.astype(MM).astype(F32)


def _acc(ref, val, first):
    @pl.when(first)
    def _():
        ref[...] = val

    @pl.when(jnp.logical_not(first))
    def _():
        ref[...] += val


def _tile(n, pref):
    if n % pref == 0:
        return pref
    return n


def matmul(a, b, *, ta=False, tb=False, b_slots=False, b_layer=None, res=None, also_sqrelu=False, times_dsqrelu=None,
           out_dtype=F32, name, tm=1024, tn=1024, tk=1024):
    m, k = (a.shape[1], a.shape[0]) if ta else a.shape
    b_shape = b.shape if b_layer is None else b.shape[1:]
    if b_slots:
        n = b_shape[1] if tb else N_CHIP * b_shape[2]
        assert (N_CHIP * b_shape[2] if tb else b_shape[1]) == k, (a.shape, b.shape, ta, tb)
        tn, tk = (tn, b_shape[2]) if tb else (b_shape[2], tk)
    else:
        n = b_shape[0] if tb else b_shape[1]
        assert (b_shape[1] if tb else b_shape[0]) == k, (a.shape, b.shape, ta, tb)
    tm, tn, tk = _tile(m, tm), _tile(n, tn), _tile(k, tk)
    nk = k // tk
    ca, cb = (0 if ta else 1), (1 if tb else 0)

    extra = tuple(e for e in (res, times_dsqrelu) if e is not None)
    assert len(extra) <= 1

    def body(a_ref, b_ref, *rest):
        e_ref = rest[0] if extra else None
        o_ref = rest[len(extra)]

        def finish(total):
            if res is not None:
                total = total + e_ref[...]
            if times_dsqrelu is not None:
                total = total * (2.0 * jnp.maximum(e_ref[...], 0.0))
            o_ref[...] = total.astype(o_ref.dtype)
            if also_sqrelu:
                rest[len(extra) + 1][...] = _sqrelu(total).astype(MM)

        if nk == 1:
            finish(_dot(a_ref[...], b_ref[...], ca, cb, False))
            return
        acc_ref, kk = rest[-1], pl.program_id(2)

        @pl.when(kk == 0)
        def _():
            acc_ref[...] = jnp.zeros_like(acc_ref)

        acc_ref[...] += _dot(a_ref[...], b_ref[...], ca, cb, False)

        @pl.when(kk == nk - 1)
        def _():
            finish(acc_ref[...])

    a_spec = pl.BlockSpec((tk, tm), lambda i, j, l: (l, i)) if ta else pl.BlockSpec((tm, tk), lambda i, j, l: (i, l))
    lead = () if b_layer is None else (b_layer,)
    if b_slots:
        b_block, b_index = ((None, tn, tk), lambda i, j, l: (l, j, 0)) if tb else ((None, tk, tn), lambda i, j, l: (j, l, 0))
    else:
        b_block, b_index = ((tn, tk), lambda i, j, l: (j, l)) if tb else ((tk, tn), lambda i, j, l: (l, j))
    b_spec = pl.BlockSpec((None,) * len(lead) + b_block, lambda i, j, l: lead + b_index(i, j, l))
    o_spec = pl.BlockSpec((tm, tn), lambda i, j, l: (i, j))
    out_shape = [jax.ShapeDtypeStruct((m, n), out_dtype)] + [jax.ShapeDtypeStruct((m, n), MM)] * also_sqrelu
    outs = pl.pallas_call(
        body, name=name, grid=(m // tm, n // tn, nk),
        in_specs=[a_spec, b_spec] + [o_spec] * len(extra), out_specs=[o_spec] * len(out_shape), out_shape=out_shape,
        scratch_shapes=[pltpu.VMEM((tm, tn), F32)] * (nk > 1),
        compiler_params=_cparams(("parallel", "parallel", "arbitrary")),
    )(a, b, *extra)
    return outs if also_sqrelu else outs[0]


def rms_fwd(x, w, *, name):
    t, d = x.shape

    def body(x_ref, w_ref, o_ref):
        o_ref[...] = _rms(x_ref[...], w_ref[...]).astype(o_ref.dtype)

    return pl.pallas_call(
        body, name=name, grid=(t // ROWS,),
        in_specs=[pl.BlockSpec((ROWS, d), lambda i: (i, 0)), pl.BlockSpec((1, d), lambda i: (0, 0))],
        out_specs=pl.BlockSpec((ROWS, d), lambda i: (i, 0)),
        out_shape=jax.ShapeDtypeStruct((t, d), MM), compiler_params=_cparams(("parallel",)),
    )(x, w)


def rms_bwd(x, w, dh, dres, *, name):
    t, d = x.shape

    def body(x_ref, w_ref, dh_ref, dr_ref, dx_ref, dw_ref):
        _, vjp = jax.vjp(_rms, x_ref[...], w_ref[...])
        dx, dw = vjp(dh_ref[...].astype(F32))
        dx_ref[...] = dx + dr_ref[...]
        _acc(dw_ref, dw, pl.program_id(0) == 0)

    row = pl.BlockSpec((ROWS, d), lambda i: (i, 0))
    vec = pl.BlockSpec((1, d), lambda i: (0, 0))
    return pl.pallas_call(
        body, name=name, grid=(t // ROWS,), in_specs=[row, vec, row, row], out_specs=[row, vec],
        out_shape=[jax.ShapeDtypeStruct((t, d), F32), jax.ShapeDtypeStruct((1, d), F32)],
        compiler_params=_cparams(("arbitrary",)),
    )(x, w, dh, dres)


def _sqrelu(x):
    return jnp.square(jnp.maximum(x, 0.0))


def loss_fwd(y, target, *, name):
    t, d = y.shape

    def body(y_ref, t_ref, dy_ref, l_ref):
        e = y_ref[...] - t_ref[...]
        dy_ref[...] = e * (1.0 / d)
        part = 0.5 * jnp.sum(jnp.sum(e * e, axis=-1, keepdims=True) * (1.0 / d), axis=0, keepdims=True)
        _acc(l_ref, jnp.broadcast_to(part, (1, HEAD_DIM)), pl.program_id(0) == 0)

    blk = pl.BlockSpec((ROWS, d), lambda i: (i, 0))
    return pl.pallas_call(
        body, name=name, grid=(t // ROWS,), in_specs=[blk, blk],
        out_specs=[blk, pl.BlockSpec((1, HEAD_DIM), lambda i: (0, 0))],
        out_shape=[jax.ShapeDtypeStruct((t, d), F32), jax.ShapeDtypeStruct((1, HEAD_DIM), F32)],
        compiler_params=_cparams(("arbitrary",)),
    )(y, target)


def _mem_kv(mem, wn, wkn, *ws):
    mn = _rms(mem, wn)
    outs = []
    for h in range(MEM_HEADS):
        outs.append(_rms(mmul(mn, ws[h], 1, 0, False), wkn))
    for h in range(MEM_HEADS):
        outs.append(mmul(mn, ws[MEM_HEADS + h], 1, 0, False))
    return tuple(outs)


def _w_cols(w_ref):
    return [w_ref[:, h * HEAD_DIM:(h + 1) * HEAD_DIM] for h in range(2 * MEM_HEADS)]


def mem_fwd(mem, wn, wkv, wkn):
    def body(mem_ref, wn_ref, w_ref, wkn_ref, k_ref, v_ref):
        outs = _mem_kv(mem_ref[...], wn_ref[...], wkn_ref[...], *_w_cols(w_ref))
        for h in range(MEM_HEADS):
            k_ref[:, h * HEAD_DIM:(h + 1) * HEAD_DIM] = outs[h]
            v_ref[:, h * HEAD_DIM:(h + 1) * HEAD_DIM] = outs[MEM_HEADS + h]

    shp = jax.ShapeDtypeStruct((mem.shape[0], MEM_WIDTH), F32)
    return pl.pallas_call(body, name="mem_fwd", out_shape=[shp, shp], compiler_params=_cparams())(mem, wn, wkv, wkn)


def mem_bwd(mem, wn, wkv, wkn, dk0, dv0, dk1, dv1):
    def body(mem_ref, wn_ref, w_ref, wkn_ref, dk0_ref, dv0_ref, dk1_ref, dv1_ref, dwn_ref, dw_ref, dwkn_ref):
        _, vjp = jax.vjp(lambda wn_, wkn_, *ws: _mem_kv(mem_ref[...], wn_, wkn_, *ws),
                         wn_ref[...], wkn_ref[...], *[w.astype(F32) for w in _w_cols(w_ref)])
        cols = lambda a, b: tuple(a[:, h * HEAD_DIM:(h + 1) * HEAD_DIM] + b[:, h * HEAD_DIM:(h + 1) * HEAD_DIM]
                                  for h in range(MEM_HEADS))
        cts = cols(dk0_ref, dk1_ref) + cols(dv0_ref, dv1_ref)
        grads = vjp(cts)
        dwn_ref[...] = grads[0]
        dwkn_ref[...] = grads[1]
        for h in range(2 * MEM_HEADS):
            dw_ref[:, h * HEAD_DIM:(h + 1) * HEAD_DIM] = grads[2 + h]

    return pl.pallas_call(
        body, name="mem_bwd",
        out_shape=[jax.ShapeDtypeStruct((1, D_MODEL), F32), jax.ShapeDtypeStruct((D_MODEL, 2 * MEM_WIDTH), F32),
                   jax.ShapeDtypeStruct((1, HEAD_DIM), F32)],
        compiler_params=_cparams(),
    )(mem, wn, wkv, wkn, dk0, dv0, dk1, dv1)


def _memattn(q, wq, mk, mv):
    qn = _rms(q, wq) * QSCALE
    s = mmul(qn, mk, 1, 1, False)
    s = s - jnp.max(s, axis=-1, keepdims=True)
    p = jnp.exp(s)
    p = p / jnp.sum(p, axis=-1, keepdims=True)
    return mmul(p, mv, 1, 0, False)


def _lanes(j):
    return slice(j * HEAD_DIM, (j + 1) * HEAD_DIM)


def _memattn_specs(t):
    qspec = pl.BlockSpec((ROWS, MEM_WIDTH), lambda i: (i, (TAIL - MEM_WIDTH) // MEM_WIDTH))
    wspec = pl.BlockSpec((1, HEAD_DIM), lambda i: (0, 0))
    mspec = pl.BlockSpec((N_MEM, MEM_WIDTH), lambda i: (0, 0))
    ospec = pl.BlockSpec((ROWS, MEM_WIDTH), lambda i: (i, 0))
    return qspec, wspec, mspec, ospec


def memattn_fwd(proj, wq, mk, mv, *, name):
    t = proj.shape[0]
    qspec, wspec, mspec, ospec = _memattn_specs(t)

    def body(q_ref, w_ref, k_ref, v_ref, o_ref):
        for h in range(MEM_HEADS):
            o_ref[:, _lanes(h)] = _memattn(q_ref[:, _lanes(h)], w_ref[...], k_ref[:, _lanes(h)],
                                           v_ref[:, _lanes(h)]).astype(o_ref.dtype)

    return pl.pallas_call(
        body, name=name, grid=(t // ROWS,), in_specs=[qspec, wspec, mspec, mspec], out_specs=ospec,
        out_shape=jax.ShapeDtypeStruct((t, MEM_WIDTH), MM), compiler_params=_cparams(("parallel",)),
    )(proj, wq, mk, mv)


def memattn_bwd(proj, wq, mk, mv, dcat, *, name):
    t = proj.shape[0]
    qspec, wspec, mspec, ospec = _memattn_specs(t)
    dospec = pl.BlockSpec((ROWS, MEM_WIDTH), lambda i: (i, D_MODEL // MEM_WIDTH))

    def body(q_ref, w_ref, k_ref, v_ref, do_ref, dq_ref, dw_ref, dk_ref, dv_ref):
        first = pl.program_id(0) == 0
        dw_sum = jnp.zeros((1, HEAD_DIM), F32)
        for h in range(MEM_HEADS):
            _, vjp = jax.vjp(_memattn, q_ref[:, _lanes(h)], w_ref[...], k_ref[:, _lanes(h)], v_ref[:, _lanes(h)])
            dq, dw, dk, dv = vjp(do_ref[:, _lanes(h)].astype(F32))
            dq_ref[:, _lanes(h)] = dq.astype(dq_ref.dtype)
            dw_sum = dw_sum + dw
            _acc(dk_ref.at[:, _lanes(h)], dk, first)
            _acc(dv_ref.at[:, _lanes(h)], dv, first)
        _acc(dw_ref, dw_sum, first)

    mshape = jax.ShapeDtypeStruct((N_MEM, MEM_WIDTH), F32)
    return pl.pallas_call(
        body, name=name, grid=(t // ROWS,), in_specs=[qspec, wspec, mspec, mspec, dospec],
        out_specs=[ospec, wspec, mspec, mspec],
        out_shape=[jax.ShapeDtypeStruct((t, MEM_WIDTH), MM), jax.ShapeDtypeStruct((1, HEAD_DIM), F32), mshape, mshape],
        compiler_params=_cparams(("arbitrary",)),
    )(proj, wq, mk, mv, dcat)


def _shift_rows(x, s, up):
    n = x.shape[0]
    r = lax.broadcasted_iota(jnp.int32, x.shape, 0)
    if up:
        return jnp.where(r < n - s, pltpu.roll(x, n - s, 0), 0.0)
    return jnp.where(r >= s, pltpu.roll(x, s, 0), 0.0)


def _conv_fwd_vals(x, w):
    xb = _bf_round(x)
    wb = _bf_round(w)
    c = xb * wb[3:4, :]
    for j in range(3):
        c = c + _shift_rows(xb, 3 - j, False) * wb[j:j + 1, :]
    return xb, wb, c


def dn_prep_fwd(proj, conv_w):
    t = proj.shape[0]

    def body(x_ref, w_ref, o_ref):
        j = pl.program_id(0)
        _, _, c = _conv_fwd_vals(x_ref[...], w_ref[...])
        s = _silu(c)
        r = lax.rsqrt(jnp.sum(s * s, axis=-1, keepdims=True) + EPS)
        scale = jnp.where(j < N_HEADS, QSCALE, 1.0)
        o_ref[...] = jnp.where(j < 2 * N_HEADS, s * r * scale, s)

    return pl.pallas_call(
        body, name="dn_prep_fwd", grid=(3 * N_HEADS,),
        in_specs=[pl.BlockSpec((t, HEAD_DIM), lambda j: (0, j)), pl.BlockSpec((4, HEAD_DIM), lambda j: (0, j))],
        out_specs=pl.BlockSpec((None, t, HEAD_DIM), lambda j: (j // N_HEADS, 0, j % N_HEADS)),
        out_shape=jax.ShapeDtypeStruct((3, t, D_MODEL), F32), compiler_params=_cparams(("parallel",)),
    )(proj, conv_w)


def dn_prep_bwd(proj, conv_w, dqkv):
    t = proj.shape[0]

    def body(x_ref, w_ref, g_ref, dx_ref, dw_ref):
        j = pl.program_id(0)
        xb, wb, c = _conv_fwd_vals(x_ref[...], w_ref[...])
        sg = _sigmoid(c)
        s = c * sg
        g = g_ref[...]
        r = lax.rsqrt(jnp.sum(s * s, axis=-1, keepdims=True) + EPS)
        scale = jnp.where(j < N_HEADS, QSCALE, 1.0)
        gn = g * scale
        ds_norm = r * gn - s * (r * r * r) * jnp.sum(gn * s, axis=-1, keepdims=True)
        ds = jnp.where(j < 2 * N_HEADS, ds_norm, g)
        dc = ds * (sg + s * (1.0 - sg))
        dx = dc * wb[3:4, :]
        rows = [jnp.sum(dc * xb, axis=0, keepdims=True)]
        for jj in range(2, -1, -1):
            sh = 3 - jj
            dx = dx + _shift_rows(dc, sh, True) * wb[jj:jj + 1, :]
            rows.insert(0, jnp.sum(dc * _shift_rows(xb, sh, False), axis=0, keepdims=True))
        dx_ref[...] = dx.astype(dx_ref.dtype)
        dw_ref[...] = jnp.concatenate(rows + [jnp.zeros((4, HEAD_DIM), F32)], axis=0)

    col = pl.BlockSpec((t, HEAD_DIM), lambda j: (0, j))
    return pl.pallas_call(
        body, name="dn_prep_bwd", grid=(3 * N_HEADS,),
        in_specs=[col, pl.BlockSpec((4, HEAD_DIM), lambda j: (0, j)),
                  pl.BlockSpec((None, t, HEAD_DIM), lambda j: (j // N_HEADS, 0, j % N_HEADS))],
        out_specs=[col, pl.BlockSpec((8, HEAD_DIM), lambda j: (0, j))],
        out_shape=[jax.ShapeDtypeStruct((t, 3 * D_MODEL), MM), jax.ShapeDtypeStruct((8, 3 * D_MODEL), F32)],
        compiler_params=_cparams(("parallel",)),
    )(proj, conv_w, dqkv)


def _tri_ones(n, upper):
    r, c = _iota2(n, n)
    return (r <= c).astype(F32) if upper else (r >= c).astype(F32)


def dn_gates_fwd(proj, a_log, dt_bias):
    t = proj.shape[0]

    def body(x_ref, al_ref, dt_ref, o_ref):
        lane = lax.broadcasted_iota(jnp.int32, (CHUNK, HEAD_DIM), 1)
        tri = _tri_ones(CHUNK, False)

        def step(c, carry):
            rows = pl.ds(pl.multiple_of(c * CHUNK, CHUNK), CHUNK)
            x = x_ref[rows, :]
            g = jnp.where(lane < N_HEADS, -jnp.exp(al_ref[...]) * _softplus(x + dt_ref[...]), 0.0)
            gc = _dot(tri, g, 1, 0, True)
            o_ref[rows, :] = jnp.where(lane < N_HEADS, gc, jnp.where(lane < 2 * N_HEADS, _sigmoid(x), 0.0))
            return carry

        lax.fori_loop(0, t // CHUNK, step, 0)

    vec = pl.BlockSpec((1, HEAD_DIM), lambda i: (0, 0))
    return pl.pallas_call(
        body, name="dn_gates_fwd", grid=(1,),
        in_specs=[pl.BlockSpec((t, HEAD_DIM), lambda i: (0, TAIL_BLK)), vec, vec],
        out_specs=pl.BlockSpec((t, HEAD_DIM), lambda i: (0, 0)),
        out_shape=jax.ShapeDtypeStruct((t, HEAD_DIM), F32), compiler_params=_cparams(("arbitrary",)),
    )(proj, a_log, dt_bias)


def dn_gates_bwd(proj, a_log, dt_bias, dgates):
    t = proj.shape[0]

    def body(x_ref, al_ref, dt_ref, g_ref, dx_ref, dal_ref, ddt_ref):
        lane = lax.broadcasted_iota(jnp.int32, (CHUNK, HEAD_DIM), 1)
        tri = _tri_ones(CHUNK, True)
        dal_ref[...] = jnp.zeros_like(dal_ref)
        ddt_ref[...] = jnp.zeros_like(ddt_ref)

        def step(c, carry):
            rows = pl.ds(pl.multiple_of(c * CHUNK, CHUNK), CHUNK)
            x = x_ref[rows, :]
            dgc = jnp.where(lane < N_HEADS, g_ref[rows, :], 0.0)
            dg = _dot(tri, dgc, 1, 0, True)
            ea = -jnp.exp(al_ref[...])
            z = x + dt_ref[...]
            da = jnp.where(lane < N_HEADS, dg * ea * _sigmoid(z), 0.0)
            gval = jnp.where(lane < N_HEADS, ea * _softplus(z), 0.0)
            beta = _sigmoid(x)
            db = jnp.where(jnp.logical_and(lane >= N_HEADS, lane < 2 * N_HEADS), g_ref[rows, :] * beta * (1.0 - beta), 0.0)
            dx_ref[rows, :] = (da + db).astype(dx_ref.dtype)
            dal_ref[...] += jnp.sum(dg * gval, axis=0, keepdims=True)
            ddt_ref[...] += jnp.sum(da, axis=0, keepdims=True)
            return carry

        lax.fori_loop(0, t // CHUNK, step, 0)

    vec = pl.BlockSpec((1, HEAD_DIM), lambda i: (0, 0))
    full = pl.BlockSpec((t, HEAD_DIM), lambda i: (0, 0))
    return pl.pallas_call(
        body, name="dn_gates_bwd", grid=(1,),
        in_specs=[pl.BlockSpec((t, HEAD_DIM), lambda i: (0, TAIL_BLK)), vec, vec, full],
        out_specs=[full, vec, vec],
        out_shape=[jax.ShapeDtypeStruct((t, HEAD_DIM), MM), jax.ShapeDtypeStruct((1, HEAD_DIM), F32),
                   jax.ShapeDtypeStruct((1, HEAD_DIM), F32)],
        compiler_params=_cparams(("arbitrary",)),
    )(proj, a_log, dt_bias, dgates)


def _dn_intra(q, k, v, gcol, grow, bcol):
    r, c = _iota2(CHUNK, CHUNK)
    causal, strict = r >= c, r > c
    decay = jnp.where(causal, jnp.exp(jnp.where(causal, gcol - grow, 0.0)), 0.0)
    kb = k * bcol
    a = jnp.where(strict, mmul(kb, k, 1, 1, False) * decay, 0.0)
    tm = tri_inv(a)
    u = mmul(tm, v * bcol, 1, 0, False)
    w = mmul(tm, kb * jnp.exp(gcol), 1, 0, False)
    qk = jnp.where(causal, mmul(q, k, 1, 1, False) * decay, 0.0)
    rr = lax.broadcasted_iota(jnp.int32, (CHUNK, 1), 0)
    g_last = jnp.sum(jnp.where(rr == CHUNK - 1, gcol, 0.0), axis=0, keepdims=True)
    return u, w, q * jnp.exp(gcol), k * jnp.exp(g_last - gcol), qk, jnp.exp(g_last)


def _dn_scan(u, w, qg, kd, qk, eg, state):
    v_new = u - mmul(w, state, 1, 0, False)
    out = mmul(qg, state, 1, 0, False) + mmul(qk, v_new, 1, 0, False)
    return out, state * eg + mmul(kd, v_new, 0, 0, False)


DN_HEADS_PER_STEP = 1
DN_GROUP = 8
DN_PARTS = ((CHUNK, HEAD_DIM),) * 4 + ((CHUNK, CHUNK), (1, 1))


def _dn_scratch(hb, nc):
    return [pltpu.VMEM((hb, nc) + shape, F32) for shape in DN_PARTS]


def _dn_part_specs(hb, nc):
    return [pl.BlockSpec((hb, nc) + shape, lambda h: (h, 0, 0, 0)) for shape in DN_PARTS]


def _dn_group(nc):
    return min(DN_GROUP, nc)


def _dn_group_args(refs, j, g, grp):
    q_ref, k_ref, v_ref, gc_ref, gr_ref, bc_ref = refs
    rows = pl.ds(pl.multiple_of(g * (grp * CHUNK), grp * CHUNK), grp * CHUNK)
    cs = pl.ds(g * grp, grp)
    split = lambda ref: ref[rows, _lanes(j)].reshape(grp, CHUNK, HEAD_DIM)
    return split(q_ref), split(k_ref), split(v_ref), gc_ref[j, cs], gr_ref[j, cs], bc_ref[j, cs]


def _dn_intra_all(refs, parts, hb, nc):
    grp = _dn_group(nc)

    def group(g, carry):
        cs = pl.ds(g * grp, grp)
        for j in range(hb):
            for part, val in zip(parts, jax.vmap(_dn_intra)(*_dn_group_args(refs, j, g, grp))):
                part[j, cs] = val
        return carry

    lax.fori_loop(0, nc // grp, group, 0)


def _dn_specs(t):
    nc, hb = t // CHUNK, DN_HEADS_PER_STEP
    head = lambda which: pl.BlockSpec((None, t, hb * HEAD_DIM), lambda h: (which, 0, h))
    flat = pl.BlockSpec((t, hb * HEAD_DIM), lambda h: (0, h))
    col = pl.BlockSpec((hb, nc, CHUNK, 1), lambda h: (h, 0, 0, 0))
    row = pl.BlockSpec((hb, nc, 1, CHUNK), lambda h: (h, 0, 0, 0))
    st = pl.BlockSpec((hb, nc, HEAD_DIM, HEAD_DIM), lambda h: (h, 0, 0, 0))
    return nc, hb, head, flat, col, row, st


def dn_core_fwd(qkv, gcol, grow, bcol):
    t = qkv.shape[1]
    nc, hb, head, flat, col, row, st = _dn_specs(t)

    def body(q_ref, k_ref, v_ref, gc_ref, gr_ref, bc_ref, o_ref, s_ref, *parts):
        _dn_intra_all((q_ref, k_ref, v_ref, gc_ref, gr_ref, bc_ref), parts, hb, nc)

        def step(c, states):
            rows = pl.ds(pl.multiple_of(c * CHUNK, CHUNK), CHUNK)
            new_states = []
            for j in range(hb):
                s_ref[j, c] = states[j]
                out, new_state = _dn_scan(*[part[j, c] for part in parts], states[j])
                o_ref[rows, _lanes(j)] = out
                new_states.append(new_state)
            return tuple(new_states)

        lax.fori_loop(0, nc, step, tuple(jnp.zeros((HEAD_DIM, HEAD_DIM), F32) for _ in range(hb)))

    outs = pl.pallas_call(
        body, name="dn_core_fwd", grid=(N_HEADS // hb,),
        in_specs=[head(0), head(1), head(2), col, row, col], out_specs=[flat, st] + _dn_part_specs(hb, nc),
        out_shape=[jax.ShapeDtypeStruct((t, D_MODEL), F32), jax.ShapeDtypeStruct((N_HEADS, nc, HEAD_DIM, HEAD_DIM), F32)]
        + [jax.ShapeDtypeStruct((N_HEADS, nc) + shape, F32) for shape in DN_PARTS],
        compiler_params=_cparams(("parallel",)),
    )(qkv, qkv, qkv, gcol, grow, bcol)
    return outs[0], outs[1], tuple(outs[2:])


def dn_core_bwd(qkv, gcol, grow, bcol, states, parts, do):
    t = qkv.shape[1]
    nc, hb, head, flat, col, row, st = _dn_specs(t)
    n_parts = len(DN_PARTS)

    def body(q_ref, k_ref, v_ref, gc_ref, gr_ref, bc_ref, s_ref, do_ref, *rest):
        parts, (dqkv_ref, dgc_ref, dgr_ref, dbc_ref), dparts = rest[:n_parts], rest[n_parts:n_parts + 4], rest[n_parts + 4:]
        refs = (q_ref, k_ref, v_ref, gc_ref, gr_ref, bc_ref)

        def step(i, dstates):
            c = nc - 1 - i
            rows = pl.ds(pl.multiple_of(c * CHUNK, CHUNK), CHUNK)
            dstates_in = []
            for j in range(hb):
                _, vjp = jax.vjp(_dn_scan, *[part[j, c] for part in parts], s_ref[j, c])
                *dvals, dstate_in = vjp((do_ref[rows, _lanes(j)], dstates[j]))
                for dpart, dval in zip(dparts, dvals):
                    dpart[j, c] = dval
                dstates_in.append(dstate_in)
            return tuple(dstates_in)

        lax.fori_loop(0, nc, step, tuple(jnp.zeros((HEAD_DIM, HEAD_DIM), F32) for _ in range(hb)))

        grp = _dn_group(nc)

        def group(g, carry):
            rows = pl.ds(pl.multiple_of(g * (grp * CHUNK), grp * CHUNK), grp * CHUNK)
            cs = pl.ds(g * grp, grp)
            for j in range(hb):
                _, vjp = jax.vjp(jax.vmap(_dn_intra), *_dn_group_args(refs, j, g, grp))
                dq, dk, dv, dgc, dgr, dbc = vjp(tuple(dpart[j, cs] for dpart in dparts))
                for which, val in enumerate((dq, dk, dv)):
                    dqkv_ref[which, rows, _lanes(j)] = val.reshape(grp * CHUNK, HEAD_DIM)
                dgc_ref[j, cs] = dgc
                dgr_ref[j, cs] = dgr
                dbc_ref[j, cs] = dbc
            return carry

        lax.fori_loop(0, nc // grp, group, 0)

    return pl.pallas_call(
        body, name="dn_core_bwd", grid=(N_HEADS // hb,), scratch_shapes=_dn_scratch(hb, nc),
        in_specs=[head(0), head(1), head(2), col, row, col, st, flat] + _dn_part_specs(hb, nc),
        out_specs=[pl.BlockSpec((3, t, hb * HEAD_DIM), lambda h: (0, 0, h)), col, row, col],
        out_shape=[jax.ShapeDtypeStruct((3, t, D_MODEL), F32)] + [
            jax.ShapeDtypeStruct((N_HEADS, nc, CHUNK, 1), F32), jax.ShapeDtypeStruct((N_HEADS, nc, 1, CHUNK), F32),
            jax.ShapeDtypeStruct((N_HEADS, nc, CHUNK, 1), F32)],
        compiler_params=_cparams(("parallel",)),
    )(qkv, qkv, qkv, gcol, grow, bcol, states, do, *parts)


def gates_to_heads(gates):
    t = gates.shape[0]
    nc = t // CHUNK
    g = gates[:, :N_HEADS].T.reshape(N_HEADS, nc, CHUNK)
    b = gates[:, N_HEADS:2 * N_HEADS].T.reshape(N_HEADS, nc, CHUNK)
    return g[..., None], g[:, :, None, :], b[..., None]


def heads_to_gates(dgcol, dgrow, dbcol):
    nh, nc = dgcol.shape[:2]
    dg = (dgcol[..., 0] + dgrow[:, :, 0, :]).reshape(nh, nc * CHUNK).T
    db = dbcol[..., 0].reshape(nh, nc * CHUNK).T
    return jnp.concatenate([dg, db, jnp.zeros((nc * CHUNK, HEAD_DIM - 2 * nh), F32)], axis=1)


def _dn_out(o, z, w):
    return _rms(o, w) * _silu(z)


def _gate_specs():
    o_spec = pl.BlockSpec((ROWS, D_MODEL), lambda i: (i, 0))
    z_spec = pl.BlockSpec((ROWS, D_MODEL), lambda i: (i, 3))
    w_spec = pl.BlockSpec((1, HEAD_DIM), lambda i: (0, 0))
    return o_spec, z_spec, w_spec


def dn_out_fwd(o, proj, w):
    t = o.shape[0]
    o_spec, z_spec, w_spec = _gate_specs()

    def body(o_ref, z_ref, w_ref, y_ref):
        for h in range(N_HEADS):
            y_ref[:, _lanes(h)] = _dn_out(o_ref[:, _lanes(h)], z_ref[:, _lanes(h)], w_ref[...]).astype(y_ref.dtype)

    return pl.pallas_call(
        body, name="dn_out_fwd", grid=(t // ROWS,), in_specs=[o_spec, z_spec, w_spec], out_specs=o_spec,
        out_shape=jax.ShapeDtypeStruct((t, D_MODEL), MM), compiler_params=_cparams(("parallel",)),
    )(o, proj, w)


def dn_out_bwd(o, proj, w, dcat):
    t = o.shape[0]
    o_spec, z_spec, w_spec = _gate_specs()

    def body(o_ref, z_ref, w_ref, g_ref, do_ref, dz_ref, dw_ref):
        dw_sum = jnp.zeros((1, HEAD_DIM), F32)
        for h in range(N_HEADS):
            _, vjp = jax.vjp(_dn_out, o_ref[:, _lanes(h)], z_ref[:, _lanes(h)], w_ref[...])
            do, dz, dw = vjp(g_ref[:, _lanes(h)].astype(F32))
            do_ref[:, _lanes(h)] = do
            dz_ref[:, _lanes(h)] = dz.astype(dz_ref.dtype)
            dw_sum = dw_sum + dw
        _acc(dw_ref, dw_sum, pl.program_id(0) == 0)

    return pl.pallas_call(
        body, name="dn_out_bwd", grid=(t // ROWS,), in_specs=[o_spec, z_spec, w_spec, o_spec],
        out_specs=[o_spec, o_spec, w_spec],
        out_shape=[jax.ShapeDtypeStruct((t, D_MODEL), F32), jax.ShapeDtypeStruct((t, D_MODEL), MM),
                   jax.ShapeDtypeStruct((1, HEAD_DIM), F32)],
        compiler_params=_cparams(("arbitrary",)),
    )(o, proj, w, dcat)


def _fox_norm(x, w, scale):
    return _rms(x, w) * scale


def _fox_prep_specs():
    x_spec = pl.BlockSpec((ROWS, 2 * D_MODEL), lambda i: (i, 0))
    w_spec = pl.BlockSpec((2, 1, HEAD_DIM), lambda i: (0, 0, 0))
    y_spec = pl.BlockSpec((2, ROWS, D_MODEL), lambda i: (0, i, 0))
    return x_spec, w_spec, y_spec


def fox_prep_fwd(proj, wqk):
    t = proj.shape[0]
    x_spec, w_spec, y_spec = _fox_prep_specs()

    def body(x_ref, w_ref, y_ref):
        for j in range(2 * N_HEADS):
            which, scale = j // N_HEADS, (QSCALE if j < N_HEADS else 1.0)
            y_ref[which, :, _lanes(j % N_HEADS)] = _fox_norm(x_ref[:, _lanes(j)], w_ref[which], scale).astype(y_ref.dtype)

    return pl.pallas_call(
        body, name="fox_prep_fwd", grid=(t // ROWS,), in_specs=[x_spec, w_spec], out_specs=y_spec,
        out_shape=jax.ShapeDtypeStruct((2, t, D_MODEL), MM), compiler_params=_cparams(("parallel",)),
    )(proj, wqk)


def fox_prep_bwd(proj, wqk, dq, dk):
    t = proj.shape[0]
    x_spec, w_spec, _ = _fox_prep_specs()
    g_spec = pl.BlockSpec((ROWS, D_MODEL), lambda i: (i, 0))

    def body(x_ref, w_ref, dq_ref, dk_ref, dx_ref, dw_ref):
        dws = [jnp.zeros((1, HEAD_DIM), F32), jnp.zeros((1, HEAD_DIM), F32)]
        for j in range(2 * N_HEADS):
            which, scale = j // N_HEADS, (QSCALE if j < N_HEADS else 1.0)
            g_ref = dq_ref if which == 0 else dk_ref
            _, vjp = jax.vjp(lambda x, w: _fox_norm(x, w, scale), x_ref[:, _lanes(j)], w_ref[which])
            dx, dw = vjp(g_ref[:, _lanes(j % N_HEADS)])
            dx_ref[:, _lanes(j)] = dx.astype(dx_ref.dtype)
            dws[which] = dws[which] + dw
        first = pl.program_id(0) == 0
        _acc(dw_ref.at[0], dws[0], first)
        _acc(dw_ref.at[1], dws[1], first)

    return pl.pallas_call(
        body, name="fox_prep_bwd", grid=(t // ROWS,), in_specs=[x_spec, w_spec, g_spec, g_spec],
        out_specs=[x_spec, w_spec],
        out_shape=[jax.ShapeDtypeStruct((t, 2 * D_MODEL), MM), jax.ShapeDtypeStruct((2, 1, HEAD_DIM), F32)],
        compiler_params=_cparams(("arbitrary",)),
    )(proj, wqk, dq, dk)


def _row_pick(x, i):
    r = lax.broadcasted_iota(jnp.int32, x.shape, 0)
    return jnp.sum(jnp.where(r == i, x, 0.0), axis=0, keepdims=True)


def fox_gates_fwd(proj, f_bias):
    t = proj.shape[0]
    blk = HEAD_DIM

    def body(x_ref, b_ref, o_ref):
        lane = lax.broadcasted_iota(jnp.int32, (blk, HEAD_DIM), 1)
        tri = _tri_ones(blk, False)

        def step(c, carry):
            rows = pl.ds(pl.multiple_of(c * blk, blk), blk)
            lf = jnp.where(lane < N_HEADS, -_softplus(-(x_ref[rows, :] + b_ref[...])), 0.0)
            cum = _dot(tri, lf, 1, 0, True) + carry
            o_ref[rows, :] = cum
            return _row_pick(cum, blk - 1)

        lax.fori_loop(0, t // blk, step, jnp.zeros((1, HEAD_DIM), F32))

    vec = pl.BlockSpec((1, HEAD_DIM), lambda i: (0, 0))
    return pl.pallas_call(
        body, name="fox_gates_fwd", grid=(1,),
        in_specs=[pl.BlockSpec((t, HEAD_DIM), lambda i: (0, TAIL_BLK)), vec],
        out_specs=pl.BlockSpec((t, HEAD_DIM), lambda i: (0, 0)),
        out_shape=jax.ShapeDtypeStruct((t, HEAD_DIM), F32), compiler_params=_cparams(("arbitrary",)),
    )(proj, f_bias)


def fox_gates_bwd(proj, f_bias, dfcum):
    t = proj.shape[0]
    blk = HEAD_DIM
    nb = t // blk

    def body(x_ref, b_ref, g_ref, dx_ref, db_ref):
        lane = lax.broadcasted_iota(jnp.int32, (blk, HEAD_DIM), 1)
        tri = _tri_ones(blk, True)
        db_ref[...] = jnp.zeros_like(db_ref)

        def step(i, carry):
            c = nb - 1 - i
            rows = pl.ds(pl.multiple_of(c * blk, blk), blk)
            g = jnp.where(lane < N_HEADS, g_ref[rows, :], 0.0)
            dlf = _dot(tri, g, 1, 0, True) + carry
            dx = jnp.where(lane < N_HEADS, dlf * _sigmoid(-(x_ref[rows, :] + b_ref[...])), 0.0)
            dx_ref[rows, :] = dx.astype(dx_ref.dtype)
            db_ref[...] += jnp.sum(dx, axis=0, keepdims=True)
            return carry + jnp.sum(g, axis=0, keepdims=True)

        lax.fori_loop(0, nb, step, jnp.zeros((1, HEAD_DIM), F32))

    vec = pl.BlockSpec((1, HEAD_DIM), lambda i: (0, 0))
    full = pl.BlockSpec((t, HEAD_DIM), lambda i: (0, 0))
    return pl.pallas_call(
        body, name="fox_gates_bwd", grid=(1,),
        in_specs=[pl.BlockSpec((t, HEAD_DIM), lambda i: (0, TAIL_BLK)), vec, full], out_specs=[full, vec],
        out_shape=[jax.ShapeDtypeStruct((t, HEAD_DIM), MM), jax.ShapeDtypeStruct((1, HEAD_DIM), F32)],
        compiler_params=_cparams(("arbitrary",)),
    )(proj, f_bias, dfcum)


def fcum_to_heads(fcum):
    f = fcum[:, :N_HEADS].T
    return f[:, :, None], f[:, None, :]


def heads_to_fcum(dfcol, dfrow):
    d = (dfcol[:, :, 0] + dfrow[:, 0, :]).T
    return jnp.concatenate([d, jnp.zeros((d.shape[0], HEAD_DIM - N_HEADS), F32)], axis=1)


def _fox_tq(t):
    return min(t, 256)


def _fox_specs(t):
    tq = _fox_tq(t)
    q_spec = pl.BlockSpec((None, tq, HEAD_DIM), lambda h, i: (0, i, h))
    k_spec = pl.BlockSpec((None, t, HEAD_DIM), lambda h, i: (1, 0, h))
    v_spec = pl.BlockSpec((t, HEAD_DIM), lambda h, i: (0, 2 * N_HEADS + h))
    gate_spec = pl.BlockSpec((tq, HEAD_DIM), lambda h, i: (i, 3 * N_HEADS + h))
    col_spec = pl.BlockSpec((None, tq, 1), lambda h, i: (h, i, 0))
    row_spec = pl.BlockSpec((None, 1, t), lambda h, i: (h, 0, 0))
    blk_spec = pl.BlockSpec((tq, HEAD_DIM), lambda h, i: (i, h))
    head_spec = pl.BlockSpec((t, HEAD_DIM), lambda h, i: (0, h))
    return tq, q_spec, k_spec, v_spec, gate_spec, col_spec, row_spec, blk_spec, head_spec


def _fox_segments(i, tq):
    return ([(0, i * tq, False)] if i else []) + [(i * tq, (i + 1) * tq, True)]


def _fox_scores(q_ref, k_ref, fc_ref, fr_ref, lo, hi, causal):
    s = _dot(q_ref[...], k_ref[lo:hi, :], 1, 1, False) + (fc_ref[...] - fr_ref[:, lo:hi])
    if not causal:
        return s, None
    r, c = _iota2(hi - lo, hi - lo)
    return s, c <= r


def fox_attn_fwd(qk, proj, fcol, frow):
    t = proj.shape[0]
    tq, q_spec, k_spec, v_spec, gate_spec, col_spec, row_spec, blk_spec, _ = _fox_specs(t)

    def body(q_ref, k_ref, v_ref, gate_ref, fc_ref, fr_ref, mix_ref, o_ref, lse_ref):
        def block(i):
            segs = _fox_segments(i, tq)
            scores = [_fox_scores(q_ref, k_ref, fc_ref, fr_ref, *seg) for seg in segs]
            scores = [(s if mask is None else jnp.where(mask, s, -1e30), mask) for s, mask in scores]
            m = functools.reduce(jnp.maximum, [jnp.max(s, axis=-1, keepdims=True) for s, _ in scores])
            l, o = 0.0, 0.0
            for (lo, hi, _), (s, mask) in zip(segs, scores):
                p = jnp.exp(s - m)
                p = p if mask is None else jnp.where(mask, p, 0.0)
                l = l + jnp.sum(p, axis=-1, keepdims=True)
                o = o + _dot(p, v_ref[lo:hi, :], 1, 0, False)
            o = o / l
            o_ref[...] = o
            mix_ref[...] = (o * _sigmoid(gate_ref[...])).astype(mix_ref.dtype)
            lse_ref[...] = m + jnp.log(l)

        for i in range(t // tq):
            pl.when(pl.program_id(1) == i)(functools.partial(block, i))

    return pl.pallas_call(
        body, name="fox_attn_fwd", grid=(N_HEADS, t // tq),
        in_specs=[q_spec, k_spec, v_spec, gate_spec, col_spec, row_spec], out_specs=[blk_spec, blk_spec, col_spec],
        out_shape=[jax.ShapeDtypeStruct((t, D_MODEL), MM), jax.ShapeDtypeStruct((t, D_MODEL), F32),
                   jax.ShapeDtypeStruct((N_HEADS, t, 1), F32)],
        compiler_params=_cparams(("parallel", "parallel")),
    )(qk, qk, proj, proj, fcol, frow)


def fox_attn_bwd(qk, proj, fcol, frow, o, lse, dcat):
    t = proj.shape[0]
    tq, q_spec, k_spec, v_spec, gate_spec, col_spec, row_spec, blk_spec, head_spec = _fox_specs(t)

    def body(q_ref, k_ref, v_ref, gate_ref, fc_ref, fr_ref, o_ref, lse_ref, g_ref,
             dq_ref, dk_ref, dv_ref, dgate_ref, dfc_ref, dfr_ref):
        @pl.when(pl.program_id(1) == 0)
        def _():
            dk_ref[...] = jnp.zeros_like(dk_ref)
            dv_ref[...] = jnp.zeros_like(dv_ref)
            dfr_ref[...] = jnp.zeros_like(dfr_ref)

        def block(i):
            sg = _sigmoid(gate_ref[...])
            g = g_ref[...].astype(F32)
            o_pre = o_ref[...]
            do = g * sg
            dgate_ref[...] = (g * o_pre * sg * (1.0 - sg)).astype(dgate_ref.dtype)
            delta = jnp.sum(do * o_pre, axis=-1, keepdims=True)
            dq, dfc = 0.0, 0.0
            for lo, hi, causal in _fox_segments(i, tq):
                s, mask = _fox_scores(q_ref, k_ref, fc_ref, fr_ref, lo, hi, causal)
                if causal:
                    p = jnp.where(mask, jnp.exp(jnp.where(mask, s, 0.0) - lse_ref[...]), 0.0)
                else:
                    p = jnp.exp(s - lse_ref[...])
                ds = p * (_dot(do, v_ref[lo:hi, :], 1, 1, False) - delta)
                dq = dq + _dot(ds, k_ref[lo:hi, :], 1, 0, False)
                dk_ref[lo:hi, :] += _dot(ds, q_ref[...], 0, 0, False)
                dv_ref[lo:hi, :] += _dot(p, do, 0, 0, False)
                dfc = dfc + jnp.sum(ds, axis=-1, keepdims=True)
                dfr_ref[:, lo:hi] += -jnp.sum(ds, axis=0, keepdims=True)
            dq_ref[...] = dq
            dfc_ref[...] = dfc

        for i in range(t // tq):
            pl.when(pl.program_id(1) == i)(functools.partial(block, i))

    f32 = lambda *s: jax.ShapeDtypeStruct(s, F32)
    return pl.pallas_call(
        body, name="fox_attn_bwd", grid=(N_HEADS, t // tq),
        in_specs=[q_spec, k_spec, v_spec, gate_spec, col_spec, row_spec, blk_spec, col_spec, blk_spec],
        out_specs=[blk_spec, head_spec, head_spec, blk_spec, col_spec, row_spec],
        out_shape=[f32(t, D_MODEL), f32(t, D_MODEL), f32(t, D_MODEL), jax.ShapeDtypeStruct((t, D_MODEL), MM),
                   f32(N_HEADS, t, 1), f32(N_HEADS, 1, t)],
        compiler_params=_cparams(("parallel", "arbitrary")),
    )(qk, qk, proj, proj, fcol, frow, o, lse, dcat)


def adamw(w, g, m, v, *, name):
    r, c = w.shape
    rb = ROWS if r % ROWS == 0 else r

    def body(w_ref, g_ref, m_ref, v_ref, d_ref, nm_ref, nv_ref):
        g_ = g_ref[...]
        m_ = ADAM_B1 * m_ref[...] + (1.0 - ADAM_B1) * g_
        v_ = ADAM_B2 * v_ref[...] + (1.0 - ADAM_B2) * jnp.square(g_)
        m_hat = m_ / (1.0 - ADAM_B1 ** ADAM_STEP)
        v_hat = v_ / (1.0 - ADAM_B2 ** ADAM_STEP)
        d_ref[...] = -ADAM_LR * (m_hat / (jnp.sqrt(v_hat) + ADAM_EPS) + ADAM_WD * w_ref[...])
        nm_ref[...] = m_
        nv_ref[...] = v_

    blk = pl.BlockSpec((rb, c), lambda i: (i, 0))
    shp = jax.ShapeDtypeStruct((r, c), F32)
    return pl.pallas_call(body, name=name, grid=(r // rb,), in_specs=[blk] * 4, out_specs=[blk] * 3,
                          out_shape=[shp] * 3, compiler_params=_cparams(("parallel",)))(w, g, m, v)


def _place():
    x, y, c = lax.axis_index("x"), lax.axis_index("y"), lax.axis_index("c")
    return x, y, c, [(1 - x, y), (x, 1 - y), (1 - x, 1 - y)]


ANY = pl.BlockSpec(memory_space=pl.ANY)


def all_reduce_small(v):
    r, w = v.shape

    def body(v_ref, o_ref, buf, send_sems, recv_sems):
        x, y, c, _ = _place()
        me = 4 * x + 2 * y + c
        flip = lambda a, bit: 1 - a if bit else a
        cps = []
        for k in range(1, N_DEV):
            peer = (flip(x, k & 4), flip(y, k & 2), flip(c, k & 1))
            cp = pltpu.make_async_remote_copy(src_ref=v_ref, dst_ref=buf.at[me], send_sem=send_sems.at[k - 1],
                                              recv_sem=recv_sems.at[k - 1], device_id=peer, device_id_type=MESH)
            cp.start()
            cps.append((cp, 4 * peer[0] + 2 * peer[1] + peer[2]))
        buf[me] = v_ref[...]
        for k, (cp, peer_id) in enumerate(cps):
            pltpu.make_async_remote_copy(src_ref=v_ref, dst_ref=buf.at[peer_id], send_sem=send_sems.at[k],
                                         recv_sem=recv_sems.at[k], device_id=(x, y, c), device_id_type=MESH).wait_recv()
        for cp, _ in cps:
            cp.wait_send()
        acc = buf[0]
        for d in range(1, N_DEV):
            acc = acc + buf[d]
        o_ref[...] = acc

    vm = pl.BlockSpec(memory_space=pltpu.VMEM)
    return pl.pallas_call(
        body, name="all_reduce_small", in_specs=[vm], out_specs=vm, out_shape=jax.ShapeDtypeStruct((r, w), F32),
        scratch_shapes=[pltpu.VMEM((N_DEV, r, w), F32), pltpu.SemaphoreType.DMA((N_DEV - 1,)),
                        pltpu.SemaphoreType.DMA((N_DEV - 1,))],
    )(v)


def _after(x, token):
    return lax.optimization_barrier((x, token))[0]


def _vec8(v):
    return jnp.zeros((1, HEAD_DIM), F32).at[0, :N_HEADS].set(v.reshape(N_HEADS))


def _layer_fwd(i, x_in, wt, sm, mem_k, mem_v, late=None):
    tag = f"l{i}_"
    h = rms_fwd(x_in, sm["norm1_w"][i][None], name=tag + "rms1")
    w_in = wt["dn_w_in"] if i == 0 else wt["fox_w_in"]
    proj = matmul(h, w_in, name=tag + "proj", tm=256, tk=1024)
    sv = dict(x_in=x_in, h=h, proj=proj)
    if i == 0:
        qkv = dn_prep_fwd(proj, wt["conv_w"])
        gates = dn_gates_fwd(proj, _vec8(sm["dn_a_log"]), _vec8(sm["dn_dt_bias"]))
        gcol, grow, bcol = gates_to_heads(gates)
        o, states, parts = dn_core_fwd(qkv, gcol, grow, bcol)
        mix = dn_out_fwd(o, proj, sm["dn_o_norm_w"])
        sv.update(qkv=qkv, gcol=gcol, grow=grow, bcol=bcol, states=states, parts=parts, o=o)
    else:
        wqk = jnp.stack([sm["fox_q_norm_w"], sm["fox_k_norm_w"]])
        qk = fox_prep_fwd(proj, wqk)
        fcum = fox_gates_fwd(proj, _vec8(sm["fox_f_bias"]))
        fcol, frow = fcum_to_heads(fcum)
        mix, o, lse = fox_attn_fwd(qk, proj, fcol, frow)
        sv.update(wqk=wqk, qk=qk, fcol=fcol, frow=frow, o=o, lse=lse)
    mem_out = memattn_fwd(proj, sm["memq_norm_w"][i][None], mem_k, mem_v, name=tag + "memattn_fwd")
    cat = jnp.concatenate([mix, mem_out], axis=1)
    if late is not None:
        wt.update(late(cat))
    x_mid = matmul(cat, wt["w_out"], b_layer=i, res=x_in, name=tag + "out_proj")
    h2 = rms_fwd(x_mid, sm["norm2_w"][i][None], name=tag + "rms2")
    ff, act = matmul(h2, wt["w_mlp1"], b_layer=i, b_slots=True, also_sqrelu=True, out_dtype=MM, name=tag + "mlp1")
    x_out = matmul(act, wt["w_mlp2"], b_layer=i, res=x_mid, name=tag + "mlp2")
    sv.update(cat=cat, x_mid=x_mid, h2=h2, ff=ff, act=act)
    return x_out, sv


def _layer_bwd(i, dx_out, sv, wt, sm, mem_k, mem_v, on_mlp=None, on_core=None):
    tag = f"l{i}_"
    big, small = {}, {}
    dff = matmul(dx_out, wt["w_mlp2"], b_layer=i, tb=True, times_dsqrelu=sv["ff"], out_dtype=MM, name=tag + "d_ff")
    big["w_mlp2"] = matmul(sv["act"], dx_out, ta=True, name=tag + "d_w_mlp2", tk=2048)
    dh2 = matmul(dff, wt["w_mlp1"], b_layer=i, tb=True, b_slots=True, name=tag + "d_h2")
    big["w_mlp1"] = matmul(sv["h2"], dff, ta=True, name=tag + "d_w_mlp1", tm=512, tn=D_FF, tk=512)
    dx_mid, small["norm2_w"] = rms_bwd(sv["x_mid"], sm["norm2_w"][i][None], dh2, dx_out, name=tag + "rms2_bwd")
    dcat = matmul(dx_mid, wt["w_out"], b_layer=i, tb=True, name=tag + "d_cat")
    big["w_out"] = matmul(sv["cat"], dx_mid, ta=True, name=tag + "d_w_out", tk=2048)
    proj = sv["proj"]
    memq_norm_w = sm["memq_norm_w"][i][None]
    if on_mlp is not None:
        memq_norm_w = _after(memq_norm_w, on_mlp(big["w_mlp2"], big["w_mlp1"], big["w_out"]))
    dqm, small["memq_norm_w"], dmk, dmv = memattn_bwd(proj, memq_norm_w, mem_k, mem_v, dcat, name=tag + "memattn_bwd")
    t = proj.shape[0]
    pad = jnp.zeros((t, PROJ_W - TAIL - HEAD_DIM), MM)
    if i == 0:
        do, dz, small["dn_o_norm_w"] = dn_out_bwd(sv["o"], proj, sm["dn_o_norm_w"], dcat)
        bcol = sv["bcol"] if on_core is None else _after(sv["bcol"], on_core(do))
        dqkv, dgc, dgr, dbc = dn_core_bwd(sv["qkv"], sv["gcol"], sv["grow"], bcol, sv["states"], sv["parts"], do)
        dtail, dal, ddt = dn_gates_bwd(proj, _vec8(sm["dn_a_log"]), _vec8(sm["dn_dt_bias"]), heads_to_gates(dgc, dgr, dbc))
        dmain, dconv = dn_prep_bwd(proj, wt["conv_w"], dqkv)
        small["dn_a_log"], small["dn_dt_bias"] = dal[:, :N_HEADS], ddt[:, :N_HEADS]
        big["conv_w"] = dconv[:4]
        dproj = jnp.concatenate([dmain, dz, dqm, dtail, pad], axis=1)
    else:
        dq, dk, dv, dgate, dfc, dfr = fox_attn_bwd(sv["qk"], proj, sv["fcol"], sv["frow"], sv["o"], sv["lse"], dcat)
        dtail, dfb = fox_gates_bwd(proj, _vec8(sm["fox_f_bias"]), heads_to_fcum(dfc, dfr))
        dqk, dwqk = fox_prep_bwd(proj, sv["wqk"], dq, dk)
        small["fox_f_bias"] = dfb[:, :N_HEADS]
        small["fox_q_norm_w"], small["fox_k_norm_w"] = dwqk[0], dwqk[1]
        dproj = jnp.concatenate([dqk, dv.astype(MM), dgate, dqm, dtail, pad], axis=1)
    w_in = wt["dn_w_in"] if i == 0 else wt["fox_w_in"]
    dh = matmul(dproj, w_in, tb=True, name=tag + "d_h", tm=512)
    big["w_in"] = matmul(sv["h"], dproj, ta=True, name=tag + "d_w_in", tm=256)
    dx_in, small["norm1_w"] = rms_bwd(sv["x_in"], sm["norm1_w"][i][None], dh, dx_mid, name=tag + "rms1_bwd")
    return dx_in, big, small, (dmk, dmv)


def local_step(x, mem, target, wt, sm, late=None, on_layer1=None, on_mlp0=None, on_core0=None):
    wt = dict(wt)
    mem_k, mem_v = mem_fwd(mem, sm["mem_norm_w"][None], wt["w_mem_kv"], sm["mem_k_norm_w"][None])
    x0, sv0 = _layer_fwd(0, x, wt, sm, mem_k, mem_v, late)
    x1, sv1 = _layer_fwd(1, x0, wt, sm, mem_k, mem_v)
    dy, loss = loss_fwd(x1, target, name="loss")
    dx1, big1, small1, dm1 = _layer_bwd(1, dy, sv1, wt, sm, mem_k, mem_v)
    if on_layer1 is not None:
        dx1 = _after(dx1, on_layer1(big1))
    dx0, big0, small0, dm0 = _layer_bwd(0, dx1, sv0, wt, sm, mem_k, mem_v, on_mlp0, on_core0)
    dwn, dwkv, dwkn = mem_bwd(mem, sm["mem_norm_w"][None], wt["w_mem_kv"], sm["mem_k_norm_w"][None], *dm0, *dm1)
    small = dict(mem_norm_w=dwn[0], mem_k_norm_w=dwkn[0],
                 norm1_w=jnp.concatenate([small0["norm1_w"], small1["norm1_w"]]),
                 norm2_w=jnp.concatenate([small0["norm2_w"], small1["norm2_w"]]),
                 memq_norm_w=jnp.concatenate([small0["memq_norm_w"], small1["memq_norm_w"]]),
                 dn_a_log=small0["dn_a_log"], dn_dt_bias=small0["dn_dt_bias"], dn_o_norm_w=small0["dn_o_norm_w"],
                 fox_f_bias=small1["fox_f_bias"], fox_q_norm_w=small1["fox_q_norm_w"], fox_k_norm_w=small1["fox_k_norm_w"])
    big = dict(w_mem_kv=dwkv, dn_w_in=big0["w_in"], fox_w_in=big1["w_in"], conv_w=big0["conv_w"],
               w_out=[big0["w_out"], big1["w_out"]], w_mlp1=[big0["w_mlp1"], big1["w_mlp1"]],
               w_mlp2=[big0["w_mlp2"], big1["w_mlp2"]])
    return loss, dx0, big, small


def w_in_slots_to_kernel(slots, n_scalars):
    c = slots.shape[2]
    cut = 4096 - 3 * c
    pad = jnp.zeros((slots.shape[1], PROJ_W - TAIL - n_scalars), slots.dtype)
    return jnp.concatenate([slots[0], slots[1], slots[2], slots[3, :, :cut], slots[3, :, cut + n_scalars:],
                            slots[3, :, cut:cut + n_scalars], pad], axis=1)


def w_in_kernel_to_slots(w, n_scalars):
    c = (4096 + n_scalars + MEM_WIDTH) // N_CHIP
    last = jnp.concatenate([w[:, 3 * c:4096], w[:, TAIL:TAIL + n_scalars], w[:, 4096:TAIL]], axis=1)
    return jnp.stack([w[:, :c], w[:, c:2 * c], w[:, 2 * c:3 * c], last])


BIG_SPECS = dict(w_mem_kv=("rows", 1, 256, 1024), w_out=("rows", 2, 384, 1024), w_mlp2=("rows", 2, 1024, 1024),
                 w_mlp1=("cols", 2, 1024, 1024), dn_w_in=("rows", 1, 1024, 1156), fox_w_in=("rows", 1, 1024, 1154))
BIG_NAMES = tuple(BIG_SPECS)
EARLY_NAMES = ("w_mem_kv", "dn_w_in")
LATE_NAMES = ("w_out", "w_mlp2", "w_mlp1", "fox_w_in")
BIG_SPECS.update({f"{name}_{i}": (BIG_SPECS[name][0], 1) + BIG_SPECS[name][2:]
                  for name in ("w_out", "w_mlp2", "w_mlp1") for i in range(2)})
RS_LAYER1 = ("fox_w_in", "w_out_1", "w_mlp2_1", "w_mlp1_1")
RS_MLP0 = ("w_mlp2_0", "w_mlp1_0", "w_out_0")
RS_LAST = ("dn_w_in", "w_mem_kv")


def _full_shape(name, half=False):
    kind, a, b, c = BIG_SPECS[name]
    b = b // 2 if half else b
    return (a, N_CHIP, b, c) if kind == "rows" else (a, b, N_CHIP * c)


def _ds(start, size, align):
    return pl.ds(start if isinstance(start, int) else pl.multiple_of(start, align), size)


def _half_rows(name, h):
    b = BIG_SPECS[name][2]
    return _ds(h * (b // 2), b // 2, 16)


def _shard_idx(name, h):
    return (slice(None), _half_rows(name, h), slice(None))


def _full_idx(name, j=None, h=None):
    kind, _, _, c = BIG_SPECS[name]
    rows = slice(None) if h is None else _half_rows(name, h)
    if kind == "rows":
        return (slice(None), slice(None) if j is None else j, rows, slice(None))
    return (slice(None), rows, slice(None) if j is None else _ds(j * c, c, 128))


def _slots_shape(name):
    _, a, b, c = BIG_SPECS[name]
    return (a, N_CHIP, b, c)


def _slots_idx(name, j, h):
    return (slice(None), j, _half_rows(name, h), slice(None))


def _row_block(name):
    hs = BIG_SPECS[name][2] // 2
    return hs if hs <= ROWS else ROWS


def _remote(src, dst, send_sem, recv_sem, to):
    return pltpu.make_async_remote_copy(src_ref=src, dst_ref=dst, send_sem=send_sem, recv_sem=recv_sem, device_id=to,
                                        device_id_type=MESH)


HBM = pl.BlockSpec(memory_space=pltpu.HBM)
SEM = pl.BlockSpec(memory_space=pltpu.SEMAPHORE)
EFFECT = pltpu.CompilerParams(has_side_effects=pltpu.SideEffectType.DATAFLOW_SIDE_EFFECTING)


def _in_hbm(a):
    return pltpu.with_memory_space_constraint(a, pltpu.HBM)


def _chip_copies(names, ins, lands, send_sems, recv_sems):
    x, y, c, chips = _place()
    return [_remote(ins[a].at[_shard_idx(name, c)], lands[a].at[_slots_idx(name, 2 * x + y, c)], send_sems.at[3 * a + k],
                    recv_sems.at[3 * a + k], (chip[0], chip[1], c))
            for a, name in enumerate(names) for k, chip in enumerate(chips)]


def _copies_start(call_name, copies, sources, land_shapes, per_source=3, land_dtype=MM):
    n = len(sources)

    def body(*refs):
        ins, lands, send_sems, recv_sems, token = refs[:n], refs[n:2 * n], refs[2 * n], refs[2 * n + 1], refs[-1]
        for cp in copies(ins, lands, send_sems, recv_sems):
            cp.start()
        token[...] = jnp.zeros_like(token)

    ins = [_in_hbm(a) for a in sources]
    lands = [_in_hbm(lax.empty(shape, land_dtype)) for shape in land_shapes]
    sems = (pltpu.SemaphoreType.DMA((per_source * n,)), pltpu.SemaphoreType.DMA((per_source * n,)))
    outs = pl.pallas_call(
        body, name=call_name, in_specs=[HBM] * (2 * n),
        out_specs=(SEM, SEM) + (HBM,) * (2 * n) + (pl.BlockSpec(memory_space=pltpu.VMEM),),
        out_shape=sems + tuple(pltpu.HBM(a.shape, a.dtype) for a in ins + lands) + (jax.ShapeDtypeStruct((8, HEAD_DIM), F32),),
        input_output_aliases={a: 2 + a for a in range(2 * n)}, compiler_params=EFFECT,
    )(*ins, *lands)
    return outs[:-1], outs[-1]


def _copies_wait(call_name, copies, state, after):
    n = (len(state) - 2) // 2

    def body(*refs):
        send_sems, recv_sems, ins, lands = refs[0], refs[1], refs[2:2 + n], refs[2 + n:2 + 2 * n]
        for cp in copies(ins, lands, send_sems, recv_sems):
            cp.wait_send()
            cp.wait_recv()

    outs = pl.pallas_call(
        body, name=call_name, in_specs=[SEM, SEM] + [HBM] * (2 * n) + [ANY], out_specs=(HBM,) * (2 * n),
        out_shape=tuple(pltpu.HBM(a.shape, a.dtype) for a in state[2:]),
        input_output_aliases={2 + a: a for a in range(2 * n)}, compiler_params=EFFECT,
    )(*state, after)
    return outs[:n], outs[n:]


def all_gather_start(shards, names):
    return _copies_start("all_gather_start", functools.partial(_chip_copies, names), [shards[name] for name in names],
                         [_slots_shape(name) for name in names])


def all_gather_wait(state, names, after):
    ins, lands = _copies_wait("all_gather_wait", functools.partial(_chip_copies, names), state, after)
    return dict(zip(names, ins)), dict(zip(names, lands))


def _chip_sends(names, ins, lands, send_sems, recv_sems):
    x, y, c, chips = _place()
    return [_remote(ins[a].at[_full_idx(name, 2 * chip[0] + chip[1])], lands[a].at[k], send_sems.at[3 * a + k],
                    recv_sems.at[3 * a + k], (chip[0], chip[1], c))
            for a, name in enumerate(names) for k, chip in enumerate(chips)]


def _got_shape(name):
    _, a_, b_, c_ = BIG_SPECS[name]
    return (3, a_, b_ // 2, c_)


def rs_chip_start(pairs, names, tag):
    return _copies_start("rs_chip_start_" + tag, functools.partial(_chip_sends, names), [pairs[name] for name in names],
                         [_got_shape(name) for name in names])


def rs_chip_wait(state, names, tag, after):
    _, lands = _copies_wait("rs_chip_wait_" + tag, functools.partial(_chip_sends, names), state, after)
    return dict(zip(names, lands))


def all_gather_pass_on(lands, names):
    n = len(names)

    def body(*refs):
        outs, send_sems, recv_sems = refs[n:2 * n], refs[2 * n], refs[2 * n + 1]
        x, y, c, chips = _place()
        work = [(3 * a + k, a, name, 2 * chip[0] + chip[1]) for a, name in enumerate(names) for k, chip in enumerate(chips)]
        cps = []
        for s, a, name, slot in work:
            landed = outs[a].at[_slots_idx(name, slot, c)]
            cps.append(_remote(landed, landed, send_sems.at[s], recv_sems.at[s], (x, y, 1 - c)))
            cps[-1].start()
        for s, a, name, slot in work:
            passed = outs[a].at[_slots_idx(name, slot, 1 - c)]
            _remote(passed, passed, send_sems.at[s], recv_sems.at[s], (x, y, 1 - c)).wait_recv()
        for cp in cps:
            cp.wait_send()

    outs = pl.pallas_call(
        body, name="all_gather_pass_on", in_specs=[ANY] * n, out_specs=[ANY] * n,
        input_output_aliases={a: a for a in range(n)},
        out_shape=[jax.ShapeDtypeStruct(_slots_shape(name), MM) for name in names],
        scratch_shapes=[pltpu.SemaphoreType.DMA((3 * n,)), pltpu.SemaphoreType.DMA((3 * n,))],
    )(*[lands[name] for name in names])
    return dict(zip(names, outs))


def all_gather_big(shards, names):
    n = len(names)

    def body(*refs):
        ins, outs = refs[:n], refs[n:2 * n]
        send_sems, recv_sems, fsend_sems, frecv_sems = refs[2 * n:]
        x, y, c, chips = _place()
        me_chip, sibling = 2 * x + y, (x, y, 1 - c)
        work = [(3 * a + k, a, name, chip) for a, name in enumerate(names) for k, chip in enumerate(chips)]
        sends = []
        for s, a, name, chip in work:
            cp = _remote(ins[a].at[_shard_idx(name, c)], outs[a].at[_slots_idx(name, me_chip, c)], send_sems.at[s],
                         recv_sems.at[s], (chip[0], chip[1], c))
            cp.start()
            sends.append(cp)
        for s, a, name, chip in work:
            landed = outs[a].at[_slots_idx(name, 2 * chip[0] + chip[1], c)]
            _remote(landed, landed, send_sems.at[s], recv_sems.at[s], (chip[0], chip[1], c)).wait_recv()
            cp = _remote(landed, landed, fsend_sems.at[s], frecv_sems.at[s], sibling)
            cp.start()
            sends.append(cp)
        for s, a, name, chip in work:
            passed = outs[a].at[_slots_idx(name, 2 * chip[0] + chip[1], 1 - c)]
            _remote(passed, passed, fsend_sems.at[s], frecv_sems.at[s], sibling).wait_recv()
        for cp in sends:
            cp.wait_send()

    outs = pl.pallas_call(
        body, name="all_gather_big", in_specs=[ANY] * n, out_specs=[ANY] * n,
        out_shape=[jax.ShapeDtypeStruct(_slots_shape(name), MM) for name in names],
        scratch_shapes=[pltpu.SemaphoreType.DMA((3 * n,))] * 4,
    )(*[shards[name] for name in names])
    return dict(zip(names, outs))


def with_own_slot(name, full, shard, chip):
    return lax.dynamic_update_slice(full, shard[:, None], (0, chip, 0, 0))


def rs_pair_exchange_big(grads, names, tag):
    n = len(names)

    def body(*refs):
        ins, outs, send_sems, recv_sems = refs[:n], refs[n:2 * n], refs[2 * n], refs[2 * n + 1]
        x, y, c, _ = _place()
        cps = []
        for a, name in enumerate(names):
            cp = _remote(ins[a].at[_full_idx(name, None, 1 - c)], outs[a], send_sems.at[a], recv_sems.at[a], (x, y, 1 - c))
            cp.start()
            cps.append(cp)
        for cp in cps:
            cp.wait()

    outs = pl.pallas_call(
        body, name="rs_pair_exchange_" + tag, in_specs=[ANY] * n, out_specs=[ANY] * n,
        out_shape=[jax.ShapeDtypeStruct(_full_shape(name, half=True), F32) for name in names],
        scratch_shapes=[pltpu.SemaphoreType.DMA((n,)), pltpu.SemaphoreType.DMA((n,))],
    )(*[grads[name] for name in names])
    return dict(zip(names, outs))


def rs_pair_add_big(name, place, g, got):
    kind, a_, b_, c_ = BIG_SPECS[name]
    rb = _row_block(name)
    nb = (b_ // 2) // rb

    def body(place_ref, g_ref, got_ref, o_ref):
        o_ref[...] = (g_ref[...] + got_ref[...]).astype(o_ref.dtype)

    if kind == "rows":
        g_spec = pl.BlockSpec((None, None, rb, c_), lambda a, j, i, p: (a, j, p[0] * nb + i, 0))
        o_spec = pl.BlockSpec((None, None, rb, c_), lambda a, j, i, p: (a, j, i, 0))
    else:
        g_spec = pl.BlockSpec((None, rb, c_), lambda a, j, i, p: (a, p[0] * nb + i, j))
        o_spec = pl.BlockSpec((None, rb, c_), lambda a, j, i, p: (a, i, j))
    return pl.pallas_call(
        body, name="rs_pair_add_" + name,
        grid_spec=pltpu.PrefetchScalarGridSpec(num_scalar_prefetch=1, grid=(a_, N_CHIP, nb), in_specs=[g_spec, o_spec],
                                               out_specs=o_spec),
        out_shape=jax.ShapeDtypeStruct(_full_shape(name, half=True), MM),
        compiler_params=_cparams(("parallel", "parallel", "parallel")),
    )(place, g, got)


def rs_chip_add_big(name, place, g, got_pair, got_chips):
    kind, a_, b_, c_ = BIG_SPECS[name]
    rb = _row_block(name)
    nb = (b_ // 2) // rb

    def body(place_ref, g_ref, s_ref, r0_ref, r1_ref, r2_ref, o_ref):
        own = g_ref[...] + s_ref[...]
        o_ref[...] = ((own + r0_ref[...].astype(F32)) + r1_ref[...].astype(F32)) + r2_ref[...].astype(F32)

    if kind == "rows":
        g_spec = pl.BlockSpec((None, None, rb, c_), lambda a, i, p: (a, p[1], p[0] * nb + i, 0))
        s_spec = pl.BlockSpec((None, None, rb, c_), lambda a, i, p: (a, p[1], i, 0))
    else:
        g_spec = pl.BlockSpec((None, rb, c_), lambda a, i, p: (a, p[0] * nb + i, p[1]))
        s_spec = pl.BlockSpec((None, rb, c_), lambda a, i, p: (a, i, p[1]))
    r_spec = lambda k: pl.BlockSpec((None, None, rb, c_), lambda a, i, p: (k, a, i, 0))
    return pl.pallas_call(
        body, name="rs_chip_add_" + name,
        grid_spec=pltpu.PrefetchScalarGridSpec(
            num_scalar_prefetch=1, grid=(a_, nb), in_specs=[g_spec, s_spec, r_spec(0), r_spec(1), r_spec(2)],
            out_specs=pl.BlockSpec((None, rb, c_), lambda a, i, p: (a, p[0] * nb + i, 0))),
        out_shape=jax.ShapeDtypeStruct((a_, b_, c_), F32), compiler_params=_cparams(("parallel", "parallel")),
    )(place, g, got_pair, got_chips, got_chips, got_chips)


def rs_pair_gather_big(halves, tag):
    names = tuple(halves)
    n = len(names)

    def body(*refs):
        outs, send_sems, recv_sems = refs[n:2 * n], refs[2 * n], refs[2 * n + 1]
        x, y, c, _ = _place()
        cps = []
        for a, name in enumerate(names):
            mine = outs[a].at[_shard_idx(name, c)]
            cp = _remote(mine, mine, send_sems.at[a], recv_sems.at[a], (x, y, 1 - c))
            cp.start()
            cps.append(cp)
        for a, name in enumerate(names):
            cps[a].wait_send()
            theirs = outs[a].at[_shard_idx(name, 1 - c)]
            _remote(theirs, theirs, send_sems.at[a], recv_sems.at[a], (x, y, 1 - c)).wait_recv()

    outs = pl.pallas_call(
        body, name="rs_pair_gather_" + tag, in_specs=[ANY] * n, out_specs=[ANY] * n,
        input_output_aliases={a: a for a in range(n)},
        out_shape=[jax.ShapeDtypeStruct(BIG_SPECS[name][1:], F32) for name in names],
        scratch_shapes=[pltpu.SemaphoreType.DMA((n,)), pltpu.SemaphoreType.DMA((n,))],
    )(*[halves[name] for name in names])
    return dict(zip(names, outs))


def _pair_sends(names, ins, lands, send_sems, recv_sems):
    x, y, c, _ = _place()
    return [_remote(ins[a].at[_full_idx(name, None, 1 - c)], lands[a], send_sems.at[a], recv_sems.at[a], (x, y, 1 - c))
            for a, name in enumerate(names)]


def rs_pair_start(grads, names, tag):
    return _copies_start("rs_pair_start_" + tag, functools.partial(_pair_sends, names), [grads[name] for name in names],
                         [_full_shape(name, half=True) for name in names], per_source=1, land_dtype=F32)


def rs_middle(pair_state, names, tag, place, after):
    ins, lands = _copies_wait("rs_pair_wait_" + tag, functools.partial(_pair_sends, names), pair_state, after)
    grads, got_pair = dict(zip(names, ins)), dict(zip(names, lands))
    pairs = {name: rs_pair_add_big(name, place, grads[name], got_pair[name]) for name in names}
    state, token = rs_chip_start(pairs, names, tag)
    return (grads, got_pair, state), token


def rs_end(begun, names, tag, place, after):
    grads, got_pair, state = begun
    got_chips = rs_chip_wait(state, names, tag, after)
    return {name: rs_chip_add_big(name, place, grads[name], got_pair[name], got_chips[name]) for name in names}


PACK_W = 1024
SMALL =(("mem_norm_w", 1024), ("mem_k_norm_w", 128), ("norm1_w", 2048), ("dn_a_log", 8), ("dn_dt_bias", 8),
         ("dn_o_norm_w", 128), ("fox_f_bias", 8), ("fox_q_norm_w", 128), ("fox_k_norm_w", 128), ("memq_norm_w", 256),
         ("norm2_w", 2048))
SMALL_ROWS = 8
CONV_ROWS = 4 * 3 * D_MODEL // PACK_W
LOSS_AT = sum(n for _, n in SMALL)


def pack_small(parts, extra=None):
    flat = [parts[name].astype(F32).reshape(-1) for name, _ in SMALL]
    used = LOSS_AT
    if extra is not None:
        flat.append(extra.reshape(1))
        used += 1
    flat.append(jnp.zeros((SMALL_ROWS * PACK_W - used,), F32))
    return jnp.concatenate(flat).reshape(SMALL_ROWS, PACK_W)


def unpack_small(packed, shapes):
    flat, out, at = packed.reshape(-1), {}, 0
    for name, n in SMALL:
        out[name] = flat[at:at + n].reshape(shapes[name])
        at += n
    return out


def _adam_all(w, g, m, v, name):
    shape = w.shape
    r2 = lambda a: a.reshape(-1, shape[-1])
    d, nm, nv = adamw(r2(w), r2(g), r2(m), r2(v), name=name)
    return d.reshape(shape), nm.reshape(shape), nv.reshape(shape)


WEIGHTS = ("mem_norm_w", "w_mem_kv", "mem_k_norm_w", "norm1_w", "dn_w_in", "dn_conv_w", "dn_a_log", "dn_dt_bias",
           "dn_o_norm_w", "fox_w_in", "fox_f_bias", "fox_q_norm_w", "fox_k_norm_w", "memq_norm_w", "w_out", "norm2_w",
           "w_mlp1", "w_mlp2")


def kernel(x, mem, mem_norm_w, w_mem_kv, mem_k_norm_w, norm1_w, dn_w_in, dn_conv_w, dn_a_log, dn_dt_bias, dn_o_norm_w, fox_w_in, fox_f_bias, fox_q_norm_w, fox_k_norm_w, memq_norm_w, w_out, norm2_w, w_mlp1, w_mlp2, loss_target, m_mem_norm_w, m_w_mem_kv, m_mem_k_norm_w, m_norm1_w, m_dn_w_in, m_dn_conv_w, m_dn_a_log, m_dn_dt_bias, m_dn_o_norm_w, m_fox_w_in, m_fox_f_bias, m_fox_q_norm_w, m_fox_k_norm_w, m_memq_norm_w, m_w_out, m_norm2_w, m_w_mlp1, m_w_mlp2, v_mem_norm_w, v_w_mem_kv, v_mem_k_norm_w, v_norm1_w, v_dn_w_in, v_dn_conv_w, v_dn_a_log, v_dn_dt_bias, v_dn_o_norm_w, v_fox_w_in, v_fox_f_bias, v_fox_q_norm_w, v_fox_k_norm_w, v_memq_norm_w, v_w_out, v_norm2_w, v_w_mlp1, v_w_mlp2):
    args = dict(locals())
    w = {n: args[n] for n in WEIGHTS}
    m = {n: args["m_" + n] for n in WEIGHTS}
    v = {n: args["v_" + n] for n in WEIGHTS}
    core, chip = lax.axis_index("c"), 2 * lax.axis_index("x") + lax.axis_index("y")
    place = jnp.stack([core, chip]).astype(jnp.int32)

    shards = {name: w[name].reshape(BIG_SPECS[name][1:]).astype(MM) for name in BIG_NAMES}
    w_in_full = lambda arr, n_scalars: w_in_slots_to_kernel(arr[0], n_scalars)
    early = {name: with_own_slot(name, arr, shards[name], chip)
             for name, arr in all_gather_big(shards, EARLY_NAMES).items()}
    conv_mine = jnp.where(core == 0, dn_conv_w[0], 0.0)
    conv_placed = lax.dynamic_update_slice(jnp.zeros((4, 3 * D_MODEL), F32), conv_mine, (0, 768 * chip))
    conv_full = all_reduce_small(jnp.pad(conv_placed.reshape(CONV_ROWS, PACK_W), ((0, 16 - CONV_ROWS), (0, 0))))
    late_shards, early, conv_full = lax.optimization_barrier(
        ({name: shards[name] for name in LATE_NAMES}, early, conv_full))
    late_state, token = all_gather_start(late_shards, LATE_NAMES)
    wt = dict(w_mem_kv=_after(early["w_mem_kv"].reshape(D_MODEL, 2 * MEM_WIDTH), token),
              dn_w_in=w_in_full(early["dn_w_in"], 2 * N_HEADS), conv_w=conv_full[:CONV_ROWS].reshape(4, 3 * D_MODEL))

    def late(after):
        late_shards, lands = all_gather_wait(late_state, LATE_NAMES, after)
        full = {name: with_own_slot(name, arr, late_shards[name], chip)
                for name, arr in all_gather_pass_on(lands, LATE_NAMES).items()}
        return dict(fox_w_in=w_in_full(full["fox_w_in"], N_HEADS), w_out=full["w_out"].reshape(2, 3 * MEM_WIDTH, D_MODEL),
                    w_mlp1=full["w_mlp1"], w_mlp2=full["w_mlp2"].reshape(2, D_FF, D_MODEL))

    sm = dict(mem_norm_w=mem_norm_w, mem_k_norm_w=mem_k_norm_w, norm1_w=norm1_w, norm2_w=norm2_w, memq_norm_w=memq_norm_w,
              dn_a_log=dn_a_log[0], dn_dt_bias=dn_dt_bias[0], dn_o_norm_w=dn_o_norm_w, fox_f_bias=fox_f_bias[0],
              fox_q_norm_w=fox_q_norm_w, fox_k_norm_w=fox_k_norm_w)
    w_in_slots = lambda g, n_scalars: w_in_kernel_to_slots(g, n_scalars)[None]
    rows_view = lambda g, name: g.reshape(_full_shape(name))
    pair_started, begun = {}, {}

    def on_layer1(big1):
        grads1 = dict(fox_w_in=w_in_slots(big1["w_in"], N_HEADS), w_out_1=rows_view(big1["w_out"], "w_out_1"),
                      w_mlp2_1=rows_view(big1["w_mlp2"], "w_mlp2_1"), w_mlp1_1=big1["w_mlp1"][None])
        pair_started["layer1"], token = rs_pair_start(grads1, RS_LAYER1, "layer1")
        return token

    def on_mlp0(d_w_mlp2, d_w_mlp1, d_w_out):
        begun["layer1"], token1 = rs_middle(pair_started["layer1"], RS_LAYER1, "layer1", place, d_w_out)
        grads0 = dict(w_mlp2_0=rows_view(d_w_mlp2, "w_mlp2_0"), w_mlp1_0=d_w_mlp1[None],
                      w_out_0=rows_view(d_w_out, "w_out_0"))
        pair_started["mlp0"], token0 = rs_pair_start(_after(grads0, token1), RS_MLP0, "mlp0")
        return token1, token0

    def on_core0(d_o):
        begun["mlp0"], token = rs_middle(pair_started["mlp0"], RS_MLP0, "mlp0", place, d_o)
        return token

    loss_part, dx, big, small = local_step(_after(x[0], token), mem[0], loss_target[0], wt, sm, late, on_layer1, on_mlp0,
                                           on_core0)
    small_pack = jnp.concatenate([pack_small(small, loss_part[0, :1]), big["conv_w"].reshape(CONV_ROWS, PACK_W),
                                  jnp.zeros((24 - SMALL_ROWS - CONV_ROWS, PACK_W), F32)])
    small_all = all_reduce_small(small_pack)
    small_sum = small_all[:SMALL_ROWS]
    conv_sum = lax.dynamic_slice(small_all[SMALL_ROWS:SMALL_ROWS + CONV_ROWS].reshape(4, 3 * D_MODEL), (0, 768 * chip), (4, 768))
    loss = small_sum.reshape(-1)[LOSS_AT]
    halves = rs_end(begun["layer1"], RS_LAYER1, "layer1", place, small_all)
    halves.update(rs_end(begun["mlp0"], RS_MLP0, "mlp0", place, small_all))
    summed = rs_pair_gather_big(halves, "early")

    last = dict(dn_w_in=w_in_slots(big["dn_w_in"], 2 * N_HEADS), w_mem_kv=rows_view(big["w_mem_kv"], "w_mem_kv"))
    last, summed = lax.optimization_barrier((last, summed))
    got_pair = rs_pair_exchange_big(last, RS_LAST, "last")
    pairs = {name: rs_pair_add_big(name, place, last[name], got_pair[name]) for name in RS_LAST}
    last_state, token = rs_chip_start(pairs, RS_LAST, "last")
    summed, token = lax.optimization_barrier((summed, token))

    big_sum = {"fox_w_in": summed["fox_w_in"]}
    big_sum.update({name: jnp.concatenate([summed[name + "_0"], summed[name + "_1"]]) for name in ("w_out", "w_mlp2", "w_mlp1")})
    grads = unpack_small(small_sum, {n: w[n].shape for n, _ in SMALL})
    grads.update({name: big_sum[name].reshape(w[name].shape) for name in big_sum}, dn_conv_w=conv_sum[None])
    delta, new_m, new_v = {}, {}, {}
    for n in ("fox_w_in", "w_out", "w_mlp1", "w_mlp2", "dn_conv_w"):
        delta[n], new_m[n], new_v[n] = _adam_all(w[n], grads[n], m[n], v[n], "adamw_" + n)

    got_chips = rs_chip_wait(last_state, RS_LAST, "last", delta["w_mlp2"])
    summed_last = rs_pair_gather_big({name: rs_chip_add_big(name, place, last[name], got_pair[name], got_chips[name])
                                      for name in RS_LAST}, "last")
    for n in RS_LAST:
        grads[n] = summed_last[n].reshape(w[n].shape)
        delta[n], new_m[n], new_v[n] = _adam_all(w[n], grads[n], m[n], v[n], "adamw_" + n)
    shapes = {n: w[n].shape for n, _ in SMALL}
    d_s, m_s, v_s = adamw(pack_small(w), small_sum, pack_small(m), pack_small(v), name="adamw_small")
    for out, packed in ((delta, d_s), (new_m, m_s), (new_v, v_s)):
        out.update(unpack_small(packed, shapes))
    return (loss, dx[None], *[grads[n] for n in WEIGHTS], *[delta[n] for n in WEIGHTS],
            *[new_m[n] for n in WEIGHTS], *[new_v[n] for n in WEIGHTS])
```

```python
import functools

import jax
import jax.numpy as jnp
from jax import lax
from jax.experimental import pallas as pl
from jax.experimental.pallas import tpu as pltpu

F32 = jnp.float32
MM = jnp.bfloat16
HI = lax.Precision.HIGHEST

D_MODEL = 1024
HEAD_DIM = 128
N_HEADS = 8
MEM_HEADS = 4
MEM_WIDTH = MEM_HEADS * HEAD_DIM
N_MEM = 256
D_FF = 4 * D_MODEL
CHUNK = 64
EPS = 1e-6
QSCALE = HEAD_DIM ** -0.5
PROJ_W = 4736
TAIL = 4608
TAIL_BLK = TAIL // HEAD_DIM
ROWS = 256
VMEM_LIMIT = 56 * 1024 * 1024

ADAM_LR = 0.001
ADAM_B1 = 0.9
ADAM_B2 = 0.999
ADAM_EPS = 1e-08
ADAM_WD = 0.01
ADAM_STEP = 10

N_DEV = 8
N_CHIP = 4
MESH = pl.DeviceIdType.MESH


def _cparams(sem=None):
    return pltpu.CompilerParams(dimension_semantics=sem, vmem_limit_bytes=VMEM_LIMIT)


def _dot(a, b, ca, cb, hi):
    dims = (((ca,), (cb,)), ((), ()))
    if hi:
        return lax.dot_general(a, b, dims, precision=HI, preferred_element_type=F32)
    return lax.dot_general(a.astype(MM), b.astype(MM), dims, preferred_element_type=F32)


@functools.partial(jax.custom_vjp, nondiff_argnums=(2, 3, 4))
def mmul(a, b, ca, cb, hi):
    return _dot(a, b, ca, cb, hi)


def _mmul_fwd(a, b, ca, cb, hi):
    return _dot(a, b, ca, cb, hi), (a, b)


def _mmul_bwd(ca, cb, hi, res, g):
    a, b = res
    if ca == 1:
        da = _dot(g, b, 1, 1, hi) if cb == 0 else _dot(g, b, 1, 0, hi)
    else:
        da = _dot(b, g, 1, 1, hi) if cb == 0 else _dot(b, g, 0, 1, hi)
    if cb == 0:
        db = _dot(a, g, 0, 0, hi) if ca == 1 else _dot(a, g, 1, 0, hi)
    else:
        db = _dot(g, a, 0, 0, hi) if ca == 1 else _dot(g, a, 0, 1, hi)
    return da.astype(a.dtype), db.astype(b.dtype)


mmul.defvjp(_mmul_fwd, _mmul_bwd)


def _iota2(n, m):
    return lax.broadcasted_iota(jnp.int32, (n, m), 0), lax.broadcasted_iota(jnp.int32, (n, m), 1)


def _same_block(r, c, shift):
    return lax.shift_right_logical(r, shift) == lax.shift_right_logical(c, shift)


def _split_bf16(x):
    hi = x.astype(jnp.bfloat16)
    return hi, (x - hi.astype(F32)).astype(jnp.bfloat16)


def _dot3(a, b, ca, cb):
    dims = (((ca,), (cb,)), ((), ()))
    (ah, al), (bh, bl) = _split_bf16(a), _split_bf16(b)
    d = lambda x, y: lax.dot_general(x, y, dims, preferred_element_type=F32)
    return d(ah, bh) + (d(ah, bl) + d(al, bh))


def _tri_inv_impl(a):
    n = a.shape[0]
    r, c = _iota2(n, n)
    eye = (r == c).astype(F32)
    b16, b32 = _same_block(r, c, 4), _same_block(r, c, 5)
    a0 = jnp.where(b16, a, 0.0)
    p = eye - a0
    b = _dot3(a0, a0, 1, 0)
    p = p + _dot3(p, b, 1, 0)
    b = _dot3(b, b, 1, 0)
    p = p + _dot3(p, b, 1, 0)
    b = _dot3(b, b, 1, 0)
    p = p + _dot3(p, b, 1, 0)
    a1 = jnp.where(jnp.logical_and(b32, jnp.logical_not(b16)), a, 0.0)
    p = p - _dot3(_dot3(p, a1, 1, 0), p, 1, 0)
    a2 = jnp.where(b32, 0.0, a)
    p = p - _dot3(_dot3(p, a2, 1, 0), p, 1, 0)
    return p


@jax.custom_vjp
def tri_inv(a):
    return _tri_inv_impl(a)


def _tri_inv_fwd(a):
    p = _tri_inv_impl(a)
    return p, p


def _tri_inv_bwd(p, g):
    return (-_dot3(_dot3(p, g, 0, 0), p, 1, 1),)


tri_inv.defvjp(_tri_inv_fwd, _tri_inv_bwd)


def _sigmoid(x):
    return 1.0 / (1.0 + jnp.exp(-x))


def _softplus(x):
    return jnp.maximum(x, 0.0) + jnp.log(1.0 + jnp.exp(-jnp.abs(x)))


def _silu(x):
    return x * _sigmoid(x)


def _rms(x, w):
    return x * lax.rsqrt(jnp.mean(x * x, axis=-1, keepdims=True) + EPS) * w


def _bf_round(x):
    return x.astype(MM).astype(F32)


def _acc(ref, val, first):
    @pl.when(first)
    def _():
        ref[...] = val

    @pl.when(jnp.logical_not(first))
    def _():
        ref[...] += val


def _tile(n, pref):
    if n % pref == 0:
        return pref
    return n


def matmul(a, b, *, ta=False, tb=False, b_slots=False, b_layer=None, res=None, also_sqrelu=False, times_dsqrelu=None,
           out_dtype=F32, name, tm=1024, tn=1024, tk=1024):
    m, k = (a.shape[1], a.shape[0]) if ta else a.shape
    b_shape = b.shape if b_layer is None else b.shape[1:]
    if b_slots:
        n = b_shape[1] if tb else N_CHIP * b_shape[2]
        assert (N_CHIP * b_shape[2] if tb else b_shape[1]) == k, (a.shape, b.shape, ta, tb)
        tn, tk = (tn, b_shape[2]) if tb else (b_shape[2], tk)
    else:
        n = b_shape[0] if tb else b_shape[1]
        assert (b_shape[1] if tb else b_shape[0]) == k, (a.shape, b.shape, ta, tb)
    tm, tn, tk = _tile(m, tm), _tile(n, tn), _tile(k, tk)
    nk = k // tk
    ca, cb = (0 if ta else 1), (1 if tb else 0)

    extra = tuple(e for e in (res, times_dsqrelu) if e is not None)
    assert len(extra) <= 1

    def body(a_ref, b_ref, *rest):
        e_ref = rest[0] if extra else None
        o_ref = rest[len(extra)]

        def finish(total):
            if res is not None:
                total = total + e_ref[...]
            if times_dsqrelu is not None:
                total = total * (2.0 * jnp.maximum(e_ref[...], 0.0))
            o_ref[...] = total.astype(o_ref.dtype)
            if also_sqrelu:
                rest[len(extra) + 1][...] = _sqrelu(total).astype(MM)

        if nk == 1:
            finish(_dot(a_ref[...], b_ref[...], ca, cb, False))
            return
        acc_ref, kk = rest[-1], pl.program_id(2)

        @pl.when(kk == 0)
        def _():
            acc_ref[...] = jnp.zeros_like(acc_ref)

        acc_ref[...] += _dot(a_ref[...], b_ref[...], ca, cb, False)

        @pl.when(kk == nk - 1)
        def _():
            finish(acc_ref[...])

    a_spec = pl.BlockSpec((tk, tm), lambda i, j, l: (l, i)) if ta else pl.BlockSpec((tm, tk), lambda i, j, l: (i, l))
    lead = () if b_layer is None else (b_layer,)
    if b_slots:
        b_block, b_index = ((None, tn, tk), lambda i, j, l: (l, j, 0)) if tb else ((None, tk, tn), lambda i, j, l: (j, l, 0))
    else:
        b_block, b_index = ((tn, tk), lambda i, j, l: (j, l)) if tb else ((tk, tn), lambda i, j, l: (l, j))
    b_spec = pl.BlockSpec((None,) * len(lead) + b_block, lambda i, j, l: lead + b_index(i, j, l))
    o_spec = pl.BlockSpec((tm, tn), lambda i, j, l: (i, j))
    out_shape = [jax.ShapeDtypeStruct((m, n), out_dtype)] + [jax.ShapeDtypeStruct((m, n), MM)] * also_sqrelu
    outs = pl.pallas_call(
        body, name=name, grid=(m // tm, n // tn, nk),
        in_specs=[a_spec, b_spec] + [o_spec] * len(extra), out_specs=[o_spec] * len(out_shape), out_shape=out_shape,
        scratch_shapes=[pltpu.VMEM((tm, tn), F32)] * (nk > 1),
        compiler_params=_cparams(("parallel", "parallel", "arbitrary")),
    )(a, b, *extra)
    return outs if also_sqrelu else outs[0]


def rms_fwd(x, w, *, name):
    t, d = x.shape

    def body(x_ref, w_ref, o_ref):
        o_ref[...] = _rms(x_ref[...], w_ref[...]).astype(o_ref.dtype)

    return pl.pallas_call(
        body, name=name, grid=(t // ROWS,),
        in_specs=[pl.BlockSpec((ROWS, d), lambda i: (i, 0)), pl.BlockSpec((1, d), lambda i: (0, 0))],
        out_specs=pl.BlockSpec((ROWS, d), lambda i: (i, 0)),
        out_shape=jax.ShapeDtypeStruct((t, d), MM), compiler_params=_cparams(("parallel",)),
    )(x, w)


def rms_bwd(x, w, dh, dres, *, name):
    t, d = x.shape

    def body(x_ref, w_ref, dh_ref, dr_ref, dx_ref, dw_ref):
        _, vjp = jax.vjp(_rms, x_ref[...], w_ref[...])
        dx, dw = vjp(dh_ref[...].astype(F32))
        dx_ref[...] = dx + dr_ref[...]
        _acc(dw_ref, dw, pl.program_id(0) == 0)

    row = pl.BlockSpec((ROWS, d), lambda i: (i, 0))
    vec = pl.BlockSpec((1, d), lambda i: (0, 0))
    return pl.pallas_call(
        body, name=name, grid=(t // ROWS,), in_specs=[row, vec, row, row], out_specs=[row, vec],
        out_shape=[jax.ShapeDtypeStruct((t, d), F32), jax.ShapeDtypeStruct((1, d), F32)],
        compiler_params=_cparams(("arbitrary",)),
    )(x, w, dh, dres)


def _sqrelu(x):
    return jnp.square(jnp.maximum(x, 0.0))


def loss_fwd(y, target, *, name):
    t, d = y.shape

    def body(y_ref, t_ref, dy_ref, l_ref):
        e = y_ref[...] - t_ref[...]
        dy_ref[...] = e * (1.0 / d)
        part = 0.5 * jnp.sum(jnp.sum(e * e, axis=-1, keepdims=True) * (1.0 / d), axis=0, keepdims=True)
        _acc(l_ref, jnp.broadcast_to(part, (1, HEAD_DIM)), pl.program_id(0) == 0)

    blk = pl.BlockSpec((ROWS, d), lambda i: (i, 0))
    return pl.pallas_call(
        body, name=name, grid=(t // ROWS,), in_specs=[blk, blk],
        out_specs=[blk, pl.BlockSpec((1, HEAD_DIM), lambda i: (0, 0))],
        out_shape=[jax.ShapeDtypeStruct((t, d), F32), jax.ShapeDtypeStruct((1, HEAD_DIM), F32)],
        compiler_params=_cparams(("arbitrary",)),
    )(y, target)


def _mem_kv(mem, wn, wkn, *ws):
    mn = _rms(mem, wn)
    outs = []
    for h in range(MEM_HEADS):
        outs.append(_rms(mmul(mn, ws[h], 1, 0, False), wkn))
    for h in range(MEM_HEADS):
        outs.append(mmul(mn, ws[MEM_HEADS + h], 1, 0, False))
    return tuple(outs)


def _w_cols(w_ref):
    return [w_ref[:, h * HEAD_DIM:(h + 1) * HEAD_DIM] for h in range(2 * MEM_HEADS)]


def mem_fwd(mem, wn, wkv, wkn):
    def body(mem_ref, wn_ref, w_ref, wkn_ref, k_ref, v_ref):
        outs = _mem_kv(mem_ref[...], wn_ref[...], wkn_ref[...], *_w_cols(w_ref))
        for h in range(MEM_HEADS):
            k_ref[:, h * HEAD_DIM:(h + 1) * HEAD_DIM] = outs[h]
            v_ref[:, h * HEAD_DIM:(h + 1) * HEAD_DIM] = outs[MEM_HEADS + h]

    shp = jax.ShapeDtypeStruct((mem.shape[0], MEM_WIDTH), F32)
    return pl.pallas_call(body, name="mem_fwd", out_shape=[shp, shp], compiler_params=_cparams())(mem, wn, wkv, wkn)


def mem_bwd(mem, wn, wkv, wkn, dk0, dv0, dk1, dv1):
    def body(mem_ref, wn_ref, w_ref, wkn_ref, dk0_ref, dv0_ref, dk1_ref, dv1_ref, dwn_ref, dw_ref, dwkn_ref):
        _, vjp = jax.vjp(lambda wn_, wkn_, *ws: _mem_kv(mem_ref[...], wn_, wkn_, *ws),
                         wn_ref[...], wkn_ref[...], *[w.astype(F32) for w in _w_cols(w_ref)])
        cols = lambda a, b: tuple(a[:, h * HEAD_DIM:(h + 1) * HEAD_DIM] + b[:, h * HEAD_DIM:(h + 1) * HEAD_DIM]
                                  for h in range(MEM_HEADS))
        cts = cols(dk0_ref, dk1_ref) + cols(dv0_ref, dv1_ref)
        grads = vjp(cts)
        dwn_ref[...] = grads[0]
        dwkn_ref[...] = grads[1]
        for h in range(2 * MEM_HEADS):
            dw_ref[:, h * HEAD_DIM:(h + 1) * HEAD_DIM] = grads[2 + h]

    return pl.pallas_call(
        body, name="mem_bwd",
        out_shape=[jax.ShapeDtypeStruct((1, D_MODEL), F32), jax.ShapeDtypeStruct((D_MODEL, 2 * MEM_WIDTH), F32),
                   jax.ShapeDtypeStruct((1, HEAD_DIM), F32)],
        compiler_params=_cparams(),
    )(mem, wn, wkv, wkn, dk0, dv0, dk1, dv1)


def _memattn(q, wq, mk, mv):
    qn = _rms(q, wq) * QSCALE
    s = mmul(qn, mk, 1, 1, False)
    s = s - jnp.max(s, axis=-1, keepdims=True)
    p = jnp.exp(s)
    p = p / jnp.sum(p, axis=-1, keepdims=True)
    return mmul(p, mv, 1, 0, False)


def _lanes(j):
    return slice(j * HEAD_DIM, (j + 1) * HEAD_DIM)


def _memattn_specs(t):
    qspec = pl.BlockSpec((ROWS, MEM_WIDTH), lambda i: (i, (TAIL - MEM_WIDTH) // MEM_WIDTH))
    wspec = pl.BlockSpec((1, HEAD_DIM), lambda i: (0, 0))
    mspec = pl.BlockSpec((N_MEM, MEM_WIDTH), lambda i: (0, 0))
    ospec = pl.BlockSpec((ROWS, MEM_WIDTH), lambda i: (i, 0))
    return qspec, wspec, mspec, ospec


def memattn_fwd(proj, wq, mk, mv, *, name):
    t = proj.shape[0]
    qspec, wspec, mspec, ospec = _memattn_specs(t)

    def body(q_ref, w_ref, k_ref, v_ref, o_ref):
        for h in range(MEM_HEADS):
            o_ref[:, _lanes(h)] = _memattn(q_ref[:, _lanes(h)], w_ref[...], k_ref[:, _lanes(h)],
                                           v_ref[:, _lanes(h)]).astype(o_ref.dtype)

    return pl.pallas_call(
        body, name=name, grid=(t // ROWS,), in_specs=[qspec, wspec, mspec, mspec], out_specs=ospec,
        out_shape=jax.ShapeDtypeStruct((t, MEM_WIDTH), MM), compiler_params=_cparams(("parallel",)),
    )(proj, wq, mk, mv)


def memattn_bwd(proj, wq, mk, mv, dcat, *, name):
    t = proj.shape[0]
    qspec, wspec, mspec, ospec = _memattn_specs(t)
    dospec = pl.BlockSpec((ROWS, MEM_WIDTH), lambda i: (i, D_MODEL // MEM_WIDTH))

    def body(q_ref, w_ref, k_ref, v_ref, do_ref, dq_ref, dw_ref, dk_ref, dv_ref):
        first = pl.program_id(0) == 0
        dw_sum = jnp.zeros((1, HEAD_DIM), F32)
        for h in range(MEM_HEADS):
            _, vjp = jax.vjp(_memattn, q_ref[:, _lanes(h)], w_ref[...], k_ref[:, _lanes(h)], v_ref[:, _lanes(h)])
            dq, dw, dk, dv = vjp(do_ref[:, _lanes(h)].astype(F32))
            dq_ref[:, _lanes(h)] = dq.astype(dq_ref.dtype)
            dw_sum = dw_sum + dw
            _acc(dk_ref.at[:, _lanes(h)], dk, first)
            _acc(dv_ref.at[:, _lanes(h)], dv, first)
        _acc(dw_ref, dw_sum, first)

    mshape = jax.ShapeDtypeStruct((N_MEM, MEM_WIDTH), F32)
    return pl.pallas_call(
        body, name=name, grid=(t // ROWS,), in_specs=[qspec, wspec, mspec, mspec, dospec],
        out_specs=[ospec, wspec, mspec, mspec],
        out_shape=[jax.ShapeDtypeStruct((t, MEM_WIDTH), MM), jax.ShapeDtypeStruct((1, HEAD_DIM), F32), mshape, mshape],
        compiler_params=_cparams(("arbitrary",)),
    )(proj, wq, mk, mv, dcat)


def _shift_rows(x, s, up):
    n = x.shape[0]
    r = lax.broadcasted_iota(jnp.int32, x.shape, 0)
    if up:
        return jnp.where(r < n - s, pltpu.roll(x, n - s, 0), 0.0)
    return jnp.where(r >= s, pltpu.roll(x, s, 0), 0.0)


def _conv_fwd_vals(x, w):
    xb = _bf_round(x)
    wb = _bf_round(w)
    c = xb * wb[3:4, :]
    for j in range(3):
        c = c + _shift_rows(xb, 3 - j, False) * wb[j:j + 1, :]
    return xb, wb, c


def dn_prep_fwd(proj, conv_w):
    t = proj.shape[0]

    def body(x_ref, w_ref, o_ref):
        j = pl.program_id(0)
        _, _, c = _conv_fwd_vals(x_ref[...], w_ref[...])
        s = _silu(c)
        r = lax.rsqrt(jnp.sum(s * s, axis=-1, keepdims=True) + EPS)
        scale = jnp.where(j < N_HEADS, QSCALE, 1.0)
        o_ref[...] = jnp.where(j < 2 * N_HEADS, s * r * scale, s)

    return pl.pallas_call(
        body, name="dn_prep_fwd", grid=(3 * N_HEADS,),
        in_specs=[pl.BlockSpec((t, HEAD_DIM), lambda j: (0, j)), pl.BlockSpec((4, HEAD_DIM), lambda j: (0, j))],
        out_specs=pl.BlockSpec((None, t, HEAD_DIM), lambda j: (j // N_HEADS, 0, j % N_HEADS)),
        out_shape=jax.ShapeDtypeStruct((3, t, D_MODEL), F32), compiler_params=_cparams(("parallel",)),
    )(proj, conv_w)


def dn_prep_bwd(proj, conv_w, dqkv):
    t = proj.shape[0]

    def body(x_ref, w_ref, g_ref, dx_ref, dw_ref):
        j = pl.program_id(0)
        xb, wb, c = _conv_fwd_vals(x_ref[...], w_ref[...])
        sg = _sigmoid(c)
        s = c * sg
        g = g_ref[...]
        r = lax.rsqrt(jnp.sum(s * s, axis=-1, keepdims=True) + EPS)
        scale = jnp.where(j < N_HEADS, QSCALE, 1.0)
        gn = g * scale
        ds_norm = r * gn - s * (r * r * r) * jnp.sum(gn * s, axis=-1, keepdims=True)
        ds = jnp.where(j < 2 * N_HEADS, ds_norm, g)
        dc = ds * (sg + s * (1.0 - sg))
        dx = dc * wb[3:4, :]
        rows = [jnp.sum(dc * xb, axis=0, keepdims=True)]
        for jj in range(2, -1, -1):
            sh = 3 - jj
            dx = dx + _shift_rows(dc, sh, True) * wb[jj:jj + 1, :]
            rows.insert(0, jnp.sum(dc * _shift_rows(xb, sh, False), axis=0, keepdims=True))
        dx_ref[...] = dx.astype(dx_ref.dtype)
        dw_ref[...] = jnp.concatenate(rows + [jnp.zeros((4, HEAD_DIM), F32)], axis=0)

    col = pl.BlockSpec((t, HEAD_DIM), lambda j: (0, j))
    return pl.pallas_call(
        body, name="dn_prep_bwd", grid=(3 * N_HEADS,),
        in_specs=[col, pl.BlockSpec((4, HEAD_DIM), lambda j: (0, j)),
                  pl.BlockSpec((None, t, HEAD_DIM), lambda j: (j // N_HEADS, 0, j % N_HEADS))],
        out_specs=[col, pl.BlockSpec((8, HEAD_DIM), lambda j: (0, j))],
        out_shape=[jax.ShapeDtypeStruct((t, 3 * D_MODEL), MM), jax.ShapeDtypeStruct((8, 3 * D_MODEL), F32)],
        compiler_params=_cparams(("parallel",)),
    )(proj, conv_w, dqkv)


def _tri_ones(n, upper):
    r, c = _iota2(n, n)
    return (r <= c).astype(F32) if upper else (r >= c).astype(F32)


def dn_gates_fwd(proj, a_log, dt_bias):
    t = proj.shape[0]

    def body(x_ref, al_ref, dt_ref, o_ref):
        lane = lax.broadcasted_iota(jnp.int32, (CHUNK, HEAD_DIM), 1)
        tri = _tri_ones(CHUNK, False)

        def step(c, carry):
            rows = pl.ds(pl.multiple_of(c * CHUNK, CHUNK), CHUNK)
            x = x_ref[rows, :]
            g = jnp.where(lane < N_HEADS, -jnp.exp(al_ref[...]) * _softplus(x + dt_ref[...]), 0.0)
            gc = _dot(tri, g, 1, 0, True)
            o_ref[rows, :] = jnp.where(lane < N_HEADS, gc, jnp.where(lane < 2 * N_HEADS, _sigmoid(x), 0.0))
            return carry

        lax.fori_loop(0, t // CHUNK, step, 0)

    vec = pl.BlockSpec((1, HEAD_DIM), lambda i: (0, 0))
    return pl.pallas_call(
        body, name="dn_gates_fwd", grid=(1,),
        in_specs=[pl.BlockSpec((t, HEAD_DIM), lambda i: (0, TAIL_BLK)), vec, vec],
        out_specs=pl.BlockSpec((t, HEAD_DIM), lambda i: (0, 0)),
        out_shape=jax.ShapeDtypeStruct((t, HEAD_DIM), F32), compiler_params=_cparams(("arbitrary",)),
    )(proj, a_log, dt_bias)


def dn_gates_bwd(proj, a_log, dt_bias, dgates):
    t = proj.shape[0]

    def body(x_ref, al_ref, dt_ref, g_ref, dx_ref, dal_ref, ddt_ref):
        lane = lax.broadcasted_iota(jnp.int32, (CHUNK, HEAD_DIM), 1)
        tri = _tri_ones(CHUNK, True)
        dal_ref[...] = jnp.zeros_like(dal_ref)
        ddt_ref[...] = jnp.zeros_like(ddt_ref)

        def step(c, carry):
            rows = pl.ds(pl.multiple_of(c * CHUNK, CHUNK), CHUNK)
            x = x_ref[rows, :]
            dgc = jnp.where(lane < N_HEADS, g_ref[rows, :], 0.0)
            dg = _dot(tri, dgc, 1, 0, True)
            ea = -jnp.exp(al_ref[...])
            z = x + dt_ref[...]
            da = jnp.where(lane < N_HEADS, dg * ea * _sigmoid(z), 0.0)
            gval = jnp.where(lane < N_HEADS, ea * _softplus(z), 0.0)
            beta = _sigmoid(x)
            db = jnp.where(jnp.logical_and(lane >= N_HEADS, lane < 2 * N_HEADS), g_ref[rows, :] * beta * (1.0 - beta), 0.0)
            dx_ref[rows, :] = (da + db).astype(dx_ref.dtype)
            dal_ref[...] += jnp.sum(dg * gval, axis=0, keepdims=True)
            ddt_ref[...] += jnp.sum(da, axis=0, keepdims=True)
            return carry

        lax.fori_loop(0, t // CHUNK, step, 0)

    vec = pl.BlockSpec((1, HEAD_DIM), lambda i: (0, 0))
    full = pl.BlockSpec((t, HEAD_DIM), lambda i: (0, 0))
    return pl.pallas_call(
        body, name="dn_gates_bwd", grid=(1,),
        in_specs=[pl.BlockSpec((t, HEAD_DIM), lambda i: (0, TAIL_BLK)), vec, vec, full],
        out_specs=[full, vec, vec],
        out_shape=[jax.ShapeDtypeStruct((t, HEAD_DIM), MM), jax.ShapeDtypeStruct((1, HEAD_DIM), F32),
                   jax.ShapeDtypeStruct((1, HEAD_DIM), F32)],
        compiler_params=_cparams(("arbitrary",)),
    )(proj, a_log, dt_bias, dgates)


def _dn_intra(q, k, v, gcol, grow, bcol):
    r, c = _iota2(CHUNK, CHUNK)
    causal, strict = r >= c, r > c
    decay = jnp.where(causal, jnp.exp(jnp.where(causal, gcol - grow, 0.0)), 0.0)
    kb = k * bcol
    a = jnp.where(strict, mmul(kb, k, 1, 1, False) * decay, 0.0)
    tm = tri_inv(a)
    u = mmul(tm, v * bcol, 1, 0, False)
    w = mmul(tm, kb * jnp.exp(gcol), 1, 0, False)
    qk = jnp.where(causal, mmul(q, k, 1, 1, False) * decay, 0.0)
    rr = lax.broadcasted_iota(jnp.int32, (CHUNK, 1), 0)
    g_last = jnp.sum(jnp.where(rr == CHUNK - 1, gcol, 0.0), axis=0, keepdims=True)
    return u, w, q * jnp.exp(gcol), k * jnp.exp(g_last - gcol), qk, jnp.exp(g_last)


def _dn_scan(u, w, qg, kd, qk, eg, state):
    v_new = u - mmul(w, state, 1, 0, False)
    out = mmul(qg, state, 1, 0, False) + mmul(qk, v_new, 1, 0, False)
    return out, state * eg + mmul(kd, v_new, 0, 0, False)


DN_HEADS_PER_STEP = 1
DN_GROUP = 8
DN_PARTS = ((CHUNK, HEAD_DIM),) * 4 + ((CHUNK, CHUNK), (1, 1))


def _dn_scratch(hb, nc):
    return [pltpu.VMEM((hb, nc) + shape, F32) for shape in DN_PARTS]


def _dn_part_specs(hb, nc):
    return [pl.BlockSpec((hb, nc) + shape, lambda h: (h, 0, 0, 0)) for shape in DN_PARTS]


def _dn_group(nc):
    return min(DN_GROUP, nc)


def _dn_group_args(refs, j, g, grp):
    q_ref, k_ref, v_ref, gc_ref, gr_ref, bc_ref = refs
    rows = pl.ds(pl.multiple_of(g * (grp * CHUNK), grp * CHUNK), grp * CHUNK)
    cs = pl.ds(g * grp, grp)
    split = lambda ref: ref[rows, _lanes(j)].reshape(grp, CHUNK, HEAD_DIM)
    return split(q_ref), split(k_ref), split(v_ref), gc_ref[j, cs], gr_ref[j, cs], bc_ref[j, cs]


def _dn_intra_all(refs, parts, hb, nc):
    grp = _dn_group(nc)

    def group(g, carry):
        cs = pl.ds(g * grp, grp)
        for j in range(hb):
            for part, val in zip(parts, jax.vmap(_dn_intra)(*_dn_group_args(refs, j, g, grp))):
                part[j, cs] = val
        return carry

    lax.fori_loop(0, nc // grp, group, 0)


def _dn_specs(t):
    nc, hb = t // CHUNK, DN_HEADS_PER_STEP
    head = lambda which: pl.BlockSpec((None, t, hb * HEAD_DIM), lambda h: (which, 0, h))
    flat = pl.BlockSpec((t, hb * HEAD_DIM), lambda h: (0, h))
    col = pl.BlockSpec((hb, nc, CHUNK, 1), lambda h: (h, 0, 0, 0))
    row = pl.BlockSpec((hb, nc, 1, CHUNK), lambda h: (h, 0, 0, 0))
    st = pl.BlockSpec((hb, nc, HEAD_DIM, HEAD_DIM), lambda h: (h, 0, 0, 0))
    return nc, hb, head, flat, col, row, st


def dn_core_fwd(qkv, gcol, grow, bcol):
    t = qkv.shape[1]
    nc, hb, head, flat, col, row, st = _dn_specs(t)

    def body(q_ref, k_ref, v_ref, gc_ref, gr_ref, bc_ref, o_ref, s_ref, *parts):
        _dn_intra_all((q_ref, k_ref, v_ref, gc_ref, gr_ref, bc_ref), parts, hb, nc)

        def step(c, states):
            rows = pl.ds(pl.multiple_of(c * CHUNK, CHUNK), CHUNK)
            new_states = []
            for j in range(hb):
                s_ref[j, c] = states[j]
                out, new_state = _dn_scan(*[part[j, c] for part in parts], states[j])
                o_ref[rows, _lanes(j)] = out
                new_states.append(new_state)
            return tuple(new_states)

        lax.fori_loop(0, nc, step, tuple(jnp.zeros((HEAD_DIM, HEAD_DIM), F32) for _ in range(hb)))

    outs = pl.pallas_call(
        body, name="dn_core_fwd", grid=(N_HEADS // hb,),
        in_specs=[head(0), head(1), head(2), col, row, col], out_specs=[flat, st] + _dn_part_specs(hb, nc),
        out_shape=[jax.ShapeDtypeStruct((t, D_MODEL), F32), jax.ShapeDtypeStruct((N_HEADS, nc, HEAD_DIM, HEAD_DIM), F32)]
        + [jax.ShapeDtypeStruct((N_HEADS, nc) + shape, F32) for shape in DN_PARTS],
        compiler_params=_cparams(("parallel",)),
    )(qkv, qkv, qkv, gcol, grow, bcol)
    return outs[0], outs[1], tuple(outs[2:])


def dn_core_bwd(qkv, gcol, grow, bcol, states, parts, do):
    t = qkv.shape[1]
    nc, hb, head, flat, col, row, st = _dn_specs(t)
    n_parts = len(DN_PARTS)

    def body(q_ref, k_ref, v_ref, gc_ref, gr_ref, bc_ref, s_ref, do_ref, *rest):
        parts, (dqkv_ref, dgc_ref, dgr_ref, dbc_ref), dparts = rest[:n_parts], rest[n_parts:n_parts + 4], rest[n_parts + 4:]
        refs = (q_ref, k_ref, v_ref, gc_ref, gr_ref, bc_ref)

        def step(i, dstates):
            c = nc - 1 - i
            rows = pl.ds(pl.multiple_of(c * CHUNK, CHUNK), CHUNK)
            dstates_in = []
            for j in range(hb):
                _, vjp = jax.vjp(_dn_scan, *[part[j, c] for part in parts], s_ref[j, c])
                *dvals, dstate_in = vjp((do_ref[rows, _lanes(j)], dstates[j]))
                for dpart, dval in zip(dparts, dvals):
                    dpart[j, c] = dval
                dstates_in.append(dstate_in)
            return tuple(dstates_in)

        lax.fori_loop(0, nc, step, tuple(jnp.zeros((HEAD_DIM, HEAD_DIM), F32) for _ in range(hb)))

        grp = _dn_group(nc)

        def group(g, carry):
            rows = pl.ds(pl.multiple_of(g * (grp * CHUNK), grp * CHUNK), grp * CHUNK)
            cs = pl.ds(g * grp, grp)
            for j in range(hb):
                _, vjp = jax.vjp(jax.vmap(_dn_intra), *_dn_group_args(refs, j, g, grp))
                dq, dk, dv, dgc, dgr, dbc = vjp(tuple(dpart[j, cs] for dpart in dparts))
                for which, val in enumerate((dq, dk, dv)):
                    dqkv_ref[which, rows, _lanes(j)] = val.reshape(grp * CHUNK, HEAD_DIM)
                dgc_ref[j, cs] = dgc
                dgr_ref[j, cs] = dgr
                dbc_ref[j, cs] = dbc
            return carry

        lax.fori_loop(0, nc // grp, group, 0)

    return pl.pallas_call(
        body, name="dn_core_bwd", grid=(N_HEADS // hb,), scratch_shapes=_dn_scratch(hb, nc),
        in_specs=[head(0), head(1), head(2), col, row, col, st, flat] + _dn_part_specs(hb, nc),
        out_specs=[pl.BlockSpec((3, t, hb * HEAD_DIM), lambda h: (0, 0, h)), col, row, col],
        out_shape=[jax.ShapeDtypeStruct((3, t, D_MODEL), F32)] + [
            jax.ShapeDtypeStruct((N_HEADS, nc, CHUNK, 1), F32), jax.ShapeDtypeStruct((N_HEADS, nc, 1, CHUNK), F32),
            jax.ShapeDtypeStruct((N_HEADS, nc, CHUNK, 1), F32)],
        compiler_params=_cparams(("parallel",)),
    )(qkv, qkv, qkv, gcol, grow, bcol, states, do, *parts)


def gates_to_heads(gates):
    t = gates.shape[0]
    nc = t // CHUNK
    g = gates[:, :N_HEADS].T.reshape(N_HEADS, nc, CHUNK)
    b = gates[:, N_HEADS:2 * N_HEADS].T.reshape(N_HEADS, nc, CHUNK)
    return g[..., None], g[:, :, None, :], b[..., None]


def heads_to_gates(dgcol, dgrow, dbcol):
    nh, nc = dgcol.shape[:2]
    dg = (dgcol[..., 0] + dgrow[:, :, 0, :]).reshape(nh, nc * CHUNK).T
    db = dbcol[..., 0].reshape(nh, nc * CHUNK).T
    return jnp.concatenate([dg, db, jnp.zeros((nc * CHUNK, HEAD_DIM - 2 * nh), F32)], axis=1)


def _dn_out(o, z, w):
    return _rms(o, w) * _silu(z)


def _gate_specs():
    o_spec = pl.BlockSpec((ROWS, D_MODEL), lambda i: (i, 0))
    z_spec = pl.BlockSpec((ROWS, D_MODEL), lambda i: (i, 3))
    w_spec = pl.BlockSpec((1, HEAD_DIM), lambda i: (0, 0))
    return o_spec, z_spec, w_spec


def dn_out_fwd(o, proj, w):
    t = o.shape[0]
    o_spec, z_spec, w_spec = _gate_specs()

    def body(o_ref, z_ref, w_ref, y_ref):
        for h in range(N_HEADS):
            y_ref[:, _lanes(h)] = _dn_out(o_ref[:, _lanes(h)], z_ref[:, _lanes(h)], w_ref[...]).astype(y_ref.dtype)

    return pl.pallas_call(
        body, name="dn_out_fwd", grid=(t // ROWS,), in_specs=[o_spec, z_spec, w_spec], out_specs=o_spec,
        out_shape=jax.ShapeDtypeStruct((t, D_MODEL), MM), compiler_params=_cparams(("parallel",)),
    )(o, proj, w)


def dn_out_bwd(o, proj, w, dcat):
    t = o.shape[0]
    o_spec, z_spec, w_spec = _gate_specs()

    def body(o_ref, z_ref, w_ref, g_ref, do_ref, dz_ref, dw_ref):
        dw_sum = jnp.zeros((1, HEAD_DIM), F32)
        for h in range(N_HEADS):
            _, vjp = jax.vjp(_dn_out, o_ref[:, _lanes(h)], z_ref[:, _lanes(h)], w_ref[...])
            do, dz, dw = vjp(g_ref[:, _lanes(h)].astype(F32))
            do_ref[:, _lanes(h)] = do
            dz_ref[:, _lanes(h)] = dz.astype(dz_ref.dtype)
            dw_sum = dw_sum + dw
        _acc(dw_ref, dw_sum, pl.program_id(0) == 0)

    return pl.pallas_call(
        body, name="dn_out_bwd", grid=(t // ROWS,), in_specs=[o_spec, z_spec, w_spec, o_spec],
        out_specs=[o_spec, o_spec, w_spec],
        out_shape=[jax.ShapeDtypeStruct((t, D_MODEL), F32), jax.ShapeDtypeStruct((t, D_MODEL), MM),
                   jax.ShapeDtypeStruct((1, HEAD_DIM), F32)],
        compiler_params=_cparams(("arbitrary",)),
    )(o, proj, w, dcat)


def _fox_norm(x, w, scale):
    return _rms(x, w) * scale


def _fox_prep_specs():
    x_spec = pl.BlockSpec((ROWS, 2 * D_MODEL), lambda i: (i, 0))
    w_spec = pl.BlockSpec((2, 1, HEAD_DIM), lambda i: (0, 0, 0))
    y_spec = pl.BlockSpec((2, ROWS, D_MODEL), lambda i: (0, i, 0))
    return x_spec, w_spec, y_spec


def fox_prep_fwd(proj, wqk):
    t = proj.shape[0]
    x_spec, w_spec, y_spec = _fox_prep_specs()

    def body(x_ref, w_ref, y_ref):
        for j in range(2 * N_HEADS):
            which, scale = j // N_HEADS, (QSCALE if j < N_HEADS else 1.0)
            y_ref[which, :, _lanes(j % N_HEADS)] = _fox_norm(x_ref[:, _lanes(j)], w_ref[which], scale).astype(y_ref.dtype)

    return pl.pallas_call(
        body, name="fox_prep_fwd", grid=(t // ROWS,), in_specs=[x_spec, w_spec], out_specs=y_spec,
        out_shape=jax.ShapeDtypeStruct((2, t, D_MODEL), MM), compiler_params=_cparams(("parallel",)),
    )(proj, wqk)


def fox_prep_bwd(proj, wqk, dq, dk):
    t = proj.shape[0]
    x_spec, w_spec, _ = _fox_prep_specs()
    g_spec = pl.BlockSpec((ROWS, D_MODEL), lambda i: (i, 0))

    def body(x_ref, w_ref, dq_ref, dk_ref, dx_ref, dw_ref):
        dws = [jnp.zeros((1, HEAD_DIM), F32), jnp.zeros((1, HEAD_DIM), F32)]
        for j in range(2 * N_HEADS):
            which, scale = j // N_HEADS, (QSCALE if j < N_HEADS else 1.0)
            g_ref = dq_ref if which == 0 else dk_ref
            _, vjp = jax.vjp(lambda x, w: _fox_norm(x, w, scale), x_ref[:, _lanes(j)], w_ref[which])
            dx, dw = vjp(g_ref[:, _lanes(j % N_HEADS)])
            dx_ref[:, _lanes(j)] = dx.astype(dx_ref.dtype)
            dws[which] = dws[which] + dw
        first = pl.program_id(0) == 0
        _acc(dw_ref.at[0], dws[0], first)
        _acc(dw_ref.at[1], dws[1], first)

    return pl.pallas_call(
        body, name="fox_prep_bwd", grid=(t // ROWS,), in_specs=[x_spec, w_spec, g_spec, g_spec],
        out_specs=[x_spec, w_spec],
        out_shape=[jax.ShapeDtypeStruct((t, 2 * D_MODEL), MM), jax.ShapeDtypeStruct((2, 1, HEAD_DIM), F32)],
        compiler_params=_cparams(("arbitrary",)),
    )(proj, wqk, dq, dk)


def _row_pick(x, i):
    r = lax.broadcasted_iota(jnp.int32, x.shape, 0)
    return jnp.sum(jnp.where(r == i, x, 0.0), axis=0, keepdims=True)


def fox_gates_fwd(proj, f_bias):
    t = proj.shape[0]
    blk = HEAD_DIM

    def body(x_ref, b_ref, o_ref):
        lane = lax.broadcasted_iota(jnp.int32, (blk, HEAD_DIM), 1)
        tri = _tri_ones(blk, False)

        def step(c, carry):
            rows = pl.ds(pl.multiple_of(c * blk, blk), blk)
            lf = jnp.where(lane < N_HEADS, -_softplus(-(x_ref[rows, :] + b_ref[...])), 0.0)
            cum = _dot(tri, lf, 1, 0, True) + carry
            o_ref[rows, :] = cum
            return _row_pick(cum, blk - 1)

        lax.fori_loop(0, t // blk, step, jnp.zeros((1, HEAD_DIM), F32))

    vec = pl.BlockSpec((1, HEAD_DIM), lambda i: (0, 0))
    return pl.pallas_call(
        body, name="fox_gates_fwd", grid=(1,),
        in_specs=[pl.BlockSpec((t, HEAD_DIM), lambda i: (0, TAIL_BLK)), vec],
        out_specs=pl.BlockSpec((t, HEAD_DIM), lambda i: (0, 0)),
        out_shape=jax.ShapeDtypeStruct((t, HEAD_DIM), F32), compiler_params=_cparams(("arbitrary",)),
    )(proj, f_bias)


def fox_gates_bwd(proj, f_bias, dfcum):
    t = proj.shape[0]
    blk = HEAD_DIM
    nb = t // blk

    def body(x_ref, b_ref, g_ref, dx_ref, db_ref):
        lane = lax.broadcasted_iota(jnp.int32, (blk, HEAD_DIM), 1)
        tri = _tri_ones(blk, True)
        db_ref[...] = jnp.zeros_like(db_ref)

        def step(i, carry):
            c = nb - 1 - i
            rows = pl.ds(pl.multiple_of(c * blk, blk), blk)
            g = jnp.where(lane < N_HEADS, g_ref[rows, :], 0.0)
            dlf = _dot(tri, g, 1, 0, True) + carry
            dx = jnp.where(lane < N_HEADS, dlf * _sigmoid(-(x_ref[rows, :] + b_ref[...])), 0.0)
            dx_ref[rows, :] = dx.astype(dx_ref.dtype)
            db_ref[...] += jnp.sum(dx, axis=0, keepdims=True)
            return carry + jnp.sum(g, axis=0, keepdims=True)

        lax.fori_loop(0, nb, step, jnp.zeros((1, HEAD_DIM), F32))

    vec = pl.BlockSpec((1, HEAD_DIM), lambda i: (0, 0))
    full = pl.BlockSpec((t, HEAD_DIM), lambda i: (0, 0))
    return pl.pallas_call(
        body, name="fox_gates_bwd", grid=(1,),
        in_specs=[pl.BlockSpec((t, HEAD_DIM), lambda i: (0, TAIL_BLK)), vec, full], out_specs=[full, vec],
        out_shape=[jax.ShapeDtypeStruct((t, HEAD_DIM), MM), jax.ShapeDtypeStruct((1, HEAD_DIM), F32)],
        compiler_params=_cparams(("arbitrary",)),
    )(proj, f_bias, dfcum)


def fcum_to_heads(fcum):
    f = fcum[:, :N_HEADS].T
    return f[:, :, None], f[:, None, :]


def heads_to_fcum(dfcol, dfrow):
    d = (dfcol[:, :, 0] + dfrow[:, 0, :]).T
    return jnp.concatenate([d, jnp.zeros((d.shape[0], HEAD_DIM - N_HEADS), F32)], axis=1)


def _fox_tq(t):
    return min(t, 256)


def _fox_specs(t):
    tq = _fox_tq(t)
    q_spec = pl.BlockSpec((None, tq, HEAD_DIM), lambda h, i: (0, i, h))
    k_spec = pl.BlockSpec((None, t, HEAD_DIM), lambda h, i: (1, 0, h))
    v_spec = pl.BlockSpec((t, HEAD_DIM), lambda h, i: (0, 2 * N_HEADS + h))
    gate_spec = pl.BlockSpec((tq, HEAD_DIM), lambda h, i: (i, 3 * N_HEADS + h))
    col_spec = pl.BlockSpec((None, tq, 1), lambda h, i: (h, i, 0))
    row_spec = pl.BlockSpec((None, 1, t), lambda h, i: (h, 0, 0))
    blk_spec = pl.BlockSpec((tq, HEAD_DIM), lambda h, i: (i, h))
    head_spec = pl.BlockSpec((t, HEAD_DIM), lambda h, i: (0, h))
    return tq, q_spec, k_spec, v_spec, gate_spec, col_spec, row_spec, blk_spec, head_spec


def _fox_segments(i, tq):
    return ([(0, i * tq, False)] if i else []) + [(i * tq, (i + 1) * tq, True)]


def _fox_scores(q_ref, k_ref, fc_ref, fr_ref, lo, hi, causal):
    s = _dot(q_ref[...], k_ref[lo:hi, :], 1, 1, False) + (fc_ref[...] - fr_ref[:, lo:hi])
    if not causal:
        return s, None
    r, c = _iota2(hi - lo, hi - lo)
    return s, c <= r


def fox_attn_fwd(qk, proj, fcol, frow):
    t = proj.shape[0]
    tq, q_spec, k_spec, v_spec, gate_spec, col_spec, row_spec, blk_spec, _ = _fox_specs(t)

    def body(q_ref, k_ref, v_ref, gate_ref, fc_ref, fr_ref, mix_ref, o_ref, lse_ref):
        def block(i):
            segs = _fox_segments(i, tq)
            scores = [_fox_scores(q_ref, k_ref, fc_ref, fr_ref, *seg) for seg in segs]
            scores = [(s if mask is None else jnp.where(mask, s, -1e30), mask) for s, mask in scores]
            m = functools.reduce(jnp.maximum, [jnp.max(s, axis=-1, keepdims=True) for s, _ in scores])
            l, o = 0.0, 0.0
            for (lo, hi, _), (s, mask) in zip(segs, scores):
                p = jnp.exp(s - m)
                p = p if mask is None else jnp.where(mask, p, 0.0)
                l = l + jnp.sum(p, axis=-1, keepdims=True)
                o = o + _dot(p, v_ref[lo:hi, :], 1, 0, False)
            o = o / l
            o_ref[...] = o
            mix_ref[...] = (o * _sigmoid(gate_ref[...])).astype(mix_ref.dtype)
            lse_ref[...] = m + jnp.log(l)

        for i in range(t // tq):
            pl.when(pl.program_id(1) == i)(functools.partial(block, i))

    return pl.pallas_call(
        body, name="fox_attn_fwd", grid=(N_HEADS, t // tq),
        in_specs=[q_spec, k_spec, v_spec, gate_spec, col_spec, row_spec], out_specs=[blk_spec, blk_spec, col_spec],
        out_shape=[jax.ShapeDtypeStruct((t, D_MODEL), MM), jax.ShapeDtypeStruct((t, D_MODEL), F32),
                   jax.ShapeDtypeStruct((N_HEADS, t, 1), F32)],
        compiler_params=_cparams(("parallel", "parallel")),
    )(qk, qk, proj, proj, fcol, frow)


def fox_attn_bwd(qk, proj, fcol, frow, o, lse, dcat):
    t = proj.shape[0]
    tq, q_spec, k_spec, v_spec, gate_spec, col_spec, row_spec, blk_spec, head_spec = _fox_specs(t)

    def body(q_ref, k_ref, v_ref, gate_ref, fc_ref, fr_ref, o_ref, lse_ref, g_ref,
             dq_ref, dk_ref, dv_ref, dgate_ref, dfc_ref, dfr_ref):
        @pl.when(pl.program_id(1) == 0)
        def _():
            dk_ref[...] = jnp.zeros_like(dk_ref)
            dv_ref[...] = jnp.zeros_like(dv_ref)
            dfr_ref[...] = jnp.zeros_like(dfr_ref)

        def block(i):
            sg = _sigmoid(gate_ref[...])
            g = g_ref[...].astype(F32)
            o_pre = o_ref[...]
            do = g * sg
            dgate_ref[...] = (g * o_pre * sg * (1.0 - sg)).astype(dgate_ref.dtype)
            delta = jnp.sum(do * o_pre, axis=-1, keepdims=True)
            dq, dfc = 0.0, 0.0
            for lo, hi, causal in _fox_segments(i, tq):
                s, mask = _fox_scores(q_ref, k_ref, fc_ref, fr_ref, lo, hi, causal)
                if causal:
                    p = jnp.where(mask, jnp.exp(jnp.where(mask, s, 0.0) - lse_ref[...]), 0.0)
                else:
                    p = jnp.exp(s - lse_ref[...])
                ds = p * (_dot(do, v_ref[lo:hi, :], 1, 1, False) - delta)
                dq = dq + _dot(ds, k_ref[lo:hi, :], 1, 0, False)
                dk_ref[lo:hi, :] += _dot(ds, q_ref[...], 0, 0, False)
                dv_ref[lo:hi, :] += _dot(p, do, 0, 0, False)
                dfc = dfc + jnp.sum(ds, axis=-1, keepdims=True)
                dfr_ref[:, lo:hi] += -jnp.sum(ds, axis=0, keepdims=True)
            dq_ref[...] = dq
            dfc_ref[...] = dfc

        for i in range(t // tq):
            pl.when(pl.program_id(1) == i)(functools.partial(block, i))

    f32 = lambda *s: jax.ShapeDtypeStruct(s, F32)
    return pl.pallas_call(
        body, name="fox_attn_bwd", grid=(N_HEADS, t // tq),
        in_specs=[q_spec, k_spec, v_spec, gate_spec, col_spec, row_spec, blk_spec, col_spec, blk_spec],
        out_specs=[blk_spec, head_spec, head_spec, blk_spec, col_spec, row_spec],
        out_shape=[f32(t, D_MODEL), f32(t, D_MODEL), f32(t, D_MODEL), jax.ShapeDtypeStruct((t, D_MODEL), MM),
                   f32(N_HEADS, t, 1), f32(N_HEADS, 1, t)],
        compiler_params=_cparams(("parallel", "arbitrary")),
    )(qk, qk, proj, proj, fcol, frow, o, lse, dcat)


def adamw(w, g, m, v, *, name):
    r, c = w.shape
    rb = ROWS if r % ROWS == 0 else r

    def body(w_ref, g_ref, m_ref, v_ref, d_ref, nm_ref, nv_ref):
        g_ = g_ref[...]
        m_ = ADAM_B1 * m_ref[...] + (1.0 - ADAM_B1) * g_
        v_ = ADAM_B2 * v_ref[...] + (1.0 - ADAM_B2) * jnp.square(g_)
        m_hat = m_ / (1.0 - ADAM_B1 ** ADAM_STEP)
        v_hat = v_ / (1.0 - ADAM_B2 ** ADAM_STEP)
        d_ref[...] = -ADAM_LR * (m_hat / (jnp.sqrt(v_hat) + ADAM_EPS) + ADAM_WD * w_ref[...])
        nm_ref[...] = m_
        nv_ref[...] = v_

    blk = pl.BlockSpec((rb, c), lambda i: (i, 0))
    shp = jax.ShapeDtypeStruct((r, c), F32)
    return pl.pallas_call(body, name=name, grid=(r // rb,), in_specs=[blk] * 4, out_specs=[blk] * 3,
                          out_shape=[shp] * 3, compiler_params=_cparams(("parallel",)))(w, g, m, v)


def _place():
    x, y, c = lax.axis_index("x"), lax.axis_index("y"), lax.axis_index("c")
    return x, y, c, [(1 - x, y), (x, 1 - y), (1 - x, 1 - y)]


ANY = pl.BlockSpec(memory_space=pl.ANY)


def all_reduce_small(v):
    r, w = v.shape

    def body(v_ref, o_ref, buf, send_sems, recv_sems):
        x, y, c, _ = _place()
        me = 4 * x + 2 * y + c
        flip = lambda a, bit: 1 - a if bit else a
        cps = []
        for k in range(1, N_DEV):
            peer = (flip(x, k & 4), flip(y, k & 2), flip(c, k & 1))
            cp = pltpu.make_async_remote_copy(src_ref=v_ref, dst_ref=buf.at[me], send_sem=send_sems.at[k - 1],
                                              recv_sem=recv_sems.at[k - 1], device_id=peer, device_id_type=MESH)
            cp.start()
            cps.append((cp, 4 * peer[0] + 2 * peer[1] + peer[2]))
        buf[me] = v_ref[...]
        for k, (cp, peer_id) in enumerate(cps):
            pltpu.make_async_remote_copy(src_ref=v_ref, dst_ref=buf.at[peer_id], send_sem=send_sems.at[k],
                                         recv_sem=recv_sems.at[k], device_id=(x, y, c), device_id_type=MESH).wait_recv()
        for cp, _ in cps:
            cp.wait_send()
        acc = buf[0]
        for d in range(1, N_DEV):
            acc = acc + buf[d]
        o_ref[...] = acc

    vm = pl.BlockSpec(memory_space=pltpu.VMEM)
    return pl.pallas_call(
        body, name="all_reduce_small", in_specs=[vm], out_specs=vm, out_shape=jax.ShapeDtypeStruct((r, w), F32),
        scratch_shapes=[pltpu.VMEM((N_DEV, r, w), F32), pltpu.SemaphoreType.DMA((N_DEV - 1,)),
                        pltpu.SemaphoreType.DMA((N_DEV - 1,))],
    )(v)


def _after(x, token):
    first = (0,) * x.ndim
    patch = lax.dynamic_slice(x, first, (1,) * x.ndim) + token[0, 0].astype(x.dtype)
    return lax.dynamic_update_slice(x, patch, first)


def _vec8(v):
    return jnp.zeros((1, HEAD_DIM), F32).at[0, :N_HEADS].set(v.reshape(N_HEADS))


def _layer_fwd(i, x_in, wt, sm, mem_k, mem_v, late=None):
    tag = f"l{i}_"
    h = rms_fwd(x_in, sm["norm1_w"][i][None], name=tag + "rms1")
    w_in = wt["dn_w_in"] if i == 0 else wt["fox_w_in"]
    proj = matmul(h, w_in, name=tag + "proj", tm=256, tk=1024)
    sv = dict(x_in=x_in, h=h, proj=proj)
    if i == 0:
        qkv = dn_prep_fwd(proj, wt["conv_w"])
        gates = dn_gates_fwd(proj, _vec8(sm["dn_a_log"]), _vec8(sm["dn_dt_bias"]))
        gcol, grow, bcol = gates_to_heads(gates)
        o, states, parts = dn_core_fwd(qkv, gcol, grow, bcol)
        mix = dn_out_fwd(o, proj, sm["dn_o_norm_w"])
        sv.update(qkv=qkv, gcol=gcol, grow=grow, bcol=bcol, states=states, parts=parts, o=o)
    else:
        wqk = jnp.stack([sm["fox_q_norm_w"], sm["fox_k_norm_w"]])
        qk = fox_prep_fwd(proj, wqk)
        fcum = fox_gates_fwd(proj, _vec8(sm["fox_f_bias"]))
        fcol, frow = fcum_to_heads(fcum)
        mix, o, lse = fox_attn_fwd(qk, proj, fcol, frow)
        sv.update(wqk=wqk, qk=qk, fcol=fcol, frow=frow, o=o, lse=lse)
    mem_out = memattn_fwd(proj, sm["memq_norm_w"][i][None], mem_k, mem_v, name=tag + "memattn_fwd")
    cat = jnp.concatenate([mix, mem_out], axis=1)
    if late is not None:
        wt.update(late(cat))
    x_mid = matmul(cat, wt["w_out"], b_layer=i, res=x_in, name=tag + "out_proj")
    h2 = rms_fwd(x_mid, sm["norm2_w"][i][None], name=tag + "rms2")
    ff, act = matmul(h2, wt["w_mlp1"], b_layer=i, b_slots=True, also_sqrelu=True, out_dtype=MM, name=tag + "mlp1")
    x_out = matmul(act, wt["w_mlp2"], b_layer=i, res=x_mid, name=tag + "mlp2")
    sv.update(cat=cat, x_mid=x_mid, h2=h2, ff=ff, act=act)
    return x_out, sv


def _layer_bwd(i, dx_out, sv, wt, sm, mem_k, mem_v, on_mlp=None, on_core=None):
    tag = f"l{i}_"
    big, small = {}, {}
    dff = matmul(dx_out, wt["w_mlp2"], b_layer=i, tb=True, times_dsqrelu=sv["ff"], out_dtype=MM, name=tag + "d_ff")
    big["w_mlp2"] = matmul(sv["act"], dx_out, ta=True, name=tag + "d_w_mlp2", tk=2048)
    dh2 = matmul(dff, wt["w_mlp1"], b_layer=i, tb=True, b_slots=True, name=tag + "d_h2")
    big["w_mlp1"] = matmul(sv["h2"], dff, ta=True, name=tag + "d_w_mlp1", tm=512, tn=D_FF, tk=512)
    dx_mid, small["norm2_w"] = rms_bwd(sv["x_mid"], sm["norm2_w"][i][None], dh2, dx_out, name=tag + "rms2_bwd")
    dcat = matmul(dx_mid, wt["w_out"], b_layer=i, tb=True, name=tag + "d_cat")
    big["w_out"] = matmul(sv["cat"], dx_mid, ta=True, name=tag + "d_w_out", tk=2048)
    proj = sv["proj"]
    memq_norm_w = sm["memq_norm_w"][i][None]
    if on_mlp is not None:
        memq_norm_w = _after(memq_norm_w, on_mlp(big["w_mlp2"], big["w_mlp1"], big["w_out"]))
    dqm, small["memq_norm_w"], dmk, dmv = memattn_bwd(proj, memq_norm_w, mem_k, mem_v, dcat, name=tag + "memattn_bwd")
    t = proj.shape[0]
    pad = jnp.zeros((t, PROJ_W - TAIL - HEAD_DIM), MM)
    if i == 0:
        do, dz, small["dn_o_norm_w"] = dn_out_bwd(sv["o"], proj, sm["dn_o_norm_w"], dcat)
        bcol = sv["bcol"] if on_core is None else _after(sv["bcol"], on_core(do))
        dqkv, dgc, dgr, dbc = dn_core_bwd(sv["qkv"], sv["gcol"], sv["grow"], bcol, sv["states"], sv["parts"], do)
        dtail, dal, ddt = dn_gates_bwd(proj, _vec8(sm["dn_a_log"]), _vec8(sm["dn_dt_bias"]), heads_to_gates(dgc, dgr, dbc))
        dmain, dconv = dn_prep_bwd(proj, wt["conv_w"], dqkv)
        small["dn_a_log"], small["dn_dt_bias"] = dal[:, :N_HEADS], ddt[:, :N_HEADS]
        big["conv_w"] = dconv[:4]
        dproj = jnp.concatenate([dmain, dz, dqm, dtail, pad], axis=1)
    else:
        dq, dk, dv, dgate, dfc, dfr = fox_attn_bwd(sv["qk"], proj, sv["fcol"], sv["frow"], sv["o"], sv["lse"], dcat)
        dtail, dfb = fox_gates_bwd(proj, _vec8(sm["fox_f_bias"]), heads_to_fcum(dfc, dfr))
        dqk, dwqk = fox_prep_bwd(proj, sv["wqk"], dq, dk)
        small["fox_f_bias"] = dfb[:, :N_HEADS]
        small["fox_q_norm_w"], small["fox_k_norm_w"] = dwqk[0], dwqk[1]
        dproj = jnp.concatenate([dqk, dv.astype(MM), dgate, dqm, dtail, pad], axis=1)
    w_in = wt["dn_w_in"] if i == 0 else wt["fox_w_in"]
    dh = matmul(dproj, w_in, tb=True, name=tag + "d_h", tm=512)
    big["w_in"] = matmul(sv["h"], dproj, ta=True, name=tag + "d_w_in", tm=256)
    dx_in, small["norm1_w"] = rms_bwd(sv["x_in"], sm["norm1_w"][i][None], dh, dx_mid, name=tag + "rms1_bwd")
    return dx_in, big, small, (dmk, dmv)


def local_step(x, mem, target, wt, sm, late=None, on_layer1=None, on_mlp0=None, on_core0=None):
    wt = dict(wt)
    mem_k, mem_v = mem_fwd(mem, sm["mem_norm_w"][None], wt["w_mem_kv"], sm["mem_k_norm_w"][None])
    x0, sv0 = _layer_fwd(0, x, wt, sm, mem_k, mem_v, late)
    x1, sv1 = _layer_fwd(1, x0, wt, sm, mem_k, mem_v)
    dy, loss = loss_fwd(x1, target, name="loss")
    dx1, big1, small1, dm1 = _layer_bwd(1, dy, sv1, wt, sm, mem_k, mem_v)
    if on_layer1 is not None:
        dx1 = _after(dx1, on_layer1(big1))
    dx0, big0, small0, dm0 = _layer_bwd(0, dx1, sv0, wt, sm, mem_k, mem_v, on_mlp0, on_core0)
    dwn, dwkv, dwkn = mem_bwd(mem, sm["mem_norm_w"][None], wt["w_mem_kv"], sm["mem_k_norm_w"][None], *dm0, *dm1)
    small = dict(mem_norm_w=dwn[0], mem_k_norm_w=dwkn[0],
                 norm1_w=jnp.concatenate([small0["norm1_w"], small1["norm1_w"]]),
                 norm2_w=jnp.concatenate([small0["norm2_w"], small1["norm2_w"]]),
                 memq_norm_w=jnp.concatenate([small0["memq_norm_w"], small1["memq_norm_w"]]),
                 dn_a_log=small0["dn_a_log"], dn_dt_bias=small0["dn_dt_bias"], dn_o_norm_w=small0["dn_o_norm_w"],
                 fox_f_bias=small1["fox_f_bias"], fox_q_norm_w=small1["fox_q_norm_w"], fox_k_norm_w=small1["fox_k_norm_w"])
    big = dict(w_mem_kv=dwkv, dn_w_in=big0["w_in"], fox_w_in=big1["w_in"], conv_w=big0["conv_w"],
               w_out=[big0["w_out"], big1["w_out"]], w_mlp1=[big0["w_mlp1"], big1["w_mlp1"]],
               w_mlp2=[big0["w_mlp2"], big1["w_mlp2"]])
    return loss, dx0, big, small


def w_in_slots_to_kernel(slots, n_scalars):
    c = slots.shape[2]
    cut = 4096 - 3 * c
    pad = jnp.zeros((slots.shape[1], PROJ_W - TAIL - n_scalars), slots.dtype)
    return jnp.concatenate([slots[0], slots[1], slots[2], slots[3, :, :cut], slots[3, :, cut + n_scalars:],
                            slots[3, :, cut:cut + n_scalars], pad], axis=1)


def w_in_kernel_to_slots(w, n_scalars):
    c = (4096 + n_scalars + MEM_WIDTH) // N_CHIP
    last = jnp.concatenate([w[:, 3 * c:4096], w[:, TAIL:TAIL + n_scalars], w[:, 4096:TAIL]], axis=1)
    return jnp.stack([w[:, :c], w[:, c:2 * c], w[:, 2 * c:3 * c], last])


BIG_SPECS = dict(w_mem_kv=("rows", 1, 256, 1024), w_out=("rows", 2, 384, 1024), w_mlp2=("rows", 2, 1024, 1024),
                 w_mlp1=("cols", 2, 1024, 1024), dn_w_in=("rows", 1, 1024, 1156), fox_w_in=("rows", 1, 1024, 1154))
BIG_NAMES = tuple(BIG_SPECS)
EARLY_NAMES = ("w_mem_kv", "dn_w_in")
LATE_NAMES = ("w_out", "w_mlp2", "w_mlp1", "fox_w_in")
BIG_SPECS.update({f"{name}_{i}": (BIG_SPECS[name][0], 1) + BIG_SPECS[name][2:]
                  for name in ("w_out", "w_mlp2", "w_mlp1") for i in range(2)})
RS_LAYER1 = ("fox_w_in", "w_out_1", "w_mlp2_1", "w_mlp1_1")
RS_MLP0 = ("w_mlp2_0", "w_mlp1_0", "w_out_0")
RS_LAST = ("dn_w_in", "w_mem_kv")


def _full_shape(name, half=False):
    kind, a, b, c = BIG_SPECS[name]
    b = b // 2 if half else b
    return (a, N_CHIP, b, c) if kind == "rows" else (a, b, N_CHIP * c)


def _ds(start, size, align):
    return pl.ds(start if isinstance(start, int) else pl.multiple_of(start, align), size)


def _half_rows(name, h):
    b = BIG_SPECS[name][2]
    return _ds(h * (b // 2), b // 2, 16)


def _shard_idx(name, h):
    return (slice(None), _half_rows(name, h), slice(None))


def _full_idx(name, j=None, h=None):
    kind, _, _, c = BIG_SPECS[name]
    rows = slice(None) if h is None else _half_rows(name, h)
    if kind == "rows":
        return (slice(None), slice(None) if j is None else j, rows, slice(None))
    return (slice(None), rows, slice(None) if j is None else _ds(j * c, c, 128))


def _slots_shape(name):
    _, a, b, c = BIG_SPECS[name]
    return (a, N_CHIP, b, c)


def _slots_idx(name, j, h):
    return (slice(None), j, _half_rows(name, h), slice(None))


def _row_block(name):
    hs = BIG_SPECS[name][2] // 2
    return hs if hs <= ROWS else ROWS


def _remote(src, dst, send_sem, recv_sem, to):
    return pltpu.make_async_remote_copy(src_ref=src, dst_ref=dst, send_sem=send_sem, recv_sem=recv_sem, device_id=to,
                                        device_id_type=MESH)


HBM = pl.BlockSpec(memory_space=pltpu.HBM)
SEM = pl.BlockSpec(memory_space=pltpu.SEMAPHORE)
EFFECT = pltpu.CompilerParams(has_side_effects=pltpu.SideEffectType.DATAFLOW_SIDE_EFFECTING)


def _in_hbm(a):
    return pltpu.with_memory_space_constraint(a, pltpu.HBM)


def _chip_copies(names, ins, lands, send_sems, recv_sems):
    x, y, c, chips = _place()
    return [_remote(ins[a].at[_shard_idx(name, c)], lands[a].at[_slots_idx(name, 2 * x + y, c)], send_sems.at[3 * a + k],
                    recv_sems.at[3 * a + k], (chip[0], chip[1], c))
            for a, name in enumerate(names) for k, chip in enumerate(chips)]


def _copies_start(call_name, copies, sources, land_shapes, per_source=3, land_dtype=MM, after=()):
    n, n_after = len(sources), len(after)

    def body(*refs):
        ins, lands, token = refs[:n], refs[n:2 * n], refs[-1]
        send_sems, recv_sems = refs[2 * n + n_after], refs[2 * n + n_after + 1]
        for cp in copies(ins, lands, send_sems, recv_sems):
            cp.start()
        token[...] = jnp.zeros_like(token)

    ins = [_in_hbm(a) for a in sources]
    lands = [_in_hbm(lax.empty(shape, land_dtype)) for shape in land_shapes]
    sems = (pltpu.SemaphoreType.DMA((per_source * n,)), pltpu.SemaphoreType.DMA((per_source * n,)))
    outs = pl.pallas_call(
        body, name=call_name, in_specs=[HBM] * (2 * n) + [ANY] * n_after,
        out_specs=(SEM, SEM) + (HBM,) * (2 * n) + (pl.BlockSpec(memory_space=pltpu.VMEM),),
        out_shape=sems + tuple(pltpu.HBM(a.shape, a.dtype) for a in ins + lands) + (jax.ShapeDtypeStruct((8, HEAD_DIM), F32),),
        input_output_aliases={a: 2 + a for a in range(2 * n)}, compiler_params=EFFECT,
    )(*ins, *lands, *after)
    return outs[:-1], outs[-1]


def _copies_wait(call_name, copies, state, after):
    n = (len(state) - 2) // 2

    def body(*refs):
        send_sems, recv_sems, ins, lands = refs[0], refs[1], refs[2:2 + n], refs[2 + n:2 + 2 * n]
        for cp in copies(ins, lands, send_sems, recv_sems):
            cp.wait_send()
            cp.wait_recv()

    outs = pl.pallas_call(
        body, name=call_name, in_specs=[SEM, SEM] + [HBM] * (2 * n) + [ANY], out_specs=(HBM,) * (2 * n),
        out_shape=tuple(pltpu.HBM(a.shape, a.dtype) for a in state[2:]),
        input_output_aliases={2 + a: a for a in range(2 * n)}, compiler_params=EFFECT,
    )(*state, after)
    return outs[:n], outs[n:]


def all_gather_start(shards, names, after):
    return _copies_start("all_gather_start", functools.partial(_chip_copies, names), [shards[name] for name in names],
                         [_slots_shape(name) for name in names], after=after)


def all_gather_wait(state, names, after):
    ins, lands = _copies_wait("all_gather_wait", functools.partial(_chip_copies, names), state, after)
    return dict(zip(names, ins)), dict(zip(names, lands))


def _chip_sends(names, ins, lands, send_sems, recv_sems):
    x, y, c, chips = _place()
    return [_remote(ins[a].at[_full_idx(name, 2 * chip[0] + chip[1])], lands[a].at[k], send_sems.at[3 * a + k],
                    recv_sems.at[3 * a + k], (chip[0], chip[1], c))
            for a, name in enumerate(names) for k, chip in enumerate(chips)]


def _got_shape(name):
    _, a_, b_, c_ = BIG_SPECS[name]
    return (3, a_, b_ // 2, c_)


def rs_chip_start(pairs, names, tag):
    return _copies_start("rs_chip_start_" + tag, functools.partial(_chip_sends, names), [pairs[name] for name in names],
                         [_got_shape(name) for name in names])


def rs_chip_wait(state, names, tag, after):
    _, lands = _copies_wait("rs_chip_wait_" + tag, functools.partial(_chip_sends, names), state, after)
    return dict(zip(names, lands))


def all_gather_pass_on(lands, names):
    n = len(names)

    def body(*refs):
        outs, send_sems, recv_sems = refs[n:2 * n], refs[2 * n], refs[2 * n + 1]
        x, y, c, chips = _place()
        work = [(3 * a + k, a, name, 2 * chip[0] + chip[1]) for a, name in enumerate(names) for k, chip in enumerate(chips)]
        cps = []
        for s, a, name, slot in work:
            landed = outs[a].at[_slots_idx(name, slot, c)]
            cps.append(_remote(landed, landed, send_sems.at[s], recv_sems.at[s], (x, y, 1 - c)))
            cps[-1].start()
        for s, a, name, slot in work:
            passed = outs[a].at[_slots_idx(name, slot, 1 - c)]
            _remote(passed, passed, send_sems.at[s], recv_sems.at[s], (x, y, 1 - c)).wait_recv()
        for cp in cps:
            cp.wait_send()

    outs = pl.pallas_call(
        body, name="all_gather_pass_on", in_specs=[ANY] * n, out_specs=[ANY] * n,
        input_output_aliases={a: a for a in range(n)},
        out_shape=[jax.ShapeDtypeStruct(_slots_shape(name), MM) for name in names],
        scratch_shapes=[pltpu.SemaphoreType.DMA((3 * n,)), pltpu.SemaphoreType.DMA((3 * n,))],
    )(*[lands[name] for name in names])
    return dict(zip(names, outs))


def all_gather_big(shards, names):
    n = len(names)

    def body(*refs):
        ins, outs = refs[:n], refs[n:2 * n]
        send_sems, recv_sems, fsend_sems, frecv_sems = refs[2 * n:]
        x, y, c, chips = _place()
        me_chip, sibling = 2 * x + y, (x, y, 1 - c)
        work = [(3 * a + k, a, name, chip) for a, name in enumerate(names) for k, chip in enumerate(chips)]
        sends = []
        for s, a, name, chip in work:
            cp = _remote(ins[a].at[_shard_idx(name, c)], outs[a].at[_slots_idx(name, me_chip, c)], send_sems.at[s],
                         recv_sems.at[s], (chip[0], chip[1], c))
            cp.start()
            sends.append(cp)
        for s, a, name, chip in work:
            landed = outs[a].at[_slots_idx(name, 2 * chip[0] + chip[1], c)]
            _remote(landed, landed, send_sems.at[s], recv_sems.at[s], (chip[0], chip[1], c)).wait_recv()
            cp = _remote(landed, landed, fsend_sems.at[s], frecv_sems.at[s], sibling)
            cp.start()
            sends.append(cp)
        for s, a, name, chip in work:
            passed = outs[a].at[_slots_idx(name, 2 * chip[0] + chip[1], 1 - c)]
            _remote(passed, passed, fsend_sems.at[s], frecv_sems.at[s], sibling).wait_recv()
        for cp in sends:
            cp.wait_send()

    outs = pl.pallas_call(
        body, name="all_gather_big", in_specs=[ANY] * n, out_specs=[ANY] * n,
        out_shape=[jax.ShapeDtypeStruct(_slots_shape(name), MM) for name in names],
        scratch_shapes=[pltpu.SemaphoreType.DMA((3 * n,))] * 4,
    )(*[shards[name] for name in names])
    return dict(zip(names, outs))


def with_own_slot(name, full, shard, chip):
    return lax.dynamic_update_slice(full, shard[:, None], (0, chip, 0, 0))


def rs_pair_exchange_big(grads, names, tag, after):
    n = len(names)

    def body(*refs):
        ins, outs, send_sems, recv_sems = refs[:n], refs[n + 1:2 * n + 1], refs[2 * n + 1], refs[2 * n + 2]
        cps = _pair_sends(names, ins, outs, send_sems, recv_sems)
        for cp in cps:
            cp.start()
        for cp in cps:
            cp.wait()

    outs = pl.pallas_call(
        body, name="rs_pair_exchange_" + tag, in_specs=[ANY] * (n + 1), out_specs=[ANY] * n,
        out_shape=[jax.ShapeDtypeStruct(_full_shape(name, half=True), F32) for name in names],
        scratch_shapes=[pltpu.SemaphoreType.DMA((n,)), pltpu.SemaphoreType.DMA((n,))],
    )(*[grads[name] for name in names], after)
    return dict(zip(names, outs))


def rs_pair_add_big(name, place, g, got):
    kind, a_, b_, c_ = BIG_SPECS[name]
    rb = _row_block(name)
    nb = (b_ // 2) // rb

    def body(place_ref, g_ref, got_ref, o_ref):
        o_ref[...] = (g_ref[...] + got_ref[...]).astype(o_ref.dtype)

    if kind == "rows":
        g_spec = pl.BlockSpec((None, None, rb, c_), lambda a, j, i, p: (a, j, p[0] * nb + i, 0))
        o_spec = pl.BlockSpec((None, None, rb, c_), lambda a, j, i, p: (a, j, i, 0))
    else:
        g_spec = pl.BlockSpec((None, rb, c_), lambda a, j, i, p: (a, p[0] * nb + i, j))
        o_spec = pl.BlockSpec((None, rb, c_), lambda a, j, i, p: (a, i, j))
    return pl.pallas_call(
        body, name="rs_pair_add_" + name,
        grid_spec=pltpu.PrefetchScalarGridSpec(num_scalar_prefetch=1, grid=(a_, N_CHIP, nb), in_specs=[g_spec, o_spec],
                                               out_specs=o_spec),
        out_shape=jax.ShapeDtypeStruct(_full_shape(name, half=True), MM),
        compiler_params=_cparams(("parallel", "parallel", "parallel")),
    )(place, g, got)


def rs_chip_add_big(name, place, g, got_pair, got_chips):
    kind, a_, b_, c_ = BIG_SPECS[name]
    rb = _row_block(name)
    nb = (b_ // 2) // rb

    def body(place_ref, g_ref, s_ref, r0_ref, r1_ref, r2_ref, o_ref):
        own = g_ref[...] + s_ref[...]
        o_ref[...] = ((own + r0_ref[...].astype(F32)) + r1_ref[...].astype(F32)) + r2_ref[...].astype(F32)

    if kind == "rows":
        g_spec = pl.BlockSpec((None, None, rb, c_), lambda a, i, p: (a, p[1], p[0] * nb + i, 0))
        s_spec = pl.BlockSpec((None, None, rb, c_), lambda a, i, p: (a, p[1], i, 0))
    else:
        g_spec = pl.BlockSpec((None, rb, c_), lambda a, i, p: (a, p[0] * nb + i, p[1]))
        s_spec = pl.BlockSpec((None, rb, c_), lambda a, i, p: (a, i, p[1]))
    r_spec = lambda k: pl.BlockSpec((None, None, rb, c_), lambda a, i, p: (k, a, i, 0))
    return pl.pallas_call(
        body, name="rs_chip_add_" + name,
        grid_spec=pltpu.PrefetchScalarGridSpec(
            num_scalar_prefetch=1, grid=(a_, nb), in_specs=[g_spec, s_spec, r_spec(0), r_spec(1), r_spec(2)],
            out_specs=pl.BlockSpec((None, rb, c_), lambda a, i, p: (a, p[0] * nb + i, 0))),
        out_shape=jax.ShapeDtypeStruct((a_, b_, c_), F32), compiler_params=_cparams(("parallel", "parallel")),
    )(place, g, got_pair, got_chips, got_chips, got_chips)


def rs_pair_gather_big(halves, tag):
    names = tuple(halves)
    n = len(names)

    def body(*refs):
        outs, send_sems, recv_sems = refs[n:2 * n], refs[2 * n], refs[2 * n + 1]
        x, y, c, _ = _place()
        cps = []
        for a, name in enumerate(names):
            mine = outs[a].at[_shard_idx(name, c)]
            cp = _remote(mine, mine, send_sems.at[a], recv_sems.at[a], (x, y, 1 - c))
            cp.start()
            cps.append(cp)
        for a, name in enumerate(names):
            cps[a].wait_send()
            theirs = outs[a].at[_shard_idx(name, 1 - c)]
            _remote(theirs, theirs, send_sems.at[a], recv_sems.at[a], (x, y, 1 - c)).wait_recv()

    outs = pl.pallas_call(
        body, name="rs_pair_gather_" + tag, in_specs=[ANY] * n, out_specs=[ANY] * n,
        input_output_aliases={a: a for a in range(n)},
        out_shape=[jax.ShapeDtypeStruct(BIG_SPECS[name][1:], F32) for name in names],
        scratch_shapes=[pltpu.SemaphoreType.DMA((n,)), pltpu.SemaphoreType.DMA((n,))],
    )(*[halves[name] for name in names])
    return dict(zip(names, outs))


def _pair_sends(names, ins, lands, send_sems, recv_sems):
    x, y, c, _ = _place()
    return [_remote(ins[a].at[_full_idx(name, None, 1 - c)], lands[a], send_sems.at[a], recv_sems.at[a], (x, y, 1 - c))
            for a, name in enumerate(names)]


def rs_pair_start(grads, names, tag):
    return _copies_start("rs_pair_start_" + tag, functools.partial(_pair_sends, names), [grads[name] for name in names],
                         [_full_shape(name, half=True) for name in names], per_source=1, land_dtype=F32)


def rs_middle(pair_state, names, tag, place, after):
    ins, lands = _copies_wait("rs_pair_wait_" + tag, functools.partial(_pair_sends, names), pair_state, after)
    grads, got_pair = dict(zip(names, ins)), dict(zip(names, lands))
    pairs = {name: rs_pair_add_big(name, place, grads[name], got_pair[name]) for name in names}
    state, token = rs_chip_start(pairs, names, tag)
    return (grads, got_pair, state), token


def rs_end(begun, names, tag, place, after):
    grads, got_pair, state = begun
    got_chips = rs_chip_wait(state, names, tag, after)
    return {name: rs_chip_add_big(name, place, grads[name], got_pair[name], got_chips[name]) for name in names}


PACK_W = 1024
SMALL =(("mem_norm_w", 1024), ("mem_k_norm_w", 128), ("norm1_w", 2048), ("dn_a_log", 8), ("dn_dt_bias", 8),
         ("dn_o_norm_w", 128), ("fox_f_bias", 8), ("fox_q_norm_w", 128), ("fox_k_norm_w", 128), ("memq_norm_w", 256),
         ("norm2_w", 2048))
SMALL_ROWS = 8
CONV_ROWS = 4 * 3 * D_MODEL // PACK_W
LOSS_AT = sum(n for _, n in SMALL)


def pack_small(parts, extra=None):
    flat = [parts[name].astype(F32).reshape(-1) for name, _ in SMALL]
    used = LOSS_AT
    if extra is not None:
        flat.append(extra.reshape(1))
        used += 1
    flat.append(jnp.zeros((SMALL_ROWS * PACK_W - used,), F32))
    return jnp.concatenate(flat).reshape(SMALL_ROWS, PACK_W)


def unpack_small(packed, shapes):
    flat, out, at = packed.reshape(-1), {}, 0
    for name, n in SMALL:
        out[name] = flat[at:at + n].reshape(shapes[name])
        at += n
    return out


def _adam_all(w, g, m, v, name):
    shape = w.shape
    r2 = lambda a: a.reshape(-1, shape[-1])
    d, nm, nv = adamw(r2(w), r2(g), r2(m), r2(v), name=name)
    return d.reshape(shape), nm.reshape(shape), nv.reshape(shape)


WEIGHTS = ("mem_norm_w", "w_mem_kv", "mem_k_norm_w", "norm1_w", "dn_w_in", "dn_conv_w", "dn_a_log", "dn_dt_bias",
           "dn_o_norm_w", "fox_w_in", "fox_f_bias", "fox_q_norm_w", "fox_k_norm_w", "memq_norm_w", "w_out", "norm2_w",
           "w_mlp1", "w_mlp2")


def kernel(x, mem, mem_norm_w, w_mem_kv, mem_k_norm_w, norm1_w, dn_w_in, dn_conv_w, dn_a_log, dn_dt_bias, dn_o_norm_w, fox_w_in, fox_f_bias, fox_q_norm_w, fox_k_norm_w, memq_norm_w, w_out, norm2_w, w_mlp1, w_mlp2, loss_target, m_mem_norm_w, m_w_mem_kv, m_mem_k_norm_w, m_norm1_w, m_dn_w_in, m_dn_conv_w, m_dn_a_log, m_dn_dt_bias, m_dn_o_norm_w, m_fox_w_in, m_fox_f_bias, m_fox_q_norm_w, m_fox_k_norm_w, m_memq_norm_w, m_w_out, m_norm2_w, m_w_mlp1, m_w_mlp2, v_mem_norm_w, v_w_mem_kv, v_mem_k_norm_w, v_norm1_w, v_dn_w_in, v_dn_conv_w, v_dn_a_log, v_dn_dt_bias, v_dn_o_norm_w, v_fox_w_in, v_fox_f_bias, v_fox_q_norm_w, v_fox_k_norm_w, v_memq_norm_w, v_w_out, v_norm2_w, v_w_mlp1, v_w_mlp2):
    args = dict(locals())
    w = {n: args[n] for n in WEIGHTS}
    m = {n: args["m_" + n] for n in WEIGHTS}
    v = {n: args["v_" + n] for n in WEIGHTS}
    core, chip = lax.axis_index("c"), 2 * lax.axis_index("x") + lax.axis_index("y")
    place = jnp.stack([core, chip]).astype(jnp.int32)

    shards = {name: w[name].reshape(BIG_SPECS[name][1:]).astype(MM) for name in BIG_NAMES}
    w_in_full = lambda arr, n_scalars: w_in_slots_to_kernel(arr[0], n_scalars)
    early = {name: with_own_slot(name, arr, shards[name], chip)
             for name, arr in all_gather_big(shards, EARLY_NAMES).items()}
    conv_mine = jnp.where(core == 0, dn_conv_w[0], 0.0)
    conv_placed = lax.dynamic_update_slice(jnp.zeros((4, 3 * D_MODEL), F32), conv_mine, (0, 768 * chip))
    conv_full = all_reduce_small(jnp.pad(conv_placed.reshape(CONV_ROWS, PACK_W), ((0, 16 - CONV_ROWS), (0, 0))))
    late_state, token = all_gather_start(shards, LATE_NAMES, after=(early["w_mem_kv"], early["dn_w_in"], conv_full))
    wt = dict(w_mem_kv=_after(early["w_mem_kv"].reshape(D_MODEL, 2 * MEM_WIDTH), token),
              dn_w_in=w_in_full(early["dn_w_in"], 2 * N_HEADS), conv_w=conv_full[:CONV_ROWS].reshape(4, 3 * D_MODEL))

    def late(after):
        late_shards, lands = all_gather_wait(late_state, LATE_NAMES, after)
        full = {name: with_own_slot(name, arr, late_shards[name], chip)
                for name, arr in all_gather_pass_on(lands, LATE_NAMES).items()}
        return dict(fox_w_in=w_in_full(full["fox_w_in"], N_HEADS), w_out=full["w_out"].reshape(2, 3 * MEM_WIDTH, D_MODEL),
                    w_mlp1=full["w_mlp1"], w_mlp2=full["w_mlp2"].reshape(2, D_FF, D_MODEL))

    sm = dict(mem_norm_w=mem_norm_w, mem_k_norm_w=mem_k_norm_w, norm1_w=norm1_w, norm2_w=norm2_w, memq_norm_w=memq_norm_w,
              dn_a_log=dn_a_log[0], dn_dt_bias=dn_dt_bias[0], dn_o_norm_w=dn_o_norm_w, fox_f_bias=fox_f_bias[0],
              fox_q_norm_w=fox_q_norm_w, fox_k_norm_w=fox_k_norm_w)
    w_in_slots = lambda g, n_scalars: w_in_kernel_to_slots(g, n_scalars)[None]
    rows_view = lambda g, name: g.reshape(_full_shape(name))
    pair_started, begun = {}, {}

    def on_layer1(big1):
        grads1 = dict(fox_w_in=w_in_slots(big1["w_in"], N_HEADS), w_out_1=rows_view(big1["w_out"], "w_out_1"),
                      w_mlp2_1=rows_view(big1["w_mlp2"], "w_mlp2_1"), w_mlp1_1=big1["w_mlp1"][None])
        pair_started["layer1"], token = rs_pair_start(grads1, RS_LAYER1, "layer1")
        return token

    def on_mlp0(d_w_mlp2, d_w_mlp1, d_w_out):
        begun["layer1"], token1 = rs_middle(pair_started["layer1"], RS_LAYER1, "layer1", place, d_w_out)
        grads0 = dict(w_mlp2_0=rows_view(d_w_mlp2, "w_mlp2_0"), w_mlp1_0=d_w_mlp1[None],
                      w_out_0=rows_view(d_w_out, "w_out_0"))
        pair_started["mlp0"], token0 = rs_pair_start(grads0, RS_MLP0, "mlp0")
        return token1 + token0

    def on_core0(d_o):
        begun["mlp0"], token = rs_middle(pair_started["mlp0"], RS_MLP0, "mlp0", place, d_o)
        return token

    sm["norm1_w"] = _after(norm1_w, token)
    loss_part, dx, big, small = local_step(x[0], mem[0], loss_target[0], wt, sm, late, on_layer1, on_mlp0, on_core0)
    small_pack = jnp.concatenate([pack_small(small, loss_part[0, :1]), big["conv_w"].reshape(CONV_ROWS, PACK_W),
                                  jnp.zeros((24 - SMALL_ROWS - CONV_ROWS, PACK_W), F32)])
    small_all = all_reduce_small(small_pack)
    small_sum = small_all[:SMALL_ROWS]
    conv_sum = lax.dynamic_slice(small_all[SMALL_ROWS:SMALL_ROWS + CONV_ROWS].reshape(4, 3 * D_MODEL), (0, 768 * chip), (4, 768))
    loss = small_sum.reshape(-1)[LOSS_AT]
    halves = rs_end(begun["layer1"], RS_LAYER1, "layer1", place, small_all)
    halves.update(rs_end(begun["mlp0"], RS_MLP0, "mlp0", place, small_all))
    summed = rs_pair_gather_big(halves, "early")

    last = dict(dn_w_in=w_in_slots(big["dn_w_in"], 2 * N_HEADS), w_mem_kv=rows_view(big["w_mem_kv"], "w_mem_kv"))
    got_pair = rs_pair_exchange_big(last, RS_LAST, "last", after=summed["fox_w_in"])
    pairs = {name: rs_pair_add_big(name, place, last[name], got_pair[name]) for name in RS_LAST}
    last_state, token = rs_chip_start(pairs, RS_LAST, "last")
    summed = {name: _after(arr, token) for name, arr in summed.items()}

    big_sum = {"fox_w_in": summed["fox_w_in"]}
    big_sum.update({name: jnp.concatenate([summed[name + "_0"], summed[name + "_1"]]) for name in ("w_out", "w_mlp2", "w_mlp1")})
    grads = unpack_small(small_sum, {n: w[n].shape for n, _ in SMALL})
    grads.update({name: big_sum[name].reshape(w[name].shape) for name in big_sum}, dn_conv_w=conv_sum[None])
    delta, new_m, new_v = {}, {}, {}
    for n in ("fox_w_in", "w_out", "w_mlp1", "w_mlp2", "dn_conv_w"):
        delta[n], new_m[n], new_v[n] = _adam_all(w[n], grads[n], m[n], v[n], "adamw_" + n)

    got_chips = rs_chip_wait(last_state, RS_LAST, "last", delta["w_mlp2"])
    summed_last = rs_pair_gather_big({name: rs_chip_add_big(name, place, last[name], got_pair[name], got_chips[name])
                                      for name in RS_LAST}, "last")
    for n in RS_LAST:
        grads[n] = summed_last[n].reshape(w[n].shape)
        delta[n], new_m[n], new_v[n] = _adam_all(w[n], grads[n], m[n], v[n], "adamw_" + n)
    shapes = {n: w[n].shape for n, _ in SMALL}
    d_s, m_s, v_s = adamw(pack_small(w), small_sum, pack_small(m), pack_small(v), name="adamw_small")
    for out, packed in ((delta, d_s), (new_m, m_s), (new_v, v_s)):
        out.update(unpack_small(packed, shapes))
    return (loss, dx[None], *[grads[n] for n in WEIGHTS], *[delta[n] for n in WEIGHTS],
            *[new_m[n] for n in WEIGHTS], *[new_v[n] for n in WEIGHTS])
```

```python
import functools

import jax
import jax.numpy as jnp
from jax import lax
from jax.experimental import pallas as pl
from jax.experimental.pallas import tpu as pltpu

F32 = jnp.float32
MM = jnp.bfloat16
HI = lax.Precision.HIGHEST

D_MODEL = 1024
HEAD_DIM = 128
N_HEADS = 8
MEM_HEADS = 4
MEM_WIDTH = MEM_HEADS * HEAD_DIM
N_MEM = 256
D_FF = 4 * D_MODEL
CHUNK = 64
EPS = 1e-6
QSCALE = HEAD_DIM ** -0.5
PROJ_W = 4736
TAIL = 4608
TAIL_BLK = TAIL // HEAD_DIM
ROWS = 512
VMEM_LIMIT = 56 * 1024 * 1024

ADAM_LR = 0.001
ADAM_B1 = 0.9
ADAM_B2 = 0.999
ADAM_EPS = 1e-08
ADAM_WD = 0.01
ADAM_STEP = 10

N_DEV = 8
N_CHIP = 4
MESH = pl.DeviceIdType.MESH


def _cparams(sem=None):
    return pltpu.CompilerParams(dimension_semantics=sem, vmem_limit_bytes=VMEM_LIMIT)


def _dot(a, b, ca, cb, hi):
    dims = (((ca,), (cb,)), ((), ()))
    if hi:
        return lax.dot_general(a, b, dims, precision=HI, preferred_element_type=F32)
    return lax.dot_general(a.astype(MM), b.astype(MM), dims, preferred_element_type=F32)


@functools.partial(jax.custom_vjp, nondiff_argnums=(2, 3, 4))
def mmul(a, b, ca, cb, hi):
    return _dot(a, b, ca, cb, hi)


def _mmul_fwd(a, b, ca, cb, hi):
    return _dot(a, b, ca, cb, hi), (a, b)


def _mmul_bwd(ca, cb, hi, res, g):
    a, b = res
    if ca == 1:
        da = _dot(g, b, 1, 1, hi) if cb == 0 else _dot(g, b, 1, 0, hi)
    else:
        da = _dot(b, g, 1, 1, hi) if cb == 0 else _dot(b, g, 0, 1, hi)
    if cb == 0:
        db = _dot(a, g, 0, 0, hi) if ca == 1 else _dot(a, g, 1, 0, hi)
    else:
        db = _dot(g, a, 0, 0, hi) if ca == 1 else _dot(g, a, 0, 1, hi)
    return da.astype(a.dtype), db.astype(b.dtype)


mmul.defvjp(_mmul_fwd, _mmul_bwd)


def _iota2(n, m):
    return lax.broadcasted_iota(jnp.int32, (n, m), 0), lax.broadcasted_iota(jnp.int32, (n, m), 1)


def _same_block(r, c, shift):
    return lax.shift_right_logical(r, shift) == lax.shift_right_logical(c, shift)


def _split_bf16(x):
    hi = x.astype(jnp.bfloat16)
    return hi, (x - hi.astype(F32)).astype(jnp.bfloat16)


def _dot3(a, b, ca, cb):
    dims = (((ca,), (cb,)), ((), ()))
    (ah, al), (bh, bl) = _split_bf16(a), _split_bf16(b)
    d = lambda x, y: lax.dot_general(x, y, dims, preferred_element_type=F32)
    return d(ah, bh) + (d(ah, bl) + d(al, bh))


def _tri_inv_impl(a):
    n = a.shape[0]
    r, c = _iota2(n, n)
    eye = (r == c).astype(F32)
    b16, b32 = _same_block(r, c, 4), _same_block(r, c, 5)
    a0 = jnp.where(b16, a, 0.0)
    p = eye - a0
    b = _dot3(a0, a0, 1, 0)
    p = p + _dot3(p, b, 1, 0)
    b = _dot3(b, b, 1, 0)
    p = p + _dot3(p, b, 1, 0)
    b = _dot3(b, b, 1, 0)
    p = p + _dot3(p, b, 1, 0)
    a1 = jnp.where(jnp.logical_and(b32, jnp.logical_not(b16)), a, 0.0)
    p = p - _dot3(_dot3(p, a1, 1, 0), p, 1, 0)
    a2 = jnp.where(b32, 0.0, a)
    p = p - _dot3(_dot3(p, a2, 1, 0), p, 1, 0)
    return p


@jax.custom_vjp
def tri_inv(a):
    return _tri_inv_impl(a)


def _tri_inv_fwd(a):
    p = _tri_inv_impl(a)
    return p, p


def _tri_inv_bwd(p, g):
    return (-_dot3(_dot3(p, g, 0, 0), p, 1, 1),)


tri_inv.defvjp(_tri_inv_fwd, _tri_inv_bwd)


def _sigmoid(x):
    return 1.0 / (1.0 + jnp.exp(-x))


def _softplus(x):
    return jnp.maximum(x, 0.0) + jnp.log(1.0 + jnp.exp(-jnp.abs(x)))


def _silu(x):
    return x * _sigmoid(x)


def _rms(x, w):
    return x * lax.rsqrt(jnp.mean(x * x, axis=-1, keepdims=True) + EPS) * w


def _bf_round(x):
    return x.astype(MM).astype(F32)


def _acc(ref, val, first):
    @pl.when(first)
    def _():
        ref[...] = val

    @pl.when(jnp.logical_not(first))
    def _():
        ref[...] += val


def _tile(n, pref):
    if n % pref == 0:
        return pref
    return n


def matmul(a, b, *, ta=False, tb=False, b_slots=False, b_layer=None, res=None, also_sqrelu=False, times_dsqrelu=None,
           out_dtype=F32, name, tm=1024, tn=1024, tk=1024):
    m, k = (a.shape[1], a.shape[0]) if ta else a.shape
    b_shape = b.shape if b_layer is None else b.shape[1:]
    if b_slots:
        n = b_shape[1] if tb else N_CHIP * b_shape[2]
        assert (N_CHIP * b_shape[2] if tb else b_shape[1]) == k, (a.shape, b.shape, ta, tb)
        tn, tk = (tn, b_shape[2]) if tb else (b_shape[2], tk)
    else:
        n = b_shape[0] if tb else b_shape[1]
        assert (b_shape[1] if tb else b_shape[0]) == k, (a.shape, b.shape, ta, tb)
    tm, tn, tk = _tile(m, tm), _tile(n, tn), _tile(k, tk)
    nk = k // tk
    ca, cb = (0 if ta else 1), (1 if tb else 0)

    extra = tuple(e for e in (res, times_dsqrelu) if e is not None)
    assert len(extra) <= 1

    def body(a_ref, b_ref, *rest):
        e_ref = rest[0] if extra else None
        o_ref = rest[len(extra)]

        def finish(total):
            if res is not None:
                total = total + e_ref[...]
            if times_dsqrelu is not None:
                total = total * (2.0 * jnp.maximum(e_ref[...], 0.0))
            o_ref[...] = total.astype(o_ref.dtype)
            if also_sqrelu:
                rest[len(extra) + 1][...] = _sqrelu(total).astype(MM)

        if nk == 1:
            finish(_dot(a_ref[...], b_ref[...], ca, cb, False))
            return
        acc_ref, kk = rest[-1], pl.program_id(2)

        @pl.when(kk == 0)
        def _():
            acc_ref[...] = jnp.zeros_like(acc_ref)

        acc_ref[...] += _dot(a_ref[...], b_ref[...], ca, cb, False)

        @pl.when(kk == nk - 1)
        def _():
            finish(acc_ref[...])

    a_spec = pl.BlockSpec((tk, tm), lambda i, j, l: (l, i)) if ta else pl.BlockSpec((tm, tk), lambda i, j, l: (i, l))
    lead = () if b_layer is None else (b_layer,)
    if b_slots:
        b_block, b_index = ((None, tn, tk), lambda i, j, l: (l, j, 0)) if tb else ((None, tk, tn), lambda i, j, l: (j, l, 0))
    else:
        b_block, b_index = ((tn, tk), lambda i, j, l: (j, l)) if tb else ((tk, tn), lambda i, j, l: (l, j))
    b_spec = pl.BlockSpec((None,) * len(lead) + b_block, lambda i, j, l: lead + b_index(i, j, l))
    o_spec = pl.BlockSpec((tm, tn), lambda i, j, l: (i, j))
    out_shape = [jax.ShapeDtypeStruct((m, n), out_dtype)] + [jax.ShapeDtypeStruct((m, n), MM)] * also_sqrelu
    outs = pl.pallas_call(
        body, name=name, grid=(m // tm, n // tn, nk),
        in_specs=[a_spec, b_spec] + [o_spec] * len(extra), out_specs=[o_spec] * len(out_shape), out_shape=out_shape,
        scratch_shapes=[pltpu.VMEM((tm, tn), F32)] * (nk > 1),
        compiler_params=_cparams(("parallel", "parallel", "arbitrary")),
    )(a, b, *extra)
    return outs if also_sqrelu else outs[0]


def rms_fwd(x, w, *, name):
    t, d = x.shape

    def body(x_ref, w_ref, o_ref):
        o_ref[...] = _rms(x_ref[...], w_ref[...]).astype(o_ref.dtype)

    return pl.pallas_call(
        body, name=name, grid=(t // ROWS,),
        in_specs=[pl.BlockSpec((ROWS, d), lambda i: (i, 0)), pl.BlockSpec((1, d), lambda i: (0, 0))],
        out_specs=pl.BlockSpec((ROWS, d), lambda i: (i, 0)),
        out_shape=jax.ShapeDtypeStruct((t, d), MM), compiler_params=_cparams(("parallel",)),
    )(x, w)


def rms_bwd(x, w, dh, dres, *, name):
    t, d = x.shape

    def body(x_ref, w_ref, dh_ref, dr_ref, dx_ref, dw_ref):
        _, vjp = jax.vjp(_rms, x_ref[...], w_ref[...])
        dx, dw = vjp(dh_ref[...].astype(F32))
        dx_ref[...] = dx + dr_ref[...]
        _acc(dw_ref, dw, pl.program_id(0) == 0)

    row = pl.BlockSpec((ROWS, d), lambda i: (i, 0))
    vec = pl.BlockSpec((1, d), lambda i: (0, 0))
    return pl.pallas_call(
        body, name=name, grid=(t // ROWS,), in_specs=[row, vec, row, row], out_specs=[row, vec],
        out_shape=[jax.ShapeDtypeStruct((t, d), F32), jax.ShapeDtypeStruct((1, d), F32)],
        compiler_params=_cparams(("arbitrary",)),
    )(x, w, dh, dres)


def _sqrelu(x):
    return jnp.square(jnp.maximum(x, 0.0))


def loss_fwd(y, target, *, name):
    t, d = y.shape

    def body(y_ref, t_ref, dy_ref, l_ref):
        e = y_ref[...] - t_ref[...]
        dy_ref[...] = e * (1.0 / d)
        part = 0.5 * jnp.sum(jnp.sum(e * e, axis=-1, keepdims=True) * (1.0 / d), axis=0, keepdims=True)
        _acc(l_ref, jnp.broadcast_to(part, (1, HEAD_DIM)), pl.program_id(0) == 0)

    blk = pl.BlockSpec((ROWS, d), lambda i: (i, 0))
    return pl.pallas_call(
        body, name=name, grid=(t // ROWS,), in_specs=[blk, blk],
        out_specs=[blk, pl.BlockSpec((1, HEAD_DIM), lambda i: (0, 0))],
        out_shape=[jax.ShapeDtypeStruct((t, d), F32), jax.ShapeDtypeStruct((1, HEAD_DIM), F32)],
        compiler_params=_cparams(("arbitrary",)),
    )(y, target)


def _mem_kv(mem, wn, wkn, *ws):
    mn = _rms(mem, wn)
    outs = []
    for h in range(MEM_HEADS):
        outs.append(_rms(mmul(mn, ws[h], 1, 0, False), wkn))
    for h in range(MEM_HEADS):
        outs.append(mmul(mn, ws[MEM_HEADS + h], 1, 0, False))
    return tuple(outs)


def _w_cols(w_ref):
    return [w_ref[:, h * HEAD_DIM:(h + 1) * HEAD_DIM] for h in range(2 * MEM_HEADS)]


def mem_fwd(mem, wn, wkv, wkn):
    def body(mem_ref, wn_ref, w_ref, wkn_ref, k_ref, v_ref):
        outs = _mem_kv(mem_ref[...], wn_ref[...], wkn_ref[...], *_w_cols(w_ref))
        for h in range(MEM_HEADS):
            k_ref[:, h * HEAD_DIM:(h + 1) * HEAD_DIM] = outs[h]
            v_ref[:, h * HEAD_DIM:(h + 1) * HEAD_DIM] = outs[MEM_HEADS + h]

    shp = jax.ShapeDtypeStruct((mem.shape[0], MEM_WIDTH), F32)
    return pl.pallas_call(body, name="mem_fwd", out_shape=[shp, shp], compiler_params=_cparams())(mem, wn, wkv, wkn)


def mem_bwd(mem, wn, wkv, wkn, dk0, dv0, dk1, dv1):
    def body(mem_ref, wn_ref, w_ref, wkn_ref, dk0_ref, dv0_ref, dk1_ref, dv1_ref, dwn_ref, dw_ref, dwkn_ref):
        _, vjp = jax.vjp(lambda wn_, wkn_, *ws: _mem_kv(mem_ref[...], wn_, wkn_, *ws),
                         wn_ref[...], wkn_ref[...], *[w.astype(F32) for w in _w_cols(w_ref)])
        cols = lambda a, b: tuple(a[:, h * HEAD_DIM:(h + 1) * HEAD_DIM] + b[:, h * HEAD_DIM:(h + 1) * HEAD_DIM]
                                  for h in range(MEM_HEADS))
        cts = cols(dk0_ref, dk1_ref) + cols(dv0_ref, dv1_ref)
        grads = vjp(cts)
        dwn_ref[...] = grads[0]
        dwkn_ref[...] = grads[1]
        for h in range(2 * MEM_HEADS):
            dw_ref[:, h * HEAD_DIM:(h + 1) * HEAD_DIM] = grads[2 + h]

    return pl.pallas_call(
        body, name="mem_bwd",
        out_shape=[jax.ShapeDtypeStruct((1, D_MODEL), F32), jax.ShapeDtypeStruct((D_MODEL, 2 * MEM_WIDTH), F32),
                   jax.ShapeDtypeStruct((1, HEAD_DIM), F32)],
        compiler_params=_cparams(),
    )(mem, wn, wkv, wkn, dk0, dv0, dk1, dv1)


def _memattn(q, wq, mk, mv):
    qn = _rms(q, wq) * QSCALE
    s = mmul(qn, mk, 1, 1, False)
    s = s - jnp.max(s, axis=-1, keepdims=True)
    p = jnp.exp(s)
    p = p / jnp.sum(p, axis=-1, keepdims=True)
    return mmul(p, mv, 1, 0, False)


def _lanes(j):
    return slice(j * HEAD_DIM, (j + 1) * HEAD_DIM)


def _memattn_specs(t):
    qspec = pl.BlockSpec((ROWS, MEM_WIDTH), lambda i: (i, (TAIL - MEM_WIDTH) // MEM_WIDTH))
    wspec = pl.BlockSpec((1, HEAD_DIM), lambda i: (0, 0))
    mspec = pl.BlockSpec((N_MEM, MEM_WIDTH), lambda i: (0, 0))
    ospec = pl.BlockSpec((ROWS, MEM_WIDTH), lambda i: (i, 0))
    return qspec, wspec, mspec, ospec


def memattn_fwd(proj, wq, mk, mv, *, name):
    t = proj.shape[0]
    qspec, wspec, mspec, ospec = _memattn_specs(t)

    def body(q_ref, w_ref, k_ref, v_ref, o_ref):
        for h in range(MEM_HEADS):
            o_ref[:, _lanes(h)] = _memattn(q_ref[:, _lanes(h)], w_ref[...], k_ref[:, _lanes(h)],
                                           v_ref[:, _lanes(h)]).astype(o_ref.dtype)

    return pl.pallas_call(
        body, name=name, grid=(t // ROWS,), in_specs=[qspec, wspec, mspec, mspec], out_specs=ospec,
        out_shape=jax.ShapeDtypeStruct((t, MEM_WIDTH), MM), compiler_params=_cparams(("parallel",)),
    )(proj, wq, mk, mv)


def memattn_bwd(proj, wq, mk, mv, dcat, *, name):
    t = proj.shape[0]
    qspec, wspec, mspec, ospec = _memattn_specs(t)
    dospec = pl.BlockSpec((ROWS, MEM_WIDTH), lambda i: (i, D_MODEL // MEM_WIDTH))

    def body(q_ref, w_ref, k_ref, v_ref, do_ref, dq_ref, dw_ref, dk_ref, dv_ref):
        first = pl.program_id(0) == 0
        dw_sum = jnp.zeros((1, HEAD_DIM), F32)
        for h in range(MEM_HEADS):
            _, vjp = jax.vjp(_memattn, q_ref[:, _lanes(h)], w_ref[...], k_ref[:, _lanes(h)], v_ref[:, _lanes(h)])
            dq, dw, dk, dv = vjp(do_ref[:, _lanes(h)].astype(F32))
            dq_ref[:, _lanes(h)] = dq.astype(dq_ref.dtype)
            dw_sum = dw_sum + dw
            _acc(dk_ref.at[:, _lanes(h)], dk, first)
            _acc(dv_ref.at[:, _lanes(h)], dv, first)
        _acc(dw_ref, dw_sum, first)

    mshape = jax.ShapeDtypeStruct((N_MEM, MEM_WIDTH), F32)
    return pl.pallas_call(
        body, name=name, grid=(t // ROWS,), in_specs=[qspec, wspec, mspec, mspec, dospec],
        out_specs=[ospec, wspec, mspec, mspec],
        out_shape=[jax.ShapeDtypeStruct((t, MEM_WIDTH), MM), jax.ShapeDtypeStruct((1, HEAD_DIM), F32), mshape, mshape],
        compiler_params=_cparams(("arbitrary",)),
    )(proj, wq, mk, mv, dcat)


def _shift_rows(x, s, up):
    n = x.shape[0]
    r = lax.broadcasted_iota(jnp.int32, x.shape, 0)
    if up:
        return jnp.where(r < n - s, pltpu.roll(x, n - s, 0), 0.0)
    return jnp.where(r >= s, pltpu.roll(x, s, 0), 0.0)


def _conv_fwd_vals(x, w):
    xb = _bf_round(x)
    wb = _bf_round(w)
    c = xb * wb[3:4, :]
    for j in range(3):
        c = c + _shift_rows(xb, 3 - j, False) * wb[j:j + 1, :]
    return xb, wb, c


def dn_prep_fwd(proj, conv_w):
    t = proj.shape[0]

    def body(x_ref, w_ref, o_ref):
        j = pl.program_id(0)
        _, _, c = _conv_fwd_vals(x_ref[...], w_ref[...])
        s = _silu(c)
        r = lax.rsqrt(jnp.sum(s * s, axis=-1, keepdims=True) + EPS)
        scale = jnp.where(j < N_HEADS, QSCALE, 1.0)
        o_ref[...] = jnp.where(j < 2 * N_HEADS, s * r * scale, s)

    return pl.pallas_call(
        body, name="dn_prep_fwd", grid=(3 * N_HEADS,),
        in_specs=[pl.BlockSpec((t, HEAD_DIM), lambda j: (0, j)), pl.BlockSpec((4, HEAD_DIM), lambda j: (0, j))],
        out_specs=pl.BlockSpec((None, t, HEAD_DIM), lambda j: (j // N_HEADS, 0, j % N_HEADS)),
        out_shape=jax.ShapeDtypeStruct((3, t, D_MODEL), F32), compiler_params=_cparams(("parallel",)),
    )(proj, conv_w)


def dn_prep_bwd(proj, conv_w, dqkv):
    t = proj.shape[0]

    def body(x_ref, w_ref, g_ref, dx_ref, dw_ref):
        j = pl.program_id(0)
        xb, wb, c = _conv_fwd_vals(x_ref[...], w_ref[...])
        sg = _sigmoid(c)
        s = c * sg
        g = g_ref[...]
        r = lax.rsqrt(jnp.sum(s * s, axis=-1, keepdims=True) + EPS)
        scale = jnp.where(j < N_HEADS, QSCALE, 1.0)
        gn = g * scale
        ds_norm = r * gn - s * (r * r * r) * jnp.sum(gn * s, axis=-1, keepdims=True)
        ds = jnp.where(j < 2 * N_HEADS, ds_norm, g)
        dc = ds * (sg + s * (1.0 - sg))
        dx = dc * wb[3:4, :]
        rows = [jnp.sum(dc * xb, axis=0, keepdims=True)]
        for jj in range(2, -1, -1):
            sh = 3 - jj
            dx = dx + _shift_rows(dc, sh, True) * wb[jj:jj + 1, :]
            rows.insert(0, jnp.sum(dc * _shift_rows(xb, sh, False), axis=0, keepdims=True))
        dx_ref[...] = dx.astype(dx_ref.dtype)
        dw_ref[...] = jnp.concatenate(rows + [jnp.zeros((4, HEAD_DIM), F32)], axis=0)

    col = pl.BlockSpec((t, HEAD_DIM), lambda j: (0, j))
    return pl.pallas_call(
        body, name="dn_prep_bwd", grid=(3 * N_HEADS,),
        in_specs=[col, pl.BlockSpec((4, HEAD_DIM), lambda j: (0, j)),
                  pl.BlockSpec((None, t, HEAD_DIM), lambda j: (j // N_HEADS, 0, j % N_HEADS))],
        out_specs=[col, pl.BlockSpec((8, HEAD_DIM), lambda j: (0, j))],
        out_shape=[jax.ShapeDtypeStruct((t, 3 * D_MODEL), MM), jax.ShapeDtypeStruct((8, 3 * D_MODEL), F32)],
        compiler_params=_cparams(("parallel",)),
    )(proj, conv_w, dqkv)


def _tri_ones(n, upper):
    r, c = _iota2(n, n)
    return (r <= c).astype(F32) if upper else (r >= c).astype(F32)


def dn_gates_fwd(proj, a_log, dt_bias):
    t = proj.shape[0]

    def body(x_ref, al_ref, dt_ref, o_ref):
        lane = lax.broadcasted_iota(jnp.int32, (CHUNK, HEAD_DIM), 1)
        tri = _tri_ones(CHUNK, False)

        def step(c, carry):
            rows = pl.ds(pl.multiple_of(c * CHUNK, CHUNK), CHUNK)
            x = x_ref[rows, :]
            g = jnp.where(lane < N_HEADS, -jnp.exp(al_ref[...]) * _softplus(x + dt_ref[...]), 0.0)
            gc = _dot(tri, g, 1, 0, True)
            o_ref[rows, :] = jnp.where(lane < N_HEADS, gc, jnp.where(lane < 2 * N_HEADS, _sigmoid(x), 0.0))
            return carry

        lax.fori_loop(0, t // CHUNK, step, 0)

    vec = pl.BlockSpec((1, HEAD_DIM), lambda i: (0, 0))
    return pl.pallas_call(
        body, name="dn_gates_fwd", grid=(1,),
        in_specs=[pl.BlockSpec((t, HEAD_DIM), lambda i: (0, TAIL_BLK)), vec, vec],
        out_specs=pl.BlockSpec((t, HEAD_DIM), lambda i: (0, 0)),
        out_shape=jax.ShapeDtypeStruct((t, HEAD_DIM), F32), compiler_params=_cparams(("arbitrary",)),
    )(proj, a_log, dt_bias)


def dn_gates_bwd(proj, a_log, dt_bias, dgates):
    t = proj.shape[0]

    def body(x_ref, al_ref, dt_ref, g_ref, dx_ref, dal_ref, ddt_ref):
        lane = lax.broadcasted_iota(jnp.int32, (CHUNK, HEAD_DIM), 1)
        tri = _tri_ones(CHUNK, True)
        dal_ref[...] = jnp.zeros_like(dal_ref)
        ddt_ref[...] = jnp.zeros_like(ddt_ref)

        def step(c, carry):
            rows = pl.ds(pl.multiple_of(c * CHUNK, CHUNK), CHUNK)
            x = x_ref[rows, :]
            dgc = jnp.where(lane < N_HEADS, g_ref[rows, :], 0.0)
            dg = _dot(tri, dgc, 1, 0, True)
            ea = -jnp.exp(al_ref[...])
            z = x + dt_ref[...]
            da = jnp.where(lane < N_HEADS, dg * ea * _sigmoid(z), 0.0)
            gval = jnp.where(lane < N_HEADS, ea * _softplus(z), 0.0)
            beta = _sigmoid(x)
            db = jnp.where(jnp.logical_and(lane >= N_HEADS, lane < 2 * N_HEADS), g_ref[rows, :] * beta * (1.0 - beta), 0.0)
            dx_ref[rows, :] = (da + db).astype(dx_ref.dtype)
            dal_ref[...] += jnp.sum(dg * gval, axis=0, keepdims=True)
            ddt_ref[...] += jnp.sum(da, axis=0, keepdims=True)
            return carry

        lax.fori_loop(0, t // CHUNK, step, 0)

    vec = pl.BlockSpec((1, HEAD_DIM), lambda i: (0, 0))
    full = pl.BlockSpec((t, HEAD_DIM), lambda i: (0, 0))
    return pl.pallas_call(
        body, name="dn_gates_bwd", grid=(1,),
        in_specs=[pl.BlockSpec((t, HEAD_DIM), lambda i: (0, TAIL_BLK)), vec, vec, full],
        out_specs=[full, vec, vec],
        out_shape=[jax.ShapeDtypeStruct((t, HEAD_DIM), MM), jax.ShapeDtypeStruct((1, HEAD_DIM), F32),
                   jax.ShapeDtypeStruct((1, HEAD_DIM), F32)],
        compiler_params=_cparams(("arbitrary",)),
    )(proj, a_log, dt_bias, dgates)


def _dn_intra(q, k, v, gcol, grow, bcol):
    r, c = _iota2(CHUNK, CHUNK)
    causal, strict = r >= c, r > c
    decay = jnp.where(causal, jnp.exp(jnp.where(causal, gcol - grow, 0.0)), 0.0)
    kb = k * bcol
    a = jnp.where(strict, mmul(kb, k, 1, 1, False) * decay, 0.0)
    tm = tri_inv(a)
    u = mmul(tm, v * bcol, 1, 0, False)
    w = mmul(tm, kb * jnp.exp(gcol), 1, 0, False)
    qk = jnp.where(causal, mmul(q, k, 1, 1, False) * decay, 0.0)
    rr = lax.broadcasted_iota(jnp.int32, (CHUNK, 1), 0)
    g_last = jnp.sum(jnp.where(rr == CHUNK - 1, gcol, 0.0), axis=0, keepdims=True)
    return u, w, q * jnp.exp(gcol), k * jnp.exp(g_last - gcol), qk, jnp.exp(g_last)


def _dn_scan(u, w, qg, kd, qk, eg, state):
    v_new = u - mmul(w, state, 1, 0, False)
    out = mmul(qg, state, 1, 0, False) + mmul(qk, v_new, 1, 0, False)
    return out, state * eg + mmul(kd, v_new, 0, 0, False)


DN_HEADS_PER_STEP = 1
DN_GROUP = 8
DN_PARTS = ((CHUNK, HEAD_DIM),) * 4 + ((CHUNK, CHUNK), (1, 1))


def _dn_scratch(hb, nc):
    return [pltpu.VMEM((hb, nc) + shape, F32) for shape in DN_PARTS]


def _dn_part_specs(hb, nc):
    return [pl.BlockSpec((hb, nc) + shape, lambda h: (h, 0, 0, 0)) for shape in DN_PARTS]


def _dn_group(nc):
    return min(DN_GROUP, nc)


def _dn_group_args(refs, j, g, grp):
    q_ref, k_ref, v_ref, gc_ref, gr_ref, bc_ref = refs
    rows = pl.ds(pl.multiple_of(g * (grp * CHUNK), grp * CHUNK), grp * CHUNK)
    cs = pl.ds(g * grp, grp)
    split = lambda ref: ref[rows, _lanes(j)].reshape(grp, CHUNK, HEAD_DIM)
    return split(q_ref), split(k_ref), split(v_ref), gc_ref[j, cs], gr_ref[j, cs], bc_ref[j, cs]


def _dn_intra_all(refs, parts, hb, nc):
    grp = _dn_group(nc)

    def group(g, carry):
        cs = pl.ds(g * grp, grp)
        for j in range(hb):
            for part, val in zip(parts, jax.vmap(_dn_intra)(*_dn_group_args(refs, j, g, grp))):
                part[j, cs] = val
        return carry

    lax.fori_loop(0, nc // grp, group, 0)


def _dn_specs(t):
    nc, hb = t // CHUNK, DN_HEADS_PER_STEP
    head = lambda which: pl.BlockSpec((None, t, hb * HEAD_DIM), lambda h: (which, 0, h))
    flat = pl.BlockSpec((t, hb * HEAD_DIM), lambda h: (0, h))
    col = pl.BlockSpec((hb, nc, CHUNK, 1), lambda h: (h, 0, 0, 0))
    row = pl.BlockSpec((hb, nc, 1, CHUNK), lambda h: (h, 0, 0, 0))
    st = pl.BlockSpec((hb, nc, HEAD_DIM, HEAD_DIM), lambda h: (h, 0, 0, 0))
    return nc, hb, head, flat, col, row, st


def dn_core_fwd(qkv, gcol, grow, bcol):
    t = qkv.shape[1]
    nc, hb, head, flat, col, row, st = _dn_specs(t)

    def body(q_ref, k_ref, v_ref, gc_ref, gr_ref, bc_ref, o_ref, s_ref, *parts):
        _dn_intra_all((q_ref, k_ref, v_ref, gc_ref, gr_ref, bc_ref), parts, hb, nc)

        def step(c, states):
            rows = pl.ds(pl.multiple_of(c * CHUNK, CHUNK), CHUNK)
            new_states = []
            for j in range(hb):
                s_ref[j, c] = states[j]
                out, new_state = _dn_scan(*[part[j, c] for part in parts], states[j])
                o_ref[rows, _lanes(j)] = out
                new_states.append(new_state)
            return tuple(new_states)

        lax.fori_loop(0, nc, step, tuple(jnp.zeros((HEAD_DIM, HEAD_DIM), F32) for _ in range(hb)))

    outs = pl.pallas_call(
        body, name="dn_core_fwd", grid=(N_HEADS // hb,),
        in_specs=[head(0), head(1), head(2), col, row, col], out_specs=[flat, st] + _dn_part_specs(hb, nc),
        out_shape=[jax.ShapeDtypeStruct((t, D_MODEL), F32), jax.ShapeDtypeStruct((N_HEADS, nc, HEAD_DIM, HEAD_DIM), F32)]
        + [jax.ShapeDtypeStruct((N_HEADS, nc) + shape, F32) for shape in DN_PARTS],
        compiler_params=_cparams(("parallel",)),
    )(qkv, qkv, qkv, gcol, grow, bcol)
    return outs[0], outs[1], tuple(outs[2:])


def dn_core_bwd(qkv, gcol, grow, bcol, states, parts, do):
    t = qkv.shape[1]
    nc, hb, head, flat, col, row, st = _dn_specs(t)
    n_parts = len(DN_PARTS)

    def body(q_ref, k_ref, v_ref, gc_ref, gr_ref, bc_ref, s_ref, do_ref, *rest):
        parts, (dqkv_ref, dgc_ref, dgr_ref, dbc_ref), dparts = rest[:n_parts], rest[n_parts:n_parts + 4], rest[n_parts + 4:]
        refs = (q_ref, k_ref, v_ref, gc_ref, gr_ref, bc_ref)

        def step(i, dstates):
            c = nc - 1 - i
            rows = pl.ds(pl.multiple_of(c * CHUNK, CHUNK), CHUNK)
            dstates_in = []
            for j in range(hb):
                _, vjp = jax.vjp(_dn_scan, *[part[j, c] for part in parts], s_ref[j, c])
                *dvals, dstate_in = vjp((do_ref[rows, _lanes(j)], dstates[j]))
                for dpart, dval in zip(dparts, dvals):
                    dpart[j, c] = dval
                dstates_in.append(dstate_in)
            return tuple(dstates_in)

        lax.fori_loop(0, nc, step, tuple(jnp.zeros((HEAD_DIM, HEAD_DIM), F32) for _ in range(hb)))

        grp = _dn_group(nc)

        def group(g, carry):
            rows = pl.ds(pl.multiple_of(g * (grp * CHUNK), grp * CHUNK), grp * CHUNK)
            cs = pl.ds(g * grp, grp)
            for j in range(hb):
                _, vjp = jax.vjp(jax.vmap(_dn_intra), *_dn_group_args(refs, j, g, grp))
                dq, dk, dv, dgc, dgr, dbc = vjp(tuple(dpart[j, cs] for dpart in dparts))
                for which, val in enumerate((dq, dk, dv)):
                    dqkv_ref[which, rows, _lanes(j)] = val.reshape(grp * CHUNK, HEAD_DIM)
                dgc_ref[j, cs] = dgc
                dgr_ref[j, cs] = dgr
                dbc_ref[j, cs] = dbc
            return carry

        lax.fori_loop(0, nc // grp, group, 0)

    return pl.pallas_call(
        body, name="dn_core_bwd", grid=(N_HEADS // hb,), scratch_shapes=_dn_scratch(hb, nc),
        in_specs=[head(0), head(1), head(2), col, row, col, st, flat] + _dn_part_specs(hb, nc),
        out_specs=[pl.BlockSpec((3, t, hb * HEAD_DIM), lambda h: (0, 0, h)), col, row, col],
        out_shape=[jax.ShapeDtypeStruct((3, t, D_MODEL), F32)] + [
            jax.ShapeDtypeStruct((N_HEADS, nc, CHUNK, 1), F32), jax.ShapeDtypeStruct((N_HEADS, nc, 1, CHUNK), F32),
            jax.ShapeDtypeStruct((N_HEADS, nc, CHUNK, 1), F32)],
        compiler_params=_cparams(("parallel",)),
    )(qkv, qkv, qkv, gcol, grow, bcol, states, do, *parts)


def gates_to_heads(gates):
    t = gates.shape[0]
    nc = t // CHUNK
    g = gates[:, :N_HEADS].T.reshape(N_HEADS, nc, CHUNK)
    b = gates[:, N_HEADS:2 * N_HEADS].T.reshape(N_HEADS, nc, CHUNK)
    return g[..., None], g[:, :, None, :], b[..., None]


def heads_to_gates(dgcol, dgrow, dbcol):
    nh, nc = dgcol.shape[:2]
    dg = (dgcol[..., 0] + dgrow[:, :, 0, :]).reshape(nh, nc * CHUNK).T
    db = dbcol[..., 0].reshape(nh, nc * CHUNK).T
    return jnp.concatenate([dg, db, jnp.zeros((nc * CHUNK, HEAD_DIM - 2 * nh), F32)], axis=1)


def _dn_out(o, z, w):
    return _rms(o, w) * _silu(z)


def _gate_specs():
    o_spec = pl.BlockSpec((ROWS, D_MODEL), lambda i: (i, 0))
    z_spec = pl.BlockSpec((ROWS, D_MODEL), lambda i: (i, 3))
    w_spec = pl.BlockSpec((1, HEAD_DIM), lambda i: (0, 0))
    return o_spec, z_spec, w_spec


def dn_out_fwd(o, proj, w):
    t = o.shape[0]
    o_spec, z_spec, w_spec = _gate_specs()

    def body(o_ref, z_ref, w_ref, y_ref):
        for h in range(N_HEADS):
            y_ref[:, _lanes(h)] = _dn_out(o_ref[:, _lanes(h)], z_ref[:, _lanes(h)], w_ref[...]).astype(y_ref.dtype)

    return pl.pallas_call(
        body, name="dn_out_fwd", grid=(t // ROWS,), in_specs=[o_spec, z_spec, w_spec], out_specs=o_spec,
        out_shape=jax.ShapeDtypeStruct((t, D_MODEL), MM), compiler_params=_cparams(("parallel",)),
    )(o, proj, w)


def dn_out_bwd(o, proj, w, dcat):
    t = o.shape[0]
    o_spec, z_spec, w_spec = _gate_specs()

    def body(o_ref, z_ref, w_ref, g_ref, do_ref, dz_ref, dw_ref):
        dw_sum = jnp.zeros((1, HEAD_DIM), F32)
        for h in range(N_HEADS):
            _, vjp = jax.vjp(_dn_out, o_ref[:, _lanes(h)], z_ref[:, _lanes(h)], w_ref[...])
            do, dz, dw = vjp(g_ref[:, _lanes(h)].astype(F32))
            do_ref[:, _lanes(h)] = do
            dz_ref[:, _lanes(h)] = dz.astype(dz_ref.dtype)
            dw_sum = dw_sum + dw
        _acc(dw_ref, dw_sum, pl.program_id(0) == 0)

    return pl.pallas_call(
        body, name="dn_out_bwd", grid=(t // ROWS,), in_specs=[o_spec, z_spec, w_spec, o_spec],
        out_specs=[o_spec, o_spec, w_spec],
        out_shape=[jax.ShapeDtypeStruct((t, D_MODEL), F32), jax.ShapeDtypeStruct((t, D_MODEL), MM),
                   jax.ShapeDtypeStruct((1, HEAD_DIM), F32)],
        compiler_params=_cparams(("arbitrary",)),
    )(o, proj, w, dcat)


def _fox_norm(x, w, scale):
    return _rms(x, w) * scale


def _fox_prep_specs():
    x_spec = pl.BlockSpec((ROWS, 2 * D_MODEL), lambda i: (i, 0))
    w_spec = pl.BlockSpec((2, 1, HEAD_DIM), lambda i: (0, 0, 0))
    y_spec = pl.BlockSpec((2, ROWS, D_MODEL), lambda i: (0, i, 0))
    return x_spec, w_spec, y_spec


def fox_prep_fwd(proj, wqk):
    t = proj.shape[0]
    x_spec, w_spec, y_spec = _fox_prep_specs()

    def body(x_ref, w_ref, y_ref):
        for j in range(2 * N_HEADS):
            which, scale = j // N_HEADS, (QSCALE if j < N_HEADS else 1.0)
            y_ref[which, :, _lanes(j % N_HEADS)] = _fox_norm(x_ref[:, _lanes(j)], w_ref[which], scale).astype(y_ref.dtype)

    return pl.pallas_call(
        body, name="fox_prep_fwd", grid=(t // ROWS,), in_specs=[x_spec, w_spec], out_specs=y_spec,
        out_shape=jax.ShapeDtypeStruct((2, t, D_MODEL), MM), compiler_params=_cparams(("parallel",)),
    )(proj, wqk)


def fox_prep_bwd(proj, wqk, dq, dk):
    t = proj.shape[0]
    x_spec, w_spec, _ = _fox_prep_specs()
    g_spec = pl.BlockSpec((ROWS, D_MODEL), lambda i: (i, 0))

    def body(x_ref, w_ref, dq_ref, dk_ref, dx_ref, dw_ref):
        dws = [jnp.zeros((1, HEAD_DIM), F32), jnp.zeros((1, HEAD_DIM), F32)]
        for j in range(2 * N_HEADS):
            which, scale = j // N_HEADS, (QSCALE if j < N_HEADS else 1.0)
            g_ref = dq_ref if which == 0 else dk_ref
            _, vjp = jax.vjp(lambda x, w: _fox_norm(x, w, scale), x_ref[:, _lanes(j)], w_ref[which])
            dx, dw = vjp(g_ref[:, _lanes(j % N_HEADS)])
            dx_ref[:, _lanes(j)] = dx.astype(dx_ref.dtype)
            dws[which] = dws[which] + dw
        first = pl.program_id(0) == 0
        _acc(dw_ref.at[0], dws[0], first)
        _acc(dw_ref.at[1], dws[1], first)

    return pl.pallas_call(
        body, name="fox_prep_bwd", grid=(t // ROWS,), in_specs=[x_spec, w_spec, g_spec, g_spec],
        out_specs=[x_spec, w_spec],
        out_shape=[jax.ShapeDtypeStruct((t, 2 * D_MODEL), MM), jax.ShapeDtypeStruct((2, 1, HEAD_DIM), F32)],
        compiler_params=_cparams(("arbitrary",)),
    )(proj, wqk, dq, dk)


def _row_pick(x, i):
    r = lax.broadcasted_iota(jnp.int32, x.shape, 0)
    return jnp.sum(jnp.where(r == i, x, 0.0), axis=0, keepdims=True)


def fox_gates_fwd(proj, f_bias):
    t = proj.shape[0]
    blk = HEAD_DIM

    def body(x_ref, b_ref, o_ref):
        lane = lax.broadcasted_iota(jnp.int32, (blk, HEAD_DIM), 1)
        tri = _tri_ones(blk, False)

        def step(c, carry):
            rows = pl.ds(pl.multiple_of(c * blk, blk), blk)
            lf = jnp.where(lane < N_HEADS, -_softplus(-(x_ref[rows, :] + b_ref[...])), 0.0)
            cum = _dot(tri, lf, 1, 0, True) + carry
            o_ref[rows, :] = cum
            return _row_pick(cum, blk - 1)

        lax.fori_loop(0, t // blk, step, jnp.zeros((1, HEAD_DIM), F32))

    vec = pl.BlockSpec((1, HEAD_DIM), lambda i: (0, 0))
    return pl.pallas_call(
        body, name="fox_gates_fwd", grid=(1,),
        in_specs=[pl.BlockSpec((t, HEAD_DIM), lambda i: (0, TAIL_BLK)), vec],
        out_specs=pl.BlockSpec((t, HEAD_DIM), lambda i: (0, 0)),
        out_shape=jax.ShapeDtypeStruct((t, HEAD_DIM), F32), compiler_params=_cparams(("arbitrary",)),
    )(proj, f_bias)


def fox_gates_bwd(proj, f_bias, dfcum):
    t = proj.shape[0]
    blk = HEAD_DIM
    nb = t // blk

    def body(x_ref, b_ref, g_ref, dx_ref, db_ref):
        lane = lax.broadcasted_iota(jnp.int32, (blk, HEAD_DIM), 1)
        tri = _tri_ones(blk, True)
        db_ref[...] = jnp.zeros_like(db_ref)

        def step(i, carry):
            c = nb - 1 - i
            rows = pl.ds(pl.multiple_of(c * blk, blk), blk)
            g = jnp.where(lane < N_HEADS, g_ref[rows, :], 0.0)
            dlf = _dot(tri, g, 1, 0, True) + carry
            dx = jnp.where(lane < N_HEADS, dlf * _sigmoid(-(x_ref[rows, :] + b_ref[...])), 0.0)
            dx_ref[rows, :] = dx.astype(dx_ref.dtype)
            db_ref[...] += jnp.sum(dx, axis=0, keepdims=True)
            return carry + jnp.sum(g, axis=0, keepdims=True)

        lax.fori_loop(0, nb, step, jnp.zeros((1, HEAD_DIM), F32))

    vec = pl.BlockSpec((1, HEAD_DIM), lambda i: (0, 0))
    full = pl.BlockSpec((t, HEAD_DIM), lambda i: (0, 0))
    return pl.pallas_call(
        body, name="fox_gates_bwd", grid=(1,),
        in_specs=[pl.BlockSpec((t, HEAD_DIM), lambda i: (0, TAIL_BLK)), vec, full], out_specs=[full, vec],
        out_shape=[jax.ShapeDtypeStruct((t, HEAD_DIM), MM), jax.ShapeDtypeStruct((1, HEAD_DIM), F32)],
        compiler_params=_cparams(("arbitrary",)),
    )(proj, f_bias, dfcum)


def fcum_to_heads(fcum):
    f = fcum[:, :N_HEADS].T
    return f[:, :, None], f[:, None, :]


def heads_to_fcum(dfcol, dfrow):
    d = (dfcol[:, :, 0] + dfrow[:, 0, :]).T
    return jnp.concatenate([d, jnp.zeros((d.shape[0], HEAD_DIM - N_HEADS), F32)], axis=1)


def _fox_tq(t):
    return min(t, 256)


def _fox_specs(t):
    tq = _fox_tq(t)
    q_spec = pl.BlockSpec((None, tq, HEAD_DIM), lambda h, i: (0, i, h))
    k_spec = pl.BlockSpec((None, t, HEAD_DIM), lambda h, i: (1, 0, h))
    v_spec = pl.BlockSpec((t, HEAD_DIM), lambda h, i: (0, 2 * N_HEADS + h))
    gate_spec = pl.BlockSpec((tq, HEAD_DIM), lambda h, i: (i, 3 * N_HEADS + h))
    col_spec = pl.BlockSpec((None, tq, 1), lambda h, i: (h, i, 0))
    row_spec = pl.BlockSpec((None, 1, t), lambda h, i: (h, 0, 0))
    blk_spec = pl.BlockSpec((tq, HEAD_DIM), lambda h, i: (i, h))
    head_spec = pl.BlockSpec((t, HEAD_DIM), lambda h, i: (0, h))
    return tq, q_spec, k_spec, v_spec, gate_spec, col_spec, row_spec, blk_spec, head_spec


def _fox_segments(i, tq):
    return ([(0, i * tq, False)] if i else []) + [(i * tq, (i + 1) * tq, True)]


def _fox_scores(q_ref, k_ref, fc_ref, fr_ref, lo, hi, causal):
    s = _dot(q_ref[...], k_ref[lo:hi, :], 1, 1, False) + (fc_ref[...] - fr_ref[:, lo:hi])
    if not causal:
        return s, None
    r, c = _iota2(hi - lo, hi - lo)
    return s, c <= r


def fox_attn_fwd(qk, proj, fcol, frow):
    t = proj.shape[0]
    tq, q_spec, k_spec, v_spec, gate_spec, col_spec, row_spec, blk_spec, _ = _fox_specs(t)

    def body(q_ref, k_ref, v_ref, gate_ref, fc_ref, fr_ref, mix_ref, o_ref, lse_ref):
        def block(i):
            segs = _fox_segments(i, tq)
            scores = [_fox_scores(q_ref, k_ref, fc_ref, fr_ref, *seg) for seg in segs]
            scores = [(s if mask is None else jnp.where(mask, s, -1e30), mask) for s, mask in scores]
            m = functools.reduce(jnp.maximum, [jnp.max(s, axis=-1, keepdims=True) for s, _ in scores])
            l, o = 0.0, 0.0
            for (lo, hi, _), (s, mask) in zip(segs, scores):
                p = jnp.exp(s - m)
                p = p if mask is None else jnp.where(mask, p, 0.0)
                l = l + jnp.sum(p, axis=-1, keepdims=True)
                o = o + _dot(p, v_ref[lo:hi, :], 1, 0, False)
            o = o / l
            o_ref[...] = o
            mix_ref[...] = (o * _sigmoid(gate_ref[...])).astype(mix_ref.dtype)
            lse_ref[...] = m + jnp.log(l)

        for i in range(t // tq):
            pl.when(pl.program_id(1) == i)(functools.partial(block, i))

    return pl.pallas_call(
        body, name="fox_attn_fwd", grid=(N_HEADS, t // tq),
        in_specs=[q_spec, k_spec, v_spec, gate_spec, col_spec, row_spec], out_specs=[blk_spec, blk_spec, col_spec],
        out_shape=[jax.ShapeDtypeStruct((t, D_MODEL), MM), jax.ShapeDtypeStruct((t, D_MODEL), F32),
                   jax.ShapeDtypeStruct((N_HEADS, t, 1), F32)],
        compiler_params=_cparams(("parallel", "parallel")),
    )(qk, qk, proj, proj, fcol, frow)


def fox_attn_bwd(qk, proj, fcol, frow, o, lse, dcat):
    t = proj.shape[0]
    tq, q_spec, k_spec, v_spec, gate_spec, col_spec, row_spec, blk_spec, head_spec = _fox_specs(t)

    def body(q_ref, k_ref, v_ref, gate_ref, fc_ref, fr_ref, o_ref, lse_ref, g_ref,
             dq_ref, dk_ref, dv_ref, dgate_ref, dfc_ref, dfr_ref):
        @pl.when(pl.program_id(1) == 0)
        def _():
            dk_ref[...] = jnp.zeros_like(dk_ref)
            dv_ref[...] = jnp.zeros_like(dv_ref)
            dfr_ref[...] = jnp.zeros_like(dfr_ref)

        def block(i):
            sg = _sigmoid(gate_ref[...])
            g = g_ref[...].astype(F32)
            o_pre = o_ref[...]
            do = g * sg
            dgate_ref[...] = (g * o_pre * sg * (1.0 - sg)).astype(dgate_ref.dtype)
            delta = jnp.sum(do * o_pre, axis=-1, keepdims=True)
            dq, dfc = 0.0, 0.0
            for lo, hi, causal in _fox_segments(i, tq):
                s, mask = _fox_scores(q_ref, k_ref, fc_ref, fr_ref, lo, hi, causal)
                if causal:
                    p = jnp.where(mask, jnp.exp(jnp.where(mask, s, 0.0) - lse_ref[...]), 0.0)
                else:
                    p = jnp.exp(s - lse_ref[...])
                ds = p * (_dot(do, v_ref[lo:hi, :], 1, 1, False) - delta)
                dq = dq + _dot(ds, k_ref[lo:hi, :], 1, 0, False)
                dk_ref[lo:hi, :] += _dot(ds, q_ref[...], 0, 0, False)
                dv_ref[lo:hi, :] += _dot(p, do, 0, 0, False)
                dfc = dfc + jnp.sum(ds, axis=-1, keepdims=True)
                dfr_ref[:, lo:hi] += -jnp.sum(ds, axis=0, keepdims=True)
            dq_ref[...] = dq
            dfc_ref[...] = dfc

        for i in range(t // tq):
            pl.when(pl.program_id(1) == i)(functools.partial(block, i))

    f32 = lambda *s: jax.ShapeDtypeStruct(s, F32)
    return pl.pallas_call(
        body, name="fox_attn_bwd", grid=(N_HEADS, t // tq),
        in_specs=[q_spec, k_spec, v_spec, gate_spec, col_spec, row_spec, blk_spec, col_spec, blk_spec],
        out_specs=[blk_spec, head_spec, head_spec, blk_spec, col_spec, row_spec],
        out_shape=[f32(t, D_MODEL), f32(t, D_MODEL), f32(t, D_MODEL), jax.ShapeDtypeStruct((t, D_MODEL), MM),
                   f32(N_HEADS, t, 1), f32(N_HEADS, 1, t)],
        compiler_params=_cparams(("parallel", "arbitrary")),
    )(qk, qk, proj, proj, fcol, frow, o, lse, dcat)


def adamw(w, g, m, v, *, name):
    r, c = w.shape
    rb = ROWS if r % ROWS == 0 else r

    def body(w_ref, g_ref, m_ref, v_ref, d_ref, nm_ref, nv_ref):
        g_ = g_ref[...]
        m_ = ADAM_B1 * m_ref[...] + (1.0 - ADAM_B1) * g_
        v_ = ADAM_B2 * v_ref[...] + (1.0 - ADAM_B2) * jnp.square(g_)
        m_hat = m_ / (1.0 - ADAM_B1 ** ADAM_STEP)
        v_hat = v_ / (1.0 - ADAM_B2 ** ADAM_STEP)
        d_ref[...] = -ADAM_LR * (m_hat / (jnp.sqrt(v_hat) + ADAM_EPS) + ADAM_WD * w_ref[...])
        nm_ref[...] = m_
        nv_ref[...] = v_

    blk = pl.BlockSpec((rb, c), lambda i: (i, 0))
    shp = jax.ShapeDtypeStruct((r, c), F32)
    return pl.pallas_call(body, name=name, grid=(r // rb,), in_specs=[blk] * 4, out_specs=[blk] * 3,
                          out_shape=[shp] * 3, compiler_params=_cparams(("parallel",)))(w, g, m, v)


def _place():
    x, y, c = lax.axis_index("x"), lax.axis_index("y"), lax.axis_index("c")
    return x, y, c, [(1 - x, y), (x, 1 - y), (1 - x, 1 - y)]


ANY = pl.BlockSpec(memory_space=pl.ANY)


def all_reduce_small(v):
    r, w = v.shape

    def body(v_ref, o_ref, buf, send_sems, recv_sems):
        x, y, c, _ = _place()
        me = 4 * x + 2 * y + c
        flip = lambda a, bit: 1 - a if bit else a
        cps = []
        for k in range(1, N_DEV):
            peer = (flip(x, k & 4), flip(y, k & 2), flip(c, k & 1))
            cp = pltpu.make_async_remote_copy(src_ref=v_ref, dst_ref=buf.at[me], send_sem=send_sems.at[k - 1],
                                              recv_sem=recv_sems.at[k - 1], device_id=peer, device_id_type=MESH)
            cp.start()
            cps.append((cp, 4 * peer[0] + 2 * peer[1] + peer[2]))
        buf[me] = v_ref[...]
        for k, (cp, peer_id) in enumerate(cps):
            pltpu.make_async_remote_copy(src_ref=v_ref, dst_ref=buf.at[peer_id], send_sem=send_sems.at[k],
                                         recv_sem=recv_sems.at[k], device_id=(x, y, c), device_id_type=MESH).wait_recv()
        for cp, _ in cps:
            cp.wait_send()
        acc = buf[0]
        for d in range(1, N_DEV):
            acc = acc + buf[d]
        o_ref[...] = acc

    vm = pl.BlockSpec(memory_space=pltpu.VMEM)
    return pl.pallas_call(
        body, name="all_reduce_small", in_specs=[vm], out_specs=vm, out_shape=jax.ShapeDtypeStruct((r, w), F32),
        scratch_shapes=[pltpu.VMEM((N_DEV, r, w), F32), pltpu.SemaphoreType.DMA((N_DEV - 1,)),
                        pltpu.SemaphoreType.DMA((N_DEV - 1,))],
    )(v)


def _after(x, token):
    first = (0,) * x.ndim
    patch = lax.dynamic_slice(x, first, (1,) * x.ndim) + token[0, 0].astype(x.dtype)
    return lax.dynamic_update_slice(x, patch, first)


def _vec8(v):
    return jnp.zeros((1, HEAD_DIM), F32).at[0, :N_HEADS].set(v.reshape(N_HEADS))


def _layer_fwd(i, x_in, wt, sm, mem_k, mem_v, late=None):
    tag = f"l{i}_"
    h = rms_fwd(x_in, sm["norm1_w"][i][None], name=tag + "rms1")
    w_in = wt["dn_w_in"] if i == 0 else wt["fox_w_in"]
    proj = matmul(h, w_in, name=tag + "proj", tm=256, tk=1024)
    sv = dict(x_in=x_in, h=h, proj=proj)
    if i == 0:
        qkv = dn_prep_fwd(proj, wt["conv_w"])
        gates = dn_gates_fwd(proj, _vec8(sm["dn_a_log"]), _vec8(sm["dn_dt_bias"]))
        gcol, grow, bcol = gates_to_heads(gates)
        o, states, parts = dn_core_fwd(qkv, gcol, grow, bcol)
        mix = dn_out_fwd(o, proj, sm["dn_o_norm_w"])
        sv.update(qkv=qkv, gcol=gcol, grow=grow, bcol=bcol, states=states, parts=parts, o=o)
    else:
        wqk = jnp.stack([sm["fox_q_norm_w"], sm["fox_k_norm_w"]])
        qk = fox_prep_fwd(proj, wqk)
        fcum = fox_gates_fwd(proj, _vec8(sm["fox_f_bias"]))
        fcol, frow = fcum_to_heads(fcum)
        mix, o, lse = fox_attn_fwd(qk, proj, fcol, frow)
        sv.update(wqk=wqk, qk=qk, fcol=fcol, frow=frow, o=o, lse=lse)
    mem_out = memattn_fwd(proj, sm["memq_norm_w"][i][None], mem_k, mem_v, name=tag + "memattn_fwd")
    cat = jnp.concatenate([mix, mem_out], axis=1)
    if late is not None:
        wt.update(late(cat))
    x_mid = matmul(cat, wt["w_out"], b_layer=i, res=x_in, name=tag + "out_proj")
    h2 = rms_fwd(x_mid, sm["norm2_w"][i][None], name=tag + "rms2")
    ff, act = matmul(h2, wt["w_mlp1"], b_layer=i, b_slots=True, also_sqrelu=True, out_dtype=MM, name=tag + "mlp1")
    x_out = matmul(act, wt["w_mlp2"], b_layer=i, res=x_mid, name=tag + "mlp2")
    sv.update(cat=cat, x_mid=x_mid, h2=h2, ff=ff, act=act)
    return x_out, sv


def _layer_bwd(i, dx_out, sv, wt, sm, mem_k, mem_v, on_mlp=None, on_core=None):
    tag = f"l{i}_"
    big, small = {}, {}
    dff = matmul(dx_out, wt["w_mlp2"], b_layer=i, tb=True, times_dsqrelu=sv["ff"], out_dtype=MM, name=tag + "d_ff")
    big["w_mlp2"] = matmul(sv["act"], dx_out, ta=True, name=tag + "d_w_mlp2", tk=2048)
    dh2 = matmul(dff, wt["w_mlp1"], b_layer=i, tb=True, b_slots=True, name=tag + "d_h2")
    big["w_mlp1"] = matmul(sv["h2"], dff, ta=True, name=tag + "d_w_mlp1", tm=512, tn=D_FF, tk=512)
    dx_mid, small["norm2_w"] = rms_bwd(sv["x_mid"], sm["norm2_w"][i][None], dh2, dx_out, name=tag + "rms2_bwd")
    dcat = matmul(dx_mid, wt["w_out"], b_layer=i, tb=True, name=tag + "d_cat")
    big["w_out"] = matmul(sv["cat"], dx_mid, ta=True, name=tag + "d_w_out", tk=2048)
    proj = sv["proj"]
    memq_norm_w = sm["memq_norm_w"][i][None]
    if on_mlp is not None:
        memq_norm_w = _after(memq_norm_w, on_mlp(big["w_mlp2"], big["w_mlp1"], big["w_out"]))
    dqm, small["memq_norm_w"], dmk, dmv = memattn_bwd(proj, memq_norm_w, mem_k, mem_v, dcat, name=tag + "memattn_bwd")
    t = proj.shape[0]
    pad = jnp.zeros((t, PROJ_W - TAIL - HEAD_DIM), MM)
    if i == 0:
        do, dz, small["dn_o_norm_w"] = dn_out_bwd(sv["o"], proj, sm["dn_o_norm_w"], dcat)
        bcol = sv["bcol"] if on_core is None else _after(sv["bcol"], on_core(do))
        dqkv, dgc, dgr, dbc = dn_core_bwd(sv["qkv"], sv["gcol"], sv["grow"], bcol, sv["states"], sv["parts"], do)
        dtail, dal, ddt = dn_gates_bwd(proj, _vec8(sm["dn_a_log"]), _vec8(sm["dn_dt_bias"]), heads_to_gates(dgc, dgr, dbc))
        dmain, dconv = dn_prep_bwd(proj, wt["conv_w"], dqkv)
        small["dn_a_log"], small["dn_dt_bias"] = dal[:, :N_HEADS], ddt[:, :N_HEADS]
        big["conv_w"] = dconv[:4]
        dproj = jnp.concatenate([dmain, dz, dqm, dtail, pad], axis=1)
    else:
        dq, dk, dv, dgate, dfc, dfr = fox_attn_bwd(sv["qk"], proj, sv["fcol"], sv["frow"], sv["o"], sv["lse"], dcat)
        dtail, dfb = fox_gates_bwd(proj, _vec8(sm["fox_f_bias"]), heads_to_fcum(dfc, dfr))
        dqk, dwqk = fox_prep_bwd(proj, sv["wqk"], dq, dk)
        small["fox_f_bias"] = dfb[:, :N_HEADS]
        small["fox_q_norm_w"], small["fox_k_norm_w"] = dwqk[0], dwqk[1]
        dproj = jnp.concatenate([dqk, dv.astype(MM), dgate, dqm, dtail, pad], axis=1)
    w_in = wt["dn_w_in"] if i == 0 else wt["fox_w_in"]
    dh = matmul(dproj, w_in, tb=True, name=tag + "d_h", tm=512)
    big["w_in"] = matmul(sv["h"], dproj, ta=True, name=tag + "d_w_in", tm=256)
    dx_in, small["norm1_w"] = rms_bwd(sv["x_in"], sm["norm1_w"][i][None], dh, dx_mid, name=tag + "rms1_bwd")
    return dx_in, big, small, (dmk, dmv)


def local_step(x, mem, target, wt, sm, late=None, on_layer1=None, on_mlp0=None, on_core0=None):
    wt = dict(wt)
    mem_k, mem_v = mem_fwd(mem, sm["mem_norm_w"][None], wt["w_mem_kv"], sm["mem_k_norm_w"][None])
    x0, sv0 = _layer_fwd(0, x, wt, sm, mem_k, mem_v, late)
    x1, sv1 = _layer_fwd(1, x0, wt, sm, mem_k, mem_v)
    dy, loss = loss_fwd(x1, target, name="loss")
    dx1, big1, small1, dm1 = _layer_bwd(1, dy, sv1, wt, sm, mem_k, mem_v)
    if on_layer1 is not None:
        dx1 = _after(dx1, on_layer1(big1))
    dx0, big0, small0, dm0 = _layer_bwd(0, dx1, sv0, wt, sm, mem_k, mem_v, on_mlp0, on_core0)
    dwn, dwkv, dwkn = mem_bwd(mem, sm["mem_norm_w"][None], wt["w_mem_kv"], sm["mem_k_norm_w"][None], *dm0, *dm1)
    small = dict(mem_norm_w=dwn[0], mem_k_norm_w=dwkn[0],
                 norm1_w=jnp.concatenate([small0["norm1_w"], small1["norm1_w"]]),
                 norm2_w=jnp.concatenate([small0["norm2_w"], small1["norm2_w"]]),
                 memq_norm_w=jnp.concatenate([small0["memq_norm_w"], small1["memq_norm_w"]]),
                 dn_a_log=small0["dn_a_log"], dn_dt_bias=small0["dn_dt_bias"], dn_o_norm_w=small0["dn_o_norm_w"],
                 fox_f_bias=small1["fox_f_bias"], fox_q_norm_w=small1["fox_q_norm_w"], fox_k_norm_w=small1["fox_k_norm_w"])
    big = dict(w_mem_kv=dwkv, dn_w_in=big0["w_in"], fox_w_in=big1["w_in"], conv_w=big0["conv_w"],
               w_out=[big0["w_out"], big1["w_out"]], w_mlp1=[big0["w_mlp1"], big1["w_mlp1"]],
               w_mlp2=[big0["w_mlp2"], big1["w_mlp2"]])
    return loss, dx0, big, small


def w_in_slots_to_kernel(slots, n_scalars):
    c = slots.shape[2]
    cut = 4096 - 3 * c
    pad = jnp.zeros((slots.shape[1], PROJ_W - TAIL - n_scalars), slots.dtype)
    return jnp.concatenate([slots[0], slots[1], slots[2], slots[3, :, :cut], slots[3, :, cut + n_scalars:],
                            slots[3, :, cut:cut + n_scalars], pad], axis=1)


def w_in_kernel_to_slots(w, n_scalars):
    c = (4096 + n_scalars + MEM_WIDTH) // N_CHIP
    last = jnp.concatenate([w[:, 3 * c:4096], w[:, TAIL:TAIL + n_scalars], w[:, 4096:TAIL]], axis=1)
    return jnp.stack([w[:, :c], w[:, c:2 * c], w[:, 2 * c:3 * c], last])


BIG_SPECS = dict(w_mem_kv=("rows", 1, 256, 1024), w_out=("rows", 2, 384, 1024), w_mlp2=("rows", 2, 1024, 1024),
                 w_mlp1=("cols", 2, 1024, 1024), dn_w_in=("rows", 1, 1024, 1156), fox_w_in=("rows", 1, 1024, 1154))
BIG_NAMES = tuple(BIG_SPECS)
EARLY_NAMES = ("w_mem_kv", "dn_w_in")
LATE_NAMES = ("w_out", "w_mlp2", "w_mlp1", "fox_w_in")
BIG_SPECS.update({f"{name}_{i}": (BIG_SPECS[name][0], 1) + BIG_SPECS[name][2:]
                  for name in ("w_out", "w_mlp2", "w_mlp1") for i in range(2)})
RS_LAYER1 = ("fox_w_in", "w_out_1", "w_mlp2_1", "w_mlp1_1")
RS_MLP0 = ("w_mlp2_0", "w_mlp1_0", "w_out_0")
RS_LAST = ("dn_w_in", "w_mem_kv")


def _full_shape(name, half=False):
    kind, a, b, c = BIG_SPECS[name]
    b = b // 2 if half else b
    return (a, N_CHIP, b, c) if kind == "rows" else (a, b, N_CHIP * c)


def _ds(start, size, align):
    return pl.ds(start if isinstance(start, int) else pl.multiple_of(start, align), size)


def _half_rows(name, h):
    b = BIG_SPECS[name][2]
    return _ds(h * (b // 2), b // 2, 16)


def _shard_idx(name, h):
    return (slice(None), _half_rows(name, h), slice(None))


def _full_idx(name, j=None, h=None):
    kind, _, _, c = BIG_SPECS[name]
    rows = slice(None) if h is None else _half_rows(name, h)
    if kind == "rows":
        return (slice(None), slice(None) if j is None else j, rows, slice(None))
    return (slice(None), rows, slice(None) if j is None else _ds(j * c, c, 128))


def _slots_shape(name):
    _, a, b, c = BIG_SPECS[name]
    return (a, N_CHIP, b, c)


def _slots_idx(name, j, h):
    return (slice(None), j, _half_rows(name, h), slice(None))


def _row_block(name):
    hs = BIG_SPECS[name][2] // 2
    return hs if hs <= ROWS else ROWS


def _remote(src, dst, send_sem, recv_sem, to):
    return pltpu.make_async_remote_copy(src_ref=src, dst_ref=dst, send_sem=send_sem, recv_sem=recv_sem, device_id=to,
                                        device_id_type=MESH)


HBM = pl.BlockSpec(memory_space=pltpu.HBM)
SEM = pl.BlockSpec(memory_space=pltpu.SEMAPHORE)
EFFECT = pltpu.CompilerParams(has_side_effects=pltpu.SideEffectType.DATAFLOW_SIDE_EFFECTING)


def _in_hbm(a):
    return pltpu.with_memory_space_constraint(a, pltpu.HBM)


def _chip_copies(names, ins, lands, send_sems, recv_sems):
    x, y, c, chips = _place()
    return [_remote(ins[a].at[_shard_idx(name, c)], lands[a].at[_slots_idx(name, 2 * x + y, c)], send_sems.at[3 * a + k],
                    recv_sems.at[3 * a + k], (chip[0], chip[1], c))
            for a, name in enumerate(names) for k, chip in enumerate(chips)]


def _copies_start(call_name, copies, sources, land_shapes, per_source=3, land_dtype=MM, after=()):
    n, n_after = len(sources), len(after)

    def body(*refs):
        ins, lands, token = refs[:n], refs[n:2 * n], refs[-1]
        send_sems, recv_sems = refs[2 * n + n_after], refs[2 * n + n_after + 1]
        for cp in copies(ins, lands, send_sems, recv_sems):
            cp.start()
        token[...] = jnp.zeros_like(token)

    ins = [_in_hbm(a) for a in sources]
    lands = [_in_hbm(lax.empty(shape, land_dtype)) for shape in land_shapes]
    sems = (pltpu.SemaphoreType.DMA((per_source * n,)), pltpu.SemaphoreType.DMA((per_source * n,)))
    outs = pl.pallas_call(
        body, name=call_name, in_specs=[HBM] * (2 * n) + [ANY] * n_after,
        out_specs=(SEM, SEM) + (HBM,) * (2 * n) + (pl.BlockSpec(memory_space=pltpu.VMEM),),
        out_shape=sems + tuple(pltpu.HBM(a.shape, a.dtype) for a in ins + lands) + (jax.ShapeDtypeStruct((8, HEAD_DIM), F32),),
        input_output_aliases={a: 2 + a for a in range(2 * n)}, compiler_params=EFFECT,
    )(*ins, *lands, *after)
    return outs[:-1], outs[-1]


def _copies_wait(call_name, copies, state, after):
    n = (len(state) - 2) // 2

    def body(*refs):
        send_sems, recv_sems, ins, lands = refs[0], refs[1], refs[2:2 + n], refs[2 + n:2 + 2 * n]
        for cp in copies(ins, lands, send_sems, recv_sems):
            cp.wait_send()
            cp.wait_recv()

    outs = pl.pallas_call(
        body, name=call_name, in_specs=[SEM, SEM] + [HBM] * (2 * n) + [ANY], out_specs=(HBM,) * (2 * n),
        out_shape=tuple(pltpu.HBM(a.shape, a.dtype) for a in state[2:]),
        input_output_aliases={2 + a: a for a in range(2 * n)}, compiler_params=EFFECT,
    )(*state, after)
    return outs[:n], outs[n:]


def all_gather_start(shards, names, after):
    return _copies_start("all_gather_start", functools.partial(_chip_copies, names), [shards[name] for name in names],
                         [_slots_shape(name) for name in names], after=after)


def all_gather_wait(state, names, after):
    ins, lands = _copies_wait("all_gather_wait", functools.partial(_chip_copies, names), state, after)
    return dict(zip(names, ins)), dict(zip(names, lands))


def _chip_sends(names, ins, lands, send_sems, recv_sems):
    x, y, c, chips = _place()
    return [_remote(ins[a].at[_full_idx(name, 2 * chip[0] + chip[1])], lands[a].at[k], send_sems.at[3 * a + k],
                    recv_sems.at[3 * a + k], (chip[0], chip[1], c))
            for a, name in enumerate(names) for k, chip in enumerate(chips)]


def _got_shape(name):
    _, a_, b_, c_ = BIG_SPECS[name]
    return (3, a_, b_ // 2, c_)


def rs_chip_start(pairs, names, tag):
    return _copies_start("rs_chip_start_" + tag, functools.partial(_chip_sends, names), [pairs[name] for name in names],
                         [_got_shape(name) for name in names])


def rs_chip_wait(state, names, tag, after):
    _, lands = _copies_wait("rs_chip_wait_" + tag, functools.partial(_chip_sends, names), state, after)
    return dict(zip(names, lands))


def all_gather_pass_on(lands, names):
    n = len(names)

    def body(*refs):
        outs, send_sems, recv_sems = refs[n:2 * n], refs[2 * n], refs[2 * n + 1]
        x, y, c, chips = _place()
        work = [(3 * a + k, a, name, 2 * chip[0] + chip[1]) for a, name in enumerate(names) for k, chip in enumerate(chips)]
        cps = []
        for s, a, name, slot in work:
            landed = outs[a].at[_slots_idx(name, slot, c)]
            cps.append(_remote(landed, landed, send_sems.at[s], recv_sems.at[s], (x, y, 1 - c)))
            cps[-1].start()
        for s, a, name, slot in work:
            passed = outs[a].at[_slots_idx(name, slot, 1 - c)]
            _remote(passed, passed, send_sems.at[s], recv_sems.at[s], (x, y, 1 - c)).wait_recv()
        for cp in cps:
            cp.wait_send()

    outs = pl.pallas_call(
        body, name="all_gather_pass_on", in_specs=[ANY] * n, out_specs=[ANY] * n,
        input_output_aliases={a: a for a in range(n)},
        out_shape=[jax.ShapeDtypeStruct(_slots_shape(name), MM) for name in names],
        scratch_shapes=[pltpu.SemaphoreType.DMA((3 * n,)), pltpu.SemaphoreType.DMA((3 * n,))],
    )(*[lands[name] for name in names])
    return dict(zip(names, outs))


def all_gather_big(shards, names):
    n = len(names)

    def body(*refs):
        ins, outs = refs[:n], refs[n:2 * n]
        send_sems, recv_sems, fsend_sems, frecv_sems = refs[2 * n:]
        x, y, c, chips = _place()
        me_chip, sibling = 2 * x + y, (x, y, 1 - c)
        work = [(3 * a + k, a, name, chip) for a, name in enumerate(names) for k, chip in enumerate(chips)]
        sends = []
        for s, a, name, chip in work:
            cp = _remote(ins[a].at[_shard_idx(name, c)], outs[a].at[_slots_idx(name, me_chip, c)], send_sems.at[s],
                         recv_sems.at[s], (chip[0], chip[1], c))
            cp.start()
            sends.append(cp)
        for s, a, name, chip in work:
            landed = outs[a].at[_slots_idx(name, 2 * chip[0] + chip[1], c)]
            _remote(landed, landed, send_sems.at[s], recv_sems.at[s], (chip[0], chip[1], c)).wait_recv()
            cp = _remote(landed, landed, fsend_sems.at[s], frecv_sems.at[s], sibling)
            cp.start()
            sends.append(cp)
        for s, a, name, chip in work:
            passed = outs[a].at[_slots_idx(name, 2 * chip[0] + chip[1], 1 - c)]
            _remote(passed, passed, fsend_sems.at[s], frecv_sems.at[s], sibling).wait_recv()
        for cp in sends:
            cp.wait_send()

    outs = pl.pallas_call(
        body, name="all_gather_big", in_specs=[ANY] * n, out_specs=[ANY] * n,
        out_shape=[jax.ShapeDtypeStruct(_slots_shape(name), MM) for name in names],
        scratch_shapes=[pltpu.SemaphoreType.DMA((3 * n,))] * 4,
    )(*[shards[name] for name in names])
    return dict(zip(names, outs))


def with_own_slot(name, full, shard, chip):
    return lax.dynamic_update_slice(full, shard[:, None], (0, chip, 0, 0))


def rs_pair_exchange_big(grads, names, tag, after):
    n = len(names)

    def body(*refs):
        ins, outs, send_sems, recv_sems = refs[:n], refs[n + 1:2 * n + 1], refs[2 * n + 1], refs[2 * n + 2]
        cps = _pair_sends(names, ins, outs, send_sems, recv_sems)
        for cp in cps:
            cp.start()
        for cp in cps:
            cp.wait()

    outs = pl.pallas_call(
        body, name="rs_pair_exchange_" + tag, in_specs=[ANY] * (n + 1), out_specs=[ANY] * n,
        out_shape=[jax.ShapeDtypeStruct(_full_shape(name, half=True), F32) for name in names],
        scratch_shapes=[pltpu.SemaphoreType.DMA((n,)), pltpu.SemaphoreType.DMA((n,))],
    )(*[grads[name] for name in names], after)
    return dict(zip(names, outs))


def rs_pair_add_big(name, place, g, got):
    kind, a_, b_, c_ = BIG_SPECS[name]
    rb = _row_block(name)
    nb = (b_ // 2) // rb

    def body(place_ref, g_ref, got_ref, o_ref):
        o_ref[...] = (g_ref[...] + got_ref[...]).astype(o_ref.dtype)

    if kind == "rows":
        g_spec = pl.BlockSpec((None, None, rb, c_), lambda a, j, i, p: (a, j, p[0] * nb + i, 0))
        o_spec = pl.BlockSpec((None, None, rb, c_), lambda a, j, i, p: (a, j, i, 0))
    else:
        g_spec = pl.BlockSpec((None, rb, c_), lambda a, j, i, p: (a, p[0] * nb + i, j))
        o_spec = pl.BlockSpec((None, rb, c_), lambda a, j, i, p: (a, i, j))
    return pl.pallas_call(
        body, name="rs_pair_add_" + name,
        grid_spec=pltpu.PrefetchScalarGridSpec(num_scalar_prefetch=1, grid=(a_, N_CHIP, nb), in_specs=[g_spec, o_spec],
                                               out_specs=o_spec),
        out_shape=jax.ShapeDtypeStruct(_full_shape(name, half=True), MM),
        compiler_params=_cparams(("parallel", "parallel", "parallel")),
    )(place, g, got)


def rs_chip_add_big(name, place, g, got_pair, got_chips):
    kind, a_, b_, c_ = BIG_SPECS[name]
    rb = _row_block(name)
    nb = (b_ // 2) // rb

    def body(place_ref, g_ref, s_ref, r0_ref, r1_ref, r2_ref, o_ref):
        own = g_ref[...] + s_ref[...]
        o_ref[...] = ((own + r0_ref[...].astype(F32)) + r1_ref[...].astype(F32)) + r2_ref[...].astype(F32)

    if kind == "rows":
        g_spec = pl.BlockSpec((None, None, rb, c_), lambda a, i, p: (a, p[1], p[0] * nb + i, 0))
        s_spec = pl.BlockSpec((None, None, rb, c_), lambda a, i, p: (a, p[1], i, 0))
    else:
        g_spec = pl.BlockSpec((None, rb, c_), lambda a, i, p: (a, p[0] * nb + i, p[1]))
        s_spec = pl.BlockSpec((None, rb, c_), lambda a, i, p: (a, i, p[1]))
    r_spec = lambda k: pl.BlockSpec((None, None, rb, c_), lambda a, i, p: (k, a, i, 0))
    return pl.pallas_call(
        body, name="rs_chip_add_" + name,
        grid_spec=pltpu.PrefetchScalarGridSpec(
            num_scalar_prefetch=1, grid=(a_, nb), in_specs=[g_spec, s_spec, r_spec(0), r_spec(1), r_spec(2)],
            out_specs=pl.BlockSpec((None, rb, c_), lambda a, i, p: (a, p[0] * nb + i, 0))),
        out_shape=jax.ShapeDtypeStruct((a_, b_, c_), F32), compiler_params=_cparams(("parallel", "parallel")),
    )(place, g, got_pair, got_chips, got_chips, got_chips)


def rs_pair_gather_big(halves, tag):
    names = tuple(halves)
    n = len(names)

    def body(*refs):
        outs, send_sems, recv_sems = refs[n:2 * n], refs[2 * n], refs[2 * n + 1]
        x, y, c, _ = _place()
        cps = []
        for a, name in enumerate(names):
            mine = outs[a].at[_shard_idx(name, c)]
            cp = _remote(mine, mine, send_sems.at[a], recv_sems.at[a], (x, y, 1 - c))
            cp.start()
            cps.append(cp)
        for a, name in enumerate(names):
            cps[a].wait_send()
            theirs = outs[a].at[_shard_idx(name, 1 - c)]
            _remote(theirs, theirs, send_sems.at[a], recv_sems.at[a], (x, y, 1 - c)).wait_recv()

    outs = pl.pallas_call(
        body, name="rs_pair_gather_" + tag, in_specs=[ANY] * n, out_specs=[ANY] * n,
        input_output_aliases={a: a for a in range(n)},
        out_shape=[jax.ShapeDtypeStruct(BIG_SPECS[name][1:], F32) for name in names],
        scratch_shapes=[pltpu.SemaphoreType.DMA((n,)), pltpu.SemaphoreType.DMA((n,))],
    )(*[halves[name] for name in names])
    return dict(zip(names, outs))


def _pair_sends(names, ins, lands, send_sems, recv_sems):
    x, y, c, _ = _place()
    return [_remote(ins[a].at[_full_idx(name, None, 1 - c)], lands[a], send_sems.at[a], recv_sems.at[a], (x, y, 1 - c))
            for a, name in enumerate(names)]


def rs_pair_start(grads, names, tag):
    return _copies_start("rs_pair_start_" + tag, functools.partial(_pair_sends, names), [grads[name] for name in names],
                         [_full_shape(name, half=True) for name in names], per_source=1, land_dtype=F32)


def rs_middle(pair_state, names, tag, place, after):
    ins, lands = _copies_wait("rs_pair_wait_" + tag, functools.partial(_pair_sends, names), pair_state, after)
    grads, got_pair = dict(zip(names, ins)), dict(zip(names, lands))
    pairs = {name: rs_pair_add_big(name, place, grads[name], got_pair[name]) for name in names}
    state, token = rs_chip_start(pairs, names, tag)
    return (grads, got_pair, state), token


def rs_end(begun, names, tag, place, after):
    grads, got_pair, state = begun
    got_chips = rs_chip_wait(state, names, tag, after)
    return {name: rs_chip_add_big(name, place, grads[name], got_pair[name], got_chips[name]) for name in names}


PACK_W = 1024
SMALL =(("mem_norm_w", 1024), ("mem_k_norm_w", 128), ("norm1_w", 2048), ("dn_a_log", 8), ("dn_dt_bias", 8),
         ("dn_o_norm_w", 128), ("fox_f_bias", 8), ("fox_q_norm_w", 128), ("fox_k_norm_w", 128), ("memq_norm_w", 256),
         ("norm2_w", 2048))
SMALL_ROWS = 8
CONV_ROWS = 4 * 3 * D_MODEL // PACK_W
LOSS_AT = sum(n for _, n in SMALL)


def pack_small(parts, extra=None):
    flat = [parts[name].astype(F32).reshape(-1) for name, _ in SMALL]
    used = LOSS_AT
    if extra is not None:
        flat.append(extra.reshape(1))
        used += 1
    flat.append(jnp.zeros((SMALL_ROWS * PACK_W - used,), F32))
    return jnp.concatenate(flat).reshape(SMALL_ROWS, PACK_W)


def unpack_small(packed, shapes):
    flat, out, at = packed.reshape(-1), {}, 0
    for name, n in SMALL:
        out[name] = flat[at:at + n].reshape(shapes[name])
        at += n
    return out


def _adam_all(w, g, m, v, name):
    shape = w.shape
    r2 = lambda a: a.reshape(-1, shape[-1])
    d, nm, nv = adamw(r2(w), r2(g), r2(m), r2(v), name=name)
    return d.reshape(shape), nm.reshape(shape), nv.reshape(shape)


WEIGHTS = ("mem_norm_w", "w_mem_kv", "mem_k_norm_w", "norm1_w", "dn_w_in", "dn_conv_w", "dn_a_log", "dn_dt_bias",
           "dn_o_norm_w", "fox_w_in", "fox_f_bias", "fox_q_norm_w", "fox_k_norm_w", "memq_norm_w", "w_out", "norm2_w",
           "w_mlp1", "w_mlp2")


def kernel(x, mem, mem_norm_w, w_mem_kv, mem_k_norm_w, norm1_w, dn_w_in, dn_conv_w, dn_a_log, dn_dt_bias, dn_o_norm_w, fox_w_in, fox_f_bias, fox_q_norm_w, fox_k_norm_w, memq_norm_w, w_out, norm2_w, w_mlp1, w_mlp2, loss_target, m_mem_norm_w, m_w_mem_kv, m_mem_k_norm_w, m_norm1_w, m_dn_w_in, m_dn_conv_w, m_dn_a_log, m_dn_dt_bias, m_dn_o_norm_w, m_fox_w_in, m_fox_f_bias, m_fox_q_norm_w, m_fox_k_norm_w, m_memq_norm_w, m_w_out, m_norm2_w, m_w_mlp1, m_w_mlp2, v_mem_norm_w, v_w_mem_kv, v_mem_k_norm_w, v_norm1_w, v_dn_w_in, v_dn_conv_w, v_dn_a_log, v_dn_dt_bias, v_dn_o_norm_w, v_fox_w_in, v_fox_f_bias, v_fox_q_norm_w, v_fox_k_norm_w, v_memq_norm_w, v_w_out, v_norm2_w, v_w_mlp1, v_w_mlp2):
    args = dict(locals())
    w = {n: args[n] for n in WEIGHTS}
    m = {n: args["m_" + n] for n in WEIGHTS}
    v = {n: args["v_" + n] for n in WEIGHTS}
    core, chip = lax.axis_index("c"), 2 * lax.axis_index("x") + lax.axis_index("y")
    place = jnp.stack([core, chip]).astype(jnp.int32)

    shards = {name: w[name].reshape(BIG_SPECS[name][1:]).astype(MM) for name in BIG_NAMES}
    w_in_full = lambda arr, n_scalars: w_in_slots_to_kernel(arr[0], n_scalars)
    early = {name: with_own_slot(name, arr, shards[name], chip)
             for name, arr in all_gather_big(shards, EARLY_NAMES).items()}
    conv_mine = jnp.where(core == 0, dn_conv_w[0], 0.0)
    conv_placed = lax.dynamic_update_slice(jnp.zeros((4, 3 * D_MODEL), F32), conv_mine, (0, 768 * chip))
    conv_full = all_reduce_small(jnp.pad(conv_placed.reshape(CONV_ROWS, PACK_W), ((0, 16 - CONV_ROWS), (0, 0))))
    late_state, token = all_gather_start(shards, LATE_NAMES, after=(early["w_mem_kv"], early["dn_w_in"], conv_full))
    wt = dict(w_mem_kv=_after(early["w_mem_kv"].reshape(D_MODEL, 2 * MEM_WIDTH), token),
              dn_w_in=w_in_full(early["dn_w_in"], 2 * N_HEADS), conv_w=conv_full[:CONV_ROWS].reshape(4, 3 * D_MODEL))

    def late(after):
        late_shards, lands = all_gather_wait(late_state, LATE_NAMES, after)
        full = {name: with_own_slot(name, arr, late_shards[name], chip)
                for name, arr in all_gather_pass_on(lands, LATE_NAMES).items()}
        return dict(fox_w_in=w_in_full(full["fox_w_in"], N_HEADS), w_out=full["w_out"].reshape(2, 3 * MEM_WIDTH, D_MODEL),
                    w_mlp1=full["w_mlp1"], w_mlp2=full["w_mlp2"].reshape(2, D_FF, D_MODEL))

    sm = dict(mem_norm_w=mem_norm_w, mem_k_norm_w=mem_k_norm_w, norm1_w=norm1_w, norm2_w=norm2_w, memq_norm_w=memq_norm_w,
              dn_a_log=dn_a_log[0], dn_dt_bias=dn_dt_bias[0], dn_o_norm_w=dn_o_norm_w, fox_f_bias=fox_f_bias[0],
              fox_q_norm_w=fox_q_norm_w, fox_k_norm_w=fox_k_norm_w)
    w_in_slots = lambda g, n_scalars: w_in_kernel_to_slots(g, n_scalars)[None]
    rows_view = lambda g, name: g.reshape(_full_shape(name))
    pair_started, begun = {}, {}

    def on_layer1(big1):
        grads1 = dict(fox_w_in=w_in_slots(big1["w_in"], N_HEADS), w_out_1=rows_view(big1["w_out"], "w_out_1"),
                      w_mlp2_1=rows_view(big1["w_mlp2"], "w_mlp2_1"), w_mlp1_1=big1["w_mlp1"][None])
        pair_started["layer1"], token = rs_pair_start(grads1, RS_LAYER1, "layer1")
        return token

    def on_mlp0(d_w_mlp2, d_w_mlp1, d_w_out):
        begun["layer1"], token1 = rs_middle(pair_started["layer1"], RS_LAYER1, "layer1", place, d_w_out)
        grads0 = dict(w_mlp2_0=rows_view(d_w_mlp2, "w_mlp2_0"), w_mlp1_0=d_w_mlp1[None],
                      w_out_0=rows_view(d_w_out, "w_out_0"))
        pair_started["mlp0"], token0 = rs_pair_start(grads0, RS_MLP0, "mlp0")
        return token1 + token0

    def on_core0(d_o):
        begun["mlp0"], token = rs_middle(pair_started["mlp0"], RS_MLP0, "mlp0", place, d_o)
        return token

    sm["norm1_w"] = _after(norm1_w, token)
    loss_part, dx, big, small = local_step(x[0], mem[0], loss_target[0], wt, sm, late, on_layer1, on_mlp0, on_core0)
    small_pack = jnp.concatenate([pack_small(small, loss_part[0, :1]), big["conv_w"].reshape(CONV_ROWS, PACK_W),
                                  jnp.zeros((24 - SMALL_ROWS - CONV_ROWS, PACK_W), F32)])
    small_all = all_reduce_small(small_pack)
    small_sum = small_all[:SMALL_ROWS]
    conv_sum = lax.dynamic_slice(small_all[SMALL_ROWS:SMALL_ROWS + CONV_ROWS].reshape(4, 3 * D_MODEL), (0, 768 * chip), (4, 768))
    loss = small_sum.reshape(-1)[LOSS_AT]
    halves = rs_end(begun["layer1"], RS_LAYER1, "layer1", place, small_all)
    halves.update(rs_end(begun["mlp0"], RS_MLP0, "mlp0", place, small_all))
    summed = rs_pair_gather_big(halves, "early")

    last = dict(dn_w_in=w_in_slots(big["dn_w_in"], 2 * N_HEADS), w_mem_kv=rows_view(big["w_mem_kv"], "w_mem_kv"))
    got_pair = rs_pair_exchange_big(last, RS_LAST, "last", after=summed["fox_w_in"])
    pairs = {name: rs_pair_add_big(name, place, last[name], got_pair[name]) for name in RS_LAST}
    last_state, token = rs_chip_start(pairs, RS_LAST, "last")
    summed = {name: _after(arr, token) for name, arr in summed.items()}

    big_sum = {"fox_w_in": summed["fox_w_in"]}
    big_sum.update({name: jnp.concatenate([summed[name + "_0"], summed[name + "_1"]]) for name in ("w_out", "w_mlp2", "w_mlp1")})
    grads = unpack_small(small_sum, {n: w[n].shape for n, _ in SMALL})
    grads.update({name: big_sum[name].reshape(w[name].shape) for name in big_sum}, dn_conv_w=conv_sum[None])
    delta, new_m, new_v = {}, {}, {}
    for n in ("fox_w_in", "w_out", "w_mlp1", "w_mlp2", "dn_conv_w"):
        delta[n], new_m[n], new_v[n] = _adam_all(w[n], grads[n], m[n], v[n], "adamw_" + n)

    got_chips = rs_chip_wait(last_state, RS_LAST, "last", delta["w_mlp2"])
    summed_last = rs_pair_gather_big({name: rs_chip_add_big(name, place, last[name], got_pair[name], got_chips[name])
                                      for name in RS_LAST}, "last")
    for n in RS_LAST:
        grads[n] = summed_last[n].reshape(w[n].shape)
        delta[n], new_m[n], new_v[n] = _adam_all(w[n], grads[n], m[n], v[n], "adamw_" + n)
    shapes = {n: w[n].shape for n, _ in SMALL}
    d_s, m_s, v_s = adamw(pack_small(w), small_sum, pack_small(m), pack_small(v), name="adamw_small")
    for out, packed in ((delta, d_s), (new_m, m_s), (new_v, v_s)):
        out.update(unpack_small(packed, shapes))
    return (loss, dx[None], *[grads[n] for n in WEIGHTS], *[delta[n] for n in WEIGHTS],
            *[new_m[n] for n in WEIGHTS], *[new_v[n] for n in WEIGHTS])
```

```python
import functools

import jax
import jax.numpy as jnp
from jax import lax
from jax.experimental import pallas as pl
from jax.experimental.pallas import tpu as pltpu

F32 = jnp.float32
MM = jnp.bfloat16
HI = lax.Precision.HIGHEST

D_MODEL = 1024
HEAD_DIM = 128
N_HEADS = 8
MEM_HEADS = 4
MEM_WIDTH = MEM_HEADS * HEAD_DIM
N_MEM = 256
D_FF = 4 * D_MODEL
CHUNK = 64
EPS = 1e-6
QSCALE = HEAD_DIM ** -0.5
PROJ_W = 4736
TAIL = 4608
TAIL_BLK = TAIL // HEAD_DIM
ROWS = 1024
ADAM_ROWS = 512
VMEM_LIMIT = 56 * 1024 * 1024

ADAM_LR = 0.001
ADAM_B1 = 0.9
ADAM_B2 = 0.999
ADAM_EPS = 1e-08
ADAM_WD = 0.01
ADAM_STEP = 10

N_DEV = 8
N_CHIP = 4
MESH = pl.DeviceIdType.MESH


def _cparams(sem=None):
    return pltpu.CompilerParams(dimension_semantics=sem, vmem_limit_bytes=VMEM_LIMIT)


def _dot(a, b, ca, cb, hi):
    dims = (((ca,), (cb,)), ((), ()))
    if hi:
        return lax.dot_general(a, b, dims, precision=HI, preferred_element_type=F32)
    return lax.dot_general(a.astype(MM), b.astype(MM), dims, preferred_element_type=F32)


@functools.partial(jax.custom_vjp, nondiff_argnums=(2, 3, 4))
def mmul(a, b, ca, cb, hi):
    return _dot(a, b, ca, cb, hi)


def _mmul_fwd(a, b, ca, cb, hi):
    return _dot(a, b, ca, cb, hi), (a, b)


def _mmul_bwd(ca, cb, hi, res, g):
    a, b = res
    if ca == 1:
        da = _dot(g, b, 1, 1, hi) if cb == 0 else _dot(g, b, 1, 0, hi)
    else:
        da = _dot(b, g, 1, 1, hi) if cb == 0 else _dot(b, g, 0, 1, hi)
    if cb == 0:
        db = _dot(a, g, 0, 0, hi) if ca == 1 else _dot(a, g, 1, 0, hi)
    else:
        db = _dot(g, a, 0, 0, hi) if ca == 1 else _dot(g, a, 0, 1, hi)
    return da.astype(a.dtype), db.astype(b.dtype)


mmul.defvjp(_mmul_fwd, _mmul_bwd)


def _iota2(n, m):
    return lax.broadcasted_iota(jnp.int32, (n, m), 0), lax.broadcasted_iota(jnp.int32, (n, m), 1)


def _same_block(r, c, shift):
    return lax.shift_right_logical(r, shift) == lax.shift_right_logical(c, shift)


def _split_bf16(x):
    hi = x.astype(jnp.bfloat16)
    return hi, (x - hi.astype(F32)).astype(jnp.bfloat16)


def _dot3(a, b, ca, cb):
    dims = (((ca,), (cb,)), ((), ()))
    (ah, al), (bh, bl) = _split_bf16(a), _split_bf16(b)
    d = lambda x, y: lax.dot_general(x, y, dims, preferred_element_type=F32)
    return d(ah, bh) + (d(ah, bl) + d(al, bh))


def _tri_inv_impl(a):
    n = a.shape[0]
    r, c = _iota2(n, n)
    eye = (r == c).astype(F32)
    b16, b32 = _same_block(r, c, 4), _same_block(r, c, 5)
    a0 = jnp.where(b16, a, 0.0)
    p = eye - a0
    b = _dot3(a0, a0, 1, 0)
    p = p + _dot3(p, b, 1, 0)
    b = _dot3(b, b, 1, 0)
    p = p + _dot3(p, b, 1, 0)
    b = _dot3(b, b, 1, 0)
    p = p + _dot3(p, b, 1, 0)
    a1 = jnp.where(jnp.logical_and(b32, jnp.logical_not(b16)), a, 0.0)
    p = p - _dot3(_dot3(p, a1, 1, 0), p, 1, 0)
    a2 = jnp.where(b32, 0.0, a)
    p = p - _dot3(_dot3(p, a2, 1, 0), p, 1, 0)
    return p


@jax.custom_vjp
def tri_inv(a):
    return _tri_inv_impl(a)


def _tri_inv_fwd(a):
    p = _tri_inv_impl(a)
    return p, p


def _tri_inv_bwd(p, g):
    return (-_dot3(_dot3(p, g, 0, 0), p, 1, 1),)


tri_inv.defvjp(_tri_inv_fwd, _tri_inv_bwd)


def _sigmoid(x):
    return 1.0 / (1.0 + jnp.exp(-x))


def _softplus(x):
    return jnp.maximum(x, 0.0) + jnp.log(1.0 + jnp.exp(-jnp.abs(x)))


def _silu(x):
    return x * _sigmoid(x)


def _rms(x, w):
    return x * lax.rsqrt(jnp.mean(x * x, axis=-1, keepdims=True) + EPS) * w


def _bf_round(x):
    return x.astype(MM).astype(F32)


def _acc(ref, val, first):
    @pl.when(first)
    def _():
        ref[...] = val

    @pl.when(jnp.logical_not(first))
    def _():
        ref[...] += val


def _tile(n, pref):
    if n % pref == 0:
        return pref
    return n


def matmul(a, b, *, ta=False, tb=False, b_slots=False, b_layer=None, res=None, also_sqrelu=False, times_dsqrelu=None,
           out_dtype=F32, name, tm=1024, tn=1024, tk=1024):
    m, k = (a.shape[1], a.shape[0]) if ta else a.shape
    b_shape = b.shape if b_layer is None else b.shape[1:]
    if b_slots:
        n = b_shape[1] if tb else N_CHIP * b_shape[2]
        assert (N_CHIP * b_shape[2] if tb else b_shape[1]) == k, (a.shape, b.shape, ta, tb)
        tn, tk = (tn, b_shape[2]) if tb else (b_shape[2], tk)
    else:
        n = b_shape[0] if tb else b_shape[1]
        assert (b_shape[1] if tb else b_shape[0]) == k, (a.shape, b.shape, ta, tb)
    tm, tn, tk = _tile(m, tm), _tile(n, tn), _tile(k, tk)
    nk = k // tk
    ca, cb = (0 if ta else 1), (1 if tb else 0)

    extra = tuple(e for e in (res, times_dsqrelu) if e is not None)
    assert len(extra) <= 1

    def body(a_ref, b_ref, *rest):
        e_ref = rest[0] if extra else None
        o_ref = rest[len(extra)]

        def finish(total):
            if res is not None:
                total = total + e_ref[...]
            if times_dsqrelu is not None:
                total = total * (2.0 * jnp.maximum(e_ref[...], 0.0))
            o_ref[...] = total.astype(o_ref.dtype)
            if also_sqrelu:
                rest[len(extra) + 1][...] = _sqrelu(total).astype(MM)

        if nk == 1:
            finish(_dot(a_ref[...], b_ref[...], ca, cb, False))
            return
        acc_ref, kk = rest[-1], pl.program_id(2)

        @pl.when(kk == 0)
        def _():
            acc_ref[...] = jnp.zeros_like(acc_ref)

        acc_ref[...] += _dot(a_ref[...], b_ref[...], ca, cb, False)

        @pl.when(kk == nk - 1)
        def _():
            finish(acc_ref[...])

    a_spec = pl.BlockSpec((tk, tm), lambda i, j, l: (l, i)) if ta else pl.BlockSpec((tm, tk), lambda i, j, l: (i, l))
    lead = () if b_layer is None else (b_layer,)
    if b_slots:
        b_block, b_index = ((None, tn, tk), lambda i, j, l: (l, j, 0)) if tb else ((None, tk, tn), lambda i, j, l: (j, l, 0))
    else:
        b_block, b_index = ((tn, tk), lambda i, j, l: (j, l)) if tb else ((tk, tn), lambda i, j, l: (l, j))
    b_spec = pl.BlockSpec((None,) * len(lead) + b_block, lambda i, j, l: lead + b_index(i, j, l))
    o_spec = pl.BlockSpec((tm, tn), lambda i, j, l: (i, j))
    out_shape = [jax.ShapeDtypeStruct((m, n), out_dtype)] + [jax.ShapeDtypeStruct((m, n), MM)] * also_sqrelu
    outs = pl.pallas_call(
        body, name=name, grid=(m // tm, n // tn, nk),
        in_specs=[a_spec, b_spec] + [o_spec] * len(extra), out_specs=[o_spec] * len(out_shape), out_shape=out_shape,
        scratch_shapes=[pltpu.VMEM((tm, tn), F32)] * (nk > 1),
        compiler_params=_cparams(("parallel", "parallel", "arbitrary")),
    )(a, b, *extra)
    return outs if also_sqrelu else outs[0]


def rms_fwd(x, w, *, name):
    t, d = x.shape

    def body(x_ref, w_ref, o_ref):
        o_ref[...] = _rms(x_ref[...], w_ref[...]).astype(o_ref.dtype)

    return pl.pallas_call(
        body, name=name, grid=(t // ROWS,),
        in_specs=[pl.BlockSpec((ROWS, d), lambda i: (i, 0)), pl.BlockSpec((1, d), lambda i: (0, 0))],
        out_specs=pl.BlockSpec((ROWS, d), lambda i: (i, 0)),
        out_shape=jax.ShapeDtypeStruct((t, d), MM), compiler_params=_cparams(("parallel",)),
    )(x, w)


def rms_bwd(x, w, dh, dres, *, name):
    t, d = x.shape

    def body(x_ref, w_ref, dh_ref, dr_ref, dx_ref, dw_ref):
        _, vjp = jax.vjp(_rms, x_ref[...], w_ref[...])
        dx, dw = vjp(dh_ref[...].astype(F32))
        dx_ref[...] = dx + dr_ref[...]
        _acc(dw_ref, dw, pl.program_id(0) == 0)

    row = pl.BlockSpec((ROWS, d), lambda i: (i, 0))
    vec = pl.BlockSpec((1, d), lambda i: (0, 0))
    return pl.pallas_call(
        body, name=name, grid=(t // ROWS,), in_specs=[row, vec, row, row], out_specs=[row, vec],
        out_shape=[jax.ShapeDtypeStruct((t, d), F32), jax.ShapeDtypeStruct((1, d), F32)],
        compiler_params=_cparams(("arbitrary",)),
    )(x, w, dh, dres)


def _sqrelu(x):
    return jnp.square(jnp.maximum(x, 0.0))


def loss_fwd(y, target, *, name):
    t, d = y.shape

    def body(y_ref, t_ref, dy_ref, l_ref):
        e = y_ref[...] - t_ref[...]
        dy_ref[...] = e * (1.0 / d)
        part = 0.5 * jnp.sum(jnp.sum(e * e, axis=-1, keepdims=True) * (1.0 / d), axis=0, keepdims=True)
        _acc(l_ref, jnp.broadcast_to(part, (1, HEAD_DIM)), pl.program_id(0) == 0)

    blk = pl.BlockSpec((ROWS, d), lambda i: (i, 0))
    return pl.pallas_call(
        body, name=name, grid=(t // ROWS,), in_specs=[blk, blk],
        out_specs=[blk, pl.BlockSpec((1, HEAD_DIM), lambda i: (0, 0))],
        out_shape=[jax.ShapeDtypeStruct((t, d), F32), jax.ShapeDtypeStruct((1, HEAD_DIM), F32)],
        compiler_params=_cparams(("arbitrary",)),
    )(y, target)


def _mem_kv(mem, wn, wkn, *ws):
    mn = _rms(mem, wn)
    outs = []
    for h in range(MEM_HEADS):
        outs.append(_rms(mmul(mn, ws[h], 1, 0, False), wkn))
    for h in range(MEM_HEADS):
        outs.append(mmul(mn, ws[MEM_HEADS + h], 1, 0, False))
    return tuple(outs)


def _w_cols(w_ref):
    return [w_ref[:, h * HEAD_DIM:(h + 1) * HEAD_DIM] for h in range(2 * MEM_HEADS)]


def mem_fwd(mem, wn, wkv, wkn):
    def body(mem_ref, wn_ref, w_ref, wkn_ref, k_ref, v_ref):
        outs = _mem_kv(mem_ref[...], wn_ref[...], wkn_ref[...], *_w_cols(w_ref))
        for h in range(MEM_HEADS):
            k_ref[:, h * HEAD_DIM:(h + 1) * HEAD_DIM] = outs[h]
            v_ref[:, h * HEAD_DIM:(h + 1) * HEAD_DIM] = outs[MEM_HEADS + h]

    shp = jax.ShapeDtypeStruct((mem.shape[0], MEM_WIDTH), F32)
    return pl.pallas_call(body, name="mem_fwd", out_shape=[shp, shp], compiler_params=_cparams())(mem, wn, wkv, wkn)


def mem_bwd(mem, wn, wkv, wkn, dk0, dv0, dk1, dv1):
    def body(mem_ref, wn_ref, w_ref, wkn_ref, dk0_ref, dv0_ref, dk1_ref, dv1_ref, dwn_ref, dw_ref, dwkn_ref):
        _, vjp = jax.vjp(lambda wn_, wkn_, *ws: _mem_kv(mem_ref[...], wn_, wkn_, *ws),
                         wn_ref[...], wkn_ref[...], *[w.astype(F32) for w in _w_cols(w_ref)])
        cols = lambda a, b: tuple(a[:, h * HEAD_DIM:(h + 1) * HEAD_DIM] + b[:, h * HEAD_DIM:(h + 1) * HEAD_DIM]
                                  for h in range(MEM_HEADS))
        cts = cols(dk0_ref, dk1_ref) + cols(dv0_ref, dv1_ref)
        grads = vjp(cts)
        dwn_ref[...] = grads[0]
        dwkn_ref[...] = grads[1]
        for h in range(2 * MEM_HEADS):
            dw_ref[:, h * HEAD_DIM:(h + 1) * HEAD_DIM] = grads[2 + h]

    return pl.pallas_call(
        body, name="mem_bwd",
        out_shape=[jax.ShapeDtypeStruct((1, D_MODEL), F32), jax.ShapeDtypeStruct((D_MODEL, 2 * MEM_WIDTH), F32),
                   jax.ShapeDtypeStruct((1, HEAD_DIM), F32)],
        compiler_params=_cparams(),
    )(mem, wn, wkv, wkn, dk0, dv0, dk1, dv1)


def _memattn(q, wq, mk, mv):
    qn = _rms(q, wq) * QSCALE
    s = mmul(qn, mk, 1, 1, False)
    s = s - jnp.max(s, axis=-1, keepdims=True)
    p = jnp.exp(s)
    p = p / jnp.sum(p, axis=-1, keepdims=True)
    return mmul(p, mv, 1, 0, False)


def _lanes(j):
    return slice(j * HEAD_DIM, (j + 1) * HEAD_DIM)


def _memattn_specs(t):
    qspec = pl.BlockSpec((ROWS, MEM_WIDTH), lambda i: (i, (TAIL - MEM_WIDTH) // MEM_WIDTH))
    wspec = pl.BlockSpec((1, HEAD_DIM), lambda i: (0, 0))
    mspec = pl.BlockSpec((N_MEM, MEM_WIDTH), lambda i: (0, 0))
    ospec = pl.BlockSpec((ROWS, MEM_WIDTH), lambda i: (i, 0))
    return qspec, wspec, mspec, ospec


def memattn_fwd(proj, wq, mk, mv, *, name):
    t = proj.shape[0]
    qspec, wspec, mspec, ospec = _memattn_specs(t)

    def body(q_ref, w_ref, k_ref, v_ref, o_ref):
        for h in range(MEM_HEADS):
            o_ref[:, _lanes(h)] = _memattn(q_ref[:, _lanes(h)], w_ref[...], k_ref[:, _lanes(h)],
                                           v_ref[:, _lanes(h)]).astype(o_ref.dtype)

    return pl.pallas_call(
        body, name=name, grid=(t // ROWS,), in_specs=[qspec, wspec, mspec, mspec], out_specs=ospec,
        out_shape=jax.ShapeDtypeStruct((t, MEM_WIDTH), MM), compiler_params=_cparams(("parallel",)),
    )(proj, wq, mk, mv)


def memattn_bwd(proj, wq, mk, mv, dcat, *, name):
    t = proj.shape[0]
    qspec, wspec, mspec, ospec = _memattn_specs(t)
    dospec = pl.BlockSpec((ROWS, MEM_WIDTH), lambda i: (i, D_MODEL // MEM_WIDTH))

    def body(q_ref, w_ref, k_ref, v_ref, do_ref, dq_ref, dw_ref, dk_ref, dv_ref):
        first = pl.program_id(0) == 0
        dw_sum = jnp.zeros((1, HEAD_DIM), F32)
        for h in range(MEM_HEADS):
            _, vjp = jax.vjp(_memattn, q_ref[:, _lanes(h)], w_ref[...], k_ref[:, _lanes(h)], v_ref[:, _lanes(h)])
            dq, dw, dk, dv = vjp(do_ref[:, _lanes(h)].astype(F32))
            dq_ref[:, _lanes(h)] = dq.astype(dq_ref.dtype)
            dw_sum = dw_sum + dw
            _acc(dk_ref.at[:, _lanes(h)], dk, first)
            _acc(dv_ref.at[:, _lanes(h)], dv, first)
        _acc(dw_ref, dw_sum, first)

    mshape = jax.ShapeDtypeStruct((N_MEM, MEM_WIDTH), F32)
    return pl.pallas_call(
        body, name=name, grid=(t // ROWS,), in_specs=[qspec, wspec, mspec, mspec, dospec],
        out_specs=[ospec, wspec, mspec, mspec],
        out_shape=[jax.ShapeDtypeStruct((t, MEM_WIDTH), MM), jax.ShapeDtypeStruct((1, HEAD_DIM), F32), mshape, mshape],
        compiler_params=_cparams(("arbitrary",)),
    )(proj, wq, mk, mv, dcat)


def _shift_rows(x, s, up):
    n = x.shape[0]
    r = lax.broadcasted_iota(jnp.int32, x.shape, 0)
    if up:
        return jnp.where(r < n - s, pltpu.roll(x, n - s, 0), 0.0)
    return jnp.where(r >= s, pltpu.roll(x, s, 0), 0.0)


def _conv_fwd_vals(x, w):
    xb = _bf_round(x)
    wb = _bf_round(w)
    c = xb * wb[3:4, :]
    for j in range(3):
        c = c + _shift_rows(xb, 3 - j, False) * wb[j:j + 1, :]
    return xb, wb, c


def dn_prep_fwd(proj, conv_w):
    t = proj.shape[0]

    def body(x_ref, w_ref, o_ref):
        j = pl.program_id(0)
        _, _, c = _conv_fwd_vals(x_ref[...], w_ref[...])
        s = _silu(c)
        r = lax.rsqrt(jnp.sum(s * s, axis=-1, keepdims=True) + EPS)
        scale = jnp.where(j < N_HEADS, QSCALE, 1.0)
        o_ref[...] = jnp.where(j < 2 * N_HEADS, s * r * scale, s)

    return pl.pallas_call(
        body, name="dn_prep_fwd", grid=(3 * N_HEADS,),
        in_specs=[pl.BlockSpec((t, HEAD_DIM), lambda j: (0, j)), pl.BlockSpec((4, HEAD_DIM), lambda j: (0, j))],
        out_specs=pl.BlockSpec((None, t, HEAD_DIM), lambda j: (j // N_HEADS, 0, j % N_HEADS)),
        out_shape=jax.ShapeDtypeStruct((3, t, D_MODEL), F32), compiler_params=_cparams(("parallel",)),
    )(proj, conv_w)


def dn_prep_bwd(proj, conv_w, dqkv):
    t = proj.shape[0]

    def body(x_ref, w_ref, g_ref, dx_ref, dw_ref):
        j = pl.program_id(0)
        xb, wb, c = _conv_fwd_vals(x_ref[...], w_ref[...])
        sg = _sigmoid(c)
        s = c * sg
        g = g_ref[...]
        r = lax.rsqrt(jnp.sum(s * s, axis=-1, keepdims=True) + EPS)
        scale = jnp.where(j < N_HEADS, QSCALE, 1.0)
        gn = g * scale
        ds_norm = r * gn - s * (r * r * r) * jnp.sum(gn * s, axis=-1, keepdims=True)
        ds = jnp.where(j < 2 * N_HEADS, ds_norm, g)
        dc = ds * (sg + s * (1.0 - sg))
        dx = dc * wb[3:4, :]
        rows = [jnp.sum(dc * xb, axis=0, keepdims=True)]
        for jj in range(2, -1, -1):
            sh = 3 - jj
            dx = dx + _shift_rows(dc, sh, True) * wb[jj:jj + 1, :]
            rows.insert(0, jnp.sum(dc * _shift_rows(xb, sh, False), axis=0, keepdims=True))
        dx_ref[...] = dx.astype(dx_ref.dtype)
        dw_ref[...] = jnp.concatenate(rows + [jnp.zeros((4, HEAD_DIM), F32)], axis=0)

    col = pl.BlockSpec((t, HEAD_DIM), lambda j: (0, j))
    return pl.pallas_call(
        body, name="dn_prep_bwd", grid=(3 * N_HEADS,),
        in_specs=[col, pl.BlockSpec((4, HEAD_DIM), lambda j: (0, j)),
                  pl.BlockSpec((None, t, HEAD_DIM), lambda j: (j // N_HEADS, 0, j % N_HEADS))],
        out_specs=[col, pl.BlockSpec((8, HEAD_DIM), lambda j: (0, j))],
        out_shape=[jax.ShapeDtypeStruct((t, 3 * D_MODEL), MM), jax.ShapeDtypeStruct((8, 3 * D_MODEL), F32)],
        compiler_params=_cparams(("parallel",)),
    )(proj, conv_w, dqkv)


def _tri_ones(n, upper):
    r, c = _iota2(n, n)
    return (r <= c).astype(F32) if upper else (r >= c).astype(F32)


def dn_gates_fwd(proj, a_log, dt_bias):
    t = proj.shape[0]

    def body(x_ref, al_ref, dt_ref, o_ref):
        lane = lax.broadcasted_iota(jnp.int32, (CHUNK, HEAD_DIM), 1)
        tri = _tri_ones(CHUNK, False)

        def step(c, carry):
            rows = pl.ds(pl.multiple_of(c * CHUNK, CHUNK), CHUNK)
            x = x_ref[rows, :]
            g = jnp.where(lane < N_HEADS, -jnp.exp(al_ref[...]) * _softplus(x + dt_ref[...]), 0.0)
            gc = _dot(tri, g, 1, 0, True)
            o_ref[rows, :] = jnp.where(lane < N_HEADS, gc, jnp.where(lane < 2 * N_HEADS, _sigmoid(x), 0.0))
            return carry

        lax.fori_loop(0, t // CHUNK, step, 0)

    vec = pl.BlockSpec((1, HEAD_DIM), lambda i: (0, 0))
    return pl.pallas_call(
        body, name="dn_gates_fwd", grid=(1,),
        in_specs=[pl.BlockSpec((t, HEAD_DIM), lambda i: (0, TAIL_BLK)), vec, vec],
        out_specs=pl.BlockSpec((t, HEAD_DIM), lambda i: (0, 0)),
        out_shape=jax.ShapeDtypeStruct((t, HEAD_DIM), F32), compiler_params=_cparams(("arbitrary",)),
    )(proj, a_log, dt_bias)


def dn_gates_bwd(proj, a_log, dt_bias, dgates):
    t = proj.shape[0]

    def body(x_ref, al_ref, dt_ref, g_ref, dx_ref, dal_ref, ddt_ref):
        lane = lax.broadcasted_iota(jnp.int32, (CHUNK, HEAD_DIM), 1)
        tri = _tri_ones(CHUNK, True)
        dal_ref[...] = jnp.zeros_like(dal_ref)
        ddt_ref[...] = jnp.zeros_like(ddt_ref)

        def step(c, carry):
            rows = pl.ds(pl.multiple_of(c * CHUNK, CHUNK), CHUNK)
            x = x_ref[rows, :]
            dgc = jnp.where(lane < N_HEADS, g_ref[rows, :], 0.0)
            dg = _dot(tri, dgc, 1, 0, True)
            ea = -jnp.exp(al_ref[...])
            z = x + dt_ref[...]
            da = jnp.where(lane < N_HEADS, dg * ea * _sigmoid(z), 0.0)
            gval = jnp.where(lane < N_HEADS, ea * _softplus(z), 0.0)
            beta = _sigmoid(x)
            db = jnp.where(jnp.logical_and(lane >= N_HEADS, lane < 2 * N_HEADS), g_ref[rows, :] * beta * (1.0 - beta), 0.0)
            dx_ref[rows, :] = (da + db).astype(dx_ref.dtype)
            dal_ref[...] += jnp.sum(dg * gval, axis=0, keepdims=True)
            ddt_ref[...] += jnp.sum(da, axis=0, keepdims=True)
            return carry

        lax.fori_loop(0, t // CHUNK, step, 0)

    vec = pl.BlockSpec((1, HEAD_DIM), lambda i: (0, 0))
    full = pl.BlockSpec((t, HEAD_DIM), lambda i: (0, 0))
    return pl.pallas_call(
        body, name="dn_gates_bwd", grid=(1,),
        in_specs=[pl.BlockSpec((t, HEAD_DIM), lambda i: (0, TAIL_BLK)), vec, vec, full],
        out_specs=[full, vec, vec],
        out_shape=[jax.ShapeDtypeStruct((t, HEAD_DIM), MM), jax.ShapeDtypeStruct((1, HEAD_DIM), F32),
                   jax.ShapeDtypeStruct((1, HEAD_DIM), F32)],
        compiler_params=_cparams(("arbitrary",)),
    )(proj, a_log, dt_bias, dgates)


def _dn_intra(q, k, v, gcol, grow, bcol):
    r, c = _iota2(CHUNK, CHUNK)
    causal, strict = r >= c, r > c
    decay = jnp.where(causal, jnp.exp(jnp.where(causal, gcol - grow, 0.0)), 0.0)
    kb = k * bcol
    a = jnp.where(strict, mmul(kb, k, 1, 1, False) * decay, 0.0)
    tm = tri_inv(a)
    u = mmul(tm, v * bcol, 1, 0, False)
    w = mmul(tm, kb * jnp.exp(gcol), 1, 0, False)
    qk = jnp.where(causal, mmul(q, k, 1, 1, False) * decay, 0.0)
    rr = lax.broadcasted_iota(jnp.int32, (CHUNK, 1), 0)
    g_last = jnp.sum(jnp.where(rr == CHUNK - 1, gcol, 0.0), axis=0, keepdims=True)
    return u, w, q * jnp.exp(gcol), k * jnp.exp(g_last - gcol), qk, jnp.exp(g_last)


def _dn_scan(u, w, qg, kd, qk, eg, state):
    v_new = u - mmul(w, state, 1, 0, False)
    out = mmul(qg, state, 1, 0, False) + mmul(qk, v_new, 1, 0, False)
    return out, state * eg + mmul(kd, v_new, 0, 0, False)


DN_HEADS_PER_STEP = 1
DN_GROUP = 8
DN_PARTS = ((CHUNK, HEAD_DIM),) * 4 + ((CHUNK, CHUNK), (1, 1))


def _dn_scratch(hb, nc):
    return [pltpu.VMEM((hb, nc) + shape, F32) for shape in DN_PARTS]


def _dn_part_specs(hb, nc):
    return [pl.BlockSpec((hb, nc) + shape, lambda h: (h, 0, 0, 0)) for shape in DN_PARTS]


def _dn_group(nc):
    return min(DN_GROUP, nc)


def _dn_group_args(refs, j, g, grp):
    q_ref, k_ref, v_ref, gc_ref, gr_ref, bc_ref = refs
    rows = pl.ds(pl.multiple_of(g * (grp * CHUNK), grp * CHUNK), grp * CHUNK)
    cs = pl.ds(g * grp, grp)
    split = lambda ref: ref[rows, _lanes(j)].reshape(grp, CHUNK, HEAD_DIM)
    return split(q_ref), split(k_ref), split(v_ref), gc_ref[j, cs], gr_ref[j, cs], bc_ref[j, cs]


def _dn_intra_all(refs, parts, hb, nc):
    grp = _dn_group(nc)

    def group(g, carry):
        cs = pl.ds(g * grp, grp)
        for j in range(hb):
            for part, val in zip(parts, jax.vmap(_dn_intra)(*_dn_group_args(refs, j, g, grp))):
                part[j, cs] = val
        return carry

    lax.fori_loop(0, nc // grp, group, 0)


def _dn_specs(t):
    nc, hb = t // CHUNK, DN_HEADS_PER_STEP
    head = lambda which: pl.BlockSpec((None, t, hb * HEAD_DIM), lambda h: (which, 0, h))
    flat = pl.BlockSpec((t, hb * HEAD_DIM), lambda h: (0, h))
    col = pl.BlockSpec((hb, nc, CHUNK, 1), lambda h: (h, 0, 0, 0))
    row = pl.BlockSpec((hb, nc, 1, CHUNK), lambda h: (h, 0, 0, 0))
    st = pl.BlockSpec((hb, nc, HEAD_DIM, HEAD_DIM), lambda h: (h, 0, 0, 0))
    return nc, hb, head, flat, col, row, st


def dn_core_fwd(qkv, gcol, grow, bcol):
    t = qkv.shape[1]
    nc, hb, head, flat, col, row, st = _dn_specs(t)

    def body(q_ref, k_ref, v_ref, gc_ref, gr_ref, bc_ref, o_ref, s_ref, *parts):
        _dn_intra_all((q_ref, k_ref, v_ref, gc_ref, gr_ref, bc_ref), parts, hb, nc)

        def step(c, states):
            rows = pl.ds(pl.multiple_of(c * CHUNK, CHUNK), CHUNK)
            new_states = []
            for j in range(hb):
                s_ref[j, c] = states[j]
                out, new_state = _dn_scan(*[part[j, c] for part in parts], states[j])
                o_ref[rows, _lanes(j)] = out
                new_states.append(new_state)
            return tuple(new_states)

        lax.fori_loop(0, nc, step, tuple(jnp.zeros((HEAD_DIM, HEAD_DIM), F32) for _ in range(hb)))

    outs = pl.pallas_call(
        body, name="dn_core_fwd", grid=(N_HEADS // hb,),
        in_specs=[head(0), head(1), head(2), col, row, col], out_specs=[flat, st] + _dn_part_specs(hb, nc),
        out_shape=[jax.ShapeDtypeStruct((t, D_MODEL), F32), jax.ShapeDtypeStruct((N_HEADS, nc, HEAD_DIM, HEAD_DIM), F32)]
        + [jax.ShapeDtypeStruct((N_HEADS, nc) + shape, F32) for shape in DN_PARTS],
        compiler_params=_cparams(("parallel",)),
    )(qkv, qkv, qkv, gcol, grow, bcol)
    return outs[0], outs[1], tuple(outs[2:])


def dn_core_bwd(qkv, gcol, grow, bcol, states, parts, do):
    t = qkv.shape[1]
    nc, hb, head, flat, col, row, st = _dn_specs(t)
    n_parts = len(DN_PARTS)

    def body(q_ref, k_ref, v_ref, gc_ref, gr_ref, bc_ref, s_ref, do_ref, *rest):
        parts, (dqkv_ref, dgc_ref, dgr_ref, dbc_ref), dparts = rest[:n_parts], rest[n_parts:n_parts + 4], rest[n_parts + 4:]
        refs = (q_ref, k_ref, v_ref, gc_ref, gr_ref, bc_ref)

        def step(i, dstates):
            c = nc - 1 - i
            rows = pl.ds(pl.multiple_of(c * CHUNK, CHUNK), CHUNK)
            dstates_in = []
            for j in range(hb):
                _, vjp = jax.vjp(_dn_scan, *[part[j, c] for part in parts], s_ref[j, c])
                *dvals, dstate_in = vjp((do_ref[rows, _lanes(j)], dstates[j]))
                for dpart, dval in zip(dparts, dvals):
                    dpart[j, c] = dval
                dstates_in.append(dstate_in)
            return tuple(dstates_in)

        lax.fori_loop(0, nc, step, tuple(jnp.zeros((HEAD_DIM, HEAD_DIM), F32) for _ in range(hb)))

        grp = _dn_group(nc)

        def group(g, carry):
            rows = pl.ds(pl.multiple_of(g * (grp * CHUNK), grp * CHUNK), grp * CHUNK)
            cs = pl.ds(g * grp, grp)
            for j in range(hb):
                _, vjp = jax.vjp(jax.vmap(_dn_intra), *_dn_group_args(refs, j, g, grp))
                dq, dk, dv, dgc, dgr, dbc = vjp(tuple(dpart[j, cs] for dpart in dparts))
                for which, val in enumerate((dq, dk, dv)):
                    dqkv_ref[which, rows, _lanes(j)] = val.reshape(grp * CHUNK, HEAD_DIM)
                dgc_ref[j, cs] = dgc
                dgr_ref[j, cs] = dgr
                dbc_ref[j, cs] = dbc
            return carry

        lax.fori_loop(0, nc // grp, group, 0)

    return pl.pallas_call(
        body, name="dn_core_bwd", grid=(N_HEADS // hb,), scratch_shapes=_dn_scratch(hb, nc),
        in_specs=[head(0), head(1), head(2), col, row, col, st, flat] + _dn_part_specs(hb, nc),
        out_specs=[pl.BlockSpec((3, t, hb * HEAD_DIM), lambda h: (0, 0, h)), col, row, col],
        out_shape=[jax.ShapeDtypeStruct((3, t, D_MODEL), F32)] + [
            jax.ShapeDtypeStruct((N_HEADS, nc, CHUNK, 1), F32), jax.ShapeDtypeStruct((N_HEADS, nc, 1, CHUNK), F32),
            jax.ShapeDtypeStruct((N_HEADS, nc, CHUNK, 1), F32)],
        compiler_params=_cparams(("parallel",)),
    )(qkv, qkv, qkv, gcol, grow, bcol, states, do, *parts)


def gates_to_heads(gates):
    t = gates.shape[0]
    nc = t // CHUNK
    g = gates[:, :N_HEADS].T.reshape(N_HEADS, nc, CHUNK)
    b = gates[:, N_HEADS:2 * N_HEADS].T.reshape(N_HEADS, nc, CHUNK)
    return g[..., None], g[:, :, None, :], b[..., None]


def heads_to_gates(dgcol, dgrow, dbcol):
    nh, nc = dgcol.shape[:2]
    dg = (dgcol[..., 0] + dgrow[:, :, 0, :]).reshape(nh, nc * CHUNK).T
    db = dbcol[..., 0].reshape(nh, nc * CHUNK).T
    return jnp.concatenate([dg, db, jnp.zeros((nc * CHUNK, HEAD_DIM - 2 * nh), F32)], axis=1)


def _dn_out(o, z, w):
    return _rms(o, w) * _silu(z)


def _gate_specs():
    o_spec = pl.BlockSpec((ROWS, D_MODEL), lambda i: (i, 0))
    z_spec = pl.BlockSpec((ROWS, D_MODEL), lambda i: (i, 3))
    w_spec = pl.BlockSpec((1, HEAD_DIM), lambda i: (0, 0))
    return o_spec, z_spec, w_spec


def dn_out_fwd(o, proj, w):
    t = o.shape[0]
    o_spec, z_spec, w_spec = _gate_specs()

    def body(o_ref, z_ref, w_ref, y_ref):
        for h in range(N_HEADS):
            y_ref[:, _lanes(h)] = _dn_out(o_ref[:, _lanes(h)], z_ref[:, _lanes(h)], w_ref[...]).astype(y_ref.dtype)

    return pl.pallas_call(
        body, name="dn_out_fwd", grid=(t // ROWS,), in_specs=[o_spec, z_spec, w_spec], out_specs=o_spec,
        out_shape=jax.ShapeDtypeStruct((t, D_MODEL), MM), compiler_params=_cparams(("parallel",)),
    )(o, proj, w)


def dn_out_bwd(o, proj, w, dcat):
    t = o.shape[0]
    o_spec, z_spec, w_spec = _gate_specs()

    def body(o_ref, z_ref, w_ref, g_ref, do_ref, dz_ref, dw_ref):
        dw_sum = jnp.zeros((1, HEAD_DIM), F32)
        for h in range(N_HEADS):
            _, vjp = jax.vjp(_dn_out, o_ref[:, _lanes(h)], z_ref[:, _lanes(h)], w_ref[...])
            do, dz, dw = vjp(g_ref[:, _lanes(h)].astype(F32))
            do_ref[:, _lanes(h)] = do
            dz_ref[:, _lanes(h)] = dz.astype(dz_ref.dtype)
            dw_sum = dw_sum + dw
        _acc(dw_ref, dw_sum, pl.program_id(0) == 0)

    return pl.pallas_call(
        body, name="dn_out_bwd", grid=(t // ROWS,), in_specs=[o_spec, z_spec, w_spec, o_spec],
        out_specs=[o_spec, o_spec, w_spec],
        out_shape=[jax.ShapeDtypeStruct((t, D_MODEL), F32), jax.ShapeDtypeStruct((t, D_MODEL), MM),
                   jax.ShapeDtypeStruct((1, HEAD_DIM), F32)],
        compiler_params=_cparams(("arbitrary",)),
    )(o, proj, w, dcat)


def _fox_norm(x, w, scale):
    return _rms(x, w) * scale


def _fox_prep_specs():
    x_spec = pl.BlockSpec((ROWS, 2 * D_MODEL), lambda i: (i, 0))
    w_spec = pl.BlockSpec((2, 1, HEAD_DIM), lambda i: (0, 0, 0))
    y_spec = pl.BlockSpec((2, ROWS, D_MODEL), lambda i: (0, i, 0))
    return x_spec, w_spec, y_spec


def fox_prep_fwd(proj, wqk):
    t = proj.shape[0]
    x_spec, w_spec, y_spec = _fox_prep_specs()

    def body(x_ref, w_ref, y_ref):
        for j in range(2 * N_HEADS):
            which, scale = j // N_HEADS, (QSCALE if j < N_HEADS else 1.0)
            y_ref[which, :, _lanes(j % N_HEADS)] = _fox_norm(x_ref[:, _lanes(j)], w_ref[which], scale).astype(y_ref.dtype)

    return pl.pallas_call(
        body, name="fox_prep_fwd", grid=(t // ROWS,), in_specs=[x_spec, w_spec], out_specs=y_spec,
        out_shape=jax.ShapeDtypeStruct((2, t, D_MODEL), MM), compiler_params=_cparams(("parallel",)),
    )(proj, wqk)


def fox_prep_bwd(proj, wqk, dq, dk):
    t = proj.shape[0]
    x_spec, w_spec, _ = _fox_prep_specs()
    g_spec = pl.BlockSpec((ROWS, D_MODEL), lambda i: (i, 0))

    def body(x_ref, w_ref, dq_ref, dk_ref, dx_ref, dw_ref):
        dws = [jnp.zeros((1, HEAD_DIM), F32), jnp.zeros((1, HEAD_DIM), F32)]
        for j in range(2 * N_HEADS):
            which, scale = j // N_HEADS, (QSCALE if j < N_HEADS else 1.0)
            g_ref = dq_ref if which == 0 else dk_ref
            _, vjp = jax.vjp(lambda x, w: _fox_norm(x, w, scale), x_ref[:, _lanes(j)], w_ref[which])
            dx, dw = vjp(g_ref[:, _lanes(j % N_HEADS)])
            dx_ref[:, _lanes(j)] = dx.astype(dx_ref.dtype)
            dws[which] = dws[which] + dw
        first = pl.program_id(0) == 0
        _acc(dw_ref.at[0], dws[0], first)
        _acc(dw_ref.at[1], dws[1], first)

    return pl.pallas_call(
        body, name="fox_prep_bwd", grid=(t // ROWS,), in_specs=[x_spec, w_spec, g_spec, g_spec],
        out_specs=[x_spec, w_spec],
        out_shape=[jax.ShapeDtypeStruct((t, 2 * D_MODEL), MM), jax.ShapeDtypeStruct((2, 1, HEAD_DIM), F32)],
        compiler_params=_cparams(("arbitrary",)),
    )(proj, wqk, dq, dk)


def _row_pick(x, i):
    r = lax.broadcasted_iota(jnp.int32, x.shape, 0)
    return jnp.sum(jnp.where(r == i, x, 0.0), axis=0, keepdims=True)


def fox_gates_fwd(proj, f_bias):
    t = proj.shape[0]
    blk = HEAD_DIM

    def body(x_ref, b_ref, o_ref):
        lane = lax.broadcasted_iota(jnp.int32, (blk, HEAD_DIM), 1)
        tri = _tri_ones(blk, False)

        def step(c, carry):
            rows = pl.ds(pl.multiple_of(c * blk, blk), blk)
            lf = jnp.where(lane < N_HEADS, -_softplus(-(x_ref[rows, :] + b_ref[...])), 0.0)
            cum = _dot(tri, lf, 1, 0, True) + carry
            o_ref[rows, :] = cum
            return _row_pick(cum, blk - 1)

        lax.fori_loop(0, t // blk, step, jnp.zeros((1, HEAD_DIM), F32))

    vec = pl.BlockSpec((1, HEAD_DIM), lambda i: (0, 0))
    return pl.pallas_call(
        body, name="fox_gates_fwd", grid=(1,),
        in_specs=[pl.BlockSpec((t, HEAD_DIM), lambda i: (0, TAIL_BLK)), vec],
        out_specs=pl.BlockSpec((t, HEAD_DIM), lambda i: (0, 0)),
        out_shape=jax.ShapeDtypeStruct((t, HEAD_DIM), F32), compiler_params=_cparams(("arbitrary",)),
    )(proj, f_bias)


def fox_gates_bwd(proj, f_bias, dfcum):
    t = proj.shape[0]
    blk = HEAD_DIM
    nb = t // blk

    def body(x_ref, b_ref, g_ref, dx_ref, db_ref):
        lane = lax.broadcasted_iota(jnp.int32, (blk, HEAD_DIM), 1)
        tri = _tri_ones(blk, True)
        db_ref[...] = jnp.zeros_like(db_ref)

        def step(i, carry):
            c = nb - 1 - i
            rows = pl.ds(pl.multiple_of(c * blk, blk), blk)
            g = jnp.where(lane < N_HEADS, g_ref[rows, :], 0.0)
            dlf = _dot(tri, g, 1, 0, True) + carry
            dx = jnp.where(lane < N_HEADS, dlf * _sigmoid(-(x_ref[rows, :] + b_ref[...])), 0.0)
            dx_ref[rows, :] = dx.astype(dx_ref.dtype)
            db_ref[...] += jnp.sum(dx, axis=0, keepdims=True)
            return carry + jnp.sum(g, axis=0, keepdims=True)

        lax.fori_loop(0, nb, step, jnp.zeros((1, HEAD_DIM), F32))

    vec = pl.BlockSpec((1, HEAD_DIM), lambda i: (0, 0))
    full = pl.BlockSpec((t, HEAD_DIM), lambda i: (0, 0))
    return pl.pallas_call(
        body, name="fox_gates_bwd", grid=(1,),
        in_specs=[pl.BlockSpec((t, HEAD_DIM), lambda i: (0, TAIL_BLK)), vec, full], out_specs=[full, vec],
        out_shape=[jax.ShapeDtypeStruct((t, HEAD_DIM), MM), jax.ShapeDtypeStruct((1, HEAD_DIM), F32)],
        compiler_params=_cparams(("arbitrary",)),
    )(proj, f_bias, dfcum)


def fcum_to_heads(fcum):
    f = fcum[:, :N_HEADS].T
    return f[:, :, None], f[:, None, :]


def heads_to_fcum(dfcol, dfrow):
    d = (dfcol[:, :, 0] + dfrow[:, 0, :]).T
    return jnp.concatenate([d, jnp.zeros((d.shape[0], HEAD_DIM - N_HEADS), F32)], axis=1)


def _fox_tq(t):
    return min(t, 256)


def _fox_specs(t):
    tq = _fox_tq(t)
    q_spec = pl.BlockSpec((None, tq, HEAD_DIM), lambda h, i: (0, i, h))
    k_spec = pl.BlockSpec((None, t, HEAD_DIM), lambda h, i: (1, 0, h))
    v_spec = pl.BlockSpec((t, HEAD_DIM), lambda h, i: (0, 2 * N_HEADS + h))
    gate_spec = pl.BlockSpec((tq, HEAD_DIM), lambda h, i: (i, 3 * N_HEADS + h))
    col_spec = pl.BlockSpec((None, tq, 1), lambda h, i: (h, i, 0))
    row_spec = pl.BlockSpec((None, 1, t), lambda h, i: (h, 0, 0))
    blk_spec = pl.BlockSpec((tq, HEAD_DIM), lambda h, i: (i, h))
    head_spec = pl.BlockSpec((t, HEAD_DIM), lambda h, i: (0, h))
    return tq, q_spec, k_spec, v_spec, gate_spec, col_spec, row_spec, blk_spec, head_spec


def _fox_segments(i, tq):
    return ([(0, i * tq, False)] if i else []) + [(i * tq, (i + 1) * tq, True)]


def _fox_scores(q_ref, k_ref, fc_ref, fr_ref, lo, hi, causal):
    s = _dot(q_ref[...], k_ref[lo:hi, :], 1, 1, False) + (fc_ref[...] - fr_ref[:, lo:hi])
    if not causal:
        return s, None
    r, c = _iota2(hi - lo, hi - lo)
    return s, c <= r


def fox_attn_fwd(qk, proj, fcol, frow):
    t = proj.shape[0]
    tq, q_spec, k_spec, v_spec, gate_spec, col_spec, row_spec, blk_spec, _ = _fox_specs(t)

    def body(q_ref, k_ref, v_ref, gate_ref, fc_ref, fr_ref, mix_ref, o_ref, lse_ref):
        def block(i):
            segs = _fox_segments(i, tq)
            scores = [_fox_scores(q_ref, k_ref, fc_ref, fr_ref, *seg) for seg in segs]
            scores = [(s if mask is None else jnp.where(mask, s, -1e30), mask) for s, mask in scores]
            m = functools.reduce(jnp.maximum, [jnp.max(s, axis=-1, keepdims=True) for s, _ in scores])
            l, o = 0.0, 0.0
            for (lo, hi, _), (s, mask) in zip(segs, scores):
                p = jnp.exp(s - m)
                p = p if mask is None else jnp.where(mask, p, 0.0)
                l = l + jnp.sum(p, axis=-1, keepdims=True)
                o = o + _dot(p, v_ref[lo:hi, :], 1, 0, False)
            o = o / l
            o_ref[...] = o
            mix_ref[...] = (o * _sigmoid(gate_ref[...])).astype(mix_ref.dtype)
            lse_ref[...] = m + jnp.log(l)

        for i in range(t // tq):
            pl.when(pl.program_id(1) == i)(functools.partial(block, i))

    return pl.pallas_call(
        body, name="fox_attn_fwd", grid=(N_HEADS, t // tq),
        in_specs=[q_spec, k_spec, v_spec, gate_spec, col_spec, row_spec], out_specs=[blk_spec, blk_spec, col_spec],
        out_shape=[jax.ShapeDtypeStruct((t, D_MODEL), MM), jax.ShapeDtypeStruct((t, D_MODEL), F32),
                   jax.ShapeDtypeStruct((N_HEADS, t, 1), F32)],
        compiler_params=_cparams(("parallel", "parallel")),
    )(qk, qk, proj, proj, fcol, frow)


def fox_attn_bwd(qk, proj, fcol, frow, o, lse, dcat):
    t = proj.shape[0]
    tq, q_spec, k_spec, v_spec, gate_spec, col_spec, row_spec, blk_spec, head_spec = _fox_specs(t)

    def body(q_ref, k_ref, v_ref, gate_ref, fc_ref, fr_ref, o_ref, lse_ref, g_ref,
             dq_ref, dk_ref, dv_ref, dgate_ref, dfc_ref, dfr_ref):
        @pl.when(pl.program_id(1) == 0)
        def _():
            dk_ref[...] = jnp.zeros_like(dk_ref)
            dv_ref[...] = jnp.zeros_like(dv_ref)
            dfr_ref[...] = jnp.zeros_like(dfr_ref)

        def block(i):
            sg = _sigmoid(gate_ref[...])
            g = g_ref[...].astype(F32)
            o_pre = o_ref[...]
            do = g * sg
            dgate_ref[...] = (g * o_pre * sg * (1.0 - sg)).astype(dgate_ref.dtype)
            delta = jnp.sum(do * o_pre, axis=-1, keepdims=True)
            dq, dfc = 0.0, 0.0
            for lo, hi, causal in _fox_segments(i, tq):
                s, mask = _fox_scores(q_ref, k_ref, fc_ref, fr_ref, lo, hi, causal)
                if causal:
                    p = jnp.where(mask, jnp.exp(jnp.where(mask, s, 0.0) - lse_ref[...]), 0.0)
                else:
                    p = jnp.exp(s - lse_ref[...])
                ds = p * (_dot(do, v_ref[lo:hi, :], 1, 1, False) - delta)
                dq = dq + _dot(ds, k_ref[lo:hi, :], 1, 0, False)
                dk_ref[lo:hi, :] += _dot(ds, q_ref[...], 0, 0, False)
                dv_ref[lo:hi, :] += _dot(p, do, 0, 0, False)
                dfc = dfc + jnp.sum(ds, axis=-1, keepdims=True)
                dfr_ref[:, lo:hi] += -jnp.sum(ds, axis=0, keepdims=True)
            dq_ref[...] = dq
            dfc_ref[...] = dfc

        for i in range(t // tq):
            pl.when(pl.program_id(1) == i)(functools.partial(block, i))

    f32 = lambda *s: jax.ShapeDtypeStruct(s, F32)
    return pl.pallas_call(
        body, name="fox_attn_bwd", grid=(N_HEADS, t // tq),
        in_specs=[q_spec, k_spec, v_spec, gate_spec, col_spec, row_spec, blk_spec, col_spec, blk_spec],
        out_specs=[blk_spec, head_spec, head_spec, blk_spec, col_spec, row_spec],
        out_shape=[f32(t, D_MODEL), f32(t, D_MODEL), f32(t, D_MODEL), jax.ShapeDtypeStruct((t, D_MODEL), MM),
                   f32(N_HEADS, t, 1), f32(N_HEADS, 1, t)],
        compiler_params=_cparams(("parallel", "arbitrary")),
    )(qk, qk, proj, proj, fcol, frow, o, lse, dcat)


def adamw(w, g, m, v, *, name):
    r, c = w.shape
    rb = ADAM_ROWS if r % ADAM_ROWS == 0 else r

    def body(w_ref, g_ref, m_ref, v_ref, d_ref, nm_ref, nv_ref):
        g_ = g_ref[...]
        m_ = ADAM_B1 * m_ref[...] + (1.0 - ADAM_B1) * g_
        v_ = ADAM_B2 * v_ref[...] + (1.0 - ADAM_B2) * jnp.square(g_)
        m_hat = m_ / (1.0 - ADAM_B1 ** ADAM_STEP)
        v_hat = v_ / (1.0 - ADAM_B2 ** ADAM_STEP)
        d_ref[...] = -ADAM_LR * (m_hat / (jnp.sqrt(v_hat) + ADAM_EPS) + ADAM_WD * w_ref[...])
        nm_ref[...] = m_
        nv_ref[...] = v_

    blk = pl.BlockSpec((rb, c), lambda i: (i, 0))
    shp = jax.ShapeDtypeStruct((r, c), F32)
    return pl.pallas_call(body, name=name, grid=(r // rb,), in_specs=[blk] * 4, out_specs=[blk] * 3,
                          out_shape=[shp] * 3, compiler_params=_cparams(("parallel",)))(w, g, m, v)


def _place():
    x, y, c = lax.axis_index("x"), lax.axis_index("y"), lax.axis_index("c")
    return x, y, c, [(1 - x, y), (x, 1 - y), (1 - x, 1 - y)]


ANY = pl.BlockSpec(memory_space=pl.ANY)


def all_reduce_small(v):
    r, w = v.shape

    def body(v_ref, o_ref, buf, send_sems, recv_sems):
        x, y, c, _ = _place()
        me = 4 * x + 2 * y + c
        flip = lambda a, bit: 1 - a if bit else a
        cps = []
        for k in range(1, N_DEV):
            peer = (flip(x, k & 4), flip(y, k & 2), flip(c, k & 1))
            cp = pltpu.make_async_remote_copy(src_ref=v_ref, dst_ref=buf.at[me], send_sem=send_sems.at[k - 1],
                                              recv_sem=recv_sems.at[k - 1], device_id=peer, device_id_type=MESH)
            cp.start()
            cps.append((cp, 4 * peer[0] + 2 * peer[1] + peer[2]))
        buf[me] = v_ref[...]
        for k, (cp, peer_id) in enumerate(cps):
            pltpu.make_async_remote_copy(src_ref=v_ref, dst_ref=buf.at[peer_id], send_sem=send_sems.at[k],
                                         recv_sem=recv_sems.at[k], device_id=(x, y, c), device_id_type=MESH).wait_recv()
        for cp, _ in cps:
            cp.wait_send()
        acc = buf[0]
        for d in range(1, N_DEV):
            acc = acc + buf[d]
        o_ref[...] = acc

    vm = pl.BlockSpec(memory_space=pltpu.VMEM)
    return pl.pallas_call(
        body, name="all_reduce_small", in_specs=[vm], out_specs=vm, out_shape=jax.ShapeDtypeStruct((r, w), F32),
        scratch_shapes=[pltpu.VMEM((N_DEV, r, w), F32), pltpu.SemaphoreType.DMA((N_DEV - 1,)),
                        pltpu.SemaphoreType.DMA((N_DEV - 1,))],
    )(v)


def _after(x, token):
    first = (0,) * x.ndim
    patch = lax.dynamic_slice(x, first, (1,) * x.ndim) + token[0, 0].astype(x.dtype)
    return lax.dynamic_update_slice(x, patch, first)


def _vec8(v):
    return jnp.zeros((1, HEAD_DIM), F32).at[0, :N_HEADS].set(v.reshape(N_HEADS))


def _layer_fwd(i, x_in, wt, sm, mem_k, mem_v, late=None):
    tag = f"l{i}_"
    h = rms_fwd(x_in, sm["norm1_w"][i][None], name=tag + "rms1")
    w_in = wt["dn_w_in"] if i == 0 else wt["fox_w_in"]
    proj = matmul(h, w_in, name=tag + "proj", tm=256, tk=1024)
    sv = dict(x_in=x_in, h=h, proj=proj)
    if i == 0:
        qkv = dn_prep_fwd(proj, wt["conv_w"])
        gates = dn_gates_fwd(proj, _vec8(sm["dn_a_log"]), _vec8(sm["dn_dt_bias"]))
        gcol, grow, bcol = gates_to_heads(gates)
        o, states, parts = dn_core_fwd(qkv, gcol, grow, bcol)
        mix = dn_out_fwd(o, proj, sm["dn_o_norm_w"])
        sv.update(qkv=qkv, gcol=gcol, grow=grow, bcol=bcol, states=states, parts=parts, o=o)
    else:
        wqk = jnp.stack([sm["fox_q_norm_w"], sm["fox_k_norm_w"]])
        qk = fox_prep_fwd(proj, wqk)
        fcum = fox_gates_fwd(proj, _vec8(sm["fox_f_bias"]))
        fcol, frow = fcum_to_heads(fcum)
        mix, o, lse = fox_attn_fwd(qk, proj, fcol, frow)
        sv.update(wqk=wqk, qk=qk, fcol=fcol, frow=frow, o=o, lse=lse)
    mem_out = memattn_fwd(proj, sm["memq_norm_w"][i][None], mem_k, mem_v, name=tag + "memattn_fwd")
    cat = jnp.concatenate([mix, mem_out], axis=1)
    if late is not None:
        wt.update(late(cat))
    x_mid = matmul(cat, wt["w_out"], b_layer=i, res=x_in, name=tag + "out_proj")
    h2 = rms_fwd(x_mid, sm["norm2_w"][i][None], name=tag + "rms2")
    ff, act = matmul(h2, wt["w_mlp1"], b_layer=i, b_slots=True, also_sqrelu=True, out_dtype=MM, name=tag + "mlp1")
    x_out = matmul(act, wt["w_mlp2"], b_layer=i, res=x_mid, name=tag + "mlp2")
    sv.update(cat=cat, x_mid=x_mid, h2=h2, ff=ff, act=act)
    return x_out, sv


def _layer_bwd(i, dx_out, sv, wt, sm, mem_k, mem_v, on_mlp=None, on_core=None):
    tag = f"l{i}_"
    big, small = {}, {}
    dff = matmul(dx_out, wt["w_mlp2"], b_layer=i, tb=True, times_dsqrelu=sv["ff"], out_dtype=MM, name=tag + "d_ff")
    big["w_mlp2"] = matmul(sv["act"], dx_out, ta=True, name=tag + "d_w_mlp2", tk=2048)
    dh2 = matmul(dff, wt["w_mlp1"], b_layer=i, tb=True, b_slots=True, name=tag + "d_h2")
    big["w_mlp1"] = matmul(sv["h2"], dff, ta=True, name=tag + "d_w_mlp1", tm=512, tn=D_FF, tk=512)
    dx_mid, small["norm2_w"] = rms_bwd(sv["x_mid"], sm["norm2_w"][i][None], dh2, dx_out, name=tag + "rms2_bwd")
    dcat = matmul(dx_mid, wt["w_out"], b_layer=i, tb=True, name=tag + "d_cat")
    big["w_out"] = matmul(sv["cat"], dx_mid, ta=True, name=tag + "d_w_out", tk=2048)
    proj = sv["proj"]
    memq_norm_w = sm["memq_norm_w"][i][None]
    if on_mlp is not None:
        memq_norm_w = _after(memq_norm_w, on_mlp(big["w_mlp2"], big["w_mlp1"], big["w_out"]))
    dqm, small["memq_norm_w"], dmk, dmv = memattn_bwd(proj, memq_norm_w, mem_k, mem_v, dcat, name=tag + "memattn_bwd")
    t = proj.shape[0]
    pad = jnp.zeros((t, PROJ_W - TAIL - HEAD_DIM), MM)
    if i == 0:
        do, dz, small["dn_o_norm_w"] = dn_out_bwd(sv["o"], proj, sm["dn_o_norm_w"], dcat)
        bcol = sv["bcol"] if on_core is None else _after(sv["bcol"], on_core(do))
        dqkv, dgc, dgr, dbc = dn_core_bwd(sv["qkv"], sv["gcol"], sv["grow"], bcol, sv["states"], sv["parts"], do)
        dtail, dal, ddt = dn_gates_bwd(proj, _vec8(sm["dn_a_log"]), _vec8(sm["dn_dt_bias"]), heads_to_gates(dgc, dgr, dbc))
        dmain, dconv = dn_prep_bwd(proj, wt["conv_w"], dqkv)
        small["dn_a_log"], small["dn_dt_bias"] = dal[:, :N_HEADS], ddt[:, :N_HEADS]
        big["conv_w"] = dconv[:4]
        dproj = jnp.concatenate([dmain, dz, dqm, dtail, pad], axis=1)
    else:
        dq, dk, dv, dgate, dfc, dfr = fox_attn_bwd(sv["qk"], proj, sv["fcol"], sv["frow"], sv["o"], sv["lse"], dcat)
        dtail, dfb = fox_gates_bwd(proj, _vec8(sm["fox_f_bias"]), heads_to_fcum(dfc, dfr))
        dqk, dwqk = fox_prep_bwd(proj, sv["wqk"], dq, dk)
        small["fox_f_bias"] = dfb[:, :N_HEADS]
        small["fox_q_norm_w"], small["fox_k_norm_w"] = dwqk[0], dwqk[1]
        dproj = jnp.concatenate([dqk, dv.astype(MM), dgate, dqm, dtail, pad], axis=1)
    w_in = wt["dn_w_in"] if i == 0 else wt["fox_w_in"]
    dh = matmul(dproj, w_in, tb=True, name=tag + "d_h", tm=512)
    big["w_in"] = matmul(sv["h"], dproj, ta=True, name=tag + "d_w_in", tm=256)
    dx_in, small["norm1_w"] = rms_bwd(sv["x_in"], sm["norm1_w"][i][None], dh, dx_mid, name=tag + "rms1_bwd")
    return dx_in, big, small, (dmk, dmv)


def local_step(x, mem, target, wt, sm, late=None, on_layer1=None, on_mlp0=None, on_core0=None):
    wt = dict(wt)
    mem_k, mem_v = mem_fwd(mem, sm["mem_norm_w"][None], wt["w_mem_kv"], sm["mem_k_norm_w"][None])
    x0, sv0 = _layer_fwd(0, x, wt, sm, mem_k, mem_v, late)
    x1, sv1 = _layer_fwd(1, x0, wt, sm, mem_k, mem_v)
    dy, loss = loss_fwd(x1, target, name="loss")
    dx1, big1, small1, dm1 = _layer_bwd(1, dy, sv1, wt, sm, mem_k, mem_v)
    if on_layer1 is not None:
        dx1 = _after(dx1, on_layer1(big1))
    dx0, big0, small0, dm0 = _layer_bwd(0, dx1, sv0, wt, sm, mem_k, mem_v, on_mlp0, on_core0)
    dwn, dwkv, dwkn = mem_bwd(mem, sm["mem_norm_w"][None], wt["w_mem_kv"], sm["mem_k_norm_w"][None], *dm0, *dm1)
    small = dict(mem_norm_w=dwn[0], mem_k_norm_w=dwkn[0],
                 norm1_w=jnp.concatenate([small0["norm1_w"], small1["norm1_w"]]),
                 norm2_w=jnp.concatenate([small0["norm2_w"], small1["norm2_w"]]),
                 memq_norm_w=jnp.concatenate([small0["memq_norm_w"], small1["memq_norm_w"]]),
                 dn_a_log=small0["dn_a_log"], dn_dt_bias=small0["dn_dt_bias"], dn_o_norm_w=small0["dn_o_norm_w"],
                 fox_f_bias=small1["fox_f_bias"], fox_q_norm_w=small1["fox_q_norm_w"], fox_k_norm_w=small1["fox_k_norm_w"])
    big = dict(w_mem_kv=dwkv, dn_w_in=big0["w_in"], fox_w_in=big1["w_in"], conv_w=big0["conv_w"],
               w_out=[big0["w_out"], big1["w_out"]], w_mlp1=[big0["w_mlp1"], big1["w_mlp1"]],
               w_mlp2=[big0["w_mlp2"], big1["w_mlp2"]])
    return loss, dx0, big, small


def w_in_slots_to_kernel(slots, n_scalars):
    c = slots.shape[2]
    cut = 4096 - 3 * c
    pad = jnp.zeros((slots.shape[1], PROJ_W - TAIL - n_scalars), slots.dtype)
    return jnp.concatenate([slots[0], slots[1], slots[2], slots[3, :, :cut], slots[3, :, cut + n_scalars:],
                            slots[3, :, cut:cut + n_scalars], pad], axis=1)


def w_in_kernel_to_slots(w, n_scalars):
    c = (4096 + n_scalars + MEM_WIDTH) // N_CHIP
    last = jnp.concatenate([w[:, 3 * c:4096], w[:, TAIL:TAIL + n_scalars], w[:, 4096:TAIL]], axis=1)
    return jnp.stack([w[:, :c], w[:, c:2 * c], w[:, 2 * c:3 * c], last])


BIG_SPECS = dict(w_mem_kv=("rows", 1, 256, 1024), w_out=("rows", 2, 384, 1024), w_mlp2=("rows", 2, 1024, 1024),
                 w_mlp1=("cols", 2, 1024, 1024), dn_w_in=("rows", 1, 1024, 1156), fox_w_in=("rows", 1, 1024, 1154))
BIG_NAMES = tuple(BIG_SPECS)
EARLY_NAMES = ("w_mem_kv", "dn_w_in")
LATE_NAMES = ("w_out", "w_mlp2", "w_mlp1", "fox_w_in")
BIG_SPECS.update({f"{name}_{i}": (BIG_SPECS[name][0], 1) + BIG_SPECS[name][2:]
                  for name in ("w_out", "w_mlp2", "w_mlp1") for i in range(2)})
RS_LAYER1 = ("fox_w_in", "w_out_1", "w_mlp2_1", "w_mlp1_1")
RS_MLP0 = ("w_mlp2_0", "w_mlp1_0", "w_out_0")
RS_LAST = ("dn_w_in", "w_mem_kv")


def _full_shape(name, half=False):
    kind, a, b, c = BIG_SPECS[name]
    b = b // 2 if half else b
    return (a, N_CHIP, b, c) if kind == "rows" else (a, b, N_CHIP * c)


def _ds(start, size, align):
    return pl.ds(start if isinstance(start, int) else pl.multiple_of(start, align), size)


def _half_rows(name, h):
    b = BIG_SPECS[name][2]
    return _ds(h * (b // 2), b // 2, 16)


def _shard_idx(name, h):
    return (slice(None), _half_rows(name, h), slice(None))


def _full_idx(name, j=None, h=None):
    kind, _, _, c = BIG_SPECS[name]
    rows = slice(None) if h is None else _half_rows(name, h)
    if kind == "rows":
        return (slice(None), slice(None) if j is None else j, rows, slice(None))
    return (slice(None), rows, slice(None) if j is None else _ds(j * c, c, 128))


def _slots_shape(name):
    _, a, b, c = BIG_SPECS[name]
    return (a, N_CHIP, b, c)


def _slots_idx(name, j, h):
    return (slice(None), j, _half_rows(name, h), slice(None))


def _row_block(name):
    hs = BIG_SPECS[name][2] // 2
    return hs if hs <= ROWS else ROWS


def _remote(src, dst, send_sem, recv_sem, to):
    return pltpu.make_async_remote_copy(src_ref=src, dst_ref=dst, send_sem=send_sem, recv_sem=recv_sem, device_id=to,
                                        device_id_type=MESH)


HBM = pl.BlockSpec(memory_space=pltpu.HBM)
SEM = pl.BlockSpec(memory_space=pltpu.SEMAPHORE)
EFFECT = pltpu.CompilerParams(has_side_effects=pltpu.SideEffectType.DATAFLOW_SIDE_EFFECTING)


def _in_hbm(a):
    return pltpu.with_memory_space_constraint(a, pltpu.HBM)


def _chip_copies(names, ins, lands, send_sems, recv_sems):
    x, y, c, chips = _place()
    return [_remote(ins[a].at[_shard_idx(name, c)], lands[a].at[_slots_idx(name, 2 * x + y, c)], send_sems.at[3 * a + k],
                    recv_sems.at[3 * a + k], (chip[0], chip[1], c))
            for a, name in enumerate(names) for k, chip in enumerate(chips)]


def _copies_start(call_name, copies, sources, land_shapes, per_source=3, land_dtype=MM, after=()):
    n, n_after = len(sources), len(after)

    def body(*refs):
        ins, lands, token = refs[:n], refs[n:2 * n], refs[-1]
        send_sems, recv_sems = refs[2 * n + n_after], refs[2 * n + n_after + 1]
        for cp in copies(ins, lands, send_sems, recv_sems):
            cp.start()
        token[...] = jnp.zeros_like(token)

    ins = [_in_hbm(a) for a in sources]
    lands = [_in_hbm(lax.empty(shape, land_dtype)) for shape in land_shapes]
    sems = (pltpu.SemaphoreType.DMA((per_source * n,)), pltpu.SemaphoreType.DMA((per_source * n,)))
    outs = pl.pallas_call(
        body, name=call_name, in_specs=[HBM] * (2 * n) + [ANY] * n_after,
        out_specs=(SEM, SEM) + (HBM,) * (2 * n) + (pl.BlockSpec(memory_space=pltpu.VMEM),),
        out_shape=sems + tuple(pltpu.HBM(a.shape, a.dtype) for a in ins + lands) + (jax.ShapeDtypeStruct((8, HEAD_DIM), F32),),
        input_output_aliases={a: 2 + a for a in range(2 * n)}, compiler_params=EFFECT,
    )(*ins, *lands, *after)
    return outs[:-1], outs[-1]


def _copies_wait(call_name, copies, state, after):
    n = (len(state) - 2) // 2

    def body(*refs):
        send_sems, recv_sems, ins, lands = refs[0], refs[1], refs[2:2 + n], refs[2 + n:2 + 2 * n]
        for cp in copies(ins, lands, send_sems, recv_sems):
            cp.wait_send()
            cp.wait_recv()

    outs = pl.pallas_call(
        body, name=call_name, in_specs=[SEM, SEM] + [HBM] * (2 * n) + [ANY], out_specs=(HBM,) * (2 * n),
        out_shape=tuple(pltpu.HBM(a.shape, a.dtype) for a in state[2:]),
        input_output_aliases={2 + a: a for a in range(2 * n)}, compiler_params=EFFECT,
    )(*state, after)
    return outs[:n], outs[n:]


def all_gather_start(shards, names, after):
    return _copies_start("all_gather_start", functools.partial(_chip_copies, names), [shards[name] for name in names],
                         [_slots_shape(name) for name in names], after=after)


def all_gather_wait(state, names, after):
    ins, lands = _copies_wait("all_gather_wait", functools.partial(_chip_copies, names), state, after)
    return dict(zip(names, ins)), dict(zip(names, lands))


def _chip_sends(names, ins, lands, send_sems, recv_sems):
    x, y, c, chips = _place()
    return [_remote(ins[a].at[_full_idx(name, 2 * chip[0] + chip[1])], lands[a].at[k], send_sems.at[3 * a + k],
                    recv_sems.at[3 * a + k], (chip[0], chip[1], c))
            for a, name in enumerate(names) for k, chip in enumerate(chips)]


def _got_shape(name):
    _, a_, b_, c_ = BIG_SPECS[name]
    return (3, a_, b_ // 2, c_)


def rs_chip_start(pairs, names, tag):
    return _copies_start("rs_chip_start_" + tag, functools.partial(_chip_sends, names), [pairs[name] for name in names],
                         [_got_shape(name) for name in names])


def rs_chip_wait(state, names, tag, after):
    _, lands = _copies_wait("rs_chip_wait_" + tag, functools.partial(_chip_sends, names), state, after)
    return dict(zip(names, lands))


def all_gather_pass_on(lands, names):
    n = len(names)

    def body(*refs):
        outs, send_sems, recv_sems = refs[n:2 * n], refs[2 * n], refs[2 * n + 1]
        x, y, c, chips = _place()
        work = [(3 * a + k, a, name, 2 * chip[0] + chip[1]) for a, name in enumerate(names) for k, chip in enumerate(chips)]
        cps = []
        for s, a, name, slot in work:
            landed = outs[a].at[_slots_idx(name, slot, c)]
            cps.append(_remote(landed, landed, send_sems.at[s], recv_sems.at[s], (x, y, 1 - c)))
            cps[-1].start()
        for s, a, name, slot in work:
            passed = outs[a].at[_slots_idx(name, slot, 1 - c)]
            _remote(passed, passed, send_sems.at[s], recv_sems.at[s], (x, y, 1 - c)).wait_recv()
        for cp in cps:
            cp.wait_send()

    outs = pl.pallas_call(
        body, name="all_gather_pass_on", in_specs=[ANY] * n, out_specs=[ANY] * n,
        input_output_aliases={a: a for a in range(n)},
        out_shape=[jax.ShapeDtypeStruct(_slots_shape(name), MM) for name in names],
        scratch_shapes=[pltpu.SemaphoreType.DMA((3 * n,)), pltpu.SemaphoreType.DMA((3 * n,))],
    )(*[lands[name] for name in names])
    return dict(zip(names, outs))


def all_gather_big(shards, names):
    n = len(names)

    def body(*refs):
        ins, outs = refs[:n], refs[n:2 * n]
        send_sems, recv_sems, fsend_sems, frecv_sems = refs[2 * n:]
        x, y, c, chips = _place()
        me_chip, sibling = 2 * x + y, (x, y, 1 - c)
        work = [(3 * a + k, a, name, chip) for a, name in enumerate(names) for k, chip in enumerate(chips)]
        sends = []
        for s, a, name, chip in work:
            cp = _remote(ins[a].at[_shard_idx(name, c)], outs[a].at[_slots_idx(name, me_chip, c)], send_sems.at[s],
                         recv_sems.at[s], (chip[0], chip[1], c))
            cp.start()
            sends.append(cp)
        for s, a, name, chip in work:
            landed = outs[a].at[_slots_idx(name, 2 * chip[0] + chip[1], c)]
            _remote(landed, landed, send_sems.at[s], recv_sems.at[s], (chip[0], chip[1], c)).wait_recv()
            cp = _remote(landed, landed, fsend_sems.at[s], frecv_sems.at[s], sibling)
            cp.start()
            sends.append(cp)
        for s, a, name, chip in work:
            passed = outs[a].at[_slots_idx(name, 2 * chip[0] + chip[1], 1 - c)]
            _remote(passed, passed, fsend_sems.at[s], frecv_sems.at[s], sibling).wait_recv()
        for cp in sends:
            cp.wait_send()

    outs = pl.pallas_call(
        body, name="all_gather_big", in_specs=[ANY] * n, out_specs=[ANY] * n,
        out_shape=[jax.ShapeDtypeStruct(_slots_shape(name), MM) for name in names],
        scratch_shapes=[pltpu.SemaphoreType.DMA((3 * n,))] * 4,
    )(*[shards[name] for name in names])
    return dict(zip(names, outs))


def with_own_slot(name, full, shard, chip):
    return lax.dynamic_update_slice(full, shard[:, None], (0, chip, 0, 0))


def rs_pair_exchange_big(grads, names, tag, after):
    n = len(names)

    def body(*refs):
        ins, outs, send_sems, recv_sems = refs[:n], refs[n + 1:2 * n + 1], refs[2 * n + 1], refs[2 * n + 2]
        cps = _pair_sends(names, ins, outs, send_sems, recv_sems)
        for cp in cps:
            cp.start()
        for cp in cps:
            cp.wait()

    outs = pl.pallas_call(
        body, name="rs_pair_exchange_" + tag, in_specs=[ANY] * (n + 1), out_specs=[ANY] * n,
        out_shape=[jax.ShapeDtypeStruct(_full_shape(name, half=True), F32) for name in names],
        scratch_shapes=[pltpu.SemaphoreType.DMA((n,)), pltpu.SemaphoreType.DMA((n,))],
    )(*[grads[name] for name in names], after)
    return dict(zip(names, outs))


def rs_pair_add_big(name, place, g, got):
    kind, a_, b_, c_ = BIG_SPECS[name]
    rb = _row_block(name)
    nb = (b_ // 2) // rb

    def body(place_ref, g_ref, got_ref, o_ref):
        o_ref[...] = (g_ref[...] + got_ref[...]).astype(o_ref.dtype)

    if kind == "rows":
        g_spec = pl.BlockSpec((None, None, rb, c_), lambda a, j, i, p: (a, j, p[0] * nb + i, 0))
        o_spec = pl.BlockSpec((None, None, rb, c_), lambda a, j, i, p: (a, j, i, 0))
    else:
        g_spec = pl.BlockSpec((None, rb, c_), lambda a, j, i, p: (a, p[0] * nb + i, j))
        o_spec = pl.BlockSpec((None, rb, c_), lambda a, j, i, p: (a, i, j))
    return pl.pallas_call(
        body, name="rs_pair_add_" + name,
        grid_spec=pltpu.PrefetchScalarGridSpec(num_scalar_prefetch=1, grid=(a_, N_CHIP, nb), in_specs=[g_spec, o_spec],
                                               out_specs=o_spec),
        out_shape=jax.ShapeDtypeStruct(_full_shape(name, half=True), MM),
        compiler_params=_cparams(("parallel", "parallel", "parallel")),
    )(place, g, got)


def rs_chip_add_big(name, place, g, got_pair, got_chips):
    kind, a_, b_, c_ = BIG_SPECS[name]
    rb = _row_block(name)
    nb = (b_ // 2) // rb

    def body(place_ref, g_ref, s_ref, r0_ref, r1_ref, r2_ref, o_ref):
        own = g_ref[...] + s_ref[...]
        o_ref[...] = ((own + r0_ref[...].astype(F32)) + r1_ref[...].astype(F32)) + r2_ref[...].astype(F32)

    if kind == "rows":
        g_spec = pl.BlockSpec((None, None, rb, c_), lambda a, i, p: (a, p[1], p[0] * nb + i, 0))
        s_spec = pl.BlockSpec((None, None, rb, c_), lambda a, i, p: (a, p[1], i, 0))
    else:
        g_spec = pl.BlockSpec((None, rb, c_), lambda a, i, p: (a, p[0] * nb + i, p[1]))
        s_spec = pl.BlockSpec((None, rb, c_), lambda a, i, p: (a, i, p[1]))
    r_spec = lambda k: pl.BlockSpec((None, None, rb, c_), lambda a, i, p: (k, a, i, 0))
    return pl.pallas_call(
        body, name="rs_chip_add_" + name,
        grid_spec=pltpu.PrefetchScalarGridSpec(
            num_scalar_prefetch=1, grid=(a_, nb), in_specs=[g_spec, s_spec, r_spec(0), r_spec(1), r_spec(2)],
            out_specs=pl.BlockSpec((None, rb, c_), lambda a, i, p: (a, p[0] * nb + i, 0))),
        out_shape=jax.ShapeDtypeStruct((a_, b_, c_), F32), compiler_params=_cparams(("parallel", "parallel")),
    )(place, g, got_pair, got_chips, got_chips, got_chips)


def rs_pair_gather_big(halves, tag):
    names = tuple(halves)
    n = len(names)

    def body(*refs):
        outs, send_sems, recv_sems = refs[n:2 * n], refs[2 * n], refs[2 * n + 1]
        x, y, c, _ = _place()
        cps = []
        for a, name in enumerate(names):
            mine = outs[a].at[_shard_idx(name, c)]
            cp = _remote(mine, mine, send_sems.at[a], recv_sems.at[a], (x, y, 1 - c))
            cp.start()
            cps.append(cp)
        for a, name in enumerate(names):
            cps[a].wait_send()
            theirs = outs[a].at[_shard_idx(name, 1 - c)]
            _remote(theirs, theirs, send_sems.at[a], recv_sems.at[a], (x, y, 1 - c)).wait_recv()

    outs = pl.pallas_call(
        body, name="rs_pair_gather_" + tag, in_specs=[ANY] * n, out_specs=[ANY] * n,
        input_output_aliases={a: a for a in range(n)},
        out_shape=[jax.ShapeDtypeStruct(BIG_SPECS[name][1:], F32) for name in names],
        scratch_shapes=[pltpu.SemaphoreType.DMA((n,)), pltpu.SemaphoreType.DMA((n,))],
    )(*[halves[name] for name in names])
    return dict(zip(names, outs))


def _pair_sends(names, ins, lands, send_sems, recv_sems):
    x, y, c, _ = _place()
    return [_remote(ins[a].at[_full_idx(name, None, 1 - c)], lands[a], send_sems.at[a], recv_sems.at[a], (x, y, 1 - c))
            for a, name in enumerate(names)]


def rs_pair_start(grads, names, tag):
    return _copies_start("rs_pair_start_" + tag, functools.partial(_pair_sends, names), [grads[name] for name in names],
                         [_full_shape(name, half=True) for name in names], per_source=1, land_dtype=F32)


def rs_middle(pair_state, names, tag, place, after):
    ins, lands = _copies_wait("rs_pair_wait_" + tag, functools.partial(_pair_sends, names), pair_state, after)
    grads, got_pair = dict(zip(names, ins)), dict(zip(names, lands))
    pairs = {name: rs_pair_add_big(name, place, grads[name], got_pair[name]) for name in names}
    state, token = rs_chip_start(pairs, names, tag)
    return (grads, got_pair, state), token


def rs_end(begun, names, tag, place, after):
    grads, got_pair, state = begun
    got_chips = rs_chip_wait(state, names, tag, after)
    return {name: rs_chip_add_big(name, place, grads[name], got_pair[name], got_chips[name]) for name in names}


PACK_W = 1024
SMALL =(("mem_norm_w", 1024), ("mem_k_norm_w", 128), ("norm1_w", 2048), ("dn_a_log", 8), ("dn_dt_bias", 8),
         ("dn_o_norm_w", 128), ("fox_f_bias", 8), ("fox_q_norm_w", 128), ("fox_k_norm_w", 128), ("memq_norm_w", 256),
         ("norm2_w", 2048))
SMALL_ROWS = 8
CONV_ROWS = 4 * 3 * D_MODEL // PACK_W
LOSS_AT = sum(n for _, n in SMALL)


def pack_small(parts, extra=None):
    flat = [parts[name].astype(F32).reshape(-1) for name, _ in SMALL]
    used = LOSS_AT
    if extra is not None:
        flat.append(extra.reshape(1))
        used += 1
    flat.append(jnp.zeros((SMALL_ROWS * PACK_W - used,), F32))
    return jnp.concatenate(flat).reshape(SMALL_ROWS, PACK_W)


def unpack_small(packed, shapes):
    flat, out, at = packed.reshape(-1), {}, 0
    for name, n in SMALL:
        out[name] = flat[at:at + n].reshape(shapes[name])
        at += n
    return out


def _adam_all(w, g, m, v, name):
    shape = w.shape
    r2 = lambda a: a.reshape(-1, shape[-1])
    d, nm, nv = adamw(r2(w), r2(g), r2(m), r2(v), name=name)
    return d.reshape(shape), nm.reshape(shape), nv.reshape(shape)


WEIGHTS = ("mem_norm_w", "w_mem_kv", "mem_k_norm_w", "norm1_w", "dn_w_in", "dn_conv_w", "dn_a_log", "dn_dt_bias",
           "dn_o_norm_w", "fox_w_in", "fox_f_bias", "fox_q_norm_w", "fox_k_norm_w", "memq_norm_w", "w_out", "norm2_w",
           "w_mlp1", "w_mlp2")


def kernel(x, mem, mem_norm_w, w_mem_kv, mem_k_norm_w, norm1_w, dn_w_in, dn_conv_w, dn_a_log, dn_dt_bias, dn_o_norm_w, fox_w_in, fox_f_bias, fox_q_norm_w, fox_k_norm_w, memq_norm_w, w_out, norm2_w, w_mlp1, w_mlp2, loss_target, m_mem_norm_w, m_w_mem_kv, m_mem_k_norm_w, m_norm1_w, m_dn_w_in, m_dn_conv_w, m_dn_a_log, m_dn_dt_bias, m_dn_o_norm_w, m_fox_w_in, m_fox_f_bias, m_fox_q_norm_w, m_fox_k_norm_w, m_memq_norm_w, m_w_out, m_norm2_w, m_w_mlp1, m_w_mlp2, v_mem_norm_w, v_w_mem_kv, v_mem_k_norm_w, v_norm1_w, v_dn_w_in, v_dn_conv_w, v_dn_a_log, v_dn_dt_bias, v_dn_o_norm_w, v_fox_w_in, v_fox_f_bias, v_fox_q_norm_w, v_fox_k_norm_w, v_memq_norm_w, v_w_out, v_norm2_w, v_w_mlp1, v_w_mlp2):
    args = dict(locals())
    w = {n: args[n] for n in WEIGHTS}
    m = {n: args["m_" + n] for n in WEIGHTS}
    v = {n: args["v_" + n] for n in WEIGHTS}
    core, chip = lax.axis_index("c"), 2 * lax.axis_index("x") + lax.axis_index("y")
    place = jnp.stack([core, chip]).astype(jnp.int32)

    shards = {name: w[name].reshape(BIG_SPECS[name][1:]).astype(MM) for name in BIG_NAMES}
    w_in_full = lambda arr, n_scalars: w_in_slots_to_kernel(arr[0], n_scalars)
    early = {name: with_own_slot(name, arr, shards[name], chip)
             for name, arr in all_gather_big(shards, EARLY_NAMES).items()}
    conv_mine = jnp.where(core == 0, dn_conv_w[0], 0.0)
    conv_placed = lax.dynamic_update_slice(jnp.zeros((4, 3 * D_MODEL), F32), conv_mine, (0, 768 * chip))
    conv_full = all_reduce_small(jnp.pad(conv_placed.reshape(CONV_ROWS, PACK_W), ((0, 16 - CONV_ROWS), (0, 0))))
    late_state, token = all_gather_start(shards, LATE_NAMES, after=(early["w_mem_kv"], early["dn_w_in"], conv_full))
    wt = dict(w_mem_kv=_after(early["w_mem_kv"].reshape(D_MODEL, 2 * MEM_WIDTH), token),
              dn_w_in=w_in_full(early["dn_w_in"], 2 * N_HEADS), conv_w=conv_full[:CONV_ROWS].reshape(4, 3 * D_MODEL))

    def late(after):
        late_shards, lands = all_gather_wait(late_state, LATE_NAMES, after)
        full = {name: with_own_slot(name, arr, late_shards[name], chip)
                for name, arr in all_gather_pass_on(lands, LATE_NAMES).items()}
        return dict(fox_w_in=w_in_full(full["fox_w_in"], N_HEADS), w_out=full["w_out"].reshape(2, 3 * MEM_WIDTH, D_MODEL),
                    w_mlp1=full["w_mlp1"], w_mlp2=full["w_mlp2"].reshape(2, D_FF, D_MODEL))

    sm = dict(mem_norm_w=mem_norm_w, mem_k_norm_w=mem_k_norm_w, norm1_w=norm1_w, norm2_w=norm2_w, memq_norm_w=memq_norm_w,
              dn_a_log=dn_a_log[0], dn_dt_bias=dn_dt_bias[0], dn_o_norm_w=dn_o_norm_w, fox_f_bias=fox_f_bias[0],
              fox_q_norm_w=fox_q_norm_w, fox_k_norm_w=fox_k_norm_w)
    w_in_slots = lambda g, n_scalars: w_in_kernel_to_slots(g, n_scalars)[None]
    rows_view = lambda g, name: g.reshape(_full_shape(name))
    pair_started, begun = {}, {}

    def on_layer1(big1):
        grads1 = dict(fox_w_in=w_in_slots(big1["w_in"], N_HEADS), w_out_1=rows_view(big1["w_out"], "w_out_1"),
                      w_mlp2_1=rows_view(big1["w_mlp2"], "w_mlp2_1"), w_mlp1_1=big1["w_mlp1"][None])
        pair_started["layer1"], token = rs_pair_start(grads1, RS_LAYER1, "layer1")
        return token

    def on_mlp0(d_w_mlp2, d_w_mlp1, d_w_out):
        begun["layer1"], token1 = rs_middle(pair_started["layer1"], RS_LAYER1, "layer1", place, d_w_out)
        grads0 = dict(w_mlp2_0=rows_view(d_w_mlp2, "w_mlp2_0"), w_mlp1_0=d_w_mlp1[None],
                      w_out_0=rows_view(d_w_out, "w_out_0"))
        pair_started["mlp0"], token0 = rs_pair_start(grads0, RS_MLP0, "mlp0")
        return token1 + token0

    def on_core0(d_o):
        begun["mlp0"], token = rs_middle(pair_started["mlp0"], RS_MLP0, "mlp0", place, d_o)
        return token

    sm["norm1_w"] = _after(norm1_w, token)
    loss_part, dx, big, small = local_step(x[0], mem[0], loss_target[0], wt, sm, late, on_layer1, on_mlp0, on_core0)
    small_pack = jnp.concatenate([pack_small(small, loss_part[0, :1]), big["conv_w"].reshape(CONV_ROWS, PACK_W),
                                  jnp.zeros((24 - SMALL_ROWS - CONV_ROWS, PACK_W), F32)])
    small_all = all_reduce_small(small_pack)
    small_sum = small_all[:SMALL_ROWS]
    conv_sum = lax.dynamic_slice(small_all[SMALL_ROWS:SMALL_ROWS + CONV_ROWS].reshape(4, 3 * D_MODEL), (0, 768 * chip), (4, 768))
    loss = small_sum.reshape(-1)[LOSS_AT]
    halves = rs_end(begun["layer1"], RS_LAYER1, "layer1", place, small_all)
    halves.update(rs_end(begun["mlp0"], RS_MLP0, "mlp0", place, small_all))
    summed = rs_pair_gather_big(halves, "early")

    last = dict(dn_w_in=w_in_slots(big["dn_w_in"], 2 * N_HEADS), w_mem_kv=rows_view(big["w_mem_kv"], "w_mem_kv"))
    got_pair = rs_pair_exchange_big(last, RS_LAST, "last", after=summed["fox_w_in"])
    pairs = {name: rs_pair_add_big(name, place, last[name], got_pair[name]) for name in RS_LAST}
    last_state, token = rs_chip_start(pairs, RS_LAST, "last")
    summed = {name: _after(arr, token) for name, arr in summed.items()}

    big_sum = {"fox_w_in": summed["fox_w_in"]}
    big_sum.update({name: jnp.concatenate([summed[name + "_0"], summed[name + "_1"]]) for name in ("w_out", "w_mlp2", "w_mlp1")})
    grads = unpack_small(small_sum, {n: w[n].shape for n, _ in SMALL})
    grads.update({name: big_sum[name].reshape(w[name].shape) for name in big_sum}, dn_conv_w=conv_sum[None])
    delta, new_m, new_v = {}, {}, {}
    for n in ("fox_w_in", "w_out", "w_mlp1", "w_mlp2", "dn_conv_w"):
        delta[n], new_m[n], new_v[n] = _adam_all(w[n], grads[n], m[n], v[n], "adamw_" + n)

    got_chips = rs_chip_wait(last_state, RS_LAST, "last", delta["w_mlp2"])
    summed_last = rs_pair_gather_big({name: rs_chip_add_big(name, place, last[name], got_pair[name], got_chips[name])
                                      for name in RS_LAST}, "last")
    for n in RS_LAST:
        grads[n] = summed_last[n].reshape(w[n].shape)
        delta[n], new_m[n], new_v[n] = _adam_all(w[n], grads[n], m[n], v[n], "adamw_" + n)
    shapes = {n: w[n].shape for n, _ in SMALL}
    d_s, m_s, v_s = adamw(pack_small(w), small_sum, pack_small(m), pack_small(v), name="adamw_small")
    for out, packed in ((delta, d_s), (new_m, m_s), (new_v, v_s)):
        out.update(unpack_small(packed, shapes))
    return (loss, dx[None], *[grads[n] for n in WEIGHTS], *[delta[n] for n in WEIGHTS],
            *[new_m[n] for n in WEIGHTS], *[new_v[n] for n in WEIGHTS])
```
